```python
import jax, jax.numpy as jnp
from jax import lax
import numpy as np

D_MODEL = 2048
BATCH = 8
SEQ = 2048
DEPTH = 2

N_EVEN = (DEPTH + 1) // 2
N_ODD = DEPTH // 2

RET_HEADS = 4
RET_DK = 256
RET_DV = 256
RET_THETA_BASE = 10000.0
GLA_HEADS = 4
GLA_DK = 128
GLA_DV = 256
GLA_GATE_RANK = 16
GLA_GATE_NORM = 16.0
CHUNK = 64

RET_QK = RET_HEADS * RET_DK
RET_V = RET_HEADS * RET_DV
GLA_QK = GLA_HEADS * GLA_DK
GLA_V = GLA_HEADS * GLA_DV
MIX_WIDTH = RET_V + GLA_V
IN_SIZES = (RET_QK, RET_QK, RET_V, RET_V, GLA_QK, GLA_QK, GLA_V, GLA_V, GLA_GATE_RANK)
IN_WIDTH = sum(IN_SIZES)
IN_OFFSETS = tuple(int(o) for o in np.cumsum(IN_SIZES)[:-1])

ATT_HEADS = 16
ATT_HEAD_DIM = D_MODEL // ATT_HEADS
ATT_WIDTH = ATT_HEADS * ATT_HEAD_DIM
DILATED_BRANCHES = ((128, 1), (512, 4), (2048, 16))
ATT_BLOCK = 128

FFN_HIDDEN = -(-8 * D_MODEL // (3 * 256)) * 256
PLE_DIM = 256
NORM_EPS = 1e-6

kernel_name = 'hybrid_retention_gla_dilated_trunk'


def rms_norm(x, w):
    x32 = x.astype(jnp.float32)
    y = x32 * lax.rsqrt(jnp.mean(x32 * x32, axis=-1, keepdims=True) + NORM_EPS)
    return (y * w.astype(jnp.float32)).astype(x.dtype)


def xpos_rotate(t, positions):
    half = t.shape[-1] // 2
    inv_freq = 1.0 / jnp.power(RET_THETA_BASE, jnp.linspace(0.0, 1.0, half, dtype=jnp.float32))
    ang = positions.astype(jnp.float32)[..., None] * inv_freq
    cos, sin = jnp.cos(ang)[:, :, None, :], jnp.sin(ang)[:, :, None, :]
    t2 = t.reshape(*t.shape[:-1], half, 2)
    te, to = t2[..., 0], t2[..., 1]
    return jnp.stack([te * cos - to * sin, te * sin + to * cos], axis=-1).reshape(t.shape)


def to_chunks(t):
    b, s, h, d = t.shape
    return t.reshape(b, s // CHUNK, CHUNK, h, d).transpose(1, 0, 2, 3, 4)


def from_chunks(t):
    nc, b, c, h, d = t.shape
    return t.transpose(1, 0, 2, 3, 4).reshape(b, nc * c, h, d)


def retention(q, k, v):
    b, s, h, dk = q.shape
    dv = v.shape[-1]
    log_g = jnp.log1p(-jnp.exp2(-5.0 - jnp.arange(h, dtype=jnp.float32)))
    idx = jnp.arange(CHUNK, dtype=jnp.float32)
    rel = idx[:, None] - idx[None, :]
    intra_decay = jnp.where(rel[None] >= 0,
                            jnp.exp(jnp.maximum(rel, 0.0)[None] * log_g[:, None, None]), 0.0)
    q_decay = jnp.exp((idx + 1.0)[:, None] * log_g[None, :])
    k_decay = jnp.exp((CHUNK - 1.0 - idx)[:, None] * log_g[None, :])
    chunk_decay = jnp.exp(CHUNK * log_g)
    k = k * dk ** -0.5

    def step(state, inp):
        qc, kc, vc = inp
        sc = jnp.einsum('bihd,bjhd->bhij', qc, kc) * intra_decay
        o = (jnp.einsum('bhij,bjhe->bihe', sc, vc)
             + jnp.einsum('bihd,bhde->bihe', qc * q_decay[..., None], state))
        state = (state * chunk_decay[:, None, None]
                 + jnp.einsum('bjhd,bjhe->bhde', kc * k_decay[..., None], vc))
        return state, o

    init = jnp.zeros((b, h, dk, dv), jnp.float32)
    _, o = lax.scan(step, init, (to_chunks(q), to_chunks(k), to_chunks(v)))
    return from_chunks(o)


def gated_linear_attention(q, k, v, log_a):
    b, s, h, dk = q.shape
    dv = v.shape[-1]
    q = q * dk ** -0.5
    causal = jnp.tril(jnp.ones((CHUNK, CHUNK), dtype=bool))

    def step(state, inp):
        qc, kc, vc, ac = inp
        cum = jnp.cumsum(ac, axis=1)
        last = cum[:, -1]
        q_t = qc * jnp.exp(cum)
        k_t = kc * jnp.exp(-cum)
        sc = jnp.where(causal, jnp.einsum('bihd,bjhd->bhij', q_t, k_t), 0.0)
        o = (jnp.einsum('bhij,bjhe->bihe', sc, vc)
             + jnp.einsum('bihd,bhde->bihe', q_t, state))
        state = (jnp.exp(last)[..., None] * state
                 + jnp.einsum('bjhd,bjhe->bhde', kc * jnp.exp(last[:, None] - cum), vc))
        return state, o

    init = jnp.zeros((b, h, dk, dv), jnp.float32)
    _, o = lax.scan(step, init, (to_chunks(q), to_chunks(k), to_chunks(v), to_chunks(log_a)))
    return from_chunks(o)


def dilated_branch(q, k, v, window, dilation):
    b, s, h, d = q.shape
    span = window // dilation
    sub_len = s // dilation
    n_blk = -(-sub_len // ATT_BLOCK)
    pad_len = n_blk * ATT_BLOCK

    def strided(t):
        t = t.reshape(b, sub_len, dilation, h, d)
        return jnp.pad(t, ((0, 0), (0, pad_len - sub_len), (0, 0), (0, 0), (0, 0)))

    def band(t):
        t = jnp.pad(strided(t), ((0, 0), (ATT_BLOCK, 0), (0, 0), (0, 0), (0, 0)))
        prev = t[:, :pad_len].reshape(b, n_blk, ATT_BLOCK, dilation, h, d)
        cur = t[:, ATT_BLOCK:].reshape(b, n_blk, ATT_BLOCK, dilation, h, d)
        return jnp.concatenate([prev, cur], axis=2)

    qb = strided(q).reshape(b, n_blk, ATT_BLOCK, dilation, h, d)
    kb, vb = band(k), band(v)
    scores = jnp.einsum('bnqrhd,bnkrhd->bnrhqk', qb, kb) * d ** -0.5
    qi = jnp.arange(ATT_BLOCK)[:, None]
    kj = jnp.arange(2 * ATT_BLOCK)[None, :]
    dist = qi + ATT_BLOCK - kj
    key_pos = jnp.arange(n_blk)[:, None, None] * ATT_BLOCK + kj[None] - ATT_BLOCK
    valid = (dist >= 0)[None] & (dist <= span)[None] & (key_pos >= 0)
    scores = jnp.where(valid[None, :, None, None], scores, -jnp.inf)
    m = jnp.max(scores, axis=-1, keepdims=True)
    pr = jnp.exp(scores - m)
    den = jnp.sum(pr, axis=-1)
    out = jnp.einsum('bnrhqk,bnkrhd->bnqrhd', pr, vb) / jnp.transpose(den, (0, 1, 4, 2, 3))[..., None]
    lse = jnp.transpose(m[..., 0] + jnp.log(den), (0, 1, 4, 2, 3))
    out = out.reshape(b, pad_len, dilation, h, d)[:, :sub_len].reshape(b, s, h, d)
    lse = lse.reshape(b, pad_len, dilation, h)[:, :sub_len].reshape(b, s, h)
    return out, lse


def dilated_attention(q, k, v):
    outs, lses = [], []
    for window, dilation in DILATED_BRANCHES:
        o, l = dilated_branch(q, k, v, window, dilation)
        outs.append(o)
        lses.append(l)
    wts = jax.nn.softmax(jnp.stack(lses, axis=0), axis=0)
    return jnp.sum(wts[..., None] * jnp.stack(outs, axis=0), axis=0)


def retention_gla_mixer(xn, positions, w_in, gate_up, gate_b, ret_norm_w, gla_norm_w, w_out):
    b, s, _ = xn.shape
    z = (xn @ w_in).astype(jnp.float32)
    rq, rk, rv, rg, gq, gk, gv, gg, glr = jnp.split(z, IN_OFFSETS, axis=-1)
    rq = xpos_rotate(rq.reshape(b, s, RET_HEADS, RET_DK), positions)
    rk = xpos_rotate(rk.reshape(b, s, RET_HEADS, RET_DK), positions)
    ret = retention(rq, rk, rv.reshape(b, s, RET_HEADS, RET_DV))
    ret = rms_norm(ret, ret_norm_w.reshape(RET_HEADS, RET_DV)).astype(jnp.float32) \
        * jax.nn.silu(rg.reshape(b, s, RET_HEADS, RET_DV))
    log_a = jax.nn.log_sigmoid(glr @ gate_up.astype(jnp.float32) + gate_b.astype(jnp.float32)) / GLA_GATE_NORM
    gla = gated_linear_attention(gq.reshape(b, s, GLA_HEADS, GLA_DK), gk.reshape(b, s, GLA_HEADS, GLA_DK),
                                 gv.reshape(b, s, GLA_HEADS, GLA_DV), log_a.reshape(b, s, GLA_HEADS, GLA_DK))
    gla = rms_norm(gla, gla_norm_w.reshape(GLA_HEADS, GLA_DV)).astype(jnp.float32) \
        * jax.nn.silu(gg.reshape(b, s, GLA_HEADS, GLA_DV))
    o = jnp.concatenate([ret.reshape(b, s, RET_V), gla.reshape(b, s, GLA_V)], axis=-1)
    return o.astype(xn.dtype) @ w_out


def dilated_mixer(xn, w_qkv, w_out):
    b, s, _ = xn.shape
    q, k, v = jnp.split((xn @ w_qkv).astype(jnp.float32), 3, axis=-1)
    heads = lambda t: t.reshape(b, s, ATT_HEADS, ATT_HEAD_DIM)
    o = dilated_attention(heads(q), heads(k), heads(v))
    return o.reshape(b, s, ATT_WIDTH).astype(xn.dtype) @ w_out


def swiglu(xn, w_gate, w_up, w_down):
    return (jax.nn.silu(xn @ w_gate) * (xn @ w_up)) @ w_down


def _fwd_setup_inputs(seed: int = 0) -> dict:
    key = jax.random.key(seed)
    ks = jax.random.split(key, 24)
    f32 = jnp.float32

    def w(k, shape, fan_in):
        return jax.random.normal(k, shape, f32) * fan_in ** -0.5

    def gain(k, shape):
        return 1.0 + 0.05 * jax.random.normal(k, shape, f32)

    x = jax.random.normal(ks[0], (BATCH, SEQ, D_MODEL), f32)
    p = jax.random.normal(ks[1], (DEPTH, BATCH, SEQ, PLE_DIM), f32)
    positions = (jnp.arange(SEQ, dtype=jnp.int32)[None, :]
                 + jax.random.randint(ks[2], (BATCH, 1), 0, 4096, dtype=jnp.int32))
    return {
        'x': x,
        'p': p,
        'positions': positions,
        'attn_norm_w': gain(ks[3], (DEPTH, D_MODEL)),
        'ffn_norm_w': gain(ks[4], (DEPTH, D_MODEL)),
        'ple_norm_w': gain(ks[5], (DEPTH, D_MODEL)),
        'final_norm_w': gain(ks[6], (D_MODEL,)),
        'ab_w_in': w(ks[7], (N_EVEN, D_MODEL, IN_WIDTH), D_MODEL),
        'ab_gla_gate_up': w(ks[8], (N_EVEN, GLA_GATE_RANK, GLA_QK), GLA_GATE_RANK),
        'ab_gla_gate_b': 0.1 * jax.random.normal(ks[9], (N_EVEN, GLA_QK), f32),
        'ab_ret_norm_w': gain(ks[10], (N_EVEN, RET_V)),
        'ab_gla_norm_w': gain(ks[11], (N_EVEN, GLA_V)),
        'ab_w_out': w(ks[12], (N_EVEN, MIX_WIDTH, D_MODEL), MIX_WIDTH),
        'c_w_qkv': w(ks[13], (N_ODD, D_MODEL, 3 * ATT_WIDTH), D_MODEL),
        'c_w_out': w(ks[14], (N_ODD, ATT_WIDTH, D_MODEL), ATT_WIDTH),
        'ffn_w_gate': w(ks[15], (DEPTH, D_MODEL, FFN_HIDDEN), D_MODEL),
        'ffn_w_up': w(ks[16], (DEPTH, D_MODEL, FFN_HIDDEN), D_MODEL),
        'ffn_w_down': w(ks[17], (DEPTH, FFN_HIDDEN, D_MODEL), FFN_HIDDEN),
        'ple_w_proj': w(ks[18], (DEPTH, PLE_DIM, D_MODEL), PLE_DIM),
        'ple_w_gate': w(ks[19], (DEPTH, D_MODEL, D_MODEL), D_MODEL),
    }


def _fwd_reference(x, p, positions, attn_norm_w, ffn_norm_w, ple_norm_w, final_norm_w,
              ab_w_in, ab_gla_gate_up, ab_gla_gate_b, ab_ret_norm_w, ab_gla_norm_w, ab_w_out,
              c_w_qkv, c_w_out, ffn_w_gate, ffn_w_up, ffn_w_down, ple_w_proj, ple_w_gate):
    for i in range(DEPTH):
        j = i // 2
        xn = rms_norm(x, attn_norm_w[i])
        if i % 2 == 0:
            mix = retention_gla_mixer(xn, positions, ab_w_in[j], ab_gla_gate_up[j], ab_gla_gate_b[j],
                                      ab_ret_norm_w[j], ab_gla_norm_w[j], ab_w_out[j])
        else:
            mix = dilated_mixer(xn, c_w_qkv[j], c_w_out[j])
        h = x + mix
        h = h + swiglu(rms_norm(h, ffn_norm_w[i]), ffn_w_gate[i], ffn_w_up[i], ffn_w_down[i])
        gate = jax.nn.sigmoid(rms_norm(h, ple_norm_w[i]) @ ple_w_gate[i])
        x = h + gate * (p[i] @ ple_w_proj[i])
    return rms_norm(x, final_norm_w)


import jax as _jax
import jax.numpy as _jnp

TWIN_FORMAT = 'train_step'
FWD_PARAMS = ['x', 'p', 'positions', 'attn_norm_w', 'ffn_norm_w', 'ple_norm_w', 'final_norm_w', 'ab_w_in', 'ab_gla_gate_up', 'ab_gla_gate_b', 'ab_ret_norm_w', 'ab_gla_norm_w', 'ab_w_out', 'c_w_qkv', 'c_w_out', 'ffn_w_gate', 'ffn_w_up', 'ffn_w_down', 'ple_w_proj', 'ple_w_gate']
TWIN_WEIGHTS = ['attn_norm_w', 'ffn_norm_w', 'ple_norm_w', 'final_norm_w', 'ab_w_in', 'ab_gla_gate_up', 'ab_gla_gate_b', 'ab_ret_norm_w', 'ab_gla_norm_w', 'ab_w_out', 'c_w_qkv', 'c_w_out', 'ffn_w_gate', 'ffn_w_up', 'ffn_w_down', 'ple_w_proj', 'ple_w_gate']
TWIN_DIFF_INPUT = 'x'
TWIN_INPUTS = ['x', 'p', 'positions', 'attn_norm_w', 'ffn_norm_w', 'ple_norm_w', 'final_norm_w', 'ab_w_in', 'ab_gla_gate_up', 'ab_gla_gate_b', 'ab_ret_norm_w', 'ab_gla_norm_w', 'ab_w_out', 'c_w_qkv', 'c_w_out', 'ffn_w_gate', 'ffn_w_up', 'ffn_w_down', 'ple_w_proj', 'ple_w_gate', 'loss_target', 'm_attn_norm_w', 'm_ffn_norm_w', 'm_ple_norm_w', 'm_final_norm_w', 'm_ab_w_in', 'm_ab_gla_gate_up', 'm_ab_gla_gate_b', 'm_ab_ret_norm_w', 'm_ab_gla_norm_w', 'm_ab_w_out', 'm_c_w_qkv', 'm_c_w_out', 'm_ffn_w_gate', 'm_ffn_w_up', 'm_ffn_w_down', 'm_ple_w_proj', 'm_ple_w_gate', 'v_attn_norm_w', 'v_ffn_norm_w', 'v_ple_norm_w', 'v_final_norm_w', 'v_ab_w_in', 'v_ab_gla_gate_up', 'v_ab_gla_gate_b', 'v_ab_ret_norm_w', 'v_ab_gla_norm_w', 'v_ab_w_out', 'v_c_w_qkv', 'v_c_w_out', 'v_ffn_w_gate', 'v_ffn_w_up', 'v_ffn_w_down', 'v_ple_w_proj', 'v_ple_w_gate']
TWIN_OUTPUTS = ['loss', 'grad_x', 'grad_attn_norm_w', 'grad_ffn_norm_w', 'grad_ple_norm_w', 'grad_final_norm_w', 'grad_ab_w_in', 'grad_ab_gla_gate_up', 'grad_ab_gla_gate_b', 'grad_ab_ret_norm_w', 'grad_ab_gla_norm_w', 'grad_ab_w_out', 'grad_c_w_qkv', 'grad_c_w_out', 'grad_ffn_w_gate', 'grad_ffn_w_up', 'grad_ffn_w_down', 'grad_ple_w_proj', 'grad_ple_w_gate', 'delta_attn_norm_w', 'delta_ffn_norm_w', 'delta_ple_norm_w', 'delta_final_norm_w', 'delta_ab_w_in', 'delta_ab_gla_gate_up', 'delta_ab_gla_gate_b', 'delta_ab_ret_norm_w', 'delta_ab_gla_norm_w', 'delta_ab_w_out', 'delta_c_w_qkv', 'delta_c_w_out', 'delta_ffn_w_gate', 'delta_ffn_w_up', 'delta_ffn_w_down', 'delta_ple_w_proj', 'delta_ple_w_gate', 'new_m_attn_norm_w', 'new_m_ffn_norm_w', 'new_m_ple_norm_w', 'new_m_final_norm_w', 'new_m_ab_w_in', 'new_m_ab_gla_gate_up', 'new_m_ab_gla_gate_b', 'new_m_ab_ret_norm_w', 'new_m_ab_gla_norm_w', 'new_m_ab_w_out', 'new_m_c_w_qkv', 'new_m_c_w_out', 'new_m_ffn_w_gate', 'new_m_ffn_w_up', 'new_m_ffn_w_down', 'new_m_ple_w_proj', 'new_m_ple_w_gate', 'new_v_attn_norm_w', 'new_v_ffn_norm_w', 'new_v_ple_norm_w', 'new_v_final_norm_w', 'new_v_ab_w_in', 'new_v_ab_gla_gate_up', 'new_v_ab_gla_gate_b', 'new_v_ab_ret_norm_w', 'new_v_ab_gla_norm_w', 'new_v_ab_w_out', 'new_v_c_w_qkv', 'new_v_c_w_out', 'new_v_ffn_w_gate', 'new_v_ffn_w_up', 'new_v_ffn_w_down', 'new_v_ple_w_proj', 'new_v_ple_w_gate']
TWIN_LEAF_KINDS = {'loss': 'loss', 'grad_x': 'grad_x', 'grad_attn_norm_w': 'grad_w', 'grad_ffn_norm_w': 'grad_w', 'grad_ple_norm_w': 'grad_w', 'grad_final_norm_w': 'grad_w', 'grad_ab_w_in': 'grad_w', 'grad_ab_gla_gate_up': 'grad_w', 'grad_ab_gla_gate_b': 'grad_w', 'grad_ab_ret_norm_w': 'grad_w', 'grad_ab_gla_norm_w': 'grad_w', 'grad_ab_w_out': 'grad_w', 'grad_c_w_qkv': 'grad_w', 'grad_c_w_out': 'grad_w', 'grad_ffn_w_gate': 'grad_w', 'grad_ffn_w_up': 'grad_w', 'grad_ffn_w_down': 'grad_w', 'grad_ple_w_proj': 'grad_w', 'grad_ple_w_gate': 'grad_w', 'delta_attn_norm_w': 'delta_w', 'delta_ffn_norm_w': 'delta_w', 'delta_ple_norm_w': 'delta_w', 'delta_final_norm_w': 'delta_w', 'delta_ab_w_in': 'delta_w', 'delta_ab_gla_gate_up': 'delta_w', 'delta_ab_gla_gate_b': 'delta_w', 'delta_ab_ret_norm_w': 'delta_w', 'delta_ab_gla_norm_w': 'delta_w', 'delta_ab_w_out': 'delta_w', 'delta_c_w_qkv': 'delta_w', 'delta_c_w_out': 'delta_w', 'delta_ffn_w_gate': 'delta_w', 'delta_ffn_w_up': 'delta_w', 'delta_ffn_w_down': 'delta_w', 'delta_ple_w_proj': 'delta_w', 'delta_ple_w_gate': 'delta_w', 'new_m_attn_norm_w': 'new_m', 'new_m_ffn_norm_w': 'new_m', 'new_m_ple_norm_w': 'new_m', 'new_m_final_norm_w': 'new_m', 'new_m_ab_w_in': 'new_m', 'new_m_ab_gla_gate_up': 'new_m', 'new_m_ab_gla_gate_b': 'new_m', 'new_m_ab_ret_norm_w': 'new_m', 'new_m_ab_gla_norm_w': 'new_m', 'new_m_ab_w_out': 'new_m', 'new_m_c_w_qkv': 'new_m', 'new_m_c_w_out': 'new_m', 'new_m_ffn_w_gate': 'new_m', 'new_m_ffn_w_up': 'new_m', 'new_m_ffn_w_down': 'new_m', 'new_m_ple_w_proj': 'new_m', 'new_m_ple_w_gate': 'new_m', 'new_v_attn_norm_w': 'new_v', 'new_v_ffn_norm_w': 'new_v', 'new_v_ple_norm_w': 'new_v', 'new_v_final_norm_w': 'new_v', 'new_v_ab_w_in': 'new_v', 'new_v_ab_gla_gate_up': 'new_v', 'new_v_ab_gla_gate_b': 'new_v', 'new_v_ab_ret_norm_w': 'new_v', 'new_v_ab_gla_norm_w': 'new_v', 'new_v_ab_w_out': 'new_v', 'new_v_c_w_qkv': 'new_v', 'new_v_c_w_out': 'new_v', 'new_v_ffn_w_gate': 'new_v', 'new_v_ffn_w_up': 'new_v', 'new_v_ffn_w_down': 'new_v', 'new_v_ple_w_proj': 'new_v', 'new_v_ple_w_gate': 'new_v'}


def _forward(args):
    return _fwd_reference(*[args[k] for k in FWD_PARAMS])


def _output_shape():
    out = _jax.eval_shape(lambda: _forward(_fwd_setup_inputs(0)))
    return out.shape, out.dtype

N_MICROBATCH = 1
ADAM_LR = 0.001
ADAM_B1 = 0.9
ADAM_B2 = 0.999
ADAM_EPS = 1e-08
ADAM_WD = 0.01
ADAM_STEP = 10
PER_EXAMPLE_BATCH_AXIS = {'x': 0, 'p': 1, 'positions': 0, 'loss_target': 0}
SHARED_INPUTS = []
_WEIGHT_DTYPES = {'attn_norm_w': _jnp.float32, 'ffn_norm_w': _jnp.float32, 'ple_norm_w': _jnp.float32, 'final_norm_w': _jnp.float32, 'ab_w_in': _jnp.float32, 'ab_gla_gate_up': _jnp.float32, 'ab_gla_gate_b': _jnp.float32, 'ab_ret_norm_w': _jnp.float32, 'ab_gla_norm_w': _jnp.float32, 'ab_w_out': _jnp.float32, 'c_w_qkv': _jnp.float32, 'c_w_out': _jnp.float32, 'ffn_w_gate': _jnp.float32, 'ffn_w_up': _jnp.float32, 'ffn_w_down': _jnp.float32, 'ple_w_proj': _jnp.float32, 'ple_w_gate': _jnp.float32}
MOMENT_SCALE = {'attn_norm_w': 5.312461e-02, 'ffn_norm_w': 3.816149e-02, 'ple_norm_w': 8.933912e-03, 'final_norm_w': 8.005047e+00, 'ab_w_in': 3.804593e-02, 'ab_gla_gate_up': 5.620618e-03, 'ab_gla_gate_b': 2.738487e-02, 'ab_ret_norm_w': 3.420783e-02, 'ab_gla_norm_w': 3.486766e-02, 'ab_w_out': 3.476924e-02, 'c_w_qkv': 9.029684e-03, 'c_w_out': 1.012406e-02, 'ffn_w_gate': 1.596264e-02, 'ffn_w_up': 1.548305e-02, 'ffn_w_down': 2.565616e-02, 'ple_w_proj': 2.316505e-02, 'ple_w_gate': 8.983463e-03}


def _to_microbatches(a, axis):
    t = _jnp.moveaxis(a, axis, 0)
    t = t.reshape((N_MICROBATCH, t.shape[0] // N_MICROBATCH) + t.shape[1:])
    return _jnp.moveaxis(t, 1, axis + 1)


def setup_inputs(seed: int = 0) -> dict:
    inp = _fwd_setup_inputs(seed)
    key = _jax.random.fold_in(_jax.random.key(seed), 7919)
    shape, _ = _output_shape()
    out = dict(inp)
    out["loss_target"] = _jax.random.normal(_jax.random.fold_in(key, 0), shape, _jnp.float32)
    for i, name in enumerate(TWIN_WEIGHTS):
        w = inp[name].astype(_jnp.float32)
        if MOMENT_SCALE is None:
            s = _jnp.sqrt(_jnp.mean(_jnp.square(w)) + 1e-30)
        else:
            s = MOMENT_SCALE[name]
        km, kv = _jax.random.split(_jax.random.fold_in(key, i + 1))
        out[name] = w
        out["m_" + name] = s * _jax.random.normal(km, w.shape, _jnp.float32)
        out["v_" + name] = (s * s) * _jax.random.uniform(kv, w.shape, _jnp.float32, 0.5, 1.5)
    if N_MICROBATCH > 1:
        for name, axis in PER_EXAMPLE_BATCH_AXIS.items():
            out[name] = _to_microbatches(out[name], axis)
    return {'x': out['x'], 'p': out['p'], 'positions': out['positions'], 'attn_norm_w': out['attn_norm_w'], 'ffn_norm_w': out['ffn_norm_w'], 'ple_norm_w': out['ple_norm_w'], 'final_norm_w': out['final_norm_w'], 'ab_w_in': out['ab_w_in'], 'ab_gla_gate_up': out['ab_gla_gate_up'], 'ab_gla_gate_b': out['ab_gla_gate_b'], 'ab_ret_norm_w': out['ab_ret_norm_w'], 'ab_gla_norm_w': out['ab_gla_norm_w'], 'ab_w_out': out['ab_w_out'], 'c_w_qkv': out['c_w_qkv'], 'c_w_out': out['c_w_out'], 'ffn_w_gate': out['ffn_w_gate'], 'ffn_w_up': out['ffn_w_up'], 'ffn_w_down': out['ffn_w_down'], 'ple_w_proj': out['ple_w_proj'], 'ple_w_gate': out['ple_w_gate'], 'loss_target': out['loss_target'], 'm_attn_norm_w': out['m_attn_norm_w'], 'm_ffn_norm_w': out['m_ffn_norm_w'], 'm_ple_norm_w': out['m_ple_norm_w'], 'm_final_norm_w': out['m_final_norm_w'], 'm_ab_w_in': out['m_ab_w_in'], 'm_ab_gla_gate_up': out['m_ab_gla_gate_up'], 'm_ab_gla_gate_b': out['m_ab_gla_gate_b'], 'm_ab_ret_norm_w': out['m_ab_ret_norm_w'], 'm_ab_gla_norm_w': out['m_ab_gla_norm_w'], 'm_ab_w_out': out['m_ab_w_out'], 'm_c_w_qkv': out['m_c_w_qkv'], 'm_c_w_out': out['m_c_w_out'], 'm_ffn_w_gate': out['m_ffn_w_gate'], 'm_ffn_w_up': out['m_ffn_w_up'], 'm_ffn_w_down': out['m_ffn_w_down'], 'm_ple_w_proj': out['m_ple_w_proj'], 'm_ple_w_gate': out['m_ple_w_gate'], 'v_attn_norm_w': out['v_attn_norm_w'], 'v_ffn_norm_w': out['v_ffn_norm_w'], 'v_ple_norm_w': out['v_ple_norm_w'], 'v_final_norm_w': out['v_final_norm_w'], 'v_ab_w_in': out['v_ab_w_in'], 'v_ab_gla_gate_up': out['v_ab_gla_gate_up'], 'v_ab_gla_gate_b': out['v_ab_gla_gate_b'], 'v_ab_ret_norm_w': out['v_ab_ret_norm_w'], 'v_ab_gla_norm_w': out['v_ab_gla_norm_w'], 'v_ab_w_out': out['v_ab_w_out'], 'v_c_w_qkv': out['v_c_w_qkv'], 'v_c_w_out': out['v_c_w_out'], 'v_ffn_w_gate': out['v_ffn_w_gate'], 'v_ffn_w_up': out['v_ffn_w_up'], 'v_ffn_w_down': out['v_ffn_w_down'], 'v_ple_w_proj': out['v_ple_w_proj'], 'v_ple_w_gate': out['v_ple_w_gate']}


def _loss(weights, diff, rest, loss_target):
    with _jax.named_scope("forward"):
        args = {**rest, TWIN_DIFF_INPUT: diff, **{k: w.astype(_WEIGHT_DTYPES[k]) for k, w in weights.items()}}
        y = _forward(args)
    with _jax.named_scope("loss_head"):
        err = _jnp.square(y.astype(_jnp.float32) - loss_target)
        return 0.5 * _jnp.sum(_jnp.mean(err, axis=-1)) if err.ndim else 0.5 * err


def _adamw(w, g, m, v):
    m = ADAM_B1 * m + (1.0 - ADAM_B1) * g
    v = ADAM_B2 * v + (1.0 - ADAM_B2) * _jnp.square(g)
    m_hat = m / (1.0 - ADAM_B1 ** ADAM_STEP)
    v_hat = v / (1.0 - ADAM_B2 ** ADAM_STEP)
    delta = -ADAM_LR * (m_hat / (_jnp.sqrt(v_hat) + ADAM_EPS) + ADAM_WD * w)
    return delta, m, v


def reference(x, p, positions, attn_norm_w, ffn_norm_w, ple_norm_w, final_norm_w, ab_w_in, ab_gla_gate_up, ab_gla_gate_b, ab_ret_norm_w, ab_gla_norm_w, ab_w_out, c_w_qkv, c_w_out, ffn_w_gate, ffn_w_up, ffn_w_down, ple_w_proj, ple_w_gate, loss_target, m_attn_norm_w, m_ffn_norm_w, m_ple_norm_w, m_final_norm_w, m_ab_w_in, m_ab_gla_gate_up, m_ab_gla_gate_b, m_ab_ret_norm_w, m_ab_gla_norm_w, m_ab_w_out, m_c_w_qkv, m_c_w_out, m_ffn_w_gate, m_ffn_w_up, m_ffn_w_down, m_ple_w_proj, m_ple_w_gate, v_attn_norm_w, v_ffn_norm_w, v_ple_norm_w, v_final_norm_w, v_ab_w_in, v_ab_gla_gate_up, v_ab_gla_gate_b, v_ab_ret_norm_w, v_ab_gla_norm_w, v_ab_w_out, v_c_w_qkv, v_c_w_out, v_ffn_w_gate, v_ffn_w_up, v_ffn_w_down, v_ple_w_proj, v_ple_w_gate):
    given = dict(x=x, p=p, positions=positions, attn_norm_w=attn_norm_w, ffn_norm_w=ffn_norm_w, ple_norm_w=ple_norm_w, final_norm_w=final_norm_w, ab_w_in=ab_w_in, ab_gla_gate_up=ab_gla_gate_up, ab_gla_gate_b=ab_gla_gate_b, ab_ret_norm_w=ab_ret_norm_w, ab_gla_norm_w=ab_gla_norm_w, ab_w_out=ab_w_out, c_w_qkv=c_w_qkv, c_w_out=c_w_out, ffn_w_gate=ffn_w_gate, ffn_w_up=ffn_w_up, ffn_w_down=ffn_w_down, ple_w_proj=ple_w_proj, ple_w_gate=ple_w_gate, loss_target=loss_target, m_attn_norm_w=m_attn_norm_w, m_ffn_norm_w=m_ffn_norm_w, m_ple_norm_w=m_ple_norm_w, m_final_norm_w=m_final_norm_w, m_ab_w_in=m_ab_w_in, m_ab_gla_gate_up=m_ab_gla_gate_up, m_ab_gla_gate_b=m_ab_gla_gate_b, m_ab_ret_norm_w=m_ab_ret_norm_w, m_ab_gla_norm_w=m_ab_gla_norm_w, m_ab_w_out=m_ab_w_out, m_c_w_qkv=m_c_w_qkv, m_c_w_out=m_c_w_out, m_ffn_w_gate=m_ffn_w_gate, m_ffn_w_up=m_ffn_w_up, m_ffn_w_down=m_ffn_w_down, m_ple_w_proj=m_ple_w_proj, m_ple_w_gate=m_ple_w_gate, v_attn_norm_w=v_attn_norm_w, v_ffn_norm_w=v_ffn_norm_w, v_ple_norm_w=v_ple_norm_w, v_final_norm_w=v_final_norm_w, v_ab_w_in=v_ab_w_in, v_ab_gla_gate_up=v_ab_gla_gate_up, v_ab_gla_gate_b=v_ab_gla_gate_b, v_ab_ret_norm_w=v_ab_ret_norm_w, v_ab_gla_norm_w=v_ab_gla_norm_w, v_ab_w_out=v_ab_w_out, v_c_w_qkv=v_c_w_qkv, v_c_w_out=v_c_w_out, v_ffn_w_gate=v_ffn_w_gate, v_ffn_w_up=v_ffn_w_up, v_ffn_w_down=v_ffn_w_down, v_ple_w_proj=v_ple_w_proj, v_ple_w_gate=v_ple_w_gate)
    weights = {n: given[n] for n in TWIN_WEIGHTS}
    shared = {n: given[n] for n in SHARED_INPUTS}
    per_example = {n: given[n] for n in ['x', 'p', 'positions']}
    grad_fn = _jax.value_and_grad(_loss, argnums=(0, 1))

    def one_microbatch(ex, loss_target):
        ex = dict(ex)
        diff = ex.pop(TWIN_DIFF_INPUT)
        return grad_fn(weights, diff, {**shared, **ex}, loss_target)

    if N_MICROBATCH == 1:
        loss, (grad_w, grad_x) = one_microbatch(per_example, given["loss_target"])
    else:
        def body(carry, xs):
            loss_sum, grad_sum = carry
            l_k, (gw_k, gx_k) = one_microbatch(xs[0], xs[1])
            with _jax.named_scope("update"):
                return (loss_sum + l_k, _jax.tree.map(_jnp.add, grad_sum, gw_k)), gx_k

        init = (_jnp.zeros((), _jnp.float32), _jax.tree.map(_jnp.zeros_like, weights))
        (loss, grad_w), grad_x = _jax.lax.scan(body, init, (per_example, given["loss_target"]))
    with _jax.named_scope("update"):
        delta_w, new_m, new_v = {}, {}, {}
        for n in TWIN_WEIGHTS:
            delta_w[n], new_m[n], new_v[n] = _adamw(weights[n], grad_w[n], given["m_" + n], given["v_" + n])
    return (loss, grad_x, *[grad_w[n] for n in TWIN_WEIGHTS], *[delta_w[n] for n in TWIN_WEIGHTS],
            *[new_m[n] for n in TWIN_WEIGHTS], *[new_v[n] for n in TWIN_WEIGHTS])
```

```python
import math

import numpy as np
import jax
import jax.numpy as jnp
from jax import lax
from jax.experimental import pallas as pl
from jax.experimental.pallas import tpu as pltpu

F32 = jnp.float32
BF16 = jnp.bfloat16
HIGHEST = lax.Precision.HIGHEST

N_DEV = 8
VMEM_LIMIT_BYTES = 48 * 1024 * 1024
LANE = 128
NORM_EPS = 1e-6

RET_HEADS, RET_DK, RET_DV = 4, 256, 256
RET_THETA_BASE = 10000.0
GLA_HEADS, GLA_DK, GLA_DV = 4, 128, 256
GLA_GATE_RANK = 16
GLA_GATE_NORM = 16.0
CHUNK = 64
ATT_HEADS = 16
DILATED_BRANCHES = ((128, 1), (512, 4), (2048, 16))
BLK = 256

ADAM_LR, ADAM_B1, ADAM_B2, ADAM_EPS, ADAM_WD, ADAM_STEP = 0.001, 0.9, 0.999, 1e-08, 0.01, 10

RET_QK = RET_HEADS * RET_DK
RET_V = RET_HEADS * RET_DV
GLA_QK = GLA_HEADS * GLA_DK
GLA_V = GLA_HEADS * GLA_DV
OFF_RQ, OFF_RK, OFF_RV, OFF_RG = 0, RET_QK, 2 * RET_QK, 2 * RET_QK + RET_V
OFF_GQ = OFF_RG + RET_V
OFF_GK = OFF_GQ + GLA_QK
OFF_GV = OFF_GK + GLA_QK
OFF_GG = OFF_GV + GLA_V
OFF_LR = OFF_GG + GLA_V


def _params(*sem):
    return pltpu.CompilerParams(dimension_semantics=sem or None, vmem_limit_bytes=VMEM_LIMIT_BYTES)


def _pick(n, cands):
    for c in cands:
        if n % c == 0:
            return c
    raise ValueError(f"no tile for {n} in {cands}")


def _mm_call(name, dims, grid, in_specs, out_spec, out_shape, acc_shape, args):
    def body(a_ref, b_ref, o_ref, acc):
        k = pl.program_id(2)

        @pl.when(k == 0)
        def _():
            acc[...] = jnp.zeros_like(acc)

        acc[...] += lax.dot_general(a_ref[...].astype(BF16), b_ref[...].astype(BF16), dims,
                                    preferred_element_type=F32)

        @pl.when(k == pl.num_programs(2) - 1)
        def _():
            o_ref[...] = acc[...].astype(o_ref.dtype)

    return pl.pallas_call(
        body, name=name, grid=grid, in_specs=in_specs, out_specs=out_spec, out_shape=out_shape,
        scratch_shapes=[pltpu.VMEM(acc_shape, F32)],
        compiler_params=_params("parallel", "parallel", "arbitrary"))(*args)


def mm_nn(name, a, w, l, out_dtype):
    _, J, K, n = w.shape
    M = a.shape[0]
    tm = _pick(M, (1024, 512, 256))
    tn = _pick(n, (1024, 768, 512, 256, 128))
    tk = _pick(K, (512, 256))
    nt = n // tn
    return _mm_call(
        name, (((1,), (0,)), ((), ())), (M // tm, J * nt, K // tk),
        [pl.BlockSpec((tm, tk), lambda i, j, k: (i, k)),
         pl.BlockSpec((None, None, tk, tn), lambda i, j, k: (l, j // nt, k, j % nt))],
        pl.BlockSpec((tm, tn), lambda i, j, k: (i, j)),
        jax.ShapeDtypeStruct((M, J * n), out_dtype), (tm, tn), (a, w))


def mm_nt(name, a, w, l, out_dtype):
    _, J, K, n = w.shape
    M = a.shape[0]
    tm = _pick(M, (1024, 512, 256))
    tq = _pick(K, (1024, 512, 256))
    tc = _pick(n, (768, 512, 256, 128))
    nc = n // tc
    return _mm_call(
        name, (((1,), (1,)), ((), ())), (M // tm, K // tq, J * nc),
        [pl.BlockSpec((tm, tc), lambda i, q, c: (i, c)),
         pl.BlockSpec((None, None, tq, tc), lambda i, q, c: (l, c // nc, q, c % nc))],
        pl.BlockSpec((tm, tq), lambda i, q, c: (i, q)),
        jax.ShapeDtypeStruct((M, K), out_dtype), (tm, tq), (a, w))


def mm_tn(name, x, dy, J, out_dtype):
    M, K = x.shape
    n = dy.shape[1] // J
    tp = _pick(K, (1024, 768, 512, 256))
    tn = _pick(n, (1024, 768, 512, 256, 128))
    tr = _pick(M, (512, 256))
    nt = n // tn
    return _mm_call(
        name, (((0,), (0,)), ((), ())), (K // tp, J * nt, M // tr),
        [pl.BlockSpec((tr, tp), lambda i, j, r: (r, i)),
         pl.BlockSpec((tr, tn), lambda i, j, r: (r, j))],
        pl.BlockSpec((None, tp, tn), lambda i, j, r: (j // nt, i, j % nt)),
        jax.ShapeDtypeStruct((J, K, n), out_dtype), (tp, tn), (x, dy))


def rowwise(name, fn, rows, ins, outs, tr=256):
    widest = max([s[1].shape[1] if s[0] != "col" else s[3] for s in ins] + [s[1] for s in outs])
    tr = min(tr if widest <= 2048 else tr // 2, rows)
    in_specs, args = [], []
    for spec in ins:
        kind, a = spec[0], spec[1]
        if kind == "row":
            in_specs.append(pl.BlockSpec((tr, a.shape[1]), lambda i: (i, 0)))
        elif kind == "col":
            cb, width = spec[2], spec[3]
            in_specs.append(pl.BlockSpec((tr, width), lambda i, cb=cb: (i, cb)))
        else:
            in_specs.append(pl.BlockSpec(a.shape, lambda i: (0, 0)))
        args.append(a)
    out_specs, out_shapes = [], []
    for spec in outs:
        if spec[0] == "row":
            out_specs.append(pl.BlockSpec((tr, spec[1]), lambda i: (i, 0)))
            out_shapes.append(jax.ShapeDtypeStruct((rows, spec[1]), spec[2]))
        else:
            out_specs.append(pl.BlockSpec((1, spec[1]), lambda i: (0, 0)))
            out_shapes.append(jax.ShapeDtypeStruct((1, spec[1]), F32))
    n_in = len(ins)

    def body(*refs):
        vals = fn(*[r[...] for r in refs[:n_in]])
        first = pl.program_id(0) == 0
        for r, v, spec in zip(refs[n_in:], vals, outs):
            if spec[0] == "row":
                r[...] = v.astype(r.dtype)
            else:
                _accumulate(r, v, first)

    return pl.pallas_call(body, name=name, grid=(rows // tr,), in_specs=in_specs, out_specs=out_specs,
                          out_shape=out_shapes, compiler_params=_params("arbitrary"))(*args)


def _accumulate(ref, v, first):
    @pl.when(first)
    def _():
        ref[...] = v

    @pl.when(jnp.logical_not(first))
    def _():
        ref[...] += v


def _rms(x, w):
    r = lax.rsqrt(jnp.mean(x * x, axis=-1, keepdims=True) + NORM_EPS)
    return x * r * w


def _rms_bwd(x, w, dy):
    r = lax.rsqrt(jnp.mean(x * x, axis=-1, keepdims=True) + NORM_EPS)
    g = dy * w
    dx = r * (g - x * (r * r) * jnp.mean(g * x, axis=-1, keepdims=True))
    dw = jnp.sum(dy * x * r, axis=0, keepdims=True)
    return dx, dw


def _sigmoid(x):
    return 1.0 / (1.0 + jnp.exp(-x))


def _silu_and_grad(g):
    s = _sigmoid(g)
    return g * s, s * (1.0 + g * (1.0 - s))


def _swap_pairs(x):
    n = x.shape[-1]
    lane = lax.broadcasted_iota(jnp.int32, x.shape, x.ndim - 1)
    return jnp.where((lane & 1) == 0, pltpu.roll(x, n - 1, x.ndim - 1), pltpu.roll(x, 1, x.ndim - 1))


def _rot(x, cosf, sins):
    return x * cosf + _swap_pairs(x) * sins


def _unrot(d, cosf, sins):
    return d * cosf + _swap_pairs(d * sins)


def _ret_log_gamma(h):
    vals = [math.log1p(-2.0 ** (-5.0 - i)) for i in range(RET_HEADS)]
    out = jnp.float32(vals[RET_HEADS - 1])
    for i in range(RET_HEADS - 2, -1, -1):
        out = jnp.where(h == i, jnp.float32(vals[i]), out)
    return out


def _decay_block(qi, kb, lg):
    ri = lax.broadcasted_iota(jnp.int32, (BLK, BLK), 0)
    ci = lax.broadcasted_iota(jnp.int32, (BLK, BLK), 1)
    dt = (qi - kb) * BLK + ri - ci
    return jnp.where(dt >= 0, jnp.exp(jnp.maximum(dt, 0).astype(F32) * lg), 0.0)


_NT = (((1,), (1,)), ((), ()))
_TN = (((0,), (0,)), ((), ()))


def _dot(a, b):
    return jnp.dot(a.astype(BF16), b.astype(BF16), preferred_element_type=F32)


def _dot_nt(a, b):
    return lax.dot_general(a.astype(BF16), b.astype(BF16), _NT, preferred_element_type=F32)


def _dot_tn(a, b):
    return lax.dot_general(a.astype(BF16), b.astype(BF16), _TN, preferred_element_type=F32)


def retention_fwd(name, z, cosf, sins, width_out):
    T = z.shape[0]
    nq = T // BLK
    scale = RET_DK ** -0.5

    def body(q_ref, k_ref, v_ref, cos_ref, sin_ref, o_ref, krot, vb):
        h, qi = pl.program_id(0), pl.program_id(1)
        lg = _ret_log_gamma(h)

        @pl.when(qi == 0)
        def _():
            krot[...] = (_rot(k_ref[...], cos_ref[...], sin_ref[...]) * scale).astype(BF16)
            vb[...] = v_ref[...].astype(BF16)

        rows = pl.ds(pl.multiple_of(qi * BLK, BLK), BLK)
        q = _rot(q_ref[...], cos_ref[rows, :], sin_ref[rows, :]).astype(BF16)

        def step(kb, acc):
            krows = pl.ds(pl.multiple_of(kb * BLK, BLK), BLK)
            s = _dot_nt(q, krot[krows, :]) * _decay_block(qi, kb, lg)
            return acc + _dot(s, vb[krows, :])

        o_ref[...] = lax.fori_loop(0, qi + 1, step, jnp.zeros((BLK, RET_DV), F32))

    return pl.pallas_call(
        body, name=name, grid=(RET_HEADS, nq),
        in_specs=[pl.BlockSpec((BLK, RET_DK), lambda h, i: (i, OFF_RQ // RET_DK + h)),
                  pl.BlockSpec((T, RET_DK), lambda h, i: (0, OFF_RK // RET_DK + h)),
                  pl.BlockSpec((T, RET_DV), lambda h, i: (0, OFF_RV // RET_DV + h)),
                  pl.BlockSpec((T, RET_DK), lambda h, i: (0, 0)),
                  pl.BlockSpec((T, RET_DK), lambda h, i: (0, 0))],
        out_specs=pl.BlockSpec((BLK, RET_DV), lambda h, i: (i, h)),
        out_shape=jax.ShapeDtypeStruct((T, width_out), F32),
        scratch_shapes=[pltpu.VMEM((T, RET_DK), BF16), pltpu.VMEM((T, RET_DV), BF16)],
        compiler_params=_params("arbitrary", "arbitrary"))(z, z, z, cosf, sins)


def retention_bwd(name, z, cosf, sins, do):
    T = z.shape[0]
    nq = T // BLK
    scale = RET_DK ** -0.5

    def body(q_ref, k_ref, v_ref, cos_ref, sin_ref, do_ref, dq_ref, dk_ref, dv_ref, krot, vb, dk_acc, dv_acc):
        h, qi = pl.program_id(0), pl.program_id(1)
        lg = _ret_log_gamma(h)

        @pl.when(qi == 0)
        def _():
            krot[...] = (_rot(k_ref[...], cos_ref[...], sin_ref[...]) * scale).astype(BF16)
            vb[...] = v_ref[...].astype(BF16)
            dk_acc[...] = jnp.zeros_like(dk_acc)
            dv_acc[...] = jnp.zeros_like(dv_acc)

        rows = pl.ds(pl.multiple_of(qi * BLK, BLK), BLK)
        cos_q, sin_q = cos_ref[rows, :], sin_ref[rows, :]
        q = _rot(q_ref[...], cos_q, sin_q).astype(BF16)
        dout = do_ref[...].astype(BF16)

        def step(kb, dq):
            krows = pl.ds(pl.multiple_of(kb * BLK, BLK), BLK)
            kk, vv = krot[krows, :], vb[krows, :]
            dec = _decay_block(qi, kb, lg)
            p = (_dot_nt(q, kk) * dec).astype(BF16)
            ds = (_dot_nt(dout, vv) * dec).astype(BF16)
            dk_acc[krows, :] += _dot_tn(ds, q)
            dv_acc[krows, :] += _dot_tn(p, dout)
            return dq + _dot(ds, kk)

        dq = lax.fori_loop(0, qi + 1, step, jnp.zeros((BLK, RET_DK), F32))
        dq_ref[...] = _unrot(dq, cos_q, sin_q).astype(dq_ref.dtype)

        @pl.when(qi == nq - 1)
        def _():
            dk_ref[...] = (_unrot(dk_acc[...], cos_ref[...], sin_ref[...]) * scale).astype(dk_ref.dtype)
            dv_ref[...] = dv_acc[...].astype(dv_ref.dtype)

    full = lambda h, i: (0, h)
    return pl.pallas_call(
        body, name=name, grid=(RET_HEADS, nq),
        in_specs=[pl.BlockSpec((BLK, RET_DK), lambda h, i: (i, OFF_RQ // RET_DK + h)),
                  pl.BlockSpec((T, RET_DK), lambda h, i: (0, OFF_RK // RET_DK + h)),
                  pl.BlockSpec((T, RET_DV), lambda h, i: (0, OFF_RV // RET_DV + h)),
                  pl.BlockSpec((T, RET_DK), lambda h, i: (0, 0)),
                  pl.BlockSpec((T, RET_DK), lambda h, i: (0, 0)),
                  pl.BlockSpec((BLK, RET_DV), lambda h, i: (i, h))],
        out_specs=[pl.BlockSpec((BLK, RET_DK), lambda h, i: (i, h)),
                   pl.BlockSpec((T, RET_DK), full), pl.BlockSpec((T, RET_DV), full)],
        out_shape=[jax.ShapeDtypeStruct((T, RET_QK), BF16), jax.ShapeDtypeStruct((T, RET_QK), BF16),
                   jax.ShapeDtypeStruct((T, RET_V), BF16)],
        scratch_shapes=[pltpu.VMEM((T, RET_DK), BF16), pltpu.VMEM((T, RET_DV), BF16),
                        pltpu.VMEM((T, RET_DK), F32), pltpu.VMEM((T, RET_DV), F32)],
        compiler_params=_params("arbitrary", "arbitrary"))(z, z, z, cosf, sins, do)


def _gla_chunk(q_ref, k_ref, v_ref, glr_ref, gu, gb, rows, trilf):
    zg = _dot(glr_ref[rows, :], gu) + gb
    la = (jnp.minimum(zg, 0.0) - jnp.log(1.0 + jnp.exp(-jnp.abs(zg)))) * (1.0 / GLA_GATE_NORM)
    cum = jnp.dot(trilf, la, precision=HIGHEST, preferred_element_type=F32)
    last = jnp.sum(la, axis=0, keepdims=True)
    ecum = jnp.exp(cum)
    k = k_ref[rows, :]
    qt = q_ref[rows, :] * (GLA_DK ** -0.5) * ecum
    kt = k * jnp.exp(-cum)
    kh = k * jnp.exp(last - cum)
    return zg, cum, last, ecum, qt, kt, kh, v_ref[rows, :].astype(BF16)


def _state_decay(last):
    e = jnp.exp(jnp.broadcast_to(last, (GLA_DK, GLA_DK)).T)
    return jnp.concatenate([e] * (GLA_DV // GLA_DK), axis=1)


def _gla_specs(T):
    return [pl.BlockSpec((T, GLA_DK), lambda h: (0, OFF_GQ // GLA_DK + h)),
            pl.BlockSpec((T, GLA_DK), lambda h: (0, OFF_GK // GLA_DK + h)),
            pl.BlockSpec((T, GLA_DV), lambda h: (0, OFF_GV // GLA_DV + h)),
            pl.BlockSpec((T, LANE), lambda h: (0, 0)),
            pl.BlockSpec((LANE, GLA_DK), lambda h: (0, h)),
            pl.BlockSpec((1, GLA_DK), lambda h: (0, h))]


def gla_fwd(name, z, glr, gu, gb, o_prev):
    T = z.shape[0]
    nc = T // CHUNK

    def body(q_ref, k_ref, v_ref, glr_ref, gu_ref, gb_ref, prev_ref, o_ref, S):
        del prev_ref
        gu_b, gb_v = gu_ref[...].astype(BF16), gb_ref[...]
        ri = lax.broadcasted_iota(jnp.int32, (CHUNK, CHUNK), 0)
        ci = lax.broadcasted_iota(jnp.int32, (CHUNK, CHUNK), 1)
        tril = ri >= ci
        trilf = tril.astype(F32)
        S[...] = jnp.zeros_like(S)

        def step(c, carry):
            rows = pl.ds(pl.multiple_of(c * CHUNK, CHUNK), CHUNK)
            _, _, last, _, qt, kt, kh, v = _gla_chunk(q_ref, k_ref, v_ref, glr_ref, gu_b, gb_v, rows, trilf)
            a = jnp.where(tril, _dot_nt(qt, kt), 0.0)
            s_prev = S[...]
            o_ref[rows, :] = _dot(a, v) + _dot(qt, s_prev)
            S[...] = s_prev * _state_decay(last) + _dot_tn(kh, v)
            return carry

        lax.fori_loop(0, nc, step, 0)

    n_in = 6
    return pl.pallas_call(
        body, name=name, grid=(GLA_HEADS,),
        in_specs=_gla_specs(T) + [pl.BlockSpec(memory_space=pl.ANY)],
        out_specs=pl.BlockSpec((T, GLA_DV), lambda h: (0, RET_V // GLA_DV + h)),
        out_shape=jax.ShapeDtypeStruct(o_prev.shape, F32),
        scratch_shapes=[pltpu.VMEM((GLA_DK, GLA_DV), F32)],
        input_output_aliases={n_in: 0},
        compiler_params=_params("arbitrary"))(z, z, z, glr, gu, gb, o_prev)


def gla_bwd(name, z, glr, gu, gb, do):
    T = z.shape[0]
    nc = T // CHUNK

    def body(q_ref, k_ref, v_ref, glr_ref, gu_ref, gb_ref, do_ref,
             dq_ref, dk_ref, dv_ref, dglr_ref, dgu_ref, dgb_ref, s_all, dS):
        gu_b, gb_v = gu_ref[...].astype(BF16), gb_ref[...]
        ri = lax.broadcasted_iota(jnp.int32, (CHUNK, CHUNK), 0)
        ci = lax.broadcasted_iota(jnp.int32, (CHUNK, CHUNK), 1)
        tril = ri >= ci
        trilf = tril.astype(F32)
        triuf = (ri <= ci).astype(F32)
        last_row = lax.broadcasted_iota(jnp.int32, (CHUNK, GLA_DK), 0) == CHUNK - 1
        ones8 = jnp.ones((8, GLA_DV), F32)

        def fstep(c, s_prev):
            rows = pl.ds(pl.multiple_of(c * CHUNK, CHUNK), CHUNK)
            s_all[c] = s_prev
            _, _, last, _, _, _, kh, v = _gla_chunk(q_ref, k_ref, v_ref, glr_ref, gu_b, gb_v, rows, trilf)
            return s_prev * _state_decay(last) + _dot_tn(kh, v)

        lax.fori_loop(0, nc, fstep, jnp.zeros((GLA_DK, GLA_DV), F32))
        dS[...] = jnp.zeros_like(dS)
        dgu_ref[...] = jnp.zeros_like(dgu_ref)
        dgb_ref[...] = jnp.zeros_like(dgb_ref)

        def bstep(i, carry):
            c = nc - 1 - i
            rows = pl.ds(pl.multiple_of(c * CHUNK, CHUNK), CHUNK)
            zg, cum, last, ecum, qt, kt, kh, v = _gla_chunk(q_ref, k_ref, v_ref, glr_ref, gu_b, gb_v, rows, trilf)
            a = jnp.where(tril, _dot_nt(qt, kt), 0.0)
            s_prev, ds_new = s_all[c], dS[...]
            dout = do_ref[rows, :].astype(BF16)
            dv_ref[rows, :] = (_dot_tn(a, dout) + _dot(kh, ds_new)).astype(dv_ref.dtype)
            da = jnp.where(tril, _dot_nt(dout, v), 0.0)
            dqt = _dot(da, kt) + _dot_nt(dout, s_prev)
            dkt = _dot_tn(da, qt)
            dkh = _dot_nt(v, ds_new)
            decay = _state_decay(last)
            dS[...] = ds_new * decay + _dot_tn(qt, dout)
            dq_ref[rows, :] = (dqt * ecum * (GLA_DK ** -0.5)).astype(dq_ref.dtype)
            dk_ref[rows, :] = (dkt * jnp.exp(-cum) + dkh * jnp.exp(last - cum)).astype(dk_ref.dtype)
            dkh_kh = dkh * kh
            dcum = dqt * qt - dkt * kt - dkh_kh
            rs = lax.dot_general(ones8, ds_new * s_prev, _NT, precision=HIGHEST, preferred_element_type=F32)
            dlast = (jnp.sum(dkh_kh, axis=0, keepdims=True)
                     + jnp.exp(last) * (jnp.sum(rs, axis=0, keepdims=True) * 0.125))
            dcum = dcum + jnp.where(last_row, dlast, 0.0)
            dla = jnp.dot(triuf, dcum, precision=HIGHEST, preferred_element_type=F32)
            dzg = dla * (1.0 / GLA_GATE_NORM) * _sigmoid(-zg)
            glr_c = glr_ref[rows, :]
            dglr_ref[rows, :] = _dot_nt(dzg, gu_b)
            dgu_ref[...] += _dot_tn(glr_c, dzg)
            dgb_ref[...] += jnp.sum(dzg, axis=0, keepdims=True)
            return carry

        lax.fori_loop(0, nc, bstep, 0)

    return pl.pallas_call(
        body, name=name, grid=(GLA_HEADS,),
        in_specs=_gla_specs(T) + [pl.BlockSpec((T, GLA_DV), lambda h: (0, RET_V // GLA_DV + h))],
        out_specs=[pl.BlockSpec((T, GLA_DK), lambda h: (0, h)), pl.BlockSpec((T, GLA_DK), lambda h: (0, h)),
                   pl.BlockSpec((T, GLA_DV), lambda h: (0, h)),
                   pl.BlockSpec((None, T, LANE), lambda h: (h, 0, 0)),
                   pl.BlockSpec((LANE, GLA_DK), lambda h: (0, h)), pl.BlockSpec((1, GLA_DK), lambda h: (0, h))],
        out_shape=[jax.ShapeDtypeStruct((T, GLA_QK), BF16), jax.ShapeDtypeStruct((T, GLA_QK), BF16),
                   jax.ShapeDtypeStruct((T, GLA_V), BF16), jax.ShapeDtypeStruct((GLA_HEADS, T, LANE), F32),
                   jax.ShapeDtypeStruct((LANE, GLA_QK), F32), jax.ShapeDtypeStruct((1, GLA_QK), F32)],
        scratch_shapes=[pltpu.VMEM((nc, GLA_DK, GLA_DV), F32), pltpu.VMEM((GLA_DK, GLA_DV), F32)],
        compiler_params=_params("arbitrary"))(z, z, z, glr, gu, gb, do)


HN_HEADS = RET_HEADS + GLA_HEADS
HN_W = RET_DV


def _gate_col(h):
    return jnp.where(h < RET_HEADS, OFF_RG // HN_W + h, OFF_GG // HN_W + h - RET_HEADS)


def headnorm_fwd(name, oraw, z, w, tr=256):
    T = oraw.shape[0]

    def body(o_ref, g_ref, w_ref, y_ref):
        y_ref[...] = (_rms(o_ref[...], w_ref[...]) * _silu_and_grad(g_ref[...])[0]).astype(y_ref.dtype)

    return pl.pallas_call(
        body, name=name, grid=(HN_HEADS, T // tr),
        in_specs=[pl.BlockSpec((tr, HN_W), lambda h, i: (i, h)),
                  pl.BlockSpec((tr, HN_W), lambda h, i: (i, _gate_col(h))),
                  pl.BlockSpec((1, HN_W), lambda h, i: (0, h))],
        out_specs=pl.BlockSpec((tr, HN_W), lambda h, i: (i, h)),
        out_shape=jax.ShapeDtypeStruct((T, HN_HEADS * HN_W), BF16),
        compiler_params=_params("arbitrary", "arbitrary"))(oraw, z, w)


def headnorm_bwd(name, oraw, z, w, dy, tr=256):
    T = oraw.shape[0]

    def body(o_ref, g_ref, w_ref, dy_ref, do_ref, dg_ref, dw_ref):
        o, wv, dyv = o_ref[...], w_ref[...], dy_ref[...].astype(F32)
        silu, dsilu = _silu_and_grad(g_ref[...])
        n = _rms(o, wv)
        dg_ref[...] = (dyv * n * dsilu).astype(dg_ref.dtype)
        dx, dw = _rms_bwd(o, wv, dyv * silu)
        do_ref[...] = dx
        _accumulate(dw_ref, dw, pl.program_id(1) == 0)

    blk = pl.BlockSpec((tr, HN_W), lambda h, i: (i, h))
    return pl.pallas_call(
        body, name=name, grid=(HN_HEADS, T // tr),
        in_specs=[blk, pl.BlockSpec((tr, HN_W), lambda h, i: (i, _gate_col(h))),
                  pl.BlockSpec((1, HN_W), lambda h, i: (0, h)), blk],
        out_specs=[blk, blk, pl.BlockSpec((1, HN_W), lambda h, i: (0, h))],
        out_shape=[jax.ShapeDtypeStruct((T, HN_HEADS * HN_W), F32),
                   jax.ShapeDtypeStruct((T, HN_HEADS * HN_W), BF16),
                   jax.ShapeDtypeStruct((1, HN_HEADS * HN_W), F32)],
        compiler_params=_params("arbitrary", "arbitrary"))(oraw, z, w, dy)


def _multiplicity(qi, kb):
    ri = lax.broadcasted_iota(jnp.int32, (BLK, BLK), 0)
    ci = lax.broadcasted_iota(jnp.int32, (BLK, BLK), 1)
    dt = (qi - kb) * BLK + ri - ci
    mult = jnp.zeros((BLK, BLK), F32)
    for window, dilation in DILATED_BRANCHES:
        hit = (dt >= 0) & (dt <= window) & ((dt & (dilation - 1)) == 0)
        mult = mult + hit.astype(F32)
    return mult


def attn_fwd(name, qkv):
    T = qkv.shape[0]
    D = qkv.shape[1] // 3
    dh = D // ATT_HEADS
    nq = T // BLK
    scale = dh ** -0.5

    def body(q_ref, k_ref, v_ref, o_ref, lse_ref):
        qi = pl.program_id(1)
        q = q_ref[...]

        def step(kb, carry):
            m, l, acc = carry
            krows = pl.ds(pl.multiple_of(kb * BLK, BLK), BLK)
            mult = _multiplicity(qi, kb)
            s = jnp.where(mult > 0, _dot_nt(q, k_ref[krows, :]) * scale, -1e30)
            m_new = jnp.maximum(m, jnp.max(s, axis=-1, keepdims=True))
            alpha = jnp.exp(m - m_new)
            p = mult * jnp.exp(s - m_new)
            l = l * alpha + jnp.sum(p, axis=-1, keepdims=True)
            acc = acc * alpha + _dot(p, v_ref[krows, :])
            return m_new, l, acc

        m, l, acc = lax.fori_loop(
            0, qi + 1, step,
            (jnp.full((BLK, 1), -1e30, F32), jnp.zeros((BLK, 1), F32), jnp.zeros((BLK, dh), F32)))
        o_ref[...] = (acc / l).astype(o_ref.dtype)
        lse_ref[...] = jnp.broadcast_to(m + jnp.log(l), (BLK, LANE))

    return pl.pallas_call(
        body, name=name, grid=(ATT_HEADS, nq),
        in_specs=[pl.BlockSpec((BLK, dh), lambda h, i: (i, h)),
                  pl.BlockSpec((T, dh), lambda h, i: (0, ATT_HEADS + h)),
                  pl.BlockSpec((T, dh), lambda h, i: (0, 2 * ATT_HEADS + h))],
        out_specs=[pl.BlockSpec((BLK, dh), lambda h, i: (i, h)),
                   pl.BlockSpec((None, BLK, LANE), lambda h, i: (h, i, 0))],
        out_shape=[jax.ShapeDtypeStruct((T, D), BF16), jax.ShapeDtypeStruct((ATT_HEADS, T, LANE), F32)],
        compiler_params=_params("arbitrary", "arbitrary"))(qkv, qkv, qkv)


def attn_bwd(name, qkv, o, lse, do):
    T = qkv.shape[0]
    D = qkv.shape[1] // 3
    dh = D // ATT_HEADS
    nq = T // BLK
    scale = dh ** -0.5

    def body(q_ref, k_ref, v_ref, o_ref, lse_ref, do_ref, dq_ref, dk_ref, dv_ref, dk_acc, dv_acc):
        qi = pl.program_id(1)

        @pl.when(qi == 0)
        def _():
            dk_acc[...] = jnp.zeros_like(dk_acc)
            dv_acc[...] = jnp.zeros_like(dv_acc)

        q, dout = q_ref[...], do_ref[...]
        delta = jnp.sum(dout.astype(F32) * o_ref[...].astype(F32), axis=-1, keepdims=True)
        lse = jnp.concatenate([lse_ref[...]] * (BLK // LANE), axis=1)

        def step(kb, dq):
            krows = pl.ds(pl.multiple_of(kb * BLK, BLK), BLK)
            kk, vv = k_ref[krows, :], v_ref[krows, :]
            mult = _multiplicity(qi, kb)
            s = jnp.where(mult > 0, _dot_nt(q, kk) * scale, -1e30)
            p = mult * jnp.exp(s - lse)
            ds = (p * (_dot_nt(dout, vv) - delta) * scale).astype(BF16)
            dk_acc[krows, :] += _dot_tn(ds, q)
            dv_acc[krows, :] += _dot_tn(p, dout)
            return dq + _dot(ds, kk)

        dq_ref[...] = lax.fori_loop(0, qi + 1, step, jnp.zeros((BLK, dh), F32)).astype(dq_ref.dtype)

        @pl.when(qi == nq - 1)
        def _():
            dk_ref[...] = dk_acc[...].astype(dk_ref.dtype)
            dv_ref[...] = dv_acc[...].astype(dv_ref.dtype)

    blk = pl.BlockSpec((BLK, dh), lambda h, i: (i, h))
    full = pl.BlockSpec((T, dh), lambda h, i: (0, h))
    return pl.pallas_call(
        body, name=name, grid=(ATT_HEADS, nq),
        in_specs=[blk, pl.BlockSpec((T, dh), lambda h, i: (0, ATT_HEADS + h)),
                  pl.BlockSpec((T, dh), lambda h, i: (0, 2 * ATT_HEADS + h)),
                  blk, pl.BlockSpec((None, BLK, LANE), lambda h, i: (h, i, 0)), blk],
        out_specs=[blk, full, full],
        out_shape=[jax.ShapeDtypeStruct((T, D), BF16)] * 3,
        scratch_shapes=[pltpu.VMEM((T, dh), F32), pltpu.VMEM((T, dh), F32)],
        compiler_params=_params("arbitrary", "arbitrary"))(qkv, qkv, qkv, o, lse, do)


def exchange(name, items):
    ins, out_shapes, plans = [], [], []
    for kind, a in items:
        o = len(out_shapes)
        if kind == "gather":
            plans.append((o, len(ins), None, kind))
            ins.append(a)
            out_shapes.append(jax.ShapeDtypeStruct((N_DEV,) + a.shape, a.dtype))
        elif kind == "gather_layers":
            for l in range(a.shape[0]):
                plans.append((o, len(ins), l, kind))
            ins.append(a)
            out_shapes.append(jax.ShapeDtypeStruct((a.shape[0], N_DEV) + a.shape[1:], a.dtype))
        else:
            for l, al in enumerate(a):
                plans.append((o, len(ins), l, kind))
                ins.append(al)
            out_shapes.append(jax.ShapeDtypeStruct((len(a),) + a[0].shape, a[0].dtype))
    n_in, n_out, n_copy = len(ins), len(out_shapes), len(plans)

    def body(*refs):
        in_refs, out_refs = refs[:n_in], refs[n_in:n_in + n_out]
        send_sems, recv_sems, local_sems = refs[n_in + n_out:]
        mx, my, mc = lax.axis_index("x"), lax.axis_index("y"), lax.axis_index("c")
        me = 4 * mx + 2 * my + mc

        def src_dst(plan, to):
            o, i, l, kind = plan
            if kind == "gather":
                return in_refs[i], out_refs[o].at[me]
            if kind == "gather_layers":
                return in_refs[i].at[l], out_refs[o].at[l, me]
            return in_refs[i].at[to], out_refs[o].at[l, me]

        copies = []
        for n, plan in enumerate(plans):
            src, dst = src_dst(plan, me)
            cp = pltpu.make_async_copy(src, dst, local_sems.at[n])
            cp.start()
            copies.append(cp)
        remote = []
        for k in range(1, N_DEV):
            px, py, pc = mx ^ (k >> 2), my ^ ((k >> 1) & 1), mc ^ (k & 1)
            to = 4 * px + 2 * py + pc
            for n, plan in enumerate(plans):
                src, dst = src_dst(plan, to)
                cp = pltpu.make_async_remote_copy(
                    src_ref=src, dst_ref=dst, send_sem=send_sems.at[n, k - 1], recv_sem=recv_sems.at[n, k - 1],
                    device_id=(px, py, pc), device_id_type=pl.DeviceIdType.MESH)
                cp.start()
                remote.append(cp)
        for cp in remote:
            cp.wait_recv()
        for cp in remote:
            cp.wait_send()
        for cp in copies:
            cp.wait()

    any_spec = pl.BlockSpec(memory_space=pl.ANY)
    outs = pl.pallas_call(
        body, name=name, in_specs=[any_spec] * n_in, out_specs=[any_spec] * n_out, out_shape=out_shapes,
        scratch_shapes=[pltpu.SemaphoreType.DMA((n_copy, N_DEV - 1)), pltpu.SemaphoreType.DMA((n_copy, N_DEV - 1)),
                        pltpu.SemaphoreType.DMA((n_copy,))],
        compiler_params=pltpu.CompilerParams(has_side_effects=True))(*ins)
    return outs


def _adamw_math(w, g, m, v):
    m2 = ADAM_B1 * m + (1.0 - ADAM_B1) * g
    v2 = ADAM_B2 * v + (1.0 - ADAM_B2) * (g * g)
    m_hat = m2 / (1.0 - ADAM_B1 ** ADAM_STEP)
    v_hat = v2 / (1.0 - ADAM_B2 ** ADAM_STEP)
    delta = -ADAM_LR * (m_hat / (jnp.sqrt(v_hat) + ADAM_EPS) + ADAM_WD * w)
    return delta, m2, v2


def adamw(name, w, m, v, parts):
    L, r, c = w.shape
    cp = parts.shape[3]
    tr = _pick(r, (256, 176, 128, 64, 32, 16, 8))

    def body(w_ref, m_ref, v_ref, p_ref, g_ref, d_ref, m2_ref, v2_ref):
        g = p_ref[0, :, pl.ds(0, c)].astype(F32)
        for s in range(1, N_DEV):
            g = g + p_ref[s, :, pl.ds(0, c)].astype(F32)
        delta, m2, v2 = _adamw_math(w_ref[...], g, m_ref[...], v_ref[...])
        g_ref[...] = g
        d_ref[...] = delta
        m2_ref[...] = m2
        v2_ref[...] = v2

    blk = pl.BlockSpec((None, tr, c), lambda l, i: (l, i, 0))
    shape = jax.ShapeDtypeStruct((L, r, c), F32)
    return pl.pallas_call(
        body, name=name, grid=(L, r // tr),
        in_specs=[blk, blk, blk, pl.BlockSpec((None, N_DEV, tr, cp), lambda l, i: (l, 0, i, 0))],
        out_specs=[blk] * 4, out_shape=[shape] * 4,
        compiler_params=_params("parallel", "parallel"))(w, m, v, parts)


def adamw_small(name, w, m, v, parts):
    n = w.shape[1]

    def body(w_ref, m_ref, v_ref, p_ref, g_ref, d_ref, m2_ref, v2_ref):
        g = p_ref[0:1, :]
        for s in range(1, N_DEV):
            g = g + p_ref[s:s + 1, :]
        delta, m2, v2 = _adamw_math(w_ref[...], g, m_ref[...], v_ref[...])
        g_ref[...] = g
        d_ref[...] = delta
        m2_ref[...] = m2
        v2_ref[...] = v2

    shape = jax.ShapeDtypeStruct((1, n), F32)
    return pl.pallas_call(body, name=name, out_shape=[shape] * 4,
                          compiler_params=pltpu.CompilerParams(vmem_limit_bytes=VMEM_LIMIT_BYTES))(w, m, v, parts)


def _rope_tables(positions):
    half = RET_DK // 2
    inv_freq = 1.0 / jnp.power(RET_THETA_BASE, jnp.linspace(0.0, 1.0, half, dtype=F32))
    ang = positions.astype(F32)[:, None] * inv_freq
    cos, sin = jnp.cos(ang), jnp.sin(ang)
    cosf = jnp.repeat(cos, 2, axis=-1)
    sins = jnp.stack([-sin, sin], axis=-1).reshape(cosf.shape)
    return cosf, sins


def _pad_to(a, axis, size):
    pad = [(0, 0)] * a.ndim
    pad[axis] = (0, size - a.shape[axis])
    return jnp.pad(a, pad)


def _round_up(n, m):
    return -(-n // m) * m


def kernel(x, p, positions, attn_norm_w, ffn_norm_w, ple_norm_w, final_norm_w, ab_w_in, ab_gla_gate_up, ab_gla_gate_b, ab_ret_norm_w, ab_gla_norm_w, ab_w_out, c_w_qkv, c_w_out, ffn_w_gate, ffn_w_up, ffn_w_down, ple_w_proj, ple_w_gate, loss_target, m_attn_norm_w, m_ffn_norm_w, m_ple_norm_w, m_final_norm_w, m_ab_w_in, m_ab_gla_gate_up, m_ab_gla_gate_b, m_ab_ret_norm_w, m_ab_gla_norm_w, m_ab_w_out, m_c_w_qkv, m_c_w_out, m_ffn_w_gate, m_ffn_w_up, m_ffn_w_down, m_ple_w_proj, m_ple_w_gate, v_attn_norm_w, v_ffn_norm_w, v_ple_norm_w, v_final_norm_w, v_ab_w_in, v_ab_gla_gate_up, v_ab_gla_gate_b, v_ab_ret_norm_w, v_ab_gla_norm_w, v_ab_w_out, v_c_w_qkv, v_c_w_out, v_ffn_w_gate, v_ffn_w_up, v_ffn_w_down, v_ple_w_proj, v_ple_w_gate):
    T, D = x.shape[1], x.shape[2]
    depth = attn_norm_w.shape[0]
    assert ab_w_in.shape[0] == 1 and c_w_qkv.shape[0] == 1 and depth == 2, "one even and one odd layer"
    me = 4 * lax.axis_index("x") + 2 * lax.axis_index("y") + lax.axis_index("c")
    in_shard = ab_w_in.shape[2]
    in_width = in_shard * N_DEV
    assert in_width == OFF_LR + GLA_GATE_RANK
    fs = ffn_w_gate.shape[2]
    fp = _round_up(fs, LANE)
    gu_cols = ab_gla_gate_up.shape[2]

    bf = lambda a: a.astype(BF16)
    (w_in_g, w_oab_g, w_qkv_g, w_oc_g, wg_g, wu_g, wd_g, wpp_g, wpg_g, gu_g) = exchange(
        "gather_weights",
        [("gather", bf(ab_w_in[0])), ("gather", bf(ab_w_out[0])), ("gather", bf(c_w_qkv[0])),
         ("gather", bf(c_w_out[0])),
         ("gather_layers", _pad_to(bf(ffn_w_gate), 2, fp)), ("gather_layers", _pad_to(bf(ffn_w_up), 2, fp)),
         ("gather_layers", _pad_to(bf(ffn_w_down), 1, fp)),
         ("gather_layers", bf(ple_w_proj)), ("gather_layers", bf(ple_w_gate)),
         ("gather", ab_gla_gate_up[0])])
    w_in_full = w_in_g.transpose(1, 0, 2).reshape(D, in_width)
    w_main = w_in_full[:, :OFF_LR].reshape(1, 1, D, OFF_LR)
    w_lr = _pad_to(w_in_full[:, OFF_LR:], 1, LANE).reshape(1, 1, D, LANE)
    w_oab = w_oab_g.reshape(1, 1, D, D)
    w_qkv = w_qkv_g.reshape((1,) + w_qkv_g.shape)
    w_oc = w_oc_g.reshape(1, 1, D, D)
    wd = wd_g.reshape(depth, 1, N_DEV * fp, D)
    wpg = wpg_g.reshape(depth, 1, D, D)
    gu_full = _pad_to(gu_g.transpose(1, 0, 2).reshape(GLA_GATE_RANK, GLA_QK), 0, LANE)
    gb = ab_gla_gate_b
    hn_w = jnp.concatenate([ab_ret_norm_w, ab_gla_norm_w], axis=1)
    cosf, sins = _rope_tables(positions[0])
    p_bf = bf(p[:, 0])

    xs = x[0]
    saved = []
    for i in range(depth):
        nm = f"l{i}_"
        w_attn, w_ffn, w_ple = attn_norm_w[i:i + 1], ffn_norm_w[i:i + 1], ple_norm_w[i:i + 1]
        (xn,) = rowwise(nm + "norm_attn", lambda a, w: (_rms(a, w),), T, [("row", xs), ("full", w_attn)],
                        [("row", D, BF16)])
        if i % 2 == 0:
            z = mm_nn(nm + "mm_in", xn, w_main, 0, F32)
            glr = mm_nn(nm + "mm_lr", xn, w_lr, 0, F32)
            oraw = retention_fwd(nm + "ret_fwd", z, cosf, sins, RET_V + GLA_V)
            oraw = gla_fwd(nm + "gla_fwd", z, glr, gu_full, gb, oraw)
            o = headnorm_fwd(nm + "headnorm_fwd", oraw, z, hn_w)
            mix = mm_nn(nm + "mm_out", o, w_oab, 0, F32)
            mixer_saved = (z, glr, oraw, o)
        else:
            qkv = mm_nn(nm + "mm_qkv", xn, w_qkv, 0, BF16)
            o, lse = attn_fwd(nm + "attn_fwd", qkv)
            mix = mm_nn(nm + "mm_out", o, w_oc, 0, F32)
            mixer_saved = (qkv, o, lse)
        h1, hn = rowwise(nm + "add_norm_ffn", lambda a, b, w: (a + b, _rms(a + b, w)), T,
                         [("row", xs), ("row", mix), ("full", w_ffn)], [("row", D, F32), ("row", D, BF16)])
        g = mm_nn(nm + "mm_gate", hn, wg_g, i, F32)
        u = mm_nn(nm + "mm_up", hn, wu_g, i, F32)
        (act,) = rowwise(nm + "swiglu", lambda a, b: (_silu_and_grad(a)[0] * b,), T, [("row", g), ("row", u)],
                         [("row", g.shape[1], BF16)])
        f = mm_nn(nm + "mm_down", act, wd, i, F32)
        h2, pn = rowwise(nm + "add_norm_ple", lambda a, b, w: (a + b, _rms(a + b, w)), T,
                         [("row", h1), ("row", f), ("full", w_ple)], [("row", D, F32), ("row", D, BF16)])
        s = mm_nn(nm + "mm_ple_gate", pn, wpg, i, F32)
        e = mm_nn(nm + "mm_ple_proj", p_bf[i], wpp_g, i, F32)
        (x_next,) = rowwise(nm + "ple_out", lambda a, b, c: (a + _sigmoid(b) * c,), T,
                            [("row", h2), ("row", s), ("row", e)], [("row", D, F32)])
        saved.append((xs, xn, mixer_saved, h1, hn, g, u, act, h2, pn, s, e))
        xs = x_next

    def loss_fn(a, w, t):
        diff = _rms(a, w) - t
        dx, dw = _rms_bwd(a, w, diff * (1.0 / D))
        part = 0.5 * jnp.sum(jnp.mean(diff * diff, axis=-1, keepdims=True), axis=0, keepdims=True)
        return dx, dw, jnp.broadcast_to(part, (1, LANE))

    dx, d_final_w, loss_part = rowwise("loss_head", loss_fn, T,
                                       [("row", xs), ("full", final_norm_w[None, :]), ("row", loss_target[0])],
                                       [("row", D, F32), ("acc", D), ("acc", LANE)])
    loss = lax.psum(loss_part[0, 0], ("x", "y", "c"))

    grads = {}
    d_attn_w, d_ffn_w, d_ple_w = [None] * depth, [None] * depth, [None] * depth
    for i in reversed(range(depth)):
        nm = f"l{i}_b_"
        xs_i, xn, mixer_saved, h1, hn, g, u, act, h2, pn, s, e = saved[i]
        w_attn, w_ffn, w_ple = attn_norm_w[i:i + 1], ffn_norm_w[i:i + 1], ple_norm_w[i:i + 1]

        def ple_bwd(d, sv, ev):
            gate = _sigmoid(sv)
            return d * gate, d * ev * gate * (1.0 - gate)

        de, ds = rowwise(nm + "ple_out", ple_bwd, T, [("row", dx), ("row", s), ("row", e)],
                         [("row", D, BF16), ("row", D, BF16)])
        grads[("ple_w_proj", i)] = mm_tn(nm + "mm_ple_proj_w", p_bf[i], de, N_DEV, BF16)
        grads[("ple_w_gate", i)] = mm_tn(nm + "mm_ple_gate_w", pn, ds, 1, BF16).reshape(N_DEV, D // N_DEV, D)
        dpn = mm_nt(nm + "mm_ple_gate_x", ds, wpg, i, F32)

        def norm_bwd_add(a, w, dn, dres):
            dxx, dw = _rms_bwd(a, w, dn)
            tot = dres + dxx
            return tot, tot, dw

        dh2, dh2_bf, d_ple_w[i] = rowwise(nm + "norm_ple", norm_bwd_add, T,
                                          [("row", h2), ("full", w_ple), ("row", dpn), ("row", dx)],
                                          [("row", D, F32), ("row", D, BF16), ("acc", D)])
        grads[("ffn_w_down", i)] = mm_tn(nm + "mm_down_w", act, dh2_bf, 1, BF16).reshape(N_DEV, fp, D)
        dact = mm_nt(nm + "mm_down_x", dh2_bf, wd, i, F32)

        def swiglu_bwd(da, gv, uv):
            silu, dsilu = _silu_and_grad(gv)
            return da * uv * dsilu, da * silu

        dg, du = rowwise(nm + "swiglu", swiglu_bwd, T, [("row", dact), ("row", g), ("row", u)],
                         [("row", g.shape[1], BF16), ("row", g.shape[1], BF16)])
        grads[("ffn_w_gate", i)] = mm_tn(nm + "mm_gate_w", hn, dg, N_DEV, BF16)
        grads[("ffn_w_up", i)] = mm_tn(nm + "mm_up_w", hn, du, N_DEV, BF16)
        dhn_g = mm_nt(nm + "mm_gate_x", dg, wg_g, i, F32)
        dhn_u = mm_nt(nm + "mm_up_x", du, wu_g, i, F32)

        def norm_bwd_add2(a, w, dn1, dn2, dres):
            dxx, dw = _rms_bwd(a, w, dn1 + dn2)
            tot = dres + dxx
            return tot, tot, dw

        dh1, dh1_bf, d_ffn_w[i] = rowwise(nm + "norm_ffn", norm_bwd_add2, T,
                                          [("row", h1), ("full", w_ffn), ("row", dhn_g), ("row", dhn_u), ("row", dh2)],
                                          [("row", D, F32), ("row", D, BF16), ("acc", D)])
        if i % 2 == 0:
            z, glr, oraw, o = mixer_saved
            grads[("ab_w_out", 0)] = mm_tn(nm + "mm_out_w", o, dh1_bf, 1, BF16).reshape(N_DEV, D // N_DEV, D)
            do = mm_nt(nm + "mm_out_x", dh1_bf, w_oab, 0, F32)
            d_oraw, d_gates, d_hn_w = headnorm_bwd(nm + "headnorm", oraw, z, hn_w, do)
            d_rq, d_rk, d_rv = retention_bwd(nm + "ret", z, cosf, sins, d_oraw)
            d_gq, d_gk, d_gv, d_glr4, d_gu, d_gb = gla_bwd(nm + "gla", z, glr, gu_full, gb, d_oraw)
            dz = jnp.concatenate([d_rq, d_rk, d_rv, d_gates[:, :RET_V], d_gq, d_gk, d_gv, d_gates[:, RET_V:]], axis=1)
            (d_glr,) = rowwise(nm + "sum_lr", lambda *a: (a[0] + a[1] + a[2] + a[3],), T,
                               [("row", d_glr4[hh]) for hh in range(GLA_HEADS)], [("row", LANE, BF16)])
            dw_main = mm_tn(nm + "mm_in_w", xn, dz, 1, BF16)[0]
            dw_lr = mm_tn(nm + "mm_lr_w", xn, d_glr, 1, BF16)[0]
            dw_in = jnp.concatenate([dw_main, dw_lr[:, :GLA_GATE_RANK]], axis=1)
            grads[("ab_w_in", 0)] = dw_in.reshape(D, N_DEV, in_shard).transpose(1, 0, 2)
            dxn_a = mm_nt(nm + "mm_in_x", dz, w_main, 0, F32)
            dxn_b = mm_nt(nm + "mm_lr_x", d_glr, w_lr, 0, F32)
        else:
            qkv, o, lse = mixer_saved
            grads[("c_w_out", 0)] = mm_tn(nm + "mm_out_w", o, dh1_bf, 1, BF16).reshape(N_DEV, D // N_DEV, D)
            do = mm_nt(nm + "mm_out_x", dh1_bf, w_oc, 0, BF16)
            dq, dk, dv = attn_bwd(nm + "attn", qkv, o, lse, do)
            dqkv = jnp.concatenate([dq, dk, dv], axis=1)
            grads[("c_w_qkv", 0)] = mm_tn(nm + "mm_qkv_w", xn, dqkv, N_DEV, BF16)
            dxn_a = mm_nt(nm + "mm_qkv_x", dqkv, w_qkv, 0, F32)
            dxn_b = None
        if dxn_b is None:
            dx, _, d_attn_w[i] = rowwise(nm + "norm_attn", norm_bwd_add, T,
                                         [("row", xs_i), ("full", w_attn), ("row", dxn_a), ("row", dh1)],
                                         [("row", D, F32), ("row", D, BF16), ("acc", D)])
        else:
            dx, _, d_attn_w[i] = rowwise(nm + "norm_attn", norm_bwd_add2, T,
                                         [("row", xs_i), ("full", w_attn), ("row", dxn_a), ("row", dxn_b), ("row", dh1)],
                                         [("row", D, F32), ("row", D, BF16), ("acc", D)])

    small_names = ["attn_norm_w", "ffn_norm_w", "ple_norm_w", "final_norm_w", "ab_gla_gate_b", "ab_ret_norm_w",
                   "ab_gla_norm_w"]
    small_grads = [jnp.concatenate(d_attn_w, 0), jnp.concatenate(d_ffn_w, 0), jnp.concatenate(d_ple_w, 0), d_final_w[0],
                   d_gb, d_hn_w[:, :RET_V], d_hn_w[:, RET_V:]]
    small_w = [attn_norm_w, ffn_norm_w, ple_norm_w, final_norm_w, ab_gla_gate_b, ab_ret_norm_w, ab_gla_norm_w]
    small_m = [m_attn_norm_w, m_ffn_norm_w, m_ple_norm_w, m_final_norm_w, m_ab_gla_gate_b, m_ab_ret_norm_w, m_ab_gla_norm_w]
    small_v = [v_attn_norm_w, v_ffn_norm_w, v_ple_norm_w, v_final_norm_w, v_ab_gla_gate_b, v_ab_ret_norm_w, v_ab_gla_norm_w]
    sizes = [int(np.prod(a.shape)) for a in small_w]
    n_gu = GLA_GATE_RANK * GLA_QK
    n_small = _round_up(sum(sizes) + n_gu, LANE)
    pack = lambda parts: _pad_to(jnp.concatenate([a.reshape(-1) for a in parts]), 0, n_small)[None, :]
    small_part = pack(small_grads + [d_gu[:GLA_GATE_RANK]])

    big = ["ab_w_in", "ab_w_out", "c_w_qkv", "c_w_out", "ffn_w_gate", "ffn_w_up", "ffn_w_down", "ple_w_proj", "ple_w_gate"]
    layers_of = lambda n: [grads[(n, l)] for l in range(depth if (n, 1) in grads else 1)]
    received = exchange("exchange_grads", [("scatter", layers_of(n)) for n in big] + [("gather", small_part)])
    parts = dict(zip(big, received[:-1]))
    small_parts = received[-1].reshape(N_DEV, n_small)

    big_w = dict(ab_w_in=(ab_w_in, m_ab_w_in, v_ab_w_in), ab_w_out=(ab_w_out, m_ab_w_out, v_ab_w_out),
                 c_w_qkv=(c_w_qkv, m_c_w_qkv, v_c_w_qkv), c_w_out=(c_w_out, m_c_w_out, v_c_w_out),
                 ffn_w_gate=(ffn_w_gate, m_ffn_w_gate, v_ffn_w_gate), ffn_w_up=(ffn_w_up, m_ffn_w_up, v_ffn_w_up),
                 ffn_w_down=(ffn_w_down, m_ffn_w_down, v_ffn_w_down), ple_w_proj=(ple_w_proj, m_ple_w_proj, v_ple_w_proj),
                 ple_w_gate=(ple_w_gate, m_ple_w_gate, v_ple_w_gate))
    results = {n: adamw("adamw_" + n, *big_w[n], parts[n]) for n in big}

    gu_off = sum(sizes)
    own_cols = lambda a: lax.dynamic_slice_in_dim(a.reshape(GLA_GATE_RANK, GLA_QK), me * gu_cols, gu_cols, axis=1)
    small_res = adamw_small("adamw_small", pack(small_w + [jnp.zeros((n_gu,), F32)]),
                            pack(small_m + [jnp.zeros((n_gu,), F32)]), pack(small_v + [jnp.ones((n_gu,), F32)]),
                            small_parts)
    g_gu_full = small_res[0][0, gu_off:gu_off + n_gu]
    g_gu = own_cols(g_gu_full)[None]
    gu_res = adamw_small("adamw_gate_up", *[_pad_to(a.reshape(1, -1), 1, _round_up(a.size, LANE)) for a in
                                            (ab_gla_gate_up, m_ab_gla_gate_up, v_ab_gla_gate_up)],
                         jnp.concatenate([_pad_to(g_gu.reshape(1, -1), 1, _round_up(g_gu.size, LANE)),
                                          jnp.zeros((N_DEV - 1, _round_up(g_gu.size, LANE)), F32)], axis=0))
    for k in range(4):
        off = 0
        for n, a, sz in zip(small_names, small_w, sizes):
            results.setdefault(n, [None] * 4)[k] = small_res[k][0, off:off + sz].reshape(a.shape)
            off += sz
        results.setdefault("ab_gla_gate_up", [None] * 4)[k] = gu_res[k][0, :g_gu.size].reshape(ab_gla_gate_up.shape)

    order = ["attn_norm_w", "ffn_norm_w", "ple_norm_w", "final_norm_w", "ab_w_in", "ab_gla_gate_up", "ab_gla_gate_b",
             "ab_ret_norm_w", "ab_gla_norm_w", "ab_w_out", "c_w_qkv", "c_w_out", "ffn_w_gate", "ffn_w_up", "ffn_w_down",
             "ple_w_proj", "ple_w_gate"]
    return (loss, dx[None], *[results[n][0] for n in order], *[results[n][1] for n in order],
            *[results[n][2] for n in order], *[results[n][3] for n in order])
```

```python
import math

import numpy as np
import jax
import jax.numpy as jnp
from jax import lax
from jax.experimental import pallas as pl
from jax.experimental.pallas import tpu as pltpu

F32 = jnp.float32
BF16 = jnp.bfloat16
HIGHEST = lax.Precision.HIGHEST

N_DEV = 8
VMEM_LIMIT_BYTES = 48 * 1024 * 1024
LANE = 128
NORM_EPS = 1e-6

RET_HEADS, RET_DK, RET_DV = 4, 256, 256
RET_THETA_BASE = 10000.0
GLA_HEADS, GLA_DK, GLA_DV = 4, 128, 256
GLA_GATE_RANK = 16
GLA_GATE_NORM = 16.0
CHUNK = 64
ATT_HEADS = 16
DILATED_BRANCHES = ((128, 1), (512, 4), (2048, 16))
BLK = 256

ADAM_LR, ADAM_B1, ADAM_B2, ADAM_EPS, ADAM_WD, ADAM_STEP = 0.001, 0.9, 0.999, 1e-08, 0.01, 10

RET_QK = RET_HEADS * RET_DK
RET_V = RET_HEADS * RET_DV
GLA_QK = GLA_HEADS * GLA_DK
GLA_V = GLA_HEADS * GLA_DV
OFF_RQ, OFF_RK, OFF_RV, OFF_RG = 0, RET_QK, 2 * RET_QK, 2 * RET_QK + RET_V
OFF_GQ = OFF_RG + RET_V
OFF_GK = OFF_GQ + GLA_QK
OFF_GV = OFF_GK + GLA_QK
OFF_GG = OFF_GV + GLA_V
OFF_LR = OFF_GG + GLA_V


def _params(*sem):
    return pltpu.CompilerParams(dimension_semantics=sem or None, vmem_limit_bytes=VMEM_LIMIT_BYTES)


def _pick(n, cands):
    for c in cands:
        if n % c == 0:
            return c
    raise ValueError(f"no tile for {n} in {cands}")


_NN = (((1,), (0,)), ((), ()))
_NT = (((1,), (1,)), ((), ()))
_TN = (((0,), (0,)), ((), ()))
_ANY = pl.BlockSpec(memory_space=pl.ANY)
MAX_CONTRACT = 2048
_TILES = (1024, 768, 512, 256, 128)


def _mm_call(name, dims, grid, in_specs, out_spec, out_shape, args, deps=()):
    steps = grid[2]
    assert steps == 1 or out_shape.dtype == F32

    def body(a_ref, b_ref, *rest):
        o_ref = rest[len(deps)]
        part = lax.dot_general(a_ref[...].astype(BF16), b_ref[...].astype(BF16), dims, preferred_element_type=F32)
        if steps == 1:
            o_ref[...] = part.astype(o_ref.dtype)
        else:
            _accumulate(o_ref, part, pl.program_id(2) == 0)

    return pl.pallas_call(
        body, name=name, grid=grid, in_specs=list(in_specs) + [_ANY] * len(deps), out_specs=out_spec,
        out_shape=out_shape, compiler_params=_params("parallel", "parallel", "arbitrary"))(*args, *deps)


def mm_nn(name, a, w, l, out_dtype, deps=()):
    _, J, K, n = w.shape
    M = a.shape[0]
    tm, tn, tk = _pick(M, _TILES), _pick(n, _TILES), _pick(K, (MAX_CONTRACT,) + _TILES)
    nt = n // tn
    return _mm_call(
        name, _NN, (M // tm, J * nt, K // tk),
        [pl.BlockSpec((tm, tk), lambda i, j, k: (i, k)),
         pl.BlockSpec((None, None, tk, tn), lambda i, j, k: (l, j // nt, k, j % nt))],
        pl.BlockSpec((tm, tn), lambda i, j, k: (i, j)),
        jax.ShapeDtypeStruct((M, J * n), out_dtype), (a, w), deps)


def mm_nt(name, a, w, l, out_dtype, deps=()):
    _, J, K, n = w.shape
    M = a.shape[0]
    tm, tq, tc = _pick(M, _TILES), _pick(K, _TILES), _pick(n, (MAX_CONTRACT,) + _TILES)
    nc = n // tc
    return _mm_call(
        name, _NT, (M // tm, K // tq, J * nc),
        [pl.BlockSpec((tm, tc), lambda i, q, c: (i, c)),
         pl.BlockSpec((None, None, tq, tc), lambda i, q, c: (l, c // nc, q, c % nc))],
        pl.BlockSpec((tm, tq), lambda i, q, c: (i, q)),
        jax.ShapeDtypeStruct((M, K), out_dtype), (a, w), deps)


def mm_tn(name, x, dy, J, out_dtype, deps=()):
    M, K = x.shape
    n = dy.shape[1] // J
    tp, tn = _pick(K, _TILES), _pick(n, _TILES)
    nt = n // tn
    assert M <= MAX_CONTRACT
    return _mm_call(
        name, _TN, (K // tp, J * nt, 1),
        [pl.BlockSpec((M, tp), lambda i, j, r: (0, i)),
         pl.BlockSpec((M, tn), lambda i, j, r: (0, j))],
        pl.BlockSpec((None, tp, tn), lambda i, j, r: (j // nt, i, j % nt)),
        jax.ShapeDtypeStruct((J, K, n), out_dtype), (x, dy), deps)


def mmt_fwd(name, a, wt, l, out_dtype, n=None, deps=()):
    _, J, rows, K = wt.shape
    n = rows if n is None else n
    M = a.shape[0]
    tm, tn = _pick(M, _TILES), _pick(n, _TILES)
    nt = n // tn
    assert K <= MAX_CONTRACT
    return _mm_call(
        name, _NT, (M // tm, J * nt, 1),
        [pl.BlockSpec((tm, K), lambda i, j, k: (i, 0)),
         pl.BlockSpec((None, None, tn, K), lambda i, j, k: (l, j // nt, j % nt, 0))],
        pl.BlockSpec((tm, tn), lambda i, j, k: (i, j)),
        jax.ShapeDtypeStruct((M, J * n), out_dtype), (a, wt), deps)


def mmt_dx(name, dy, wt, l, out_dtype, n=None, deps=()):
    _, J, rows, K = wt.shape
    n = rows if n is None else n
    M = dy.shape[0]
    tm, tq, tc = _pick(M, _TILES), _pick(K, _TILES), _pick(n, _TILES)
    nc = n // tc
    return _mm_call(
        name, _NN, (M // tm, K // tq, J * nc),
        [pl.BlockSpec((tm, tc), lambda i, q, c: (i, c)),
         pl.BlockSpec((None, None, tc, tq), lambda i, q, c: (l, c // nc, c % nc, q))],
        pl.BlockSpec((tm, tq), lambda i, q, c: (i, q)),
        jax.ShapeDtypeStruct((M, K), out_dtype), (dy, wt), deps)


def mmt_dw(name, dy, x, J, out_dtype, deps=()):
    M, K = x.shape
    n = dy.shape[1] // J
    tn, tp = _pick(n, _TILES), _pick(K, _TILES)
    nt = n // tn
    assert M <= MAX_CONTRACT
    return _mm_call(
        name, _TN, (J * nt, K // tp, 1),
        [pl.BlockSpec((M, tn), lambda j, i, r: (0, j)),
         pl.BlockSpec((M, tp), lambda j, i, r: (0, i))],
        pl.BlockSpec((None, tn, tp), lambda j, i, r: (j // nt, j % nt, i)),
        jax.ShapeDtypeStruct((J, n, K), out_dtype), (dy, x), deps)


def rowwise(name, fn, rows, ins, outs, tr=256, deps=()):
    widest = max([s[1].shape[1] if s[0] != "col" else s[3] for s in ins] + [s[1] for s in outs])
    tr = min(tr if widest <= 2048 else tr // 2, rows)
    in_specs, args = [], []
    for spec in ins:
        kind, a = spec[0], spec[1]
        if kind == "row":
            in_specs.append(pl.BlockSpec((tr, a.shape[1]), lambda i: (i, 0)))
        elif kind == "col":
            cb, width = spec[2], spec[3]
            in_specs.append(pl.BlockSpec((tr, width), lambda i, cb=cb: (i, cb)))
        else:
            in_specs.append(pl.BlockSpec(a.shape, lambda i: (0, 0)))
        args.append(a)
    out_specs, out_shapes = [], []
    for spec in outs:
        if spec[0] == "row":
            out_specs.append(pl.BlockSpec((tr, spec[1]), lambda i: (i, 0)))
            out_shapes.append(jax.ShapeDtypeStruct((rows, spec[1]), spec[2]))
        else:
            out_specs.append(pl.BlockSpec((1, spec[1]), lambda i: (0, 0)))
            out_shapes.append(jax.ShapeDtypeStruct((1, spec[1]), F32))
    n_in = len(ins)

    def body(*refs):
        vals = fn(*[r[...] for r in refs[:n_in]])
        first = pl.program_id(0) == 0
        for r, v, spec in zip(refs[n_in + len(deps):], vals, outs):
            if spec[0] == "row":
                r[...] = v.astype(r.dtype)
            else:
                _accumulate(r, v, first)

    return pl.pallas_call(body, name=name, grid=(rows // tr,), in_specs=in_specs + [_ANY] * len(deps),
                          out_specs=out_specs, out_shape=out_shapes,
                          compiler_params=_params("arbitrary"))(*args, *deps)


def _accumulate(ref, v, first):
    @pl.when(first)
    def _():
        ref[...] = v

    @pl.when(jnp.logical_not(first))
    def _():
        ref[...] += v


def _rms(x, w):
    r = lax.rsqrt(jnp.mean(x * x, axis=-1, keepdims=True) + NORM_EPS)
    return x * r * w


def _rms_bwd(x, w, dy):
    r = lax.rsqrt(jnp.mean(x * x, axis=-1, keepdims=True) + NORM_EPS)
    g = dy * w
    dx = r * (g - x * (r * r) * jnp.mean(g * x, axis=-1, keepdims=True))
    dw = jnp.sum(dy * x * r, axis=0, keepdims=True)
    return dx, dw


def _sigmoid(x):
    return 1.0 / (1.0 + jnp.exp(-x))


def _silu_and_grad(g):
    s = _sigmoid(g)
    return g * s, s * (1.0 + g * (1.0 - s))


def _swap_pairs(x):
    n = x.shape[-1]
    lane = lax.broadcasted_iota(jnp.int32, x.shape, x.ndim - 1)
    return jnp.where((lane & 1) == 0, pltpu.roll(x, n - 1, x.ndim - 1), pltpu.roll(x, 1, x.ndim - 1))


def _rot(x, cosf, sins):
    return x * cosf + _swap_pairs(x) * sins


def _unrot(d, cosf, sins):
    return d * cosf + _swap_pairs(d * sins)


def _ret_log_gamma(h):
    vals = [math.log1p(-2.0 ** (-5.0 - i)) for i in range(RET_HEADS)]
    out = jnp.float32(vals[RET_HEADS - 1])
    for i in range(RET_HEADS - 2, -1, -1):
        out = jnp.where(h == i, jnp.float32(vals[i]), out)
    return out


def _decay_block(qi, kb, lg):
    ri = lax.broadcasted_iota(jnp.int32, (BLK, BLK), 0)
    ci = lax.broadcasted_iota(jnp.int32, (BLK, BLK), 1)
    dt = (qi - kb) * BLK + ri - ci
    return jnp.where(dt >= 0, jnp.exp(jnp.maximum(dt, 0).astype(F32) * lg), 0.0)


_NT = (((1,), (1,)), ((), ()))
_TN = (((0,), (0,)), ((), ()))


def _dot(a, b):
    return jnp.dot(a.astype(BF16), b.astype(BF16), preferred_element_type=F32)


def _dot_nt(a, b):
    return lax.dot_general(a.astype(BF16), b.astype(BF16), _NT, preferred_element_type=F32)


def _dot_tn(a, b):
    return lax.dot_general(a.astype(BF16), b.astype(BF16), _TN, preferred_element_type=F32)


def retention_fwd(name, z, cosf, sins, width_out):
    T = z.shape[0]
    nq = T // BLK
    scale = RET_DK ** -0.5

    def body(q_ref, k_ref, v_ref, cos_ref, sin_ref, o_ref, krot, vb):
        h, qi = pl.program_id(0), pl.program_id(1)
        lg = _ret_log_gamma(h)

        @pl.when(qi == 0)
        def _():
            krot[...] = (_rot(k_ref[...], cos_ref[...], sin_ref[...]) * scale).astype(BF16)
            vb[...] = v_ref[...].astype(BF16)

        rows = pl.ds(pl.multiple_of(qi * BLK, BLK), BLK)
        q = _rot(q_ref[...], cos_ref[rows, :], sin_ref[rows, :]).astype(BF16)

        def step(kb, acc):
            krows = pl.ds(pl.multiple_of(kb * BLK, BLK), BLK)
            s = _dot_nt(q, krot[krows, :]) * _decay_block(qi, kb, lg)
            return acc + _dot(s, vb[krows, :])

        o_ref[...] = lax.fori_loop(0, qi + 1, step, jnp.zeros((BLK, RET_DV), F32))

    return pl.pallas_call(
        body, name=name, grid=(RET_HEADS, nq),
        in_specs=[pl.BlockSpec((BLK, RET_DK), lambda h, i: (i, OFF_RQ // RET_DK + h)),
                  pl.BlockSpec((T, RET_DK), lambda h, i: (0, OFF_RK // RET_DK + h)),
                  pl.BlockSpec((T, RET_DV), lambda h, i: (0, OFF_RV // RET_DV + h)),
                  pl.BlockSpec((T, RET_DK), lambda h, i: (0, 0)),
                  pl.BlockSpec((T, RET_DK), lambda h, i: (0, 0))],
        out_specs=pl.BlockSpec((BLK, RET_DV), lambda h, i: (i, h)),
        out_shape=jax.ShapeDtypeStruct((T, width_out), F32),
        scratch_shapes=[pltpu.VMEM((T, RET_DK), BF16), pltpu.VMEM((T, RET_DV), BF16)],
        compiler_params=_params("arbitrary", "arbitrary"))(z, z, z, cosf, sins)


def retention_bwd(name, z, cosf, sins, do):
    T = z.shape[0]
    nq = T // BLK
    scale = RET_DK ** -0.5

    def body(q_ref, k_ref, v_ref, cos_ref, sin_ref, do_ref, dq_ref, dk_ref, dv_ref, krot, vb, dk_acc, dv_acc):
        h, qi = pl.program_id(0), pl.program_id(1)
        lg = _ret_log_gamma(h)

        @pl.when(qi == 0)
        def _():
            krot[...] = (_rot(k_ref[...], cos_ref[...], sin_ref[...]) * scale).astype(BF16)
            vb[...] = v_ref[...].astype(BF16)
            dk_acc[...] = jnp.zeros_like(dk_acc)
            dv_acc[...] = jnp.zeros_like(dv_acc)

        rows = pl.ds(pl.multiple_of(qi * BLK, BLK), BLK)
        cos_q, sin_q = cos_ref[rows, :], sin_ref[rows, :]
        q = _rot(q_ref[...], cos_q, sin_q).astype(BF16)
        dout = do_ref[...].astype(BF16)

        def step(kb, dq):
            krows = pl.ds(pl.multiple_of(kb * BLK, BLK), BLK)
            kk, vv = krot[krows, :], vb[krows, :]
            dec = _decay_block(qi, kb, lg)
            p = (_dot_nt(q, kk) * dec).astype(BF16)
            ds = (_dot_nt(dout, vv) * dec).astype(BF16)
            dk_acc[krows, :] += _dot_tn(ds, q)
            dv_acc[krows, :] += _dot_tn(p, dout)
            return dq + _dot(ds, kk)

        dq = lax.fori_loop(0, qi + 1, step, jnp.zeros((BLK, RET_DK), F32))
        dq_ref[...] = _unrot(dq, cos_q, sin_q).astype(dq_ref.dtype)

        @pl.when(qi == nq - 1)
        def _():
            dk_ref[...] = (_unrot(dk_acc[...], cos_ref[...], sin_ref[...]) * scale).astype(dk_ref.dtype)
            dv_ref[...] = dv_acc[...].astype(dv_ref.dtype)

    full = lambda h, i: (0, h)
    return pl.pallas_call(
        body, name=name, grid=(RET_HEADS, nq),
        in_specs=[pl.BlockSpec((BLK, RET_DK), lambda h, i: (i, OFF_RQ // RET_DK + h)),
                  pl.BlockSpec((T, RET_DK), lambda h, i: (0, OFF_RK // RET_DK + h)),
                  pl.BlockSpec((T, RET_DV), lambda h, i: (0, OFF_RV // RET_DV + h)),
                  pl.BlockSpec((T, RET_DK), lambda h, i: (0, 0)),
                  pl.BlockSpec((T, RET_DK), lambda h, i: (0, 0)),
                  pl.BlockSpec((BLK, RET_DV), lambda h, i: (i, h))],
        out_specs=[pl.BlockSpec((BLK, RET_DK), lambda h, i: (i, h)),
                   pl.BlockSpec((T, RET_DK), full), pl.BlockSpec((T, RET_DV), full)],
        out_shape=[jax.ShapeDtypeStruct((T, RET_QK), BF16), jax.ShapeDtypeStruct((T, RET_QK), BF16),
                   jax.ShapeDtypeStruct((T, RET_V), BF16)],
        scratch_shapes=[pltpu.VMEM((T, RET_DK), BF16), pltpu.VMEM((T, RET_DV), BF16),
                        pltpu.VMEM((T, RET_DK), F32), pltpu.VMEM((T, RET_DV), F32)],
        compiler_params=_params("arbitrary", "arbitrary"))(z, z, z, cosf, sins, do)


def _gla_chunk(q_ref, k_ref, v_ref, glr_ref, gu, gb, rows, trilf):
    zg = _dot(glr_ref[rows, :], gu) + gb
    la = (jnp.minimum(zg, 0.0) - jnp.log(1.0 + jnp.exp(-jnp.abs(zg)))) * (1.0 / GLA_GATE_NORM)
    cum = jnp.dot(trilf, la, precision=HIGHEST, preferred_element_type=F32)
    last = jnp.sum(la, axis=0, keepdims=True)
    ecum = jnp.exp(cum)
    k = k_ref[rows, :]
    qt = q_ref[rows, :] * (GLA_DK ** -0.5) * ecum
    kt = k * jnp.exp(-cum)
    kh = k * jnp.exp(last - cum)
    return zg, cum, last, ecum, qt, kt, kh, v_ref[rows, :].astype(BF16)


def _state_decay(last):
    e = jnp.exp(jnp.broadcast_to(last, (GLA_DK, GLA_DK)).T)
    return jnp.concatenate([e] * (GLA_DV // GLA_DK), axis=1)


def _gla_specs(T):
    return [pl.BlockSpec((T, GLA_DK), lambda h: (0, OFF_GQ // GLA_DK + h)),
            pl.BlockSpec((T, GLA_DK), lambda h: (0, OFF_GK // GLA_DK + h)),
            pl.BlockSpec((T, GLA_DV), lambda h: (0, OFF_GV // GLA_DV + h)),
            pl.BlockSpec((T, LANE), lambda h: (0, 0)),
            pl.BlockSpec((LANE, GLA_DK), lambda h: (0, h)),
            pl.BlockSpec((1, GLA_DK), lambda h: (0, h))]


def gla_fwd(name, z, glr, gu, gb, o_prev):
    T = z.shape[0]
    nc = T // CHUNK

    def body(q_ref, k_ref, v_ref, glr_ref, gu_ref, gb_ref, prev_ref, o_ref, S):
        del prev_ref
        gu_b, gb_v = gu_ref[...].astype(BF16), gb_ref[...]
        ri = lax.broadcasted_iota(jnp.int32, (CHUNK, CHUNK), 0)
        ci = lax.broadcasted_iota(jnp.int32, (CHUNK, CHUNK), 1)
        tril = ri >= ci
        trilf = tril.astype(F32)
        S[...] = jnp.zeros_like(S)

        def step(c, carry):
            rows = pl.ds(pl.multiple_of(c * CHUNK, CHUNK), CHUNK)
            _, _, last, _, qt, kt, kh, v = _gla_chunk(q_ref, k_ref, v_ref, glr_ref, gu_b, gb_v, rows, trilf)
            a = jnp.where(tril, _dot_nt(qt, kt), 0.0)
            s_prev = S[...]
            o_ref[rows, :] = _dot(a, v) + _dot(qt, s_prev)
            S[...] = s_prev * _state_decay(last) + _dot_tn(kh, v)
            return carry

        lax.fori_loop(0, nc, step, 0)

    n_in = 6
    return pl.pallas_call(
        body, name=name, grid=(GLA_HEADS,),
        in_specs=_gla_specs(T) + [pl.BlockSpec(memory_space=pl.ANY)],
        out_specs=pl.BlockSpec((T, GLA_DV), lambda h: (0, RET_V // GLA_DV + h)),
        out_shape=jax.ShapeDtypeStruct(o_prev.shape, F32),
        scratch_shapes=[pltpu.VMEM((GLA_DK, GLA_DV), F32)],
        input_output_aliases={n_in: 0},
        compiler_params=_params("arbitrary"))(z, z, z, glr, gu, gb, o_prev)


def gla_bwd(name, z, glr, gu, gb, do):
    T = z.shape[0]
    nc = T // CHUNK

    def body(q_ref, k_ref, v_ref, glr_ref, gu_ref, gb_ref, do_ref,
             dq_ref, dk_ref, dv_ref, dglr_ref, dgu_ref, dgb_ref, s_all, dS):
        gu_b, gb_v = gu_ref[...].astype(BF16), gb_ref[...]
        ri = lax.broadcasted_iota(jnp.int32, (CHUNK, CHUNK), 0)
        ci = lax.broadcasted_iota(jnp.int32, (CHUNK, CHUNK), 1)
        tril = ri >= ci
        trilf = tril.astype(F32)
        triuf = (ri <= ci).astype(F32)
        last_row = lax.broadcasted_iota(jnp.int32, (CHUNK, GLA_DK), 0) == CHUNK - 1
        ones8 = jnp.ones((8, GLA_DV), F32)

        def fstep(c, s_prev):
            rows = pl.ds(pl.multiple_of(c * CHUNK, CHUNK), CHUNK)
            s_all[c] = s_prev
            _, _, last, _, _, _, kh, v = _gla_chunk(q_ref, k_ref, v_ref, glr_ref, gu_b, gb_v, rows, trilf)
            return s_prev * _state_decay(last) + _dot_tn(kh, v)

        lax.fori_loop(0, nc, fstep, jnp.zeros((GLA_DK, GLA_DV), F32))
        dS[...] = jnp.zeros_like(dS)
        dgu_ref[...] = jnp.zeros_like(dgu_ref)
        dgb_ref[...] = jnp.zeros_like(dgb_ref)

        def bstep(i, carry):
            c = nc - 1 - i
            rows = pl.ds(pl.multiple_of(c * CHUNK, CHUNK), CHUNK)
            zg, cum, last, ecum, qt, kt, kh, v = _gla_chunk(q_ref, k_ref, v_ref, glr_ref, gu_b, gb_v, rows, trilf)
            a = jnp.where(tril, _dot_nt(qt, kt), 0.0)
            s_prev, ds_new = s_all[c], dS[...]
            dout = do_ref[rows, :].astype(BF16)
            dv_ref[rows, :] = (_dot_tn(a, dout) + _dot(kh, ds_new)).astype(dv_ref.dtype)
            da = jnp.where(tril, _dot_nt(dout, v), 0.0)
            dqt = _dot(da, kt) + _dot_nt(dout, s_prev)
            dkt = _dot_tn(da, qt)
            dkh = _dot_nt(v, ds_new)
            decay = _state_decay(last)
            dS[...] = ds_new * decay + _dot_tn(qt, dout)
            dq_ref[rows, :] = (dqt * ecum * (GLA_DK ** -0.5)).astype(dq_ref.dtype)
            dk_ref[rows, :] = (dkt * jnp.exp(-cum) + dkh * jnp.exp(last - cum)).astype(dk_ref.dtype)
            dkh_kh = dkh * kh
            dcum = dqt * qt - dkt * kt - dkh_kh
            rs = lax.dot_general(ones8, ds_new * s_prev, _NT, precision=HIGHEST, preferred_element_type=F32)
            dlast = (jnp.sum(dkh_kh, axis=0, keepdims=True)
                     + jnp.exp(last) * (jnp.sum(rs, axis=0, keepdims=True) * 0.125))
            dcum = dcum + jnp.where(last_row, dlast, 0.0)
            dla = jnp.dot(triuf, dcum, precision=HIGHEST, preferred_element_type=F32)
            dzg = dla * (1.0 / GLA_GATE_NORM) * _sigmoid(-zg)
            glr_c = glr_ref[rows, :]
            dglr_ref[rows, :] = _dot_nt(dzg, gu_b)
            dgu_ref[...] += _dot_tn(glr_c, dzg)
            dgb_ref[...] += jnp.sum(dzg, axis=0, keepdims=True)
            return carry

        lax.fori_loop(0, nc, bstep, 0)

    return pl.pallas_call(
        body, name=name, grid=(GLA_HEADS,),
        in_specs=_gla_specs(T) + [pl.BlockSpec((T, GLA_DV), lambda h: (0, RET_V // GLA_DV + h))],
        out_specs=[pl.BlockSpec((T, GLA_DK), lambda h: (0, h)), pl.BlockSpec((T, GLA_DK), lambda h: (0, h)),
                   pl.BlockSpec((T, GLA_DV), lambda h: (0, h)),
                   pl.BlockSpec((None, T, LANE), lambda h: (h, 0, 0)),
                   pl.BlockSpec((LANE, GLA_DK), lambda h: (0, h)), pl.BlockSpec((1, GLA_DK), lambda h: (0, h))],
        out_shape=[jax.ShapeDtypeStruct((T, GLA_QK), BF16), jax.ShapeDtypeStruct((T, GLA_QK), BF16),
                   jax.ShapeDtypeStruct((T, GLA_V), BF16), jax.ShapeDtypeStruct((GLA_HEADS, T, LANE), F32),
                   jax.ShapeDtypeStruct((LANE, GLA_QK), F32), jax.ShapeDtypeStruct((1, GLA_QK), F32)],
        scratch_shapes=[pltpu.VMEM((nc, GLA_DK, GLA_DV), F32), pltpu.VMEM((GLA_DK, GLA_DV), F32)],
        compiler_params=_params("arbitrary"))(z, z, z, glr, gu, gb, do)


HN_HEADS = RET_HEADS + GLA_HEADS
HN_W = RET_DV


def _gate_col(h):
    return jnp.where(h < RET_HEADS, OFF_RG // HN_W + h, OFF_GG // HN_W + h - RET_HEADS)


def headnorm_fwd(name, oraw, z, w, tr=256):
    T = oraw.shape[0]

    def body(o_ref, g_ref, w_ref, y_ref):
        y_ref[...] = (_rms(o_ref[...], w_ref[...]) * _silu_and_grad(g_ref[...])[0]).astype(y_ref.dtype)

    return pl.pallas_call(
        body, name=name, grid=(HN_HEADS, T // tr),
        in_specs=[pl.BlockSpec((tr, HN_W), lambda h, i: (i, h)),
                  pl.BlockSpec((tr, HN_W), lambda h, i: (i, _gate_col(h))),
                  pl.BlockSpec((1, HN_W), lambda h, i: (0, h))],
        out_specs=pl.BlockSpec((tr, HN_W), lambda h, i: (i, h)),
        out_shape=jax.ShapeDtypeStruct((T, HN_HEADS * HN_W), BF16),
        compiler_params=_params("arbitrary", "arbitrary"))(oraw, z, w)


def headnorm_bwd(name, oraw, z, w, dy, tr=256):
    T = oraw.shape[0]

    def body(o_ref, g_ref, w_ref, dy_ref, do_ref, dg_ref, dw_ref):
        o, wv, dyv = o_ref[...], w_ref[...], dy_ref[...].astype(F32)
        silu, dsilu = _silu_and_grad(g_ref[...])
        n = _rms(o, wv)
        dg_ref[...] = (dyv * n * dsilu).astype(dg_ref.dtype)
        dx, dw = _rms_bwd(o, wv, dyv * silu)
        do_ref[...] = dx
        _accumulate(dw_ref, dw, pl.program_id(1) == 0)

    blk = pl.BlockSpec((tr, HN_W), lambda h, i: (i, h))
    return pl.pallas_call(
        body, name=name, grid=(HN_HEADS, T // tr),
        in_specs=[blk, pl.BlockSpec((tr, HN_W), lambda h, i: (i, _gate_col(h))),
                  pl.BlockSpec((1, HN_W), lambda h, i: (0, h)), blk],
        out_specs=[blk, blk, pl.BlockSpec((1, HN_W), lambda h, i: (0, h))],
        out_shape=[jax.ShapeDtypeStruct((T, HN_HEADS * HN_W), F32),
                   jax.ShapeDtypeStruct((T, HN_HEADS * HN_W), BF16),
                   jax.ShapeDtypeStruct((1, HN_HEADS * HN_W), F32)],
        compiler_params=_params("arbitrary", "arbitrary"))(oraw, z, w, dy)


def _multiplicity(qi, kb):
    ri = lax.broadcasted_iota(jnp.int32, (BLK, BLK), 0)
    ci = lax.broadcasted_iota(jnp.int32, (BLK, BLK), 1)
    dt = (qi - kb) * BLK + ri - ci
    mult = jnp.zeros((BLK, BLK), F32)
    for window, dilation in DILATED_BRANCHES:
        hit = (dt >= 0) & (dt <= window) & ((dt & (dilation - 1)) == 0)
        mult = mult + hit.astype(F32)
    return mult


def attn_fwd(name, qkv):
    T = qkv.shape[0]
    D = qkv.shape[1] // 3
    dh = D // ATT_HEADS
    nq = T // BLK
    scale = dh ** -0.5

    def body(q_ref, k_ref, v_ref, o_ref, lse_ref):
        qi = pl.program_id(1)
        q = q_ref[...]

        def step(kb, carry):
            m, l, acc = carry
            krows = pl.ds(pl.multiple_of(kb * BLK, BLK), BLK)
            mult = _multiplicity(qi, kb)
            s = jnp.where(mult > 0, _dot_nt(q, k_ref[krows, :]) * scale, -1e30)
            m_new = jnp.maximum(m, jnp.max(s, axis=-1, keepdims=True))
            alpha = jnp.exp(m - m_new)
            p = mult * jnp.exp(s - m_new)
            l = l * alpha + jnp.sum(p, axis=-1, keepdims=True)
            acc = acc * alpha + _dot(p, v_ref[krows, :])
            return m_new, l, acc

        m, l, acc = lax.fori_loop(
            0, qi + 1, step,
            (jnp.full((BLK, 1), -1e30, F32), jnp.zeros((BLK, 1), F32), jnp.zeros((BLK, dh), F32)))
        o_ref[...] = (acc / l).astype(o_ref.dtype)
        lse_ref[...] = jnp.broadcast_to(m + jnp.log(l), (BLK, LANE))

    return pl.pallas_call(
        body, name=name, grid=(ATT_HEADS, nq),
        in_specs=[pl.BlockSpec((BLK, dh), lambda h, i: (i, h)),
                  pl.BlockSpec((T, dh), lambda h, i: (0, ATT_HEADS + h)),
                  pl.BlockSpec((T, dh), lambda h, i: (0, 2 * ATT_HEADS + h))],
        out_specs=[pl.BlockSpec((BLK, dh), lambda h, i: (i, h)),
                   pl.BlockSpec((None, BLK, LANE), lambda h, i: (h, i, 0))],
        out_shape=[jax.ShapeDtypeStruct((T, D), BF16), jax.ShapeDtypeStruct((ATT_HEADS, T, LANE), F32)],
        compiler_params=_params("arbitrary", "arbitrary"))(qkv, qkv, qkv)


def attn_bwd(name, qkv, o, lse, do):
    T = qkv.shape[0]
    D = qkv.shape[1] // 3
    dh = D // ATT_HEADS
    nq = T // BLK
    scale = dh ** -0.5

    def body(q_ref, k_ref, v_ref, o_ref, lse_ref, do_ref, dq_ref, dk_ref, dv_ref, dk_acc, dv_acc):
        qi = pl.program_id(1)

        @pl.when(qi == 0)
        def _():
            dk_acc[...] = jnp.zeros_like(dk_acc)
            dv_acc[...] = jnp.zeros_like(dv_acc)

        q, dout = q_ref[...], do_ref[...]
        delta = jnp.sum(dout.astype(F32) * o_ref[...].astype(F32), axis=-1, keepdims=True)
        lse = jnp.concatenate([lse_ref[...]] * (BLK // LANE), axis=1)

        def step(kb, dq):
            krows = pl.ds(pl.multiple_of(kb * BLK, BLK), BLK)
            kk, vv = k_ref[krows, :], v_ref[krows, :]
            mult = _multiplicity(qi, kb)
            s = jnp.where(mult > 0, _dot_nt(q, kk) * scale, -1e30)
            p = mult * jnp.exp(s - lse)
            ds = (p * (_dot_nt(dout, vv) - delta) * scale).astype(BF16)
            dk_acc[krows, :] += _dot_tn(ds, q)
            dv_acc[krows, :] += _dot_tn(p, dout)
            return dq + _dot(ds, kk)

        dq_ref[...] = lax.fori_loop(0, qi + 1, step, jnp.zeros((BLK, dh), F32)).astype(dq_ref.dtype)

        @pl.when(qi == nq - 1)
        def _():
            dk_ref[...] = dk_acc[...].astype(dk_ref.dtype)
            dv_ref[...] = dv_acc[...].astype(dv_ref.dtype)

    blk = pl.BlockSpec((BLK, dh), lambda h, i: (i, h))
    full = pl.BlockSpec((T, dh), lambda h, i: (0, h))
    return pl.pallas_call(
        body, name=name, grid=(ATT_HEADS, nq),
        in_specs=[blk, pl.BlockSpec((T, dh), lambda h, i: (0, ATT_HEADS + h)),
                  pl.BlockSpec((T, dh), lambda h, i: (0, 2 * ATT_HEADS + h)),
                  blk, pl.BlockSpec((None, BLK, LANE), lambda h, i: (h, i, 0)), blk],
        out_specs=[blk, full, full],
        out_shape=[jax.ShapeDtypeStruct((T, D), BF16)] * 3,
        scratch_shapes=[pltpu.VMEM((T, dh), F32), pltpu.VMEM((T, dh), F32)],
        compiler_params=_params("arbitrary", "arbitrary"))(qkv, qkv, qkv, o, lse, do)


def _mesh_pos():
    mx, my, mc = lax.axis_index("x"), lax.axis_index("y"), lax.axis_index("c")
    return mx, my, mc, 4 * mx + 2 * my + mc


def _peer(k, mx, my, mc):
    px, py, pc = mx ^ (k >> 2), my ^ ((k >> 1) & 1), mc ^ (k & 1)
    return (px, py, pc), 4 * px + 2 * py + pc


def _direct_copies(kind, src_refs, land_refs, send_sems, recv_sems):
    mx, my, mc, me = _mesh_pos()
    out = []
    for i, (src, land) in enumerate(zip(src_refs, land_refs)):
        for k in range(1, N_DEV):
            peer, to = _peer(k, mx, my, mc)
            n = i * (N_DEV - 1) + k - 1
            out.append(pltpu.make_async_remote_copy(
                src_ref=src if kind == "gather" else src.at[to], dst_ref=land.at[me],
                send_sem=send_sems.at[n], recv_sem=recv_sems.at[n],
                device_id=peer, device_id_type=pl.DeviceIdType.MESH))
    return out


_HBM = pl.BlockSpec(memory_space=pltpu.HBM)
_SEM = pl.BlockSpec(memory_space=pltpu.SEMAPHORE)
_DATAFLOW = pltpu.SideEffectType.DATAFLOW_SIDE_EFFECTING


def exchange_start(name, kind, groups, deps=()):
    flat = [pair for g in groups for pair in g]
    n_buf, n_grp = len(flat), len(groups)

    def body(*refs):
        srcs, lands = refs[:n_buf], refs[n_buf:2 * n_buf]
        outs = refs[2 * n_buf + len(deps):]
        off = 0
        for gi, g in enumerate(groups):
            for cp in _direct_copies(kind, srcs[off:off + len(g)], lands[off:off + len(g)], outs[2 * gi], outs[2 * gi + 1]):
                cp.start()
            off += len(g)
        outs[-1][...] = jnp.zeros_like(outs[-1])

    hbm = lambda a: pltpu.with_memory_space_constraint(a, pltpu.HBM)
    bufs = [hbm(s) for s, _ in flat] + [hbm(l) for _, l in flat]
    sem_shapes = []
    for g in groups:
        sem_shapes += [pltpu.SemaphoreType.DMA((len(g) * (N_DEV - 1),))] * 2
    outs = pl.pallas_call(
        body, name=name,
        out_shape=sem_shapes + [pltpu.HBM(b.shape, b.dtype) for b in bufs] + [jax.ShapeDtypeStruct((8, LANE), F32)],
        in_specs=[_HBM] * (2 * n_buf) + [_ANY] * len(deps),
        out_specs=[_SEM] * (2 * n_grp) + [_HBM] * (2 * n_buf) + [pl.BlockSpec(memory_space=pltpu.VMEM)],
        input_output_aliases={i: 2 * n_grp + i for i in range(2 * n_buf)},
        compiler_params=pltpu.CompilerParams(has_side_effects=_DATAFLOW))(*bufs, *deps)
    sems, thru, token = outs[:2 * n_grp], outs[2 * n_grp:-1], outs[-1]
    handles, off = [], 0
    for gi, g in enumerate(groups):
        handles.append((thru[off:off + len(g)], thru[n_buf + off:n_buf + off + len(g)], sems[2 * gi], sems[2 * gi + 1]))
        off += len(g)
    return handles, token


def exchange_wait(name, kind, handle, deps=()):
    srcs, lands, send_sems, recv_sems = handle
    n = len(srcs)

    def body(*refs):
        copies = _direct_copies(kind, refs[:n], refs[n:2 * n], refs[2 * n], refs[2 * n + 1])
        for cp in copies:
            cp.wait_send()
        for cp in copies:
            cp.wait_recv()

    outs = pl.pallas_call(
        body, name=name, out_shape=[pltpu.HBM(b.shape, b.dtype) for b in list(srcs) + list(lands)],
        in_specs=[_HBM] * (2 * n) + [_SEM, _SEM] + [_ANY] * len(deps), out_specs=[_HBM] * (2 * n),
        input_output_aliases={i: i for i in range(2 * n)},
        compiler_params=pltpu.CompilerParams(has_side_effects=_DATAFLOW))(*srcs, *lands, send_sems, recv_sems, *deps)
    return outs[n:]


def gather_small(name, a):
    def body(a_ref, o_ref, send_sems, recv_sems, local_sem):
        me = _mesh_pos()[3]
        own = pltpu.make_async_copy(a_ref, o_ref.at[me], local_sem)
        own.start()
        copies = _direct_copies("gather", [a_ref], [o_ref], send_sems, recv_sems)
        for cp in copies:
            cp.start()
        for cp in copies:
            cp.wait_recv()
        for cp in copies:
            cp.wait_send()
        own.wait()

    return pl.pallas_call(
        body, name=name, in_specs=[_ANY], out_specs=_ANY,
        out_shape=jax.ShapeDtypeStruct((N_DEV,) + a.shape, a.dtype),
        scratch_shapes=[pltpu.SemaphoreType.DMA((N_DEV - 1,)), pltpu.SemaphoreType.DMA((N_DEV - 1,)),
                        pltpu.SemaphoreType.DMA],
        compiler_params=pltpu.CompilerParams(has_side_effects=True))(a)


def _adamw_math(w, g, m, v):
    m2 = ADAM_B1 * m + (1.0 - ADAM_B1) * g
    v2 = ADAM_B2 * v + (1.0 - ADAM_B2) * (g * g)
    m_hat = m2 / (1.0 - ADAM_B1 ** ADAM_STEP)
    v_hat = v2 / (1.0 - ADAM_B2 ** ADAM_STEP)
    delta = -ADAM_LR * (m_hat / (jnp.sqrt(v_hat) + ADAM_EPS) + ADAM_WD * w)
    return delta, m2, v2


def adamw(name, w, m, v, l, land, own, prev=None):
    L, r, c = w.shape
    cp = land.shape[2]
    tr = _pick(r, (256, 176, 128, 64, 32, 16, 8))

    def body(w_ref, m_ref, v_ref, land_ref, own_ref, *rest):
        g_ref, d_ref, m2_ref, v2_ref = rest[-4:]
        me = _mesh_pos()[3]
        mine = own_ref[:, pl.ds(0, c)].astype(F32)
        g = None
        for s in range(N_DEV):
            part = jnp.where(me == s, mine, land_ref[s, :, pl.ds(0, c)].astype(F32))
            g = part if g is None else g + part
        delta, m2, v2 = _adamw_math(w_ref[...], g, m_ref[...], v_ref[...])
        g_ref[...] = g
        d_ref[...] = delta
        m2_ref[...] = m2
        v2_ref[...] = v2

    blk = pl.BlockSpec((None, tr, c), lambda i: (l, i, 0))
    shape = jax.ShapeDtypeStruct((L, r, c), F32)
    extra = [] if prev is None else list(prev)
    return pl.pallas_call(
        body, name=name, grid=(r // tr,),
        in_specs=[blk, blk, blk, pl.BlockSpec((N_DEV, tr, cp), lambda i: (0, i, 0)),
                  pl.BlockSpec((None, tr, cp), lambda i: (_mesh_pos()[3], i, 0))] + [_ANY] * len(extra),
        out_specs=[blk] * 4, out_shape=[shape] * 4,
        input_output_aliases={5 + k: k for k in range(len(extra))},
        compiler_params=_params("parallel"))(w, m, v, land, own, *extra)


def adamw_small(name, w, m, v, parts):
    n = w.shape[1]

    def body(w_ref, m_ref, v_ref, p_ref, g_ref, d_ref, m2_ref, v2_ref):
        g = p_ref[0:1, :]
        for s in range(1, N_DEV):
            g = g + p_ref[s:s + 1, :]
        delta, m2, v2 = _adamw_math(w_ref[...], g, m_ref[...], v_ref[...])
        g_ref[...] = g
        d_ref[...] = delta
        m2_ref[...] = m2
        v2_ref[...] = v2

    shape = jax.ShapeDtypeStruct((1, n), F32)
    return pl.pallas_call(body, name=name, out_shape=[shape] * 4,
                          compiler_params=pltpu.CompilerParams(vmem_limit_bytes=VMEM_LIMIT_BYTES))(w, m, v, parts)


def _rope_tables(positions):
    half = RET_DK // 2
    inv_freq = 1.0 / jnp.power(RET_THETA_BASE, jnp.linspace(0.0, 1.0, half, dtype=F32))
    ang = positions.astype(F32)[:, None] * inv_freq
    cos, sin = jnp.cos(ang), jnp.sin(ang)
    cosf = jnp.repeat(cos, 2, axis=-1)
    sins = jnp.stack([-sin, sin], axis=-1).reshape(cosf.shape)
    return cosf, sins


def _pad_to(a, axis, size):
    pad = [(0, 0)] * a.ndim
    pad[axis] = (0, size - a.shape[axis])
    return jnp.pad(a, pad)


def _round_up(n, m):
    return -(-n // m) * m


def kernel(x, p, positions, attn_norm_w, ffn_norm_w, ple_norm_w, final_norm_w, ab_w_in, ab_gla_gate_up, ab_gla_gate_b, ab_ret_norm_w, ab_gla_norm_w, ab_w_out, c_w_qkv, c_w_out, ffn_w_gate, ffn_w_up, ffn_w_down, ple_w_proj, ple_w_gate, loss_target, m_attn_norm_w, m_ffn_norm_w, m_ple_norm_w, m_final_norm_w, m_ab_w_in, m_ab_gla_gate_up, m_ab_gla_gate_b, m_ab_ret_norm_w, m_ab_gla_norm_w, m_ab_w_out, m_c_w_qkv, m_c_w_out, m_ffn_w_gate, m_ffn_w_up, m_ffn_w_down, m_ple_w_proj, m_ple_w_gate, v_attn_norm_w, v_ffn_norm_w, v_ple_norm_w, v_final_norm_w, v_ab_w_in, v_ab_gla_gate_up, v_ab_gla_gate_b, v_ab_ret_norm_w, v_ab_gla_norm_w, v_ab_w_out, v_c_w_qkv, v_c_w_out, v_ffn_w_gate, v_ffn_w_up, v_ffn_w_down, v_ple_w_proj, v_ple_w_gate):
    T, D = x.shape[1], x.shape[2]
    depth = attn_norm_w.shape[0]
    assert ab_w_in.shape[0] == 1 and c_w_qkv.shape[0] == 1 and depth == 2, "one even and one odd layer"
    me = 4 * lax.axis_index("x") + 2 * lax.axis_index("y") + lax.axis_index("c")
    in_shard = ab_w_in.shape[2]
    in_width = in_shard * N_DEV
    assert in_width == OFF_LR + GLA_GATE_RANK
    fs = ffn_w_gate.shape[2]
    fp = _round_up(fs, LANE)
    gu_cols = ab_gla_gate_up.shape[2]

    bf = lambda a: a.astype(BF16)
    tr_ = lambda a: jnp.swapaxes(a, -1, -2)
    wg_t, wu_t = tr_(ffn_w_gate), tr_(ffn_w_up)
    srcs = {"w_in": bf(tr_(ab_w_in[0])), "w_oab": bf(ab_w_out[0]), "gu": ab_gla_gate_up[0],
            "w_qkv": bf(c_w_qkv[0]), "w_oc": bf(c_w_out[0])}
    for l in range(depth):
        srcs[f"wg{l}"] = _pad_to(bf(wg_t[l]), 0, fp)
        srcs[f"wu{l}"] = _pad_to(bf(wu_t[l]), 0, fp)
        srcs[f"wd{l}"] = _pad_to(bf(ffn_w_down[l]), 0, fp)
        srcs[f"wpg{l}"] = bf(ple_w_gate[l])
        srcs[f"wpp{l}"] = bf(ple_w_proj[l])
    group_keys = [["w_in", "w_oab", "gu"], ["wg0", "wu0", "wd0", "wpg0", "wpp0"], ["w_qkv", "w_oc"],
                  ["wg1", "wu1", "wd1", "wpg1", "wpp1"]]

    def landing(a):
        return lax.dynamic_update_slice(lax.empty((N_DEV,) + a.shape, a.dtype), a[None], (me,) + (0,) * a.ndim)

    gather_handles, gather_token = exchange_start(
        "gather_start", "gather", [[(srcs[k], landing(srcs[k])) for k in keys] for keys in group_keys])
    weights = {}

    def gather_wait(gi, dep):
        lands = exchange_wait(f"gather_wait{gi}", "gather", gather_handles[gi], deps=(dep,))
        weights.update(zip(group_keys[gi], lands))

    gb = ab_gla_gate_b
    hn_w = jnp.concatenate([ab_ret_norm_w, ab_gla_norm_w], axis=1)
    cosf, sins = _rope_tables(positions[0])
    p_bf = bf(p[:, 0])

    xs = x[0]
    saved = []
    for i in range(depth):
        nm = f"l{i}_"
        w_attn, w_ffn, w_ple = attn_norm_w[i:i + 1], ffn_norm_w[i:i + 1], ple_norm_w[i:i + 1]
        (xn,) = rowwise(nm + "norm_attn", lambda a, w: (_rms(a, w),), T, [("row", xs), ("full", w_attn)],
                        [("row", D, BF16)], deps=(gather_token,) if i == 0 else ())
        gather_wait(2 * i, xn)
        if i % 2 == 0:
            w_in_t = weights["w_in"].reshape(1, 1, in_width, D)
            w_lr_t = _pad_to(weights["w_in"].reshape(in_width, D)[OFF_LR:], 0, LANE).reshape(1, 1, LANE, D)
            w_oab = weights["w_oab"].reshape(1, 1, D, D)
            gu_full = _pad_to(weights["gu"].transpose(1, 0, 2).reshape(GLA_GATE_RANK, GLA_QK), 0, LANE)
            z = mmt_fwd(nm + "mm_in", xn, w_in_t, 0, F32, n=OFF_LR)
            glr = mmt_fwd(nm + "mm_lr", xn, w_lr_t, 0, F32)
            oraw = retention_fwd(nm + "ret_fwd", z, cosf, sins, RET_V + GLA_V)
            oraw = gla_fwd(nm + "gla_fwd", z, glr, gu_full, gb, oraw)
            o = headnorm_fwd(nm + "headnorm_fwd", oraw, z, hn_w)
            mix = mm_nn(nm + "mm_out", o, w_oab, 0, F32)
            mixer_saved = (z, glr, oraw, o)
        else:
            w_qkv = weights["w_qkv"].reshape((1,) + weights["w_qkv"].shape)
            w_oc = weights["w_oc"].reshape(1, 1, D, D)
            qkv = mm_nn(nm + "mm_qkv", xn, w_qkv, 0, BF16)
            o, lse = attn_fwd(nm + "attn_fwd", qkv)
            mix = mm_nn(nm + "mm_out", o, w_oc, 0, F32)
            mixer_saved = (qkv, o, lse)
        h1, hn = rowwise(nm + "add_norm_ffn", lambda a, b, w: (a + b, _rms(a + b, w)), T,
                         [("row", xs), ("row", mix), ("full", w_ffn)], [("row", D, F32), ("row", D, BF16)])
        gather_wait(2 * i + 1, hn)
        wg = weights[f"wg{i}"].reshape(1, N_DEV, fp, D)
        wu = weights[f"wu{i}"].reshape(1, N_DEV, fp, D)
        wd = weights[f"wd{i}"].reshape(1, 1, N_DEV * fp, D)
        wpg = weights[f"wpg{i}"].reshape(1, 1, D, D)
        wpp = weights[f"wpp{i}"].reshape((1,) + weights[f"wpp{i}"].shape)
        g = mmt_fwd(nm + "mm_gate", hn, wg, 0, F32)
        u = mmt_fwd(nm + "mm_up", hn, wu, 0, F32)
        (act,) = rowwise(nm + "swiglu", lambda a, b: (_silu_and_grad(a)[0] * b,), T, [("row", g), ("row", u)],
                         [("row", g.shape[1], BF16)])
        f = mm_nn(nm + "mm_down", act, wd, 0, F32)
        h2, pn = rowwise(nm + "add_norm_ple", lambda a, b, w: (a + b, _rms(a + b, w)), T,
                         [("row", h1), ("row", f), ("full", w_ple)], [("row", D, F32), ("row", D, BF16)])
        s = mm_nn(nm + "mm_ple_gate", pn, wpg, 0, F32)
        e = mm_nn(nm + "mm_ple_proj", p_bf[i], wpp, 0, F32)
        (x_next,) = rowwise(nm + "ple_out", lambda a, b, c: (a + _sigmoid(b) * c,), T,
                            [("row", h2), ("row", s), ("row", e)], [("row", D, F32)])
        mixer_w = (w_in_t, w_lr_t, w_oab, gu_full) if i % 2 == 0 else (w_qkv, w_oc)
        saved.append((xs, xn, mixer_saved, mixer_w, (wg, wu, wd, wpg), h1, hn, g, u, act, h2, pn, s, e))
        xs = x_next

    def loss_fn(a, w, t):
        diff = _rms(a, w) - t
        dx, dw = _rms_bwd(a, w, diff * (1.0 / D))
        part = 0.5 * jnp.sum(jnp.mean(diff * diff, axis=-1, keepdims=True), axis=0, keepdims=True)
        return dx, dw, jnp.broadcast_to(part, (1, LANE))

    dx, d_final_w, loss_part = rowwise("loss_head", loss_fn, T,
                                       [("row", xs), ("full", final_norm_w[None, :]), ("row", loss_target[0])],
                                       [("row", D, F32), ("acc", D), ("acc", LANE)])
    loss = lax.psum(loss_part[0, 0], ("x", "y", "c"))

    grads = {}
    scatters = []

    def scatter_start(name, keys):
        handles, token = exchange_start(name, "scatter", [[(grads[k], lax.empty(grads[k].shape, BF16)) for k in keys]])
        scatters.append((keys, handles[0]))
        return token

    d_attn_w, d_ffn_w, d_ple_w = [None] * depth, [None] * depth, [None] * depth
    for i in reversed(range(depth)):
        nm = f"l{i}_b_"
        xs_i, xn, mixer_saved, mixer_w, (wg, wu, wd, wpg), h1, hn, g, u, act, h2, pn, s, e = saved[i]
        w_attn, w_ffn, w_ple = attn_norm_w[i:i + 1], ffn_norm_w[i:i + 1], ple_norm_w[i:i + 1]

        def ple_bwd(d, sv, ev):
            gate = _sigmoid(sv)
            return d * gate, d * ev * gate * (1.0 - gate)

        de, ds = rowwise(nm + "ple_out", ple_bwd, T, [("row", dx), ("row", s), ("row", e)],
                         [("row", D, BF16), ("row", D, BF16)])
        grads[("ple_w_proj", i)] = mm_tn(nm + "mm_ple_proj_w", p_bf[i], de, N_DEV, BF16)
        grads[("ple_w_gate", i)] = mm_tn(nm + "mm_ple_gate_w", pn, ds, 1, BF16).reshape(N_DEV, D // N_DEV, D)
        dpn = mm_nt(nm + "mm_ple_gate_x", ds, wpg, 0, F32)

        def norm_bwd_add(a, w, dn, dres):
            dxx, dw = _rms_bwd(a, w, dn)
            tot = dres + dxx
            return tot, tot, dw

        dh2, dh2_bf, d_ple_w[i] = rowwise(nm + "norm_ple", norm_bwd_add, T,
                                          [("row", h2), ("full", w_ple), ("row", dpn), ("row", dx)],
                                          [("row", D, F32), ("row", D, BF16), ("acc", D)])
        grads[("ffn_w_down", i)] = mm_tn(nm + "mm_down_w", act, dh2_bf, 1, BF16).reshape(N_DEV, fp, D)
        token = scatter_start(nm + "scatter_ple_down", [("ple_w_proj", i), ("ple_w_gate", i), ("ffn_w_down", i)])
        dact = mm_nt(nm + "mm_down_x", dh2_bf, wd, 0, F32, deps=(token,))

        def swiglu_bwd(da, gv, uv):
            silu, dsilu = _silu_and_grad(gv)
            return da * uv * dsilu, da * silu

        dg, du = rowwise(nm + "swiglu", swiglu_bwd, T, [("row", dact), ("row", g), ("row", u)],
                         [("row", g.shape[1], BF16), ("row", g.shape[1], BF16)])
        grads[("ffn_w_gate", i)] = mmt_dw(nm + "mm_gate_w", dg, hn, N_DEV, BF16)
        grads[("ffn_w_up", i)] = mmt_dw(nm + "mm_up_w", du, hn, N_DEV, BF16)
        token = scatter_start(nm + "scatter_gate_up", [("ffn_w_gate", i), ("ffn_w_up", i)])
        dhn_g = mmt_dx(nm + "mm_gate_x", dg, wg, 0, F32, deps=(token,))
        dhn_u = mmt_dx(nm + "mm_up_x", du, wu, 0, F32)

        def norm_bwd_add2(a, w, dn1, dn2, dres):
            dxx, dw = _rms_bwd(a, w, dn1 + dn2)
            tot = dres + dxx
            return tot, tot, dw

        dh1, dh1_bf, d_ffn_w[i] = rowwise(nm + "norm_ffn", norm_bwd_add2, T,
                                          [("row", h1), ("full", w_ffn), ("row", dhn_g), ("row", dhn_u), ("row", dh2)],
                                          [("row", D, F32), ("row", D, BF16), ("acc", D)])
        if i % 2 == 0:
            z, glr, oraw, o = mixer_saved
            w_in_t, w_lr_t, w_oab, gu_full = mixer_w
            grads[("ab_w_out", 0)] = mm_tn(nm + "mm_out_w", o, dh1_bf, 1, BF16).reshape(N_DEV, D // N_DEV, D)
            token = scatter_start(nm + "scatter_out", [("ab_w_out", 0)])
            do = mm_nt(nm + "mm_out_x", dh1_bf, w_oab, 0, F32, deps=(token,))
            d_oraw, d_gates, d_hn_w = headnorm_bwd(nm + "headnorm", oraw, z, hn_w, do)
            d_rq, d_rk, d_rv = retention_bwd(nm + "ret", z, cosf, sins, d_oraw)
            d_gq, d_gk, d_gv, d_glr4, d_gu, d_gb = gla_bwd(nm + "gla", z, glr, gu_full, gb, d_oraw)
            dz = jnp.concatenate([d_rq, d_rk, d_rv, d_gates[:, :RET_V], d_gq, d_gk, d_gv, d_gates[:, RET_V:]], axis=1)
            (d_glr,) = rowwise(nm + "sum_lr", lambda *a: (a[0] + a[1] + a[2] + a[3],), T,
                               [("row", d_glr4[hh]) for hh in range(GLA_HEADS)], [("row", LANE, BF16)])
            dw_main = mm_tn(nm + "mm_in_w", xn, dz, 1, BF16)[0]
            dw_lr = mm_tn(nm + "mm_lr_w", xn, d_glr, 1, BF16)[0]
            dw_in = jnp.concatenate([dw_main, dw_lr[:, :GLA_GATE_RANK]], axis=1)
            grads[("ab_w_in", 0)] = dw_in.reshape(D, N_DEV, in_shard).transpose(1, 0, 2)
            token = scatter_start(nm + "scatter_in", [("ab_w_in", 0)])
            dxn_a = mmt_dx(nm + "mm_in_x", dz, w_in_t, 0, F32, n=OFF_LR, deps=(token,))
            dxn_b = mmt_dx(nm + "mm_lr_x", d_glr, w_lr_t, 0, F32)
        else:
            qkv, o, lse = mixer_saved
            w_qkv, w_oc = mixer_w
            grads[("c_w_out", 0)] = mm_tn(nm + "mm_out_w", o, dh1_bf, 1, BF16).reshape(N_DEV, D // N_DEV, D)
            do = mm_nt(nm + "mm_out_x", dh1_bf, w_oc, 0, BF16)
            dq, dk, dv = attn_bwd(nm + "attn", qkv, o, lse, do)
            dqkv = jnp.concatenate([dq, dk, dv], axis=1)
            grads[("c_w_qkv", 0)] = mm_tn(nm + "mm_qkv_w", xn, dqkv, N_DEV, BF16)
            token = scatter_start(nm + "scatter_attn", [("c_w_out", 0), ("c_w_qkv", 0)])
            dxn_a = mm_nt(nm + "mm_qkv_x", dqkv, w_qkv, 0, F32, deps=(token,))
            dxn_b = None
        if dxn_b is None:
            dx, _, d_attn_w[i] = rowwise(nm + "norm_attn", norm_bwd_add, T,
                                         [("row", xs_i), ("full", w_attn), ("row", dxn_a), ("row", dh1)],
                                         [("row", D, F32), ("row", D, BF16), ("acc", D)])
        else:
            dx, _, d_attn_w[i] = rowwise(nm + "norm_attn", norm_bwd_add2, T,
                                         [("row", xs_i), ("full", w_attn), ("row", dxn_a), ("row", dxn_b), ("row", dh1)],
                                         [("row", D, F32), ("row", D, BF16), ("acc", D)])

    small_names = ["attn_norm_w", "ffn_norm_w", "ple_norm_w", "final_norm_w", "ab_gla_gate_b", "ab_ret_norm_w",
                   "ab_gla_norm_w"]
    small_grads = [jnp.concatenate(d_attn_w, 0), jnp.concatenate(d_ffn_w, 0), jnp.concatenate(d_ple_w, 0), d_final_w[0],
                   d_gb, d_hn_w[:, :RET_V], d_hn_w[:, RET_V:]]
    small_w = [attn_norm_w, ffn_norm_w, ple_norm_w, final_norm_w, ab_gla_gate_b, ab_ret_norm_w, ab_gla_norm_w]
    small_m = [m_attn_norm_w, m_ffn_norm_w, m_ple_norm_w, m_final_norm_w, m_ab_gla_gate_b, m_ab_ret_norm_w, m_ab_gla_norm_w]
    small_v = [v_attn_norm_w, v_ffn_norm_w, v_ple_norm_w, v_final_norm_w, v_ab_gla_gate_b, v_ab_ret_norm_w, v_ab_gla_norm_w]
    sizes = [int(np.prod(a.shape)) for a in small_w]
    n_gu = GLA_GATE_RANK * GLA_QK
    n_small = _round_up(sum(sizes) + n_gu, LANE)
    pack = lambda parts: _pad_to(jnp.concatenate([a.reshape(-1) for a in parts]), 0, n_small)[None, :]
    small_part = pack(small_grads + [d_gu[:GLA_GATE_RANK]])

    small_parts = gather_small("gather_small", small_part).reshape(N_DEV, n_small)
    landed = {}
    for gi, (keys, handle) in enumerate(scatters):
        landed.update(zip(keys, exchange_wait(f"scatter_wait{gi}", "scatter", handle, deps=(dx,))))

    big_w = dict(ab_w_in=(ab_w_in, m_ab_w_in, v_ab_w_in), ab_w_out=(ab_w_out, m_ab_w_out, v_ab_w_out),
                 c_w_qkv=(c_w_qkv, m_c_w_qkv, v_c_w_qkv), c_w_out=(c_w_out, m_c_w_out, v_c_w_out),
                 ffn_w_gate=(wg_t, tr_(m_ffn_w_gate), tr_(v_ffn_w_gate)),
                 ffn_w_up=(wu_t, tr_(m_ffn_w_up), tr_(v_ffn_w_up)),
                 ffn_w_down=(ffn_w_down, m_ffn_w_down, v_ffn_w_down), ple_w_proj=(ple_w_proj, m_ple_w_proj, v_ple_w_proj),
                 ple_w_gate=(ple_w_gate, m_ple_w_gate, v_ple_w_gate))
    results = {}
    for n, (w_, m_, v_) in big_w.items():
        res = None
        for l in range(w_.shape[0]):
            res = adamw(f"adamw_{n}{l}", w_, m_, v_, l, landed[(n, l)], grads[(n, l)], prev=res)
        results[n] = [tr_(a) for a in res] if n in ("ffn_w_gate", "ffn_w_up") else list(res)

    gu_off = sum(sizes)
    own_cols = lambda a: lax.dynamic_slice_in_dim(a.reshape(GLA_GATE_RANK, GLA_QK), me * gu_cols, gu_cols, axis=1)
    small_res = adamw_small("adamw_small", pack(small_w + [jnp.zeros((n_gu,), F32)]),
                            pack(small_m + [jnp.zeros((n_gu,), F32)]), pack(small_v + [jnp.ones((n_gu,), F32)]),
                            small_parts)
    g_gu_full = small_res[0][0, gu_off:gu_off + n_gu]
    g_gu = own_cols(g_gu_full)[None]
    gu_res = adamw_small("adamw_gate_up", *[_pad_to(a.reshape(1, -1), 1, _round_up(a.size, LANE)) for a in
                                            (ab_gla_gate_up, m_ab_gla_gate_up, v_ab_gla_gate_up)],
                         jnp.concatenate([_pad_to(g_gu.reshape(1, -1), 1, _round_up(g_gu.size, LANE)),
                                          jnp.zeros((N_DEV - 1, _round_up(g_gu.size, LANE)), F32)], axis=0))
    for k in range(4):
        off = 0
        for n, a, sz in zip(small_names, small_w, sizes):
            results.setdefault(n, [None] * 4)[k] = small_res[k][0, off:off + sz].reshape(a.shape)
            off += sz
        results.setdefault("ab_gla_gate_up", [None] * 4)[k] = gu_res[k][0, :g_gu.size].reshape(ab_gla_gate_up.shape)

    order = ["attn_norm_w", "ffn_norm_w", "ple_norm_w", "final_norm_w", "ab_w_in", "ab_gla_gate_up", "ab_gla_gate_b",
             "ab_ret_norm_w", "ab_gla_norm_w", "ab_w_out", "c_w_qkv", "c_w_out", "ffn_w_gate", "ffn_w_up", "ffn_w_down",
             "ple_w_proj", "ple_w_gate"]
    return (loss, dx[None], *[results[n][0] for n in order], *[results[n][1] for n in order],
            *[results[n][2] for n in order], *[results[n][3] for n in order])
```

```python
import math

import numpy as np
import jax
import jax.numpy as jnp
from jax import lax
from jax.experimental import pallas as pl
from jax.experimental.pallas import tpu as pltpu

F32 = jnp.float32
BF16 = jnp.bfloat16
HIGHEST = lax.Precision.HIGHEST

N_DEV = 8
VMEM_LIMIT_BYTES = 48 * 1024 * 1024
LANE = 128
NORM_EPS = 1e-6

RET_HEADS, RET_DK, RET_DV = 4, 256, 256
RET_THETA_BASE = 10000.0
GLA_HEADS, GLA_DK, GLA_DV = 4, 128, 256
GLA_GATE_RANK = 16
GLA_GATE_NORM = 16.0
CHUNK = 64
ATT_HEADS = 16
DILATED_BRANCHES = ((128, 1), (512, 4), (2048, 16))
BLK = 256

ADAM_LR, ADAM_B1, ADAM_B2, ADAM_EPS, ADAM_WD, ADAM_STEP = 0.001, 0.9, 0.999, 1e-08, 0.01, 10

RET_QK = RET_HEADS * RET_DK
RET_V = RET_HEADS * RET_DV
GLA_QK = GLA_HEADS * GLA_DK
GLA_V = GLA_HEADS * GLA_DV
OFF_RQ, OFF_RK, OFF_RV, OFF_RG = 0, RET_QK, 2 * RET_QK, 2 * RET_QK + RET_V
OFF_GQ = OFF_RG + RET_V
OFF_GK = OFF_GQ + GLA_QK
OFF_GV = OFF_GK + GLA_QK
OFF_GG = OFF_GV + GLA_V
OFF_LR = OFF_GG + GLA_V


def _params(*sem):
    return pltpu.CompilerParams(dimension_semantics=sem or None, vmem_limit_bytes=VMEM_LIMIT_BYTES)


def _pick(n, cands):
    for c in cands:
        if n % c == 0:
            return c
    raise ValueError(f"no tile for {n} in {cands}")


_NN = (((1,), (0,)), ((), ()))
_NT = (((1,), (1,)), ((), ()))
_TN = (((0,), (0,)), ((), ()))
_ANY = pl.BlockSpec(memory_space=pl.ANY)
MAX_CONTRACT = 2048
_TILES = (1024, 768, 512, 256, 128)


def _mm_call(name, dims, grid, in_specs, out_spec, out_shape, args, deps=()):
    steps = grid[2]
    assert steps == 1 or out_shape.dtype == F32

    def body(a_ref, b_ref, *rest):
        o_ref = rest[len(deps)]
        part = lax.dot_general(a_ref[...].astype(BF16), b_ref[...].astype(BF16), dims, preferred_element_type=F32)
        if steps == 1:
            o_ref[...] = part.astype(o_ref.dtype)
        else:
            _accumulate(o_ref, part, pl.program_id(2) == 0)

    return pl.pallas_call(
        body, name=name, grid=grid, in_specs=list(in_specs) + [_ANY] * len(deps), out_specs=out_spec,
        out_shape=out_shape, compiler_params=_params("parallel", "parallel", "arbitrary"))(*args, *deps)


def mm_nn(name, a, w, l, out_dtype, deps=()):
    _, J, K, n = w.shape
    M = a.shape[0]
    tm, tn, tk = _pick(M, _TILES), _pick(n, _TILES), _pick(K, (MAX_CONTRACT,) + _TILES)
    nt = n // tn
    return _mm_call(
        name, _NN, (M // tm, J * nt, K // tk),
        [pl.BlockSpec((tm, tk), lambda i, j, k: (i, k)),
         pl.BlockSpec((None, None, tk, tn), lambda i, j, k: (l, j // nt, k, j % nt))],
        pl.BlockSpec((tm, tn), lambda i, j, k: (i, j)),
        jax.ShapeDtypeStruct((M, J * n), out_dtype), (a, w), deps)


def mm_nt(name, a, w, l, out_dtype, deps=()):
    _, J, K, n = w.shape
    M = a.shape[0]
    tm, tq, tc = _pick(M, _TILES), _pick(K, _TILES), _pick(n, (MAX_CONTRACT,) + _TILES)
    nc = n // tc
    return _mm_call(
        name, _NT, (M // tm, K // tq, J * nc),
        [pl.BlockSpec((tm, tc), lambda i, q, c: (i, c)),
         pl.BlockSpec((None, None, tq, tc), lambda i, q, c: (l, c // nc, q, c % nc))],
        pl.BlockSpec((tm, tq), lambda i, q, c: (i, q)),
        jax.ShapeDtypeStruct((M, K), out_dtype), (a, w), deps)


def mm_tn(name, x, dy, J, out_dtype, deps=()):
    M, K = x.shape
    n = dy.shape[1] // J
    tp, tn = _pick(K, _TILES), _pick(n, _TILES)
    nt = n // tn
    assert M <= MAX_CONTRACT
    return _mm_call(
        name, _TN, (K // tp, J * nt, 1),
        [pl.BlockSpec((M, tp), lambda i, j, r: (0, i)),
         pl.BlockSpec((M, tn), lambda i, j, r: (0, j))],
        pl.BlockSpec((None, tp, tn), lambda i, j, r: (j // nt, i, j % nt)),
        jax.ShapeDtypeStruct((J, K, n), out_dtype), (x, dy), deps)


def mmt_fwd(name, a, wt, l, out_dtype, n=None, deps=()):
    _, J, rows, K = wt.shape
    n = rows if n is None else n
    M = a.shape[0]
    tm, tn = _pick(M, _TILES), _pick(n, _TILES)
    nt = n // tn
    assert K <= MAX_CONTRACT
    return _mm_call(
        name, _NT, (M // tm, J * nt, 1),
        [pl.BlockSpec((tm, K), lambda i, j, k: (i, 0)),
         pl.BlockSpec((None, None, tn, K), lambda i, j, k: (l, j // nt, j % nt, 0))],
        pl.BlockSpec((tm, tn), lambda i, j, k: (i, j)),
        jax.ShapeDtypeStruct((M, J * n), out_dtype), (a, wt), deps)


def mmt_dx(name, dy, wt, l, out_dtype, n=None, deps=()):
    _, J, rows, K = wt.shape
    n = rows if n is None else n
    M = dy.shape[0]
    tm, tq, tc = _pick(M, _TILES), _pick(K, _TILES), _pick(n, _TILES)
    nc = n // tc
    return _mm_call(
        name, _NN, (M // tm, K // tq, J * nc),
        [pl.BlockSpec((tm, tc), lambda i, q, c: (i, c)),
         pl.BlockSpec((None, None, tc, tq), lambda i, q, c: (l, c // nc, c % nc, q))],
        pl.BlockSpec((tm, tq), lambda i, q, c: (i, q)),
        jax.ShapeDtypeStruct((M, K), out_dtype), (dy, wt), deps)


def mmt_dw(name, dy, x, J, out_dtype, deps=()):
    M, K = x.shape
    n = dy.shape[1] // J
    tn, tp = _pick(n, _TILES), _pick(K, _TILES)
    nt = n // tn
    assert M <= MAX_CONTRACT
    return _mm_call(
        name, _TN, (J * nt, K // tp, 1),
        [pl.BlockSpec((M, tn), lambda j, i, r: (0, j)),
         pl.BlockSpec((M, tp), lambda j, i, r: (0, i))],
        pl.BlockSpec((None, tn, tp), lambda j, i, r: (j // nt, j % nt, i)),
        jax.ShapeDtypeStruct((J, n, K), out_dtype), (dy, x), deps)


def rowwise(name, fn, rows, ins, outs, tr=256, deps=()):
    widest = max([s[1].shape[1] if s[0] != "col" else s[3] for s in ins] + [s[1] for s in outs])
    tr = min(tr if widest <= 2048 else tr // 2, rows)
    in_specs, args = [], []
    for spec in ins:
        kind, a = spec[0], spec[1]
        if kind == "row":
            in_specs.append(pl.BlockSpec((tr, a.shape[1]), lambda i: (i, 0)))
        elif kind == "col":
            cb, width = spec[2], spec[3]
            in_specs.append(pl.BlockSpec((tr, width), lambda i, cb=cb: (i, cb)))
        else:
            in_specs.append(pl.BlockSpec(a.shape, lambda i: (0, 0)))
        args.append(a)
    out_specs, out_shapes = [], []
    for spec in outs:
        if spec[0] == "row":
            out_specs.append(pl.BlockSpec((tr, spec[1]), lambda i: (i, 0)))
            out_shapes.append(jax.ShapeDtypeStruct((rows, spec[1]), spec[2]))
        else:
            out_specs.append(pl.BlockSpec((1, spec[1]), lambda i: (0, 0)))
            out_shapes.append(jax.ShapeDtypeStruct((1, spec[1]), F32))
    n_in = len(ins)

    def body(*refs):
        vals = fn(*[r[...] for r in refs[:n_in]])
        first = pl.program_id(0) == 0
        for r, v, spec in zip(refs[n_in + len(deps):], vals, outs):
            if spec[0] == "row":
                r[...] = v.astype(r.dtype)
            else:
                _accumulate(r, v, first)

    return pl.pallas_call(body, name=name, grid=(rows // tr,), in_specs=in_specs + [_ANY] * len(deps),
                          out_specs=out_specs, out_shape=out_shapes,
                          compiler_params=_params("arbitrary"))(*args, *deps)


def _accumulate(ref, v, first):
    @pl.when(first)
    def _():
        ref[...] = v

    @pl.when(jnp.logical_not(first))
    def _():
        ref[...] += v


def _rms(x, w):
    r = lax.rsqrt(jnp.mean(x * x, axis=-1, keepdims=True) + NORM_EPS)
    return x * r * w


def _rms_bwd(x, w, dy):
    r = lax.rsqrt(jnp.mean(x * x, axis=-1, keepdims=True) + NORM_EPS)
    g = dy * w
    dx = r * (g - x * (r * r) * jnp.mean(g * x, axis=-1, keepdims=True))
    dw = jnp.sum(dy * x * r, axis=0, keepdims=True)
    return dx, dw


def _sigmoid(x):
    return 1.0 / (1.0 + jnp.exp(-x))


def _silu_and_grad(g):
    s = _sigmoid(g)
    return g * s, s * (1.0 + g * (1.0 - s))


def _swap_pairs(x):
    n = x.shape[-1]
    lane = lax.broadcasted_iota(jnp.int32, x.shape, x.ndim - 1)
    return jnp.where((lane & 1) == 0, pltpu.roll(x, n - 1, x.ndim - 1), pltpu.roll(x, 1, x.ndim - 1))


def _rot(x, cosf, sins):
    return x * cosf + _swap_pairs(x) * sins


def _unrot(d, cosf, sins):
    return d * cosf + _swap_pairs(d * sins)


def _ret_log_gamma(h):
    vals = [math.log1p(-2.0 ** (-5.0 - i)) for i in range(RET_HEADS)]
    out = jnp.float32(vals[RET_HEADS - 1])
    for i in range(RET_HEADS - 2, -1, -1):
        out = jnp.where(h == i, jnp.float32(vals[i]), out)
    return out


def _decay_block(qi, kb, lg):
    ri = lax.broadcasted_iota(jnp.int32, (BLK, BLK), 0)
    ci = lax.broadcasted_iota(jnp.int32, (BLK, BLK), 1)
    dt = (qi - kb) * BLK + ri - ci
    return jnp.where(dt >= 0, jnp.exp(jnp.maximum(dt, 0).astype(F32) * lg), 0.0)


_NT = (((1,), (1,)), ((), ()))
_TN = (((0,), (0,)), ((), ()))


def _dot(a, b):
    return jnp.dot(a.astype(BF16), b.astype(BF16), preferred_element_type=F32)


def _dot_nt(a, b):
    return lax.dot_general(a.astype(BF16), b.astype(BF16), _NT, preferred_element_type=F32)


def _dot_tn(a, b):
    return lax.dot_general(a.astype(BF16), b.astype(BF16), _TN, preferred_element_type=F32)


def retention_fwd(name, z, cosf, sins, width_out):
    T = z.shape[0]
    nq = T // BLK
    scale = RET_DK ** -0.5

    def body(q_ref, k_ref, v_ref, cos_ref, sin_ref, o_ref, krot, vb):
        h, qi = pl.program_id(0), pl.program_id(1)
        lg = _ret_log_gamma(h)

        @pl.when(qi == 0)
        def _():
            krot[...] = (_rot(k_ref[...], cos_ref[...], sin_ref[...]) * scale).astype(BF16)
            vb[...] = v_ref[...].astype(BF16)

        rows = pl.ds(pl.multiple_of(qi * BLK, BLK), BLK)
        q = _rot(q_ref[...], cos_ref[rows, :], sin_ref[rows, :]).astype(BF16)

        def step(kb, acc):
            krows = pl.ds(pl.multiple_of(kb * BLK, BLK), BLK)
            s = _dot_nt(q, krot[krows, :]) * _decay_block(qi, kb, lg)
            return acc + _dot(s, vb[krows, :])

        o_ref[...] = lax.fori_loop(0, qi + 1, step, jnp.zeros((BLK, RET_DV), F32))

    return pl.pallas_call(
        body, name=name, grid=(RET_HEADS, nq),
        in_specs=[pl.BlockSpec((BLK, RET_DK), lambda h, i: (i, OFF_RQ // RET_DK + h)),
                  pl.BlockSpec((T, RET_DK), lambda h, i: (0, OFF_RK // RET_DK + h)),
                  pl.BlockSpec((T, RET_DV), lambda h, i: (0, OFF_RV // RET_DV + h)),
                  pl.BlockSpec((T, RET_DK), lambda h, i: (0, 0)),
                  pl.BlockSpec((T, RET_DK), lambda h, i: (0, 0))],
        out_specs=pl.BlockSpec((BLK, RET_DV), lambda h, i: (i, h)),
        out_shape=jax.ShapeDtypeStruct((T, width_out), F32),
        scratch_shapes=[pltpu.VMEM((T, RET_DK), BF16), pltpu.VMEM((T, RET_DV), BF16)],
        compiler_params=_params("arbitrary", "arbitrary"))(z, z, z, cosf, sins)


def retention_bwd(name, z, cosf, sins, do):
    T = z.shape[0]
    nq = T // BLK
    scale = RET_DK ** -0.5

    def body(q_ref, k_ref, v_ref, cos_ref, sin_ref, do_ref, dq_ref, dk_ref, dv_ref, krot, vb, dk_acc, dv_acc):
        h, qi = pl.program_id(0), pl.program_id(1)
        lg = _ret_log_gamma(h)

        @pl.when(qi == 0)
        def _():
            krot[...] = (_rot(k_ref[...], cos_ref[...], sin_ref[...]) * scale).astype(BF16)
            vb[...] = v_ref[...].astype(BF16)
            dk_acc[...] = jnp.zeros_like(dk_acc)
            dv_acc[...] = jnp.zeros_like(dv_acc)

        rows = pl.ds(pl.multiple_of(qi * BLK, BLK), BLK)
        cos_q, sin_q = cos_ref[rows, :], sin_ref[rows, :]
        q = _rot(q_ref[...], cos_q, sin_q).astype(BF16)
        dout = do_ref[...].astype(BF16)

        def step(kb, dq):
            krows = pl.ds(pl.multiple_of(kb * BLK, BLK), BLK)
            kk, vv = krot[krows, :], vb[krows, :]
            dec = _decay_block(qi, kb, lg)
            p = (_dot_nt(q, kk) * dec).astype(BF16)
            ds = (_dot_nt(dout, vv) * dec).astype(BF16)
            dk_acc[krows, :] += _dot_tn(ds, q)
            dv_acc[krows, :] += _dot_tn(p, dout)
            return dq + _dot(ds, kk)

        dq = lax.fori_loop(0, qi + 1, step, jnp.zeros((BLK, RET_DK), F32))
        dq_ref[...] = _unrot(dq, cos_q, sin_q).astype(dq_ref.dtype)

        @pl.when(qi == nq - 1)
        def _():
            dk_ref[...] = (_unrot(dk_acc[...], cos_ref[...], sin_ref[...]) * scale).astype(dk_ref.dtype)
            dv_ref[...] = dv_acc[...].astype(dv_ref.dtype)

    full = lambda h, i: (0, h)
    return pl.pallas_call(
        body, name=name, grid=(RET_HEADS, nq),
        in_specs=[pl.BlockSpec((BLK, RET_DK), lambda h, i: (i, OFF_RQ // RET_DK + h)),
                  pl.BlockSpec((T, RET_DK), lambda h, i: (0, OFF_RK // RET_DK + h)),
                  pl.BlockSpec((T, RET_DV), lambda h, i: (0, OFF_RV // RET_DV + h)),
                  pl.BlockSpec((T, RET_DK), lambda h, i: (0, 0)),
                  pl.BlockSpec((T, RET_DK), lambda h, i: (0, 0)),
                  pl.BlockSpec((BLK, RET_DV), lambda h, i: (i, h))],
        out_specs=[pl.BlockSpec((BLK, RET_DK), lambda h, i: (i, h)),
                   pl.BlockSpec((T, RET_DK), full), pl.BlockSpec((T, RET_DV), full)],
        out_shape=[jax.ShapeDtypeStruct((T, RET_QK), BF16), jax.ShapeDtypeStruct((T, RET_QK), BF16),
                   jax.ShapeDtypeStruct((T, RET_V), BF16)],
        scratch_shapes=[pltpu.VMEM((T, RET_DK), BF16), pltpu.VMEM((T, RET_DV), BF16),
                        pltpu.VMEM((T, RET_DK), F32), pltpu.VMEM((T, RET_DV), F32)],
        compiler_params=_params("arbitrary", "arbitrary"))(z, z, z, cosf, sins, do)


def _gla_chunk(q_ref, k_ref, v_ref, glr_ref, gu, gb, rows, trilf):
    zg = _dot(glr_ref[rows, :], gu) + gb
    la = (jnp.minimum(zg, 0.0) - jnp.log(1.0 + jnp.exp(-jnp.abs(zg)))) * (1.0 / GLA_GATE_NORM)
    cum = jnp.dot(trilf, la, precision=HIGHEST, preferred_element_type=F32)
    last = jnp.sum(la, axis=0, keepdims=True)
    ecum = jnp.exp(cum)
    k = k_ref[rows, :]
    qt = q_ref[rows, :] * (GLA_DK ** -0.5) * ecum
    kt = k * jnp.exp(-cum)
    kh = k * jnp.exp(last - cum)
    return zg, cum, last, ecum, qt, kt, kh, v_ref[rows, :].astype(BF16)


def _state_decay(last):
    e = jnp.exp(jnp.broadcast_to(last, (GLA_DK, GLA_DK)).T)
    return jnp.concatenate([e] * (GLA_DV // GLA_DK), axis=1)


def _gla_specs(T):
    return [pl.BlockSpec((T, GLA_DK), lambda h: (0, OFF_GQ // GLA_DK + h)),
            pl.BlockSpec((T, GLA_DK), lambda h: (0, OFF_GK // GLA_DK + h)),
            pl.BlockSpec((T, GLA_DV), lambda h: (0, OFF_GV // GLA_DV + h)),
            pl.BlockSpec((T, LANE), lambda h: (0, 0)),
            pl.BlockSpec((LANE, GLA_DK), lambda h: (0, h)),
            pl.BlockSpec((1, GLA_DK), lambda h: (0, h))]


def gla_fwd(name, z, glr, gu, gb, o_prev):
    T = z.shape[0]
    nc = T // CHUNK

    def body(q_ref, k_ref, v_ref, glr_ref, gu_ref, gb_ref, prev_ref, o_ref, S):
        del prev_ref
        gu_b, gb_v = gu_ref[...].astype(BF16), gb_ref[...]
        ri = lax.broadcasted_iota(jnp.int32, (CHUNK, CHUNK), 0)
        ci = lax.broadcasted_iota(jnp.int32, (CHUNK, CHUNK), 1)
        tril = ri >= ci
        trilf = tril.astype(F32)
        S[...] = jnp.zeros_like(S)

        def step(c, carry):
            rows = pl.ds(pl.multiple_of(c * CHUNK, CHUNK), CHUNK)
            _, _, last, _, qt, kt, kh, v = _gla_chunk(q_ref, k_ref, v_ref, glr_ref, gu_b, gb_v, rows, trilf)
            a = jnp.where(tril, _dot_nt(qt, kt), 0.0)
            s_prev = S[...]
            o_ref[rows, :] = _dot(a, v) + _dot(qt, s_prev)
            S[...] = s_prev * _state_decay(last) + _dot_tn(kh, v)
            return carry

        lax.fori_loop(0, nc, step, 0)

    n_in = 6
    return pl.pallas_call(
        body, name=name, grid=(GLA_HEADS,),
        in_specs=_gla_specs(T) + [pl.BlockSpec(memory_space=pl.ANY)],
        out_specs=pl.BlockSpec((T, GLA_DV), lambda h: (0, RET_V // GLA_DV + h)),
        out_shape=jax.ShapeDtypeStruct(o_prev.shape, F32),
        scratch_shapes=[pltpu.VMEM((GLA_DK, GLA_DV), F32)],
        input_output_aliases={n_in: 0},
        compiler_params=_params("arbitrary"))(z, z, z, glr, gu, gb, o_prev)


def gla_bwd(name, z, glr, gu, gb, do):
    T = z.shape[0]
    nc = T // CHUNK

    def body(q_ref, k_ref, v_ref, glr_ref, gu_ref, gb_ref, do_ref,
             dq_ref, dk_ref, dv_ref, dglr_ref, dgu_ref, dgb_ref, s_all, dS):
        gu_b, gb_v = gu_ref[...].astype(BF16), gb_ref[...]
        ri = lax.broadcasted_iota(jnp.int32, (CHUNK, CHUNK), 0)
        ci = lax.broadcasted_iota(jnp.int32, (CHUNK, CHUNK), 1)
        tril = ri >= ci
        trilf = tril.astype(F32)
        triuf = (ri <= ci).astype(F32)
        last_row = lax.broadcasted_iota(jnp.int32, (CHUNK, GLA_DK), 0) == CHUNK - 1
        ones8 = jnp.ones((8, GLA_DV), F32)

        def fstep(c, s_prev):
            rows = pl.ds(pl.multiple_of(c * CHUNK, CHUNK), CHUNK)
            s_all[c] = s_prev
            _, _, last, _, _, _, kh, v = _gla_chunk(q_ref, k_ref, v_ref, glr_ref, gu_b, gb_v, rows, trilf)
            return s_prev * _state_decay(last) + _dot_tn(kh, v)

        lax.fori_loop(0, nc, fstep, jnp.zeros((GLA_DK, GLA_DV), F32))
        dS[...] = jnp.zeros_like(dS)
        dgu_ref[...] = jnp.zeros_like(dgu_ref)
        dgb_ref[...] = jnp.zeros_like(dgb_ref)

        def bstep(i, carry):
            c = nc - 1 - i
            rows = pl.ds(pl.multiple_of(c * CHUNK, CHUNK), CHUNK)
            zg, cum, last, ecum, qt, kt, kh, v = _gla_chunk(q_ref, k_ref, v_ref, glr_ref, gu_b, gb_v, rows, trilf)
            a = jnp.where(tril, _dot_nt(qt, kt), 0.0)
            s_prev, ds_new = s_all[c], dS[...]
            dout = do_ref[rows, :].astype(BF16)
            dv_ref[rows, :] = (_dot_tn(a, dout) + _dot(kh, ds_new)).astype(dv_ref.dtype)
            da = jnp.where(tril, _dot_nt(dout, v), 0.0)
            dqt = _dot(da, kt) + _dot_nt(dout, s_prev)
            dkt = _dot_tn(da, qt)
            dkh = _dot_nt(v, ds_new)
            decay = _state_decay(last)
            dS[...] = ds_new * decay + _dot_tn(qt, dout)
            dq_ref[rows, :] = (dqt * ecum * (GLA_DK ** -0.5)).astype(dq_ref.dtype)
            dk_ref[rows, :] = (dkt * jnp.exp(-cum) + dkh * jnp.exp(last - cum)).astype(dk_ref.dtype)
            dkh_kh = dkh * kh
            dcum = dqt * qt - dkt * kt - dkh_kh
            rs = lax.dot_general(ones8, ds_new * s_prev, _NT, precision=HIGHEST, preferred_element_type=F32)
            dlast = (jnp.sum(dkh_kh, axis=0, keepdims=True)
                     + jnp.exp(last) * (jnp.sum(rs, axis=0, keepdims=True) * 0.125))
            dcum = dcum + jnp.where(last_row, dlast, 0.0)
            dla = jnp.dot(triuf, dcum, precision=HIGHEST, preferred_element_type=F32)
            dzg = dla * (1.0 / GLA_GATE_NORM) * _sigmoid(-zg)
            glr_c = glr_ref[rows, :]
            dglr_ref[rows, :] = _dot_nt(dzg, gu_b)
            dgu_ref[...] += _dot_tn(glr_c, dzg)
            dgb_ref[...] += jnp.sum(dzg, axis=0, keepdims=True)
            return carry

        lax.fori_loop(0, nc, bstep, 0)

    return pl.pallas_call(
        body, name=name, grid=(GLA_HEADS,),
        in_specs=_gla_specs(T) + [pl.BlockSpec((T, GLA_DV), lambda h: (0, RET_V // GLA_DV + h))],
        out_specs=[pl.BlockSpec((T, GLA_DK), lambda h: (0, h)), pl.BlockSpec((T, GLA_DK), lambda h: (0, h)),
                   pl.BlockSpec((T, GLA_DV), lambda h: (0, h)),
                   pl.BlockSpec((None, T, LANE), lambda h: (h, 0, 0)),
                   pl.BlockSpec((LANE, GLA_DK), lambda h: (0, h)), pl.BlockSpec((1, GLA_DK), lambda h: (0, h))],
        out_shape=[jax.ShapeDtypeStruct((T, GLA_QK), BF16), jax.ShapeDtypeStruct((T, GLA_QK), BF16),
                   jax.ShapeDtypeStruct((T, GLA_V), BF16), jax.ShapeDtypeStruct((GLA_HEADS, T, LANE), F32),
                   jax.ShapeDtypeStruct((LANE, GLA_QK), F32), jax.ShapeDtypeStruct((1, GLA_QK), F32)],
        scratch_shapes=[pltpu.VMEM((nc, GLA_DK, GLA_DV), F32), pltpu.VMEM((GLA_DK, GLA_DV), F32)],
        compiler_params=_params("arbitrary"))(z, z, z, glr, gu, gb, do)


HN_HEADS = RET_HEADS + GLA_HEADS
HN_W = RET_DV


def _gate_col(h):
    return jnp.where(h < RET_HEADS, OFF_RG // HN_W + h, OFF_GG // HN_W + h - RET_HEADS)


def headnorm_fwd(name, oraw, z, w, tr=256):
    T = oraw.shape[0]

    def body(o_ref, g_ref, w_ref, y_ref):
        y_ref[...] = (_rms(o_ref[...], w_ref[...]) * _silu_and_grad(g_ref[...])[0]).astype(y_ref.dtype)

    return pl.pallas_call(
        body, name=name, grid=(HN_HEADS, T // tr),
        in_specs=[pl.BlockSpec((tr, HN_W), lambda h, i: (i, h)),
                  pl.BlockSpec((tr, HN_W), lambda h, i: (i, _gate_col(h))),
                  pl.BlockSpec((1, HN_W), lambda h, i: (0, h))],
        out_specs=pl.BlockSpec((tr, HN_W), lambda h, i: (i, h)),
        out_shape=jax.ShapeDtypeStruct((T, HN_HEADS * HN_W), BF16),
        compiler_params=_params("arbitrary", "arbitrary"))(oraw, z, w)


def headnorm_bwd(name, oraw, z, w, dy, tr=256):
    T = oraw.shape[0]

    def body(o_ref, g_ref, w_ref, dy_ref, do_ref, dg_ref, dw_ref):
        o, wv, dyv = o_ref[...], w_ref[...], dy_ref[...].astype(F32)
        silu, dsilu = _silu_and_grad(g_ref[...])
        n = _rms(o, wv)
        dg_ref[...] = (dyv * n * dsilu).astype(dg_ref.dtype)
        dx, dw = _rms_bwd(o, wv, dyv * silu)
        do_ref[...] = dx
        _accumulate(dw_ref, dw, pl.program_id(1) == 0)

    blk = pl.BlockSpec((tr, HN_W), lambda h, i: (i, h))
    return pl.pallas_call(
        body, name=name, grid=(HN_HEADS, T // tr),
        in_specs=[blk, pl.BlockSpec((tr, HN_W), lambda h, i: (i, _gate_col(h))),
                  pl.BlockSpec((1, HN_W), lambda h, i: (0, h)), blk],
        out_specs=[blk, blk, pl.BlockSpec((1, HN_W), lambda h, i: (0, h))],
        out_shape=[jax.ShapeDtypeStruct((T, HN_HEADS * HN_W), F32),
                   jax.ShapeDtypeStruct((T, HN_HEADS * HN_W), BF16),
                   jax.ShapeDtypeStruct((1, HN_HEADS * HN_W), F32)],
        compiler_params=_params("arbitrary", "arbitrary"))(oraw, z, w, dy)


N_MASKS = 4


def _check_mask_classes(T):
    for window, dilation in DILATED_BRANCHES[:-1]:
        assert window < (N_MASKS - 1) * BLK - (BLK - 1) and BLK % dilation == 0
    assert DILATED_BRANCHES[-1][0] >= T and BLK % DILATED_BRANCHES[-1][1] == 0


def _fill_masks(mult_ref, bias_ref):
    ri = lax.broadcasted_iota(jnp.int32, (BLK, BLK), 0)
    ci = lax.broadcasted_iota(jnp.int32, (BLK, BLK), 1)
    for d in range(N_MASKS):
        dt = d * BLK + ri - ci
        mult = jnp.zeros((BLK, BLK), F32)
        for window, dilation in DILATED_BRANCHES:
            hit = (dt >= 0) & (dt <= window) & ((dt & (dilation - 1)) == 0)
            mult = mult + hit.astype(F32)
        mult_ref[d] = mult
        bias_ref[d] = jnp.where(mult > 0, 0.0, -1e30)


def attn_fwd(name, qkv):
    T = qkv.shape[0]
    D = qkv.shape[1] // 3
    dh = D // ATT_HEADS
    nq = T // BLK
    scale = dh ** -0.5

    _check_mask_classes(T)

    def body(q_ref, k_ref, v_ref, o_ref, lse_ref, mult_ref, bias_ref):
        qi = pl.program_id(1)
        q = q_ref[...]

        @pl.when((pl.program_id(0) == 0) & (qi == 0))
        def _():
            _fill_masks(mult_ref, bias_ref)

        def step(kb, carry):
            m, l, acc = carry
            krows = pl.ds(pl.multiple_of(kb * BLK, BLK), BLK)
            d = jnp.minimum(qi - kb, N_MASKS - 1)
            s = _dot_nt(q, k_ref[krows, :]) * scale + bias_ref[d]
            m_new = jnp.maximum(m, jnp.max(s, axis=-1, keepdims=True))
            alpha = jnp.exp(m - m_new)
            p = mult_ref[d] * jnp.exp(s - m_new)
            l = l * alpha + jnp.sum(p, axis=-1, keepdims=True)
            acc = acc * alpha + _dot(p, v_ref[krows, :])
            return m_new, l, acc

        m, l, acc = lax.fori_loop(
            0, qi + 1, step,
            (jnp.full((BLK, 1), -1e30, F32), jnp.zeros((BLK, 1), F32), jnp.zeros((BLK, dh), F32)))
        o_ref[...] = (acc / l).astype(o_ref.dtype)
        lse_ref[...] = jnp.broadcast_to(m + jnp.log(l), (BLK, LANE))

    return pl.pallas_call(
        body, name=name, grid=(ATT_HEADS, nq),
        in_specs=[pl.BlockSpec((BLK, dh), lambda h, i: (i, h)),
                  pl.BlockSpec((T, dh), lambda h, i: (0, ATT_HEADS + h)),
                  pl.BlockSpec((T, dh), lambda h, i: (0, 2 * ATT_HEADS + h))],
        out_specs=[pl.BlockSpec((BLK, dh), lambda h, i: (i, h)),
                   pl.BlockSpec((None, BLK, LANE), lambda h, i: (h, i, 0))],
        out_shape=[jax.ShapeDtypeStruct((T, D), BF16), jax.ShapeDtypeStruct((ATT_HEADS, T, LANE), F32)],
        scratch_shapes=[pltpu.VMEM((N_MASKS, BLK, BLK), F32), pltpu.VMEM((N_MASKS, BLK, BLK), F32)],
        compiler_params=_params("arbitrary", "arbitrary"))(qkv, qkv, qkv)


def attn_bwd(name, qkv, o, lse, do):
    T = qkv.shape[0]
    D = qkv.shape[1] // 3
    dh = D // ATT_HEADS
    nq = T // BLK
    scale = dh ** -0.5

    _check_mask_classes(T)

    def body(q_ref, k_ref, v_ref, o_ref, lse_ref, do_ref, dq_ref, dk_ref, dv_ref, dk_acc, dv_acc, mult_ref, bias_ref):
        qi = pl.program_id(1)

        @pl.when((pl.program_id(0) == 0) & (qi == 0))
        def _():
            _fill_masks(mult_ref, bias_ref)

        @pl.when(qi == 0)
        def _():
            dk_acc[...] = jnp.zeros_like(dk_acc)
            dv_acc[...] = jnp.zeros_like(dv_acc)

        q, dout = q_ref[...], do_ref[...]
        delta = jnp.sum(dout.astype(F32) * o_ref[...].astype(F32), axis=-1, keepdims=True)
        lse = jnp.concatenate([lse_ref[...]] * (BLK // LANE), axis=1)

        def step(kb, dq):
            krows = pl.ds(pl.multiple_of(kb * BLK, BLK), BLK)
            kk, vv = k_ref[krows, :], v_ref[krows, :]
            d = jnp.minimum(qi - kb, N_MASKS - 1)
            s = _dot_nt(q, kk) * scale + bias_ref[d]
            p = mult_ref[d] * jnp.exp(s - lse)
            ds = (p * (_dot_nt(dout, vv) - delta) * scale).astype(BF16)
            dk_acc[krows, :] += _dot_tn(ds, q)
            dv_acc[krows, :] += _dot_tn(p, dout)
            return dq + _dot(ds, kk)

        dq_ref[...] = lax.fori_loop(0, qi + 1, step, jnp.zeros((BLK, dh), F32)).astype(dq_ref.dtype)

        @pl.when(qi == nq - 1)
        def _():
            dk_ref[...] = dk_acc[...].astype(dk_ref.dtype)
            dv_ref[...] = dv_acc[...].astype(dv_ref.dtype)

    blk = pl.BlockSpec((BLK, dh), lambda h, i: (i, h))
    full = pl.BlockSpec((T, dh), lambda h, i: (0, h))
    return pl.pallas_call(
        body, name=name, grid=(ATT_HEADS, nq),
        in_specs=[blk, pl.BlockSpec((T, dh), lambda h, i: (0, ATT_HEADS + h)),
                  pl.BlockSpec((T, dh), lambda h, i: (0, 2 * ATT_HEADS + h)),
                  blk, pl.BlockSpec((None, BLK, LANE), lambda h, i: (h, i, 0)), blk],
        out_specs=[blk, full, full],
        out_shape=[jax.ShapeDtypeStruct((T, D), BF16)] * 3,
        scratch_shapes=[pltpu.VMEM((T, dh), F32), pltpu.VMEM((T, dh), F32),
                        pltpu.VMEM((N_MASKS, BLK, BLK), F32), pltpu.VMEM((N_MASKS, BLK, BLK), F32)],
        compiler_params=_params("arbitrary", "arbitrary"))(qkv, qkv, qkv, o, lse, do)


def _mesh_pos():
    mx, my, mc = lax.axis_index("x"), lax.axis_index("y"), lax.axis_index("c")
    return mx, my, mc, 4 * mx + 2 * my + mc


def _peer(k, mx, my, mc):
    px, py, pc = mx ^ (k >> 2), my ^ ((k >> 1) & 1), mc ^ (k & 1)
    return (px, py, pc), 4 * px + 2 * py + pc


_SIBLING = 1
_OTHER_CHIPS = (4, 2, 6)
_PLANS = {"gather": (2, tuple(range(1, N_DEV))), "scatter": (2, tuple(range(1, N_DEV))),
          "to_chips": (2, (_SIBLING,) + _OTHER_CHIPS), "to_sibling": (1, _OTHER_CHIPS)}


def _copies(kind, items, send_sems, recv_sems):
    mx, my, mc, me = _mesh_pos()
    peers = _PLANS[kind][1]
    out = []
    for i, refs in enumerate(items):
        for j, k in enumerate(peers):
            peer, to = _peer(k, mx, my, mc)
            if kind == "to_sibling":
                src, dst, peer = refs[0].at[to], refs[0].at[to], _peer(_SIBLING, mx, my, mc)[0]
            elif kind == "scatter":
                src, dst = refs[0].at[to], refs[1].at[me]
            else:
                src, dst = refs[0], refs[1].at[me]
            n = i * len(peers) + j
            out.append(pltpu.make_async_remote_copy(
                src_ref=src, dst_ref=dst, send_sem=send_sems.at[n], recv_sem=recv_sems.at[n],
                device_id=peer, device_id_type=pl.DeviceIdType.MESH))
    return out


_HBM = pl.BlockSpec(memory_space=pltpu.HBM)
_SEM = pl.BlockSpec(memory_space=pltpu.SEMAPHORE)
_DATAFLOW = pltpu.SideEffectType.DATAFLOW_SIDE_EFFECTING


def _regroup(flat, groups, width):
    out, off = [], 0
    for g in groups:
        out.append([tuple(flat[off + i * width:off + (i + 1) * width]) for i in range(len(g))])
        off += len(g) * width
    return out


def exchange_start(name, kind, groups, deps=()):
    width, peers = _PLANS[kind]
    flat = [b for g in groups for item in g for b in item]
    n_buf, n_grp = len(flat), len(groups)

    def body(*refs):
        outs = refs[n_buf + len(deps):]
        for gi, items in enumerate(_regroup(refs[:n_buf], groups, width)):
            for cp in _copies(kind, items, outs[2 * gi], outs[2 * gi + 1]):
                cp.start()
        outs[-1][...] = jnp.zeros_like(outs[-1])

    bufs = [pltpu.with_memory_space_constraint(b, pltpu.HBM) for b in flat]
    sem_shapes = []
    for g in groups:
        sem_shapes += [pltpu.SemaphoreType.DMA((len(g) * len(peers),))] * 2
    outs = pl.pallas_call(
        body, name=name,
        out_shape=sem_shapes + [pltpu.HBM(b.shape, b.dtype) for b in bufs] + [jax.ShapeDtypeStruct((8, LANE), F32)],
        in_specs=[_HBM] * n_buf + [_ANY] * len(deps),
        out_specs=[_SEM] * (2 * n_grp) + [_HBM] * n_buf + [pl.BlockSpec(memory_space=pltpu.VMEM)],
        input_output_aliases={i: 2 * n_grp + i for i in range(n_buf)},
        compiler_params=pltpu.CompilerParams(has_side_effects=_DATAFLOW))(*bufs, *deps)
    sems, thru, token = outs[:2 * n_grp], outs[2 * n_grp:-1], outs[-1]
    items = _regroup(thru, groups, width)
    return [(items[gi], sems[2 * gi], sems[2 * gi + 1]) for gi in range(n_grp)], token


def exchange_wait(name, kind, handle, deps=()):
    items, send_sems, recv_sems = handle
    width = _PLANS[kind][0]
    flat = [b for item in items for b in item]
    n_buf = len(flat)

    def body(*refs):
        copies = _copies(kind, _regroup(refs[:n_buf], [items], width)[0], refs[n_buf], refs[n_buf + 1])
        for cp in copies:
            cp.wait_send()
        for cp in copies:
            cp.wait_recv()

    outs = pl.pallas_call(
        body, name=name, out_shape=[pltpu.HBM(b.shape, b.dtype) for b in flat],
        in_specs=[_HBM] * n_buf + [_SEM, _SEM] + [_ANY] * len(deps), out_specs=[_HBM] * n_buf,
        input_output_aliases={i: i for i in range(n_buf)},
        compiler_params=pltpu.CompilerParams(has_side_effects=_DATAFLOW))(*flat, send_sems, recv_sems, *deps)
    return _regroup(outs, [items], width)[0]


def gather_small(name, a, deps=()):
    def body(a_ref, *rest):
        o_ref, send_sems, recv_sems, local_sem = rest[len(deps):]
        me = _mesh_pos()[3]
        own = pltpu.make_async_copy(a_ref, o_ref.at[me], local_sem)
        own.start()
        copies = _copies("gather", [(a_ref, o_ref)], send_sems, recv_sems)
        for cp in copies:
            cp.start()
        for cp in copies:
            cp.wait_recv()
        for cp in copies:
            cp.wait_send()
        own.wait()

    return pl.pallas_call(
        body, name=name, in_specs=[_ANY] * (1 + len(deps)), out_specs=_ANY,
        out_shape=jax.ShapeDtypeStruct((N_DEV,) + a.shape, a.dtype),
        scratch_shapes=[pltpu.SemaphoreType.DMA((N_DEV - 1,)), pltpu.SemaphoreType.DMA((N_DEV - 1,)),
                        pltpu.SemaphoreType.DMA],
        compiler_params=pltpu.CompilerParams(has_side_effects=True))(a, *deps)


def _adamw_math(w, g, m, v):
    m2 = ADAM_B1 * m + (1.0 - ADAM_B1) * g
    v2 = ADAM_B2 * v + (1.0 - ADAM_B2) * (g * g)
    m_hat = m2 / (1.0 - ADAM_B1 ** ADAM_STEP)
    v_hat = v2 / (1.0 - ADAM_B2 ** ADAM_STEP)
    delta = -ADAM_LR * (m_hat / (jnp.sqrt(v_hat) + ADAM_EPS) + ADAM_WD * w)
    return delta, m2, v2


def adamw(name, w, m, v, l, land, own, prev=None):
    L, r, c = w.shape
    cp = land.shape[2]
    tr = _pick(r, (256, 176, 128, 64, 32, 16, 8))

    def body(w_ref, m_ref, v_ref, land_ref, own_ref, *rest):
        g_ref, d_ref, m2_ref, v2_ref = rest[-4:]
        me = _mesh_pos()[3]
        mine = own_ref[:, pl.ds(0, c)].astype(F32)
        g = None
        for s in range(N_DEV):
            part = jnp.where(me == s, mine, land_ref[s, :, pl.ds(0, c)].astype(F32))
            g = part if g is None else g + part
        delta, m2, v2 = _adamw_math(w_ref[...], g, m_ref[...], v_ref[...])
        g_ref[...] = g
        d_ref[...] = delta
        m2_ref[...] = m2
        v2_ref[...] = v2

    blk = pl.BlockSpec((None, tr, c), lambda i: (l, i, 0))
    shape = jax.ShapeDtypeStruct((L, r, c), F32)
    extra = [] if prev is None else list(prev)
    return pl.pallas_call(
        body, name=name, grid=(r // tr,),
        in_specs=[blk, blk, blk, pl.BlockSpec((N_DEV, tr, cp), lambda i: (0, i, 0)),
                  pl.BlockSpec((None, tr, cp), lambda i: (_mesh_pos()[3], i, 0))] + [_ANY] * len(extra),
        out_specs=[blk] * 4, out_shape=[shape] * 4,
        input_output_aliases={5 + k: k for k in range(len(extra))},
        compiler_params=_params("parallel"))(w, m, v, land, own, *extra)


def adamw_small(name, w, m, v, parts):
    n = w.shape[1]

    def body(w_ref, m_ref, v_ref, p_ref, g_ref, d_ref, m2_ref, v2_ref):
        g = p_ref[0:1, :]
        for s in range(1, N_DEV):
            g = g + p_ref[s:s + 1, :]
        delta, m2, v2 = _adamw_math(w_ref[...], g, m_ref[...], v_ref[...])
        g_ref[...] = g
        d_ref[...] = delta
        m2_ref[...] = m2
        v2_ref[...] = v2

    shape = jax.ShapeDtypeStruct((1, n), F32)
    return pl.pallas_call(body, name=name, out_shape=[shape] * 4,
                          compiler_params=pltpu.CompilerParams(vmem_limit_bytes=VMEM_LIMIT_BYTES))(w, m, v, parts)


def _rope_tables(positions):
    half = RET_DK // 2
    inv_freq = 1.0 / jnp.power(RET_THETA_BASE, jnp.linspace(0.0, 1.0, half, dtype=F32))
    ang = positions.astype(F32)[:, None] * inv_freq
    cos, sin = jnp.cos(ang), jnp.sin(ang)
    cosf = jnp.repeat(cos, 2, axis=-1)
    sins = jnp.stack([-sin, sin], axis=-1).reshape(cosf.shape)
    return cosf, sins


def _pad_to(a, axis, size):
    pad = [(0, 0)] * a.ndim
    pad[axis] = (0, size - a.shape[axis])
    return jnp.pad(a, pad)


def _round_up(n, m):
    return -(-n // m) * m


def kernel(x, p, positions, attn_norm_w, ffn_norm_w, ple_norm_w, final_norm_w, ab_w_in, ab_gla_gate_up, ab_gla_gate_b, ab_ret_norm_w, ab_gla_norm_w, ab_w_out, c_w_qkv, c_w_out, ffn_w_gate, ffn_w_up, ffn_w_down, ple_w_proj, ple_w_gate, loss_target, m_attn_norm_w, m_ffn_norm_w, m_ple_norm_w, m_final_norm_w, m_ab_w_in, m_ab_gla_gate_up, m_ab_gla_gate_b, m_ab_ret_norm_w, m_ab_gla_norm_w, m_ab_w_out, m_c_w_qkv, m_c_w_out, m_ffn_w_gate, m_ffn_w_up, m_ffn_w_down, m_ple_w_proj, m_ple_w_gate, v_attn_norm_w, v_ffn_norm_w, v_ple_norm_w, v_final_norm_w, v_ab_w_in, v_ab_gla_gate_up, v_ab_gla_gate_b, v_ab_ret_norm_w, v_ab_gla_norm_w, v_ab_w_out, v_c_w_qkv, v_c_w_out, v_ffn_w_gate, v_ffn_w_up, v_ffn_w_down, v_ple_w_proj, v_ple_w_gate):
    T, D = x.shape[1], x.shape[2]
    depth = attn_norm_w.shape[0]
    assert ab_w_in.shape[0] == 1 and c_w_qkv.shape[0] == 1 and depth == 2, "one even and one odd layer"
    me = 4 * lax.axis_index("x") + 2 * lax.axis_index("y") + lax.axis_index("c")
    in_shard = ab_w_in.shape[2]
    in_width = in_shard * N_DEV
    assert in_width == OFF_LR + GLA_GATE_RANK
    fs = ffn_w_gate.shape[2]
    fp = _round_up(fs, LANE)
    gu_cols = ab_gla_gate_up.shape[2]

    bf = lambda a: a.astype(BF16)
    tr_ = lambda a: jnp.swapaxes(a, -1, -2)
    wg_t, wu_t = tr_(ffn_w_gate), tr_(ffn_w_up)
    srcs = {"w_in": bf(tr_(ab_w_in[0])), "w_oab": bf(ab_w_out[0]), "gu": ab_gla_gate_up[0],
            "w_qkv": bf(c_w_qkv[0]), "w_oc": bf(c_w_out[0])}
    for l in range(depth):
        srcs[f"wg{l}"] = _pad_to(bf(wg_t[l]), 0, fp)
        srcs[f"wu{l}"] = _pad_to(bf(wu_t[l]), 0, fp)
        srcs[f"wd{l}"] = _pad_to(bf(ffn_w_down[l]), 0, fp)
        srcs[f"wpg{l}"] = bf(ple_w_gate[l])
        srcs[f"wpp{l}"] = bf(ple_w_proj[l])
    group_keys = [["w_in", "w_oab", "gu"], ["wg0", "wu0", "wd0", "wpg0", "wpp0"], ["w_qkv", "w_oc"],
                  ["wg1", "wu1", "wd1", "wpg1", "wpp1"]]

    def landing(a):
        return lax.dynamic_update_slice(lax.empty((N_DEV,) + a.shape, a.dtype), a[None], (me,) + (0,) * a.ndim)

    gather_handles, gather_token = exchange_start(
        "gather_start", "to_chips", [[(srcs[k], landing(srcs[k])) for k in keys] for keys in group_keys])
    weights = {}

    def gather_wait(gi, dep):
        arrived = exchange_wait(f"gather_wait{gi}", "to_chips", gather_handles[gi], deps=(dep,))
        passing, _ = exchange_start(f"gather_pass{gi}", "to_sibling", [[(land,) for _, land in arrived]])
        complete = exchange_wait(f"gather_pass_wait{gi}", "to_sibling", passing[0])
        weights.update(zip(group_keys[gi], [land for (land,) in complete]))

    gb = ab_gla_gate_b
    hn_w = jnp.concatenate([ab_ret_norm_w, ab_gla_norm_w], axis=1)
    cosf, sins = _rope_tables(positions[0])
    p_bf = bf(p[:, 0])

    xs = x[0]
    saved = []
    for i in range(depth):
        nm = f"l{i}_"
        w_attn, w_ffn, w_ple = attn_norm_w[i:i + 1], ffn_norm_w[i:i + 1], ple_norm_w[i:i + 1]
        (xn,) = rowwise(nm + "norm_attn", lambda a, w: (_rms(a, w),), T, [("row", xs), ("full", w_attn)],
                        [("row", D, BF16)], deps=(gather_token,) if i == 0 else ())
        gather_wait(2 * i, xn)
        if i % 2 == 0:
            w_in_t = weights["w_in"].reshape(1, 1, in_width, D)
            w_lr_t = _pad_to(weights["w_in"].reshape(in_width, D)[OFF_LR:], 0, LANE).reshape(1, 1, LANE, D)
            w_oab = weights["w_oab"].reshape(1, 1, D, D)
            gu_full = _pad_to(weights["gu"].transpose(1, 0, 2).reshape(GLA_GATE_RANK, GLA_QK), 0, LANE)
            z = mmt_fwd(nm + "mm_in", xn, w_in_t, 0, F32, n=OFF_LR)
            glr = mmt_fwd(nm + "mm_lr", xn, w_lr_t, 0, F32)
            oraw = retention_fwd(nm + "ret_fwd", z, cosf, sins, RET_V + GLA_V)
            oraw = gla_fwd(nm + "gla_fwd", z, glr, gu_full, gb, oraw)
            o = headnorm_fwd(nm + "headnorm_fwd", oraw, z, hn_w)
            mix = mm_nn(nm + "mm_out", o, w_oab, 0, F32)
            mixer_saved = (z, glr, oraw, o)
        else:
            w_qkv = weights["w_qkv"].reshape((1,) + weights["w_qkv"].shape)
            w_oc = weights["w_oc"].reshape(1, 1, D, D)
            qkv = mm_nn(nm + "mm_qkv", xn, w_qkv, 0, BF16)
            o, lse = attn_fwd(nm + "attn_fwd", qkv)
            mix = mm_nn(nm + "mm_out", o, w_oc, 0, F32)
            mixer_saved = (qkv, o, lse)
        h1, hn = rowwise(nm + "add_norm_ffn", lambda a, b, w: (a + b, _rms(a + b, w)), T,
                         [("row", xs), ("row", mix), ("full", w_ffn)], [("row", D, F32), ("row", D, BF16)])
        gather_wait(2 * i + 1, hn)
        wg = weights[f"wg{i}"].reshape(1, N_DEV, fp, D)
        wu = weights[f"wu{i}"].reshape(1, N_DEV, fp, D)
        wd = weights[f"wd{i}"].reshape(1, 1, N_DEV * fp, D)
        wpg = weights[f"wpg{i}"].reshape(1, 1, D, D)
        wpp = weights[f"wpp{i}"].reshape((1,) + weights[f"wpp{i}"].shape)
        g = mmt_fwd(nm + "mm_gate", hn, wg, 0, F32)
        u = mmt_fwd(nm + "mm_up", hn, wu, 0, F32)
        (act,) = rowwise(nm + "swiglu", lambda a, b: (_silu_and_grad(a)[0] * b,), T, [("row", g), ("row", u)],
                         [("row", g.shape[1], BF16)])
        f = mm_nn(nm + "mm_down", act, wd, 0, F32)
        h2, pn = rowwise(nm + "add_norm_ple", lambda a, b, w: (a + b, _rms(a + b, w)), T,
                         [("row", h1), ("row", f), ("full", w_ple)], [("row", D, F32), ("row", D, BF16)])
        s = mm_nn(nm + "mm_ple_gate", pn, wpg, 0, F32)
        e = mm_nn(nm + "mm_ple_proj", p_bf[i], wpp, 0, F32)
        (x_next,) = rowwise(nm + "ple_out", lambda a, b, c: (a + _sigmoid(b) * c,), T,
                            [("row", h2), ("row", s), ("row", e)], [("row", D, F32)])
        mixer_w = (w_in_t, w_lr_t, w_oab, gu_full) if i % 2 == 0 else (w_qkv, w_oc)
        saved.append((xs, xn, mixer_saved, mixer_w, (wg, wu, wd, wpg), h1, hn, g, u, act, h2, pn, s, e))
        xs = x_next

    def loss_fn(a, w, t):
        diff = _rms(a, w) - t
        dx, dw = _rms_bwd(a, w, diff * (1.0 / D))
        part = 0.5 * jnp.sum(jnp.mean(diff * diff, axis=-1, keepdims=True), axis=0, keepdims=True)
        return dx, dw, jnp.broadcast_to(part, (1, LANE))

    dx, d_final_w, loss_part = rowwise("loss_head", loss_fn, T,
                                       [("row", xs), ("full", final_norm_w[None, :]), ("row", loss_target[0])],
                                       [("row", D, F32), ("acc", D), ("acc", LANE)])
    loss = lax.psum(loss_part[0, 0], ("x", "y", "c"))

    grads = {}
    scatters = []

    def scatter_start(name, keys):
        handles, token = exchange_start(name, "scatter", [[(grads[k], lax.empty(grads[k].shape, BF16)) for k in keys]])
        scatters.append((keys, handles[0]))
        return token

    d_attn_w, d_ffn_w, d_ple_w = [None] * depth, [None] * depth, [None] * depth
    for i in reversed(range(depth)):
        nm = f"l{i}_b_"
        xs_i, xn, mixer_saved, mixer_w, (wg, wu, wd, wpg), h1, hn, g, u, act, h2, pn, s, e = saved[i]
        w_attn, w_ffn, w_ple = attn_norm_w[i:i + 1], ffn_norm_w[i:i + 1], ple_norm_w[i:i + 1]

        def ple_bwd(d, sv, ev):
            gate = _sigmoid(sv)
            return d * gate, d * ev * gate * (1.0 - gate)

        de, ds = rowwise(nm + "ple_out", ple_bwd, T, [("row", dx), ("row", s), ("row", e)],
                         [("row", D, BF16), ("row", D, BF16)])
        grads[("ple_w_proj", i)] = mm_tn(nm + "mm_ple_proj_w", p_bf[i], de, N_DEV, BF16)
        grads[("ple_w_gate", i)] = mm_tn(nm + "mm_ple_gate_w", pn, ds, 1, BF16).reshape(N_DEV, D // N_DEV, D)
        dpn = mm_nt(nm + "mm_ple_gate_x", ds, wpg, 0, F32)

        def norm_bwd_add(a, w, dn, dres):
            dxx, dw = _rms_bwd(a, w, dn)
            tot = dres + dxx
            return tot, tot, dw

        dh2, dh2_bf, d_ple_w[i] = rowwise(nm + "norm_ple", norm_bwd_add, T,
                                          [("row", h2), ("full", w_ple), ("row", dpn), ("row", dx)],
                                          [("row", D, F32), ("row", D, BF16), ("acc", D)])
        grads[("ffn_w_down", i)] = mm_tn(nm + "mm_down_w", act, dh2_bf, 1, BF16).reshape(N_DEV, fp, D)
        token = scatter_start(nm + "scatter_ple_down", [("ple_w_proj", i), ("ple_w_gate", i), ("ffn_w_down", i)])
        dact = mm_nt(nm + "mm_down_x", dh2_bf, wd, 0, F32, deps=(token,))

        def swiglu_bwd(da, gv, uv):
            silu, dsilu = _silu_and_grad(gv)
            return da * uv * dsilu, da * silu

        dg, du = rowwise(nm + "swiglu", swiglu_bwd, T, [("row", dact), ("row", g), ("row", u)],
                         [("row", g.shape[1], BF16), ("row", g.shape[1], BF16)])
        grads[("ffn_w_gate", i)] = mmt_dw(nm + "mm_gate_w", dg, hn, N_DEV, BF16)
        grads[("ffn_w_up", i)] = mmt_dw(nm + "mm_up_w", du, hn, N_DEV, BF16)
        token = scatter_start(nm + "scatter_gate_up", [("ffn_w_gate", i), ("ffn_w_up", i)])
        dhn_g = mmt_dx(nm + "mm_gate_x", dg, wg, 0, F32, deps=(token,))
        dhn_u = mmt_dx(nm + "mm_up_x", du, wu, 0, F32)

        def norm_bwd_add2(a, w, dn1, dn2, dres):
            dxx, dw = _rms_bwd(a, w, dn1 + dn2)
            tot = dres + dxx
            return tot, tot, dw

        dh1, dh1_bf, d_ffn_w[i] = rowwise(nm + "norm_ffn", norm_bwd_add2, T,
                                          [("row", h1), ("full", w_ffn), ("row", dhn_g), ("row", dhn_u), ("row", dh2)],
                                          [("row", D, F32), ("row", D, BF16), ("acc", D)])
        if i % 2 == 0:
            z, glr, oraw, o = mixer_saved
            w_in_t, w_lr_t, w_oab, gu_full = mixer_w
            grads[("ab_w_out", 0)] = mm_tn(nm + "mm_out_w", o, dh1_bf, 1, BF16).reshape(N_DEV, D // N_DEV, D)
            token = scatter_start(nm + "scatter_out", [("ab_w_out", 0)])
            do = mm_nt(nm + "mm_out_x", dh1_bf, w_oab, 0, F32, deps=(token,))
            d_oraw, d_gates, d_hn_w = headnorm_bwd(nm + "headnorm", oraw, z, hn_w, do)
            d_rq, d_rk, d_rv = retention_bwd(nm + "ret", z, cosf, sins, d_oraw)
            d_gq, d_gk, d_gv, d_glr4, d_gu, d_gb = gla_bwd(nm + "gla", z, glr, gu_full, gb, d_oraw)
            dz = jnp.concatenate([d_rq, d_rk, d_rv, d_gates[:, :RET_V], d_gq, d_gk, d_gv, d_gates[:, RET_V:]], axis=1)
            (d_glr,) = rowwise(nm + "sum_lr", lambda *a: (a[0] + a[1] + a[2] + a[3],), T,
                               [("row", d_glr4[hh]) for hh in range(GLA_HEADS)], [("row", LANE, BF16)])
            dw_main = mm_tn(nm + "mm_in_w", xn, dz, 1, BF16)[0]
            dw_lr = mm_tn(nm + "mm_lr_w", xn, d_glr, 1, BF16)[0]
            dw_in = jnp.concatenate([dw_main, dw_lr[:, :GLA_GATE_RANK]], axis=1)
            grads[("ab_w_in", 0)] = dw_in.reshape(D, N_DEV, in_shard).transpose(1, 0, 2)
            token = scatter_start(nm + "scatter_in", [("ab_w_in", 0)])
            dxn_a = mmt_dx(nm + "mm_in_x", dz, w_in_t, 0, F32, n=OFF_LR, deps=(token,))
            dxn_b = mmt_dx(nm + "mm_lr_x", d_glr, w_lr_t, 0, F32)
        else:
            qkv, o, lse = mixer_saved
            w_qkv, w_oc = mixer_w
            grads[("c_w_out", 0)] = mm_tn(nm + "mm_out_w", o, dh1_bf, 1, BF16).reshape(N_DEV, D // N_DEV, D)
            do = mm_nt(nm + "mm_out_x", dh1_bf, w_oc, 0, BF16)
            dq, dk, dv = attn_bwd(nm + "attn", qkv, o, lse, do)
            dqkv = jnp.concatenate([dq, dk, dv], axis=1)
            grads[("c_w_qkv", 0)] = mm_tn(nm + "mm_qkv_w", xn, dqkv, N_DEV, BF16)
            token = scatter_start(nm + "scatter_attn", [("c_w_out", 0), ("c_w_qkv", 0)])
            dxn_a = mm_nt(nm + "mm_qkv_x", dqkv, w_qkv, 0, F32, deps=(token,))
            dxn_b = None
        if dxn_b is None:
            dx, _, d_attn_w[i] = rowwise(nm + "norm_attn", norm_bwd_add, T,
                                         [("row", xs_i), ("full", w_attn), ("row", dxn_a), ("row", dh1)],
                                         [("row", D, F32), ("row", D, BF16), ("acc", D)])
        else:
            dx, _, d_attn_w[i] = rowwise(nm + "norm_attn", norm_bwd_add2, T,
                                         [("row", xs_i), ("full", w_attn), ("row", dxn_a), ("row", dxn_b), ("row", dh1)],
                                         [("row", D, F32), ("row", D, BF16), ("acc", D)])

    small_names = ["attn_norm_w", "ffn_norm_w", "ple_norm_w", "final_norm_w", "ab_gla_gate_b", "ab_ret_norm_w",
                   "ab_gla_norm_w"]
    small_grads = [jnp.concatenate(d_attn_w, 0), jnp.concatenate(d_ffn_w, 0), jnp.concatenate(d_ple_w, 0), d_final_w[0],
                   d_gb, d_hn_w[:, :RET_V], d_hn_w[:, RET_V:]]
    small_w = [attn_norm_w, ffn_norm_w, ple_norm_w, final_norm_w, ab_gla_gate_b, ab_ret_norm_w, ab_gla_norm_w]
    small_m = [m_attn_norm_w, m_ffn_norm_w, m_ple_norm_w, m_final_norm_w, m_ab_gla_gate_b, m_ab_ret_norm_w, m_ab_gla_norm_w]
    small_v = [v_attn_norm_w, v_ffn_norm_w, v_ple_norm_w, v_final_norm_w, v_ab_gla_gate_b, v_ab_ret_norm_w, v_ab_gla_norm_w]
    sizes = [int(np.prod(a.shape)) for a in small_w]
    n_gu = GLA_GATE_RANK * GLA_QK
    n_small = _round_up(sum(sizes) + n_gu, LANE)
    pack = lambda parts: _pad_to(jnp.concatenate([a.reshape(-1) for a in parts]), 0, n_small)[None, :]
    small_part = pack(small_grads + [d_gu[:GLA_GATE_RANK]])

    big_w = dict(ab_w_in=(ab_w_in, m_ab_w_in, v_ab_w_in), ab_w_out=(ab_w_out, m_ab_w_out, v_ab_w_out),
                 c_w_qkv=(c_w_qkv, m_c_w_qkv, v_c_w_qkv), c_w_out=(c_w_out, m_c_w_out, v_c_w_out),
                 ffn_w_gate=(wg_t, tr_(m_ffn_w_gate), tr_(v_ffn_w_gate)),
                 ffn_w_up=(wu_t, tr_(m_ffn_w_up), tr_(v_ffn_w_up)),
                 ffn_w_down=(ffn_w_down, m_ffn_w_down, v_ffn_w_down), ple_w_proj=(ple_w_proj, m_ple_w_proj, v_ple_w_proj),
                 ple_w_gate=(ple_w_gate, m_ple_w_gate, v_ple_w_gate))
    results, last = {}, dx
    for gi, (keys, handle) in enumerate(scatters):
        arrived = exchange_wait(f"scatter_wait{gi}", "scatter", handle, deps=(last,))
        for (n, l), (own, land) in zip(keys, arrived):
            results[n] = adamw(f"adamw_{n}{l}", *big_w[n], l, land, own, prev=results.get(n))
            last = results[n][0]
    for n in ("ffn_w_gate", "ffn_w_up"):
        results[n] = [tr_(a) for a in results[n]]
    small_parts = gather_small("gather_small", small_part, deps=(last,)).reshape(N_DEV, n_small)

    gu_off = sum(sizes)
    own_cols = lambda a: lax.dynamic_slice_in_dim(a.reshape(GLA_GATE_RANK, GLA_QK), me * gu_cols, gu_cols, axis=1)
    small_res = adamw_small("adamw_small", pack(small_w + [jnp.zeros((n_gu,), F32)]),
                            pack(small_m + [jnp.zeros((n_gu,), F32)]), pack(small_v + [jnp.ones((n_gu,), F32)]),
                            small_parts)
    g_gu_full = small_res[0][0, gu_off:gu_off + n_gu]
    g_gu = own_cols(g_gu_full)[None]
    gu_res = adamw_small("adamw_gate_up", *[_pad_to(a.reshape(1, -1), 1, _round_up(a.size, LANE)) for a in
                                            (ab_gla_gate_up, m_ab_gla_gate_up, v_ab_gla_gate_up)],
                         jnp.concatenate([_pad_to(g_gu.reshape(1, -1), 1, _round_up(g_gu.size, LANE)),
                                          jnp.zeros((N_DEV - 1, _round_up(g_gu.size, LANE)), F32)], axis=0))
    for k in range(4):
        off = 0
        for n, a, sz in zip(small_names, small_w, sizes):
            results.setdefault(n, [None] * 4)[k] = small_res[k][0, off:off + sz].reshape(a.shape)
            off += sz
        results.setdefault("ab_gla_gate_up", [None] * 4)[k] = gu_res[k][0, :g_gu.size].reshape(ab_gla_gate_up.shape)

    order = ["attn_norm_w", "ffn_norm_w", "ple_norm_w", "final_norm_w", "ab_w_in", "ab_gla_gate_up", "ab_gla_gate_b",
             "ab_ret_norm_w", "ab_gla_norm_w", "ab_w_out", "c_w_qkv", "c_w_out", "ffn_w_gate", "ffn_w_up", "ffn_w_down",
             "ple_w_proj", "ple_w_gate"]
    return (loss, dx[None], *[results[n][0] for n in order], *[results[n][1] for n in order],
            *[results[n][2] for n in order], *[results[n][3] for n in order])
```

```python
import math

import numpy as np
import jax
import jax.numpy as jnp
from jax import lax
from jax.experimental import pallas as pl
from jax.experimental.pallas import tpu as pltpu

F32 = jnp.float32
BF16 = jnp.bfloat16
HIGHEST = lax.Precision.HIGHEST

N_DEV = 8
VMEM_LIMIT_BYTES = 48 * 1024 * 1024
LANE = 128
NORM_EPS = 1e-6

RET_HEADS, RET_DK, RET_DV = 4, 256, 256
RET_THETA_BASE = 10000.0
GLA_HEADS, GLA_DK, GLA_DV = 4, 128, 256
GLA_GATE_RANK = 16
GLA_GATE_NORM = 16.0
CHUNK = 64
ATT_HEADS = 16
DILATED_BRANCHES = ((128, 1), (512, 4), (2048, 16))
BLK = 256

ADAM_LR, ADAM_B1, ADAM_B2, ADAM_EPS, ADAM_WD, ADAM_STEP = 0.001, 0.9, 0.999, 1e-08, 0.01, 10

RET_QK = RET_HEADS * RET_DK
RET_V = RET_HEADS * RET_DV
GLA_QK = GLA_HEADS * GLA_DK
GLA_V = GLA_HEADS * GLA_DV
OFF_RQ, OFF_RK, OFF_RV, OFF_RG = 0, RET_QK, 2 * RET_QK, 2 * RET_QK + RET_V
OFF_GQ = OFF_RG + RET_V
OFF_GK = OFF_GQ + GLA_QK
OFF_GV = OFF_GK + GLA_QK
OFF_GG = OFF_GV + GLA_V
OFF_LR = OFF_GG + GLA_V


def _params(*sem):
    return pltpu.CompilerParams(dimension_semantics=sem or None, vmem_limit_bytes=VMEM_LIMIT_BYTES)


def _pick(n, cands):
    for c in cands:
        if n % c == 0:
            return c
    raise ValueError(f"no tile for {n} in {cands}")


_NN = (((1,), (0,)), ((), ()))
_NT = (((1,), (1,)), ((), ()))
_TN = (((0,), (0,)), ((), ()))
_ANY = pl.BlockSpec(memory_space=pl.ANY)
MAX_CONTRACT = 2048
_TILES = (1024, 768, 512, 256, 128)


def _mm_call(name, dims, grid, in_specs, out_spec, out_shape, args, deps=()):
    steps = grid[2]
    assert steps == 1 or out_shape.dtype == F32

    def body(a_ref, b_ref, *rest):
        o_ref = rest[len(deps)]
        part = lax.dot_general(a_ref[...].astype(BF16), b_ref[...].astype(BF16), dims, preferred_element_type=F32)
        if steps == 1:
            o_ref[...] = part.astype(o_ref.dtype)
        else:
            _accumulate(o_ref, part, pl.program_id(2) == 0)

    return pl.pallas_call(
        body, name=name, grid=grid, in_specs=list(in_specs) + [_ANY] * len(deps), out_specs=out_spec,
        out_shape=out_shape, compiler_params=_params("parallel", "parallel", "arbitrary"))(*args, *deps)


def mm_nn(name, a, w, l, out_dtype, deps=()):
    _, J, K, n = w.shape
    M = a.shape[0]
    tm, tn, tk = _pick(M, _TILES), _pick(n, _TILES), _pick(K, (MAX_CONTRACT,) + _TILES)
    nt = n // tn
    return _mm_call(
        name, _NN, (M // tm, J * nt, K // tk),
        [pl.BlockSpec((tm, tk), lambda i, j, k: (i, k)),
         pl.BlockSpec((None, None, tk, tn), lambda i, j, k: (l, j // nt, k, j % nt))],
        pl.BlockSpec((tm, tn), lambda i, j, k: (i, j)),
        jax.ShapeDtypeStruct((M, J * n), out_dtype), (a, w), deps)


def mm_nt(name, a, w, l, out_dtype, deps=()):
    _, J, K, n = w.shape
    M = a.shape[0]
    tm, tq, tc = _pick(M, _TILES), _pick(K, _TILES), _pick(n, (MAX_CONTRACT,) + _TILES)
    nc = n // tc
    return _mm_call(
        name, _NT, (M // tm, K // tq, J * nc),
        [pl.BlockSpec((tm, tc), lambda i, q, c: (i, c)),
         pl.BlockSpec((None, None, tq, tc), lambda i, q, c: (l, c // nc, q, c % nc))],
        pl.BlockSpec((tm, tq), lambda i, q, c: (i, q)),
        jax.ShapeDtypeStruct((M, K), out_dtype), (a, w), deps)


def mm_tn(name, x, dy, J, out_dtype, deps=()):
    M, K = x.shape
    n = dy.shape[1] // J
    tp, tn = _pick(K, _TILES), _pick(n, _TILES)
    nt = n // tn
    assert M <= MAX_CONTRACT
    return _mm_call(
        name, _TN, (K // tp, J * nt, 1),
        [pl.BlockSpec((M, tp), lambda i, j, r: (0, i)),
         pl.BlockSpec((M, tn), lambda i, j, r: (0, j))],
        pl.BlockSpec((None, tp, tn), lambda i, j, r: (j // nt, i, j % nt)),
        jax.ShapeDtypeStruct((J, K, n), out_dtype), (x, dy), deps)


def mmt_fwd(name, a, wt, l, out_dtype, n=None, deps=()):
    _, J, rows, K = wt.shape
    n = rows if n is None else n
    M = a.shape[0]
    tm, tn = _pick(M, _TILES), _pick(n, _TILES)
    nt = n // tn
    assert K <= MAX_CONTRACT
    return _mm_call(
        name, _NT, (M // tm, J * nt, 1),
        [pl.BlockSpec((tm, K), lambda i, j, k: (i, 0)),
         pl.BlockSpec((None, None, tn, K), lambda i, j, k: (l, j // nt, j % nt, 0))],
        pl.BlockSpec((tm, tn), lambda i, j, k: (i, j)),
        jax.ShapeDtypeStruct((M, J * n), out_dtype), (a, wt), deps)


def mmt_dx(name, dy, wt, l, out_dtype, n=None, deps=()):
    _, J, rows, K = wt.shape
    n = rows if n is None else n
    M = dy.shape[0]
    tm, tq, tc = _pick(M, _TILES), _pick(K, _TILES), _pick(n, _TILES)
    nc = n // tc
    return _mm_call(
        name, _NN, (M // tm, K // tq, J * nc),
        [pl.BlockSpec((tm, tc), lambda i, q, c: (i, c)),
         pl.BlockSpec((None, None, tc, tq), lambda i, q, c: (l, c // nc, c % nc, q))],
        pl.BlockSpec((tm, tq), lambda i, q, c: (i, q)),
        jax.ShapeDtypeStruct((M, K), out_dtype), (dy, wt), deps)


def mmt_dw(name, dy, x, J, out_dtype, deps=()):
    M, K = x.shape
    n = dy.shape[1] // J
    tn, tp = _pick(n, _TILES), _pick(K, _TILES)
    nt = n // tn
    assert M <= MAX_CONTRACT
    return _mm_call(
        name, _TN, (J * nt, K // tp, 1),
        [pl.BlockSpec((M, tn), lambda j, i, r: (0, j)),
         pl.BlockSpec((M, tp), lambda j, i, r: (0, i))],
        pl.BlockSpec((None, tn, tp), lambda j, i, r: (j // nt, j % nt, i)),
        jax.ShapeDtypeStruct((J, n, K), out_dtype), (dy, x), deps)


def ffn_gate_up(name, a, wg, wu):
    _, J, n, K = wg.shape
    M = a.shape[0]
    tm, tn = _pick(M, _TILES), _pick(n, _TILES)
    nt = n // tn
    assert K <= MAX_CONTRACT

    def body(a_ref, wg_ref, wu_ref, g_ref, u_ref, act_ref):
        x = a_ref[...]
        g = lax.dot_general(x, wg_ref[...], _NT, preferred_element_type=F32)
        u = lax.dot_general(x, wu_ref[...], _NT, preferred_element_type=F32)
        g_ref[...] = g.astype(g_ref.dtype)
        u_ref[...] = u.astype(u_ref.dtype)
        act_ref[...] = (_silu_and_grad(g)[0] * u).astype(act_ref.dtype)

    w_spec = pl.BlockSpec((None, None, tn, K), lambda i, j: (0, j // nt, j % nt, 0))
    out = pl.BlockSpec((tm, tn), lambda i, j: (i, j))
    return pl.pallas_call(
        body, name=name, grid=(M // tm, J * nt),
        in_specs=[pl.BlockSpec((tm, K), lambda i, j: (i, 0)), w_spec, w_spec],
        out_specs=[out] * 3, out_shape=[jax.ShapeDtypeStruct((M, J * n), BF16)] * 3,
        compiler_params=_params("parallel", "parallel"))(a, wg, wu)


def ffn_down_bwd(name, dy, wd, g, u, deps=()):
    _, _, K, n = wd.shape
    M = dy.shape[0]
    tm, tq = _pick(M, _TILES), _pick(K, _TILES)
    assert n <= MAX_CONTRACT

    def body(dy_ref, w_ref, g_ref, u_ref, *rest):
        dg_ref, du_ref = rest[len(deps):]
        dact = lax.dot_general(dy_ref[...], w_ref[...], _NT, preferred_element_type=F32)
        silu, dsilu = _silu_and_grad(g_ref[...].astype(F32))
        dg_ref[...] = (dact * u_ref[...].astype(F32) * dsilu).astype(dg_ref.dtype)
        du_ref[...] = (dact * silu).astype(du_ref.dtype)

    blk = pl.BlockSpec((tm, tq), lambda i, q: (i, q))
    return pl.pallas_call(
        body, name=name, grid=(M // tm, K // tq),
        in_specs=[pl.BlockSpec((tm, n), lambda i, q: (i, 0)),
                  pl.BlockSpec((None, None, tq, n), lambda i, q: (0, 0, q, 0)), blk, blk] + [_ANY] * len(deps),
        out_specs=[blk, blk], out_shape=[jax.ShapeDtypeStruct((M, K), BF16)] * 2,
        compiler_params=_params("parallel", "parallel"))(dy, wd, g, u, *deps)


def rowwise(name, fn, rows, ins, outs, tr=256, deps=()):
    widest = max([s[1].shape[1] if s[0] != "col" else s[3] for s in ins] + [s[1] for s in outs])
    tr = min(tr if widest <= 2048 else tr // 2, rows)
    in_specs, args = [], []
    for spec in ins:
        kind, a = spec[0], spec[1]
        if kind == "row":
            in_specs.append(pl.BlockSpec((tr, a.shape[1]), lambda i: (i, 0)))
        elif kind == "col":
            cb, width = spec[2], spec[3]
            in_specs.append(pl.BlockSpec((tr, width), lambda i, cb=cb: (i, cb)))
        else:
            in_specs.append(pl.BlockSpec(a.shape, lambda i: (0, 0)))
        args.append(a)
    out_specs, out_shapes = [], []
    for spec in outs:
        if spec[0] == "row":
            out_specs.append(pl.BlockSpec((tr, spec[1]), lambda i: (i, 0)))
            out_shapes.append(jax.ShapeDtypeStruct((rows, spec[1]), spec[2]))
        else:
            out_specs.append(pl.BlockSpec((1, spec[1]), lambda i: (0, 0)))
            out_shapes.append(jax.ShapeDtypeStruct((1, spec[1]), F32))
    n_in = len(ins)

    def body(*refs):
        vals = fn(*[r[...] for r in refs[:n_in]])
        first = pl.program_id(0) == 0
        for r, v, spec in zip(refs[n_in + len(deps):], vals, outs):
            if spec[0] == "row":
                r[...] = v.astype(r.dtype)
            else:
                _accumulate(r, v, first)

    return pl.pallas_call(body, name=name, grid=(rows // tr,), in_specs=in_specs + [_ANY] * len(deps),
                          out_specs=out_specs, out_shape=out_shapes,
                          compiler_params=_params("arbitrary"))(*args, *deps)


def _accumulate(ref, v, first):
    @pl.when(first)
    def _():
        ref[...] = v

    @pl.when(jnp.logical_not(first))
    def _():
        ref[...] += v


def _rms(x, w):
    r = lax.rsqrt(jnp.mean(x * x, axis=-1, keepdims=True) + NORM_EPS)
    return x * r * w


def _rms_bwd(x, w, dy):
    r = lax.rsqrt(jnp.mean(x * x, axis=-1, keepdims=True) + NORM_EPS)
    g = dy * w
    dx = r * (g - x * (r * r) * jnp.mean(g * x, axis=-1, keepdims=True))
    dw = jnp.sum(dy * x * r, axis=0, keepdims=True)
    return dx, dw


def _sigmoid(x):
    return 1.0 / (1.0 + jnp.exp(-x))


def _silu_and_grad(g):
    s = _sigmoid(g)
    return g * s, s * (1.0 + g * (1.0 - s))


def _swap_pairs(x):
    n = x.shape[-1]
    lane = lax.broadcasted_iota(jnp.int32, x.shape, x.ndim - 1)
    return jnp.where((lane & 1) == 0, pltpu.roll(x, n - 1, x.ndim - 1), pltpu.roll(x, 1, x.ndim - 1))


def _rot(x, cosf, sins):
    return x * cosf + _swap_pairs(x) * sins


def _unrot(d, cosf, sins):
    return d * cosf + _swap_pairs(d * sins)


def _ret_log_gamma(h):
    vals = [math.log1p(-2.0 ** (-5.0 - i)) for i in range(RET_HEADS)]
    out = jnp.float32(vals[RET_HEADS - 1])
    for i in range(RET_HEADS - 2, -1, -1):
        out = jnp.where(h == i, jnp.float32(vals[i]), out)
    return out


def _fill_decays(dec_ref, lg):
    ri = lax.broadcasted_iota(jnp.int32, (BLK, BLK), 0)
    ci = lax.broadcasted_iota(jnp.int32, (BLK, BLK), 1)
    for d in range(dec_ref.shape[0]):
        dt = d * BLK + ri - ci
        dec_ref[d] = jnp.where(dt >= 0, jnp.exp(jnp.maximum(dt, 0).astype(F32) * lg), 0.0)


def _decay_row(dec_ref, qi):
    return jnp.concatenate([dec_ref[qi - kb] for kb in range(qi + 1)], axis=1)


def _once(block_shape, index_map):
    return pl.BlockSpec(block_shape, index_map, pipeline_mode=pl.Buffered(1))


def _dot(a, b):
    return jnp.dot(a.astype(BF16), b.astype(BF16), preferred_element_type=F32)


def _dot_nt(a, b):
    return lax.dot_general(a.astype(BF16), b.astype(BF16), _NT, preferred_element_type=F32)


def _dot_tn(a, b):
    return lax.dot_general(a.astype(BF16), b.astype(BF16), _TN, preferred_element_type=F32)


def retention_fwd(name, z, cosf, sins, width_out):
    T = z.shape[0]
    nq = T // BLK
    scale = RET_DK ** -0.5

    def body(q_ref, k_ref, v_ref, cos_ref, sin_ref, o_ref, krot, vb, dec_ref):
        _fill_decays(dec_ref, _ret_log_gamma(pl.program_id(0)))
        krot[...] = (_rot(k_ref[...], cos_ref[...], sin_ref[...]) * scale).astype(BF16)
        vb[...] = v_ref[...].astype(BF16)
        for qi in range(nq):
            rows, n = slice(qi * BLK, (qi + 1) * BLK), (qi + 1) * BLK
            q = _rot(q_ref[rows, :], cos_ref[rows, :], sin_ref[rows, :])
            s = _dot_nt(q, krot[0:n, :]) * _decay_row(dec_ref, qi)
            o_ref[rows, :] = _dot(s, vb[0:n, :])

    return pl.pallas_call(
        body, name=name, grid=(RET_HEADS,),
        in_specs=[pl.BlockSpec((T, RET_DK), lambda h: (0, OFF_RQ // RET_DK + h)),
                  pl.BlockSpec((T, RET_DK), lambda h: (0, OFF_RK // RET_DK + h)),
                  pl.BlockSpec((T, RET_DV), lambda h: (0, OFF_RV // RET_DV + h)),
                  _once((T, RET_DK), lambda h: (0, 0)), _once((T, RET_DK), lambda h: (0, 0))],
        out_specs=pl.BlockSpec((T, RET_DV), lambda h: (0, h)),
        out_shape=jax.ShapeDtypeStruct((T, width_out), F32),
        scratch_shapes=[pltpu.VMEM((T, RET_DK), BF16), pltpu.VMEM((T, RET_DV), BF16),
                        pltpu.VMEM((nq, BLK, BLK), F32)],
        compiler_params=_params("arbitrary"))(z, z, z, cosf, sins)


def retention_bwd(name, z, cosf, sins, do):
    T = z.shape[0]
    nq = T // BLK
    scale = RET_DK ** -0.5

    def body(q_ref, k_ref, v_ref, cos_ref, sin_ref, do_ref, dq_ref, dk_ref, dv_ref, krot, vb, dk_acc, dv_acc, dec_ref):
        _fill_decays(dec_ref, _ret_log_gamma(pl.program_id(0)))
        krot[...] = (_rot(k_ref[...], cos_ref[...], sin_ref[...]) * scale).astype(BF16)
        vb[...] = v_ref[...].astype(BF16)
        dk_acc[...] = jnp.zeros_like(dk_acc)
        dv_acc[...] = jnp.zeros_like(dv_acc)
        for qi in range(nq):
            rows, n = slice(qi * BLK, (qi + 1) * BLK), (qi + 1) * BLK
            cos_q, sin_q = cos_ref[rows, :], sin_ref[rows, :]
            q = _rot(q_ref[rows, :], cos_q, sin_q).astype(BF16)
            dout = do_ref[rows, :].astype(BF16)
            kk, vv, dec = krot[0:n, :], vb[0:n, :], _decay_row(dec_ref, qi)
            p = (_dot_nt(q, kk) * dec).astype(BF16)
            ds = (_dot_nt(dout, vv) * dec).astype(BF16)
            dq_ref[rows, :] = _unrot(_dot(ds, kk), cos_q, sin_q).astype(dq_ref.dtype)
            dk_acc[0:n, :] += _dot_tn(ds, q)
            dv_acc[0:n, :] += _dot_tn(p, dout)
        dk_ref[...] = (_unrot(dk_acc[...], cos_ref[...], sin_ref[...]) * scale).astype(dk_ref.dtype)
        dv_ref[...] = dv_acc[...].astype(dv_ref.dtype)

    head = lambda h: (0, h)
    return pl.pallas_call(
        body, name=name, grid=(RET_HEADS,),
        in_specs=[pl.BlockSpec((T, RET_DK), lambda h: (0, OFF_RQ // RET_DK + h)),
                  pl.BlockSpec((T, RET_DK), lambda h: (0, OFF_RK // RET_DK + h)),
                  pl.BlockSpec((T, RET_DV), lambda h: (0, OFF_RV // RET_DV + h)),
                  _once((T, RET_DK), lambda h: (0, 0)), _once((T, RET_DK), lambda h: (0, 0)),
                  pl.BlockSpec((T, RET_DV), head)],
        out_specs=[pl.BlockSpec((T, RET_DK), head), pl.BlockSpec((T, RET_DK), head), pl.BlockSpec((T, RET_DV), head)],
        out_shape=[jax.ShapeDtypeStruct((T, RET_QK), BF16), jax.ShapeDtypeStruct((T, RET_QK), BF16),
                   jax.ShapeDtypeStruct((T, RET_V), BF16)],
        scratch_shapes=[pltpu.VMEM((T, RET_DK), BF16), pltpu.VMEM((T, RET_DV), BF16),
                        pltpu.VMEM((T, RET_DK), F32), pltpu.VMEM((T, RET_DV), F32),
                        pltpu.VMEM((nq, BLK, BLK), F32)],
        compiler_params=_params("arbitrary"))(z, z, z, cosf, sins, do)


GLA_PAIR = 2


def _gla_chunk(q_ref, k_ref, v_ref, glr_ref, gu, gb, rows, hh, trilf):
    ck = slice(hh * GLA_DK, (hh + 1) * GLA_DK)
    zg = _dot(glr_ref[rows, :], gu[:, ck]) + gb[:, ck]
    la = (jnp.minimum(zg, 0.0) - jnp.log(1.0 + jnp.exp(-jnp.abs(zg)))) * (1.0 / GLA_GATE_NORM)
    cum = jnp.dot(trilf, la, precision=HIGHEST, preferred_element_type=F32)
    last = jnp.sum(la, axis=0, keepdims=True)
    ecum = jnp.exp(cum)
    k = k_ref[rows, ck]
    qt = q_ref[rows, ck] * (GLA_DK ** -0.5) * ecum
    kt = k * jnp.exp(-cum)
    kh = k * jnp.exp(last - cum)
    return zg, cum, last, ecum, qt, kt, kh, v_ref[rows, hh * GLA_DV:(hh + 1) * GLA_DV].astype(BF16)


def _state_decay(last):
    e = jnp.exp(jnp.broadcast_to(last, (GLA_DK, GLA_DK)).T)
    return jnp.concatenate([e] * (GLA_DV // GLA_DK), axis=1)


def _gla_specs(T):
    wk, wv = GLA_PAIR * GLA_DK, GLA_PAIR * GLA_DV
    return [_once((T, wk), lambda h: (0, OFF_GQ // wk + h)),
            _once((T, wk), lambda h: (0, OFF_GK // wk + h)),
            _once((T, wv), lambda h: (0, OFF_GV // wv + h)),
            _once((T, LANE), lambda h: (0, 0)),
            pl.BlockSpec((LANE, wk), lambda h: (0, h)),
            pl.BlockSpec((1, wk), lambda h: (0, h))]


def gla_fwd(name, z, glr, gu, gb, o_prev):
    T = z.shape[0]
    nc = T // CHUNK
    wv = GLA_PAIR * GLA_DV

    def body(q_ref, k_ref, v_ref, glr_ref, gu_ref, gb_ref, prev_ref, o_ref, S):
        del prev_ref
        gu_b, gb_v = gu_ref[...].astype(BF16), gb_ref[...]
        ri = lax.broadcasted_iota(jnp.int32, (CHUNK, CHUNK), 0)
        ci = lax.broadcasted_iota(jnp.int32, (CHUNK, CHUNK), 1)
        tril = ri >= ci
        trilf = tril.astype(F32)
        S[...] = jnp.zeros_like(S)

        def step(c, carry):
            rows = pl.ds(pl.multiple_of(c * CHUNK, CHUNK), CHUNK)
            for hh in range(GLA_PAIR):
                _, _, last, _, qt, kt, kh, v = _gla_chunk(q_ref, k_ref, v_ref, glr_ref, gu_b, gb_v, rows, hh, trilf)
                a = jnp.where(tril, _dot_nt(qt, kt), 0.0)
                s_prev = S[hh]
                o_ref[rows, hh * GLA_DV:(hh + 1) * GLA_DV] = _dot(a, v) + _dot(qt, s_prev)
                S[hh] = s_prev * _state_decay(last) + _dot_tn(kh, v)
            return carry

        lax.fori_loop(0, nc, step, 0)

    n_in = 6
    return pl.pallas_call(
        body, name=name, grid=(GLA_HEADS // GLA_PAIR,),
        in_specs=_gla_specs(T) + [pl.BlockSpec(memory_space=pl.ANY)],
        out_specs=pl.BlockSpec((T, wv), lambda h: (0, RET_V // wv + h)),
        out_shape=jax.ShapeDtypeStruct(o_prev.shape, F32),
        scratch_shapes=[pltpu.VMEM((GLA_PAIR, GLA_DK, GLA_DV), F32)],
        input_output_aliases={n_in: 0},
        compiler_params=_params("arbitrary"))(z, z, z, glr, gu, gb, o_prev)


def gla_bwd(name, z, glr, gu, gb, do):
    T = z.shape[0]
    nc = T // CHUNK

    def body(q_ref, k_ref, v_ref, glr_ref, gu_ref, gb_ref, do_ref,
             dq_ref, dk_ref, dv_ref, dglr_ref, dgu_ref, dgb_ref, s_all, dS):
        gu_b, gb_v = gu_ref[...].astype(BF16), gb_ref[...]
        ri = lax.broadcasted_iota(jnp.int32, (CHUNK, CHUNK), 0)
        ci = lax.broadcasted_iota(jnp.int32, (CHUNK, CHUNK), 1)
        tril = ri >= ci
        trilf = tril.astype(F32)
        triuf = (ri <= ci).astype(F32)
        last_row = lax.broadcasted_iota(jnp.int32, (CHUNK, GLA_DK), 0) == CHUNK - 1
        ones8 = jnp.ones((8, GLA_DV), F32)

        def fstep(c, carry):
            rows = pl.ds(pl.multiple_of(c * CHUNK, CHUNK), CHUNK)
            for hh in range(GLA_PAIR):
                s_prev = dS[hh]
                s_all[hh, c] = s_prev
                _, _, last, _, _, _, kh, v = _gla_chunk(q_ref, k_ref, v_ref, glr_ref, gu_b, gb_v, rows, hh, trilf)
                dS[hh] = s_prev * _state_decay(last) + _dot_tn(kh, v)
            return carry

        dS[...] = jnp.zeros_like(dS)
        lax.fori_loop(0, nc, fstep, 0)
        dS[...] = jnp.zeros_like(dS)
        dgu_ref[...] = jnp.zeros_like(dgu_ref)
        dgb_ref[...] = jnp.zeros_like(dgb_ref)

        def bstep(i, carry):
            c = nc - 1 - i
            rows = pl.ds(pl.multiple_of(c * CHUNK, CHUNK), CHUNK)
            glr_c = glr_ref[rows, :]
            for hh in range(GLA_PAIR):
                ck, cv = slice(hh * GLA_DK, (hh + 1) * GLA_DK), slice(hh * GLA_DV, (hh + 1) * GLA_DV)
                zg, cum, last, ecum, qt, kt, kh, v = _gla_chunk(q_ref, k_ref, v_ref, glr_ref, gu_b, gb_v, rows, hh, trilf)
                a = jnp.where(tril, _dot_nt(qt, kt), 0.0)
                s_prev, ds_new = s_all[hh, c], dS[hh]
                dout = do_ref[rows, cv].astype(BF16)
                dv_ref[rows, cv] = (_dot_tn(a, dout) + _dot(kh, ds_new)).astype(dv_ref.dtype)
                da = jnp.where(tril, _dot_nt(dout, v), 0.0)
                dqt = _dot(da, kt) + _dot_nt(dout, s_prev)
                dkt = _dot_tn(da, qt)
                dkh = _dot_nt(v, ds_new)
                dS[hh] = ds_new * _state_decay(last) + _dot_tn(qt, dout)
                dq_ref[rows, ck] = (dqt * ecum * (GLA_DK ** -0.5)).astype(dq_ref.dtype)
                dk_ref[rows, ck] = (dkt * jnp.exp(-cum) + dkh * jnp.exp(last - cum)).astype(dk_ref.dtype)
                dkh_kh = dkh * kh
                dcum = dqt * qt - dkt * kt - dkh_kh
                rs = lax.dot_general(ones8, ds_new * s_prev, _NT, precision=HIGHEST, preferred_element_type=F32)
                dlast = (jnp.sum(dkh_kh, axis=0, keepdims=True)
                         + jnp.exp(last) * (jnp.sum(rs, axis=0, keepdims=True) * 0.125))
                dcum = dcum + jnp.where(last_row, dlast, 0.0)
                dla = jnp.dot(triuf, dcum, precision=HIGHEST, preferred_element_type=F32)
                dzg = dla * (1.0 / GLA_GATE_NORM) * _sigmoid(-zg)
                dglr_ref[hh, rows, :] = _dot_nt(dzg, gu_b[:, ck])
                dgu_ref[:, ck] += _dot_tn(glr_c, dzg)
                dgb_ref[:, ck] += jnp.sum(dzg, axis=0, keepdims=True)
            return carry

        lax.fori_loop(0, nc, bstep, 0)

    wk, wv = GLA_PAIR * GLA_DK, GLA_PAIR * GLA_DV
    return pl.pallas_call(
        body, name=name, grid=(GLA_HEADS // GLA_PAIR,),
        in_specs=_gla_specs(T) + [_once((T, wv), lambda h: (0, RET_V // wv + h))],
        out_specs=[pl.BlockSpec((T, wk), lambda h: (0, h)), pl.BlockSpec((T, wk), lambda h: (0, h)),
                   pl.BlockSpec((T, wv), lambda h: (0, h)),
                   pl.BlockSpec((GLA_PAIR, T, LANE), lambda h: (h, 0, 0)),
                   pl.BlockSpec((LANE, wk), lambda h: (0, h)), pl.BlockSpec((1, wk), lambda h: (0, h))],
        out_shape=[jax.ShapeDtypeStruct((T, GLA_QK), BF16), jax.ShapeDtypeStruct((T, GLA_QK), BF16),
                   jax.ShapeDtypeStruct((T, GLA_V), BF16), jax.ShapeDtypeStruct((GLA_HEADS, T, LANE), F32),
                   jax.ShapeDtypeStruct((LANE, GLA_QK), F32), jax.ShapeDtypeStruct((1, GLA_QK), F32)],
        scratch_shapes=[pltpu.VMEM((GLA_PAIR, nc, GLA_DK, GLA_DV), F32), pltpu.VMEM((GLA_PAIR, GLA_DK, GLA_DV), F32)],
        compiler_params=_params("arbitrary"))(z, z, z, glr, gu, gb, do)


HN_HEADS = RET_HEADS + GLA_HEADS
HN_W = RET_DV


def _gate_col(h):
    return jnp.where(h < RET_HEADS, OFF_RG // HN_W + h, OFF_GG // HN_W + h - RET_HEADS)


def headnorm_fwd(name, oraw, z, w, tr=256):
    T = oraw.shape[0]

    def body(o_ref, g_ref, w_ref, y_ref):
        y_ref[...] = (_rms(o_ref[...], w_ref[...]) * _silu_and_grad(g_ref[...])[0]).astype(y_ref.dtype)

    return pl.pallas_call(
        body, name=name, grid=(HN_HEADS, T // tr),
        in_specs=[pl.BlockSpec((tr, HN_W), lambda h, i: (i, h)),
                  pl.BlockSpec((tr, HN_W), lambda h, i: (i, _gate_col(h))),
                  pl.BlockSpec((1, HN_W), lambda h, i: (0, h))],
        out_specs=pl.BlockSpec((tr, HN_W), lambda h, i: (i, h)),
        out_shape=jax.ShapeDtypeStruct((T, HN_HEADS * HN_W), BF16),
        compiler_params=_params("arbitrary", "arbitrary"))(oraw, z, w)


def headnorm_bwd(name, oraw, z, w, dy, tr=256):
    T = oraw.shape[0]

    def body(o_ref, g_ref, w_ref, dy_ref, do_ref, dg_ref, dw_ref):
        o, wv, dyv = o_ref[...], w_ref[...], dy_ref[...].astype(F32)
        silu, dsilu = _silu_and_grad(g_ref[...])
        n = _rms(o, wv)
        dg_ref[...] = (dyv * n * dsilu).astype(dg_ref.dtype)
        dx, dw = _rms_bwd(o, wv, dyv * silu)
        do_ref[...] = dx
        _accumulate(dw_ref, dw, pl.program_id(1) == 0)

    blk = pl.BlockSpec((tr, HN_W), lambda h, i: (i, h))
    return pl.pallas_call(
        body, name=name, grid=(HN_HEADS, T // tr),
        in_specs=[blk, pl.BlockSpec((tr, HN_W), lambda h, i: (i, _gate_col(h))),
                  pl.BlockSpec((1, HN_W), lambda h, i: (0, h)), blk],
        out_specs=[blk, blk, pl.BlockSpec((1, HN_W), lambda h, i: (0, h))],
        out_shape=[jax.ShapeDtypeStruct((T, HN_HEADS * HN_W), F32),
                   jax.ShapeDtypeStruct((T, HN_HEADS * HN_W), BF16),
                   jax.ShapeDtypeStruct((1, HN_HEADS * HN_W), F32)],
        compiler_params=_params("arbitrary", "arbitrary"))(oraw, z, w, dy)


N_MASKS = 4


def _check_mask_classes(T):
    for window, dilation in DILATED_BRANCHES[:-1]:
        assert window < (N_MASKS - 1) * BLK - (BLK - 1) and BLK % dilation == 0
    assert DILATED_BRANCHES[-1][0] >= T and BLK % DILATED_BRANCHES[-1][1] == 0


def _fill_masks(mult_ref, bias_ref):
    ri = lax.broadcasted_iota(jnp.int32, (BLK, BLK), 0)
    ci = lax.broadcasted_iota(jnp.int32, (BLK, BLK), 1)
    for d in range(N_MASKS):
        dt = d * BLK + ri - ci
        mult = jnp.zeros((BLK, BLK), F32)
        for window, dilation in DILATED_BRANCHES:
            hit = (dt >= 0) & (dt <= window) & ((dt & (dilation - 1)) == 0)
            mult = mult + hit.astype(F32)
        mult_ref[d] = mult
        bias_ref[d] = jnp.where(mult > 0, 0.0, -1e30)


def _mask_row(ref, qi):
    return jnp.concatenate([ref[min(qi - kb, N_MASKS - 1)] for kb in range(qi + 1)], axis=1)


def attn_fwd(name, qkv):
    T = qkv.shape[0]
    D = qkv.shape[1] // 3
    dh = D // ATT_HEADS
    nq = T // BLK
    scale = dh ** -0.5

    _check_mask_classes(T)

    def body(q_ref, k_ref, v_ref, o_ref, lse_ref, mult_ref, bias_ref):
        @pl.when(pl.program_id(0) == 0)
        def _():
            _fill_masks(mult_ref, bias_ref)

        for qi in range(nq):
            rows, n = slice(qi * BLK, (qi + 1) * BLK), (qi + 1) * BLK
            s = (_dot_nt(q_ref[rows, :], k_ref[0:n, :]) * scale
                 + _mask_row(bias_ref, qi))
            m = jnp.max(s, axis=-1, keepdims=True)
            p = _mask_row(mult_ref, qi) * jnp.exp(s - m)
            l = jnp.sum(p, axis=-1, keepdims=True)
            o_ref[rows, :] = (_dot(p, v_ref[0:n, :]) / l).astype(o_ref.dtype)
            lse_ref[rows, :] = jnp.broadcast_to(m + jnp.log(l), (BLK, LANE))

    return pl.pallas_call(
        body, name=name, grid=(ATT_HEADS,),
        in_specs=[pl.BlockSpec((T, dh), lambda h: (0, h)),
                  pl.BlockSpec((T, dh), lambda h: (0, ATT_HEADS + h)),
                  pl.BlockSpec((T, dh), lambda h: (0, 2 * ATT_HEADS + h))],
        out_specs=[pl.BlockSpec((T, dh), lambda h: (0, h)),
                   pl.BlockSpec((None, T, LANE), lambda h: (h, 0, 0))],
        out_shape=[jax.ShapeDtypeStruct((T, D), BF16), jax.ShapeDtypeStruct((ATT_HEADS, T, LANE), F32)],
        scratch_shapes=[pltpu.VMEM((N_MASKS, BLK, BLK), F32), pltpu.VMEM((N_MASKS, BLK, BLK), F32)],
        compiler_params=_params("arbitrary"))(qkv, qkv, qkv)


def attn_bwd(name, qkv, o, lse, do):
    T = qkv.shape[0]
    D = qkv.shape[1] // 3
    dh = D // ATT_HEADS
    nq = T // BLK
    scale = dh ** -0.5

    _check_mask_classes(T)

    def body(q_ref, k_ref, v_ref, o_ref, lse_ref, do_ref, dq_ref, dk_ref, dv_ref, dk_acc, dv_acc, mult_ref, bias_ref):
        @pl.when(pl.program_id(0) == 0)
        def _():
            _fill_masks(mult_ref, bias_ref)

        dk_acc[...] = jnp.zeros_like(dk_acc)
        dv_acc[...] = jnp.zeros_like(dv_acc)
        for qi in range(nq):
            rows, n = slice(qi * BLK, (qi + 1) * BLK), (qi + 1) * BLK
            q, dout = q_ref[rows, :], do_ref[rows, :]
            kk, vv = k_ref[0:n, :], v_ref[0:n, :]
            delta = jnp.sum(dout.astype(F32) * o_ref[rows, :].astype(F32), axis=-1, keepdims=True)
            lse = jnp.max(lse_ref[rows, :], axis=-1, keepdims=True)
            s = _dot_nt(q, kk) * scale + _mask_row(bias_ref, qi)
            p = _mask_row(mult_ref, qi) * jnp.exp(s - lse)
            ds = (p * (_dot_nt(dout, vv) - delta) * scale).astype(BF16)
            dq_ref[rows, :] = _dot(ds, kk).astype(dq_ref.dtype)
            dk_acc[0:n, :] += _dot_tn(ds, q)
            dv_acc[0:n, :] += _dot_tn(p, dout)
        dk_ref[...] = dk_acc[...].astype(dk_ref.dtype)
        dv_ref[...] = dv_acc[...].astype(dv_ref.dtype)

    full = pl.BlockSpec((T, dh), lambda h: (0, h))
    return pl.pallas_call(
        body, name=name, grid=(ATT_HEADS,),
        in_specs=[full, pl.BlockSpec((T, dh), lambda h: (0, ATT_HEADS + h)),
                  pl.BlockSpec((T, dh), lambda h: (0, 2 * ATT_HEADS + h)),
                  full, pl.BlockSpec((None, T, LANE), lambda h: (h, 0, 0)), full],
        out_specs=[full, full, full],
        out_shape=[jax.ShapeDtypeStruct((T, D), BF16)] * 3,
        scratch_shapes=[pltpu.VMEM((T, dh), F32), pltpu.VMEM((T, dh), F32),
                        pltpu.VMEM((N_MASKS, BLK, BLK), F32), pltpu.VMEM((N_MASKS, BLK, BLK), F32)],
        compiler_params=_params("arbitrary"))(qkv, qkv, qkv, o, lse, do)


def _mesh_pos():
    mx, my, mc = lax.axis_index("x"), lax.axis_index("y"), lax.axis_index("c")
    return mx, my, mc, 4 * mx + 2 * my + mc


def _peer(k, mx, my, mc):
    px, py, pc = mx ^ (k >> 2), my ^ ((k >> 1) & 1), mc ^ (k & 1)
    return (px, py, pc), 4 * px + 2 * py + pc


_SIBLING = 1
_OTHER_CHIPS = (4, 2, 6)
_PLANS = {"gather": (2, tuple(range(1, N_DEV))), "scatter": (2, tuple(range(1, N_DEV))),
          "to_chips": (2, (_SIBLING,) + _OTHER_CHIPS), "to_sibling": (1, _OTHER_CHIPS)}


def _copies(kind, items, send_sems, recv_sems):
    mx, my, mc, me = _mesh_pos()
    peers = _PLANS[kind][1]
    out = []
    for i, refs in enumerate(items):
        for j, k in enumerate(peers):
            peer, to = _peer(k, mx, my, mc)
            if kind == "to_sibling":
                src, dst, peer = refs[0].at[to], refs[0].at[to], _peer(_SIBLING, mx, my, mc)[0]
            elif kind == "scatter":
                src, dst = refs[0].at[to], refs[1].at[me]
            else:
                src, dst = refs[0], refs[1].at[me]
            n = i * len(peers) + j
            out.append(pltpu.make_async_remote_copy(
                src_ref=src, dst_ref=dst, send_sem=send_sems.at[n], recv_sem=recv_sems.at[n],
                device_id=peer, device_id_type=pl.DeviceIdType.MESH))
    return out


_HBM = pl.BlockSpec(memory_space=pltpu.HBM)
_SEM = pl.BlockSpec(memory_space=pltpu.SEMAPHORE)
_DATAFLOW = pltpu.SideEffectType.DATAFLOW_SIDE_EFFECTING


def _regroup(flat, groups, width):
    out, off = [], 0
    for g in groups:
        out.append([tuple(flat[off + i * width:off + (i + 1) * width]) for i in range(len(g))])
        off += len(g) * width
    return out


def exchange_start(name, kind, groups, deps=()):
    width, peers = _PLANS[kind]
    flat = [b for g in groups for item in g for b in item]
    n_buf, n_grp = len(flat), len(groups)

    def body(*refs):
        outs = refs[n_buf + len(deps):]
        for gi, items in enumerate(_regroup(refs[:n_buf], groups, width)):
            for cp in _copies(kind, items, outs[2 * gi], outs[2 * gi + 1]):
                cp.start()
        outs[-1][...] = jnp.zeros_like(outs[-1])

    bufs = [pltpu.with_memory_space_constraint(b, pltpu.HBM) for b in flat]
    sem_shapes = []
    for g in groups:
        sem_shapes += [pltpu.SemaphoreType.DMA((len(g) * len(peers),))] * 2
    outs = pl.pallas_call(
        body, name=name,
        out_shape=sem_shapes + [pltpu.HBM(b.shape, b.dtype) for b in bufs] + [jax.ShapeDtypeStruct((8, LANE), F32)],
        in_specs=[_HBM] * n_buf + [_ANY] * len(deps),
        out_specs=[_SEM] * (2 * n_grp) + [_HBM] * n_buf + [pl.BlockSpec(memory_space=pltpu.VMEM)],
        input_output_aliases={i: 2 * n_grp + i for i in range(n_buf)},
        compiler_params=pltpu.CompilerParams(has_side_effects=_DATAFLOW))(*bufs, *deps)
    sems, thru, token = outs[:2 * n_grp], outs[2 * n_grp:-1], outs[-1]
    items = _regroup(thru, groups, width)
    return [(items[gi], sems[2 * gi], sems[2 * gi + 1]) for gi in range(n_grp)], token


def exchange_wait(name, kind, handle, deps=()):
    items, send_sems, recv_sems = handle
    width = _PLANS[kind][0]
    flat = [b for item in items for b in item]
    n_buf = len(flat)

    def body(*refs):
        copies = _copies(kind, _regroup(refs[:n_buf], [items], width)[0], refs[n_buf], refs[n_buf + 1])
        for cp in copies:
            cp.wait_send()
        for cp in copies:
            cp.wait_recv()

    outs = pl.pallas_call(
        body, name=name, out_shape=[pltpu.HBM(b.shape, b.dtype) for b in flat],
        in_specs=[_HBM] * n_buf + [_SEM, _SEM] + [_ANY] * len(deps), out_specs=[_HBM] * n_buf,
        input_output_aliases={i: i for i in range(n_buf)},
        compiler_params=pltpu.CompilerParams(has_side_effects=_DATAFLOW))(*flat, send_sems, recv_sems, *deps)
    return _regroup(outs, [items], width)[0]


def gather_small(name, a, deps=()):
    def body(a_ref, *rest):
        o_ref, send_sems, recv_sems, local_sem = rest[len(deps):]
        me = _mesh_pos()[3]
        own = pltpu.make_async_copy(a_ref, o_ref.at[me], local_sem)
        own.start()
        copies = _copies("gather", [(a_ref, o_ref)], send_sems, recv_sems)
        for cp in copies:
            cp.start()
        for cp in copies:
            cp.wait_recv()
        for cp in copies:
            cp.wait_send()
        own.wait()

    return pl.pallas_call(
        body, name=name, in_specs=[_ANY] * (1 + len(deps)), out_specs=_ANY,
        out_shape=jax.ShapeDtypeStruct((N_DEV,) + a.shape, a.dtype),
        scratch_shapes=[pltpu.SemaphoreType.DMA((N_DEV - 1,)), pltpu.SemaphoreType.DMA((N_DEV - 1,)),
                        pltpu.SemaphoreType.DMA],
        compiler_params=pltpu.CompilerParams(has_side_effects=True))(a, *deps)


def _adamw_math(w, g, m, v):
    m2 = ADAM_B1 * m + (1.0 - ADAM_B1) * g
    v2 = ADAM_B2 * v + (1.0 - ADAM_B2) * (g * g)
    m_hat = m2 / (1.0 - ADAM_B1 ** ADAM_STEP)
    v_hat = v2 / (1.0 - ADAM_B2 ** ADAM_STEP)
    delta = -ADAM_LR * (m_hat / (jnp.sqrt(v_hat) + ADAM_EPS) + ADAM_WD * w)
    return delta, m2, v2


def adamw(name, w, m, v, l, land, own, prev=None):
    L, r, c = w.shape
    cp = land.shape[2]
    tr = _pick(r, (256, 176, 128, 64, 32, 16, 8))

    def body(w_ref, m_ref, v_ref, land_ref, own_ref, *rest):
        g_ref, d_ref, m2_ref, v2_ref = rest[-4:]
        me = _mesh_pos()[3]
        mine = own_ref[:, pl.ds(0, c)].astype(F32)
        g = None
        for s in range(N_DEV):
            part = jnp.where(me == s, mine, land_ref[s, :, pl.ds(0, c)].astype(F32))
            g = part if g is None else g + part
        delta, m2, v2 = _adamw_math(w_ref[...], g, m_ref[...], v_ref[...])
        g_ref[...] = g
        d_ref[...] = delta
        m2_ref[...] = m2
        v2_ref[...] = v2

    blk = pl.BlockSpec((None, tr, c), lambda i: (l, i, 0))
    shape = jax.ShapeDtypeStruct((L, r, c), F32)
    extra = [] if prev is None else list(prev)
    return pl.pallas_call(
        body, name=name, grid=(r // tr,),
        in_specs=[blk, blk, blk, pl.BlockSpec((N_DEV, tr, cp), lambda i: (0, i, 0)),
                  pl.BlockSpec((None, tr, cp), lambda i: (_mesh_pos()[3], i, 0))] + [_ANY] * len(extra),
        out_specs=[blk] * 4, out_shape=[shape] * 4,
        input_output_aliases={5 + k: k for k in range(len(extra))},
        compiler_params=_params("parallel"))(w, m, v, land, own, *extra)


def adamw_small(name, w, m, v, parts):
    n = w.shape[1]

    def body(w_ref, m_ref, v_ref, p_ref, g_ref, d_ref, m2_ref, v2_ref):
        g = p_ref[0:1, :]
        for s in range(1, N_DEV):
            g = g + p_ref[s:s + 1, :]
        delta, m2, v2 = _adamw_math(w_ref[...], g, m_ref[...], v_ref[...])
        g_ref[...] = g
        d_ref[...] = delta
        m2_ref[...] = m2
        v2_ref[...] = v2

    shape = jax.ShapeDtypeStruct((1, n), F32)
    return pl.pallas_call(body, name=name, out_shape=[shape] * 4,
                          compiler_params=pltpu.CompilerParams(vmem_limit_bytes=VMEM_LIMIT_BYTES))(w, m, v, parts)


def _rope_tables(positions):
    half = RET_DK // 2
    inv_freq = 1.0 / jnp.power(RET_THETA_BASE, jnp.linspace(0.0, 1.0, half, dtype=F32))
    ang = positions.astype(F32)[:, None] * inv_freq
    cos, sin = jnp.cos(ang), jnp.sin(ang)
    cosf = jnp.repeat(cos, 2, axis=-1)
    sins = jnp.stack([-sin, sin], axis=-1).reshape(cosf.shape)
    return cosf, sins


def _pad_to(a, axis, size):
    pad = [(0, 0)] * a.ndim
    pad[axis] = (0, size - a.shape[axis])
    return jnp.pad(a, pad)


def _round_up(n, m):
    return -(-n // m) * m


def kernel(x, p, positions, attn_norm_w, ffn_norm_w, ple_norm_w, final_norm_w, ab_w_in, ab_gla_gate_up, ab_gla_gate_b, ab_ret_norm_w, ab_gla_norm_w, ab_w_out, c_w_qkv, c_w_out, ffn_w_gate, ffn_w_up, ffn_w_down, ple_w_proj, ple_w_gate, loss_target, m_attn_norm_w, m_ffn_norm_w, m_ple_norm_w, m_final_norm_w, m_ab_w_in, m_ab_gla_gate_up, m_ab_gla_gate_b, m_ab_ret_norm_w, m_ab_gla_norm_w, m_ab_w_out, m_c_w_qkv, m_c_w_out, m_ffn_w_gate, m_ffn_w_up, m_ffn_w_down, m_ple_w_proj, m_ple_w_gate, v_attn_norm_w, v_ffn_norm_w, v_ple_norm_w, v_final_norm_w, v_ab_w_in, v_ab_gla_gate_up, v_ab_gla_gate_b, v_ab_ret_norm_w, v_ab_gla_norm_w, v_ab_w_out, v_c_w_qkv, v_c_w_out, v_ffn_w_gate, v_ffn_w_up, v_ffn_w_down, v_ple_w_proj, v_ple_w_gate):
    T, D = x.shape[1], x.shape[2]
    depth = attn_norm_w.shape[0]
    assert ab_w_in.shape[0] == 1 and c_w_qkv.shape[0] == 1 and depth == 2, "one even and one odd layer"
    me = 4 * lax.axis_index("x") + 2 * lax.axis_index("y") + lax.axis_index("c")
    in_shard = ab_w_in.shape[2]
    in_width = in_shard * N_DEV
    assert in_width == OFF_LR + GLA_GATE_RANK
    fs = ffn_w_gate.shape[2]
    fp = _round_up(fs, LANE)
    gu_cols = ab_gla_gate_up.shape[2]

    bf = lambda a: a.astype(BF16)
    tr_ = lambda a: jnp.swapaxes(a, -1, -2)
    wg_t, wu_t = tr_(ffn_w_gate), tr_(ffn_w_up)
    srcs = {"w_in": bf(tr_(ab_w_in[0])), "w_oab": bf(ab_w_out[0]), "gu": ab_gla_gate_up[0],
            "w_qkv": bf(c_w_qkv[0]), "w_oc": bf(c_w_out[0])}
    for l in range(depth):
        srcs[f"wg{l}"] = _pad_to(bf(wg_t[l]), 0, fp)
        srcs[f"wu{l}"] = _pad_to(bf(wu_t[l]), 0, fp)
        srcs[f"wd{l}"] = _pad_to(bf(ffn_w_down[l]), 0, fp)
        srcs[f"wpg{l}"] = bf(ple_w_gate[l])
        srcs[f"wpp{l}"] = bf(ple_w_proj[l])
    group_keys = [["w_in", "w_oab", "gu"], ["wg0", "wu0", "wd0", "wpg0", "wpp0"], ["w_qkv", "w_oc"],
                  ["wg1", "wu1", "wd1", "wpg1", "wpp1"]]

    def landing(a):
        return lax.dynamic_update_slice(lax.empty((N_DEV,) + a.shape, a.dtype), a[None], (me,) + (0,) * a.ndim)

    gather_handles, gather_token = exchange_start(
        "gather_start", "to_chips", [[(srcs[k], landing(srcs[k])) for k in keys] for keys in group_keys])
    weights = {}

    def gather_wait(gi, dep):
        arrived = exchange_wait(f"gather_wait{gi}", "to_chips", gather_handles[gi], deps=(dep,))
        passing, _ = exchange_start(f"gather_pass{gi}", "to_sibling", [[(land,) for _, land in arrived]])
        complete = exchange_wait(f"gather_pass_wait{gi}", "to_sibling", passing[0])
        weights.update(zip(group_keys[gi], [land for (land,) in complete]))

    gb = ab_gla_gate_b
    hn_w = jnp.concatenate([ab_ret_norm_w, ab_gla_norm_w], axis=1)
    cosf, sins = _rope_tables(positions[0])
    p_bf = bf(p[:, 0])

    xs = x[0]
    saved = []
    for i in range(depth):
        nm = f"l{i}_"
        w_attn, w_ffn, w_ple = attn_norm_w[i:i + 1], ffn_norm_w[i:i + 1], ple_norm_w[i:i + 1]
        (xn,) = rowwise(nm + "norm_attn", lambda a, w: (_rms(a, w),), T, [("row", xs), ("full", w_attn)],
                        [("row", D, BF16)], deps=(gather_token,) if i == 0 else ())
        gather_wait(2 * i, xn)
        if i % 2 == 0:
            w_in_t = weights["w_in"].reshape(1, 1, in_width, D)
            w_lr_t = _pad_to(w_in_t[0, 0, OFF_LR:], 0, LANE).reshape(1, 1, LANE, D)
            w_oab = weights["w_oab"].reshape(1, 1, D, D)
            gu_full = _pad_to(weights["gu"].transpose(1, 0, 2).reshape(GLA_GATE_RANK, GLA_QK), 0, LANE)
            z = mmt_fwd(nm + "mm_in", xn, w_in_t, 0, F32, n=OFF_LR)
            glr = mmt_fwd(nm + "mm_lr", xn, w_lr_t, 0, F32)
            oraw = retention_fwd(nm + "ret_fwd", z, cosf, sins, RET_V + GLA_V)
            oraw = gla_fwd(nm + "gla_fwd", z, glr, gu_full, gb, oraw)
            o = headnorm_fwd(nm + "headnorm_fwd", oraw, z, hn_w)
            mix = mm_nn(nm + "mm_out", o, w_oab, 0, F32)
            mixer_saved = (z, glr, oraw, o)
        else:
            w_qkv = weights["w_qkv"].reshape((1,) + weights["w_qkv"].shape)
            w_oc = weights["w_oc"].reshape(1, 1, D, D)
            qkv = mm_nn(nm + "mm_qkv", xn, w_qkv, 0, BF16)
            o, lse = attn_fwd(nm + "attn_fwd", qkv)
            mix = mm_nn(nm + "mm_out", o, w_oc, 0, F32)
            mixer_saved = (qkv, o, lse)
        h1, hn = rowwise(nm + "add_norm_ffn", lambda a, b, w: (a + b, _rms(a + b, w)), T,
                         [("row", xs), ("row", mix), ("full", w_ffn)], [("row", D, F32), ("row", D, BF16)])
        gather_wait(2 * i + 1, hn)
        wg = weights[f"wg{i}"].reshape(1, N_DEV, fp, D)
        wu = weights[f"wu{i}"].reshape(1, N_DEV, fp, D)
        wd = weights[f"wd{i}"].reshape(1, 1, N_DEV * fp, D)
        wpg = weights[f"wpg{i}"].reshape(1, 1, D, D)
        wpp = weights[f"wpp{i}"].reshape((1,) + weights[f"wpp{i}"].shape)
        g, u, act = ffn_gate_up(nm + "ffn_gate_up", hn, wg, wu)
        f = mm_nn(nm + "mm_down", act, wd, 0, F32)
        h2, pn = rowwise(nm + "add_norm_ple", lambda a, b, w: (a + b, _rms(a + b, w)), T,
                         [("row", h1), ("row", f), ("full", w_ple)], [("row", D, F32), ("row", D, BF16)])
        s = mm_nn(nm + "mm_ple_gate", pn, wpg, 0, F32)
        e = mm_nn(nm + "mm_ple_proj", p_bf[i], wpp, 0, F32)
        (x_next,) = rowwise(nm + "ple_out", lambda a, b, c: (a + _sigmoid(b) * c,), T,
                            [("row", h2), ("row", s), ("row", e)], [("row", D, F32)])
        mixer_w = (w_in_t, w_lr_t, w_oab, gu_full) if i % 2 == 0 else (w_qkv, w_oc)
        saved.append((xs, xn, mixer_saved, mixer_w, (wg, wu, wd, wpg), h1, hn, g, u, act, h2, pn, s, e))
        xs = x_next

    def loss_fn(a, w, t):
        diff = _rms(a, w) - t
        dx, dw = _rms_bwd(a, w, diff * (1.0 / D))
        part = 0.5 * jnp.sum(jnp.mean(diff * diff, axis=-1, keepdims=True), axis=0, keepdims=True)
        return dx, dw, jnp.broadcast_to(part, (1, LANE))

    dx, d_final_w, loss_part = rowwise("loss_head", loss_fn, T,
                                       [("row", xs), ("full", final_norm_w[None, :]), ("row", loss_target[0])],
                                       [("row", D, F32), ("acc", D), ("acc", LANE)])
    loss = lax.psum(loss_part[0, 0], ("x", "y", "c"))

    grads = {}
    scatters = []

    def scatter_start(name, keys):
        handles, token = exchange_start(name, "scatter", [[(grads[k], lax.empty(grads[k].shape, BF16)) for k in keys]])
        scatters.append((keys, handles[0]))
        return token

    d_attn_w, d_ffn_w, d_ple_w = [None] * depth, [None] * depth, [None] * depth
    for i in reversed(range(depth)):
        nm = f"l{i}_b_"
        xs_i, xn, mixer_saved, mixer_w, (wg, wu, wd, wpg), h1, hn, g, u, act, h2, pn, s, e = saved[i]
        w_attn, w_ffn, w_ple = attn_norm_w[i:i + 1], ffn_norm_w[i:i + 1], ple_norm_w[i:i + 1]

        def ple_bwd(d, sv, ev):
            gate = _sigmoid(sv)
            return d * gate, d * ev * gate * (1.0 - gate)

        de, ds = rowwise(nm + "ple_out", ple_bwd, T, [("row", dx), ("row", s), ("row", e)],
                         [("row", D, BF16), ("row", D, BF16)])
        grads[("ple_w_proj", i)] = mm_tn(nm + "mm_ple_proj_w", p_bf[i], de, N_DEV, BF16)
        grads[("ple_w_gate", i)] = mm_tn(nm + "mm_ple_gate_w", pn, ds, 1, BF16).reshape(N_DEV, D // N_DEV, D)
        dpn = mm_nt(nm + "mm_ple_gate_x", ds, wpg, 0, F32)

        def norm_bwd_add(a, w, dn, dres):
            dxx, dw = _rms_bwd(a, w, dn)
            tot = dres + dxx
            return tot, tot, dw

        dh2, dh2_bf, d_ple_w[i] = rowwise(nm + "norm_ple", norm_bwd_add, T,
                                          [("row", h2), ("full", w_ple), ("row", dpn), ("row", dx)],
                                          [("row", D, F32), ("row", D, BF16), ("acc", D)])
        grads[("ffn_w_down", i)] = mm_tn(nm + "mm_down_w", act, dh2_bf, 1, BF16).reshape(N_DEV, fp, D)
        token = scatter_start(nm + "scatter_ple_down", [("ple_w_proj", i), ("ple_w_gate", i), ("ffn_w_down", i)])
        dg, du = ffn_down_bwd(nm + "ffn_down_x", dh2_bf, wd, g, u, deps=(token,))
        grads[("ffn_w_gate", i)] = mmt_dw(nm + "mm_gate_w", dg, hn, N_DEV, BF16)
        grads[("ffn_w_up", i)] = mmt_dw(nm + "mm_up_w", du, hn, N_DEV, BF16)
        token = scatter_start(nm + "scatter_gate_up", [("ffn_w_gate", i), ("ffn_w_up", i)])
        dhn_g = mmt_dx(nm + "mm_gate_x", dg, wg, 0, F32, deps=(token,))
        dhn_u = mmt_dx(nm + "mm_up_x", du, wu, 0, F32)

        def norm_bwd_add2(a, w, dn1, dn2, dres):
            dxx, dw = _rms_bwd(a, w, dn1 + dn2)
            tot = dres + dxx
            return tot, tot, dw

        dh1, dh1_bf, d_ffn_w[i] = rowwise(nm + "norm_ffn", norm_bwd_add2, T,
                                          [("row", h1), ("full", w_ffn), ("row", dhn_g), ("row", dhn_u), ("row", dh2)],
                                          [("row", D, F32), ("row", D, BF16), ("acc", D)])
        if i % 2 == 0:
            z, glr, oraw, o = mixer_saved
            w_in_t, w_lr_t, w_oab, gu_full = mixer_w
            grads[("ab_w_out", 0)] = mm_tn(nm + "mm_out_w", o, dh1_bf, 1, BF16).reshape(N_DEV, D // N_DEV, D)
            token = scatter_start(nm + "scatter_out", [("ab_w_out", 0)])
            do = mm_nt(nm + "mm_out_x", dh1_bf, w_oab, 0, F32, deps=(token,))
            d_oraw, d_gates, d_hn_w = headnorm_bwd(nm + "headnorm", oraw, z, hn_w, do)
            d_rq, d_rk, d_rv = retention_bwd(nm + "ret", z, cosf, sins, d_oraw)
            d_gq, d_gk, d_gv, d_glr4, d_gu, d_gb = gla_bwd(nm + "gla", z, glr, gu_full, gb, d_oraw)
            dz = jnp.concatenate([d_rq, d_rk, d_rv, d_gates[:, :RET_V], d_gq, d_gk, d_gv, d_gates[:, RET_V:]], axis=1)
            (d_glr,) = rowwise(nm + "sum_lr", lambda *a: (a[0] + a[1] + a[2] + a[3],), T,
                               [("row", d_glr4[hh]) for hh in range(GLA_HEADS)], [("row", LANE, BF16)])
            dw_main = mm_tn(nm + "mm_in_w", xn, dz, 1, BF16)[0]
            dw_lr = mm_tn(nm + "mm_lr_w", xn, d_glr, 1, BF16)[0]
            dw_in = jnp.concatenate([dw_main, dw_lr[:, :GLA_GATE_RANK]], axis=1)
            grads[("ab_w_in", 0)] = dw_in.reshape(D, N_DEV, in_shard).transpose(1, 0, 2)
            token = scatter_start(nm + "scatter_in", [("ab_w_in", 0)])
            dxn_a = mmt_dx(nm + "mm_in_x", dz, w_in_t, 0, F32, n=OFF_LR, deps=(token,))
            dxn_b = mmt_dx(nm + "mm_lr_x", d_glr, w_lr_t, 0, F32)
        else:
            qkv, o, lse = mixer_saved
            w_qkv, w_oc = mixer_w
            grads[("c_w_out", 0)] = mm_tn(nm + "mm_out_w", o, dh1_bf, 1, BF16).reshape(N_DEV, D // N_DEV, D)
            do = mm_nt(nm + "mm_out_x", dh1_bf, w_oc, 0, BF16)
            dq, dk, dv = attn_bwd(nm + "attn", qkv, o, lse, do)
            dqkv = jnp.concatenate([dq, dk, dv], axis=1)
            grads[("c_w_qkv", 0)] = mm_tn(nm + "mm_qkv_w", xn, dqkv, N_DEV, BF16)
            token = scatter_start(nm + "scatter_attn", [("c_w_out", 0), ("c_w_qkv", 0)])
            dxn_a = mm_nt(nm + "mm_qkv_x", dqkv, w_qkv, 0, F32, deps=(token,))
            dxn_b = None
        if dxn_b is None:
            dx, _, d_attn_w[i] = rowwise(nm + "norm_attn", norm_bwd_add, T,
                                         [("row", xs_i), ("full", w_attn), ("row", dxn_a), ("row", dh1)],
                                         [("row", D, F32), ("row", D, BF16), ("acc", D)])
        else:
            dx, _, d_attn_w[i] = rowwise(nm + "norm_attn", norm_bwd_add2, T,
                                         [("row", xs_i), ("full", w_attn), ("row", dxn_a), ("row", dxn_b), ("row", dh1)],
                                         [("row", D, F32), ("row", D, BF16), ("acc", D)])

    small_names = ["attn_norm_w", "ffn_norm_w", "ple_norm_w", "final_norm_w", "ab_gla_gate_b", "ab_ret_norm_w",
                   "ab_gla_norm_w"]
    small_grads = [jnp.concatenate(d_attn_w, 0), jnp.concatenate(d_ffn_w, 0), jnp.concatenate(d_ple_w, 0), d_final_w[0],
                   d_gb, d_hn_w[:, :RET_V], d_hn_w[:, RET_V:]]
    small_w = [attn_norm_w, ffn_norm_w, ple_norm_w, final_norm_w, ab_gla_gate_b, ab_ret_norm_w, ab_gla_norm_w]
    small_m = [m_attn_norm_w, m_ffn_norm_w, m_ple_norm_w, m_final_norm_w, m_ab_gla_gate_b, m_ab_ret_norm_w, m_ab_gla_norm_w]
    small_v = [v_attn_norm_w, v_ffn_norm_w, v_ple_norm_w, v_final_norm_w, v_ab_gla_gate_b, v_ab_ret_norm_w, v_ab_gla_norm_w]
    sizes = [int(np.prod(a.shape)) for a in small_w]
    n_gu = GLA_GATE_RANK * GLA_QK
    n_small = _round_up(sum(sizes) + n_gu, LANE)
    pack = lambda parts: _pad_to(jnp.concatenate([a.reshape(-1) for a in parts]), 0, n_small)[None, :]
    small_part = pack(small_grads + [d_gu[:GLA_GATE_RANK]])

    big_w = dict(ab_w_in=(ab_w_in, m_ab_w_in, v_ab_w_in), ab_w_out=(ab_w_out, m_ab_w_out, v_ab_w_out),
                 c_w_qkv=(c_w_qkv, m_c_w_qkv, v_c_w_qkv), c_w_out=(c_w_out, m_c_w_out, v_c_w_out),
                 ffn_w_gate=(wg_t, tr_(m_ffn_w_gate), tr_(v_ffn_w_gate)),
                 ffn_w_up=(wu_t, tr_(m_ffn_w_up), tr_(v_ffn_w_up)),
                 ffn_w_down=(ffn_w_down, m_ffn_w_down, v_ffn_w_down), ple_w_proj=(ple_w_proj, m_ple_w_proj, v_ple_w_proj),
                 ple_w_gate=(ple_w_gate, m_ple_w_gate, v_ple_w_gate))
    results, last = {}, dx
    for gi, (keys, handle) in enumerate(scatters):
        arrived = exchange_wait(f"scatter_wait{gi}", "scatter", handle, deps=(last,))
        for (n, l), (own, land) in zip(keys, arrived):
            results[n] = adamw(f"adamw_{n}{l}", *big_w[n], l, land, own, prev=results.get(n))
            last = results[n][0]
    for n in ("ffn_w_gate", "ffn_w_up"):
        results[n] = [tr_(a) for a in results[n]]
    small_parts = gather_small("gather_small", small_part, deps=(last,)).reshape(N_DEV, n_small)

    gu_off = sum(sizes)
    own_cols = lambda a: lax.dynamic_slice_in_dim(a.reshape(GLA_GATE_RANK, GLA_QK), me * gu_cols, gu_cols, axis=1)
    small_res = adamw_small("adamw_small", pack(small_w + [jnp.zeros((n_gu,), F32)]),
                            pack(small_m + [jnp.zeros((n_gu,), F32)]), pack(small_v + [jnp.ones((n_gu,), F32)]),
                            small_parts)
    g_gu_full = small_res[0][0, gu_off:gu_off + n_gu]
    g_gu = own_cols(g_gu_full)[None]
    gu_res = adamw_small("adamw_gate_up", *[_pad_to(a.reshape(1, -1), 1, _round_up(a.size, LANE)) for a in
                                            (ab_gla_gate_up, m_ab_gla_gate_up, v_ab_gla_gate_up)],
                         jnp.concatenate([_pad_to(g_gu.reshape(1, -1), 1, _round_up(g_gu.size, LANE)),
                                          jnp.zeros((N_DEV - 1, _round_up(g_gu.size, LANE)), F32)], axis=0))
    for k in range(4):
        off = 0
        for n, a, sz in zip(small_names, small_w, sizes):
            results.setdefault(n, [None] * 4)[k] = small_res[k][0, off:off + sz].reshape(a.shape)
            off += sz
        results.setdefault("ab_gla_gate_up", [None] * 4)[k] = gu_res[k][0, :g_gu.size].reshape(ab_gla_gate_up.shape)

    order = ["attn_norm_w", "ffn_norm_w", "ple_norm_w", "final_norm_w", "ab_w_in", "ab_gla_gate_up", "ab_gla_gate_b",
             "ab_ret_norm_w", "ab_gla_norm_w", "ab_w_out", "c_w_qkv", "c_w_out", "ffn_w_gate", "ffn_w_up", "ffn_w_down",
             "ple_w_proj", "ple_w_gate"]
    return (loss, dx[None], *[results[n][0] for n in order], *[results[n][1] for n in order],
            *[results[n][2] for n in order], *[results[n][3] for n in order])
```

```python
import math

import numpy as np
import jax
import jax.numpy as jnp
from jax import lax
from jax.experimental import pallas as pl
from jax.experimental.pallas import tpu as pltpu

F32 = jnp.float32
BF16 = jnp.bfloat16
HIGHEST = lax.Precision.HIGHEST

N_DEV = 8
VMEM_LIMIT_BYTES = 48 * 1024 * 1024
LANE = 128
NORM_EPS = 1e-6

RET_HEADS, RET_DK, RET_DV = 4, 256, 256
RET_THETA_BASE = 10000.0
GLA_HEADS, GLA_DK, GLA_DV = 4, 128, 256
GLA_GATE_RANK = 16
GLA_GATE_NORM = 16.0
CHUNK = 64
ATT_HEADS = 16
DILATED_BRANCHES = ((128, 1), (512, 4), (2048, 16))
BLK = 256

ADAM_LR, ADAM_B1, ADAM_B2, ADAM_EPS, ADAM_WD, ADAM_STEP = 0.001, 0.9, 0.999, 1e-08, 0.01, 10

RET_QK = RET_HEADS * RET_DK
RET_V = RET_HEADS * RET_DV
GLA_QK = GLA_HEADS * GLA_DK
GLA_V = GLA_HEADS * GLA_DV
OFF_RQ, OFF_RK, OFF_RV, OFF_RG = 0, RET_QK, 2 * RET_QK, 2 * RET_QK + RET_V
OFF_GQ = OFF_RG + RET_V
OFF_GK = OFF_GQ + GLA_QK
OFF_GV = OFF_GK + GLA_QK
OFF_GG = OFF_GV + GLA_V
OFF_LR = OFF_GG + GLA_V


def _params(*sem):
    return pltpu.CompilerParams(dimension_semantics=sem or None, vmem_limit_bytes=VMEM_LIMIT_BYTES)


def _pick(n, cands):
    for c in cands:
        if n % c == 0:
            return c
    raise ValueError(f"no tile for {n} in {cands}")


_NN = (((1,), (0,)), ((), ()))
_NT = (((1,), (1,)), ((), ()))
_TN = (((0,), (0,)), ((), ()))
_ANY = pl.BlockSpec(memory_space=pl.ANY)
MAX_CONTRACT = 2048
_TILES = (1024, 768, 512, 256, 128)


def _mm_call(name, dims, grid, in_specs, out_spec, out_shape, args, deps=()):
    steps = grid[2]
    assert steps == 1 or out_shape.dtype == F32

    def body(a_ref, b_ref, *rest):
        o_ref = rest[len(deps)]
        part = lax.dot_general(a_ref[...].astype(BF16), b_ref[...].astype(BF16), dims, preferred_element_type=F32)
        if steps == 1:
            o_ref[...] = part.astype(o_ref.dtype)
        else:
            _accumulate(o_ref, part, pl.program_id(2) == 0)

    return pl.pallas_call(
        body, name=name, grid=grid, in_specs=list(in_specs) + [_ANY] * len(deps), out_specs=out_spec,
        out_shape=out_shape, compiler_params=_params("parallel", "parallel", "arbitrary"))(*args, *deps)


def mm_nn(name, a, w, l, out_dtype, deps=()):
    _, J, K, n = w.shape
    M = a.shape[0]
    tm, tn, tk = _pick(M, _TILES), _pick(n, _TILES), _pick(K, (MAX_CONTRACT,) + _TILES)
    nt = n // tn
    return _mm_call(
        name, _NN, (M // tm, J * nt, K // tk),
        [pl.BlockSpec((tm, tk), lambda i, j, k: (i, k)),
         pl.BlockSpec((None, None, tk, tn), lambda i, j, k: (l, j // nt, k, j % nt))],
        pl.BlockSpec((tm, tn), lambda i, j, k: (i, j)),
        jax.ShapeDtypeStruct((M, J * n), out_dtype), (a, w), deps)


def mm_nt(name, a, w, l, out_dtype, deps=()):
    _, J, K, n = w.shape
    M = a.shape[0]
    tm, tq, tc = _pick(M, _TILES), _pick(K, _TILES), _pick(n, (MAX_CONTRACT,) + _TILES)
    nc = n // tc
    return _mm_call(
        name, _NT, (M // tm, K // tq, J * nc),
        [pl.BlockSpec((tm, tc), lambda i, q, c: (i, c)),
         pl.BlockSpec((None, None, tq, tc), lambda i, q, c: (l, c // nc, q, c % nc))],
        pl.BlockSpec((tm, tq), lambda i, q, c: (i, q)),
        jax.ShapeDtypeStruct((M, K), out_dtype), (a, w), deps)


def mm_tn(name, x, dy, J, out_dtype, deps=()):
    M, K = x.shape
    n = dy.shape[1] // J
    tp, tn = _pick(K, _TILES), _pick(n, _TILES)
    nt = n // tn
    assert M <= MAX_CONTRACT
    return _mm_call(
        name, _TN, (K // tp, J * nt, 1),
        [pl.BlockSpec((M, tp), lambda i, j, r: (0, i)),
         pl.BlockSpec((M, tn), lambda i, j, r: (0, j))],
        pl.BlockSpec((None, tp, tn), lambda i, j, r: (j // nt, i, j % nt)),
        jax.ShapeDtypeStruct((J, K, n), out_dtype), (x, dy), deps)


def mmt_fwd(name, a, wt, l, out_dtype, n=None, deps=()):
    _, J, rows, K = wt.shape
    n = rows if n is None else n
    M = a.shape[0]
    tm, tn = _pick(M, _TILES), _pick(n, _TILES)
    nt = n // tn
    assert K <= MAX_CONTRACT
    return _mm_call(
        name, _NT, (M // tm, J * nt, 1),
        [pl.BlockSpec((tm, K), lambda i, j, k: (i, 0)),
         pl.BlockSpec((None, None, tn, K), lambda i, j, k: (l, j // nt, j % nt, 0))],
        pl.BlockSpec((tm, tn), lambda i, j, k: (i, j)),
        jax.ShapeDtypeStruct((M, J * n), out_dtype), (a, wt), deps)


def mmt_dx(name, dy, wt, l, out_dtype, n=None, deps=()):
    _, J, rows, K = wt.shape
    n = rows if n is None else n
    M = dy.shape[0]
    tm, tq, tc = _pick(M, _TILES), _pick(K, _TILES), _pick(n, _TILES)
    nc = n // tc
    return _mm_call(
        name, _NN, (M // tm, K // tq, J * nc),
        [pl.BlockSpec((tm, tc), lambda i, q, c: (i, c)),
         pl.BlockSpec((None, None, tc, tq), lambda i, q, c: (l, c // nc, c % nc, q))],
        pl.BlockSpec((tm, tq), lambda i, q, c: (i, q)),
        jax.ShapeDtypeStruct((M, K), out_dtype), (dy, wt), deps)


def mmt_dw(name, dy, x, J, out_dtype, deps=()):
    M, K = x.shape
    n = dy.shape[1] // J
    tn, tp = _pick(n, _TILES), _pick(K, _TILES)
    nt = n // tn
    assert M <= MAX_CONTRACT
    return _mm_call(
        name, _TN, (J * nt, K // tp, 1),
        [pl.BlockSpec((M, tn), lambda j, i, r: (0, j)),
         pl.BlockSpec((M, tp), lambda j, i, r: (0, i))],
        pl.BlockSpec((None, tn, tp), lambda j, i, r: (j // nt, j % nt, i)),
        jax.ShapeDtypeStruct((J, n, K), out_dtype), (dy, x), deps)


def ffn_gate_up(name, a, wg, wu):
    _, J, n, K = wg.shape
    M = a.shape[0]
    tm, tn = _pick(M, _TILES), _pick(n, _TILES)
    nt = n // tn
    assert K <= MAX_CONTRACT

    def body(a_ref, wg_ref, wu_ref, g_ref, u_ref, act_ref):
        x = a_ref[...]
        g = lax.dot_general(x, wg_ref[...], _NT, preferred_element_type=F32)
        u = lax.dot_general(x, wu_ref[...], _NT, preferred_element_type=F32)
        g_ref[...] = g.astype(g_ref.dtype)
        u_ref[...] = u.astype(u_ref.dtype)
        act_ref[...] = (_silu_and_grad(g)[0] * u).astype(act_ref.dtype)

    w_spec = pl.BlockSpec((None, None, tn, K), lambda i, j: (0, j // nt, j % nt, 0))
    out = pl.BlockSpec((tm, tn), lambda i, j: (i, j))
    return pl.pallas_call(
        body, name=name, grid=(M // tm, J * nt),
        in_specs=[pl.BlockSpec((tm, K), lambda i, j: (i, 0)), w_spec, w_spec],
        out_specs=[out] * 3, out_shape=[jax.ShapeDtypeStruct((M, J * n), BF16)] * 3,
        compiler_params=_params("parallel", "parallel"))(a, wg, wu)


def ffn_down_bwd(name, dy, wd, g, u, deps=()):
    _, _, K, n = wd.shape
    M = dy.shape[0]
    tm, tq = _pick(M, _TILES), _pick(K, _TILES)
    assert n <= MAX_CONTRACT

    def body(dy_ref, w_ref, g_ref, u_ref, *rest):
        dg_ref, du_ref = rest[len(deps):]
        dact = lax.dot_general(dy_ref[...], w_ref[...], _NT, preferred_element_type=F32)
        silu, dsilu = _silu_and_grad(g_ref[...].astype(F32))
        dg_ref[...] = (dact * u_ref[...].astype(F32) * dsilu).astype(dg_ref.dtype)
        du_ref[...] = (dact * silu).astype(du_ref.dtype)

    blk = pl.BlockSpec((tm, tq), lambda i, q: (i, q))
    return pl.pallas_call(
        body, name=name, grid=(M // tm, K // tq),
        in_specs=[pl.BlockSpec((tm, n), lambda i, q: (i, 0)),
                  pl.BlockSpec((None, None, tq, n), lambda i, q: (0, 0, q, 0)), blk, blk] + [_ANY] * len(deps),
        out_specs=[blk, blk], out_shape=[jax.ShapeDtypeStruct((M, K), BF16)] * 2,
        compiler_params=_params("parallel", "parallel"))(dy, wd, g, u, *deps)


def rowwise(name, fn, rows, ins, outs, tr=256, deps=()):
    widest = max([s[1].shape[1] if s[0] != "col" else s[3] for s in ins] + [s[1] for s in outs])
    tr = min(tr if widest <= 2048 else tr // 2, rows)
    in_specs, args = [], []
    for spec in ins:
        kind, a = spec[0], spec[1]
        if kind == "row":
            in_specs.append(pl.BlockSpec((tr, a.shape[1]), lambda i: (i, 0)))
        elif kind == "col":
            cb, width = spec[2], spec[3]
            in_specs.append(pl.BlockSpec((tr, width), lambda i, cb=cb: (i, cb)))
        else:
            in_specs.append(pl.BlockSpec(a.shape, lambda i: (0, 0)))
        args.append(a)
    out_specs, out_shapes = [], []
    for spec in outs:
        if spec[0] == "row":
            out_specs.append(pl.BlockSpec((tr, spec[1]), lambda i: (i, 0)))
            out_shapes.append(jax.ShapeDtypeStruct((rows, spec[1]), spec[2]))
        else:
            out_specs.append(pl.BlockSpec((1, spec[1]), lambda i: (0, 0)))
            out_shapes.append(jax.ShapeDtypeStruct((1, spec[1]), F32))
    n_in = len(ins)

    def body(*refs):
        vals = fn(*[r[...] for r in refs[:n_in]])
        first = pl.program_id(0) == 0
        for r, v, spec in zip(refs[n_in + len(deps):], vals, outs):
            if spec[0] == "row":
                r[...] = v.astype(r.dtype)
            else:
                _accumulate(r, v, first)

    return pl.pallas_call(body, name=name, grid=(rows // tr,), in_specs=in_specs + [_ANY] * len(deps),
                          out_specs=out_specs, out_shape=out_shapes,
                          compiler_params=_params("arbitrary"))(*args, *deps)


def _accumulate(ref, v, first):
    @pl.when(first)
    def _():
        ref[...] = v

    @pl.when(jnp.logical_not(first))
    def _():
        ref[...] += v


def _rms(x, w):
    r = lax.rsqrt(jnp.mean(x * x, axis=-1, keepdims=True) + NORM_EPS)
    return x * r * w


def _rms_bwd(x, w, dy):
    r = lax.rsqrt(jnp.mean(x * x, axis=-1, keepdims=True) + NORM_EPS)
    g = dy * w
    dx = r * (g - x * (r * r) * jnp.mean(g * x, axis=-1, keepdims=True))
    dw = jnp.sum(dy * x * r, axis=0, keepdims=True)
    return dx, dw


def _sigmoid(x):
    return 1.0 / (1.0 + jnp.exp(-x))


def _silu_and_grad(g):
    s = _sigmoid(g)
    return g * s, s * (1.0 + g * (1.0 - s))


def _swap_pairs(x):
    n = x.shape[-1]
    lane = lax.broadcasted_iota(jnp.int32, x.shape, x.ndim - 1)
    return jnp.where((lane & 1) == 0, pltpu.roll(x, n - 1, x.ndim - 1), pltpu.roll(x, 1, x.ndim - 1))


def _rot(x, cosf, sins):
    return x * cosf + _swap_pairs(x) * sins


def _unrot(d, cosf, sins):
    return d * cosf + _swap_pairs(d * sins)


def _ret_log_gamma(h):
    vals = [math.log1p(-2.0 ** (-5.0 - i)) for i in range(RET_HEADS)]
    out = jnp.float32(vals[RET_HEADS - 1])
    for i in range(RET_HEADS - 2, -1, -1):
        out = jnp.where(h == i, jnp.float32(vals[i]), out)
    return out


def _fill_decays(dec_ref, lg):
    ri = lax.broadcasted_iota(jnp.int32, (BLK, BLK), 0)
    ci = lax.broadcasted_iota(jnp.int32, (BLK, BLK), 1)
    for d in range(dec_ref.shape[0]):
        dt = d * BLK + ri - ci
        dec_ref[d] = jnp.where(dt >= 0, jnp.exp(jnp.maximum(dt, 0).astype(F32) * lg), 0.0)


def _decay_row(dec_ref, qi):
    return jnp.concatenate([dec_ref[qi - kb] for kb in range(qi + 1)], axis=1)


def _once(block_shape, index_map):
    return pl.BlockSpec(block_shape, index_map, pipeline_mode=pl.Buffered(1))


def _dot(a, b):
    return jnp.dot(a.astype(BF16), b.astype(BF16), preferred_element_type=F32)


def _dot_nt(a, b):
    return lax.dot_general(a.astype(BF16), b.astype(BF16), _NT, preferred_element_type=F32)


def _dot_tn(a, b):
    return lax.dot_general(a.astype(BF16), b.astype(BF16), _TN, preferred_element_type=F32)


def retention_fwd(name, z, cosf, sins, width_out):
    T = z.shape[0]
    nq = T // BLK
    scale = RET_DK ** -0.5

    def body(q_ref, k_ref, v_ref, cos_ref, sin_ref, o_ref, krot, vb, dec_ref):
        _fill_decays(dec_ref, _ret_log_gamma(pl.program_id(0)))
        krot[...] = (_rot(k_ref[...], cos_ref[...], sin_ref[...]) * scale).astype(BF16)
        vb[...] = v_ref[...].astype(BF16)
        for qi in range(nq):
            rows, n = slice(qi * BLK, (qi + 1) * BLK), (qi + 1) * BLK
            q = _rot(q_ref[rows, :], cos_ref[rows, :], sin_ref[rows, :])
            s = _dot_nt(q, krot[0:n, :]) * _decay_row(dec_ref, qi)
            o_ref[rows, :] = _dot(s, vb[0:n, :])

    return pl.pallas_call(
        body, name=name, grid=(RET_HEADS,),
        in_specs=[pl.BlockSpec((T, RET_DK), lambda h: (0, OFF_RQ // RET_DK + h)),
                  pl.BlockSpec((T, RET_DK), lambda h: (0, OFF_RK // RET_DK + h)),
                  pl.BlockSpec((T, RET_DV), lambda h: (0, OFF_RV // RET_DV + h)),
                  _once((T, RET_DK), lambda h: (0, 0)), _once((T, RET_DK), lambda h: (0, 0))],
        out_specs=pl.BlockSpec((T, RET_DV), lambda h: (0, h)),
        out_shape=jax.ShapeDtypeStruct((T, width_out), F32),
        scratch_shapes=[pltpu.VMEM((T, RET_DK), BF16), pltpu.VMEM((T, RET_DV), BF16),
                        pltpu.VMEM((nq, BLK, BLK), F32)],
        compiler_params=_params("arbitrary"))(z, z, z, cosf, sins)


def retention_bwd(name, z, cosf, sins, do):
    T = z.shape[0]
    nq = T // BLK
    scale = RET_DK ** -0.5

    def body(q_ref, k_ref, v_ref, cos_ref, sin_ref, do_ref, dq_ref, dk_ref, dv_ref, krot, vb, dk_acc, dv_acc, dec_ref):
        _fill_decays(dec_ref, _ret_log_gamma(pl.program_id(0)))
        krot[...] = (_rot(k_ref[...], cos_ref[...], sin_ref[...]) * scale).astype(BF16)
        vb[...] = v_ref[...].astype(BF16)
        dk_acc[...] = jnp.zeros_like(dk_acc)
        dv_acc[...] = jnp.zeros_like(dv_acc)
        for qi in range(nq):
            rows, n = slice(qi * BLK, (qi + 1) * BLK), (qi + 1) * BLK
            cos_q, sin_q = cos_ref[rows, :], sin_ref[rows, :]
            q = _rot(q_ref[rows, :], cos_q, sin_q).astype(BF16)
            dout = do_ref[rows, :].astype(BF16)
            kk, vv, dec = krot[0:n, :], vb[0:n, :], _decay_row(dec_ref, qi)
            p = (_dot_nt(q, kk) * dec).astype(BF16)
            ds = (_dot_nt(dout, vv) * dec).astype(BF16)
            dq_ref[rows, :] = _unrot(_dot(ds, kk), cos_q, sin_q).astype(dq_ref.dtype)
            dk_acc[0:n, :] += _dot_tn(ds, q)
            dv_acc[0:n, :] += _dot_tn(p, dout)
        dk_ref[...] = (_unrot(dk_acc[...], cos_ref[...], sin_ref[...]) * scale).astype(dk_ref.dtype)
        dv_ref[...] = dv_acc[...].astype(dv_ref.dtype)

    head = lambda h: (0, h)
    return pl.pallas_call(
        body, name=name, grid=(RET_HEADS,),
        in_specs=[pl.BlockSpec((T, RET_DK), lambda h: (0, OFF_RQ // RET_DK + h)),
                  pl.BlockSpec((T, RET_DK), lambda h: (0, OFF_RK // RET_DK + h)),
                  pl.BlockSpec((T, RET_DV), lambda h: (0, OFF_RV // RET_DV + h)),
                  _once((T, RET_DK), lambda h: (0, 0)), _once((T, RET_DK), lambda h: (0, 0)),
                  pl.BlockSpec((T, RET_DV), head)],
        out_specs=[pl.BlockSpec((T, RET_DK), head), pl.BlockSpec((T, RET_DK), head), pl.BlockSpec((T, RET_DV), head)],
        out_shape=[jax.ShapeDtypeStruct((T, RET_QK), BF16), jax.ShapeDtypeStruct((T, RET_QK), BF16),
                   jax.ShapeDtypeStruct((T, RET_V), BF16)],
        scratch_shapes=[pltpu.VMEM((T, RET_DK), BF16), pltpu.VMEM((T, RET_DV), BF16),
                        pltpu.VMEM((T, RET_DK), F32), pltpu.VMEM((T, RET_DV), F32),
                        pltpu.VMEM((nq, BLK, BLK), F32)],
        compiler_params=_params("arbitrary"))(z, z, z, cosf, sins, do)


GLA_PAIR = 2


def _gla_chunk(q_ref, k_ref, v_ref, glr_ref, gu, gb, rows, hh, trilf):
    ck = slice(hh * GLA_DK, (hh + 1) * GLA_DK)
    zg = _dot(glr_ref[rows, :], gu[:, ck]) + gb[:, ck]
    la = (jnp.minimum(zg, 0.0) - jnp.log(1.0 + jnp.exp(-jnp.abs(zg)))) * (1.0 / GLA_GATE_NORM)
    cum = jnp.dot(trilf, la, precision=HIGHEST, preferred_element_type=F32)
    last = jnp.sum(la, axis=0, keepdims=True)
    ecum = jnp.exp(cum)
    k = k_ref[rows, ck]
    qt = q_ref[rows, ck] * (GLA_DK ** -0.5) * ecum
    kt = k * jnp.exp(-cum)
    kh = k * jnp.exp(last - cum)
    return zg, cum, last, ecum, qt, kt, kh, v_ref[rows, hh * GLA_DV:(hh + 1) * GLA_DV].astype(BF16)


def _state_decay(last):
    e = jnp.exp(jnp.broadcast_to(last, (GLA_DK, GLA_DK)).T)
    return jnp.concatenate([e] * (GLA_DV // GLA_DK), axis=1)


def _gla_specs(T):
    wk, wv = GLA_PAIR * GLA_DK, GLA_PAIR * GLA_DV
    return [_once((T, wk), lambda h: (0, OFF_GQ // wk + h)),
            _once((T, wk), lambda h: (0, OFF_GK // wk + h)),
            _once((T, wv), lambda h: (0, OFF_GV // wv + h)),
            _once((T, LANE), lambda h: (0, 0)),
            pl.BlockSpec((LANE, wk), lambda h: (0, h)),
            pl.BlockSpec((1, wk), lambda h: (0, h))]


def gla_fwd(name, z, glr, gu, gb, o_prev):
    T = z.shape[0]
    nc = T // CHUNK
    wv = GLA_PAIR * GLA_DV

    def body(q_ref, k_ref, v_ref, glr_ref, gu_ref, gb_ref, prev_ref, o_ref, S):
        del prev_ref
        gu_b, gb_v = gu_ref[...].astype(BF16), gb_ref[...]
        ri = lax.broadcasted_iota(jnp.int32, (CHUNK, CHUNK), 0)
        ci = lax.broadcasted_iota(jnp.int32, (CHUNK, CHUNK), 1)
        tril = ri >= ci
        trilf = tril.astype(F32)
        S[...] = jnp.zeros_like(S)

        def step(c, carry):
            rows = pl.ds(pl.multiple_of(c * CHUNK, CHUNK), CHUNK)
            for hh in range(GLA_PAIR):
                _, _, last, _, qt, kt, kh, v = _gla_chunk(q_ref, k_ref, v_ref, glr_ref, gu_b, gb_v, rows, hh, trilf)
                a = jnp.where(tril, _dot_nt(qt, kt), 0.0)
                s_prev = S[hh]
                o_ref[rows, hh * GLA_DV:(hh + 1) * GLA_DV] = _dot(a, v) + _dot(qt, s_prev)
                S[hh] = s_prev * _state_decay(last) + _dot_tn(kh, v)
            return carry

        lax.fori_loop(0, nc, step, 0)

    n_in = 6
    return pl.pallas_call(
        body, name=name, grid=(GLA_HEADS // GLA_PAIR,),
        in_specs=_gla_specs(T) + [pl.BlockSpec(memory_space=pl.ANY)],
        out_specs=pl.BlockSpec((T, wv), lambda h: (0, RET_V // wv + h)),
        out_shape=jax.ShapeDtypeStruct(o_prev.shape, F32),
        scratch_shapes=[pltpu.VMEM((GLA_PAIR, GLA_DK, GLA_DV), F32)],
        input_output_aliases={n_in: 0},
        compiler_params=_params("arbitrary"))(z, z, z, glr, gu, gb, o_prev)


def gla_bwd(name, z, glr, gu, gb, do):
    T = z.shape[0]
    nc = T // CHUNK

    def body(q_ref, k_ref, v_ref, glr_ref, gu_ref, gb_ref, do_ref,
             dq_ref, dk_ref, dv_ref, dglr_ref, dgu_ref, dgb_ref, s_all, dS):
        gu_b, gb_v = gu_ref[...].astype(BF16), gb_ref[...]
        ri = lax.broadcasted_iota(jnp.int32, (CHUNK, CHUNK), 0)
        ci = lax.broadcasted_iota(jnp.int32, (CHUNK, CHUNK), 1)
        tril = ri >= ci
        trilf = tril.astype(F32)
        triuf = (ri <= ci).astype(F32)
        last_row = lax.broadcasted_iota(jnp.int32, (CHUNK, GLA_DK), 0) == CHUNK - 1
        ones8 = jnp.ones((8, GLA_DV), F32)

        def fstep(c, carry):
            rows = pl.ds(pl.multiple_of(c * CHUNK, CHUNK), CHUNK)
            for hh in range(GLA_PAIR):
                s_prev = dS[hh]
                s_all[hh, c] = s_prev
                _, _, last, _, _, _, kh, v = _gla_chunk(q_ref, k_ref, v_ref, glr_ref, gu_b, gb_v, rows, hh, trilf)
                dS[hh] = s_prev * _state_decay(last) + _dot_tn(kh, v)
            return carry

        dS[...] = jnp.zeros_like(dS)
        lax.fori_loop(0, nc, fstep, 0)
        dS[...] = jnp.zeros_like(dS)
        dgu_ref[...] = jnp.zeros_like(dgu_ref)
        dgb_ref[...] = jnp.zeros_like(dgb_ref)

        def bstep(i, carry):
            c = nc - 1 - i
            rows = pl.ds(pl.multiple_of(c * CHUNK, CHUNK), CHUNK)
            glr_c = glr_ref[rows, :]
            for hh in range(GLA_PAIR):
                ck, cv = slice(hh * GLA_DK, (hh + 1) * GLA_DK), slice(hh * GLA_DV, (hh + 1) * GLA_DV)
                zg, cum, last, ecum, qt, kt, kh, v = _gla_chunk(q_ref, k_ref, v_ref, glr_ref, gu_b, gb_v, rows, hh, trilf)
                a = jnp.where(tril, _dot_nt(qt, kt), 0.0)
                s_prev, ds_new = s_all[hh, c], dS[hh]
                dout = do_ref[rows, cv].astype(BF16)
                dv_ref[rows, cv] = (_dot_tn(a, dout) + _dot(kh, ds_new)).astype(dv_ref.dtype)
                da = jnp.where(tril, _dot_nt(dout, v), 0.0)
                dqt = _dot(da, kt) + _dot_nt(dout, s_prev)
                dkt = _dot_tn(da, qt)
                dkh = _dot_nt(v, ds_new)
                dS[hh] = ds_new * _state_decay(last) + _dot_tn(qt, dout)
                dq_ref[rows, ck] = (dqt * ecum * (GLA_DK ** -0.5)).astype(dq_ref.dtype)
                dk_ref[rows, ck] = (dkt * jnp.exp(-cum) + dkh * jnp.exp(last - cum)).astype(dk_ref.dtype)
                dkh_kh = dkh * kh
                dcum = dqt * qt - dkt * kt - dkh_kh
                rs = lax.dot_general(ones8, ds_new * s_prev, _NT, precision=HIGHEST, preferred_element_type=F32)
                dlast = (jnp.sum(dkh_kh, axis=0, keepdims=True)
                         + jnp.exp(last) * (jnp.sum(rs, axis=0, keepdims=True) * 0.125))
                dcum = dcum + jnp.where(last_row, dlast, 0.0)
                dla = jnp.dot(triuf, dcum, precision=HIGHEST, preferred_element_type=F32)
                dzg = dla * (1.0 / GLA_GATE_NORM) * _sigmoid(-zg)
                dglr_ref[hh, rows, :] = _dot_nt(dzg, gu_b[:, ck])
                dgu_ref[:, ck] += _dot_tn(glr_c, dzg)
                dgb_ref[:, ck] += jnp.sum(dzg, axis=0, keepdims=True)
            return carry

        lax.fori_loop(0, nc, bstep, 0)

    wk, wv = GLA_PAIR * GLA_DK, GLA_PAIR * GLA_DV
    return pl.pallas_call(
        body, name=name, grid=(GLA_HEADS // GLA_PAIR,),
        in_specs=_gla_specs(T) + [_once((T, wv), lambda h: (0, RET_V // wv + h))],
        out_specs=[pl.BlockSpec((T, wk), lambda h: (0, h)), pl.BlockSpec((T, wk), lambda h: (0, h)),
                   pl.BlockSpec((T, wv), lambda h: (0, h)),
                   pl.BlockSpec((GLA_PAIR, T, LANE), lambda h: (h, 0, 0)),
                   pl.BlockSpec((LANE, wk), lambda h: (0, h)), pl.BlockSpec((1, wk), lambda h: (0, h))],
        out_shape=[jax.ShapeDtypeStruct((T, GLA_QK), BF16), jax.ShapeDtypeStruct((T, GLA_QK), BF16),
                   jax.ShapeDtypeStruct((T, GLA_V), BF16), jax.ShapeDtypeStruct((GLA_HEADS, T, LANE), F32),
                   jax.ShapeDtypeStruct((LANE, GLA_QK), F32), jax.ShapeDtypeStruct((1, GLA_QK), F32)],
        scratch_shapes=[pltpu.VMEM((GLA_PAIR, nc, GLA_DK, GLA_DV), F32), pltpu.VMEM((GLA_PAIR, GLA_DK, GLA_DV), F32)],
        compiler_params=_params("arbitrary"))(z, z, z, glr, gu, gb, do)


HN_HEADS = RET_HEADS + GLA_HEADS
HN_W = RET_DV


def _gate_col(h):
    return jnp.where(h < RET_HEADS, OFF_RG // HN_W + h, OFF_GG // HN_W + h - RET_HEADS)


def headnorm_fwd(name, oraw, z, w, tr=256):
    T = oraw.shape[0]

    def body(o_ref, g_ref, w_ref, y_ref):
        y_ref[...] = (_rms(o_ref[...], w_ref[...]) * _silu_and_grad(g_ref[...])[0]).astype(y_ref.dtype)

    return pl.pallas_call(
        body, name=name, grid=(HN_HEADS, T // tr),
        in_specs=[pl.BlockSpec((tr, HN_W), lambda h, i: (i, h)),
                  pl.BlockSpec((tr, HN_W), lambda h, i: (i, _gate_col(h))),
                  pl.BlockSpec((1, HN_W), lambda h, i: (0, h))],
        out_specs=pl.BlockSpec((tr, HN_W), lambda h, i: (i, h)),
        out_shape=jax.ShapeDtypeStruct((T, HN_HEADS * HN_W), BF16),
        compiler_params=_params("arbitrary", "arbitrary"))(oraw, z, w)


def headnorm_bwd(name, oraw, z, w, dy, tr=256):
    T = oraw.shape[0]

    def body(o_ref, g_ref, w_ref, dy_ref, do_ref, dg_ref, dw_ref):
        o, wv, dyv = o_ref[...], w_ref[...], dy_ref[...].astype(F32)
        silu, dsilu = _silu_and_grad(g_ref[...])
        n = _rms(o, wv)
        dg_ref[...] = (dyv * n * dsilu).astype(dg_ref.dtype)
        dx, dw = _rms_bwd(o, wv, dyv * silu)
        do_ref[...] = dx
        _accumulate(dw_ref, dw, pl.program_id(1) == 0)

    blk = pl.BlockSpec((tr, HN_W), lambda h, i: (i, h))
    return pl.pallas_call(
        body, name=name, grid=(HN_HEADS, T // tr),
        in_specs=[blk, pl.BlockSpec((tr, HN_W), lambda h, i: (i, _gate_col(h))),
                  pl.BlockSpec((1, HN_W), lambda h, i: (0, h)), blk],
        out_specs=[blk, blk, pl.BlockSpec((1, HN_W), lambda h, i: (0, h))],
        out_shape=[jax.ShapeDtypeStruct((T, HN_HEADS * HN_W), F32),
                   jax.ShapeDtypeStruct((T, HN_HEADS * HN_W), BF16),
                   jax.ShapeDtypeStruct((1, HN_HEADS * HN_W), F32)],
        compiler_params=_params("arbitrary", "arbitrary"))(oraw, z, w, dy)


N_MASKS = 4


def _check_mask_classes(T):
    for window, dilation in DILATED_BRANCHES[:-1]:
        assert window < (N_MASKS - 1) * BLK - (BLK - 1) and BLK % dilation == 0
    assert DILATED_BRANCHES[-1][0] >= T and BLK % DILATED_BRANCHES[-1][1] == 0


def _fill_masks(mult_ref, bias_ref):
    ri = lax.broadcasted_iota(jnp.int32, (BLK, BLK), 0)
    ci = lax.broadcasted_iota(jnp.int32, (BLK, BLK), 1)
    for d in range(N_MASKS):
        dt = d * BLK + ri - ci
        mult = jnp.zeros((BLK, BLK), F32)
        for window, dilation in DILATED_BRANCHES:
            hit = (dt >= 0) & (dt <= window) & ((dt & (dilation - 1)) == 0)
            mult = mult + hit.astype(F32)
        mult_ref[d] = mult
        bias_ref[d] = jnp.where(mult > 0, 0.0, -1e30)


def _mask_row(ref, qi):
    return jnp.concatenate([ref[min(qi - kb, N_MASKS - 1)] for kb in range(qi + 1)], axis=1)


def attn_fwd(name, qkv):
    T = qkv.shape[0]
    D = qkv.shape[1] // 3
    dh = D // ATT_HEADS
    nq = T // BLK
    scale = dh ** -0.5

    _check_mask_classes(T)

    def body(q_ref, k_ref, v_ref, o_ref, lse_ref, mult_ref, bias_ref):
        @pl.when(pl.program_id(0) == 0)
        def _():
            _fill_masks(mult_ref, bias_ref)

        for qi in range(nq):
            rows, n = slice(qi * BLK, (qi + 1) * BLK), (qi + 1) * BLK
            s = (_dot_nt(q_ref[rows, :], k_ref[0:n, :]) * scale
                 + _mask_row(bias_ref, qi))
            m = jnp.max(s, axis=-1, keepdims=True)
            p = _mask_row(mult_ref, qi) * jnp.exp(s - m)
            l = jnp.sum(p, axis=-1, keepdims=True)
            o_ref[rows, :] = (_dot(p, v_ref[0:n, :]) / l).astype(o_ref.dtype)
            lse_ref[rows, :] = jnp.broadcast_to(m + jnp.log(l), (BLK, LANE))

    return pl.pallas_call(
        body, name=name, grid=(ATT_HEADS,),
        in_specs=[pl.BlockSpec((T, dh), lambda h: (0, h)),
                  pl.BlockSpec((T, dh), lambda h: (0, ATT_HEADS + h)),
                  pl.BlockSpec((T, dh), lambda h: (0, 2 * ATT_HEADS + h))],
        out_specs=[pl.BlockSpec((T, dh), lambda h: (0, h)),
                   pl.BlockSpec((None, T, LANE), lambda h: (h, 0, 0))],
        out_shape=[jax.ShapeDtypeStruct((T, D), BF16), jax.ShapeDtypeStruct((ATT_HEADS, T, LANE), F32)],
        scratch_shapes=[pltpu.VMEM((N_MASKS, BLK, BLK), F32), pltpu.VMEM((N_MASKS, BLK, BLK), F32)],
        compiler_params=_params("arbitrary"))(qkv, qkv, qkv)


def attn_bwd(name, qkv, o, lse, do):
    T = qkv.shape[0]
    D = qkv.shape[1] // 3
    dh = D // ATT_HEADS
    nq = T // BLK
    scale = dh ** -0.5

    _check_mask_classes(T)

    def body(q_ref, k_ref, v_ref, o_ref, lse_ref, do_ref, dq_ref, dk_ref, dv_ref, dk_acc, dv_acc, mult_ref, bias_ref):
        @pl.when(pl.program_id(0) == 0)
        def _():
            _fill_masks(mult_ref, bias_ref)

        dk_acc[...] = jnp.zeros_like(dk_acc)
        dv_acc[...] = jnp.zeros_like(dv_acc)
        for qi in range(nq):
            rows, n = slice(qi * BLK, (qi + 1) * BLK), (qi + 1) * BLK
            q, dout = q_ref[rows, :], do_ref[rows, :]
            kk, vv = k_ref[0:n, :], v_ref[0:n, :]
            delta = jnp.sum(dout.astype(F32) * o_ref[rows, :].astype(F32), axis=-1, keepdims=True)
            lse = jnp.max(lse_ref[rows, :], axis=-1, keepdims=True)
            s = _dot_nt(q, kk) * scale + _mask_row(bias_ref, qi)
            p = _mask_row(mult_ref, qi) * jnp.exp(s - lse)
            ds = (p * (_dot_nt(dout, vv) - delta) * scale).astype(BF16)
            dq_ref[rows, :] = _dot(ds, kk).astype(dq_ref.dtype)
            dk_acc[0:n, :] += _dot_tn(ds, q)
            dv_acc[0:n, :] += _dot_tn(p, dout)
        dk_ref[...] = dk_acc[...].astype(dk_ref.dtype)
        dv_ref[...] = dv_acc[...].astype(dv_ref.dtype)

    full = pl.BlockSpec((T, dh), lambda h: (0, h))
    return pl.pallas_call(
        body, name=name, grid=(ATT_HEADS,),
        in_specs=[full, pl.BlockSpec((T, dh), lambda h: (0, ATT_HEADS + h)),
                  pl.BlockSpec((T, dh), lambda h: (0, 2 * ATT_HEADS + h)),
                  full, pl.BlockSpec((None, T, LANE), lambda h: (h, 0, 0)), full],
        out_specs=[full, full, full],
        out_shape=[jax.ShapeDtypeStruct((T, D), BF16)] * 3,
        scratch_shapes=[pltpu.VMEM((T, dh), F32), pltpu.VMEM((T, dh), F32),
                        pltpu.VMEM((N_MASKS, BLK, BLK), F32), pltpu.VMEM((N_MASKS, BLK, BLK), F32)],
        compiler_params=_params("arbitrary"))(qkv, qkv, qkv, o, lse, do)


def _mesh_pos():
    mx, my, mc = lax.axis_index("x"), lax.axis_index("y"), lax.axis_index("c")
    return mx, my, mc, 4 * mx + 2 * my + mc


def _peer(k, mx, my, mc):
    px, py, pc = mx ^ (k >> 2), my ^ ((k >> 1) & 1), mc ^ (k & 1)
    return (px, py, pc), 4 * px + 2 * py + pc


_SIBLING, _X_CHIP, _Y_CHIP, _FAR_CHIP = 1, 4, 2, 6
_PLANS = {"gather": (2, N_DEV - 1), "scatter": (2, N_DEV - 1), "to_near": (2, 3), "relay": (1, 1),
          "pass_near": (1, 2), "pass_far": (1, 1)}


def _copies(kind, items, send_sems, recv_sems):
    mx, my, mc, me = _mesh_pos()
    north = mc == 1
    out = []

    def add(n, src, dst, peer):
        out.append(pltpu.make_async_remote_copy(
            src_ref=src, dst_ref=dst, send_sem=send_sems.at[n], recv_sem=recv_sems.at[n],
            device_id=peer, device_id_type=pl.DeviceIdType.MESH))

    per_item = _PLANS[kind][1]
    sibling = _peer(_SIBLING, mx, my, mc)[0]
    for i, refs in enumerate(items):
        n = i * per_item
        if kind in ("gather", "scatter"):
            for k in range(1, N_DEV):
                peer, to = _peer(k, mx, my, mc)
                add(n + k - 1, refs[0] if kind == "gather" else refs[0].at[to], refs[1].at[me], peer)
        elif kind == "to_near":
            for j, k in enumerate((_SIBLING, _X_CHIP, _Y_CHIP)):
                add(n + j, refs[0], refs[1].at[me], _peer(k, mx, my, mc)[0])
        elif kind == "relay":
            came_from = me ^ jnp.where(north, _Y_CHIP, _X_CHIP)
            onward = (mx ^ jnp.where(north, 1, 0), my ^ jnp.where(north, 0, 1), mc)
            add(n, refs[0].at[came_from], refs[0].at[came_from], onward)
        elif kind == "pass_near":
            for j, k in enumerate((_X_CHIP, _Y_CHIP)):
                add(n + j, refs[0].at[me ^ k], refs[0].at[me ^ k], sibling)
        else:
            add(n, refs[0].at[me ^ _FAR_CHIP], refs[0].at[me ^ _FAR_CHIP], sibling)
    return out


_HBM = pl.BlockSpec(memory_space=pltpu.HBM)
_SEM = pl.BlockSpec(memory_space=pltpu.SEMAPHORE)
_DATAFLOW = pltpu.SideEffectType.DATAFLOW_SIDE_EFFECTING


def exchange_call(name, waits, starts, deps=()):
    bufs, slot_of = [], {}

    def slots(items):
        out = []
        for item in items:
            for b in item:
                if id(b) not in slot_of:
                    slot_of[id(b)] = len(bufs)
                    bufs.append(b)
            out.append(tuple(slot_of[id(b)] for b in item))
        return out

    wait_plan = [(kind, slots(handle[0])) for kind, handle in waits]
    start_plan = [(kind, slots(items)) for kind, items in starts]
    wait_sems = [s for _, handle in waits for s in handle[1:]]
    n_buf, n_ws, n_start = len(bufs), len(wait_sems), len(starts)

    def body(*refs):
        buf_refs, sems_in = refs[:n_buf], refs[n_buf:n_buf + n_ws]
        outs = refs[n_buf + n_ws + len(deps):]
        pick = lambda plan: [tuple(buf_refs[s] for s in item) for item in plan]
        for wi, (kind, plan) in enumerate(wait_plan):
            copies = _copies(kind, pick(plan), sems_in[2 * wi], sems_in[2 * wi + 1])
            for cp in copies:
                cp.wait_send()
            for cp in copies:
                cp.wait_recv()
        for si, (kind, plan) in enumerate(start_plan):
            for cp in _copies(kind, pick(plan), outs[2 * si], outs[2 * si + 1]):
                cp.start()
        outs[-1][...] = jnp.zeros_like(outs[-1])

    hbm_bufs = [pltpu.with_memory_space_constraint(b, pltpu.HBM) for b in bufs]
    sem_shapes = []
    for kind, plan in start_plan:
        sem_shapes += [pltpu.SemaphoreType.DMA((len(plan) * _PLANS[kind][1],))] * 2
    outs = pl.pallas_call(
        body, name=name,
        out_shape=sem_shapes + [pltpu.HBM(b.shape, b.dtype) for b in bufs] + [jax.ShapeDtypeStruct((8, LANE), F32)],
        in_specs=[_HBM] * n_buf + [_SEM] * n_ws + [_ANY] * len(deps),
        out_specs=[_SEM] * (2 * n_start) + [_HBM] * n_buf + [pl.BlockSpec(memory_space=pltpu.VMEM)],
        input_output_aliases={i: 2 * n_start + i for i in range(n_buf)},
        compiler_params=pltpu.CompilerParams(has_side_effects=_DATAFLOW))(*hbm_bufs, *wait_sems, *deps)
    sems, thru, token = outs[:2 * n_start], outs[2 * n_start:-1], outs[-1]
    through = lambda plan: [tuple(thru[s] for s in item) for item in plan]
    waited = [through(plan) for _, plan in wait_plan]
    handles = [(through(plan), sems[2 * si], sems[2 * si + 1]) for si, (_, plan) in enumerate(start_plan)]
    return waited, handles, token


def gather_small(name, a, deps=()):
    def body(a_ref, *rest):
        o_ref, send_sems, recv_sems, local_sem = rest[len(deps):]
        me = _mesh_pos()[3]
        own = pltpu.make_async_copy(a_ref, o_ref.at[me], local_sem)
        own.start()
        copies = _copies("gather", [(a_ref, o_ref)], send_sems, recv_sems)
        for cp in copies:
            cp.start()
        for cp in copies:
            cp.wait_recv()
        for cp in copies:
            cp.wait_send()
        own.wait()

    return pl.pallas_call(
        body, name=name, in_specs=[_ANY] * (1 + len(deps)), out_specs=_ANY,
        out_shape=jax.ShapeDtypeStruct((N_DEV,) + a.shape, a.dtype),
        scratch_shapes=[pltpu.SemaphoreType.DMA((N_DEV - 1,)), pltpu.SemaphoreType.DMA((N_DEV - 1,)),
                        pltpu.SemaphoreType.DMA],
        compiler_params=pltpu.CompilerParams(has_side_effects=True))(a, *deps)


def _adamw_math(w, g, m, v):
    m2 = ADAM_B1 * m + (1.0 - ADAM_B1) * g
    v2 = ADAM_B2 * v + (1.0 - ADAM_B2) * (g * g)
    m_hat = m2 / (1.0 - ADAM_B1 ** ADAM_STEP)
    v_hat = v2 / (1.0 - ADAM_B2 ** ADAM_STEP)
    delta = -ADAM_LR * (m_hat / (jnp.sqrt(v_hat) + ADAM_EPS) + ADAM_WD * w)
    return delta, m2, v2


def adamw(name, w, m, v, l, land, own, prev=None):
    L, r, c = w.shape
    cp = land.shape[2]
    tr = _pick(r, (256, 176, 128, 64, 32, 16, 8))

    def body(w_ref, m_ref, v_ref, land_ref, own_ref, *rest):
        g_ref, d_ref, m2_ref, v2_ref = rest[-4:]
        me = _mesh_pos()[3]
        mine = own_ref[:, pl.ds(0, c)].astype(F32)
        g = None
        for s in range(N_DEV):
            part = jnp.where(me == s, mine, land_ref[s, :, pl.ds(0, c)].astype(F32))
            g = part if g is None else g + part
        delta, m2, v2 = _adamw_math(w_ref[...], g, m_ref[...], v_ref[...])
        g_ref[...] = g
        d_ref[...] = delta
        m2_ref[...] = m2
        v2_ref[...] = v2

    blk = pl.BlockSpec((None, tr, c), lambda i: (l, i, 0))
    shape = jax.ShapeDtypeStruct((L, r, c), F32)
    extra = [] if prev is None else list(prev)
    return pl.pallas_call(
        body, name=name, grid=(r // tr,),
        in_specs=[blk, blk, blk, pl.BlockSpec((N_DEV, tr, cp), lambda i: (0, i, 0)),
                  pl.BlockSpec((None, tr, cp), lambda i: (_mesh_pos()[3], i, 0))] + [_ANY] * len(extra),
        out_specs=[blk] * 4, out_shape=[shape] * 4,
        input_output_aliases={5 + k: k for k in range(len(extra))},
        compiler_params=_params("parallel"))(w, m, v, land, own, *extra)


def adamw_small(name, w, m, v, parts):
    n = w.shape[1]

    def body(w_ref, m_ref, v_ref, p_ref, g_ref, d_ref, m2_ref, v2_ref):
        g = p_ref[0:1, :]
        for s in range(1, N_DEV):
            g = g + p_ref[s:s + 1, :]
        delta, m2, v2 = _adamw_math(w_ref[...], g, m_ref[...], v_ref[...])
        g_ref[...] = g
        d_ref[...] = delta
        m2_ref[...] = m2
        v2_ref[...] = v2

    shape = jax.ShapeDtypeStruct((1, n), F32)
    return pl.pallas_call(body, name=name, out_shape=[shape] * 4,
                          compiler_params=pltpu.CompilerParams(vmem_limit_bytes=VMEM_LIMIT_BYTES))(w, m, v, parts)


def _rope_tables(positions):
    half = RET_DK // 2
    inv_freq = 1.0 / jnp.power(RET_THETA_BASE, jnp.linspace(0.0, 1.0, half, dtype=F32))
    ang = positions.astype(F32)[:, None] * inv_freq
    cos, sin = jnp.cos(ang), jnp.sin(ang)
    cosf = jnp.repeat(cos, 2, axis=-1)
    sins = jnp.stack([-sin, sin], axis=-1).reshape(cosf.shape)
    return cosf, sins


def _pad_to(a, axis, size):
    pad = [(0, 0)] * a.ndim
    pad[axis] = (0, size - a.shape[axis])
    return jnp.pad(a, pad)


def _round_up(n, m):
    return -(-n // m) * m


def kernel(x, p, positions, attn_norm_w, ffn_norm_w, ple_norm_w, final_norm_w, ab_w_in, ab_gla_gate_up, ab_gla_gate_b, ab_ret_norm_w, ab_gla_norm_w, ab_w_out, c_w_qkv, c_w_out, ffn_w_gate, ffn_w_up, ffn_w_down, ple_w_proj, ple_w_gate, loss_target, m_attn_norm_w, m_ffn_norm_w, m_ple_norm_w, m_final_norm_w, m_ab_w_in, m_ab_gla_gate_up, m_ab_gla_gate_b, m_ab_ret_norm_w, m_ab_gla_norm_w, m_ab_w_out, m_c_w_qkv, m_c_w_out, m_ffn_w_gate, m_ffn_w_up, m_ffn_w_down, m_ple_w_proj, m_ple_w_gate, v_attn_norm_w, v_ffn_norm_w, v_ple_norm_w, v_final_norm_w, v_ab_w_in, v_ab_gla_gate_up, v_ab_gla_gate_b, v_ab_ret_norm_w, v_ab_gla_norm_w, v_ab_w_out, v_c_w_qkv, v_c_w_out, v_ffn_w_gate, v_ffn_w_up, v_ffn_w_down, v_ple_w_proj, v_ple_w_gate):
    T, D = x.shape[1], x.shape[2]
    depth = attn_norm_w.shape[0]
    assert ab_w_in.shape[0] == 1 and c_w_qkv.shape[0] == 1 and depth == 2, "one even and one odd layer"
    me = 4 * lax.axis_index("x") + 2 * lax.axis_index("y") + lax.axis_index("c")
    in_shard = ab_w_in.shape[2]
    in_width = in_shard * N_DEV
    assert in_width == OFF_LR + GLA_GATE_RANK
    fs = ffn_w_gate.shape[2]
    fp = _round_up(fs, LANE)
    gu_cols = ab_gla_gate_up.shape[2]

    bf = lambda a: a.astype(BF16)
    tr_ = lambda a: jnp.swapaxes(a, -1, -2)
    wg_t, wu_t = tr_(ffn_w_gate), tr_(ffn_w_up)
    srcs = {"w_in": bf(tr_(ab_w_in[0])), "w_oab": bf(ab_w_out[0]), "gu": ab_gla_gate_up[0],
            "w_qkv": bf(c_w_qkv[0]), "w_oc": bf(c_w_out[0])}
    for l in range(depth):
        srcs[f"wg{l}"] = _pad_to(bf(wg_t[l]), 0, fp)
        srcs[f"wu{l}"] = _pad_to(bf(wu_t[l]), 0, fp)
        srcs[f"wd{l}"] = _pad_to(bf(ffn_w_down[l]), 0, fp)
        srcs[f"wpg{l}"] = bf(ple_w_gate[l])
        srcs[f"wpp{l}"] = bf(ple_w_proj[l])
    group_keys = [["w_in", "w_oab", "gu"], ["wg0", "wu0"], ["wd0", "wpg0", "wpp0"], ["w_qkv", "w_oc"],
                  ["wg1", "wu1"], ["wd1", "wpg1", "wpp1"]]

    def landing(a):
        return lax.dynamic_update_slice(lax.empty((N_DEV,) + a.shape, a.dtype), a[None], (me,) + (0,) * a.ndim)

    _, near_handles, gather_token = exchange_call(
        "gather_start", [], [("to_near", [(srcs[k], landing(srcs[k])) for k in keys]) for keys in group_keys])
    weights = {}

    def gather_wait(gi, dep):
        lands = [(land,) for _, land in near_handles[gi][0]]
        _, (relay, pass_near), _ = exchange_call(
            f"gather{gi}_relay", [("to_near", near_handles[gi])], [("relay", lands), ("pass_near", lands)], deps=(dep,))
        _, (pass_far,), _ = exchange_call(f"gather{gi}_pass", [("relay", relay)], [("pass_far", relay[0])])
        (_, complete), _, _ = exchange_call(
            f"gather{gi}_done", [("pass_near", (pass_far[0],) + pass_near[1:]), ("pass_far", pass_far)], [])
        weights.update(zip(group_keys[gi], [land for (land,) in complete]))

    gb = ab_gla_gate_b
    hn_w = jnp.concatenate([ab_ret_norm_w, ab_gla_norm_w], axis=1)
    cosf, sins = _rope_tables(positions[0])
    p_bf = bf(p[:, 0])

    xs = x[0]
    saved = []
    for i in range(depth):
        nm = f"l{i}_"
        w_attn, w_ffn, w_ple = attn_norm_w[i:i + 1], ffn_norm_w[i:i + 1], ple_norm_w[i:i + 1]
        (xn,) = rowwise(nm + "norm_attn", lambda a, w: (_rms(a, w),), T, [("row", xs), ("full", w_attn)],
                        [("row", D, BF16)], deps=(gather_token,) if i == 0 else ())
        gather_wait(3 * i, xn)
        if i % 2 == 0:
            w_in_t = weights["w_in"].reshape(1, 1, in_width, D)
            w_lr_t = _pad_to(w_in_t[0, 0, OFF_LR:], 0, LANE).reshape(1, 1, LANE, D)
            w_oab = weights["w_oab"].reshape(1, 1, D, D)
            gu_full = _pad_to(weights["gu"].transpose(1, 0, 2).reshape(GLA_GATE_RANK, GLA_QK), 0, LANE)
            z = mmt_fwd(nm + "mm_in", xn, w_in_t, 0, F32, n=OFF_LR)
            glr = mmt_fwd(nm + "mm_lr", xn, w_lr_t, 0, F32)
            oraw = retention_fwd(nm + "ret_fwd", z, cosf, sins, RET_V + GLA_V)
            oraw = gla_fwd(nm + "gla_fwd", z, glr, gu_full, gb, oraw)
            o = headnorm_fwd(nm + "headnorm_fwd", oraw, z, hn_w)
            mix = mm_nn(nm + "mm_out", o, w_oab, 0, F32)
            mixer_saved = (z, glr, oraw, o)
        else:
            w_qkv = weights["w_qkv"].reshape((1,) + weights["w_qkv"].shape)
            w_oc = weights["w_oc"].reshape(1, 1, D, D)
            qkv = mm_nn(nm + "mm_qkv", xn, w_qkv, 0, BF16)
            o, lse = attn_fwd(nm + "attn_fwd", qkv)
            mix = mm_nn(nm + "mm_out", o, w_oc, 0, F32)
            mixer_saved = (qkv, o, lse)
        h1, hn = rowwise(nm + "add_norm_ffn", lambda a, b, w: (a + b, _rms(a + b, w)), T,
                         [("row", xs), ("row", mix), ("full", w_ffn)], [("row", D, F32), ("row", D, BF16)])
        gather_wait(3 * i + 1, hn)
        wg = weights[f"wg{i}"].reshape(1, N_DEV, fp, D)
        wu = weights[f"wu{i}"].reshape(1, N_DEV, fp, D)
        g, u, act = ffn_gate_up(nm + "ffn_gate_up", hn, wg, wu)
        gather_wait(3 * i + 2, act)
        wd = weights[f"wd{i}"].reshape(1, 1, N_DEV * fp, D)
        wpg = weights[f"wpg{i}"].reshape(1, 1, D, D)
        wpp = weights[f"wpp{i}"].reshape((1,) + weights[f"wpp{i}"].shape)
        f = mm_nn(nm + "mm_down", act, wd, 0, F32)
        h2, pn = rowwise(nm + "add_norm_ple", lambda a, b, w: (a + b, _rms(a + b, w)), T,
                         [("row", h1), ("row", f), ("full", w_ple)], [("row", D, F32), ("row", D, BF16)])
        s = mm_nn(nm + "mm_ple_gate", pn, wpg, 0, F32)
        e = mm_nn(nm + "mm_ple_proj", p_bf[i], wpp, 0, F32)
        (x_next,) = rowwise(nm + "ple_out", lambda a, b, c: (a + _sigmoid(b) * c,), T,
                            [("row", h2), ("row", s), ("row", e)], [("row", D, F32)])
        mixer_w = (w_in_t, w_lr_t, w_oab, gu_full) if i % 2 == 0 else (w_qkv, w_oc)
        saved.append((xs, xn, mixer_saved, mixer_w, (wg, wu, wd, wpg), h1, hn, g, u, act, h2, pn, s, e))
        xs = x_next

    def loss_fn(a, w, t):
        diff = _rms(a, w) - t
        dx, dw = _rms_bwd(a, w, diff * (1.0 / D))
        part = 0.5 * jnp.sum(jnp.mean(diff * diff, axis=-1, keepdims=True), axis=0, keepdims=True)
        return dx, dw, jnp.broadcast_to(part, (1, LANE))

    dx, d_final_w, loss_part = rowwise("loss_head", loss_fn, T,
                                       [("row", xs), ("full", final_norm_w[None, :]), ("row", loss_target[0])],
                                       [("row", D, F32), ("acc", D), ("acc", LANE)])
    loss = lax.psum(loss_part[0, 0], ("x", "y", "c"))

    grads = {}
    scatters = []

    def scatter_start(name, keys):
        _, handles, token = exchange_call(
            name, [], [("scatter", [(grads[k], lax.empty(grads[k].shape, BF16)) for k in keys])])
        scatters.append((keys, handles[0]))
        return token

    d_attn_w, d_ffn_w, d_ple_w = [None] * depth, [None] * depth, [None] * depth
    for i in reversed(range(depth)):
        nm = f"l{i}_b_"
        xs_i, xn, mixer_saved, mixer_w, (wg, wu, wd, wpg), h1, hn, g, u, act, h2, pn, s, e = saved[i]
        w_attn, w_ffn, w_ple = attn_norm_w[i:i + 1], ffn_norm_w[i:i + 1], ple_norm_w[i:i + 1]

        def ple_bwd(d, sv, ev):
            gate = _sigmoid(sv)
            return d * gate, d * ev * gate * (1.0 - gate)

        de, ds = rowwise(nm + "ple_out", ple_bwd, T, [("row", dx), ("row", s), ("row", e)],
                         [("row", D, BF16), ("row", D, BF16)])
        grads[("ple_w_proj", i)] = mm_tn(nm + "mm_ple_proj_w", p_bf[i], de, N_DEV, BF16)
        grads[("ple_w_gate", i)] = mm_tn(nm + "mm_ple_gate_w", pn, ds, 1, BF16).reshape(N_DEV, D // N_DEV, D)
        dpn = mm_nt(nm + "mm_ple_gate_x", ds, wpg, 0, F32)

        def norm_bwd_add(a, w, dn, dres):
            dxx, dw = _rms_bwd(a, w, dn)
            tot = dres + dxx
            return tot, tot, dw

        dh2, dh2_bf, d_ple_w[i] = rowwise(nm + "norm_ple", norm_bwd_add, T,
                                          [("row", h2), ("full", w_ple), ("row", dpn), ("row", dx)],
                                          [("row", D, F32), ("row", D, BF16), ("acc", D)])
        grads[("ffn_w_down", i)] = mm_tn(nm + "mm_down_w", act, dh2_bf, 1, BF16).reshape(N_DEV, fp, D)
        token = scatter_start(nm + "scatter_ple_down", [("ple_w_proj", i), ("ple_w_gate", i), ("ffn_w_down", i)])
        dg, du = ffn_down_bwd(nm + "ffn_down_x", dh2_bf, wd, g, u, deps=(token,))
        grads[("ffn_w_gate", i)] = mmt_dw(nm + "mm_gate_w", dg, hn, N_DEV, BF16)
        grads[("ffn_w_up", i)] = mmt_dw(nm + "mm_up_w", du, hn, N_DEV, BF16)
        token = scatter_start(nm + "scatter_gate_up", [("ffn_w_gate", i), ("ffn_w_up", i)])
        dhn_g = mmt_dx(nm + "mm_gate_x", dg, wg, 0, F32, deps=(token,))
        dhn_u = mmt_dx(nm + "mm_up_x", du, wu, 0, F32)

        def norm_bwd_add2(a, w, dn1, dn2, dres):
            dxx, dw = _rms_bwd(a, w, dn1 + dn2)
            tot = dres + dxx
            return tot, tot, dw

        dh1, dh1_bf, d_ffn_w[i] = rowwise(nm + "norm_ffn", norm_bwd_add2, T,
                                          [("row", h1), ("full", w_ffn), ("row", dhn_g), ("row", dhn_u), ("row", dh2)],
                                          [("row", D, F32), ("row", D, BF16), ("acc", D)])
        if i % 2 == 0:
            z, glr, oraw, o = mixer_saved
            w_in_t, w_lr_t, w_oab, gu_full = mixer_w
            grads[("ab_w_out", 0)] = mm_tn(nm + "mm_out_w", o, dh1_bf, 1, BF16).reshape(N_DEV, D // N_DEV, D)
            token = scatter_start(nm + "scatter_out", [("ab_w_out", 0)])
            do = mm_nt(nm + "mm_out_x", dh1_bf, w_oab, 0, F32, deps=(token,))
            d_oraw, d_gates, d_hn_w = headnorm_bwd(nm + "headnorm", oraw, z, hn_w, do)
            d_rq, d_rk, d_rv = retention_bwd(nm + "ret", z, cosf, sins, d_oraw)
            d_gq, d_gk, d_gv, d_glr4, d_gu, d_gb = gla_bwd(nm + "gla", z, glr, gu_full, gb, d_oraw)
            dz = jnp.concatenate([d_rq, d_rk, d_rv, d_gates[:, :RET_V], d_gq, d_gk, d_gv, d_gates[:, RET_V:]], axis=1)
            (d_glr,) = rowwise(nm + "sum_lr", lambda *a: (a[0] + a[1] + a[2] + a[3],), T,
                               [("row", d_glr4[hh]) for hh in range(GLA_HEADS)], [("row", LANE, BF16)])
            dw_main = mm_tn(nm + "mm_in_w", xn, dz, 1, BF16)[0]
            dw_lr = mm_tn(nm + "mm_lr_w", xn, d_glr, 1, BF16)[0]
            dw_in = jnp.concatenate([dw_main, dw_lr[:, :GLA_GATE_RANK]], axis=1)
            grads[("ab_w_in", 0)] = dw_in.reshape(D, N_DEV, in_shard).transpose(1, 0, 2)
            token = scatter_start(nm + "scatter_in", [("ab_w_in", 0)])
            dxn_a = mmt_dx(nm + "mm_in_x", dz, w_in_t, 0, F32, n=OFF_LR, deps=(token,))
            dxn_b = mmt_dx(nm + "mm_lr_x", d_glr, w_lr_t, 0, F32)
        else:
            qkv, o, lse = mixer_saved
            w_qkv, w_oc = mixer_w
            grads[("c_w_out", 0)] = mm_tn(nm + "mm_out_w", o, dh1_bf, 1, BF16).reshape(N_DEV, D // N_DEV, D)
            do = mm_nt(nm + "mm_out_x", dh1_bf, w_oc, 0, BF16)
            dq, dk, dv = attn_bwd(nm + "attn", qkv, o, lse, do)
            dqkv = jnp.concatenate([dq, dk, dv], axis=1)
            grads[("c_w_qkv", 0)] = mm_tn(nm + "mm_qkv_w", xn, dqkv, N_DEV, BF16)
            token = scatter_start(nm + "scatter_attn", [("c_w_out", 0), ("c_w_qkv", 0)])
            dxn_a = mm_nt(nm + "mm_qkv_x", dqkv, w_qkv, 0, F32, deps=(token,))
            dxn_b = None
        if dxn_b is None:
            dx, _, d_attn_w[i] = rowwise(nm + "norm_attn", norm_bwd_add, T,
                                         [("row", xs_i), ("full", w_attn), ("row", dxn_a), ("row", dh1)],
                                         [("row", D, F32), ("row", D, BF16), ("acc", D)])
        else:
            dx, _, d_attn_w[i] = rowwise(nm + "norm_attn", norm_bwd_add2, T,
                                         [("row", xs_i), ("full", w_attn), ("row", dxn_a), ("row", dxn_b), ("row", dh1)],
                                         [("row", D, F32), ("row", D, BF16), ("acc", D)])

    small_names = ["attn_norm_w", "ffn_norm_w", "ple_norm_w", "final_norm_w", "ab_gla_gate_b", "ab_ret_norm_w",
                   "ab_gla_norm_w"]
    small_grads = [jnp.concatenate(d_attn_w, 0), jnp.concatenate(d_ffn_w, 0), jnp.concatenate(d_ple_w, 0), d_final_w[0],
                   d_gb, d_hn_w[:, :RET_V], d_hn_w[:, RET_V:]]
    small_w = [attn_norm_w, ffn_norm_w, ple_norm_w, final_norm_w, ab_gla_gate_b, ab_ret_norm_w, ab_gla_norm_w]
    small_m = [m_attn_norm_w, m_ffn_norm_w, m_ple_norm_w, m_final_norm_w, m_ab_gla_gate_b, m_ab_ret_norm_w, m_ab_gla_norm_w]
    small_v = [v_attn_norm_w, v_ffn_norm_w, v_ple_norm_w, v_final_norm_w, v_ab_gla_gate_b, v_ab_ret_norm_w, v_ab_gla_norm_w]
    sizes = [int(np.prod(a.shape)) for a in small_w]
    n_gu = GLA_GATE_RANK * GLA_QK
    n_small = _round_up(sum(sizes) + n_gu, LANE)
    pack = lambda parts: _pad_to(jnp.concatenate([a.reshape(-1) for a in parts]), 0, n_small)[None, :]
    small_part = pack(small_grads + [d_gu[:GLA_GATE_RANK]])

    big_w = dict(ab_w_in=(ab_w_in, m_ab_w_in, v_ab_w_in), ab_w_out=(ab_w_out, m_ab_w_out, v_ab_w_out),
                 c_w_qkv=(c_w_qkv, m_c_w_qkv, v_c_w_qkv), c_w_out=(c_w_out, m_c_w_out, v_c_w_out),
                 ffn_w_gate=(wg_t, tr_(m_ffn_w_gate), tr_(v_ffn_w_gate)),
                 ffn_w_up=(wu_t, tr_(m_ffn_w_up), tr_(v_ffn_w_up)),
                 ffn_w_down=(ffn_w_down, m_ffn_w_down, v_ffn_w_down), ple_w_proj=(ple_w_proj, m_ple_w_proj, v_ple_w_proj),
                 ple_w_gate=(ple_w_gate, m_ple_w_gate, v_ple_w_gate))
    results, last = {}, dx
    for gi, (keys, handle) in enumerate(scatters):
        (arrived,), _, _ = exchange_call(f"scatter_wait{gi}", [("scatter", handle)], [], deps=(last,))
        for (n, l), (own, land) in zip(keys, arrived):
            results[n] = adamw(f"adamw_{n}{l}", *big_w[n], l, land, own, prev=results.get(n))
            last = results[n][0]
    for n in ("ffn_w_gate", "ffn_w_up"):
        results[n] = [tr_(a) for a in results[n]]
    small_parts = gather_small("gather_small", small_part, deps=(last,)).reshape(N_DEV, n_small)

    gu_off = sum(sizes)
    own_cols = lambda a: lax.dynamic_slice_in_dim(a.reshape(GLA_GATE_RANK, GLA_QK), me * gu_cols, gu_cols, axis=1)
    small_res = adamw_small("adamw_small", pack(small_w + [jnp.zeros((n_gu,), F32)]),
                            pack(small_m + [jnp.zeros((n_gu,), F32)]), pack(small_v + [jnp.ones((n_gu,), F32)]),
                            small_parts)
    g_gu_full = small_res[0][0, gu_off:gu_off + n_gu]
    g_gu = own_cols(g_gu_full)[None]
    gu_res = adamw_small("adamw_gate_up", *[_pad_to(a.reshape(1, -1), 1, _round_up(a.size, LANE)) for a in
                                            (ab_gla_gate_up, m_ab_gla_gate_up, v_ab_gla_gate_up)],
                         jnp.concatenate([_pad_to(g_gu.reshape(1, -1), 1, _round_up(g_gu.size, LANE)),
                                          jnp.zeros((N_DEV - 1, _round_up(g_gu.size, LANE)), F32)], axis=0))
    for k in range(4):
        off = 0
        for n, a, sz in zip(small_names, small_w, sizes):
            results.setdefault(n, [None] * 4)[k] = small_res[k][0, off:off + sz].reshape(a.shape)
            off += sz
        results.setdefault("ab_gla_gate_up", [None] * 4)[k] = gu_res[k][0, :g_gu.size].reshape(ab_gla_gate_up.shape)

    order = ["attn_norm_w", "ffn_norm_w", "ple_norm_w", "final_norm_w", "ab_w_in", "ab_gla_gate_up", "ab_gla_gate_b",
             "ab_ret_norm_w", "ab_gla_norm_w", "ab_w_out", "c_w_qkv", "c_w_out", "ffn_w_gate", "ffn_w_up", "ffn_w_down",
             "ple_w_proj", "ple_w_gate"]
    return (loss, dx[None], *[results[n][0] for n in order], *[results[n][1] for n in order],
            *[results[n][2] for n in order], *[results[n][3] for n in order])
```

```python
import math

import numpy as np
import jax
import jax.numpy as jnp
from jax import lax
from jax.experimental import pallas as pl
from jax.experimental.pallas import tpu as pltpu

F32 = jnp.float32
BF16 = jnp.bfloat16
HIGHEST = lax.Precision.HIGHEST

N_DEV = 8
VMEM_LIMIT_BYTES = 48 * 1024 * 1024
LANE = 128
NORM_EPS = 1e-6

RET_HEADS, RET_DK, RET_DV = 4, 256, 256
RET_THETA_BASE = 10000.0
GLA_HEADS, GLA_DK, GLA_DV = 4, 128, 256
GLA_GATE_RANK = 16
GLA_GATE_NORM = 16.0
CHUNK = 64
ATT_HEADS = 16
DILATED_BRANCHES = ((128, 1), (512, 4), (2048, 16))
BLK = 256

ADAM_LR, ADAM_B1, ADAM_B2, ADAM_EPS, ADAM_WD, ADAM_STEP = 0.001, 0.9, 0.999, 1e-08, 0.01, 10

RET_QK = RET_HEADS * RET_DK
RET_V = RET_HEADS * RET_DV
GLA_QK = GLA_HEADS * GLA_DK
GLA_V = GLA_HEADS * GLA_DV
OFF_RQ, OFF_RK, OFF_RV, OFF_RG = 0, RET_QK, 2 * RET_QK, 2 * RET_QK + RET_V
OFF_GQ = OFF_RG + RET_V
OFF_GK = OFF_GQ + GLA_QK
OFF_GV = OFF_GK + GLA_QK
OFF_GG = OFF_GV + GLA_V
OFF_LR = OFF_GG + GLA_V


def _params(*sem):
    return pltpu.CompilerParams(dimension_semantics=sem or None, vmem_limit_bytes=VMEM_LIMIT_BYTES)


def _pick(n, cands):
    for c in cands:
        if n % c == 0:
            return c
    raise ValueError(f"no tile for {n} in {cands}")


_NN = (((1,), (0,)), ((), ()))
_NT = (((1,), (1,)), ((), ()))
_TN = (((0,), (0,)), ((), ()))
_ANY = pl.BlockSpec(memory_space=pl.ANY)
MAX_CONTRACT = 2048
_TILES = (1024, 768, 512, 256, 128)


def _mm_call(name, dims, grid, in_specs, out_spec, out_shape, args, deps=()):
    steps = grid[2]
    assert steps == 1 or out_shape.dtype == F32

    def body(a_ref, b_ref, *rest):
        o_ref = rest[len(deps)]
        part = lax.dot_general(a_ref[...].astype(BF16), b_ref[...].astype(BF16), dims, preferred_element_type=F32)
        if steps == 1:
            o_ref[...] = part.astype(o_ref.dtype)
        else:
            _accumulate(o_ref, part, pl.program_id(2) == 0)

    return pl.pallas_call(
        body, name=name, grid=grid, in_specs=list(in_specs) + [_ANY] * len(deps), out_specs=out_spec,
        out_shape=out_shape, compiler_params=_params("parallel", "parallel", "arbitrary"))(*args, *deps)


def mm_nn(name, a, w, l, out_dtype, deps=()):
    _, J, K, n = w.shape
    M = a.shape[0]
    tm, tn, tk = _pick(M, _TILES), _pick(n, _TILES), _pick(K, (MAX_CONTRACT,) + _TILES)
    nt = n // tn
    return _mm_call(
        name, _NN, (M // tm, J * nt, K // tk),
        [pl.BlockSpec((tm, tk), lambda i, j, k: (i, k)),
         pl.BlockSpec((None, None, tk, tn), lambda i, j, k: (l, j // nt, k, j % nt))],
        pl.BlockSpec((tm, tn), lambda i, j, k: (i, j)),
        jax.ShapeDtypeStruct((M, J * n), out_dtype), (a, w), deps)


def mm_nt(name, a, w, l, out_dtype, deps=()):
    _, J, K, n = w.shape
    M = a.shape[0]
    tm, tq, tc = _pick(M, _TILES), _pick(K, _TILES), _pick(n, (MAX_CONTRACT,) + _TILES)
    nc = n // tc
    return _mm_call(
        name, _NT, (M // tm, K // tq, J * nc),
        [pl.BlockSpec((tm, tc), lambda i, q, c: (i, c)),
         pl.BlockSpec((None, None, tq, tc), lambda i, q, c: (l, c // nc, q, c % nc))],
        pl.BlockSpec((tm, tq), lambda i, q, c: (i, q)),
        jax.ShapeDtypeStruct((M, K), out_dtype), (a, w), deps)


def mm_tn(name, x, dy, J, out_dtype, deps=()):
    M, K = x.shape
    n = dy.shape[1] // J
    tp, tn = _pick(K, _TILES), _pick(n, _TILES)
    nt = n // tn
    assert M <= MAX_CONTRACT
    return _mm_call(
        name, _TN, (K // tp, J * nt, 1),
        [pl.BlockSpec((M, tp), lambda i, j, r: (0, i)),
         pl.BlockSpec((M, tn), lambda i, j, r: (0, j))],
        pl.BlockSpec((None, tp, tn), lambda i, j, r: (j // nt, i, j % nt)),
        jax.ShapeDtypeStruct((J, K, n), out_dtype), (x, dy), deps)


def mmt_fwd(name, a, wt, l, out_dtype, n=None, deps=()):
    _, J, rows, K = wt.shape
    n = rows if n is None else n
    M = a.shape[0]
    tm, tn = _pick(M, _TILES), _pick(n, _TILES)
    nt = n // tn
    assert K <= MAX_CONTRACT
    return _mm_call(
        name, _NT, (M // tm, J * nt, 1),
        [pl.BlockSpec((tm, K), lambda i, j, k: (i, 0)),
         pl.BlockSpec((None, None, tn, K), lambda i, j, k: (l, j // nt, j % nt, 0))],
        pl.BlockSpec((tm, tn), lambda i, j, k: (i, j)),
        jax.ShapeDtypeStruct((M, J * n), out_dtype), (a, wt), deps)


def mmt_dx(name, dy, wt, l, out_dtype, n=None, deps=()):
    _, J, rows, K = wt.shape
    n = rows if n is None else n
    M = dy.shape[0]
    tm, tq, tc = _pick(M, _TILES), _pick(K, _TILES), _pick(n, _TILES)
    nc = n // tc
    return _mm_call(
        name, _NN, (M // tm, K // tq, J * nc),
        [pl.BlockSpec((tm, tc), lambda i, q, c: (i, c)),
         pl.BlockSpec((None, None, tc, tq), lambda i, q, c: (l, c // nc, c % nc, q))],
        pl.BlockSpec((tm, tq), lambda i, q, c: (i, q)),
        jax.ShapeDtypeStruct((M, K), out_dtype), (dy, wt), deps)


def mmt_dw(name, dy, x, J, out_dtype, deps=()):
    M, K = x.shape
    n = dy.shape[1] // J
    tn, tp = _pick(n, _TILES), _pick(K, _TILES)
    nt = n // tn
    assert M <= MAX_CONTRACT
    return _mm_call(
        name, _TN, (J * nt, K // tp, 1),
        [pl.BlockSpec((M, tn), lambda j, i, r: (0, j)),
         pl.BlockSpec((M, tp), lambda j, i, r: (0, i))],
        pl.BlockSpec((None, tn, tp), lambda j, i, r: (j // nt, j % nt, i)),
        jax.ShapeDtypeStruct((J, n, K), out_dtype), (dy, x), deps)


def ffn_gate_up(name, a, wg, wu):
    _, J, n, K = wg.shape
    M = a.shape[0]
    tm, tn = _pick(M, _TILES), _pick(n, _TILES)
    nt = n // tn
    assert K <= MAX_CONTRACT

    def body(a_ref, wg_ref, wu_ref, g_ref, u_ref, act_ref):
        x = a_ref[...]
        g = lax.dot_general(x, wg_ref[...], _NT, preferred_element_type=F32)
        u = lax.dot_general(x, wu_ref[...], _NT, preferred_element_type=F32)
        g_ref[...] = g.astype(g_ref.dtype)
        u_ref[...] = u.astype(u_ref.dtype)
        act_ref[...] = (_silu_and_grad(g)[0] * u).astype(act_ref.dtype)

    w_spec = pl.BlockSpec((None, None, tn, K), lambda i, j: (0, j // nt, j % nt, 0))
    out = pl.BlockSpec((tm, tn), lambda i, j: (i, j))
    return pl.pallas_call(
        body, name=name, grid=(M // tm, J * nt),
        in_specs=[pl.BlockSpec((tm, K), lambda i, j: (i, 0)), w_spec, w_spec],
        out_specs=[out] * 3, out_shape=[jax.ShapeDtypeStruct((M, J * n), BF16)] * 3,
        compiler_params=_params("parallel", "parallel"))(a, wg, wu)


def ffn_down_bwd(name, dy, wd, g, u, deps=()):
    _, _, K, n = wd.shape
    M = dy.shape[0]
    tm, tq = _pick(M, _TILES), _pick(K, _TILES)
    assert n <= MAX_CONTRACT

    def body(dy_ref, w_ref, g_ref, u_ref, *rest):
        dg_ref, du_ref = rest[len(deps):]
        dact = lax.dot_general(dy_ref[...], w_ref[...], _NT, preferred_element_type=F32)
        silu, dsilu = _silu_and_grad(g_ref[...].astype(F32))
        dg_ref[...] = (dact * u_ref[...].astype(F32) * dsilu).astype(dg_ref.dtype)
        du_ref[...] = (dact * silu).astype(du_ref.dtype)

    blk = pl.BlockSpec((tm, tq), lambda i, q: (i, q))
    return pl.pallas_call(
        body, name=name, grid=(M // tm, K // tq),
        in_specs=[pl.BlockSpec((tm, n), lambda i, q: (i, 0)),
                  pl.BlockSpec((None, None, tq, n), lambda i, q: (0, 0, q, 0)), blk, blk] + [_ANY] * len(deps),
        out_specs=[blk, blk], out_shape=[jax.ShapeDtypeStruct((M, K), BF16)] * 2,
        compiler_params=_params("parallel", "parallel"))(dy, wd, g, u, *deps)


def rowwise(name, fn, rows, ins, outs, tr=256, deps=()):
    widest = max([s[1].shape[1] if s[0] != "col" else s[3] for s in ins] + [s[1] for s in outs])
    tr = min(tr if widest <= 2048 else tr // 2, rows)
    in_specs, args = [], []
    for spec in ins:
        kind, a = spec[0], spec[1]
        if kind == "row":
            in_specs.append(pl.BlockSpec((tr, a.shape[1]), lambda i: (i, 0)))
        elif kind == "col":
            cb, width = spec[2], spec[3]
            in_specs.append(pl.BlockSpec((tr, width), lambda i, cb=cb: (i, cb)))
        else:
            in_specs.append(pl.BlockSpec(a.shape, lambda i: (0, 0)))
        args.append(a)
    out_specs, out_shapes = [], []
    for spec in outs:
        if spec[0] == "row":
            out_specs.append(pl.BlockSpec((tr, spec[1]), lambda i: (i, 0)))
            out_shapes.append(jax.ShapeDtypeStruct((rows, spec[1]), spec[2]))
        else:
            out_specs.append(pl.BlockSpec((1, spec[1]), lambda i: (0, 0)))
            out_shapes.append(jax.ShapeDtypeStruct((1, spec[1]), F32))
    n_in = len(ins)

    def body(*refs):
        vals = fn(*[r[...] for r in refs[:n_in]])
        first = pl.program_id(0) == 0
        for r, v, spec in zip(refs[n_in + len(deps):], vals, outs):
            if spec[0] == "row":
                r[...] = v.astype(r.dtype)
            else:
                _accumulate(r, v, first)

    return pl.pallas_call(body, name=name, grid=(rows // tr,), in_specs=in_specs + [_ANY] * len(deps),
                          out_specs=out_specs, out_shape=out_shapes,
                          compiler_params=_params("arbitrary"))(*args, *deps)


def _accumulate(ref, v, first):
    @pl.when(first)
    def _():
        ref[...] = v

    @pl.when(jnp.logical_not(first))
    def _():
        ref[...] += v


def _rms(x, w):
    r = lax.rsqrt(jnp.mean(x * x, axis=-1, keepdims=True) + NORM_EPS)
    return x * r * w


def _rms_bwd(x, w, dy):
    r = lax.rsqrt(jnp.mean(x * x, axis=-1, keepdims=True) + NORM_EPS)
    g = dy * w
    dx = r * (g - x * (r * r) * jnp.mean(g * x, axis=-1, keepdims=True))
    dw = jnp.sum(dy * x * r, axis=0, keepdims=True)
    return dx, dw


def _sigmoid(x):
    return 1.0 / (1.0 + jnp.exp(-x))


def _silu_and_grad(g):
    s = _sigmoid(g)
    return g * s, s * (1.0 + g * (1.0 - s))


def _swap_pairs(x):
    n = x.shape[-1]
    lane = lax.broadcasted_iota(jnp.int32, x.shape, x.ndim - 1)
    return jnp.where((lane & 1) == 0, pltpu.roll(x, n - 1, x.ndim - 1), pltpu.roll(x, 1, x.ndim - 1))


def _rot(x, cosf, sins):
    return x * cosf + _swap_pairs(x) * sins


def _unrot(d, cosf, sins):
    return d * cosf + _swap_pairs(d * sins)


def _ret_log_gamma(h):
    vals = [math.log1p(-2.0 ** (-5.0 - i)) for i in range(RET_HEADS)]
    out = jnp.float32(vals[RET_HEADS - 1])
    for i in range(RET_HEADS - 2, -1, -1):
        out = jnp.where(h == i, jnp.float32(vals[i]), out)
    return out


def _fill_decays(dec_ref, lg):
    ri = lax.broadcasted_iota(jnp.int32, (BLK, BLK), 0)
    ci = lax.broadcasted_iota(jnp.int32, (BLK, BLK), 1)
    for d in range(dec_ref.shape[0]):
        dt = d * BLK + ri - ci
        dec_ref[d] = jnp.where(dt >= 0, jnp.exp(jnp.maximum(dt, 0).astype(F32) * lg), 0.0)


def _decay_row(dec_ref, qi):
    return jnp.concatenate([dec_ref[qi - kb] for kb in range(qi + 1)], axis=1)


def _once(block_shape, index_map):
    return pl.BlockSpec(block_shape, index_map, pipeline_mode=pl.Buffered(1))


def _dot(a, b):
    return jnp.dot(a.astype(BF16), b.astype(BF16), preferred_element_type=F32)


def _dot_nt(a, b):
    return lax.dot_general(a.astype(BF16), b.astype(BF16), _NT, preferred_element_type=F32)


def _dot_tn(a, b):
    return lax.dot_general(a.astype(BF16), b.astype(BF16), _TN, preferred_element_type=F32)


def retention_fwd(name, z, cosf, sins, width_out):
    T = z.shape[0]
    nq = T // BLK
    scale = RET_DK ** -0.5

    def body(q_ref, k_ref, v_ref, cos_ref, sin_ref, o_ref, krot, vb, dec_ref):
        _fill_decays(dec_ref, _ret_log_gamma(pl.program_id(0)))
        krot[...] = (_rot(k_ref[...], cos_ref[...], sin_ref[...]) * scale).astype(BF16)
        vb[...] = v_ref[...].astype(BF16)
        for qi in range(nq):
            rows, n = slice(qi * BLK, (qi + 1) * BLK), (qi + 1) * BLK
            q = _rot(q_ref[rows, :], cos_ref[rows, :], sin_ref[rows, :])
            s = _dot_nt(q, krot[0:n, :]) * _decay_row(dec_ref, qi)
            o_ref[rows, :] = _dot(s, vb[0:n, :])

    return pl.pallas_call(
        body, name=name, grid=(RET_HEADS,),
        in_specs=[pl.BlockSpec((T, RET_DK), lambda h: (0, OFF_RQ // RET_DK + h)),
                  pl.BlockSpec((T, RET_DK), lambda h: (0, OFF_RK // RET_DK + h)),
                  pl.BlockSpec((T, RET_DV), lambda h: (0, OFF_RV // RET_DV + h)),
                  _once((T, RET_DK), lambda h: (0, 0)), _once((T, RET_DK), lambda h: (0, 0))],
        out_specs=pl.BlockSpec((T, RET_DV), lambda h: (0, h)),
        out_shape=jax.ShapeDtypeStruct((T, width_out), F32),
        scratch_shapes=[pltpu.VMEM((T, RET_DK), BF16), pltpu.VMEM((T, RET_DV), BF16),
                        pltpu.VMEM((nq, BLK, BLK), F32)],
        compiler_params=_params("arbitrary"))(z, z, z, cosf, sins)


def retention_bwd(name, z, cosf, sins, do):
    T = z.shape[0]
    nq = T // BLK
    scale = RET_DK ** -0.5

    def body(q_ref, k_ref, v_ref, cos_ref, sin_ref, do_ref, dq_ref, dk_ref, dv_ref, krot, vb, dk_acc, dv_acc, dec_ref):
        _fill_decays(dec_ref, _ret_log_gamma(pl.program_id(0)))
        krot[...] = (_rot(k_ref[...], cos_ref[...], sin_ref[...]) * scale).astype(BF16)
        vb[...] = v_ref[...].astype(BF16)
        dk_acc[...] = jnp.zeros_like(dk_acc)
        dv_acc[...] = jnp.zeros_like(dv_acc)
        for qi in range(nq):
            rows, n = slice(qi * BLK, (qi + 1) * BLK), (qi + 1) * BLK
            cos_q, sin_q = cos_ref[rows, :], sin_ref[rows, :]
            q = _rot(q_ref[rows, :], cos_q, sin_q).astype(BF16)
            dout = do_ref[rows, :].astype(BF16)
            kk, vv, dec = krot[0:n, :], vb[0:n, :], _decay_row(dec_ref, qi)
            p = (_dot_nt(q, kk) * dec).astype(BF16)
            ds = (_dot_nt(dout, vv) * dec).astype(BF16)
            dq_ref[rows, :] = _unrot(_dot(ds, kk), cos_q, sin_q).astype(dq_ref.dtype)
            dk_acc[0:n, :] += _dot_tn(ds, q)
            dv_acc[0:n, :] += _dot_tn(p, dout)
        dk_ref[...] = (_unrot(dk_acc[...], cos_ref[...], sin_ref[...]) * scale).astype(dk_ref.dtype)
        dv_ref[...] = dv_acc[...].astype(dv_ref.dtype)

    head = lambda h: (0, h)
    return pl.pallas_call(
        body, name=name, grid=(RET_HEADS,),
        in_specs=[pl.BlockSpec((T, RET_DK), lambda h: (0, OFF_RQ // RET_DK + h)),
                  pl.BlockSpec((T, RET_DK), lambda h: (0, OFF_RK // RET_DK + h)),
                  pl.BlockSpec((T, RET_DV), lambda h: (0, OFF_RV // RET_DV + h)),
                  _once((T, RET_DK), lambda h: (0, 0)), _once((T, RET_DK), lambda h: (0, 0)),
                  pl.BlockSpec((T, RET_DV), head)],
        out_specs=[pl.BlockSpec((T, RET_DK), head), pl.BlockSpec((T, RET_DK), head), pl.BlockSpec((T, RET_DV), head)],
        out_shape=[jax.ShapeDtypeStruct((T, RET_QK), BF16), jax.ShapeDtypeStruct((T, RET_QK), BF16),
                   jax.ShapeDtypeStruct((T, RET_V), BF16)],
        scratch_shapes=[pltpu.VMEM((T, RET_DK), BF16), pltpu.VMEM((T, RET_DV), BF16),
                        pltpu.VMEM((T, RET_DK), F32), pltpu.VMEM((T, RET_DV), F32),
                        pltpu.VMEM((nq, BLK, BLK), F32)],
        compiler_params=_params("arbitrary"))(z, z, z, cosf, sins, do)


GLA_PAIR = 2


def _gla_chunk(q_ref, k_ref, v_ref, glr_ref, gu, gb, rows, hh, trilf):
    ck = slice(hh * GLA_DK, (hh + 1) * GLA_DK)
    zg = _dot(glr_ref[rows, :], gu[:, ck]) + gb[:, ck]
    la = (jnp.minimum(zg, 0.0) - jnp.log(1.0 + jnp.exp(-jnp.abs(zg)))) * (1.0 / GLA_GATE_NORM)
    cum = jnp.dot(trilf, la, precision=HIGHEST, preferred_element_type=F32)
    last = jnp.sum(la, axis=0, keepdims=True)
    ecum = jnp.exp(cum)
    k = k_ref[rows, ck]
    qt = q_ref[rows, ck] * (GLA_DK ** -0.5) * ecum
    kt = k * jnp.exp(-cum)
    kh = k * jnp.exp(last - cum)
    return zg, cum, last, ecum, qt, kt, kh, v_ref[rows, hh * GLA_DV:(hh + 1) * GLA_DV].astype(BF16)


def _state_decay(last):
    e = jnp.exp(jnp.broadcast_to(last, (GLA_DK, GLA_DK)).T)
    return jnp.concatenate([e] * (GLA_DV // GLA_DK), axis=1)


def _gla_specs(T):
    wk, wv = GLA_PAIR * GLA_DK, GLA_PAIR * GLA_DV
    return [_once((T, wk), lambda h: (0, OFF_GQ // wk + h)),
            _once((T, wk), lambda h: (0, OFF_GK // wk + h)),
            _once((T, wv), lambda h: (0, OFF_GV // wv + h)),
            _once((T, LANE), lambda h: (0, 0)),
            pl.BlockSpec((LANE, wk), lambda h: (0, h)),
            pl.BlockSpec((1, wk), lambda h: (0, h))]


def gla_fwd(name, z, glr, gu, gb, o_prev):
    T = z.shape[0]
    nc = T // CHUNK
    wv = GLA_PAIR * GLA_DV

    def body(q_ref, k_ref, v_ref, glr_ref, gu_ref, gb_ref, prev_ref, o_ref, S):
        del prev_ref
        gu_b, gb_v = gu_ref[...].astype(BF16), gb_ref[...]
        ri = lax.broadcasted_iota(jnp.int32, (CHUNK, CHUNK), 0)
        ci = lax.broadcasted_iota(jnp.int32, (CHUNK, CHUNK), 1)
        tril = ri >= ci
        trilf = tril.astype(F32)
        S[...] = jnp.zeros_like(S)

        def step(c, carry):
            rows = pl.ds(pl.multiple_of(c * CHUNK, CHUNK), CHUNK)
            for hh in range(GLA_PAIR):
                _, _, last, _, qt, kt, kh, v = _gla_chunk(q_ref, k_ref, v_ref, glr_ref, gu_b, gb_v, rows, hh, trilf)
                a = jnp.where(tril, _dot_nt(qt, kt), 0.0)
                s_prev = S[hh]
                o_ref[rows, hh * GLA_DV:(hh + 1) * GLA_DV] = _dot(a, v) + _dot(qt, s_prev)
                S[hh] = s_prev * _state_decay(last) + _dot_tn(kh, v)
            return carry

        lax.fori_loop(0, nc, step, 0)

    n_in = 6
    return pl.pallas_call(
        body, name=name, grid=(GLA_HEADS // GLA_PAIR,),
        in_specs=_gla_specs(T) + [pl.BlockSpec(memory_space=pl.ANY)],
        out_specs=pl.BlockSpec((T, wv), lambda h: (0, RET_V // wv + h)),
        out_shape=jax.ShapeDtypeStruct(o_prev.shape, F32),
        scratch_shapes=[pltpu.VMEM((GLA_PAIR, GLA_DK, GLA_DV), F32)],
        input_output_aliases={n_in: 0},
        compiler_params=_params("arbitrary"))(z, z, z, glr, gu, gb, o_prev)


def gla_bwd(name, z, glr, gu, gb, do):
    T = z.shape[0]
    nc = T // CHUNK

    def body(q_ref, k_ref, v_ref, glr_ref, gu_ref, gb_ref, do_ref,
             dq_ref, dk_ref, dv_ref, dglr_ref, dgu_ref, dgb_ref, s_all, dS):
        gu_b, gb_v = gu_ref[...].astype(BF16), gb_ref[...]
        ri = lax.broadcasted_iota(jnp.int32, (CHUNK, CHUNK), 0)
        ci = lax.broadcasted_iota(jnp.int32, (CHUNK, CHUNK), 1)
        tril = ri >= ci
        trilf = tril.astype(F32)
        triuf = (ri <= ci).astype(F32)
        last_row = lax.broadcasted_iota(jnp.int32, (CHUNK, GLA_DK), 0) == CHUNK - 1
        ones8 = jnp.ones((8, GLA_DV), F32)

        def fstep(c, carry):
            rows = pl.ds(pl.multiple_of(c * CHUNK, CHUNK), CHUNK)
            for hh in range(GLA_PAIR):
                s_prev = dS[hh]
                s_all[hh, c] = s_prev
                _, _, last, _, _, _, kh, v = _gla_chunk(q_ref, k_ref, v_ref, glr_ref, gu_b, gb_v, rows, hh, trilf)
                dS[hh] = s_prev * _state_decay(last) + _dot_tn(kh, v)
            return carry

        dS[...] = jnp.zeros_like(dS)
        lax.fori_loop(0, nc, fstep, 0)
        dS[...] = jnp.zeros_like(dS)
        dgu_ref[...] = jnp.zeros_like(dgu_ref)
        dgb_ref[...] = jnp.zeros_like(dgb_ref)

        def bstep(i, carry):
            c = nc - 1 - i
            rows = pl.ds(pl.multiple_of(c * CHUNK, CHUNK), CHUNK)
            glr_c = glr_ref[rows, :]
            for hh in range(GLA_PAIR):
                ck, cv = slice(hh * GLA_DK, (hh + 1) * GLA_DK), slice(hh * GLA_DV, (hh + 1) * GLA_DV)
                zg, cum, last, ecum, qt, kt, kh, v = _gla_chunk(q_ref, k_ref, v_ref, glr_ref, gu_b, gb_v, rows, hh, trilf)
                a = jnp.where(tril, _dot_nt(qt, kt), 0.0)
                s_prev, ds_new = s_all[hh, c], dS[hh]
                dout = do_ref[rows, cv].astype(BF16)
                dv_ref[rows, cv] = (_dot_tn(a, dout) + _dot(kh, ds_new)).astype(dv_ref.dtype)
                da = jnp.where(tril, _dot_nt(dout, v), 0.0)
                dqt = _dot(da, kt) + _dot_nt(dout, s_prev)
                dkt = _dot_tn(da, qt)
                dkh = _dot_nt(v, ds_new)
                dS[hh] = ds_new * _state_decay(last) + _dot_tn(qt, dout)
                dq_ref[rows, ck] = (dqt * ecum * (GLA_DK ** -0.5)).astype(dq_ref.dtype)
                dk_ref[rows, ck] = (dkt * jnp.exp(-cum) + dkh * jnp.exp(last - cum)).astype(dk_ref.dtype)
                dkh_kh = dkh * kh
                dcum = dqt * qt - dkt * kt - dkh_kh
                rs = lax.dot_general(ones8, ds_new * s_prev, _NT, precision=HIGHEST, preferred_element_type=F32)
                dlast = (jnp.sum(dkh_kh, axis=0, keepdims=True)
                         + jnp.exp(last) * (jnp.sum(rs, axis=0, keepdims=True) * 0.125))
                dcum = dcum + jnp.where(last_row, dlast, 0.0)
                dla = jnp.dot(triuf, dcum, precision=HIGHEST, preferred_element_type=F32)
                dzg = dla * (1.0 / GLA_GATE_NORM) * _sigmoid(-zg)
                dglr_ref[hh, rows, :] = _dot_nt(dzg, gu_b[:, ck])
                dgu_ref[:, ck] += _dot_tn(glr_c, dzg)
                dgb_ref[:, ck] += jnp.sum(dzg, axis=0, keepdims=True)
            return carry

        lax.fori_loop(0, nc, bstep, 0)

    wk, wv = GLA_PAIR * GLA_DK, GLA_PAIR * GLA_DV
    return pl.pallas_call(
        body, name=name, grid=(GLA_HEADS // GLA_PAIR,),
        in_specs=_gla_specs(T) + [_once((T, wv), lambda h: (0, RET_V // wv + h))],
        out_specs=[pl.BlockSpec((T, wk), lambda h: (0, h)), pl.BlockSpec((T, wk), lambda h: (0, h)),
                   pl.BlockSpec((T, wv), lambda h: (0, h)),
                   pl.BlockSpec((GLA_PAIR, T, LANE), lambda h: (h, 0, 0)),
                   pl.BlockSpec((LANE, wk), lambda h: (0, h)), pl.BlockSpec((1, wk), lambda h: (0, h))],
        out_shape=[jax.ShapeDtypeStruct((T, GLA_QK), BF16), jax.ShapeDtypeStruct((T, GLA_QK), BF16),
                   jax.ShapeDtypeStruct((T, GLA_V), BF16), jax.ShapeDtypeStruct((GLA_HEADS, T, LANE), F32),
                   jax.ShapeDtypeStruct((LANE, GLA_QK), F32), jax.ShapeDtypeStruct((1, GLA_QK), F32)],
        scratch_shapes=[pltpu.VMEM((GLA_PAIR, nc, GLA_DK, GLA_DV), F32), pltpu.VMEM((GLA_PAIR, GLA_DK, GLA_DV), F32)],
        compiler_params=_params("arbitrary"))(z, z, z, glr, gu, gb, do)


HN_HEADS = RET_HEADS + GLA_HEADS
HN_W = RET_DV


def _gate_col(h):
    return jnp.where(h < RET_HEADS, OFF_RG // HN_W + h, OFF_GG // HN_W + h - RET_HEADS)


def headnorm_fwd(name, oraw, z, w, tr=256):
    T = oraw.shape[0]

    def body(o_ref, g_ref, w_ref, y_ref):
        y_ref[...] = (_rms(o_ref[...], w_ref[...]) * _silu_and_grad(g_ref[...])[0]).astype(y_ref.dtype)

    return pl.pallas_call(
        body, name=name, grid=(HN_HEADS, T // tr),
        in_specs=[pl.BlockSpec((tr, HN_W), lambda h, i: (i, h)),
                  pl.BlockSpec((tr, HN_W), lambda h, i: (i, _gate_col(h))),
                  pl.BlockSpec((1, HN_W), lambda h, i: (0, h))],
        out_specs=pl.BlockSpec((tr, HN_W), lambda h, i: (i, h)),
        out_shape=jax.ShapeDtypeStruct((T, HN_HEADS * HN_W), BF16),
        compiler_params=_params("arbitrary", "arbitrary"))(oraw, z, w)


def headnorm_bwd(name, oraw, z, w, dy, tr=256):
    T = oraw.shape[0]

    def body(o_ref, g_ref, w_ref, dy_ref, do_ref, dg_ref, dw_ref):
        o, wv, dyv = o_ref[...], w_ref[...], dy_ref[...].astype(F32)
        silu, dsilu = _silu_and_grad(g_ref[...])
        n = _rms(o, wv)
        dg_ref[...] = (dyv * n * dsilu).astype(dg_ref.dtype)
        dx, dw = _rms_bwd(o, wv, dyv * silu)
        do_ref[...] = dx
        _accumulate(dw_ref, dw, pl.program_id(1) == 0)

    blk = pl.BlockSpec((tr, HN_W), lambda h, i: (i, h))
    return pl.pallas_call(
        body, name=name, grid=(HN_HEADS, T // tr),
        in_specs=[blk, pl.BlockSpec((tr, HN_W), lambda h, i: (i, _gate_col(h))),
                  pl.BlockSpec((1, HN_W), lambda h, i: (0, h)), blk],
        out_specs=[blk, blk, pl.BlockSpec((1, HN_W), lambda h, i: (0, h))],
        out_shape=[jax.ShapeDtypeStruct((T, HN_HEADS * HN_W), F32),
                   jax.ShapeDtypeStruct((T, HN_HEADS * HN_W), BF16),
                   jax.ShapeDtypeStruct((1, HN_HEADS * HN_W), F32)],
        compiler_params=_params("arbitrary", "arbitrary"))(oraw, z, w, dy)


N_MASKS = 4


def _check_mask_classes(T):
    for window, dilation in DILATED_BRANCHES[:-1]:
        assert window < (N_MASKS - 1) * BLK - (BLK - 1) and BLK % dilation == 0
    assert DILATED_BRANCHES[-1][0] >= T and BLK % DILATED_BRANCHES[-1][1] == 0


def _fill_masks(mult_ref, bias_ref):
    ri = lax.broadcasted_iota(jnp.int32, (BLK, BLK), 0)
    ci = lax.broadcasted_iota(jnp.int32, (BLK, BLK), 1)
    for d in range(N_MASKS):
        dt = d * BLK + ri - ci
        mult = jnp.zeros((BLK, BLK), F32)
        for window, dilation in DILATED_BRANCHES:
            hit = (dt >= 0) & (dt <= window) & ((dt & (dilation - 1)) == 0)
            mult = mult + hit.astype(F32)
        mult_ref[d] = mult
        bias_ref[d] = jnp.where(mult > 0, 0.0, -1e30)


def _mask_row(ref, qi):
    return jnp.concatenate([ref[min(qi - kb, N_MASKS - 1)] for kb in range(qi + 1)], axis=1)


def attn_fwd(name, qkv):
    T = qkv.shape[0]
    D = qkv.shape[1] // 3
    dh = D // ATT_HEADS
    nq = T // BLK
    scale = dh ** -0.5

    _check_mask_classes(T)

    def body(q_ref, k_ref, v_ref, o_ref, lse_ref, mult_ref, bias_ref):
        @pl.when(pl.program_id(0) == 0)
        def _():
            _fill_masks(mult_ref, bias_ref)

        for qi in range(nq):
            rows, n = slice(qi * BLK, (qi + 1) * BLK), (qi + 1) * BLK
            s = (_dot_nt(q_ref[rows, :], k_ref[0:n, :]) * scale
                 + _mask_row(bias_ref, qi))
            m = jnp.max(s, axis=-1, keepdims=True)
            p = _mask_row(mult_ref, qi) * jnp.exp(s - m)
            l = jnp.sum(p, axis=-1, keepdims=True)
            o_ref[rows, :] = (_dot(p, v_ref[0:n, :]) / l).astype(o_ref.dtype)
            lse_ref[rows, :] = jnp.broadcast_to(m + jnp.log(l), (BLK, LANE))

    return pl.pallas_call(
        body, name=name, grid=(ATT_HEADS,),
        in_specs=[pl.BlockSpec((T, dh), lambda h: (0, h)),
                  pl.BlockSpec((T, dh), lambda h: (0, ATT_HEADS + h)),
                  pl.BlockSpec((T, dh), lambda h: (0, 2 * ATT_HEADS + h))],
        out_specs=[pl.BlockSpec((T, dh), lambda h: (0, h)),
                   pl.BlockSpec((None, T, LANE), lambda h: (h, 0, 0))],
        out_shape=[jax.ShapeDtypeStruct((T, D), BF16), jax.ShapeDtypeStruct((ATT_HEADS, T, LANE), F32)],
        scratch_shapes=[pltpu.VMEM((N_MASKS, BLK, BLK), F32), pltpu.VMEM((N_MASKS, BLK, BLK), F32)],
        compiler_params=_params("arbitrary"))(qkv, qkv, qkv)


def attn_bwd(name, qkv, o, lse, do):
    T = qkv.shape[0]
    D = qkv.shape[1] // 3
    dh = D // ATT_HEADS
    nq = T // BLK
    scale = dh ** -0.5

    _check_mask_classes(T)

    def body(q_ref, k_ref, v_ref, o_ref, lse_ref, do_ref, dq_ref, dk_ref, dv_ref, dk_acc, dv_acc, mult_ref, bias_ref):
        @pl.when(pl.program_id(0) == 0)
        def _():
            _fill_masks(mult_ref, bias_ref)

        dk_acc[...] = jnp.zeros_like(dk_acc)
        dv_acc[...] = jnp.zeros_like(dv_acc)
        for qi in range(nq):
            rows, n = slice(qi * BLK, (qi + 1) * BLK), (qi + 1) * BLK
            q, dout = q_ref[rows, :], do_ref[rows, :]
            kk, vv = k_ref[0:n, :], v_ref[0:n, :]
            delta = jnp.sum(dout.astype(F32) * o_ref[rows, :].astype(F32), axis=-1, keepdims=True)
            lse = jnp.max(lse_ref[rows, :], axis=-1, keepdims=True)
            s = _dot_nt(q, kk) * scale + _mask_row(bias_ref, qi)
            p = _mask_row(mult_ref, qi) * jnp.exp(s - lse)
            ds = (p * (_dot_nt(dout, vv) - delta) * scale).astype(BF16)
            dq_ref[rows, :] = _dot(ds, kk).astype(dq_ref.dtype)
            dk_acc[0:n, :] += _dot_tn(ds, q)
            dv_acc[0:n, :] += _dot_tn(p, dout)
        dk_ref[...] = dk_acc[...].astype(dk_ref.dtype)
        dv_ref[...] = dv_acc[...].astype(dv_ref.dtype)

    full = pl.BlockSpec((T, dh), lambda h: (0, h))
    return pl.pallas_call(
        body, name=name, grid=(ATT_HEADS,),
        in_specs=[full, pl.BlockSpec((T, dh), lambda h: (0, ATT_HEADS + h)),
                  pl.BlockSpec((T, dh), lambda h: (0, 2 * ATT_HEADS + h)),
                  full, pl.BlockSpec((None, T, LANE), lambda h: (h, 0, 0)), full],
        out_specs=[full, full, full],
        out_shape=[jax.ShapeDtypeStruct((T, D), BF16)] * 3,
        scratch_shapes=[pltpu.VMEM((T, dh), F32), pltpu.VMEM((T, dh), F32),
                        pltpu.VMEM((N_MASKS, BLK, BLK), F32), pltpu.VMEM((N_MASKS, BLK, BLK), F32)],
        compiler_params=_params("arbitrary"))(qkv, qkv, qkv, o, lse, do)


def _mesh_pos():
    mx, my, mc = lax.axis_index("x"), lax.axis_index("y"), lax.axis_index("c")
    return mx, my, mc, 4 * mx + 2 * my + mc


def _peer(k, mx, my, mc):
    px, py, pc = mx ^ (k >> 2), my ^ ((k >> 1) & 1), mc ^ (k & 1)
    return (px, py, pc), 4 * px + 2 * py + pc


_SIBLING = 1
_OTHER_CHIPS = (4, 2, 6)
_PLANS = {"gather": (2, N_DEV - 1), "scatter": (2, N_DEV - 1), "to_chips": (2, 1 + len(_OTHER_CHIPS)),
          "pass_on": (1, len(_OTHER_CHIPS))}


def _copies(kind, items, send_sems, recv_sems):
    mx, my, mc, me = _mesh_pos()
    out = []

    def add(n, src, dst, peer):
        out.append(pltpu.make_async_remote_copy(
            src_ref=src, dst_ref=dst, send_sem=send_sems.at[n], recv_sem=recv_sems.at[n],
            device_id=peer, device_id_type=pl.DeviceIdType.MESH))

    per_item = _PLANS[kind][1]
    for i, refs in enumerate(items):
        n = i * per_item
        if kind in ("gather", "scatter"):
            for k in range(1, N_DEV):
                peer, to = _peer(k, mx, my, mc)
                add(n + k - 1, refs[0] if kind == "gather" else refs[0].at[to], refs[1].at[me], peer)
        elif kind == "to_chips":
            for j, k in enumerate((_SIBLING,) + _OTHER_CHIPS):
                add(n + j, refs[0], refs[1].at[me], _peer(k, mx, my, mc)[0])
        else:
            for j, k in enumerate(_OTHER_CHIPS):
                add(n + j, refs[0].at[me ^ k], refs[0].at[me ^ k], _peer(_SIBLING, mx, my, mc)[0])
    return out


_HBM = pl.BlockSpec(memory_space=pltpu.HBM)
_SEM = pl.BlockSpec(memory_space=pltpu.SEMAPHORE)
_DATAFLOW = pltpu.SideEffectType.DATAFLOW_SIDE_EFFECTING


def exchange_call(name, waits, starts, deps=()):
    bufs, slot_of = [], {}

    def slots(items):
        out = []
        for item in items:
            for b in item:
                if id(b) not in slot_of:
                    slot_of[id(b)] = len(bufs)
                    bufs.append(b)
            out.append(tuple(slot_of[id(b)] for b in item))
        return out

    wait_plan = [(kind, slots(handle[0])) for kind, handle in waits]
    start_plan = [(kind, slots(items)) for kind, items in starts]
    wait_sems = [s for _, handle in waits for s in handle[1:]]
    n_buf, n_ws, n_start = len(bufs), len(wait_sems), len(starts)

    def body(*refs):
        buf_refs, sems_in = refs[:n_buf], refs[n_buf:n_buf + n_ws]
        outs = refs[n_buf + n_ws + len(deps):]
        pick = lambda plan: [tuple(buf_refs[s] for s in item) for item in plan]
        for wi, (kind, plan) in enumerate(wait_plan):
            copies = _copies(kind, pick(plan), sems_in[2 * wi], sems_in[2 * wi + 1])
            for cp in copies:
                cp.wait_send()
            for cp in copies:
                cp.wait_recv()
        for si, (kind, plan) in enumerate(start_plan):
            for cp in _copies(kind, pick(plan), outs[2 * si], outs[2 * si + 1]):
                cp.start()
        outs[-1][...] = jnp.zeros_like(outs[-1])

    hbm_bufs = [pltpu.with_memory_space_constraint(b, pltpu.HBM) for b in bufs]
    sem_shapes = []
    for kind, plan in start_plan:
        sem_shapes += [pltpu.SemaphoreType.DMA((len(plan) * _PLANS[kind][1],))] * 2
    outs = pl.pallas_call(
        body, name=name,
        out_shape=sem_shapes + [pltpu.HBM(b.shape, b.dtype) for b in bufs] + [jax.ShapeDtypeStruct((8, LANE), F32)],
        in_specs=[_HBM] * n_buf + [_SEM] * n_ws + [_ANY] * len(deps),
        out_specs=[_SEM] * (2 * n_start) + [_HBM] * n_buf + [pl.BlockSpec(memory_space=pltpu.VMEM)],
        input_output_aliases={i: 2 * n_start + i for i in range(n_buf)},
        compiler_params=pltpu.CompilerParams(has_side_effects=_DATAFLOW))(*hbm_bufs, *wait_sems, *deps)
    sems, thru, token = outs[:2 * n_start], outs[2 * n_start:-1], outs[-1]
    through = lambda plan: [tuple(thru[s] for s in item) for item in plan]
    waited = [through(plan) for _, plan in wait_plan]
    handles = [(through(plan), sems[2 * si], sems[2 * si + 1]) for si, (_, plan) in enumerate(start_plan)]
    return waited, handles, token


def gather_small(name, a, deps=()):
    def body(a_ref, *rest):
        o_ref, send_sems, recv_sems, local_sem = rest[len(deps):]
        me = _mesh_pos()[3]
        own = pltpu.make_async_copy(a_ref, o_ref.at[me], local_sem)
        own.start()
        copies = _copies("gather", [(a_ref, o_ref)], send_sems, recv_sems)
        for cp in copies:
            cp.start()
        for cp in copies:
            cp.wait_recv()
        for cp in copies:
            cp.wait_send()
        own.wait()

    return pl.pallas_call(
        body, name=name, in_specs=[_ANY] * (1 + len(deps)), out_specs=_ANY,
        out_shape=jax.ShapeDtypeStruct((N_DEV,) + a.shape, a.dtype),
        scratch_shapes=[pltpu.SemaphoreType.DMA((N_DEV - 1,)), pltpu.SemaphoreType.DMA((N_DEV - 1,)),
                        pltpu.SemaphoreType.DMA],
        compiler_params=pltpu.CompilerParams(has_side_effects=True))(a, *deps)


def _adamw_math(w, g, m, v):
    m2 = ADAM_B1 * m + (1.0 - ADAM_B1) * g
    v2 = ADAM_B2 * v + (1.0 - ADAM_B2) * (g * g)
    m_hat = m2 / (1.0 - ADAM_B1 ** ADAM_STEP)
    v_hat = v2 / (1.0 - ADAM_B2 ** ADAM_STEP)
    delta = -ADAM_LR * (m_hat / (jnp.sqrt(v_hat) + ADAM_EPS) + ADAM_WD * w)
    return delta, m2, v2


def adamw(name, w, m, v, l, land, own, prev=None):
    L, r, c = w.shape
    cp = land.shape[2]
    tr = _pick(r, (256, 176, 128, 64, 32, 16, 8))

    def body(w_ref, m_ref, v_ref, land_ref, own_ref, *rest):
        g_ref, d_ref, m2_ref, v2_ref = rest[-4:]
        me = _mesh_pos()[3]
        mine = own_ref[:, pl.ds(0, c)].astype(F32)
        g = None
        for s in range(N_DEV):
            part = jnp.where(me == s, mine, land_ref[s, :, pl.ds(0, c)].astype(F32))
            g = part if g is None else g + part
        delta, m2, v2 = _adamw_math(w_ref[...], g, m_ref[...], v_ref[...])
        g_ref[...] = g
        d_ref[...] = delta
        m2_ref[...] = m2
        v2_ref[...] = v2

    blk = pl.BlockSpec((None, tr, c), lambda i: (l, i, 0))
    shape = jax.ShapeDtypeStruct((L, r, c), F32)
    extra = [] if prev is None else list(prev)
    return pl.pallas_call(
        body, name=name, grid=(r // tr,),
        in_specs=[blk, blk, blk, pl.BlockSpec((N_DEV, tr, cp), lambda i: (0, i, 0)),
                  pl.BlockSpec((None, tr, cp), lambda i: (_mesh_pos()[3], i, 0))] + [_ANY] * len(extra),
        out_specs=[blk] * 4, out_shape=[shape] * 4,
        input_output_aliases={5 + k: k for k in range(len(extra))},
        compiler_params=_params("parallel"))(w, m, v, land, own, *extra)


def adamw_small(name, w, m, v, parts):
    n = w.shape[1]

    def body(w_ref, m_ref, v_ref, p_ref, g_ref, d_ref, m2_ref, v2_ref):
        g = p_ref[0:1, :]
        for s in range(1, N_DEV):
            g = g + p_ref[s:s + 1, :]
        delta, m2, v2 = _adamw_math(w_ref[...], g, m_ref[...], v_ref[...])
        g_ref[...] = g
        d_ref[...] = delta
        m2_ref[...] = m2
        v2_ref[...] = v2

    shape = jax.ShapeDtypeStruct((1, n), F32)
    return pl.pallas_call(body, name=name, out_shape=[shape] * 4,
                          compiler_params=pltpu.CompilerParams(vmem_limit_bytes=VMEM_LIMIT_BYTES))(w, m, v, parts)


def _rope_tables(positions):
    half = RET_DK // 2
    inv_freq = 1.0 / jnp.power(RET_THETA_BASE, jnp.linspace(0.0, 1.0, half, dtype=F32))
    ang = positions.astype(F32)[:, None] * inv_freq
    cos, sin = jnp.cos(ang), jnp.sin(ang)
    cosf = jnp.repeat(cos, 2, axis=-1)
    sins = jnp.stack([-sin, sin], axis=-1).reshape(cosf.shape)
    return cosf, sins


def _pad_to(a, axis, size):
    pad = [(0, 0)] * a.ndim
    pad[axis] = (0, size - a.shape[axis])
    return jnp.pad(a, pad)


def _round_up(n, m):
    return -(-n // m) * m


def kernel(x, p, positions, attn_norm_w, ffn_norm_w, ple_norm_w, final_norm_w, ab_w_in, ab_gla_gate_up, ab_gla_gate_b, ab_ret_norm_w, ab_gla_norm_w, ab_w_out, c_w_qkv, c_w_out, ffn_w_gate, ffn_w_up, ffn_w_down, ple_w_proj, ple_w_gate, loss_target, m_attn_norm_w, m_ffn_norm_w, m_ple_norm_w, m_final_norm_w, m_ab_w_in, m_ab_gla_gate_up, m_ab_gla_gate_b, m_ab_ret_norm_w, m_ab_gla_norm_w, m_ab_w_out, m_c_w_qkv, m_c_w_out, m_ffn_w_gate, m_ffn_w_up, m_ffn_w_down, m_ple_w_proj, m_ple_w_gate, v_attn_norm_w, v_ffn_norm_w, v_ple_norm_w, v_final_norm_w, v_ab_w_in, v_ab_gla_gate_up, v_ab_gla_gate_b, v_ab_ret_norm_w, v_ab_gla_norm_w, v_ab_w_out, v_c_w_qkv, v_c_w_out, v_ffn_w_gate, v_ffn_w_up, v_ffn_w_down, v_ple_w_proj, v_ple_w_gate):
    T, D = x.shape[1], x.shape[2]
    depth = attn_norm_w.shape[0]
    assert ab_w_in.shape[0] == 1 and c_w_qkv.shape[0] == 1 and depth == 2, "one even and one odd layer"
    me = 4 * lax.axis_index("x") + 2 * lax.axis_index("y") + lax.axis_index("c")
    in_shard = ab_w_in.shape[2]
    in_width = in_shard * N_DEV
    assert in_width == OFF_LR + GLA_GATE_RANK
    fs = ffn_w_gate.shape[2]
    fp = _round_up(fs, LANE)
    gu_cols = ab_gla_gate_up.shape[2]

    bf = lambda a: a.astype(BF16)
    tr_ = lambda a: jnp.swapaxes(a, -1, -2)
    wg_t, wu_t = tr_(ffn_w_gate), tr_(ffn_w_up)
    srcs = {"w_in": bf(tr_(ab_w_in[0])), "w_oab": bf(ab_w_out[0]), "gu": ab_gla_gate_up[0],
            "w_qkv": bf(c_w_qkv[0]), "w_oc": bf(c_w_out[0])}
    for l in range(depth):
        srcs[f"wg{l}"] = _pad_to(bf(wg_t[l]), 0, fp)
        srcs[f"wu{l}"] = _pad_to(bf(wu_t[l]), 0, fp)
        srcs[f"wd{l}"] = _pad_to(bf(ffn_w_down[l]), 0, fp)
        srcs[f"wpg{l}"] = bf(ple_w_gate[l])
        srcs[f"wpp{l}"] = bf(ple_w_proj[l])
    group_keys = [["w_in", "w_oab", "gu"], ["wg0", "wu0"], ["wd0", "wpg0", "wpp0"], ["w_qkv", "w_oc"],
                  ["wg1", "wu1"], ["wd1", "wpg1", "wpp1"]]

    def landing(a):
        return lax.dynamic_update_slice(lax.empty((N_DEV,) + a.shape, a.dtype), a[None], (me,) + (0,) * a.ndim)

    _, chip_handles, gather_token = exchange_call(
        "gather_start", [], [("to_chips", [(srcs[k], landing(srcs[k])) for k in keys]) for keys in group_keys])
    weights = {}

    def gather_wait(gi, dep):
        lands = [(land,) for _, land in chip_handles[gi][0]]
        _, (passing,), _ = exchange_call(
            f"gather{gi}_pass", [("to_chips", chip_handles[gi])], [("pass_on", lands)], deps=(dep,))
        (complete,), _, _ = exchange_call(f"gather{gi}_done", [("pass_on", passing)], [])
        weights.update(zip(group_keys[gi], [land for (land,) in complete]))

    gb = ab_gla_gate_b
    hn_w = jnp.concatenate([ab_ret_norm_w, ab_gla_norm_w], axis=1)
    cosf, sins = _rope_tables(positions[0])
    p_bf = bf(p[:, 0])

    xs = x[0]
    saved = []
    for i in range(depth):
        nm = f"l{i}_"
        w_attn, w_ffn, w_ple = attn_norm_w[i:i + 1], ffn_norm_w[i:i + 1], ple_norm_w[i:i + 1]
        (xn,) = rowwise(nm + "norm_attn", lambda a, w: (_rms(a, w),), T, [("row", xs), ("full", w_attn)],
                        [("row", D, BF16)], deps=(gather_token,) if i == 0 else ())
        gather_wait(3 * i, xn)
        if i % 2 == 0:
            w_in_t = weights["w_in"].reshape(1, 1, in_width, D)
            w_lr_t = _pad_to(w_in_t[0, 0, OFF_LR:], 0, LANE).reshape(1, 1, LANE, D)
            w_oab = weights["w_oab"].reshape(1, 1, D, D)
            gu_full = _pad_to(weights["gu"].transpose(1, 0, 2).reshape(GLA_GATE_RANK, GLA_QK), 0, LANE)
            z = mmt_fwd(nm + "mm_in", xn, w_in_t, 0, F32, n=OFF_LR)
            glr = mmt_fwd(nm + "mm_lr", xn, w_lr_t, 0, F32)
            oraw = retention_fwd(nm + "ret_fwd", z, cosf, sins, RET_V + GLA_V)
            oraw = gla_fwd(nm + "gla_fwd", z, glr, gu_full, gb, oraw)
            o = headnorm_fwd(nm + "headnorm_fwd", oraw, z, hn_w)
            mix = mm_nn(nm + "mm_out", o, w_oab, 0, F32)
            mixer_saved = (z, glr, oraw, o)
        else:
            w_qkv = weights["w_qkv"].reshape((1,) + weights["w_qkv"].shape)
            w_oc = weights["w_oc"].reshape(1, 1, D, D)
            qkv = mm_nn(nm + "mm_qkv", xn, w_qkv, 0, BF16)
            o, lse = attn_fwd(nm + "attn_fwd", qkv)
            mix = mm_nn(nm + "mm_out", o, w_oc, 0, F32)
            mixer_saved = (qkv, o, lse)
        h1, hn = rowwise(nm + "add_norm_ffn", lambda a, b, w: (a + b, _rms(a + b, w)), T,
                         [("row", xs), ("row", mix), ("full", w_ffn)], [("row", D, F32), ("row", D, BF16)])
        gather_wait(3 * i + 1, hn)
        wg = weights[f"wg{i}"].reshape(1, N_DEV, fp, D)
        wu = weights[f"wu{i}"].reshape(1, N_DEV, fp, D)
        g, u, act = ffn_gate_up(nm + "ffn_gate_up", hn, wg, wu)
        gather_wait(3 * i + 2, act)
        wd = weights[f"wd{i}"].reshape(1, 1, N_DEV * fp, D)
        wpg = weights[f"wpg{i}"].reshape(1, 1, D, D)
        wpp = weights[f"wpp{i}"].reshape((1,) + weights[f"wpp{i}"].shape)
        f = mm_nn(nm + "mm_down", act, wd, 0, F32)
        h2, pn = rowwise(nm + "add_norm_ple", lambda a, b, w: (a + b, _rms(a + b, w)), T,
                         [("row", h1), ("row", f), ("full", w_ple)], [("row", D, F32), ("row", D, BF16)])
        s = mm_nn(nm + "mm_ple_gate", pn, wpg, 0, F32)
        e = mm_nn(nm + "mm_ple_proj", p_bf[i], wpp, 0, F32)
        (x_next,) = rowwise(nm + "ple_out", lambda a, b, c: (a + _sigmoid(b) * c,), T,
                            [("row", h2), ("row", s), ("row", e)], [("row", D, F32)])
        mixer_w = (w_in_t, w_lr_t, w_oab, gu_full) if i % 2 == 0 else (w_qkv, w_oc)
        saved.append((xs, xn, mixer_saved, mixer_w, (wg, wu, wd, wpg), h1, hn, g, u, act, h2, pn, s, e))
        xs = x_next

    def loss_fn(a, w, t):
        diff = _rms(a, w) - t
        dx, dw = _rms_bwd(a, w, diff * (1.0 / D))
        part = 0.5 * jnp.sum(jnp.mean(diff * diff, axis=-1, keepdims=True), axis=0, keepdims=True)
        return dx, dw, jnp.broadcast_to(part, (1, LANE))

    dx, d_final_w, loss_part = rowwise("loss_head", loss_fn, T,
                                       [("row", xs), ("full", final_norm_w[None, :]), ("row", loss_target[0])],
                                       [("row", D, F32), ("acc", D), ("acc", LANE)])
    loss = lax.psum(loss_part[0, 0], ("x", "y", "c"))

    grads = {}
    scatters = []

    def scatter_start(name, keys):
        _, handles, token = exchange_call(
            name, [], [("scatter", [(grads[k], lax.empty(grads[k].shape, BF16)) for k in keys])])
        scatters.append((keys, handles[0]))
        return token

    d_attn_w, d_ffn_w, d_ple_w = [None] * depth, [None] * depth, [None] * depth
    for i in reversed(range(depth)):
        nm = f"l{i}_b_"
        xs_i, xn, mixer_saved, mixer_w, (wg, wu, wd, wpg), h1, hn, g, u, act, h2, pn, s, e = saved[i]
        w_attn, w_ffn, w_ple = attn_norm_w[i:i + 1], ffn_norm_w[i:i + 1], ple_norm_w[i:i + 1]

        def ple_bwd(d, sv, ev):
            gate = _sigmoid(sv)
            return d * gate, d * ev * gate * (1.0 - gate)

        de, ds = rowwise(nm + "ple_out", ple_bwd, T, [("row", dx), ("row", s), ("row", e)],
                         [("row", D, BF16), ("row", D, BF16)], deps=(loss.reshape(1, 1),) if i == depth - 1 else ())
        grads[("ple_w_proj", i)] = mm_tn(nm + "mm_ple_proj_w", p_bf[i], de, N_DEV, BF16)
        grads[("ple_w_gate", i)] = mm_tn(nm + "mm_ple_gate_w", pn, ds, 1, BF16).reshape(N_DEV, D // N_DEV, D)
        dpn = mm_nt(nm + "mm_ple_gate_x", ds, wpg, 0, F32)

        def norm_bwd_add(a, w, dn, dres):
            dxx, dw = _rms_bwd(a, w, dn)
            tot = dres + dxx
            return tot, tot, dw

        dh2, dh2_bf, d_ple_w[i] = rowwise(nm + "norm_ple", norm_bwd_add, T,
                                          [("row", h2), ("full", w_ple), ("row", dpn), ("row", dx)],
                                          [("row", D, F32), ("row", D, BF16), ("acc", D)])
        grads[("ffn_w_down", i)] = mm_tn(nm + "mm_down_w", act, dh2_bf, 1, BF16).reshape(N_DEV, fp, D)
        token = scatter_start(nm + "scatter_ple_down", [("ple_w_proj", i), ("ple_w_gate", i), ("ffn_w_down", i)])
        dg, du = ffn_down_bwd(nm + "ffn_down_x", dh2_bf, wd, g, u, deps=(token,))
        grads[("ffn_w_gate", i)] = mmt_dw(nm + "mm_gate_w", dg, hn, N_DEV, BF16)
        grads[("ffn_w_up", i)] = mmt_dw(nm + "mm_up_w", du, hn, N_DEV, BF16)
        token = scatter_start(nm + "scatter_gate_up", [("ffn_w_gate", i), ("ffn_w_up", i)])
        dhn_g = mmt_dx(nm + "mm_gate_x", dg, wg, 0, F32, deps=(token,))
        dhn_u = mmt_dx(nm + "mm_up_x", du, wu, 0, F32)

        def norm_bwd_add2(a, w, dn1, dn2, dres):
            dxx, dw = _rms_bwd(a, w, dn1 + dn2)
            tot = dres + dxx
            return tot, tot, dw

        dh1, dh1_bf, d_ffn_w[i] = rowwise(nm + "norm_ffn", norm_bwd_add2, T,
                                          [("row", h1), ("full", w_ffn), ("row", dhn_g), ("row", dhn_u), ("row", dh2)],
                                          [("row", D, F32), ("row", D, BF16), ("acc", D)])
        if i % 2 == 0:
            z, glr, oraw, o = mixer_saved
            w_in_t, w_lr_t, w_oab, gu_full = mixer_w
            grads[("ab_w_out", 0)] = mm_tn(nm + "mm_out_w", o, dh1_bf, 1, BF16).reshape(N_DEV, D // N_DEV, D)
            token = scatter_start(nm + "scatter_out", [("ab_w_out", 0)])
            do = mm_nt(nm + "mm_out_x", dh1_bf, w_oab, 0, F32, deps=(token,))
            d_oraw, d_gates, d_hn_w = headnorm_bwd(nm + "headnorm", oraw, z, hn_w, do)
            d_rq, d_rk, d_rv = retention_bwd(nm + "ret", z, cosf, sins, d_oraw)
            d_gq, d_gk, d_gv, d_glr4, d_gu, d_gb = gla_bwd(nm + "gla", z, glr, gu_full, gb, d_oraw)
            dz = jnp.concatenate([d_rq, d_rk, d_rv, d_gates[:, :RET_V], d_gq, d_gk, d_gv, d_gates[:, RET_V:]], axis=1)
            (d_glr,) = rowwise(nm + "sum_lr", lambda *a: (a[0] + a[1] + a[2] + a[3],), T,
                               [("row", d_glr4[hh]) for hh in range(GLA_HEADS)], [("row", LANE, BF16)])
            dw_main = mm_tn(nm + "mm_in_w", xn, dz, 1, BF16)[0]
            dw_lr = mm_tn(nm + "mm_lr_w", xn, d_glr, 1, BF16)[0]
            dw_in = jnp.concatenate([dw_main, dw_lr[:, :GLA_GATE_RANK]], axis=1)
            grads[("ab_w_in", 0)] = dw_in.reshape(D, N_DEV, in_shard).transpose(1, 0, 2)
            token = scatter_start(nm + "scatter_in", [("ab_w_in", 0)])
            dxn_a = mmt_dx(nm + "mm_in_x", dz, w_in_t, 0, F32, n=OFF_LR, deps=(token,))
            dxn_b = mmt_dx(nm + "mm_lr_x", d_glr, w_lr_t, 0, F32)
        else:
            qkv, o, lse = mixer_saved
            w_qkv, w_oc = mixer_w
            grads[("c_w_out", 0)] = mm_tn(nm + "mm_out_w", o, dh1_bf, 1, BF16).reshape(N_DEV, D // N_DEV, D)
            do = mm_nt(nm + "mm_out_x", dh1_bf, w_oc, 0, BF16)
            dq, dk, dv = attn_bwd(nm + "attn", qkv, o, lse, do)
            dqkv = jnp.concatenate([dq, dk, dv], axis=1)
            grads[("c_w_qkv", 0)] = mm_tn(nm + "mm_qkv_w", xn, dqkv, N_DEV, BF16)
            token = scatter_start(nm + "scatter_attn", [("c_w_out", 0), ("c_w_qkv", 0)])
            dxn_a = mm_nt(nm + "mm_qkv_x", dqkv, w_qkv, 0, F32, deps=(token,))
            dxn_b = None
        if dxn_b is None:
            dx, _, d_attn_w[i] = rowwise(nm + "norm_attn", norm_bwd_add, T,
                                         [("row", xs_i), ("full", w_attn), ("row", dxn_a), ("row", dh1)],
                                         [("row", D, F32), ("row", D, BF16), ("acc", D)])
        else:
            dx, _, d_attn_w[i] = rowwise(nm + "norm_attn", norm_bwd_add2, T,
                                         [("row", xs_i), ("full", w_attn), ("row", dxn_a), ("row", dxn_b), ("row", dh1)],
                                         [("row", D, F32), ("row", D, BF16), ("acc", D)])

    small_names = ["attn_norm_w", "ffn_norm_w", "ple_norm_w", "final_norm_w", "ab_gla_gate_b", "ab_ret_norm_w",
                   "ab_gla_norm_w"]
    small_grads = [jnp.concatenate(d_attn_w, 0), jnp.concatenate(d_ffn_w, 0), jnp.concatenate(d_ple_w, 0), d_final_w[0],
                   d_gb, d_hn_w[:, :RET_V], d_hn_w[:, RET_V:]]
    small_w = [attn_norm_w, ffn_norm_w, ple_norm_w, final_norm_w, ab_gla_gate_b, ab_ret_norm_w, ab_gla_norm_w]
    small_m = [m_attn_norm_w, m_ffn_norm_w, m_ple_norm_w, m_final_norm_w, m_ab_gla_gate_b, m_ab_ret_norm_w, m_ab_gla_norm_w]
    small_v = [v_attn_norm_w, v_ffn_norm_w, v_ple_norm_w, v_final_norm_w, v_ab_gla_gate_b, v_ab_ret_norm_w, v_ab_gla_norm_w]
    sizes = [int(np.prod(a.shape)) for a in small_w]
    n_gu = GLA_GATE_RANK * GLA_QK
    n_small = _round_up(sum(sizes) + n_gu, LANE)
    pack = lambda parts: _pad_to(jnp.concatenate([a.reshape(-1) for a in parts]), 0, n_small)[None, :]
    small_part = pack(small_grads + [d_gu[:GLA_GATE_RANK]])

    big_w = dict(ab_w_in=(ab_w_in, m_ab_w_in, v_ab_w_in), ab_w_out=(ab_w_out, m_ab_w_out, v_ab_w_out),
                 c_w_qkv=(c_w_qkv, m_c_w_qkv, v_c_w_qkv), c_w_out=(c_w_out, m_c_w_out, v_c_w_out),
                 ffn_w_gate=(wg_t, tr_(m_ffn_w_gate), tr_(v_ffn_w_gate)),
                 ffn_w_up=(wu_t, tr_(m_ffn_w_up), tr_(v_ffn_w_up)),
                 ffn_w_down=(ffn_w_down, m_ffn_w_down, v_ffn_w_down), ple_w_proj=(ple_w_proj, m_ple_w_proj, v_ple_w_proj),
                 ple_w_gate=(ple_w_gate, m_ple_w_gate, v_ple_w_gate))
    results, last = {}, dx
    for gi, (keys, handle) in enumerate(scatters):
        (arrived,), _, _ = exchange_call(f"scatter_wait{gi}", [("scatter", handle)], [], deps=(last,))
        for (n, l), (own, land) in zip(keys, arrived):
            results[n] = adamw(f"adamw_{n}{l}", *big_w[n], l, land, own, prev=results.get(n))
            last = results[n][0]
    for n in ("ffn_w_gate", "ffn_w_up"):
        results[n] = [tr_(a) for a in results[n]]
    small_parts = gather_small("gather_small", small_part, deps=(last,)).reshape(N_DEV, n_small)

    gu_off = sum(sizes)
    own_cols = lambda a: lax.dynamic_slice_in_dim(a.reshape(GLA_GATE_RANK, GLA_QK), me * gu_cols, gu_cols, axis=1)
    small_res = adamw_small("adamw_small", pack(small_w + [jnp.zeros((n_gu,), F32)]),
                            pack(small_m + [jnp.zeros((n_gu,), F32)]), pack(small_v + [jnp.ones((n_gu,), F32)]),
                            small_parts)
    g_gu_full = small_res[0][0, gu_off:gu_off + n_gu]
    g_gu = own_cols(g_gu_full)[None]
    gu_res = adamw_small("adamw_gate_up", *[_pad_to(a.reshape(1, -1), 1, _round_up(a.size, LANE)) for a in
                                            (ab_gla_gate_up, m_ab_gla_gate_up, v_ab_gla_gate_up)],
                         jnp.concatenate([_pad_to(g_gu.reshape(1, -1), 1, _round_up(g_gu.size, LANE)),
                                          jnp.zeros((N_DEV - 1, _round_up(g_gu.size, LANE)), F32)], axis=0))
    for k in range(4):
        off = 0
        for n, a, sz in zip(small_names, small_w, sizes):
            results.setdefault(n, [None] * 4)[k] = small_res[k][0, off:off + sz].reshape(a.shape)
            off += sz
        results.setdefault("ab_gla_gate_up", [None] * 4)[k] = gu_res[k][0, :g_gu.size].reshape(ab_gla_gate_up.shape)

    order = ["attn_norm_w", "ffn_norm_w", "ple_norm_w", "final_norm_w", "ab_w_in", "ab_gla_gate_up", "ab_gla_gate_b",
             "ab_ret_norm_w", "ab_gla_norm_w", "ab_w_out", "c_w_qkv", "c_w_out", "ffn_w_gate", "ffn_w_up", "ffn_w_down",
             "ple_w_proj", "ple_w_gate"]
    return (loss, dx[None], *[results[n][0] for n in order], *[results[n][1] for n in order],
            *[results[n][2] for n in order], *[results[n][3] for n in order])
```

```python
import math

import numpy as np
import jax
import jax.numpy as jnp
from jax import lax
from jax.experimental import pallas as pl
from jax.experimental.pallas import tpu as pltpu

F32 = jnp.float32
BF16 = jnp.bfloat16
HIGHEST = lax.Precision.HIGHEST

N_DEV = 8
VMEM_LIMIT_BYTES = 48 * 1024 * 1024
LANE = 128
NORM_EPS = 1e-6

RET_HEADS, RET_DK, RET_DV = 4, 256, 256
RET_THETA_BASE = 10000.0
GLA_HEADS, GLA_DK, GLA_DV = 4, 128, 256
GLA_GATE_RANK = 16
GLA_GATE_NORM = 16.0
CHUNK = 64
ATT_HEADS = 16
DILATED_BRANCHES = ((128, 1), (512, 4), (2048, 16))
BLK = 256

ADAM_LR, ADAM_B1, ADAM_B2, ADAM_EPS, ADAM_WD, ADAM_STEP = 0.001, 0.9, 0.999, 1e-08, 0.01, 10

RET_QK = RET_HEADS * RET_DK
RET_V = RET_HEADS * RET_DV
GLA_QK = GLA_HEADS * GLA_DK
GLA_V = GLA_HEADS * GLA_DV
OFF_RQ, OFF_RK, OFF_RV, OFF_RG = 0, RET_QK, 2 * RET_QK, 2 * RET_QK + RET_V
OFF_GQ = OFF_RG + RET_V
OFF_GK = OFF_GQ + GLA_QK
OFF_GV = OFF_GK + GLA_QK
OFF_GG = OFF_GV + GLA_V
OFF_LR = OFF_GG + GLA_V


def _params(*sem):
    return pltpu.CompilerParams(dimension_semantics=sem or None, vmem_limit_bytes=VMEM_LIMIT_BYTES)


def _pick(n, cands):
    for c in cands:
        if n % c == 0:
            return c
    raise ValueError(f"no tile for {n} in {cands}")


_NN = (((1,), (0,)), ((), ()))
_NT = (((1,), (1,)), ((), ()))
_TN = (((0,), (0,)), ((), ()))
_ANY = pl.BlockSpec(memory_space=pl.ANY)
MAX_CONTRACT = 2048
_TILES = (1024, 768, 512, 256, 128)


def _mm_call(name, dims, grid, in_specs, out_spec, out_shape, args, deps=()):
    steps = grid[2]
    assert steps == 1 or out_shape.dtype == F32

    def body(a_ref, b_ref, *rest):
        o_ref = rest[len(deps)]
        part = lax.dot_general(a_ref[...].astype(BF16), b_ref[...].astype(BF16), dims, preferred_element_type=F32)
        if steps == 1:
            o_ref[...] = part.astype(o_ref.dtype)
        else:
            _accumulate(o_ref, part, pl.program_id(2) == 0)

    return pl.pallas_call(
        body, name=name, grid=grid, in_specs=list(in_specs) + [_ANY] * len(deps), out_specs=out_spec,
        out_shape=out_shape, compiler_params=_params("parallel", "parallel", "arbitrary"))(*args, *deps)


def mm_nn(name, a, w, l, out_dtype, deps=()):
    _, J, K, n = w.shape
    M = a.shape[0]
    tm, tn, tk = _pick(M, _TILES), _pick(n, _TILES), _pick(K, (MAX_CONTRACT,) + _TILES)
    nt = n // tn
    return _mm_call(
        name, _NN, (M // tm, J * nt, K // tk),
        [pl.BlockSpec((tm, tk), lambda i, j, k: (i, k)),
         pl.BlockSpec((None, None, tk, tn), lambda i, j, k: (l, j // nt, k, j % nt))],
        pl.BlockSpec((tm, tn), lambda i, j, k: (i, j)),
        jax.ShapeDtypeStruct((M, J * n), out_dtype), (a, w), deps)


def mm_nt(name, a, w, l, out_dtype, deps=()):
    _, J, K, n = w.shape
    M = a.shape[0]
    tm, tq, tc = _pick(M, _TILES), _pick(K, _TILES), _pick(n, (MAX_CONTRACT,) + _TILES)
    nc = n // tc
    return _mm_call(
        name, _NT, (M // tm, K // tq, J * nc),
        [pl.BlockSpec((tm, tc), lambda i, q, c: (i, c)),
         pl.BlockSpec((None, None, tq, tc), lambda i, q, c: (l, c // nc, q, c % nc))],
        pl.BlockSpec((tm, tq), lambda i, q, c: (i, q)),
        jax.ShapeDtypeStruct((M, K), out_dtype), (a, w), deps)


def mm_tn(name, x, dy, J, out_dtype, deps=()):
    M, K = x.shape
    n = dy.shape[1] // J
    tp, tn = _pick(K, _TILES), _pick(n, _TILES)
    nt = n // tn
    assert M <= MAX_CONTRACT
    return _mm_call(
        name, _TN, (K // tp, J * nt, 1),
        [pl.BlockSpec((M, tp), lambda i, j, r: (0, i)),
         pl.BlockSpec((M, tn), lambda i, j, r: (0, j))],
        pl.BlockSpec((None, tp, tn), lambda i, j, r: (j // nt, i, j % nt)),
        jax.ShapeDtypeStruct((J, K, n), out_dtype), (x, dy), deps)


def mmt_fwd(name, a, wt, l, out_dtype, n=None, deps=()):
    _, J, rows, K = wt.shape
    n = rows if n is None else n
    M = a.shape[0]
    tm, tn = _pick(M, _TILES), _pick(n, _TILES)
    nt = n // tn
    assert K <= MAX_CONTRACT
    return _mm_call(
        name, _NT, (M // tm, J * nt, 1),
        [pl.BlockSpec((tm, K), lambda i, j, k: (i, 0)),
         pl.BlockSpec((None, None, tn, K), lambda i, j, k: (l, j // nt, j % nt, 0))],
        pl.BlockSpec((tm, tn), lambda i, j, k: (i, j)),
        jax.ShapeDtypeStruct((M, J * n), out_dtype), (a, wt), deps)


def mmt_dx(name, dy, wt, l, out_dtype, n=None, deps=()):
    _, J, rows, K = wt.shape
    n = rows if n is None else n
    M = dy.shape[0]
    tm, tq, tc = _pick(M, _TILES), _pick(K, _TILES), _pick(n, _TILES)
    nc = n // tc
    return _mm_call(
        name, _NN, (M // tm, K // tq, J * nc),
        [pl.BlockSpec((tm, tc), lambda i, q, c: (i, c)),
         pl.BlockSpec((None, None, tc, tq), lambda i, q, c: (l, c // nc, c % nc, q))],
        pl.BlockSpec((tm, tq), lambda i, q, c: (i, q)),
        jax.ShapeDtypeStruct((M, K), out_dtype), (dy, wt), deps)


def mmt_dw(name, dy, x, J, out_dtype, deps=()):
    M, K = x.shape
    n = dy.shape[1] // J
    tn, tp = _pick(n, _TILES), _pick(K, _TILES)
    nt = n // tn
    assert M <= MAX_CONTRACT
    return _mm_call(
        name, _TN, (J * nt, K // tp, 1),
        [pl.BlockSpec((M, tn), lambda j, i, r: (0, j)),
         pl.BlockSpec((M, tp), lambda j, i, r: (0, i))],
        pl.BlockSpec((None, tn, tp), lambda j, i, r: (j // nt, j % nt, i)),
        jax.ShapeDtypeStruct((J, n, K), out_dtype), (dy, x), deps)


def ffn_gate_up(name, a, wg, wu):
    _, J, n, K = wg.shape
    M = a.shape[0]
    tm, tn = _pick(M, _TILES), _pick(n, _TILES)
    nt = n // tn
    assert K <= MAX_CONTRACT

    def body(a_ref, wg_ref, wu_ref, g_ref, u_ref, act_ref):
        x = a_ref[...]
        g = lax.dot_general(x, wg_ref[...], _NT, preferred_element_type=F32)
        u = lax.dot_general(x, wu_ref[...], _NT, preferred_element_type=F32)
        g_ref[...] = g.astype(g_ref.dtype)
        u_ref[...] = u.astype(u_ref.dtype)
        act_ref[...] = (_silu_and_grad(g)[0] * u).astype(act_ref.dtype)

    w_spec = pl.BlockSpec((None, None, tn, K), lambda i, j: (0, j // nt, j % nt, 0))
    out = pl.BlockSpec((tm, tn), lambda i, j: (i, j))
    return pl.pallas_call(
        body, name=name, grid=(M // tm, J * nt),
        in_specs=[pl.BlockSpec((tm, K), lambda i, j: (i, 0)), w_spec, w_spec],
        out_specs=[out] * 3, out_shape=[jax.ShapeDtypeStruct((M, J * n), BF16)] * 3,
        compiler_params=_params("parallel", "parallel"))(a, wg, wu)


def ffn_down_bwd(name, dy, wd, g, u, deps=()):
    _, _, K, n = wd.shape
    M = dy.shape[0]
    tm, tq = _pick(M, _TILES), _pick(K, _TILES)
    assert n <= MAX_CONTRACT

    def body(dy_ref, w_ref, g_ref, u_ref, *rest):
        dg_ref, du_ref = rest[len(deps):]
        dact = lax.dot_general(dy_ref[...], w_ref[...], _NT, preferred_element_type=F32)
        silu, dsilu = _silu_and_grad(g_ref[...].astype(F32))
        dg_ref[...] = (dact * u_ref[...].astype(F32) * dsilu).astype(dg_ref.dtype)
        du_ref[...] = (dact * silu).astype(du_ref.dtype)

    blk = pl.BlockSpec((tm, tq), lambda i, q: (i, q))
    return pl.pallas_call(
        body, name=name, grid=(M // tm, K // tq),
        in_specs=[pl.BlockSpec((tm, n), lambda i, q: (i, 0)),
                  pl.BlockSpec((None, None, tq, n), lambda i, q: (0, 0, q, 0)), blk, blk] + [_ANY] * len(deps),
        out_specs=[blk, blk], out_shape=[jax.ShapeDtypeStruct((M, K), BF16)] * 2,
        compiler_params=_params("parallel", "parallel"))(dy, wd, g, u, *deps)


def rowwise(name, fn, rows, ins, outs, tr=256, deps=()):
    widest = max([s[1].shape[1] if s[0] != "col" else s[3] for s in ins] + [s[1] for s in outs])
    tr = min(tr if widest <= 2048 else tr // 2, rows)
    in_specs, args = [], []
    for spec in ins:
        kind, a = spec[0], spec[1]
        if kind == "row":
            in_specs.append(pl.BlockSpec((tr, a.shape[1]), lambda i: (i, 0)))
        elif kind == "col":
            cb, width = spec[2], spec[3]
            in_specs.append(pl.BlockSpec((tr, width), lambda i, cb=cb: (i, cb)))
        else:
            in_specs.append(pl.BlockSpec(a.shape, lambda i: (0, 0)))
        args.append(a)
    out_specs, out_shapes = [], []
    for spec in outs:
        if spec[0] == "row":
            out_specs.append(pl.BlockSpec((tr, spec[1]), lambda i: (i, 0)))
            out_shapes.append(jax.ShapeDtypeStruct((rows, spec[1]), spec[2]))
        else:
            out_specs.append(pl.BlockSpec((1, spec[1]), lambda i: (0, 0)))
            out_shapes.append(jax.ShapeDtypeStruct((1, spec[1]), F32))
    n_in = len(ins)

    def body(*refs):
        vals = fn(*[r[...] for r in refs[:n_in]])
        first = pl.program_id(0) == 0
        for r, v, spec in zip(refs[n_in + len(deps):], vals, outs):
            if spec[0] == "row":
                r[...] = v.astype(r.dtype)
            else:
                _accumulate(r, v, first)

    return pl.pallas_call(body, name=name, grid=(rows // tr,), in_specs=in_specs + [_ANY] * len(deps),
                          out_specs=out_specs, out_shape=out_shapes,
                          compiler_params=_params("arbitrary"))(*args, *deps)


def _accumulate(ref, v, first):
    @pl.when(first)
    def _():
        ref[...] = v

    @pl.when(jnp.logical_not(first))
    def _():
        ref[...] += v


def _rms(x, w):
    r = lax.rsqrt(jnp.mean(x * x, axis=-1, keepdims=True) + NORM_EPS)
    return x * r * w


def _rms_bwd(x, w, dy):
    r = lax.rsqrt(jnp.mean(x * x, axis=-1, keepdims=True) + NORM_EPS)
    g = dy * w
    dx = r * (g - x * (r * r) * jnp.mean(g * x, axis=-1, keepdims=True))
    dw = jnp.sum(dy * x * r, axis=0, keepdims=True)
    return dx, dw


def _sigmoid(x):
    return 1.0 / (1.0 + jnp.exp(-x))


def _silu_and_grad(g):
    s = _sigmoid(g)
    return g * s, s * (1.0 + g * (1.0 - s))


def _swap_pairs(x):
    n = x.shape[-1]
    lane = lax.broadcasted_iota(jnp.int32, x.shape, x.ndim - 1)
    return jnp.where((lane & 1) == 0, pltpu.roll(x, n - 1, x.ndim - 1), pltpu.roll(x, 1, x.ndim - 1))


def _rot(x, cosf, sins):
    return x * cosf + _swap_pairs(x) * sins


def _unrot(d, cosf, sins):
    return d * cosf + _swap_pairs(d * sins)


def _ret_log_gamma(h):
    vals = [math.log1p(-2.0 ** (-5.0 - i)) for i in range(RET_HEADS)]
    out = jnp.float32(vals[RET_HEADS - 1])
    for i in range(RET_HEADS - 2, -1, -1):
        out = jnp.where(h == i, jnp.float32(vals[i]), out)
    return out


def _fill_decays(dec_ref, lg):
    ri = lax.broadcasted_iota(jnp.int32, (BLK, BLK), 0)
    ci = lax.broadcasted_iota(jnp.int32, (BLK, BLK), 1)
    for d in range(dec_ref.shape[0]):
        dt = d * BLK + ri - ci
        dec_ref[d] = jnp.where(dt >= 0, jnp.exp(jnp.maximum(dt, 0).astype(F32) * lg), 0.0)


def _decay_row(dec_ref, qi):
    return jnp.concatenate([dec_ref[qi - kb] for kb in range(qi + 1)], axis=1)


def _once(block_shape, index_map):
    return pl.BlockSpec(block_shape, index_map, pipeline_mode=pl.Buffered(1))


def _dot(a, b):
    return jnp.dot(a.astype(BF16), b.astype(BF16), preferred_element_type=F32)


def _dot_nt(a, b):
    return lax.dot_general(a.astype(BF16), b.astype(BF16), _NT, preferred_element_type=F32)


def _dot_tn(a, b):
    return lax.dot_general(a.astype(BF16), b.astype(BF16), _TN, preferred_element_type=F32)


def retention_fwd(name, z, cosf, sins, width_out):
    T = z.shape[0]
    nq = T // BLK
    scale = RET_DK ** -0.5

    def body(q_ref, k_ref, v_ref, cos_ref, sin_ref, o_ref, krot, vb, dec_ref):
        _fill_decays(dec_ref, _ret_log_gamma(pl.program_id(0)))
        krot[...] = (_rot(k_ref[...], cos_ref[...], sin_ref[...]) * scale).astype(BF16)
        vb[...] = v_ref[...].astype(BF16)
        for qi in range(nq):
            rows, n = slice(qi * BLK, (qi + 1) * BLK), (qi + 1) * BLK
            q = _rot(q_ref[rows, :], cos_ref[rows, :], sin_ref[rows, :])
            s = _dot_nt(q, krot[0:n, :]) * _decay_row(dec_ref, qi)
            o_ref[rows, :] = _dot(s, vb[0:n, :])

    return pl.pallas_call(
        body, name=name, grid=(RET_HEADS,),
        in_specs=[pl.BlockSpec((T, RET_DK), lambda h: (0, OFF_RQ // RET_DK + h)),
                  pl.BlockSpec((T, RET_DK), lambda h: (0, OFF_RK // RET_DK + h)),
                  pl.BlockSpec((T, RET_DV), lambda h: (0, OFF_RV // RET_DV + h)),
                  _once((T, RET_DK), lambda h: (0, 0)), _once((T, RET_DK), lambda h: (0, 0))],
        out_specs=pl.BlockSpec((T, RET_DV), lambda h: (0, h)),
        out_shape=jax.ShapeDtypeStruct((T, width_out), F32),
        scratch_shapes=[pltpu.VMEM((T, RET_DK), BF16), pltpu.VMEM((T, RET_DV), BF16),
                        pltpu.VMEM((nq, BLK, BLK), F32)],
        compiler_params=_params("arbitrary"))(z, z, z, cosf, sins)


def retention_bwd(name, z, cosf, sins, do):
    T = z.shape[0]
    nq = T // BLK
    scale = RET_DK ** -0.5

    def body(q_ref, k_ref, v_ref, cos_ref, sin_ref, do_ref, dq_ref, dk_ref, dv_ref, krot, vb, dk_acc, dv_acc, dec_ref):
        _fill_decays(dec_ref, _ret_log_gamma(pl.program_id(0)))
        krot[...] = (_rot(k_ref[...], cos_ref[...], sin_ref[...]) * scale).astype(BF16)
        vb[...] = v_ref[...].astype(BF16)
        dk_acc[...] = jnp.zeros_like(dk_acc)
        dv_acc[...] = jnp.zeros_like(dv_acc)
        for qi in range(nq):
            rows, n = slice(qi * BLK, (qi + 1) * BLK), (qi + 1) * BLK
            cos_q, sin_q = cos_ref[rows, :], sin_ref[rows, :]
            q = _rot(q_ref[rows, :], cos_q, sin_q).astype(BF16)
            dout = do_ref[rows, :].astype(BF16)
            kk, vv, dec = krot[0:n, :], vb[0:n, :], _decay_row(dec_ref, qi)
            p = (_dot_nt(q, kk) * dec).astype(BF16)
            ds = (_dot_nt(dout, vv) * dec).astype(BF16)
            dq_ref[rows, :] = _unrot(_dot(ds, kk), cos_q, sin_q).astype(dq_ref.dtype)
            dk_acc[0:n, :] += _dot_tn(ds, q)
            dv_acc[0:n, :] += _dot_tn(p, dout)
        dk_ref[...] = (_unrot(dk_acc[...], cos_ref[...], sin_ref[...]) * scale).astype(dk_ref.dtype)
        dv_ref[...] = dv_acc[...].astype(dv_ref.dtype)

    head = lambda h: (0, h)
    return pl.pallas_call(
        body, name=name, grid=(RET_HEADS,),
        in_specs=[pl.BlockSpec((T, RET_DK), lambda h: (0, OFF_RQ // RET_DK + h)),
                  pl.BlockSpec((T, RET_DK), lambda h: (0, OFF_RK // RET_DK + h)),
                  pl.BlockSpec((T, RET_DV), lambda h: (0, OFF_RV // RET_DV + h)),
                  _once((T, RET_DK), lambda h: (0, 0)), _once((T, RET_DK), lambda h: (0, 0)),
                  pl.BlockSpec((T, RET_DV), head)],
        out_specs=[pl.BlockSpec((T, RET_DK), head), pl.BlockSpec((T, RET_DK), head), pl.BlockSpec((T, RET_DV), head)],
        out_shape=[jax.ShapeDtypeStruct((T, RET_QK), BF16), jax.ShapeDtypeStruct((T, RET_QK), BF16),
                   jax.ShapeDtypeStruct((T, RET_V), BF16)],
        scratch_shapes=[pltpu.VMEM((T, RET_DK), BF16), pltpu.VMEM((T, RET_DV), BF16),
                        pltpu.VMEM((T, RET_DK), F32), pltpu.VMEM((T, RET_DV), F32),
                        pltpu.VMEM((nq, BLK, BLK), F32)],
        compiler_params=_params("arbitrary"))(z, z, z, cosf, sins, do)


GLA_PAIR = 2


def _gla_chunk(q_ref, k_ref, v_ref, glr_ref, gu, gb, rows, hh, trilf):
    ck = slice(hh * GLA_DK, (hh + 1) * GLA_DK)
    zg = _dot(glr_ref[rows, :], gu[:, ck]) + gb[:, ck]
    la = (jnp.minimum(zg, 0.0) - jnp.log(1.0 + jnp.exp(-jnp.abs(zg)))) * (1.0 / GLA_GATE_NORM)
    cum = jnp.dot(trilf, la, precision=HIGHEST, preferred_element_type=F32)
    last = jnp.sum(la, axis=0, keepdims=True)
    ecum = jnp.exp(cum)
    k = k_ref[rows, ck]
    qt = q_ref[rows, ck] * (GLA_DK ** -0.5) * ecum
    kt = k * jnp.exp(-cum)
    kh = k * jnp.exp(last - cum)
    return zg, cum, last, ecum, qt, kt, kh, v_ref[rows, hh * GLA_DV:(hh + 1) * GLA_DV].astype(BF16)


def _state_decay(last):
    e = jnp.exp(jnp.broadcast_to(last, (GLA_DK, GLA_DK)).T)
    return jnp.concatenate([e] * (GLA_DV // GLA_DK), axis=1)


def _gla_specs(T):
    wk, wv = GLA_PAIR * GLA_DK, GLA_PAIR * GLA_DV
    return [_once((T, wk), lambda h: (0, OFF_GQ // wk + h)),
            _once((T, wk), lambda h: (0, OFF_GK // wk + h)),
            _once((T, wv), lambda h: (0, OFF_GV // wv + h)),
            _once((T, LANE), lambda h: (0, 0)),
            pl.BlockSpec((LANE, wk), lambda h: (0, h)),
            pl.BlockSpec((1, wk), lambda h: (0, h))]


def gla_fwd(name, z, glr, gu, gb, o_prev):
    T = z.shape[0]
    nc = T // CHUNK
    wv = GLA_PAIR * GLA_DV

    def body(q_ref, k_ref, v_ref, glr_ref, gu_ref, gb_ref, prev_ref, o_ref, S):
        del prev_ref
        gu_b, gb_v = gu_ref[...].astype(BF16), gb_ref[...]
        ri = lax.broadcasted_iota(jnp.int32, (CHUNK, CHUNK), 0)
        ci = lax.broadcasted_iota(jnp.int32, (CHUNK, CHUNK), 1)
        tril = ri >= ci
        trilf = tril.astype(F32)
        S[...] = jnp.zeros_like(S)

        def step(c, carry):
            rows = pl.ds(pl.multiple_of(c * CHUNK, CHUNK), CHUNK)
            for hh in range(GLA_PAIR):
                _, _, last, _, qt, kt, kh, v = _gla_chunk(q_ref, k_ref, v_ref, glr_ref, gu_b, gb_v, rows, hh, trilf)
                a = jnp.where(tril, _dot_nt(qt, kt), 0.0)
                s_prev = S[hh]
                o_ref[rows, hh * GLA_DV:(hh + 1) * GLA_DV] = _dot(a, v) + _dot(qt, s_prev)
                S[hh] = s_prev * _state_decay(last) + _dot_tn(kh, v)
            return carry

        lax.fori_loop(0, nc, step, 0)

    n_in = 6
    return pl.pallas_call(
        body, name=name, grid=(GLA_HEADS // GLA_PAIR,),
        in_specs=_gla_specs(T) + [pl.BlockSpec(memory_space=pl.ANY)],
        out_specs=pl.BlockSpec((T, wv), lambda h: (0, RET_V // wv + h)),
        out_shape=jax.ShapeDtypeStruct(o_prev.shape, F32),
        scratch_shapes=[pltpu.VMEM((GLA_PAIR, GLA_DK, GLA_DV), F32)],
        input_output_aliases={n_in: 0},
        compiler_params=_params("arbitrary"))(z, z, z, glr, gu, gb, o_prev)


def gla_bwd(name, z, glr, gu, gb, do):
    T = z.shape[0]
    nc = T // CHUNK

    def body(q_ref, k_ref, v_ref, glr_ref, gu_ref, gb_ref, do_ref,
             dq_ref, dk_ref, dv_ref, dglr_ref, dgu_ref, dgb_ref, s_all, dS):
        gu_b, gb_v = gu_ref[...].astype(BF16), gb_ref[...]
        ri = lax.broadcasted_iota(jnp.int32, (CHUNK, CHUNK), 0)
        ci = lax.broadcasted_iota(jnp.int32, (CHUNK, CHUNK), 1)
        tril = ri >= ci
        trilf = tril.astype(F32)
        triuf = (ri <= ci).astype(F32)
        last_row = lax.broadcasted_iota(jnp.int32, (CHUNK, GLA_DK), 0) == CHUNK - 1
        ones8 = jnp.ones((8, GLA_DV), F32)

        def fstep(c, carry):
            rows = pl.ds(pl.multiple_of(c * CHUNK, CHUNK), CHUNK)
            for hh in range(GLA_PAIR):
                s_prev = dS[hh]
                s_all[hh, c] = s_prev
                _, _, last, _, _, _, kh, v = _gla_chunk(q_ref, k_ref, v_ref, glr_ref, gu_b, gb_v, rows, hh, trilf)
                dS[hh] = s_prev * _state_decay(last) + _dot_tn(kh, v)
            return carry

        dS[...] = jnp.zeros_like(dS)
        lax.fori_loop(0, nc, fstep, 0)
        dS[...] = jnp.zeros_like(dS)
        dgu_ref[...] = jnp.zeros_like(dgu_ref)
        dgb_ref[...] = jnp.zeros_like(dgb_ref)

        def bstep(i, carry):
            c = nc - 1 - i
            rows = pl.ds(pl.multiple_of(c * CHUNK, CHUNK), CHUNK)
            glr_c = glr_ref[rows, :]
            for hh in range(GLA_PAIR):
                ck, cv = slice(hh * GLA_DK, (hh + 1) * GLA_DK), slice(hh * GLA_DV, (hh + 1) * GLA_DV)
                zg, cum, last, ecum, qt, kt, kh, v = _gla_chunk(q_ref, k_ref, v_ref, glr_ref, gu_b, gb_v, rows, hh, trilf)
                a = jnp.where(tril, _dot_nt(qt, kt), 0.0)
                s_prev, ds_new = s_all[hh, c], dS[hh]
                dout = do_ref[rows, cv].astype(BF16)
                dv_ref[rows, cv] = (_dot_tn(a, dout) + _dot(kh, ds_new)).astype(dv_ref.dtype)
                da = jnp.where(tril, _dot_nt(dout, v), 0.0)
                dqt = _dot(da, kt) + _dot_nt(dout, s_prev)
                dkt = _dot_tn(da, qt)
                dkh = _dot_nt(v, ds_new)
                dS[hh] = ds_new * _state_decay(last) + _dot_tn(qt, dout)
                dq_ref[rows, ck] = (dqt * ecum * (GLA_DK ** -0.5)).astype(dq_ref.dtype)
                dk_ref[rows, ck] = (dkt * jnp.exp(-cum) + dkh * jnp.exp(last - cum)).astype(dk_ref.dtype)
                dkh_kh = dkh * kh
                dcum = dqt * qt - dkt * kt - dkh_kh
                rs = lax.dot_general(ones8, ds_new * s_prev, _NT, precision=HIGHEST, preferred_element_type=F32)
                dlast = (jnp.sum(dkh_kh, axis=0, keepdims=True)
                         + jnp.exp(last) * (jnp.sum(rs, axis=0, keepdims=True) * 0.125))
                dcum = dcum + jnp.where(last_row, dlast, 0.0)
                dla = jnp.dot(triuf, dcum, precision=HIGHEST, preferred_element_type=F32)
                dzg = dla * (1.0 / GLA_GATE_NORM) * _sigmoid(-zg)
                dglr_ref[hh, rows, :] = _dot_nt(dzg, gu_b[:, ck])
                dgu_ref[:, ck] += _dot_tn(glr_c, dzg)
                dgb_ref[:, ck] += jnp.sum(dzg, axis=0, keepdims=True)
            return carry

        lax.fori_loop(0, nc, bstep, 0)

    wk, wv = GLA_PAIR * GLA_DK, GLA_PAIR * GLA_DV
    return pl.pallas_call(
        body, name=name, grid=(GLA_HEADS // GLA_PAIR,),
        in_specs=_gla_specs(T) + [_once((T, wv), lambda h: (0, RET_V // wv + h))],
        out_specs=[pl.BlockSpec((T, wk), lambda h: (0, h)), pl.BlockSpec((T, wk), lambda h: (0, h)),
                   pl.BlockSpec((T, wv), lambda h: (0, h)),
                   pl.BlockSpec((GLA_PAIR, T, LANE), lambda h: (h, 0, 0)),
                   pl.BlockSpec((LANE, wk), lambda h: (0, h)), pl.BlockSpec((1, wk), lambda h: (0, h))],
        out_shape=[jax.ShapeDtypeStruct((T, GLA_QK), BF16), jax.ShapeDtypeStruct((T, GLA_QK), BF16),
                   jax.ShapeDtypeStruct((T, GLA_V), BF16), jax.ShapeDtypeStruct((GLA_HEADS, T, LANE), F32),
                   jax.ShapeDtypeStruct((LANE, GLA_QK), F32), jax.ShapeDtypeStruct((1, GLA_QK), F32)],
        scratch_shapes=[pltpu.VMEM((GLA_PAIR, nc, GLA_DK, GLA_DV), F32), pltpu.VMEM((GLA_PAIR, GLA_DK, GLA_DV), F32)],
        compiler_params=_params("arbitrary"))(z, z, z, glr, gu, gb, do)


HN_HEADS = RET_HEADS + GLA_HEADS
HN_W = RET_DV


def _gate_col(h):
    return jnp.where(h < RET_HEADS, OFF_RG // HN_W + h, OFF_GG // HN_W + h - RET_HEADS)


def headnorm_fwd(name, oraw, z, w, tr=256):
    T = oraw.shape[0]

    def body(o_ref, g_ref, w_ref, y_ref):
        y_ref[...] = (_rms(o_ref[...], w_ref[...]) * _silu_and_grad(g_ref[...])[0]).astype(y_ref.dtype)

    return pl.pallas_call(
        body, name=name, grid=(HN_HEADS, T // tr),
        in_specs=[pl.BlockSpec((tr, HN_W), lambda h, i: (i, h)),
                  pl.BlockSpec((tr, HN_W), lambda h, i: (i, _gate_col(h))),
                  pl.BlockSpec((1, HN_W), lambda h, i: (0, h))],
        out_specs=pl.BlockSpec((tr, HN_W), lambda h, i: (i, h)),
        out_shape=jax.ShapeDtypeStruct((T, HN_HEADS * HN_W), BF16),
        compiler_params=_params("arbitrary", "arbitrary"))(oraw, z, w)


def headnorm_bwd(name, oraw, z, w, dy, tr=256):
    T = oraw.shape[0]

    def body(o_ref, g_ref, w_ref, dy_ref, do_ref, dg_ref, dw_ref):
        o, wv, dyv = o_ref[...], w_ref[...], dy_ref[...].astype(F32)
        silu, dsilu = _silu_and_grad(g_ref[...])
        n = _rms(o, wv)
        dg_ref[...] = (dyv * n * dsilu).astype(dg_ref.dtype)
        dx, dw = _rms_bwd(o, wv, dyv * silu)
        do_ref[...] = dx
        _accumulate(dw_ref, dw, pl.program_id(1) == 0)

    blk = pl.BlockSpec((tr, HN_W), lambda h, i: (i, h))
    return pl.pallas_call(
        body, name=name, grid=(HN_HEADS, T // tr),
        in_specs=[blk, pl.BlockSpec((tr, HN_W), lambda h, i: (i, _gate_col(h))),
                  pl.BlockSpec((1, HN_W), lambda h, i: (0, h)), blk],
        out_specs=[blk, blk, pl.BlockSpec((1, HN_W), lambda h, i: (0, h))],
        out_shape=[jax.ShapeDtypeStruct((T, HN_HEADS * HN_W), F32),
                   jax.ShapeDtypeStruct((T, HN_HEADS * HN_W), BF16),
                   jax.ShapeDtypeStruct((1, HN_HEADS * HN_W), F32)],
        compiler_params=_params("arbitrary", "arbitrary"))(oraw, z, w, dy)


N_MASKS = 4


def _check_mask_classes(T):
    for window, dilation in DILATED_BRANCHES[:-1]:
        assert window < (N_MASKS - 1) * BLK - (BLK - 1) and BLK % dilation == 0
    assert DILATED_BRANCHES[-1][0] >= T and BLK % DILATED_BRANCHES[-1][1] == 0


def _fill_masks(mult_ref, bias_ref):
    ri = lax.broadcasted_iota(jnp.int32, (BLK, BLK), 0)
    ci = lax.broadcasted_iota(jnp.int32, (BLK, BLK), 1)
    for d in range(N_MASKS):
        dt = d * BLK + ri - ci
        mult = jnp.zeros((BLK, BLK), F32)
        for window, dilation in DILATED_BRANCHES:
            hit = (dt >= 0) & (dt <= window) & ((dt & (dilation - 1)) == 0)
            mult = mult + hit.astype(F32)
        mult_ref[d] = mult
        bias_ref[d] = jnp.where(mult > 0, 0.0, -1e30)


def _mask_row(ref, qi):
    return jnp.concatenate([ref[min(qi - kb, N_MASKS - 1)] for kb in range(qi + 1)], axis=1)


def attn_fwd(name, qkv):
    T = qkv.shape[0]
    D = qkv.shape[1] // 3
    dh = D // ATT_HEADS
    nq = T // BLK
    scale = dh ** -0.5

    _check_mask_classes(T)

    def body(q_ref, k_ref, v_ref, o_ref, lse_ref, mult_ref, bias_ref):
        @pl.when(pl.program_id(0) == 0)
        def _():
            _fill_masks(mult_ref, bias_ref)

        for qi in range(nq):
            rows, n = slice(qi * BLK, (qi + 1) * BLK), (qi + 1) * BLK
            s = (_dot_nt(q_ref[rows, :], k_ref[0:n, :]) * scale
                 + _mask_row(bias_ref, qi))
            m = jnp.max(s, axis=-1, keepdims=True)
            p = _mask_row(mult_ref, qi) * jnp.exp(s - m)
            l = jnp.sum(p, axis=-1, keepdims=True)
            o_ref[rows, :] = (_dot(p, v_ref[0:n, :]) / l).astype(o_ref.dtype)
            lse_ref[rows, :] = jnp.broadcast_to(m + jnp.log(l), (BLK, LANE))

    return pl.pallas_call(
        body, name=name, grid=(ATT_HEADS,),
        in_specs=[pl.BlockSpec((T, dh), lambda h: (0, h)),
                  pl.BlockSpec((T, dh), lambda h: (0, ATT_HEADS + h)),
                  pl.BlockSpec((T, dh), lambda h: (0, 2 * ATT_HEADS + h))],
        out_specs=[pl.BlockSpec((T, dh), lambda h: (0, h)),
                   pl.BlockSpec((None, T, LANE), lambda h: (h, 0, 0))],
        out_shape=[jax.ShapeDtypeStruct((T, D), BF16), jax.ShapeDtypeStruct((ATT_HEADS, T, LANE), F32)],
        scratch_shapes=[pltpu.VMEM((N_MASKS, BLK, BLK), F32), pltpu.VMEM((N_MASKS, BLK, BLK), F32)],
        compiler_params=_params("arbitrary"))(qkv, qkv, qkv)


def attn_bwd(name, qkv, o, lse, do):
    T = qkv.shape[0]
    D = qkv.shape[1] // 3
    dh = D // ATT_HEADS
    nq = T // BLK
    scale = dh ** -0.5

    _check_mask_classes(T)

    def body(q_ref, k_ref, v_ref, o_ref, lse_ref, do_ref, dq_ref, dk_ref, dv_ref, dk_acc, dv_acc, mult_ref, bias_ref):
        @pl.when(pl.program_id(0) == 0)
        def _():
            _fill_masks(mult_ref, bias_ref)

        dk_acc[...] = jnp.zeros_like(dk_acc)
        dv_acc[...] = jnp.zeros_like(dv_acc)
        for qi in range(nq):
            rows, n = slice(qi * BLK, (qi + 1) * BLK), (qi + 1) * BLK
            q, dout = q_ref[rows, :], do_ref[rows, :]
            kk, vv = k_ref[0:n, :], v_ref[0:n, :]
            delta = jnp.sum(dout.astype(F32) * o_ref[rows, :].astype(F32), axis=-1, keepdims=True)
            lse = jnp.max(lse_ref[rows, :], axis=-1, keepdims=True)
            s = _dot_nt(q, kk) * scale + _mask_row(bias_ref, qi)
            p = _mask_row(mult_ref, qi) * jnp.exp(s - lse)
            ds = (p * (_dot_nt(dout, vv) - delta) * scale).astype(BF16)
            dq_ref[rows, :] = _dot(ds, kk).astype(dq_ref.dtype)
            dk_acc[0:n, :] += _dot_tn(ds, q)
            dv_acc[0:n, :] += _dot_tn(p, dout)
        dk_ref[...] = dk_acc[...].astype(dk_ref.dtype)
        dv_ref[...] = dv_acc[...].astype(dv_ref.dtype)

    full = pl.BlockSpec((T, dh), lambda h: (0, h))
    return pl.pallas_call(
        body, name=name, grid=(ATT_HEADS,),
        in_specs=[full, pl.BlockSpec((T, dh), lambda h: (0, ATT_HEADS + h)),
                  pl.BlockSpec((T, dh), lambda h: (0, 2 * ATT_HEADS + h)),
                  full, pl.BlockSpec((None, T, LANE), lambda h: (h, 0, 0)), full],
        out_specs=[full, full, full],
        out_shape=[jax.ShapeDtypeStruct((T, D), BF16)] * 3,
        scratch_shapes=[pltpu.VMEM((T, dh), F32), pltpu.VMEM((T, dh), F32),
                        pltpu.VMEM((N_MASKS, BLK, BLK), F32), pltpu.VMEM((N_MASKS, BLK, BLK), F32)],
        compiler_params=_params("arbitrary"))(qkv, qkv, qkv, o, lse, do)


def _mesh_pos():
    mx, my, mc = lax.axis_index("x"), lax.axis_index("y"), lax.axis_index("c")
    return mx, my, mc, 4 * mx + 2 * my + mc


def _peer(k, mx, my, mc):
    px, py, pc = mx ^ (k >> 2), my ^ ((k >> 1) & 1), mc ^ (k & 1)
    return (px, py, pc), 4 * px + 2 * py + pc


_SIBLING = 1
_OTHER_CHIPS = (4, 2, 6)
N_CHIP = N_DEV // 2
_PLANS = {"gather": (2, N_DEV - 1), "to_chips": (2, 1 + len(_OTHER_CHIPS)), "pass_on": (1, len(_OTHER_CHIPS)),
          "halves": (2, N_CHIP), "chip_sums": (2, len(_OTHER_CHIPS))}


def _copies(kind, items, send_sems, recv_sems):
    mx, my, mc, me = _mesh_pos()
    out = []

    def add(n, src, dst, peer):
        out.append(pltpu.make_async_remote_copy(
            src_ref=src, dst_ref=dst, send_sem=send_sems.at[n], recv_sem=recv_sems.at[n],
            device_id=peer, device_id_type=pl.DeviceIdType.MESH))

    per_item = _PLANS[kind][1]
    sibling = _peer(_SIBLING, mx, my, mc)[0]
    for i, refs in enumerate(items):
        n = i * per_item
        if kind == "gather":
            for k in range(1, N_DEV):
                add(n + k - 1, refs[0], refs[1].at[me], _peer(k, mx, my, mc)[0])
        elif kind == "to_chips":
            for j, k in enumerate((_SIBLING,) + _OTHER_CHIPS):
                add(n + j, refs[0], refs[1].at[me], _peer(k, mx, my, mc)[0])
        elif kind == "pass_on":
            for j, k in enumerate(_OTHER_CHIPS):
                add(n + j, refs[0].at[me ^ k], refs[0].at[me ^ k], sibling)
        elif kind == "halves":
            for chip in range(N_CHIP):
                add(n + chip, refs[0].at[2 * chip + 1 - mc], refs[1].at[chip], sibling)
        else:
            for j, k in enumerate(_OTHER_CHIPS):
                peer, to = _peer(k, mx, my, mc)
                add(n + j, refs[0].at[to // 2], refs[1].at[me // 2], peer)
    return out


_HBM = pl.BlockSpec(memory_space=pltpu.HBM)
_SEM = pl.BlockSpec(memory_space=pltpu.SEMAPHORE)
_DATAFLOW = pltpu.SideEffectType.DATAFLOW_SIDE_EFFECTING


def exchange_call(name, waits, starts, deps=()):
    bufs, slot_of = [], {}

    def slots(items):
        out = []
        for item in items:
            for b in item:
                if id(b) not in slot_of:
                    slot_of[id(b)] = len(bufs)
                    bufs.append(b)
            out.append(tuple(slot_of[id(b)] for b in item))
        return out

    wait_plan = [(kind, slots(handle[0])) for kind, handle in waits]
    start_plan = [(kind, slots(items)) for kind, items in starts]
    wait_sems = [s for _, handle in waits for s in handle[1:]]
    n_buf, n_ws, n_start = len(bufs), len(wait_sems), len(starts)

    def body(*refs):
        buf_refs, sems_in = refs[:n_buf], refs[n_buf:n_buf + n_ws]
        outs = refs[n_buf + n_ws + len(deps):]
        pick = lambda plan: [tuple(buf_refs[s] for s in item) for item in plan]
        for wi, (kind, plan) in enumerate(wait_plan):
            copies = _copies(kind, pick(plan), sems_in[2 * wi], sems_in[2 * wi + 1])
            for cp in copies:
                cp.wait_send()
            for cp in copies:
                cp.wait_recv()
        for si, (kind, plan) in enumerate(start_plan):
            for cp in _copies(kind, pick(plan), outs[2 * si], outs[2 * si + 1]):
                cp.start()
        outs[-1][...] = jnp.zeros_like(outs[-1])

    hbm_bufs = [pltpu.with_memory_space_constraint(b, pltpu.HBM) for b in bufs]
    sem_shapes = []
    for kind, plan in start_plan:
        sem_shapes += [pltpu.SemaphoreType.DMA((len(plan) * _PLANS[kind][1],))] * 2
    outs = pl.pallas_call(
        body, name=name,
        out_shape=sem_shapes + [pltpu.HBM(b.shape, b.dtype) for b in bufs] + [jax.ShapeDtypeStruct((8, LANE), F32)],
        in_specs=[_HBM] * n_buf + [_SEM] * n_ws + [_ANY] * len(deps),
        out_specs=[_SEM] * (2 * n_start) + [_HBM] * n_buf + [pl.BlockSpec(memory_space=pltpu.VMEM)],
        input_output_aliases={i: 2 * n_start + i for i in range(n_buf)},
        compiler_params=pltpu.CompilerParams(has_side_effects=_DATAFLOW))(*hbm_bufs, *wait_sems, *deps)
    sems, thru, token = outs[:2 * n_start], outs[2 * n_start:-1], outs[-1]
    through = lambda plan: [tuple(thru[s] for s in item) for item in plan]
    waited = [through(plan) for _, plan in wait_plan]
    handles = [(through(plan), sems[2 * si], sems[2 * si + 1]) for si, (_, plan) in enumerate(start_plan)]
    return waited, handles, token


def gather_small(name, a, deps=()):
    def body(a_ref, *rest):
        o_ref, send_sems, recv_sems, local_sem = rest[len(deps):]
        me = _mesh_pos()[3]
        own = pltpu.make_async_copy(a_ref, o_ref.at[me], local_sem)
        own.start()
        copies = _copies("gather", [(a_ref, o_ref)], send_sems, recv_sems)
        for cp in copies:
            cp.start()
        for cp in copies:
            cp.wait_recv()
        for cp in copies:
            cp.wait_send()
        own.wait()

    return pl.pallas_call(
        body, name=name, in_specs=[_ANY] * (1 + len(deps)), out_specs=_ANY,
        out_shape=jax.ShapeDtypeStruct((N_DEV,) + a.shape, a.dtype),
        scratch_shapes=[pltpu.SemaphoreType.DMA((N_DEV - 1,)), pltpu.SemaphoreType.DMA((N_DEV - 1,)),
                        pltpu.SemaphoreType.DMA],
        compiler_params=pltpu.CompilerParams(has_side_effects=True))(a, *deps)


def _adamw_math(w, g, m, v):
    m2 = ADAM_B1 * m + (1.0 - ADAM_B1) * g
    v2 = ADAM_B2 * v + (1.0 - ADAM_B2) * (g * g)
    m_hat = m2 / (1.0 - ADAM_B1 ** ADAM_STEP)
    v_hat = v2 / (1.0 - ADAM_B2 ** ADAM_STEP)
    delta = -ADAM_LR * (m_hat / (jnp.sqrt(v_hat) + ADAM_EPS) + ADAM_WD * w)
    return delta, m2, v2


def chip_sum(name, a, half):
    _, r, c = a.shape
    tr = _pick(r, (256, 128, 64, 32, 16))

    def body(a_ref, h_ref, o_ref):
        o_ref[...] = (a_ref[...].astype(F32) + h_ref[...].astype(F32)).astype(o_ref.dtype)

    blk = pl.BlockSpec((N_CHIP, tr, c), lambda i: (0, i, 0))
    return pl.pallas_call(
        body, name=name, grid=(r // tr,),
        in_specs=[pl.BlockSpec((N_CHIP, None, tr, c), lambda i: (0, lax.axis_index("c"), i, 0)), blk],
        out_specs=blk, out_shape=jax.ShapeDtypeStruct((N_CHIP, r, c), BF16),
        compiler_params=_params("parallel"))(a.reshape(N_CHIP, 2, r, c), half)


def adamw(name, w, m, v, l, land, own, prev=None):
    L, r, c = w.shape
    cp = land.shape[2]
    tr = _pick(r, (256, 176, 128, 64, 32, 16, 8))

    def body(w_ref, m_ref, v_ref, land_ref, own_ref, *rest):
        g_ref, d_ref, m2_ref, v2_ref = rest[-4:]
        chip = _mesh_pos()[3] // 2
        mine = own_ref[:, pl.ds(0, c)].astype(F32)
        g = None
        for s in range(N_CHIP):
            part = jnp.where(chip == s, mine, land_ref[s, :, pl.ds(0, c)].astype(F32))
            g = part if g is None else g + part
        delta, m2, v2 = _adamw_math(w_ref[...], g, m_ref[...], v_ref[...])
        g_ref[...] = g
        d_ref[...] = delta
        m2_ref[...] = m2
        v2_ref[...] = v2

    blk = pl.BlockSpec((None, tr, c), lambda i: (l, i, 0))
    shape = jax.ShapeDtypeStruct((L, r, c), F32)
    extra = [] if prev is None else list(prev)
    return pl.pallas_call(
        body, name=name, grid=(r // tr,),
        in_specs=[blk, blk, blk, pl.BlockSpec((N_CHIP, tr, cp), lambda i: (0, i, 0)),
                  pl.BlockSpec((None, tr, cp), lambda i: (_mesh_pos()[3] // 2, i, 0))] + [_ANY] * len(extra),
        out_specs=[blk] * 4, out_shape=[shape] * 4,
        input_output_aliases={5 + k: k for k in range(len(extra))},
        compiler_params=_params("parallel"))(w, m, v, land, own, *extra)


def adamw_small(name, w, m, v, parts):
    n = w.shape[1]

    def body(w_ref, m_ref, v_ref, p_ref, g_ref, d_ref, m2_ref, v2_ref):
        g = p_ref[0:1, :]
        for s in range(1, N_DEV):
            g = g + p_ref[s:s + 1, :]
        delta, m2, v2 = _adamw_math(w_ref[...], g, m_ref[...], v_ref[...])
        g_ref[...] = g
        d_ref[...] = delta
        m2_ref[...] = m2
        v2_ref[...] = v2

    shape = jax.ShapeDtypeStruct((1, n), F32)
    return pl.pallas_call(body, name=name, out_shape=[shape] * 4,
                          compiler_params=pltpu.CompilerParams(vmem_limit_bytes=VMEM_LIMIT_BYTES))(w, m, v, parts)


def _rope_tables(positions):
    half = RET_DK // 2
    inv_freq = 1.0 / jnp.power(RET_THETA_BASE, jnp.linspace(0.0, 1.0, half, dtype=F32))
    ang = positions.astype(F32)[:, None] * inv_freq
    cos, sin = jnp.cos(ang), jnp.sin(ang)
    cosf = jnp.repeat(cos, 2, axis=-1)
    sins = jnp.stack([-sin, sin], axis=-1).reshape(cosf.shape)
    return cosf, sins


def _pad_to(a, axis, size):
    pad = [(0, 0)] * a.ndim
    pad[axis] = (0, size - a.shape[axis])
    return jnp.pad(a, pad)


def _round_up(n, m):
    return -(-n // m) * m


def kernel(x, p, positions, attn_norm_w, ffn_norm_w, ple_norm_w, final_norm_w, ab_w_in, ab_gla_gate_up, ab_gla_gate_b, ab_ret_norm_w, ab_gla_norm_w, ab_w_out, c_w_qkv, c_w_out, ffn_w_gate, ffn_w_up, ffn_w_down, ple_w_proj, ple_w_gate, loss_target, m_attn_norm_w, m_ffn_norm_w, m_ple_norm_w, m_final_norm_w, m_ab_w_in, m_ab_gla_gate_up, m_ab_gla_gate_b, m_ab_ret_norm_w, m_ab_gla_norm_w, m_ab_w_out, m_c_w_qkv, m_c_w_out, m_ffn_w_gate, m_ffn_w_up, m_ffn_w_down, m_ple_w_proj, m_ple_w_gate, v_attn_norm_w, v_ffn_norm_w, v_ple_norm_w, v_final_norm_w, v_ab_w_in, v_ab_gla_gate_up, v_ab_gla_gate_b, v_ab_ret_norm_w, v_ab_gla_norm_w, v_ab_w_out, v_c_w_qkv, v_c_w_out, v_ffn_w_gate, v_ffn_w_up, v_ffn_w_down, v_ple_w_proj, v_ple_w_gate):
    T, D = x.shape[1], x.shape[2]
    depth = attn_norm_w.shape[0]
    assert ab_w_in.shape[0] == 1 and c_w_qkv.shape[0] == 1 and depth == 2, "one even and one odd layer"
    me = 4 * lax.axis_index("x") + 2 * lax.axis_index("y") + lax.axis_index("c")
    in_shard = ab_w_in.shape[2]
    in_width = in_shard * N_DEV
    assert in_width == OFF_LR + GLA_GATE_RANK
    fs = ffn_w_gate.shape[2]
    fp = _round_up(fs, LANE)
    gu_cols = ab_gla_gate_up.shape[2]

    bf = lambda a: a.astype(BF16)
    tr_ = lambda a: jnp.swapaxes(a, -1, -2)
    wg_t, wu_t = tr_(ffn_w_gate), tr_(ffn_w_up)
    srcs = {"w_in": bf(tr_(ab_w_in[0])), "w_oab": bf(ab_w_out[0]), "gu": ab_gla_gate_up[0],
            "w_qkv": bf(c_w_qkv[0]), "w_oc": bf(c_w_out[0])}
    for l in range(depth):
        srcs[f"wg{l}"] = _pad_to(bf(wg_t[l]), 0, fp)
        srcs[f"wu{l}"] = _pad_to(bf(wu_t[l]), 0, fp)
        srcs[f"wd{l}"] = _pad_to(bf(ffn_w_down[l]), 0, fp)
        srcs[f"wpg{l}"] = bf(ple_w_gate[l])
        srcs[f"wpp{l}"] = bf(ple_w_proj[l])
    group_keys = [["w_in", "w_oab", "gu"], ["wg0", "wu0"], ["wd0", "wpg0", "wpp0"], ["w_qkv", "w_oc"],
                  ["wg1", "wu1"], ["wd1", "wpg1", "wpp1"]]

    def landing(a):
        return lax.dynamic_update_slice(lax.empty((N_DEV,) + a.shape, a.dtype), a[None], (me,) + (0,) * a.ndim)

    _, chip_handles, gather_token = exchange_call(
        "gather_start", [], [("to_chips", [(srcs[k], landing(srcs[k])) for k in keys]) for keys in group_keys])
    weights = {}

    def gather_wait(gi, dep):
        lands = [(land,) for _, land in chip_handles[gi][0]]
        _, (passing,), _ = exchange_call(
            f"gather{gi}_pass", [("to_chips", chip_handles[gi])], [("pass_on", lands)], deps=(dep,))
        (complete,), _, _ = exchange_call(f"gather{gi}_done", [("pass_on", passing)], [])
        weights.update(zip(group_keys[gi], [land for (land,) in complete]))

    gb = ab_gla_gate_b
    hn_w = jnp.concatenate([ab_ret_norm_w, ab_gla_norm_w], axis=1)
    cosf, sins = _rope_tables(positions[0])
    p_bf = bf(p[:, 0])

    xs = x[0]
    saved = []
    for i in range(depth):
        nm = f"l{i}_"
        w_attn, w_ffn, w_ple = attn_norm_w[i:i + 1], ffn_norm_w[i:i + 1], ple_norm_w[i:i + 1]
        (xn,) = rowwise(nm + "norm_attn", lambda a, w: (_rms(a, w),), T, [("row", xs), ("full", w_attn)],
                        [("row", D, BF16)], deps=(gather_token,) if i == 0 else ())
        gather_wait(3 * i, xn)
        if i % 2 == 0:
            w_in_t = weights["w_in"].reshape(1, 1, in_width, D)
            w_lr_t = _pad_to(w_in_t[0, 0, OFF_LR:], 0, LANE).reshape(1, 1, LANE, D)
            w_oab = weights["w_oab"].reshape(1, 1, D, D)
            gu_full = _pad_to(weights["gu"].transpose(1, 0, 2).reshape(GLA_GATE_RANK, GLA_QK), 0, LANE)
            z = mmt_fwd(nm + "mm_in", xn, w_in_t, 0, F32, n=OFF_LR)
            glr = mmt_fwd(nm + "mm_lr", xn, w_lr_t, 0, F32)
            oraw = retention_fwd(nm + "ret_fwd", z, cosf, sins, RET_V + GLA_V)
            oraw = gla_fwd(nm + "gla_fwd", z, glr, gu_full, gb, oraw)
            o = headnorm_fwd(nm + "headnorm_fwd", oraw, z, hn_w)
            mix = mm_nn(nm + "mm_out", o, w_oab, 0, F32)
            mixer_saved = (z, glr, oraw, o)
        else:
            w_qkv = weights["w_qkv"].reshape((1,) + weights["w_qkv"].shape)
            w_oc = weights["w_oc"].reshape(1, 1, D, D)
            qkv = mm_nn(nm + "mm_qkv", xn, w_qkv, 0, BF16)
            o, lse = attn_fwd(nm + "attn_fwd", qkv)
            mix = mm_nn(nm + "mm_out", o, w_oc, 0, F32)
            mixer_saved = (qkv, o, lse)
        h1, hn = rowwise(nm + "add_norm_ffn", lambda a, b, w: (a + b, _rms(a + b, w)), T,
                         [("row", xs), ("row", mix), ("full", w_ffn)], [("row", D, F32), ("row", D, BF16)])
        gather_wait(3 * i + 1, hn)
        wg = weights[f"wg{i}"].reshape(1, N_DEV, fp, D)
        wu = weights[f"wu{i}"].reshape(1, N_DEV, fp, D)
        g, u, act = ffn_gate_up(nm + "ffn_gate_up", hn, wg, wu)
        gather_wait(3 * i + 2, act)
        wd = weights[f"wd{i}"].reshape(1, 1, N_DEV * fp, D)
        wpg = weights[f"wpg{i}"].reshape(1, 1, D, D)
        wpp = weights[f"wpp{i}"].reshape((1,) + weights[f"wpp{i}"].shape)
        f = mm_nn(nm + "mm_down", act, wd, 0, F32)
        h2, pn = rowwise(nm + "add_norm_ple", lambda a, b, w: (a + b, _rms(a + b, w)), T,
                         [("row", h1), ("row", f), ("full", w_ple)], [("row", D, F32), ("row", D, BF16)])
        s = mm_nn(nm + "mm_ple_gate", pn, wpg, 0, F32)
        e = mm_nn(nm + "mm_ple_proj", p_bf[i], wpp, 0, F32)
        (x_next,) = rowwise(nm + "ple_out", lambda a, b, c: (a + _sigmoid(b) * c,), T,
                            [("row", h2), ("row", s), ("row", e)], [("row", D, F32)])
        mixer_w = (w_in_t, w_lr_t, w_oab, gu_full) if i % 2 == 0 else (w_qkv, w_oc)
        saved.append((xs, xn, mixer_saved, mixer_w, (wg, wu, wd, wpg), h1, hn, g, u, act, h2, pn, s, e))
        xs = x_next

    def loss_fn(a, w, t):
        diff = _rms(a, w) - t
        dx, dw = _rms_bwd(a, w, diff * (1.0 / D))
        part = 0.5 * jnp.sum(jnp.mean(diff * diff, axis=-1, keepdims=True), axis=0, keepdims=True)
        return dx, dw, jnp.broadcast_to(part, (1, LANE))

    dx, d_final_w, loss_part = rowwise("loss_head", loss_fn, T,
                                       [("row", xs), ("full", final_norm_w[None, :]), ("row", loss_target[0])],
                                       [("row", D, F32), ("acc", D), ("acc", LANE)])
    loss = lax.psum(loss_part[0, 0], ("x", "y", "c"))

    grads = {}
    on_chip = []
    scatters = []

    def scatter_start(name, keys, deps=()):
        waits = [("halves", on_chip[0][1])] if on_chip else []
        starts = [("halves", [(grads[k], lax.empty((N_CHIP,) + grads[k].shape[1:], BF16)) for k in keys])] if keys else []
        waited, handles, token = exchange_call(name, waits, starts, deps=deps)
        if on_chip:
            done_keys, _ = on_chip.pop()
            sums = [chip_sum(f"{name}_sum{j}", a, half) for j, (a, half) in enumerate(waited[0])]
            _, (handle,), token = exchange_call(
                name + "_chips", [], [("chip_sums", [(cs, lax.empty(cs.shape, BF16)) for cs in sums])])
            scatters.append((done_keys, handle))
        if keys:
            on_chip.append((keys, handles[0]))
        return token

    d_attn_w, d_ffn_w, d_ple_w = [None] * depth, [None] * depth, [None] * depth
    for i in reversed(range(depth)):
        nm = f"l{i}_b_"
        xs_i, xn, mixer_saved, mixer_w, (wg, wu, wd, wpg), h1, hn, g, u, act, h2, pn, s, e = saved[i]
        w_attn, w_ffn, w_ple = attn_norm_w[i:i + 1], ffn_norm_w[i:i + 1], ple_norm_w[i:i + 1]

        def ple_bwd(d, sv, ev):
            gate = _sigmoid(sv)
            return d * gate, d * ev * gate * (1.0 - gate)

        de, ds = rowwise(nm + "ple_out", ple_bwd, T, [("row", dx), ("row", s), ("row", e)],
                         [("row", D, BF16), ("row", D, BF16)], deps=(loss.reshape(1, 1),) if i == depth - 1 else ())
        grads[("ple_w_proj", i)] = mm_tn(nm + "mm_ple_proj_w", p_bf[i], de, N_DEV, BF16)
        grads[("ple_w_gate", i)] = mm_tn(nm + "mm_ple_gate_w", pn, ds, 1, BF16).reshape(N_DEV, D // N_DEV, D)
        dpn = mm_nt(nm + "mm_ple_gate_x", ds, wpg, 0, F32)

        def norm_bwd_add(a, w, dn, dres):
            dxx, dw = _rms_bwd(a, w, dn)
            tot = dres + dxx
            return tot, tot, dw

        dh2, dh2_bf, d_ple_w[i] = rowwise(nm + "norm_ple", norm_bwd_add, T,
                                          [("row", h2), ("full", w_ple), ("row", dpn), ("row", dx)],
                                          [("row", D, F32), ("row", D, BF16), ("acc", D)])
        grads[("ffn_w_down", i)] = mm_tn(nm + "mm_down_w", act, dh2_bf, 1, BF16).reshape(N_DEV, fp, D)
        token = scatter_start(nm + "scatter_ple_down", [("ple_w_proj", i), ("ple_w_gate", i), ("ffn_w_down", i)])
        dg, du = ffn_down_bwd(nm + "ffn_down_x", dh2_bf, wd, g, u, deps=(token,))
        grads[("ffn_w_gate", i)] = mmt_dw(nm + "mm_gate_w", dg, hn, N_DEV, BF16)
        grads[("ffn_w_up", i)] = mmt_dw(nm + "mm_up_w", du, hn, N_DEV, BF16)
        token = scatter_start(nm + "scatter_gate_up", [("ffn_w_gate", i), ("ffn_w_up", i)])
        dhn_g = mmt_dx(nm + "mm_gate_x", dg, wg, 0, F32, deps=(token,))
        dhn_u = mmt_dx(nm + "mm_up_x", du, wu, 0, F32)

        def norm_bwd_add2(a, w, dn1, dn2, dres):
            dxx, dw = _rms_bwd(a, w, dn1 + dn2)
            tot = dres + dxx
            return tot, tot, dw

        dh1, dh1_bf, d_ffn_w[i] = rowwise(nm + "norm_ffn", norm_bwd_add2, T,
                                          [("row", h1), ("full", w_ffn), ("row", dhn_g), ("row", dhn_u), ("row", dh2)],
                                          [("row", D, F32), ("row", D, BF16), ("acc", D)])
        if i % 2 == 0:
            z, glr, oraw, o = mixer_saved
            w_in_t, w_lr_t, w_oab, gu_full = mixer_w
            grads[("ab_w_out", 0)] = mm_tn(nm + "mm_out_w", o, dh1_bf, 1, BF16).reshape(N_DEV, D // N_DEV, D)
            token = scatter_start(nm + "scatter_out", [("ab_w_out", 0)])
            do = mm_nt(nm + "mm_out_x", dh1_bf, w_oab, 0, F32, deps=(token,))
            d_oraw, d_gates, d_hn_w = headnorm_bwd(nm + "headnorm", oraw, z, hn_w, do)
            d_rq, d_rk, d_rv = retention_bwd(nm + "ret", z, cosf, sins, d_oraw)
            d_gq, d_gk, d_gv, d_glr4, d_gu, d_gb = gla_bwd(nm + "gla", z, glr, gu_full, gb, d_oraw)
            dz = jnp.concatenate([d_rq, d_rk, d_rv, d_gates[:, :RET_V], d_gq, d_gk, d_gv, d_gates[:, RET_V:]], axis=1)
            (d_glr,) = rowwise(nm + "sum_lr", lambda *a: (a[0] + a[1] + a[2] + a[3],), T,
                               [("row", d_glr4[hh]) for hh in range(GLA_HEADS)], [("row", LANE, BF16)])
            dw_main = mm_tn(nm + "mm_in_w", xn, dz, 1, BF16)[0]
            dw_lr = mm_tn(nm + "mm_lr_w", xn, d_glr, 1, BF16)[0]
            dw_in = jnp.concatenate([dw_main, dw_lr[:, :GLA_GATE_RANK]], axis=1)
            grads[("ab_w_in", 0)] = dw_in.reshape(D, N_DEV, in_shard).transpose(1, 0, 2)
            token = scatter_start(nm + "scatter_in", [("ab_w_in", 0)])
            dxn_a = mmt_dx(nm + "mm_in_x", dz, w_in_t, 0, F32, n=OFF_LR, deps=(token,))
            dxn_b = mmt_dx(nm + "mm_lr_x", d_glr, w_lr_t, 0, F32)
        else:
            qkv, o, lse = mixer_saved
            w_qkv, w_oc = mixer_w
            grads[("c_w_out", 0)] = mm_tn(nm + "mm_out_w", o, dh1_bf, 1, BF16).reshape(N_DEV, D // N_DEV, D)
            do = mm_nt(nm + "mm_out_x", dh1_bf, w_oc, 0, BF16)
            dq, dk, dv = attn_bwd(nm + "attn", qkv, o, lse, do)
            dqkv = jnp.concatenate([dq, dk, dv], axis=1)
            grads[("c_w_qkv", 0)] = mm_tn(nm + "mm_qkv_w", xn, dqkv, N_DEV, BF16)
            token = scatter_start(nm + "scatter_attn", [("c_w_out", 0), ("c_w_qkv", 0)])
            dxn_a = mm_nt(nm + "mm_qkv_x", dqkv, w_qkv, 0, F32, deps=(token,))
            dxn_b = None
        if dxn_b is None:
            dx, _, d_attn_w[i] = rowwise(nm + "norm_attn", norm_bwd_add, T,
                                         [("row", xs_i), ("full", w_attn), ("row", dxn_a), ("row", dh1)],
                                         [("row", D, F32), ("row", D, BF16), ("acc", D)])
        else:
            dx, _, d_attn_w[i] = rowwise(nm + "norm_attn", norm_bwd_add2, T,
                                         [("row", xs_i), ("full", w_attn), ("row", dxn_a), ("row", dxn_b), ("row", dh1)],
                                         [("row", D, F32), ("row", D, BF16), ("acc", D)])

    small_names = ["attn_norm_w", "ffn_norm_w", "ple_norm_w", "final_norm_w", "ab_gla_gate_b", "ab_ret_norm_w",
                   "ab_gla_norm_w"]
    small_grads = [jnp.concatenate(d_attn_w, 0), jnp.concatenate(d_ffn_w, 0), jnp.concatenate(d_ple_w, 0), d_final_w[0],
                   d_gb, d_hn_w[:, :RET_V], d_hn_w[:, RET_V:]]
    small_w = [attn_norm_w, ffn_norm_w, ple_norm_w, final_norm_w, ab_gla_gate_b, ab_ret_norm_w, ab_gla_norm_w]
    small_m = [m_attn_norm_w, m_ffn_norm_w, m_ple_norm_w, m_final_norm_w, m_ab_gla_gate_b, m_ab_ret_norm_w, m_ab_gla_norm_w]
    small_v = [v_attn_norm_w, v_ffn_norm_w, v_ple_norm_w, v_final_norm_w, v_ab_gla_gate_b, v_ab_ret_norm_w, v_ab_gla_norm_w]
    sizes = [int(np.prod(a.shape)) for a in small_w]
    n_gu = GLA_GATE_RANK * GLA_QK
    n_small = _round_up(sum(sizes) + n_gu, LANE)
    pack = lambda parts: _pad_to(jnp.concatenate([a.reshape(-1) for a in parts]), 0, n_small)[None, :]
    small_part = pack(small_grads + [d_gu[:GLA_GATE_RANK]])

    big_w = dict(ab_w_in=(ab_w_in, m_ab_w_in, v_ab_w_in), ab_w_out=(ab_w_out, m_ab_w_out, v_ab_w_out),
                 c_w_qkv=(c_w_qkv, m_c_w_qkv, v_c_w_qkv), c_w_out=(c_w_out, m_c_w_out, v_c_w_out),
                 ffn_w_gate=(wg_t, tr_(m_ffn_w_gate), tr_(v_ffn_w_gate)),
                 ffn_w_up=(wu_t, tr_(m_ffn_w_up), tr_(v_ffn_w_up)),
                 ffn_w_down=(ffn_w_down, m_ffn_w_down, v_ffn_w_down), ple_w_proj=(ple_w_proj, m_ple_w_proj, v_ple_w_proj),
                 ple_w_gate=(ple_w_gate, m_ple_w_gate, v_ple_w_gate))
    scatter_start("scatter_last", [], deps=(dx,))
    results, last = {}, dx
    for gi, (keys, handle) in enumerate(scatters):
        (arrived,), _, _ = exchange_call(f"scatter_wait{gi}", [("chip_sums", handle)], [], deps=(last,))
        for (n, l), (own, land) in zip(keys, arrived):
            results[n] = adamw(f"adamw_{n}{l}", *big_w[n], l, land, own, prev=results.get(n))
            last = results[n][0]
    for n in ("ffn_w_gate", "ffn_w_up"):
        results[n] = [tr_(a) for a in results[n]]
    small_parts = gather_small("gather_small", small_part, deps=(last,)).reshape(N_DEV, n_small)

    gu_off = sum(sizes)
    own_cols = lambda a: lax.dynamic_slice_in_dim(a.reshape(GLA_GATE_RANK, GLA_QK), me * gu_cols, gu_cols, axis=1)
    small_res = adamw_small("adamw_small", pack(small_w + [jnp.zeros((n_gu,), F32)]),
                            pack(small_m + [jnp.zeros((n_gu,), F32)]), pack(small_v + [jnp.ones((n_gu,), F32)]),
                            small_parts)
    g_gu_full = small_res[0][0, gu_off:gu_off + n_gu]
    g_gu = own_cols(g_gu_full)[None]
    gu_res = adamw_small("adamw_gate_up", *[_pad_to(a.reshape(1, -1), 1, _round_up(a.size, LANE)) for a in
                                            (ab_gla_gate_up, m_ab_gla_gate_up, v_ab_gla_gate_up)],
                         jnp.concatenate([_pad_to(g_gu.reshape(1, -1), 1, _round_up(g_gu.size, LANE)),
                                          jnp.zeros((N_DEV - 1, _round_up(g_gu.size, LANE)), F32)], axis=0))
    for k in range(4):
        off = 0
        for n, a, sz in zip(small_names, small_w, sizes):
            results.setdefault(n, [None] * 4)[k] = small_res[k][0, off:off + sz].reshape(a.shape)
            off += sz
        results.setdefault("ab_gla_gate_up", [None] * 4)[k] = gu_res[k][0, :g_gu.size].reshape(ab_gla_gate_up.shape)

    order = ["attn_norm_w", "ffn_norm_w", "ple_norm_w", "final_norm_w", "ab_w_in", "ab_gla_gate_up", "ab_gla_gate_b",
             "ab_ret_norm_w", "ab_gla_norm_w", "ab_w_out", "c_w_qkv", "c_w_out", "ffn_w_gate", "ffn_w_up", "ffn_w_down",
             "ple_w_proj", "ple_w_gate"]
    return (loss, dx[None], *[results[n][0] for n in order], *[results[n][1] for n in order],
            *[results[n][2] for n in order], *[results[n][3] for n in order])
```

```python
import math

import numpy as np
import jax
import jax.numpy as jnp
from jax import lax
from jax.experimental import pallas as pl
from jax.experimental.pallas import tpu as pltpu

F32 = jnp.float32
BF16 = jnp.bfloat16
HIGHEST = lax.Precision.HIGHEST

N_DEV = 8
VMEM_LIMIT_BYTES = 48 * 1024 * 1024
LANE = 128
NORM_EPS = 1e-6

RET_HEADS, RET_DK, RET_DV = 4, 256, 256
RET_THETA_BASE = 10000.0
GLA_HEADS, GLA_DK, GLA_DV = 4, 128, 256
GLA_GATE_RANK = 16
GLA_GATE_NORM = 16.0
CHUNK = 64
ATT_HEADS = 16
DILATED_BRANCHES = ((128, 1), (512, 4), (2048, 16))
BLK = 256

ADAM_LR, ADAM_B1, ADAM_B2, ADAM_EPS, ADAM_WD, ADAM_STEP = 0.001, 0.9, 0.999, 1e-08, 0.01, 10

RET_QK = RET_HEADS * RET_DK
RET_V = RET_HEADS * RET_DV
GLA_QK = GLA_HEADS * GLA_DK
GLA_V = GLA_HEADS * GLA_DV
OFF_RQ, OFF_RK, OFF_RV, OFF_RG = 0, RET_QK, 2 * RET_QK, 2 * RET_QK + RET_V
OFF_GQ = OFF_RG + RET_V
OFF_GK = OFF_GQ + GLA_QK
OFF_GV = OFF_GK + GLA_QK
OFF_GG = OFF_GV + GLA_V
OFF_LR = OFF_GG + GLA_V


def _params(*sem):
    return pltpu.CompilerParams(dimension_semantics=sem or None, vmem_limit_bytes=VMEM_LIMIT_BYTES)


def _pick(n, cands):
    for c in cands:
        if n % c == 0:
            return c
    raise ValueError(f"no tile for {n} in {cands}")


_NN = (((1,), (0,)), ((), ()))
_NT = (((1,), (1,)), ((), ()))
_TN = (((0,), (0,)), ((), ()))
_ANY = pl.BlockSpec(memory_space=pl.ANY)
MAX_CONTRACT = 2048
_TILES = (1024, 768, 512, 256, 128)


def _mm_call(name, dims, grid, in_specs, out_spec, out_shape, args, deps=()):
    steps = grid[2]
    assert steps == 1 or out_shape.dtype == F32

    def body(a_ref, b_ref, *rest):
        o_ref = rest[len(deps)]
        part = lax.dot_general(a_ref[...].astype(BF16), b_ref[...].astype(BF16), dims, preferred_element_type=F32)
        if steps == 1:
            o_ref[...] = part.astype(o_ref.dtype)
        else:
            _accumulate(o_ref, part, pl.program_id(2) == 0)

    return pl.pallas_call(
        body, name=name, grid=grid, in_specs=list(in_specs) + [_ANY] * len(deps), out_specs=out_spec,
        out_shape=out_shape, compiler_params=_params("parallel", "parallel", "arbitrary"))(*args, *deps)


def mm_nn(name, a, w, l, out_dtype, deps=()):
    _, J, K, n = w.shape
    M = a.shape[0]
    tm, tn, tk = _pick(M, _TILES), _pick(n, _TILES), _pick(K, (MAX_CONTRACT,) + _TILES)
    nt = n // tn
    return _mm_call(
        name, _NN, (M // tm, J * nt, K // tk),
        [pl.BlockSpec((tm, tk), lambda i, j, k: (i, k)),
         pl.BlockSpec((None, None, tk, tn), lambda i, j, k: (l, j // nt, k, j % nt))],
        pl.BlockSpec((tm, tn), lambda i, j, k: (i, j)),
        jax.ShapeDtypeStruct((M, J * n), out_dtype), (a, w), deps)


def mm_nt(name, a, w, l, out_dtype, deps=()):
    _, J, K, n = w.shape
    M = a.shape[0]
    tm, tq, tc = _pick(M, _TILES), _pick(K, _TILES), _pick(n, (MAX_CONTRACT,) + _TILES)
    nc = n // tc
    return _mm_call(
        name, _NT, (M // tm, K // tq, J * nc),
        [pl.BlockSpec((tm, tc), lambda i, q, c: (i, c)),
         pl.BlockSpec((None, None, tq, tc), lambda i, q, c: (l, c // nc, q, c % nc))],
        pl.BlockSpec((tm, tq), lambda i, q, c: (i, q)),
        jax.ShapeDtypeStruct((M, K), out_dtype), (a, w), deps)


def mm_tn(name, x, dy, J, out_dtype, deps=()):
    M, K = x.shape
    n = dy.shape[1] // J
    tp, tn = _pick(K, _TILES), _pick(n, _TILES)
    nt = n // tn
    assert M <= MAX_CONTRACT
    return _mm_call(
        name, _TN, (K // tp, J * nt, 1),
        [pl.BlockSpec((M, tp), lambda i, j, r: (0, i)),
         pl.BlockSpec((M, tn), lambda i, j, r: (0, j))],
        pl.BlockSpec((None, tp, tn), lambda i, j, r: (j // nt, i, j % nt)),
        jax.ShapeDtypeStruct((J, K, n), out_dtype), (x, dy), deps)


def mmt_fwd(name, a, wt, l, out_dtype, n=None, deps=()):
    _, J, rows, K = wt.shape
    n = rows if n is None else n
    M = a.shape[0]
    tm, tn = _pick(M, _TILES), _pick(n, _TILES)
    nt = n // tn
    assert K <= MAX_CONTRACT
    return _mm_call(
        name, _NT, (M // tm, J * nt, 1),
        [pl.BlockSpec((tm, K), lambda i, j, k: (i, 0)),
         pl.BlockSpec((None, None, tn, K), lambda i, j, k: (l, j // nt, j % nt, 0))],
        pl.BlockSpec((tm, tn), lambda i, j, k: (i, j)),
        jax.ShapeDtypeStruct((M, J * n), out_dtype), (a, wt), deps)


def mmt_dx(name, dy, wt, l, out_dtype, n=None, deps=()):
    _, J, rows, K = wt.shape
    n = rows if n is None else n
    M = dy.shape[0]
    tm, tq, tc = _pick(M, _TILES), _pick(K, _TILES), _pick(n, _TILES)
    nc = n // tc
    return _mm_call(
        name, _NN, (M // tm, K // tq, J * nc),
        [pl.BlockSpec((tm, tc), lambda i, q, c: (i, c)),
         pl.BlockSpec((None, None, tc, tq), lambda i, q, c: (l, c // nc, c % nc, q))],
        pl.BlockSpec((tm, tq), lambda i, q, c: (i, q)),
        jax.ShapeDtypeStruct((M, K), out_dtype), (dy, wt), deps)


WIDE_TILE = 512


def _wide_call(name, body, M, K, a, w, a_spec, w_spec, out_dtype, deps):
    def kernel_body(a_ref, w_ref, *rest):
        o_ref = rest[len(deps)]
        o_ref[...] = body(a_ref, w_ref).astype(o_ref.dtype)

    return pl.pallas_call(
        kernel_body, name=name, grid=(M // WIDE_TILE, K // WIDE_TILE),
        in_specs=[a_spec, w_spec] + [_ANY] * len(deps),
        out_specs=pl.BlockSpec((WIDE_TILE, WIDE_TILE), lambda i, q: (i, q)),
        out_shape=jax.ShapeDtypeStruct((M, K), out_dtype),
        compiler_params=_params("parallel", "parallel"))(a, w, *deps)


def mmt_dx_wide(name, dy, wt, out_dtype, n=None, deps=()):
    _, J, rows, K = wt.shape
    n = rows if n is None else n
    M = dy.shape[0]

    def body(dy_ref, w_ref):
        return jnp.dot(dy_ref[...].astype(BF16), w_ref[...].reshape(J * n, WIDE_TILE), preferred_element_type=F32)

    return _wide_call(name, body, M, K, dy, wt,
                      pl.BlockSpec((WIDE_TILE, J * n), lambda i, q: (i, 0)),
                      pl.BlockSpec((None, J, n, WIDE_TILE), lambda i, q: (0, 0, 0, q)), out_dtype, deps)


def mm_nt_wide(name, a, w, out_dtype, deps=()):
    _, J, K, n = w.shape
    M = a.shape[0]

    def body(a_ref, w_ref):
        acc = None
        for j in range(J):
            part = lax.dot_general(a_ref[:, j * n:(j + 1) * n].astype(BF16), w_ref[j], _NT, preferred_element_type=F32)
            acc = part if acc is None else acc + part
        return acc

    return _wide_call(name, body, M, K, a, w,
                      pl.BlockSpec((WIDE_TILE, J * n), lambda i, q: (i, 0)),
                      pl.BlockSpec((None, J, WIDE_TILE, n), lambda i, q: (0, 0, q, 0)), out_dtype, deps)


def mmt_dw(name, dy, x, J, out_dtype, deps=()):
    M, K = x.shape
    n = dy.shape[1] // J
    tn, tp = _pick(n, _TILES), _pick(K, _TILES)
    nt = n // tn
    assert M <= MAX_CONTRACT
    return _mm_call(
        name, _TN, (J * nt, K // tp, 1),
        [pl.BlockSpec((M, tn), lambda j, i, r: (0, j)),
         pl.BlockSpec((M, tp), lambda j, i, r: (0, i))],
        pl.BlockSpec((None, tn, tp), lambda j, i, r: (j // nt, j % nt, i)),
        jax.ShapeDtypeStruct((J, n, K), out_dtype), (dy, x), deps)


def ffn_gate_up(name, a, wg, wu):
    _, J, n, K = wg.shape
    M = a.shape[0]
    tm, tn = _pick(M, _TILES), _pick(n, _TILES)
    nt = n // tn
    assert K <= MAX_CONTRACT

    def body(a_ref, wg_ref, wu_ref, g_ref, u_ref, act_ref):
        x = a_ref[...]
        g = lax.dot_general(x, wg_ref[...], _NT, preferred_element_type=F32)
        u = lax.dot_general(x, wu_ref[...], _NT, preferred_element_type=F32)
        g_ref[...] = g.astype(g_ref.dtype)
        u_ref[...] = u.astype(u_ref.dtype)
        act_ref[...] = (_silu_and_grad(g)[0] * u).astype(act_ref.dtype)

    w_spec = pl.BlockSpec((None, None, tn, K), lambda i, j: (0, j // nt, j % nt, 0))
    out = pl.BlockSpec((tm, tn), lambda i, j: (i, j))
    return pl.pallas_call(
        body, name=name, grid=(M // tm, J * nt),
        in_specs=[pl.BlockSpec((tm, K), lambda i, j: (i, 0)), w_spec, w_spec],
        out_specs=[out] * 3, out_shape=[jax.ShapeDtypeStruct((M, J * n), BF16)] * 3,
        compiler_params=_params("parallel", "parallel"))(a, wg, wu)


def ffn_down_bwd(name, dy, wd, g, u, deps=()):
    _, _, K, n = wd.shape
    M = dy.shape[0]
    tm, tq = _pick(M, _TILES), _pick(K, _TILES)
    assert n <= MAX_CONTRACT

    def body(dy_ref, w_ref, g_ref, u_ref, *rest):
        dg_ref, du_ref = rest[len(deps):]
        dact = lax.dot_general(dy_ref[...], w_ref[...], _NT, preferred_element_type=F32)
        silu, dsilu = _silu_and_grad(g_ref[...].astype(F32))
        dg_ref[...] = (dact * u_ref[...].astype(F32) * dsilu).astype(dg_ref.dtype)
        du_ref[...] = (dact * silu).astype(du_ref.dtype)

    blk = pl.BlockSpec((tm, tq), lambda i, q: (i, q))
    return pl.pallas_call(
        body, name=name, grid=(M // tm, K // tq),
        in_specs=[pl.BlockSpec((tm, n), lambda i, q: (i, 0)),
                  pl.BlockSpec((None, None, tq, n), lambda i, q: (0, 0, q, 0)), blk, blk] + [_ANY] * len(deps),
        out_specs=[blk, blk], out_shape=[jax.ShapeDtypeStruct((M, K), BF16)] * 2,
        compiler_params=_params("parallel", "parallel"))(dy, wd, g, u, *deps)


def rowwise(name, fn, rows, ins, outs, tr=256, deps=()):
    widest = max([s[1].shape[1] if s[0] != "col" else s[3] for s in ins] + [s[1] for s in outs])
    tr = min(tr if widest <= 2048 else tr // 2, rows)
    in_specs, args = [], []
    for spec in ins:
        kind, a = spec[0], spec[1]
        if kind == "row":
            in_specs.append(pl.BlockSpec((tr, a.shape[1]), lambda i: (i, 0)))
        elif kind == "col":
            cb, width = spec[2], spec[3]
            in_specs.append(pl.BlockSpec((tr, width), lambda i, cb=cb: (i, cb)))
        else:
            in_specs.append(pl.BlockSpec(a.shape, lambda i: (0, 0)))
        args.append(a)
    out_specs, out_shapes = [], []
    for spec in outs:
        if spec[0] == "row":
            out_specs.append(pl.BlockSpec((tr, spec[1]), lambda i: (i, 0)))
            out_shapes.append(jax.ShapeDtypeStruct((rows, spec[1]), spec[2]))
        else:
            out_specs.append(pl.BlockSpec((1, spec[1]), lambda i: (0, 0)))
            out_shapes.append(jax.ShapeDtypeStruct((1, spec[1]), F32))
    n_in = len(ins)

    def body(*refs):
        vals = fn(*[r[...] for r in refs[:n_in]])
        first = pl.program_id(0) == 0
        for r, v, spec in zip(refs[n_in + len(deps):], vals, outs):
            if spec[0] == "row":
                r[...] = v.astype(r.dtype)
            else:
                _accumulate(r, v, first)

    return pl.pallas_call(body, name=name, grid=(rows // tr,), in_specs=in_specs + [_ANY] * len(deps),
                          out_specs=out_specs, out_shape=out_shapes,
                          compiler_params=_params("arbitrary"))(*args, *deps)


def _accumulate(ref, v, first):
    @pl.when(first)
    def _():
        ref[...] = v

    @pl.when(jnp.logical_not(first))
    def _():
        ref[...] += v


def _rms(x, w):
    r = lax.rsqrt(jnp.mean(x * x, axis=-1, keepdims=True) + NORM_EPS)
    return x * r * w


def _rms_bwd(x, w, dy):
    r = lax.rsqrt(jnp.mean(x * x, axis=-1, keepdims=True) + NORM_EPS)
    g = dy * w
    dx = r * (g - x * (r * r) * jnp.mean(g * x, axis=-1, keepdims=True))
    dw = jnp.sum(dy * x * r, axis=0, keepdims=True)
    return dx, dw


def _sigmoid(x):
    return 1.0 / (1.0 + jnp.exp(-x))


def _silu_and_grad(g):
    s = _sigmoid(g)
    return g * s, s * (1.0 + g * (1.0 - s))


def _swap_pairs(x):
    n = x.shape[-1]
    lane = lax.broadcasted_iota(jnp.int32, x.shape, x.ndim - 1)
    return jnp.where((lane & 1) == 0, pltpu.roll(x, n - 1, x.ndim - 1), pltpu.roll(x, 1, x.ndim - 1))


def _rot(x, cosf, sins):
    return x * cosf + _swap_pairs(x) * sins


def _unrot(d, cosf, sins):
    return d * cosf + _swap_pairs(d * sins)


def _ret_log_gamma(h):
    vals = [math.log1p(-2.0 ** (-5.0 - i)) for i in range(RET_HEADS)]
    out = jnp.float32(vals[RET_HEADS - 1])
    for i in range(RET_HEADS - 2, -1, -1):
        out = jnp.where(h == i, jnp.float32(vals[i]), out)
    return out


def _fill_decays(dec_ref, lg):
    ri = lax.broadcasted_iota(jnp.int32, (BLK, BLK), 0)
    ci = lax.broadcasted_iota(jnp.int32, (BLK, BLK), 1)
    for d in range(dec_ref.shape[0]):
        dt = d * BLK + ri - ci
        dec_ref[d] = jnp.where(dt >= 0, jnp.exp(jnp.maximum(dt, 0).astype(F32) * lg), 0.0)


def _decay_row(dec_ref, qi):
    return jnp.concatenate([dec_ref[qi - kb] for kb in range(qi + 1)], axis=1)


def _once(block_shape, index_map):
    return pl.BlockSpec(block_shape, index_map, pipeline_mode=pl.Buffered(1))


def _dot(a, b):
    return jnp.dot(a.astype(BF16), b.astype(BF16), preferred_element_type=F32)


def _dot_nt(a, b):
    return lax.dot_general(a.astype(BF16), b.astype(BF16), _NT, preferred_element_type=F32)


def _dot_tn(a, b):
    return lax.dot_general(a.astype(BF16), b.astype(BF16), _TN, preferred_element_type=F32)


def retention_fwd(name, z, cosf, sins, width_out):
    T = z.shape[0]
    nq = T // BLK
    scale = RET_DK ** -0.5

    def body(q_ref, k_ref, v_ref, cos_ref, sin_ref, o_ref, krot, vb, dec_ref):
        _fill_decays(dec_ref, _ret_log_gamma(pl.program_id(0)))
        krot[...] = (_rot(k_ref[...], cos_ref[...], sin_ref[...]) * scale).astype(BF16)
        vb[...] = v_ref[...].astype(BF16)
        for qi in range(nq):
            rows, n = slice(qi * BLK, (qi + 1) * BLK), (qi + 1) * BLK
            q = _rot(q_ref[rows, :], cos_ref[rows, :], sin_ref[rows, :])
            s = _dot_nt(q, krot[0:n, :]) * _decay_row(dec_ref, qi)
            o_ref[rows, :] = _dot(s, vb[0:n, :])

    return pl.pallas_call(
        body, name=name, grid=(RET_HEADS,),
        in_specs=[pl.BlockSpec((T, RET_DK), lambda h: (0, OFF_RQ // RET_DK + h)),
                  pl.BlockSpec((T, RET_DK), lambda h: (0, OFF_RK // RET_DK + h)),
                  pl.BlockSpec((T, RET_DV), lambda h: (0, OFF_RV // RET_DV + h)),
                  _once((T, RET_DK), lambda h: (0, 0)), _once((T, RET_DK), lambda h: (0, 0))],
        out_specs=pl.BlockSpec((T, RET_DV), lambda h: (0, h)),
        out_shape=jax.ShapeDtypeStruct((T, width_out), F32),
        scratch_shapes=[pltpu.VMEM((T, RET_DK), BF16), pltpu.VMEM((T, RET_DV), BF16),
                        pltpu.VMEM((nq, BLK, BLK), F32)],
        compiler_params=_params("arbitrary"))(z, z, z, cosf, sins)


def retention_bwd(name, z, cosf, sins, do):
    T = z.shape[0]
    nq = T // BLK
    scale = RET_DK ** -0.5

    def body(q_ref, k_ref, v_ref, cos_ref, sin_ref, do_ref, dq_ref, dk_ref, dv_ref, krot, vb, dk_acc, dv_acc, dec_ref):
        _fill_decays(dec_ref, _ret_log_gamma(pl.program_id(0)))
        krot[...] = (_rot(k_ref[...], cos_ref[...], sin_ref[...]) * scale).astype(BF16)
        vb[...] = v_ref[...].astype(BF16)
        dk_acc[...] = jnp.zeros_like(dk_acc)
        dv_acc[...] = jnp.zeros_like(dv_acc)
        for qi in range(nq):
            rows, n = slice(qi * BLK, (qi + 1) * BLK), (qi + 1) * BLK
            cos_q, sin_q = cos_ref[rows, :], sin_ref[rows, :]
            q = _rot(q_ref[rows, :], cos_q, sin_q).astype(BF16)
            dout = do_ref[rows, :].astype(BF16)
            kk, vv, dec = krot[0:n, :], vb[0:n, :], _decay_row(dec_ref, qi)
            p = (_dot_nt(q, kk) * dec).astype(BF16)
            ds = (_dot_nt(dout, vv) * dec).astype(BF16)
            dq_ref[rows, :] = _unrot(_dot(ds, kk), cos_q, sin_q).astype(dq_ref.dtype)
            dk_acc[0:n, :] += _dot_tn(ds, q)
            dv_acc[0:n, :] += _dot_tn(p, dout)
        dk_ref[...] = (_unrot(dk_acc[...], cos_ref[...], sin_ref[...]) * scale).astype(dk_ref.dtype)
        dv_ref[...] = dv_acc[...].astype(dv_ref.dtype)

    head = lambda h: (0, h)
    return pl.pallas_call(
        body, name=name, grid=(RET_HEADS,),
        in_specs=[pl.BlockSpec((T, RET_DK), lambda h: (0, OFF_RQ // RET_DK + h)),
                  pl.BlockSpec((T, RET_DK), lambda h: (0, OFF_RK // RET_DK + h)),
                  pl.BlockSpec((T, RET_DV), lambda h: (0, OFF_RV // RET_DV + h)),
                  _once((T, RET_DK), lambda h: (0, 0)), _once((T, RET_DK), lambda h: (0, 0)),
                  pl.BlockSpec((T, RET_DV), head)],
        out_specs=[pl.BlockSpec((T, RET_DK), head), pl.BlockSpec((T, RET_DK), head), pl.BlockSpec((T, RET_DV), head)],
        out_shape=[jax.ShapeDtypeStruct((T, RET_QK), BF16), jax.ShapeDtypeStruct((T, RET_QK), BF16),
                   jax.ShapeDtypeStruct((T, RET_V), BF16)],
        scratch_shapes=[pltpu.VMEM((T, RET_DK), BF16), pltpu.VMEM((T, RET_DV), BF16),
                        pltpu.VMEM((T, RET_DK), F32), pltpu.VMEM((T, RET_DV), F32),
                        pltpu.VMEM((nq, BLK, BLK), F32)],
        compiler_params=_params("arbitrary"))(z, z, z, cosf, sins, do)


GLA_PAIR = 2


def _gla_chunk(q_ref, k_ref, v_ref, glr_ref, gu, gb, rows, hh, trilf):
    ck = slice(hh * GLA_DK, (hh + 1) * GLA_DK)
    zg = _dot(glr_ref[rows, :], gu[:, ck]) + gb[:, ck]
    la = (jnp.minimum(zg, 0.0) - jnp.log(1.0 + jnp.exp(-jnp.abs(zg)))) * (1.0 / GLA_GATE_NORM)
    cum = jnp.dot(trilf, la, precision=HIGHEST, preferred_element_type=F32)
    last = jnp.sum(la, axis=0, keepdims=True)
    ecum = jnp.exp(cum)
    k = k_ref[rows, ck]
    qt = q_ref[rows, ck] * (GLA_DK ** -0.5) * ecum
    kt = k * jnp.exp(-cum)
    kh = k * jnp.exp(last - cum)
    return zg, cum, last, ecum, qt, kt, kh, v_ref[rows, hh * GLA_DV:(hh + 1) * GLA_DV].astype(BF16)


def _state_decay(last):
    e = jnp.exp(jnp.broadcast_to(last, (GLA_DK, GLA_DK)).T)
    return jnp.concatenate([e] * (GLA_DV // GLA_DK), axis=1)


def _gla_specs(T):
    wk, wv = GLA_PAIR * GLA_DK, GLA_PAIR * GLA_DV
    return [_once((T, wk), lambda h: (0, OFF_GQ // wk + h)),
            _once((T, wk), lambda h: (0, OFF_GK // wk + h)),
            _once((T, wv), lambda h: (0, OFF_GV // wv + h)),
            _once((T, LANE), lambda h: (0, 0)),
            pl.BlockSpec((LANE, wk), lambda h: (0, h)),
            pl.BlockSpec((1, wk), lambda h: (0, h))]


def gla_fwd(name, z, glr, gu, gb, o_prev):
    T = z.shape[0]
    nc = T // CHUNK
    wv = GLA_PAIR * GLA_DV

    def body(q_ref, k_ref, v_ref, glr_ref, gu_ref, gb_ref, prev_ref, o_ref, S):
        del prev_ref
        gu_b, gb_v = gu_ref[...].astype(BF16), gb_ref[...]
        ri = lax.broadcasted_iota(jnp.int32, (CHUNK, CHUNK), 0)
        ci = lax.broadcasted_iota(jnp.int32, (CHUNK, CHUNK), 1)
        tril = ri >= ci
        trilf = tril.astype(F32)
        S[...] = jnp.zeros_like(S)

        def step(c, carry):
            rows = pl.ds(pl.multiple_of(c * CHUNK, CHUNK), CHUNK)
            for hh in range(GLA_PAIR):
                _, _, last, _, qt, kt, kh, v = _gla_chunk(q_ref, k_ref, v_ref, glr_ref, gu_b, gb_v, rows, hh, trilf)
                a = jnp.where(tril, _dot_nt(qt, kt), 0.0)
                s_prev = S[hh]
                o_ref[rows, hh * GLA_DV:(hh + 1) * GLA_DV] = _dot(a, v) + _dot(qt, s_prev)
                S[hh] = s_prev * _state_decay(last) + _dot_tn(kh, v)
            return carry

        lax.fori_loop(0, nc, step, 0)

    n_in = 6
    return pl.pallas_call(
        body, name=name, grid=(GLA_HEADS // GLA_PAIR,),
        in_specs=_gla_specs(T) + [pl.BlockSpec(memory_space=pl.ANY)],
        out_specs=pl.BlockSpec((T, wv), lambda h: (0, RET_V // wv + h)),
        out_shape=jax.ShapeDtypeStruct(o_prev.shape, F32),
        scratch_shapes=[pltpu.VMEM((GLA_PAIR, GLA_DK, GLA_DV), F32)],
        input_output_aliases={n_in: 0},
        compiler_params=_params("arbitrary"))(z, z, z, glr, gu, gb, o_prev)


def gla_bwd(name, z, glr, gu, gb, do):
    T = z.shape[0]
    nc = T // CHUNK

    def body(q_ref, k_ref, v_ref, glr_ref, gu_ref, gb_ref, do_ref,
             dq_ref, dk_ref, dv_ref, dglr_ref, dgu_ref, dgb_ref, s_all, dS):
        gu_b, gb_v = gu_ref[...].astype(BF16), gb_ref[...]
        ri = lax.broadcasted_iota(jnp.int32, (CHUNK, CHUNK), 0)
        ci = lax.broadcasted_iota(jnp.int32, (CHUNK, CHUNK), 1)
        tril = ri >= ci
        trilf = tril.astype(F32)
        triuf = (ri <= ci).astype(F32)
        last_row = lax.broadcasted_iota(jnp.int32, (CHUNK, GLA_DK), 0) == CHUNK - 1
        ones8 = jnp.ones((8, GLA_DV), F32)

        def fstep(c, carry):
            rows = pl.ds(pl.multiple_of(c * CHUNK, CHUNK), CHUNK)
            for hh in range(GLA_PAIR):
                s_prev = dS[hh]
                s_all[hh, c] = s_prev
                _, _, last, _, _, _, kh, v = _gla_chunk(q_ref, k_ref, v_ref, glr_ref, gu_b, gb_v, rows, hh, trilf)
                dS[hh] = s_prev * _state_decay(last) + _dot_tn(kh, v)
            return carry

        dS[...] = jnp.zeros_like(dS)
        lax.fori_loop(0, nc, fstep, 0)
        dS[...] = jnp.zeros_like(dS)
        dgu_ref[...] = jnp.zeros_like(dgu_ref)
        dgb_ref[...] = jnp.zeros_like(dgb_ref)

        def bstep(i, carry):
            c = nc - 1 - i
            rows = pl.ds(pl.multiple_of(c * CHUNK, CHUNK), CHUNK)
            glr_c = glr_ref[rows, :]
            for hh in range(GLA_PAIR):
                ck, cv = slice(hh * GLA_DK, (hh + 1) * GLA_DK), slice(hh * GLA_DV, (hh + 1) * GLA_DV)
                zg, cum, last, ecum, qt, kt, kh, v = _gla_chunk(q_ref, k_ref, v_ref, glr_ref, gu_b, gb_v, rows, hh, trilf)
                a = jnp.where(tril, _dot_nt(qt, kt), 0.0)
                s_prev, ds_new = s_all[hh, c], dS[hh]
                dout = do_ref[rows, cv].astype(BF16)
                dv_ref[rows, cv] = (_dot_tn(a, dout) + _dot(kh, ds_new)).astype(dv_ref.dtype)
                da = jnp.where(tril, _dot_nt(dout, v), 0.0)
                dqt = _dot(da, kt) + _dot_nt(dout, s_prev)
                dkt = _dot_tn(da, qt)
                dkh = _dot_nt(v, ds_new)
                dS[hh] = ds_new * _state_decay(last) + _dot_tn(qt, dout)
                dq_ref[rows, ck] = (dqt * ecum * (GLA_DK ** -0.5)).astype(dq_ref.dtype)
                dk_ref[rows, ck] = (dkt * jnp.exp(-cum) + dkh * jnp.exp(last - cum)).astype(dk_ref.dtype)
                dkh_kh = dkh * kh
                dcum = dqt * qt - dkt * kt - dkh_kh
                rs = lax.dot_general(ones8, ds_new * s_prev, _NT, precision=HIGHEST, preferred_element_type=F32)
                dlast = (jnp.sum(dkh_kh, axis=0, keepdims=True)
                         + jnp.exp(last) * (jnp.sum(rs, axis=0, keepdims=True) * 0.125))
                dcum = dcum + jnp.where(last_row, dlast, 0.0)
                dla = jnp.dot(triuf, dcum, precision=HIGHEST, preferred_element_type=F32)
                dzg = dla * (1.0 / GLA_GATE_NORM) * _sigmoid(-zg)
                dglr_ref[hh, rows, :] = _dot_nt(dzg, gu_b[:, ck])
                dgu_ref[:, ck] += _dot_tn(glr_c, dzg)
                dgb_ref[:, ck] += jnp.sum(dzg, axis=0, keepdims=True)
            return carry

        lax.fori_loop(0, nc, bstep, 0)

    wk, wv = GLA_PAIR * GLA_DK, GLA_PAIR * GLA_DV
    return pl.pallas_call(
        body, name=name, grid=(GLA_HEADS // GLA_PAIR,),
        in_specs=_gla_specs(T) + [_once((T, wv), lambda h: (0, RET_V // wv + h))],
        out_specs=[pl.BlockSpec((T, wk), lambda h: (0, h)), pl.BlockSpec((T, wk), lambda h: (0, h)),
                   pl.BlockSpec((T, wv), lambda h: (0, h)),
                   pl.BlockSpec((GLA_PAIR, T, LANE), lambda h: (h, 0, 0)),
                   pl.BlockSpec((LANE, wk), lambda h: (0, h)), pl.BlockSpec((1, wk), lambda h: (0, h))],
        out_shape=[jax.ShapeDtypeStruct((T, GLA_QK), BF16), jax.ShapeDtypeStruct((T, GLA_QK), BF16),
                   jax.ShapeDtypeStruct((T, GLA_V), BF16), jax.ShapeDtypeStruct((GLA_HEADS, T, LANE), F32),
                   jax.ShapeDtypeStruct((LANE, GLA_QK), F32), jax.ShapeDtypeStruct((1, GLA_QK), F32)],
        scratch_shapes=[pltpu.VMEM((GLA_PAIR, nc, GLA_DK, GLA_DV), F32), pltpu.VMEM((GLA_PAIR, GLA_DK, GLA_DV), F32)],
        compiler_params=_params("arbitrary"))(z, z, z, glr, gu, gb, do)


HN_HEADS = RET_HEADS + GLA_HEADS
HN_W = RET_DV


def _gate_col(h):
    return jnp.where(h < RET_HEADS, OFF_RG // HN_W + h, OFF_GG // HN_W + h - RET_HEADS)


def headnorm_fwd(name, oraw, z, w, tr=256):
    T = oraw.shape[0]

    def body(o_ref, g_ref, w_ref, y_ref):
        y_ref[...] = (_rms(o_ref[...], w_ref[...]) * _silu_and_grad(g_ref[...])[0]).astype(y_ref.dtype)

    return pl.pallas_call(
        body, name=name, grid=(HN_HEADS, T // tr),
        in_specs=[pl.BlockSpec((tr, HN_W), lambda h, i: (i, h)),
                  pl.BlockSpec((tr, HN_W), lambda h, i: (i, _gate_col(h))),
                  pl.BlockSpec((1, HN_W), lambda h, i: (0, h))],
        out_specs=pl.BlockSpec((tr, HN_W), lambda h, i: (i, h)),
        out_shape=jax.ShapeDtypeStruct((T, HN_HEADS * HN_W), BF16),
        compiler_params=_params("arbitrary", "arbitrary"))(oraw, z, w)


def headnorm_bwd(name, oraw, z, w, dy, tr=256):
    T = oraw.shape[0]

    def body(o_ref, g_ref, w_ref, dy_ref, do_ref, dg_ref, dw_ref):
        o, wv, dyv = o_ref[...], w_ref[...], dy_ref[...].astype(F32)
        silu, dsilu = _silu_and_grad(g_ref[...])
        n = _rms(o, wv)
        dg_ref[...] = (dyv * n * dsilu).astype(dg_ref.dtype)
        dx, dw = _rms_bwd(o, wv, dyv * silu)
        do_ref[...] = dx
        _accumulate(dw_ref, dw, pl.program_id(1) == 0)

    blk = pl.BlockSpec((tr, HN_W), lambda h, i: (i, h))
    return pl.pallas_call(
        body, name=name, grid=(HN_HEADS, T // tr),
        in_specs=[blk, pl.BlockSpec((tr, HN_W), lambda h, i: (i, _gate_col(h))),
                  pl.BlockSpec((1, HN_W), lambda h, i: (0, h)), blk],
        out_specs=[blk, blk, pl.BlockSpec((1, HN_W), lambda h, i: (0, h))],
        out_shape=[jax.ShapeDtypeStruct((T, HN_HEADS * HN_W), F32),
                   jax.ShapeDtypeStruct((T, HN_HEADS * HN_W), BF16),
                   jax.ShapeDtypeStruct((1, HN_HEADS * HN_W), F32)],
        compiler_params=_params("arbitrary", "arbitrary"))(oraw, z, w, dy)


N_MASKS = 4


def _check_mask_classes(T):
    for window, dilation in DILATED_BRANCHES[:-1]:
        assert window < (N_MASKS - 1) * BLK - (BLK - 1) and BLK % dilation == 0
    assert DILATED_BRANCHES[-1][0] >= T and BLK % DILATED_BRANCHES[-1][1] == 0


def _fill_masks(mult_ref, bias_ref):
    ri = lax.broadcasted_iota(jnp.int32, (BLK, BLK), 0)
    ci = lax.broadcasted_iota(jnp.int32, (BLK, BLK), 1)
    for d in range(N_MASKS):
        dt = d * BLK + ri - ci
        mult = jnp.zeros((BLK, BLK), F32)
        for window, dilation in DILATED_BRANCHES:
            hit = (dt >= 0) & (dt <= window) & ((dt & (dilation - 1)) == 0)
            mult = mult + hit.astype(F32)
        mult_ref[d] = mult
        bias_ref[d] = jnp.where(mult > 0, 0.0, -1e30)


def _mask_row(ref, qi):
    return jnp.concatenate([ref[min(qi - kb, N_MASKS - 1)] for kb in range(qi + 1)], axis=1)


def attn_fwd(name, qkv):
    T = qkv.shape[0]
    D = qkv.shape[1] // 3
    dh = D // ATT_HEADS
    nq = T // BLK
    scale = dh ** -0.5

    _check_mask_classes(T)

    def body(q_ref, k_ref, v_ref, o_ref, lse_ref, mult_ref, bias_ref):
        @pl.when(pl.program_id(0) == 0)
        def _():
            _fill_masks(mult_ref, bias_ref)

        for qi in range(nq):
            rows, n = slice(qi * BLK, (qi + 1) * BLK), (qi + 1) * BLK
            s = (_dot_nt(q_ref[rows, :], k_ref[0:n, :]) * scale
                 + _mask_row(bias_ref, qi))
            m = jnp.max(s, axis=-1, keepdims=True)
            p = _mask_row(mult_ref, qi) * jnp.exp(s - m)
            l = jnp.sum(p, axis=-1, keepdims=True)
            o_ref[rows, :] = (_dot(p, v_ref[0:n, :]) / l).astype(o_ref.dtype)
            lse_ref[rows, :] = jnp.broadcast_to(m + jnp.log(l), (BLK, LANE))

    return pl.pallas_call(
        body, name=name, grid=(ATT_HEADS,),
        in_specs=[pl.BlockSpec((T, dh), lambda h: (0, h)),
                  pl.BlockSpec((T, dh), lambda h: (0, ATT_HEADS + h)),
                  pl.BlockSpec((T, dh), lambda h: (0, 2 * ATT_HEADS + h))],
        out_specs=[pl.BlockSpec((T, dh), lambda h: (0, h)),
                   pl.BlockSpec((None, T, LANE), lambda h: (h, 0, 0))],
        out_shape=[jax.ShapeDtypeStruct((T, D), BF16), jax.ShapeDtypeStruct((ATT_HEADS, T, LANE), F32)],
        scratch_shapes=[pltpu.VMEM((N_MASKS, BLK, BLK), F32), pltpu.VMEM((N_MASKS, BLK, BLK), F32)],
        compiler_params=_params("arbitrary"))(qkv, qkv, qkv)


def attn_bwd(name, qkv, o, lse, do):
    T = qkv.shape[0]
    D = qkv.shape[1] // 3
    dh = D // ATT_HEADS
    nq = T // BLK
    scale = dh ** -0.5

    _check_mask_classes(T)

    def body(q_ref, k_ref, v_ref, o_ref, lse_ref, do_ref, dq_ref, dk_ref, dv_ref, dk_acc, dv_acc, mult_ref, bias_ref):
        @pl.when(pl.program_id(0) == 0)
        def _():
            _fill_masks(mult_ref, bias_ref)

        dk_acc[...] = jnp.zeros_like(dk_acc)
        dv_acc[...] = jnp.zeros_like(dv_acc)
        for qi in range(nq):
            rows, n = slice(qi * BLK, (qi + 1) * BLK), (qi + 1) * BLK
            q, dout = q_ref[rows, :], do_ref[rows, :]
            kk, vv = k_ref[0:n, :], v_ref[0:n, :]
            delta = jnp.sum(dout.astype(F32) * o_ref[rows, :].astype(F32), axis=-1, keepdims=True)
            lse = jnp.max(lse_ref[rows, :], axis=-1, keepdims=True)
            s = _dot_nt(q, kk) * scale + _mask_row(bias_ref, qi)
            p = _mask_row(mult_ref, qi) * jnp.exp(s - lse)
            ds = (p * (_dot_nt(dout, vv) - delta) * scale).astype(BF16)
            dq_ref[rows, :] = _dot(ds, kk).astype(dq_ref.dtype)
            dk_acc[0:n, :] += _dot_tn(ds, q)
            dv_acc[0:n, :] += _dot_tn(p, dout)
        dk_ref[...] = dk_acc[...].astype(dk_ref.dtype)
        dv_ref[...] = dv_acc[...].astype(dv_ref.dtype)

    full = pl.BlockSpec((T, dh), lambda h: (0, h))
    return pl.pallas_call(
        body, name=name, grid=(ATT_HEADS,),
        in_specs=[full, pl.BlockSpec((T, dh), lambda h: (0, ATT_HEADS + h)),
                  pl.BlockSpec((T, dh), lambda h: (0, 2 * ATT_HEADS + h)),
                  full, pl.BlockSpec((None, T, LANE), lambda h: (h, 0, 0)), full],
        out_specs=[full, full, full],
        out_shape=[jax.ShapeDtypeStruct((T, D), BF16)] * 3,
        scratch_shapes=[pltpu.VMEM((T, dh), F32), pltpu.VMEM((T, dh), F32),
                        pltpu.VMEM((N_MASKS, BLK, BLK), F32), pltpu.VMEM((N_MASKS, BLK, BLK), F32)],
        compiler_params=_params("arbitrary"))(qkv, qkv, qkv, o, lse, do)


def _mesh_pos():
    mx, my, mc = lax.axis_index("x"), lax.axis_index("y"), lax.axis_index("c")
    return mx, my, mc, 4 * mx + 2 * my + mc


def _peer(k, mx, my, mc):
    px, py, pc = mx ^ (k >> 2), my ^ ((k >> 1) & 1), mc ^ (k & 1)
    return (px, py, pc), 4 * px + 2 * py + pc


_SIBLING = 1
_OTHER_CHIPS = (4, 2, 6)
N_CHIP = N_DEV // 2
_PLANS = {"gather": (2, N_DEV - 1), "to_chips": (2, 1 + len(_OTHER_CHIPS)), "pass_on": (1, len(_OTHER_CHIPS)),
          "halves": (2, N_CHIP), "chip_sums": (2, len(_OTHER_CHIPS))}


def _copies(kind, items, send_sems, recv_sems):
    mx, my, mc, me = _mesh_pos()
    out = []

    def add(n, src, dst, peer):
        out.append(pltpu.make_async_remote_copy(
            src_ref=src, dst_ref=dst, send_sem=send_sems.at[n], recv_sem=recv_sems.at[n],
            device_id=peer, device_id_type=pl.DeviceIdType.MESH))

    per_item = _PLANS[kind][1]
    sibling = _peer(_SIBLING, mx, my, mc)[0]
    for i, refs in enumerate(items):
        n = i * per_item
        if kind == "gather":
            for k in range(1, N_DEV):
                add(n + k - 1, refs[0], refs[1].at[me], _peer(k, mx, my, mc)[0])
        elif kind == "to_chips":
            for j, k in enumerate((_SIBLING,) + _OTHER_CHIPS):
                add(n + j, refs[0], refs[1].at[me], _peer(k, mx, my, mc)[0])
        elif kind == "pass_on":
            for j, k in enumerate(_OTHER_CHIPS):
                add(n + j, refs[0].at[me ^ k], refs[0].at[me ^ k], sibling)
        elif kind == "halves":
            for chip in range(N_CHIP):
                add(n + chip, refs[0].at[2 * chip + 1 - mc], refs[1].at[chip], sibling)
        else:
            for j, k in enumerate(_OTHER_CHIPS):
                peer, to = _peer(k, mx, my, mc)
                add(n + j, refs[0].at[to // 2], refs[1].at[me // 2], peer)
    return out


_HBM = pl.BlockSpec(memory_space=pltpu.HBM)
_SEM = pl.BlockSpec(memory_space=pltpu.SEMAPHORE)
_DATAFLOW = pltpu.SideEffectType.DATAFLOW_SIDE_EFFECTING


def exchange_call(name, waits, starts, deps=()):
    bufs, slot_of = [], {}

    def slots(items):
        out = []
        for item in items:
            for b in item:
                if id(b) not in slot_of:
                    slot_of[id(b)] = len(bufs)
                    bufs.append(b)
            out.append(tuple(slot_of[id(b)] for b in item))
        return out

    wait_plan = [(kind, slots(handle[0])) for kind, handle in waits]
    start_plan = [(kind, slots(items)) for kind, items in starts]
    wait_sems = [s for _, handle in waits for s in handle[1:]]
    n_buf, n_ws, n_start = len(bufs), len(wait_sems), len(starts)

    def body(*refs):
        buf_refs, sems_in = refs[:n_buf], refs[n_buf:n_buf + n_ws]
        outs = refs[n_buf + n_ws + len(deps):]
        pick = lambda plan: [tuple(buf_refs[s] for s in item) for item in plan]
        for wi, (kind, plan) in enumerate(wait_plan):
            copies = _copies(kind, pick(plan), sems_in[2 * wi], sems_in[2 * wi + 1])
            for cp in copies:
                cp.wait_send()
            for cp in copies:
                cp.wait_recv()
        for si, (kind, plan) in enumerate(start_plan):
            for cp in _copies(kind, pick(plan), outs[2 * si], outs[2 * si + 1]):
                cp.start()
        outs[-1][...] = jnp.zeros_like(outs[-1])

    hbm_bufs = [pltpu.with_memory_space_constraint(b, pltpu.HBM) for b in bufs]
    sem_shapes = []
    for kind, plan in start_plan:
        sem_shapes += [pltpu.SemaphoreType.DMA((len(plan) * _PLANS[kind][1],))] * 2
    outs = pl.pallas_call(
        body, name=name,
        out_shape=sem_shapes + [pltpu.HBM(b.shape, b.dtype) for b in bufs] + [jax.ShapeDtypeStruct((8, LANE), F32)],
        in_specs=[_HBM] * n_buf + [_SEM] * n_ws + [_ANY] * len(deps),
        out_specs=[_SEM] * (2 * n_start) + [_HBM] * n_buf + [pl.BlockSpec(memory_space=pltpu.VMEM)],
        input_output_aliases={i: 2 * n_start + i for i in range(n_buf)},
        compiler_params=pltpu.CompilerParams(has_side_effects=_DATAFLOW))(*hbm_bufs, *wait_sems, *deps)
    sems, thru, token = outs[:2 * n_start], outs[2 * n_start:-1], outs[-1]
    through = lambda plan: [tuple(thru[s] for s in item) for item in plan]
    waited = [through(plan) for _, plan in wait_plan]
    handles = [(through(plan), sems[2 * si], sems[2 * si + 1]) for si, (_, plan) in enumerate(start_plan)]
    return waited, handles, token


def gather_small(name, a, deps=()):
    def body(a_ref, *rest):
        o_ref, send_sems, recv_sems, local_sem = rest[len(deps):]
        me = _mesh_pos()[3]
        own = pltpu.make_async_copy(a_ref, o_ref.at[me], local_sem)
        own.start()
        copies = _copies("gather", [(a_ref, o_ref)], send_sems, recv_sems)
        for cp in copies:
            cp.start()
        for cp in copies:
            cp.wait_recv()
        for cp in copies:
            cp.wait_send()
        own.wait()

    return pl.pallas_call(
        body, name=name, in_specs=[_ANY] * (1 + len(deps)), out_specs=_ANY,
        out_shape=jax.ShapeDtypeStruct((N_DEV,) + a.shape, a.dtype),
        scratch_shapes=[pltpu.SemaphoreType.DMA((N_DEV - 1,)), pltpu.SemaphoreType.DMA((N_DEV - 1,)),
                        pltpu.SemaphoreType.DMA],
        compiler_params=pltpu.CompilerParams(has_side_effects=True))(a, *deps)


def _adamw_math(w, g, m, v):
    m2 = ADAM_B1 * m + (1.0 - ADAM_B1) * g
    v2 = ADAM_B2 * v + (1.0 - ADAM_B2) * (g * g)
    m_hat = m2 / (1.0 - ADAM_B1 ** ADAM_STEP)
    v_hat = v2 / (1.0 - ADAM_B2 ** ADAM_STEP)
    delta = -ADAM_LR * (m_hat / (jnp.sqrt(v_hat) + ADAM_EPS) + ADAM_WD * w)
    return delta, m2, v2


def chip_sum(name, a, half):
    _, r, c = a.shape
    tr = _pick(r, (256, 128, 64, 32, 16))

    def body(core_ref, a_ref, h_ref, o_ref):
        del core_ref
        o_ref[...] = (a_ref[...].astype(F32) + h_ref[...].astype(F32)).astype(o_ref.dtype)

    blk = pl.BlockSpec((N_CHIP, tr, c), lambda i, core: (0, i, 0))
    grid_spec = pltpu.PrefetchScalarGridSpec(
        num_scalar_prefetch=1, grid=(r // tr,),
        in_specs=[pl.BlockSpec((N_CHIP, None, tr, c), lambda i, core: (0, core[0], i, 0)), blk], out_specs=blk)
    return pl.pallas_call(
        body, name=name, grid_spec=grid_spec, out_shape=jax.ShapeDtypeStruct((N_CHIP, r, c), BF16),
        compiler_params=_params("parallel"))(lax.axis_index("c").astype(jnp.int32).reshape(1),
                                             a.reshape(N_CHIP, 2, r, c), half)


def adamw(name, w, m, v, l, land, own, prev=None):
    L, r, c = w.shape
    cp = land.shape[2]
    tr = _pick(r, (256, 176, 128, 64, 32, 16, 8))

    def body(w_ref, m_ref, v_ref, land_ref, own_ref, *rest):
        g_ref, d_ref, m2_ref, v2_ref = rest[-4:]
        chip = _mesh_pos()[3] // 2
        mine = own_ref[:, pl.ds(0, c)].astype(F32)
        g = None
        for s in range(N_CHIP):
            part = jnp.where(chip == s, mine, land_ref[s, :, pl.ds(0, c)].astype(F32))
            g = part if g is None else g + part
        delta, m2, v2 = _adamw_math(w_ref[...], g, m_ref[...], v_ref[...])
        g_ref[...] = g
        d_ref[...] = delta
        m2_ref[...] = m2
        v2_ref[...] = v2

    blk = pl.BlockSpec((None, tr, c), lambda i: (l, i, 0))
    shape = jax.ShapeDtypeStruct((L, r, c), F32)
    extra = [] if prev is None else list(prev)
    return pl.pallas_call(
        body, name=name, grid=(r // tr,),
        in_specs=[blk, blk, blk, pl.BlockSpec((N_CHIP, tr, cp), lambda i: (0, i, 0)),
                  pl.BlockSpec((None, tr, cp), lambda i: (_mesh_pos()[3] // 2, i, 0))] + [_ANY] * len(extra),
        out_specs=[blk] * 4, out_shape=[shape] * 4,
        input_output_aliases={5 + k: k for k in range(len(extra))},
        compiler_params=_params("parallel"))(w, m, v, land, own, *extra)


def adamw_small(name, w, m, v, parts):
    n = w.shape[1]

    def body(w_ref, m_ref, v_ref, p_ref, g_ref, d_ref, m2_ref, v2_ref):
        g = p_ref[0:1, :]
        for s in range(1, N_DEV):
            g = g + p_ref[s:s + 1, :]
        delta, m2, v2 = _adamw_math(w_ref[...], g, m_ref[...], v_ref[...])
        g_ref[...] = g
        d_ref[...] = delta
        m2_ref[...] = m2
        v2_ref[...] = v2

    shape = jax.ShapeDtypeStruct((1, n), F32)
    return pl.pallas_call(body, name=name, out_shape=[shape] * 4,
                          compiler_params=pltpu.CompilerParams(vmem_limit_bytes=VMEM_LIMIT_BYTES))(w, m, v, parts)


def _rope_tables(positions):
    half = RET_DK // 2
    inv_freq = 1.0 / jnp.power(RET_THETA_BASE, jnp.linspace(0.0, 1.0, half, dtype=F32))
    ang = positions.astype(F32)[:, None] * inv_freq
    cos, sin = jnp.cos(ang), jnp.sin(ang)
    cosf = jnp.repeat(cos, 2, axis=-1)
    sins = jnp.stack([-sin, sin], axis=-1).reshape(cosf.shape)
    return cosf, sins


def _pad_to(a, axis, size):
    pad = [(0, 0)] * a.ndim
    pad[axis] = (0, size - a.shape[axis])
    return jnp.pad(a, pad)


def _round_up(n, m):
    return -(-n // m) * m


def kernel(x, p, positions, attn_norm_w, ffn_norm_w, ple_norm_w, final_norm_w, ab_w_in, ab_gla_gate_up, ab_gla_gate_b, ab_ret_norm_w, ab_gla_norm_w, ab_w_out, c_w_qkv, c_w_out, ffn_w_gate, ffn_w_up, ffn_w_down, ple_w_proj, ple_w_gate, loss_target, m_attn_norm_w, m_ffn_norm_w, m_ple_norm_w, m_final_norm_w, m_ab_w_in, m_ab_gla_gate_up, m_ab_gla_gate_b, m_ab_ret_norm_w, m_ab_gla_norm_w, m_ab_w_out, m_c_w_qkv, m_c_w_out, m_ffn_w_gate, m_ffn_w_up, m_ffn_w_down, m_ple_w_proj, m_ple_w_gate, v_attn_norm_w, v_ffn_norm_w, v_ple_norm_w, v_final_norm_w, v_ab_w_in, v_ab_gla_gate_up, v_ab_gla_gate_b, v_ab_ret_norm_w, v_ab_gla_norm_w, v_ab_w_out, v_c_w_qkv, v_c_w_out, v_ffn_w_gate, v_ffn_w_up, v_ffn_w_down, v_ple_w_proj, v_ple_w_gate):
    T, D = x.shape[1], x.shape[2]
    depth = attn_norm_w.shape[0]
    assert ab_w_in.shape[0] == 1 and c_w_qkv.shape[0] == 1 and depth == 2, "one even and one odd layer"
    me = 4 * lax.axis_index("x") + 2 * lax.axis_index("y") + lax.axis_index("c")
    in_shard = ab_w_in.shape[2]
    in_width = in_shard * N_DEV
    assert in_width == OFF_LR + GLA_GATE_RANK
    fs = ffn_w_gate.shape[2]
    fp = _round_up(fs, LANE)
    gu_cols = ab_gla_gate_up.shape[2]

    bf = lambda a: a.astype(BF16)
    tr_ = lambda a: jnp.swapaxes(a, -1, -2)
    wg_t, wu_t = tr_(ffn_w_gate), tr_(ffn_w_up)
    srcs = {"w_in": bf(tr_(ab_w_in[0])), "w_oab": bf(ab_w_out[0]), "gu": ab_gla_gate_up[0],
            "w_qkv": bf(c_w_qkv[0]), "w_oc": bf(c_w_out[0])}
    for l in range(depth):
        srcs[f"wg{l}"] = _pad_to(bf(wg_t[l]), 0, fp)
        srcs[f"wu{l}"] = _pad_to(bf(wu_t[l]), 0, fp)
        srcs[f"wd{l}"] = _pad_to(bf(ffn_w_down[l]), 0, fp)
        srcs[f"wpg{l}"] = bf(ple_w_gate[l])
        srcs[f"wpp{l}"] = bf(ple_w_proj[l])
    group_keys = [["w_in"], ["gu", "w_oab"], ["wg0", "wu0"], ["wd0", "wpg0", "wpp0"], ["w_qkv", "w_oc"],
                  ["wg1", "wu1"], ["wd1", "wpg1", "wpp1"]]
    G_IN, G_OUT, G_QKV = 0, 1, 4
    g_ffn = lambda layer: (2, 3) if layer == 0 else (5, 6)

    def landing(a):
        return lax.dynamic_update_slice(lax.empty((N_DEV,) + a.shape, a.dtype), a[None], (me,) + (0,) * a.ndim)

    _, chip_handles, gather_token = exchange_call(
        "gather_start", [], [("to_chips", [(srcs[k], landing(srcs[k])) for k in keys]) for keys in group_keys])
    weights = {}

    def gather_wait(gi, dep):
        lands = [(land,) for _, land in chip_handles[gi][0]]
        _, (passing,), _ = exchange_call(
            f"gather{gi}_pass", [("to_chips", chip_handles[gi])], [("pass_on", lands)], deps=(dep,))
        (complete,), _, _ = exchange_call(f"gather{gi}_done", [("pass_on", passing)], [])
        weights.update(zip(group_keys[gi], [land for (land,) in complete]))

    gb = ab_gla_gate_b
    hn_w = jnp.concatenate([ab_ret_norm_w, ab_gla_norm_w], axis=1)
    cosf, sins = _rope_tables(positions[0])
    p_bf = bf(p[:, 0])

    xs = x[0]
    saved = []
    for i in range(depth):
        nm = f"l{i}_"
        w_attn, w_ffn, w_ple = attn_norm_w[i:i + 1], ffn_norm_w[i:i + 1], ple_norm_w[i:i + 1]
        (xn,) = rowwise(nm + "norm_attn", lambda a, w: (_rms(a, w),), T, [("row", xs), ("full", w_attn)],
                        [("row", D, BF16)], deps=(gather_token,) if i == 0 else ())
        if i % 2 == 0:
            gather_wait(G_IN, xn)
            w_in_t = weights["w_in"].reshape(1, 1, in_width, D)
            w_lr_t = _pad_to(w_in_t[0, 0, OFF_LR:], 0, LANE).reshape(1, 1, LANE, D)
            z = mmt_fwd(nm + "mm_in", xn, w_in_t, 0, F32, n=OFF_LR)
            glr = mmt_fwd(nm + "mm_lr", xn, w_lr_t, 0, F32)
            oraw = retention_fwd(nm + "ret_fwd", z, cosf, sins, RET_V + GLA_V)
            gather_wait(G_OUT, oraw)
            w_oab = weights["w_oab"].reshape(1, 1, D, D)
            gu_full = _pad_to(weights["gu"].transpose(1, 0, 2).reshape(GLA_GATE_RANK, GLA_QK), 0, LANE)
            oraw = gla_fwd(nm + "gla_fwd", z, glr, gu_full, gb, oraw)
            o = headnorm_fwd(nm + "headnorm_fwd", oraw, z, hn_w)
            mix = mm_nn(nm + "mm_out", o, w_oab, 0, F32)
            mixer_saved = (z, glr, oraw, o)
        else:
            gather_wait(G_QKV, xn)
            w_qkv = weights["w_qkv"].reshape((1,) + weights["w_qkv"].shape)
            w_oc = weights["w_oc"].reshape(1, 1, D, D)
            qkv = mm_nn(nm + "mm_qkv", xn, w_qkv, 0, BF16)
            o, lse = attn_fwd(nm + "attn_fwd", qkv)
            mix = mm_nn(nm + "mm_out", o, w_oc, 0, F32)
            mixer_saved = (qkv, o, lse)
        h1, hn = rowwise(nm + "add_norm_ffn", lambda a, b, w: (a + b, _rms(a + b, w)), T,
                         [("row", xs), ("row", mix), ("full", w_ffn)], [("row", D, F32), ("row", D, BF16)])
        gather_wait(g_ffn(i)[0], hn)
        wg = weights[f"wg{i}"].reshape(1, N_DEV, fp, D)
        wu = weights[f"wu{i}"].reshape(1, N_DEV, fp, D)
        g, u, act = ffn_gate_up(nm + "ffn_gate_up", hn, wg, wu)
        gather_wait(g_ffn(i)[1], act)
        wd = weights[f"wd{i}"].reshape(1, 1, N_DEV * fp, D)
        wpg = weights[f"wpg{i}"].reshape(1, 1, D, D)
        wpp = weights[f"wpp{i}"].reshape((1,) + weights[f"wpp{i}"].shape)
        f = mm_nn(nm + "mm_down", act, wd, 0, F32)
        h2, pn = rowwise(nm + "add_norm_ple", lambda a, b, w: (a + b, _rms(a + b, w)), T,
                         [("row", h1), ("row", f), ("full", w_ple)], [("row", D, F32), ("row", D, BF16)])
        s = mm_nn(nm + "mm_ple_gate", pn, wpg, 0, F32)
        e = mm_nn(nm + "mm_ple_proj", p_bf[i], wpp, 0, F32)
        (x_next,) = rowwise(nm + "ple_out", lambda a, b, c: (a + _sigmoid(b) * c,), T,
                            [("row", h2), ("row", s), ("row", e)], [("row", D, F32)])
        mixer_w = (w_in_t, w_lr_t, w_oab, gu_full) if i % 2 == 0 else (w_qkv, w_oc)
        saved.append((xs, xn, mixer_saved, mixer_w, (wg, wu, wd, wpg), h1, hn, g, u, act, h2, pn, s, e))
        xs = x_next

    def loss_fn(a, w, t):
        diff = _rms(a, w) - t
        dx, dw = _rms_bwd(a, w, diff * (1.0 / D))
        part = 0.5 * jnp.sum(jnp.mean(diff * diff, axis=-1, keepdims=True), axis=0, keepdims=True)
        return dx, dw, jnp.broadcast_to(part, (1, LANE))

    dx, d_final_w, loss_part = rowwise("loss_head", loss_fn, T,
                                       [("row", xs), ("full", final_norm_w[None, :]), ("row", loss_target[0])],
                                       [("row", D, F32), ("acc", D), ("acc", LANE)])
    loss = lax.psum(loss_part[0, 0], ("x", "y", "c"))

    grads = {}
    on_chip = []
    scatters = []

    def scatter_start(name, keys, deps=()):
        waits = [("halves", on_chip[0][1])] if on_chip else []
        starts = [("halves", [(grads[k], lax.empty((N_CHIP,) + grads[k].shape[1:], BF16)) for k in keys])] if keys else []
        waited, handles, token = exchange_call(name, waits, starts, deps=deps)
        if on_chip:
            done_keys, _ = on_chip.pop()
            sums = [chip_sum(f"{name}_sum{j}", a, half) for j, (a, half) in enumerate(waited[0])]
            _, (handle,), token = exchange_call(
                name + "_chips", [], [("chip_sums", [(cs, lax.empty(cs.shape, BF16)) for cs in sums])])
            scatters.append((done_keys, handle))
        if keys:
            on_chip.append((keys, handles[0]))
        return token

    d_attn_w, d_ffn_w, d_ple_w = [None] * depth, [None] * depth, [None] * depth
    for i in reversed(range(depth)):
        nm = f"l{i}_b_"
        xs_i, xn, mixer_saved, mixer_w, (wg, wu, wd, wpg), h1, hn, g, u, act, h2, pn, s, e = saved[i]
        w_attn, w_ffn, w_ple = attn_norm_w[i:i + 1], ffn_norm_w[i:i + 1], ple_norm_w[i:i + 1]

        def ple_bwd(d, sv, ev):
            gate = _sigmoid(sv)
            return d * gate, d * ev * gate * (1.0 - gate)

        de, ds = rowwise(nm + "ple_out", ple_bwd, T, [("row", dx), ("row", s), ("row", e)],
                         [("row", D, BF16), ("row", D, BF16)], deps=(loss.reshape(1, 1),) if i == depth - 1 else ())
        grads[("ple_w_proj", i)] = mm_tn(nm + "mm_ple_proj_w", p_bf[i], de, N_DEV, BF16)
        grads[("ple_w_gate", i)] = mm_tn(nm + "mm_ple_gate_w", pn, ds, 1, BF16).reshape(N_DEV, D // N_DEV, D)
        dpn = mm_nt(nm + "mm_ple_gate_x", ds, wpg, 0, F32)

        def norm_bwd_add(a, w, dn, dres):
            dxx, dw = _rms_bwd(a, w, dn)
            tot = dres + dxx
            return tot, tot, dw

        dh2, dh2_bf, d_ple_w[i] = rowwise(nm + "norm_ple", norm_bwd_add, T,
                                          [("row", h2), ("full", w_ple), ("row", dpn), ("row", dx)],
                                          [("row", D, F32), ("row", D, BF16), ("acc", D)])
        grads[("ffn_w_down", i)] = mm_tn(nm + "mm_down_w", act, dh2_bf, 1, BF16).reshape(N_DEV, fp, D)
        token = scatter_start(nm + "scatter_ple_down", [("ple_w_proj", i), ("ple_w_gate", i), ("ffn_w_down", i)])
        dg, du = ffn_down_bwd(nm + "ffn_down_x", dh2_bf, wd, g, u, deps=(token,))
        grads[("ffn_w_gate", i)] = mmt_dw(nm + "mm_gate_w", dg, hn, N_DEV, BF16)
        grads[("ffn_w_up", i)] = mmt_dw(nm + "mm_up_w", du, hn, N_DEV, BF16)
        token = scatter_start(nm + "scatter_gate_up", [("ffn_w_gate", i), ("ffn_w_up", i)])
        dhn_g = mmt_dx_wide(nm + "mm_gate_x", dg, wg, F32, deps=(token,))
        dhn_u = mmt_dx_wide(nm + "mm_up_x", du, wu, F32)

        def norm_bwd_add2(a, w, dn1, dn2, dres):
            dxx, dw = _rms_bwd(a, w, dn1 + dn2)
            tot = dres + dxx
            return tot, tot, dw

        dh1, dh1_bf, d_ffn_w[i] = rowwise(nm + "norm_ffn", norm_bwd_add2, T,
                                          [("row", h1), ("full", w_ffn), ("row", dhn_g), ("row", dhn_u), ("row", dh2)],
                                          [("row", D, F32), ("row", D, BF16), ("acc", D)])
        if i % 2 == 0:
            z, glr, oraw, o = mixer_saved
            w_in_t, w_lr_t, w_oab, gu_full = mixer_w
            grads[("ab_w_out", 0)] = mm_tn(nm + "mm_out_w", o, dh1_bf, 1, BF16).reshape(N_DEV, D // N_DEV, D)
            token = scatter_start(nm + "scatter_out", [("ab_w_out", 0)])
            do = mm_nt(nm + "mm_out_x", dh1_bf, w_oab, 0, F32, deps=(token,))
            d_oraw, d_gates, d_hn_w = headnorm_bwd(nm + "headnorm", oraw, z, hn_w, do)
            d_rq, d_rk, d_rv = retention_bwd(nm + "ret", z, cosf, sins, d_oraw)
            d_gq, d_gk, d_gv, d_glr4, d_gu, d_gb = gla_bwd(nm + "gla", z, glr, gu_full, gb, d_oraw)
            dz = jnp.concatenate([d_rq, d_rk, d_rv, d_gates[:, :RET_V], d_gq, d_gk, d_gv, d_gates[:, RET_V:]], axis=1)
            (d_glr,) = rowwise(nm + "sum_lr", lambda *a: (a[0] + a[1] + a[2] + a[3],), T,
                               [("row", d_glr4[hh]) for hh in range(GLA_HEADS)], [("row", LANE, BF16)])
            dw_main = mm_tn(nm + "mm_in_w", xn, dz, 1, BF16)[0]
            dw_lr = mm_tn(nm + "mm_lr_w", xn, d_glr, 1, BF16)[0]
            dw_in = jnp.concatenate([dw_main, dw_lr[:, :GLA_GATE_RANK]], axis=1)
            grads[("ab_w_in", 0)] = dw_in.reshape(D, N_DEV, in_shard).transpose(1, 0, 2)
            token = scatter_start(nm + "scatter_in", [("ab_w_in", 0)])
            dxn_a = mmt_dx_wide(nm + "mm_in_x", dz, w_in_t, F32, n=OFF_LR, deps=(token,))
            dxn_b = mmt_dx(nm + "mm_lr_x", d_glr, w_lr_t, 0, F32)
        else:
            qkv, o, lse = mixer_saved
            w_qkv, w_oc = mixer_w
            grads[("c_w_out", 0)] = mm_tn(nm + "mm_out_w", o, dh1_bf, 1, BF16).reshape(N_DEV, D // N_DEV, D)
            do = mm_nt(nm + "mm_out_x", dh1_bf, w_oc, 0, BF16)
            dq, dk, dv = attn_bwd(nm + "attn", qkv, o, lse, do)
            dqkv = jnp.concatenate([dq, dk, dv], axis=1)
            grads[("c_w_qkv", 0)] = mm_tn(nm + "mm_qkv_w", xn, dqkv, N_DEV, BF16)
            token = scatter_start(nm + "scatter_attn", [("c_w_out", 0), ("c_w_qkv", 0)])
            dxn_a = mm_nt_wide(nm + "mm_qkv_x", dqkv, w_qkv, F32, deps=(token,))
            dxn_b = None
        if dxn_b is None:
            dx, _, d_attn_w[i] = rowwise(nm + "norm_attn", norm_bwd_add, T,
                                         [("row", xs_i), ("full", w_attn), ("row", dxn_a), ("row", dh1)],
                                         [("row", D, F32), ("row", D, BF16), ("acc", D)])
        else:
            dx, _, d_attn_w[i] = rowwise(nm + "norm_attn", norm_bwd_add2, T,
                                         [("row", xs_i), ("full", w_attn), ("row", dxn_a), ("row", dxn_b), ("row", dh1)],
                                         [("row", D, F32), ("row", D, BF16), ("acc", D)])

    small_names = ["attn_norm_w", "ffn_norm_w", "ple_norm_w", "final_norm_w", "ab_gla_gate_b", "ab_ret_norm_w",
                   "ab_gla_norm_w"]
    small_grads = [jnp.concatenate(d_attn_w, 0), jnp.concatenate(d_ffn_w, 0), jnp.concatenate(d_ple_w, 0), d_final_w[0],
                   d_gb, d_hn_w[:, :RET_V], d_hn_w[:, RET_V:]]
    small_w = [attn_norm_w, ffn_norm_w, ple_norm_w, final_norm_w, ab_gla_gate_b, ab_ret_norm_w, ab_gla_norm_w]
    small_m = [m_attn_norm_w, m_ffn_norm_w, m_ple_norm_w, m_final_norm_w, m_ab_gla_gate_b, m_ab_ret_norm_w, m_ab_gla_norm_w]
    small_v = [v_attn_norm_w, v_ffn_norm_w, v_ple_norm_w, v_final_norm_w, v_ab_gla_gate_b, v_ab_ret_norm_w, v_ab_gla_norm_w]
    sizes = [int(np.prod(a.shape)) for a in small_w]
    n_gu = GLA_GATE_RANK * GLA_QK
    n_small = _round_up(sum(sizes) + n_gu, LANE)
    pack = lambda parts: _pad_to(jnp.concatenate([a.reshape(-1) for a in parts]), 0, n_small)[None, :]
    small_part = pack(small_grads + [d_gu[:GLA_GATE_RANK]])

    big_w = dict(ab_w_in=(ab_w_in, m_ab_w_in, v_ab_w_in), ab_w_out=(ab_w_out, m_ab_w_out, v_ab_w_out),
                 c_w_qkv=(c_w_qkv, m_c_w_qkv, v_c_w_qkv), c_w_out=(c_w_out, m_c_w_out, v_c_w_out),
                 ffn_w_gate=(wg_t, tr_(m_ffn_w_gate), tr_(v_ffn_w_gate)),
                 ffn_w_up=(wu_t, tr_(m_ffn_w_up), tr_(v_ffn_w_up)),
                 ffn_w_down=(ffn_w_down, m_ffn_w_down, v_ffn_w_down), ple_w_proj=(ple_w_proj, m_ple_w_proj, v_ple_w_proj),
                 ple_w_gate=(ple_w_gate, m_ple_w_gate, v_ple_w_gate))
    scatter_start("scatter_last", [], deps=(dx,))
    results, last = {}, dx
    for gi, (keys, handle) in enumerate(scatters):
        (arrived,), _, _ = exchange_call(f"scatter_wait{gi}", [("chip_sums", handle)], [], deps=(last,))
        for (n, l), (own, land) in zip(keys, arrived):
            results[n] = adamw(f"adamw_{n}{l}", *big_w[n], l, land, own, prev=results.get(n))
            last = results[n][0]
    for n in ("ffn_w_gate", "ffn_w_up"):
        results[n] = [tr_(a) for a in results[n]]
    small_parts = gather_small("gather_small", small_part, deps=(last,)).reshape(N_DEV, n_small)

    gu_off = sum(sizes)
    own_cols = lambda a: lax.dynamic_slice_in_dim(a.reshape(GLA_GATE_RANK, GLA_QK), me * gu_cols, gu_cols, axis=1)
    small_res = adamw_small("adamw_small", pack(small_w + [jnp.zeros((n_gu,), F32)]),
                            pack(small_m + [jnp.zeros((n_gu,), F32)]), pack(small_v + [jnp.ones((n_gu,), F32)]),
                            small_parts)
    g_gu_full = small_res[0][0, gu_off:gu_off + n_gu]
    g_gu = own_cols(g_gu_full)[None]
    gu_res = adamw_small("adamw_gate_up", *[_pad_to(a.reshape(1, -1), 1, _round_up(a.size, LANE)) for a in
                                            (ab_gla_gate_up, m_ab_gla_gate_up, v_ab_gla_gate_up)],
                         jnp.concatenate([_pad_to(g_gu.reshape(1, -1), 1, _round_up(g_gu.size, LANE)),
                                          jnp.zeros((N_DEV - 1, _round_up(g_gu.size, LANE)), F32)], axis=0))
    for k in range(4):
        off = 0
        for n, a, sz in zip(small_names, small_w, sizes):
            results.setdefault(n, [None] * 4)[k] = small_res[k][0, off:off + sz].reshape(a.shape)
            off += sz
        results.setdefault("ab_gla_gate_up", [None] * 4)[k] = gu_res[k][0, :g_gu.size].reshape(ab_gla_gate_up.shape)

    order = ["attn_norm_w", "ffn_norm_w", "ple_norm_w", "final_norm_w", "ab_w_in", "ab_gla_gate_up", "ab_gla_gate_b",
             "ab_ret_norm_w", "ab_gla_norm_w", "ab_w_out", "c_w_qkv", "c_w_out", "ffn_w_gate", "ffn_w_up", "ffn_w_down",
             "ple_w_proj", "ple_w_gate"]
    return (loss, dx[None], *[results[n][0] for n in order], *[results[n][1] for n in order],
            *[results[n][2] for n in order], *[results[n][3] for n in order])
```

```python
import math

import numpy as np
import jax
import jax.numpy as jnp
from jax import lax
from jax.experimental import pallas as pl
from jax.experimental.pallas import tpu as pltpu

F32 = jnp.float32
BF16 = jnp.bfloat16
HIGHEST = lax.Precision.HIGHEST

N_DEV = 8
VMEM_LIMIT_BYTES = 48 * 1024 * 1024
LANE = 128
NORM_EPS = 1e-6

RET_HEADS, RET_DK, RET_DV = 4, 256, 256
RET_THETA_BASE = 10000.0
GLA_HEADS, GLA_DK, GLA_DV = 4, 128, 256
GLA_GATE_RANK = 16
GLA_GATE_NORM = 16.0
CHUNK = 64
ATT_HEADS = 16
DILATED_BRANCHES = ((128, 1), (512, 4), (2048, 16))
BLK = 256

ADAM_LR, ADAM_B1, ADAM_B2, ADAM_EPS, ADAM_WD, ADAM_STEP = 0.001, 0.9, 0.999, 1e-08, 0.01, 10

RET_QK = RET_HEADS * RET_DK
RET_V = RET_HEADS * RET_DV
GLA_QK = GLA_HEADS * GLA_DK
GLA_V = GLA_HEADS * GLA_DV
OFF_RQ, OFF_RK, OFF_RV, OFF_RG = 0, RET_QK, 2 * RET_QK, 2 * RET_QK + RET_V
OFF_GQ = OFF_RG + RET_V
OFF_GK = OFF_GQ + GLA_QK
OFF_GV = OFF_GK + GLA_QK
OFF_GG = OFF_GV + GLA_V
OFF_LR = OFF_GG + GLA_V


def _params(*sem):
    return pltpu.CompilerParams(dimension_semantics=sem or None, vmem_limit_bytes=VMEM_LIMIT_BYTES)


def _pick(n, cands):
    for c in cands:
        if n % c == 0:
            return c
    raise ValueError(f"no tile for {n} in {cands}")


_NN = (((1,), (0,)), ((), ()))
_NT = (((1,), (1,)), ((), ()))
_TN = (((0,), (0,)), ((), ()))
_ANY = pl.BlockSpec(memory_space=pl.ANY)
MAX_CONTRACT = 2048
_TILES = (1024, 768, 512, 256, 128)


def _mm_call(name, dims, grid, in_specs, out_spec, out_shape, args, deps=()):
    steps = grid[2]
    assert steps == 1 or out_shape.dtype == F32

    def body(a_ref, b_ref, *rest):
        o_ref = rest[len(deps)]
        part = lax.dot_general(a_ref[...].astype(BF16), b_ref[...].astype(BF16), dims, preferred_element_type=F32)
        if steps == 1:
            o_ref[...] = part.astype(o_ref.dtype)
        else:
            _accumulate(o_ref, part, pl.program_id(2) == 0)

    return pl.pallas_call(
        body, name=name, grid=grid, in_specs=list(in_specs) + [_ANY] * len(deps), out_specs=out_spec,
        out_shape=out_shape, compiler_params=_params("parallel", "parallel", "arbitrary"))(*args, *deps)


def mm_nn(name, a, w, l, out_dtype, deps=()):
    _, J, K, n = w.shape
    M = a.shape[0]
    tm, tn, tk = _pick(M, _TILES), _pick(n, _TILES), _pick(K, (MAX_CONTRACT,) + _TILES)
    nt = n // tn
    return _mm_call(
        name, _NN, (M // tm, J * nt, K // tk),
        [pl.BlockSpec((tm, tk), lambda i, j, k: (i, k)),
         pl.BlockSpec((None, None, tk, tn), lambda i, j, k: (l, j // nt, k, j % nt))],
        pl.BlockSpec((tm, tn), lambda i, j, k: (i, j)),
        jax.ShapeDtypeStruct((M, J * n), out_dtype), (a, w), deps)


def mm_nt(name, a, w, l, out_dtype, deps=()):
    _, J, K, n = w.shape
    M = a.shape[0]
    tm, tq, tc = _pick(M, _TILES), _pick(K, _TILES), _pick(n, (MAX_CONTRACT,) + _TILES)
    nc = n // tc
    return _mm_call(
        name, _NT, (M // tm, K // tq, J * nc),
        [pl.BlockSpec((tm, tc), lambda i, q, c: (i, c)),
         pl.BlockSpec((None, None, tq, tc), lambda i, q, c: (l, c // nc, q, c % nc))],
        pl.BlockSpec((tm, tq), lambda i, q, c: (i, q)),
        jax.ShapeDtypeStruct((M, K), out_dtype), (a, w), deps)


def mm_tn(name, x, dy, J, out_dtype, deps=()):
    M, K = x.shape
    n = dy.shape[1] // J
    tp, tn = _pick(K, _TILES), _pick(n, _TILES)
    nt = n // tn
    assert M <= MAX_CONTRACT
    return _mm_call(
        name, _TN, (K // tp, J * nt, 1),
        [pl.BlockSpec((M, tp), lambda i, j, r: (0, i)),
         pl.BlockSpec((M, tn), lambda i, j, r: (0, j))],
        pl.BlockSpec((None, tp, tn), lambda i, j, r: (j // nt, i, j % nt)),
        jax.ShapeDtypeStruct((J, K, n), out_dtype), (x, dy), deps)


def mmt_fwd(name, a, wt, l, out_dtype, n=None, deps=()):
    _, J, rows, K = wt.shape
    n = rows if n is None else n
    M = a.shape[0]
    tm, tn = _pick(M, _TILES), _pick(n, _TILES)
    nt = n // tn
    assert K <= MAX_CONTRACT
    return _mm_call(
        name, _NT, (M // tm, J * nt, 1),
        [pl.BlockSpec((tm, K), lambda i, j, k: (i, 0)),
         pl.BlockSpec((None, None, tn, K), lambda i, j, k: (l, j // nt, j % nt, 0))],
        pl.BlockSpec((tm, tn), lambda i, j, k: (i, j)),
        jax.ShapeDtypeStruct((M, J * n), out_dtype), (a, wt), deps)


def mmt_dx(name, dy, wt, l, out_dtype, n=None, deps=()):
    _, J, rows, K = wt.shape
    n = rows if n is None else n
    M = dy.shape[0]
    tm, tq, tc = _pick(M, _TILES), _pick(K, _TILES), _pick(n, _TILES)
    nc = n // tc
    return _mm_call(
        name, _NN, (M // tm, K // tq, J * nc),
        [pl.BlockSpec((tm, tc), lambda i, q, c: (i, c)),
         pl.BlockSpec((None, None, tc, tq), lambda i, q, c: (l, c // nc, c % nc, q))],
        pl.BlockSpec((tm, tq), lambda i, q, c: (i, q)),
        jax.ShapeDtypeStruct((M, K), out_dtype), (dy, wt), deps)


WIDE_TILE = 512


def _wide_call(name, body, M, K, a, w, a_spec, w_spec, out_dtype, deps):
    def kernel_body(a_ref, w_ref, *rest):
        o_ref = rest[len(deps)]
        o_ref[...] = body(a_ref, w_ref).astype(o_ref.dtype)

    return pl.pallas_call(
        kernel_body, name=name, grid=(M // WIDE_TILE, K // WIDE_TILE),
        in_specs=[a_spec, w_spec] + [_ANY] * len(deps),
        out_specs=pl.BlockSpec((WIDE_TILE, WIDE_TILE), lambda i, q: (i, q)),
        out_shape=jax.ShapeDtypeStruct((M, K), out_dtype),
        compiler_params=_params("parallel", "parallel"))(a, w, *deps)


def mmt_dx_wide(name, dy, wt, out_dtype, n=None, deps=()):
    _, J, rows, K = wt.shape
    n = rows if n is None else n
    M = dy.shape[0]

    def body(dy_ref, w_ref):
        return jnp.dot(dy_ref[...].astype(BF16), w_ref[...].reshape(J * n, WIDE_TILE), preferred_element_type=F32)

    return _wide_call(name, body, M, K, dy, wt,
                      pl.BlockSpec((WIDE_TILE, J * n), lambda i, q: (i, 0)),
                      pl.BlockSpec((None, J, n, WIDE_TILE), lambda i, q: (0, 0, 0, q)), out_dtype, deps)


def mm_nt_wide(name, a, w, out_dtype, deps=()):
    _, J, K, n = w.shape
    M = a.shape[0]

    def body(a_ref, w_ref):
        acc = None
        for j in range(J):
            part = lax.dot_general(a_ref[:, j * n:(j + 1) * n].astype(BF16), w_ref[j], _NT, preferred_element_type=F32)
            acc = part if acc is None else acc + part
        return acc

    return _wide_call(name, body, M, K, a, w,
                      pl.BlockSpec((WIDE_TILE, J * n), lambda i, q: (i, 0)),
                      pl.BlockSpec((None, J, WIDE_TILE, n), lambda i, q: (0, 0, q, 0)), out_dtype, deps)


def mmt_dw(name, dy, x, J, out_dtype, deps=()):
    M, K = x.shape
    n = dy.shape[1] // J
    tn, tp = _pick(n, _TILES), _pick(K, _TILES)
    nt = n // tn
    assert M <= MAX_CONTRACT
    return _mm_call(
        name, _TN, (J * nt, K // tp, 1),
        [pl.BlockSpec((M, tn), lambda j, i, r: (0, j)),
         pl.BlockSpec((M, tp), lambda j, i, r: (0, i))],
        pl.BlockSpec((None, tn, tp), lambda j, i, r: (j // nt, j % nt, i)),
        jax.ShapeDtypeStruct((J, n, K), out_dtype), (dy, x), deps)


def ffn_gate_up(name, a, wg, wu):
    _, J, n, K = wg.shape
    M = a.shape[0]
    tm, tn = _pick(M, _TILES), _pick(n, _TILES)
    nt = n // tn
    assert K <= MAX_CONTRACT

    def body(a_ref, wg_ref, wu_ref, g_ref, u_ref, act_ref):
        x = a_ref[...]
        g = lax.dot_general(x, wg_ref[...], _NT, preferred_element_type=F32)
        u = lax.dot_general(x, wu_ref[...], _NT, preferred_element_type=F32)
        g_ref[...] = g.astype(g_ref.dtype)
        u_ref[...] = u.astype(u_ref.dtype)
        act_ref[...] = (_silu_and_grad(g)[0] * u).astype(act_ref.dtype)

    w_spec = pl.BlockSpec((None, None, tn, K), lambda i, j: (0, j // nt, j % nt, 0))
    out = pl.BlockSpec((tm, tn), lambda i, j: (i, j))
    return pl.pallas_call(
        body, name=name, grid=(M // tm, J * nt),
        in_specs=[pl.BlockSpec((tm, K), lambda i, j: (i, 0)), w_spec, w_spec],
        out_specs=[out] * 3, out_shape=[jax.ShapeDtypeStruct((M, J * n), BF16)] * 3,
        compiler_params=_params("parallel", "parallel"))(a, wg, wu)


def ffn_down_bwd(name, dy, wd, g, u, deps=()):
    _, _, K, n = wd.shape
    M = dy.shape[0]
    tm, tq = _pick(M, _TILES), _pick(K, _TILES)
    assert n <= MAX_CONTRACT

    def body(dy_ref, w_ref, g_ref, u_ref, *rest):
        dg_ref, du_ref = rest[len(deps):]
        dact = lax.dot_general(dy_ref[...], w_ref[...], _NT, preferred_element_type=F32)
        silu, dsilu = _silu_and_grad(g_ref[...].astype(F32))
        dg_ref[...] = (dact * u_ref[...].astype(F32) * dsilu).astype(dg_ref.dtype)
        du_ref[...] = (dact * silu).astype(du_ref.dtype)

    blk = pl.BlockSpec((tm, tq), lambda i, q: (i, q))
    return pl.pallas_call(
        body, name=name, grid=(M // tm, K // tq),
        in_specs=[pl.BlockSpec((tm, n), lambda i, q: (i, 0)),
                  pl.BlockSpec((None, None, tq, n), lambda i, q: (0, 0, q, 0)), blk, blk] + [_ANY] * len(deps),
        out_specs=[blk, blk], out_shape=[jax.ShapeDtypeStruct((M, K), BF16)] * 2,
        compiler_params=_params("parallel", "parallel"))(dy, wd, g, u, *deps)


def rowwise(name, fn, rows, ins, outs, tr=256, deps=()):
    widest = max([s[1].shape[1] if s[0] != "col" else s[3] for s in ins] + [s[1] for s in outs])
    tr = min(tr if widest <= 2048 else tr // 2, rows)
    in_specs, args = [], []
    for spec in ins:
        kind, a = spec[0], spec[1]
        if kind == "row":
            in_specs.append(pl.BlockSpec((tr, a.shape[1]), lambda i: (i, 0)))
        elif kind == "col":
            cb, width = spec[2], spec[3]
            in_specs.append(pl.BlockSpec((tr, width), lambda i, cb=cb: (i, cb)))
        else:
            in_specs.append(pl.BlockSpec(a.shape, lambda i: (0, 0)))
        args.append(a)
    out_specs, out_shapes = [], []
    for spec in outs:
        if spec[0] == "row":
            out_specs.append(pl.BlockSpec((tr, spec[1]), lambda i: (i, 0)))
            out_shapes.append(jax.ShapeDtypeStruct((rows, spec[1]), spec[2]))
        else:
            out_specs.append(pl.BlockSpec((1, spec[1]), lambda i: (0, 0)))
            out_shapes.append(jax.ShapeDtypeStruct((1, spec[1]), F32))
    n_in = len(ins)

    def body(*refs):
        vals = fn(*[r[...] for r in refs[:n_in]])
        first = pl.program_id(0) == 0
        for r, v, spec in zip(refs[n_in + len(deps):], vals, outs):
            if spec[0] == "row":
                r[...] = v.astype(r.dtype)
            else:
                _accumulate(r, v, first)

    return pl.pallas_call(body, name=name, grid=(rows // tr,), in_specs=in_specs + [_ANY] * len(deps),
                          out_specs=out_specs, out_shape=out_shapes,
                          compiler_params=_params("arbitrary"))(*args, *deps)


def _accumulate(ref, v, first):
    @pl.when(first)
    def _():
        ref[...] = v

    @pl.when(jnp.logical_not(first))
    def _():
        ref[...] += v


def _rms(x, w):
    r = lax.rsqrt(jnp.mean(x * x, axis=-1, keepdims=True) + NORM_EPS)
    return x * r * w


def _rms_bwd(x, w, dy):
    r = lax.rsqrt(jnp.mean(x * x, axis=-1, keepdims=True) + NORM_EPS)
    g = dy * w
    dx = r * (g - x * (r * r) * jnp.mean(g * x, axis=-1, keepdims=True))
    dw = jnp.sum(dy * x * r, axis=0, keepdims=True)
    return dx, dw


def _sigmoid(x):
    return 1.0 / (1.0 + jnp.exp(-x))


def _silu_and_grad(g):
    s = _sigmoid(g)
    return g * s, s * (1.0 + g * (1.0 - s))


def _swap_pairs(x):
    n = x.shape[-1]
    lane = lax.broadcasted_iota(jnp.int32, x.shape, x.ndim - 1)
    return jnp.where((lane & 1) == 0, pltpu.roll(x, n - 1, x.ndim - 1), pltpu.roll(x, 1, x.ndim - 1))


def _rot(x, cosf, sins):
    return x * cosf + _swap_pairs(x) * sins


def _unrot(d, cosf, sins):
    return d * cosf + _swap_pairs(d * sins)


def _ret_log_gamma(h):
    vals = [math.log1p(-2.0 ** (-5.0 - i)) for i in range(RET_HEADS)]
    out = jnp.float32(vals[RET_HEADS - 1])
    for i in range(RET_HEADS - 2, -1, -1):
        out = jnp.where(h == i, jnp.float32(vals[i]), out)
    return out


def _fill_decays(dec_ref, lg):
    ri = lax.broadcasted_iota(jnp.int32, (BLK, BLK), 0)
    ci = lax.broadcasted_iota(jnp.int32, (BLK, BLK), 1)
    for d in range(dec_ref.shape[0]):
        dt = d * BLK + ri - ci
        dec_ref[d] = jnp.where(dt >= 0, jnp.exp(jnp.maximum(dt, 0).astype(F32) * lg), 0.0)


def _decay_row(dec_ref, qi):
    return jnp.concatenate([dec_ref[qi - kb] for kb in range(qi + 1)], axis=1)


def _once(block_shape, index_map):
    return pl.BlockSpec(block_shape, index_map, pipeline_mode=pl.Buffered(1))


def _dot(a, b):
    return jnp.dot(a.astype(BF16), b.astype(BF16), preferred_element_type=F32)


def _dot_nt(a, b):
    return lax.dot_general(a.astype(BF16), b.astype(BF16), _NT, preferred_element_type=F32)


def _dot_tn(a, b):
    return lax.dot_general(a.astype(BF16), b.astype(BF16), _TN, preferred_element_type=F32)


def retention_fwd(name, z, cosf, sins, width_out):
    T = z.shape[0]
    nq = T // BLK
    scale = RET_DK ** -0.5

    def body(q_ref, k_ref, v_ref, cos_ref, sin_ref, o_ref, krot, vb, dec_ref):
        _fill_decays(dec_ref, _ret_log_gamma(pl.program_id(0)))
        krot[...] = (_rot(k_ref[...], cos_ref[...], sin_ref[...]) * scale).astype(BF16)
        vb[...] = v_ref[...].astype(BF16)
        for qi in range(nq):
            rows, n = slice(qi * BLK, (qi + 1) * BLK), (qi + 1) * BLK
            q = _rot(q_ref[rows, :], cos_ref[rows, :], sin_ref[rows, :])
            s = _dot_nt(q, krot[0:n, :]) * _decay_row(dec_ref, qi)
            o_ref[rows, :] = _dot(s, vb[0:n, :])

    return pl.pallas_call(
        body, name=name, grid=(RET_HEADS,),
        in_specs=[pl.BlockSpec((T, RET_DK), lambda h: (0, OFF_RQ // RET_DK + h)),
                  pl.BlockSpec((T, RET_DK), lambda h: (0, OFF_RK // RET_DK + h)),
                  pl.BlockSpec((T, RET_DV), lambda h: (0, OFF_RV // RET_DV + h)),
                  _once((T, RET_DK), lambda h: (0, 0)), _once((T, RET_DK), lambda h: (0, 0))],
        out_specs=pl.BlockSpec((T, RET_DV), lambda h: (0, h)),
        out_shape=jax.ShapeDtypeStruct((T, width_out), F32),
        scratch_shapes=[pltpu.VMEM((T, RET_DK), BF16), pltpu.VMEM((T, RET_DV), BF16),
                        pltpu.VMEM((nq, BLK, BLK), F32)],
        compiler_params=_params("arbitrary"))(z, z, z, cosf, sins)


def retention_bwd(name, z, cosf, sins, do):
    T = z.shape[0]
    nq = T // BLK
    scale = RET_DK ** -0.5

    def body(q_ref, k_ref, v_ref, cos_ref, sin_ref, do_ref, dq_ref, dk_ref, dv_ref, krot, vb, dk_acc, dv_acc, dec_ref):
        _fill_decays(dec_ref, _ret_log_gamma(pl.program_id(0)))
        krot[...] = (_rot(k_ref[...], cos_ref[...], sin_ref[...]) * scale).astype(BF16)
        vb[...] = v_ref[...].astype(BF16)
        dk_acc[...] = jnp.zeros_like(dk_acc)
        dv_acc[...] = jnp.zeros_like(dv_acc)
        for qi in range(nq):
            rows, n = slice(qi * BLK, (qi + 1) * BLK), (qi + 1) * BLK
            cos_q, sin_q = cos_ref[rows, :], sin_ref[rows, :]
            q = _rot(q_ref[rows, :], cos_q, sin_q).astype(BF16)
            dout = do_ref[rows, :].astype(BF16)
            kk, vv, dec = krot[0:n, :], vb[0:n, :], _decay_row(dec_ref, qi)
            p = (_dot_nt(q, kk) * dec).astype(BF16)
            ds = (_dot_nt(dout, vv) * dec).astype(BF16)
            dq_ref[rows, :] = _unrot(_dot(ds, kk), cos_q, sin_q).astype(dq_ref.dtype)
            dk_acc[0:n, :] += _dot_tn(ds, q)
            dv_acc[0:n, :] += _dot_tn(p, dout)
        dk_ref[...] = (_unrot(dk_acc[...], cos_ref[...], sin_ref[...]) * scale).astype(dk_ref.dtype)
        dv_ref[...] = dv_acc[...].astype(dv_ref.dtype)

    head = lambda h: (0, h)
    return pl.pallas_call(
        body, name=name, grid=(RET_HEADS,),
        in_specs=[pl.BlockSpec((T, RET_DK), lambda h: (0, OFF_RQ // RET_DK + h)),
                  pl.BlockSpec((T, RET_DK), lambda h: (0, OFF_RK // RET_DK + h)),
                  pl.BlockSpec((T, RET_DV), lambda h: (0, OFF_RV // RET_DV + h)),
                  _once((T, RET_DK), lambda h: (0, 0)), _once((T, RET_DK), lambda h: (0, 0)),
                  pl.BlockSpec((T, RET_DV), head)],
        out_specs=[pl.BlockSpec((T, RET_DK), head), pl.BlockSpec((T, RET_DK), head), pl.BlockSpec((T, RET_DV), head)],
        out_shape=[jax.ShapeDtypeStruct((T, RET_QK), BF16), jax.ShapeDtypeStruct((T, RET_QK), BF16),
                   jax.ShapeDtypeStruct((T, RET_V), BF16)],
        scratch_shapes=[pltpu.VMEM((T, RET_DK), BF16), pltpu.VMEM((T, RET_DV), BF16),
                        pltpu.VMEM((T, RET_DK), F32), pltpu.VMEM((T, RET_DV), F32),
                        pltpu.VMEM((nq, BLK, BLK), F32)],
        compiler_params=_params("arbitrary"))(z, z, z, cosf, sins, do)


GLA_PAIR = 2


def _gla_chunk(q_ref, k_ref, v_ref, glr_ref, gu, gb, rows, hh, trilf):
    ck = slice(hh * GLA_DK, (hh + 1) * GLA_DK)
    zg = _dot(glr_ref[rows, :], gu[:, ck]) + gb[:, ck]
    la = (jnp.minimum(zg, 0.0) - jnp.log(1.0 + jnp.exp(-jnp.abs(zg)))) * (1.0 / GLA_GATE_NORM)
    cum = jnp.dot(trilf, la, precision=HIGHEST, preferred_element_type=F32)
    last = jnp.sum(la, axis=0, keepdims=True)
    ecum = jnp.exp(cum)
    k = k_ref[rows, ck]
    qt = q_ref[rows, ck] * (GLA_DK ** -0.5) * ecum
    kt = k * jnp.exp(-cum)
    kh = k * jnp.exp(last - cum)
    return zg, cum, last, ecum, qt, kt, kh, v_ref[rows, hh * GLA_DV:(hh + 1) * GLA_DV].astype(BF16)


def _state_decay(last):
    e = jnp.exp(jnp.broadcast_to(last, (GLA_DK, GLA_DK)).T)
    return jnp.concatenate([e] * (GLA_DV // GLA_DK), axis=1)


def _gla_specs(T):
    wk, wv = GLA_PAIR * GLA_DK, GLA_PAIR * GLA_DV
    return [_once((T, wk), lambda h: (0, OFF_GQ // wk + h)),
            _once((T, wk), lambda h: (0, OFF_GK // wk + h)),
            _once((T, wv), lambda h: (0, OFF_GV // wv + h)),
            _once((T, LANE), lambda h: (0, 0)),
            pl.BlockSpec((LANE, wk), lambda h: (0, h)),
            pl.BlockSpec((1, wk), lambda h: (0, h))]


def gla_fwd(name, z, glr, gu, gb, o_prev):
    T = z.shape[0]
    nc = T // CHUNK
    wv = GLA_PAIR * GLA_DV

    def body(q_ref, k_ref, v_ref, glr_ref, gu_ref, gb_ref, prev_ref, o_ref, S):
        del prev_ref
        gu_b, gb_v = gu_ref[...].astype(BF16), gb_ref[...]
        ri = lax.broadcasted_iota(jnp.int32, (CHUNK, CHUNK), 0)
        ci = lax.broadcasted_iota(jnp.int32, (CHUNK, CHUNK), 1)
        tril = ri >= ci
        trilf = tril.astype(F32)
        S[...] = jnp.zeros_like(S)

        def step(c, carry):
            rows = pl.ds(pl.multiple_of(c * CHUNK, CHUNK), CHUNK)
            for hh in range(GLA_PAIR):
                _, _, last, _, qt, kt, kh, v = _gla_chunk(q_ref, k_ref, v_ref, glr_ref, gu_b, gb_v, rows, hh, trilf)
                a = jnp.where(tril, _dot_nt(qt, kt), 0.0)
                s_prev = S[hh]
                o_ref[rows, hh * GLA_DV:(hh + 1) * GLA_DV] = _dot(a, v) + _dot(qt, s_prev)
                S[hh] = s_prev * _state_decay(last) + _dot_tn(kh, v)
            return carry

        lax.fori_loop(0, nc, step, 0)

    n_in = 6
    return pl.pallas_call(
        body, name=name, grid=(GLA_HEADS // GLA_PAIR,),
        in_specs=_gla_specs(T) + [pl.BlockSpec(memory_space=pl.ANY)],
        out_specs=pl.BlockSpec((T, wv), lambda h: (0, RET_V // wv + h)),
        out_shape=jax.ShapeDtypeStruct(o_prev.shape, F32),
        scratch_shapes=[pltpu.VMEM((GLA_PAIR, GLA_DK, GLA_DV), F32)],
        input_output_aliases={n_in: 0},
        compiler_params=_params("arbitrary"))(z, z, z, glr, gu, gb, o_prev)


def gla_bwd(name, z, glr, gu, gb, do):
    T = z.shape[0]
    nc = T // CHUNK

    def body(q_ref, k_ref, v_ref, glr_ref, gu_ref, gb_ref, do_ref,
             dq_ref, dk_ref, dv_ref, dglr_ref, dgu_ref, dgb_ref, s_all, dS):
        gu_b, gb_v = gu_ref[...].astype(BF16), gb_ref[...]
        ri = lax.broadcasted_iota(jnp.int32, (CHUNK, CHUNK), 0)
        ci = lax.broadcasted_iota(jnp.int32, (CHUNK, CHUNK), 1)
        tril = ri >= ci
        trilf = tril.astype(F32)
        triuf = (ri <= ci).astype(F32)
        last_row = lax.broadcasted_iota(jnp.int32, (CHUNK, GLA_DK), 0) == CHUNK - 1
        ones8 = jnp.ones((8, GLA_DV), F32)

        def fstep(c, carry):
            rows = pl.ds(pl.multiple_of(c * CHUNK, CHUNK), CHUNK)
            for hh in range(GLA_PAIR):
                s_prev = dS[hh]
                s_all[hh, c] = s_prev
                _, _, last, _, _, _, kh, v = _gla_chunk(q_ref, k_ref, v_ref, glr_ref, gu_b, gb_v, rows, hh, trilf)
                dS[hh] = s_prev * _state_decay(last) + _dot_tn(kh, v)
            return carry

        dS[...] = jnp.zeros_like(dS)
        lax.fori_loop(0, nc, fstep, 0)
        dS[...] = jnp.zeros_like(dS)
        dgu_ref[...] = jnp.zeros_like(dgu_ref)
        dgb_ref[...] = jnp.zeros_like(dgb_ref)

        def bstep(i, carry):
            c = nc - 1 - i
            rows = pl.ds(pl.multiple_of(c * CHUNK, CHUNK), CHUNK)
            glr_c = glr_ref[rows, :]
            for hh in range(GLA_PAIR):
                ck, cv = slice(hh * GLA_DK, (hh + 1) * GLA_DK), slice(hh * GLA_DV, (hh + 1) * GLA_DV)
                zg, cum, last, ecum, qt, kt, kh, v = _gla_chunk(q_ref, k_ref, v_ref, glr_ref, gu_b, gb_v, rows, hh, trilf)
                a = jnp.where(tril, _dot_nt(qt, kt), 0.0)
                s_prev, ds_new = s_all[hh, c], dS[hh]
                dout = do_ref[rows, cv].astype(BF16)
                dv_ref[rows, cv] = (_dot_tn(a, dout) + _dot(kh, ds_new)).astype(dv_ref.dtype)
                da = jnp.where(tril, _dot_nt(dout, v), 0.0)
                dqt = _dot(da, kt) + _dot_nt(dout, s_prev)
                dkt = _dot_tn(da, qt)
                dkh = _dot_nt(v, ds_new)
                dS[hh] = ds_new * _state_decay(last) + _dot_tn(qt, dout)
                dq_ref[rows, ck] = (dqt * ecum * (GLA_DK ** -0.5)).astype(dq_ref.dtype)
                dk_ref[rows, ck] = (dkt * jnp.exp(-cum) + dkh * jnp.exp(last - cum)).astype(dk_ref.dtype)
                dkh_kh = dkh * kh
                dcum = dqt * qt - dkt * kt - dkh_kh
                rs = lax.dot_general(ones8, ds_new * s_prev, _NT, precision=HIGHEST, preferred_element_type=F32)
                dlast = (jnp.sum(dkh_kh, axis=0, keepdims=True)
                         + jnp.exp(last) * (jnp.sum(rs, axis=0, keepdims=True) * 0.125))
                dcum = dcum + jnp.where(last_row, dlast, 0.0)
                dla = jnp.dot(triuf, dcum, precision=HIGHEST, preferred_element_type=F32)
                dzg = dla * (1.0 / GLA_GATE_NORM) * _sigmoid(-zg)
                dglr_ref[hh, rows, :] = _dot_nt(dzg, gu_b[:, ck])
                dgu_ref[:, ck] += _dot_tn(glr_c, dzg)
                dgb_ref[:, ck] += jnp.sum(dzg, axis=0, keepdims=True)
            return carry

        lax.fori_loop(0, nc, bstep, 0)

    wk, wv = GLA_PAIR * GLA_DK, GLA_PAIR * GLA_DV
    return pl.pallas_call(
        body, name=name, grid=(GLA_HEADS // GLA_PAIR,),
        in_specs=_gla_specs(T) + [_once((T, wv), lambda h: (0, RET_V // wv + h))],
        out_specs=[pl.BlockSpec((T, wk), lambda h: (0, h)), pl.BlockSpec((T, wk), lambda h: (0, h)),
                   pl.BlockSpec((T, wv), lambda h: (0, h)),
                   pl.BlockSpec((GLA_PAIR, T, LANE), lambda h: (h, 0, 0)),
                   pl.BlockSpec((LANE, wk), lambda h: (0, h)), pl.BlockSpec((1, wk), lambda h: (0, h))],
        out_shape=[jax.ShapeDtypeStruct((T, GLA_QK), BF16), jax.ShapeDtypeStruct((T, GLA_QK), BF16),
                   jax.ShapeDtypeStruct((T, GLA_V), BF16), jax.ShapeDtypeStruct((GLA_HEADS, T, LANE), F32),
                   jax.ShapeDtypeStruct((LANE, GLA_QK), F32), jax.ShapeDtypeStruct((1, GLA_QK), F32)],
        scratch_shapes=[pltpu.VMEM((GLA_PAIR, nc, GLA_DK, GLA_DV), F32), pltpu.VMEM((GLA_PAIR, GLA_DK, GLA_DV), F32)],
        compiler_params=_params("arbitrary"))(z, z, z, glr, gu, gb, do)


HN_HEADS = RET_HEADS + GLA_HEADS
HN_W = RET_DV


def _gate_col(h):
    return jnp.where(h < RET_HEADS, OFF_RG // HN_W + h, OFF_GG // HN_W + h - RET_HEADS)


def headnorm_fwd(name, oraw, z, w):
    T = oraw.shape[0]
    tr = _pick(T, _TILES)

    def body(o_ref, g_ref, w_ref, y_ref):
        y_ref[...] = (_rms(o_ref[...], w_ref[...]) * _silu_and_grad(g_ref[...])[0]).astype(y_ref.dtype)

    return pl.pallas_call(
        body, name=name, grid=(HN_HEADS, T // tr),
        in_specs=[pl.BlockSpec((tr, HN_W), lambda h, i: (i, h)),
                  pl.BlockSpec((tr, HN_W), lambda h, i: (i, _gate_col(h))),
                  pl.BlockSpec((1, HN_W), lambda h, i: (0, h))],
        out_specs=pl.BlockSpec((tr, HN_W), lambda h, i: (i, h)),
        out_shape=jax.ShapeDtypeStruct((T, HN_HEADS * HN_W), BF16),
        compiler_params=_params("arbitrary", "arbitrary"))(oraw, z, w)


def headnorm_bwd(name, oraw, z, w, dy):
    T = oraw.shape[0]
    tr = _pick(T, _TILES)

    def body(o_ref, g_ref, w_ref, dy_ref, do_ref, dg_ref, dw_ref):
        o, wv, dyv = o_ref[...], w_ref[...], dy_ref[...].astype(F32)
        silu, dsilu = _silu_and_grad(g_ref[...])
        n = _rms(o, wv)
        dg_ref[...] = (dyv * n * dsilu).astype(dg_ref.dtype)
        dx, dw = _rms_bwd(o, wv, dyv * silu)
        do_ref[...] = dx
        _accumulate(dw_ref, dw, pl.program_id(1) == 0)

    blk = pl.BlockSpec((tr, HN_W), lambda h, i: (i, h))
    return pl.pallas_call(
        body, name=name, grid=(HN_HEADS, T // tr),
        in_specs=[blk, pl.BlockSpec((tr, HN_W), lambda h, i: (i, _gate_col(h))),
                  pl.BlockSpec((1, HN_W), lambda h, i: (0, h)), blk],
        out_specs=[blk, blk, pl.BlockSpec((1, HN_W), lambda h, i: (0, h))],
        out_shape=[jax.ShapeDtypeStruct((T, HN_HEADS * HN_W), F32),
                   jax.ShapeDtypeStruct((T, HN_HEADS * HN_W), BF16),
                   jax.ShapeDtypeStruct((1, HN_HEADS * HN_W), F32)],
        compiler_params=_params("arbitrary", "arbitrary"))(oraw, z, w, dy)


N_MASKS = 4


def _check_mask_classes(T):
    for window, dilation in DILATED_BRANCHES[:-1]:
        assert window < (N_MASKS - 1) * BLK - (BLK - 1) and BLK % dilation == 0
    assert DILATED_BRANCHES[-1][0] >= T and BLK % DILATED_BRANCHES[-1][1] == 0


def _fill_masks(mult_ref, bias_ref):
    ri = lax.broadcasted_iota(jnp.int32, (BLK, BLK), 0)
    ci = lax.broadcasted_iota(jnp.int32, (BLK, BLK), 1)
    for d in range(N_MASKS):
        dt = d * BLK + ri - ci
        mult = jnp.zeros((BLK, BLK), F32)
        for window, dilation in DILATED_BRANCHES:
            hit = (dt >= 0) & (dt <= window) & ((dt & (dilation - 1)) == 0)
            mult = mult + hit.astype(F32)
        mult_ref[d] = mult
        bias_ref[d] = jnp.where(mult > 0, 0.0, -1e30)


def _mask_row(ref, qi):
    return jnp.concatenate([ref[min(qi - kb, N_MASKS - 1)] for kb in range(qi + 1)], axis=1)


def attn_fwd(name, qkv):
    T = qkv.shape[0]
    D = qkv.shape[1] // 3
    dh = D // ATT_HEADS
    nq = T // BLK
    scale = dh ** -0.5

    _check_mask_classes(T)

    def body(q_ref, k_ref, v_ref, o_ref, lse_ref, mult_ref, bias_ref):
        @pl.when(pl.program_id(0) == 0)
        def _():
            _fill_masks(mult_ref, bias_ref)

        for qi in range(nq):
            rows, n = slice(qi * BLK, (qi + 1) * BLK), (qi + 1) * BLK
            s = (_dot_nt(q_ref[rows, :], k_ref[0:n, :]) * scale
                 + _mask_row(bias_ref, qi))
            m = jnp.max(s, axis=-1, keepdims=True)
            p = _mask_row(mult_ref, qi) * jnp.exp(s - m)
            l = jnp.sum(p, axis=-1, keepdims=True)
            o_ref[rows, :] = (_dot(p, v_ref[0:n, :]) / l).astype(o_ref.dtype)
            lse_ref[rows, :] = jnp.broadcast_to(m + jnp.log(l), (BLK, LANE))

    return pl.pallas_call(
        body, name=name, grid=(ATT_HEADS,),
        in_specs=[pl.BlockSpec((T, dh), lambda h: (0, h)),
                  pl.BlockSpec((T, dh), lambda h: (0, ATT_HEADS + h)),
                  pl.BlockSpec((T, dh), lambda h: (0, 2 * ATT_HEADS + h))],
        out_specs=[pl.BlockSpec((T, dh), lambda h: (0, h)),
                   pl.BlockSpec((None, T, LANE), lambda h: (h, 0, 0))],
        out_shape=[jax.ShapeDtypeStruct((T, D), BF16), jax.ShapeDtypeStruct((ATT_HEADS, T, LANE), F32)],
        scratch_shapes=[pltpu.VMEM((N_MASKS, BLK, BLK), F32), pltpu.VMEM((N_MASKS, BLK, BLK), F32)],
        compiler_params=_params("arbitrary"))(qkv, qkv, qkv)


def attn_bwd(name, qkv, o, lse, do):
    T = qkv.shape[0]
    D = qkv.shape[1] // 3
    dh = D // ATT_HEADS
    nq = T // BLK
    scale = dh ** -0.5

    _check_mask_classes(T)

    def body(q_ref, k_ref, v_ref, o_ref, lse_ref, do_ref, dq_ref, dk_ref, dv_ref, dk_acc, dv_acc, mult_ref, bias_ref):
        @pl.when(pl.program_id(0) == 0)
        def _():
            _fill_masks(mult_ref, bias_ref)

        dk_acc[...] = jnp.zeros_like(dk_acc)
        dv_acc[...] = jnp.zeros_like(dv_acc)
        for qi in range(nq):
            rows, n = slice(qi * BLK, (qi + 1) * BLK), (qi + 1) * BLK
            q, dout = q_ref[rows, :], do_ref[rows, :]
            kk, vv = k_ref[0:n, :], v_ref[0:n, :]
            delta = jnp.sum(dout.astype(F32) * o_ref[rows, :].astype(F32), axis=-1, keepdims=True)
            lse = jnp.max(lse_ref[rows, :], axis=-1, keepdims=True)
            s = _dot_nt(q, kk) * scale + _mask_row(bias_ref, qi)
            p = _mask_row(mult_ref, qi) * jnp.exp(s - lse)
            ds = (p * (_dot_nt(dout, vv) - delta) * scale).astype(BF16)
            dq_ref[rows, :] = _dot(ds, kk).astype(dq_ref.dtype)
            dk_acc[0:n, :] += _dot_tn(ds, q)
            dv_acc[0:n, :] += _dot_tn(p, dout)
        dk_ref[...] = dk_acc[...].astype(dk_ref.dtype)
        dv_ref[...] = dv_acc[...].astype(dv_ref.dtype)

    full = pl.BlockSpec((T, dh), lambda h: (0, h))
    return pl.pallas_call(
        body, name=name, grid=(ATT_HEADS,),
        in_specs=[full, pl.BlockSpec((T, dh), lambda h: (0, ATT_HEADS + h)),
                  pl.BlockSpec((T, dh), lambda h: (0, 2 * ATT_HEADS + h)),
                  full, pl.BlockSpec((None, T, LANE), lambda h: (h, 0, 0)), full],
        out_specs=[full, full, full],
        out_shape=[jax.ShapeDtypeStruct((T, D), BF16)] * 3,
        scratch_shapes=[pltpu.VMEM((T, dh), F32), pltpu.VMEM((T, dh), F32),
                        pltpu.VMEM((N_MASKS, BLK, BLK), F32), pltpu.VMEM((N_MASKS, BLK, BLK), F32)],
        compiler_params=_params("arbitrary"))(qkv, qkv, qkv, o, lse, do)


def _mesh_pos():
    mx, my, mc = lax.axis_index("x"), lax.axis_index("y"), lax.axis_index("c")
    return mx, my, mc, 4 * mx + 2 * my + mc


def _peer(k, mx, my, mc):
    px, py, pc = mx ^ (k >> 2), my ^ ((k >> 1) & 1), mc ^ (k & 1)
    return (px, py, pc), 4 * px + 2 * py + pc


_SIBLING = 1
_OTHER_CHIPS = (4, 2, 6)
N_CHIP = N_DEV // 2
_PLANS = {"gather": (2, N_DEV - 1), "to_chips": (2, 1 + len(_OTHER_CHIPS)), "pass_on": (1, len(_OTHER_CHIPS)),
          "halves": (2, N_CHIP), "chip_sums": (2, len(_OTHER_CHIPS))}


def _copies(kind, items, send_sems, recv_sems):
    mx, my, mc, me = _mesh_pos()
    out = []

    def add(n, src, dst, peer):
        out.append(pltpu.make_async_remote_copy(
            src_ref=src, dst_ref=dst, send_sem=send_sems.at[n], recv_sem=recv_sems.at[n],
            device_id=peer, device_id_type=pl.DeviceIdType.MESH))

    per_item = _PLANS[kind][1]
    sibling = _peer(_SIBLING, mx, my, mc)[0]
    for i, refs in enumerate(items):
        n = i * per_item
        if kind == "gather":
            for k in range(1, N_DEV):
                add(n + k - 1, refs[0], refs[1].at[me], _peer(k, mx, my, mc)[0])
        elif kind == "to_chips":
            rows = refs[0].shape[0]
            dst = refs[1].at[me] if rows == refs[1].shape[1] else refs[1].at[me, pl.ds(0, rows)]
            for j, k in enumerate((_SIBLING,) + _OTHER_CHIPS):
                add(n + j, refs[0], dst, _peer(k, mx, my, mc)[0])
        elif kind == "pass_on":
            for j, k in enumerate(_OTHER_CHIPS):
                add(n + j, refs[0].at[me ^ k], refs[0].at[me ^ k], sibling)
        elif kind == "halves":
            for chip in range(N_CHIP):
                add(n + chip, refs[0].at[2 * chip + 1 - mc], refs[1].at[chip], sibling)
        else:
            for j, k in enumerate(_OTHER_CHIPS):
                peer, to = _peer(k, mx, my, mc)
                add(n + j, refs[0].at[to // 2], refs[1].at[me // 2], peer)
    return out


_HBM = pl.BlockSpec(memory_space=pltpu.HBM)
_SEM = pl.BlockSpec(memory_space=pltpu.SEMAPHORE)
_DATAFLOW = pltpu.SideEffectType.DATAFLOW_SIDE_EFFECTING


def exchange_call(name, waits, starts, deps=()):
    bufs, slot_of = [], {}

    def slots(items):
        out = []
        for item in items:
            for b in item:
                if id(b) not in slot_of:
                    slot_of[id(b)] = len(bufs)
                    bufs.append(b)
            out.append(tuple(slot_of[id(b)] for b in item))
        return out

    wait_plan = [(kind, slots(handle[0])) for kind, handle in waits]
    start_plan = [(kind, slots(items)) for kind, items in starts]
    wait_sems = [s for _, handle in waits for s in handle[1:]]
    n_buf, n_ws, n_start = len(bufs), len(wait_sems), len(starts)

    def body(*refs):
        buf_refs, sems_in = refs[:n_buf], refs[n_buf:n_buf + n_ws]
        outs = refs[n_buf + n_ws + len(deps):]
        pick = lambda plan: [tuple(buf_refs[s] for s in item) for item in plan]
        for wi, (kind, plan) in enumerate(wait_plan):
            copies = _copies(kind, pick(plan), sems_in[2 * wi], sems_in[2 * wi + 1])
            for cp in copies:
                cp.wait_send()
            for cp in copies:
                cp.wait_recv()
        for si, (kind, plan) in enumerate(start_plan):
            for cp in _copies(kind, pick(plan), outs[2 * si], outs[2 * si + 1]):
                cp.start()
        outs[-1][...] = jnp.zeros_like(outs[-1])

    hbm_bufs = [pltpu.with_memory_space_constraint(b, pltpu.HBM) for b in bufs]
    sem_shapes = []
    for kind, plan in start_plan:
        sem_shapes += [pltpu.SemaphoreType.DMA((len(plan) * _PLANS[kind][1],))] * 2
    outs = pl.pallas_call(
        body, name=name,
        out_shape=sem_shapes + [pltpu.HBM(b.shape, b.dtype) for b in bufs] + [jax.ShapeDtypeStruct((8, LANE), F32)],
        in_specs=[_HBM] * n_buf + [_SEM] * n_ws + [_ANY] * len(deps),
        out_specs=[_SEM] * (2 * n_start) + [_HBM] * n_buf + [pl.BlockSpec(memory_space=pltpu.VMEM)],
        input_output_aliases={i: 2 * n_start + i for i in range(n_buf)},
        compiler_params=pltpu.CompilerParams(has_side_effects=_DATAFLOW))(*hbm_bufs, *wait_sems, *deps)
    sems, thru, token = outs[:2 * n_start], outs[2 * n_start:-1], outs[-1]
    through = lambda plan: [tuple(thru[s] for s in item) for item in plan]
    waited = [through(plan) for _, plan in wait_plan]
    handles = [(through(plan), sems[2 * si], sems[2 * si + 1]) for si, (_, plan) in enumerate(start_plan)]
    return waited, handles, token


def gather_small(name, a, deps=()):
    def body(a_ref, *rest):
        o_ref, send_sems, recv_sems, local_sem = rest[len(deps):]
        me = _mesh_pos()[3]
        own = pltpu.make_async_copy(a_ref, o_ref.at[me], local_sem)
        own.start()
        copies = _copies("gather", [(a_ref, o_ref)], send_sems, recv_sems)
        for cp in copies:
            cp.start()
        for cp in copies:
            cp.wait_recv()
        for cp in copies:
            cp.wait_send()
        own.wait()

    return pl.pallas_call(
        body, name=name, in_specs=[_ANY] * (1 + len(deps)), out_specs=_ANY,
        out_shape=jax.ShapeDtypeStruct((N_DEV,) + a.shape, a.dtype),
        scratch_shapes=[pltpu.SemaphoreType.DMA((N_DEV - 1,)), pltpu.SemaphoreType.DMA((N_DEV - 1,)),
                        pltpu.SemaphoreType.DMA],
        compiler_params=pltpu.CompilerParams(has_side_effects=True))(a, *deps)


def _adamw_math(w, g, m, v):
    m2 = ADAM_B1 * m + (1.0 - ADAM_B1) * g
    v2 = ADAM_B2 * v + (1.0 - ADAM_B2) * (g * g)
    m_hat = m2 / (1.0 - ADAM_B1 ** ADAM_STEP)
    v_hat = v2 / (1.0 - ADAM_B2 ** ADAM_STEP)
    delta = -ADAM_LR * (m_hat / (jnp.sqrt(v_hat) + ADAM_EPS) + ADAM_WD * w)
    return delta, m2, v2


def chip_sum(name, a, half):
    _, r, c = a.shape
    tr = _pick(r, (256, 128, 64, 32, 16))

    def body(core_ref, a_ref, h_ref, o_ref):
        del core_ref
        o_ref[...] = (a_ref[...].astype(F32) + h_ref[...].astype(F32)).astype(o_ref.dtype)

    blk = pl.BlockSpec((N_CHIP, tr, c), lambda i, core: (0, i, 0))
    grid_spec = pltpu.PrefetchScalarGridSpec(
        num_scalar_prefetch=1, grid=(r // tr,),
        in_specs=[pl.BlockSpec((N_CHIP, None, tr, c), lambda i, core: (0, core[0], i, 0)), blk], out_specs=blk)
    return pl.pallas_call(
        body, name=name, grid_spec=grid_spec, out_shape=jax.ShapeDtypeStruct((N_CHIP, r, c), BF16),
        compiler_params=_params("parallel"))(lax.axis_index("c").astype(jnp.int32).reshape(1),
                                             a.reshape(N_CHIP, 2, r, c), half)


def adamw(name, w, m, v, l, land, own, prev=None):
    L, r, c = w.shape
    cp = land.shape[2]
    tr = _pick(r, (256, 176, 128, 64, 32, 16, 8))

    def body(w_ref, m_ref, v_ref, land_ref, own_ref, *rest):
        g_ref, d_ref, m2_ref, v2_ref = rest[-4:]
        chip = _mesh_pos()[3] // 2
        mine = own_ref[:, pl.ds(0, c)].astype(F32)
        g = None
        for s in range(N_CHIP):
            part = jnp.where(chip == s, mine, land_ref[s, :, pl.ds(0, c)].astype(F32))
            g = part if g is None else g + part
        delta, m2, v2 = _adamw_math(w_ref[...], g, m_ref[...], v_ref[...])
        g_ref[...] = g
        d_ref[...] = delta
        m2_ref[...] = m2
        v2_ref[...] = v2

    blk = pl.BlockSpec((None, tr, c), lambda i: (l, i, 0))
    shape = jax.ShapeDtypeStruct((L, r, c), F32)
    extra = [] if prev is None else list(prev)
    return pl.pallas_call(
        body, name=name, grid=(r // tr,),
        in_specs=[blk, blk, blk, pl.BlockSpec((N_CHIP, tr, cp), lambda i: (0, i, 0)),
                  pl.BlockSpec((None, tr, cp), lambda i: (_mesh_pos()[3] // 2, i, 0))] + [_ANY] * len(extra),
        out_specs=[blk] * 4, out_shape=[shape] * 4,
        input_output_aliases={5 + k: k for k in range(len(extra))},
        compiler_params=_params("parallel"))(w, m, v, land, own, *extra)


def adamw_small(name, w, m, v, parts):
    n = w.shape[1]

    def body(w_ref, m_ref, v_ref, p_ref, g_ref, d_ref, m2_ref, v2_ref):
        g = p_ref[0:1, :]
        for s in range(1, N_DEV):
            g = g + p_ref[s:s + 1, :]
        delta, m2, v2 = _adamw_math(w_ref[...], g, m_ref[...], v_ref[...])
        g_ref[...] = g
        d_ref[...] = delta
        m2_ref[...] = m2
        v2_ref[...] = v2

    shape = jax.ShapeDtypeStruct((1, n), F32)
    return pl.pallas_call(body, name=name, out_shape=[shape] * 4,
                          compiler_params=pltpu.CompilerParams(vmem_limit_bytes=VMEM_LIMIT_BYTES))(w, m, v, parts)


def _rope_tables(positions):
    half = RET_DK // 2
    inv_freq = 1.0 / jnp.power(RET_THETA_BASE, jnp.linspace(0.0, 1.0, half, dtype=F32))
    ang = positions.astype(F32)[:, None] * inv_freq
    cos, sin = jnp.cos(ang), jnp.sin(ang)
    cosf = jnp.repeat(cos, 2, axis=-1)
    sins = jnp.stack([-sin, sin], axis=-1).reshape(cosf.shape)
    return cosf, sins


def _pad_to(a, axis, size):
    pad = [(0, 0)] * a.ndim
    pad[axis] = (0, size - a.shape[axis])
    return jnp.pad(a, pad)


def _round_up(n, m):
    return -(-n // m) * m


def kernel(x, p, positions, attn_norm_w, ffn_norm_w, ple_norm_w, final_norm_w, ab_w_in, ab_gla_gate_up, ab_gla_gate_b, ab_ret_norm_w, ab_gla_norm_w, ab_w_out, c_w_qkv, c_w_out, ffn_w_gate, ffn_w_up, ffn_w_down, ple_w_proj, ple_w_gate, loss_target, m_attn_norm_w, m_ffn_norm_w, m_ple_norm_w, m_final_norm_w, m_ab_w_in, m_ab_gla_gate_up, m_ab_gla_gate_b, m_ab_ret_norm_w, m_ab_gla_norm_w, m_ab_w_out, m_c_w_qkv, m_c_w_out, m_ffn_w_gate, m_ffn_w_up, m_ffn_w_down, m_ple_w_proj, m_ple_w_gate, v_attn_norm_w, v_ffn_norm_w, v_ple_norm_w, v_final_norm_w, v_ab_w_in, v_ab_gla_gate_up, v_ab_gla_gate_b, v_ab_ret_norm_w, v_ab_gla_norm_w, v_ab_w_out, v_c_w_qkv, v_c_w_out, v_ffn_w_gate, v_ffn_w_up, v_ffn_w_down, v_ple_w_proj, v_ple_w_gate):
    T, D = x.shape[1], x.shape[2]
    depth = attn_norm_w.shape[0]
    assert ab_w_in.shape[0] == 1 and c_w_qkv.shape[0] == 1 and depth == 2, "one even and one odd layer"
    me = 4 * lax.axis_index("x") + 2 * lax.axis_index("y") + lax.axis_index("c")
    in_shard = ab_w_in.shape[2]
    in_width = in_shard * N_DEV
    assert in_width == OFF_LR + GLA_GATE_RANK
    fs = ffn_w_gate.shape[2]
    fp = _round_up(fs, LANE)
    gu_cols = ab_gla_gate_up.shape[2]

    bf = lambda a: a.astype(BF16)
    tr_ = lambda a: jnp.swapaxes(a, -1, -2)
    wg_t, wu_t = tr_(ffn_w_gate), tr_(ffn_w_up)
    srcs = {"w_in": bf(tr_(ab_w_in[0])), "w_oab": bf(ab_w_out[0]), "gu": ab_gla_gate_up[0],
            "w_qkv": bf(c_w_qkv[0]), "w_oc": bf(c_w_out[0])}
    for l in range(depth):
        srcs[f"wg{l}"] = bf(wg_t[l])
        srcs[f"wu{l}"] = bf(wu_t[l])
        srcs[f"wd{l}"] = bf(ffn_w_down[l])
        srcs[f"wpg{l}"] = bf(ple_w_gate[l])
        srcs[f"wpp{l}"] = bf(ple_w_proj[l])
    group_keys = [["w_in"], ["gu", "w_oab"], ["wg0", "wu0"], ["wd0", "wpg0", "wpp0"], ["w_qkv", "w_oc"],
                  ["wg1", "wu1"], ["wd1", "wpg1", "wpp1"]]
    G_IN, G_OUT, G_QKV = 0, 1, 4
    g_ffn = lambda layer: (2, 3) if layer == 0 else (5, 6)

    def landing(key):
        a = srcs[key]
        rows = fp if key[:2] in ("wg", "wu", "wd") else a.shape[0]
        buf = lax.empty((N_DEV, rows) + a.shape[1:], a.dtype)
        if rows > a.shape[0]:
            zeros = jnp.zeros((N_DEV, rows - a.shape[0]) + a.shape[1:], a.dtype)
            buf = lax.dynamic_update_slice(buf, zeros, (0, a.shape[0]) + (0,) * (a.ndim - 1))
        return lax.dynamic_update_slice(buf, a[None], (me,) + (0,) * a.ndim)

    _, chip_handles, gather_token = exchange_call(
        "gather_start", [], [("to_chips", [(srcs[k], landing(k)) for k in keys]) for keys in group_keys])
    weights = {}

    def gather_wait(gi, dep):
        lands = [(land,) for _, land in chip_handles[gi][0]]
        _, (passing,), _ = exchange_call(
            f"gather{gi}_pass", [("to_chips", chip_handles[gi])], [("pass_on", lands)], deps=(dep,))
        (complete,), _, _ = exchange_call(f"gather{gi}_done", [("pass_on", passing)], [])
        weights.update(zip(group_keys[gi], [land for (land,) in complete]))

    gb = ab_gla_gate_b
    hn_w = jnp.concatenate([ab_ret_norm_w, ab_gla_norm_w], axis=1)
    cosf, sins = _rope_tables(positions[0])
    p_bf = bf(p[:, 0])

    xs = x[0]
    saved = []
    for i in range(depth):
        nm = f"l{i}_"
        w_attn, w_ffn, w_ple = attn_norm_w[i:i + 1], ffn_norm_w[i:i + 1], ple_norm_w[i:i + 1]
        (xn,) = rowwise(nm + "norm_attn", lambda a, w: (_rms(a, w),), T, [("row", xs), ("full", w_attn)],
                        [("row", D, BF16)], deps=(gather_token,) if i == 0 else ())
        if i % 2 == 0:
            gather_wait(G_IN, xn)
            w_in_t = weights["w_in"].reshape(1, 1, in_width, D)
            w_lr_t = _pad_to(w_in_t[0, 0, OFF_LR:], 0, LANE).reshape(1, 1, LANE, D)
            z = mmt_fwd(nm + "mm_in", xn, w_in_t, 0, F32, n=OFF_LR)
            glr = mmt_fwd(nm + "mm_lr", xn, w_lr_t, 0, F32)
            oraw = retention_fwd(nm + "ret_fwd", z, cosf, sins, RET_V + GLA_V)
            gather_wait(G_OUT, oraw)
            w_oab = weights["w_oab"].reshape(1, 1, D, D)
            gu_full = _pad_to(weights["gu"].transpose(1, 0, 2).reshape(GLA_GATE_RANK, GLA_QK), 0, LANE)
            oraw = gla_fwd(nm + "gla_fwd", z, glr, gu_full, gb, oraw)
            o = headnorm_fwd(nm + "headnorm_fwd", oraw, z, hn_w)
            mix = mm_nn(nm + "mm_out", o, w_oab, 0, F32)
            mixer_saved = (z, glr, oraw, o)
        else:
            gather_wait(G_QKV, xn)
            w_qkv = weights["w_qkv"].reshape((1,) + weights["w_qkv"].shape)
            w_oc = weights["w_oc"].reshape(1, 1, D, D)
            qkv = mm_nn(nm + "mm_qkv", xn, w_qkv, 0, BF16)
            o, lse = attn_fwd(nm + "attn_fwd", qkv)
            mix = mm_nn(nm + "mm_out", o, w_oc, 0, F32)
            mixer_saved = (qkv, o, lse)
        h1, hn = rowwise(nm + "add_norm_ffn", lambda a, b, w: (a + b, _rms(a + b, w)), T,
                         [("row", xs), ("row", mix), ("full", w_ffn)], [("row", D, F32), ("row", D, BF16)])
        gather_wait(g_ffn(i)[0], hn)
        wg = weights[f"wg{i}"].reshape(1, N_DEV, fp, D)
        wu = weights[f"wu{i}"].reshape(1, N_DEV, fp, D)
        g, u, act = ffn_gate_up(nm + "ffn_gate_up", hn, wg, wu)
        gather_wait(g_ffn(i)[1], act)
        wd = weights[f"wd{i}"].reshape(1, 1, N_DEV * fp, D)
        wpg = weights[f"wpg{i}"].reshape(1, 1, D, D)
        wpp = weights[f"wpp{i}"].reshape((1,) + weights[f"wpp{i}"].shape)
        f = mm_nn(nm + "mm_down", act, wd, 0, F32)
        h2, pn = rowwise(nm + "add_norm_ple", lambda a, b, w: (a + b, _rms(a + b, w)), T,
                         [("row", h1), ("row", f), ("full", w_ple)], [("row", D, F32), ("row", D, BF16)])
        s = mm_nn(nm + "mm_ple_gate", pn, wpg, 0, F32)
        e = mm_nn(nm + "mm_ple_proj", p_bf[i], wpp, 0, F32)
        (x_next,) = rowwise(nm + "ple_out", lambda a, b, c: (a + _sigmoid(b) * c,), T,
                            [("row", h2), ("row", s), ("row", e)], [("row", D, F32)])
        mixer_w = (w_in_t, w_lr_t, w_oab, gu_full) if i % 2 == 0 else (w_qkv, w_oc)
        saved.append((xs, xn, mixer_saved, mixer_w, (wg, wu, wd, wpg), h1, hn, g, u, act, h2, pn, s, e))
        xs = x_next

    def loss_fn(a, w, t):
        diff = _rms(a, w) - t
        dx, dw = _rms_bwd(a, w, diff * (1.0 / D))
        part = 0.5 * jnp.sum(jnp.mean(diff * diff, axis=-1, keepdims=True), axis=0, keepdims=True)
        return dx, dw, jnp.broadcast_to(part, (1, LANE))

    dx, d_final_w, loss_part = rowwise("loss_head", loss_fn, T,
                                       [("row", xs), ("full", final_norm_w[None, :]), ("row", loss_target[0])],
                                       [("row", D, F32), ("acc", D), ("acc", LANE)])
    loss = lax.psum(loss_part[0, 0], ("x", "y", "c"))

    grads = {}
    on_chip = []
    scatters = []

    def scatter_start(name, keys, deps=()):
        waits = [("halves", on_chip[0][1])] if on_chip else []
        starts = [("halves", [(grads[k], lax.empty((N_CHIP,) + grads[k].shape[1:], BF16)) for k in keys])] if keys else []
        waited, handles, token = exchange_call(name, waits, starts, deps=deps)
        if on_chip:
            done_keys, _ = on_chip.pop()
            sums = [chip_sum(f"{name}_sum{j}", a, half) for j, (a, half) in enumerate(waited[0])]
            _, (handle,), token = exchange_call(
                name + "_chips", [], [("chip_sums", [(cs, lax.empty(cs.shape, BF16)) for cs in sums])])
            scatters.append((done_keys, handle))
        if keys:
            on_chip.append((keys, handles[0]))
        return token

    d_attn_w, d_ffn_w, d_ple_w = [None] * depth, [None] * depth, [None] * depth
    for i in reversed(range(depth)):
        nm = f"l{i}_b_"
        xs_i, xn, mixer_saved, mixer_w, (wg, wu, wd, wpg), h1, hn, g, u, act, h2, pn, s, e = saved[i]
        w_attn, w_ffn, w_ple = attn_norm_w[i:i + 1], ffn_norm_w[i:i + 1], ple_norm_w[i:i + 1]

        def ple_bwd(d, sv, ev):
            gate = _sigmoid(sv)
            return d * gate, d * ev * gate * (1.0 - gate)

        de, ds = rowwise(nm + "ple_out", ple_bwd, T, [("row", dx), ("row", s), ("row", e)],
                         [("row", D, BF16), ("row", D, BF16)], deps=(loss.reshape(1, 1),) if i == depth - 1 else ())
        grads[("ple_w_proj", i)] = mm_tn(nm + "mm_ple_proj_w", p_bf[i], de, N_DEV, BF16)
        grads[("ple_w_gate", i)] = mm_tn(nm + "mm_ple_gate_w", pn, ds, 1, BF16).reshape(N_DEV, D // N_DEV, D)
        dpn = mm_nt(nm + "mm_ple_gate_x", ds, wpg, 0, F32)

        def norm_bwd_add(a, w, dn, dres):
            dxx, dw = _rms_bwd(a, w, dn)
            tot = dres + dxx
            return tot, tot, dw

        dh2, dh2_bf, d_ple_w[i] = rowwise(nm + "norm_ple", norm_bwd_add, T,
                                          [("row", h2), ("full", w_ple), ("row", dpn), ("row", dx)],
                                          [("row", D, F32), ("row", D, BF16), ("acc", D)])
        grads[("ffn_w_down", i)] = mm_tn(nm + "mm_down_w", act, dh2_bf, 1, BF16).reshape(N_DEV, fp, D)
        dg, du = ffn_down_bwd(nm + "ffn_down_x", dh2_bf, wd, g, u)
        grads[("ffn_w_gate", i)] = mmt_dw(nm + "mm_gate_w", dg, hn, N_DEV, BF16)
        grads[("ffn_w_up", i)] = mmt_dw(nm + "mm_up_w", du, hn, N_DEV, BF16)
        ffn_keys = [("ple_w_proj", i), ("ple_w_gate", i), ("ffn_w_down", i), ("ffn_w_gate", i), ("ffn_w_up", i)]
        token = scatter_start(nm + "scatter_ffn", ffn_keys) if i % 2 == 1 else None
        dhn_g = mmt_dx_wide(nm + "mm_gate_x", dg, wg, F32, deps=() if token is None else (token,))
        dhn_u = mmt_dx_wide(nm + "mm_up_x", du, wu, F32)

        def norm_bwd_add2(a, w, dn1, dn2, dres):
            dxx, dw = _rms_bwd(a, w, dn1 + dn2)
            tot = dres + dxx
            return tot, tot, dw

        dh1, dh1_bf, d_ffn_w[i] = rowwise(nm + "norm_ffn", norm_bwd_add2, T,
                                          [("row", h1), ("full", w_ffn), ("row", dhn_g), ("row", dhn_u), ("row", dh2)],
                                          [("row", D, F32), ("row", D, BF16), ("acc", D)])
        if i % 2 == 0:
            z, glr, oraw, o = mixer_saved
            w_in_t, w_lr_t, w_oab, gu_full = mixer_w
            grads[("ab_w_out", 0)] = mm_tn(nm + "mm_out_w", o, dh1_bf, 1, BF16).reshape(N_DEV, D // N_DEV, D)
            token = scatter_start(nm + "scatter_ffn_out", ffn_keys + [("ab_w_out", 0)])
            do = mm_nt(nm + "mm_out_x", dh1_bf, w_oab, 0, F32, deps=(token,))
            d_oraw, d_gates, d_hn_w = headnorm_bwd(nm + "headnorm", oraw, z, hn_w, do)
            d_rq, d_rk, d_rv = retention_bwd(nm + "ret", z, cosf, sins, d_oraw)
            d_gq, d_gk, d_gv, d_glr4, d_gu, d_gb = gla_bwd(nm + "gla", z, glr, gu_full, gb, d_oraw)
            dz = jnp.concatenate([d_rq, d_rk, d_rv, d_gates[:, :RET_V], d_gq, d_gk, d_gv, d_gates[:, RET_V:]], axis=1)
            (d_glr,) = rowwise(nm + "sum_lr", lambda *a: (a[0] + a[1] + a[2] + a[3],), T,
                               [("row", d_glr4[hh]) for hh in range(GLA_HEADS)], [("row", LANE, BF16)])
            dw_main = mm_tn(nm + "mm_in_w", xn, dz, 1, BF16)[0]
            dw_lr = mm_tn(nm + "mm_lr_w", xn, d_glr, 1, BF16)[0]
            dw_in = jnp.concatenate([dw_main, dw_lr[:, :GLA_GATE_RANK]], axis=1)
            grads[("ab_w_in", 0)] = dw_in.reshape(D, N_DEV, in_shard).transpose(1, 0, 2)
            token = scatter_start(nm + "scatter_in", [("ab_w_in", 0)])
            dxn_a = mmt_dx_wide(nm + "mm_in_x", dz, w_in_t, F32, n=OFF_LR, deps=(token,))
            dxn_b = mmt_dx(nm + "mm_lr_x", d_glr, w_lr_t, 0, F32)
        else:
            qkv, o, lse = mixer_saved
            w_qkv, w_oc = mixer_w
            grads[("c_w_out", 0)] = mm_tn(nm + "mm_out_w", o, dh1_bf, 1, BF16).reshape(N_DEV, D // N_DEV, D)
            do = mm_nt(nm + "mm_out_x", dh1_bf, w_oc, 0, BF16)
            dq, dk, dv = attn_bwd(nm + "attn", qkv, o, lse, do)
            dqkv = jnp.concatenate([dq, dk, dv], axis=1)
            grads[("c_w_qkv", 0)] = mm_tn(nm + "mm_qkv_w", xn, dqkv, N_DEV, BF16)
            token = scatter_start(nm + "scatter_attn", [("c_w_out", 0), ("c_w_qkv", 0)])
            dxn_a = mm_nt_wide(nm + "mm_qkv_x", dqkv, w_qkv, F32, deps=(token,))
            dxn_b = None
        if dxn_b is None:
            dx, _, d_attn_w[i] = rowwise(nm + "norm_attn", norm_bwd_add, T,
                                         [("row", xs_i), ("full", w_attn), ("row", dxn_a), ("row", dh1)],
                                         [("row", D, F32), ("row", D, BF16), ("acc", D)])
        else:
            dx, _, d_attn_w[i] = rowwise(nm + "norm_attn", norm_bwd_add2, T,
                                         [("row", xs_i), ("full", w_attn), ("row", dxn_a), ("row", dxn_b), ("row", dh1)],
                                         [("row", D, F32), ("row", D, BF16), ("acc", D)])

    small_names = ["attn_norm_w", "ffn_norm_w", "ple_norm_w", "final_norm_w", "ab_gla_gate_b", "ab_ret_norm_w",
                   "ab_gla_norm_w"]
    small_grads = [jnp.concatenate(d_attn_w, 0), jnp.concatenate(d_ffn_w, 0), jnp.concatenate(d_ple_w, 0), d_final_w[0],
                   d_gb, d_hn_w[:, :RET_V], d_hn_w[:, RET_V:]]
    small_w = [attn_norm_w, ffn_norm_w, ple_norm_w, final_norm_w, ab_gla_gate_b, ab_ret_norm_w, ab_gla_norm_w]
    small_m = [m_attn_norm_w, m_ffn_norm_w, m_ple_norm_w, m_final_norm_w, m_ab_gla_gate_b, m_ab_ret_norm_w, m_ab_gla_norm_w]
    small_v = [v_attn_norm_w, v_ffn_norm_w, v_ple_norm_w, v_final_norm_w, v_ab_gla_gate_b, v_ab_ret_norm_w, v_ab_gla_norm_w]
    sizes = [int(np.prod(a.shape)) for a in small_w]
    n_gu = GLA_GATE_RANK * GLA_QK
    n_small = _round_up(sum(sizes) + n_gu, LANE)
    pack = lambda parts: _pad_to(jnp.concatenate([a.reshape(-1) for a in parts]), 0, n_small)[None, :]
    small_part = pack(small_grads + [d_gu[:GLA_GATE_RANK]])

    big_w = dict(ab_w_in=(ab_w_in, m_ab_w_in, v_ab_w_in), ab_w_out=(ab_w_out, m_ab_w_out, v_ab_w_out),
                 c_w_qkv=(c_w_qkv, m_c_w_qkv, v_c_w_qkv), c_w_out=(c_w_out, m_c_w_out, v_c_w_out),
                 ffn_w_gate=(wg_t, tr_(m_ffn_w_gate), tr_(v_ffn_w_gate)),
                 ffn_w_up=(wu_t, tr_(m_ffn_w_up), tr_(v_ffn_w_up)),
                 ffn_w_down=(ffn_w_down, m_ffn_w_down, v_ffn_w_down), ple_w_proj=(ple_w_proj, m_ple_w_proj, v_ple_w_proj),
                 ple_w_gate=(ple_w_gate, m_ple_w_gate, v_ple_w_gate))
    scatter_start("scatter_last", [], deps=(dx,))
    results, last = {}, dx
    for gi, (keys, handle) in enumerate(scatters):
        (arrived,), _, _ = exchange_call(f"scatter_wait{gi}", [("chip_sums", handle)], [], deps=(last,))
        for (n, l), (own, land) in zip(keys, arrived):
            results[n] = adamw(f"adamw_{n}{l}", *big_w[n], l, land, own, prev=results.get(n))
            last = results[n][0]
    for n in ("ffn_w_gate", "ffn_w_up"):
        results[n] = [tr_(a) for a in results[n]]
    small_parts = gather_small("gather_small", small_part, deps=(last,)).reshape(N_DEV, n_small)

    gu_off = sum(sizes)
    own_cols = lambda a: lax.dynamic_slice_in_dim(a.reshape(GLA_GATE_RANK, GLA_QK), me * gu_cols, gu_cols, axis=1)
    small_res = adamw_small("adamw_small", pack(small_w + [jnp.zeros((n_gu,), F32)]),
                            pack(small_m + [jnp.zeros((n_gu,), F32)]), pack(small_v + [jnp.ones((n_gu,), F32)]),
                            small_parts)
    g_gu_full = small_res[0][0, gu_off:gu_off + n_gu]
    g_gu = own_cols(g_gu_full)[None]
    gu_res = adamw_small("adamw_gate_up", *[_pad_to(a.reshape(1, -1), 1, _round_up(a.size, LANE)) for a in
                                            (ab_gla_gate_up, m_ab_gla_gate_up, v_ab_gla_gate_up)],
                         jnp.concatenate([_pad_to(g_gu.reshape(1, -1), 1, _round_up(g_gu.size, LANE)),
                                          jnp.zeros((N_DEV - 1, _round_up(g_gu.size, LANE)), F32)], axis=0))
    for k in range(4):
        off = 0
        for n, a, sz in zip(small_names, small_w, sizes):
            results.setdefault(n, [None] * 4)[k] = small_res[k][0, off:off + sz].reshape(a.shape)
            off += sz
        results.setdefault("ab_gla_gate_up", [None] * 4)[k] = gu_res[k][0, :g_gu.size].reshape(ab_gla_gate_up.shape)

    order = ["attn_norm_w", "ffn_norm_w", "ple_norm_w", "final_norm_w", "ab_w_in", "ab_gla_gate_up", "ab_gla_gate_b",
             "ab_ret_norm_w", "ab_gla_norm_w", "ab_w_out", "c_w_qkv", "c_w_out", "ffn_w_gate", "ffn_w_up", "ffn_w_down",
             "ple_w_proj", "ple_w_gate"]
    return (loss, dx[None], *[results[n][0] for n in order], *[results[n][1] for n in order],
            *[results[n][2] for n in order], *[results[n][3] for n in order])
```

```python
import math

import numpy as np
import jax
import jax.numpy as jnp
from jax import lax
from jax.experimental import pallas as pl
from jax.experimental.pallas import tpu as pltpu

F32 = jnp.float32
BF16 = jnp.bfloat16
HIGHEST = lax.Precision.HIGHEST

N_DEV = 8
VMEM_LIMIT_BYTES = 48 * 1024 * 1024
LANE = 128
NORM_EPS = 1e-6

RET_HEADS, RET_DK, RET_DV = 4, 256, 256
RET_THETA_BASE = 10000.0
GLA_HEADS, GLA_DK, GLA_DV = 4, 128, 256
GLA_GATE_RANK = 16
GLA_GATE_NORM = 16.0
CHUNK = 64
ATT_HEADS = 16
DILATED_BRANCHES = ((128, 1), (512, 4), (2048, 16))
BLK = 256

ADAM_LR, ADAM_B1, ADAM_B2, ADAM_EPS, ADAM_WD, ADAM_STEP = 0.001, 0.9, 0.999, 1e-08, 0.01, 10

RET_QK = RET_HEADS * RET_DK
RET_V = RET_HEADS * RET_DV
GLA_QK = GLA_HEADS * GLA_DK
GLA_V = GLA_HEADS * GLA_DV
OFF_RQ, OFF_RK, OFF_RV, OFF_RG = 0, RET_QK, 2 * RET_QK, 2 * RET_QK + RET_V
OFF_GQ = OFF_RG + RET_V
OFF_GK = OFF_GQ + GLA_QK
OFF_GV = OFF_GK + GLA_QK
OFF_GG = OFF_GV + GLA_V
OFF_LR = OFF_GG + GLA_V


def _params(*sem):
    return pltpu.CompilerParams(dimension_semantics=sem or None, vmem_limit_bytes=VMEM_LIMIT_BYTES)


def _pick(n, cands):
    for c in cands:
        if n % c == 0:
            return c
    raise ValueError(f"no tile for {n} in {cands}")


_NN = (((1,), (0,)), ((), ()))
_NT = (((1,), (1,)), ((), ()))
_TN = (((0,), (0,)), ((), ()))
_ANY = pl.BlockSpec(memory_space=pl.ANY)
MAX_CONTRACT = 2048
_TILES = (1024, 768, 512, 256, 128)


def _mm_call(name, dims, grid, in_specs, out_spec, out_shape, args, deps=()):
    steps = grid[2]
    assert steps == 1 or out_shape.dtype == F32

    def body(a_ref, b_ref, *rest):
        o_ref = rest[len(deps)]
        part = lax.dot_general(a_ref[...].astype(BF16), b_ref[...].astype(BF16), dims, preferred_element_type=F32)
        if steps == 1:
            o_ref[...] = part.astype(o_ref.dtype)
        else:
            _accumulate(o_ref, part, pl.program_id(2) == 0)

    return pl.pallas_call(
        body, name=name, grid=grid, in_specs=list(in_specs) + [_ANY] * len(deps), out_specs=out_spec,
        out_shape=out_shape, compiler_params=_params("parallel", "parallel", "arbitrary"))(*args, *deps)


def mm_nn(name, a, w, l, out_dtype, deps=()):
    _, J, K, n = w.shape
    M = a.shape[0]
    tm, tn, tk = _pick(M, _TILES), _pick(n, _TILES), _pick(K, (MAX_CONTRACT,) + _TILES)
    nt = n // tn
    return _mm_call(
        name, _NN, (M // tm, J * nt, K // tk),
        [pl.BlockSpec((tm, tk), lambda i, j, k: (i, k)),
         pl.BlockSpec((None, None, tk, tn), lambda i, j, k: (l, j // nt, k, j % nt))],
        pl.BlockSpec((tm, tn), lambda i, j, k: (i, j)),
        jax.ShapeDtypeStruct((M, J * n), out_dtype), (a, w), deps)


def mm_nt(name, a, w, l, out_dtype, deps=()):
    _, J, K, n = w.shape
    M = a.shape[0]
    tm, tq, tc = _pick(M, _TILES), _pick(K, _TILES), _pick(n, (MAX_CONTRACT,) + _TILES)
    nc = n // tc
    return _mm_call(
        name, _NT, (M // tm, K // tq, J * nc),
        [pl.BlockSpec((tm, tc), lambda i, q, c: (i, c)),
         pl.BlockSpec((None, None, tq, tc), lambda i, q, c: (l, c // nc, q, c % nc))],
        pl.BlockSpec((tm, tq), lambda i, q, c: (i, q)),
        jax.ShapeDtypeStruct((M, K), out_dtype), (a, w), deps)


def mm_tn(name, x, dy, J, out_dtype, deps=()):
    M, K = x.shape
    n = dy.shape[1] // J
    tp, tn = _pick(K, _TILES), _pick(n, _TILES)
    nt = n // tn
    assert M <= MAX_CONTRACT
    return _mm_call(
        name, _TN, (K // tp, J * nt, 1),
        [pl.BlockSpec((M, tp), lambda i, j, r: (0, i)),
         pl.BlockSpec((M, tn), lambda i, j, r: (0, j))],
        pl.BlockSpec((None, tp, tn), lambda i, j, r: (j // nt, i, j % nt)),
        jax.ShapeDtypeStruct((J, K, n), out_dtype), (x, dy), deps)


def mmt_fwd(name, a, wt, l, out_dtype, n=None, deps=()):
    _, J, rows, K = wt.shape
    n = rows if n is None else n
    M = a.shape[0]
    tm, tn = _pick(M, _TILES), _pick(n, _TILES)
    nt = n // tn
    assert K <= MAX_CONTRACT
    return _mm_call(
        name, _NT, (M // tm, J * nt, 1),
        [pl.BlockSpec((tm, K), lambda i, j, k: (i, 0)),
         pl.BlockSpec((None, None, tn, K), lambda i, j, k: (l, j // nt, j % nt, 0))],
        pl.BlockSpec((tm, tn), lambda i, j, k: (i, j)),
        jax.ShapeDtypeStruct((M, J * n), out_dtype), (a, wt), deps)


def mmt_dx(name, dy, wt, l, out_dtype, n=None, deps=()):
    _, J, rows, K = wt.shape
    n = rows if n is None else n
    M = dy.shape[0]
    tm, tq, tc = _pick(M, _TILES), _pick(K, _TILES), _pick(n, _TILES)
    nc = n // tc
    return _mm_call(
        name, _NN, (M // tm, K // tq, J * nc),
        [pl.BlockSpec((tm, tc), lambda i, q, c: (i, c)),
         pl.BlockSpec((None, None, tc, tq), lambda i, q, c: (l, c // nc, c % nc, q))],
        pl.BlockSpec((tm, tq), lambda i, q, c: (i, q)),
        jax.ShapeDtypeStruct((M, K), out_dtype), (dy, wt), deps)


WIDE_TILE = 512


def _wide_call(name, body, M, K, a, w, a_spec, w_spec, out_dtype, deps):
    def kernel_body(a_ref, w_ref, *rest):
        o_ref = rest[len(deps)]
        o_ref[...] = body(a_ref, w_ref).astype(o_ref.dtype)

    return pl.pallas_call(
        kernel_body, name=name, grid=(M // WIDE_TILE, K // WIDE_TILE),
        in_specs=[a_spec, w_spec] + [_ANY] * len(deps),
        out_specs=pl.BlockSpec((WIDE_TILE, WIDE_TILE), lambda i, q: (i, q)),
        out_shape=jax.ShapeDtypeStruct((M, K), out_dtype),
        compiler_params=_params("parallel", "parallel"))(a, w, *deps)


def mmt_dx_wide(name, dy, wt, out_dtype, n=None, deps=()):
    _, J, rows, K = wt.shape
    n = rows if n is None else n
    M = dy.shape[0]

    def body(dy_ref, w_ref):
        return jnp.dot(dy_ref[...].astype(BF16), w_ref[...].reshape(J * n, WIDE_TILE), preferred_element_type=F32)

    return _wide_call(name, body, M, K, dy, wt,
                      pl.BlockSpec((WIDE_TILE, J * n), lambda i, q: (i, 0)),
                      pl.BlockSpec((None, J, n, WIDE_TILE), lambda i, q: (0, 0, 0, q)), out_dtype, deps)


def mm_nt_wide(name, a, w, out_dtype, deps=()):
    _, J, K, n = w.shape
    M = a.shape[0]

    def body(a_ref, w_ref):
        acc = None
        for j in range(J):
            part = lax.dot_general(a_ref[:, j * n:(j + 1) * n].astype(BF16), w_ref[j], _NT, preferred_element_type=F32)
            acc = part if acc is None else acc + part
        return acc

    return _wide_call(name, body, M, K, a, w,
                      pl.BlockSpec((WIDE_TILE, J * n), lambda i, q: (i, 0)),
                      pl.BlockSpec((None, J, WIDE_TILE, n), lambda i, q: (0, 0, q, 0)), out_dtype, deps)


def mmt_dw(name, dy, x, J, out_dtype, deps=()):
    M, K = x.shape
    n = dy.shape[1] // J
    tn, tp = _pick(n, _TILES), _pick(K, _TILES)
    nt = n // tn
    assert M <= MAX_CONTRACT
    return _mm_call(
        name, _TN, (J * nt, K // tp, 1),
        [pl.BlockSpec((M, tn), lambda j, i, r: (0, j)),
         pl.BlockSpec((M, tp), lambda j, i, r: (0, i))],
        pl.BlockSpec((None, tn, tp), lambda j, i, r: (j // nt, j % nt, i)),
        jax.ShapeDtypeStruct((J, n, K), out_dtype), (dy, x), deps)


def ffn_gate_up(name, a, wg, wu):
    _, J, n, K = wg.shape
    M = a.shape[0]
    tm, tn = _pick(M, _TILES), _pick(n, _TILES)
    nt = n // tn
    assert K <= MAX_CONTRACT

    def body(a_ref, wg_ref, wu_ref, g_ref, u_ref, act_ref):
        x = a_ref[...]
        g = lax.dot_general(x, wg_ref[...], _NT, preferred_element_type=F32)
        u = lax.dot_general(x, wu_ref[...], _NT, preferred_element_type=F32)
        g_ref[...] = g.astype(g_ref.dtype)
        u_ref[...] = u.astype(u_ref.dtype)
        act_ref[...] = (_silu_and_grad(g)[0] * u).astype(act_ref.dtype)

    w_spec = pl.BlockSpec((None, None, tn, K), lambda i, j: (0, j // nt, j % nt, 0))
    out = pl.BlockSpec((tm, tn), lambda i, j: (i, j))
    return pl.pallas_call(
        body, name=name, grid=(M // tm, J * nt),
        in_specs=[pl.BlockSpec((tm, K), lambda i, j: (i, 0)), w_spec, w_spec],
        out_specs=[out] * 3, out_shape=[jax.ShapeDtypeStruct((M, J * n), BF16)] * 3,
        compiler_params=_params("parallel", "parallel"))(a, wg, wu)


def ffn_down_bwd(name, dy, wd, g, u, deps=()):
    _, _, K, n = wd.shape
    M = dy.shape[0]
    tm, tq = _pick(M, _TILES), _pick(K, _TILES)
    assert n <= MAX_CONTRACT

    def body(dy_ref, w_ref, g_ref, u_ref, *rest):
        dg_ref, du_ref = rest[len(deps):]
        dact = lax.dot_general(dy_ref[...], w_ref[...], _NT, preferred_element_type=F32)
        silu, dsilu = _silu_and_grad(g_ref[...].astype(F32))
        dg_ref[...] = (dact * u_ref[...].astype(F32) * dsilu).astype(dg_ref.dtype)
        du_ref[...] = (dact * silu).astype(du_ref.dtype)

    blk = pl.BlockSpec((tm, tq), lambda i, q: (i, q))
    return pl.pallas_call(
        body, name=name, grid=(M // tm, K // tq),
        in_specs=[pl.BlockSpec((tm, n), lambda i, q: (i, 0)),
                  pl.BlockSpec((None, None, tq, n), lambda i, q: (0, 0, q, 0)), blk, blk] + [_ANY] * len(deps),
        out_specs=[blk, blk], out_shape=[jax.ShapeDtypeStruct((M, K), BF16)] * 2,
        compiler_params=_params("parallel", "parallel"))(dy, wd, g, u, *deps)


def rowwise(name, fn, rows, ins, outs, tr=256, deps=()):
    widest = max([s[1].shape[1] if s[0] != "col" else s[3] for s in ins] + [s[1] for s in outs])
    tr = min(tr if widest <= 2048 else tr // 2, rows)
    in_specs, args = [], []
    for spec in ins:
        kind, a = spec[0], spec[1]
        if kind == "row":
            in_specs.append(pl.BlockSpec((tr, a.shape[1]), lambda i: (i, 0)))
        elif kind == "col":
            cb, width = spec[2], spec[3]
            in_specs.append(pl.BlockSpec((tr, width), lambda i, cb=cb: (i, cb)))
        else:
            in_specs.append(pl.BlockSpec(a.shape, lambda i: (0, 0)))
        args.append(a)
    out_specs, out_shapes = [], []
    for spec in outs:
        if spec[0] == "row":
            out_specs.append(pl.BlockSpec((tr, spec[1]), lambda i: (i, 0)))
            out_shapes.append(jax.ShapeDtypeStruct((rows, spec[1]), spec[2]))
        else:
            out_specs.append(pl.BlockSpec((1, spec[1]), lambda i: (0, 0)))
            out_shapes.append(jax.ShapeDtypeStruct((1, spec[1]), F32))
    n_in = len(ins)

    def body(*refs):
        vals = fn(*[r[...] for r in refs[:n_in]])
        first = pl.program_id(0) == 0
        for r, v, spec in zip(refs[n_in + len(deps):], vals, outs):
            if spec[0] == "row":
                r[...] = v.astype(r.dtype)
            else:
                _accumulate(r, v, first)

    return pl.pallas_call(body, name=name, grid=(rows // tr,), in_specs=in_specs + [_ANY] * len(deps),
                          out_specs=out_specs, out_shape=out_shapes,
                          compiler_params=_params("arbitrary"))(*args, *deps)


def _accumulate(ref, v, first):
    @pl.when(first)
    def _():
        ref[...] = v

    @pl.when(jnp.logical_not(first))
    def _():
        ref[...] += v


def _rms(x, w):
    r = lax.rsqrt(jnp.mean(x * x, axis=-1, keepdims=True) + NORM_EPS)
    return x * r * w


def _rms_bwd(x, w, dy):
    r = lax.rsqrt(jnp.mean(x * x, axis=-1, keepdims=True) + NORM_EPS)
    g = dy * w
    dx = r * (g - x * (r * r) * jnp.mean(g * x, axis=-1, keepdims=True))
    dw = jnp.sum(dy * x * r, axis=0, keepdims=True)
    return dx, dw


def _sigmoid(x):
    return 1.0 / (1.0 + jnp.exp(-x))


def _silu_and_grad(g):
    s = _sigmoid(g)
    return g * s, s * (1.0 + g * (1.0 - s))


def _swap_pairs(x):
    n = x.shape[-1]
    lane = lax.broadcasted_iota(jnp.int32, x.shape, x.ndim - 1)
    return jnp.where((lane & 1) == 0, pltpu.roll(x, n - 1, x.ndim - 1), pltpu.roll(x, 1, x.ndim - 1))


def _rot(x, cosf, sins):
    return x * cosf + _swap_pairs(x) * sins


def _unrot(d, cosf, sins):
    return d * cosf + _swap_pairs(d * sins)


def _ret_log_gamma(h):
    vals = [math.log1p(-2.0 ** (-5.0 - i)) for i in range(RET_HEADS)]
    out = jnp.float32(vals[RET_HEADS - 1])
    for i in range(RET_HEADS - 2, -1, -1):
        out = jnp.where(h == i, jnp.float32(vals[i]), out)
    return out


def _fill_decays(dec_ref, lg):
    ri = lax.broadcasted_iota(jnp.int32, (BLK, BLK), 0)
    ci = lax.broadcasted_iota(jnp.int32, (BLK, BLK), 1)
    for d in range(dec_ref.shape[0]):
        dt = d * BLK + ri - ci
        dec_ref[d] = jnp.where(dt >= 0, jnp.exp(jnp.maximum(dt, 0).astype(F32) * lg), 0.0)


def _decay_row(dec_ref, qi):
    return jnp.concatenate([dec_ref[qi - kb] for kb in range(qi + 1)], axis=1)


def _once(block_shape, index_map):
    return pl.BlockSpec(block_shape, index_map, pipeline_mode=pl.Buffered(1))


def _dot(a, b):
    return jnp.dot(a.astype(BF16), b.astype(BF16), preferred_element_type=F32)


def _dot_nt(a, b):
    return lax.dot_general(a.astype(BF16), b.astype(BF16), _NT, preferred_element_type=F32)


def _dot_tn(a, b):
    return lax.dot_general(a.astype(BF16), b.astype(BF16), _TN, preferred_element_type=F32)


def retention_fwd(name, z, cosf, sins, width_out):
    T = z.shape[0]
    nq = T // BLK
    scale = RET_DK ** -0.5

    def body(q_ref, k_ref, v_ref, cos_ref, sin_ref, o_ref, krot, vb, dec_ref):
        _fill_decays(dec_ref, _ret_log_gamma(pl.program_id(0)))
        krot[...] = (_rot(k_ref[...], cos_ref[...], sin_ref[...]) * scale).astype(BF16)
        vb[...] = v_ref[...].astype(BF16)
        for qi in range(nq):
            rows, n = slice(qi * BLK, (qi + 1) * BLK), (qi + 1) * BLK
            q = _rot(q_ref[rows, :], cos_ref[rows, :], sin_ref[rows, :])
            s = _dot_nt(q, krot[0:n, :]) * _decay_row(dec_ref, qi)
            o_ref[rows, :] = _dot(s, vb[0:n, :])

    return pl.pallas_call(
        body, name=name, grid=(RET_HEADS,),
        in_specs=[pl.BlockSpec((T, RET_DK), lambda h: (0, OFF_RQ // RET_DK + h)),
                  pl.BlockSpec((T, RET_DK), lambda h: (0, OFF_RK // RET_DK + h)),
                  pl.BlockSpec((T, RET_DV), lambda h: (0, OFF_RV // RET_DV + h)),
                  _once((T, RET_DK), lambda h: (0, 0)), _once((T, RET_DK), lambda h: (0, 0))],
        out_specs=pl.BlockSpec((T, RET_DV), lambda h: (0, h)),
        out_shape=jax.ShapeDtypeStruct((T, width_out), F32),
        scratch_shapes=[pltpu.VMEM((T, RET_DK), BF16), pltpu.VMEM((T, RET_DV), BF16),
                        pltpu.VMEM((nq, BLK, BLK), F32)],
        compiler_params=_params("arbitrary"))(z, z, z, cosf, sins)


def retention_bwd(name, z, cosf, sins, do):
    T = z.shape[0]
    nq = T // BLK
    scale = RET_DK ** -0.5

    def body(q_ref, k_ref, v_ref, cos_ref, sin_ref, do_ref, dq_ref, dk_ref, dv_ref, krot, vb, dk_acc, dv_acc, dec_ref):
        _fill_decays(dec_ref, _ret_log_gamma(pl.program_id(0)))
        krot[...] = (_rot(k_ref[...], cos_ref[...], sin_ref[...]) * scale).astype(BF16)
        vb[...] = v_ref[...].astype(BF16)
        dk_acc[...] = jnp.zeros_like(dk_acc)
        dv_acc[...] = jnp.zeros_like(dv_acc)
        for qi in range(nq):
            rows, n = slice(qi * BLK, (qi + 1) * BLK), (qi + 1) * BLK
            cos_q, sin_q = cos_ref[rows, :], sin_ref[rows, :]
            q = _rot(q_ref[rows, :], cos_q, sin_q).astype(BF16)
            dout = do_ref[rows, :].astype(BF16)
            kk, vv, dec = krot[0:n, :], vb[0:n, :], _decay_row(dec_ref, qi)
            p = (_dot_nt(q, kk) * dec).astype(BF16)
            ds = (_dot_nt(dout, vv) * dec).astype(BF16)
            dq_ref[rows, :] = _unrot(_dot(ds, kk), cos_q, sin_q).astype(dq_ref.dtype)
            dk_acc[0:n, :] += _dot_tn(ds, q)
            dv_acc[0:n, :] += _dot_tn(p, dout)
        dk_ref[...] = (_unrot(dk_acc[...], cos_ref[...], sin_ref[...]) * scale).astype(dk_ref.dtype)
        dv_ref[...] = dv_acc[...].astype(dv_ref.dtype)

    head = lambda h: (0, h)
    return pl.pallas_call(
        body, name=name, grid=(RET_HEADS,),
        in_specs=[pl.BlockSpec((T, RET_DK), lambda h: (0, OFF_RQ // RET_DK + h)),
                  pl.BlockSpec((T, RET_DK), lambda h: (0, OFF_RK // RET_DK + h)),
                  pl.BlockSpec((T, RET_DV), lambda h: (0, OFF_RV // RET_DV + h)),
                  _once((T, RET_DK), lambda h: (0, 0)), _once((T, RET_DK), lambda h: (0, 0)),
                  pl.BlockSpec((T, RET_DV), head)],
        out_specs=[pl.BlockSpec((T, RET_DK), head), pl.BlockSpec((T, RET_DK), head), pl.BlockSpec((T, RET_DV), head)],
        out_shape=[jax.ShapeDtypeStruct((T, RET_QK), BF16), jax.ShapeDtypeStruct((T, RET_QK), BF16),
                   jax.ShapeDtypeStruct((T, RET_V), BF16)],
        scratch_shapes=[pltpu.VMEM((T, RET_DK), BF16), pltpu.VMEM((T, RET_DV), BF16),
                        pltpu.VMEM((T, RET_DK), F32), pltpu.VMEM((T, RET_DV), F32),
                        pltpu.VMEM((nq, BLK, BLK), F32)],
        compiler_params=_params("arbitrary"))(z, z, z, cosf, sins, do)


GLA_PAIR = 2


def _gla_chunk(q_ref, k_ref, v_ref, glr_ref, gu, gb, rows, hh, trilf):
    ck = slice(hh * GLA_DK, (hh + 1) * GLA_DK)
    zg = _dot(glr_ref[rows, :], gu[:, ck]) + gb[:, ck]
    la = (jnp.minimum(zg, 0.0) - jnp.log(1.0 + jnp.exp(-jnp.abs(zg)))) * (1.0 / GLA_GATE_NORM)
    cum = jnp.dot(trilf, la, precision=HIGHEST, preferred_element_type=F32)
    last = jnp.sum(la, axis=0, keepdims=True)
    ecum = jnp.exp(cum)
    k = k_ref[rows, ck]
    qt = q_ref[rows, ck] * (GLA_DK ** -0.5) * ecum
    kt = k * jnp.exp(-cum)
    kh = k * jnp.exp(last - cum)
    return zg, cum, last, ecum, qt, kt, kh, v_ref[rows, hh * GLA_DV:(hh + 1) * GLA_DV].astype(BF16)


def _state_decay(last):
    e = jnp.exp(jnp.broadcast_to(last, (GLA_DK, GLA_DK)).T)
    return jnp.concatenate([e] * (GLA_DV // GLA_DK), axis=1)


def _gla_specs(T):
    wk, wv = GLA_PAIR * GLA_DK, GLA_PAIR * GLA_DV
    return [_once((T, wk), lambda h: (0, OFF_GQ // wk + h)),
            _once((T, wk), lambda h: (0, OFF_GK // wk + h)),
            _once((T, wv), lambda h: (0, OFF_GV // wv + h)),
            _once((T, LANE), lambda h: (0, 0)),
            pl.BlockSpec((LANE, wk), lambda h: (0, h)),
            pl.BlockSpec((1, wk), lambda h: (0, h))]


def gla_fwd(name, z, glr, gu, gb, o_prev):
    T = z.shape[0]
    nc = T // CHUNK
    wv = GLA_PAIR * GLA_DV

    def body(q_ref, k_ref, v_ref, glr_ref, gu_ref, gb_ref, prev_ref, o_ref, S):
        del prev_ref
        gu_b, gb_v = gu_ref[...].astype(BF16), gb_ref[...]
        ri = lax.broadcasted_iota(jnp.int32, (CHUNK, CHUNK), 0)
        ci = lax.broadcasted_iota(jnp.int32, (CHUNK, CHUNK), 1)
        tril = ri >= ci
        trilf = tril.astype(F32)
        S[...] = jnp.zeros_like(S)

        def step(c, carry):
            rows = pl.ds(pl.multiple_of(c * CHUNK, CHUNK), CHUNK)
            for hh in range(GLA_PAIR):
                _, _, last, _, qt, kt, kh, v = _gla_chunk(q_ref, k_ref, v_ref, glr_ref, gu_b, gb_v, rows, hh, trilf)
                a = jnp.where(tril, _dot_nt(qt, kt), 0.0)
                s_prev = S[hh]
                o_ref[rows, hh * GLA_DV:(hh + 1) * GLA_DV] = _dot(a, v) + _dot(qt, s_prev)
                S[hh] = s_prev * _state_decay(last) + _dot_tn(kh, v)
            return carry

        lax.fori_loop(0, nc, step, 0)

    n_in = 6
    return pl.pallas_call(
        body, name=name, grid=(GLA_HEADS // GLA_PAIR,),
        in_specs=_gla_specs(T) + [pl.BlockSpec(memory_space=pl.ANY)],
        out_specs=pl.BlockSpec((T, wv), lambda h: (0, RET_V // wv + h)),
        out_shape=jax.ShapeDtypeStruct(o_prev.shape, F32),
        scratch_shapes=[pltpu.VMEM((GLA_PAIR, GLA_DK, GLA_DV), F32)],
        input_output_aliases={n_in: 0},
        compiler_params=_params("arbitrary"))(z, z, z, glr, gu, gb, o_prev)


def gla_bwd(name, z, glr, gu, gb, do):
    T = z.shape[0]
    nc = T // CHUNK

    def body(q_ref, k_ref, v_ref, glr_ref, gu_ref, gb_ref, do_ref,
             dq_ref, dk_ref, dv_ref, dglr_ref, dgu_ref, dgb_ref, s_all, dS):
        gu_b, gb_v = gu_ref[...].astype(BF16), gb_ref[...]
        ri = lax.broadcasted_iota(jnp.int32, (CHUNK, CHUNK), 0)
        ci = lax.broadcasted_iota(jnp.int32, (CHUNK, CHUNK), 1)
        tril = ri >= ci
        trilf = tril.astype(F32)
        triuf = (ri <= ci).astype(F32)
        last_row = lax.broadcasted_iota(jnp.int32, (CHUNK, GLA_DK), 0) == CHUNK - 1
        ones8 = jnp.ones((8, GLA_DV), F32)

        def fstep(c, carry):
            rows = pl.ds(pl.multiple_of(c * CHUNK, CHUNK), CHUNK)
            for hh in range(GLA_PAIR):
                s_prev = dS[hh]
                s_all[hh, c] = s_prev
                _, _, last, _, _, _, kh, v = _gla_chunk(q_ref, k_ref, v_ref, glr_ref, gu_b, gb_v, rows, hh, trilf)
                dS[hh] = s_prev * _state_decay(last) + _dot_tn(kh, v)
            return carry

        dS[...] = jnp.zeros_like(dS)
        lax.fori_loop(0, nc, fstep, 0)
        dS[...] = jnp.zeros_like(dS)
        dgu_ref[...] = jnp.zeros_like(dgu_ref)
        dgb_ref[...] = jnp.zeros_like(dgb_ref)

        def bstep(i, carry):
            c = nc - 1 - i
            rows = pl.ds(pl.multiple_of(c * CHUNK, CHUNK), CHUNK)
            glr_c = glr_ref[rows, :]
            for hh in range(GLA_PAIR):
                ck, cv = slice(hh * GLA_DK, (hh + 1) * GLA_DK), slice(hh * GLA_DV, (hh + 1) * GLA_DV)
                zg, cum, last, ecum, qt, kt, kh, v = _gla_chunk(q_ref, k_ref, v_ref, glr_ref, gu_b, gb_v, rows, hh, trilf)
                a = jnp.where(tril, _dot_nt(qt, kt), 0.0)
                s_prev, ds_new = s_all[hh, c], dS[hh]
                dout = do_ref[rows, cv].astype(BF16)
                dv_ref[rows, cv] = (_dot_tn(a, dout) + _dot(kh, ds_new)).astype(dv_ref.dtype)
                da = jnp.where(tril, _dot_nt(dout, v), 0.0)
                dqt = _dot(da, kt) + _dot_nt(dout, s_prev)
                dkt = _dot_tn(da, qt)
                dkh = _dot_nt(v, ds_new)
                dS[hh] = ds_new * _state_decay(last) + _dot_tn(qt, dout)
                dq_ref[rows, ck] = (dqt * ecum * (GLA_DK ** -0.5)).astype(dq_ref.dtype)
                dk_ref[rows, ck] = (dkt * jnp.exp(-cum) + dkh * jnp.exp(last - cum)).astype(dk_ref.dtype)
                dkh_kh = dkh * kh
                dcum = dqt * qt - dkt * kt - dkh_kh
                rs = lax.dot_general(ones8, ds_new * s_prev, _NT, precision=HIGHEST, preferred_element_type=F32)
                dlast = (jnp.sum(dkh_kh, axis=0, keepdims=True)
                         + jnp.exp(last) * (jnp.sum(rs, axis=0, keepdims=True) * 0.125))
                dcum = dcum + jnp.where(last_row, dlast, 0.0)
                dla = jnp.dot(triuf, dcum, precision=HIGHEST, preferred_element_type=F32)
                dzg = dla * (1.0 / GLA_GATE_NORM) * _sigmoid(-zg)
                dglr_ref[hh, rows, :] = _dot_nt(dzg, gu_b[:, ck])
                dgu_ref[:, ck] += _dot_tn(glr_c, dzg)
                dgb_ref[:, ck] += jnp.sum(dzg, axis=0, keepdims=True)
            return carry

        lax.fori_loop(0, nc, bstep, 0)

    wk, wv = GLA_PAIR * GLA_DK, GLA_PAIR * GLA_DV
    return pl.pallas_call(
        body, name=name, grid=(GLA_HEADS // GLA_PAIR,),
        in_specs=_gla_specs(T) + [_once((T, wv), lambda h: (0, RET_V // wv + h))],
        out_specs=[pl.BlockSpec((T, wk), lambda h: (0, h)), pl.BlockSpec((T, wk), lambda h: (0, h)),
                   pl.BlockSpec((T, wv), lambda h: (0, h)),
                   pl.BlockSpec((GLA_PAIR, T, LANE), lambda h: (h, 0, 0)),
                   pl.BlockSpec((LANE, wk), lambda h: (0, h)), pl.BlockSpec((1, wk), lambda h: (0, h))],
        out_shape=[jax.ShapeDtypeStruct((T, GLA_QK), BF16), jax.ShapeDtypeStruct((T, GLA_QK), BF16),
                   jax.ShapeDtypeStruct((T, GLA_V), BF16), jax.ShapeDtypeStruct((GLA_HEADS, T, LANE), F32),
                   jax.ShapeDtypeStruct((LANE, GLA_QK), F32), jax.ShapeDtypeStruct((1, GLA_QK), F32)],
        scratch_shapes=[pltpu.VMEM((GLA_PAIR, nc, GLA_DK, GLA_DV), F32), pltpu.VMEM((GLA_PAIR, GLA_DK, GLA_DV), F32)],
        compiler_params=_params("arbitrary"))(z, z, z, glr, gu, gb, do)


HN_HEADS = RET_HEADS + GLA_HEADS
HN_W = RET_DV


def _gate_col(h):
    return jnp.where(h < RET_HEADS, OFF_RG // HN_W + h, OFF_GG // HN_W + h - RET_HEADS)


def headnorm_fwd(name, oraw, z, w):
    T = oraw.shape[0]
    tr = _pick(T, _TILES)

    def body(o_ref, g_ref, w_ref, y_ref):
        y_ref[...] = (_rms(o_ref[...], w_ref[...]) * _silu_and_grad(g_ref[...])[0]).astype(y_ref.dtype)

    return pl.pallas_call(
        body, name=name, grid=(HN_HEADS, T // tr),
        in_specs=[pl.BlockSpec((tr, HN_W), lambda h, i: (i, h)),
                  pl.BlockSpec((tr, HN_W), lambda h, i: (i, _gate_col(h))),
                  pl.BlockSpec((1, HN_W), lambda h, i: (0, h))],
        out_specs=pl.BlockSpec((tr, HN_W), lambda h, i: (i, h)),
        out_shape=jax.ShapeDtypeStruct((T, HN_HEADS * HN_W), BF16),
        compiler_params=_params("arbitrary", "arbitrary"))(oraw, z, w)


def headnorm_bwd(name, oraw, z, w, dy):
    T = oraw.shape[0]
    tr = _pick(T, _TILES)

    def body(o_ref, g_ref, w_ref, dy_ref, do_ref, dg_ref, dw_ref):
        o, wv, dyv = o_ref[...], w_ref[...], dy_ref[...].astype(F32)
        silu, dsilu = _silu_and_grad(g_ref[...])
        n = _rms(o, wv)
        dg_ref[...] = (dyv * n * dsilu).astype(dg_ref.dtype)
        dx, dw = _rms_bwd(o, wv, dyv * silu)
        do_ref[...] = dx
        _accumulate(dw_ref, dw, pl.program_id(1) == 0)

    blk = pl.BlockSpec((tr, HN_W), lambda h, i: (i, h))
    return pl.pallas_call(
        body, name=name, grid=(HN_HEADS, T // tr),
        in_specs=[blk, pl.BlockSpec((tr, HN_W), lambda h, i: (i, _gate_col(h))),
                  pl.BlockSpec((1, HN_W), lambda h, i: (0, h)), blk],
        out_specs=[blk, blk, pl.BlockSpec((1, HN_W), lambda h, i: (0, h))],
        out_shape=[jax.ShapeDtypeStruct((T, HN_HEADS * HN_W), F32),
                   jax.ShapeDtypeStruct((T, HN_HEADS * HN_W), BF16),
                   jax.ShapeDtypeStruct((1, HN_HEADS * HN_W), F32)],
        compiler_params=_params("arbitrary", "arbitrary"))(oraw, z, w, dy)


N_MASKS = 4


def _check_mask_classes(T):
    for window, dilation in DILATED_BRANCHES[:-1]:
        assert window < (N_MASKS - 1) * BLK - (BLK - 1) and BLK % dilation == 0
    assert DILATED_BRANCHES[-1][0] >= T and BLK % DILATED_BRANCHES[-1][1] == 0


def _fill_masks(mult_ref, bias_ref):
    ri = lax.broadcasted_iota(jnp.int32, (BLK, BLK), 0)
    ci = lax.broadcasted_iota(jnp.int32, (BLK, BLK), 1)
    for d in range(N_MASKS):
        dt = d * BLK + ri - ci
        mult = jnp.zeros((BLK, BLK), F32)
        for window, dilation in DILATED_BRANCHES:
            hit = (dt >= 0) & (dt <= window) & ((dt & (dilation - 1)) == 0)
            mult = mult + hit.astype(F32)
        mult_ref[d] = mult
        bias_ref[d] = jnp.where(mult > 0, 0.0, -1e30)


def _mask_row(ref, qi):
    return jnp.concatenate([ref[min(qi - kb, N_MASKS - 1)] for kb in range(qi + 1)], axis=1)


def attn_fwd(name, qkv):
    T = qkv.shape[0]
    D = qkv.shape[1] // 3
    dh = D // ATT_HEADS
    nq = T // BLK
    scale = dh ** -0.5

    _check_mask_classes(T)

    def body(q_ref, k_ref, v_ref, o_ref, lse_ref, mult_ref, bias_ref):
        @pl.when(pl.program_id(0) == 0)
        def _():
            _fill_masks(mult_ref, bias_ref)

        for qi in range(nq):
            rows, n = slice(qi * BLK, (qi + 1) * BLK), (qi + 1) * BLK
            s = (_dot_nt(q_ref[rows, :], k_ref[0:n, :]) * scale
                 + _mask_row(bias_ref, qi))
            m = jnp.max(s, axis=-1, keepdims=True)
            p = _mask_row(mult_ref, qi) * jnp.exp(s - m)
            l = jnp.sum(p, axis=-1, keepdims=True)
            o_ref[rows, :] = (_dot(p, v_ref[0:n, :]) / l).astype(o_ref.dtype)
            lse_ref[rows, :] = jnp.broadcast_to(m + jnp.log(l), (BLK, LANE))

    return pl.pallas_call(
        body, name=name, grid=(ATT_HEADS,),
        in_specs=[pl.BlockSpec((T, dh), lambda h: (0, h)),
                  pl.BlockSpec((T, dh), lambda h: (0, ATT_HEADS + h)),
                  pl.BlockSpec((T, dh), lambda h: (0, 2 * ATT_HEADS + h))],
        out_specs=[pl.BlockSpec((T, dh), lambda h: (0, h)),
                   pl.BlockSpec((None, T, LANE), lambda h: (h, 0, 0))],
        out_shape=[jax.ShapeDtypeStruct((T, D), BF16), jax.ShapeDtypeStruct((ATT_HEADS, T, LANE), F32)],
        scratch_shapes=[pltpu.VMEM((N_MASKS, BLK, BLK), F32), pltpu.VMEM((N_MASKS, BLK, BLK), F32)],
        compiler_params=_params("arbitrary"))(qkv, qkv, qkv)


def attn_bwd(name, qkv, o, lse, do):
    T = qkv.shape[0]
    D = qkv.shape[1] // 3
    dh = D // ATT_HEADS
    nq = T // BLK
    scale = dh ** -0.5

    _check_mask_classes(T)

    def body(q_ref, k_ref, v_ref, o_ref, lse_ref, do_ref, dq_ref, dk_ref, dv_ref, dk_acc, dv_acc, mult_ref, bias_ref):
        @pl.when(pl.program_id(0) == 0)
        def _():
            _fill_masks(mult_ref, bias_ref)

        dk_acc[...] = jnp.zeros_like(dk_acc)
        dv_acc[...] = jnp.zeros_like(dv_acc)
        for qi in range(nq):
            rows, n = slice(qi * BLK, (qi + 1) * BLK), (qi + 1) * BLK
            q, dout = q_ref[rows, :], do_ref[rows, :]
            kk, vv = k_ref[0:n, :], v_ref[0:n, :]
            delta = jnp.sum(dout.astype(F32) * o_ref[rows, :].astype(F32), axis=-1, keepdims=True)
            lse = jnp.max(lse_ref[rows, :], axis=-1, keepdims=True)
            s = _dot_nt(q, kk) * scale + _mask_row(bias_ref, qi)
            p = _mask_row(mult_ref, qi) * jnp.exp(s - lse)
            ds = (p * (_dot_nt(dout, vv) - delta) * scale).astype(BF16)
            dq_ref[rows, :] = _dot(ds, kk).astype(dq_ref.dtype)
            dk_acc[0:n, :] += _dot_tn(ds, q)
            dv_acc[0:n, :] += _dot_tn(p, dout)
        dk_ref[...] = dk_acc[...].astype(dk_ref.dtype)
        dv_ref[...] = dv_acc[...].astype(dv_ref.dtype)

    full = pl.BlockSpec((T, dh), lambda h: (0, h))
    return pl.pallas_call(
        body, name=name, grid=(ATT_HEADS,),
        in_specs=[full, pl.BlockSpec((T, dh), lambda h: (0, ATT_HEADS + h)),
                  pl.BlockSpec((T, dh), lambda h: (0, 2 * ATT_HEADS + h)),
                  full, pl.BlockSpec((None, T, LANE), lambda h: (h, 0, 0)), full],
        out_specs=[full, full, full],
        out_shape=[jax.ShapeDtypeStruct((T, D), BF16)] * 3,
        scratch_shapes=[pltpu.VMEM((T, dh), F32), pltpu.VMEM((T, dh), F32),
                        pltpu.VMEM((N_MASKS, BLK, BLK), F32), pltpu.VMEM((N_MASKS, BLK, BLK), F32)],
        compiler_params=_params("arbitrary"))(qkv, qkv, qkv, o, lse, do)


def _mesh_pos():
    mx, my, mc = lax.axis_index("x"), lax.axis_index("y"), lax.axis_index("c")
    return mx, my, mc, 4 * mx + 2 * my + mc


def _peer(k, mx, my, mc):
    px, py, pc = mx ^ (k >> 2), my ^ ((k >> 1) & 1), mc ^ (k & 1)
    return (px, py, pc), 4 * px + 2 * py + pc


_SIBLING = 1
_OTHER_CHIPS = (4, 2, 6)
N_CHIP = N_DEV // 2
_PLANS = {"gather": (2, N_DEV - 1), "to_chips": (2, 1 + len(_OTHER_CHIPS)), "pass_on": (1, len(_OTHER_CHIPS)),
          "halves": (2, N_CHIP), "chip_sums": (2, len(_OTHER_CHIPS))}


def _copies(kind, items, send_sems, recv_sems):
    mx, my, mc, me = _mesh_pos()
    out = []

    def add(n, src, dst, peer):
        out.append(pltpu.make_async_remote_copy(
            src_ref=src, dst_ref=dst, send_sem=send_sems.at[n], recv_sem=recv_sems.at[n],
            device_id=peer, device_id_type=pl.DeviceIdType.MESH))

    per_item = _PLANS[kind][1]
    sibling = _peer(_SIBLING, mx, my, mc)[0]
    for i, refs in enumerate(items):
        n = i * per_item
        if kind == "gather":
            for k in range(1, N_DEV):
                add(n + k - 1, refs[0], refs[1].at[me], _peer(k, mx, my, mc)[0])
        elif kind == "to_chips":
            rows = refs[0].shape[0]
            dst = refs[1].at[me] if rows == refs[1].shape[1] else refs[1].at[me, pl.ds(0, rows)]
            for j, k in enumerate((_SIBLING,) + _OTHER_CHIPS):
                add(n + j, refs[0], dst, _peer(k, mx, my, mc)[0])
        elif kind == "pass_on":
            for j, k in enumerate(_OTHER_CHIPS):
                add(n + j, refs[0].at[me ^ k], refs[0].at[me ^ k], sibling)
        elif kind == "halves":
            for chip in range(N_CHIP):
                add(n + chip, refs[0].at[2 * chip + 1 - mc], refs[1].at[chip], sibling)
        else:
            for j, k in enumerate(_OTHER_CHIPS):
                peer, to = _peer(k, mx, my, mc)
                add(n + j, refs[0].at[to // 2], refs[1].at[me // 2], peer)
    return out


_HBM = pl.BlockSpec(memory_space=pltpu.HBM)
_SEM = pl.BlockSpec(memory_space=pltpu.SEMAPHORE)
_DATAFLOW = pltpu.SideEffectType.DATAFLOW_SIDE_EFFECTING


def exchange_call(name, waits, starts, deps=()):
    bufs, slot_of = [], {}

    def slots(items):
        out = []
        for item in items:
            for b in item:
                if id(b) not in slot_of:
                    slot_of[id(b)] = len(bufs)
                    bufs.append(b)
            out.append(tuple(slot_of[id(b)] for b in item))
        return out

    wait_plan = [(kind, slots(handle[0])) for kind, handle in waits]
    start_plan = [(kind, slots(items)) for kind, items in starts]
    wait_sems = [s for _, handle in waits for s in handle[1:]]
    n_buf, n_ws, n_start = len(bufs), len(wait_sems), len(starts)

    def body(*refs):
        buf_refs, sems_in = refs[:n_buf], refs[n_buf:n_buf + n_ws]
        outs = refs[n_buf + n_ws + len(deps):]
        pick = lambda plan: [tuple(buf_refs[s] for s in item) for item in plan]
        for wi, (kind, plan) in enumerate(wait_plan):
            copies = _copies(kind, pick(plan), sems_in[2 * wi], sems_in[2 * wi + 1])
            for cp in copies:
                cp.wait_send()
            for cp in copies:
                cp.wait_recv()
        for si, (kind, plan) in enumerate(start_plan):
            for cp in _copies(kind, pick(plan), outs[2 * si], outs[2 * si + 1]):
                cp.start()
        outs[-1][...] = jnp.zeros_like(outs[-1])

    hbm_bufs = [pltpu.with_memory_space_constraint(b, pltpu.HBM) for b in bufs]
    sem_shapes = []
    for kind, plan in start_plan:
        sem_shapes += [pltpu.SemaphoreType.DMA((len(plan) * _PLANS[kind][1],))] * 2
    outs = pl.pallas_call(
        body, name=name,
        out_shape=sem_shapes + [pltpu.HBM(b.shape, b.dtype) for b in bufs] + [jax.ShapeDtypeStruct((8, LANE), F32)],
        in_specs=[_HBM] * n_buf + [_SEM] * n_ws + [_ANY] * len(deps),
        out_specs=[_SEM] * (2 * n_start) + [_HBM] * n_buf + [pl.BlockSpec(memory_space=pltpu.VMEM)],
        input_output_aliases={i: 2 * n_start + i for i in range(n_buf)},
        compiler_params=pltpu.CompilerParams(has_side_effects=_DATAFLOW))(*hbm_bufs, *wait_sems, *deps)
    sems, thru, token = outs[:2 * n_start], outs[2 * n_start:-1], outs[-1]
    through = lambda plan: [tuple(thru[s] for s in item) for item in plan]
    waited = [through(plan) for _, plan in wait_plan]
    handles = [(through(plan), sems[2 * si], sems[2 * si + 1]) for si, (_, plan) in enumerate(start_plan)]
    return waited, handles, token


def gather_small(name, a, deps=()):
    def body(a_ref, *rest):
        o_ref, send_sems, recv_sems, local_sem = rest[len(deps):]
        me = _mesh_pos()[3]
        own = pltpu.make_async_copy(a_ref, o_ref.at[me], local_sem)
        own.start()
        copies = _copies("gather", [(a_ref, o_ref)], send_sems, recv_sems)
        for cp in copies:
            cp.start()
        for cp in copies:
            cp.wait_recv()
        for cp in copies:
            cp.wait_send()
        own.wait()

    return pl.pallas_call(
        body, name=name, in_specs=[_ANY] * (1 + len(deps)), out_specs=_ANY,
        out_shape=jax.ShapeDtypeStruct((N_DEV,) + a.shape, a.dtype),
        scratch_shapes=[pltpu.SemaphoreType.DMA((N_DEV - 1,)), pltpu.SemaphoreType.DMA((N_DEV - 1,)),
                        pltpu.SemaphoreType.DMA],
        compiler_params=pltpu.CompilerParams(has_side_effects=True))(a, *deps)


def _adamw_math(w, g, m, v):
    m2 = ADAM_B1 * m + (1.0 - ADAM_B1) * g
    v2 = ADAM_B2 * v + (1.0 - ADAM_B2) * (g * g)
    m_hat = m2 / (1.0 - ADAM_B1 ** ADAM_STEP)
    v_hat = v2 / (1.0 - ADAM_B2 ** ADAM_STEP)
    delta = -ADAM_LR * (m_hat / (jnp.sqrt(v_hat) + ADAM_EPS) + ADAM_WD * w)
    return delta, m2, v2


def chip_sum(name, a, half):
    _, r, c = a.shape
    tr = _pick(r, (256, 128, 64, 32, 16))

    def body(core_ref, a_ref, h_ref, o_ref):
        del core_ref
        o_ref[...] = (a_ref[...].astype(F32) + h_ref[...].astype(F32)).astype(o_ref.dtype)

    blk = pl.BlockSpec((N_CHIP, tr, c), lambda i, core: (0, i, 0))
    grid_spec = pltpu.PrefetchScalarGridSpec(
        num_scalar_prefetch=1, grid=(r // tr,),
        in_specs=[pl.BlockSpec((N_CHIP, None, tr, c), lambda i, core: (0, core[0], i, 0)), blk], out_specs=blk)
    return pl.pallas_call(
        body, name=name, grid_spec=grid_spec, out_shape=jax.ShapeDtypeStruct((N_CHIP, r, c), BF16),
        compiler_params=_params("parallel"))(lax.axis_index("c").astype(jnp.int32).reshape(1),
                                             a.reshape(N_CHIP, 2, r, c), half)


def adamw(name, w, m, v, l, land, own, prev=None):
    L, r, c = w.shape
    cp = land.shape[2]
    tr = _pick(r, (256, 176, 128, 64, 32, 16, 8))

    def body(w_ref, m_ref, v_ref, land_ref, own_ref, *rest):
        g_ref, d_ref, m2_ref, v2_ref = rest[-4:]
        chip = _mesh_pos()[3] // 2
        mine = own_ref[:, pl.ds(0, c)].astype(F32)
        g = None
        for s in range(N_CHIP):
            part = jnp.where(chip == s, mine, land_ref[s, :, pl.ds(0, c)].astype(F32))
            g = part if g is None else g + part
        delta, m2, v2 = _adamw_math(w_ref[...], g, m_ref[...], v_ref[...])
        g_ref[...] = g
        d_ref[...] = delta
        m2_ref[...] = m2
        v2_ref[...] = v2

    blk = pl.BlockSpec((None, tr, c), lambda i: (l, i, 0))
    shape = jax.ShapeDtypeStruct((L, r, c), F32)
    extra = [] if prev is None else list(prev)
    return pl.pallas_call(
        body, name=name, grid=(r // tr,),
        in_specs=[blk, blk, blk, pl.BlockSpec((N_CHIP, tr, cp), lambda i: (0, i, 0)),
                  pl.BlockSpec((None, tr, cp), lambda i: (_mesh_pos()[3] // 2, i, 0))] + [_ANY] * len(extra),
        out_specs=[blk] * 4, out_shape=[shape] * 4,
        input_output_aliases={5 + k: k for k in range(len(extra))},
        compiler_params=_params("parallel"))(w, m, v, land, own, *extra)


def adamw_small(name, w, m, v, parts):
    n = w.shape[1]

    def body(w_ref, m_ref, v_ref, p_ref, g_ref, d_ref, m2_ref, v2_ref):
        g = p_ref[0:1, :]
        for s in range(1, N_DEV):
            g = g + p_ref[s:s + 1, :]
        delta, m2, v2 = _adamw_math(w_ref[...], g, m_ref[...], v_ref[...])
        g_ref[...] = g
        d_ref[...] = delta
        m2_ref[...] = m2
        v2_ref[...] = v2

    shape = jax.ShapeDtypeStruct((1, n), F32)
    return pl.pallas_call(body, name=name, out_shape=[shape] * 4,
                          compiler_params=pltpu.CompilerParams(vmem_limit_bytes=VMEM_LIMIT_BYTES))(w, m, v, parts)


def _rope_tables(positions):
    half = RET_DK // 2
    inv_freq = 1.0 / jnp.power(RET_THETA_BASE, jnp.linspace(0.0, 1.0, half, dtype=F32))
    ang = positions.astype(F32)[:, None] * inv_freq
    cos, sin = jnp.cos(ang), jnp.sin(ang)
    cosf = jnp.repeat(cos, 2, axis=-1)
    sins = jnp.stack([-sin, sin], axis=-1).reshape(cosf.shape)
    return cosf, sins


def _pad_to(a, axis, size):
    pad = [(0, 0)] * a.ndim
    pad[axis] = (0, size - a.shape[axis])
    return jnp.pad(a, pad)


def _round_up(n, m):
    return -(-n // m) * m


def kernel(x, p, positions, attn_norm_w, ffn_norm_w, ple_norm_w, final_norm_w, ab_w_in, ab_gla_gate_up, ab_gla_gate_b, ab_ret_norm_w, ab_gla_norm_w, ab_w_out, c_w_qkv, c_w_out, ffn_w_gate, ffn_w_up, ffn_w_down, ple_w_proj, ple_w_gate, loss_target, m_attn_norm_w, m_ffn_norm_w, m_ple_norm_w, m_final_norm_w, m_ab_w_in, m_ab_gla_gate_up, m_ab_gla_gate_b, m_ab_ret_norm_w, m_ab_gla_norm_w, m_ab_w_out, m_c_w_qkv, m_c_w_out, m_ffn_w_gate, m_ffn_w_up, m_ffn_w_down, m_ple_w_proj, m_ple_w_gate, v_attn_norm_w, v_ffn_norm_w, v_ple_norm_w, v_final_norm_w, v_ab_w_in, v_ab_gla_gate_up, v_ab_gla_gate_b, v_ab_ret_norm_w, v_ab_gla_norm_w, v_ab_w_out, v_c_w_qkv, v_c_w_out, v_ffn_w_gate, v_ffn_w_up, v_ffn_w_down, v_ple_w_proj, v_ple_w_gate):
    T, D = x.shape[1], x.shape[2]
    depth = attn_norm_w.shape[0]
    assert ab_w_in.shape[0] == 1 and c_w_qkv.shape[0] == 1 and depth == 2, "one even and one odd layer"
    me = 4 * lax.axis_index("x") + 2 * lax.axis_index("y") + lax.axis_index("c")
    in_shard = ab_w_in.shape[2]
    in_width = in_shard * N_DEV
    assert in_width == OFF_LR + GLA_GATE_RANK
    fs = ffn_w_gate.shape[2]
    fp = _round_up(fs, LANE)
    gu_cols = ab_gla_gate_up.shape[2]

    bf = lambda a: a.astype(BF16)
    tr_ = lambda a: jnp.swapaxes(a, -1, -2)
    wg_t, wu_t = tr_(ffn_w_gate), tr_(ffn_w_up)
    srcs = {"w_in": bf(tr_(ab_w_in[0]))}
    group_keys = [["w_in"], ["gu", "w_oab"], ["wg0", "wu0"], ["wd0", "wpg0", "wpp0"], ["w_qkv", "w_oc"],
                  ["wg1", "wu1"], ["wd1", "wpg1", "wpp1"]]
    G_IN, G_OUT, G_QKV = 0, 1, 4
    g_ffn = lambda layer: (2, 3) if layer == 0 else (5, 6)

    def landing(key):
        a = srcs[key]
        rows = fp if key[:2] in ("wg", "wu", "wd") else a.shape[0]
        buf = lax.empty((N_DEV, rows) + a.shape[1:], a.dtype)
        if rows > a.shape[0]:
            zeros = jnp.zeros((N_DEV, rows - a.shape[0]) + a.shape[1:], a.dtype)
            buf = lax.dynamic_update_slice(buf, zeros, (0, a.shape[0]) + (0,) * (a.ndim - 1))
        return lax.dynamic_update_slice(buf, a[None], (me,) + (0,) * a.ndim)

    _, chip_handles, gather_token = exchange_call(
        "gather_start_in", [], [("to_chips", [(srcs[k], landing(k)) for k in group_keys[G_IN]])])
    (gather_token, w_out_, gu_, w_qkv_, w_oc_, wg_, wu_, wd_, wpg_, wpp_) = lax.optimization_barrier(
        (gather_token, ab_w_out, ab_gla_gate_up, c_w_qkv, c_w_out, wg_t, wu_t, ffn_w_down, ple_w_gate, ple_w_proj))
    srcs.update(w_oab=bf(w_out_[0]), gu=gu_[0], w_qkv=bf(w_qkv_[0]), w_oc=bf(w_oc_[0]))
    for l in range(depth):
        srcs[f"wg{l}"] = bf(wg_[l])
        srcs[f"wu{l}"] = bf(wu_[l])
        srcs[f"wd{l}"] = bf(wd_[l])
        srcs[f"wpg{l}"] = bf(wpg_[l])
        srcs[f"wpp{l}"] = bf(wpp_[l])
    _, more, gather_token = exchange_call(
        "gather_start", [], [("to_chips", [(srcs[k], landing(k)) for k in keys]) for keys in group_keys[1:]],
        deps=(gather_token,))
    chip_handles = chip_handles + more
    weights = {}

    def gather_wait(gi, dep):
        lands = [(land,) for _, land in chip_handles[gi][0]]
        _, (passing,), _ = exchange_call(
            f"gather{gi}_pass", [("to_chips", chip_handles[gi])], [("pass_on", lands)], deps=(dep,))
        (complete,), _, _ = exchange_call(f"gather{gi}_done", [("pass_on", passing)], [])
        weights.update(zip(group_keys[gi], [land for (land,) in complete]))

    gb = ab_gla_gate_b
    hn_w = jnp.concatenate([ab_ret_norm_w, ab_gla_norm_w], axis=1)
    cosf, sins = _rope_tables(positions[0])
    p_bf = bf(p[:, 0])

    xs = x[0]
    saved = []
    for i in range(depth):
        nm = f"l{i}_"
        w_attn, w_ffn, w_ple = attn_norm_w[i:i + 1], ffn_norm_w[i:i + 1], ple_norm_w[i:i + 1]
        (xn,) = rowwise(nm + "norm_attn", lambda a, w: (_rms(a, w),), T, [("row", xs), ("full", w_attn)],
                        [("row", D, BF16)], deps=(gather_token,) if i == 0 else ())
        if i % 2 == 0:
            gather_wait(G_IN, xn)
            w_in_t = weights["w_in"].reshape(1, 1, in_width, D)
            w_lr_t = _pad_to(w_in_t[0, 0, OFF_LR:], 0, LANE).reshape(1, 1, LANE, D)
            z = mmt_fwd(nm + "mm_in", xn, w_in_t, 0, F32, n=OFF_LR)
            glr = mmt_fwd(nm + "mm_lr", xn, w_lr_t, 0, F32)
            oraw = retention_fwd(nm + "ret_fwd", z, cosf, sins, RET_V + GLA_V)
            gather_wait(G_OUT, oraw)
            w_oab = weights["w_oab"].reshape(1, 1, D, D)
            gu_full = _pad_to(weights["gu"].transpose(1, 0, 2).reshape(GLA_GATE_RANK, GLA_QK), 0, LANE)
            oraw = gla_fwd(nm + "gla_fwd", z, glr, gu_full, gb, oraw)
            o = headnorm_fwd(nm + "headnorm_fwd", oraw, z, hn_w)
            mix = mm_nn(nm + "mm_out", o, w_oab, 0, F32)
            mixer_saved = (z, glr, oraw, o)
        else:
            gather_wait(G_QKV, xn)
            w_qkv = weights["w_qkv"].reshape((1,) + weights["w_qkv"].shape)
            w_oc = weights["w_oc"].reshape(1, 1, D, D)
            qkv = mm_nn(nm + "mm_qkv", xn, w_qkv, 0, BF16)
            o, lse = attn_fwd(nm + "attn_fwd", qkv)
            mix = mm_nn(nm + "mm_out", o, w_oc, 0, F32)
            mixer_saved = (qkv, o, lse)
        h1, hn = rowwise(nm + "add_norm_ffn", lambda a, b, w: (a + b, _rms(a + b, w)), T,
                         [("row", xs), ("row", mix), ("full", w_ffn)], [("row", D, F32), ("row", D, BF16)])
        gather_wait(g_ffn(i)[0], hn)
        wg = weights[f"wg{i}"].reshape(1, N_DEV, fp, D)
        wu = weights[f"wu{i}"].reshape(1, N_DEV, fp, D)
        g, u, act = ffn_gate_up(nm + "ffn_gate_up", hn, wg, wu)
        gather_wait(g_ffn(i)[1], act)
        wd = weights[f"wd{i}"].reshape(1, 1, N_DEV * fp, D)
        wpg = weights[f"wpg{i}"].reshape(1, 1, D, D)
        wpp = weights[f"wpp{i}"].reshape((1,) + weights[f"wpp{i}"].shape)
        f = mm_nn(nm + "mm_down", act, wd, 0, F32)
        h2, pn = rowwise(nm + "add_norm_ple", lambda a, b, w: (a + b, _rms(a + b, w)), T,
                         [("row", h1), ("row", f), ("full", w_ple)], [("row", D, F32), ("row", D, BF16)])
        s = mm_nn(nm + "mm_ple_gate", pn, wpg, 0, F32)
        e = mm_nn(nm + "mm_ple_proj", p_bf[i], wpp, 0, F32)
        (x_next,) = rowwise(nm + "ple_out", lambda a, b, c: (a + _sigmoid(b) * c,), T,
                            [("row", h2), ("row", s), ("row", e)], [("row", D, F32)])
        mixer_w = (w_in_t, w_lr_t, w_oab, gu_full) if i % 2 == 0 else (w_qkv, w_oc)
        saved.append((xs, xn, mixer_saved, mixer_w, (wg, wu, wd, wpg), h1, hn, g, u, act, h2, pn, s, e))
        xs = x_next

    def loss_fn(a, w, t):
        diff = _rms(a, w) - t
        dx, dw = _rms_bwd(a, w, diff * (1.0 / D))
        part = 0.5 * jnp.sum(jnp.mean(diff * diff, axis=-1, keepdims=True), axis=0, keepdims=True)
        return dx, dw, jnp.broadcast_to(part, (1, LANE))

    dx, d_final_w, loss_part = rowwise("loss_head", loss_fn, T,
                                       [("row", xs), ("full", final_norm_w[None, :]), ("row", loss_target[0])],
                                       [("row", D, F32), ("acc", D), ("acc", LANE)])
    loss = lax.psum(loss_part[0, 0], ("x", "y", "c"))

    grads = {}
    on_chip = []
    scatters = []

    def scatter_start(name, keys, deps=()):
        waits = [("halves", on_chip[0][1])] if on_chip else []
        starts = [("halves", [(grads[k], lax.empty((N_CHIP,) + grads[k].shape[1:], BF16)) for k in keys])] if keys else []
        waited, handles, token = exchange_call(name, waits, starts, deps=deps)
        if on_chip:
            done_keys, _ = on_chip.pop()
            sums = [chip_sum(f"{name}_sum{j}", a, half) for j, (a, half) in enumerate(waited[0])]
            _, (handle,), token = exchange_call(
                name + "_chips", [], [("chip_sums", [(cs, lax.empty(cs.shape, BF16)) for cs in sums])])
            scatters.append((done_keys, handle))
        if keys:
            on_chip.append((keys, handles[0]))
        return token

    d_attn_w, d_ffn_w, d_ple_w = [None] * depth, [None] * depth, [None] * depth
    for i in reversed(range(depth)):
        nm = f"l{i}_b_"
        xs_i, xn, mixer_saved, mixer_w, (wg, wu, wd, wpg), h1, hn, g, u, act, h2, pn, s, e = saved[i]
        w_attn, w_ffn, w_ple = attn_norm_w[i:i + 1], ffn_norm_w[i:i + 1], ple_norm_w[i:i + 1]

        def ple_bwd(d, sv, ev):
            gate = _sigmoid(sv)
            return d * gate, d * ev * gate * (1.0 - gate)

        de, ds = rowwise(nm + "ple_out", ple_bwd, T, [("row", dx), ("row", s), ("row", e)],
                         [("row", D, BF16), ("row", D, BF16)], deps=(loss.reshape(1, 1),) if i == depth - 1 else ())
        grads[("ple_w_proj", i)] = mm_tn(nm + "mm_ple_proj_w", p_bf[i], de, N_DEV, BF16)
        grads[("ple_w_gate", i)] = mm_tn(nm + "mm_ple_gate_w", pn, ds, 1, BF16).reshape(N_DEV, D // N_DEV, D)
        dpn = mm_nt(nm + "mm_ple_gate_x", ds, wpg, 0, F32)

        def norm_bwd_add(a, w, dn, dres):
            dxx, dw = _rms_bwd(a, w, dn)
            tot = dres + dxx
            return tot, tot, dw

        dh2, dh2_bf, d_ple_w[i] = rowwise(nm + "norm_ple", norm_bwd_add, T,
                                          [("row", h2), ("full", w_ple), ("row", dpn), ("row", dx)],
                                          [("row", D, F32), ("row", D, BF16), ("acc", D)])
        grads[("ffn_w_down", i)] = mm_tn(nm + "mm_down_w", act, dh2_bf, 1, BF16).reshape(N_DEV, fp, D)
        token = scatter_start(nm + "scatter_ple_down", [("ple_w_proj", i), ("ple_w_gate", i), ("ffn_w_down", i)])
        dg, du = ffn_down_bwd(nm + "ffn_down_x", dh2_bf, wd, g, u, deps=(token,))
        grads[("ffn_w_gate", i)] = mmt_dw(nm + "mm_gate_w", dg, hn, N_DEV, BF16)
        grads[("ffn_w_up", i)] = mmt_dw(nm + "mm_up_w", du, hn, N_DEV, BF16)
        token = scatter_start(nm + "scatter_gate_up", [("ffn_w_gate", i), ("ffn_w_up", i)])
        dhn_g = mmt_dx_wide(nm + "mm_gate_x", dg, wg, F32, deps=(token,))
        dhn_u = mmt_dx_wide(nm + "mm_up_x", du, wu, F32)

        def norm_bwd_add2(a, w, dn1, dn2, dres):
            dxx, dw = _rms_bwd(a, w, dn1 + dn2)
            tot = dres + dxx
            return tot, tot, dw

        dh1, dh1_bf, d_ffn_w[i] = rowwise(nm + "norm_ffn", norm_bwd_add2, T,
                                          [("row", h1), ("full", w_ffn), ("row", dhn_g), ("row", dhn_u), ("row", dh2)],
                                          [("row", D, F32), ("row", D, BF16), ("acc", D)])
        if i % 2 == 0:
            z, glr, oraw, o = mixer_saved
            w_in_t, w_lr_t, w_oab, gu_full = mixer_w
            grads[("ab_w_out", 0)] = mm_tn(nm + "mm_out_w", o, dh1_bf, 1, BF16).reshape(N_DEV, D // N_DEV, D)
            token = scatter_start(nm + "scatter_out", [("ab_w_out", 0)])
            do = mm_nt(nm + "mm_out_x", dh1_bf, w_oab, 0, F32, deps=(token,))
            d_oraw, d_gates, d_hn_w = headnorm_bwd(nm + "headnorm", oraw, z, hn_w, do)
            d_rq, d_rk, d_rv = retention_bwd(nm + "ret", z, cosf, sins, d_oraw)
            d_gq, d_gk, d_gv, d_glr4, d_gu, d_gb = gla_bwd(nm + "gla", z, glr, gu_full, gb, d_oraw)
            dz = jnp.concatenate([d_rq, d_rk, d_rv, d_gates[:, :RET_V], d_gq, d_gk, d_gv, d_gates[:, RET_V:]], axis=1)
            (d_glr,) = rowwise(nm + "sum_lr", lambda *a: (a[0] + a[1] + a[2] + a[3],), T,
                               [("row", d_glr4[hh]) for hh in range(GLA_HEADS)], [("row", LANE, BF16)])
            dw_main = mm_tn(nm + "mm_in_w", xn, dz, 1, BF16)[0]
            dw_lr = mm_tn(nm + "mm_lr_w", xn, d_glr, 1, BF16)[0]
            dw_in = jnp.concatenate([dw_main, dw_lr[:, :GLA_GATE_RANK]], axis=1)
            grads[("ab_w_in", 0)] = dw_in.reshape(D, N_DEV, in_shard).transpose(1, 0, 2)
            token = scatter_start(nm + "scatter_in", [("ab_w_in", 0)])
            dxn_a = mmt_dx_wide(nm + "mm_in_x", dz, w_in_t, F32, n=OFF_LR, deps=(token,))
            token = scatter_start(nm + "scatter_in_on", [], deps=(dxn_a,))
            dxn_b = mmt_dx(nm + "mm_lr_x", d_glr, w_lr_t, 0, F32, deps=(token,))
        else:
            qkv, o, lse = mixer_saved
            w_qkv, w_oc = mixer_w
            grads[("c_w_out", 0)] = mm_tn(nm + "mm_out_w", o, dh1_bf, 1, BF16).reshape(N_DEV, D // N_DEV, D)
            do = mm_nt(nm + "mm_out_x", dh1_bf, w_oc, 0, BF16)
            dq, dk, dv = attn_bwd(nm + "attn", qkv, o, lse, do)
            dqkv = jnp.concatenate([dq, dk, dv], axis=1)
            grads[("c_w_qkv", 0)] = mm_tn(nm + "mm_qkv_w", xn, dqkv, N_DEV, BF16)
            token = scatter_start(nm + "scatter_attn", [("c_w_out", 0), ("c_w_qkv", 0)])
            dxn_a = mm_nt_wide(nm + "mm_qkv_x", dqkv, w_qkv, F32, deps=(token,))
            dxn_b = None
        if dxn_b is None:
            dx, _, d_attn_w[i] = rowwise(nm + "norm_attn", norm_bwd_add, T,
                                         [("row", xs_i), ("full", w_attn), ("row", dxn_a), ("row", dh1)],
                                         [("row", D, F32), ("row", D, BF16), ("acc", D)])
        else:
            dx, _, d_attn_w[i] = rowwise(nm + "norm_attn", norm_bwd_add2, T,
                                         [("row", xs_i), ("full", w_attn), ("row", dxn_a), ("row", dxn_b), ("row", dh1)],
                                         [("row", D, F32), ("row", D, BF16), ("acc", D)])

    small_names = ["attn_norm_w", "ffn_norm_w", "ple_norm_w", "final_norm_w", "ab_gla_gate_b", "ab_ret_norm_w",
                   "ab_gla_norm_w"]
    small_grads = [jnp.concatenate(d_attn_w, 0), jnp.concatenate(d_ffn_w, 0), jnp.concatenate(d_ple_w, 0), d_final_w[0],
                   d_gb, d_hn_w[:, :RET_V], d_hn_w[:, RET_V:]]
    small_w = [attn_norm_w, ffn_norm_w, ple_norm_w, final_norm_w, ab_gla_gate_b, ab_ret_norm_w, ab_gla_norm_w]
    small_m = [m_attn_norm_w, m_ffn_norm_w, m_ple_norm_w, m_final_norm_w, m_ab_gla_gate_b, m_ab_ret_norm_w, m_ab_gla_norm_w]
    small_v = [v_attn_norm_w, v_ffn_norm_w, v_ple_norm_w, v_final_norm_w, v_ab_gla_gate_b, v_ab_ret_norm_w, v_ab_gla_norm_w]
    sizes = [int(np.prod(a.shape)) for a in small_w]
    n_gu = GLA_GATE_RANK * GLA_QK
    n_small = _round_up(sum(sizes) + n_gu, LANE)
    pack = lambda parts: _pad_to(jnp.concatenate([a.reshape(-1) for a in parts]), 0, n_small)[None, :]
    small_part = pack(small_grads + [d_gu[:GLA_GATE_RANK]])

    big_w = dict(ab_w_in=(ab_w_in, m_ab_w_in, v_ab_w_in), ab_w_out=(ab_w_out, m_ab_w_out, v_ab_w_out),
                 c_w_qkv=(c_w_qkv, m_c_w_qkv, v_c_w_qkv), c_w_out=(c_w_out, m_c_w_out, v_c_w_out),
                 ffn_w_gate=(wg_t, tr_(m_ffn_w_gate), tr_(v_ffn_w_gate)),
                 ffn_w_up=(wu_t, tr_(m_ffn_w_up), tr_(v_ffn_w_up)),
                 ffn_w_down=(ffn_w_down, m_ffn_w_down, v_ffn_w_down), ple_w_proj=(ple_w_proj, m_ple_w_proj, v_ple_w_proj),
                 ple_w_gate=(ple_w_gate, m_ple_w_gate, v_ple_w_gate))
    if on_chip:
        scatter_start("scatter_last", [], deps=(dx,))
    results, last = {}, dx
    for gi, (keys, handle) in enumerate(scatters):
        (arrived,), _, _ = exchange_call(f"scatter_wait{gi}", [("chip_sums", handle)], [], deps=(last,))
        for (n, l), (own, land) in zip(keys, arrived):
            results[n] = adamw(f"adamw_{n}{l}", *big_w[n], l, land, own, prev=results.get(n))
            last = results[n][0]
    for n in ("ffn_w_gate", "ffn_w_up"):
        results[n] = [tr_(a) for a in results[n]]
    small_parts = gather_small("gather_small", small_part, deps=(last,)).reshape(N_DEV, n_small)

    gu_off = sum(sizes)
    own_cols = lambda a: lax.dynamic_slice_in_dim(a.reshape(GLA_GATE_RANK, GLA_QK), me * gu_cols, gu_cols, axis=1)
    small_res = adamw_small("adamw_small", pack(small_w + [jnp.zeros((n_gu,), F32)]),
                            pack(small_m + [jnp.zeros((n_gu,), F32)]), pack(small_v + [jnp.ones((n_gu,), F32)]),
                            small_parts)
    g_gu_full = small_res[0][0, gu_off:gu_off + n_gu]
    g_gu = own_cols(g_gu_full)[None]
    gu_res = adamw_small("adamw_gate_up", *[_pad_to(a.reshape(1, -1), 1, _round_up(a.size, LANE)) for a in
                                            (ab_gla_gate_up, m_ab_gla_gate_up, v_ab_gla_gate_up)],
                         jnp.concatenate([_pad_to(g_gu.reshape(1, -1), 1, _round_up(g_gu.size, LANE)),
                                          jnp.zeros((N_DEV - 1, _round_up(g_gu.size, LANE)), F32)], axis=0))
    for k in range(4):
        off = 0
        for n, a, sz in zip(small_names, small_w, sizes):
            results.setdefault(n, [None] * 4)[k] = small_res[k][0, off:off + sz].reshape(a.shape)
            off += sz
        results.setdefault("ab_gla_gate_up", [None] * 4)[k] = gu_res[k][0, :g_gu.size].reshape(ab_gla_gate_up.shape)

    order = ["attn_norm_w", "ffn_norm_w", "ple_norm_w", "final_norm_w", "ab_w_in", "ab_gla_gate_up", "ab_gla_gate_b",
             "ab_ret_norm_w", "ab_gla_norm_w", "ab_w_out", "c_w_qkv", "c_w_out", "ffn_w_gate", "ffn_w_up", "ffn_w_down",
             "ple_w_proj", "ple_w_gate"]
    return (loss, dx[None], *[results[n][0] for n in order], *[results[n][1] for n in order],
            *[results[n][2] for n in order], *[results[n][3] for n in order])
```

```python
import math

import numpy as np
import jax
import jax.numpy as jnp
from jax import lax
from jax.experimental import pallas as pl
from jax.experimental.pallas import tpu as pltpu

F32 = jnp.float32
BF16 = jnp.bfloat16
HIGHEST = lax.Precision.HIGHEST

N_DEV = 8
VMEM_LIMIT_BYTES = 48 * 1024 * 1024
LANE = 128
NORM_EPS = 1e-6

RET_HEADS, RET_DK, RET_DV = 4, 256, 256
RET_THETA_BASE = 10000.0
GLA_HEADS, GLA_DK, GLA_DV = 4, 128, 256
GLA_GATE_RANK = 16
GLA_GATE_NORM = 16.0
CHUNK = 64
ATT_HEADS = 16
DILATED_BRANCHES = ((128, 1), (512, 4), (2048, 16))
BLK = 256

ADAM_LR, ADAM_B1, ADAM_B2, ADAM_EPS, ADAM_WD, ADAM_STEP = 0.001, 0.9, 0.999, 1e-08, 0.01, 10

RET_QK = RET_HEADS * RET_DK
RET_V = RET_HEADS * RET_DV
GLA_QK = GLA_HEADS * GLA_DK
GLA_V = GLA_HEADS * GLA_DV
OFF_RQ, OFF_RK, OFF_RV, OFF_RG = 0, RET_QK, 2 * RET_QK, 2 * RET_QK + RET_V
OFF_GQ = OFF_RG + RET_V
OFF_GK = OFF_GQ + GLA_QK
OFF_GV = OFF_GK + GLA_QK
OFF_GG = OFF_GV + GLA_V
OFF_LR = OFF_GG + GLA_V


def _params(*sem):
    return pltpu.CompilerParams(dimension_semantics=sem or None, vmem_limit_bytes=VMEM_LIMIT_BYTES)


def _pick(n, cands):
    for c in cands:
        if n % c == 0:
            return c
    raise ValueError(f"no tile for {n} in {cands}")


_NN = (((1,), (0,)), ((), ()))
_NT = (((1,), (1,)), ((), ()))
_TN = (((0,), (0,)), ((), ()))
_ANY = pl.BlockSpec(memory_space=pl.ANY)
MAX_CONTRACT = 2048
_TILES = (1024, 768, 512, 256, 128)


def _mm_call(name, dims, grid, in_specs, out_spec, out_shape, args, deps=()):
    steps = grid[2]
    assert steps == 1 or out_shape.dtype == F32

    def body(a_ref, b_ref, *rest):
        o_ref = rest[len(deps)]
        part = lax.dot_general(a_ref[...].astype(BF16), b_ref[...].astype(BF16), dims, preferred_element_type=F32)
        if steps == 1:
            o_ref[...] = part.astype(o_ref.dtype)
        else:
            _accumulate(o_ref, part, pl.program_id(2) == 0)

    return pl.pallas_call(
        body, name=name, grid=grid, in_specs=list(in_specs) + [_ANY] * len(deps), out_specs=out_spec,
        out_shape=out_shape, compiler_params=_params("parallel", "parallel", "arbitrary"))(*args, *deps)


def mm_nn(name, a, w, l, out_dtype, deps=()):
    _, J, K, n = w.shape
    M = a.shape[0]
    tm, tn, tk = _pick(M, _TILES), _pick(n, _TILES), _pick(K, (MAX_CONTRACT,) + _TILES)
    nt = n // tn
    return _mm_call(
        name, _NN, (M // tm, J * nt, K // tk),
        [pl.BlockSpec((tm, tk), lambda i, j, k: (i, k)),
         pl.BlockSpec((None, None, tk, tn), lambda i, j, k: (l, j // nt, k, j % nt))],
        pl.BlockSpec((tm, tn), lambda i, j, k: (i, j)),
        jax.ShapeDtypeStruct((M, J * n), out_dtype), (a, w), deps)


def mm_nt(name, a, w, l, out_dtype, deps=()):
    _, J, K, n = w.shape
    M = a.shape[0]
    tm, tq, tc = _pick(M, _TILES), _pick(K, _TILES), _pick(n, (MAX_CONTRACT,) + _TILES)
    nc = n // tc
    return _mm_call(
        name, _NT, (M // tm, K // tq, J * nc),
        [pl.BlockSpec((tm, tc), lambda i, q, c: (i, c)),
         pl.BlockSpec((None, None, tq, tc), lambda i, q, c: (l, c // nc, q, c % nc))],
        pl.BlockSpec((tm, tq), lambda i, q, c: (i, q)),
        jax.ShapeDtypeStruct((M, K), out_dtype), (a, w), deps)


def mm_tn(name, x, dy, J, out_dtype, deps=()):
    M, K = x.shape
    n = dy.shape[1] // J
    tp, tn = _pick(K, _TILES), _pick(n, _TILES)
    nt = n // tn
    assert M <= MAX_CONTRACT
    return _mm_call(
        name, _TN, (K // tp, J * nt, 1),
        [pl.BlockSpec((M, tp), lambda i, j, r: (0, i)),
         pl.BlockSpec((M, tn), lambda i, j, r: (0, j))],
        pl.BlockSpec((None, tp, tn), lambda i, j, r: (j // nt, i, j % nt)),
        jax.ShapeDtypeStruct((J, K, n), out_dtype), (x, dy), deps)


def mmt_fwd(name, a, wt, l, out_dtype, n=None, deps=()):
    _, J, rows, K = wt.shape
    n = rows if n is None else n
    M = a.shape[0]
    tm, tn = _pick(M, _TILES), _pick(n, _TILES)
    nt = n // tn
    assert K <= MAX_CONTRACT
    return _mm_call(
        name, _NT, (M // tm, J * nt, 1),
        [pl.BlockSpec((tm, K), lambda i, j, k: (i, 0)),
         pl.BlockSpec((None, None, tn, K), lambda i, j, k: (l, j // nt, j % nt, 0))],
        pl.BlockSpec((tm, tn), lambda i, j, k: (i, j)),
        jax.ShapeDtypeStruct((M, J * n), out_dtype), (a, wt), deps)


def mmt_dx(name, dy, wt, l, out_dtype, n=None, deps=()):
    _, J, rows, K = wt.shape
    n = rows if n is None else n
    M = dy.shape[0]
    tm, tq, tc = _pick(M, _TILES), _pick(K, _TILES), _pick(n, _TILES)
    nc = n // tc
    return _mm_call(
        name, _NN, (M // tm, K // tq, J * nc),
        [pl.BlockSpec((tm, tc), lambda i, q, c: (i, c)),
         pl.BlockSpec((None, None, tc, tq), lambda i, q, c: (l, c // nc, c % nc, q))],
        pl.BlockSpec((tm, tq), lambda i, q, c: (i, q)),
        jax.ShapeDtypeStruct((M, K), out_dtype), (dy, wt), deps)


WIDE_TILE = 512


def _wide_call(name, body, M, K, a, w, a_spec, w_spec, out_dtype, deps):
    def kernel_body(a_ref, w_ref, *rest):
        o_ref = rest[len(deps)]
        o_ref[...] = body(a_ref, w_ref).astype(o_ref.dtype)

    return pl.pallas_call(
        kernel_body, name=name, grid=(M // WIDE_TILE, K // WIDE_TILE),
        in_specs=[a_spec, w_spec] + [_ANY] * len(deps),
        out_specs=pl.BlockSpec((WIDE_TILE, WIDE_TILE), lambda i, q: (i, q)),
        out_shape=jax.ShapeDtypeStruct((M, K), out_dtype),
        compiler_params=_params("parallel", "parallel"))(a, w, *deps)


def mmt_dx_wide(name, dy, wt, out_dtype, n=None, deps=()):
    _, J, rows, K = wt.shape
    n = rows if n is None else n
    M = dy.shape[0]

    def body(dy_ref, w_ref):
        return jnp.dot(dy_ref[...].astype(BF16), w_ref[...].reshape(J * n, WIDE_TILE), preferred_element_type=F32)

    return _wide_call(name, body, M, K, dy, wt,
                      pl.BlockSpec((WIDE_TILE, J * n), lambda i, q: (i, 0)),
                      pl.BlockSpec((None, J, n, WIDE_TILE), lambda i, q: (0, 0, 0, q)), out_dtype, deps)


def mm_nt_wide(name, a, w, out_dtype, deps=()):
    _, J, K, n = w.shape
    M = a.shape[0]

    def body(a_ref, w_ref):
        acc = None
        for j in range(J):
            part = lax.dot_general(a_ref[:, j * n:(j + 1) * n].astype(BF16), w_ref[j], _NT, preferred_element_type=F32)
            acc = part if acc is None else acc + part
        return acc

    return _wide_call(name, body, M, K, a, w,
                      pl.BlockSpec((WIDE_TILE, J * n), lambda i, q: (i, 0)),
                      pl.BlockSpec((None, J, WIDE_TILE, n), lambda i, q: (0, 0, q, 0)), out_dtype, deps)


def mmt_dw(name, dy, x, J, out_dtype, deps=()):
    M, K = x.shape
    n = dy.shape[1] // J
    tn, tp = _pick(n, _TILES), _pick(K, _TILES)
    nt = n // tn
    assert M <= MAX_CONTRACT
    return _mm_call(
        name, _TN, (J * nt, K // tp, 1),
        [pl.BlockSpec((M, tn), lambda j, i, r: (0, j)),
         pl.BlockSpec((M, tp), lambda j, i, r: (0, i))],
        pl.BlockSpec((None, tn, tp), lambda j, i, r: (j // nt, j % nt, i)),
        jax.ShapeDtypeStruct((J, n, K), out_dtype), (dy, x), deps)


def ffn_gate_up(name, a, wg, wu):
    _, J, n, K = wg.shape
    M = a.shape[0]
    tm, tn = _pick(M, _TILES), _pick(n, _TILES)
    nt = n // tn
    assert K <= MAX_CONTRACT

    def body(a_ref, wg_ref, wu_ref, g_ref, u_ref, act_ref):
        x = a_ref[...]
        g = lax.dot_general(x, wg_ref[...], _NT, preferred_element_type=F32)
        u = lax.dot_general(x, wu_ref[...], _NT, preferred_element_type=F32)
        g_ref[...] = g.astype(g_ref.dtype)
        u_ref[...] = u.astype(u_ref.dtype)
        act_ref[...] = (_silu_and_grad(g)[0] * u).astype(act_ref.dtype)

    w_spec = pl.BlockSpec((None, None, tn, K), lambda i, j: (0, j // nt, j % nt, 0))
    out = pl.BlockSpec((tm, tn), lambda i, j: (i, j))
    return pl.pallas_call(
        body, name=name, grid=(M // tm, J * nt),
        in_specs=[pl.BlockSpec((tm, K), lambda i, j: (i, 0)), w_spec, w_spec],
        out_specs=[out] * 3, out_shape=[jax.ShapeDtypeStruct((M, J * n), BF16)] * 3,
        compiler_params=_params("parallel", "parallel"))(a, wg, wu)


def mm_add_norm(name, a, w, res, norm_w):
    _, _, K, N = w.shape
    M = a.shape[0]
    tm, tk = _pick(M, (WIDE_TILE, 256)), _pick(K, (1024, 512, 256))
    steps = K // tk

    def body(a_ref, w_ref, res_ref, nw_ref, h_ref, hn_ref):
        k = pl.program_id(1)
        part = jnp.dot(a_ref[...], w_ref[...], preferred_element_type=F32)
        _accumulate(h_ref, part, k == 0)

        @pl.when(k == steps - 1)
        def _():
            h = h_ref[...] + res_ref[...]
            h_ref[...] = h
            hn_ref[...] = _rms(h, nw_ref[...]).astype(hn_ref.dtype)

    rows = pl.BlockSpec((tm, N), lambda i, k: (i, 0))
    return pl.pallas_call(
        body, name=name, grid=(M // tm, steps),
        in_specs=[pl.BlockSpec((tm, tk), lambda i, k: (i, k)),
                  pl.BlockSpec((None, None, tk, N), lambda i, k: (0, 0, k, 0)), rows,
                  pl.BlockSpec((1, N), lambda i, k: (0, 0))],
        out_specs=[rows, rows],
        out_shape=[jax.ShapeDtypeStruct((M, N), F32), jax.ShapeDtypeStruct((M, N), BF16)],
        compiler_params=_params("parallel", "arbitrary"))(a, w, res, norm_w)


def ple_fwd(name, pn, wpg, p_in, wpp, h):
    _, J, P, n = wpp.shape
    M, D = h.shape
    tm, tn = _pick(M, (WIDE_TILE, 256)), _pick(D, _TILES)
    per_tile = tn // n

    def body(pn_ref, wg_ref, p_ref, wp_ref, h_ref, x_ref, s_ref, e_ref):
        s = jnp.dot(pn_ref[...], wg_ref[...], preferred_element_type=F32)
        p_blk = p_ref[...]
        e = jnp.concatenate([jnp.dot(p_blk, wp_ref[j], preferred_element_type=F32) for j in range(per_tile)], axis=1)
        s_ref[...] = s
        e_ref[...] = e
        x_ref[...] = h_ref[...] + _sigmoid(s) * e

    tile = pl.BlockSpec((tm, tn), lambda i, j: (i, j))
    return pl.pallas_call(
        body, name=name, grid=(M // tm, D // tn),
        in_specs=[pl.BlockSpec((tm, D), lambda i, j: (i, 0)),
                  pl.BlockSpec((None, None, D, tn), lambda i, j: (0, 0, 0, j)),
                  pl.BlockSpec((tm, P), lambda i, j: (i, 0)),
                  pl.BlockSpec((None, per_tile, P, n), lambda i, j: (0, j, 0, 0)), tile],
        out_specs=[tile] * 3, out_shape=[jax.ShapeDtypeStruct((M, D), F32)] * 3,
        compiler_params=_params("parallel", "parallel"))(pn, wpg, p_in, wpp, h)


def ffn_down_bwd(name, dy, wd, g, u, deps=()):
    _, _, K, n = wd.shape
    M = dy.shape[0]
    tm, tq = _pick(M, _TILES), _pick(K, _TILES)
    assert n <= MAX_CONTRACT

    def body(dy_ref, w_ref, g_ref, u_ref, *rest):
        dg_ref, du_ref = rest[len(deps):]
        dact = lax.dot_general(dy_ref[...], w_ref[...], _NT, preferred_element_type=F32)
        silu, dsilu = _silu_and_grad(g_ref[...].astype(F32))
        dg_ref[...] = (dact * u_ref[...].astype(F32) * dsilu).astype(dg_ref.dtype)
        du_ref[...] = (dact * silu).astype(du_ref.dtype)

    blk = pl.BlockSpec((tm, tq), lambda i, q: (i, q))
    return pl.pallas_call(
        body, name=name, grid=(M // tm, K // tq),
        in_specs=[pl.BlockSpec((tm, n), lambda i, q: (i, 0)),
                  pl.BlockSpec((None, None, tq, n), lambda i, q: (0, 0, q, 0)), blk, blk] + [_ANY] * len(deps),
        out_specs=[blk, blk], out_shape=[jax.ShapeDtypeStruct((M, K), BF16)] * 2,
        compiler_params=_params("parallel", "parallel"))(dy, wd, g, u, *deps)


def rowwise(name, fn, rows, ins, outs, tr=256, deps=()):
    widest = max([s[1].shape[1] if s[0] != "col" else s[3] for s in ins] + [s[1] for s in outs])
    tr = min(tr if widest <= 2048 else tr // 2, rows)
    in_specs, args = [], []
    for spec in ins:
        kind, a = spec[0], spec[1]
        if kind == "row":
            in_specs.append(pl.BlockSpec((tr, a.shape[1]), lambda i: (i, 0)))
        elif kind == "col":
            cb, width = spec[2], spec[3]
            in_specs.append(pl.BlockSpec((tr, width), lambda i, cb=cb: (i, cb)))
        else:
            in_specs.append(pl.BlockSpec(a.shape, lambda i: (0, 0)))
        args.append(a)
    out_specs, out_shapes = [], []
    for spec in outs:
        if spec[0] == "row":
            out_specs.append(pl.BlockSpec((tr, spec[1]), lambda i: (i, 0)))
            out_shapes.append(jax.ShapeDtypeStruct((rows, spec[1]), spec[2]))
        else:
            out_specs.append(pl.BlockSpec((1, spec[1]), lambda i: (0, 0)))
            out_shapes.append(jax.ShapeDtypeStruct((1, spec[1]), F32))
    n_in = len(ins)

    def body(*refs):
        vals = fn(*[r[...] for r in refs[:n_in]])
        first = pl.program_id(0) == 0
        for r, v, spec in zip(refs[n_in + len(deps):], vals, outs):
            if spec[0] == "row":
                r[...] = v.astype(r.dtype)
            else:
                _accumulate(r, v, first)

    return pl.pallas_call(body, name=name, grid=(rows // tr,), in_specs=in_specs + [_ANY] * len(deps),
                          out_specs=out_specs, out_shape=out_shapes,
                          compiler_params=_params("arbitrary"))(*args, *deps)


def _accumulate(ref, v, first):
    @pl.when(first)
    def _():
        ref[...] = v

    @pl.when(jnp.logical_not(first))
    def _():
        ref[...] += v


def _rms(x, w):
    r = lax.rsqrt(jnp.mean(x * x, axis=-1, keepdims=True) + NORM_EPS)
    return x * r * w


def _rms_bwd(x, w, dy):
    r = lax.rsqrt(jnp.mean(x * x, axis=-1, keepdims=True) + NORM_EPS)
    g = dy * w
    dx = r * (g - x * (r * r) * jnp.mean(g * x, axis=-1, keepdims=True))
    dw = jnp.sum(dy * x * r, axis=0, keepdims=True)
    return dx, dw


def _sigmoid(x):
    return 1.0 / (1.0 + jnp.exp(-x))


def _silu_and_grad(g):
    s = _sigmoid(g)
    return g * s, s * (1.0 + g * (1.0 - s))


def _swap_pairs(x):
    n = x.shape[-1]
    lane = lax.broadcasted_iota(jnp.int32, x.shape, x.ndim - 1)
    return jnp.where((lane & 1) == 0, pltpu.roll(x, n - 1, x.ndim - 1), pltpu.roll(x, 1, x.ndim - 1))


def _rot(x, cosf, sins):
    return x * cosf + _swap_pairs(x) * sins


def _unrot(d, cosf, sins):
    return d * cosf + _swap_pairs(d * sins)


def _ret_log_gamma(h):
    vals = [math.log1p(-2.0 ** (-5.0 - i)) for i in range(RET_HEADS)]
    out = jnp.float32(vals[RET_HEADS - 1])
    for i in range(RET_HEADS - 2, -1, -1):
        out = jnp.where(h == i, jnp.float32(vals[i]), out)
    return out


def _fill_decays(dec_ref, lg):
    ri = lax.broadcasted_iota(jnp.int32, (BLK, BLK), 0)
    ci = lax.broadcasted_iota(jnp.int32, (BLK, BLK), 1)
    for d in range(dec_ref.shape[0]):
        dt = d * BLK + ri - ci
        dec_ref[d] = jnp.where(dt >= 0, jnp.exp(jnp.maximum(dt, 0).astype(F32) * lg), 0.0)


def _decay_row(dec_ref, qi):
    return jnp.concatenate([dec_ref[qi - kb] for kb in range(qi + 1)], axis=1)


def _once(block_shape, index_map):
    return pl.BlockSpec(block_shape, index_map, pipeline_mode=pl.Buffered(1))


def _dot(a, b):
    return jnp.dot(a.astype(BF16), b.astype(BF16), preferred_element_type=F32)


def _dot_nt(a, b):
    return lax.dot_general(a.astype(BF16), b.astype(BF16), _NT, preferred_element_type=F32)


def _dot_tn(a, b):
    return lax.dot_general(a.astype(BF16), b.astype(BF16), _TN, preferred_element_type=F32)


def retention_fwd(name, z, cosf, sins, width_out):
    T = z.shape[0]
    nq = T // BLK
    scale = RET_DK ** -0.5

    def body(q_ref, k_ref, v_ref, cos_ref, sin_ref, o_ref, krot, vb, dec_ref):
        _fill_decays(dec_ref, _ret_log_gamma(pl.program_id(0)))
        krot[...] = (_rot(k_ref[...], cos_ref[...], sin_ref[...]) * scale).astype(BF16)
        vb[...] = v_ref[...].astype(BF16)
        for qi in range(nq):
            rows, n = slice(qi * BLK, (qi + 1) * BLK), (qi + 1) * BLK
            q = _rot(q_ref[rows, :], cos_ref[rows, :], sin_ref[rows, :])
            s = _dot_nt(q, krot[0:n, :]) * _decay_row(dec_ref, qi)
            o_ref[rows, :] = _dot(s, vb[0:n, :])

    return pl.pallas_call(
        body, name=name, grid=(RET_HEADS,),
        in_specs=[pl.BlockSpec((T, RET_DK), lambda h: (0, OFF_RQ // RET_DK + h)),
                  pl.BlockSpec((T, RET_DK), lambda h: (0, OFF_RK // RET_DK + h)),
                  pl.BlockSpec((T, RET_DV), lambda h: (0, OFF_RV // RET_DV + h)),
                  _once((T, RET_DK), lambda h: (0, 0)), _once((T, RET_DK), lambda h: (0, 0))],
        out_specs=pl.BlockSpec((T, RET_DV), lambda h: (0, h)),
        out_shape=jax.ShapeDtypeStruct((T, width_out), F32),
        scratch_shapes=[pltpu.VMEM((T, RET_DK), BF16), pltpu.VMEM((T, RET_DV), BF16),
                        pltpu.VMEM((nq, BLK, BLK), F32)],
        compiler_params=_params("arbitrary"))(z, z, z, cosf, sins)


def retention_bwd(name, z, cosf, sins, do):
    T = z.shape[0]
    nq = T // BLK
    scale = RET_DK ** -0.5

    def body(q_ref, k_ref, v_ref, cos_ref, sin_ref, do_ref, dq_ref, dk_ref, dv_ref, krot, vb, dk_acc, dv_acc, dec_ref):
        _fill_decays(dec_ref, _ret_log_gamma(pl.program_id(0)))
        krot[...] = (_rot(k_ref[...], cos_ref[...], sin_ref[...]) * scale).astype(BF16)
        vb[...] = v_ref[...].astype(BF16)
        dk_acc[...] = jnp.zeros_like(dk_acc)
        dv_acc[...] = jnp.zeros_like(dv_acc)
        for qi in range(nq):
            rows, n = slice(qi * BLK, (qi + 1) * BLK), (qi + 1) * BLK
            cos_q, sin_q = cos_ref[rows, :], sin_ref[rows, :]
            q = _rot(q_ref[rows, :], cos_q, sin_q).astype(BF16)
            dout = do_ref[rows, :].astype(BF16)
            kk, vv, dec = krot[0:n, :], vb[0:n, :], _decay_row(dec_ref, qi)
            p = (_dot_nt(q, kk) * dec).astype(BF16)
            ds = (_dot_nt(dout, vv) * dec).astype(BF16)
            dq_ref[rows, :] = _unrot(_dot(ds, kk), cos_q, sin_q).astype(dq_ref.dtype)
            dk_acc[0:n, :] += _dot_tn(ds, q)
            dv_acc[0:n, :] += _dot_tn(p, dout)
        dk_ref[...] = (_unrot(dk_acc[...], cos_ref[...], sin_ref[...]) * scale).astype(dk_ref.dtype)
        dv_ref[...] = dv_acc[...].astype(dv_ref.dtype)

    head = lambda h: (0, h)
    return pl.pallas_call(
        body, name=name, grid=(RET_HEADS,),
        in_specs=[pl.BlockSpec((T, RET_DK), lambda h: (0, OFF_RQ // RET_DK + h)),
                  pl.BlockSpec((T, RET_DK), lambda h: (0, OFF_RK // RET_DK + h)),
                  pl.BlockSpec((T, RET_DV), lambda h: (0, OFF_RV // RET_DV + h)),
                  _once((T, RET_DK), lambda h: (0, 0)), _once((T, RET_DK), lambda h: (0, 0)),
                  pl.BlockSpec((T, RET_DV), head)],
        out_specs=[pl.BlockSpec((T, RET_DK), head), pl.BlockSpec((T, RET_DK), head), pl.BlockSpec((T, RET_DV), head)],
        out_shape=[jax.ShapeDtypeStruct((T, RET_QK), BF16), jax.ShapeDtypeStruct((T, RET_QK), BF16),
                   jax.ShapeDtypeStruct((T, RET_V), BF16)],
        scratch_shapes=[pltpu.VMEM((T, RET_DK), BF16), pltpu.VMEM((T, RET_DV), BF16),
                        pltpu.VMEM((T, RET_DK), F32), pltpu.VMEM((T, RET_DV), F32),
                        pltpu.VMEM((nq, BLK, BLK), F32)],
        compiler_params=_params("arbitrary"))(z, z, z, cosf, sins, do)


GLA_PAIR = 2


def _gla_chunk(q_ref, k_ref, v_ref, glr_ref, gu, gb, rows, hh, trilf):
    ck = slice(hh * GLA_DK, (hh + 1) * GLA_DK)
    zg = _dot(glr_ref[rows, :], gu[:, ck]) + gb[:, ck]
    la = (jnp.minimum(zg, 0.0) - jnp.log(1.0 + jnp.exp(-jnp.abs(zg)))) * (1.0 / GLA_GATE_NORM)
    cum = jnp.dot(trilf, la, precision=HIGHEST, preferred_element_type=F32)
    last = jnp.sum(la, axis=0, keepdims=True)
    ecum = jnp.exp(cum)
    k = k_ref[rows, ck]
    qt = q_ref[rows, ck] * (GLA_DK ** -0.5) * ecum
    kt = k * jnp.exp(-cum)
    kh = k * jnp.exp(last - cum)
    return zg, cum, last, ecum, qt, kt, kh, v_ref[rows, hh * GLA_DV:(hh + 1) * GLA_DV].astype(BF16)


def _state_decay(last):
    e = jnp.exp(jnp.broadcast_to(last, (GLA_DK, GLA_DK)).T)
    return jnp.concatenate([e] * (GLA_DV // GLA_DK), axis=1)


def _gla_specs(T):
    wk, wv = GLA_PAIR * GLA_DK, GLA_PAIR * GLA_DV
    return [_once((T, wk), lambda h: (0, OFF_GQ // wk + h)),
            _once((T, wk), lambda h: (0, OFF_GK // wk + h)),
            _once((T, wv), lambda h: (0, OFF_GV // wv + h)),
            _once((T, LANE), lambda h: (0, 0)),
            pl.BlockSpec((LANE, wk), lambda h: (0, h)),
            pl.BlockSpec((1, wk), lambda h: (0, h))]


def gla_fwd(name, z, glr, gu, gb, o_prev):
    T = z.shape[0]
    nc = T // CHUNK
    wv = GLA_PAIR * GLA_DV

    def body(q_ref, k_ref, v_ref, glr_ref, gu_ref, gb_ref, prev_ref, o_ref, S):
        del prev_ref
        gu_b, gb_v = gu_ref[...].astype(BF16), gb_ref[...]
        ri = lax.broadcasted_iota(jnp.int32, (CHUNK, CHUNK), 0)
        ci = lax.broadcasted_iota(jnp.int32, (CHUNK, CHUNK), 1)
        tril = ri >= ci
        trilf = tril.astype(F32)
        S[...] = jnp.zeros_like(S)

        def step(c, carry):
            rows = pl.ds(pl.multiple_of(c * CHUNK, CHUNK), CHUNK)
            for hh in range(GLA_PAIR):
                _, _, last, _, qt, kt, kh, v = _gla_chunk(q_ref, k_ref, v_ref, glr_ref, gu_b, gb_v, rows, hh, trilf)
                a = jnp.where(tril, _dot_nt(qt, kt), 0.0)
                s_prev = S[hh]
                o_ref[rows, hh * GLA_DV:(hh + 1) * GLA_DV] = _dot(a, v) + _dot(qt, s_prev)
                S[hh] = s_prev * _state_decay(last) + _dot_tn(kh, v)
            return carry

        lax.fori_loop(0, nc, step, 0)

    n_in = 6
    return pl.pallas_call(
        body, name=name, grid=(GLA_HEADS // GLA_PAIR,),
        in_specs=_gla_specs(T) + [pl.BlockSpec(memory_space=pl.ANY)],
        out_specs=pl.BlockSpec((T, wv), lambda h: (0, RET_V // wv + h)),
        out_shape=jax.ShapeDtypeStruct(o_prev.shape, F32),
        scratch_shapes=[pltpu.VMEM((GLA_PAIR, GLA_DK, GLA_DV), F32)],
        input_output_aliases={n_in: 0},
        compiler_params=_params("arbitrary"))(z, z, z, glr, gu, gb, o_prev)


def gla_bwd(name, z, glr, gu, gb, do):
    T = z.shape[0]
    nc = T // CHUNK

    def body(q_ref, k_ref, v_ref, glr_ref, gu_ref, gb_ref, do_ref,
             dq_ref, dk_ref, dv_ref, dglr_ref, dgu_ref, dgb_ref, s_all, dS):
        gu_b, gb_v = gu_ref[...].astype(BF16), gb_ref[...]
        ri = lax.broadcasted_iota(jnp.int32, (CHUNK, CHUNK), 0)
        ci = lax.broadcasted_iota(jnp.int32, (CHUNK, CHUNK), 1)
        tril = ri >= ci
        trilf = tril.astype(F32)
        triuf = (ri <= ci).astype(F32)
        last_row = lax.broadcasted_iota(jnp.int32, (CHUNK, GLA_DK), 0) == CHUNK - 1
        ones8 = jnp.ones((8, GLA_DV), F32)

        def fstep(c, carry):
            rows = pl.ds(pl.multiple_of(c * CHUNK, CHUNK), CHUNK)
            for hh in range(GLA_PAIR):
                s_prev = dS[hh]
                s_all[hh, c] = s_prev
                _, _, last, _, _, _, kh, v = _gla_chunk(q_ref, k_ref, v_ref, glr_ref, gu_b, gb_v, rows, hh, trilf)
                dS[hh] = s_prev * _state_decay(last) + _dot_tn(kh, v)
            return carry

        dS[...] = jnp.zeros_like(dS)
        lax.fori_loop(0, nc, fstep, 0)
        dS[...] = jnp.zeros_like(dS)
        dgu_ref[...] = jnp.zeros_like(dgu_ref)
        dgb_ref[...] = jnp.zeros_like(dgb_ref)

        def bstep(i, carry):
            c = nc - 1 - i
            rows = pl.ds(pl.multiple_of(c * CHUNK, CHUNK), CHUNK)
            glr_c = glr_ref[rows, :]
            for hh in range(GLA_PAIR):
                ck, cv = slice(hh * GLA_DK, (hh + 1) * GLA_DK), slice(hh * GLA_DV, (hh + 1) * GLA_DV)
                zg, cum, last, ecum, qt, kt, kh, v = _gla_chunk(q_ref, k_ref, v_ref, glr_ref, gu_b, gb_v, rows, hh, trilf)
                a = jnp.where(tril, _dot_nt(qt, kt), 0.0)
                s_prev, ds_new = s_all[hh, c], dS[hh]
                dout = do_ref[rows, cv].astype(BF16)
                dv_ref[rows, cv] = (_dot_tn(a, dout) + _dot(kh, ds_new)).astype(dv_ref.dtype)
                da = jnp.where(tril, _dot_nt(dout, v), 0.0)
                dqt = _dot(da, kt) + _dot_nt(dout, s_prev)
                dkt = _dot_tn(da, qt)
                dkh = _dot_nt(v, ds_new)
                dS[hh] = ds_new * _state_decay(last) + _dot_tn(qt, dout)
                dq_ref[rows, ck] = (dqt * ecum * (GLA_DK ** -0.5)).astype(dq_ref.dtype)
                dk_ref[rows, ck] = (dkt * jnp.exp(-cum) + dkh * jnp.exp(last - cum)).astype(dk_ref.dtype)
                dkh_kh = dkh * kh
                dcum = dqt * qt - dkt * kt - dkh_kh
                rs = lax.dot_general(ones8, ds_new * s_prev, _NT, precision=HIGHEST, preferred_element_type=F32)
                dlast = (jnp.sum(dkh_kh, axis=0, keepdims=True)
                         + jnp.exp(last) * (jnp.sum(rs, axis=0, keepdims=True) * 0.125))
                dcum = dcum + jnp.where(last_row, dlast, 0.0)
                dla = jnp.dot(triuf, dcum, precision=HIGHEST, preferred_element_type=F32)
                dzg = dla * (1.0 / GLA_GATE_NORM) * _sigmoid(-zg)
                dglr_ref[hh, rows, :] = _dot_nt(dzg, gu_b[:, ck])
                dgu_ref[:, ck] += _dot_tn(glr_c, dzg)
                dgb_ref[:, ck] += jnp.sum(dzg, axis=0, keepdims=True)
            return carry

        lax.fori_loop(0, nc, bstep, 0)

    wk, wv = GLA_PAIR * GLA_DK, GLA_PAIR * GLA_DV
    return pl.pallas_call(
        body, name=name, grid=(GLA_HEADS // GLA_PAIR,),
        in_specs=_gla_specs(T) + [_once((T, wv), lambda h: (0, RET_V // wv + h))],
        out_specs=[pl.BlockSpec((T, wk), lambda h: (0, h)), pl.BlockSpec((T, wk), lambda h: (0, h)),
                   pl.BlockSpec((T, wv), lambda h: (0, h)),
                   pl.BlockSpec((GLA_PAIR, T, LANE), lambda h: (h, 0, 0)),
                   pl.BlockSpec((LANE, wk), lambda h: (0, h)), pl.BlockSpec((1, wk), lambda h: (0, h))],
        out_shape=[jax.ShapeDtypeStruct((T, GLA_QK), BF16), jax.ShapeDtypeStruct((T, GLA_QK), BF16),
                   jax.ShapeDtypeStruct((T, GLA_V), BF16), jax.ShapeDtypeStruct((GLA_HEADS, T, LANE), F32),
                   jax.ShapeDtypeStruct((LANE, GLA_QK), F32), jax.ShapeDtypeStruct((1, GLA_QK), F32)],
        scratch_shapes=[pltpu.VMEM((GLA_PAIR, nc, GLA_DK, GLA_DV), F32), pltpu.VMEM((GLA_PAIR, GLA_DK, GLA_DV), F32)],
        compiler_params=_params("arbitrary"))(z, z, z, glr, gu, gb, do)


HN_HEADS = RET_HEADS + GLA_HEADS
HN_W = RET_DV


def _gate_col(h):
    return jnp.where(h < RET_HEADS, OFF_RG // HN_W + h, OFF_GG // HN_W + h - RET_HEADS)


def headnorm_fwd(name, oraw, z, w):
    T = oraw.shape[0]
    tr = _pick(T, _TILES)

    def body(o_ref, g_ref, w_ref, y_ref):
        y_ref[...] = (_rms(o_ref[...], w_ref[...]) * _silu_and_grad(g_ref[...])[0]).astype(y_ref.dtype)

    return pl.pallas_call(
        body, name=name, grid=(HN_HEADS, T // tr),
        in_specs=[pl.BlockSpec((tr, HN_W), lambda h, i: (i, h)),
                  pl.BlockSpec((tr, HN_W), lambda h, i: (i, _gate_col(h))),
                  pl.BlockSpec((1, HN_W), lambda h, i: (0, h))],
        out_specs=pl.BlockSpec((tr, HN_W), lambda h, i: (i, h)),
        out_shape=jax.ShapeDtypeStruct((T, HN_HEADS * HN_W), BF16),
        compiler_params=_params("arbitrary", "arbitrary"))(oraw, z, w)


def headnorm_bwd(name, oraw, z, w, dy):
    T = oraw.shape[0]
    tr = _pick(T, _TILES)

    def body(o_ref, g_ref, w_ref, dy_ref, do_ref, dg_ref, dw_ref):
        o, wv, dyv = o_ref[...], w_ref[...], dy_ref[...].astype(F32)
        silu, dsilu = _silu_and_grad(g_ref[...])
        n = _rms(o, wv)
        dg_ref[...] = (dyv * n * dsilu).astype(dg_ref.dtype)
        dx, dw = _rms_bwd(o, wv, dyv * silu)
        do_ref[...] = dx
        _accumulate(dw_ref, dw, pl.program_id(1) == 0)

    blk = pl.BlockSpec((tr, HN_W), lambda h, i: (i, h))
    return pl.pallas_call(
        body, name=name, grid=(HN_HEADS, T // tr),
        in_specs=[blk, pl.BlockSpec((tr, HN_W), lambda h, i: (i, _gate_col(h))),
                  pl.BlockSpec((1, HN_W), lambda h, i: (0, h)), blk],
        out_specs=[blk, blk, pl.BlockSpec((1, HN_W), lambda h, i: (0, h))],
        out_shape=[jax.ShapeDtypeStruct((T, HN_HEADS * HN_W), F32),
                   jax.ShapeDtypeStruct((T, HN_HEADS * HN_W), BF16),
                   jax.ShapeDtypeStruct((1, HN_HEADS * HN_W), F32)],
        compiler_params=_params("arbitrary", "arbitrary"))(oraw, z, w, dy)


N_MASKS = 4


def _check_mask_classes(T):
    for window, dilation in DILATED_BRANCHES[:-1]:
        assert window < (N_MASKS - 1) * BLK - (BLK - 1) and BLK % dilation == 0
    assert DILATED_BRANCHES[-1][0] >= T and BLK % DILATED_BRANCHES[-1][1] == 0


def _fill_masks(mult_ref, bias_ref):
    ri = lax.broadcasted_iota(jnp.int32, (BLK, BLK), 0)
    ci = lax.broadcasted_iota(jnp.int32, (BLK, BLK), 1)
    for d in range(N_MASKS):
        dt = d * BLK + ri - ci
        mult = jnp.zeros((BLK, BLK), F32)
        for window, dilation in DILATED_BRANCHES:
            hit = (dt >= 0) & (dt <= window) & ((dt & (dilation - 1)) == 0)
            mult = mult + hit.astype(F32)
        mult_ref[d] = mult
        bias_ref[d] = jnp.where(mult > 0, 0.0, -1e30)


def _mask_row(ref, qi):
    return jnp.concatenate([ref[min(qi - kb, N_MASKS - 1)] for kb in range(qi + 1)], axis=1)


def attn_fwd(name, qkv):
    T = qkv.shape[0]
    D = qkv.shape[1] // 3
    dh = D // ATT_HEADS
    nq = T // BLK
    scale = dh ** -0.5

    _check_mask_classes(T)

    def body(q_ref, k_ref, v_ref, o_ref, lse_ref, mult_ref, bias_ref):
        @pl.when(pl.program_id(0) == 0)
        def _():
            _fill_masks(mult_ref, bias_ref)

        for qi in range(nq):
            rows, n = slice(qi * BLK, (qi + 1) * BLK), (qi + 1) * BLK
            s = (_dot_nt(q_ref[rows, :], k_ref[0:n, :]) * scale
                 + _mask_row(bias_ref, qi))
            m = jnp.max(s, axis=-1, keepdims=True)
            p = _mask_row(mult_ref, qi) * jnp.exp(s - m)
            l = jnp.sum(p, axis=-1, keepdims=True)
            o_ref[rows, :] = (_dot(p, v_ref[0:n, :]) / l).astype(o_ref.dtype)
            lse_ref[rows, :] = jnp.broadcast_to(m + jnp.log(l), (BLK, LANE))

    return pl.pallas_call(
        body, name=name, grid=(ATT_HEADS,),
        in_specs=[pl.BlockSpec((T, dh), lambda h: (0, h)),
                  pl.BlockSpec((T, dh), lambda h: (0, ATT_HEADS + h)),
                  pl.BlockSpec((T, dh), lambda h: (0, 2 * ATT_HEADS + h))],
        out_specs=[pl.BlockSpec((T, dh), lambda h: (0, h)),
                   pl.BlockSpec((None, T, LANE), lambda h: (h, 0, 0))],
        out_shape=[jax.ShapeDtypeStruct((T, D), BF16), jax.ShapeDtypeStruct((ATT_HEADS, T, LANE), F32)],
        scratch_shapes=[pltpu.VMEM((N_MASKS, BLK, BLK), F32), pltpu.VMEM((N_MASKS, BLK, BLK), F32)],
        compiler_params=_params("arbitrary"))(qkv, qkv, qkv)


def attn_bwd(name, qkv, o, lse, do):
    T = qkv.shape[0]
    D = qkv.shape[1] // 3
    dh = D // ATT_HEADS
    nq = T // BLK
    scale = dh ** -0.5

    _check_mask_classes(T)

    def body(q_ref, k_ref, v_ref, o_ref, lse_ref, do_ref, dq_ref, dk_ref, dv_ref, dk_acc, dv_acc, mult_ref, bias_ref):
        @pl.when(pl.program_id(0) == 0)
        def _():
            _fill_masks(mult_ref, bias_ref)

        dk_acc[...] = jnp.zeros_like(dk_acc)
        dv_acc[...] = jnp.zeros_like(dv_acc)
        for qi in range(nq):
            rows, n = slice(qi * BLK, (qi + 1) * BLK), (qi + 1) * BLK
            q, dout = q_ref[rows, :], do_ref[rows, :]
            kk, vv = k_ref[0:n, :], v_ref[0:n, :]
            delta = jnp.sum(dout.astype(F32) * o_ref[rows, :].astype(F32), axis=-1, keepdims=True)
            lse = jnp.max(lse_ref[rows, :], axis=-1, keepdims=True)
            s = _dot_nt(q, kk) * scale + _mask_row(bias_ref, qi)
            p = _mask_row(mult_ref, qi) * jnp.exp(s - lse)
            ds = (p * (_dot_nt(dout, vv) - delta) * scale).astype(BF16)
            dq_ref[rows, :] = _dot(ds, kk).astype(dq_ref.dtype)
            dk_acc[0:n, :] += _dot_tn(ds, q)
            dv_acc[0:n, :] += _dot_tn(p, dout)
        dk_ref[...] = dk_acc[...].astype(dk_ref.dtype)
        dv_ref[...] = dv_acc[...].astype(dv_ref.dtype)

    full = pl.BlockSpec((T, dh), lambda h: (0, h))
    return pl.pallas_call(
        body, name=name, grid=(ATT_HEADS,),
        in_specs=[full, pl.BlockSpec((T, dh), lambda h: (0, ATT_HEADS + h)),
                  pl.BlockSpec((T, dh), lambda h: (0, 2 * ATT_HEADS + h)),
                  full, pl.BlockSpec((None, T, LANE), lambda h: (h, 0, 0)), full],
        out_specs=[full, full, full],
        out_shape=[jax.ShapeDtypeStruct((T, D), BF16)] * 3,
        scratch_shapes=[pltpu.VMEM((T, dh), F32), pltpu.VMEM((T, dh), F32),
                        pltpu.VMEM((N_MASKS, BLK, BLK), F32), pltpu.VMEM((N_MASKS, BLK, BLK), F32)],
        compiler_params=_params("arbitrary"))(qkv, qkv, qkv, o, lse, do)


def _mesh_pos():
    mx, my, mc = lax.axis_index("x"), lax.axis_index("y"), lax.axis_index("c")
    return mx, my, mc, 4 * mx + 2 * my + mc


def _peer(k, mx, my, mc):
    px, py, pc = mx ^ (k >> 2), my ^ ((k >> 1) & 1), mc ^ (k & 1)
    return (px, py, pc), 4 * px + 2 * py + pc


_SIBLING = 1
_OTHER_CHIPS = (4, 2, 6)
N_CHIP = N_DEV // 2
_PLANS = {"gather": (2, N_DEV - 1), "to_chips": (2, 1 + len(_OTHER_CHIPS)), "pass_on": (1, len(_OTHER_CHIPS)),
          "halves": (2, N_CHIP), "chip_sums": (2, len(_OTHER_CHIPS))}


def _copies(kind, items, send_sems, recv_sems):
    mx, my, mc, me = _mesh_pos()
    out = []

    def add(n, src, dst, peer):
        out.append(pltpu.make_async_remote_copy(
            src_ref=src, dst_ref=dst, send_sem=send_sems.at[n], recv_sem=recv_sems.at[n],
            device_id=peer, device_id_type=pl.DeviceIdType.MESH))

    per_item = _PLANS[kind][1]
    sibling = _peer(_SIBLING, mx, my, mc)[0]
    for i, refs in enumerate(items):
        n = i * per_item
        if kind == "gather":
            for k in range(1, N_DEV):
                add(n + k - 1, refs[0], refs[1].at[me], _peer(k, mx, my, mc)[0])
        elif kind == "to_chips":
            rows = refs[0].shape[0]
            dst = refs[1].at[me] if rows == refs[1].shape[1] else refs[1].at[me, pl.ds(0, rows)]
            for j, k in enumerate((_SIBLING,) + _OTHER_CHIPS):
                add(n + j, refs[0], dst, _peer(k, mx, my, mc)[0])
        elif kind == "pass_on":
            for j, k in enumerate(_OTHER_CHIPS):
                add(n + j, refs[0].at[me ^ k], refs[0].at[me ^ k], sibling)
        elif kind == "halves":
            for chip in range(N_CHIP):
                add(n + chip, refs[0].at[2 * chip + 1 - mc], refs[1].at[chip], sibling)
        else:
            for j, k in enumerate(_OTHER_CHIPS):
                peer, to = _peer(k, mx, my, mc)
                add(n + j, refs[0].at[to // 2], refs[1].at[me // 2], peer)
    return out


_HBM = pl.BlockSpec(memory_space=pltpu.HBM)
_SEM = pl.BlockSpec(memory_space=pltpu.SEMAPHORE)
_DATAFLOW = pltpu.SideEffectType.DATAFLOW_SIDE_EFFECTING


def exchange_call(name, waits, starts, deps=()):
    bufs, slot_of = [], {}

    def slots(items):
        out = []
        for item in items:
            for b in item:
                if id(b) not in slot_of:
                    slot_of[id(b)] = len(bufs)
                    bufs.append(b)
            out.append(tuple(slot_of[id(b)] for b in item))
        return out

    wait_plan = [(kind, slots(handle[0])) for kind, handle in waits]
    start_plan = [(kind, slots(items)) for kind, items in starts]
    wait_sems = [s for _, handle in waits for s in handle[1:]]
    n_buf, n_ws, n_start = len(bufs), len(wait_sems), len(starts)

    def body(*refs):
        buf_refs, sems_in = refs[:n_buf], refs[n_buf:n_buf + n_ws]
        outs = refs[n_buf + n_ws + len(deps):]
        pick = lambda plan: [tuple(buf_refs[s] for s in item) for item in plan]
        for wi, (kind, plan) in enumerate(wait_plan):
            copies = _copies(kind, pick(plan), sems_in[2 * wi], sems_in[2 * wi + 1])
            for cp in copies:
                cp.wait_send()
            for cp in copies:
                cp.wait_recv()
        for si, (kind, plan) in enumerate(start_plan):
            for cp in _copies(kind, pick(plan), outs[2 * si], outs[2 * si + 1]):
                cp.start()
        outs[-1][...] = jnp.zeros_like(outs[-1])

    hbm_bufs = [pltpu.with_memory_space_constraint(b, pltpu.HBM) for b in bufs]
    sem_shapes = []
    for kind, plan in start_plan:
        sem_shapes += [pltpu.SemaphoreType.DMA((len(plan) * _PLANS[kind][1],))] * 2
    outs = pl.pallas_call(
        body, name=name,
        out_shape=sem_shapes + [pltpu.HBM(b.shape, b.dtype) for b in bufs] + [jax.ShapeDtypeStruct((8, LANE), F32)],
        in_specs=[_HBM] * n_buf + [_SEM] * n_ws + [_ANY] * len(deps),
        out_specs=[_SEM] * (2 * n_start) + [_HBM] * n_buf + [pl.BlockSpec(memory_space=pltpu.VMEM)],
        input_output_aliases={i: 2 * n_start + i for i in range(n_buf)},
        compiler_params=pltpu.CompilerParams(has_side_effects=_DATAFLOW))(*hbm_bufs, *wait_sems, *deps)
    sems, thru, token = outs[:2 * n_start], outs[2 * n_start:-1], outs[-1]
    through = lambda plan: [tuple(thru[s] for s in item) for item in plan]
    waited = [through(plan) for _, plan in wait_plan]
    handles = [(through(plan), sems[2 * si], sems[2 * si + 1]) for si, (_, plan) in enumerate(start_plan)]
    return waited, handles, token


def gather_small(name, a, deps=()):
    def body(a_ref, *rest):
        o_ref, send_sems, recv_sems, local_sem = rest[len(deps):]
        me = _mesh_pos()[3]
        own = pltpu.make_async_copy(a_ref, o_ref.at[me], local_sem)
        own.start()
        copies = _copies("gather", [(a_ref, o_ref)], send_sems, recv_sems)
        for cp in copies:
            cp.start()
        for cp in copies:
            cp.wait_recv()
        for cp in copies:
            cp.wait_send()
        own.wait()

    return pl.pallas_call(
        body, name=name, in_specs=[_ANY] * (1 + len(deps)), out_specs=_ANY,
        out_shape=jax.ShapeDtypeStruct((N_DEV,) + a.shape, a.dtype),
        scratch_shapes=[pltpu.SemaphoreType.DMA((N_DEV - 1,)), pltpu.SemaphoreType.DMA((N_DEV - 1,)),
                        pltpu.SemaphoreType.DMA],
        compiler_params=pltpu.CompilerParams(has_side_effects=True))(a, *deps)


def _adamw_math(w, g, m, v):
    m2 = ADAM_B1 * m + (1.0 - ADAM_B1) * g
    v2 = ADAM_B2 * v + (1.0 - ADAM_B2) * (g * g)
    m_hat = m2 / (1.0 - ADAM_B1 ** ADAM_STEP)
    v_hat = v2 / (1.0 - ADAM_B2 ** ADAM_STEP)
    delta = -ADAM_LR * (m_hat / (jnp.sqrt(v_hat) + ADAM_EPS) + ADAM_WD * w)
    return delta, m2, v2


def chip_sum(name, a, half):
    _, r, c = a.shape
    tr = _pick(r, (512, 256, 128, 64, 32, 16))
    chip = 2 * lax.axis_index("x") + lax.axis_index("y")
    where = jnp.stack([lax.axis_index("c"), chip ^ 1, chip ^ 2, chip ^ 3]).astype(jnp.int32)

    def body(where_ref, a_ref, h_ref, o_ref):
        del where_ref
        o_ref[...] = (a_ref[...].astype(F32) + h_ref[...].astype(F32)).astype(o_ref.dtype)

    blk = pl.BlockSpec((None, tr, c), lambda g, i, where: (where[1 + g], i, 0))
    grid_spec = pltpu.PrefetchScalarGridSpec(
        num_scalar_prefetch=1, grid=(N_CHIP - 1, r // tr),
        in_specs=[pl.BlockSpec((None, None, tr, c), lambda g, i, where: (where[1 + g], where[0], i, 0)), blk],
        out_specs=blk)
    return pl.pallas_call(
        body, name=name, grid_spec=grid_spec, out_shape=jax.ShapeDtypeStruct((N_CHIP, r, c), BF16),
        compiler_params=_params("parallel", "parallel"))(where, a.reshape(N_CHIP, 2, r, c), half)


def adamw(name, w, m, v, l, land, a, half, prev=None):
    L, r, c = w.shape
    cp = land.shape[2]
    tr = _pick(r, (256, 176, 128, 64, 32, 16, 8))

    def body(w_ref, m_ref, v_ref, land_ref, a_ref, half_ref, *rest):
        g_ref, d_ref, m2_ref, v2_ref = rest[-4:]
        chip = _mesh_pos()[3] // 2
        mine = a_ref[:, pl.ds(0, c)].astype(F32) + half_ref[:, pl.ds(0, c)].astype(F32)
        g = None
        for s in range(N_CHIP):
            part = jnp.where(chip == s, mine, land_ref[s, :, pl.ds(0, c)].astype(F32))
            g = part if g is None else g + part
        delta, m2, v2 = _adamw_math(w_ref[...], g, m_ref[...], v_ref[...])
        g_ref[...] = g
        d_ref[...] = delta
        m2_ref[...] = m2
        v2_ref[...] = v2

    blk = pl.BlockSpec((None, tr, c), lambda i: (l, i, 0))
    shape = jax.ShapeDtypeStruct((L, r, c), F32)
    extra = [] if prev is None else list(prev)
    return pl.pallas_call(
        body, name=name, grid=(r // tr,),
        in_specs=[blk, blk, blk, pl.BlockSpec((N_CHIP, tr, cp), lambda i: (0, i, 0)),
                  pl.BlockSpec((None, tr, cp), lambda i: (_mesh_pos()[3], i, 0)),
                  pl.BlockSpec((None, tr, cp), lambda i: (_mesh_pos()[3] // 2, i, 0))] + [_ANY] * len(extra),
        out_specs=[blk] * 4, out_shape=[shape] * 4,
        input_output_aliases={6 + k: k for k in range(len(extra))},
        compiler_params=_params("parallel"))(w, m, v, land, a, half, *extra)


def adamw_small(name, w, m, v, parts):
    n = w.shape[1]

    def body(w_ref, m_ref, v_ref, p_ref, g_ref, d_ref, m2_ref, v2_ref):
        g = p_ref[0:1, :]
        for s in range(1, N_DEV):
            g = g + p_ref[s:s + 1, :]
        delta, m2, v2 = _adamw_math(w_ref[...], g, m_ref[...], v_ref[...])
        g_ref[...] = g
        d_ref[...] = delta
        m2_ref[...] = m2
        v2_ref[...] = v2

    shape = jax.ShapeDtypeStruct((1, n), F32)
    return pl.pallas_call(body, name=name, out_shape=[shape] * 4,
                          compiler_params=pltpu.CompilerParams(vmem_limit_bytes=VMEM_LIMIT_BYTES))(w, m, v, parts)


def _rope_tables(positions):
    half = RET_DK // 2
    inv_freq = 1.0 / jnp.power(RET_THETA_BASE, jnp.linspace(0.0, 1.0, half, dtype=F32))
    ang = positions.astype(F32)[:, None] * inv_freq
    cos, sin = jnp.cos(ang), jnp.sin(ang)
    cosf = jnp.repeat(cos, 2, axis=-1)
    sins = jnp.stack([-sin, sin], axis=-1).reshape(cosf.shape)
    return cosf, sins


def _pad_to(a, axis, size):
    pad = [(0, 0)] * a.ndim
    pad[axis] = (0, size - a.shape[axis])
    return jnp.pad(a, pad)


def _round_up(n, m):
    return -(-n // m) * m


def kernel(x, p, positions, attn_norm_w, ffn_norm_w, ple_norm_w, final_norm_w, ab_w_in, ab_gla_gate_up, ab_gla_gate_b, ab_ret_norm_w, ab_gla_norm_w, ab_w_out, c_w_qkv, c_w_out, ffn_w_gate, ffn_w_up, ffn_w_down, ple_w_proj, ple_w_gate, loss_target, m_attn_norm_w, m_ffn_norm_w, m_ple_norm_w, m_final_norm_w, m_ab_w_in, m_ab_gla_gate_up, m_ab_gla_gate_b, m_ab_ret_norm_w, m_ab_gla_norm_w, m_ab_w_out, m_c_w_qkv, m_c_w_out, m_ffn_w_gate, m_ffn_w_up, m_ffn_w_down, m_ple_w_proj, m_ple_w_gate, v_attn_norm_w, v_ffn_norm_w, v_ple_norm_w, v_final_norm_w, v_ab_w_in, v_ab_gla_gate_up, v_ab_gla_gate_b, v_ab_ret_norm_w, v_ab_gla_norm_w, v_ab_w_out, v_c_w_qkv, v_c_w_out, v_ffn_w_gate, v_ffn_w_up, v_ffn_w_down, v_ple_w_proj, v_ple_w_gate):
    T, D = x.shape[1], x.shape[2]
    depth = attn_norm_w.shape[0]
    assert ab_w_in.shape[0] == 1 and c_w_qkv.shape[0] == 1 and depth == 2, "one even and one odd layer"
    me = 4 * lax.axis_index("x") + 2 * lax.axis_index("y") + lax.axis_index("c")
    in_shard = ab_w_in.shape[2]
    in_width = in_shard * N_DEV
    assert in_width == OFF_LR + GLA_GATE_RANK
    fs = ffn_w_gate.shape[2]
    fp = _round_up(fs, LANE)
    gu_cols = ab_gla_gate_up.shape[2]

    bf = lambda a: a.astype(BF16)
    tr_ = lambda a: jnp.swapaxes(a, -1, -2)
    wg_t, wu_t = tr_(ffn_w_gate), tr_(ffn_w_up)
    srcs = {"w_in": bf(tr_(ab_w_in[0]))}
    group_keys = [["w_in"], ["gu", "w_oab"], ["wg0", "wu0"], ["wd0", "wpg0", "wpp0"], ["w_qkv", "w_oc"],
                  ["wg1", "wu1"], ["wd1", "wpg1", "wpp1"]]
    G_IN, G_OUT, G_QKV = 0, 1, 4
    g_ffn = lambda layer: (2, 3) if layer == 0 else (5, 6)

    def landing(key):
        a = srcs[key]
        rows = fp if key[:2] in ("wg", "wu", "wd") else a.shape[0]
        buf = lax.empty((N_DEV, rows) + a.shape[1:], a.dtype)
        if rows > a.shape[0]:
            zeros = jnp.zeros((N_DEV, rows - a.shape[0]) + a.shape[1:], a.dtype)
            buf = lax.dynamic_update_slice(buf, zeros, (0, a.shape[0]) + (0,) * (a.ndim - 1))
        return lax.dynamic_update_slice(buf, a[None], (me,) + (0,) * a.ndim)

    _, chip_handles, gather_token = exchange_call(
        "gather_start_in", [], [("to_chips", [(srcs[k], landing(k)) for k in group_keys[G_IN]])])
    (gather_token, w_out_, gu_, w_qkv_, w_oc_, wg_, wu_, wd_, wpg_, wpp_) = lax.optimization_barrier(
        (gather_token, ab_w_out, ab_gla_gate_up, c_w_qkv, c_w_out, wg_t, wu_t, ffn_w_down, ple_w_gate, ple_w_proj))
    srcs.update(w_oab=bf(w_out_[0]), gu=gu_[0], w_qkv=bf(w_qkv_[0]), w_oc=bf(w_oc_[0]))
    for l in range(depth):
        srcs[f"wg{l}"] = bf(wg_[l])
        srcs[f"wu{l}"] = bf(wu_[l])
        srcs[f"wd{l}"] = bf(wd_[l])
        srcs[f"wpg{l}"] = bf(wpg_[l])
        srcs[f"wpp{l}"] = bf(wpp_[l])
    _, more, gather_token = exchange_call(
        "gather_start", [], [("to_chips", [(srcs[k], landing(k)) for k in keys]) for keys in group_keys[1:]],
        deps=(gather_token,))
    chip_handles = chip_handles + more
    weights = {}

    def gather_wait(gi, dep):
        lands = [(land,) for _, land in chip_handles[gi][0]]
        _, (passing,), _ = exchange_call(
            f"gather{gi}_pass", [("to_chips", chip_handles[gi])], [("pass_on", lands)], deps=(dep,))
        (complete,), _, _ = exchange_call(f"gather{gi}_done", [("pass_on", passing)], [])
        weights.update(zip(group_keys[gi], [land for (land,) in complete]))

    gb = ab_gla_gate_b
    hn_w = jnp.concatenate([ab_ret_norm_w, ab_gla_norm_w], axis=1)
    cosf, sins = _rope_tables(positions[0])
    p_bf = bf(p[:, 0])

    xs = x[0]
    saved = []
    for i in range(depth):
        nm = f"l{i}_"
        w_attn, w_ffn, w_ple = attn_norm_w[i:i + 1], ffn_norm_w[i:i + 1], ple_norm_w[i:i + 1]
        (xn,) = rowwise(nm + "norm_attn", lambda a, w: (_rms(a, w),), T, [("row", xs), ("full", w_attn)],
                        [("row", D, BF16)], deps=(gather_token,) if i == 0 else ())
        if i % 2 == 0:
            gather_wait(G_IN, xn)
            w_in_t = weights["w_in"].reshape(1, 1, in_width, D)
            w_lr_t = _pad_to(w_in_t[0, 0, OFF_LR:], 0, LANE).reshape(1, 1, LANE, D)
            z = mmt_fwd(nm + "mm_in", xn, w_in_t, 0, F32, n=OFF_LR)
            glr = mmt_fwd(nm + "mm_lr", xn, w_lr_t, 0, F32)
            oraw = retention_fwd(nm + "ret_fwd", z, cosf, sins, RET_V + GLA_V)
            gather_wait(G_OUT, oraw)
            w_oab = weights["w_oab"].reshape(1, 1, D, D)
            gu_full = _pad_to(weights["gu"].transpose(1, 0, 2).reshape(GLA_GATE_RANK, GLA_QK), 0, LANE)
            oraw = gla_fwd(nm + "gla_fwd", z, glr, gu_full, gb, oraw)
            o = headnorm_fwd(nm + "headnorm_fwd", oraw, z, hn_w)
            h1, hn = mm_add_norm(nm + "mm_out", o, w_oab, xs, w_ffn)
            mixer_saved = (z, glr, oraw, o)
        else:
            gather_wait(G_QKV, xn)
            w_qkv = weights["w_qkv"].reshape((1,) + weights["w_qkv"].shape)
            w_oc = weights["w_oc"].reshape(1, 1, D, D)
            qkv = mm_nn(nm + "mm_qkv", xn, w_qkv, 0, BF16)
            o, lse = attn_fwd(nm + "attn_fwd", qkv)
            h1, hn = mm_add_norm(nm + "mm_out", o, w_oc, xs, w_ffn)
            mixer_saved = (qkv, o, lse)
        gather_wait(g_ffn(i)[0], hn)
        wg = weights[f"wg{i}"].reshape(1, N_DEV, fp, D)
        wu = weights[f"wu{i}"].reshape(1, N_DEV, fp, D)
        g, u, act = ffn_gate_up(nm + "ffn_gate_up", hn, wg, wu)
        gather_wait(g_ffn(i)[1], act)
        wd = weights[f"wd{i}"].reshape(1, 1, N_DEV * fp, D)
        wpg = weights[f"wpg{i}"].reshape(1, 1, D, D)
        wpp = weights[f"wpp{i}"].reshape((1,) + weights[f"wpp{i}"].shape)
        h2, pn = mm_add_norm(nm + "mm_down", act, wd, h1, w_ple)
        x_next, s, e = ple_fwd(nm + "ple", pn, wpg, p_bf[i], wpp, h2)
        mixer_w = (w_in_t, w_lr_t, w_oab, gu_full) if i % 2 == 0 else (w_qkv, w_oc)
        saved.append((xs, xn, mixer_saved, mixer_w, (wg, wu, wd, wpg), h1, hn, g, u, act, h2, pn, s, e))
        xs = x_next

    def loss_fn(a, w, t):
        diff = _rms(a, w) - t
        dx, dw = _rms_bwd(a, w, diff * (1.0 / D))
        part = 0.5 * jnp.sum(jnp.mean(diff * diff, axis=-1, keepdims=True), axis=0, keepdims=True)
        return dx, dw, jnp.broadcast_to(part, (1, LANE))

    dx, d_final_w, loss_part = rowwise("loss_head", loss_fn, T,
                                       [("row", xs), ("full", final_norm_w[None, :]), ("row", loss_target[0])],
                                       [("row", D, F32), ("acc", D), ("acc", LANE)])
    loss = lax.psum(loss_part[0, 0], ("x", "y", "c"))

    grads = {}
    on_chip = []
    scatters = []

    def scatter_start(name, keys, deps=()):
        waits = [("halves", on_chip[0][1])] if on_chip else []
        starts = [("halves", [(grads[k], lax.empty((N_CHIP,) + grads[k].shape[1:], BF16)) for k in keys])] if keys else []
        waited, handles, token = exchange_call(name, waits, starts, deps=deps)
        if on_chip:
            done_keys, _ = on_chip.pop()
            sums = [chip_sum(f"{name}_sum{j}", a, half) for j, (a, half) in enumerate(waited[0])]
            _, (handle,), token = exchange_call(
                name + "_chips", [], [("chip_sums", [(cs, lax.empty(cs.shape, BF16)) for cs in sums])])
            scatters.append((done_keys, handle, waited[0]))
        if keys:
            on_chip.append((keys, handles[0]))
        return token

    d_attn_w, d_ffn_w, d_ple_w = [None] * depth, [None] * depth, [None] * depth
    for i in reversed(range(depth)):
        nm = f"l{i}_b_"
        xs_i, xn, mixer_saved, mixer_w, (wg, wu, wd, wpg), h1, hn, g, u, act, h2, pn, s, e = saved[i]
        w_attn, w_ffn, w_ple = attn_norm_w[i:i + 1], ffn_norm_w[i:i + 1], ple_norm_w[i:i + 1]

        def ple_bwd(d, sv, ev):
            gate = _sigmoid(sv)
            return d * gate, d * ev * gate * (1.0 - gate)

        de, ds = rowwise(nm + "ple_out", ple_bwd, T, [("row", dx), ("row", s), ("row", e)],
                         [("row", D, BF16), ("row", D, BF16)], deps=(loss.reshape(1, 1),) if i == depth - 1 else ())
        grads[("ple_w_proj", i)] = mm_tn(nm + "mm_ple_proj_w", p_bf[i], de, N_DEV, BF16)
        grads[("ple_w_gate", i)] = mm_tn(nm + "mm_ple_gate_w", pn, ds, 1, BF16).reshape(N_DEV, D // N_DEV, D)
        dpn = mm_nt(nm + "mm_ple_gate_x", ds, wpg, 0, F32)

        def norm_bwd_add(a, w, dn, dres):
            dxx, dw = _rms_bwd(a, w, dn)
            tot = dres + dxx
            return tot, tot, dw

        dh2, dh2_bf, d_ple_w[i] = rowwise(nm + "norm_ple", norm_bwd_add, T,
                                          [("row", h2), ("full", w_ple), ("row", dpn), ("row", dx)],
                                          [("row", D, F32), ("row", D, BF16), ("acc", D)])
        grads[("ffn_w_down", i)] = mm_tn(nm + "mm_down_w", act, dh2_bf, 1, BF16).reshape(N_DEV, fp, D)
        token = scatter_start(nm + "scatter_ple_down", [("ple_w_proj", i), ("ple_w_gate", i), ("ffn_w_down", i)])
        dg, du = ffn_down_bwd(nm + "ffn_down_x", dh2_bf, wd, g, u, deps=(token,))
        grads[("ffn_w_gate", i)] = mmt_dw(nm + "mm_gate_w", dg, hn, N_DEV, BF16)
        grads[("ffn_w_up", i)] = mmt_dw(nm + "mm_up_w", du, hn, N_DEV, BF16)
        token = scatter_start(nm + "scatter_gate_up", [("ffn_w_gate", i), ("ffn_w_up", i)])
        dhn_g = mmt_dx_wide(nm + "mm_gate_x", dg, wg, F32, deps=(token,))
        dhn_u = mmt_dx_wide(nm + "mm_up_x", du, wu, F32)

        def norm_bwd_add2(a, w, dn1, dn2, dres):
            dxx, dw = _rms_bwd(a, w, dn1 + dn2)
            tot = dres + dxx
            return tot, tot, dw

        dh1, dh1_bf, d_ffn_w[i] = rowwise(nm + "norm_ffn", norm_bwd_add2, T,
                                          [("row", h1), ("full", w_ffn), ("row", dhn_g), ("row", dhn_u), ("row", dh2)],
                                          [("row", D, F32), ("row", D, BF16), ("acc", D)])
        if i % 2 == 0:
            z, glr, oraw, o = mixer_saved
            w_in_t, w_lr_t, w_oab, gu_full = mixer_w
            grads[("ab_w_out", 0)] = mm_tn(nm + "mm_out_w", o, dh1_bf, 1, BF16).reshape(N_DEV, D // N_DEV, D)
            token = scatter_start(nm + "scatter_out", [("ab_w_out", 0)])
            do = mm_nt(nm + "mm_out_x", dh1_bf, w_oab, 0, F32, deps=(token,))
            d_oraw, d_gates, d_hn_w = headnorm_bwd(nm + "headnorm", oraw, z, hn_w, do)
            d_rq, d_rk, d_rv = retention_bwd(nm + "ret", z, cosf, sins, d_oraw)
            d_gq, d_gk, d_gv, d_glr4, d_gu, d_gb = gla_bwd(nm + "gla", z, glr, gu_full, gb, d_oraw)
            dz = jnp.concatenate([d_rq, d_rk, d_rv, d_gates[:, :RET_V], d_gq, d_gk, d_gv, d_gates[:, RET_V:]], axis=1)
            (d_glr,) = rowwise(nm + "sum_lr", lambda *a: (a[0] + a[1] + a[2] + a[3],), T,
                               [("row", d_glr4[hh]) for hh in range(GLA_HEADS)], [("row", LANE, BF16)])
            dw_main = mm_tn(nm + "mm_in_w", xn, dz, 1, BF16)[0]
            dw_lr = mm_tn(nm + "mm_lr_w", xn, d_glr, 1, BF16)[0]
            dw_in = jnp.concatenate([dw_main, dw_lr[:, :GLA_GATE_RANK]], axis=1)
            grads[("ab_w_in", 0)] = dw_in.reshape(D, N_DEV, in_shard).transpose(1, 0, 2)
            token = scatter_start(nm + "scatter_in", [("ab_w_in", 0)])
            dxn_a = mmt_dx_wide(nm + "mm_in_x", dz, w_in_t, F32, n=OFF_LR, deps=(token,))
            token = scatter_start(nm + "scatter_in_on", [], deps=(dxn_a,))
            dxn_b = mmt_dx(nm + "mm_lr_x", d_glr, w_lr_t, 0, F32, deps=(token,))
        else:
            qkv, o, lse = mixer_saved
            w_qkv, w_oc = mixer_w
            grads[("c_w_out", 0)] = mm_tn(nm + "mm_out_w", o, dh1_bf, 1, BF16).reshape(N_DEV, D // N_DEV, D)
            do = mm_nt(nm + "mm_out_x", dh1_bf, w_oc, 0, BF16)
            dq, dk, dv = attn_bwd(nm + "attn", qkv, o, lse, do)
            dqkv = jnp.concatenate([dq, dk, dv], axis=1)
            grads[("c_w_qkv", 0)] = mm_tn(nm + "mm_qkv_w", xn, dqkv, N_DEV, BF16)
            token = scatter_start(nm + "scatter_attn", [("c_w_out", 0), ("c_w_qkv", 0)])
            dxn_a = mm_nt_wide(nm + "mm_qkv_x", dqkv, w_qkv, F32, deps=(token,))
            dxn_b = None
        if dxn_b is None:
            dx, _, d_attn_w[i] = rowwise(nm + "norm_attn", norm_bwd_add, T,
                                         [("row", xs_i), ("full", w_attn), ("row", dxn_a), ("row", dh1)],
                                         [("row", D, F32), ("row", D, BF16), ("acc", D)])
        else:
            dx, _, d_attn_w[i] = rowwise(nm + "norm_attn", norm_bwd_add2, T,
                                         [("row", xs_i), ("full", w_attn), ("row", dxn_a), ("row", dxn_b), ("row", dh1)],
                                         [("row", D, F32), ("row", D, BF16), ("acc", D)])

    small_names = ["attn_norm_w", "ffn_norm_w", "ple_norm_w", "final_norm_w", "ab_gla_gate_b", "ab_ret_norm_w",
                   "ab_gla_norm_w"]
    small_grads = [jnp.concatenate(d_attn_w, 0), jnp.concatenate(d_ffn_w, 0), jnp.concatenate(d_ple_w, 0), d_final_w[0],
                   d_gb, d_hn_w[:, :RET_V], d_hn_w[:, RET_V:]]
    small_w = [attn_norm_w, ffn_norm_w, ple_norm_w, final_norm_w, ab_gla_gate_b, ab_ret_norm_w, ab_gla_norm_w]
    small_m = [m_attn_norm_w, m_ffn_norm_w, m_ple_norm_w, m_final_norm_w, m_ab_gla_gate_b, m_ab_ret_norm_w, m_ab_gla_norm_w]
    small_v = [v_attn_norm_w, v_ffn_norm_w, v_ple_norm_w, v_final_norm_w, v_ab_gla_gate_b, v_ab_ret_norm_w, v_ab_gla_norm_w]
    sizes = [int(np.prod(a.shape)) for a in small_w]
    n_gu = GLA_GATE_RANK * GLA_QK
    n_small = _round_up(sum(sizes) + n_gu, LANE)
    pack = lambda parts: _pad_to(jnp.concatenate([a.reshape(-1) for a in parts]), 0, n_small)[None, :]
    small_part = pack(small_grads + [d_gu[:GLA_GATE_RANK]])

    big_w = dict(ab_w_in=(ab_w_in, m_ab_w_in, v_ab_w_in), ab_w_out=(ab_w_out, m_ab_w_out, v_ab_w_out),
                 c_w_qkv=(c_w_qkv, m_c_w_qkv, v_c_w_qkv), c_w_out=(c_w_out, m_c_w_out, v_c_w_out),
                 ffn_w_gate=(wg_t, tr_(m_ffn_w_gate), tr_(v_ffn_w_gate)),
                 ffn_w_up=(wu_t, tr_(m_ffn_w_up), tr_(v_ffn_w_up)),
                 ffn_w_down=(ffn_w_down, m_ffn_w_down, v_ffn_w_down), ple_w_proj=(ple_w_proj, m_ple_w_proj, v_ple_w_proj),
                 ple_w_gate=(ple_w_gate, m_ple_w_gate, v_ple_w_gate))
    if on_chip:
        scatter_start("scatter_last", [], deps=(dx,))
    results, last = {}, dx
    for gi, (keys, handle, partials) in enumerate(scatters):
        (arrived,), _, _ = exchange_call(f"scatter_wait{gi}", [("chip_sums", handle)], [], deps=(last,))
        for (n, l), (_, land), (a, half) in zip(keys, arrived, partials):
            results[n] = adamw(f"adamw_{n}{l}", *big_w[n], l, land, a, half, prev=results.get(n))
            last = results[n][0]
    for n in ("ffn_w_gate", "ffn_w_up"):
        results[n] = [tr_(a) for a in results[n]]
    small_parts = gather_small("gather_small", small_part, deps=(last,)).reshape(N_DEV, n_small)

    gu_off = sum(sizes)
    own_cols = lambda a: lax.dynamic_slice_in_dim(a.reshape(GLA_GATE_RANK, GLA_QK), me * gu_cols, gu_cols, axis=1)
    small_res = adamw_small("adamw_small", pack(small_w + [jnp.zeros((n_gu,), F32)]),
                            pack(small_m + [jnp.zeros((n_gu,), F32)]), pack(small_v + [jnp.ones((n_gu,), F32)]),
                            small_parts)
    g_gu_full = small_res[0][0, gu_off:gu_off + n_gu]
    g_gu = own_cols(g_gu_full)[None]
    gu_res = adamw_small("adamw_gate_up", *[_pad_to(a.reshape(1, -1), 1, _round_up(a.size, LANE)) for a in
                                            (ab_gla_gate_up, m_ab_gla_gate_up, v_ab_gla_gate_up)],
                         jnp.concatenate([_pad_to(g_gu.reshape(1, -1), 1, _round_up(g_gu.size, LANE)),
                                          jnp.zeros((N_DEV - 1, _round_up(g_gu.size, LANE)), F32)], axis=0))
    for k in range(4):
        off = 0
        for n, a, sz in zip(small_names, small_w, sizes):
            results.setdefault(n, [None] * 4)[k] = small_res[k][0, off:off + sz].reshape(a.shape)
            off += sz
        results.setdefault("ab_gla_gate_up", [None] * 4)[k] = gu_res[k][0, :g_gu.size].reshape(ab_gla_gate_up.shape)

    order = ["attn_norm_w", "ffn_norm_w", "ple_norm_w", "final_norm_w", "ab_w_in", "ab_gla_gate_up", "ab_gla_gate_b",
             "ab_ret_norm_w", "ab_gla_norm_w", "ab_w_out", "c_w_qkv", "c_w_out", "ffn_w_gate", "ffn_w_up", "ffn_w_down",
             "ple_w_proj", "ple_w_gate"]
    return (loss, dx[None], *[results[n][0] for n in order], *[results[n][1] for n in order],
            *[results[n][2] for n in order], *[results[n][3] for n in order])
```

```python
import math

import numpy as np
import jax
import jax.numpy as jnp
from jax import lax
from jax.experimental import pallas as pl
from jax.experimental.pallas import tpu as pltpu

F32 = jnp.float32
BF16 = jnp.bfloat16
HIGHEST = lax.Precision.HIGHEST

N_DEV = 8
VMEM_LIMIT_BYTES = 48 * 1024 * 1024
LANE = 128
NORM_EPS = 1e-6

RET_HEADS, RET_DK, RET_DV = 4, 256, 256
RET_THETA_BASE = 10000.0
GLA_HEADS, GLA_DK, GLA_DV = 4, 128, 256
GLA_GATE_RANK = 16
GLA_GATE_NORM = 16.0
CHUNK = 64
ATT_HEADS = 16
DILATED_BRANCHES = ((128, 1), (512, 4), (2048, 16))
BLK = 256

ADAM_LR, ADAM_B1, ADAM_B2, ADAM_EPS, ADAM_WD, ADAM_STEP = 0.001, 0.9, 0.999, 1e-08, 0.01, 10

RET_QK = RET_HEADS * RET_DK
RET_V = RET_HEADS * RET_DV
GLA_QK = GLA_HEADS * GLA_DK
GLA_V = GLA_HEADS * GLA_DV
OFF_RQ, OFF_RK, OFF_RV, OFF_RG = 0, RET_QK, 2 * RET_QK, 2 * RET_QK + RET_V
OFF_GQ = OFF_RG + RET_V
OFF_GK = OFF_GQ + GLA_QK
OFF_GV = OFF_GK + GLA_QK
OFF_GG = OFF_GV + GLA_V
OFF_LR = OFF_GG + GLA_V


def _params(*sem):
    return pltpu.CompilerParams(dimension_semantics=sem or None, vmem_limit_bytes=VMEM_LIMIT_BYTES)


def _pick(n, cands):
    for c in cands:
        if n % c == 0:
            return c
    raise ValueError(f"no tile for {n} in {cands}")


_NN = (((1,), (0,)), ((), ()))
_NT = (((1,), (1,)), ((), ()))
_TN = (((0,), (0,)), ((), ()))
_ANY = pl.BlockSpec(memory_space=pl.ANY)
MAX_CONTRACT = 2048
_TILES = (1024, 768, 512, 256, 128)


def _mm_call(name, dims, grid, in_specs, out_spec, out_shape, args, deps=()):
    steps = grid[2]
    assert steps == 1 or out_shape.dtype == F32

    def body(a_ref, b_ref, *rest):
        o_ref = rest[len(deps)]
        part = lax.dot_general(a_ref[...].astype(BF16), b_ref[...].astype(BF16), dims, preferred_element_type=F32)
        if steps == 1:
            o_ref[...] = part.astype(o_ref.dtype)
        else:
            _accumulate(o_ref, part, pl.program_id(2) == 0)

    return pl.pallas_call(
        body, name=name, grid=grid, in_specs=list(in_specs) + [_ANY] * len(deps), out_specs=out_spec,
        out_shape=out_shape, compiler_params=_params("parallel", "parallel", "arbitrary"))(*args, *deps)


def mm_nn(name, a, w, l, out_dtype, deps=()):
    _, J, K, n = w.shape
    M = a.shape[0]
    tm, tn, tk = _pick(M, _TILES), _pick(n, _TILES), _pick(K, (MAX_CONTRACT,) + _TILES)
    nt = n // tn
    return _mm_call(
        name, _NN, (M // tm, J * nt, K // tk),
        [pl.BlockSpec((tm, tk), lambda i, j, k: (i, k)),
         pl.BlockSpec((None, None, tk, tn), lambda i, j, k: (l, j // nt, k, j % nt))],
        pl.BlockSpec((tm, tn), lambda i, j, k: (i, j)),
        jax.ShapeDtypeStruct((M, J * n), out_dtype), (a, w), deps)


def mm_nt(name, a, w, l, out_dtype, deps=()):
    _, J, K, n = w.shape
    M = a.shape[0]
    tm, tq, tc = _pick(M, _TILES), _pick(K, _TILES), _pick(n, (MAX_CONTRACT,) + _TILES)
    nc = n // tc
    return _mm_call(
        name, _NT, (M // tm, K // tq, J * nc),
        [pl.BlockSpec((tm, tc), lambda i, q, c: (i, c)),
         pl.BlockSpec((None, None, tq, tc), lambda i, q, c: (l, c // nc, q, c % nc))],
        pl.BlockSpec((tm, tq), lambda i, q, c: (i, q)),
        jax.ShapeDtypeStruct((M, K), out_dtype), (a, w), deps)


def mm_tn(name, x, dy, J, out_dtype, deps=()):
    M, K = x.shape
    n = dy.shape[1] // J
    tp, tn = _pick(K, _TILES), _pick(n, _TILES)
    nt = n // tn
    assert M <= MAX_CONTRACT
    return _mm_call(
        name, _TN, (K // tp, J * nt, 1),
        [pl.BlockSpec((M, tp), lambda i, j, r: (0, i)),
         pl.BlockSpec((M, tn), lambda i, j, r: (0, j))],
        pl.BlockSpec((None, tp, tn), lambda i, j, r: (j // nt, i, j % nt)),
        jax.ShapeDtypeStruct((J, K, n), out_dtype), (x, dy), deps)


def mmt_fwd(name, a, wt, l, out_dtype, n=None, deps=()):
    _, J, rows, K = wt.shape
    n = rows if n is None else n
    M = a.shape[0]
    tm, tn = _pick(M, _TILES), _pick(n, _TILES)
    nt = n // tn
    assert K <= MAX_CONTRACT
    return _mm_call(
        name, _NT, (M // tm, J * nt, 1),
        [pl.BlockSpec((tm, K), lambda i, j, k: (i, 0)),
         pl.BlockSpec((None, None, tn, K), lambda i, j, k: (l, j // nt, j % nt, 0))],
        pl.BlockSpec((tm, tn), lambda i, j, k: (i, j)),
        jax.ShapeDtypeStruct((M, J * n), out_dtype), (a, wt), deps)


def mmt_dx(name, dy, wt, l, out_dtype, n=None, deps=()):
    _, J, rows, K = wt.shape
    n = rows if n is None else n
    M = dy.shape[0]
    tm, tq, tc = _pick(M, _TILES), _pick(K, _TILES), _pick(n, _TILES)
    nc = n // tc
    return _mm_call(
        name, _NN, (M // tm, K // tq, J * nc),
        [pl.BlockSpec((tm, tc), lambda i, q, c: (i, c)),
         pl.BlockSpec((None, None, tc, tq), lambda i, q, c: (l, c // nc, c % nc, q))],
        pl.BlockSpec((tm, tq), lambda i, q, c: (i, q)),
        jax.ShapeDtypeStruct((M, K), out_dtype), (dy, wt), deps)


WIDE_TILE = 512


def _wide_call(name, body, M, K, a, w, a_spec, w_spec, out_dtype, deps):
    def kernel_body(a_ref, w_ref, *rest):
        o_ref = rest[len(deps)]
        o_ref[...] = body(a_ref, w_ref).astype(o_ref.dtype)

    return pl.pallas_call(
        kernel_body, name=name, grid=(M // WIDE_TILE, K // WIDE_TILE),
        in_specs=[a_spec, w_spec] + [_ANY] * len(deps),
        out_specs=pl.BlockSpec((WIDE_TILE, WIDE_TILE), lambda i, q: (i, q)),
        out_shape=jax.ShapeDtypeStruct((M, K), out_dtype),
        compiler_params=_params("parallel", "parallel"))(a, w, *deps)


def mmt_dx_wide(name, dy, wt, out_dtype, n=None, deps=()):
    _, J, rows, K = wt.shape
    n = rows if n is None else n
    M = dy.shape[0]

    def body(dy_ref, w_ref):
        return jnp.dot(dy_ref[...].astype(BF16), w_ref[...].reshape(J * n, WIDE_TILE), preferred_element_type=F32)

    return _wide_call(name, body, M, K, dy, wt,
                      pl.BlockSpec((WIDE_TILE, J * n), lambda i, q: (i, 0)),
                      pl.BlockSpec((None, J, n, WIDE_TILE), lambda i, q: (0, 0, 0, q)), out_dtype, deps)


def mm_nt_wide(name, a, w, out_dtype, deps=()):
    _, J, K, n = w.shape
    M = a.shape[0]

    def body(a_ref, w_ref):
        acc = None
        for j in range(J):
            part = lax.dot_general(a_ref[:, j * n:(j + 1) * n].astype(BF16), w_ref[j], _NT, preferred_element_type=F32)
            acc = part if acc is None else acc + part
        return acc

    return _wide_call(name, body, M, K, a, w,
                      pl.BlockSpec((WIDE_TILE, J * n), lambda i, q: (i, 0)),
                      pl.BlockSpec((None, J, WIDE_TILE, n), lambda i, q: (0, 0, q, 0)), out_dtype, deps)


def mmt_dw(name, dy, x, J, out_dtype, deps=()):
    M, K = x.shape
    n = dy.shape[1] // J
    tn, tp = _pick(n, _TILES), _pick(K, _TILES)
    nt = n // tn
    assert M <= MAX_CONTRACT
    return _mm_call(
        name, _TN, (J * nt, K // tp, 1),
        [pl.BlockSpec((M, tn), lambda j, i, r: (0, j)),
         pl.BlockSpec((M, tp), lambda j, i, r: (0, i))],
        pl.BlockSpec((None, tn, tp), lambda j, i, r: (j // nt, j % nt, i)),
        jax.ShapeDtypeStruct((J, n, K), out_dtype), (dy, x), deps)


def ffn_gate_up(name, a, wg, wu):
    _, J, n, K = wg.shape
    M = a.shape[0]
    tm, tn = _pick(M, _TILES), _pick(n, _TILES)
    nt = n // tn
    assert K <= MAX_CONTRACT

    def body(a_ref, wg_ref, wu_ref, g_ref, u_ref, act_ref):
        x = a_ref[...]
        g = lax.dot_general(x, wg_ref[...], _NT, preferred_element_type=F32)
        u = lax.dot_general(x, wu_ref[...], _NT, preferred_element_type=F32)
        g_ref[...] = g.astype(g_ref.dtype)
        u_ref[...] = u.astype(u_ref.dtype)
        act_ref[...] = (_silu_and_grad(g)[0] * u).astype(act_ref.dtype)

    w_spec = pl.BlockSpec((None, None, tn, K), lambda i, j: (0, j // nt, j % nt, 0))
    out = pl.BlockSpec((tm, tn), lambda i, j: (i, j))
    return pl.pallas_call(
        body, name=name, grid=(M // tm, J * nt),
        in_specs=[pl.BlockSpec((tm, K), lambda i, j: (i, 0)), w_spec, w_spec],
        out_specs=[out] * 3, out_shape=[jax.ShapeDtypeStruct((M, J * n), BF16)] * 3,
        compiler_params=_params("parallel", "parallel"))(a, wg, wu)


def mm_add_norm(name, a, w, res, norm_w):
    _, _, K, N = w.shape
    M = a.shape[0]
    tm, tk = _pick(M, (WIDE_TILE, 256)), _pick(K, (1024, 512, 256))
    steps = K // tk

    def body(a_ref, w_ref, res_ref, nw_ref, h_ref, hn_ref):
        k = pl.program_id(1)
        part = jnp.dot(a_ref[...], w_ref[...], preferred_element_type=F32)
        _accumulate(h_ref, part, k == 0)

        @pl.when(k == steps - 1)
        def _():
            h = h_ref[...] + res_ref[...]
            h_ref[...] = h
            hn_ref[...] = _rms(h, nw_ref[...]).astype(hn_ref.dtype)

    rows = pl.BlockSpec((tm, N), lambda i, k: (i, 0))
    return pl.pallas_call(
        body, name=name, grid=(M // tm, steps),
        in_specs=[pl.BlockSpec((tm, tk), lambda i, k: (i, k)),
                  pl.BlockSpec((None, None, tk, N), lambda i, k: (0, 0, k, 0)), rows,
                  pl.BlockSpec((1, N), lambda i, k: (0, 0))],
        out_specs=[rows, rows],
        out_shape=[jax.ShapeDtypeStruct((M, N), F32), jax.ShapeDtypeStruct((M, N), BF16)],
        compiler_params=_params("parallel", "arbitrary"))(a, w, res, norm_w)


def ple_fwd(name, pn, wpg, p_in, wpp, h):
    _, J, P, n = wpp.shape
    M, D = h.shape
    tm, tn = _pick(M, (WIDE_TILE, 256)), _pick(D, _TILES)
    per_tile = tn // n

    def body(pn_ref, wg_ref, p_ref, wp_ref, h_ref, x_ref, s_ref, e_ref):
        s = jnp.dot(pn_ref[...], wg_ref[...], preferred_element_type=F32)
        p_blk = p_ref[...]
        e = jnp.concatenate([jnp.dot(p_blk, wp_ref[j], preferred_element_type=F32) for j in range(per_tile)], axis=1)
        s_ref[...] = s
        e_ref[...] = e
        x_ref[...] = h_ref[...] + _sigmoid(s) * e

    tile = pl.BlockSpec((tm, tn), lambda i, j: (i, j))
    return pl.pallas_call(
        body, name=name, grid=(M // tm, D // tn),
        in_specs=[pl.BlockSpec((tm, D), lambda i, j: (i, 0)),
                  pl.BlockSpec((None, None, D, tn), lambda i, j: (0, 0, 0, j)),
                  pl.BlockSpec((tm, P), lambda i, j: (i, 0)),
                  pl.BlockSpec((None, per_tile, P, n), lambda i, j: (0, j, 0, 0)), tile],
        out_specs=[tile] * 3, out_shape=[jax.ShapeDtypeStruct((M, D), F32)] * 3,
        compiler_params=_params("parallel", "parallel"))(pn, wpg, p_in, wpp, h)


def ffn_down_bwd(name, dy, wd, g, u, deps=()):
    _, _, K, n = wd.shape
    M = dy.shape[0]
    tm, tq = _pick(M, _TILES), _pick(K, _TILES)
    assert n <= MAX_CONTRACT

    def body(dy_ref, w_ref, g_ref, u_ref, *rest):
        dg_ref, du_ref = rest[len(deps):]
        dact = lax.dot_general(dy_ref[...], w_ref[...], _NT, preferred_element_type=F32)
        silu, dsilu = _silu_and_grad(g_ref[...].astype(F32))
        dg_ref[...] = (dact * u_ref[...].astype(F32) * dsilu).astype(dg_ref.dtype)
        du_ref[...] = (dact * silu).astype(du_ref.dtype)

    blk = pl.BlockSpec((tm, tq), lambda i, q: (i, q))
    return pl.pallas_call(
        body, name=name, grid=(M // tm, K // tq),
        in_specs=[pl.BlockSpec((tm, n), lambda i, q: (i, 0)),
                  pl.BlockSpec((None, None, tq, n), lambda i, q: (0, 0, q, 0)), blk, blk] + [_ANY] * len(deps),
        out_specs=[blk, blk], out_shape=[jax.ShapeDtypeStruct((M, K), BF16)] * 2,
        compiler_params=_params("parallel", "parallel"))(dy, wd, g, u, *deps)


def rowwise(name, fn, rows, ins, outs, tr=256, deps=()):
    widest = max([s[1].shape[1] if s[0] != "col" else s[3] for s in ins] + [s[1] for s in outs])
    tr = min(tr if widest <= 2048 else tr // 2, rows)
    in_specs, args = [], []
    for spec in ins:
        kind, a = spec[0], spec[1]
        if kind == "row":
            in_specs.append(pl.BlockSpec((tr, a.shape[1]), lambda i: (i, 0)))
        elif kind == "col":
            cb, width = spec[2], spec[3]
            in_specs.append(pl.BlockSpec((tr, width), lambda i, cb=cb: (i, cb)))
        else:
            in_specs.append(pl.BlockSpec(a.shape, lambda i: (0, 0)))
        args.append(a)
    out_specs, out_shapes = [], []
    for spec in outs:
        if spec[0] == "row":
            out_specs.append(pl.BlockSpec((tr, spec[1]), lambda i: (i, 0)))
            out_shapes.append(jax.ShapeDtypeStruct((rows, spec[1]), spec[2]))
        else:
            out_specs.append(pl.BlockSpec((1, spec[1]), lambda i: (0, 0)))
            out_shapes.append(jax.ShapeDtypeStruct((1, spec[1]), F32))
    n_in = len(ins)

    def body(*refs):
        vals = fn(*[r[...] for r in refs[:n_in]])
        first = pl.program_id(0) == 0
        for r, v, spec in zip(refs[n_in + len(deps):], vals, outs):
            if spec[0] == "row":
                r[...] = v.astype(r.dtype)
            else:
                _accumulate(r, v, first)

    return pl.pallas_call(body, name=name, grid=(rows // tr,), in_specs=in_specs + [_ANY] * len(deps),
                          out_specs=out_specs, out_shape=out_shapes,
                          compiler_params=_params("arbitrary"))(*args, *deps)


def _accumulate(ref, v, first):
    @pl.when(first)
    def _():
        ref[...] = v

    @pl.when(jnp.logical_not(first))
    def _():
        ref[...] += v


def _rms(x, w):
    r = lax.rsqrt(jnp.mean(x * x, axis=-1, keepdims=True) + NORM_EPS)
    return x * r * w


def _rms_bwd(x, w, dy):
    r = lax.rsqrt(jnp.mean(x * x, axis=-1, keepdims=True) + NORM_EPS)
    g = dy * w
    dx = r * (g - x * (r * r) * jnp.mean(g * x, axis=-1, keepdims=True))
    dw = jnp.sum(dy * x * r, axis=0, keepdims=True)
    return dx, dw


def _sigmoid(x):
    return 1.0 / (1.0 + jnp.exp(-x))


def _silu_and_grad(g):
    s = _sigmoid(g)
    return g * s, s * (1.0 + g * (1.0 - s))


def _swap_pairs(x):
    n = x.shape[-1]
    lane = lax.broadcasted_iota(jnp.int32, x.shape, x.ndim - 1)
    return jnp.where((lane & 1) == 0, pltpu.roll(x, n - 1, x.ndim - 1), pltpu.roll(x, 1, x.ndim - 1))


def _rot(x, cosf, sins):
    return x * cosf + _swap_pairs(x) * sins


def _unrot(d, cosf, sins):
    return d * cosf + _swap_pairs(d * sins)


def _ret_log_gamma(h):
    vals = [math.log1p(-2.0 ** (-5.0 - i)) for i in range(RET_HEADS)]
    out = jnp.float32(vals[RET_HEADS - 1])
    for i in range(RET_HEADS - 2, -1, -1):
        out = jnp.where(h == i, jnp.float32(vals[i]), out)
    return out


def _fill_decays(dec_ref, lg):
    ri = lax.broadcasted_iota(jnp.int32, (BLK, BLK), 0)
    ci = lax.broadcasted_iota(jnp.int32, (BLK, BLK), 1)
    for d in range(dec_ref.shape[0]):
        dt = d * BLK + ri - ci
        dec_ref[d] = jnp.where(dt >= 0, jnp.exp(jnp.maximum(dt, 0).astype(F32) * lg), 0.0)


def _decay_row(dec_ref, qi):
    return jnp.concatenate([dec_ref[qi - kb] for kb in range(qi + 1)], axis=1)


def _once(block_shape, index_map):
    return pl.BlockSpec(block_shape, index_map, pipeline_mode=pl.Buffered(1))


def _dot(a, b):
    return jnp.dot(a.astype(BF16), b.astype(BF16), preferred_element_type=F32)


def _dot_nt(a, b):
    return lax.dot_general(a.astype(BF16), b.astype(BF16), _NT, preferred_element_type=F32)


def _dot_tn(a, b):
    return lax.dot_general(a.astype(BF16), b.astype(BF16), _TN, preferred_element_type=F32)


def retention_fwd(name, z, cosf, sins, width_out):
    T = z.shape[0]
    nq = T // BLK
    scale = RET_DK ** -0.5

    def body(q_ref, k_ref, v_ref, cos_ref, sin_ref, o_ref, krot, vb, dec_ref):
        _fill_decays(dec_ref, _ret_log_gamma(pl.program_id(0)))
        krot[...] = (_rot(k_ref[...], cos_ref[...], sin_ref[...]) * scale).astype(BF16)
        vb[...] = v_ref[...].astype(BF16)
        for qi in range(nq):
            rows, n = slice(qi * BLK, (qi + 1) * BLK), (qi + 1) * BLK
            q = _rot(q_ref[rows, :], cos_ref[rows, :], sin_ref[rows, :])
            s = _dot_nt(q, krot[0:n, :]) * _decay_row(dec_ref, qi)
            o_ref[rows, :] = _dot(s, vb[0:n, :])

    return pl.pallas_call(
        body, name=name, grid=(RET_HEADS,),
        in_specs=[pl.BlockSpec((T, RET_DK), lambda h: (0, OFF_RQ // RET_DK + h)),
                  pl.BlockSpec((T, RET_DK), lambda h: (0, OFF_RK // RET_DK + h)),
                  pl.BlockSpec((T, RET_DV), lambda h: (0, OFF_RV // RET_DV + h)),
                  _once((T, RET_DK), lambda h: (0, 0)), _once((T, RET_DK), lambda h: (0, 0))],
        out_specs=pl.BlockSpec((T, RET_DV), lambda h: (0, h)),
        out_shape=jax.ShapeDtypeStruct((T, width_out), F32),
        scratch_shapes=[pltpu.VMEM((T, RET_DK), BF16), pltpu.VMEM((T, RET_DV), BF16),
                        pltpu.VMEM((nq, BLK, BLK), F32)],
        compiler_params=_params("arbitrary"))(z, z, z, cosf, sins)


def retention_bwd(name, z, cosf, sins, do):
    T = z.shape[0]
    nq = T // BLK
    scale = RET_DK ** -0.5

    def body(q_ref, k_ref, v_ref, cos_ref, sin_ref, do_ref, dq_ref, dk_ref, dv_ref, krot, vb, dk_acc, dv_acc, dec_ref):
        _fill_decays(dec_ref, _ret_log_gamma(pl.program_id(0)))
        krot[...] = (_rot(k_ref[...], cos_ref[...], sin_ref[...]) * scale).astype(BF16)
        vb[...] = v_ref[...].astype(BF16)
        dk_acc[...] = jnp.zeros_like(dk_acc)
        dv_acc[...] = jnp.zeros_like(dv_acc)
        for qi in range(nq):
            rows, n = slice(qi * BLK, (qi + 1) * BLK), (qi + 1) * BLK
            cos_q, sin_q = cos_ref[rows, :], sin_ref[rows, :]
            q = _rot(q_ref[rows, :], cos_q, sin_q).astype(BF16)
            dout = do_ref[rows, :].astype(BF16)
            kk, vv, dec = krot[0:n, :], vb[0:n, :], _decay_row(dec_ref, qi)
            p = (_dot_nt(q, kk) * dec).astype(BF16)
            ds = (_dot_nt(dout, vv) * dec).astype(BF16)
            dq_ref[rows, :] = _unrot(_dot(ds, kk), cos_q, sin_q).astype(dq_ref.dtype)
            dk_acc[0:n, :] += _dot_tn(ds, q)
            dv_acc[0:n, :] += _dot_tn(p, dout)
        dk_ref[...] = (_unrot(dk_acc[...], cos_ref[...], sin_ref[...]) * scale).astype(dk_ref.dtype)
        dv_ref[...] = dv_acc[...].astype(dv_ref.dtype)

    head = lambda h: (0, h)
    return pl.pallas_call(
        body, name=name, grid=(RET_HEADS,),
        in_specs=[pl.BlockSpec((T, RET_DK), lambda h: (0, OFF_RQ // RET_DK + h)),
                  pl.BlockSpec((T, RET_DK), lambda h: (0, OFF_RK // RET_DK + h)),
                  pl.BlockSpec((T, RET_DV), lambda h: (0, OFF_RV // RET_DV + h)),
                  _once((T, RET_DK), lambda h: (0, 0)), _once((T, RET_DK), lambda h: (0, 0)),
                  pl.BlockSpec((T, RET_DV), head)],
        out_specs=[pl.BlockSpec((T, RET_DK), head), pl.BlockSpec((T, RET_DK), head), pl.BlockSpec((T, RET_DV), head)],
        out_shape=[jax.ShapeDtypeStruct((T, RET_QK), BF16), jax.ShapeDtypeStruct((T, RET_QK), BF16),
                   jax.ShapeDtypeStruct((T, RET_V), BF16)],
        scratch_shapes=[pltpu.VMEM((T, RET_DK), BF16), pltpu.VMEM((T, RET_DV), BF16),
                        pltpu.VMEM((T, RET_DK), F32), pltpu.VMEM((T, RET_DV), F32),
                        pltpu.VMEM((nq, BLK, BLK), F32)],
        compiler_params=_params("arbitrary"))(z, z, z, cosf, sins, do)


GLA_PAIR = 2


def _gla_chunk(q_ref, k_ref, v_ref, glr_ref, gu, gb, rows, hh, trilf):
    ck = slice(hh * GLA_DK, (hh + 1) * GLA_DK)
    zg = _dot(glr_ref[rows, :], gu[:, ck]) + gb[:, ck]
    la = (jnp.minimum(zg, 0.0) - jnp.log(1.0 + jnp.exp(-jnp.abs(zg)))) * (1.0 / GLA_GATE_NORM)
    cum = jnp.dot(trilf, la, precision=HIGHEST, preferred_element_type=F32)
    last = jnp.sum(la, axis=0, keepdims=True)
    ecum = jnp.exp(cum)
    k = k_ref[rows, ck]
    qt = q_ref[rows, ck] * (GLA_DK ** -0.5) * ecum
    kt = k * jnp.exp(-cum)
    kh = k * jnp.exp(last - cum)
    return zg, cum, last, ecum, qt, kt, kh, v_ref[rows, hh * GLA_DV:(hh + 1) * GLA_DV].astype(BF16)


def _state_decay(last):
    e = jnp.exp(jnp.broadcast_to(last, (GLA_DK, GLA_DK)).T)
    return jnp.concatenate([e] * (GLA_DV // GLA_DK), axis=1)


def _gla_specs(T):
    wk, wv = GLA_PAIR * GLA_DK, GLA_PAIR * GLA_DV
    return [_once((T, wk), lambda h: (0, OFF_GQ // wk + h)),
            _once((T, wk), lambda h: (0, OFF_GK // wk + h)),
            _once((T, wv), lambda h: (0, OFF_GV // wv + h)),
            _once((T, LANE), lambda h: (0, 0)),
            pl.BlockSpec((LANE, wk), lambda h: (0, h)),
            pl.BlockSpec((1, wk), lambda h: (0, h))]


def gla_fwd(name, z, glr, gu, gb, o_prev):
    T = z.shape[0]
    nc = T // CHUNK
    wv = GLA_PAIR * GLA_DV

    def body(q_ref, k_ref, v_ref, glr_ref, gu_ref, gb_ref, prev_ref, o_ref, S):
        del prev_ref
        gu_b, gb_v = gu_ref[...].astype(BF16), gb_ref[...]
        ri = lax.broadcasted_iota(jnp.int32, (CHUNK, CHUNK), 0)
        ci = lax.broadcasted_iota(jnp.int32, (CHUNK, CHUNK), 1)
        tril = ri >= ci
        trilf = tril.astype(F32)
        S[...] = jnp.zeros_like(S)

        def step(c, carry):
            rows = pl.ds(pl.multiple_of(c * CHUNK, CHUNK), CHUNK)
            for hh in range(GLA_PAIR):
                _, _, last, _, qt, kt, kh, v = _gla_chunk(q_ref, k_ref, v_ref, glr_ref, gu_b, gb_v, rows, hh, trilf)
                a = jnp.where(tril, _dot_nt(qt, kt), 0.0)
                s_prev = S[hh]
                o_ref[rows, hh * GLA_DV:(hh + 1) * GLA_DV] = _dot(a, v) + _dot(qt, s_prev)
                S[hh] = s_prev * _state_decay(last) + _dot_tn(kh, v)
            return carry

        lax.fori_loop(0, nc, step, 0)

    n_in = 6
    return pl.pallas_call(
        body, name=name, grid=(GLA_HEADS // GLA_PAIR,),
        in_specs=_gla_specs(T) + [pl.BlockSpec(memory_space=pl.ANY)],
        out_specs=pl.BlockSpec((T, wv), lambda h: (0, RET_V // wv + h)),
        out_shape=jax.ShapeDtypeStruct(o_prev.shape, F32),
        scratch_shapes=[pltpu.VMEM((GLA_PAIR, GLA_DK, GLA_DV), F32)],
        input_output_aliases={n_in: 0},
        compiler_params=_params("arbitrary"))(z, z, z, glr, gu, gb, o_prev)


def gla_bwd(name, z, glr, gu, gb, do):
    T = z.shape[0]
    nc = T // CHUNK

    def body(q_ref, k_ref, v_ref, glr_ref, gu_ref, gb_ref, do_ref,
             dq_ref, dk_ref, dv_ref, dglr_ref, dgu_ref, dgb_ref, s_all, dS):
        gu_b, gb_v = gu_ref[...].astype(BF16), gb_ref[...]
        ri = lax.broadcasted_iota(jnp.int32, (CHUNK, CHUNK), 0)
        ci = lax.broadcasted_iota(jnp.int32, (CHUNK, CHUNK), 1)
        tril = ri >= ci
        trilf = tril.astype(F32)
        triuf = (ri <= ci).astype(F32)
        last_row = lax.broadcasted_iota(jnp.int32, (CHUNK, GLA_DK), 0) == CHUNK - 1
        ones8 = jnp.ones((8, GLA_DV), F32)

        def fstep(c, carry):
            rows = pl.ds(pl.multiple_of(c * CHUNK, CHUNK), CHUNK)
            for hh in range(GLA_PAIR):
                s_prev = dS[hh]
                s_all[hh, c] = s_prev
                _, _, last, _, _, _, kh, v = _gla_chunk(q_ref, k_ref, v_ref, glr_ref, gu_b, gb_v, rows, hh, trilf)
                dS[hh] = s_prev * _state_decay(last) + _dot_tn(kh, v)
            return carry

        dS[...] = jnp.zeros_like(dS)
        lax.fori_loop(0, nc, fstep, 0)
        dS[...] = jnp.zeros_like(dS)
        dgu_ref[...] = jnp.zeros_like(dgu_ref)
        dgb_ref[...] = jnp.zeros_like(dgb_ref)

        def bstep(i, carry):
            c = nc - 1 - i
            rows = pl.ds(pl.multiple_of(c * CHUNK, CHUNK), CHUNK)
            glr_c = glr_ref[rows, :]
            for hh in range(GLA_PAIR):
                ck, cv = slice(hh * GLA_DK, (hh + 1) * GLA_DK), slice(hh * GLA_DV, (hh + 1) * GLA_DV)
                zg, cum, last, ecum, qt, kt, kh, v = _gla_chunk(q_ref, k_ref, v_ref, glr_ref, gu_b, gb_v, rows, hh, trilf)
                a = jnp.where(tril, _dot_nt(qt, kt), 0.0)
                s_prev, ds_new = s_all[hh, c], dS[hh]
                dout = do_ref[rows, cv].astype(BF16)
                dv_ref[rows, cv] = (_dot_tn(a, dout) + _dot(kh, ds_new)).astype(dv_ref.dtype)
                da = jnp.where(tril, _dot_nt(dout, v), 0.0)
                dqt = _dot(da, kt) + _dot_nt(dout, s_prev)
                dkt = _dot_tn(da, qt)
                dkh = _dot_nt(v, ds_new)
                dS[hh] = ds_new * _state_decay(last) + _dot_tn(qt, dout)
                dq_ref[rows, ck] = (dqt * ecum * (GLA_DK ** -0.5)).astype(dq_ref.dtype)
                dk_ref[rows, ck] = (dkt * jnp.exp(-cum) + dkh * jnp.exp(last - cum)).astype(dk_ref.dtype)
                dkh_kh = dkh * kh
                dcum = dqt * qt - dkt * kt - dkh_kh
                rs = lax.dot_general(ones8, ds_new * s_prev, _NT, precision=HIGHEST, preferred_element_type=F32)
                dlast = (jnp.sum(dkh_kh, axis=0, keepdims=True)
                         + jnp.exp(last) * (jnp.sum(rs, axis=0, keepdims=True) * 0.125))
                dcum = dcum + jnp.where(last_row, dlast, 0.0)
                dla = jnp.dot(triuf, dcum, precision=HIGHEST, preferred_element_type=F32)
                dzg = dla * (1.0 / GLA_GATE_NORM) * _sigmoid(-zg)
                dglr_ref[hh, rows, :] = _dot_nt(dzg, gu_b[:, ck])
                dgu_ref[:, ck] += _dot_tn(glr_c, dzg)
                dgb_ref[:, ck] += jnp.sum(dzg, axis=0, keepdims=True)
            return carry

        lax.fori_loop(0, nc, bstep, 0)

    wk, wv = GLA_PAIR * GLA_DK, GLA_PAIR * GLA_DV
    return pl.pallas_call(
        body, name=name, grid=(GLA_HEADS // GLA_PAIR,),
        in_specs=_gla_specs(T) + [_once((T, wv), lambda h: (0, RET_V // wv + h))],
        out_specs=[pl.BlockSpec((T, wk), lambda h: (0, h)), pl.BlockSpec((T, wk), lambda h: (0, h)),
                   pl.BlockSpec((T, wv), lambda h: (0, h)),
                   pl.BlockSpec((GLA_PAIR, T, LANE), lambda h: (h, 0, 0)),
                   pl.BlockSpec((LANE, wk), lambda h: (0, h)), pl.BlockSpec((1, wk), lambda h: (0, h))],
        out_shape=[jax.ShapeDtypeStruct((T, GLA_QK), BF16), jax.ShapeDtypeStruct((T, GLA_QK), BF16),
                   jax.ShapeDtypeStruct((T, GLA_V), BF16), jax.ShapeDtypeStruct((GLA_HEADS, T, LANE), F32),
                   jax.ShapeDtypeStruct((LANE, GLA_QK), F32), jax.ShapeDtypeStruct((1, GLA_QK), F32)],
        scratch_shapes=[pltpu.VMEM((GLA_PAIR, nc, GLA_DK, GLA_DV), F32), pltpu.VMEM((GLA_PAIR, GLA_DK, GLA_DV), F32)],
        compiler_params=_params("arbitrary"))(z, z, z, glr, gu, gb, do)


HN_HEADS = RET_HEADS + GLA_HEADS
HN_W = RET_DV


def _gate_col(h):
    return jnp.where(h < RET_HEADS, OFF_RG // HN_W + h, OFF_GG // HN_W + h - RET_HEADS)


def headnorm_fwd(name, oraw, z, w):
    T = oraw.shape[0]
    tr = _pick(T, _TILES)

    def body(o_ref, g_ref, w_ref, y_ref):
        y_ref[...] = (_rms(o_ref[...], w_ref[...]) * _silu_and_grad(g_ref[...])[0]).astype(y_ref.dtype)

    return pl.pallas_call(
        body, name=name, grid=(HN_HEADS, T // tr),
        in_specs=[pl.BlockSpec((tr, HN_W), lambda h, i: (i, h)),
                  pl.BlockSpec((tr, HN_W), lambda h, i: (i, _gate_col(h))),
                  pl.BlockSpec((1, HN_W), lambda h, i: (0, h))],
        out_specs=pl.BlockSpec((tr, HN_W), lambda h, i: (i, h)),
        out_shape=jax.ShapeDtypeStruct((T, HN_HEADS * HN_W), BF16),
        compiler_params=_params("arbitrary", "arbitrary"))(oraw, z, w)


def headnorm_bwd(name, oraw, z, w, dy):
    T = oraw.shape[0]
    tr = _pick(T, _TILES)

    def body(o_ref, g_ref, w_ref, dy_ref, do_ref, dg_ref, dw_ref):
        o, wv, dyv = o_ref[...], w_ref[...], dy_ref[...].astype(F32)
        silu, dsilu = _silu_and_grad(g_ref[...])
        n = _rms(o, wv)
        dg_ref[...] = (dyv * n * dsilu).astype(dg_ref.dtype)
        dx, dw = _rms_bwd(o, wv, dyv * silu)
        do_ref[...] = dx
        _accumulate(dw_ref, dw, pl.program_id(1) == 0)

    blk = pl.BlockSpec((tr, HN_W), lambda h, i: (i, h))
    return pl.pallas_call(
        body, name=name, grid=(HN_HEADS, T // tr),
        in_specs=[blk, pl.BlockSpec((tr, HN_W), lambda h, i: (i, _gate_col(h))),
                  pl.BlockSpec((1, HN_W), lambda h, i: (0, h)), blk],
        out_specs=[blk, blk, pl.BlockSpec((1, HN_W), lambda h, i: (0, h))],
        out_shape=[jax.ShapeDtypeStruct((T, HN_HEADS * HN_W), F32),
                   jax.ShapeDtypeStruct((T, HN_HEADS * HN_W), BF16),
                   jax.ShapeDtypeStruct((1, HN_HEADS * HN_W), F32)],
        compiler_params=_params("arbitrary", "arbitrary"))(oraw, z, w, dy)


N_MASKS = 4


def _check_mask_classes(T):
    for window, dilation in DILATED_BRANCHES[:-1]:
        assert window < (N_MASKS - 1) * BLK - (BLK - 1) and BLK % dilation == 0
    assert DILATED_BRANCHES[-1][0] >= T and BLK % DILATED_BRANCHES[-1][1] == 0


def _fill_masks(mult_ref, bias_ref):
    ri = lax.broadcasted_iota(jnp.int32, (BLK, BLK), 0)
    ci = lax.broadcasted_iota(jnp.int32, (BLK, BLK), 1)
    for d in range(N_MASKS):
        dt = d * BLK + ri - ci
        mult = jnp.zeros((BLK, BLK), F32)
        for window, dilation in DILATED_BRANCHES:
            hit = (dt >= 0) & (dt <= window) & ((dt & (dilation - 1)) == 0)
            mult = mult + hit.astype(F32)
        mult_ref[d] = mult
        bias_ref[d] = jnp.where(mult > 0, 0.0, -1e30)


def _mask_row(ref, qi):
    return jnp.concatenate([ref[min(qi - kb, N_MASKS - 1)] for kb in range(qi + 1)], axis=1)


def attn_fwd(name, qkv):
    T = qkv.shape[0]
    D = qkv.shape[1] // 3
    dh = D // ATT_HEADS
    nq = T // BLK
    scale = dh ** -0.5

    _check_mask_classes(T)

    def body(q_ref, k_ref, v_ref, o_ref, lse_ref, mult_ref, bias_ref):
        @pl.when(pl.program_id(0) == 0)
        def _():
            _fill_masks(mult_ref, bias_ref)

        for qi in range(nq):
            rows, n = slice(qi * BLK, (qi + 1) * BLK), (qi + 1) * BLK
            s = (_dot_nt(q_ref[rows, :], k_ref[0:n, :]) * scale
                 + _mask_row(bias_ref, qi))
            m = jnp.max(s, axis=-1, keepdims=True)
            p = _mask_row(mult_ref, qi) * jnp.exp(s - m)
            l = jnp.sum(p, axis=-1, keepdims=True)
            o_ref[rows, :] = (_dot(p, v_ref[0:n, :]) / l).astype(o_ref.dtype)
            lse_ref[rows, :] = jnp.broadcast_to(m + jnp.log(l), (BLK, LANE))

    return pl.pallas_call(
        body, name=name, grid=(ATT_HEADS,),
        in_specs=[pl.BlockSpec((T, dh), lambda h: (0, h)),
                  pl.BlockSpec((T, dh), lambda h: (0, ATT_HEADS + h)),
                  pl.BlockSpec((T, dh), lambda h: (0, 2 * ATT_HEADS + h))],
        out_specs=[pl.BlockSpec((T, dh), lambda h: (0, h)),
                   pl.BlockSpec((None, T, LANE), lambda h: (h, 0, 0))],
        out_shape=[jax.ShapeDtypeStruct((T, D), BF16), jax.ShapeDtypeStruct((ATT_HEADS, T, LANE), F32)],
        scratch_shapes=[pltpu.VMEM((N_MASKS, BLK, BLK), F32), pltpu.VMEM((N_MASKS, BLK, BLK), F32)],
        compiler_params=_params("arbitrary"))(qkv, qkv, qkv)


def attn_bwd(name, qkv, o, lse, do):
    T = qkv.shape[0]
    D = qkv.shape[1] // 3
    dh = D // ATT_HEADS
    nq = T // BLK
    scale = dh ** -0.5

    _check_mask_classes(T)

    def body(q_ref, k_ref, v_ref, o_ref, lse_ref, do_ref, dq_ref, dk_ref, dv_ref, dk_acc, dv_acc, mult_ref, bias_ref):
        @pl.when(pl.program_id(0) == 0)
        def _():
            _fill_masks(mult_ref, bias_ref)

        dk_acc[...] = jnp.zeros_like(dk_acc)
        dv_acc[...] = jnp.zeros_like(dv_acc)
        for qi in range(nq):
            rows, n = slice(qi * BLK, (qi + 1) * BLK), (qi + 1) * BLK
            q, dout = q_ref[rows, :], do_ref[rows, :]
            kk, vv = k_ref[0:n, :], v_ref[0:n, :]
            delta = jnp.sum(dout.astype(F32) * o_ref[rows, :].astype(F32), axis=-1, keepdims=True)
            lse = jnp.max(lse_ref[rows, :], axis=-1, keepdims=True)
            s = _dot_nt(q, kk) * scale + _mask_row(bias_ref, qi)
            p = _mask_row(mult_ref, qi) * jnp.exp(s - lse)
            ds = (p * (_dot_nt(dout, vv) - delta) * scale).astype(BF16)
            dq_ref[rows, :] = _dot(ds, kk).astype(dq_ref.dtype)
            dk_acc[0:n, :] += _dot_tn(ds, q)
            dv_acc[0:n, :] += _dot_tn(p, dout)
        dk_ref[...] = dk_acc[...].astype(dk_ref.dtype)
        dv_ref[...] = dv_acc[...].astype(dv_ref.dtype)

    full = pl.BlockSpec((T, dh), lambda h: (0, h))
    return pl.pallas_call(
        body, name=name, grid=(ATT_HEADS,),
        in_specs=[full, pl.BlockSpec((T, dh), lambda h: (0, ATT_HEADS + h)),
                  pl.BlockSpec((T, dh), lambda h: (0, 2 * ATT_HEADS + h)),
                  full, pl.BlockSpec((None, T, LANE), lambda h: (h, 0, 0)), full],
        out_specs=[full, full, full],
        out_shape=[jax.ShapeDtypeStruct((T, D), BF16)] * 3,
        scratch_shapes=[pltpu.VMEM((T, dh), F32), pltpu.VMEM((T, dh), F32),
                        pltpu.VMEM((N_MASKS, BLK, BLK), F32), pltpu.VMEM((N_MASKS, BLK, BLK), F32)],
        compiler_params=_params("arbitrary"))(qkv, qkv, qkv, o, lse, do)


def _mesh_pos():
    mx, my, mc = lax.axis_index("x"), lax.axis_index("y"), lax.axis_index("c")
    return mx, my, mc, 4 * mx + 2 * my + mc


def _peer(k, mx, my, mc):
    px, py, pc = mx ^ (k >> 2), my ^ ((k >> 1) & 1), mc ^ (k & 1)
    return (px, py, pc), 4 * px + 2 * py + pc


_SIBLING = 1
_OTHER_CHIPS = (4, 2, 6)
N_CHIP = N_DEV // 2
_PLANS = {"gather": (2, N_DEV - 1), "to_chips": (2, 1 + len(_OTHER_CHIPS)), "pass_on": (1, len(_OTHER_CHIPS)),
          "halves": (2, N_CHIP), "chip_sums": (2, len(_OTHER_CHIPS))}


def _copies(kind, items, send_sems, recv_sems):
    mx, my, mc, me = _mesh_pos()
    out = []

    def add(n, src, dst, peer):
        out.append(pltpu.make_async_remote_copy(
            src_ref=src, dst_ref=dst, send_sem=send_sems.at[n], recv_sem=recv_sems.at[n],
            device_id=peer, device_id_type=pl.DeviceIdType.MESH))

    per_item = _PLANS[kind][1]
    sibling = _peer(_SIBLING, mx, my, mc)[0]
    for i, refs in enumerate(items):
        n = i * per_item
        if kind == "gather":
            for k in range(1, N_DEV):
                add(n + k - 1, refs[0], refs[1].at[me], _peer(k, mx, my, mc)[0])
        elif kind == "to_chips":
            rows = refs[0].shape[0]
            dst = refs[1].at[me] if rows == refs[1].shape[1] else refs[1].at[me, pl.ds(0, rows)]
            for j, k in enumerate((_SIBLING,) + _OTHER_CHIPS):
                add(n + j, refs[0], dst, _peer(k, mx, my, mc)[0])
        elif kind == "pass_on":
            for j, k in enumerate(_OTHER_CHIPS):
                add(n + j, refs[0].at[me ^ k], refs[0].at[me ^ k], sibling)
        elif kind == "halves":
            for chip in range(N_CHIP):
                add(n + chip, refs[0].at[2 * chip + 1 - mc], refs[1].at[chip], sibling)
        else:
            for j, k in enumerate(_OTHER_CHIPS):
                peer, to = _peer(k, mx, my, mc)
                add(n + j, refs[0].at[to // 2], refs[1].at[me // 2], peer)
    return out


_HBM = pl.BlockSpec(memory_space=pltpu.HBM)
_SEM = pl.BlockSpec(memory_space=pltpu.SEMAPHORE)
_DATAFLOW = pltpu.SideEffectType.DATAFLOW_SIDE_EFFECTING


def exchange_call(name, waits, starts, deps=()):
    bufs, slot_of = [], {}

    def slots(items):
        out = []
        for item in items:
            for b in item:
                if id(b) not in slot_of:
                    slot_of[id(b)] = len(bufs)
                    bufs.append(b)
            out.append(tuple(slot_of[id(b)] for b in item))
        return out

    wait_plan = [(kind, slots(handle[0])) for kind, handle in waits]
    start_plan = [(kind, slots(items)) for kind, items in starts]
    wait_sems = [s for _, handle in waits for s in handle[1:]]
    n_buf, n_ws, n_start = len(bufs), len(wait_sems), len(starts)

    def body(*refs):
        buf_refs, sems_in = refs[:n_buf], refs[n_buf:n_buf + n_ws]
        outs = refs[n_buf + n_ws + len(deps):]
        pick = lambda plan: [tuple(buf_refs[s] for s in item) for item in plan]
        for wi, (kind, plan) in enumerate(wait_plan):
            copies = _copies(kind, pick(plan), sems_in[2 * wi], sems_in[2 * wi + 1])
            for cp in copies:
                cp.wait_send()
            for cp in copies:
                cp.wait_recv()
        for si, (kind, plan) in enumerate(start_plan):
            for cp in _copies(kind, pick(plan), outs[2 * si], outs[2 * si + 1]):
                cp.start()
        outs[-1][...] = jnp.zeros_like(outs[-1])

    hbm_bufs = [pltpu.with_memory_space_constraint(b, pltpu.HBM) for b in bufs]
    sem_shapes = []
    for kind, plan in start_plan:
        sem_shapes += [pltpu.SemaphoreType.DMA((len(plan) * _PLANS[kind][1],))] * 2
    outs = pl.pallas_call(
        body, name=name,
        out_shape=sem_shapes + [pltpu.HBM(b.shape, b.dtype) for b in bufs] + [jax.ShapeDtypeStruct((8, LANE), F32)],
        in_specs=[_HBM] * n_buf + [_SEM] * n_ws + [_ANY] * len(deps),
        out_specs=[_SEM] * (2 * n_start) + [_HBM] * n_buf + [pl.BlockSpec(memory_space=pltpu.VMEM)],
        input_output_aliases={i: 2 * n_start + i for i in range(n_buf)},
        compiler_params=pltpu.CompilerParams(has_side_effects=_DATAFLOW))(*hbm_bufs, *wait_sems, *deps)
    sems, thru, token = outs[:2 * n_start], outs[2 * n_start:-1], outs[-1]
    through = lambda plan: [tuple(thru[s] for s in item) for item in plan]
    waited = [through(plan) for _, plan in wait_plan]
    handles = [(through(plan), sems[2 * si], sems[2 * si + 1]) for si, (_, plan) in enumerate(start_plan)]
    return waited, handles, token


def gather_small(name, a, deps=()):
    def body(a_ref, *rest):
        o_ref, send_sems, recv_sems, local_sem = rest[len(deps):]
        me = _mesh_pos()[3]
        own = pltpu.make_async_copy(a_ref, o_ref.at[me], local_sem)
        own.start()
        copies = _copies("gather", [(a_ref, o_ref)], send_sems, recv_sems)
        for cp in copies:
            cp.start()
        for cp in copies:
            cp.wait_recv()
        for cp in copies:
            cp.wait_send()
        own.wait()

    return pl.pallas_call(
        body, name=name, in_specs=[_ANY] * (1 + len(deps)), out_specs=_ANY,
        out_shape=jax.ShapeDtypeStruct((N_DEV,) + a.shape, a.dtype),
        scratch_shapes=[pltpu.SemaphoreType.DMA((N_DEV - 1,)), pltpu.SemaphoreType.DMA((N_DEV - 1,)),
                        pltpu.SemaphoreType.DMA],
        compiler_params=pltpu.CompilerParams(has_side_effects=True))(a, *deps)


def _adamw_math(w, g, m, v):
    m2 = ADAM_B1 * m + (1.0 - ADAM_B1) * g
    v2 = ADAM_B2 * v + (1.0 - ADAM_B2) * (g * g)
    m_hat = m2 / (1.0 - ADAM_B1 ** ADAM_STEP)
    v_hat = v2 / (1.0 - ADAM_B2 ** ADAM_STEP)
    delta = -ADAM_LR * (m_hat / (jnp.sqrt(v_hat) + ADAM_EPS) + ADAM_WD * w)
    return delta, m2, v2


def chip_sum(name, a, half):
    _, r, c = a.shape
    tr = r
    chip = 2 * lax.axis_index("x") + lax.axis_index("y")
    where = jnp.stack([lax.axis_index("c"), chip ^ 1, chip ^ 2, chip ^ 3]).astype(jnp.int32)

    def body(where_ref, a_ref, h_ref, o_ref):
        del where_ref
        o_ref[...] = (a_ref[...].astype(F32) + h_ref[...].astype(F32)).astype(o_ref.dtype)

    blk = pl.BlockSpec((None, tr, c), lambda g, i, where: (where[1 + g], i, 0))
    grid_spec = pltpu.PrefetchScalarGridSpec(
        num_scalar_prefetch=1, grid=(N_CHIP - 1, r // tr),
        in_specs=[pl.BlockSpec((None, None, tr, c), lambda g, i, where: (where[1 + g], where[0], i, 0)), blk],
        out_specs=blk)
    return pl.pallas_call(
        body, name=name, grid_spec=grid_spec, out_shape=jax.ShapeDtypeStruct((N_CHIP, r, c), BF16),
        compiler_params=_params("parallel", "parallel"))(where, a.reshape(N_CHIP, 2, r, c), half)


def adamw(name, w, m, v, l, land, a, half, prev=None):
    L, r, c = w.shape
    cp = land.shape[2]
    tr = _pick(r, (256, 176, 128, 64, 32, 16, 8))

    def body(w_ref, m_ref, v_ref, land_ref, a_ref, half_ref, *rest):
        g_ref, d_ref, m2_ref, v2_ref = rest[-4:]
        chip = _mesh_pos()[3] // 2
        mine = a_ref[:, pl.ds(0, c)].astype(F32) + half_ref[:, pl.ds(0, c)].astype(F32)
        g = None
        for s in range(N_CHIP):
            part = jnp.where(chip == s, mine, land_ref[s, :, pl.ds(0, c)].astype(F32))
            g = part if g is None else g + part
        delta, m2, v2 = _adamw_math(w_ref[...], g, m_ref[...], v_ref[...])
        g_ref[...] = g
        d_ref[...] = delta
        m2_ref[...] = m2
        v2_ref[...] = v2

    blk = pl.BlockSpec((None, tr, c), lambda i: (l, i, 0))
    shape = jax.ShapeDtypeStruct((L, r, c), F32)
    extra = [] if prev is None else list(prev)
    return pl.pallas_call(
        body, name=name, grid=(r // tr,),
        in_specs=[blk, blk, blk, pl.BlockSpec((N_CHIP, tr, cp), lambda i: (0, i, 0)),
                  pl.BlockSpec((None, tr, cp), lambda i: (_mesh_pos()[3], i, 0)),
                  pl.BlockSpec((None, tr, cp), lambda i: (_mesh_pos()[3] // 2, i, 0))] + [_ANY] * len(extra),
        out_specs=[blk] * 4, out_shape=[shape] * 4,
        input_output_aliases={6 + k: k for k in range(len(extra))},
        compiler_params=_params("parallel"))(w, m, v, land, a, half, *extra)


def adamw_columns(name, w, m, v, land, a, half):
    r, _, D = w.shape
    tc = _pick(D, (256, 128))

    def body(w_ref, m_ref, v_ref, land_ref, a_ref, half_ref, g_ref, d_ref, m2_ref, v2_ref):
        chip = _mesh_pos()[3] // 2
        mine = a_ref[...].astype(F32) + half_ref[...].astype(F32)
        g = None
        for s in range(N_CHIP):
            part = jnp.where(chip == s, mine, land_ref[s].astype(F32))
            g = part if g is None else g + part
        flat = lambda ref: ref[...].reshape(r, tc)
        delta, m2, v2 = _adamw_math(flat(w_ref), g, flat(m_ref), flat(v_ref))
        for ref, val in ((g_ref, g), (d_ref, delta), (m2_ref, m2), (v2_ref, v2)):
            ref[...] = val.reshape(r, 1, tc)

    blk = pl.BlockSpec((r, 1, tc), lambda i: (0, 0, i))
    shape = jax.ShapeDtypeStruct((r, 1, D), F32)
    return pl.pallas_call(
        body, name=name, grid=(D // tc,),
        in_specs=[blk, blk, blk, pl.BlockSpec((N_CHIP, r, tc), lambda i: (0, 0, i)),
                  pl.BlockSpec((None, r, tc), lambda i: (_mesh_pos()[3], 0, i)),
                  pl.BlockSpec((None, r, tc), lambda i: (_mesh_pos()[3] // 2, 0, i))],
        out_specs=[blk] * 4, out_shape=[shape] * 4,
        compiler_params=_params("parallel"))(w, m, v, land, a, half)


def adamw_small(name, w, m, v, parts):
    n = w.shape[1]

    def body(w_ref, m_ref, v_ref, p_ref, g_ref, d_ref, m2_ref, v2_ref):
        g = p_ref[0:1, :]
        for s in range(1, N_DEV):
            g = g + p_ref[s:s + 1, :]
        delta, m2, v2 = _adamw_math(w_ref[...], g, m_ref[...], v_ref[...])
        g_ref[...] = g
        d_ref[...] = delta
        m2_ref[...] = m2
        v2_ref[...] = v2

    shape = jax.ShapeDtypeStruct((1, n), F32)
    return pl.pallas_call(body, name=name, out_shape=[shape] * 4,
                          compiler_params=pltpu.CompilerParams(vmem_limit_bytes=VMEM_LIMIT_BYTES))(w, m, v, parts)


def _rope_tables(positions):
    half = RET_DK // 2
    inv_freq = 1.0 / jnp.power(RET_THETA_BASE, jnp.linspace(0.0, 1.0, half, dtype=F32))
    ang = positions.astype(F32)[:, None] * inv_freq
    cos, sin = jnp.cos(ang), jnp.sin(ang)
    cosf = jnp.repeat(cos, 2, axis=-1)
    sins = jnp.stack([-sin, sin], axis=-1).reshape(cosf.shape)
    return cosf, sins


def _pad_to(a, axis, size):
    pad = [(0, 0)] * a.ndim
    pad[axis] = (0, size - a.shape[axis])
    return jnp.pad(a, pad)


def _round_up(n, m):
    return -(-n // m) * m


def kernel(x, p, positions, attn_norm_w, ffn_norm_w, ple_norm_w, final_norm_w, ab_w_in, ab_gla_gate_up, ab_gla_gate_b, ab_ret_norm_w, ab_gla_norm_w, ab_w_out, c_w_qkv, c_w_out, ffn_w_gate, ffn_w_up, ffn_w_down, ple_w_proj, ple_w_gate, loss_target, m_attn_norm_w, m_ffn_norm_w, m_ple_norm_w, m_final_norm_w, m_ab_w_in, m_ab_gla_gate_up, m_ab_gla_gate_b, m_ab_ret_norm_w, m_ab_gla_norm_w, m_ab_w_out, m_c_w_qkv, m_c_w_out, m_ffn_w_gate, m_ffn_w_up, m_ffn_w_down, m_ple_w_proj, m_ple_w_gate, v_attn_norm_w, v_ffn_norm_w, v_ple_norm_w, v_final_norm_w, v_ab_w_in, v_ab_gla_gate_up, v_ab_gla_gate_b, v_ab_ret_norm_w, v_ab_gla_norm_w, v_ab_w_out, v_c_w_qkv, v_c_w_out, v_ffn_w_gate, v_ffn_w_up, v_ffn_w_down, v_ple_w_proj, v_ple_w_gate):
    T, D = x.shape[1], x.shape[2]
    depth = attn_norm_w.shape[0]
    assert ab_w_in.shape[0] == 1 and c_w_qkv.shape[0] == 1 and depth == 2, "one even and one odd layer"
    me = 4 * lax.axis_index("x") + 2 * lax.axis_index("y") + lax.axis_index("c")
    in_shard = ab_w_in.shape[2]
    in_width = in_shard * N_DEV
    assert in_width == OFF_LR + GLA_GATE_RANK
    fs = ffn_w_gate.shape[2]
    fp = _round_up(fs, LANE)
    gu_cols = ab_gla_gate_up.shape[2]

    bf = lambda a: a.astype(BF16)
    tr_ = lambda a: jnp.swapaxes(a, -1, -2)
    wg_t, wu_t = tr_(ffn_w_gate), tr_(ffn_w_up)
    srcs = {"w_in": bf(tr_(ab_w_in[0]))}
    group_keys = [["w_in"], ["gu", "w_oab"], ["wg0", "wu0"], ["wd0", "wpg0", "wpp0"], ["w_qkv", "w_oc"],
                  ["wg1", "wu1"], ["wd1", "wpg1", "wpp1"]]
    G_IN, G_OUT, G_QKV = 0, 1, 4
    g_ffn = lambda layer: (2, 3) if layer == 0 else (5, 6)

    def landing(key):
        a = srcs[key]
        rows = fp if key[:2] in ("wg", "wu", "wd") else a.shape[0]
        buf = lax.empty((N_DEV, rows) + a.shape[1:], a.dtype)
        if rows > a.shape[0]:
            zeros = jnp.zeros((N_DEV, rows - a.shape[0]) + a.shape[1:], a.dtype)
            buf = lax.dynamic_update_slice(buf, zeros, (0, a.shape[0]) + (0,) * (a.ndim - 1))
        return lax.dynamic_update_slice(buf, a[None], (me,) + (0,) * a.ndim)

    _, chip_handles, gather_token = exchange_call(
        "gather_start_in", [], [("to_chips", [(srcs[k], landing(k)) for k in group_keys[G_IN]])])
    (gather_token, w_out_, gu_, w_qkv_, w_oc_, wg_, wu_, wd_, wpg_, wpp_) = lax.optimization_barrier(
        (gather_token, ab_w_out, ab_gla_gate_up, c_w_qkv, c_w_out, wg_t, wu_t, ffn_w_down, ple_w_gate, ple_w_proj))
    srcs.update(w_oab=bf(w_out_[0]), gu=gu_[0], w_qkv=bf(w_qkv_[0]), w_oc=bf(w_oc_[0]))
    for l in range(depth):
        srcs[f"wg{l}"] = bf(wg_[l])
        srcs[f"wu{l}"] = bf(wu_[l])
        srcs[f"wd{l}"] = bf(wd_[l])
        srcs[f"wpg{l}"] = bf(wpg_[l])
        srcs[f"wpp{l}"] = bf(wpp_[l])
    _, more, gather_token = exchange_call(
        "gather_start", [], [("to_chips", [(srcs[k], landing(k)) for k in keys]) for keys in group_keys[1:]],
        deps=(gather_token,))
    chip_handles = chip_handles + more
    weights = {}

    def gather_wait(gi, dep):
        lands = [(land,) for _, land in chip_handles[gi][0]]
        _, (passing,), _ = exchange_call(
            f"gather{gi}_pass", [("to_chips", chip_handles[gi])], [("pass_on", lands)], deps=(dep,))
        (complete,), _, _ = exchange_call(f"gather{gi}_done", [("pass_on", passing)], [])
        weights.update(zip(group_keys[gi], [land for (land,) in complete]))

    gb = ab_gla_gate_b
    hn_w = jnp.concatenate([ab_ret_norm_w, ab_gla_norm_w], axis=1)
    cosf, sins = _rope_tables(positions[0])
    p_bf = bf(p[:, 0])

    xs = x[0]
    saved = []
    for i in range(depth):
        nm = f"l{i}_"
        w_attn, w_ffn, w_ple = attn_norm_w[i:i + 1], ffn_norm_w[i:i + 1], ple_norm_w[i:i + 1]
        (xn,) = rowwise(nm + "norm_attn", lambda a, w: (_rms(a, w),), T, [("row", xs), ("full", w_attn)],
                        [("row", D, BF16)], deps=(gather_token,) if i == 0 else ())
        if i % 2 == 0:
            gather_wait(G_IN, xn)
            w_in_t = weights["w_in"].reshape(1, 1, in_width, D)
            w_lr_t = _pad_to(w_in_t[0, 0, OFF_LR:], 0, LANE).reshape(1, 1, LANE, D)
            z = mmt_fwd(nm + "mm_in", xn, w_in_t, 0, F32, n=OFF_LR)
            glr = mmt_fwd(nm + "mm_lr", xn, w_lr_t, 0, F32)
            oraw = retention_fwd(nm + "ret_fwd", z, cosf, sins, RET_V + GLA_V)
            gather_wait(G_OUT, oraw)
            w_oab = weights["w_oab"].reshape(1, 1, D, D)
            gu_full = _pad_to(weights["gu"].transpose(1, 0, 2).reshape(GLA_GATE_RANK, GLA_QK), 0, LANE)
            oraw = gla_fwd(nm + "gla_fwd", z, glr, gu_full, gb, oraw)
            o = headnorm_fwd(nm + "headnorm_fwd", oraw, z, hn_w)
            h1, hn = mm_add_norm(nm + "mm_out", o, w_oab, xs, w_ffn)
            mixer_saved = (z, glr, oraw, o)
        else:
            gather_wait(G_QKV, xn)
            w_qkv = weights["w_qkv"].reshape((1,) + weights["w_qkv"].shape)
            w_oc = weights["w_oc"].reshape(1, 1, D, D)
            qkv = mm_nn(nm + "mm_qkv", xn, w_qkv, 0, BF16)
            o, lse = attn_fwd(nm + "attn_fwd", qkv)
            h1, hn = mm_add_norm(nm + "mm_out", o, w_oc, xs, w_ffn)
            mixer_saved = (qkv, o, lse)
        gather_wait(g_ffn(i)[0], hn)
        wg = weights[f"wg{i}"].reshape(1, N_DEV, fp, D)
        wu = weights[f"wu{i}"].reshape(1, N_DEV, fp, D)
        g, u, act = ffn_gate_up(nm + "ffn_gate_up", hn, wg, wu)
        gather_wait(g_ffn(i)[1], act)
        wd = weights[f"wd{i}"].reshape(1, 1, N_DEV * fp, D)
        wpg = weights[f"wpg{i}"].reshape(1, 1, D, D)
        wpp = weights[f"wpp{i}"].reshape((1,) + weights[f"wpp{i}"].shape)
        h2, pn = mm_add_norm(nm + "mm_down", act, wd, h1, w_ple)
        x_next, s, e = ple_fwd(nm + "ple", pn, wpg, p_bf[i], wpp, h2)
        mixer_w = (w_in_t, w_lr_t, w_oab, gu_full) if i % 2 == 0 else (w_qkv, w_oc)
        saved.append((xs, xn, mixer_saved, mixer_w, (wg, wu, wd, wpg), h1, hn, g, u, act, h2, pn, s, e))
        xs = x_next

    def loss_fn(a, w, t):
        diff = _rms(a, w) - t
        dx, dw = _rms_bwd(a, w, diff * (1.0 / D))
        part = 0.5 * jnp.sum(jnp.mean(diff * diff, axis=-1, keepdims=True), axis=0, keepdims=True)
        return dx, dw, jnp.broadcast_to(part, (1, LANE))

    dx, d_final_w, loss_part = rowwise("loss_head", loss_fn, T,
                                       [("row", xs), ("full", final_norm_w[None, :]), ("row", loss_target[0])],
                                       [("row", D, F32), ("acc", D), ("acc", LANE)])
    loss = lax.psum(loss_part[0, 0], ("x", "y", "c"))

    grads = {}
    on_chip = []
    scatters = []

    def scatter_start(name, keys, deps=()):
        waits = [("halves", on_chip[0][1])] if on_chip else []
        starts = [("halves", [(grads[k], lax.empty((N_CHIP,) + grads[k].shape[1:], BF16)) for k in keys])] if keys else []
        waited, handles, token = exchange_call(name, waits, starts, deps=deps)
        if on_chip:
            done_keys, _ = on_chip.pop()
            sums = [chip_sum(f"{name}_sum{j}", a, half) for j, (a, half) in enumerate(waited[0])]
            _, (handle,), token = exchange_call(
                name + "_chips", [], [("chip_sums", [(cs, lax.empty(cs.shape, BF16)) for cs in sums])])
            scatters.append((done_keys, handle, waited[0]))
        if keys:
            on_chip.append((keys, handles[0]))
        return token

    d_attn_w, d_ffn_w, d_ple_w = [None] * depth, [None] * depth, [None] * depth
    for i in reversed(range(depth)):
        nm = f"l{i}_b_"
        xs_i, xn, mixer_saved, mixer_w, (wg, wu, wd, wpg), h1, hn, g, u, act, h2, pn, s, e = saved[i]
        w_attn, w_ffn, w_ple = attn_norm_w[i:i + 1], ffn_norm_w[i:i + 1], ple_norm_w[i:i + 1]

        def ple_bwd(d, sv, ev):
            gate = _sigmoid(sv)
            return d * gate, d * ev * gate * (1.0 - gate)

        de, ds = rowwise(nm + "ple_out", ple_bwd, T, [("row", dx), ("row", s), ("row", e)],
                         [("row", D, BF16), ("row", D, BF16)], deps=(loss.reshape(1, 1),) if i == depth - 1 else ())
        grads[("ple_w_proj", i)] = mm_tn(nm + "mm_ple_proj_w", p_bf[i], de, N_DEV, BF16)
        grads[("ple_w_gate", i)] = mm_tn(nm + "mm_ple_gate_w", pn, ds, 1, BF16).reshape(N_DEV, D // N_DEV, D)
        dpn = mm_nt(nm + "mm_ple_gate_x", ds, wpg, 0, F32)

        def norm_bwd_add(a, w, dn, dres):
            dxx, dw = _rms_bwd(a, w, dn)
            tot = dres + dxx
            return tot, tot, dw

        dh2, dh2_bf, d_ple_w[i] = rowwise(nm + "norm_ple", norm_bwd_add, T,
                                          [("row", h2), ("full", w_ple), ("row", dpn), ("row", dx)],
                                          [("row", D, F32), ("row", D, BF16), ("acc", D)])
        grads[("ffn_w_down", i)] = mm_tn(nm + "mm_down_w", act, dh2_bf, 1, BF16).reshape(N_DEV, fp, D)
        token = scatter_start(nm + "scatter_ple_down", [("ple_w_proj", i), ("ple_w_gate", i), ("ffn_w_down", i)])
        dg, du = ffn_down_bwd(nm + "ffn_down_x", dh2_bf, wd, g, u, deps=(token,))
        grads[("ffn_w_gate", i)] = mmt_dw(nm + "mm_gate_w", dg, hn, N_DEV, BF16)
        grads[("ffn_w_up", i)] = mmt_dw(nm + "mm_up_w", du, hn, N_DEV, BF16)
        token = scatter_start(nm + "scatter_gate_up", [("ffn_w_gate", i), ("ffn_w_up", i)])
        dhn_g = mmt_dx_wide(nm + "mm_gate_x", dg, wg, F32, deps=(token,))
        dhn_u = mmt_dx_wide(nm + "mm_up_x", du, wu, F32)

        def norm_bwd_add2(a, w, dn1, dn2, dres):
            dxx, dw = _rms_bwd(a, w, dn1 + dn2)
            tot = dres + dxx
            return tot, tot, dw

        dh1, dh1_bf, d_ffn_w[i] = rowwise(nm + "norm_ffn", norm_bwd_add2, T,
                                          [("row", h1), ("full", w_ffn), ("row", dhn_g), ("row", dhn_u), ("row", dh2)],
                                          [("row", D, F32), ("row", D, BF16), ("acc", D)])
        if i % 2 == 0:
            z, glr, oraw, o = mixer_saved
            w_in_t, w_lr_t, w_oab, gu_full = mixer_w
            grads[("ab_w_out", 0)] = mm_tn(nm + "mm_out_w", o, dh1_bf, 1, BF16).reshape(N_DEV, D // N_DEV, D)
            token = scatter_start(nm + "scatter_out", [("ab_w_out", 0)])
            do = mm_nt(nm + "mm_out_x", dh1_bf, w_oab, 0, F32, deps=(token,))
            d_oraw, d_gates, d_hn_w = headnorm_bwd(nm + "headnorm", oraw, z, hn_w, do)
            d_rq, d_rk, d_rv = retention_bwd(nm + "ret", z, cosf, sins, d_oraw)
            d_gq, d_gk, d_gv, d_glr4, d_gu, d_gb = gla_bwd(nm + "gla", z, glr, gu_full, gb, d_oraw)
            dz = jnp.concatenate([d_rq, d_rk, d_rv, d_gates[:, :RET_V], d_gq, d_gk, d_gv, d_gates[:, RET_V:]], axis=1)
            (d_glr,) = rowwise(nm + "sum_lr", lambda *a: (a[0] + a[1] + a[2] + a[3],), T,
                               [("row", d_glr4[hh]) for hh in range(GLA_HEADS)], [("row", LANE, BF16)])
            dwt_main = mmt_dw(nm + "mm_in_w", dz, xn, 1, BF16)[0]
            dwt_lr = mmt_dw(nm + "mm_lr_w", d_glr, xn, 1, BF16)[0]
            dwt_in = jnp.concatenate([dwt_main, dwt_lr[:GLA_GATE_RANK]], axis=0)
            grads[("ab_w_in", 0)] = dwt_in.reshape(N_DEV, in_shard, D)
            token = scatter_start(nm + "scatter_in", [("ab_w_in", 0)])
            dxn_a = mmt_dx_wide(nm + "mm_in_x", dz, w_in_t, F32, n=OFF_LR, deps=(token,))
            token = scatter_start(nm + "scatter_in_on", [], deps=(dxn_a,))
            dxn_b = mmt_dx(nm + "mm_lr_x", d_glr, w_lr_t, 0, F32, deps=(token,))
        else:
            qkv, o, lse = mixer_saved
            w_qkv, w_oc = mixer_w
            grads[("c_w_out", 0)] = mm_tn(nm + "mm_out_w", o, dh1_bf, 1, BF16).reshape(N_DEV, D // N_DEV, D)
            do = mm_nt(nm + "mm_out_x", dh1_bf, w_oc, 0, BF16)
            dq, dk, dv = attn_bwd(nm + "attn", qkv, o, lse, do)
            dqkv = jnp.concatenate([dq, dk, dv], axis=1)
            grads[("c_w_qkv", 0)] = mm_tn(nm + "mm_qkv_w", xn, dqkv, N_DEV, BF16)
            token = scatter_start(nm + "scatter_attn", [("c_w_out", 0), ("c_w_qkv", 0)])
            dxn_a = mm_nt_wide(nm + "mm_qkv_x", dqkv, w_qkv, F32, deps=(token,))
            dxn_b = None
        if dxn_b is None:
            dx, _, d_attn_w[i] = rowwise(nm + "norm_attn", norm_bwd_add, T,
                                         [("row", xs_i), ("full", w_attn), ("row", dxn_a), ("row", dh1)],
                                         [("row", D, F32), ("row", D, BF16), ("acc", D)])
        else:
            dx, _, d_attn_w[i] = rowwise(nm + "norm_attn", norm_bwd_add2, T,
                                         [("row", xs_i), ("full", w_attn), ("row", dxn_a), ("row", dxn_b), ("row", dh1)],
                                         [("row", D, F32), ("row", D, BF16), ("acc", D)])

    small_names = ["attn_norm_w", "ffn_norm_w", "ple_norm_w", "final_norm_w", "ab_gla_gate_b", "ab_ret_norm_w",
                   "ab_gla_norm_w"]
    small_grads = [jnp.concatenate(d_attn_w, 0), jnp.concatenate(d_ffn_w, 0), jnp.concatenate(d_ple_w, 0), d_final_w[0],
                   d_gb, d_hn_w[:, :RET_V], d_hn_w[:, RET_V:]]
    small_w = [attn_norm_w, ffn_norm_w, ple_norm_w, final_norm_w, ab_gla_gate_b, ab_ret_norm_w, ab_gla_norm_w]
    small_m = [m_attn_norm_w, m_ffn_norm_w, m_ple_norm_w, m_final_norm_w, m_ab_gla_gate_b, m_ab_ret_norm_w, m_ab_gla_norm_w]
    small_v = [v_attn_norm_w, v_ffn_norm_w, v_ple_norm_w, v_final_norm_w, v_ab_gla_gate_b, v_ab_ret_norm_w, v_ab_gla_norm_w]
    sizes = [int(np.prod(a.shape)) for a in small_w]
    n_gu = GLA_GATE_RANK * GLA_QK
    n_small = _round_up(sum(sizes) + n_gu, LANE)
    pack = lambda parts: _pad_to(jnp.concatenate([a.reshape(-1) for a in parts]), 0, n_small)[None, :]
    small_part = pack(small_grads + [d_gu[:GLA_GATE_RANK]])

    cols_first = lambda a: jnp.transpose(a, (2, 0, 1))
    big_w = dict(ab_w_in=tuple(cols_first(a) for a in (ab_w_in, m_ab_w_in, v_ab_w_in)),
                 ab_w_out=(ab_w_out, m_ab_w_out, v_ab_w_out),
                 c_w_qkv=(c_w_qkv, m_c_w_qkv, v_c_w_qkv), c_w_out=(c_w_out, m_c_w_out, v_c_w_out),
                 ffn_w_gate=(wg_t, tr_(m_ffn_w_gate), tr_(v_ffn_w_gate)),
                 ffn_w_up=(wu_t, tr_(m_ffn_w_up), tr_(v_ffn_w_up)),
                 ffn_w_down=(ffn_w_down, m_ffn_w_down, v_ffn_w_down), ple_w_proj=(ple_w_proj, m_ple_w_proj, v_ple_w_proj),
                 ple_w_gate=(ple_w_gate, m_ple_w_gate, v_ple_w_gate))
    if on_chip:
        scatter_start("scatter_last", [], deps=(dx,))
    results, last = {}, dx
    for gi, (keys, handle, partials) in enumerate(scatters):
        (arrived,), _, _ = exchange_call(f"scatter_wait{gi}", [("chip_sums", handle)], [], deps=(last,))
        for (n, l), (_, land), (a, half) in zip(keys, arrived, partials):
            if n == "ab_w_in":
                results[n] = adamw_columns(f"adamw_{n}", *big_w[n], land, a, half)
            else:
                results[n] = adamw(f"adamw_{n}{l}", *big_w[n], l, land, a, half, prev=results.get(n))
            last = results[n][0]
    for n in ("ffn_w_gate", "ffn_w_up"):
        results[n] = [tr_(a) for a in results[n]]
    results["ab_w_in"] = [jnp.transpose(a, (1, 2, 0)) for a in results["ab_w_in"]]
    small_parts = gather_small("gather_small", small_part, deps=(last,)).reshape(N_DEV, n_small)

    gu_off = sum(sizes)
    own_cols = lambda a: lax.dynamic_slice_in_dim(a.reshape(GLA_GATE_RANK, GLA_QK), me * gu_cols, gu_cols, axis=1)
    small_res = adamw_small("adamw_small", pack(small_w + [jnp.zeros((n_gu,), F32)]),
                            pack(small_m + [jnp.zeros((n_gu,), F32)]), pack(small_v + [jnp.ones((n_gu,), F32)]),
                            small_parts)
    g_gu_full = small_res[0][0, gu_off:gu_off + n_gu]
    g_gu = own_cols(g_gu_full)[None]
    gu_res = adamw_small("adamw_gate_up", *[_pad_to(a.reshape(1, -1), 1, _round_up(a.size, LANE)) for a in
                                            (ab_gla_gate_up, m_ab_gla_gate_up, v_ab_gla_gate_up)],
                         jnp.concatenate([_pad_to(g_gu.reshape(1, -1), 1, _round_up(g_gu.size, LANE)),
                                          jnp.zeros((N_DEV - 1, _round_up(g_gu.size, LANE)), F32)], axis=0))
    for k in range(4):
        off = 0
        for n, a, sz in zip(small_names, small_w, sizes):
            results.setdefault(n, [None] * 4)[k] = small_res[k][0, off:off + sz].reshape(a.shape)
            off += sz
        results.setdefault("ab_gla_gate_up", [None] * 4)[k] = gu_res[k][0, :g_gu.size].reshape(ab_gla_gate_up.shape)

    order = ["attn_norm_w", "ffn_norm_w", "ple_norm_w", "final_norm_w", "ab_w_in", "ab_gla_gate_up", "ab_gla_gate_b",
             "ab_ret_norm_w", "ab_gla_norm_w", "ab_w_out", "c_w_qkv", "c_w_out", "ffn_w_gate", "ffn_w_up", "ffn_w_down",
             "ple_w_proj", "ple_w_gate"]
    return (loss, dx[None], *[results[n][0] for n in order], *[results[n][1] for n in order],
            *[results[n][2] for n in order], *[results[n][3] for n in order])
```

```python
import math

import numpy as np
import jax
import jax.numpy as jnp
from jax import lax
from jax.experimental import pallas as pl
from jax.experimental.pallas import tpu as pltpu

F32 = jnp.float32
BF16 = jnp.bfloat16
HIGHEST = lax.Precision.HIGHEST

N_DEV = 8
VMEM_LIMIT_BYTES = 48 * 1024 * 1024
LANE = 128
NORM_EPS = 1e-6

RET_HEADS, RET_DK, RET_DV = 4, 256, 256
RET_THETA_BASE = 10000.0
GLA_HEADS, GLA_DK, GLA_DV = 4, 128, 256
GLA_GATE_RANK = 16
GLA_GATE_NORM = 16.0
CHUNK = 64
ATT_HEADS = 16
DILATED_BRANCHES = ((128, 1), (512, 4), (2048, 16))
BLK = 256

ADAM_LR, ADAM_B1, ADAM_B2, ADAM_EPS, ADAM_WD, ADAM_STEP = 0.001, 0.9, 0.999, 1e-08, 0.01, 10

RET_QK = RET_HEADS * RET_DK
RET_V = RET_HEADS * RET_DV
GLA_QK = GLA_HEADS * GLA_DK
GLA_V = GLA_HEADS * GLA_DV
OFF_RQ, OFF_RK, OFF_RV, OFF_RG = 0, RET_QK, 2 * RET_QK, 2 * RET_QK + RET_V
OFF_GQ = OFF_RG + RET_V
OFF_GK = OFF_GQ + GLA_QK
OFF_GV = OFF_GK + GLA_QK
OFF_GG = OFF_GV + GLA_V
OFF_LR = OFF_GG + GLA_V


def _params(*sem):
    return pltpu.CompilerParams(dimension_semantics=sem or None, vmem_limit_bytes=VMEM_LIMIT_BYTES)


def _pick(n, cands):
    for c in cands:
        if n % c == 0:
            return c
    raise ValueError(f"no tile for {n} in {cands}")


_NN = (((1,), (0,)), ((), ()))
_NT = (((1,), (1,)), ((), ()))
_TN = (((0,), (0,)), ((), ()))
_ANY = pl.BlockSpec(memory_space=pl.ANY)
MAX_CONTRACT = 2048
_TILES = (1024, 768, 512, 256, 128)


def _mm_call(name, dims, grid, in_specs, out_spec, out_shape, args, deps=()):
    steps = grid[2]
    assert steps == 1 or out_shape.dtype == F32

    def body(a_ref, b_ref, *rest):
        o_ref = rest[len(deps)]
        part = lax.dot_general(a_ref[...].astype(BF16), b_ref[...].astype(BF16), dims, preferred_element_type=F32)
        if steps == 1:
            o_ref[...] = part.astype(o_ref.dtype)
        else:
            _accumulate(o_ref, part, pl.program_id(2) == 0)

    return pl.pallas_call(
        body, name=name, grid=grid, in_specs=list(in_specs) + [_ANY] * len(deps), out_specs=out_spec,
        out_shape=out_shape, compiler_params=_params("parallel", "parallel", "arbitrary"))(*args, *deps)


def mm_nn(name, a, w, l, out_dtype, deps=()):
    _, J, K, n = w.shape
    M = a.shape[0]
    tm, tn, tk = _pick(M, _TILES), _pick(n, _TILES), _pick(K, (MAX_CONTRACT,) + _TILES)
    nt = n // tn
    return _mm_call(
        name, _NN, (M // tm, J * nt, K // tk),
        [pl.BlockSpec((tm, tk), lambda i, j, k: (i, k)),
         pl.BlockSpec((None, None, tk, tn), lambda i, j, k: (l, j // nt, k, j % nt))],
        pl.BlockSpec((tm, tn), lambda i, j, k: (i, j)),
        jax.ShapeDtypeStruct((M, J * n), out_dtype), (a, w), deps)


def mm_nt(name, a, w, l, out_dtype, deps=()):
    _, J, K, n = w.shape
    M = a.shape[0]
    tm, tq, tc = _pick(M, _TILES), _pick(K, _TILES), _pick(n, (MAX_CONTRACT,) + _TILES)
    nc = n // tc
    return _mm_call(
        name, _NT, (M // tm, K // tq, J * nc),
        [pl.BlockSpec((tm, tc), lambda i, q, c: (i, c)),
         pl.BlockSpec((None, None, tq, tc), lambda i, q, c: (l, c // nc, q, c % nc))],
        pl.BlockSpec((tm, tq), lambda i, q, c: (i, q)),
        jax.ShapeDtypeStruct((M, K), out_dtype), (a, w), deps)


def mm_tn(name, x, dy, J, out_dtype, deps=()):
    M, K = x.shape
    n = dy.shape[1] // J
    tp, tn = _pick(K, _TILES), _pick(n, _TILES)
    nt = n // tn
    assert M <= MAX_CONTRACT
    return _mm_call(
        name, _TN, (K // tp, J * nt, 1),
        [pl.BlockSpec((M, tp), lambda i, j, r: (0, i)),
         pl.BlockSpec((M, tn), lambda i, j, r: (0, j))],
        pl.BlockSpec((None, tp, tn), lambda i, j, r: (j // nt, i, j % nt)),
        jax.ShapeDtypeStruct((J, K, n), out_dtype), (x, dy), deps)


def mmt_fwd(name, a, wt, l, out_dtype, n=None, deps=()):
    _, J, rows, K = wt.shape
    n = rows if n is None else n
    M = a.shape[0]
    tm, tn = _pick(M, _TILES), _pick(n, _TILES)
    nt = n // tn
    assert K <= MAX_CONTRACT
    return _mm_call(
        name, _NT, (M // tm, J * nt, 1),
        [pl.BlockSpec((tm, K), lambda i, j, k: (i, 0)),
         pl.BlockSpec((None, None, tn, K), lambda i, j, k: (l, j // nt, j % nt, 0))],
        pl.BlockSpec((tm, tn), lambda i, j, k: (i, j)),
        jax.ShapeDtypeStruct((M, J * n), out_dtype), (a, wt), deps)


def mmt_dx(name, dy, wt, l, out_dtype, n=None, deps=()):
    _, J, rows, K = wt.shape
    n = rows if n is None else n
    M = dy.shape[0]
    tm, tq, tc = _pick(M, _TILES), _pick(K, _TILES), _pick(n, _TILES)
    nc = n // tc
    return _mm_call(
        name, _NN, (M // tm, K // tq, J * nc),
        [pl.BlockSpec((tm, tc), lambda i, q, c: (i, c)),
         pl.BlockSpec((None, None, tc, tq), lambda i, q, c: (l, c // nc, c % nc, q))],
        pl.BlockSpec((tm, tq), lambda i, q, c: (i, q)),
        jax.ShapeDtypeStruct((M, K), out_dtype), (dy, wt), deps)


WIDE_TILE = 512


def _wide_call(name, body, M, K, a, w, a_spec, w_spec, out_dtype, deps):
    def kernel_body(a_ref, w_ref, *rest):
        o_ref = rest[len(deps)]
        o_ref[...] = body(a_ref, w_ref).astype(o_ref.dtype)

    return pl.pallas_call(
        kernel_body, name=name, grid=(M // WIDE_TILE, K // WIDE_TILE),
        in_specs=[a_spec, w_spec] + [_ANY] * len(deps),
        out_specs=pl.BlockSpec((WIDE_TILE, WIDE_TILE), lambda i, q: (i, q)),
        out_shape=jax.ShapeDtypeStruct((M, K), out_dtype),
        compiler_params=_params("parallel", "parallel"))(a, w, *deps)


def mmt_dx_wide(name, dy, wt, out_dtype, n=None, deps=()):
    _, J, rows, K = wt.shape
    n = rows if n is None else n
    M = dy.shape[0]

    def body(dy_ref, w_ref):
        return jnp.dot(dy_ref[...].astype(BF16), w_ref[...].reshape(J * n, WIDE_TILE), preferred_element_type=F32)

    return _wide_call(name, body, M, K, dy, wt,
                      pl.BlockSpec((WIDE_TILE, J * n), lambda i, q: (i, 0)),
                      pl.BlockSpec((None, J, n, WIDE_TILE), lambda i, q: (0, 0, 0, q)), out_dtype, deps)


def mm_nt_wide(name, a, w, out_dtype, deps=()):
    _, J, K, n = w.shape
    M = a.shape[0]

    def body(a_ref, w_ref):
        acc = None
        for j in range(J):
            part = lax.dot_general(a_ref[:, j * n:(j + 1) * n].astype(BF16), w_ref[j], _NT, preferred_element_type=F32)
            acc = part if acc is None else acc + part
        return acc

    return _wide_call(name, body, M, K, a, w,
                      pl.BlockSpec((WIDE_TILE, J * n), lambda i, q: (i, 0)),
                      pl.BlockSpec((None, J, WIDE_TILE, n), lambda i, q: (0, 0, q, 0)), out_dtype, deps)


def mmt_dw(name, dy, x, J, out_dtype, deps=()):
    M, K = x.shape
    n = dy.shape[1] // J
    tn, tp = _pick(n, _TILES), _pick(K, _TILES)
    nt = n // tn
    assert M <= MAX_CONTRACT
    return _mm_call(
        name, _TN, (J * nt, K // tp, 1),
        [pl.BlockSpec((M, tn), lambda j, i, r: (0, j)),
         pl.BlockSpec((M, tp), lambda j, i, r: (0, i))],
        pl.BlockSpec((None, tn, tp), lambda j, i, r: (j // nt, j % nt, i)),
        jax.ShapeDtypeStruct((J, n, K), out_dtype), (dy, x), deps)


def ffn_gate_up(name, a, wg, wu):
    _, J, n, K = wg.shape
    M = a.shape[0]
    tm, tn = _pick(M, _TILES), _pick(n, _TILES)
    nt = n // tn
    assert K <= MAX_CONTRACT

    def body(a_ref, wg_ref, wu_ref, g_ref, u_ref, act_ref):
        x = a_ref[...]
        g = lax.dot_general(x, wg_ref[...], _NT, preferred_element_type=F32)
        u = lax.dot_general(x, wu_ref[...], _NT, preferred_element_type=F32)
        g_ref[...] = g.astype(g_ref.dtype)
        u_ref[...] = u.astype(u_ref.dtype)
        act_ref[...] = (_silu_and_grad(g)[0] * u).astype(act_ref.dtype)

    w_spec = pl.BlockSpec((None, None, tn, K), lambda i, j: (0, j // nt, j % nt, 0))
    out = pl.BlockSpec((tm, tn), lambda i, j: (i, j))
    return pl.pallas_call(
        body, name=name, grid=(M // tm, J * nt),
        in_specs=[pl.BlockSpec((tm, K), lambda i, j: (i, 0)), w_spec, w_spec],
        out_specs=[out] * 3, out_shape=[jax.ShapeDtypeStruct((M, J * n), BF16)] * 3,
        compiler_params=_params("parallel", "parallel"))(a, wg, wu)


def mm_add_norm(name, a, w, res, norm_w):
    _, _, K, N = w.shape
    M = a.shape[0]
    tm, tk = _pick(M, (WIDE_TILE, 256)), _pick(K, (1024, 512, 256))
    steps = K // tk

    def body(a_ref, w_ref, res_ref, nw_ref, h_ref, hn_ref):
        k = pl.program_id(1)
        part = jnp.dot(a_ref[...], w_ref[...], preferred_element_type=F32)
        _accumulate(h_ref, part, k == 0)

        @pl.when(k == steps - 1)
        def _():
            h = h_ref[...] + res_ref[...]
            h_ref[...] = h
            hn_ref[...] = _rms(h, nw_ref[...]).astype(hn_ref.dtype)

    rows = pl.BlockSpec((tm, N), lambda i, k: (i, 0))
    return pl.pallas_call(
        body, name=name, grid=(M // tm, steps),
        in_specs=[pl.BlockSpec((tm, tk), lambda i, k: (i, k)),
                  pl.BlockSpec((None, None, tk, N), lambda i, k: (0, 0, k, 0)), rows,
                  pl.BlockSpec((1, N), lambda i, k: (0, 0))],
        out_specs=[rows, rows],
        out_shape=[jax.ShapeDtypeStruct((M, N), F32), jax.ShapeDtypeStruct((M, N), BF16)],
        compiler_params=_params("parallel", "arbitrary"))(a, w, res, norm_w)


def ple_fwd(name, pn, wpg, p_in, wpp, h):
    _, J, P, n = wpp.shape
    M, D = h.shape
    tm, tn = _pick(M, (WIDE_TILE, 256)), _pick(D, _TILES)
    per_tile = tn // n

    def body(pn_ref, wg_ref, p_ref, wp_ref, h_ref, x_ref, s_ref, e_ref):
        s = jnp.dot(pn_ref[...], wg_ref[...], preferred_element_type=F32)
        p_blk = p_ref[...]
        e = jnp.concatenate([jnp.dot(p_blk, wp_ref[j], preferred_element_type=F32) for j in range(per_tile)], axis=1)
        s_ref[...] = s
        e_ref[...] = e
        x_ref[...] = h_ref[...] + _sigmoid(s) * e

    tile = pl.BlockSpec((tm, tn), lambda i, j: (i, j))
    return pl.pallas_call(
        body, name=name, grid=(M // tm, D // tn),
        in_specs=[pl.BlockSpec((tm, D), lambda i, j: (i, 0)),
                  pl.BlockSpec((None, None, D, tn), lambda i, j: (0, 0, 0, j)),
                  pl.BlockSpec((tm, P), lambda i, j: (i, 0)),
                  pl.BlockSpec((None, per_tile, P, n), lambda i, j: (0, j, 0, 0)), tile],
        out_specs=[tile] * 3, out_shape=[jax.ShapeDtypeStruct((M, D), F32)] * 3,
        compiler_params=_params("parallel", "parallel"))(pn, wpg, p_in, wpp, h)


def mm_norm_bwd(name, dy, w, x, norm_w, dres):
    _, _, K, n = w.shape
    M = dy.shape[0]
    tm, tc = _pick(M, (256,)), _pick(n, (1024, 512, 256))
    steps = n // tc

    def body(dy_ref, w_ref, x_ref, nw_ref, dres_ref, dx_ref, dxb_ref, dw_ref):
        i, k = pl.program_id(0), pl.program_id(1)
        part = lax.dot_general(dy_ref[...], w_ref[...], _NT, preferred_element_type=F32)
        _accumulate(dx_ref, part, k == 0)

        @pl.when(k == steps - 1)
        def _():
            dxx, dw = _rms_bwd(x_ref[...], nw_ref[...], dx_ref[...])
            tot = dres_ref[...] + dxx
            dx_ref[...] = tot
            dxb_ref[...] = tot.astype(dxb_ref.dtype)
            _accumulate(dw_ref, dw, i == 0)

    rows = pl.BlockSpec((tm, K), lambda i, k: (i, 0))
    vec = pl.BlockSpec((1, K), lambda i, k: (0, 0))
    return pl.pallas_call(
        body, name=name, grid=(M // tm, steps),
        in_specs=[pl.BlockSpec((tm, tc), lambda i, k: (i, k)),
                  pl.BlockSpec((None, None, K, tc), lambda i, k: (0, 0, 0, k)), rows, vec, rows],
        out_specs=[rows, rows, vec],
        out_shape=[jax.ShapeDtypeStruct((M, K), F32), jax.ShapeDtypeStruct((M, K), BF16),
                   jax.ShapeDtypeStruct((1, K), F32)],
        compiler_params=_params("arbitrary", "arbitrary"))(dy, w, x, norm_w, dres)


def ffn_down_bwd(name, dy, wd, g, u, deps=()):
    _, _, K, n = wd.shape
    M = dy.shape[0]
    tm, tq = _pick(M, _TILES), _pick(K, _TILES)
    assert n <= MAX_CONTRACT

    def body(dy_ref, w_ref, g_ref, u_ref, *rest):
        dg_ref, du_ref = rest[len(deps):]
        dy_blk = dy_ref[...]
        for c0 in range(0, tq, 256):
            cols = slice(c0, c0 + 256)
            dact = lax.dot_general(dy_blk, w_ref[cols, :], _NT, preferred_element_type=F32)
            silu, dsilu = _silu_and_grad(g_ref[:, cols].astype(F32))
            dg_ref[:, cols] = (dact * u_ref[:, cols].astype(F32) * dsilu).astype(dg_ref.dtype)
            du_ref[:, cols] = (dact * silu).astype(du_ref.dtype)

    blk = pl.BlockSpec((tm, tq), lambda i, q: (i, q))
    return pl.pallas_call(
        body, name=name, grid=(M // tm, K // tq),
        in_specs=[pl.BlockSpec((tm, n), lambda i, q: (i, 0)),
                  pl.BlockSpec((None, None, tq, n), lambda i, q: (0, 0, q, 0)), blk, blk] + [_ANY] * len(deps),
        out_specs=[blk, blk], out_shape=[jax.ShapeDtypeStruct((M, K), BF16)] * 2,
        compiler_params=_params("parallel", "parallel"))(dy, wd, g, u, *deps)


def rowwise(name, fn, rows, ins, outs, tr=256, deps=()):
    widest = max([s[1].shape[1] if s[0] != "col" else s[3] for s in ins] + [s[1] for s in outs])
    tr = min(tr if widest <= 2048 else tr // 2, rows)
    in_specs, args = [], []
    for spec in ins:
        kind, a = spec[0], spec[1]
        if kind == "row":
            in_specs.append(pl.BlockSpec((tr, a.shape[1]), lambda i: (i, 0)))
        elif kind == "col":
            cb, width = spec[2], spec[3]
            in_specs.append(pl.BlockSpec((tr, width), lambda i, cb=cb: (i, cb)))
        else:
            in_specs.append(pl.BlockSpec(a.shape, lambda i: (0, 0)))
        args.append(a)
    out_specs, out_shapes = [], []
    for spec in outs:
        if spec[0] == "row":
            out_specs.append(pl.BlockSpec((tr, spec[1]), lambda i: (i, 0)))
            out_shapes.append(jax.ShapeDtypeStruct((rows, spec[1]), spec[2]))
        else:
            out_specs.append(pl.BlockSpec((1, spec[1]), lambda i: (0, 0)))
            out_shapes.append(jax.ShapeDtypeStruct((1, spec[1]), F32))
    n_in = len(ins)

    def body(*refs):
        vals = fn(*[r[...] for r in refs[:n_in]])
        first = pl.program_id(0) == 0
        for r, v, spec in zip(refs[n_in + len(deps):], vals, outs):
            if spec[0] == "row":
                r[...] = v.astype(r.dtype)
            else:
                _accumulate(r, v, first)

    return pl.pallas_call(body, name=name, grid=(rows // tr,), in_specs=in_specs + [_ANY] * len(deps),
                          out_specs=out_specs, out_shape=out_shapes,
                          compiler_params=_params("arbitrary"))(*args, *deps)


def _accumulate(ref, v, first):
    @pl.when(first)
    def _():
        ref[...] = v

    @pl.when(jnp.logical_not(first))
    def _():
        ref[...] += v


def _rms(x, w):
    r = lax.rsqrt(jnp.mean(x * x, axis=-1, keepdims=True) + NORM_EPS)
    return x * r * w


def _rms_bwd(x, w, dy):
    r = lax.rsqrt(jnp.mean(x * x, axis=-1, keepdims=True) + NORM_EPS)
    g = dy * w
    dx = r * (g - x * (r * r) * jnp.mean(g * x, axis=-1, keepdims=True))
    dw = jnp.sum(dy * x * r, axis=0, keepdims=True)
    return dx, dw


def _sigmoid(x):
    return 1.0 / (1.0 + jnp.exp(-x))


def _silu_and_grad(g):
    s = _sigmoid(g)
    return g * s, s * (1.0 + g * (1.0 - s))


def _swap_pairs(x):
    n = x.shape[-1]
    lane = lax.broadcasted_iota(jnp.int32, x.shape, x.ndim - 1)
    return jnp.where((lane & 1) == 0, pltpu.roll(x, n - 1, x.ndim - 1), pltpu.roll(x, 1, x.ndim - 1))


def _rot(x, cosf, sins):
    return x * cosf + _swap_pairs(x) * sins


def _unrot(d, cosf, sins):
    return d * cosf + _swap_pairs(d * sins)


def _ret_log_gamma(h):
    vals = [math.log1p(-2.0 ** (-5.0 - i)) for i in range(RET_HEADS)]
    out = jnp.float32(vals[RET_HEADS - 1])
    for i in range(RET_HEADS - 2, -1, -1):
        out = jnp.where(h == i, jnp.float32(vals[i]), out)
    return out


def _fill_decays(dec_ref, lg):
    ri = lax.broadcasted_iota(jnp.int32, (BLK, BLK), 0)
    ci = lax.broadcasted_iota(jnp.int32, (BLK, BLK), 1)
    for d in range(dec_ref.shape[0]):
        dt = d * BLK + ri - ci
        dec_ref[d] = jnp.where(dt >= 0, jnp.exp(jnp.maximum(dt, 0).astype(F32) * lg), 0.0)


def _decay_row(dec_ref, qi):
    return jnp.concatenate([dec_ref[qi - kb] for kb in range(qi + 1)], axis=1)


def _once(block_shape, index_map):
    return pl.BlockSpec(block_shape, index_map, pipeline_mode=pl.Buffered(1))


def _dot(a, b):
    return jnp.dot(a.astype(BF16), b.astype(BF16), preferred_element_type=F32)


def _dot_nt(a, b):
    return lax.dot_general(a.astype(BF16), b.astype(BF16), _NT, preferred_element_type=F32)


def _dot_tn(a, b):
    return lax.dot_general(a.astype(BF16), b.astype(BF16), _TN, preferred_element_type=F32)


def retention_fwd(name, z, cosf, sins, width_out):
    T = z.shape[0]
    nq = T // BLK
    scale = RET_DK ** -0.5

    def body(q_ref, k_ref, v_ref, cos_ref, sin_ref, o_ref, krot, vb, dec_ref):
        _fill_decays(dec_ref, _ret_log_gamma(pl.program_id(0)))
        krot[...] = (_rot(k_ref[...], cos_ref[...], sin_ref[...]) * scale).astype(BF16)
        vb[...] = v_ref[...].astype(BF16)
        for qi in range(nq):
            rows, n = slice(qi * BLK, (qi + 1) * BLK), (qi + 1) * BLK
            q = _rot(q_ref[rows, :], cos_ref[rows, :], sin_ref[rows, :])
            s = _dot_nt(q, krot[0:n, :]) * _decay_row(dec_ref, qi)
            o_ref[rows, :] = _dot(s, vb[0:n, :])

    return pl.pallas_call(
        body, name=name, grid=(RET_HEADS,),
        in_specs=[pl.BlockSpec((T, RET_DK), lambda h: (0, OFF_RQ // RET_DK + h)),
                  pl.BlockSpec((T, RET_DK), lambda h: (0, OFF_RK // RET_DK + h)),
                  pl.BlockSpec((T, RET_DV), lambda h: (0, OFF_RV // RET_DV + h)),
                  _once((T, RET_DK), lambda h: (0, 0)), _once((T, RET_DK), lambda h: (0, 0))],
        out_specs=pl.BlockSpec((T, RET_DV), lambda h: (0, h)),
        out_shape=jax.ShapeDtypeStruct((T, width_out), F32),
        scratch_shapes=[pltpu.VMEM((T, RET_DK), BF16), pltpu.VMEM((T, RET_DV), BF16),
                        pltpu.VMEM((nq, BLK, BLK), F32)],
        compiler_params=_params("arbitrary"))(z, z, z, cosf, sins)


def retention_bwd(name, z, cosf, sins, do):
    T = z.shape[0]
    nq = T // BLK
    scale = RET_DK ** -0.5

    def body(q_ref, k_ref, v_ref, cos_ref, sin_ref, do_ref, dq_ref, dk_ref, dv_ref, krot, vb, dk_acc, dv_acc, dec_ref):
        _fill_decays(dec_ref, _ret_log_gamma(pl.program_id(0)))
        krot[...] = (_rot(k_ref[...], cos_ref[...], sin_ref[...]) * scale).astype(BF16)
        vb[...] = v_ref[...].astype(BF16)
        dk_acc[...] = jnp.zeros_like(dk_acc)
        dv_acc[...] = jnp.zeros_like(dv_acc)
        for qi in range(nq):
            rows, n = slice(qi * BLK, (qi + 1) * BLK), (qi + 1) * BLK
            cos_q, sin_q = cos_ref[rows, :], sin_ref[rows, :]
            q = _rot(q_ref[rows, :], cos_q, sin_q).astype(BF16)
            dout = do_ref[rows, :].astype(BF16)
            kk, vv, dec = krot[0:n, :], vb[0:n, :], _decay_row(dec_ref, qi)
            p = (_dot_nt(q, kk) * dec).astype(BF16)
            ds = (_dot_nt(dout, vv) * dec).astype(BF16)
            dq_ref[rows, :] = _unrot(_dot(ds, kk), cos_q, sin_q).astype(dq_ref.dtype)
            dk_acc[0:n, :] += _dot_tn(ds, q)
            dv_acc[0:n, :] += _dot_tn(p, dout)
        dk_ref[...] = (_unrot(dk_acc[...], cos_ref[...], sin_ref[...]) * scale).astype(dk_ref.dtype)
        dv_ref[...] = dv_acc[...].astype(dv_ref.dtype)

    head = lambda h: (0, h)
    return pl.pallas_call(
        body, name=name, grid=(RET_HEADS,),
        in_specs=[pl.BlockSpec((T, RET_DK), lambda h: (0, OFF_RQ // RET_DK + h)),
                  pl.BlockSpec((T, RET_DK), lambda h: (0, OFF_RK // RET_DK + h)),
                  pl.BlockSpec((T, RET_DV), lambda h: (0, OFF_RV // RET_DV + h)),
                  _once((T, RET_DK), lambda h: (0, 0)), _once((T, RET_DK), lambda h: (0, 0)),
                  pl.BlockSpec((T, RET_DV), head)],
        out_specs=[pl.BlockSpec((T, RET_DK), head), pl.BlockSpec((T, RET_DK), head), pl.BlockSpec((T, RET_DV), head)],
        out_shape=[jax.ShapeDtypeStruct((T, RET_QK), BF16), jax.ShapeDtypeStruct((T, RET_QK), BF16),
                   jax.ShapeDtypeStruct((T, RET_V), BF16)],
        scratch_shapes=[pltpu.VMEM((T, RET_DK), BF16), pltpu.VMEM((T, RET_DV), BF16),
                        pltpu.VMEM((T, RET_DK), F32), pltpu.VMEM((T, RET_DV), F32),
                        pltpu.VMEM((nq, BLK, BLK), F32)],
        compiler_params=_params("arbitrary"))(z, z, z, cosf, sins, do)


GLA_PAIR = 2


def _gla_chunk(q_ref, k_ref, v_ref, glr_ref, gu, gb, rows, hh, trilf):
    ck = slice(hh * GLA_DK, (hh + 1) * GLA_DK)
    zg = _dot(glr_ref[rows, :], gu[:, ck]) + gb[:, ck]
    la = (jnp.minimum(zg, 0.0) - jnp.log(1.0 + jnp.exp(-jnp.abs(zg)))) * (1.0 / GLA_GATE_NORM)
    cum = jnp.dot(trilf, la, precision=HIGHEST, preferred_element_type=F32)
    last = jnp.sum(la, axis=0, keepdims=True)
    ecum = jnp.exp(cum)
    k = k_ref[rows, ck]
    qt = q_ref[rows, ck] * (GLA_DK ** -0.5) * ecum
    kt = k * jnp.exp(-cum)
    kh = k * jnp.exp(last - cum)
    return zg, cum, last, ecum, qt, kt, kh, v_ref[rows, hh * GLA_DV:(hh + 1) * GLA_DV].astype(BF16)


def _state_decay(last):
    e = jnp.exp(jnp.broadcast_to(last, (GLA_DK, GLA_DK)).T)
    return jnp.concatenate([e] * (GLA_DV // GLA_DK), axis=1)


def _gla_specs(T):
    wk, wv = GLA_PAIR * GLA_DK, GLA_PAIR * GLA_DV
    return [_once((T, wk), lambda h: (0, OFF_GQ // wk + h)),
            _once((T, wk), lambda h: (0, OFF_GK // wk + h)),
            _once((T, wv), lambda h: (0, OFF_GV // wv + h)),
            _once((T, LANE), lambda h: (0, 0)),
            pl.BlockSpec((LANE, wk), lambda h: (0, h)),
            pl.BlockSpec((1, wk), lambda h: (0, h))]


def gla_fwd(name, z, glr, gu, gb, o_prev):
    T = z.shape[0]
    nc = T // CHUNK
    wv = GLA_PAIR * GLA_DV

    def body(q_ref, k_ref, v_ref, glr_ref, gu_ref, gb_ref, prev_ref, o_ref, S):
        del prev_ref
        gu_b, gb_v = gu_ref[...].astype(BF16), gb_ref[...]
        ri = lax.broadcasted_iota(jnp.int32, (CHUNK, CHUNK), 0)
        ci = lax.broadcasted_iota(jnp.int32, (CHUNK, CHUNK), 1)
        tril = ri >= ci
        trilf = tril.astype(F32)
        S[...] = jnp.zeros_like(S)

        def step(c, carry):
            rows = pl.ds(pl.multiple_of(c * CHUNK, CHUNK), CHUNK)
            for hh in range(GLA_PAIR):
                _, _, last, _, qt, kt, kh, v = _gla_chunk(q_ref, k_ref, v_ref, glr_ref, gu_b, gb_v, rows, hh, trilf)
                a = jnp.where(tril, _dot_nt(qt, kt), 0.0)
                s_prev = S[hh]
                o_ref[rows, hh * GLA_DV:(hh + 1) * GLA_DV] = _dot(a, v) + _dot(qt, s_prev)
                S[hh] = s_prev * _state_decay(last) + _dot_tn(kh, v)
            return carry

        lax.fori_loop(0, nc, step, 0)

    n_in = 6
    return pl.pallas_call(
        body, name=name, grid=(GLA_HEADS // GLA_PAIR,),
        in_specs=_gla_specs(T) + [pl.BlockSpec(memory_space=pl.ANY)],
        out_specs=pl.BlockSpec((T, wv), lambda h: (0, RET_V // wv + h)),
        out_shape=jax.ShapeDtypeStruct(o_prev.shape, F32),
        scratch_shapes=[pltpu.VMEM((GLA_PAIR, GLA_DK, GLA_DV), F32)],
        input_output_aliases={n_in: 0},
        compiler_params=_params("arbitrary"))(z, z, z, glr, gu, gb, o_prev)


def gla_bwd(name, z, glr, gu, gb, do):
    T = z.shape[0]
    nc = T // CHUNK

    def body(q_ref, k_ref, v_ref, glr_ref, gu_ref, gb_ref, do_ref,
             dq_ref, dk_ref, dv_ref, dglr_ref, dgu_ref, dgb_ref, s_all, dS):
        gu_b, gb_v = gu_ref[...].astype(BF16), gb_ref[...]
        ri = lax.broadcasted_iota(jnp.int32, (CHUNK, CHUNK), 0)
        ci = lax.broadcasted_iota(jnp.int32, (CHUNK, CHUNK), 1)
        tril = ri >= ci
        trilf = tril.astype(F32)
        triuf = (ri <= ci).astype(F32)
        last_row = lax.broadcasted_iota(jnp.int32, (CHUNK, GLA_DK), 0) == CHUNK - 1
        ones8 = jnp.ones((8, GLA_DV), F32)

        def fstep(c, carry):
            rows = pl.ds(pl.multiple_of(c * CHUNK, CHUNK), CHUNK)
            for hh in range(GLA_PAIR):
                s_prev = dS[hh]
                s_all[hh, c] = s_prev
                _, _, last, _, _, _, kh, v = _gla_chunk(q_ref, k_ref, v_ref, glr_ref, gu_b, gb_v, rows, hh, trilf)
                dS[hh] = s_prev * _state_decay(last) + _dot_tn(kh, v)
            return carry

        dS[...] = jnp.zeros_like(dS)
        lax.fori_loop(0, nc, fstep, 0)
        dS[...] = jnp.zeros_like(dS)
        dgu_ref[...] = jnp.zeros_like(dgu_ref)
        dgb_ref[...] = jnp.zeros_like(dgb_ref)

        def bstep(i, carry):
            c = nc - 1 - i
            rows = pl.ds(pl.multiple_of(c * CHUNK, CHUNK), CHUNK)
            glr_c = glr_ref[rows, :]
            for hh in range(GLA_PAIR):
                ck, cv = slice(hh * GLA_DK, (hh + 1) * GLA_DK), slice(hh * GLA_DV, (hh + 1) * GLA_DV)
                zg, cum, last, ecum, qt, kt, kh, v = _gla_chunk(q_ref, k_ref, v_ref, glr_ref, gu_b, gb_v, rows, hh, trilf)
                a = jnp.where(tril, _dot_nt(qt, kt), 0.0)
                s_prev, ds_new = s_all[hh, c], dS[hh]
                dout = do_ref[rows, cv].astype(BF16)
                dv_ref[rows, cv] = (_dot_tn(a, dout) + _dot(kh, ds_new)).astype(dv_ref.dtype)
                da = jnp.where(tril, _dot_nt(dout, v), 0.0)
                dqt = _dot(da, kt) + _dot_nt(dout, s_prev)
                dkt = _dot_tn(da, qt)
                dkh = _dot_nt(v, ds_new)
                dS[hh] = ds_new * _state_decay(last) + _dot_tn(qt, dout)
                dq_ref[rows, ck] = (dqt * ecum * (GLA_DK ** -0.5)).astype(dq_ref.dtype)
                dk_ref[rows, ck] = (dkt * jnp.exp(-cum) + dkh * jnp.exp(last - cum)).astype(dk_ref.dtype)
                dkh_kh = dkh * kh
                dcum = dqt * qt - dkt * kt - dkh_kh
                rs = lax.dot_general(ones8, ds_new * s_prev, _NT, precision=HIGHEST, preferred_element_type=F32)
                dlast = (jnp.sum(dkh_kh, axis=0, keepdims=True)
                         + jnp.exp(last) * (jnp.sum(rs, axis=0, keepdims=True) * 0.125))
                dcum = dcum + jnp.where(last_row, dlast, 0.0)
                dla = jnp.dot(triuf, dcum, precision=HIGHEST, preferred_element_type=F32)
                dzg = dla * (1.0 / GLA_GATE_NORM) * _sigmoid(-zg)
                dglr_ref[hh, rows, :] = _dot_nt(dzg, gu_b[:, ck])
                dgu_ref[:, ck] += _dot_tn(glr_c, dzg)
                dgb_ref[:, ck] += jnp.sum(dzg, axis=0, keepdims=True)
            return carry

        lax.fori_loop(0, nc, bstep, 0)

    wk, wv = GLA_PAIR * GLA_DK, GLA_PAIR * GLA_DV
    return pl.pallas_call(
        body, name=name, grid=(GLA_HEADS // GLA_PAIR,),
        in_specs=_gla_specs(T) + [_once((T, wv), lambda h: (0, RET_V // wv + h))],
        out_specs=[pl.BlockSpec((T, wk), lambda h: (0, h)), pl.BlockSpec((T, wk), lambda h: (0, h)),
                   pl.BlockSpec((T, wv), lambda h: (0, h)),
                   pl.BlockSpec((GLA_PAIR, T, LANE), lambda h: (h, 0, 0)),
                   pl.BlockSpec((LANE, wk), lambda h: (0, h)), pl.BlockSpec((1, wk), lambda h: (0, h))],
        out_shape=[jax.ShapeDtypeStruct((T, GLA_QK), BF16), jax.ShapeDtypeStruct((T, GLA_QK), BF16),
                   jax.ShapeDtypeStruct((T, GLA_V), BF16), jax.ShapeDtypeStruct((GLA_HEADS, T, LANE), F32),
                   jax.ShapeDtypeStruct((LANE, GLA_QK), F32), jax.ShapeDtypeStruct((1, GLA_QK), F32)],
        scratch_shapes=[pltpu.VMEM((GLA_PAIR, nc, GLA_DK, GLA_DV), F32), pltpu.VMEM((GLA_PAIR, GLA_DK, GLA_DV), F32)],
        compiler_params=_params("arbitrary"))(z, z, z, glr, gu, gb, do)


HN_HEADS = RET_HEADS + GLA_HEADS
HN_W = RET_DV


def _gate_col(h):
    return jnp.where(h < RET_HEADS, OFF_RG // HN_W + h, OFF_GG // HN_W + h - RET_HEADS)


def headnorm_fwd(name, oraw, z, w):
    T = oraw.shape[0]
    tr = _pick(T, _TILES)

    def body(o_ref, g_ref, w_ref, y_ref):
        y_ref[...] = (_rms(o_ref[...], w_ref[...]) * _silu_and_grad(g_ref[...])[0]).astype(y_ref.dtype)

    return pl.pallas_call(
        body, name=name, grid=(HN_HEADS, T // tr),
        in_specs=[pl.BlockSpec((tr, HN_W), lambda h, i: (i, h)),
                  pl.BlockSpec((tr, HN_W), lambda h, i: (i, _gate_col(h))),
                  pl.BlockSpec((1, HN_W), lambda h, i: (0, h))],
        out_specs=pl.BlockSpec((tr, HN_W), lambda h, i: (i, h)),
        out_shape=jax.ShapeDtypeStruct((T, HN_HEADS * HN_W), BF16),
        compiler_params=_params("arbitrary", "arbitrary"))(oraw, z, w)


def headnorm_bwd(name, oraw, z, w, dy):
    T = oraw.shape[0]
    tr = _pick(T, _TILES)

    def body(o_ref, g_ref, w_ref, dy_ref, do_ref, dg_ref, dw_ref):
        o, wv, dyv = o_ref[...], w_ref[...], dy_ref[...].astype(F32)
        silu, dsilu = _silu_and_grad(g_ref[...])
        n = _rms(o, wv)
        dg_ref[...] = (dyv * n * dsilu).astype(dg_ref.dtype)
        dx, dw = _rms_bwd(o, wv, dyv * silu)
        do_ref[...] = dx
        _accumulate(dw_ref, dw, pl.program_id(1) == 0)

    blk = pl.BlockSpec((tr, HN_W), lambda h, i: (i, h))
    return pl.pallas_call(
        body, name=name, grid=(HN_HEADS, T // tr),
        in_specs=[blk, pl.BlockSpec((tr, HN_W), lambda h, i: (i, _gate_col(h))),
                  pl.BlockSpec((1, HN_W), lambda h, i: (0, h)), blk],
        out_specs=[blk, blk, pl.BlockSpec((1, HN_W), lambda h, i: (0, h))],
        out_shape=[jax.ShapeDtypeStruct((T, HN_HEADS * HN_W), F32),
                   jax.ShapeDtypeStruct((T, HN_HEADS * HN_W), BF16),
                   jax.ShapeDtypeStruct((1, HN_HEADS * HN_W), F32)],
        compiler_params=_params("arbitrary", "arbitrary"))(oraw, z, w, dy)


N_MASKS = 4


def _check_mask_classes(T):
    for window, dilation in DILATED_BRANCHES[:-1]:
        assert window < (N_MASKS - 1) * BLK - (BLK - 1) and BLK % dilation == 0
    assert DILATED_BRANCHES[-1][0] >= T and BLK % DILATED_BRANCHES[-1][1] == 0


def _fill_masks(mult_ref, bias_ref):
    ri = lax.broadcasted_iota(jnp.int32, (BLK, BLK), 0)
    ci = lax.broadcasted_iota(jnp.int32, (BLK, BLK), 1)
    for d in range(N_MASKS):
        dt = d * BLK + ri - ci
        mult = jnp.zeros((BLK, BLK), F32)
        for window, dilation in DILATED_BRANCHES:
            hit = (dt >= 0) & (dt <= window) & ((dt & (dilation - 1)) == 0)
            mult = mult + hit.astype(F32)
        mult_ref[d] = mult
        bias_ref[d] = jnp.where(mult > 0, 0.0, -1e30)


def _mask_row(ref, qi):
    return jnp.concatenate([ref[min(qi - kb, N_MASKS - 1)] for kb in range(qi + 1)], axis=1)


def attn_fwd(name, qkv):
    T = qkv.shape[0]
    D = qkv.shape[1] // 3
    dh = D // ATT_HEADS
    nq = T // BLK
    scale = dh ** -0.5

    _check_mask_classes(T)

    def body(q_ref, k_ref, v_ref, o_ref, lse_ref, mult_ref, bias_ref):
        @pl.when(pl.program_id(0) == 0)
        def _():
            _fill_masks(mult_ref, bias_ref)

        for qi in range(nq):
            rows, n = slice(qi * BLK, (qi + 1) * BLK), (qi + 1) * BLK
            s = (_dot_nt(q_ref[rows, :], k_ref[0:n, :]) * scale
                 + _mask_row(bias_ref, qi))
            m = jnp.max(s, axis=-1, keepdims=True)
            p = _mask_row(mult_ref, qi) * jnp.exp(s - m)
            l = jnp.sum(p, axis=-1, keepdims=True)
            o_ref[rows, :] = (_dot(p, v_ref[0:n, :]) / l).astype(o_ref.dtype)
            lse_ref[rows, :] = jnp.broadcast_to(m + jnp.log(l), (BLK, LANE))

    return pl.pallas_call(
        body, name=name, grid=(ATT_HEADS,),
        in_specs=[pl.BlockSpec((T, dh), lambda h: (0, h)),
                  pl.BlockSpec((T, dh), lambda h: (0, ATT_HEADS + h)),
                  pl.BlockSpec((T, dh), lambda h: (0, 2 * ATT_HEADS + h))],
        out_specs=[pl.BlockSpec((T, dh), lambda h: (0, h)),
                   pl.BlockSpec((None, T, LANE), lambda h: (h, 0, 0))],
        out_shape=[jax.ShapeDtypeStruct((T, D), BF16), jax.ShapeDtypeStruct((ATT_HEADS, T, LANE), F32)],
        scratch_shapes=[pltpu.VMEM((N_MASKS, BLK, BLK), F32), pltpu.VMEM((N_MASKS, BLK, BLK), F32)],
        compiler_params=_params("arbitrary"))(qkv, qkv, qkv)


def attn_bwd(name, qkv, o, lse, do):
    T = qkv.shape[0]
    D = qkv.shape[1] // 3
    dh = D // ATT_HEADS
    nq = T // BLK
    scale = dh ** -0.5

    _check_mask_classes(T)

    def body(q_ref, k_ref, v_ref, o_ref, lse_ref, do_ref, dq_ref, dk_ref, dv_ref, dk_acc, dv_acc, mult_ref, bias_ref):
        @pl.when(pl.program_id(0) == 0)
        def _():
            _fill_masks(mult_ref, bias_ref)

        dk_acc[...] = jnp.zeros_like(dk_acc)
        dv_acc[...] = jnp.zeros_like(dv_acc)
        for qi in range(nq):
            rows, n = slice(qi * BLK, (qi + 1) * BLK), (qi + 1) * BLK
            q, dout = q_ref[rows, :], do_ref[rows, :]
            kk, vv = k_ref[0:n, :], v_ref[0:n, :]
            delta = jnp.sum(dout.astype(F32) * o_ref[rows, :].astype(F32), axis=-1, keepdims=True)
            lse = jnp.max(lse_ref[rows, :], axis=-1, keepdims=True)
            s = _dot_nt(q, kk) * scale + _mask_row(bias_ref, qi)
            p = _mask_row(mult_ref, qi) * jnp.exp(s - lse)
            ds = (p * (_dot_nt(dout, vv) - delta) * scale).astype(BF16)
            dq_ref[rows, :] = _dot(ds, kk).astype(dq_ref.dtype)
            dk_acc[0:n, :] += _dot_tn(ds, q)
            dv_acc[0:n, :] += _dot_tn(p, dout)
        dk_ref[...] = dk_acc[...].astype(dk_ref.dtype)
        dv_ref[...] = dv_acc[...].astype(dv_ref.dtype)

    full = pl.BlockSpec((T, dh), lambda h: (0, h))
    return pl.pallas_call(
        body, name=name, grid=(ATT_HEADS,),
        in_specs=[full, pl.BlockSpec((T, dh), lambda h: (0, ATT_HEADS + h)),
                  pl.BlockSpec((T, dh), lambda h: (0, 2 * ATT_HEADS + h)),
                  full, pl.BlockSpec((None, T, LANE), lambda h: (h, 0, 0)), full],
        out_specs=[full, full, full],
        out_shape=[jax.ShapeDtypeStruct((T, D), BF16)] * 3,
        scratch_shapes=[pltpu.VMEM((T, dh), F32), pltpu.VMEM((T, dh), F32),
                        pltpu.VMEM((N_MASKS, BLK, BLK), F32), pltpu.VMEM((N_MASKS, BLK, BLK), F32)],
        compiler_params=_params("arbitrary"))(qkv, qkv, qkv, o, lse, do)


def _mesh_pos():
    mx, my, mc = lax.axis_index("x"), lax.axis_index("y"), lax.axis_index("c")
    return mx, my, mc, 4 * mx + 2 * my + mc


def _peer(k, mx, my, mc):
    px, py, pc = mx ^ (k >> 2), my ^ ((k >> 1) & 1), mc ^ (k & 1)
    return (px, py, pc), 4 * px + 2 * py + pc


_SIBLING = 1
_OTHER_CHIPS = (4, 2, 6)
N_CHIP = N_DEV // 2
_PLANS = {"gather": (2, N_DEV - 1), "to_chips": (2, 1 + len(_OTHER_CHIPS)), "pass_on": (1, len(_OTHER_CHIPS)),
          "halves": (2, N_CHIP), "chip_sums": (2, len(_OTHER_CHIPS))}


def _copies(kind, items, send_sems, recv_sems):
    mx, my, mc, me = _mesh_pos()
    out = []

    def add(n, src, dst, peer):
        out.append(pltpu.make_async_remote_copy(
            src_ref=src, dst_ref=dst, send_sem=send_sems.at[n], recv_sem=recv_sems.at[n],
            device_id=peer, device_id_type=pl.DeviceIdType.MESH))

    per_item = _PLANS[kind][1]
    sibling = _peer(_SIBLING, mx, my, mc)[0]
    for i, refs in enumerate(items):
        n = i * per_item
        if kind == "gather":
            for k in range(1, N_DEV):
                add(n + k - 1, refs[0], refs[1].at[me], _peer(k, mx, my, mc)[0])
        elif kind == "to_chips":
            rows = refs[0].shape[0]
            dst = refs[1].at[me] if rows == refs[1].shape[1] else refs[1].at[me, pl.ds(0, rows)]
            for j, k in enumerate((_SIBLING,) + _OTHER_CHIPS):
                add(n + j, refs[0], dst, _peer(k, mx, my, mc)[0])
        elif kind == "pass_on":
            for j, k in enumerate(_OTHER_CHIPS):
                add(n + j, refs[0].at[me ^ k], refs[0].at[me ^ k], sibling)
        elif kind == "halves":
            for chip in range(N_CHIP):
                add(n + chip, refs[0].at[2 * chip + 1 - mc], refs[1].at[chip], sibling)
        else:
            for j, k in enumerate(_OTHER_CHIPS):
                peer, to = _peer(k, mx, my, mc)
                add(n + j, refs[0].at[to // 2], refs[1].at[me // 2], peer)
    return out


_HBM = pl.BlockSpec(memory_space=pltpu.HBM)
_SEM = pl.BlockSpec(memory_space=pltpu.SEMAPHORE)
_DATAFLOW = pltpu.SideEffectType.DATAFLOW_SIDE_EFFECTING


def exchange_call(name, waits, starts, deps=()):
    bufs, slot_of = [], {}

    def slots(items):
        out = []
        for item in items:
            for b in item:
                if id(b) not in slot_of:
                    slot_of[id(b)] = len(bufs)
                    bufs.append(b)
            out.append(tuple(slot_of[id(b)] for b in item))
        return out

    wait_plan = [(kind, slots(handle[0])) for kind, handle in waits]
    start_plan = [(kind, slots(items)) for kind, items in starts]
    wait_sems = [s for _, handle in waits for s in handle[1:]]
    n_buf, n_ws, n_start = len(bufs), len(wait_sems), len(starts)

    def body(*refs):
        buf_refs, sems_in = refs[:n_buf], refs[n_buf:n_buf + n_ws]
        outs = refs[n_buf + n_ws + len(deps):]
        pick = lambda plan: [tuple(buf_refs[s] for s in item) for item in plan]
        for wi, (kind, plan) in enumerate(wait_plan):
            copies = _copies(kind, pick(plan), sems_in[2 * wi], sems_in[2 * wi + 1])
            for cp in copies:
                cp.wait_send()
            for cp in copies:
                cp.wait_recv()
        for si, (kind, plan) in enumerate(start_plan):
            for cp in _copies(kind, pick(plan), outs[2 * si], outs[2 * si + 1]):
                cp.start()
        outs[-1][...] = jnp.zeros_like(outs[-1])

    hbm_bufs = [pltpu.with_memory_space_constraint(b, pltpu.HBM) for b in bufs]
    sem_shapes = []
    for kind, plan in start_plan:
        sem_shapes += [pltpu.SemaphoreType.DMA((len(plan) * _PLANS[kind][1],))] * 2
    outs = pl.pallas_call(
        body, name=name,
        out_shape=sem_shapes + [pltpu.HBM(b.shape, b.dtype) for b in bufs] + [jax.ShapeDtypeStruct((8, LANE), F32)],
        in_specs=[_HBM] * n_buf + [_SEM] * n_ws + [_ANY] * len(deps),
        out_specs=[_SEM] * (2 * n_start) + [_HBM] * n_buf + [pl.BlockSpec(memory_space=pltpu.VMEM)],
        input_output_aliases={i: 2 * n_start + i for i in range(n_buf)},
        compiler_params=pltpu.CompilerParams(has_side_effects=_DATAFLOW))(*hbm_bufs, *wait_sems, *deps)
    sems, thru, token = outs[:2 * n_start], outs[2 * n_start:-1], outs[-1]
    through = lambda plan: [tuple(thru[s] for s in item) for item in plan]
    waited = [through(plan) for _, plan in wait_plan]
    handles = [(through(plan), sems[2 * si], sems[2 * si + 1]) for si, (_, plan) in enumerate(start_plan)]
    return waited, handles, token


def gather_small(name, a, deps=()):
    def body(a_ref, *rest):
        o_ref, send_sems, recv_sems, local_sem = rest[len(deps):]
        me = _mesh_pos()[3]
        own = pltpu.make_async_copy(a_ref, o_ref.at[me], local_sem)
        own.start()
        copies = _copies("gather", [(a_ref, o_ref)], send_sems, recv_sems)
        for cp in copies:
            cp.start()
        for cp in copies:
            cp.wait_recv()
        for cp in copies:
            cp.wait_send()
        own.wait()

    return pl.pallas_call(
        body, name=name, in_specs=[_ANY] * (1 + len(deps)), out_specs=_ANY,
        out_shape=jax.ShapeDtypeStruct((N_DEV,) + a.shape, a.dtype),
        scratch_shapes=[pltpu.SemaphoreType.DMA((N_DEV - 1,)), pltpu.SemaphoreType.DMA((N_DEV - 1,)),
                        pltpu.SemaphoreType.DMA],
        compiler_params=pltpu.CompilerParams(has_side_effects=True))(a, *deps)


def _adamw_math(w, g, m, v):
    m2 = ADAM_B1 * m + (1.0 - ADAM_B1) * g
    v2 = ADAM_B2 * v + (1.0 - ADAM_B2) * (g * g)
    m_hat = m2 / (1.0 - ADAM_B1 ** ADAM_STEP)
    v_hat = v2 / (1.0 - ADAM_B2 ** ADAM_STEP)
    delta = -ADAM_LR * (m_hat / (jnp.sqrt(v_hat) + ADAM_EPS) + ADAM_WD * w)
    return delta, m2, v2


def chip_sum(name, a, half):
    _, r, c = a.shape
    tr = r
    chip = 2 * lax.axis_index("x") + lax.axis_index("y")
    where = jnp.stack([lax.axis_index("c"), chip ^ 1, chip ^ 2, chip ^ 3]).astype(jnp.int32)

    def body(where_ref, a_ref, h_ref, o_ref):
        del where_ref
        o_ref[...] = (a_ref[...].astype(F32) + h_ref[...].astype(F32)).astype(o_ref.dtype)

    blk = pl.BlockSpec((None, tr, c), lambda g, i, where: (where[1 + g], i, 0))
    grid_spec = pltpu.PrefetchScalarGridSpec(
        num_scalar_prefetch=1, grid=(N_CHIP - 1, r // tr),
        in_specs=[pl.BlockSpec((None, None, tr, c), lambda g, i, where: (where[1 + g], where[0], i, 0)), blk],
        out_specs=blk)
    return pl.pallas_call(
        body, name=name, grid_spec=grid_spec, out_shape=jax.ShapeDtypeStruct((N_CHIP, r, c), BF16),
        compiler_params=_params("parallel", "parallel"))(where, a.reshape(N_CHIP, 2, r, c), half)


def adamw(name, w, m, v, l, land, a, half, prev=None):
    L, r, c = w.shape
    cp = land.shape[2]
    tr = _pick(r, (256, 176, 128, 64, 32, 16, 8))

    def body(w_ref, m_ref, v_ref, land_ref, a_ref, half_ref, *rest):
        g_ref, d_ref, m2_ref, v2_ref = rest[-4:]
        chip = _mesh_pos()[3] // 2
        mine = a_ref[:, pl.ds(0, c)].astype(F32) + half_ref[:, pl.ds(0, c)].astype(F32)
        g = None
        for s in range(N_CHIP):
            part = jnp.where(chip == s, mine, land_ref[s, :, pl.ds(0, c)].astype(F32))
            g = part if g is None else g + part
        delta, m2, v2 = _adamw_math(w_ref[...], g, m_ref[...], v_ref[...])
        g_ref[...] = g
        d_ref[...] = delta
        m2_ref[...] = m2
        v2_ref[...] = v2

    blk = pl.BlockSpec((None, tr, c), lambda i: (l, i, 0))
    shape = jax.ShapeDtypeStruct((L, r, c), F32)
    extra = [] if prev is None else list(prev)
    return pl.pallas_call(
        body, name=name, grid=(r // tr,),
        in_specs=[blk, blk, blk, pl.BlockSpec((N_CHIP, tr, cp), lambda i: (0, i, 0)),
                  pl.BlockSpec((None, tr, cp), lambda i: (_mesh_pos()[3], i, 0)),
                  pl.BlockSpec((None, tr, cp), lambda i: (_mesh_pos()[3] // 2, i, 0))] + [_ANY] * len(extra),
        out_specs=[blk] * 4, out_shape=[shape] * 4,
        input_output_aliases={6 + k: k for k in range(len(extra))},
        compiler_params=_params("parallel"))(w, m, v, land, a, half, *extra)


def adamw_columns(name, w, m, v, land, a, half):
    r, _, D = w.shape
    tc = _pick(D, (256, 128))

    def body(w_ref, m_ref, v_ref, land_ref, a_ref, half_ref, g_ref, d_ref, m2_ref, v2_ref):
        chip = _mesh_pos()[3] // 2
        mine = a_ref[...].astype(F32) + half_ref[...].astype(F32)
        g = None
        for s in range(N_CHIP):
            part = jnp.where(chip == s, mine, land_ref[s].astype(F32))
            g = part if g is None else g + part
        flat = lambda ref: ref[...].reshape(r, tc)
        delta, m2, v2 = _adamw_math(flat(w_ref), g, flat(m_ref), flat(v_ref))
        for ref, val in ((g_ref, g), (d_ref, delta), (m2_ref, m2), (v2_ref, v2)):
            ref[...] = val.reshape(r, 1, tc)

    blk = pl.BlockSpec((r, 1, tc), lambda i: (0, 0, i))
    shape = jax.ShapeDtypeStruct((r, 1, D), F32)
    return pl.pallas_call(
        body, name=name, grid=(D // tc,),
        in_specs=[blk, blk, blk, pl.BlockSpec((N_CHIP, r, tc), lambda i: (0, 0, i)),
                  pl.BlockSpec((None, r, tc), lambda i: (_mesh_pos()[3], 0, i)),
                  pl.BlockSpec((None, r, tc), lambda i: (_mesh_pos()[3] // 2, 0, i))],
        out_specs=[blk] * 4, out_shape=[shape] * 4,
        compiler_params=_params("parallel"))(w, m, v, land, a, half)


def adamw_small(name, w, m, v, parts):
    n = w.shape[1]

    def body(w_ref, m_ref, v_ref, p_ref, g_ref, d_ref, m2_ref, v2_ref):
        g = p_ref[0:1, :]
        for s in range(1, N_DEV):
            g = g + p_ref[s:s + 1, :]
        delta, m2, v2 = _adamw_math(w_ref[...], g, m_ref[...], v_ref[...])
        g_ref[...] = g
        d_ref[...] = delta
        m2_ref[...] = m2
        v2_ref[...] = v2

    shape = jax.ShapeDtypeStruct((1, n), F32)
    return pl.pallas_call(body, name=name, out_shape=[shape] * 4,
                          compiler_params=pltpu.CompilerParams(vmem_limit_bytes=VMEM_LIMIT_BYTES))(w, m, v, parts)


def _rope_tables(positions):
    half = RET_DK // 2
    inv_freq = 1.0 / jnp.power(RET_THETA_BASE, jnp.linspace(0.0, 1.0, half, dtype=F32))
    ang = positions.astype(F32)[:, None] * inv_freq
    cos, sin = jnp.cos(ang), jnp.sin(ang)
    cosf = jnp.repeat(cos, 2, axis=-1)
    sins = jnp.stack([-sin, sin], axis=-1).reshape(cosf.shape)
    return cosf, sins


def _pad_to(a, axis, size):
    pad = [(0, 0)] * a.ndim
    pad[axis] = (0, size - a.shape[axis])
    return jnp.pad(a, pad)


def _round_up(n, m):
    return -(-n // m) * m


def kernel(x, p, positions, attn_norm_w, ffn_norm_w, ple_norm_w, final_norm_w, ab_w_in, ab_gla_gate_up, ab_gla_gate_b, ab_ret_norm_w, ab_gla_norm_w, ab_w_out, c_w_qkv, c_w_out, ffn_w_gate, ffn_w_up, ffn_w_down, ple_w_proj, ple_w_gate, loss_target, m_attn_norm_w, m_ffn_norm_w, m_ple_norm_w, m_final_norm_w, m_ab_w_in, m_ab_gla_gate_up, m_ab_gla_gate_b, m_ab_ret_norm_w, m_ab_gla_norm_w, m_ab_w_out, m_c_w_qkv, m_c_w_out, m_ffn_w_gate, m_ffn_w_up, m_ffn_w_down, m_ple_w_proj, m_ple_w_gate, v_attn_norm_w, v_ffn_norm_w, v_ple_norm_w, v_final_norm_w, v_ab_w_in, v_ab_gla_gate_up, v_ab_gla_gate_b, v_ab_ret_norm_w, v_ab_gla_norm_w, v_ab_w_out, v_c_w_qkv, v_c_w_out, v_ffn_w_gate, v_ffn_w_up, v_ffn_w_down, v_ple_w_proj, v_ple_w_gate):
    T, D = x.shape[1], x.shape[2]
    depth = attn_norm_w.shape[0]
    assert ab_w_in.shape[0] == 1 and c_w_qkv.shape[0] == 1 and depth == 2, "one even and one odd layer"
    me = 4 * lax.axis_index("x") + 2 * lax.axis_index("y") + lax.axis_index("c")
    in_shard = ab_w_in.shape[2]
    in_width = in_shard * N_DEV
    assert in_width == OFF_LR + GLA_GATE_RANK
    fs = ffn_w_gate.shape[2]
    fp = _round_up(fs, LANE)
    gu_cols = ab_gla_gate_up.shape[2]

    bf = lambda a: a.astype(BF16)
    tr_ = lambda a: jnp.swapaxes(a, -1, -2)
    wg_t, wu_t = tr_(ffn_w_gate), tr_(ffn_w_up)
    srcs = {"w_in": bf(tr_(ab_w_in[0]))}
    group_keys = [["w_in"], ["gu", "w_oab"], ["wg0", "wu0"], ["wd0", "wpg0", "wpp0"], ["w_qkv", "w_oc"],
                  ["wg1", "wu1"], ["wd1", "wpg1", "wpp1"]]
    G_IN, G_OUT, G_QKV = 0, 1, 4
    g_ffn = lambda layer: (2, 3) if layer == 0 else (5, 6)

    def landing(key):
        a = srcs[key]
        rows = fp if key[:2] in ("wg", "wu", "wd") else a.shape[0]
        buf = lax.empty((N_DEV, rows) + a.shape[1:], a.dtype)
        if rows > a.shape[0]:
            zeros = jnp.zeros((N_DEV, rows - a.shape[0]) + a.shape[1:], a.dtype)
            buf = lax.dynamic_update_slice(buf, zeros, (0, a.shape[0]) + (0,) * (a.ndim - 1))
        return lax.dynamic_update_slice(buf, a[None], (me,) + (0,) * a.ndim)

    _, chip_handles, gather_token = exchange_call(
        "gather_start_in", [], [("to_chips", [(srcs[k], landing(k)) for k in group_keys[G_IN]])])
    (gather_token, w_out_, gu_, w_qkv_, w_oc_, wg_, wu_, wd_, wpg_, wpp_) = lax.optimization_barrier(
        (gather_token, ab_w_out, ab_gla_gate_up, c_w_qkv, c_w_out, wg_t, wu_t, ffn_w_down, ple_w_gate, ple_w_proj))
    srcs.update(w_oab=bf(w_out_[0]), gu=gu_[0], w_qkv=bf(w_qkv_[0]), w_oc=bf(w_oc_[0]))
    for l in range(depth):
        srcs[f"wg{l}"] = bf(wg_[l])
        srcs[f"wu{l}"] = bf(wu_[l])
        srcs[f"wd{l}"] = bf(wd_[l])
        srcs[f"wpg{l}"] = bf(wpg_[l])
        srcs[f"wpp{l}"] = bf(wpp_[l])
    _, more, gather_token = exchange_call(
        "gather_start", [], [("to_chips", [(srcs[k], landing(k)) for k in keys]) for keys in group_keys[1:]],
        deps=(gather_token,))
    chip_handles = chip_handles + more
    weights = {}

    def gather_wait(gi, dep):
        lands = [(land,) for _, land in chip_handles[gi][0]]
        _, (passing,), _ = exchange_call(
            f"gather{gi}_pass", [("to_chips", chip_handles[gi])], [("pass_on", lands)], deps=(dep,))
        (complete,), _, _ = exchange_call(f"gather{gi}_done", [("pass_on", passing)], [])
        weights.update(zip(group_keys[gi], [land for (land,) in complete]))

    gb = ab_gla_gate_b
    hn_w = jnp.concatenate([ab_ret_norm_w, ab_gla_norm_w], axis=1)
    cosf, sins = _rope_tables(positions[0])
    p_bf = bf(p[:, 0])

    xs = x[0]
    saved = []
    for i in range(depth):
        nm = f"l{i}_"
        w_attn, w_ffn, w_ple = attn_norm_w[i:i + 1], ffn_norm_w[i:i + 1], ple_norm_w[i:i + 1]
        (xn,) = rowwise(nm + "norm_attn", lambda a, w: (_rms(a, w),), T, [("row", xs), ("full", w_attn)],
                        [("row", D, BF16)], deps=(gather_token,) if i == 0 else ())
        if i % 2 == 0:
            gather_wait(G_IN, xn)
            w_in_t = weights["w_in"].reshape(1, 1, in_width, D)
            w_lr_t = _pad_to(w_in_t[0, 0, OFF_LR:], 0, LANE).reshape(1, 1, LANE, D)
            z = mmt_fwd(nm + "mm_in", xn, w_in_t, 0, F32, n=OFF_LR)
            glr = mmt_fwd(nm + "mm_lr", xn, w_lr_t, 0, F32)
            oraw = retention_fwd(nm + "ret_fwd", z, cosf, sins, RET_V + GLA_V)
            gather_wait(G_OUT, oraw)
            w_oab = weights["w_oab"].reshape(1, 1, D, D)
            gu_full = _pad_to(weights["gu"].transpose(1, 0, 2).reshape(GLA_GATE_RANK, GLA_QK), 0, LANE)
            oraw = gla_fwd(nm + "gla_fwd", z, glr, gu_full, gb, oraw)
            o = headnorm_fwd(nm + "headnorm_fwd", oraw, z, hn_w)
            h1, hn = mm_add_norm(nm + "mm_out", o, w_oab, xs, w_ffn)
            mixer_saved = (z, glr, oraw, o)
        else:
            gather_wait(G_QKV, xn)
            w_qkv = weights["w_qkv"].reshape((1,) + weights["w_qkv"].shape)
            w_oc = weights["w_oc"].reshape(1, 1, D, D)
            qkv = mm_nn(nm + "mm_qkv", xn, w_qkv, 0, BF16)
            o, lse = attn_fwd(nm + "attn_fwd", qkv)
            h1, hn = mm_add_norm(nm + "mm_out", o, w_oc, xs, w_ffn)
            mixer_saved = (qkv, o, lse)
        gather_wait(g_ffn(i)[0], hn)
        wg = weights[f"wg{i}"].reshape(1, N_DEV, fp, D)
        wu = weights[f"wu{i}"].reshape(1, N_DEV, fp, D)
        g, u, act = ffn_gate_up(nm + "ffn_gate_up", hn, wg, wu)
        gather_wait(g_ffn(i)[1], act)
        wd = weights[f"wd{i}"].reshape(1, 1, N_DEV * fp, D)
        wpg = weights[f"wpg{i}"].reshape(1, 1, D, D)
        wpp = weights[f"wpp{i}"].reshape((1,) + weights[f"wpp{i}"].shape)
        h2, pn = mm_add_norm(nm + "mm_down", act, wd, h1, w_ple)
        x_next, s, e = ple_fwd(nm + "ple", pn, wpg, p_bf[i], wpp, h2)
        mixer_w = (w_in_t, w_lr_t, w_oab, gu_full) if i % 2 == 0 else (w_qkv, w_oc)
        saved.append((xs, xn, mixer_saved, mixer_w, (wg, wu, wd, wpg), h1, hn, g, u, act, h2, pn, s, e))
        xs = x_next

    def loss_fn(a, w, t):
        diff = _rms(a, w) - t
        dx, dw = _rms_bwd(a, w, diff * (1.0 / D))
        part = 0.5 * jnp.sum(jnp.mean(diff * diff, axis=-1, keepdims=True), axis=0, keepdims=True)
        return dx, dw, jnp.broadcast_to(part, (1, LANE))

    dx, d_final_w, loss_part = rowwise("loss_head", loss_fn, T,
                                       [("row", xs), ("full", final_norm_w[None, :]), ("row", loss_target[0])],
                                       [("row", D, F32), ("acc", D), ("acc", LANE)])
    loss = lax.psum(loss_part[0, 0], ("x", "y", "c"))

    grads = {}
    on_chip = []
    scatters = []

    def scatter_start(name, keys, deps=()):
        waits = [("halves", on_chip[0][1])] if on_chip else []
        starts = [("halves", [(grads[k], lax.empty((N_CHIP,) + grads[k].shape[1:], BF16)) for k in keys])] if keys else []
        waited, handles, token = exchange_call(name, waits, starts, deps=deps)
        if on_chip:
            done_keys, _ = on_chip.pop()
            sums = [chip_sum(f"{name}_sum{j}", a, half) for j, (a, half) in enumerate(waited[0])]
            _, (handle,), token = exchange_call(
                name + "_chips", [], [("chip_sums", [(cs, lax.empty(cs.shape, BF16)) for cs in sums])])
            scatters.append((done_keys, handle, waited[0]))
        if keys:
            on_chip.append((keys, handles[0]))
        return token

    d_attn_w, d_ffn_w, d_ple_w = [None] * depth, [None] * depth, [None] * depth
    for i in reversed(range(depth)):
        nm = f"l{i}_b_"
        xs_i, xn, mixer_saved, mixer_w, (wg, wu, wd, wpg), h1, hn, g, u, act, h2, pn, s, e = saved[i]
        w_attn, w_ffn, w_ple = attn_norm_w[i:i + 1], ffn_norm_w[i:i + 1], ple_norm_w[i:i + 1]

        def ple_bwd(d, sv, ev):
            gate = _sigmoid(sv)
            return d * gate, d * ev * gate * (1.0 - gate)

        de, ds = rowwise(nm + "ple_out", ple_bwd, T, [("row", dx), ("row", s), ("row", e)],
                         [("row", D, BF16), ("row", D, BF16)], deps=(loss.reshape(1, 1),) if i == depth - 1 else ())
        grads[("ple_w_proj", i)] = mm_tn(nm + "mm_ple_proj_w", p_bf[i], de, N_DEV, BF16)
        grads[("ple_w_gate", i)] = mm_tn(nm + "mm_ple_gate_w", pn, ds, 1, BF16).reshape(N_DEV, D // N_DEV, D)
        def norm_bwd_add(a, w, dn, dres):
            dxx, dw = _rms_bwd(a, w, dn)
            tot = dres + dxx
            return tot, tot, dw

        dh2, dh2_bf, d_ple_w[i] = mm_norm_bwd(nm + "mm_ple_gate_x", ds, wpg, h2, w_ple, dx)
        grads[("ffn_w_down", i)] = mm_tn(nm + "mm_down_w", act, dh2_bf, 1, BF16).reshape(N_DEV, fp, D)
        token = scatter_start(nm + "scatter_ple_down", [("ple_w_proj", i), ("ple_w_gate", i), ("ffn_w_down", i)])
        dg, du = ffn_down_bwd(nm + "ffn_down_x", dh2_bf, wd, g, u, deps=(token,))
        grads[("ffn_w_gate", i)] = mmt_dw(nm + "mm_gate_w", dg, hn, N_DEV, BF16)
        grads[("ffn_w_up", i)] = mmt_dw(nm + "mm_up_w", du, hn, N_DEV, BF16)
        token = scatter_start(nm + "scatter_gate_up", [("ffn_w_gate", i), ("ffn_w_up", i)])
        dhn_g = mmt_dx_wide(nm + "mm_gate_x", dg, wg, F32, deps=(token,))
        dhn_u = mmt_dx_wide(nm + "mm_up_x", du, wu, F32)

        def norm_bwd_add2(a, w, dn1, dn2, dres):
            dxx, dw = _rms_bwd(a, w, dn1 + dn2)
            tot = dres + dxx
            return tot, tot, dw

        dh1, dh1_bf, d_ffn_w[i] = rowwise(nm + "norm_ffn", norm_bwd_add2, T,
                                          [("row", h1), ("full", w_ffn), ("row", dhn_g), ("row", dhn_u), ("row", dh2)],
                                          [("row", D, F32), ("row", D, BF16), ("acc", D)])
        if i % 2 == 0:
            z, glr, oraw, o = mixer_saved
            w_in_t, w_lr_t, w_oab, gu_full = mixer_w
            grads[("ab_w_out", 0)] = mm_tn(nm + "mm_out_w", o, dh1_bf, 1, BF16).reshape(N_DEV, D // N_DEV, D)
            token = scatter_start(nm + "scatter_out", [("ab_w_out", 0)])
            do = mm_nt(nm + "mm_out_x", dh1_bf, w_oab, 0, F32, deps=(token,))
            d_oraw, d_gates, d_hn_w = headnorm_bwd(nm + "headnorm", oraw, z, hn_w, do)
            d_rq, d_rk, d_rv = retention_bwd(nm + "ret", z, cosf, sins, d_oraw)
            d_gq, d_gk, d_gv, d_glr4, d_gu, d_gb = gla_bwd(nm + "gla", z, glr, gu_full, gb, d_oraw)
            dz = jnp.concatenate([d_rq, d_rk, d_rv, d_gates[:, :RET_V], d_gq, d_gk, d_gv, d_gates[:, RET_V:]], axis=1)
            (d_glr,) = rowwise(nm + "sum_lr", lambda *a: (a[0] + a[1] + a[2] + a[3],), T,
                               [("row", d_glr4[hh]) for hh in range(GLA_HEADS)], [("row", LANE, BF16)])
            dwt_main = mmt_dw(nm + "mm_in_w", dz, xn, 1, BF16)[0]
            dwt_lr = mmt_dw(nm + "mm_lr_w", d_glr, xn, 1, BF16)[0]
            dwt_in = jnp.concatenate([dwt_main, dwt_lr[:GLA_GATE_RANK]], axis=0)
            grads[("ab_w_in", 0)] = dwt_in.reshape(N_DEV, in_shard, D)
            token = scatter_start(nm + "scatter_in", [("ab_w_in", 0)])
            dxn_a = mmt_dx_wide(nm + "mm_in_x", dz, w_in_t, F32, n=OFF_LR, deps=(token,))
            token = scatter_start(nm + "scatter_in_on", [], deps=(dxn_a,))
            dxn_b = mmt_dx(nm + "mm_lr_x", d_glr, w_lr_t, 0, F32, deps=(token,))
        else:
            qkv, o, lse = mixer_saved
            w_qkv, w_oc = mixer_w
            grads[("c_w_out", 0)] = mm_tn(nm + "mm_out_w", o, dh1_bf, 1, BF16).reshape(N_DEV, D // N_DEV, D)
            do = mm_nt(nm + "mm_out_x", dh1_bf, w_oc, 0, BF16)
            dq, dk, dv = attn_bwd(nm + "attn", qkv, o, lse, do)
            dqkv = jnp.concatenate([dq, dk, dv], axis=1)
            grads[("c_w_qkv", 0)] = mm_tn(nm + "mm_qkv_w", xn, dqkv, N_DEV, BF16)
            token = scatter_start(nm + "scatter_attn", [("c_w_out", 0), ("c_w_qkv", 0)])
            dxn_a = mm_nt_wide(nm + "mm_qkv_x", dqkv, w_qkv, F32, deps=(token,))
            dxn_b = None
        if dxn_b is None:
            dx, _, d_attn_w[i] = rowwise(nm + "norm_attn", norm_bwd_add, T,
                                         [("row", xs_i), ("full", w_attn), ("row", dxn_a), ("row", dh1)],
                                         [("row", D, F32), ("row", D, BF16), ("acc", D)])
        else:
            dx, _, d_attn_w[i] = rowwise(nm + "norm_attn", norm_bwd_add2, T,
                                         [("row", xs_i), ("full", w_attn), ("row", dxn_a), ("row", dxn_b), ("row", dh1)],
                                         [("row", D, F32), ("row", D, BF16), ("acc", D)])

    small_names = ["attn_norm_w", "ffn_norm_w", "ple_norm_w", "final_norm_w", "ab_gla_gate_b", "ab_ret_norm_w",
                   "ab_gla_norm_w"]
    small_grads = [jnp.concatenate(d_attn_w, 0), jnp.concatenate(d_ffn_w, 0), jnp.concatenate(d_ple_w, 0), d_final_w[0],
                   d_gb, d_hn_w[:, :RET_V], d_hn_w[:, RET_V:]]
    small_w = [attn_norm_w, ffn_norm_w, ple_norm_w, final_norm_w, ab_gla_gate_b, ab_ret_norm_w, ab_gla_norm_w]
    small_m = [m_attn_norm_w, m_ffn_norm_w, m_ple_norm_w, m_final_norm_w, m_ab_gla_gate_b, m_ab_ret_norm_w, m_ab_gla_norm_w]
    small_v = [v_attn_norm_w, v_ffn_norm_w, v_ple_norm_w, v_final_norm_w, v_ab_gla_gate_b, v_ab_ret_norm_w, v_ab_gla_norm_w]
    sizes = [int(np.prod(a.shape)) for a in small_w]
    n_gu = GLA_GATE_RANK * GLA_QK
    n_small = _round_up(sum(sizes) + n_gu, LANE)
    pack = lambda parts: _pad_to(jnp.concatenate([a.reshape(-1) for a in parts]), 0, n_small)[None, :]
    small_part = pack(small_grads + [d_gu[:GLA_GATE_RANK]])

    cols_first = lambda a: jnp.transpose(a, (2, 0, 1))
    big_w = dict(ab_w_in=tuple(cols_first(a) for a in (ab_w_in, m_ab_w_in, v_ab_w_in)),
                 ab_w_out=(ab_w_out, m_ab_w_out, v_ab_w_out),
                 c_w_qkv=(c_w_qkv, m_c_w_qkv, v_c_w_qkv), c_w_out=(c_w_out, m_c_w_out, v_c_w_out),
                 ffn_w_gate=(wg_t, tr_(m_ffn_w_gate), tr_(v_ffn_w_gate)),
                 ffn_w_up=(wu_t, tr_(m_ffn_w_up), tr_(v_ffn_w_up)),
                 ffn_w_down=(ffn_w_down, m_ffn_w_down, v_ffn_w_down), ple_w_proj=(ple_w_proj, m_ple_w_proj, v_ple_w_proj),
                 ple_w_gate=(ple_w_gate, m_ple_w_gate, v_ple_w_gate))
    if on_chip:
        scatter_start("scatter_last", [], deps=(dx,))
    results, last = {}, dx
    for gi, (keys, handle, partials) in enumerate(scatters):
        (arrived,), _, _ = exchange_call(f"scatter_wait{gi}", [("chip_sums", handle)], [], deps=(last,))
        for (n, l), (_, land), (a, half) in zip(keys, arrived, partials):
            if n == "ab_w_in":
                results[n] = adamw_columns(f"adamw_{n}", *big_w[n], land, a, half)
            else:
                results[n] = adamw(f"adamw_{n}{l}", *big_w[n], l, land, a, half, prev=results.get(n))
            last = results[n][0]
    for n in ("ffn_w_gate", "ffn_w_up"):
        results[n] = [tr_(a) for a in results[n]]
    results["ab_w_in"] = [jnp.transpose(a, (1, 2, 0)) for a in results["ab_w_in"]]
    small_parts = gather_small("gather_small", small_part, deps=(last,)).reshape(N_DEV, n_small)

    gu_off = sum(sizes)
    own_cols = lambda a: lax.dynamic_slice_in_dim(a.reshape(GLA_GATE_RANK, GLA_QK), me * gu_cols, gu_cols, axis=1)
    small_res = adamw_small("adamw_small", pack(small_w + [jnp.zeros((n_gu,), F32)]),
                            pack(small_m + [jnp.zeros((n_gu,), F32)]), pack(small_v + [jnp.ones((n_gu,), F32)]),
                            small_parts)
    g_gu_full = small_res[0][0, gu_off:gu_off + n_gu]
    g_gu = own_cols(g_gu_full)[None]
    gu_res = adamw_small("adamw_gate_up", *[_pad_to(a.reshape(1, -1), 1, _round_up(a.size, LANE)) for a in
                                            (ab_gla_gate_up, m_ab_gla_gate_up, v_ab_gla_gate_up)],
                         jnp.concatenate([_pad_to(g_gu.reshape(1, -1), 1, _round_up(g_gu.size, LANE)),
                                          jnp.zeros((N_DEV - 1, _round_up(g_gu.size, LANE)), F32)], axis=0))
    for k in range(4):
        off = 0
        for n, a, sz in zip(small_names, small_w, sizes):
            results.setdefault(n, [None] * 4)[k] = small_res[k][0, off:off + sz].reshape(a.shape)
            off += sz
        results.setdefault("ab_gla_gate_up", [None] * 4)[k] = gu_res[k][0, :g_gu.size].reshape(ab_gla_gate_up.shape)

    order = ["attn_norm_w", "ffn_norm_w", "ple_norm_w", "final_norm_w", "ab_w_in", "ab_gla_gate_up", "ab_gla_gate_b",
             "ab_ret_norm_w", "ab_gla_norm_w", "ab_w_out", "c_w_qkv", "c_w_out", "ffn_w_gate", "ffn_w_up", "ffn_w_down",
             "ple_w_proj", "ple_w_gate"]
    return (loss, dx[None], *[results[n][0] for n in order], *[results[n][1] for n in order],
            *[results[n][2] for n in order], *[results[n][3] for n in order])
```

```python
import math

import numpy as np
import jax
import jax.numpy as jnp
from jax import lax
from jax.experimental import pallas as pl
from jax.experimental.pallas import tpu as pltpu

F32 = jnp.float32
BF16 = jnp.bfloat16
HIGHEST = lax.Precision.HIGHEST

N_DEV = 8
VMEM_LIMIT_BYTES = 48 * 1024 * 1024
LANE = 128
NORM_EPS = 1e-6

RET_HEADS, RET_DK, RET_DV = 4, 256, 256
RET_THETA_BASE = 10000.0
GLA_HEADS, GLA_DK, GLA_DV = 4, 128, 256
GLA_GATE_RANK = 16
GLA_GATE_NORM = 16.0
CHUNK = 64
ATT_HEADS = 16
DILATED_BRANCHES = ((128, 1), (512, 4), (2048, 16))
BLK = 256

ADAM_LR, ADAM_B1, ADAM_B2, ADAM_EPS, ADAM_WD, ADAM_STEP = 0.001, 0.9, 0.999, 1e-08, 0.01, 10

RET_QK = RET_HEADS * RET_DK
RET_V = RET_HEADS * RET_DV
GLA_QK = GLA_HEADS * GLA_DK
GLA_V = GLA_HEADS * GLA_DV
OFF_RQ, OFF_RK, OFF_RV, OFF_RG = 0, RET_QK, 2 * RET_QK, 2 * RET_QK + RET_V
OFF_GQ = OFF_RG + RET_V
OFF_GK = OFF_GQ + GLA_QK
OFF_GV = OFF_GK + GLA_QK
OFF_GG = OFF_GV + GLA_V
OFF_LR = OFF_GG + GLA_V


def _params(*sem):
    return pltpu.CompilerParams(dimension_semantics=sem or None, vmem_limit_bytes=VMEM_LIMIT_BYTES)


def _pick(n, cands):
    for c in cands:
        if n % c == 0:
            return c
    raise ValueError(f"no tile for {n} in {cands}")


_NN = (((1,), (0,)), ((), ()))
_NT = (((1,), (1,)), ((), ()))
_TN = (((0,), (0,)), ((), ()))
_ANY = pl.BlockSpec(memory_space=pl.ANY)
MAX_CONTRACT = 2048
_TILES = (1024, 768, 512, 256, 128)


def _mm_call(name, dims, grid, in_specs, out_spec, out_shape, args, deps=()):
    steps = grid[2]
    assert steps == 1 or out_shape.dtype == F32

    def body(a_ref, b_ref, *rest):
        o_ref = rest[len(deps)]
        part = lax.dot_general(a_ref[...].astype(BF16), b_ref[...].astype(BF16), dims, preferred_element_type=F32)
        if steps == 1:
            o_ref[...] = part.astype(o_ref.dtype)
        else:
            _accumulate(o_ref, part, pl.program_id(2) == 0)

    return pl.pallas_call(
        body, name=name, grid=grid, in_specs=list(in_specs) + [_ANY] * len(deps), out_specs=out_spec,
        out_shape=out_shape, compiler_params=_params("parallel", "parallel", "arbitrary"))(*args, *deps)


def mm_nn(name, a, w, l, out_dtype, deps=()):
    _, J, K, n = w.shape
    M = a.shape[0]
    tm, tn, tk = _pick(M, _TILES), _pick(n, _TILES), _pick(K, (MAX_CONTRACT,) + _TILES)
    nt = n // tn
    return _mm_call(
        name, _NN, (M // tm, J * nt, K // tk),
        [pl.BlockSpec((tm, tk), lambda i, j, k: (i, k)),
         pl.BlockSpec((None, None, tk, tn), lambda i, j, k: (l, j // nt, k, j % nt))],
        pl.BlockSpec((tm, tn), lambda i, j, k: (i, j)),
        jax.ShapeDtypeStruct((M, J * n), out_dtype), (a, w), deps)


def mm_nt(name, a, w, l, out_dtype, deps=()):
    _, J, K, n = w.shape
    M = a.shape[0]
    tm, tq, tc = _pick(M, _TILES), _pick(K, _TILES), _pick(n, (MAX_CONTRACT,) + _TILES)
    nc = n // tc
    return _mm_call(
        name, _NT, (M // tm, K // tq, J * nc),
        [pl.BlockSpec((tm, tc), lambda i, q, c: (i, c)),
         pl.BlockSpec((None, None, tq, tc), lambda i, q, c: (l, c // nc, q, c % nc))],
        pl.BlockSpec((tm, tq), lambda i, q, c: (i, q)),
        jax.ShapeDtypeStruct((M, K), out_dtype), (a, w), deps)


def mm_tn(name, x, dy, J, out_dtype, deps=()):
    M, K = x.shape
    n = dy.shape[1] // J
    tp, tn = _pick(K, _TILES), _pick(n, _TILES)
    nt = n // tn
    assert M <= MAX_CONTRACT
    return _mm_call(
        name, _TN, (K // tp, J * nt, 1),
        [pl.BlockSpec((M, tp), lambda i, j, r: (0, i)),
         pl.BlockSpec((M, tn), lambda i, j, r: (0, j))],
        pl.BlockSpec((None, tp, tn), lambda i, j, r: (j // nt, i, j % nt)),
        jax.ShapeDtypeStruct((J, K, n), out_dtype), (x, dy), deps)


def mmt_fwd(name, a, wt, l, out_dtype, n=None, deps=()):
    _, J, rows, K = wt.shape
    n = rows if n is None else n
    M = a.shape[0]
    tm, tn = _pick(M, _TILES), _pick(n, _TILES)
    nt = n // tn
    assert K <= MAX_CONTRACT
    return _mm_call(
        name, _NT, (M // tm, J * nt, 1),
        [pl.BlockSpec((tm, K), lambda i, j, k: (i, 0)),
         pl.BlockSpec((None, None, tn, K), lambda i, j, k: (l, j // nt, j % nt, 0))],
        pl.BlockSpec((tm, tn), lambda i, j, k: (i, j)),
        jax.ShapeDtypeStruct((M, J * n), out_dtype), (a, wt), deps)


def mmt_dx(name, dy, wt, l, out_dtype, n=None, deps=()):
    _, J, rows, K = wt.shape
    n = rows if n is None else n
    M = dy.shape[0]
    tm, tq, tc = _pick(M, _TILES), _pick(K, _TILES), _pick(n, _TILES)
    nc = n // tc
    return _mm_call(
        name, _NN, (M // tm, K // tq, J * nc),
        [pl.BlockSpec((tm, tc), lambda i, q, c: (i, c)),
         pl.BlockSpec((None, None, tc, tq), lambda i, q, c: (l, c // nc, c % nc, q))],
        pl.BlockSpec((tm, tq), lambda i, q, c: (i, q)),
        jax.ShapeDtypeStruct((M, K), out_dtype), (dy, wt), deps)


WIDE_TILE = 512


def _wide_call(name, body, M, K, a, w, a_spec, w_spec, out_dtype, deps):
    def kernel_body(a_ref, w_ref, *rest):
        o_ref = rest[len(deps)]
        o_ref[...] = body(a_ref, w_ref).astype(o_ref.dtype)

    return pl.pallas_call(
        kernel_body, name=name, grid=(M // WIDE_TILE, K // WIDE_TILE),
        in_specs=[a_spec, w_spec] + [_ANY] * len(deps),
        out_specs=pl.BlockSpec((WIDE_TILE, WIDE_TILE), lambda i, q: (i, q)),
        out_shape=jax.ShapeDtypeStruct((M, K), out_dtype),
        compiler_params=_params("parallel", "parallel"))(a, w, *deps)


def mmt_dx_wide(name, dy, wt, out_dtype, n=None, deps=()):
    _, J, rows, K = wt.shape
    n = rows if n is None else n
    M = dy.shape[0]

    def body(dy_ref, w_ref):
        return jnp.dot(dy_ref[...].astype(BF16), w_ref[...].reshape(J * n, WIDE_TILE), preferred_element_type=F32)

    return _wide_call(name, body, M, K, dy, wt,
                      pl.BlockSpec((WIDE_TILE, J * n), lambda i, q: (i, 0)),
                      pl.BlockSpec((None, J, n, WIDE_TILE), lambda i, q: (0, 0, 0, q)), out_dtype, deps)


def mm_nt_wide(name, a, w, out_dtype, deps=()):
    _, J, K, n = w.shape
    M = a.shape[0]

    def body(a_ref, w_ref):
        acc = None
        for j in range(J):
            part = lax.dot_general(a_ref[:, j * n:(j + 1) * n].astype(BF16), w_ref[j], _NT, preferred_element_type=F32)
            acc = part if acc is None else acc + part
        return acc

    return _wide_call(name, body, M, K, a, w,
                      pl.BlockSpec((WIDE_TILE, J * n), lambda i, q: (i, 0)),
                      pl.BlockSpec((None, J, WIDE_TILE, n), lambda i, q: (0, 0, q, 0)), out_dtype, deps)


def mmt_dw(name, dy, x, J, out_dtype, deps=()):
    M, K = x.shape
    n = dy.shape[1] // J
    tn, tp = _pick(n, _TILES), _pick(K, _TILES)
    nt = n // tn
    assert M <= MAX_CONTRACT
    return _mm_call(
        name, _TN, (J * nt, K // tp, 1),
        [pl.BlockSpec((M, tn), lambda j, i, r: (0, j)),
         pl.BlockSpec((M, tp), lambda j, i, r: (0, i))],
        pl.BlockSpec((None, tn, tp), lambda j, i, r: (j // nt, j % nt, i)),
        jax.ShapeDtypeStruct((J, n, K), out_dtype), (dy, x), deps)


def ffn_gate_up(name, a, wg, wu):
    _, J, n, K = wg.shape
    M = a.shape[0]
    tm, tn = _pick(M, _TILES), _pick(n, _TILES)
    nt = n // tn
    assert K <= MAX_CONTRACT

    def body(a_ref, wg_ref, wu_ref, g_ref, u_ref, act_ref):
        x = a_ref[...]
        g = lax.dot_general(x, wg_ref[...], _NT, preferred_element_type=F32)
        u = lax.dot_general(x, wu_ref[...], _NT, preferred_element_type=F32)
        g_ref[...] = g.astype(g_ref.dtype)
        u_ref[...] = u.astype(u_ref.dtype)
        act_ref[...] = (_silu_and_grad(g)[0] * u).astype(act_ref.dtype)

    w_spec = pl.BlockSpec((None, None, tn, K), lambda i, j: (0, j // nt, j % nt, 0))
    out = pl.BlockSpec((tm, tn), lambda i, j: (i, j))
    return pl.pallas_call(
        body, name=name, grid=(M // tm, J * nt),
        in_specs=[pl.BlockSpec((tm, K), lambda i, j: (i, 0)), w_spec, w_spec],
        out_specs=[out] * 3, out_shape=[jax.ShapeDtypeStruct((M, J * n), BF16)] * 3,
        compiler_params=_params("parallel", "parallel"))(a, wg, wu)


def mm_add_norm(name, a, w, res, norm_w):
    _, _, K, N = w.shape
    M = a.shape[0]
    tm, tk = _pick(M, (WIDE_TILE, 256)), _pick(K, (1024, 512, 256))
    steps = K // tk

    def body(a_ref, w_ref, res_ref, nw_ref, h_ref, hn_ref):
        k = pl.program_id(1)
        part = jnp.dot(a_ref[...], w_ref[...], preferred_element_type=F32)
        _accumulate(h_ref, part, k == 0)

        @pl.when(k == steps - 1)
        def _():
            h = h_ref[...] + res_ref[...]
            h_ref[...] = h
            hn_ref[...] = _rms(h, nw_ref[...]).astype(hn_ref.dtype)

    rows = pl.BlockSpec((tm, N), lambda i, k: (i, 0))
    return pl.pallas_call(
        body, name=name, grid=(M // tm, steps),
        in_specs=[pl.BlockSpec((tm, tk), lambda i, k: (i, k)),
                  pl.BlockSpec((None, None, tk, N), lambda i, k: (0, 0, k, 0)), rows,
                  pl.BlockSpec((1, N), lambda i, k: (0, 0))],
        out_specs=[rows, rows],
        out_shape=[jax.ShapeDtypeStruct((M, N), F32), jax.ShapeDtypeStruct((M, N), BF16)],
        compiler_params=_params("parallel", "arbitrary"))(a, w, res, norm_w)


def ple_fwd(name, pn, wpg, p_in, wpp, h):
    _, J, P, n = wpp.shape
    M, D = h.shape
    tm, tn = _pick(M, (WIDE_TILE, 256)), _pick(D, _TILES)
    per_tile = tn // n

    def body(pn_ref, wg_ref, p_ref, wp_ref, h_ref, x_ref, s_ref, e_ref):
        s = jnp.dot(pn_ref[...], wg_ref[...], preferred_element_type=F32)
        p_blk = p_ref[...]
        e = jnp.concatenate([jnp.dot(p_blk, wp_ref[j], preferred_element_type=F32) for j in range(per_tile)], axis=1)
        s_ref[...] = s
        e_ref[...] = e
        x_ref[...] = h_ref[...] + _sigmoid(s) * e

    tile = pl.BlockSpec((tm, tn), lambda i, j: (i, j))
    return pl.pallas_call(
        body, name=name, grid=(M // tm, D // tn),
        in_specs=[pl.BlockSpec((tm, D), lambda i, j: (i, 0)),
                  pl.BlockSpec((None, None, D, tn), lambda i, j: (0, 0, 0, j)),
                  pl.BlockSpec((tm, P), lambda i, j: (i, 0)),
                  pl.BlockSpec((None, per_tile, P, n), lambda i, j: (0, j, 0, 0)), tile],
        out_specs=[tile] * 3, out_shape=[jax.ShapeDtypeStruct((M, D), F32)] * 3,
        compiler_params=_params("parallel", "parallel"))(pn, wpg, p_in, wpp, h)


def ffn_down_bwd(name, dy, wd, g, u, deps=()):
    _, _, K, n = wd.shape
    M = dy.shape[0]
    tm, tq = _pick(M, _TILES), _pick(K, _TILES)
    assert n <= MAX_CONTRACT

    def body(dy_ref, w_ref, g_ref, u_ref, *rest):
        dg_ref, du_ref = rest[len(deps):]
        dact = lax.dot_general(dy_ref[...], w_ref[...], _NT, preferred_element_type=F32)
        silu, dsilu = _silu_and_grad(g_ref[...].astype(F32))
        dg_ref[...] = (dact * u_ref[...].astype(F32) * dsilu).astype(dg_ref.dtype)
        du_ref[...] = (dact * silu).astype(du_ref.dtype)

    blk = pl.BlockSpec((tm, tq), lambda i, q: (i, q))
    return pl.pallas_call(
        body, name=name, grid=(M // tm, K // tq),
        in_specs=[pl.BlockSpec((tm, n), lambda i, q: (i, 0)),
                  pl.BlockSpec((None, None, tq, n), lambda i, q: (0, 0, q, 0)), blk, blk] + [_ANY] * len(deps),
        out_specs=[blk, blk], out_shape=[jax.ShapeDtypeStruct((M, K), BF16)] * 2,
        compiler_params=_params("parallel", "parallel"))(dy, wd, g, u, *deps)


def rowwise(name, fn, rows, ins, outs, tr=256, deps=()):
    widest = max([s[1].shape[1] if s[0] != "col" else s[3] for s in ins] + [s[1] for s in outs])
    tr = min(tr if widest <= 2048 else tr // 2, rows)
    in_specs, args = [], []
    for spec in ins:
        kind, a = spec[0], spec[1]
        if kind == "row":
            in_specs.append(pl.BlockSpec((tr, a.shape[1]), lambda i: (i, 0)))
        elif kind == "col":
            cb, width = spec[2], spec[3]
            in_specs.append(pl.BlockSpec((tr, width), lambda i, cb=cb: (i, cb)))
        else:
            in_specs.append(pl.BlockSpec(a.shape, lambda i: (0, 0)))
        args.append(a)
    out_specs, out_shapes = [], []
    for spec in outs:
        if spec[0] == "row":
            out_specs.append(pl.BlockSpec((tr, spec[1]), lambda i: (i, 0)))
            out_shapes.append(jax.ShapeDtypeStruct((rows, spec[1]), spec[2]))
        else:
            out_specs.append(pl.BlockSpec((1, spec[1]), lambda i: (0, 0)))
            out_shapes.append(jax.ShapeDtypeStruct((1, spec[1]), F32))
    n_in = len(ins)

    def body(*refs):
        vals = fn(*[r[...] for r in refs[:n_in]])
        first = pl.program_id(0) == 0
        for r, v, spec in zip(refs[n_in + len(deps):], vals, outs):
            if spec[0] == "row":
                r[...] = v.astype(r.dtype)
            else:
                _accumulate(r, v, first)

    return pl.pallas_call(body, name=name, grid=(rows // tr,), in_specs=in_specs + [_ANY] * len(deps),
                          out_specs=out_specs, out_shape=out_shapes,
                          compiler_params=_params("arbitrary"))(*args, *deps)


def _accumulate(ref, v, first):
    @pl.when(first)
    def _():
        ref[...] = v

    @pl.when(jnp.logical_not(first))
    def _():
        ref[...] += v


def _rms(x, w):
    r = lax.rsqrt(jnp.mean(x * x, axis=-1, keepdims=True) + NORM_EPS)
    return x * r * w


def _rms_bwd(x, w, dy):
    r = lax.rsqrt(jnp.mean(x * x, axis=-1, keepdims=True) + NORM_EPS)
    g = dy * w
    dx = r * (g - x * (r * r) * jnp.mean(g * x, axis=-1, keepdims=True))
    dw = jnp.sum(dy * x * r, axis=0, keepdims=True)
    return dx, dw


def _sigmoid(x):
    return 1.0 / (1.0 + jnp.exp(-x))


def _silu_and_grad(g):
    s = _sigmoid(g)
    return g * s, s * (1.0 + g * (1.0 - s))


def _swap_pairs(x):
    n = x.shape[-1]
    lane = lax.broadcasted_iota(jnp.int32, x.shape, x.ndim - 1)
    return jnp.where((lane & 1) == 0, pltpu.roll(x, n - 1, x.ndim - 1), pltpu.roll(x, 1, x.ndim - 1))


def _rot(x, cosf, sins):
    return x * cosf + _swap_pairs(x) * sins


def _unrot(d, cosf, sins):
    return d * cosf + _swap_pairs(d * sins)


def _ret_log_gamma(h):
    vals = [math.log1p(-2.0 ** (-5.0 - i)) for i in range(RET_HEADS)]
    out = jnp.float32(vals[RET_HEADS - 1])
    for i in range(RET_HEADS - 2, -1, -1):
        out = jnp.where(h == i, jnp.float32(vals[i]), out)
    return out


def _fill_decays(dec_ref, lg):
    ri = lax.broadcasted_iota(jnp.int32, (BLK, BLK), 0)
    ci = lax.broadcasted_iota(jnp.int32, (BLK, BLK), 1)
    for d in range(dec_ref.shape[0]):
        dt = d * BLK + ri - ci
        dec_ref[d] = jnp.where(dt >= 0, jnp.exp(jnp.maximum(dt, 0).astype(F32) * lg), 0.0)


def _decay_row(dec_ref, qi):
    return jnp.concatenate([dec_ref[qi - kb] for kb in range(qi + 1)], axis=1)


def _once(block_shape, index_map):
    return pl.BlockSpec(block_shape, index_map, pipeline_mode=pl.Buffered(1))


def _dot(a, b):
    return jnp.dot(a.astype(BF16), b.astype(BF16), preferred_element_type=F32)


def _dot_nt(a, b):
    return lax.dot_general(a.astype(BF16), b.astype(BF16), _NT, preferred_element_type=F32)


def _dot_tn(a, b):
    return lax.dot_general(a.astype(BF16), b.astype(BF16), _TN, preferred_element_type=F32)


def retention_fwd(name, z, cosf, sins, width_out):
    T = z.shape[0]
    nq = T // BLK
    scale = RET_DK ** -0.5

    def body(q_ref, k_ref, v_ref, cos_ref, sin_ref, o_ref, krot, vb, dec_ref):
        _fill_decays(dec_ref, _ret_log_gamma(pl.program_id(0)))
        krot[...] = (_rot(k_ref[...], cos_ref[...], sin_ref[...]) * scale).astype(BF16)
        vb[...] = v_ref[...].astype(BF16)
        for qi in range(nq):
            rows, n = slice(qi * BLK, (qi + 1) * BLK), (qi + 1) * BLK
            q = _rot(q_ref[rows, :], cos_ref[rows, :], sin_ref[rows, :])
            s = _dot_nt(q, krot[0:n, :]) * _decay_row(dec_ref, qi)
            o_ref[rows, :] = _dot(s, vb[0:n, :])

    return pl.pallas_call(
        body, name=name, grid=(RET_HEADS,),
        in_specs=[pl.BlockSpec((T, RET_DK), lambda h: (0, OFF_RQ // RET_DK + h)),
                  pl.BlockSpec((T, RET_DK), lambda h: (0, OFF_RK // RET_DK + h)),
                  pl.BlockSpec((T, RET_DV), lambda h: (0, OFF_RV // RET_DV + h)),
                  _once((T, RET_DK), lambda h: (0, 0)), _once((T, RET_DK), lambda h: (0, 0))],
        out_specs=pl.BlockSpec((T, RET_DV), lambda h: (0, h)),
        out_shape=jax.ShapeDtypeStruct((T, width_out), F32),
        scratch_shapes=[pltpu.VMEM((T, RET_DK), BF16), pltpu.VMEM((T, RET_DV), BF16),
                        pltpu.VMEM((nq, BLK, BLK), F32)],
        compiler_params=_params("arbitrary"))(z, z, z, cosf, sins)


def retention_bwd(name, z, cosf, sins, do):
    T = z.shape[0]
    nq = T // BLK
    scale = RET_DK ** -0.5

    def body(q_ref, k_ref, v_ref, cos_ref, sin_ref, do_ref, dq_ref, dk_ref, dv_ref, krot, vb, dk_acc, dv_acc, dec_ref):
        _fill_decays(dec_ref, _ret_log_gamma(pl.program_id(0)))
        krot[...] = (_rot(k_ref[...], cos_ref[...], sin_ref[...]) * scale).astype(BF16)
        vb[...] = v_ref[...].astype(BF16)
        dk_acc[...] = jnp.zeros_like(dk_acc)
        dv_acc[...] = jnp.zeros_like(dv_acc)
        for qi in range(nq):
            rows, n = slice(qi * BLK, (qi + 1) * BLK), (qi + 1) * BLK
            cos_q, sin_q = cos_ref[rows, :], sin_ref[rows, :]
            q = _rot(q_ref[rows, :], cos_q, sin_q).astype(BF16)
            dout = do_ref[rows, :].astype(BF16)
            kk, vv, dec = krot[0:n, :], vb[0:n, :], _decay_row(dec_ref, qi)
            p = (_dot_nt(q, kk) * dec).astype(BF16)
            ds = (_dot_nt(dout, vv) * dec).astype(BF16)
            dq_ref[rows, :] = _unrot(_dot(ds, kk), cos_q, sin_q).astype(dq_ref.dtype)
            dk_acc[0:n, :] += _dot_tn(ds, q)
            dv_acc[0:n, :] += _dot_tn(p, dout)
        dk_ref[...] = (_unrot(dk_acc[...], cos_ref[...], sin_ref[...]) * scale).astype(dk_ref.dtype)
        dv_ref[...] = dv_acc[...].astype(dv_ref.dtype)

    head = lambda h: (0, h)
    return pl.pallas_call(
        body, name=name, grid=(RET_HEADS,),
        in_specs=[pl.BlockSpec((T, RET_DK), lambda h: (0, OFF_RQ // RET_DK + h)),
                  pl.BlockSpec((T, RET_DK), lambda h: (0, OFF_RK // RET_DK + h)),
                  pl.BlockSpec((T, RET_DV), lambda h: (0, OFF_RV // RET_DV + h)),
                  _once((T, RET_DK), lambda h: (0, 0)), _once((T, RET_DK), lambda h: (0, 0)),
                  pl.BlockSpec((T, RET_DV), head)],
        out_specs=[pl.BlockSpec((T, RET_DK), head), pl.BlockSpec((T, RET_DK), head), pl.BlockSpec((T, RET_DV), head)],
        out_shape=[jax.ShapeDtypeStruct((T, RET_QK), BF16), jax.ShapeDtypeStruct((T, RET_QK), BF16),
                   jax.ShapeDtypeStruct((T, RET_V), BF16)],
        scratch_shapes=[pltpu.VMEM((T, RET_DK), BF16), pltpu.VMEM((T, RET_DV), BF16),
                        pltpu.VMEM((T, RET_DK), F32), pltpu.VMEM((T, RET_DV), F32),
                        pltpu.VMEM((nq, BLK, BLK), F32)],
        compiler_params=_params("arbitrary"))(z, z, z, cosf, sins, do)


GLA_PAIR = 2


def _gla_chunk(q_ref, k_ref, v_ref, glr_ref, gu, gb, rows, hh, trilf):
    ck = slice(hh * GLA_DK, (hh + 1) * GLA_DK)
    zg = _dot(glr_ref[rows, :], gu[:, ck]) + gb[:, ck]
    la = (jnp.minimum(zg, 0.0) - jnp.log(1.0 + jnp.exp(-jnp.abs(zg)))) * (1.0 / GLA_GATE_NORM)
    cum = jnp.dot(trilf, la, precision=HIGHEST, preferred_element_type=F32)
    last = jnp.sum(la, axis=0, keepdims=True)
    ecum = jnp.exp(cum)
    k = k_ref[rows, ck]
    qt = q_ref[rows, ck] * (GLA_DK ** -0.5) * ecum
    kt = k * jnp.exp(-cum)
    kh = k * jnp.exp(last - cum)
    return zg, cum, last, ecum, qt, kt, kh, v_ref[rows, hh * GLA_DV:(hh + 1) * GLA_DV].astype(BF16)


def _state_decay(last):
    e = jnp.exp(jnp.broadcast_to(last, (GLA_DK, GLA_DK)).T)
    return jnp.concatenate([e] * (GLA_DV // GLA_DK), axis=1)


def _gla_specs(T):
    wk, wv = GLA_PAIR * GLA_DK, GLA_PAIR * GLA_DV
    return [_once((T, wk), lambda h: (0, OFF_GQ // wk + h)),
            _once((T, wk), lambda h: (0, OFF_GK // wk + h)),
            _once((T, wv), lambda h: (0, OFF_GV // wv + h)),
            _once((T, LANE), lambda h: (0, 0)),
            pl.BlockSpec((LANE, wk), lambda h: (0, h)),
            pl.BlockSpec((1, wk), lambda h: (0, h))]


def gla_fwd(name, z, glr, gu, gb, o_prev):
    T = z.shape[0]
    nc = T // CHUNK
    wv = GLA_PAIR * GLA_DV

    def body(q_ref, k_ref, v_ref, glr_ref, gu_ref, gb_ref, prev_ref, o_ref, *S):
        del prev_ref
        gu_b, gb_v = gu_ref[...].astype(BF16), gb_ref[...]
        ri = lax.broadcasted_iota(jnp.int32, (CHUNK, CHUNK), 0)
        ci = lax.broadcasted_iota(jnp.int32, (CHUNK, CHUNK), 1)
        tril = ri >= ci
        trilf = tril.astype(F32)
        for s_ref in S:
            s_ref[...] = jnp.zeros_like(s_ref)

        def step(c, carry):
            rows = pl.ds(pl.multiple_of(c * CHUNK, CHUNK), CHUNK)
            heads = range(GLA_PAIR)
            ch = [_gla_chunk(q_ref, k_ref, v_ref, glr_ref, gu_b, gb_v, rows, hh, trilf) for hh in heads]
            a = [jnp.where(tril, _dot_nt(ch[hh][4], ch[hh][5]), 0.0) for hh in heads]
            s_prev = [S[hh][...] for hh in heads]
            intra = [_dot(a[hh], ch[hh][7]) for hh in heads]
            inter = [_dot(ch[hh][4], s_prev[hh]) for hh in heads]
            added = [_dot_tn(ch[hh][6], ch[hh][7]) for hh in heads]
            for hh in heads:
                o_ref[rows, hh * GLA_DV:(hh + 1) * GLA_DV] = intra[hh] + inter[hh]
                S[hh][...] = s_prev[hh] * _state_decay(ch[hh][2]) + added[hh]
            return carry

        lax.fori_loop(0, nc, step, 0)

    n_in = 6
    return pl.pallas_call(
        body, name=name, grid=(GLA_HEADS // GLA_PAIR,),
        in_specs=_gla_specs(T) + [pl.BlockSpec(memory_space=pl.ANY)],
        out_specs=pl.BlockSpec((T, wv), lambda h: (0, RET_V // wv + h)),
        out_shape=jax.ShapeDtypeStruct(o_prev.shape, F32),
        scratch_shapes=[pltpu.VMEM((GLA_DK, GLA_DV), F32)] * GLA_PAIR,
        input_output_aliases={n_in: 0},
        compiler_params=_params("arbitrary"))(z, z, z, glr, gu, gb, o_prev)


def gla_bwd(name, z, glr, gu, gb, do):
    T = z.shape[0]
    nc = T // CHUNK

    def body(q_ref, k_ref, v_ref, glr_ref, gu_ref, gb_ref, do_ref,
             dq_ref, dk_ref, dv_ref, dglr_ref, dgu_ref, dgb_ref, s_all, dS):
        gu_b, gb_v = gu_ref[...].astype(BF16), gb_ref[...]
        ri = lax.broadcasted_iota(jnp.int32, (CHUNK, CHUNK), 0)
        ci = lax.broadcasted_iota(jnp.int32, (CHUNK, CHUNK), 1)
        tril = ri >= ci
        trilf = tril.astype(F32)
        triuf = (ri <= ci).astype(F32)
        last_row = lax.broadcasted_iota(jnp.int32, (CHUNK, GLA_DK), 0) == CHUNK - 1
        ones8 = jnp.ones((8, GLA_DV), F32)

        heads = range(GLA_PAIR)

        def fstep(c, carry):
            rows = pl.ds(pl.multiple_of(c * CHUNK, CHUNK), CHUNK)
            ch = [_gla_chunk(q_ref, k_ref, v_ref, glr_ref, gu_b, gb_v, rows, hh, trilf) for hh in heads]
            added = [_dot_tn(ch[hh][6], ch[hh][7]) for hh in heads]
            for hh in heads:
                s_prev = dS[hh]
                s_all[hh, c] = s_prev
                dS[hh] = s_prev * _state_decay(ch[hh][2]) + added[hh]
            return carry

        dS[...] = jnp.zeros_like(dS)
        lax.fori_loop(0, nc, fstep, 0)
        dS[...] = jnp.zeros_like(dS)
        dgu_ref[...] = jnp.zeros_like(dgu_ref)
        dgb_ref[...] = jnp.zeros_like(dgb_ref)

        def bstep(i, carry):
            c = nc - 1 - i
            rows = pl.ds(pl.multiple_of(c * CHUNK, CHUNK), CHUNK)
            glr_c = glr_ref[rows, :]
            cks = [slice(hh * GLA_DK, (hh + 1) * GLA_DK) for hh in heads]
            cvs = [slice(hh * GLA_DV, (hh + 1) * GLA_DV) for hh in heads]
            ch = [_gla_chunk(q_ref, k_ref, v_ref, glr_ref, gu_b, gb_v, rows, hh, trilf) for hh in heads]
            zg, cum, last, ecum, qt, kt, kh, v = [[ch[hh][j] for hh in heads] for j in range(8)]
            s_prev = [s_all[hh, c] for hh in heads]
            ds_new = [dS[hh] for hh in heads]
            dout = [do_ref[rows, cvs[hh]].astype(BF16) for hh in heads]
            a = [jnp.where(tril, _dot_nt(qt[hh], kt[hh]), 0.0) for hh in heads]
            da = [jnp.where(tril, _dot_nt(dout[hh], v[hh]), 0.0) for hh in heads]
            dv_a = [_dot_tn(a[hh], dout[hh]) for hh in heads]
            dv_b = [_dot(kh[hh], ds_new[hh]) for hh in heads]
            dqt_a = [_dot(da[hh], kt[hh]) for hh in heads]
            dqt_b = [_dot_nt(dout[hh], s_prev[hh]) for hh in heads]
            dkt = [_dot_tn(da[hh], qt[hh]) for hh in heads]
            dkh = [_dot_nt(v[hh], ds_new[hh]) for hh in heads]
            ds_add = [_dot_tn(qt[hh], dout[hh]) for hh in heads]
            rs = [lax.dot_general(ones8, ds_new[hh] * s_prev[hh], _NT, precision=HIGHEST, preferred_element_type=F32)
                  for hh in heads]
            dcum = []
            for hh in heads:
                dv_ref[rows, cvs[hh]] = (dv_a[hh] + dv_b[hh]).astype(dv_ref.dtype)
                dS[hh] = ds_new[hh] * _state_decay(last[hh]) + ds_add[hh]
                dqt = dqt_a[hh] + dqt_b[hh]
                dq_ref[rows, cks[hh]] = (dqt * ecum[hh] * (GLA_DK ** -0.5)).astype(dq_ref.dtype)
                dk_ref[rows, cks[hh]] = (dkt[hh] * jnp.exp(-cum[hh])
                                         + dkh[hh] * jnp.exp(last[hh] - cum[hh])).astype(dk_ref.dtype)
                dkh_kh = dkh[hh] * kh[hh]
                dlast = (jnp.sum(dkh_kh, axis=0, keepdims=True)
                         + jnp.exp(last[hh]) * (jnp.sum(rs[hh], axis=0, keepdims=True) * 0.125))
                dcum.append(dqt * qt[hh] - dkt[hh] * kt[hh] - dkh_kh + jnp.where(last_row, dlast, 0.0))
            dla = [jnp.dot(triuf, dcum[hh], precision=HIGHEST, preferred_element_type=F32) for hh in heads]
            dzg = [dla[hh] * (1.0 / GLA_GATE_NORM) * _sigmoid(-zg[hh]) for hh in heads]
            dglr = [_dot_nt(dzg[hh], gu_b[:, cks[hh]]) for hh in heads]
            dgu = [_dot_tn(glr_c, dzg[hh]) for hh in heads]
            for hh in heads:
                dglr_ref[hh, rows, :] = dglr[hh]
                dgu_ref[:, cks[hh]] += dgu[hh]
                dgb_ref[:, cks[hh]] += jnp.sum(dzg[hh], axis=0, keepdims=True)
            return carry

        lax.fori_loop(0, nc, bstep, 0)

    wk, wv = GLA_PAIR * GLA_DK, GLA_PAIR * GLA_DV
    return pl.pallas_call(
        body, name=name, grid=(GLA_HEADS // GLA_PAIR,),
        in_specs=_gla_specs(T) + [_once((T, wv), lambda h: (0, RET_V // wv + h))],
        out_specs=[pl.BlockSpec((T, wk), lambda h: (0, h)), pl.BlockSpec((T, wk), lambda h: (0, h)),
                   pl.BlockSpec((T, wv), lambda h: (0, h)),
                   pl.BlockSpec((GLA_PAIR, T, LANE), lambda h: (h, 0, 0)),
                   pl.BlockSpec((LANE, wk), lambda h: (0, h)), pl.BlockSpec((1, wk), lambda h: (0, h))],
        out_shape=[jax.ShapeDtypeStruct((T, GLA_QK), BF16), jax.ShapeDtypeStruct((T, GLA_QK), BF16),
                   jax.ShapeDtypeStruct((T, GLA_V), BF16), jax.ShapeDtypeStruct((GLA_HEADS, T, LANE), F32),
                   jax.ShapeDtypeStruct((LANE, GLA_QK), F32), jax.ShapeDtypeStruct((1, GLA_QK), F32)],
        scratch_shapes=[pltpu.VMEM((GLA_PAIR, nc, GLA_DK, GLA_DV), F32), pltpu.VMEM((GLA_PAIR, GLA_DK, GLA_DV), F32)],
        compiler_params=_params("arbitrary"))(z, z, z, glr, gu, gb, do)


HN_HEADS = RET_HEADS + GLA_HEADS
HN_W = RET_DV


def _gate_col(h):
    return jnp.where(h < RET_HEADS, OFF_RG // HN_W + h, OFF_GG // HN_W + h - RET_HEADS)


def headnorm_fwd(name, oraw, z, w):
    T = oraw.shape[0]
    tr = _pick(T, _TILES)

    def body(o_ref, g_ref, w_ref, y_ref):
        y_ref[...] = (_rms(o_ref[...], w_ref[...]) * _silu_and_grad(g_ref[...])[0]).astype(y_ref.dtype)

    return pl.pallas_call(
        body, name=name, grid=(HN_HEADS, T // tr),
        in_specs=[pl.BlockSpec((tr, HN_W), lambda h, i: (i, h)),
                  pl.BlockSpec((tr, HN_W), lambda h, i: (i, _gate_col(h))),
                  pl.BlockSpec((1, HN_W), lambda h, i: (0, h))],
        out_specs=pl.BlockSpec((tr, HN_W), lambda h, i: (i, h)),
        out_shape=jax.ShapeDtypeStruct((T, HN_HEADS * HN_W), BF16),
        compiler_params=_params("arbitrary", "arbitrary"))(oraw, z, w)


def headnorm_bwd(name, oraw, z, w, dy):
    T = oraw.shape[0]
    tr = _pick(T, _TILES)

    def body(o_ref, g_ref, w_ref, dy_ref, do_ref, dg_ref, dw_ref):
        o, wv, dyv = o_ref[...], w_ref[...], dy_ref[...].astype(F32)
        silu, dsilu = _silu_and_grad(g_ref[...])
        n = _rms(o, wv)
        dg_ref[...] = (dyv * n * dsilu).astype(dg_ref.dtype)
        dx, dw = _rms_bwd(o, wv, dyv * silu)
        do_ref[...] = dx
        _accumulate(dw_ref, dw, pl.program_id(1) == 0)

    blk = pl.BlockSpec((tr, HN_W), lambda h, i: (i, h))
    return pl.pallas_call(
        body, name=name, grid=(HN_HEADS, T // tr),
        in_specs=[blk, pl.BlockSpec((tr, HN_W), lambda h, i: (i, _gate_col(h))),
                  pl.BlockSpec((1, HN_W), lambda h, i: (0, h)), blk],
        out_specs=[blk, blk, pl.BlockSpec((1, HN_W), lambda h, i: (0, h))],
        out_shape=[jax.ShapeDtypeStruct((T, HN_HEADS * HN_W), F32),
                   jax.ShapeDtypeStruct((T, HN_HEADS * HN_W), BF16),
                   jax.ShapeDtypeStruct((1, HN_HEADS * HN_W), F32)],
        compiler_params=_params("arbitrary", "arbitrary"))(oraw, z, w, dy)


N_MASKS = 4


def _check_mask_classes(T):
    for window, dilation in DILATED_BRANCHES[:-1]:
        assert window < (N_MASKS - 1) * BLK - (BLK - 1) and BLK % dilation == 0
    assert DILATED_BRANCHES[-1][0] >= T and BLK % DILATED_BRANCHES[-1][1] == 0


def _fill_masks(mult_ref, bias_ref):
    ri = lax.broadcasted_iota(jnp.int32, (BLK, BLK), 0)
    ci = lax.broadcasted_iota(jnp.int32, (BLK, BLK), 1)
    for d in range(N_MASKS):
        dt = d * BLK + ri - ci
        mult = jnp.zeros((BLK, BLK), F32)
        for window, dilation in DILATED_BRANCHES:
            hit = (dt >= 0) & (dt <= window) & ((dt & (dilation - 1)) == 0)
            mult = mult + hit.astype(F32)
        mult_ref[d] = mult
        bias_ref[d] = jnp.where(mult > 0, 0.0, -1e30)


def _mask_row(ref, qi):
    return jnp.concatenate([ref[min(qi - kb, N_MASKS - 1)] for kb in range(qi + 1)], axis=1)


def attn_fwd(name, qkv):
    T = qkv.shape[0]
    D = qkv.shape[1] // 3
    dh = D // ATT_HEADS
    nq = T // BLK
    scale = dh ** -0.5

    _check_mask_classes(T)

    def body(q_ref, k_ref, v_ref, o_ref, lse_ref, mult_ref, bias_ref):
        @pl.when(pl.program_id(0) == 0)
        def _():
            _fill_masks(mult_ref, bias_ref)

        for q0 in range(0, nq, 2):
            qis = range(q0, min(q0 + 2, nq))
            rows = [slice(qi * BLK, (qi + 1) * BLK) for qi in qis]
            ns = [(qi + 1) * BLK for qi in qis]
            s = [_dot_nt(q_ref[r, :], k_ref[0:n, :]) for r, n in zip(rows, ns)]
            s = [x * scale + _mask_row(bias_ref, qi) for x, qi in zip(s, qis)]
            m = [jnp.max(x, axis=-1, keepdims=True) for x in s]
            p = [_mask_row(mult_ref, qi) * jnp.exp(x - mx) for qi, x, mx in zip(qis, s, m)]
            l = [jnp.sum(x, axis=-1, keepdims=True) for x in p]
            pv = [_dot(x, v_ref[0:n, :]) for x, n in zip(p, ns)]
            for r, x, lx, mx in zip(rows, pv, l, m):
                o_ref[r, :] = (x / lx).astype(o_ref.dtype)
                lse_ref[r, :] = jnp.broadcast_to(mx + jnp.log(lx), (BLK, LANE))

    return pl.pallas_call(
        body, name=name, grid=(ATT_HEADS,),
        in_specs=[pl.BlockSpec((T, dh), lambda h: (0, h)),
                  pl.BlockSpec((T, dh), lambda h: (0, ATT_HEADS + h)),
                  pl.BlockSpec((T, dh), lambda h: (0, 2 * ATT_HEADS + h))],
        out_specs=[pl.BlockSpec((T, dh), lambda h: (0, h)),
                   pl.BlockSpec((None, T, LANE), lambda h: (h, 0, 0))],
        out_shape=[jax.ShapeDtypeStruct((T, D), BF16), jax.ShapeDtypeStruct((ATT_HEADS, T, LANE), F32)],
        scratch_shapes=[pltpu.VMEM((N_MASKS, BLK, BLK), F32), pltpu.VMEM((N_MASKS, BLK, BLK), F32)],
        compiler_params=_params("arbitrary"))(qkv, qkv, qkv)


def attn_bwd(name, qkv, o, lse, do):
    T = qkv.shape[0]
    D = qkv.shape[1] // 3
    dh = D // ATT_HEADS
    nq = T // BLK
    scale = dh ** -0.5

    _check_mask_classes(T)

    def body(q_ref, k_ref, v_ref, o_ref, lse_ref, do_ref, dq_ref, dk_ref, dv_ref, dk_acc, dv_acc, mult_ref, bias_ref):
        @pl.when(pl.program_id(0) == 0)
        def _():
            _fill_masks(mult_ref, bias_ref)

        dk_acc[...] = jnp.zeros_like(dk_acc)
        dv_acc[...] = jnp.zeros_like(dv_acc)
        for qi in range(nq):
            rows, n = slice(qi * BLK, (qi + 1) * BLK), (qi + 1) * BLK
            q, dout = q_ref[rows, :], do_ref[rows, :]
            kk, vv = k_ref[0:n, :], v_ref[0:n, :]
            delta = jnp.sum(dout.astype(F32) * o_ref[rows, :].astype(F32), axis=-1, keepdims=True)
            lse = jnp.max(lse_ref[rows, :], axis=-1, keepdims=True)
            s = _dot_nt(q, kk) * scale + _mask_row(bias_ref, qi)
            p = _mask_row(mult_ref, qi) * jnp.exp(s - lse)
            ds = (p * (_dot_nt(dout, vv) - delta) * scale).astype(BF16)
            dq_ref[rows, :] = _dot(ds, kk).astype(dq_ref.dtype)
            dk_acc[0:n, :] += _dot_tn(ds, q)
            dv_acc[0:n, :] += _dot_tn(p, dout)
        dk_ref[...] = dk_acc[...].astype(dk_ref.dtype)
        dv_ref[...] = dv_acc[...].astype(dv_ref.dtype)

    full = pl.BlockSpec((T, dh), lambda h: (0, h))
    return pl.pallas_call(
        body, name=name, grid=(ATT_HEADS,),
        in_specs=[full, pl.BlockSpec((T, dh), lambda h: (0, ATT_HEADS + h)),
                  pl.BlockSpec((T, dh), lambda h: (0, 2 * ATT_HEADS + h)),
                  full, pl.BlockSpec((None, T, LANE), lambda h: (h, 0, 0)), full],
        out_specs=[full, full, full],
        out_shape=[jax.ShapeDtypeStruct((T, D), BF16)] * 3,
        scratch_shapes=[pltpu.VMEM((T, dh), F32), pltpu.VMEM((T, dh), F32),
                        pltpu.VMEM((N_MASKS, BLK, BLK), F32), pltpu.VMEM((N_MASKS, BLK, BLK), F32)],
        compiler_params=_params("arbitrary"))(qkv, qkv, qkv, o, lse, do)


def _mesh_pos():
    mx, my, mc = lax.axis_index("x"), lax.axis_index("y"), lax.axis_index("c")
    return mx, my, mc, 4 * mx + 2 * my + mc


def _peer(k, mx, my, mc):
    px, py, pc = mx ^ (k >> 2), my ^ ((k >> 1) & 1), mc ^ (k & 1)
    return (px, py, pc), 4 * px + 2 * py + pc


_SIBLING = 1
_OTHER_CHIPS = (4, 2, 6)
N_CHIP = N_DEV // 2
_PLANS = {"gather": (2, N_DEV - 1), "to_chips": (2, 1 + len(_OTHER_CHIPS)), "pass_on": (1, len(_OTHER_CHIPS)),
          "halves": (2, N_CHIP), "chip_sums": (2, len(_OTHER_CHIPS))}


def _copies(kind, items, send_sems, recv_sems):
    mx, my, mc, me = _mesh_pos()
    out = []

    def add(n, src, dst, peer):
        out.append(pltpu.make_async_remote_copy(
            src_ref=src, dst_ref=dst, send_sem=send_sems.at[n], recv_sem=recv_sems.at[n],
            device_id=peer, device_id_type=pl.DeviceIdType.MESH))

    per_item = _PLANS[kind][1]
    sibling = _peer(_SIBLING, mx, my, mc)[0]
    for i, refs in enumerate(items):
        n = i * per_item
        if kind == "gather":
            for k in range(1, N_DEV):
                add(n + k - 1, refs[0], refs[1].at[me], _peer(k, mx, my, mc)[0])
        elif kind == "to_chips":
            rows = refs[0].shape[0]
            dst = refs[1].at[me] if rows == refs[1].shape[1] else refs[1].at[me, pl.ds(0, rows)]
            for j, k in enumerate((_SIBLING,) + _OTHER_CHIPS):
                add(n + j, refs[0], dst, _peer(k, mx, my, mc)[0])
        elif kind == "pass_on":
            for j, k in enumerate(_OTHER_CHIPS):
                add(n + j, refs[0].at[me ^ k], refs[0].at[me ^ k], sibling)
        elif kind == "halves":
            for chip in range(N_CHIP):
                add(n + chip, refs[0].at[2 * chip + 1 - mc], refs[1].at[chip], sibling)
        else:
            for j, k in enumerate(_OTHER_CHIPS):
                peer, to = _peer(k, mx, my, mc)
                add(n + j, refs[0].at[to // 2], refs[1].at[me // 2], peer)
    return out


_HBM = pl.BlockSpec(memory_space=pltpu.HBM)
_SEM = pl.BlockSpec(memory_space=pltpu.SEMAPHORE)
_DATAFLOW = pltpu.SideEffectType.DATAFLOW_SIDE_EFFECTING


def exchange_call(name, waits, starts, deps=()):
    bufs, slot_of = [], {}

    def slots(items):
        out = []
        for item in items:
            for b in item:
                if id(b) not in slot_of:
                    slot_of[id(b)] = len(bufs)
                    bufs.append(b)
            out.append(tuple(slot_of[id(b)] for b in item))
        return out

    wait_plan = [(kind, slots(handle[0])) for kind, handle in waits]
    start_plan = [(kind, slots(items)) for kind, items in starts]
    wait_sems = [s for _, handle in waits for s in handle[1:]]
    n_buf, n_ws, n_start = len(bufs), len(wait_sems), len(starts)

    def body(*refs):
        buf_refs, sems_in = refs[:n_buf], refs[n_buf:n_buf + n_ws]
        outs = refs[n_buf + n_ws + len(deps):]
        pick = lambda plan: [tuple(buf_refs[s] for s in item) for item in plan]
        for wi, (kind, plan) in enumerate(wait_plan):
            copies = _copies(kind, pick(plan), sems_in[2 * wi], sems_in[2 * wi + 1])
            for cp in copies:
                cp.wait_send()
            for cp in copies:
                cp.wait_recv()
        for si, (kind, plan) in enumerate(start_plan):
            for cp in _copies(kind, pick(plan), outs[2 * si], outs[2 * si + 1]):
                cp.start()
        outs[-1][...] = jnp.zeros_like(outs[-1])

    hbm_bufs = [pltpu.with_memory_space_constraint(b, pltpu.HBM) for b in bufs]
    sem_shapes = []
    for kind, plan in start_plan:
        sem_shapes += [pltpu.SemaphoreType.DMA((len(plan) * _PLANS[kind][1],))] * 2
    outs = pl.pallas_call(
        body, name=name,
        out_shape=sem_shapes + [pltpu.HBM(b.shape, b.dtype) for b in bufs] + [jax.ShapeDtypeStruct((8, LANE), F32)],
        in_specs=[_HBM] * n_buf + [_SEM] * n_ws + [_ANY] * len(deps),
        out_specs=[_SEM] * (2 * n_start) + [_HBM] * n_buf + [pl.BlockSpec(memory_space=pltpu.VMEM)],
        input_output_aliases={i: 2 * n_start + i for i in range(n_buf)},
        compiler_params=pltpu.CompilerParams(has_side_effects=_DATAFLOW))(*hbm_bufs, *wait_sems, *deps)
    sems, thru, token = outs[:2 * n_start], outs[2 * n_start:-1], outs[-1]
    through = lambda plan: [tuple(thru[s] for s in item) for item in plan]
    waited = [through(plan) for _, plan in wait_plan]
    handles = [(through(plan), sems[2 * si], sems[2 * si + 1]) for si, (_, plan) in enumerate(start_plan)]
    return waited, handles, token


def gather_small(name, a, deps=()):
    def body(a_ref, *rest):
        o_ref, send_sems, recv_sems, local_sem = rest[len(deps):]
        me = _mesh_pos()[3]
        own = pltpu.make_async_copy(a_ref, o_ref.at[me], local_sem)
        own.start()
        copies = _copies("gather", [(a_ref, o_ref)], send_sems, recv_sems)
        for cp in copies:
            cp.start()
        for cp in copies:
            cp.wait_recv()
        for cp in copies:
            cp.wait_send()
        own.wait()

    return pl.pallas_call(
        body, name=name, in_specs=[_ANY] * (1 + len(deps)), out_specs=_ANY,
        out_shape=jax.ShapeDtypeStruct((N_DEV,) + a.shape, a.dtype),
        scratch_shapes=[pltpu.SemaphoreType.DMA((N_DEV - 1,)), pltpu.SemaphoreType.DMA((N_DEV - 1,)),
                        pltpu.SemaphoreType.DMA],
        compiler_params=pltpu.CompilerParams(has_side_effects=True))(a, *deps)


def _adamw_math(w, g, m, v):
    m2 = ADAM_B1 * m + (1.0 - ADAM_B1) * g
    v2 = ADAM_B2 * v + (1.0 - ADAM_B2) * (g * g)
    m_hat = m2 / (1.0 - ADAM_B1 ** ADAM_STEP)
    v_hat = v2 / (1.0 - ADAM_B2 ** ADAM_STEP)
    delta = -ADAM_LR * (m_hat / (jnp.sqrt(v_hat) + ADAM_EPS) + ADAM_WD * w)
    return delta, m2, v2


def chip_sum(name, a, half):
    _, r, c = a.shape
    tr = r
    chip = 2 * lax.axis_index("x") + lax.axis_index("y")
    where = jnp.stack([lax.axis_index("c"), chip ^ 1, chip ^ 2, chip ^ 3]).astype(jnp.int32)

    def body(where_ref, a_ref, h_ref, o_ref):
        del where_ref
        o_ref[...] = (a_ref[...].astype(F32) + h_ref[...].astype(F32)).astype(o_ref.dtype)

    blk = pl.BlockSpec((None, tr, c), lambda g, i, where: (where[1 + g], i, 0))
    grid_spec = pltpu.PrefetchScalarGridSpec(
        num_scalar_prefetch=1, grid=(N_CHIP - 1, r // tr),
        in_specs=[pl.BlockSpec((None, None, tr, c), lambda g, i, where: (where[1 + g], where[0], i, 0)), blk],
        out_specs=blk)
    return pl.pallas_call(
        body, name=name, grid_spec=grid_spec, out_shape=jax.ShapeDtypeStruct((N_CHIP, r, c), BF16),
        compiler_params=_params("parallel", "parallel"))(where, a.reshape(N_CHIP, 2, r, c), half)


def adamw(name, w, m, v, l, land, a, half, prev=None):
    L, r, c = w.shape
    cp = land.shape[2]
    tr = _pick(r, (256, 176, 128, 64, 32, 16, 8))

    def body(w_ref, m_ref, v_ref, land_ref, a_ref, half_ref, *rest):
        g_ref, d_ref, m2_ref, v2_ref = rest[-4:]
        chip = _mesh_pos()[3] // 2
        mine = a_ref[:, pl.ds(0, c)].astype(F32) + half_ref[:, pl.ds(0, c)].astype(F32)
        g = None
        for s in range(N_CHIP):
            part = jnp.where(chip == s, mine, land_ref[s, :, pl.ds(0, c)].astype(F32))
            g = part if g is None else g + part
        delta, m2, v2 = _adamw_math(w_ref[...], g, m_ref[...], v_ref[...])
        g_ref[...] = g
        d_ref[...] = delta
        m2_ref[...] = m2
        v2_ref[...] = v2

    blk = pl.BlockSpec((None, tr, c), lambda i: (l, i, 0))
    shape = jax.ShapeDtypeStruct((L, r, c), F32)
    extra = [] if prev is None else list(prev)
    return pl.pallas_call(
        body, name=name, grid=(r // tr,),
        in_specs=[blk, blk, blk, pl.BlockSpec((N_CHIP, tr, cp), lambda i: (0, i, 0)),
                  pl.BlockSpec((None, tr, cp), lambda i: (_mesh_pos()[3], i, 0)),
                  pl.BlockSpec((None, tr, cp), lambda i: (_mesh_pos()[3] // 2, i, 0))] + [_ANY] * len(extra),
        out_specs=[blk] * 4, out_shape=[shape] * 4,
        input_output_aliases={6 + k: k for k in range(len(extra))},
        compiler_params=_params("parallel"))(w, m, v, land, a, half, *extra)


def adamw_columns(name, w, m, v, land, a, half):
    r, _, D = w.shape
    tc = _pick(D, (256, 128))

    def body(w_ref, m_ref, v_ref, land_ref, a_ref, half_ref, g_ref, d_ref, m2_ref, v2_ref):
        chip = _mesh_pos()[3] // 2
        mine = a_ref[...].astype(F32) + half_ref[...].astype(F32)
        g = None
        for s in range(N_CHIP):
            part = jnp.where(chip == s, mine, land_ref[s].astype(F32))
            g = part if g is None else g + part
        flat = lambda ref: ref[...].reshape(r, tc)
        delta, m2, v2 = _adamw_math(flat(w_ref), g, flat(m_ref), flat(v_ref))
        for ref, val in ((g_ref, g), (d_ref, delta), (m2_ref, m2), (v2_ref, v2)):
            ref[...] = val.reshape(r, 1, tc)

    blk = pl.BlockSpec((r, 1, tc), lambda i: (0, 0, i))
    shape = jax.ShapeDtypeStruct((r, 1, D), F32)
    return pl.pallas_call(
        body, name=name, grid=(D // tc,),
        in_specs=[blk, blk, blk, pl.BlockSpec((N_CHIP, r, tc), lambda i: (0, 0, i)),
                  pl.BlockSpec((None, r, tc), lambda i: (_mesh_pos()[3], 0, i)),
                  pl.BlockSpec((None, r, tc), lambda i: (_mesh_pos()[3] // 2, 0, i))],
        out_specs=[blk] * 4, out_shape=[shape] * 4,
        compiler_params=_params("parallel"))(w, m, v, land, a, half)


def adamw_small(name, w, m, v, parts):
    n = w.shape[1]

    def body(w_ref, m_ref, v_ref, p_ref, g_ref, d_ref, m2_ref, v2_ref):
        g = p_ref[0:1, :]
        for s in range(1, N_DEV):
            g = g + p_ref[s:s + 1, :]
        delta, m2, v2 = _adamw_math(w_ref[...], g, m_ref[...], v_ref[...])
        g_ref[...] = g
        d_ref[...] = delta
        m2_ref[...] = m2
        v2_ref[...] = v2

    shape = jax.ShapeDtypeStruct((1, n), F32)
    return pl.pallas_call(body, name=name, out_shape=[shape] * 4,
                          compiler_params=pltpu.CompilerParams(vmem_limit_bytes=VMEM_LIMIT_BYTES))(w, m, v, parts)


def _rope_tables(positions):
    half = RET_DK // 2
    inv_freq = 1.0 / jnp.power(RET_THETA_BASE, jnp.linspace(0.0, 1.0, half, dtype=F32))
    ang = positions.astype(F32)[:, None] * inv_freq
    cos, sin = jnp.cos(ang), jnp.sin(ang)
    cosf = jnp.repeat(cos, 2, axis=-1)
    sins = jnp.stack([-sin, sin], axis=-1).reshape(cosf.shape)
    return cosf, sins


def _pad_to(a, axis, size):
    pad = [(0, 0)] * a.ndim
    pad[axis] = (0, size - a.shape[axis])
    return jnp.pad(a, pad)


def _round_up(n, m):
    return -(-n // m) * m


def kernel(x, p, positions, attn_norm_w, ffn_norm_w, ple_norm_w, final_norm_w, ab_w_in, ab_gla_gate_up, ab_gla_gate_b, ab_ret_norm_w, ab_gla_norm_w, ab_w_out, c_w_qkv, c_w_out, ffn_w_gate, ffn_w_up, ffn_w_down, ple_w_proj, ple_w_gate, loss_target, m_attn_norm_w, m_ffn_norm_w, m_ple_norm_w, m_final_norm_w, m_ab_w_in, m_ab_gla_gate_up, m_ab_gla_gate_b, m_ab_ret_norm_w, m_ab_gla_norm_w, m_ab_w_out, m_c_w_qkv, m_c_w_out, m_ffn_w_gate, m_ffn_w_up, m_ffn_w_down, m_ple_w_proj, m_ple_w_gate, v_attn_norm_w, v_ffn_norm_w, v_ple_norm_w, v_final_norm_w, v_ab_w_in, v_ab_gla_gate_up, v_ab_gla_gate_b, v_ab_ret_norm_w, v_ab_gla_norm_w, v_ab_w_out, v_c_w_qkv, v_c_w_out, v_ffn_w_gate, v_ffn_w_up, v_ffn_w_down, v_ple_w_proj, v_ple_w_gate):
    T, D = x.shape[1], x.shape[2]
    depth = attn_norm_w.shape[0]
    assert ab_w_in.shape[0] == 1 and c_w_qkv.shape[0] == 1 and depth == 2, "one even and one odd layer"
    me = 4 * lax.axis_index("x") + 2 * lax.axis_index("y") + lax.axis_index("c")
    in_shard = ab_w_in.shape[2]
    in_width = in_shard * N_DEV
    assert in_width == OFF_LR + GLA_GATE_RANK
    fs = ffn_w_gate.shape[2]
    fp = _round_up(fs, LANE)
    gu_cols = ab_gla_gate_up.shape[2]

    bf = lambda a: a.astype(BF16)
    tr_ = lambda a: jnp.swapaxes(a, -1, -2)
    wg_t, wu_t = tr_(ffn_w_gate), tr_(ffn_w_up)
    srcs = {"w_in": bf(tr_(ab_w_in[0]))}
    group_keys = [["w_in"], ["gu", "w_oab"], ["wg0", "wu0"], ["wd0", "wpg0", "wpp0"], ["w_qkv", "w_oc"],
                  ["wg1", "wu1"], ["wd1", "wpg1", "wpp1"]]
    G_IN, G_OUT, G_QKV = 0, 1, 4
    g_ffn = lambda layer: (2, 3) if layer == 0 else (5, 6)

    def landing(key):
        a = srcs[key]
        rows = fp if key[:2] in ("wg", "wu", "wd") else a.shape[0]
        buf = lax.empty((N_DEV, rows) + a.shape[1:], a.dtype)
        if rows > a.shape[0]:
            zeros = jnp.zeros((N_DEV, rows - a.shape[0]) + a.shape[1:], a.dtype)
            buf = lax.dynamic_update_slice(buf, zeros, (0, a.shape[0]) + (0,) * (a.ndim - 1))
        return lax.dynamic_update_slice(buf, a[None], (me,) + (0,) * a.ndim)

    _, chip_handles, gather_token = exchange_call(
        "gather_start_in", [], [("to_chips", [(srcs[k], landing(k)) for k in group_keys[G_IN]])])
    (gather_token, w_out_, gu_, w_qkv_, w_oc_, wg_, wu_, wd_, wpg_, wpp_) = lax.optimization_barrier(
        (gather_token, ab_w_out, ab_gla_gate_up, c_w_qkv, c_w_out, wg_t, wu_t, ffn_w_down, ple_w_gate, ple_w_proj))
    srcs.update(w_oab=bf(w_out_[0]), gu=gu_[0], w_qkv=bf(w_qkv_[0]), w_oc=bf(w_oc_[0]))
    for l in range(depth):
        srcs[f"wg{l}"] = bf(wg_[l])
        srcs[f"wu{l}"] = bf(wu_[l])
        srcs[f"wd{l}"] = bf(wd_[l])
        srcs[f"wpg{l}"] = bf(wpg_[l])
        srcs[f"wpp{l}"] = bf(wpp_[l])
    _, more, gather_token = exchange_call(
        "gather_start", [], [("to_chips", [(srcs[k], landing(k)) for k in keys]) for keys in group_keys[1:]],
        deps=(gather_token,))
    chip_handles = chip_handles + more
    weights = {}

    def gather_wait(gi, dep):
        lands = [(land,) for _, land in chip_handles[gi][0]]
        _, (passing,), _ = exchange_call(
            f"gather{gi}_pass", [("to_chips", chip_handles[gi])], [("pass_on", lands)], deps=(dep,))
        (complete,), _, _ = exchange_call(f"gather{gi}_done", [("pass_on", passing)], [])
        weights.update(zip(group_keys[gi], [land for (land,) in complete]))

    gb = ab_gla_gate_b
    hn_w = jnp.concatenate([ab_ret_norm_w, ab_gla_norm_w], axis=1)
    cosf, sins = _rope_tables(positions[0])
    p_bf = bf(p[:, 0])

    xs = x[0]
    saved = []
    for i in range(depth):
        nm = f"l{i}_"
        w_attn, w_ffn, w_ple = attn_norm_w[i:i + 1], ffn_norm_w[i:i + 1], ple_norm_w[i:i + 1]
        (xn,) = rowwise(nm + "norm_attn", lambda a, w: (_rms(a, w),), T, [("row", xs), ("full", w_attn)],
                        [("row", D, BF16)], deps=(gather_token,) if i == 0 else ())
        if i % 2 == 0:
            gather_wait(G_IN, xn)
            w_in_t = weights["w_in"].reshape(1, 1, in_width, D)
            w_lr_t = _pad_to(w_in_t[0, 0, OFF_LR:], 0, LANE).reshape(1, 1, LANE, D)
            z = mmt_fwd(nm + "mm_in", xn, w_in_t, 0, F32, n=OFF_LR)
            glr = mmt_fwd(nm + "mm_lr", xn, w_lr_t, 0, F32)
            oraw = retention_fwd(nm + "ret_fwd", z, cosf, sins, RET_V + GLA_V)
            gather_wait(G_OUT, oraw)
            w_oab = weights["w_oab"].reshape(1, 1, D, D)
            gu_full = _pad_to(weights["gu"].transpose(1, 0, 2).reshape(GLA_GATE_RANK, GLA_QK), 0, LANE)
            oraw = gla_fwd(nm + "gla_fwd", z, glr, gu_full, gb, oraw)
            o = headnorm_fwd(nm + "headnorm_fwd", oraw, z, hn_w)
            h1, hn = mm_add_norm(nm + "mm_out", o, w_oab, xs, w_ffn)
            mixer_saved = (z, glr, oraw, o)
        else:
            gather_wait(G_QKV, xn)
            w_qkv = weights["w_qkv"].reshape((1,) + weights["w_qkv"].shape)
            w_oc = weights["w_oc"].reshape(1, 1, D, D)
            qkv = mm_nn(nm + "mm_qkv", xn, w_qkv, 0, BF16)
            o, lse = attn_fwd(nm + "attn_fwd", qkv)
            h1, hn = mm_add_norm(nm + "mm_out", o, w_oc, xs, w_ffn)
            mixer_saved = (qkv, o, lse)
        gather_wait(g_ffn(i)[0], hn)
        wg = weights[f"wg{i}"].reshape(1, N_DEV, fp, D)
        wu = weights[f"wu{i}"].reshape(1, N_DEV, fp, D)
        g, u, act = ffn_gate_up(nm + "ffn_gate_up", hn, wg, wu)
        gather_wait(g_ffn(i)[1], act)
        wd = weights[f"wd{i}"].reshape(1, 1, N_DEV * fp, D)
        wpg = weights[f"wpg{i}"].reshape(1, 1, D, D)
        wpp = weights[f"wpp{i}"].reshape((1,) + weights[f"wpp{i}"].shape)
        h2, pn = mm_add_norm(nm + "mm_down", act, wd, h1, w_ple)
        x_next, s, e = ple_fwd(nm + "ple", pn, wpg, p_bf[i], wpp, h2)
        mixer_w = (w_in_t, w_lr_t, w_oab, gu_full) if i % 2 == 0 else (w_qkv, w_oc)
        saved.append((xs, xn, mixer_saved, mixer_w, (wg, wu, wd, wpg), h1, hn, g, u, act, h2, pn, s, e))
        xs = x_next

    def loss_fn(a, w, t):
        diff = _rms(a, w) - t
        dx, dw = _rms_bwd(a, w, diff * (1.0 / D))
        part = 0.5 * jnp.sum(jnp.mean(diff * diff, axis=-1, keepdims=True), axis=0, keepdims=True)
        return dx, dw, jnp.broadcast_to(part, (1, LANE))

    dx, d_final_w, loss_part = rowwise("loss_head", loss_fn, T,
                                       [("row", xs), ("full", final_norm_w[None, :]), ("row", loss_target[0])],
                                       [("row", D, F32), ("acc", D), ("acc", LANE)])
    loss = lax.psum(loss_part[0, 0], ("x", "y", "c"))

    grads = {}
    on_chip = []
    scatters = []

    def scatter_start(name, keys, deps=()):
        waits = [("halves", on_chip[0][1])] if on_chip else []
        starts = [("halves", [(grads[k], lax.empty((N_CHIP,) + grads[k].shape[1:], BF16)) for k in keys])] if keys else []
        waited, handles, token = exchange_call(name, waits, starts, deps=deps)
        if on_chip:
            done_keys, _ = on_chip.pop()
            sums = [chip_sum(f"{name}_sum{j}", a, half) for j, (a, half) in enumerate(waited[0])]
            _, (handle,), token = exchange_call(
                name + "_chips", [], [("chip_sums", [(cs, lax.empty(cs.shape, BF16)) for cs in sums])])
            scatters.append((done_keys, handle, waited[0]))
        if keys:
            on_chip.append((keys, handles[0]))
        return token

    d_attn_w, d_ffn_w, d_ple_w = [None] * depth, [None] * depth, [None] * depth
    for i in reversed(range(depth)):
        nm = f"l{i}_b_"
        xs_i, xn, mixer_saved, mixer_w, (wg, wu, wd, wpg), h1, hn, g, u, act, h2, pn, s, e = saved[i]
        w_attn, w_ffn, w_ple = attn_norm_w[i:i + 1], ffn_norm_w[i:i + 1], ple_norm_w[i:i + 1]

        def ple_bwd(d, sv, ev):
            gate = _sigmoid(sv)
            return d * gate, d * ev * gate * (1.0 - gate)

        de, ds = rowwise(nm + "ple_out", ple_bwd, T, [("row", dx), ("row", s), ("row", e)],
                         [("row", D, BF16), ("row", D, BF16)], deps=(loss.reshape(1, 1),) if i == depth - 1 else ())
        grads[("ple_w_proj", i)] = mm_tn(nm + "mm_ple_proj_w", p_bf[i], de, N_DEV, BF16)
        grads[("ple_w_gate", i)] = mm_tn(nm + "mm_ple_gate_w", pn, ds, 1, BF16).reshape(N_DEV, D // N_DEV, D)
        dpn = mm_nt(nm + "mm_ple_gate_x", ds, wpg, 0, F32)

        def norm_bwd_add(a, w, dn, dres):
            dxx, dw = _rms_bwd(a, w, dn)
            tot = dres + dxx
            return tot, tot, dw

        dh2, dh2_bf, d_ple_w[i] = rowwise(nm + "norm_ple", norm_bwd_add, T,
                                          [("row", h2), ("full", w_ple), ("row", dpn), ("row", dx)],
                                          [("row", D, F32), ("row", D, BF16), ("acc", D)])
        grads[("ffn_w_down", i)] = mm_tn(nm + "mm_down_w", act, dh2_bf, 1, BF16).reshape(N_DEV, fp, D)
        token = scatter_start(nm + "scatter_ple_down", [("ple_w_proj", i), ("ple_w_gate", i), ("ffn_w_down", i)])
        dg, du = ffn_down_bwd(nm + "ffn_down_x", dh2_bf, wd, g, u, deps=(token,))
        grads[("ffn_w_gate", i)] = mmt_dw(nm + "mm_gate_w", dg, hn, N_DEV, BF16)
        grads[("ffn_w_up", i)] = mmt_dw(nm + "mm_up_w", du, hn, N_DEV, BF16)
        token = scatter_start(nm + "scatter_gate_up", [("ffn_w_gate", i), ("ffn_w_up", i)])
        dhn_g = mmt_dx_wide(nm + "mm_gate_x", dg, wg, F32, deps=(token,))
        dhn_u = mmt_dx_wide(nm + "mm_up_x", du, wu, F32)

        def norm_bwd_add2(a, w, dn1, dn2, dres):
            dxx, dw = _rms_bwd(a, w, dn1 + dn2)
            tot = dres + dxx
            return tot, tot, dw

        dh1, dh1_bf, d_ffn_w[i] = rowwise(nm + "norm_ffn", norm_bwd_add2, T,
                                          [("row", h1), ("full", w_ffn), ("row", dhn_g), ("row", dhn_u), ("row", dh2)],
                                          [("row", D, F32), ("row", D, BF16), ("acc", D)])
        if i % 2 == 0:
            z, glr, oraw, o = mixer_saved
            w_in_t, w_lr_t, w_oab, gu_full = mixer_w
            grads[("ab_w_out", 0)] = mm_tn(nm + "mm_out_w", o, dh1_bf, 1, BF16).reshape(N_DEV, D // N_DEV, D)
            token = scatter_start(nm + "scatter_out", [("ab_w_out", 0)])
            do = mm_nt(nm + "mm_out_x", dh1_bf, w_oab, 0, F32, deps=(token,))
            d_oraw, d_gates, d_hn_w = headnorm_bwd(nm + "headnorm", oraw, z, hn_w, do)
            d_rq, d_rk, d_rv = retention_bwd(nm + "ret", z, cosf, sins, d_oraw)
            d_gq, d_gk, d_gv, d_glr4, d_gu, d_gb = gla_bwd(nm + "gla", z, glr, gu_full, gb, d_oraw)
            dz = jnp.concatenate([d_rq, d_rk, d_rv, d_gates[:, :RET_V], d_gq, d_gk, d_gv, d_gates[:, RET_V:]], axis=1)
            (d_glr,) = rowwise(nm + "sum_lr", lambda *a: (a[0] + a[1] + a[2] + a[3],), T,
                               [("row", d_glr4[hh]) for hh in range(GLA_HEADS)], [("row", LANE, BF16)])
            dwt_main = mmt_dw(nm + "mm_in_w", dz, xn, 1, BF16)[0]
            dwt_lr = mmt_dw(nm + "mm_lr_w", d_glr, xn, 1, BF16)[0]
            dwt_in = jnp.concatenate([dwt_main, dwt_lr[:GLA_GATE_RANK]], axis=0)
            grads[("ab_w_in", 0)] = dwt_in.reshape(N_DEV, in_shard, D)
            token = scatter_start(nm + "scatter_in", [("ab_w_in", 0)])
            dxn_a = mmt_dx_wide(nm + "mm_in_x", dz, w_in_t, F32, n=OFF_LR, deps=(token,))
            token = scatter_start(nm + "scatter_in_on", [], deps=(dxn_a,))
            dxn_b = mmt_dx(nm + "mm_lr_x", d_glr, w_lr_t, 0, F32, deps=(token,))
        else:
            qkv, o, lse = mixer_saved
            w_qkv, w_oc = mixer_w
            grads[("c_w_out", 0)] = mm_tn(nm + "mm_out_w", o, dh1_bf, 1, BF16).reshape(N_DEV, D // N_DEV, D)
            do = mm_nt(nm + "mm_out_x", dh1_bf, w_oc, 0, BF16)
            dq, dk, dv = attn_bwd(nm + "attn", qkv, o, lse, do)
            dqkv = jnp.concatenate([dq, dk, dv], axis=1)
            grads[("c_w_qkv", 0)] = mm_tn(nm + "mm_qkv_w", xn, dqkv, N_DEV, BF16)
            token = scatter_start(nm + "scatter_attn", [("c_w_out", 0), ("c_w_qkv", 0)])
            dxn_a = mm_nt_wide(nm + "mm_qkv_x", dqkv, w_qkv, F32, deps=(token,))
            dxn_b = None
        if dxn_b is None:
            dx, _, d_attn_w[i] = rowwise(nm + "norm_attn", norm_bwd_add, T,
                                         [("row", xs_i), ("full", w_attn), ("row", dxn_a), ("row", dh1)],
                                         [("row", D, F32), ("row", D, BF16), ("acc", D)])
        else:
            dx, _, d_attn_w[i] = rowwise(nm + "norm_attn", norm_bwd_add2, T,
                                         [("row", xs_i), ("full", w_attn), ("row", dxn_a), ("row", dxn_b), ("row", dh1)],
                                         [("row", D, F32), ("row", D, BF16), ("acc", D)])

    small_names = ["attn_norm_w", "ffn_norm_w", "ple_norm_w", "final_norm_w", "ab_gla_gate_b", "ab_ret_norm_w",
                   "ab_gla_norm_w"]
    small_grads = [jnp.concatenate(d_attn_w, 0), jnp.concatenate(d_ffn_w, 0), jnp.concatenate(d_ple_w, 0), d_final_w[0],
                   d_gb, d_hn_w[:, :RET_V], d_hn_w[:, RET_V:]]
    small_w = [attn_norm_w, ffn_norm_w, ple_norm_w, final_norm_w, ab_gla_gate_b, ab_ret_norm_w, ab_gla_norm_w]
    small_m = [m_attn_norm_w, m_ffn_norm_w, m_ple_norm_w, m_final_norm_w, m_ab_gla_gate_b, m_ab_ret_norm_w, m_ab_gla_norm_w]
    small_v = [v_attn_norm_w, v_ffn_norm_w, v_ple_norm_w, v_final_norm_w, v_ab_gla_gate_b, v_ab_ret_norm_w, v_ab_gla_norm_w]
    sizes = [int(np.prod(a.shape)) for a in small_w]
    n_gu = GLA_GATE_RANK * GLA_QK
    n_small = _round_up(sum(sizes) + n_gu, LANE)
    pack = lambda parts: _pad_to(jnp.concatenate([a.reshape(-1) for a in parts]), 0, n_small)[None, :]
    small_part = pack(small_grads + [d_gu[:GLA_GATE_RANK]])

    cols_first = lambda a: jnp.transpose(a, (2, 0, 1))
    big_w = dict(ab_w_in=tuple(cols_first(a) for a in (ab_w_in, m_ab_w_in, v_ab_w_in)),
                 ab_w_out=(ab_w_out, m_ab_w_out, v_ab_w_out),
                 c_w_qkv=(c_w_qkv, m_c_w_qkv, v_c_w_qkv), c_w_out=(c_w_out, m_c_w_out, v_c_w_out),
                 ffn_w_gate=(wg_t, tr_(m_ffn_w_gate), tr_(v_ffn_w_gate)),
                 ffn_w_up=(wu_t, tr_(m_ffn_w_up), tr_(v_ffn_w_up)),
                 ffn_w_down=(ffn_w_down, m_ffn_w_down, v_ffn_w_down), ple_w_proj=(ple_w_proj, m_ple_w_proj, v_ple_w_proj),
                 ple_w_gate=(ple_w_gate, m_ple_w_gate, v_ple_w_gate))
    if on_chip:
        scatter_start("scatter_last", [], deps=(dx,))
    results, last = {}, dx
    for gi, (keys, handle, partials) in enumerate(scatters):
        (arrived,), _, _ = exchange_call(f"scatter_wait{gi}", [("chip_sums", handle)], [], deps=(last,))
        for (n, l), (_, land), (a, half) in zip(keys, arrived, partials):
            if n == "ab_w_in":
                results[n] = adamw_columns(f"adamw_{n}", *big_w[n], land, a, half)
            else:
                results[n] = adamw(f"adamw_{n}{l}", *big_w[n], l, land, a, half, prev=results.get(n))
            last = results[n][0]
    for n in ("ffn_w_gate", "ffn_w_up"):
        results[n] = [tr_(a) for a in results[n]]
    results["ab_w_in"] = [jnp.transpose(a, (1, 2, 0)) for a in results["ab_w_in"]]
    small_parts = gather_small("gather_small", small_part, deps=(last,)).reshape(N_DEV, n_small)

    gu_off = sum(sizes)
    own_cols = lambda a: lax.dynamic_slice_in_dim(a.reshape(GLA_GATE_RANK, GLA_QK), me * gu_cols, gu_cols, axis=1)
    small_res = adamw_small("adamw_small", pack(small_w + [jnp.zeros((n_gu,), F32)]),
                            pack(small_m + [jnp.zeros((n_gu,), F32)]), pack(small_v + [jnp.ones((n_gu,), F32)]),
                            small_parts)
    g_gu_full = small_res[0][0, gu_off:gu_off + n_gu]
    g_gu = own_cols(g_gu_full)[None]
    gu_res = adamw_small("adamw_gate_up", *[_pad_to(a.reshape(1, -1), 1, _round_up(a.size, LANE)) for a in
                                            (ab_gla_gate_up, m_ab_gla_gate_up, v_ab_gla_gate_up)],
                         jnp.concatenate([_pad_to(g_gu.reshape(1, -1), 1, _round_up(g_gu.size, LANE)),
                                          jnp.zeros((N_DEV - 1, _round_up(g_gu.size, LANE)), F32)], axis=0))
    for k in range(4):
        off = 0
        for n, a, sz in zip(small_names, small_w, sizes):
            results.setdefault(n, [None] * 4)[k] = small_res[k][0, off:off + sz].reshape(a.shape)
            off += sz
        results.setdefault("ab_gla_gate_up", [None] * 4)[k] = gu_res[k][0, :g_gu.size].reshape(ab_gla_gate_up.shape)

    order = ["attn_norm_w", "ffn_norm_w", "ple_norm_w", "final_norm_w", "ab_w_in", "ab_gla_gate_up", "ab_gla_gate_b",
             "ab_ret_norm_w", "ab_gla_norm_w", "ab_w_out", "c_w_qkv", "c_w_out", "ffn_w_gate", "ffn_w_up", "ffn_w_down",
             "ple_w_proj", "ple_w_gate"]
    return (loss, dx[None], *[results[n][0] for n in order], *[results[n][1] for n in order],
            *[results[n][2] for n in order], *[results[n][3] for n in order])
```

```python
import math

import numpy as np
import jax
import jax.numpy as jnp
from jax import lax
from jax.experimental import pallas as pl
from jax.experimental.pallas import tpu as pltpu

F32 = jnp.float32
BF16 = jnp.bfloat16
HIGHEST = lax.Precision.HIGHEST

N_DEV = 8
VMEM_LIMIT_BYTES = 48 * 1024 * 1024
LANE = 128
NORM_EPS = 1e-6

RET_HEADS, RET_DK, RET_DV = 4, 256, 256
RET_THETA_BASE = 10000.0
GLA_HEADS, GLA_DK, GLA_DV = 4, 128, 256
GLA_GATE_RANK = 16
GLA_GATE_NORM = 16.0
CHUNK = 64
ATT_HEADS = 16
DILATED_BRANCHES = ((128, 1), (512, 4), (2048, 16))
BLK = 256

ADAM_LR, ADAM_B1, ADAM_B2, ADAM_EPS, ADAM_WD, ADAM_STEP = 0.001, 0.9, 0.999, 1e-08, 0.01, 10

RET_QK = RET_HEADS * RET_DK
RET_V = RET_HEADS * RET_DV
GLA_QK = GLA_HEADS * GLA_DK
GLA_V = GLA_HEADS * GLA_DV
OFF_RQ, OFF_RK, OFF_RV, OFF_RG = 0, RET_QK, 2 * RET_QK, 2 * RET_QK + RET_V
OFF_GQ = OFF_RG + RET_V
OFF_GK = OFF_GQ + GLA_QK
OFF_GV = OFF_GK + GLA_QK
OFF_GG = OFF_GV + GLA_V
OFF_LR = OFF_GG + GLA_V


def _params(*sem):
    return pltpu.CompilerParams(dimension_semantics=sem or None, vmem_limit_bytes=VMEM_LIMIT_BYTES)


def _pick(n, cands):
    for c in cands:
        if n % c == 0:
            return c
    raise ValueError(f"no tile for {n} in {cands}")


_NN = (((1,), (0,)), ((), ()))
_NT = (((1,), (1,)), ((), ()))
_TN = (((0,), (0,)), ((), ()))
_ANY = pl.BlockSpec(memory_space=pl.ANY)
MAX_CONTRACT = 2048
_TILES = (1024, 768, 512, 256, 128)


def _mm_call(name, dims, grid, in_specs, out_spec, out_shape, args, deps=()):
    steps = grid[2]
    assert steps == 1 or out_shape.dtype == F32

    def body(a_ref, b_ref, *rest):
        o_ref = rest[len(deps)]
        part = lax.dot_general(a_ref[...].astype(BF16), b_ref[...].astype(BF16), dims, preferred_element_type=F32)
        if steps == 1:
            o_ref[...] = part.astype(o_ref.dtype)
        else:
            _accumulate(o_ref, part, pl.program_id(2) == 0)

    return pl.pallas_call(
        body, name=name, grid=grid, in_specs=list(in_specs) + [_ANY] * len(deps), out_specs=out_spec,
        out_shape=out_shape, compiler_params=_params("parallel", "parallel", "arbitrary"))(*args, *deps)


def mm_nn(name, a, w, l, out_dtype, deps=()):
    _, J, K, n = w.shape
    M = a.shape[0]
    tm, tn, tk = _pick(M, _TILES), _pick(n, _TILES), _pick(K, (MAX_CONTRACT,) + _TILES)
    nt = n // tn
    return _mm_call(
        name, _NN, (M // tm, J * nt, K // tk),
        [pl.BlockSpec((tm, tk), lambda i, j, k: (i, k)),
         pl.BlockSpec((None, None, tk, tn), lambda i, j, k: (l, j // nt, k, j % nt))],
        pl.BlockSpec((tm, tn), lambda i, j, k: (i, j)),
        jax.ShapeDtypeStruct((M, J * n), out_dtype), (a, w), deps)


def mm_nt(name, a, w, l, out_dtype, deps=()):
    _, J, K, n = w.shape
    M = a.shape[0]
    tm, tq, tc = _pick(M, _TILES), _pick(K, _TILES), _pick(n, (MAX_CONTRACT,) + _TILES)
    nc = n // tc
    return _mm_call(
        name, _NT, (M // tm, K // tq, J * nc),
        [pl.BlockSpec((tm, tc), lambda i, q, c: (i, c)),
         pl.BlockSpec((None, None, tq, tc), lambda i, q, c: (l, c // nc, q, c % nc))],
        pl.BlockSpec((tm, tq), lambda i, q, c: (i, q)),
        jax.ShapeDtypeStruct((M, K), out_dtype), (a, w), deps)


def mm_tn(name, x, dy, J, out_dtype, deps=()):
    M, K = x.shape
    n = dy.shape[1] // J
    tp, tn = _pick(K, _TILES), _pick(n, _TILES)
    nt = n // tn
    assert M <= MAX_CONTRACT
    return _mm_call(
        name, _TN, (K // tp, J * nt, 1),
        [pl.BlockSpec((M, tp), lambda i, j, r: (0, i)),
         pl.BlockSpec((M, tn), lambda i, j, r: (0, j))],
        pl.BlockSpec((None, tp, tn), lambda i, j, r: (j // nt, i, j % nt)),
        jax.ShapeDtypeStruct((J, K, n), out_dtype), (x, dy), deps)


def mmt_fwd(name, a, wt, l, out_dtype, n=None, deps=()):
    _, J, rows, K = wt.shape
    n = rows if n is None else n
    M = a.shape[0]
    tm, tn = _pick(M, _TILES), _pick(n, _TILES)
    nt = n // tn
    assert K <= MAX_CONTRACT
    return _mm_call(
        name, _NT, (M // tm, J * nt, 1),
        [pl.BlockSpec((tm, K), lambda i, j, k: (i, 0)),
         pl.BlockSpec((None, None, tn, K), lambda i, j, k: (l, j // nt, j % nt, 0))],
        pl.BlockSpec((tm, tn), lambda i, j, k: (i, j)),
        jax.ShapeDtypeStruct((M, J * n), out_dtype), (a, wt), deps)


def mmt_dx(name, dy, wt, l, out_dtype, n=None, deps=()):
    _, J, rows, K = wt.shape
    n = rows if n is None else n
    M = dy.shape[0]
    tm, tq, tc = _pick(M, _TILES), _pick(K, _TILES), _pick(n, _TILES)
    nc = n // tc
    return _mm_call(
        name, _NN, (M // tm, K // tq, J * nc),
        [pl.BlockSpec((tm, tc), lambda i, q, c: (i, c)),
         pl.BlockSpec((None, None, tc, tq), lambda i, q, c: (l, c // nc, c % nc, q))],
        pl.BlockSpec((tm, tq), lambda i, q, c: (i, q)),
        jax.ShapeDtypeStruct((M, K), out_dtype), (dy, wt), deps)


WIDE_TILE = 512


def _wide_call(name, body, M, K, a, w, a_spec, w_spec, out_dtype, deps):
    def kernel_body(a_ref, w_ref, *rest):
        o_ref = rest[len(deps)]
        o_ref[...] = body(a_ref, w_ref).astype(o_ref.dtype)

    return pl.pallas_call(
        kernel_body, name=name, grid=(M // WIDE_TILE, K // WIDE_TILE),
        in_specs=[a_spec, w_spec] + [_ANY] * len(deps),
        out_specs=pl.BlockSpec((WIDE_TILE, WIDE_TILE), lambda i, q: (i, q)),
        out_shape=jax.ShapeDtypeStruct((M, K), out_dtype),
        compiler_params=_params("parallel", "parallel"))(a, w, *deps)


def mmt_dx_wide(name, dy, wt, out_dtype, n=None, deps=()):
    _, J, rows, K = wt.shape
    n = rows if n is None else n
    M = dy.shape[0]

    def body(dy_ref, w_ref):
        return jnp.dot(dy_ref[...].astype(BF16), w_ref[...].reshape(J * n, WIDE_TILE), preferred_element_type=F32)

    return _wide_call(name, body, M, K, dy, wt,
                      pl.BlockSpec((WIDE_TILE, J * n), lambda i, q: (i, 0)),
                      pl.BlockSpec((None, J, n, WIDE_TILE), lambda i, q: (0, 0, 0, q)), out_dtype, deps)


def mm_nt_wide(name, a, w, out_dtype, deps=()):
    _, J, K, n = w.shape
    M = a.shape[0]

    def body(a_ref, w_ref):
        acc = None
        for j in range(J):
            part = lax.dot_general(a_ref[:, j * n:(j + 1) * n].astype(BF16), w_ref[j], _NT, preferred_element_type=F32)
            acc = part if acc is None else acc + part
        return acc

    return _wide_call(name, body, M, K, a, w,
                      pl.BlockSpec((WIDE_TILE, J * n), lambda i, q: (i, 0)),
                      pl.BlockSpec((None, J, WIDE_TILE, n), lambda i, q: (0, 0, q, 0)), out_dtype, deps)


def mmt_dw(name, dy, x, J, out_dtype, deps=(), rows=None):
    M, K = x.shape
    n = dy.shape[1] // J
    tn, tp = _pick(n, _TILES), _pick(K, _TILES)
    nt = n // tn
    assert M <= MAX_CONTRACT
    return _mm_call(
        name, _TN, (J * nt, K // tp, 1),
        [pl.BlockSpec((M, tn), lambda j, i, r: (0, j)),
         pl.BlockSpec((M, tp), lambda j, i, r: (0, i))],
        pl.BlockSpec((None, tn, tp), lambda j, i, r: (j // nt, j % nt, i)),
        jax.ShapeDtypeStruct((J, n if rows is None else rows, K), out_dtype), (dy, x), deps)


def mmt_dw_rows(name, dy, x, out, row0, rank):
    M, K = x.shape
    tp = _pick(K, _TILES)

    def body(dy_ref, x_ref, prev_ref, o_ref):
        del prev_ref
        full = lax.dot_general(dy_ref[...], x_ref[...], _TN, preferred_element_type=F32)
        o_ref[...] = full[:rank].astype(o_ref.dtype)

    return pl.pallas_call(
        body, name=name, grid=(K // tp,),
        in_specs=[pl.BlockSpec((M, dy.shape[1]), lambda i: (0, 0)), pl.BlockSpec((M, tp), lambda i: (0, i)), _ANY],
        out_specs=pl.BlockSpec((None, rank, tp), lambda i: (0, row0 // rank, i)),
        out_shape=jax.ShapeDtypeStruct(out.shape, out.dtype), input_output_aliases={2: 0},
        compiler_params=_params("parallel"))(dy, x, out)


def ffn_gate_up(name, a, wg, wu):
    _, J, n, K = wg.shape
    M = a.shape[0]
    tm, tn = _pick(M, _TILES), _pick(n, _TILES)
    nt = n // tn
    assert K <= MAX_CONTRACT

    def body(a_ref, wg_ref, wu_ref, g_ref, u_ref, act_ref):
        x = a_ref[...]
        g = lax.dot_general(x, wg_ref[...], _NT, preferred_element_type=F32)
        u = lax.dot_general(x, wu_ref[...], _NT, preferred_element_type=F32)
        g_ref[...] = g.astype(g_ref.dtype)
        u_ref[...] = u.astype(u_ref.dtype)
        act_ref[...] = (_silu_and_grad(g)[0] * u).astype(act_ref.dtype)

    w_spec = pl.BlockSpec((None, None, tn, K), lambda i, j: (0, j // nt, j % nt, 0))
    out = pl.BlockSpec((tm, tn), lambda i, j: (i, j))
    return pl.pallas_call(
        body, name=name, grid=(M // tm, J * nt),
        in_specs=[pl.BlockSpec((tm, K), lambda i, j: (i, 0)), w_spec, w_spec],
        out_specs=[out] * 3, out_shape=[jax.ShapeDtypeStruct((M, J * n), BF16)] * 3,
        compiler_params=_params("parallel", "parallel"))(a, wg, wu)


def mm_add_norm(name, a, w, res, norm_w):
    _, _, K, N = w.shape
    M = a.shape[0]
    tm, tk = _pick(M, (WIDE_TILE, 256)), _pick(K, (1024, 512, 256))
    steps = K // tk

    def body(a_ref, w_ref, res_ref, nw_ref, h_ref, hn_ref):
        k = pl.program_id(1)
        part = jnp.dot(a_ref[...], w_ref[...], preferred_element_type=F32)
        _accumulate(h_ref, part, k == 0)

        @pl.when(k == steps - 1)
        def _():
            h = h_ref[...] + res_ref[...]
            h_ref[...] = h
            hn_ref[...] = _rms(h, nw_ref[...]).astype(hn_ref.dtype)

    rows = pl.BlockSpec((tm, N), lambda i, k: (i, 0))
    return pl.pallas_call(
        body, name=name, grid=(M // tm, steps),
        in_specs=[pl.BlockSpec((tm, tk), lambda i, k: (i, k)),
                  pl.BlockSpec((None, None, tk, N), lambda i, k: (0, 0, k, 0)), rows,
                  pl.BlockSpec((1, N), lambda i, k: (0, 0))],
        out_specs=[rows, rows],
        out_shape=[jax.ShapeDtypeStruct((M, N), F32), jax.ShapeDtypeStruct((M, N), BF16)],
        compiler_params=_params("parallel", "arbitrary"))(a, w, res, norm_w)


def ple_fwd(name, pn, wpg, p_in, wpp, h):
    _, J, P, n = wpp.shape
    M, D = h.shape
    tm, tn = _pick(M, (WIDE_TILE, 256)), _pick(D, _TILES)
    per_tile = tn // n

    def body(pn_ref, wg_ref, p_ref, wp_ref, h_ref, x_ref, s_ref, e_ref):
        s = jnp.dot(pn_ref[...], wg_ref[...], preferred_element_type=F32)
        p_blk = p_ref[...]
        e = jnp.concatenate([jnp.dot(p_blk, wp_ref[j], preferred_element_type=F32) for j in range(per_tile)], axis=1)
        s_ref[...] = s
        e_ref[...] = e
        x_ref[...] = h_ref[...] + _sigmoid(s) * e

    tile = pl.BlockSpec((tm, tn), lambda i, j: (i, j))
    return pl.pallas_call(
        body, name=name, grid=(M // tm, D // tn),
        in_specs=[pl.BlockSpec((tm, D), lambda i, j: (i, 0)),
                  pl.BlockSpec((None, None, D, tn), lambda i, j: (0, 0, 0, j)),
                  pl.BlockSpec((tm, P), lambda i, j: (i, 0)),
                  pl.BlockSpec((None, per_tile, P, n), lambda i, j: (0, j, 0, 0)), tile],
        out_specs=[tile] * 3, out_shape=[jax.ShapeDtypeStruct((M, D), F32)] * 3,
        compiler_params=_params("parallel", "parallel"))(pn, wpg, p_in, wpp, h)


def ffn_down_bwd(name, dy, wd, g, u, deps=()):
    _, _, K, n = wd.shape
    M = dy.shape[0]
    tm, tq = _pick(M, _TILES), _pick(K, _TILES)
    assert n <= MAX_CONTRACT

    def body(dy_ref, w_ref, g_ref, u_ref, *rest):
        dg_ref, du_ref = rest[len(deps):]
        dact = lax.dot_general(dy_ref[...], w_ref[...], _NT, preferred_element_type=F32)
        silu, dsilu = _silu_and_grad(g_ref[...].astype(F32))
        dg_ref[...] = (dact * u_ref[...].astype(F32) * dsilu).astype(dg_ref.dtype)
        du_ref[...] = (dact * silu).astype(du_ref.dtype)

    blk = pl.BlockSpec((tm, tq), lambda i, q: (i, q))
    return pl.pallas_call(
        body, name=name, grid=(M // tm, K // tq),
        in_specs=[pl.BlockSpec((tm, n), lambda i, q: (i, 0)),
                  pl.BlockSpec((None, None, tq, n), lambda i, q: (0, 0, q, 0)), blk, blk] + [_ANY] * len(deps),
        out_specs=[blk, blk], out_shape=[jax.ShapeDtypeStruct((M, K), BF16)] * 2,
        compiler_params=_params("parallel", "parallel"))(dy, wd, g, u, *deps)


def rowwise(name, fn, rows, ins, outs, tr=256, deps=()):
    widest = max([s[1].shape[1] if s[0] != "col" else s[3] for s in ins] + [s[1] for s in outs])
    tr = min(tr if widest <= 2048 else tr // 2, rows)
    in_specs, args = [], []
    for spec in ins:
        kind, a = spec[0], spec[1]
        if kind == "row":
            in_specs.append(pl.BlockSpec((tr, a.shape[1]), lambda i: (i, 0)))
        elif kind == "col":
            cb, width = spec[2], spec[3]
            in_specs.append(pl.BlockSpec((tr, width), lambda i, cb=cb: (i, cb)))
        else:
            in_specs.append(pl.BlockSpec(a.shape, lambda i: (0, 0)))
        args.append(a)
    out_specs, out_shapes = [], []
    for spec in outs:
        if spec[0] == "row":
            out_specs.append(pl.BlockSpec((tr, spec[1]), lambda i: (i, 0)))
            out_shapes.append(jax.ShapeDtypeStruct((rows, spec[1]), spec[2]))
        else:
            out_specs.append(pl.BlockSpec((1, spec[1]), lambda i: (0, 0)))
            out_shapes.append(jax.ShapeDtypeStruct((1, spec[1]), F32))
    n_in = len(ins)

    def body(*refs):
        vals = fn(*[r[...] for r in refs[:n_in]])
        first = pl.program_id(0) == 0
        for r, v, spec in zip(refs[n_in + len(deps):], vals, outs):
            if spec[0] == "row":
                r[...] = v.astype(r.dtype)
            else:
                _accumulate(r, v, first)

    return pl.pallas_call(body, name=name, grid=(rows // tr,), in_specs=in_specs + [_ANY] * len(deps),
                          out_specs=out_specs, out_shape=out_shapes,
                          compiler_params=_params("arbitrary"))(*args, *deps)


def _accumulate(ref, v, first):
    @pl.when(first)
    def _():
        ref[...] = v

    @pl.when(jnp.logical_not(first))
    def _():
        ref[...] += v


def _rms(x, w):
    r = lax.rsqrt(jnp.mean(x * x, axis=-1, keepdims=True) + NORM_EPS)
    return x * r * w


def _rms_bwd(x, w, dy):
    r = lax.rsqrt(jnp.mean(x * x, axis=-1, keepdims=True) + NORM_EPS)
    g = dy * w
    dx = r * (g - x * (r * r) * jnp.mean(g * x, axis=-1, keepdims=True))
    dw = jnp.sum(dy * x * r, axis=0, keepdims=True)
    return dx, dw


def _sigmoid(x):
    return 1.0 / (1.0 + jnp.exp(-x))


def _silu_and_grad(g):
    s = _sigmoid(g)
    return g * s, s * (1.0 + g * (1.0 - s))


def _swap_pairs(x):
    n = x.shape[-1]
    lane = lax.broadcasted_iota(jnp.int32, x.shape, x.ndim - 1)
    return jnp.where((lane & 1) == 0, pltpu.roll(x, n - 1, x.ndim - 1), pltpu.roll(x, 1, x.ndim - 1))


def _rot(x, cosf, sins):
    return x * cosf + _swap_pairs(x) * sins


def _unrot(d, cosf, sins):
    return d * cosf + _swap_pairs(d * sins)


def _ret_log_gamma(h):
    vals = [math.log1p(-2.0 ** (-5.0 - i)) for i in range(RET_HEADS)]
    out = jnp.float32(vals[RET_HEADS - 1])
    for i in range(RET_HEADS - 2, -1, -1):
        out = jnp.where(h == i, jnp.float32(vals[i]), out)
    return out


def _fill_decays(dec_ref, lg):
    ri = lax.broadcasted_iota(jnp.int32, (BLK, BLK), 0)
    ci = lax.broadcasted_iota(jnp.int32, (BLK, BLK), 1)
    for d in range(dec_ref.shape[0]):
        dt = d * BLK + ri - ci
        dec_ref[d] = jnp.where(dt >= 0, jnp.exp(jnp.maximum(dt, 0).astype(F32) * lg), 0.0)


def _decay_row(dec_ref, qi):
    return jnp.concatenate([dec_ref[qi - kb] for kb in range(qi + 1)], axis=1)


def _once(block_shape, index_map):
    return pl.BlockSpec(block_shape, index_map, pipeline_mode=pl.Buffered(1))


def _dot(a, b):
    return jnp.dot(a.astype(BF16), b.astype(BF16), preferred_element_type=F32)


def _dot_nt(a, b):
    return lax.dot_general(a.astype(BF16), b.astype(BF16), _NT, preferred_element_type=F32)


def _dot_tn(a, b):
    return lax.dot_general(a.astype(BF16), b.astype(BF16), _TN, preferred_element_type=F32)


def retention_fwd(name, z, cosf, sins, width_out):
    T = z.shape[0]
    nq = T // BLK
    scale = RET_DK ** -0.5

    def body(q_ref, k_ref, v_ref, cos_ref, sin_ref, o_ref, krot, vb, dec_ref):
        _fill_decays(dec_ref, _ret_log_gamma(pl.program_id(0)))
        krot[...] = (_rot(k_ref[...], cos_ref[...], sin_ref[...]) * scale).astype(BF16)
        vb[...] = v_ref[...].astype(BF16)
        for qi in range(nq):
            rows, n = slice(qi * BLK, (qi + 1) * BLK), (qi + 1) * BLK
            q = _rot(q_ref[rows, :], cos_ref[rows, :], sin_ref[rows, :])
            s = _dot_nt(q, krot[0:n, :]) * _decay_row(dec_ref, qi)
            o_ref[rows, :] = _dot(s, vb[0:n, :])

    return pl.pallas_call(
        body, name=name, grid=(RET_HEADS,),
        in_specs=[pl.BlockSpec((T, RET_DK), lambda h: (0, OFF_RQ // RET_DK + h)),
                  pl.BlockSpec((T, RET_DK), lambda h: (0, OFF_RK // RET_DK + h)),
                  pl.BlockSpec((T, RET_DV), lambda h: (0, OFF_RV // RET_DV + h)),
                  _once((T, RET_DK), lambda h: (0, 0)), _once((T, RET_DK), lambda h: (0, 0))],
        out_specs=pl.BlockSpec((T, RET_DV), lambda h: (0, h)),
        out_shape=jax.ShapeDtypeStruct((T, width_out), F32),
        scratch_shapes=[pltpu.VMEM((T, RET_DK), BF16), pltpu.VMEM((T, RET_DV), BF16),
                        pltpu.VMEM((nq, BLK, BLK), F32)],
        compiler_params=_params("arbitrary"))(z, z, z, cosf, sins)


def retention_bwd(name, z, cosf, sins, do):
    T = z.shape[0]
    nq = T // BLK
    scale = RET_DK ** -0.5

    def body(q_ref, k_ref, v_ref, cos_ref, sin_ref, do_ref, dq_ref, dk_ref, dv_ref, krot, vb, dk_acc, dv_acc, dec_ref):
        _fill_decays(dec_ref, _ret_log_gamma(pl.program_id(0)))
        krot[...] = (_rot(k_ref[...], cos_ref[...], sin_ref[...]) * scale).astype(BF16)
        vb[...] = v_ref[...].astype(BF16)
        dk_acc[...] = jnp.zeros_like(dk_acc)
        dv_acc[...] = jnp.zeros_like(dv_acc)
        for qi in range(nq):
            rows, n = slice(qi * BLK, (qi + 1) * BLK), (qi + 1) * BLK
            cos_q, sin_q = cos_ref[rows, :], sin_ref[rows, :]
            q = _rot(q_ref[rows, :], cos_q, sin_q).astype(BF16)
            dout = do_ref[rows, :].astype(BF16)
            kk, vv, dec = krot[0:n, :], vb[0:n, :], _decay_row(dec_ref, qi)
            p = (_dot_nt(q, kk) * dec).astype(BF16)
            ds = (_dot_nt(dout, vv) * dec).astype(BF16)
            dq_ref[rows, :] = _unrot(_dot(ds, kk), cos_q, sin_q).astype(dq_ref.dtype)
            dk_acc[0:n, :] += _dot_tn(ds, q)
            dv_acc[0:n, :] += _dot_tn(p, dout)
        dk_ref[...] = (_unrot(dk_acc[...], cos_ref[...], sin_ref[...]) * scale).astype(dk_ref.dtype)
        dv_ref[...] = dv_acc[...].astype(dv_ref.dtype)

    head = lambda h: (0, h)
    return pl.pallas_call(
        body, name=name, grid=(RET_HEADS,),
        in_specs=[pl.BlockSpec((T, RET_DK), lambda h: (0, OFF_RQ // RET_DK + h)),
                  pl.BlockSpec((T, RET_DK), lambda h: (0, OFF_RK // RET_DK + h)),
                  pl.BlockSpec((T, RET_DV), lambda h: (0, OFF_RV // RET_DV + h)),
                  _once((T, RET_DK), lambda h: (0, 0)), _once((T, RET_DK), lambda h: (0, 0)),
                  pl.BlockSpec((T, RET_DV), head)],
        out_specs=[pl.BlockSpec((T, RET_DK), head), pl.BlockSpec((T, RET_DK), head), pl.BlockSpec((T, RET_DV), head)],
        out_shape=[jax.ShapeDtypeStruct((T, RET_QK), BF16), jax.ShapeDtypeStruct((T, RET_QK), BF16),
                   jax.ShapeDtypeStruct((T, RET_V), BF16)],
        scratch_shapes=[pltpu.VMEM((T, RET_DK), BF16), pltpu.VMEM((T, RET_DV), BF16),
                        pltpu.VMEM((T, RET_DK), F32), pltpu.VMEM((T, RET_DV), F32),
                        pltpu.VMEM((nq, BLK, BLK), F32)],
        compiler_params=_params("arbitrary"))(z, z, z, cosf, sins, do)


GLA_PAIR = 2


def _gla_chunk(q_ref, k_ref, v_ref, glr_ref, gu, gb, rows, hh, trilf):
    ck = slice(hh * GLA_DK, (hh + 1) * GLA_DK)
    zg = _dot(glr_ref[rows, :], gu[:, ck]) + gb[:, ck]
    la = (jnp.minimum(zg, 0.0) - jnp.log(1.0 + jnp.exp(-jnp.abs(zg)))) * (1.0 / GLA_GATE_NORM)
    cum = jnp.dot(trilf, la, precision=HIGHEST, preferred_element_type=F32)
    last = jnp.sum(la, axis=0, keepdims=True)
    ecum = jnp.exp(cum)
    k = k_ref[rows, ck]
    qt = q_ref[rows, ck] * (GLA_DK ** -0.5) * ecum
    kt = k * jnp.exp(-cum)
    kh = k * jnp.exp(last - cum)
    return zg, cum, last, ecum, qt, kt, kh, v_ref[rows, hh * GLA_DV:(hh + 1) * GLA_DV].astype(BF16)


def _state_decay(last):
    e = jnp.exp(jnp.broadcast_to(last, (GLA_DK, GLA_DK)).T)
    return jnp.concatenate([e] * (GLA_DV // GLA_DK), axis=1)


def _gla_specs(T):
    wk, wv = GLA_PAIR * GLA_DK, GLA_PAIR * GLA_DV
    return [_once((T, wk), lambda h: (0, OFF_GQ // wk + h)),
            _once((T, wk), lambda h: (0, OFF_GK // wk + h)),
            _once((T, wv), lambda h: (0, OFF_GV // wv + h)),
            _once((T, LANE), lambda h: (0, 0)),
            pl.BlockSpec((LANE, wk), lambda h: (0, h)),
            pl.BlockSpec((1, wk), lambda h: (0, h))]


def gla_fwd(name, z, glr, gu, gb, o_prev):
    T = z.shape[0]
    nc = T // CHUNK
    wv = GLA_PAIR * GLA_DV

    def body(q_ref, k_ref, v_ref, glr_ref, gu_ref, gb_ref, prev_ref, o_ref, *S):
        del prev_ref
        gu_b, gb_v = gu_ref[...].astype(BF16), gb_ref[...]
        ri = lax.broadcasted_iota(jnp.int32, (CHUNK, CHUNK), 0)
        ci = lax.broadcasted_iota(jnp.int32, (CHUNK, CHUNK), 1)
        tril = ri >= ci
        trilf = tril.astype(F32)
        for s_ref in S:
            s_ref[...] = jnp.zeros_like(s_ref)

        def step(c, carry):
            rows = pl.ds(pl.multiple_of(c * CHUNK, CHUNK), CHUNK)
            heads = range(GLA_PAIR)
            ch = [_gla_chunk(q_ref, k_ref, v_ref, glr_ref, gu_b, gb_v, rows, hh, trilf) for hh in heads]
            a = [jnp.where(tril, _dot_nt(ch[hh][4], ch[hh][5]), 0.0) for hh in heads]
            s_prev = [S[hh][...] for hh in heads]
            intra = [_dot(a[hh], ch[hh][7]) for hh in heads]
            inter = [_dot(ch[hh][4], s_prev[hh]) for hh in heads]
            added = [_dot_tn(ch[hh][6], ch[hh][7]) for hh in heads]
            for hh in heads:
                o_ref[rows, hh * GLA_DV:(hh + 1) * GLA_DV] = intra[hh] + inter[hh]
                S[hh][...] = s_prev[hh] * _state_decay(ch[hh][2]) + added[hh]
            return carry

        lax.fori_loop(0, nc, step, 0)

    n_in = 6
    return pl.pallas_call(
        body, name=name, grid=(GLA_HEADS // GLA_PAIR,),
        in_specs=_gla_specs(T) + [pl.BlockSpec(memory_space=pl.ANY)],
        out_specs=pl.BlockSpec((T, wv), lambda h: (0, RET_V // wv + h)),
        out_shape=jax.ShapeDtypeStruct(o_prev.shape, F32),
        scratch_shapes=[pltpu.VMEM((GLA_DK, GLA_DV), F32)] * GLA_PAIR,
        input_output_aliases={n_in: 0},
        compiler_params=_params("arbitrary"))(z, z, z, glr, gu, gb, o_prev)


def gla_bwd(name, z, glr, gu, gb, do):
    T = z.shape[0]
    nc = T // CHUNK

    def body(q_ref, k_ref, v_ref, glr_ref, gu_ref, gb_ref, do_ref,
             dq_ref, dk_ref, dv_ref, dglr_ref, dgu_ref, dgb_ref, s_all, dS):
        gu_b, gb_v = gu_ref[...].astype(BF16), gb_ref[...]
        ri = lax.broadcasted_iota(jnp.int32, (CHUNK, CHUNK), 0)
        ci = lax.broadcasted_iota(jnp.int32, (CHUNK, CHUNK), 1)
        tril = ri >= ci
        trilf = tril.astype(F32)
        triuf = (ri <= ci).astype(F32)
        last_row = lax.broadcasted_iota(jnp.int32, (CHUNK, GLA_DK), 0) == CHUNK - 1
        ones8 = jnp.ones((8, GLA_DV), F32)

        heads = range(GLA_PAIR)

        def fstep(c, carry):
            rows = pl.ds(pl.multiple_of(c * CHUNK, CHUNK), CHUNK)
            ch = [_gla_chunk(q_ref, k_ref, v_ref, glr_ref, gu_b, gb_v, rows, hh, trilf) for hh in heads]
            added = [_dot_tn(ch[hh][6], ch[hh][7]) for hh in heads]
            for hh in heads:
                s_prev = dS[hh]
                s_all[hh, c] = s_prev
                dS[hh] = s_prev * _state_decay(ch[hh][2]) + added[hh]
            return carry

        dS[...] = jnp.zeros_like(dS)
        lax.fori_loop(0, nc, fstep, 0)
        dS[...] = jnp.zeros_like(dS)
        dgu_ref[...] = jnp.zeros_like(dgu_ref)
        dgb_ref[...] = jnp.zeros_like(dgb_ref)

        def bstep(i, carry):
            c = nc - 1 - i
            rows = pl.ds(pl.multiple_of(c * CHUNK, CHUNK), CHUNK)
            glr_c = glr_ref[rows, :]
            cks = [slice(hh * GLA_DK, (hh + 1) * GLA_DK) for hh in heads]
            cvs = [slice(hh * GLA_DV, (hh + 1) * GLA_DV) for hh in heads]
            ch = [_gla_chunk(q_ref, k_ref, v_ref, glr_ref, gu_b, gb_v, rows, hh, trilf) for hh in heads]
            zg, cum, last, ecum, qt, kt, kh, v = [[ch[hh][j] for hh in heads] for j in range(8)]
            s_prev = [s_all[hh, c] for hh in heads]
            ds_new = [dS[hh] for hh in heads]
            dout = [do_ref[rows, cvs[hh]].astype(BF16) for hh in heads]
            a = [jnp.where(tril, _dot_nt(qt[hh], kt[hh]), 0.0) for hh in heads]
            da = [jnp.where(tril, _dot_nt(dout[hh], v[hh]), 0.0) for hh in heads]
            dv_a = [_dot_tn(a[hh], dout[hh]) for hh in heads]
            dv_b = [_dot(kh[hh], ds_new[hh]) for hh in heads]
            dqt_a = [_dot(da[hh], kt[hh]) for hh in heads]
            dqt_b = [_dot_nt(dout[hh], s_prev[hh]) for hh in heads]
            dkt = [_dot_tn(da[hh], qt[hh]) for hh in heads]
            dkh = [_dot_nt(v[hh], ds_new[hh]) for hh in heads]
            ds_add = [_dot_tn(qt[hh], dout[hh]) for hh in heads]
            rs = [lax.dot_general(ones8, ds_new[hh] * s_prev[hh], _NT, precision=HIGHEST, preferred_element_type=F32)
                  for hh in heads]
            dcum = []
            for hh in heads:
                dv_ref[rows, cvs[hh]] = (dv_a[hh] + dv_b[hh]).astype(dv_ref.dtype)
                dS[hh] = ds_new[hh] * _state_decay(last[hh]) + ds_add[hh]
                dqt = dqt_a[hh] + dqt_b[hh]
                dq_ref[rows, cks[hh]] = (dqt * ecum[hh] * (GLA_DK ** -0.5)).astype(dq_ref.dtype)
                dk_ref[rows, cks[hh]] = (dkt[hh] * jnp.exp(-cum[hh])
                                         + dkh[hh] * jnp.exp(last[hh] - cum[hh])).astype(dk_ref.dtype)
                dkh_kh = dkh[hh] * kh[hh]
                dlast = (jnp.sum(dkh_kh, axis=0, keepdims=True)
                         + jnp.exp(last[hh]) * (jnp.sum(rs[hh], axis=0, keepdims=True) * 0.125))
                dcum.append(dqt * qt[hh] - dkt[hh] * kt[hh] - dkh_kh + jnp.where(last_row, dlast, 0.0))
            dla = [jnp.dot(triuf, dcum[hh], precision=HIGHEST, preferred_element_type=F32) for hh in heads]
            dzg = [dla[hh] * (1.0 / GLA_GATE_NORM) * _sigmoid(-zg[hh]) for hh in heads]
            dglr = [_dot_nt(dzg[hh], gu_b[:, cks[hh]]) for hh in heads]
            dgu = [_dot_tn(glr_c, dzg[hh]) for hh in heads]
            for hh in heads:
                dglr_ref[hh, rows, :] = dglr[hh]
                dgu_ref[:, cks[hh]] += dgu[hh]
                dgb_ref[:, cks[hh]] += jnp.sum(dzg[hh], axis=0, keepdims=True)
            return carry

        lax.fori_loop(0, nc, bstep, 0)

    wk, wv = GLA_PAIR * GLA_DK, GLA_PAIR * GLA_DV
    return pl.pallas_call(
        body, name=name, grid=(GLA_HEADS // GLA_PAIR,),
        in_specs=_gla_specs(T) + [_once((T, wv), lambda h: (0, RET_V // wv + h))],
        out_specs=[pl.BlockSpec((T, wk), lambda h: (0, h)), pl.BlockSpec((T, wk), lambda h: (0, h)),
                   pl.BlockSpec((T, wv), lambda h: (0, h)),
                   pl.BlockSpec((GLA_PAIR, T, LANE), lambda h: (h, 0, 0)),
                   pl.BlockSpec((LANE, wk), lambda h: (0, h)), pl.BlockSpec((1, wk), lambda h: (0, h))],
        out_shape=[jax.ShapeDtypeStruct((T, GLA_QK), BF16), jax.ShapeDtypeStruct((T, GLA_QK), BF16),
                   jax.ShapeDtypeStruct((T, GLA_V), BF16), jax.ShapeDtypeStruct((GLA_HEADS, T, LANE), F32),
                   jax.ShapeDtypeStruct((LANE, GLA_QK), F32), jax.ShapeDtypeStruct((1, GLA_QK), F32)],
        scratch_shapes=[pltpu.VMEM((GLA_PAIR, nc, GLA_DK, GLA_DV), F32), pltpu.VMEM((GLA_PAIR, GLA_DK, GLA_DV), F32)],
        compiler_params=_params("arbitrary"))(z, z, z, glr, gu, gb, do)


HN_HEADS = RET_HEADS + GLA_HEADS
HN_W = RET_DV


def _gate_col(h):
    return jnp.where(h < RET_HEADS, OFF_RG // HN_W + h, OFF_GG // HN_W + h - RET_HEADS)


def headnorm_fwd(name, oraw, z, w):
    T = oraw.shape[0]
    tr = _pick(T, _TILES)

    def body(o_ref, g_ref, w_ref, y_ref):
        y_ref[...] = (_rms(o_ref[...], w_ref[...]) * _silu_and_grad(g_ref[...])[0]).astype(y_ref.dtype)

    return pl.pallas_call(
        body, name=name, grid=(HN_HEADS, T // tr),
        in_specs=[pl.BlockSpec((tr, HN_W), lambda h, i: (i, h)),
                  pl.BlockSpec((tr, HN_W), lambda h, i: (i, _gate_col(h))),
                  pl.BlockSpec((1, HN_W), lambda h, i: (0, h))],
        out_specs=pl.BlockSpec((tr, HN_W), lambda h, i: (i, h)),
        out_shape=jax.ShapeDtypeStruct((T, HN_HEADS * HN_W), BF16),
        compiler_params=_params("arbitrary", "arbitrary"))(oraw, z, w)


def headnorm_bwd(name, oraw, z, w, dy):
    T = oraw.shape[0]
    tr = _pick(T, _TILES)

    def body(o_ref, g_ref, w_ref, dy_ref, do_ref, dg_ref, dw_ref):
        o, wv, dyv = o_ref[...], w_ref[...], dy_ref[...].astype(F32)
        silu, dsilu = _silu_and_grad(g_ref[...])
        n = _rms(o, wv)
        dg_ref[...] = (dyv * n * dsilu).astype(dg_ref.dtype)
        dx, dw = _rms_bwd(o, wv, dyv * silu)
        do_ref[...] = dx
        _accumulate(dw_ref, dw, pl.program_id(1) == 0)

    blk = pl.BlockSpec((tr, HN_W), lambda h, i: (i, h))
    return pl.pallas_call(
        body, name=name, grid=(HN_HEADS, T // tr),
        in_specs=[blk, pl.BlockSpec((tr, HN_W), lambda h, i: (i, _gate_col(h))),
                  pl.BlockSpec((1, HN_W), lambda h, i: (0, h)), blk],
        out_specs=[blk, blk, pl.BlockSpec((1, HN_W), lambda h, i: (0, h))],
        out_shape=[jax.ShapeDtypeStruct((T, HN_HEADS * HN_W), F32),
                   jax.ShapeDtypeStruct((T, HN_HEADS * HN_W), BF16),
                   jax.ShapeDtypeStruct((1, HN_HEADS * HN_W), F32)],
        compiler_params=_params("arbitrary", "arbitrary"))(oraw, z, w, dy)


N_MASKS = 4


def _check_mask_classes(T):
    for window, dilation in DILATED_BRANCHES[:-1]:
        assert window < (N_MASKS - 1) * BLK - (BLK - 1) and BLK % dilation == 0
    assert DILATED_BRANCHES[-1][0] >= T and BLK % DILATED_BRANCHES[-1][1] == 0


def _fill_masks(mult_ref, bias_ref):
    ri = lax.broadcasted_iota(jnp.int32, (BLK, BLK), 0)
    ci = lax.broadcasted_iota(jnp.int32, (BLK, BLK), 1)
    for d in range(N_MASKS):
        dt = d * BLK + ri - ci
        mult = jnp.zeros((BLK, BLK), F32)
        for window, dilation in DILATED_BRANCHES:
            hit = (dt >= 0) & (dt <= window) & ((dt & (dilation - 1)) == 0)
            mult = mult + hit.astype(F32)
        mult_ref[d] = mult
        bias_ref[d] = jnp.where(mult > 0, 0.0, -1e30)


def _mask_row(ref, qi):
    return jnp.concatenate([ref[min(qi - kb, N_MASKS - 1)] for kb in range(qi + 1)], axis=1)


def attn_fwd(name, qkv):
    T = qkv.shape[0]
    D = qkv.shape[1] // 3
    dh = D // ATT_HEADS
    nq = T // BLK
    scale = dh ** -0.5

    _check_mask_classes(T)

    def body(q_ref, k_ref, v_ref, o_ref, lse_ref, mult_ref, bias_ref):
        @pl.when(pl.program_id(0) == 0)
        def _():
            _fill_masks(mult_ref, bias_ref)

        for q0 in range(0, nq, 2):
            qis = range(q0, min(q0 + 2, nq))
            rows = [slice(qi * BLK, (qi + 1) * BLK) for qi in qis]
            ns = [(qi + 1) * BLK for qi in qis]
            s = [_dot_nt(q_ref[r, :], k_ref[0:n, :]) for r, n in zip(rows, ns)]
            s = [x * scale + _mask_row(bias_ref, qi) for x, qi in zip(s, qis)]
            m = [jnp.max(x, axis=-1, keepdims=True) for x in s]
            p = [_mask_row(mult_ref, qi) * jnp.exp(x - mx) for qi, x, mx in zip(qis, s, m)]
            l = [jnp.sum(x, axis=-1, keepdims=True) for x in p]
            pv = [_dot(x, v_ref[0:n, :]) for x, n in zip(p, ns)]
            for r, x, lx, mx in zip(rows, pv, l, m):
                o_ref[r, :] = (x / lx).astype(o_ref.dtype)
                lse_ref[r, :] = jnp.broadcast_to(mx + jnp.log(lx), (BLK, LANE))

    return pl.pallas_call(
        body, name=name, grid=(ATT_HEADS,),
        in_specs=[pl.BlockSpec((T, dh), lambda h: (0, h)),
                  pl.BlockSpec((T, dh), lambda h: (0, ATT_HEADS + h)),
                  pl.BlockSpec((T, dh), lambda h: (0, 2 * ATT_HEADS + h))],
        out_specs=[pl.BlockSpec((T, dh), lambda h: (0, h)),
                   pl.BlockSpec((None, T, LANE), lambda h: (h, 0, 0))],
        out_shape=[jax.ShapeDtypeStruct((T, D), BF16), jax.ShapeDtypeStruct((ATT_HEADS, T, LANE), F32)],
        scratch_shapes=[pltpu.VMEM((N_MASKS, BLK, BLK), F32), pltpu.VMEM((N_MASKS, BLK, BLK), F32)],
        compiler_params=_params("arbitrary"))(qkv, qkv, qkv)


def attn_bwd(name, qkv, o, lse, do):
    T = qkv.shape[0]
    D = qkv.shape[1] // 3
    dh = D // ATT_HEADS
    nq = T // BLK
    scale = dh ** -0.5

    _check_mask_classes(T)

    def body(q_ref, k_ref, v_ref, o_ref, lse_ref, do_ref, dq_ref, dk_ref, dv_ref, dk_acc, dv_acc, mult_ref, bias_ref):
        @pl.when(pl.program_id(0) == 0)
        def _():
            _fill_masks(mult_ref, bias_ref)

        dk_acc[...] = jnp.zeros_like(dk_acc)
        dv_acc[...] = jnp.zeros_like(dv_acc)
        for qi in range(nq):
            rows, n = slice(qi * BLK, (qi + 1) * BLK), (qi + 1) * BLK
            q, dout = q_ref[rows, :], do_ref[rows, :]
            kk, vv = k_ref[0:n, :], v_ref[0:n, :]
            delta = jnp.sum(dout.astype(F32) * o_ref[rows, :].astype(F32), axis=-1, keepdims=True)
            lse = jnp.max(lse_ref[rows, :], axis=-1, keepdims=True)
            s = _dot_nt(q, kk) * scale + _mask_row(bias_ref, qi)
            p = _mask_row(mult_ref, qi) * jnp.exp(s - lse)
            ds = (p * (_dot_nt(dout, vv) - delta) * scale).astype(BF16)
            dq_ref[rows, :] = _dot(ds, kk).astype(dq_ref.dtype)
            dk_acc[0:n, :] += _dot_tn(ds, q)
            dv_acc[0:n, :] += _dot_tn(p, dout)
        dk_ref[...] = dk_acc[...].astype(dk_ref.dtype)
        dv_ref[...] = dv_acc[...].astype(dv_ref.dtype)

    full = pl.BlockSpec((T, dh), lambda h: (0, h))
    return pl.pallas_call(
        body, name=name, grid=(ATT_HEADS,),
        in_specs=[full, pl.BlockSpec((T, dh), lambda h: (0, ATT_HEADS + h)),
                  pl.BlockSpec((T, dh), lambda h: (0, 2 * ATT_HEADS + h)),
                  full, pl.BlockSpec((None, T, LANE), lambda h: (h, 0, 0)), full],
        out_specs=[full, full, full],
        out_shape=[jax.ShapeDtypeStruct((T, D), BF16)] * 3,
        scratch_shapes=[pltpu.VMEM((T, dh), F32), pltpu.VMEM((T, dh), F32),
                        pltpu.VMEM((N_MASKS, BLK, BLK), F32), pltpu.VMEM((N_MASKS, BLK, BLK), F32)],
        compiler_params=_params("arbitrary"))(qkv, qkv, qkv, o, lse, do)


def _mesh_pos():
    mx, my, mc = lax.axis_index("x"), lax.axis_index("y"), lax.axis_index("c")
    return mx, my, mc, 4 * mx + 2 * my + mc


def _peer(k, mx, my, mc):
    px, py, pc = mx ^ (k >> 2), my ^ ((k >> 1) & 1), mc ^ (k & 1)
    return (px, py, pc), 4 * px + 2 * py + pc


_SIBLING = 1
_OTHER_CHIPS = (4, 2, 6)
N_CHIP = N_DEV // 2
_PLANS = {"gather": (2, N_DEV - 1), "to_chips": (2, 1 + len(_OTHER_CHIPS)), "pass_on": (1, len(_OTHER_CHIPS)),
          "halves": (2, N_CHIP), "chip_sums": (2, len(_OTHER_CHIPS))}


def _copies(kind, items, send_sems, recv_sems):
    mx, my, mc, me = _mesh_pos()
    out = []

    def add(n, src, dst, peer):
        out.append(pltpu.make_async_remote_copy(
            src_ref=src, dst_ref=dst, send_sem=send_sems.at[n], recv_sem=recv_sems.at[n],
            device_id=peer, device_id_type=pl.DeviceIdType.MESH))

    per_item = _PLANS[kind][1]
    sibling = _peer(_SIBLING, mx, my, mc)[0]
    for i, refs in enumerate(items):
        n = i * per_item
        if kind == "gather":
            for k in range(1, N_DEV):
                add(n + k - 1, refs[0], refs[1].at[me], _peer(k, mx, my, mc)[0])
        elif kind == "to_chips":
            rows = refs[0].shape[0]
            dst = refs[1].at[me] if rows == refs[1].shape[1] else refs[1].at[me, pl.ds(0, rows)]
            for j, k in enumerate((_SIBLING,) + _OTHER_CHIPS):
                add(n + j, refs[0], dst, _peer(k, mx, my, mc)[0])
        elif kind == "pass_on":
            for j, k in enumerate(_OTHER_CHIPS):
                add(n + j, refs[0].at[me ^ k], refs[0].at[me ^ k], sibling)
        elif kind == "halves":
            for chip in range(N_CHIP):
                add(n + chip, refs[0].at[2 * chip + 1 - mc], refs[1].at[chip], sibling)
        else:
            for j, k in enumerate(_OTHER_CHIPS):
                peer, to = _peer(k, mx, my, mc)
                add(n + j, refs[0].at[to // 2], refs[1].at[me // 2], peer)
    return out


_HBM = pl.BlockSpec(memory_space=pltpu.HBM)
_SEM = pl.BlockSpec(memory_space=pltpu.SEMAPHORE)
_DATAFLOW = pltpu.SideEffectType.DATAFLOW_SIDE_EFFECTING


def exchange_call(name, waits, starts, deps=()):
    bufs, slot_of = [], {}

    def slots(items):
        out = []
        for item in items:
            for b in item:
                if id(b) not in slot_of:
                    slot_of[id(b)] = len(bufs)
                    bufs.append(b)
            out.append(tuple(slot_of[id(b)] for b in item))
        return out

    wait_plan = [(kind, slots(handle[0])) for kind, handle in waits]
    start_plan = [(kind, slots(items)) for kind, items in starts]
    wait_sems = [s for _, handle in waits for s in handle[1:]]
    n_buf, n_ws, n_start = len(bufs), len(wait_sems), len(starts)

    def body(*refs):
        buf_refs, sems_in = refs[:n_buf], refs[n_buf:n_buf + n_ws]
        outs = refs[n_buf + n_ws + len(deps):]
        pick = lambda plan: [tuple(buf_refs[s] for s in item) for item in plan]
        for wi, (kind, plan) in enumerate(wait_plan):
            copies = _copies(kind, pick(plan), sems_in[2 * wi], sems_in[2 * wi + 1])
            for cp in copies:
                cp.wait_send()
            for cp in copies:
                cp.wait_recv()
        for si, (kind, plan) in enumerate(start_plan):
            for cp in _copies(kind, pick(plan), outs[2 * si], outs[2 * si + 1]):
                cp.start()
        outs[-1][...] = jnp.zeros_like(outs[-1])

    hbm_bufs = [pltpu.with_memory_space_constraint(b, pltpu.HBM) for b in bufs]
    sem_shapes = []
    for kind, plan in start_plan:
        sem_shapes += [pltpu.SemaphoreType.DMA((len(plan) * _PLANS[kind][1],))] * 2
    outs = pl.pallas_call(
        body, name=name,
        out_shape=sem_shapes + [pltpu.HBM(b.shape, b.dtype) for b in bufs] + [jax.ShapeDtypeStruct((8, LANE), F32)],
        in_specs=[_HBM] * n_buf + [_SEM] * n_ws + [_ANY] * len(deps),
        out_specs=[_SEM] * (2 * n_start) + [_HBM] * n_buf + [pl.BlockSpec(memory_space=pltpu.VMEM)],
        input_output_aliases={i: 2 * n_start + i for i in range(n_buf)},
        compiler_params=pltpu.CompilerParams(has_side_effects=_DATAFLOW))(*hbm_bufs, *wait_sems, *deps)
    sems, thru, token = outs[:2 * n_start], outs[2 * n_start:-1], outs[-1]
    through = lambda plan: [tuple(thru[s] for s in item) for item in plan]
    waited = [through(plan) for _, plan in wait_plan]
    handles = [(through(plan), sems[2 * si], sems[2 * si + 1]) for si, (_, plan) in enumerate(start_plan)]
    return waited, handles, token


def gather_small(name, a, deps=()):
    def body(a_ref, *rest):
        o_ref, send_sems, recv_sems, local_sem = rest[len(deps):]
        me = _mesh_pos()[3]
        own = pltpu.make_async_copy(a_ref, o_ref.at[me], local_sem)
        own.start()
        copies = _copies("gather", [(a_ref, o_ref)], send_sems, recv_sems)
        for cp in copies:
            cp.start()
        for cp in copies:
            cp.wait_recv()
        for cp in copies:
            cp.wait_send()
        own.wait()

    return pl.pallas_call(
        body, name=name, in_specs=[_ANY] * (1 + len(deps)), out_specs=_ANY,
        out_shape=jax.ShapeDtypeStruct((N_DEV,) + a.shape, a.dtype),
        scratch_shapes=[pltpu.SemaphoreType.DMA((N_DEV - 1,)), pltpu.SemaphoreType.DMA((N_DEV - 1,)),
                        pltpu.SemaphoreType.DMA],
        compiler_params=pltpu.CompilerParams(has_side_effects=True))(a, *deps)


def _adamw_math(w, g, m, v):
    m2 = ADAM_B1 * m + (1.0 - ADAM_B1) * g
    v2 = ADAM_B2 * v + (1.0 - ADAM_B2) * (g * g)
    m_hat = m2 / (1.0 - ADAM_B1 ** ADAM_STEP)
    v_hat = v2 / (1.0 - ADAM_B2 ** ADAM_STEP)
    delta = -ADAM_LR * (m_hat / (jnp.sqrt(v_hat) + ADAM_EPS) + ADAM_WD * w)
    return delta, m2, v2


def chip_sum(name, a, half):
    _, r, c = a.shape
    tr = r
    chip = 2 * lax.axis_index("x") + lax.axis_index("y")
    where = jnp.stack([lax.axis_index("c"), chip ^ 1, chip ^ 2, chip ^ 3]).astype(jnp.int32)

    def body(where_ref, a_ref, h_ref, o_ref):
        del where_ref
        o_ref[...] = (a_ref[...].astype(F32) + h_ref[...].astype(F32)).astype(o_ref.dtype)

    blk = pl.BlockSpec((None, tr, c), lambda g, i, where: (where[1 + g], i, 0))
    grid_spec = pltpu.PrefetchScalarGridSpec(
        num_scalar_prefetch=1, grid=(N_CHIP - 1, r // tr),
        in_specs=[pl.BlockSpec((None, None, tr, c), lambda g, i, where: (where[1 + g], where[0], i, 0)), blk],
        out_specs=blk)
    return pl.pallas_call(
        body, name=name, grid_spec=grid_spec, out_shape=jax.ShapeDtypeStruct((N_CHIP, r, c), BF16),
        compiler_params=_params("parallel", "parallel"))(where, a.reshape(N_CHIP, 2, r, c), half)


def adamw(name, w, m, v, l, land, a, half, prev=None):
    L, r, c = w.shape
    cp = land.shape[2]
    tr = _pick(r, (256, 176, 128, 64, 32, 16, 8))

    def body(w_ref, m_ref, v_ref, land_ref, a_ref, half_ref, *rest):
        g_ref, d_ref, m2_ref, v2_ref = rest[-4:]
        chip = _mesh_pos()[3] // 2
        mine = a_ref[:, pl.ds(0, c)].astype(F32) + half_ref[:, pl.ds(0, c)].astype(F32)
        g = None
        for s in range(N_CHIP):
            part = jnp.where(chip == s, mine, land_ref[s, :, pl.ds(0, c)].astype(F32))
            g = part if g is None else g + part
        delta, m2, v2 = _adamw_math(w_ref[...], g, m_ref[...], v_ref[...])
        g_ref[...] = g
        d_ref[...] = delta
        m2_ref[...] = m2
        v2_ref[...] = v2

    blk = pl.BlockSpec((None, tr, c), lambda i: (l, i, 0))
    shape = jax.ShapeDtypeStruct((L, r, c), F32)
    extra = [] if prev is None else list(prev)
    return pl.pallas_call(
        body, name=name, grid=(r // tr,),
        in_specs=[blk, blk, blk, pl.BlockSpec((N_CHIP, tr, cp), lambda i: (0, i, 0)),
                  pl.BlockSpec((None, tr, cp), lambda i: (_mesh_pos()[3], i, 0)),
                  pl.BlockSpec((None, tr, cp), lambda i: (_mesh_pos()[3] // 2, i, 0))] + [_ANY] * len(extra),
        out_specs=[blk] * 4, out_shape=[shape] * 4,
        input_output_aliases={6 + k: k for k in range(len(extra))},
        compiler_params=_params("parallel"))(w, m, v, land, a, half, *extra)


def adamw_columns(name, w, m, v, land, a, half):
    r, _, D = w.shape
    tc = _pick(D, (256, 128))

    def body(w_ref, m_ref, v_ref, land_ref, a_ref, half_ref, g_ref, d_ref, m2_ref, v2_ref):
        chip = _mesh_pos()[3] // 2
        mine = a_ref[...].astype(F32) + half_ref[...].astype(F32)
        g = None
        for s in range(N_CHIP):
            part = jnp.where(chip == s, mine, land_ref[s].astype(F32))
            g = part if g is None else g + part
        flat = lambda ref: ref[...].reshape(r, tc)
        delta, m2, v2 = _adamw_math(flat(w_ref), g, flat(m_ref), flat(v_ref))
        for ref, val in ((g_ref, g), (d_ref, delta), (m2_ref, m2), (v2_ref, v2)):
            ref[...] = val.reshape(r, 1, tc)

    blk = pl.BlockSpec((r, 1, tc), lambda i: (0, 0, i))
    shape = jax.ShapeDtypeStruct((r, 1, D), F32)
    return pl.pallas_call(
        body, name=name, grid=(D // tc,),
        in_specs=[blk, blk, blk, pl.BlockSpec((N_CHIP, r, tc), lambda i: (0, 0, i)),
                  pl.BlockSpec((None, r, tc), lambda i: (_mesh_pos()[3], 0, i)),
                  pl.BlockSpec((None, r, tc), lambda i: (_mesh_pos()[3] // 2, 0, i))],
        out_specs=[blk] * 4, out_shape=[shape] * 4,
        compiler_params=_params("parallel"))(w, m, v, land, a, half)


def adamw_small(name, w, m, v, parts):
    n = w.shape[1]

    def body(w_ref, m_ref, v_ref, p_ref, g_ref, d_ref, m2_ref, v2_ref):
        g = p_ref[0:1, :]
        for s in range(1, N_DEV):
            g = g + p_ref[s:s + 1, :]
        delta, m2, v2 = _adamw_math(w_ref[...], g, m_ref[...], v_ref[...])
        g_ref[...] = g
        d_ref[...] = delta
        m2_ref[...] = m2
        v2_ref[...] = v2

    shape = jax.ShapeDtypeStruct((1, n), F32)
    return pl.pallas_call(body, name=name, out_shape=[shape] * 4,
                          compiler_params=pltpu.CompilerParams(vmem_limit_bytes=VMEM_LIMIT_BYTES))(w, m, v, parts)


def _rope_tables(positions):
    half = RET_DK // 2
    inv_freq = 1.0 / jnp.power(RET_THETA_BASE, jnp.linspace(0.0, 1.0, half, dtype=F32))
    ang = positions.astype(F32)[:, None] * inv_freq
    cos, sin = jnp.cos(ang), jnp.sin(ang)
    cosf = jnp.repeat(cos, 2, axis=-1)
    sins = jnp.stack([-sin, sin], axis=-1).reshape(cosf.shape)
    return cosf, sins


def _pad_to(a, axis, size):
    pad = [(0, 0)] * a.ndim
    pad[axis] = (0, size - a.shape[axis])
    return jnp.pad(a, pad)


def _round_up(n, m):
    return -(-n // m) * m


def kernel(x, p, positions, attn_norm_w, ffn_norm_w, ple_norm_w, final_norm_w, ab_w_in, ab_gla_gate_up, ab_gla_gate_b, ab_ret_norm_w, ab_gla_norm_w, ab_w_out, c_w_qkv, c_w_out, ffn_w_gate, ffn_w_up, ffn_w_down, ple_w_proj, ple_w_gate, loss_target, m_attn_norm_w, m_ffn_norm_w, m_ple_norm_w, m_final_norm_w, m_ab_w_in, m_ab_gla_gate_up, m_ab_gla_gate_b, m_ab_ret_norm_w, m_ab_gla_norm_w, m_ab_w_out, m_c_w_qkv, m_c_w_out, m_ffn_w_gate, m_ffn_w_up, m_ffn_w_down, m_ple_w_proj, m_ple_w_gate, v_attn_norm_w, v_ffn_norm_w, v_ple_norm_w, v_final_norm_w, v_ab_w_in, v_ab_gla_gate_up, v_ab_gla_gate_b, v_ab_ret_norm_w, v_ab_gla_norm_w, v_ab_w_out, v_c_w_qkv, v_c_w_out, v_ffn_w_gate, v_ffn_w_up, v_ffn_w_down, v_ple_w_proj, v_ple_w_gate):
    T, D = x.shape[1], x.shape[2]
    depth = attn_norm_w.shape[0]
    assert ab_w_in.shape[0] == 1 and c_w_qkv.shape[0] == 1 and depth == 2, "one even and one odd layer"
    me = 4 * lax.axis_index("x") + 2 * lax.axis_index("y") + lax.axis_index("c")
    in_shard = ab_w_in.shape[2]
    in_width = in_shard * N_DEV
    assert in_width == OFF_LR + GLA_GATE_RANK
    fs = ffn_w_gate.shape[2]
    fp = _round_up(fs, LANE)
    gu_cols = ab_gla_gate_up.shape[2]

    bf = lambda a: a.astype(BF16)
    tr_ = lambda a: jnp.swapaxes(a, -1, -2)
    wg_t, wu_t = tr_(ffn_w_gate), tr_(ffn_w_up)
    srcs = {"w_in": bf(tr_(ab_w_in[0]))}
    group_keys = [["w_in"], ["gu", "w_oab"], ["wg0", "wu0"], ["wd0", "wpg0", "wpp0"], ["w_qkv", "w_oc"],
                  ["wg1", "wu1"], ["wd1", "wpg1", "wpp1"]]
    G_IN, G_OUT, G_QKV = 0, 1, 4
    g_ffn = lambda layer: (2, 3) if layer == 0 else (5, 6)

    def landing(key):
        a = srcs[key]
        rows = fp if key[:2] in ("wg", "wu", "wd") else a.shape[0]
        buf = lax.empty((N_DEV, rows) + a.shape[1:], a.dtype)
        if rows > a.shape[0]:
            zeros = jnp.zeros((N_DEV, rows - a.shape[0]) + a.shape[1:], a.dtype)
            buf = lax.dynamic_update_slice(buf, zeros, (0, a.shape[0]) + (0,) * (a.ndim - 1))
        return lax.dynamic_update_slice(buf, a[None], (me,) + (0,) * a.ndim)

    _, chip_handles, gather_token = exchange_call(
        "gather_start_in", [], [("to_chips", [(srcs[k], landing(k)) for k in group_keys[G_IN]])])
    (gather_token, w_out_, gu_, w_qkv_, w_oc_, wg_, wu_, wd_, wpg_, wpp_) = lax.optimization_barrier(
        (gather_token, ab_w_out, ab_gla_gate_up, c_w_qkv, c_w_out, wg_t, wu_t, ffn_w_down, ple_w_gate, ple_w_proj))
    srcs.update(w_oab=bf(w_out_[0]), gu=gu_[0], w_qkv=bf(w_qkv_[0]), w_oc=bf(w_oc_[0]))
    for l in range(depth):
        srcs[f"wg{l}"] = bf(wg_[l])
        srcs[f"wu{l}"] = bf(wu_[l])
        srcs[f"wd{l}"] = bf(wd_[l])
        srcs[f"wpg{l}"] = bf(wpg_[l])
        srcs[f"wpp{l}"] = bf(wpp_[l])
    _, more, gather_token = exchange_call(
        "gather_start", [], [("to_chips", [(srcs[k], landing(k)) for k in keys]) for keys in group_keys[1:]],
        deps=(gather_token,))
    chip_handles = chip_handles + more
    weights = {}

    def gather_wait(gi, dep):
        lands = [(land,) for _, land in chip_handles[gi][0]]
        _, (passing,), _ = exchange_call(
            f"gather{gi}_pass", [("to_chips", chip_handles[gi])], [("pass_on", lands)], deps=(dep,))
        (complete,), _, _ = exchange_call(f"gather{gi}_done", [("pass_on", passing)], [])
        weights.update(zip(group_keys[gi], [land for (land,) in complete]))

    gb = ab_gla_gate_b
    hn_w = jnp.concatenate([ab_ret_norm_w, ab_gla_norm_w], axis=1)
    cosf, sins = _rope_tables(positions[0])
    p_bf = bf(p[:, 0])

    xs = x[0]
    saved = []
    for i in range(depth):
        nm = f"l{i}_"
        w_attn, w_ffn, w_ple = attn_norm_w[i:i + 1], ffn_norm_w[i:i + 1], ple_norm_w[i:i + 1]
        (xn,) = rowwise(nm + "norm_attn", lambda a, w: (_rms(a, w),), T, [("row", xs), ("full", w_attn)],
                        [("row", D, BF16)], deps=(gather_token,) if i == 0 else ())
        if i % 2 == 0:
            gather_wait(G_IN, xn)
            w_in_t = weights["w_in"].reshape(1, 1, in_width, D)
            w_lr_t = _pad_to(w_in_t[0, 0, OFF_LR:], 0, LANE).reshape(1, 1, LANE, D)
            z = mmt_fwd(nm + "mm_in", xn, w_in_t, 0, F32, n=OFF_LR)
            glr = mmt_fwd(nm + "mm_lr", xn, w_lr_t, 0, F32)
            oraw = retention_fwd(nm + "ret_fwd", z, cosf, sins, RET_V + GLA_V)
            gather_wait(G_OUT, oraw)
            w_oab = weights["w_oab"].reshape(1, 1, D, D)
            gu_full = _pad_to(weights["gu"].transpose(1, 0, 2).reshape(GLA_GATE_RANK, GLA_QK), 0, LANE)
            oraw = gla_fwd(nm + "gla_fwd", z, glr, gu_full, gb, oraw)
            o = headnorm_fwd(nm + "headnorm_fwd", oraw, z, hn_w)
            h1, hn = mm_add_norm(nm + "mm_out", o, w_oab, xs, w_ffn)
            mixer_saved = (z, glr, oraw, o)
        else:
            gather_wait(G_QKV, xn)
            w_qkv = weights["w_qkv"].reshape((1,) + weights["w_qkv"].shape)
            w_oc = weights["w_oc"].reshape(1, 1, D, D)
            qkv = mm_nn(nm + "mm_qkv", xn, w_qkv, 0, BF16)
            o, lse = attn_fwd(nm + "attn_fwd", qkv)
            h1, hn = mm_add_norm(nm + "mm_out", o, w_oc, xs, w_ffn)
            mixer_saved = (qkv, o, lse)
        gather_wait(g_ffn(i)[0], hn)
        wg = weights[f"wg{i}"].reshape(1, N_DEV, fp, D)
        wu = weights[f"wu{i}"].reshape(1, N_DEV, fp, D)
        g, u, act = ffn_gate_up(nm + "ffn_gate_up", hn, wg, wu)
        gather_wait(g_ffn(i)[1], act)
        wd = weights[f"wd{i}"].reshape(1, 1, N_DEV * fp, D)
        wpg = weights[f"wpg{i}"].reshape(1, 1, D, D)
        wpp = weights[f"wpp{i}"].reshape((1,) + weights[f"wpp{i}"].shape)
        h2, pn = mm_add_norm(nm + "mm_down", act, wd, h1, w_ple)
        x_next, s, e = ple_fwd(nm + "ple", pn, wpg, p_bf[i], wpp, h2)
        mixer_w = (w_in_t, w_lr_t, w_oab, gu_full) if i % 2 == 0 else (w_qkv, w_oc)
        saved.append((xs, xn, mixer_saved, mixer_w, (wg, wu, wd, wpg), h1, hn, g, u, act, h2, pn, s, e))
        xs = x_next

    def loss_fn(a, w, t):
        diff = _rms(a, w) - t
        dx, dw = _rms_bwd(a, w, diff * (1.0 / D))
        part = 0.5 * jnp.sum(jnp.mean(diff * diff, axis=-1, keepdims=True), axis=0, keepdims=True)
        return dx, dw, jnp.broadcast_to(part, (1, LANE))

    dx, d_final_w, loss_part = rowwise("loss_head", loss_fn, T,
                                       [("row", xs), ("full", final_norm_w[None, :]), ("row", loss_target[0])],
                                       [("row", D, F32), ("acc", D), ("acc", LANE)])
    loss = lax.psum(loss_part[0, 0], ("x", "y", "c"))

    grads = {}
    on_chip = []
    scatters = []

    def scatter_start(name, keys, deps=()):
        waits = [("halves", on_chip[0][1])] if on_chip else []
        starts = [("halves", [(grads[k], lax.empty((N_CHIP,) + grads[k].shape[1:], BF16)) for k in keys])] if keys else []
        waited, handles, token = exchange_call(name, waits, starts, deps=deps)
        if on_chip:
            done_keys, _ = on_chip.pop()
            sums = [chip_sum(f"{name}_sum{j}", a, half) for j, (a, half) in enumerate(waited[0])]
            _, (handle,), token = exchange_call(
                name + "_chips", [], [("chip_sums", [(cs, lax.empty(cs.shape, BF16)) for cs in sums])])
            scatters.append((done_keys, handle, waited[0]))
        if keys:
            on_chip.append((keys, handles[0]))
        return token

    d_attn_w, d_ffn_w, d_ple_w = [None] * depth, [None] * depth, [None] * depth
    for i in reversed(range(depth)):
        nm = f"l{i}_b_"
        xs_i, xn, mixer_saved, mixer_w, (wg, wu, wd, wpg), h1, hn, g, u, act, h2, pn, s, e = saved[i]
        w_attn, w_ffn, w_ple = attn_norm_w[i:i + 1], ffn_norm_w[i:i + 1], ple_norm_w[i:i + 1]

        def ple_bwd(d, sv, ev):
            gate = _sigmoid(sv)
            return d * gate, d * ev * gate * (1.0 - gate)

        de, ds = rowwise(nm + "ple_out", ple_bwd, T, [("row", dx), ("row", s), ("row", e)],
                         [("row", D, BF16), ("row", D, BF16)], deps=(loss.reshape(1, 1),) if i == depth - 1 else ())
        grads[("ple_w_proj", i)] = mm_tn(nm + "mm_ple_proj_w", p_bf[i], de, N_DEV, BF16)
        grads[("ple_w_gate", i)] = mm_tn(nm + "mm_ple_gate_w", pn, ds, 1, BF16).reshape(N_DEV, D // N_DEV, D)
        dpn = mm_nt(nm + "mm_ple_gate_x", ds, wpg, 0, F32)

        def norm_bwd_add(a, w, dn, dres):
            dxx, dw = _rms_bwd(a, w, dn)
            tot = dres + dxx
            return tot, tot, dw

        dh2, dh2_bf, d_ple_w[i] = rowwise(nm + "norm_ple", norm_bwd_add, T,
                                          [("row", h2), ("full", w_ple), ("row", dpn), ("row", dx)],
                                          [("row", D, F32), ("row", D, BF16), ("acc", D)])
        grads[("ffn_w_down", i)] = mm_tn(nm + "mm_down_w", act, dh2_bf, 1, BF16).reshape(N_DEV, fp, D)
        token = scatter_start(nm + "scatter_ple_down", [("ple_w_proj", i), ("ple_w_gate", i), ("ffn_w_down", i)])
        dg, du = ffn_down_bwd(nm + "ffn_down_x", dh2_bf, wd, g, u, deps=(token,))
        grads[("ffn_w_gate", i)] = mmt_dw(nm + "mm_gate_w", dg, hn, N_DEV, BF16)
        grads[("ffn_w_up", i)] = mmt_dw(nm + "mm_up_w", du, hn, N_DEV, BF16)
        token = scatter_start(nm + "scatter_gate_up", [("ffn_w_gate", i), ("ffn_w_up", i)])
        dhn_g = mmt_dx_wide(nm + "mm_gate_x", dg, wg, F32, deps=(token,))
        dhn_u = mmt_dx_wide(nm + "mm_up_x", du, wu, F32)

        def norm_bwd_add2(a, w, dn1, dn2, dres):
            dxx, dw = _rms_bwd(a, w, dn1 + dn2)
            tot = dres + dxx
            return tot, tot, dw

        dh1, dh1_bf, d_ffn_w[i] = rowwise(nm + "norm_ffn", norm_bwd_add2, T,
                                          [("row", h1), ("full", w_ffn), ("row", dhn_g), ("row", dhn_u), ("row", dh2)],
                                          [("row", D, F32), ("row", D, BF16), ("acc", D)])
        if i % 2 == 0:
            z, glr, oraw, o = mixer_saved
            w_in_t, w_lr_t, w_oab, gu_full = mixer_w
            grads[("ab_w_out", 0)] = mm_tn(nm + "mm_out_w", o, dh1_bf, 1, BF16).reshape(N_DEV, D // N_DEV, D)
            token = scatter_start(nm + "scatter_out", [("ab_w_out", 0)])
            do = mm_nt(nm + "mm_out_x", dh1_bf, w_oab, 0, F32, deps=(token,))
            d_oraw, d_gates, d_hn_w = headnorm_bwd(nm + "headnorm", oraw, z, hn_w, do)
            d_rq, d_rk, d_rv = retention_bwd(nm + "ret", z, cosf, sins, d_oraw)
            d_gq, d_gk, d_gv, d_glr4, d_gu, d_gb = gla_bwd(nm + "gla", z, glr, gu_full, gb, d_oraw)
            dz = jnp.concatenate([d_rq, d_rk, d_rv, d_gates[:, :RET_V], d_gq, d_gk, d_gv, d_gates[:, RET_V:]], axis=1)
            (d_glr,) = rowwise(nm + "sum_lr", lambda *a: (a[0] + a[1] + a[2] + a[3],), T,
                               [("row", d_glr4[hh]) for hh in range(GLA_HEADS)], [("row", LANE, BF16)])
            dwt_in = mmt_dw(nm + "mm_in_w", dz, xn, 1, BF16, rows=in_width)
            dwt_in = mmt_dw_rows(nm + "mm_lr_w", d_glr, xn, dwt_in, OFF_LR, GLA_GATE_RANK)
            grads[("ab_w_in", 0)] = dwt_in.reshape(N_DEV, in_shard, D)
            token = scatter_start(nm + "scatter_in", [("ab_w_in", 0)])
            dxn_a = mmt_dx_wide(nm + "mm_in_x", dz, w_in_t, F32, n=OFF_LR, deps=(token,))
            token = scatter_start(nm + "scatter_in_on", [], deps=(dxn_a,))
            dxn_b = mmt_dx(nm + "mm_lr_x", d_glr, w_lr_t, 0, F32, deps=(token,))
        else:
            qkv, o, lse = mixer_saved
            w_qkv, w_oc = mixer_w
            grads[("c_w_out", 0)] = mm_tn(nm + "mm_out_w", o, dh1_bf, 1, BF16).reshape(N_DEV, D // N_DEV, D)
            do = mm_nt(nm + "mm_out_x", dh1_bf, w_oc, 0, BF16)
            dq, dk, dv = attn_bwd(nm + "attn", qkv, o, lse, do)
            dqkv = jnp.concatenate([dq, dk, dv], axis=1)
            grads[("c_w_qkv", 0)] = mm_tn(nm + "mm_qkv_w", xn, dqkv, N_DEV, BF16)
            token = scatter_start(nm + "scatter_attn", [("c_w_out", 0), ("c_w_qkv", 0)])
            dxn_a = mm_nt_wide(nm + "mm_qkv_x", dqkv, w_qkv, F32, deps=(token,))
            dxn_b = None
        dxn = [dxn_a] if dxn_b is None else [dxn_a, dxn_b]

        def norm_bwd_in(a, w, *rest):
            dxx, dw = _rms_bwd(a, w, sum(rest[1:-1], rest[0]))
            return rest[-1] + dxx, dw

        dx, d_attn_w[i] = rowwise(nm + "norm_attn", norm_bwd_in, T,
                                  [("row", xs_i), ("full", w_attn)] + [("row", d) for d in dxn] + [("row", dh1)],
                                  [("row", D, F32), ("acc", D)])

    small_names = ["attn_norm_w", "ffn_norm_w", "ple_norm_w", "final_norm_w", "ab_gla_gate_b", "ab_ret_norm_w",
                   "ab_gla_norm_w"]
    small_grads = [jnp.concatenate(d_attn_w, 0), jnp.concatenate(d_ffn_w, 0), jnp.concatenate(d_ple_w, 0), d_final_w[0],
                   d_gb, d_hn_w[:, :RET_V], d_hn_w[:, RET_V:]]
    small_w = [attn_norm_w, ffn_norm_w, ple_norm_w, final_norm_w, ab_gla_gate_b, ab_ret_norm_w, ab_gla_norm_w]
    small_m = [m_attn_norm_w, m_ffn_norm_w, m_ple_norm_w, m_final_norm_w, m_ab_gla_gate_b, m_ab_ret_norm_w, m_ab_gla_norm_w]
    small_v = [v_attn_norm_w, v_ffn_norm_w, v_ple_norm_w, v_final_norm_w, v_ab_gla_gate_b, v_ab_ret_norm_w, v_ab_gla_norm_w]
    sizes = [int(np.prod(a.shape)) for a in small_w]
    n_gu = GLA_GATE_RANK * GLA_QK
    n_small = _round_up(sum(sizes) + n_gu, LANE)
    pack = lambda parts: _pad_to(jnp.concatenate([a.reshape(-1) for a in parts]), 0, n_small)[None, :]
    small_part = pack(small_grads + [d_gu[:GLA_GATE_RANK]])

    cols_first = lambda a: jnp.transpose(a, (2, 0, 1))
    big_w = dict(ab_w_in=tuple(cols_first(a) for a in (ab_w_in, m_ab_w_in, v_ab_w_in)),
                 ab_w_out=(ab_w_out, m_ab_w_out, v_ab_w_out),
                 c_w_qkv=(c_w_qkv, m_c_w_qkv, v_c_w_qkv), c_w_out=(c_w_out, m_c_w_out, v_c_w_out),
                 ffn_w_gate=(wg_t, tr_(m_ffn_w_gate), tr_(v_ffn_w_gate)),
                 ffn_w_up=(wu_t, tr_(m_ffn_w_up), tr_(v_ffn_w_up)),
                 ffn_w_down=(ffn_w_down, m_ffn_w_down, v_ffn_w_down), ple_w_proj=(ple_w_proj, m_ple_w_proj, v_ple_w_proj),
                 ple_w_gate=(ple_w_gate, m_ple_w_gate, v_ple_w_gate))
    if on_chip:
        scatter_start("scatter_last", [], deps=(dx,))
    results, last = {}, dx
    for gi, (keys, handle, partials) in enumerate(scatters):
        (arrived,), _, _ = exchange_call(f"scatter_wait{gi}", [("chip_sums", handle)], [], deps=(last,))
        for (n, l), (_, land), (a, half) in zip(keys, arrived, partials):
            if n == "ab_w_in":
                results[n] = adamw_columns(f"adamw_{n}", *big_w[n], land, a, half)
            else:
                results[n] = adamw(f"adamw_{n}{l}", *big_w[n], l, land, a, half, prev=results.get(n))
            last = results[n][0]
    for n in ("ffn_w_gate", "ffn_w_up"):
        results[n] = [tr_(a) for a in results[n]]
    results["ab_w_in"] = [jnp.transpose(a, (1, 2, 0)) for a in results["ab_w_in"]]
    small_parts = gather_small("gather_small", small_part, deps=(last,)).reshape(N_DEV, n_small)

    gu_off = sum(sizes)
    own_cols = lambda a: lax.dynamic_slice_in_dim(a.reshape(GLA_GATE_RANK, GLA_QK), me * gu_cols, gu_cols, axis=1)
    small_res = adamw_small("adamw_small", pack(small_w + [jnp.zeros((n_gu,), F32)]),
                            pack(small_m + [jnp.zeros((n_gu,), F32)]), pack(small_v + [jnp.ones((n_gu,), F32)]),
                            small_parts)
    g_gu_full = small_res[0][0, gu_off:gu_off + n_gu]
    g_gu = own_cols(g_gu_full)[None]
    gu_res = adamw_small("adamw_gate_up", *[_pad_to(a.reshape(1, -1), 1, _round_up(a.size, LANE)) for a in
                                            (ab_gla_gate_up, m_ab_gla_gate_up, v_ab_gla_gate_up)],
                         jnp.concatenate([_pad_to(g_gu.reshape(1, -1), 1, _round_up(g_gu.size, LANE)),
                                          jnp.zeros((N_DEV - 1, _round_up(g_gu.size, LANE)), F32)], axis=0))
    for k in range(4):
        off = 0
        for n, a, sz in zip(small_names, small_w, sizes):
            results.setdefault(n, [None] * 4)[k] = small_res[k][0, off:off + sz].reshape(a.shape)
            off += sz
        results.setdefault("ab_gla_gate_up", [None] * 4)[k] = gu_res[k][0, :g_gu.size].reshape(ab_gla_gate_up.shape)

    order = ["attn_norm_w", "ffn_norm_w", "ple_norm_w", "final_norm_w", "ab_w_in", "ab_gla_gate_up", "ab_gla_gate_b",
             "ab_ret_norm_w", "ab_gla_norm_w", "ab_w_out", "c_w_qkv", "c_w_out", "ffn_w_gate", "ffn_w_up", "ffn_w_down",
             "ple_w_proj", "ple_w_gate"]
    return (loss, dx[None], *[results[n][0] for n in order], *[results[n][1] for n in order],
            *[results[n][2] for n in order], *[results[n][3] for n in order])
```

```python
import math

import numpy as np
import jax
import jax.numpy as jnp
from jax import lax
from jax.experimental import pallas as pl
from jax.experimental.pallas import tpu as pltpu

F32 = jnp.float32
BF16 = jnp.bfloat16
HIGHEST = lax.Precision.HIGHEST

N_DEV = 8
VMEM_LIMIT_BYTES = 48 * 1024 * 1024
LANE = 128
NORM_EPS = 1e-6

RET_HEADS, RET_DK, RET_DV = 4, 256, 256
RET_THETA_BASE = 10000.0
GLA_HEADS, GLA_DK, GLA_DV = 4, 128, 256
GLA_GATE_RANK = 16
GLA_GATE_NORM = 16.0
CHUNK = 64
ATT_HEADS = 16
DILATED_BRANCHES = ((128, 1), (512, 4), (2048, 16))
BLK = 256

ADAM_LR, ADAM_B1, ADAM_B2, ADAM_EPS, ADAM_WD, ADAM_STEP = 0.001, 0.9, 0.999, 1e-08, 0.01, 10

RET_QK = RET_HEADS * RET_DK
RET_V = RET_HEADS * RET_DV
GLA_QK = GLA_HEADS * GLA_DK
GLA_V = GLA_HEADS * GLA_DV
OFF_RQ, OFF_RK, OFF_RV, OFF_RG = 0, RET_QK, 2 * RET_QK, 2 * RET_QK + RET_V
OFF_GQ = OFF_RG + RET_V
OFF_GK = OFF_GQ + GLA_QK
OFF_GV = OFF_GK + GLA_QK
OFF_GG = OFF_GV + GLA_V
OFF_LR = OFF_GG + GLA_V


def _params(*sem):
    return pltpu.CompilerParams(dimension_semantics=sem or None, vmem_limit_bytes=VMEM_LIMIT_BYTES)


def _pick(n, cands):
    for c in cands:
        if n % c == 0:
            return c
    raise ValueError(f"no tile for {n} in {cands}")


_NN = (((1,), (0,)), ((), ()))
_NT = (((1,), (1,)), ((), ()))
_TN = (((0,), (0,)), ((), ()))
_ANY = pl.BlockSpec(memory_space=pl.ANY)
MAX_CONTRACT = 2048
_TILES = (1024, 768, 512, 256, 128)


def _mm_call(name, dims, grid, in_specs, out_spec, out_shape, args, deps=()):
    steps = grid[2]
    assert steps == 1 or out_shape.dtype == F32

    def body(a_ref, b_ref, *rest):
        o_ref = rest[len(deps)]
        part = lax.dot_general(a_ref[...].astype(BF16), b_ref[...].astype(BF16), dims, preferred_element_type=F32)
        if steps == 1:
            o_ref[...] = part.astype(o_ref.dtype)
        else:
            _accumulate(o_ref, part, pl.program_id(2) == 0)

    return pl.pallas_call(
        body, name=name, grid=grid, in_specs=list(in_specs) + [_ANY] * len(deps), out_specs=out_spec,
        out_shape=out_shape, compiler_params=_params("parallel", "parallel", "arbitrary"))(*args, *deps)


def mm_nn(name, a, w, l, out_dtype, deps=()):
    _, J, K, n = w.shape
    M = a.shape[0]
    tm, tn, tk = _pick(M, _TILES), _pick(n, _TILES), _pick(K, (MAX_CONTRACT,) + _TILES)
    nt = n // tn
    return _mm_call(
        name, _NN, (M // tm, J * nt, K // tk),
        [pl.BlockSpec((tm, tk), lambda i, j, k: (i, k)),
         pl.BlockSpec((None, None, tk, tn), lambda i, j, k: (l, j // nt, k, j % nt))],
        pl.BlockSpec((tm, tn), lambda i, j, k: (i, j)),
        jax.ShapeDtypeStruct((M, J * n), out_dtype), (a, w), deps)


def mm_nt(name, a, w, l, out_dtype, deps=()):
    _, J, K, n = w.shape
    M = a.shape[0]
    tm, tq, tc = _pick(M, _TILES), _pick(K, _TILES), _pick(n, (MAX_CONTRACT,) + _TILES)
    nc = n // tc
    return _mm_call(
        name, _NT, (M // tm, K // tq, J * nc),
        [pl.BlockSpec((tm, tc), lambda i, q, c: (i, c)),
         pl.BlockSpec((None, None, tq, tc), lambda i, q, c: (l, c // nc, q, c % nc))],
        pl.BlockSpec((tm, tq), lambda i, q, c: (i, q)),
        jax.ShapeDtypeStruct((M, K), out_dtype), (a, w), deps)


def mm_tn(name, x, dy, J, out_dtype, deps=()):
    M, K = x.shape
    n = dy.shape[1] // J
    tp, tn = _pick(K, _TILES), _pick(n, _TILES)
    nt = n // tn
    assert M <= MAX_CONTRACT
    return _mm_call(
        name, _TN, (K // tp, J * nt, 1),
        [pl.BlockSpec((M, tp), lambda i, j, r: (0, i)),
         pl.BlockSpec((M, tn), lambda i, j, r: (0, j))],
        pl.BlockSpec((None, tp, tn), lambda i, j, r: (j // nt, i, j % nt)),
        jax.ShapeDtypeStruct((J, K, n), out_dtype), (x, dy), deps)


def mmt_fwd(name, a, wt, l, out_dtype, n=None, deps=()):
    _, J, rows, K = wt.shape
    n = rows if n is None else n
    M = a.shape[0]
    tm, tn = _pick(M, _TILES), _pick(n, _TILES)
    nt = n // tn
    assert K <= MAX_CONTRACT
    return _mm_call(
        name, _NT, (M // tm, J * nt, 1),
        [pl.BlockSpec((tm, K), lambda i, j, k: (i, 0)),
         pl.BlockSpec((None, None, tn, K), lambda i, j, k: (l, j // nt, j % nt, 0))],
        pl.BlockSpec((tm, tn), lambda i, j, k: (i, j)),
        jax.ShapeDtypeStruct((M, J * n), out_dtype), (a, wt), deps)


def mmt_dx(name, dy, wt, l, out_dtype, n=None, deps=()):
    _, J, rows, K = wt.shape
    n = rows if n is None else n
    M = dy.shape[0]
    tm, tq, tc = _pick(M, _TILES), _pick(K, _TILES), _pick(n, _TILES)
    nc = n // tc
    return _mm_call(
        name, _NN, (M // tm, K // tq, J * nc),
        [pl.BlockSpec((tm, tc), lambda i, q, c: (i, c)),
         pl.BlockSpec((None, None, tc, tq), lambda i, q, c: (l, c // nc, c % nc, q))],
        pl.BlockSpec((tm, tq), lambda i, q, c: (i, q)),
        jax.ShapeDtypeStruct((M, K), out_dtype), (dy, wt), deps)


WIDE_TILE = 512


def _wide_call(name, body, M, K, a, w, a_spec, w_spec, out_dtype, deps):
    def kernel_body(a_ref, w_ref, *rest):
        o_ref = rest[len(deps)]
        o_ref[...] = body(a_ref, w_ref).astype(o_ref.dtype)

    return pl.pallas_call(
        kernel_body, name=name, grid=(M // WIDE_TILE, K // WIDE_TILE),
        in_specs=[a_spec, w_spec] + [_ANY] * len(deps),
        out_specs=pl.BlockSpec((WIDE_TILE, WIDE_TILE), lambda i, q: (i, q)),
        out_shape=jax.ShapeDtypeStruct((M, K), out_dtype),
        compiler_params=_params("parallel", "parallel"))(a, w, *deps)


def mmt_dx_wide(name, dy, wt, out_dtype, n=None, deps=()):
    _, J, rows, K = wt.shape
    n = rows if n is None else n
    M = dy.shape[0]

    def body(dy_ref, w_ref):
        return jnp.dot(dy_ref[...].astype(BF16), w_ref[...].reshape(J * n, WIDE_TILE), preferred_element_type=F32)

    return _wide_call(name, body, M, K, dy, wt,
                      pl.BlockSpec((WIDE_TILE, J * n), lambda i, q: (i, 0)),
                      pl.BlockSpec((None, J, n, WIDE_TILE), lambda i, q: (0, 0, 0, q)), out_dtype, deps)


def mmt_dx_pair(name, dy1, wt1, dy2, wt2, out_dtype, deps=()):
    _, J, n, K = wt1.shape
    M = dy1.shape[0]

    def body(dy1_ref, w1_ref, dy2_ref, w2_ref, *rest):
        o_ref = rest[len(deps)]
        acc = jnp.dot(dy1_ref[...], w1_ref[...].reshape(J * n, WIDE_TILE), preferred_element_type=F32)
        acc = acc + jnp.dot(dy2_ref[...], w2_ref[...].reshape(J * n, WIDE_TILE), preferred_element_type=F32)
        o_ref[...] = acc.astype(o_ref.dtype)

    rows = _once((WIDE_TILE, J * n), lambda i, q: (i, 0))
    cols = pl.BlockSpec((None, J, n, WIDE_TILE), lambda i, q: (0, 0, 0, q))
    return pl.pallas_call(
        body, name=name, grid=(M // WIDE_TILE, K // WIDE_TILE),
        in_specs=[rows, cols, rows, cols] + [_ANY] * len(deps),
        out_specs=pl.BlockSpec((WIDE_TILE, WIDE_TILE), lambda i, q: (i, q)),
        out_shape=jax.ShapeDtypeStruct((M, K), out_dtype),
        compiler_params=_params("parallel", "parallel"))(dy1, wt1, dy2, wt2, *deps)


def mm_nt_wide(name, a, w, out_dtype, deps=()):
    _, J, K, n = w.shape
    M = a.shape[0]

    def body(a_ref, w_ref):
        acc = None
        for j in range(J):
            part = lax.dot_general(a_ref[:, j * n:(j + 1) * n].astype(BF16), w_ref[j], _NT, preferred_element_type=F32)
            acc = part if acc is None else acc + part
        return acc

    return _wide_call(name, body, M, K, a, w,
                      pl.BlockSpec((WIDE_TILE, J * n), lambda i, q: (i, 0)),
                      pl.BlockSpec((None, J, WIDE_TILE, n), lambda i, q: (0, 0, q, 0)), out_dtype, deps)


def mmt_dw(name, dy, x, J, out_dtype, deps=(), rows=None):
    M, K = x.shape
    n = dy.shape[1] // J
    tn, tp = _pick(n, _TILES), _pick(K, _TILES)
    nt = n // tn
    assert M <= MAX_CONTRACT
    return _mm_call(
        name, _TN, (J * nt, K // tp, 1),
        [pl.BlockSpec((M, tn), lambda j, i, r: (0, j)),
         pl.BlockSpec((M, tp), lambda j, i, r: (0, i))],
        pl.BlockSpec((None, tn, tp), lambda j, i, r: (j // nt, j % nt, i)),
        jax.ShapeDtypeStruct((J, n if rows is None else rows, K), out_dtype), (dy, x), deps)


def mmt_dw_rows(name, dy, x, out, row0, rank):
    M, K = x.shape
    tp = _pick(K, _TILES)

    def body(dy_ref, x_ref, prev_ref, o_ref):
        del prev_ref
        full = lax.dot_general(dy_ref[...], x_ref[...], _TN, preferred_element_type=F32)
        o_ref[...] = full[:rank].astype(o_ref.dtype)

    return pl.pallas_call(
        body, name=name, grid=(K // tp,),
        in_specs=[pl.BlockSpec((M, dy.shape[1]), lambda i: (0, 0)), pl.BlockSpec((M, tp), lambda i: (0, i)), _ANY],
        out_specs=pl.BlockSpec((None, rank, tp), lambda i: (0, row0 // rank, i)),
        out_shape=jax.ShapeDtypeStruct(out.shape, out.dtype), input_output_aliases={2: 0},
        compiler_params=_params("parallel"))(dy, x, out)


def ffn_gate_up(name, a, wg, wu):
    _, J, n, K = wg.shape
    M = a.shape[0]
    tm, tn = _pick(M, _TILES), _pick(n, _TILES)
    nt = n // tn
    assert K <= MAX_CONTRACT

    def body(a_ref, wg_ref, wu_ref, g_ref, u_ref, act_ref):
        x = a_ref[...]
        g = lax.dot_general(x, wg_ref[...], _NT, preferred_element_type=F32)
        u = lax.dot_general(x, wu_ref[...], _NT, preferred_element_type=F32)
        g_ref[...] = g.astype(g_ref.dtype)
        u_ref[...] = u.astype(u_ref.dtype)
        act_ref[...] = (_silu_and_grad(g)[0] * u).astype(act_ref.dtype)

    w_spec = pl.BlockSpec((None, None, tn, K), lambda i, j: (0, j // nt, j % nt, 0))
    out = pl.BlockSpec((tm, tn), lambda i, j: (i, j))
    return pl.pallas_call(
        body, name=name, grid=(M // tm, J * nt),
        in_specs=[pl.BlockSpec((tm, K), lambda i, j: (i, 0)), w_spec, w_spec],
        out_specs=[out] * 3, out_shape=[jax.ShapeDtypeStruct((M, J * n), BF16)] * 3,
        compiler_params=_params("parallel", "parallel"))(a, wg, wu)


def mm_add_norm(name, a, w, res, norm_w):
    _, _, K, N = w.shape
    M = a.shape[0]
    tm, tk = _pick(M, (WIDE_TILE, 256)), _pick(K, (1024, 512, 256))
    steps = K // tk

    def body(a_ref, w_ref, res_ref, nw_ref, h_ref, hn_ref):
        k = pl.program_id(1)
        part = jnp.dot(a_ref[...], w_ref[...], preferred_element_type=F32)
        _accumulate(h_ref, part, k == 0)

        @pl.when(k == steps - 1)
        def _():
            h = h_ref[...] + res_ref[...]
            h_ref[...] = h
            hn_ref[...] = _rms(h, nw_ref[...]).astype(hn_ref.dtype)

    rows = pl.BlockSpec((tm, N), lambda i, k: (i, 0))
    return pl.pallas_call(
        body, name=name, grid=(M // tm, steps),
        in_specs=[pl.BlockSpec((tm, tk), lambda i, k: (i, k)),
                  pl.BlockSpec((None, None, tk, N), lambda i, k: (0, 0, k, 0)), rows,
                  pl.BlockSpec((1, N), lambda i, k: (0, 0))],
        out_specs=[rows, rows],
        out_shape=[jax.ShapeDtypeStruct((M, N), F32), jax.ShapeDtypeStruct((M, N), BF16)],
        compiler_params=_params("parallel", "arbitrary"))(a, w, res, norm_w)


def ple_fwd(name, pn, wpg, p_in, wpp, h):
    _, J, P, n = wpp.shape
    M, D = h.shape
    tm, tn = _pick(M, (WIDE_TILE, 256)), _pick(D, _TILES)
    per_tile = tn // n

    def body(pn_ref, wg_ref, p_ref, wp_ref, h_ref, x_ref, s_ref, e_ref):
        s = jnp.dot(pn_ref[...], wg_ref[...], preferred_element_type=F32)
        p_blk = p_ref[...]
        e = jnp.concatenate([jnp.dot(p_blk, wp_ref[j], preferred_element_type=F32) for j in range(per_tile)], axis=1)
        s_ref[...] = s
        e_ref[...] = e
        x_ref[...] = h_ref[...] + _sigmoid(s) * e

    tile = pl.BlockSpec((tm, tn), lambda i, j: (i, j))
    return pl.pallas_call(
        body, name=name, grid=(M // tm, D // tn),
        in_specs=[pl.BlockSpec((tm, D), lambda i, j: (i, 0)),
                  pl.BlockSpec((None, None, D, tn), lambda i, j: (0, 0, 0, j)),
                  pl.BlockSpec((tm, P), lambda i, j: (i, 0)),
                  pl.BlockSpec((None, per_tile, P, n), lambda i, j: (0, j, 0, 0)), tile],
        out_specs=[tile] * 3, out_shape=[jax.ShapeDtypeStruct((M, D), F32)] * 3,
        compiler_params=_params("parallel", "parallel"))(pn, wpg, p_in, wpp, h)


def ffn_down_bwd(name, dy, wd, g, u, deps=()):
    _, _, K, n = wd.shape
    M = dy.shape[0]
    tm, tq = _pick(M, _TILES), _pick(K, _TILES)
    assert n <= MAX_CONTRACT

    def body(dy_ref, w_ref, g_ref, u_ref, *rest):
        dg_ref, du_ref = rest[len(deps):]
        dact = lax.dot_general(dy_ref[...], w_ref[...], _NT, preferred_element_type=F32)
        silu, dsilu = _silu_and_grad(g_ref[...].astype(F32))
        dg_ref[...] = (dact * u_ref[...].astype(F32) * dsilu).astype(dg_ref.dtype)
        du_ref[...] = (dact * silu).astype(du_ref.dtype)

    blk = pl.BlockSpec((tm, tq), lambda i, q: (i, q))
    return pl.pallas_call(
        body, name=name, grid=(M // tm, K // tq),
        in_specs=[pl.BlockSpec((tm, n), lambda i, q: (i, 0)),
                  pl.BlockSpec((None, None, tq, n), lambda i, q: (0, 0, q, 0)), blk, blk] + [_ANY] * len(deps),
        out_specs=[blk, blk], out_shape=[jax.ShapeDtypeStruct((M, K), BF16)] * 2,
        compiler_params=_params("parallel", "parallel"))(dy, wd, g, u, *deps)


def rowwise(name, fn, rows, ins, outs, tr=256, deps=()):
    widest = max([s[1].shape[1] if s[0] != "col" else s[3] for s in ins] + [s[1] for s in outs])
    tr = min(tr if widest <= 2048 else tr // 2, rows)
    in_specs, args = [], []
    for spec in ins:
        kind, a = spec[0], spec[1]
        if kind == "row":
            in_specs.append(pl.BlockSpec((tr, a.shape[1]), lambda i: (i, 0)))
        elif kind == "col":
            cb, width = spec[2], spec[3]
            in_specs.append(pl.BlockSpec((tr, width), lambda i, cb=cb: (i, cb)))
        else:
            in_specs.append(pl.BlockSpec(a.shape, lambda i: (0, 0)))
        args.append(a)
    out_specs, out_shapes = [], []
    for spec in outs:
        if spec[0] == "row":
            out_specs.append(pl.BlockSpec((tr, spec[1]), lambda i: (i, 0)))
            out_shapes.append(jax.ShapeDtypeStruct((rows, spec[1]), spec[2]))
        else:
            out_specs.append(pl.BlockSpec((1, spec[1]), lambda i: (0, 0)))
            out_shapes.append(jax.ShapeDtypeStruct((1, spec[1]), F32))
    n_in = len(ins)

    def body(*refs):
        vals = fn(*[r[...] for r in refs[:n_in]])
        first = pl.program_id(0) == 0
        for r, v, spec in zip(refs[n_in + len(deps):], vals, outs):
            if spec[0] == "row":
                r[...] = v.astype(r.dtype)
            else:
                _accumulate(r, v, first)

    return pl.pallas_call(body, name=name, grid=(rows // tr,), in_specs=in_specs + [_ANY] * len(deps),
                          out_specs=out_specs, out_shape=out_shapes,
                          compiler_params=_params("arbitrary"))(*args, *deps)


def _accumulate(ref, v, first):
    @pl.when(first)
    def _():
        ref[...] = v

    @pl.when(jnp.logical_not(first))
    def _():
        ref[...] += v


def _rms(x, w):
    r = lax.rsqrt(jnp.mean(x * x, axis=-1, keepdims=True) + NORM_EPS)
    return x * r * w


def _rms_bwd(x, w, dy):
    r = lax.rsqrt(jnp.mean(x * x, axis=-1, keepdims=True) + NORM_EPS)
    g = dy * w
    dx = r * (g - x * (r * r) * jnp.mean(g * x, axis=-1, keepdims=True))
    dw = jnp.sum(dy * x * r, axis=0, keepdims=True)
    return dx, dw


def _sigmoid(x):
    return 1.0 / (1.0 + jnp.exp(-x))


def _silu_and_grad(g):
    s = _sigmoid(g)
    return g * s, s * (1.0 + g * (1.0 - s))


def _swap_pairs(x):
    n = x.shape[-1]
    lane = lax.broadcasted_iota(jnp.int32, x.shape, x.ndim - 1)
    return jnp.where((lane & 1) == 0, pltpu.roll(x, n - 1, x.ndim - 1), pltpu.roll(x, 1, x.ndim - 1))


def _rot(x, cosf, sins):
    return x * cosf + _swap_pairs(x) * sins


def _unrot(d, cosf, sins):
    return d * cosf + _swap_pairs(d * sins)


def _ret_log_gamma(h):
    vals = [math.log1p(-2.0 ** (-5.0 - i)) for i in range(RET_HEADS)]
    out = jnp.float32(vals[RET_HEADS - 1])
    for i in range(RET_HEADS - 2, -1, -1):
        out = jnp.where(h == i, jnp.float32(vals[i]), out)
    return out


def _fill_decays(dec_ref, lg):
    ri = lax.broadcasted_iota(jnp.int32, (BLK, BLK), 0)
    ci = lax.broadcasted_iota(jnp.int32, (BLK, BLK), 1)
    for d in range(dec_ref.shape[0]):
        dt = d * BLK + ri - ci
        dec_ref[d] = jnp.where(dt >= 0, jnp.exp(jnp.maximum(dt, 0).astype(F32) * lg), 0.0)


def _decay_row(dec_ref, qi):
    return jnp.concatenate([dec_ref[qi - kb] for kb in range(qi + 1)], axis=1)


def _once(block_shape, index_map):
    return pl.BlockSpec(block_shape, index_map, pipeline_mode=pl.Buffered(1))


def _dot(a, b):
    return jnp.dot(a.astype(BF16), b.astype(BF16), preferred_element_type=F32)


def _dot_nt(a, b):
    return lax.dot_general(a.astype(BF16), b.astype(BF16), _NT, preferred_element_type=F32)


def _dot_tn(a, b):
    return lax.dot_general(a.astype(BF16), b.astype(BF16), _TN, preferred_element_type=F32)


def retention_fwd(name, z, cosf, sins, width_out):
    T = z.shape[0]
    nq = T // BLK
    scale = RET_DK ** -0.5

    def body(q_ref, k_ref, v_ref, cos_ref, sin_ref, o_ref, krot, vb, dec_ref):
        _fill_decays(dec_ref, _ret_log_gamma(pl.program_id(0)))
        krot[...] = (_rot(k_ref[...], cos_ref[...], sin_ref[...]) * scale).astype(BF16)
        vb[...] = v_ref[...].astype(BF16)
        for qi in range(nq):
            rows, n = slice(qi * BLK, (qi + 1) * BLK), (qi + 1) * BLK
            q = _rot(q_ref[rows, :], cos_ref[rows, :], sin_ref[rows, :])
            s = _dot_nt(q, krot[0:n, :]) * _decay_row(dec_ref, qi)
            o_ref[rows, :] = _dot(s, vb[0:n, :])

    return pl.pallas_call(
        body, name=name, grid=(RET_HEADS,),
        in_specs=[pl.BlockSpec((T, RET_DK), lambda h: (0, OFF_RQ // RET_DK + h)),
                  pl.BlockSpec((T, RET_DK), lambda h: (0, OFF_RK // RET_DK + h)),
                  pl.BlockSpec((T, RET_DV), lambda h: (0, OFF_RV // RET_DV + h)),
                  _once((T, RET_DK), lambda h: (0, 0)), _once((T, RET_DK), lambda h: (0, 0))],
        out_specs=pl.BlockSpec((T, RET_DV), lambda h: (0, h)),
        out_shape=jax.ShapeDtypeStruct((T, width_out), F32),
        scratch_shapes=[pltpu.VMEM((T, RET_DK), BF16), pltpu.VMEM((T, RET_DV), BF16),
                        pltpu.VMEM((nq, BLK, BLK), F32)],
        compiler_params=_params("arbitrary"))(z, z, z, cosf, sins)


def retention_bwd(name, z, cosf, sins, do):
    T = z.shape[0]
    nq = T // BLK
    scale = RET_DK ** -0.5

    def body(q_ref, k_ref, v_ref, cos_ref, sin_ref, do_ref, dq_ref, dk_ref, dv_ref, krot, vb, dk_acc, dv_acc, dec_ref):
        _fill_decays(dec_ref, _ret_log_gamma(pl.program_id(0)))
        krot[...] = (_rot(k_ref[...], cos_ref[...], sin_ref[...]) * scale).astype(BF16)
        vb[...] = v_ref[...].astype(BF16)
        dk_acc[...] = jnp.zeros_like(dk_acc)
        dv_acc[...] = jnp.zeros_like(dv_acc)
        for qi in range(nq):
            rows, n = slice(qi * BLK, (qi + 1) * BLK), (qi + 1) * BLK
            cos_q, sin_q = cos_ref[rows, :], sin_ref[rows, :]
            q = _rot(q_ref[rows, :], cos_q, sin_q).astype(BF16)
            dout = do_ref[rows, :].astype(BF16)
            kk, vv, dec = krot[0:n, :], vb[0:n, :], _decay_row(dec_ref, qi)
            p = (_dot_nt(q, kk) * dec).astype(BF16)
            ds = (_dot_nt(dout, vv) * dec).astype(BF16)
            dq_ref[rows, :] = _unrot(_dot(ds, kk), cos_q, sin_q).astype(dq_ref.dtype)
            dk_acc[0:n, :] += _dot_tn(ds, q)
            dv_acc[0:n, :] += _dot_tn(p, dout)
        dk_ref[...] = (_unrot(dk_acc[...], cos_ref[...], sin_ref[...]) * scale).astype(dk_ref.dtype)
        dv_ref[...] = dv_acc[...].astype(dv_ref.dtype)

    head = lambda h: (0, h)
    return pl.pallas_call(
        body, name=name, grid=(RET_HEADS,),
        in_specs=[pl.BlockSpec((T, RET_DK), lambda h: (0, OFF_RQ // RET_DK + h)),
                  pl.BlockSpec((T, RET_DK), lambda h: (0, OFF_RK // RET_DK + h)),
                  pl.BlockSpec((T, RET_DV), lambda h: (0, OFF_RV // RET_DV + h)),
                  _once((T, RET_DK), lambda h: (0, 0)), _once((T, RET_DK), lambda h: (0, 0)),
                  pl.BlockSpec((T, RET_DV), head)],
        out_specs=[pl.BlockSpec((T, RET_DK), head), pl.BlockSpec((T, RET_DK), head), pl.BlockSpec((T, RET_DV), head)],
        out_shape=[jax.ShapeDtypeStruct((T, RET_QK), BF16), jax.ShapeDtypeStruct((T, RET_QK), BF16),
                   jax.ShapeDtypeStruct((T, RET_V), BF16)],
        scratch_shapes=[pltpu.VMEM((T, RET_DK), BF16), pltpu.VMEM((T, RET_DV), BF16),
                        pltpu.VMEM((T, RET_DK), F32), pltpu.VMEM((T, RET_DV), F32),
                        pltpu.VMEM((nq, BLK, BLK), F32)],
        compiler_params=_params("arbitrary"))(z, z, z, cosf, sins, do)


GLA_PAIR = 2


def _gla_chunk(q_ref, k_ref, v_ref, glr_ref, gu, gb, rows, hh, trilf):
    ck = slice(hh * GLA_DK, (hh + 1) * GLA_DK)
    zg = _dot(glr_ref[rows, :], gu[:, ck]) + gb[:, ck]
    la = (jnp.minimum(zg, 0.0) - jnp.log(1.0 + jnp.exp(-jnp.abs(zg)))) * (1.0 / GLA_GATE_NORM)
    cum = jnp.dot(trilf, la, precision=HIGHEST, preferred_element_type=F32)
    last = jnp.sum(la, axis=0, keepdims=True)
    ecum = jnp.exp(cum)
    k = k_ref[rows, ck]
    qt = q_ref[rows, ck] * (GLA_DK ** -0.5) * ecum
    kt = k * jnp.exp(-cum)
    kh = k * jnp.exp(last - cum)
    return zg, cum, last, ecum, qt, kt, kh, v_ref[rows, hh * GLA_DV:(hh + 1) * GLA_DV].astype(BF16)


def _state_decay(last):
    e = jnp.exp(jnp.broadcast_to(last, (GLA_DK, GLA_DK)).T)
    return jnp.concatenate([e] * (GLA_DV // GLA_DK), axis=1)


def _gla_specs(T):
    wk, wv = GLA_PAIR * GLA_DK, GLA_PAIR * GLA_DV
    return [_once((T, wk), lambda h: (0, OFF_GQ // wk + h)),
            _once((T, wk), lambda h: (0, OFF_GK // wk + h)),
            _once((T, wv), lambda h: (0, OFF_GV // wv + h)),
            _once((T, LANE), lambda h: (0, 0)),
            pl.BlockSpec((LANE, wk), lambda h: (0, h)),
            pl.BlockSpec((1, wk), lambda h: (0, h))]


def gla_fwd(name, z, glr, gu, gb, o_prev):
    T = z.shape[0]
    nc = T // CHUNK
    wv = GLA_PAIR * GLA_DV

    def body(q_ref, k_ref, v_ref, glr_ref, gu_ref, gb_ref, prev_ref, o_ref, *S):
        del prev_ref
        gu_b, gb_v = gu_ref[...].astype(BF16), gb_ref[...]
        ri = lax.broadcasted_iota(jnp.int32, (CHUNK, CHUNK), 0)
        ci = lax.broadcasted_iota(jnp.int32, (CHUNK, CHUNK), 1)
        tril = ri >= ci
        trilf = tril.astype(F32)
        for s_ref in S:
            s_ref[...] = jnp.zeros_like(s_ref)

        def step(c, carry):
            rows = pl.ds(pl.multiple_of(c * CHUNK, CHUNK), CHUNK)
            heads = range(GLA_PAIR)
            ch = [_gla_chunk(q_ref, k_ref, v_ref, glr_ref, gu_b, gb_v, rows, hh, trilf) for hh in heads]
            a = [jnp.where(tril, _dot_nt(ch[hh][4], ch[hh][5]), 0.0) for hh in heads]
            s_prev = [S[hh][...] for hh in heads]
            intra = [_dot(a[hh], ch[hh][7]) for hh in heads]
            inter = [_dot(ch[hh][4], s_prev[hh]) for hh in heads]
            added = [_dot_tn(ch[hh][6], ch[hh][7]) for hh in heads]
            for hh in heads:
                o_ref[rows, hh * GLA_DV:(hh + 1) * GLA_DV] = intra[hh] + inter[hh]
                S[hh][...] = s_prev[hh] * _state_decay(ch[hh][2]) + added[hh]
            return carry

        lax.fori_loop(0, nc, step, 0)

    n_in = 6
    return pl.pallas_call(
        body, name=name, grid=(GLA_HEADS // GLA_PAIR,),
        in_specs=_gla_specs(T) + [pl.BlockSpec(memory_space=pl.ANY)],
        out_specs=pl.BlockSpec((T, wv), lambda h: (0, RET_V // wv + h)),
        out_shape=jax.ShapeDtypeStruct(o_prev.shape, F32),
        scratch_shapes=[pltpu.VMEM((GLA_DK, GLA_DV), F32)] * GLA_PAIR,
        input_output_aliases={n_in: 0},
        compiler_params=_params("arbitrary"))(z, z, z, glr, gu, gb, o_prev)


def gla_bwd(name, z, glr, gu, gb, do):
    T = z.shape[0]
    nc = T // CHUNK

    def body(q_ref, k_ref, v_ref, glr_ref, gu_ref, gb_ref, do_ref,
             dq_ref, dk_ref, dv_ref, dglr_ref, dgu_ref, dgb_ref, s_all, dS):
        gu_b, gb_v = gu_ref[...].astype(BF16), gb_ref[...]
        ri = lax.broadcasted_iota(jnp.int32, (CHUNK, CHUNK), 0)
        ci = lax.broadcasted_iota(jnp.int32, (CHUNK, CHUNK), 1)
        tril = ri >= ci
        trilf = tril.astype(F32)
        triuf = (ri <= ci).astype(F32)
        last_row = lax.broadcasted_iota(jnp.int32, (CHUNK, GLA_DK), 0) == CHUNK - 1
        ones8 = jnp.ones((8, GLA_DV), F32)

        heads = range(GLA_PAIR)

        def fstep(c, carry):
            rows = pl.ds(pl.multiple_of(c * CHUNK, CHUNK), CHUNK)
            ch = [_gla_chunk(q_ref, k_ref, v_ref, glr_ref, gu_b, gb_v, rows, hh, trilf) for hh in heads]
            added = [_dot_tn(ch[hh][6], ch[hh][7]) for hh in heads]
            for hh in heads:
                s_prev = dS[hh]
                s_all[hh, c] = s_prev
                dS[hh] = s_prev * _state_decay(ch[hh][2]) + added[hh]
            return carry

        dS[...] = jnp.zeros_like(dS)
        lax.fori_loop(0, nc, fstep, 0)
        dS[...] = jnp.zeros_like(dS)
        dgu_ref[...] = jnp.zeros_like(dgu_ref)
        dgb_ref[...] = jnp.zeros_like(dgb_ref)

        def bstep(i, carry):
            c = nc - 1 - i
            rows = pl.ds(pl.multiple_of(c * CHUNK, CHUNK), CHUNK)
            glr_c = glr_ref[rows, :]
            cks = [slice(hh * GLA_DK, (hh + 1) * GLA_DK) for hh in heads]
            cvs = [slice(hh * GLA_DV, (hh + 1) * GLA_DV) for hh in heads]
            ch = [_gla_chunk(q_ref, k_ref, v_ref, glr_ref, gu_b, gb_v, rows, hh, trilf) for hh in heads]
            zg, cum, last, ecum, qt, kt, kh, v = [[ch[hh][j] for hh in heads] for j in range(8)]
            s_prev = [s_all[hh, c] for hh in heads]
            ds_new = [dS[hh] for hh in heads]
            dout = [do_ref[rows, cvs[hh]].astype(BF16) for hh in heads]
            a = [jnp.where(tril, _dot_nt(qt[hh], kt[hh]), 0.0) for hh in heads]
            da = [jnp.where(tril, _dot_nt(dout[hh], v[hh]), 0.0) for hh in heads]
            dv_a = [_dot_tn(a[hh], dout[hh]) for hh in heads]
            dv_b = [_dot(kh[hh], ds_new[hh]) for hh in heads]
            dqt_a = [_dot(da[hh], kt[hh]) for hh in heads]
            dqt_b = [_dot_nt(dout[hh], s_prev[hh]) for hh in heads]
            dkt = [_dot_tn(da[hh], qt[hh]) for hh in heads]
            dkh = [_dot_nt(v[hh], ds_new[hh]) for hh in heads]
            ds_add = [_dot_tn(qt[hh], dout[hh]) for hh in heads]
            rs = [lax.dot_general(ones8, ds_new[hh] * s_prev[hh], _NT, precision=HIGHEST, preferred_element_type=F32)
                  for hh in heads]
            dcum = []
            for hh in heads:
                dv_ref[rows, cvs[hh]] = (dv_a[hh] + dv_b[hh]).astype(dv_ref.dtype)
                dS[hh] = ds_new[hh] * _state_decay(last[hh]) + ds_add[hh]
                dqt = dqt_a[hh] + dqt_b[hh]
                dq_ref[rows, cks[hh]] = (dqt * ecum[hh] * (GLA_DK ** -0.5)).astype(dq_ref.dtype)
                dk_ref[rows, cks[hh]] = (dkt[hh] * jnp.exp(-cum[hh])
                                         + dkh[hh] * jnp.exp(last[hh] - cum[hh])).astype(dk_ref.dtype)
                dkh_kh = dkh[hh] * kh[hh]
                dlast = (jnp.sum(dkh_kh, axis=0, keepdims=True)
                         + jnp.exp(last[hh]) * (jnp.sum(rs[hh], axis=0, keepdims=True) * 0.125))
                dcum.append(dqt * qt[hh] - dkt[hh] * kt[hh] - dkh_kh + jnp.where(last_row, dlast, 0.0))
            dla = [jnp.dot(triuf, dcum[hh], precision=HIGHEST, preferred_element_type=F32) for hh in heads]
            dzg = [dla[hh] * (1.0 / GLA_GATE_NORM) * _sigmoid(-zg[hh]) for hh in heads]
            dglr = [_dot_nt(dzg[hh], gu_b[:, cks[hh]]) for hh in heads]
            dgu = [_dot_tn(glr_c, dzg[hh]) for hh in heads]
            for hh in heads:
                dglr_ref[hh, rows, :] = dglr[hh]
                dgu_ref[:, cks[hh]] += dgu[hh]
                dgb_ref[:, cks[hh]] += jnp.sum(dzg[hh], axis=0, keepdims=True)
            return carry

        lax.fori_loop(0, nc, bstep, 0)

    wk, wv = GLA_PAIR * GLA_DK, GLA_PAIR * GLA_DV
    return pl.pallas_call(
        body, name=name, grid=(GLA_HEADS // GLA_PAIR,),
        in_specs=_gla_specs(T) + [_once((T, wv), lambda h: (0, RET_V // wv + h))],
        out_specs=[pl.BlockSpec((T, wk), lambda h: (0, h)), pl.BlockSpec((T, wk), lambda h: (0, h)),
                   pl.BlockSpec((T, wv), lambda h: (0, h)),
                   pl.BlockSpec((GLA_PAIR, T, LANE), lambda h: (h, 0, 0)),
                   pl.BlockSpec((LANE, wk), lambda h: (0, h)), pl.BlockSpec((1, wk), lambda h: (0, h))],
        out_shape=[jax.ShapeDtypeStruct((T, GLA_QK), BF16), jax.ShapeDtypeStruct((T, GLA_QK), BF16),
                   jax.ShapeDtypeStruct((T, GLA_V), BF16), jax.ShapeDtypeStruct((GLA_HEADS, T, LANE), F32),
                   jax.ShapeDtypeStruct((LANE, GLA_QK), F32), jax.ShapeDtypeStruct((1, GLA_QK), F32)],
        scratch_shapes=[pltpu.VMEM((GLA_PAIR, nc, GLA_DK, GLA_DV), F32), pltpu.VMEM((GLA_PAIR, GLA_DK, GLA_DV), F32)],
        compiler_params=_params("arbitrary"))(z, z, z, glr, gu, gb, do)


HN_HEADS = RET_HEADS + GLA_HEADS
HN_W = RET_DV


def _gate_col(h):
    return jnp.where(h < RET_HEADS, OFF_RG // HN_W + h, OFF_GG // HN_W + h - RET_HEADS)


def headnorm_fwd(name, oraw, z, w):
    T = oraw.shape[0]
    tr = _pick(T, _TILES)

    def body(o_ref, g_ref, w_ref, y_ref):
        y_ref[...] = (_rms(o_ref[...], w_ref[...]) * _silu_and_grad(g_ref[...])[0]).astype(y_ref.dtype)

    return pl.pallas_call(
        body, name=name, grid=(HN_HEADS, T // tr),
        in_specs=[pl.BlockSpec((tr, HN_W), lambda h, i: (i, h)),
                  pl.BlockSpec((tr, HN_W), lambda h, i: (i, _gate_col(h))),
                  pl.BlockSpec((1, HN_W), lambda h, i: (0, h))],
        out_specs=pl.BlockSpec((tr, HN_W), lambda h, i: (i, h)),
        out_shape=jax.ShapeDtypeStruct((T, HN_HEADS * HN_W), BF16),
        compiler_params=_params("arbitrary", "arbitrary"))(oraw, z, w)


def headnorm_bwd(name, oraw, z, w, dy):
    T = oraw.shape[0]
    tr = _pick(T, _TILES)

    def body(o_ref, g_ref, w_ref, dy_ref, do_ref, dg_ref, dw_ref):
        o, wv, dyv = o_ref[...], w_ref[...], dy_ref[...].astype(F32)
        silu, dsilu = _silu_and_grad(g_ref[...])
        n = _rms(o, wv)
        dg_ref[...] = (dyv * n * dsilu).astype(dg_ref.dtype)
        dx, dw = _rms_bwd(o, wv, dyv * silu)
        do_ref[...] = dx
        _accumulate(dw_ref, dw, pl.program_id(1) == 0)

    blk = pl.BlockSpec((tr, HN_W), lambda h, i: (i, h))
    return pl.pallas_call(
        body, name=name, grid=(HN_HEADS, T // tr),
        in_specs=[blk, pl.BlockSpec((tr, HN_W), lambda h, i: (i, _gate_col(h))),
                  pl.BlockSpec((1, HN_W), lambda h, i: (0, h)), blk],
        out_specs=[blk, blk, pl.BlockSpec((1, HN_W), lambda h, i: (0, h))],
        out_shape=[jax.ShapeDtypeStruct((T, HN_HEADS * HN_W), F32),
                   jax.ShapeDtypeStruct((T, HN_HEADS * HN_W), BF16),
                   jax.ShapeDtypeStruct((1, HN_HEADS * HN_W), F32)],
        compiler_params=_params("arbitrary", "arbitrary"))(oraw, z, w, dy)


N_MASKS = 4


def _check_mask_classes(T):
    for window, dilation in DILATED_BRANCHES[:-1]:
        assert window < (N_MASKS - 1) * BLK - (BLK - 1) and BLK % dilation == 0
    assert DILATED_BRANCHES[-1][0] >= T and BLK % DILATED_BRANCHES[-1][1] == 0


def _fill_masks(mult_ref, bias_ref):
    ri = lax.broadcasted_iota(jnp.int32, (BLK, BLK), 0)
    ci = lax.broadcasted_iota(jnp.int32, (BLK, BLK), 1)
    for d in range(N_MASKS):
        dt = d * BLK + ri - ci
        mult = jnp.zeros((BLK, BLK), F32)
        for window, dilation in DILATED_BRANCHES:
            hit = (dt >= 0) & (dt <= window) & ((dt & (dilation - 1)) == 0)
            mult = mult + hit.astype(F32)
        mult_ref[d] = mult
        bias_ref[d] = jnp.where(mult > 0, 0.0, -1e30)


def _mask_row(ref, qi):
    return jnp.concatenate([ref[min(qi - kb, N_MASKS - 1)] for kb in range(qi + 1)], axis=1)


def attn_fwd(name, qkv):
    T = qkv.shape[0]
    D = qkv.shape[1] // 3
    dh = D // ATT_HEADS
    nq = T // BLK
    scale = dh ** -0.5

    _check_mask_classes(T)

    def body(q_ref, k_ref, v_ref, o_ref, lse_ref, mult_ref, bias_ref):
        @pl.when(pl.program_id(0) == 0)
        def _():
            _fill_masks(mult_ref, bias_ref)

        for q0 in range(0, nq, 2):
            qis = range(q0, min(q0 + 2, nq))
            rows = [slice(qi * BLK, (qi + 1) * BLK) for qi in qis]
            ns = [(qi + 1) * BLK for qi in qis]
            s = [_dot_nt(q_ref[r, :], k_ref[0:n, :]) for r, n in zip(rows, ns)]
            s = [x * scale + _mask_row(bias_ref, qi) for x, qi in zip(s, qis)]
            m = [jnp.max(x, axis=-1, keepdims=True) for x in s]
            p = [_mask_row(mult_ref, qi) * jnp.exp(x - mx) for qi, x, mx in zip(qis, s, m)]
            l = [jnp.sum(x, axis=-1, keepdims=True) for x in p]
            pv = [_dot(x, v_ref[0:n, :]) for x, n in zip(p, ns)]
            for r, x, lx, mx in zip(rows, pv, l, m):
                o_ref[r, :] = (x / lx).astype(o_ref.dtype)
                lse_ref[r, :] = jnp.broadcast_to(mx + jnp.log(lx), (BLK, LANE))

    return pl.pallas_call(
        body, name=name, grid=(ATT_HEADS,),
        in_specs=[pl.BlockSpec((T, dh), lambda h: (0, h)),
                  pl.BlockSpec((T, dh), lambda h: (0, ATT_HEADS + h)),
                  pl.BlockSpec((T, dh), lambda h: (0, 2 * ATT_HEADS + h))],
        out_specs=[pl.BlockSpec((T, dh), lambda h: (0, h)),
                   pl.BlockSpec((None, T, LANE), lambda h: (h, 0, 0))],
        out_shape=[jax.ShapeDtypeStruct((T, D), BF16), jax.ShapeDtypeStruct((ATT_HEADS, T, LANE), F32)],
        scratch_shapes=[pltpu.VMEM((N_MASKS, BLK, BLK), F32), pltpu.VMEM((N_MASKS, BLK, BLK), F32)],
        compiler_params=_params("arbitrary"))(qkv, qkv, qkv)


def attn_bwd(name, qkv, o, lse, do):
    T = qkv.shape[0]
    D = qkv.shape[1] // 3
    dh = D // ATT_HEADS
    nq = T // BLK
    scale = dh ** -0.5

    _check_mask_classes(T)

    def body(q_ref, k_ref, v_ref, o_ref, lse_ref, do_ref, dq_ref, dk_ref, dv_ref, dk_acc, dv_acc, mult_ref, bias_ref):
        @pl.when(pl.program_id(0) == 0)
        def _():
            _fill_masks(mult_ref, bias_ref)

        dk_acc[...] = jnp.zeros_like(dk_acc)
        dv_acc[...] = jnp.zeros_like(dv_acc)
        for qi in range(nq):
            rows, n = slice(qi * BLK, (qi + 1) * BLK), (qi + 1) * BLK
            q, dout = q_ref[rows, :], do_ref[rows, :]
            kk, vv = k_ref[0:n, :], v_ref[0:n, :]
            delta = jnp.sum(dout.astype(F32) * o_ref[rows, :].astype(F32), axis=-1, keepdims=True)
            lse = jnp.max(lse_ref[rows, :], axis=-1, keepdims=True)
            s = _dot_nt(q, kk) * scale + _mask_row(bias_ref, qi)
            p = _mask_row(mult_ref, qi) * jnp.exp(s - lse)
            ds = (p * (_dot_nt(dout, vv) - delta) * scale).astype(BF16)
            dq_ref[rows, :] = _dot(ds, kk).astype(dq_ref.dtype)
            dk_acc[0:n, :] += _dot_tn(ds, q)
            dv_acc[0:n, :] += _dot_tn(p, dout)
        dk_ref[...] = dk_acc[...].astype(dk_ref.dtype)
        dv_ref[...] = dv_acc[...].astype(dv_ref.dtype)

    full = pl.BlockSpec((T, dh), lambda h: (0, h))
    return pl.pallas_call(
        body, name=name, grid=(ATT_HEADS,),
        in_specs=[full, pl.BlockSpec((T, dh), lambda h: (0, ATT_HEADS + h)),
                  pl.BlockSpec((T, dh), lambda h: (0, 2 * ATT_HEADS + h)),
                  full, pl.BlockSpec((None, T, LANE), lambda h: (h, 0, 0)), full],
        out_specs=[full, full, full],
        out_shape=[jax.ShapeDtypeStruct((T, D), BF16)] * 3,
        scratch_shapes=[pltpu.VMEM((T, dh), F32), pltpu.VMEM((T, dh), F32),
                        pltpu.VMEM((N_MASKS, BLK, BLK), F32), pltpu.VMEM((N_MASKS, BLK, BLK), F32)],
        compiler_params=_params("arbitrary"))(qkv, qkv, qkv, o, lse, do)


def _mesh_pos():
    mx, my, mc = lax.axis_index("x"), lax.axis_index("y"), lax.axis_index("c")
    return mx, my, mc, 4 * mx + 2 * my + mc


def _peer(k, mx, my, mc):
    px, py, pc = mx ^ (k >> 2), my ^ ((k >> 1) & 1), mc ^ (k & 1)
    return (px, py, pc), 4 * px + 2 * py + pc


_SIBLING = 1
_OTHER_CHIPS = (4, 2, 6)
N_CHIP = N_DEV // 2
_PLANS = {"gather": (2, N_DEV - 1), "to_chips": (2, 1 + len(_OTHER_CHIPS)), "pass_on": (1, len(_OTHER_CHIPS)),
          "halves": (2, N_CHIP), "chip_sums": (2, len(_OTHER_CHIPS))}


def _copies(kind, items, send_sems, recv_sems):
    mx, my, mc, me = _mesh_pos()
    out = []

    def add(n, src, dst, peer):
        out.append(pltpu.make_async_remote_copy(
            src_ref=src, dst_ref=dst, send_sem=send_sems.at[n], recv_sem=recv_sems.at[n],
            device_id=peer, device_id_type=pl.DeviceIdType.MESH))

    per_item = _PLANS[kind][1]
    sibling = _peer(_SIBLING, mx, my, mc)[0]
    for i, refs in enumerate(items):
        n = i * per_item
        if kind == "gather":
            for k in range(1, N_DEV):
                add(n + k - 1, refs[0], refs[1].at[me], _peer(k, mx, my, mc)[0])
        elif kind == "to_chips":
            rows = refs[0].shape[0]
            dst = refs[1].at[me] if rows == refs[1].shape[1] else refs[1].at[me, pl.ds(0, rows)]
            for j, k in enumerate((_SIBLING,) + _OTHER_CHIPS):
                add(n + j, refs[0], dst, _peer(k, mx, my, mc)[0])
        elif kind == "pass_on":
            for j, k in enumerate(_OTHER_CHIPS):
                add(n + j, refs[0].at[me ^ k], refs[0].at[me ^ k], sibling)
        elif kind == "halves":
            for chip in range(N_CHIP):
                add(n + chip, refs[0].at[2 * chip + 1 - mc], refs[1].at[chip], sibling)
        else:
            for j, k in enumerate(_OTHER_CHIPS):
                peer, to = _peer(k, mx, my, mc)
                add(n + j, refs[0].at[to // 2], refs[1].at[me // 2], peer)
    return out


_HBM = pl.BlockSpec(memory_space=pltpu.HBM)
_SEM = pl.BlockSpec(memory_space=pltpu.SEMAPHORE)
_DATAFLOW = pltpu.SideEffectType.DATAFLOW_SIDE_EFFECTING


def exchange_call(name, waits, starts, deps=()):
    bufs, slot_of = [], {}

    def slots(items):
        out = []
        for item in items:
            for b in item:
                if id(b) not in slot_of:
                    slot_of[id(b)] = len(bufs)
                    bufs.append(b)
            out.append(tuple(slot_of[id(b)] for b in item))
        return out

    wait_plan = [(kind, slots(handle[0])) for kind, handle in waits]
    start_plan = [(kind, slots(items)) for kind, items in starts]
    wait_sems = [s for _, handle in waits for s in handle[1:]]
    n_buf, n_ws, n_start = len(bufs), len(wait_sems), len(starts)

    def body(*refs):
        buf_refs, sems_in = refs[:n_buf], refs[n_buf:n_buf + n_ws]
        outs = refs[n_buf + n_ws + len(deps):]
        pick = lambda plan: [tuple(buf_refs[s] for s in item) for item in plan]
        for wi, (kind, plan) in enumerate(wait_plan):
            copies = _copies(kind, pick(plan), sems_in[2 * wi], sems_in[2 * wi + 1])
            for cp in copies:
                cp.wait_send()
            for cp in copies:
                cp.wait_recv()
        for si, (kind, plan) in enumerate(start_plan):
            for cp in _copies(kind, pick(plan), outs[2 * si], outs[2 * si + 1]):
                cp.start()
        outs[-1][...] = jnp.zeros_like(outs[-1])

    hbm_bufs = [pltpu.with_memory_space_constraint(b, pltpu.HBM) for b in bufs]
    sem_shapes = []
    for kind, plan in start_plan:
        sem_shapes += [pltpu.SemaphoreType.DMA((len(plan) * _PLANS[kind][1],))] * 2
    outs = pl.pallas_call(
        body, name=name,
        out_shape=sem_shapes + [pltpu.HBM(b.shape, b.dtype) for b in bufs] + [jax.ShapeDtypeStruct((8, LANE), F32)],
        in_specs=[_HBM] * n_buf + [_SEM] * n_ws + [_ANY] * len(deps),
        out_specs=[_SEM] * (2 * n_start) + [_HBM] * n_buf + [pl.BlockSpec(memory_space=pltpu.VMEM)],
        input_output_aliases={i: 2 * n_start + i for i in range(n_buf)},
        compiler_params=pltpu.CompilerParams(has_side_effects=_DATAFLOW))(*hbm_bufs, *wait_sems, *deps)
    sems, thru, token = outs[:2 * n_start], outs[2 * n_start:-1], outs[-1]
    through = lambda plan: [tuple(thru[s] for s in item) for item in plan]
    waited = [through(plan) for _, plan in wait_plan]
    handles = [(through(plan), sems[2 * si], sems[2 * si + 1]) for si, (_, plan) in enumerate(start_plan)]
    return waited, handles, token


def gather_small(name, a, deps=()):
    def body(a_ref, *rest):
        o_ref, send_sems, recv_sems, local_sem = rest[len(deps):]
        me = _mesh_pos()[3]
        own = pltpu.make_async_copy(a_ref, o_ref.at[me], local_sem)
        own.start()
        copies = _copies("gather", [(a_ref, o_ref)], send_sems, recv_sems)
        for cp in copies:
            cp.start()
        for cp in copies:
            cp.wait_recv()
        for cp in copies:
            cp.wait_send()
        own.wait()

    return pl.pallas_call(
        body, name=name, in_specs=[_ANY] * (1 + len(deps)), out_specs=_ANY,
        out_shape=jax.ShapeDtypeStruct((N_DEV,) + a.shape, a.dtype),
        scratch_shapes=[pltpu.SemaphoreType.DMA((N_DEV - 1,)), pltpu.SemaphoreType.DMA((N_DEV - 1,)),
                        pltpu.SemaphoreType.DMA],
        compiler_params=pltpu.CompilerParams(has_side_effects=True))(a, *deps)


def _adamw_math(w, g, m, v):
    m2 = ADAM_B1 * m + (1.0 - ADAM_B1) * g
    v2 = ADAM_B2 * v + (1.0 - ADAM_B2) * (g * g)
    m_hat = m2 / (1.0 - ADAM_B1 ** ADAM_STEP)
    v_hat = v2 / (1.0 - ADAM_B2 ** ADAM_STEP)
    delta = -ADAM_LR * (m_hat / (jnp.sqrt(v_hat) + ADAM_EPS) + ADAM_WD * w)
    return delta, m2, v2


def chip_sum(name, a, half):
    _, r, c = a.shape
    tr = r
    chip = 2 * lax.axis_index("x") + lax.axis_index("y")
    where = jnp.stack([lax.axis_index("c"), chip ^ 1, chip ^ 2, chip ^ 3]).astype(jnp.int32)

    def body(where_ref, a_ref, h_ref, o_ref):
        del where_ref
        o_ref[...] = (a_ref[...].astype(F32) + h_ref[...].astype(F32)).astype(o_ref.dtype)

    blk = pl.BlockSpec((None, tr, c), lambda g, i, where: (where[1 + g], i, 0))
    grid_spec = pltpu.PrefetchScalarGridSpec(
        num_scalar_prefetch=1, grid=(N_CHIP - 1, r // tr),
        in_specs=[pl.BlockSpec((None, None, tr, c), lambda g, i, where: (where[1 + g], where[0], i, 0)), blk],
        out_specs=blk)
    return pl.pallas_call(
        body, name=name, grid_spec=grid_spec, out_shape=jax.ShapeDtypeStruct((N_CHIP, r, c), BF16),
        compiler_params=_params("parallel", "parallel"))(where, a.reshape(N_CHIP, 2, r, c), half)


def adamw(name, w, m, v, l, land, a, half, prev=None):
    L, r, c = w.shape
    cp = land.shape[2]
    tr = _pick(r, (256, 176, 128, 64, 32, 16, 8))

    def body(w_ref, m_ref, v_ref, land_ref, a_ref, half_ref, *rest):
        g_ref, d_ref, m2_ref, v2_ref = rest[-4:]
        chip = _mesh_pos()[3] // 2
        mine = a_ref[:, pl.ds(0, c)].astype(F32) + half_ref[:, pl.ds(0, c)].astype(F32)
        g = None
        for s in range(N_CHIP):
            part = jnp.where(chip == s, mine, land_ref[s, :, pl.ds(0, c)].astype(F32))
            g = part if g is None else g + part
        delta, m2, v2 = _adamw_math(w_ref[...], g, m_ref[...], v_ref[...])
        g_ref[...] = g
        d_ref[...] = delta
        m2_ref[...] = m2
        v2_ref[...] = v2

    blk = pl.BlockSpec((None, tr, c), lambda i: (l, i, 0))
    shape = jax.ShapeDtypeStruct((L, r, c), F32)
    extra = [] if prev is None else list(prev)
    return pl.pallas_call(
        body, name=name, grid=(r // tr,),
        in_specs=[blk, blk, blk, pl.BlockSpec((N_CHIP, tr, cp), lambda i: (0, i, 0)),
                  pl.BlockSpec((None, tr, cp), lambda i: (_mesh_pos()[3], i, 0)),
                  pl.BlockSpec((None, tr, cp), lambda i: (_mesh_pos()[3] // 2, i, 0))] + [_ANY] * len(extra),
        out_specs=[blk] * 4, out_shape=[shape] * 4,
        input_output_aliases={6 + k: k for k in range(len(extra))},
        compiler_params=_params("parallel"))(w, m, v, land, a, half, *extra)


def adamw_columns(name, w, m, v, land, a, half):
    r, _, D = w.shape
    tc = _pick(D, (256, 128))

    def body(w_ref, m_ref, v_ref, land_ref, a_ref, half_ref, g_ref, d_ref, m2_ref, v2_ref):
        chip = _mesh_pos()[3] // 2
        mine = a_ref[...].astype(F32) + half_ref[...].astype(F32)
        g = None
        for s in range(N_CHIP):
            part = jnp.where(chip == s, mine, land_ref[s].astype(F32))
            g = part if g is None else g + part
        flat = lambda ref: ref[...].reshape(r, tc)
        delta, m2, v2 = _adamw_math(flat(w_ref), g, flat(m_ref), flat(v_ref))
        for ref, val in ((g_ref, g), (d_ref, delta), (m2_ref, m2), (v2_ref, v2)):
            ref[...] = val.reshape(r, 1, tc)

    blk = pl.BlockSpec((r, 1, tc), lambda i: (0, 0, i))
    shape = jax.ShapeDtypeStruct((r, 1, D), F32)
    return pl.pallas_call(
        body, name=name, grid=(D // tc,),
        in_specs=[blk, blk, blk, pl.BlockSpec((N_CHIP, r, tc), lambda i: (0, 0, i)),
                  pl.BlockSpec((None, r, tc), lambda i: (_mesh_pos()[3], 0, i)),
                  pl.BlockSpec((None, r, tc), lambda i: (_mesh_pos()[3] // 2, 0, i))],
        out_specs=[blk] * 4, out_shape=[shape] * 4,
        compiler_params=_params("parallel"))(w, m, v, land, a, half)


def adamw_small(name, w, m, v, parts):
    n = w.shape[1]

    def body(w_ref, m_ref, v_ref, p_ref, g_ref, d_ref, m2_ref, v2_ref):
        g = p_ref[0:1, :]
        for s in range(1, N_DEV):
            g = g + p_ref[s:s + 1, :]
        delta, m2, v2 = _adamw_math(w_ref[...], g, m_ref[...], v_ref[...])
        g_ref[...] = g
        d_ref[...] = delta
        m2_ref[...] = m2
        v2_ref[...] = v2

    shape = jax.ShapeDtypeStruct((1, n), F32)
    return pl.pallas_call(body, name=name, out_shape=[shape] * 4,
                          compiler_params=pltpu.CompilerParams(vmem_limit_bytes=VMEM_LIMIT_BYTES))(w, m, v, parts)


def _rope_tables(positions):
    half = RET_DK // 2
    inv_freq = 1.0 / jnp.power(RET_THETA_BASE, jnp.linspace(0.0, 1.0, half, dtype=F32))
    ang = positions.astype(F32)[:, None] * inv_freq
    cos, sin = jnp.cos(ang), jnp.sin(ang)
    cosf = jnp.repeat(cos, 2, axis=-1)
    sins = jnp.stack([-sin, sin], axis=-1).reshape(cosf.shape)
    return cosf, sins


def _pad_to(a, axis, size):
    pad = [(0, 0)] * a.ndim
    pad[axis] = (0, size - a.shape[axis])
    return jnp.pad(a, pad)


def _round_up(n, m):
    return -(-n // m) * m


def kernel(x, p, positions, attn_norm_w, ffn_norm_w, ple_norm_w, final_norm_w, ab_w_in, ab_gla_gate_up, ab_gla_gate_b, ab_ret_norm_w, ab_gla_norm_w, ab_w_out, c_w_qkv, c_w_out, ffn_w_gate, ffn_w_up, ffn_w_down, ple_w_proj, ple_w_gate, loss_target, m_attn_norm_w, m_ffn_norm_w, m_ple_norm_w, m_final_norm_w, m_ab_w_in, m_ab_gla_gate_up, m_ab_gla_gate_b, m_ab_ret_norm_w, m_ab_gla_norm_w, m_ab_w_out, m_c_w_qkv, m_c_w_out, m_ffn_w_gate, m_ffn_w_up, m_ffn_w_down, m_ple_w_proj, m_ple_w_gate, v_attn_norm_w, v_ffn_norm_w, v_ple_norm_w, v_final_norm_w, v_ab_w_in, v_ab_gla_gate_up, v_ab_gla_gate_b, v_ab_ret_norm_w, v_ab_gla_norm_w, v_ab_w_out, v_c_w_qkv, v_c_w_out, v_ffn_w_gate, v_ffn_w_up, v_ffn_w_down, v_ple_w_proj, v_ple_w_gate):
    T, D = x.shape[1], x.shape[2]
    depth = attn_norm_w.shape[0]
    assert ab_w_in.shape[0] == 1 and c_w_qkv.shape[0] == 1 and depth == 2, "one even and one odd layer"
    me = 4 * lax.axis_index("x") + 2 * lax.axis_index("y") + lax.axis_index("c")
    in_shard = ab_w_in.shape[2]
    in_width = in_shard * N_DEV
    assert in_width == OFF_LR + GLA_GATE_RANK
    fs = ffn_w_gate.shape[2]
    fp = _round_up(fs, LANE)
    gu_cols = ab_gla_gate_up.shape[2]

    bf = lambda a: a.astype(BF16)
    tr_ = lambda a: jnp.swapaxes(a, -1, -2)
    wg_t, wu_t = tr_(ffn_w_gate), tr_(ffn_w_up)
    srcs = {"w_in": bf(tr_(ab_w_in[0]))}
    group_keys = [["w_in"], ["gu", "w_oab"], ["wg0", "wu0"], ["wd0", "wpg0", "wpp0"], ["w_qkv", "w_oc"],
                  ["wg1", "wu1"], ["wd1", "wpg1", "wpp1"]]
    G_IN, G_OUT, G_QKV = 0, 1, 4
    g_ffn = lambda layer: (2, 3) if layer == 0 else (5, 6)

    def landing(key):
        a = srcs[key]
        rows = fp if key[:2] in ("wg", "wu", "wd") else a.shape[0]
        buf = lax.empty((N_DEV, rows) + a.shape[1:], a.dtype)
        if rows > a.shape[0]:
            zeros = jnp.zeros((N_DEV, rows - a.shape[0]) + a.shape[1:], a.dtype)
            buf = lax.dynamic_update_slice(buf, zeros, (0, a.shape[0]) + (0,) * (a.ndim - 1))
        return lax.dynamic_update_slice(buf, a[None], (me,) + (0,) * a.ndim)

    _, chip_handles, gather_token = exchange_call(
        "gather_start_in", [], [("to_chips", [(srcs[k], landing(k)) for k in group_keys[G_IN]])])
    (gather_token, w_out_, gu_, w_qkv_, w_oc_, wg_, wu_, wd_, wpg_, wpp_) = lax.optimization_barrier(
        (gather_token, ab_w_out, ab_gla_gate_up, c_w_qkv, c_w_out, wg_t, wu_t, ffn_w_down, ple_w_gate, ple_w_proj))
    srcs.update(w_oab=bf(w_out_[0]), gu=gu_[0], w_qkv=bf(w_qkv_[0]), w_oc=bf(w_oc_[0]))
    for l in range(depth):
        srcs[f"wg{l}"] = bf(wg_[l])
        srcs[f"wu{l}"] = bf(wu_[l])
        srcs[f"wd{l}"] = bf(wd_[l])
        srcs[f"wpg{l}"] = bf(wpg_[l])
        srcs[f"wpp{l}"] = bf(wpp_[l])
    _, more, gather_token = exchange_call(
        "gather_start", [], [("to_chips", [(srcs[k], landing(k)) for k in keys]) for keys in group_keys[1:]],
        deps=(gather_token,))
    chip_handles = chip_handles + more
    weights = {}

    def gather_wait(gi, dep):
        lands = [(land,) for _, land in chip_handles[gi][0]]
        _, (passing,), _ = exchange_call(
            f"gather{gi}_pass", [("to_chips", chip_handles[gi])], [("pass_on", lands)], deps=(dep,))
        (complete,), _, _ = exchange_call(f"gather{gi}_done", [("pass_on", passing)], [])
        weights.update(zip(group_keys[gi], [land for (land,) in complete]))

    gb = ab_gla_gate_b
    hn_w = jnp.concatenate([ab_ret_norm_w, ab_gla_norm_w], axis=1)
    cosf, sins = _rope_tables(positions[0])
    p_bf = bf(p[:, 0])

    xs = x[0]
    saved = []
    for i in range(depth):
        nm = f"l{i}_"
        w_attn, w_ffn, w_ple = attn_norm_w[i:i + 1], ffn_norm_w[i:i + 1], ple_norm_w[i:i + 1]
        (xn,) = rowwise(nm + "norm_attn", lambda a, w: (_rms(a, w),), T, [("row", xs), ("full", w_attn)],
                        [("row", D, BF16)], deps=(gather_token,) if i == 0 else ())
        if i % 2 == 0:
            gather_wait(G_IN, xn)
            w_in_t = weights["w_in"].reshape(1, 1, in_width, D)
            w_lr_t = _pad_to(w_in_t[0, 0, OFF_LR:], 0, LANE).reshape(1, 1, LANE, D)
            z = mmt_fwd(nm + "mm_in", xn, w_in_t, 0, F32, n=OFF_LR)
            glr = mmt_fwd(nm + "mm_lr", xn, w_lr_t, 0, F32)
            oraw = retention_fwd(nm + "ret_fwd", z, cosf, sins, RET_V + GLA_V)
            gather_wait(G_OUT, oraw)
            w_oab = weights["w_oab"].reshape(1, 1, D, D)
            gu_full = _pad_to(weights["gu"].transpose(1, 0, 2).reshape(GLA_GATE_RANK, GLA_QK), 0, LANE)
            oraw = gla_fwd(nm + "gla_fwd", z, glr, gu_full, gb, oraw)
            o = headnorm_fwd(nm + "headnorm_fwd", oraw, z, hn_w)
            h1, hn = mm_add_norm(nm + "mm_out", o, w_oab, xs, w_ffn)
            mixer_saved = (z, glr, oraw, o)
        else:
            gather_wait(G_QKV, xn)
            w_qkv = weights["w_qkv"].reshape((1,) + weights["w_qkv"].shape)
            w_oc = weights["w_oc"].reshape(1, 1, D, D)
            qkv = mm_nn(nm + "mm_qkv", xn, w_qkv, 0, BF16)
            o, lse = attn_fwd(nm + "attn_fwd", qkv)
            h1, hn = mm_add_norm(nm + "mm_out", o, w_oc, xs, w_ffn)
            mixer_saved = (qkv, o, lse)
        gather_wait(g_ffn(i)[0], hn)
        wg = weights[f"wg{i}"].reshape(1, N_DEV, fp, D)
        wu = weights[f"wu{i}"].reshape(1, N_DEV, fp, D)
        g, u, act = ffn_gate_up(nm + "ffn_gate_up", hn, wg, wu)
        gather_wait(g_ffn(i)[1], act)
        wd = weights[f"wd{i}"].reshape(1, 1, N_DEV * fp, D)
        wpg = weights[f"wpg{i}"].reshape(1, 1, D, D)
        wpp = weights[f"wpp{i}"].reshape((1,) + weights[f"wpp{i}"].shape)
        h2, pn = mm_add_norm(nm + "mm_down", act, wd, h1, w_ple)
        x_next, s, e = ple_fwd(nm + "ple", pn, wpg, p_bf[i], wpp, h2)
        mixer_w = (w_in_t, w_lr_t, w_oab, gu_full) if i % 2 == 0 else (w_qkv, w_oc)
        saved.append((xs, xn, mixer_saved, mixer_w, (wg, wu, wd, wpg), h1, hn, g, u, act, h2, pn, s, e))
        xs = x_next

    def loss_fn(a, w, t):
        diff = _rms(a, w) - t
        dx, dw = _rms_bwd(a, w, diff * (1.0 / D))
        part = 0.5 * jnp.sum(jnp.mean(diff * diff, axis=-1, keepdims=True), axis=0, keepdims=True)
        return dx, dw, jnp.broadcast_to(part, (1, LANE))

    dx, d_final_w, loss_part = rowwise("loss_head", loss_fn, T,
                                       [("row", xs), ("full", final_norm_w[None, :]), ("row", loss_target[0])],
                                       [("row", D, F32), ("acc", D), ("acc", LANE)])
    loss = lax.psum(loss_part[0, 0], ("x", "y", "c"))

    grads = {}
    on_chip = []
    scatters = []

    def scatter_start(name, keys, deps=()):
        waits = [("halves", on_chip[0][1])] if on_chip else []
        starts = [("halves", [(grads[k], lax.empty((N_CHIP,) + grads[k].shape[1:], BF16)) for k in keys])] if keys else []
        waited, handles, token = exchange_call(name, waits, starts, deps=deps)
        if on_chip:
            done_keys, _ = on_chip.pop()
            sums = [chip_sum(f"{name}_sum{j}", a, half) for j, (a, half) in enumerate(waited[0])]
            _, (handle,), token = exchange_call(
                name + "_chips", [], [("chip_sums", [(cs, lax.empty(cs.shape, BF16)) for cs in sums])])
            scatters.append((done_keys, handle, waited[0]))
        if keys:
            on_chip.append((keys, handles[0]))
        return token

    d_attn_w, d_ffn_w, d_ple_w = [None] * depth, [None] * depth, [None] * depth
    for i in reversed(range(depth)):
        nm = f"l{i}_b_"
        xs_i, xn, mixer_saved, mixer_w, (wg, wu, wd, wpg), h1, hn, g, u, act, h2, pn, s, e = saved[i]
        w_attn, w_ffn, w_ple = attn_norm_w[i:i + 1], ffn_norm_w[i:i + 1], ple_norm_w[i:i + 1]

        def ple_bwd(d, sv, ev):
            gate = _sigmoid(sv)
            return d * gate, d * ev * gate * (1.0 - gate)

        de, ds = rowwise(nm + "ple_out", ple_bwd, T, [("row", dx), ("row", s), ("row", e)],
                         [("row", D, BF16), ("row", D, BF16)], deps=(loss.reshape(1, 1),) if i == depth - 1 else ())
        grads[("ple_w_proj", i)] = mm_tn(nm + "mm_ple_proj_w", p_bf[i], de, N_DEV, BF16)
        grads[("ple_w_gate", i)] = mm_tn(nm + "mm_ple_gate_w", pn, ds, 1, BF16).reshape(N_DEV, D // N_DEV, D)
        dpn = mm_nt(nm + "mm_ple_gate_x", ds, wpg, 0, F32)

        def norm_bwd_add(a, w, dn, dres):
            dxx, dw = _rms_bwd(a, w, dn)
            tot = dres + dxx
            return tot, tot, dw

        dh2, dh2_bf, d_ple_w[i] = rowwise(nm + "norm_ple", norm_bwd_add, T,
                                          [("row", h2), ("full", w_ple), ("row", dpn), ("row", dx)],
                                          [("row", D, F32), ("row", D, BF16), ("acc", D)])
        grads[("ffn_w_down", i)] = mm_tn(nm + "mm_down_w", act, dh2_bf, 1, BF16).reshape(N_DEV, fp, D)
        token = scatter_start(nm + "scatter_ple_down", [("ple_w_proj", i), ("ple_w_gate", i), ("ffn_w_down", i)])
        dg, du = ffn_down_bwd(nm + "ffn_down_x", dh2_bf, wd, g, u, deps=(token,))
        grads[("ffn_w_gate", i)] = mmt_dw(nm + "mm_gate_w", dg, hn, N_DEV, BF16)
        grads[("ffn_w_up", i)] = mmt_dw(nm + "mm_up_w", du, hn, N_DEV, BF16)
        token = scatter_start(nm + "scatter_gate_up", [("ffn_w_gate", i), ("ffn_w_up", i)])
        dhn = mmt_dx_pair(nm + "mm_gate_up_x", dg, wg, du, wu, F32, deps=(token,))
        dh1, dh1_bf, d_ffn_w[i] = rowwise(nm + "norm_ffn", norm_bwd_add, T,
                                          [("row", h1), ("full", w_ffn), ("row", dhn), ("row", dh2)],
                                          [("row", D, F32), ("row", D, BF16), ("acc", D)])
        if i % 2 == 0:
            z, glr, oraw, o = mixer_saved
            w_in_t, w_lr_t, w_oab, gu_full = mixer_w
            grads[("ab_w_out", 0)] = mm_tn(nm + "mm_out_w", o, dh1_bf, 1, BF16).reshape(N_DEV, D // N_DEV, D)
            token = scatter_start(nm + "scatter_out", [("ab_w_out", 0)])
            do = mm_nt(nm + "mm_out_x", dh1_bf, w_oab, 0, F32, deps=(token,))
            d_oraw, d_gates, d_hn_w = headnorm_bwd(nm + "headnorm", oraw, z, hn_w, do)
            d_rq, d_rk, d_rv = retention_bwd(nm + "ret", z, cosf, sins, d_oraw)
            d_gq, d_gk, d_gv, d_glr4, d_gu, d_gb = gla_bwd(nm + "gla", z, glr, gu_full, gb, d_oraw)
            dz = jnp.concatenate([d_rq, d_rk, d_rv, d_gates[:, :RET_V], d_gq, d_gk, d_gv, d_gates[:, RET_V:]], axis=1)
            (d_glr,) = rowwise(nm + "sum_lr", lambda *a: (a[0] + a[1] + a[2] + a[3],), T,
                               [("row", d_glr4[hh]) for hh in range(GLA_HEADS)], [("row", LANE, BF16)])
            dwt_in = mmt_dw(nm + "mm_in_w", dz, xn, 1, BF16, rows=in_width)
            dwt_in = mmt_dw_rows(nm + "mm_lr_w", d_glr, xn, dwt_in, OFF_LR, GLA_GATE_RANK)
            grads[("ab_w_in", 0)] = dwt_in.reshape(N_DEV, in_shard, D)
            token = scatter_start(nm + "scatter_in", [("ab_w_in", 0)])
            dxn_a = mmt_dx_wide(nm + "mm_in_x", dz, w_in_t, F32, n=OFF_LR, deps=(token,))
            token = scatter_start(nm + "scatter_in_on", [], deps=(dxn_a,))
            dxn_b = mmt_dx(nm + "mm_lr_x", d_glr, w_lr_t, 0, F32, deps=(token,))
        else:
            qkv, o, lse = mixer_saved
            w_qkv, w_oc = mixer_w
            grads[("c_w_out", 0)] = mm_tn(nm + "mm_out_w", o, dh1_bf, 1, BF16).reshape(N_DEV, D // N_DEV, D)
            do = mm_nt(nm + "mm_out_x", dh1_bf, w_oc, 0, BF16)
            dq, dk, dv = attn_bwd(nm + "attn", qkv, o, lse, do)
            dqkv = jnp.concatenate([dq, dk, dv], axis=1)
            grads[("c_w_qkv", 0)] = mm_tn(nm + "mm_qkv_w", xn, dqkv, N_DEV, BF16)
            token = scatter_start(nm + "scatter_attn", [("c_w_out", 0), ("c_w_qkv", 0)])
            dxn_a = mm_nt_wide(nm + "mm_qkv_x", dqkv, w_qkv, F32, deps=(token,))
            dxn_b = None
        dxn = [dxn_a] if dxn_b is None else [dxn_a, dxn_b]

        def norm_bwd_in(a, w, *rest):
            dxx, dw = _rms_bwd(a, w, sum(rest[1:-1], rest[0]))
            return rest[-1] + dxx, dw

        dx, d_attn_w[i] = rowwise(nm + "norm_attn", norm_bwd_in, T,
                                  [("row", xs_i), ("full", w_attn)] + [("row", d) for d in dxn] + [("row", dh1)],
                                  [("row", D, F32), ("acc", D)])

    small_names = ["attn_norm_w", "ffn_norm_w", "ple_norm_w", "final_norm_w", "ab_gla_gate_b", "ab_ret_norm_w",
                   "ab_gla_norm_w"]
    small_grads = [jnp.concatenate(d_attn_w, 0), jnp.concatenate(d_ffn_w, 0), jnp.concatenate(d_ple_w, 0), d_final_w[0],
                   d_gb, d_hn_w[:, :RET_V], d_hn_w[:, RET_V:]]
    small_w = [attn_norm_w, ffn_norm_w, ple_norm_w, final_norm_w, ab_gla_gate_b, ab_ret_norm_w, ab_gla_norm_w]
    small_m = [m_attn_norm_w, m_ffn_norm_w, m_ple_norm_w, m_final_norm_w, m_ab_gla_gate_b, m_ab_ret_norm_w, m_ab_gla_norm_w]
    small_v = [v_attn_norm_w, v_ffn_norm_w, v_ple_norm_w, v_final_norm_w, v_ab_gla_gate_b, v_ab_ret_norm_w, v_ab_gla_norm_w]
    sizes = [int(np.prod(a.shape)) for a in small_w]
    n_gu = GLA_GATE_RANK * GLA_QK
    n_small = _round_up(sum(sizes) + n_gu, LANE)
    pack = lambda parts: _pad_to(jnp.concatenate([a.reshape(-1) for a in parts]), 0, n_small)[None, :]
    small_part = pack(small_grads + [d_gu[:GLA_GATE_RANK]])

    cols_first = lambda a: jnp.transpose(a, (2, 0, 1))
    big_w = dict(ab_w_in=tuple(cols_first(a) for a in (ab_w_in, m_ab_w_in, v_ab_w_in)),
                 ab_w_out=(ab_w_out, m_ab_w_out, v_ab_w_out),
                 c_w_qkv=(c_w_qkv, m_c_w_qkv, v_c_w_qkv), c_w_out=(c_w_out, m_c_w_out, v_c_w_out),
                 ffn_w_gate=(wg_t, tr_(m_ffn_w_gate), tr_(v_ffn_w_gate)),
                 ffn_w_up=(wu_t, tr_(m_ffn_w_up), tr_(v_ffn_w_up)),
                 ffn_w_down=(ffn_w_down, m_ffn_w_down, v_ffn_w_down), ple_w_proj=(ple_w_proj, m_ple_w_proj, v_ple_w_proj),
                 ple_w_gate=(ple_w_gate, m_ple_w_gate, v_ple_w_gate))
    if on_chip:
        scatter_start("scatter_last", [], deps=(dx,))
    results, last = {}, dx
    for gi, (keys, handle, partials) in enumerate(scatters):
        (arrived,), _, _ = exchange_call(f"scatter_wait{gi}", [("chip_sums", handle)], [], deps=(last,))
        for (n, l), (_, land), (a, half) in zip(keys, arrived, partials):
            if n == "ab_w_in":
                results[n] = adamw_columns(f"adamw_{n}", *big_w[n], land, a, half)
            else:
                results[n] = adamw(f"adamw_{n}{l}", *big_w[n], l, land, a, half, prev=results.get(n))
            last = results[n][0]
    for n in ("ffn_w_gate", "ffn_w_up"):
        results[n] = [tr_(a) for a in results[n]]
    results["ab_w_in"] = [jnp.transpose(a, (1, 2, 0)) for a in results["ab_w_in"]]
    small_parts = gather_small("gather_small", small_part, deps=(last,)).reshape(N_DEV, n_small)

    gu_off = sum(sizes)
    own_cols = lambda a: lax.dynamic_slice_in_dim(a.reshape(GLA_GATE_RANK, GLA_QK), me * gu_cols, gu_cols, axis=1)
    small_res = adamw_small("adamw_small", pack(small_w + [jnp.zeros((n_gu,), F32)]),
                            pack(small_m + [jnp.zeros((n_gu,), F32)]), pack(small_v + [jnp.ones((n_gu,), F32)]),
                            small_parts)
    g_gu_full = small_res[0][0, gu_off:gu_off + n_gu]
    g_gu = own_cols(g_gu_full)[None]
    gu_res = adamw_small("adamw_gate_up", *[_pad_to(a.reshape(1, -1), 1, _round_up(a.size, LANE)) for a in
                                            (ab_gla_gate_up, m_ab_gla_gate_up, v_ab_gla_gate_up)],
                         jnp.concatenate([_pad_to(g_gu.reshape(1, -1), 1, _round_up(g_gu.size, LANE)),
                                          jnp.zeros((N_DEV - 1, _round_up(g_gu.size, LANE)), F32)], axis=0))
    for k in range(4):
        off = 0
        for n, a, sz in zip(small_names, small_w, sizes):
            results.setdefault(n, [None] * 4)[k] = small_res[k][0, off:off + sz].reshape(a.shape)
            off += sz
        results.setdefault("ab_gla_gate_up", [None] * 4)[k] = gu_res[k][0, :g_gu.size].reshape(ab_gla_gate_up.shape)

    order = ["attn_norm_w", "ffn_norm_w", "ple_norm_w", "final_norm_w", "ab_w_in", "ab_gla_gate_up", "ab_gla_gate_b",
             "ab_ret_norm_w", "ab_gla_norm_w", "ab_w_out", "c_w_qkv", "c_w_out", "ffn_w_gate", "ffn_w_up", "ffn_w_down",
             "ple_w_proj", "ple_w_gate"]
    return (loss, dx[None], *[results[n][0] for n in order], *[results[n][1] for n in order],
            *[results[n][2] for n in order], *[results[n][3] for n in order])
```

```python
import math

import numpy as np
import jax
import jax.numpy as jnp
from jax import lax
from jax.experimental import pallas as pl
from jax.experimental.pallas import tpu as pltpu

F32 = jnp.float32
BF16 = jnp.bfloat16
HIGHEST = lax.Precision.HIGHEST

N_DEV = 8
VMEM_LIMIT_BYTES = 48 * 1024 * 1024
LANE = 128
NORM_EPS = 1e-6

RET_HEADS, RET_DK, RET_DV = 4, 256, 256
RET_THETA_BASE = 10000.0
GLA_HEADS, GLA_DK, GLA_DV = 4, 128, 256
GLA_GATE_RANK = 16
GLA_GATE_NORM = 16.0
CHUNK = 64
ATT_HEADS = 16
DILATED_BRANCHES = ((128, 1), (512, 4), (2048, 16))
BLK = 256

ADAM_LR, ADAM_B1, ADAM_B2, ADAM_EPS, ADAM_WD, ADAM_STEP = 0.001, 0.9, 0.999, 1e-08, 0.01, 10

RET_QK = RET_HEADS * RET_DK
RET_V = RET_HEADS * RET_DV
GLA_QK = GLA_HEADS * GLA_DK
GLA_V = GLA_HEADS * GLA_DV
OFF_RQ, OFF_RK, OFF_RV, OFF_RG = 0, RET_QK, 2 * RET_QK, 2 * RET_QK + RET_V
OFF_GQ = OFF_RG + RET_V
OFF_GK = OFF_GQ + GLA_QK
OFF_GV = OFF_GK + GLA_QK
OFF_GG = OFF_GV + GLA_V
OFF_LR = OFF_GG + GLA_V


def _params(*sem):
    return pltpu.CompilerParams(dimension_semantics=sem or None, vmem_limit_bytes=VMEM_LIMIT_BYTES)


def _pick(n, cands):
    for c in cands:
        if n % c == 0:
            return c
    raise ValueError(f"no tile for {n} in {cands}")


_NN = (((1,), (0,)), ((), ()))
_NT = (((1,), (1,)), ((), ()))
_TN = (((0,), (0,)), ((), ()))
_ANY = pl.BlockSpec(memory_space=pl.ANY)
MAX_CONTRACT = 2048
_TILES = (1024, 768, 512, 256, 128)


def _mm_call(name, dims, grid, in_specs, out_spec, out_shape, args, deps=()):
    steps = grid[2]
    assert steps == 1 or out_shape.dtype == F32

    def body(a_ref, b_ref, *rest):
        o_ref = rest[len(deps)]
        part = lax.dot_general(a_ref[...].astype(BF16), b_ref[...].astype(BF16), dims, preferred_element_type=F32)
        if steps == 1:
            o_ref[...] = part.astype(o_ref.dtype)
        else:
            _accumulate(o_ref, part, pl.program_id(2) == 0)

    return pl.pallas_call(
        body, name=name, grid=grid, in_specs=list(in_specs) + [_ANY] * len(deps), out_specs=out_spec,
        out_shape=out_shape, compiler_params=_params("parallel", "parallel", "arbitrary"))(*args, *deps)


def mm_nn(name, a, w, l, out_dtype, deps=()):
    _, J, K, n = w.shape
    M = a.shape[0]
    tm, tn, tk = _pick(M, _TILES), _pick(n, _TILES), _pick(K, (MAX_CONTRACT,) + _TILES)
    nt = n // tn
    return _mm_call(
        name, _NN, (M // tm, J * nt, K // tk),
        [pl.BlockSpec((tm, tk), lambda i, j, k: (i, k)),
         pl.BlockSpec((None, None, tk, tn), lambda i, j, k: (l, j // nt, k, j % nt))],
        pl.BlockSpec((tm, tn), lambda i, j, k: (i, j)),
        jax.ShapeDtypeStruct((M, J * n), out_dtype), (a, w), deps)


def mm_nt(name, a, w, l, out_dtype, deps=()):
    _, J, K, n = w.shape
    M = a.shape[0]
    tm, tq, tc = _pick(M, _TILES), _pick(K, _TILES), _pick(n, (MAX_CONTRACT,) + _TILES)
    nc = n // tc
    return _mm_call(
        name, _NT, (M // tm, K // tq, J * nc),
        [pl.BlockSpec((tm, tc), lambda i, q, c: (i, c)),
         pl.BlockSpec((None, None, tq, tc), lambda i, q, c: (l, c // nc, q, c % nc))],
        pl.BlockSpec((tm, tq), lambda i, q, c: (i, q)),
        jax.ShapeDtypeStruct((M, K), out_dtype), (a, w), deps)


def mm_tn(name, x, dy, J, out_dtype, deps=()):
    M, K = x.shape
    n = dy.shape[1] // J
    tp, tn = _pick(K, _TILES), _pick(n, _TILES)
    nt = n // tn
    assert M <= MAX_CONTRACT
    return _mm_call(
        name, _TN, (K // tp, J * nt, 1),
        [pl.BlockSpec((M, tp), lambda i, j, r: (0, i)),
         pl.BlockSpec((M, tn), lambda i, j, r: (0, j))],
        pl.BlockSpec((None, tp, tn), lambda i, j, r: (j // nt, i, j % nt)),
        jax.ShapeDtypeStruct((J, K, n), out_dtype), (x, dy), deps)


def mmt_fwd(name, a, wt, l, out_dtype, n=None, deps=()):
    _, J, rows, K = wt.shape
    n = rows if n is None else n
    M = a.shape[0]
    tm, tn = _pick(M, _TILES), _pick(n, _TILES)
    nt = n // tn
    assert K <= MAX_CONTRACT
    return _mm_call(
        name, _NT, (M // tm, J * nt, 1),
        [pl.BlockSpec((tm, K), lambda i, j, k: (i, 0)),
         pl.BlockSpec((None, None, tn, K), lambda i, j, k: (l, j // nt, j % nt, 0))],
        pl.BlockSpec((tm, tn), lambda i, j, k: (i, j)),
        jax.ShapeDtypeStruct((M, J * n), out_dtype), (a, wt), deps)


def mmt_dx(name, dy, wt, l, out_dtype, n=None, deps=()):
    _, J, rows, K = wt.shape
    n = rows if n is None else n
    M = dy.shape[0]
    tm, tq, tc = _pick(M, _TILES), _pick(K, _TILES), _pick(n, _TILES)
    nc = n // tc
    return _mm_call(
        name, _NN, (M // tm, K // tq, J * nc),
        [pl.BlockSpec((tm, tc), lambda i, q, c: (i, c)),
         pl.BlockSpec((None, None, tc, tq), lambda i, q, c: (l, c // nc, c % nc, q))],
        pl.BlockSpec((tm, tq), lambda i, q, c: (i, q)),
        jax.ShapeDtypeStruct((M, K), out_dtype), (dy, wt), deps)


WIDE_TILE = 512


def _wide_call(name, body, M, K, a, w, a_spec, w_spec, out_dtype, deps):
    def kernel_body(a_ref, w_ref, *rest):
        o_ref = rest[len(deps)]
        o_ref[...] = body(a_ref, w_ref).astype(o_ref.dtype)

    return pl.pallas_call(
        kernel_body, name=name, grid=(M // WIDE_TILE, K // WIDE_TILE),
        in_specs=[a_spec, w_spec] + [_ANY] * len(deps),
        out_specs=pl.BlockSpec((WIDE_TILE, WIDE_TILE), lambda i, q: (i, q)),
        out_shape=jax.ShapeDtypeStruct((M, K), out_dtype),
        compiler_params=_params("parallel", "parallel"))(a, w, *deps)


def mmt_dx_wide(name, dy, wt, out_dtype, n=None, deps=()):
    _, J, rows, K = wt.shape
    n = rows if n is None else n
    M = dy.shape[0]

    def body(dy_ref, w_ref):
        return jnp.dot(dy_ref[...].astype(BF16), w_ref[...].reshape(J * n, WIDE_TILE), preferred_element_type=F32)

    return _wide_call(name, body, M, K, dy, wt,
                      pl.BlockSpec((WIDE_TILE, J * n), lambda i, q: (i, 0)),
                      pl.BlockSpec((None, J, n, WIDE_TILE), lambda i, q: (0, 0, 0, q)), out_dtype, deps)


def mmt_dx_pair(name, dy1, wt1, dy2, wt2, out_dtype, deps=()):
    _, J, n, K = wt1.shape
    M = dy1.shape[0]

    def body(dy1_ref, w1_ref, dy2_ref, w2_ref, *rest):
        o_ref = rest[len(deps)]
        acc = jnp.dot(dy1_ref[...], w1_ref[...].reshape(J * n, WIDE_TILE), preferred_element_type=F32)
        acc = acc + jnp.dot(dy2_ref[...], w2_ref[...].reshape(J * n, WIDE_TILE), preferred_element_type=F32)
        o_ref[...] = acc.astype(o_ref.dtype)

    rows = _once((WIDE_TILE, J * n), lambda i, q: (i, 0))
    cols = pl.BlockSpec((None, J, n, WIDE_TILE), lambda i, q: (0, 0, 0, q))
    return pl.pallas_call(
        body, name=name, grid=(M // WIDE_TILE, K // WIDE_TILE),
        in_specs=[rows, cols, rows, cols] + [_ANY] * len(deps),
        out_specs=pl.BlockSpec((WIDE_TILE, WIDE_TILE), lambda i, q: (i, q)),
        out_shape=jax.ShapeDtypeStruct((M, K), out_dtype),
        compiler_params=_params("parallel", "parallel"))(dy1, wt1, dy2, wt2, *deps)


def mm_nt_wide(name, a, w, out_dtype, deps=()):
    _, J, K, n = w.shape
    M = a.shape[0]

    def body(a_ref, w_ref):
        acc = None
        for j in range(J):
            part = lax.dot_general(a_ref[:, j * n:(j + 1) * n].astype(BF16), w_ref[j], _NT, preferred_element_type=F32)
            acc = part if acc is None else acc + part
        return acc

    return _wide_call(name, body, M, K, a, w,
                      pl.BlockSpec((WIDE_TILE, J * n), lambda i, q: (i, 0)),
                      pl.BlockSpec((None, J, WIDE_TILE, n), lambda i, q: (0, 0, q, 0)), out_dtype, deps)


def mmt_dw(name, dy, x, J, out_dtype, deps=(), rows=None):
    M, K = x.shape
    n = dy.shape[1] // J
    tn, tp = _pick(n, _TILES), _pick(K, _TILES)
    nt = n // tn
    assert M <= MAX_CONTRACT
    return _mm_call(
        name, _TN, (J * nt, K // tp, 1),
        [pl.BlockSpec((M, tn), lambda j, i, r: (0, j)),
         pl.BlockSpec((M, tp), lambda j, i, r: (0, i))],
        pl.BlockSpec((None, tn, tp), lambda j, i, r: (j // nt, j % nt, i)),
        jax.ShapeDtypeStruct((J, n if rows is None else rows, K), out_dtype), (dy, x), deps)


def mmt_dw_rows(name, dy, x, out, row0, rank):
    M, K = x.shape
    tp = _pick(K, _TILES)

    def body(dy_ref, x_ref, prev_ref, o_ref):
        del prev_ref
        full = lax.dot_general(dy_ref[...], x_ref[...], _TN, preferred_element_type=F32)
        o_ref[...] = full[:rank].astype(o_ref.dtype)

    return pl.pallas_call(
        body, name=name, grid=(K // tp,),
        in_specs=[pl.BlockSpec((M, dy.shape[1]), lambda i: (0, 0)), pl.BlockSpec((M, tp), lambda i: (0, i)), _ANY],
        out_specs=pl.BlockSpec((None, rank, tp), lambda i: (0, row0 // rank, i)),
        out_shape=jax.ShapeDtypeStruct(out.shape, out.dtype), input_output_aliases={2: 0},
        compiler_params=_params("parallel"))(dy, x, out)


def ffn_gate_up(name, a, wg, wu):
    _, J, n, K = wg.shape
    M = a.shape[0]
    tm, tn = _pick(M, _TILES), _pick(n, _TILES)
    nt = n // tn
    assert K <= MAX_CONTRACT

    def body(a_ref, wg_ref, wu_ref, g_ref, u_ref, act_ref):
        x = a_ref[...]
        g = lax.dot_general(x, wg_ref[...], _NT, preferred_element_type=F32)
        u = lax.dot_general(x, wu_ref[...], _NT, preferred_element_type=F32)
        g_ref[...] = g.astype(g_ref.dtype)
        u_ref[...] = u.astype(u_ref.dtype)
        act_ref[...] = (_silu_and_grad(g)[0] * u).astype(act_ref.dtype)

    w_spec = pl.BlockSpec((None, None, tn, K), lambda i, j: (0, j // nt, j % nt, 0))
    out = pl.BlockSpec((tm, tn), lambda i, j: (i, j))
    return pl.pallas_call(
        body, name=name, grid=(M // tm, J * nt),
        in_specs=[pl.BlockSpec((tm, K), lambda i, j: (i, 0)), w_spec, w_spec],
        out_specs=[out] * 3, out_shape=[jax.ShapeDtypeStruct((M, J * n), BF16)] * 3,
        compiler_params=_params("parallel", "parallel"))(a, wg, wu)


def mm_add_norm(name, a, w, res, norm_w):
    _, _, K, N = w.shape
    M = a.shape[0]
    tm, tk = _pick(M, (WIDE_TILE, 256)), _pick(K, (1024, 512, 256))
    steps = K // tk

    def body(a_ref, w_ref, res_ref, nw_ref, h_ref, hn_ref):
        k = pl.program_id(1)
        part = jnp.dot(a_ref[...], w_ref[...], preferred_element_type=F32)
        _accumulate(h_ref, part, k == 0)

        @pl.when(k == steps - 1)
        def _():
            h = h_ref[...] + res_ref[...]
            h_ref[...] = h
            hn_ref[...] = _rms(h, nw_ref[...]).astype(hn_ref.dtype)

    rows = pl.BlockSpec((tm, N), lambda i, k: (i, 0))
    return pl.pallas_call(
        body, name=name, grid=(M // tm, steps),
        in_specs=[pl.BlockSpec((tm, tk), lambda i, k: (i, k)),
                  pl.BlockSpec((None, None, tk, N), lambda i, k: (0, 0, k, 0)), rows,
                  pl.BlockSpec((1, N), lambda i, k: (0, 0))],
        out_specs=[rows, rows],
        out_shape=[jax.ShapeDtypeStruct((M, N), F32), jax.ShapeDtypeStruct((M, N), BF16)],
        compiler_params=_params("parallel", "arbitrary"))(a, w, res, norm_w)


def ple_fwd(name, pn, wpg, p_in, wpp, h):
    _, J, P, n = wpp.shape
    M, D = h.shape
    tm, tn = _pick(M, (WIDE_TILE, 256)), _pick(D, _TILES)
    per_tile = tn // n

    def body(pn_ref, wg_ref, p_ref, wp_ref, h_ref, x_ref, s_ref, e_ref):
        s = jnp.dot(pn_ref[...], wg_ref[...], preferred_element_type=F32)
        p_blk = p_ref[...]
        e = jnp.concatenate([jnp.dot(p_blk, wp_ref[j], preferred_element_type=F32) for j in range(per_tile)], axis=1)
        s_ref[...] = s
        e_ref[...] = e
        x_ref[...] = h_ref[...] + _sigmoid(s) * e

    tile = pl.BlockSpec((tm, tn), lambda i, j: (i, j))
    return pl.pallas_call(
        body, name=name, grid=(M // tm, D // tn),
        in_specs=[pl.BlockSpec((tm, D), lambda i, j: (i, 0)),
                  pl.BlockSpec((None, None, D, tn), lambda i, j: (0, 0, 0, j)),
                  pl.BlockSpec((tm, P), lambda i, j: (i, 0)),
                  pl.BlockSpec((None, per_tile, P, n), lambda i, j: (0, j, 0, 0)), tile],
        out_specs=[tile] * 3, out_shape=[jax.ShapeDtypeStruct((M, D), F32)] * 3,
        compiler_params=_params("parallel", "parallel"))(pn, wpg, p_in, wpp, h)


def ffn_down_bwd(name, dy, wd, g, u, deps=()):
    _, _, K, n = wd.shape
    M = dy.shape[0]
    tm, tq = _pick(M, _TILES), _pick(K, _TILES)
    assert n <= MAX_CONTRACT

    def body(dy_ref, w_ref, g_ref, u_ref, *rest):
        dg_ref, du_ref = rest[len(deps):]
        dact = lax.dot_general(dy_ref[...], w_ref[...], _NT, preferred_element_type=F32)
        silu, dsilu = _silu_and_grad(g_ref[...].astype(F32))
        dg_ref[...] = (dact * u_ref[...].astype(F32) * dsilu).astype(dg_ref.dtype)
        du_ref[...] = (dact * silu).astype(du_ref.dtype)

    blk = pl.BlockSpec((tm, tq), lambda i, q: (i, q))
    return pl.pallas_call(
        body, name=name, grid=(M // tm, K // tq),
        in_specs=[pl.BlockSpec((tm, n), lambda i, q: (i, 0)),
                  pl.BlockSpec((None, None, tq, n), lambda i, q: (0, 0, q, 0)), blk, blk] + [_ANY] * len(deps),
        out_specs=[blk, blk], out_shape=[jax.ShapeDtypeStruct((M, K), BF16)] * 2,
        compiler_params=_params("parallel", "parallel"))(dy, wd, g, u, *deps)


def rowwise(name, fn, rows, ins, outs, tr=256, deps=()):
    widest = max([s[1].shape[1] if s[0] != "col" else s[3] for s in ins] + [s[1] for s in outs])
    tr = min(tr if widest <= 2048 else tr // 2, rows)
    in_specs, args = [], []
    for spec in ins:
        kind, a = spec[0], spec[1]
        if kind == "row":
            in_specs.append(pl.BlockSpec((tr, a.shape[1]), lambda i: (i, 0)))
        elif kind == "col":
            cb, width = spec[2], spec[3]
            in_specs.append(pl.BlockSpec((tr, width), lambda i, cb=cb: (i, cb)))
        else:
            in_specs.append(pl.BlockSpec(a.shape, lambda i: (0, 0)))
        args.append(a)
    out_specs, out_shapes = [], []
    for spec in outs:
        if spec[0] == "row":
            out_specs.append(pl.BlockSpec((tr, spec[1]), lambda i: (i, 0)))
            out_shapes.append(jax.ShapeDtypeStruct((rows, spec[1]), spec[2]))
        else:
            out_specs.append(pl.BlockSpec((1, spec[1]), lambda i: (0, 0)))
            out_shapes.append(jax.ShapeDtypeStruct((1, spec[1]), F32))
    n_in = len(ins)

    def body(*refs):
        vals = fn(*[r[...] for r in refs[:n_in]])
        first = pl.program_id(0) == 0
        for r, v, spec in zip(refs[n_in + len(deps):], vals, outs):
            if spec[0] == "row":
                r[...] = v.astype(r.dtype)
            else:
                _accumulate(r, v, first)

    return pl.pallas_call(body, name=name, grid=(rows // tr,), in_specs=in_specs + [_ANY] * len(deps),
                          out_specs=out_specs, out_shape=out_shapes,
                          compiler_params=_params("arbitrary"))(*args, *deps)


def _accumulate(ref, v, first):
    @pl.when(first)
    def _():
        ref[...] = v

    @pl.when(jnp.logical_not(first))
    def _():
        ref[...] += v


def _rms(x, w):
    r = lax.rsqrt(jnp.mean(x * x, axis=-1, keepdims=True) + NORM_EPS)
    return x * r * w


def _rms_bwd(x, w, dy):
    r = lax.rsqrt(jnp.mean(x * x, axis=-1, keepdims=True) + NORM_EPS)
    g = dy * w
    dx = r * (g - x * (r * r) * jnp.mean(g * x, axis=-1, keepdims=True))
    dw = jnp.sum(dy * x * r, axis=0, keepdims=True)
    return dx, dw


def _sigmoid(x):
    return 1.0 / (1.0 + jnp.exp(-x))


def _silu_and_grad(g):
    s = _sigmoid(g)
    return g * s, s * (1.0 + g * (1.0 - s))


def _swap_pairs(x):
    n = x.shape[-1]
    lane = lax.broadcasted_iota(jnp.int32, x.shape, x.ndim - 1)
    return jnp.where((lane & 1) == 0, pltpu.roll(x, n - 1, x.ndim - 1), pltpu.roll(x, 1, x.ndim - 1))


def _rot(x, cosf, sins):
    return x * cosf + _swap_pairs(x) * sins


def _unrot(d, cosf, sins):
    return d * cosf + _swap_pairs(d * sins)


def _ret_log_gamma(h):
    vals = [math.log1p(-2.0 ** (-5.0 - i)) for i in range(RET_HEADS)]
    out = jnp.float32(vals[RET_HEADS - 1])
    for i in range(RET_HEADS - 2, -1, -1):
        out = jnp.where(h == i, jnp.float32(vals[i]), out)
    return out


def _fill_decays(dec_ref, lg):
    ri = lax.broadcasted_iota(jnp.int32, (BLK, BLK), 0)
    ci = lax.broadcasted_iota(jnp.int32, (BLK, BLK), 1)
    for d in range(dec_ref.shape[0]):
        dt = d * BLK + ri - ci
        dec_ref[d] = jnp.where(dt >= 0, jnp.exp(jnp.maximum(dt, 0).astype(F32) * lg), 0.0)


def _decay_row(dec_ref, qi):
    return jnp.concatenate([dec_ref[qi - kb] for kb in range(qi + 1)], axis=1)


def _once(block_shape, index_map):
    return pl.BlockSpec(block_shape, index_map, pipeline_mode=pl.Buffered(1))


def _dot(a, b):
    return jnp.dot(a.astype(BF16), b.astype(BF16), preferred_element_type=F32)


def _dot_nt(a, b):
    return lax.dot_general(a.astype(BF16), b.astype(BF16), _NT, preferred_element_type=F32)


def _dot_tn(a, b):
    return lax.dot_general(a.astype(BF16), b.astype(BF16), _TN, preferred_element_type=F32)


def retention_fwd(name, z, cosf, sins, width_out):
    T = z.shape[0]
    nq = T // BLK
    scale = RET_DK ** -0.5

    def body(q_ref, k_ref, v_ref, cos_ref, sin_ref, o_ref, krot, vb, dec_ref):
        _fill_decays(dec_ref, _ret_log_gamma(pl.program_id(0)))
        krot[...] = (_rot(k_ref[...], cos_ref[...], sin_ref[...]) * scale).astype(BF16)
        vb[...] = v_ref[...].astype(BF16)
        for qi in range(nq):
            rows, n = slice(qi * BLK, (qi + 1) * BLK), (qi + 1) * BLK
            q = _rot(q_ref[rows, :], cos_ref[rows, :], sin_ref[rows, :])
            s = _dot_nt(q, krot[0:n, :]) * _decay_row(dec_ref, qi)
            o_ref[rows, :] = _dot(s, vb[0:n, :])

    return pl.pallas_call(
        body, name=name, grid=(RET_HEADS,),
        in_specs=[pl.BlockSpec((T, RET_DK), lambda h: (0, OFF_RQ // RET_DK + h)),
                  pl.BlockSpec((T, RET_DK), lambda h: (0, OFF_RK // RET_DK + h)),
                  pl.BlockSpec((T, RET_DV), lambda h: (0, OFF_RV // RET_DV + h)),
                  _once((T, RET_DK), lambda h: (0, 0)), _once((T, RET_DK), lambda h: (0, 0))],
        out_specs=pl.BlockSpec((T, RET_DV), lambda h: (0, h)),
        out_shape=jax.ShapeDtypeStruct((T, width_out), F32),
        scratch_shapes=[pltpu.VMEM((T, RET_DK), BF16), pltpu.VMEM((T, RET_DV), BF16),
                        pltpu.VMEM((nq, BLK, BLK), F32)],
        compiler_params=_params("arbitrary"))(z, z, z, cosf, sins)


def retention_bwd(name, z, cosf, sins, do):
    T = z.shape[0]
    nq = T // BLK
    scale = RET_DK ** -0.5

    def body(q_ref, k_ref, v_ref, cos_ref, sin_ref, do_ref, dq_ref, dk_ref, dv_ref, krot, vb, dk_acc, dv_acc, dec_ref):
        _fill_decays(dec_ref, _ret_log_gamma(pl.program_id(0)))
        krot[...] = (_rot(k_ref[...], cos_ref[...], sin_ref[...]) * scale).astype(BF16)
        vb[...] = v_ref[...].astype(BF16)
        dk_acc[...] = jnp.zeros_like(dk_acc)
        dv_acc[...] = jnp.zeros_like(dv_acc)
        for qi in range(nq):
            rows, n = slice(qi * BLK, (qi + 1) * BLK), (qi + 1) * BLK
            cos_q, sin_q = cos_ref[rows, :], sin_ref[rows, :]
            q = _rot(q_ref[rows, :], cos_q, sin_q).astype(BF16)
            dout = do_ref[rows, :].astype(BF16)
            kk, vv, dec = krot[0:n, :], vb[0:n, :], _decay_row(dec_ref, qi)
            p = (_dot_nt(q, kk) * dec).astype(BF16)
            ds = (_dot_nt(dout, vv) * dec).astype(BF16)
            dq_ref[rows, :] = _unrot(_dot(ds, kk), cos_q, sin_q).astype(dq_ref.dtype)
            dk_acc[0:n, :] += _dot_tn(ds, q)
            dv_acc[0:n, :] += _dot_tn(p, dout)
        dk_ref[...] = (_unrot(dk_acc[...], cos_ref[...], sin_ref[...]) * scale).astype(dk_ref.dtype)
        dv_ref[...] = dv_acc[...].astype(dv_ref.dtype)

    head = lambda h: (0, h)
    return pl.pallas_call(
        body, name=name, grid=(RET_HEADS,),
        in_specs=[pl.BlockSpec((T, RET_DK), lambda h: (0, OFF_RQ // RET_DK + h)),
                  pl.BlockSpec((T, RET_DK), lambda h: (0, OFF_RK // RET_DK + h)),
                  pl.BlockSpec((T, RET_DV), lambda h: (0, OFF_RV // RET_DV + h)),
                  _once((T, RET_DK), lambda h: (0, 0)), _once((T, RET_DK), lambda h: (0, 0)),
                  pl.BlockSpec((T, RET_DV), head)],
        out_specs=[pl.BlockSpec((T, RET_DK), head), pl.BlockSpec((T, RET_DK), head), pl.BlockSpec((T, RET_DV), head)],
        out_shape=[jax.ShapeDtypeStruct((T, RET_QK), BF16), jax.ShapeDtypeStruct((T, RET_QK), BF16),
                   jax.ShapeDtypeStruct((T, RET_V), BF16)],
        scratch_shapes=[pltpu.VMEM((T, RET_DK), BF16), pltpu.VMEM((T, RET_DV), BF16),
                        pltpu.VMEM((T, RET_DK), F32), pltpu.VMEM((T, RET_DV), F32),
                        pltpu.VMEM((nq, BLK, BLK), F32)],
        compiler_params=_params("arbitrary"))(z, z, z, cosf, sins, do)


GLA_PAIR = 2


def _gla_chunk(q_ref, k_ref, v_ref, glr_ref, gu, gb, rows, hh, trilf):
    ck = slice(hh * GLA_DK, (hh + 1) * GLA_DK)
    zg = _dot(glr_ref[rows, :], gu[:, ck]) + gb[:, ck]
    la = (jnp.minimum(zg, 0.0) - jnp.log(1.0 + jnp.exp(-jnp.abs(zg)))) * (1.0 / GLA_GATE_NORM)
    cum = jnp.dot(trilf, la, precision=HIGHEST, preferred_element_type=F32)
    last = jnp.sum(la, axis=0, keepdims=True)
    ecum = jnp.exp(cum)
    k = k_ref[rows, ck]
    qt = q_ref[rows, ck] * (GLA_DK ** -0.5) * ecum
    kt = k * jnp.exp(-cum)
    kh = k * jnp.exp(last - cum)
    return zg, cum, last, ecum, qt, kt, kh, v_ref[rows, hh * GLA_DV:(hh + 1) * GLA_DV].astype(BF16)


def _state_decay(last):
    e = jnp.exp(jnp.broadcast_to(last, (GLA_DK, GLA_DK)).T)
    return jnp.concatenate([e] * (GLA_DV // GLA_DK), axis=1)


def _gla_specs(T):
    wk, wv = GLA_PAIR * GLA_DK, GLA_PAIR * GLA_DV
    return [_once((T, wk), lambda h: (0, OFF_GQ // wk + h)),
            _once((T, wk), lambda h: (0, OFF_GK // wk + h)),
            _once((T, wv), lambda h: (0, OFF_GV // wv + h)),
            _once((T, LANE), lambda h: (0, 0)),
            pl.BlockSpec((LANE, wk), lambda h: (0, h)),
            pl.BlockSpec((1, wk), lambda h: (0, h))]


def gla_fwd(name, z, glr, gu, gb, o_prev):
    T = z.shape[0]
    nc = T // CHUNK
    wv = GLA_PAIR * GLA_DV

    def body(q_ref, k_ref, v_ref, glr_ref, gu_ref, gb_ref, prev_ref, o_ref, *S):
        del prev_ref
        gu_b, gb_v = gu_ref[...].astype(BF16), gb_ref[...]
        ri = lax.broadcasted_iota(jnp.int32, (CHUNK, CHUNK), 0)
        ci = lax.broadcasted_iota(jnp.int32, (CHUNK, CHUNK), 1)
        tril = ri >= ci
        trilf = tril.astype(F32)
        for s_ref in S:
            s_ref[...] = jnp.zeros_like(s_ref)

        def step(c, carry):
            rows = pl.ds(pl.multiple_of(c * CHUNK, CHUNK), CHUNK)
            heads = range(GLA_PAIR)
            ch = [_gla_chunk(q_ref, k_ref, v_ref, glr_ref, gu_b, gb_v, rows, hh, trilf) for hh in heads]
            a = [jnp.where(tril, _dot_nt(ch[hh][4], ch[hh][5]), 0.0) for hh in heads]
            s_prev = [S[hh][...] for hh in heads]
            intra = [_dot(a[hh], ch[hh][7]) for hh in heads]
            inter = [_dot(ch[hh][4], s_prev[hh]) for hh in heads]
            added = [_dot_tn(ch[hh][6], ch[hh][7]) for hh in heads]
            for hh in heads:
                o_ref[rows, hh * GLA_DV:(hh + 1) * GLA_DV] = intra[hh] + inter[hh]
                S[hh][...] = s_prev[hh] * _state_decay(ch[hh][2]) + added[hh]
            return carry

        lax.fori_loop(0, nc, step, 0)

    n_in = 6
    return pl.pallas_call(
        body, name=name, grid=(GLA_HEADS // GLA_PAIR,),
        in_specs=_gla_specs(T) + [pl.BlockSpec(memory_space=pl.ANY)],
        out_specs=pl.BlockSpec((T, wv), lambda h: (0, RET_V // wv + h)),
        out_shape=jax.ShapeDtypeStruct(o_prev.shape, F32),
        scratch_shapes=[pltpu.VMEM((GLA_DK, GLA_DV), F32)] * GLA_PAIR,
        input_output_aliases={n_in: 0},
        compiler_params=_params("arbitrary"))(z, z, z, glr, gu, gb, o_prev)


def gla_bwd(name, z, glr, gu, gb, do):
    T = z.shape[0]
    nc = T // CHUNK

    def body(q_ref, k_ref, v_ref, glr_ref, gu_ref, gb_ref, do_ref,
             dq_ref, dk_ref, dv_ref, dglr_ref, dgu_ref, dgb_ref, s_all, dS):
        gu_b, gb_v = gu_ref[...].astype(BF16), gb_ref[...]
        ri = lax.broadcasted_iota(jnp.int32, (CHUNK, CHUNK), 0)
        ci = lax.broadcasted_iota(jnp.int32, (CHUNK, CHUNK), 1)
        tril = ri >= ci
        trilf = tril.astype(F32)
        triuf = (ri <= ci).astype(F32)
        last_row = lax.broadcasted_iota(jnp.int32, (CHUNK, GLA_DK), 0) == CHUNK - 1
        ones8 = jnp.ones((8, GLA_DV), F32)

        heads = range(GLA_PAIR)

        def fstep(c, carry):
            rows = pl.ds(pl.multiple_of(c * CHUNK, CHUNK), CHUNK)
            ch = [_gla_chunk(q_ref, k_ref, v_ref, glr_ref, gu_b, gb_v, rows, hh, trilf) for hh in heads]
            added = [_dot_tn(ch[hh][6], ch[hh][7]) for hh in heads]
            for hh in heads:
                s_prev = dS[hh]
                s_all[hh, c] = s_prev
                dS[hh] = s_prev * _state_decay(ch[hh][2]) + added[hh]
            return carry

        dS[...] = jnp.zeros_like(dS)
        lax.fori_loop(0, nc, fstep, 0)
        dS[...] = jnp.zeros_like(dS)
        dgu_ref[...] = jnp.zeros_like(dgu_ref)
        dgb_ref[...] = jnp.zeros_like(dgb_ref)

        def bstep(i, carry):
            c = nc - 1 - i
            rows = pl.ds(pl.multiple_of(c * CHUNK, CHUNK), CHUNK)
            glr_c = glr_ref[rows, :]
            cks = [slice(hh * GLA_DK, (hh + 1) * GLA_DK) for hh in heads]
            cvs = [slice(hh * GLA_DV, (hh + 1) * GLA_DV) for hh in heads]
            ch = [_gla_chunk(q_ref, k_ref, v_ref, glr_ref, gu_b, gb_v, rows, hh, trilf) for hh in heads]
            zg, cum, last, ecum, qt, kt, kh, v = [[ch[hh][j] for hh in heads] for j in range(8)]
            s_prev = [s_all[hh, c] for hh in heads]
            ds_new = [dS[hh] for hh in heads]
            dout = [do_ref[rows, cvs[hh]].astype(BF16) for hh in heads]
            a = [jnp.where(tril, _dot_nt(qt[hh], kt[hh]), 0.0) for hh in heads]
            da = [jnp.where(tril, _dot_nt(dout[hh], v[hh]), 0.0) for hh in heads]
            dv_a = [_dot_tn(a[hh], dout[hh]) for hh in heads]
            dv_b = [_dot(kh[hh], ds_new[hh]) for hh in heads]
            dqt_a = [_dot(da[hh], kt[hh]) for hh in heads]
            dqt_b = [_dot_nt(dout[hh], s_prev[hh]) for hh in heads]
            dkt = [_dot_tn(da[hh], qt[hh]) for hh in heads]
            dkh = [_dot_nt(v[hh], ds_new[hh]) for hh in heads]
            ds_add = [_dot_tn(qt[hh], dout[hh]) for hh in heads]
            rs = [lax.dot_general(ones8, ds_new[hh] * s_prev[hh], _NT, precision=HIGHEST, preferred_element_type=F32)
                  for hh in heads]
            dcum = []
            for hh in heads:
                dv_ref[rows, cvs[hh]] = (dv_a[hh] + dv_b[hh]).astype(dv_ref.dtype)
                dS[hh] = ds_new[hh] * _state_decay(last[hh]) + ds_add[hh]
                dqt = dqt_a[hh] + dqt_b[hh]
                dq_ref[rows, cks[hh]] = (dqt * ecum[hh] * (GLA_DK ** -0.5)).astype(dq_ref.dtype)
                dk_ref[rows, cks[hh]] = (dkt[hh] * jnp.exp(-cum[hh])
                                         + dkh[hh] * jnp.exp(last[hh] - cum[hh])).astype(dk_ref.dtype)
                dkh_kh = dkh[hh] * kh[hh]
                dlast = (jnp.sum(dkh_kh, axis=0, keepdims=True)
                         + jnp.exp(last[hh]) * (jnp.sum(rs[hh], axis=0, keepdims=True) * 0.125))
                dcum.append(dqt * qt[hh] - dkt[hh] * kt[hh] - dkh_kh + jnp.where(last_row, dlast, 0.0))
            dla = [jnp.dot(triuf, dcum[hh], precision=HIGHEST, preferred_element_type=F32) for hh in heads]
            dzg = [dla[hh] * (1.0 / GLA_GATE_NORM) * _sigmoid(-zg[hh]) for hh in heads]
            dglr = [_dot_nt(dzg[hh], gu_b[:, cks[hh]]) for hh in heads]
            dgu = [_dot_tn(glr_c, dzg[hh]) for hh in heads]
            for hh in heads:
                dglr_ref[hh, rows, :] = dglr[hh]
                dgu_ref[:, cks[hh]] += dgu[hh]
                dgb_ref[:, cks[hh]] += jnp.sum(dzg[hh], axis=0, keepdims=True)
            return carry

        lax.fori_loop(0, nc, bstep, 0)

    wk, wv = GLA_PAIR * GLA_DK, GLA_PAIR * GLA_DV
    return pl.pallas_call(
        body, name=name, grid=(GLA_HEADS // GLA_PAIR,),
        in_specs=_gla_specs(T) + [_once((T, wv), lambda h: (0, RET_V // wv + h))],
        out_specs=[pl.BlockSpec((T, wk), lambda h: (0, h)), pl.BlockSpec((T, wk), lambda h: (0, h)),
                   pl.BlockSpec((T, wv), lambda h: (0, h)),
                   pl.BlockSpec((GLA_PAIR, T, LANE), lambda h: (h, 0, 0)),
                   pl.BlockSpec((LANE, wk), lambda h: (0, h)), pl.BlockSpec((1, wk), lambda h: (0, h))],
        out_shape=[jax.ShapeDtypeStruct((T, GLA_QK), BF16), jax.ShapeDtypeStruct((T, GLA_QK), BF16),
                   jax.ShapeDtypeStruct((T, GLA_V), BF16), jax.ShapeDtypeStruct((GLA_HEADS, T, LANE), F32),
                   jax.ShapeDtypeStruct((LANE, GLA_QK), F32), jax.ShapeDtypeStruct((1, GLA_QK), F32)],
        scratch_shapes=[pltpu.VMEM((GLA_PAIR, nc, GLA_DK, GLA_DV), F32), pltpu.VMEM((GLA_PAIR, GLA_DK, GLA_DV), F32)],
        compiler_params=_params("arbitrary"))(z, z, z, glr, gu, gb, do)


HN_HEADS = RET_HEADS + GLA_HEADS
HN_W = RET_DV


def _gate_col(h):
    return jnp.where(h < RET_HEADS, OFF_RG // HN_W + h, OFF_GG // HN_W + h - RET_HEADS)


def headnorm_fwd(name, oraw, z, w):
    T = oraw.shape[0]
    tr = _pick(T, _TILES)

    def body(o_ref, g_ref, w_ref, y_ref):
        y_ref[...] = (_rms(o_ref[...], w_ref[...]) * _silu_and_grad(g_ref[...])[0]).astype(y_ref.dtype)

    return pl.pallas_call(
        body, name=name, grid=(HN_HEADS, T // tr),
        in_specs=[pl.BlockSpec((tr, HN_W), lambda h, i: (i, h)),
                  pl.BlockSpec((tr, HN_W), lambda h, i: (i, _gate_col(h))),
                  pl.BlockSpec((1, HN_W), lambda h, i: (0, h))],
        out_specs=pl.BlockSpec((tr, HN_W), lambda h, i: (i, h)),
        out_shape=jax.ShapeDtypeStruct((T, HN_HEADS * HN_W), BF16),
        compiler_params=_params("arbitrary", "arbitrary"))(oraw, z, w)


def headnorm_bwd(name, oraw, z, w, dy):
    T = oraw.shape[0]
    tr = _pick(T, _TILES)

    def body(o_ref, g_ref, w_ref, dy_ref, do_ref, dg_ref, dw_ref):
        o, wv, dyv = o_ref[...], w_ref[...], dy_ref[...].astype(F32)
        silu, dsilu = _silu_and_grad(g_ref[...])
        n = _rms(o, wv)
        dg_ref[...] = (dyv * n * dsilu).astype(dg_ref.dtype)
        dx, dw = _rms_bwd(o, wv, dyv * silu)
        do_ref[...] = dx
        _accumulate(dw_ref, dw, pl.program_id(1) == 0)

    blk = pl.BlockSpec((tr, HN_W), lambda h, i: (i, h))
    return pl.pallas_call(
        body, name=name, grid=(HN_HEADS, T // tr),
        in_specs=[blk, pl.BlockSpec((tr, HN_W), lambda h, i: (i, _gate_col(h))),
                  pl.BlockSpec((1, HN_W), lambda h, i: (0, h)), blk],
        out_specs=[blk, blk, pl.BlockSpec((1, HN_W), lambda h, i: (0, h))],
        out_shape=[jax.ShapeDtypeStruct((T, HN_HEADS * HN_W), F32),
                   jax.ShapeDtypeStruct((T, HN_HEADS * HN_W), BF16),
                   jax.ShapeDtypeStruct((1, HN_HEADS * HN_W), F32)],
        compiler_params=_params("arbitrary", "arbitrary"))(oraw, z, w, dy)


N_MASKS = 4


def _check_mask_classes(T):
    for window, dilation in DILATED_BRANCHES[:-1]:
        assert window < (N_MASKS - 1) * BLK - (BLK - 1) and BLK % dilation == 0
    assert DILATED_BRANCHES[-1][0] >= T and BLK % DILATED_BRANCHES[-1][1] == 0


def _fill_masks(mult_ref, bias_ref):
    ri = lax.broadcasted_iota(jnp.int32, (BLK, BLK), 0)
    ci = lax.broadcasted_iota(jnp.int32, (BLK, BLK), 1)
    for d in range(N_MASKS):
        dt = d * BLK + ri - ci
        mult = jnp.zeros((BLK, BLK), F32)
        for window, dilation in DILATED_BRANCHES:
            hit = (dt >= 0) & (dt <= window) & ((dt & (dilation - 1)) == 0)
            mult = mult + hit.astype(F32)
        mult_ref[d] = mult
        bias_ref[d] = jnp.where(mult > 0, 0.0, -1e30)


def _mask_row(ref, qi):
    return jnp.concatenate([ref[min(qi - kb, N_MASKS - 1)] for kb in range(qi + 1)], axis=1)


def attn_fwd(name, qkv):
    T = qkv.shape[0]
    D = qkv.shape[1] // 3
    dh = D // ATT_HEADS
    nq = T // BLK
    scale = dh ** -0.5

    _check_mask_classes(T)

    def body(q_ref, k_ref, v_ref, o_ref, lse_ref, mult_ref, bias_ref):
        @pl.when(pl.program_id(0) == 0)
        def _():
            _fill_masks(mult_ref, bias_ref)

        for q0 in range(0, nq, 2):
            qis = range(q0, min(q0 + 2, nq))
            rows = [slice(qi * BLK, (qi + 1) * BLK) for qi in qis]
            ns = [(qi + 1) * BLK for qi in qis]
            s = [_dot_nt(q_ref[r, :], k_ref[0:n, :]) for r, n in zip(rows, ns)]
            s = [x * scale + _mask_row(bias_ref, qi) for x, qi in zip(s, qis)]
            m = [jnp.max(x, axis=-1, keepdims=True) for x in s]
            p = [_mask_row(mult_ref, qi) * jnp.exp(x - mx) for qi, x, mx in zip(qis, s, m)]
            l = [jnp.sum(x, axis=-1, keepdims=True) for x in p]
            pv = [_dot(x, v_ref[0:n, :]) for x, n in zip(p, ns)]
            for r, x, lx, mx in zip(rows, pv, l, m):
                o_ref[r, :] = (x / lx).astype(o_ref.dtype)
                lse_ref[r, :] = jnp.broadcast_to(mx + jnp.log(lx), (BLK, LANE))

    return pl.pallas_call(
        body, name=name, grid=(ATT_HEADS,),
        in_specs=[pl.BlockSpec((T, dh), lambda h: (0, h)),
                  pl.BlockSpec((T, dh), lambda h: (0, ATT_HEADS + h)),
                  pl.BlockSpec((T, dh), lambda h: (0, 2 * ATT_HEADS + h))],
        out_specs=[pl.BlockSpec((T, dh), lambda h: (0, h)),
                   pl.BlockSpec((None, T, LANE), lambda h: (h, 0, 0))],
        out_shape=[jax.ShapeDtypeStruct((T, D), BF16), jax.ShapeDtypeStruct((ATT_HEADS, T, LANE), F32)],
        scratch_shapes=[pltpu.VMEM((N_MASKS, BLK, BLK), F32), pltpu.VMEM((N_MASKS, BLK, BLK), F32)],
        compiler_params=_params("arbitrary"))(qkv, qkv, qkv)


def attn_bwd(name, qkv, o, lse, do):
    T = qkv.shape[0]
    D = qkv.shape[1] // 3
    dh = D // ATT_HEADS
    nq = T // BLK
    scale = dh ** -0.5

    _check_mask_classes(T)

    def body(q_ref, k_ref, v_ref, o_ref, lse_ref, do_ref, dq_ref, dk_ref, dv_ref, dk_acc, dv_acc, mult_ref, bias_ref):
        @pl.when(pl.program_id(0) == 0)
        def _():
            _fill_masks(mult_ref, bias_ref)

        dk_acc[...] = jnp.zeros_like(dk_acc)
        dv_acc[...] = jnp.zeros_like(dv_acc)
        for qi in range(nq):
            rows, n = slice(qi * BLK, (qi + 1) * BLK), (qi + 1) * BLK
            q, dout = q_ref[rows, :], do_ref[rows, :]
            kk, vv = k_ref[0:n, :], v_ref[0:n, :]
            delta = jnp.sum(dout.astype(F32) * o_ref[rows, :].astype(F32), axis=-1, keepdims=True)
            lse = jnp.max(lse_ref[rows, :], axis=-1, keepdims=True)
            s = _dot_nt(q, kk) * scale + _mask_row(bias_ref, qi)
            p = _mask_row(mult_ref, qi) * jnp.exp(s - lse)
            ds = (p * (_dot_nt(dout, vv) - delta) * scale).astype(BF16)
            dq_ref[rows, :] = _dot(ds, kk).astype(dq_ref.dtype)
            dk_acc[0:n, :] += _dot_tn(ds, q)
            dv_acc[0:n, :] += _dot_tn(p, dout)
        dk_ref[...] = dk_acc[...].astype(dk_ref.dtype)
        dv_ref[...] = dv_acc[...].astype(dv_ref.dtype)

    full = pl.BlockSpec((T, dh), lambda h: (0, h))
    return pl.pallas_call(
        body, name=name, grid=(ATT_HEADS,),
        in_specs=[full, pl.BlockSpec((T, dh), lambda h: (0, ATT_HEADS + h)),
                  pl.BlockSpec((T, dh), lambda h: (0, 2 * ATT_HEADS + h)),
                  full, pl.BlockSpec((None, T, LANE), lambda h: (h, 0, 0)), full],
        out_specs=[full, full, full],
        out_shape=[jax.ShapeDtypeStruct((T, D), BF16)] * 3,
        scratch_shapes=[pltpu.VMEM((T, dh), F32), pltpu.VMEM((T, dh), F32),
                        pltpu.VMEM((N_MASKS, BLK, BLK), F32), pltpu.VMEM((N_MASKS, BLK, BLK), F32)],
        compiler_params=_params("arbitrary"))(qkv, qkv, qkv, o, lse, do)


def _mesh_pos():
    mx, my, mc = lax.axis_index("x"), lax.axis_index("y"), lax.axis_index("c")
    return mx, my, mc, 4 * mx + 2 * my + mc


def _peer(k, mx, my, mc):
    px, py, pc = mx ^ (k >> 2), my ^ ((k >> 1) & 1), mc ^ (k & 1)
    return (px, py, pc), 4 * px + 2 * py + pc


_SIBLING = 1
_OTHER_CHIPS = (4, 2, 6)
N_CHIP = N_DEV // 2
_X_CHIP, _Y_CHIP, _FAR_CHIP = _OTHER_CHIPS
_PLANS = {"gather": (2, N_DEV - 1), "to_chips": (2, 1 + len(_OTHER_CHIPS)), "pass_on": (1, len(_OTHER_CHIPS)),
          "to_near": (2, 3), "relay": (1, 1), "pass_near": (1, 2), "pass_far": (1, 1),
          "halves": (2, N_CHIP), "chip_sums": (2, len(_OTHER_CHIPS))}


def _copies(kind, items, send_sems, recv_sems):
    mx, my, mc, me = _mesh_pos()
    out = []

    def add(n, src, dst, peer):
        out.append(pltpu.make_async_remote_copy(
            src_ref=src, dst_ref=dst, send_sem=send_sems.at[n], recv_sem=recv_sems.at[n],
            device_id=peer, device_id_type=pl.DeviceIdType.MESH))

    per_item = _PLANS[kind][1]
    sibling = _peer(_SIBLING, mx, my, mc)[0]
    for i, refs in enumerate(items):
        n = i * per_item
        if kind == "gather":
            for k in range(1, N_DEV):
                add(n + k - 1, refs[0], refs[1].at[me], _peer(k, mx, my, mc)[0])
        elif kind in ("to_chips", "to_near"):
            rows = refs[0].shape[0]
            dst = refs[1].at[me] if rows == refs[1].shape[1] else refs[1].at[me, pl.ds(0, rows)]
            to = (_SIBLING,) + _OTHER_CHIPS if kind == "to_chips" else (_SIBLING, _X_CHIP, _Y_CHIP)
            for j, k in enumerate(to):
                add(n + j, refs[0], dst, _peer(k, mx, my, mc)[0])
        elif kind == "pass_on":
            for j, k in enumerate(_OTHER_CHIPS):
                add(n + j, refs[0].at[me ^ k], refs[0].at[me ^ k], sibling)
        elif kind == "relay":
            north = mc == 1
            came_from = me ^ jnp.where(north, _Y_CHIP, _X_CHIP)
            onward = (mx ^ jnp.where(north, 1, 0), my ^ jnp.where(north, 0, 1), mc)
            add(n, refs[0].at[came_from], refs[0].at[came_from], onward)
        elif kind == "pass_near":
            for j, k in enumerate((_X_CHIP, _Y_CHIP)):
                add(n + j, refs[0].at[me ^ k], refs[0].at[me ^ k], sibling)
        elif kind == "pass_far":
            add(n, refs[0].at[me ^ _FAR_CHIP], refs[0].at[me ^ _FAR_CHIP], sibling)
        elif kind == "halves":
            for chip in range(N_CHIP):
                add(n + chip, refs[0].at[2 * chip + 1 - mc], refs[1].at[chip], sibling)
        else:
            for j, k in enumerate(_OTHER_CHIPS):
                peer, to = _peer(k, mx, my, mc)
                add(n + j, refs[0].at[to // 2], refs[1].at[me // 2], peer)
    return out


_HBM = pl.BlockSpec(memory_space=pltpu.HBM)
_SEM = pl.BlockSpec(memory_space=pltpu.SEMAPHORE)
_DATAFLOW = pltpu.SideEffectType.DATAFLOW_SIDE_EFFECTING


def exchange_call(name, waits, starts, deps=()):
    bufs, slot_of = [], {}

    def slots(items):
        out = []
        for item in items:
            for b in item:
                if id(b) not in slot_of:
                    slot_of[id(b)] = len(bufs)
                    bufs.append(b)
            out.append(tuple(slot_of[id(b)] for b in item))
        return out

    wait_plan = [(kind, slots(handle[0])) for kind, handle in waits]
    start_plan = [(kind, slots(items)) for kind, items in starts]
    wait_sems = [s for _, handle in waits for s in handle[1:]]
    n_buf, n_ws, n_start = len(bufs), len(wait_sems), len(starts)

    def body(*refs):
        buf_refs, sems_in = refs[:n_buf], refs[n_buf:n_buf + n_ws]
        outs = refs[n_buf + n_ws + len(deps):]
        pick = lambda plan: [tuple(buf_refs[s] for s in item) for item in plan]
        for wi, (kind, plan) in enumerate(wait_plan):
            copies = _copies(kind, pick(plan), sems_in[2 * wi], sems_in[2 * wi + 1])
            for cp in copies:
                cp.wait_send()
            for cp in copies:
                cp.wait_recv()
        for si, (kind, plan) in enumerate(start_plan):
            for cp in _copies(kind, pick(plan), outs[2 * si], outs[2 * si + 1]):
                cp.start()
        outs[-1][...] = jnp.zeros_like(outs[-1])

    hbm_bufs = [pltpu.with_memory_space_constraint(b, pltpu.HBM) for b in bufs]
    sem_shapes = []
    for kind, plan in start_plan:
        sem_shapes += [pltpu.SemaphoreType.DMA((len(plan) * _PLANS[kind][1],))] * 2
    outs = pl.pallas_call(
        body, name=name,
        out_shape=sem_shapes + [pltpu.HBM(b.shape, b.dtype) for b in bufs] + [jax.ShapeDtypeStruct((8, LANE), F32)],
        in_specs=[_HBM] * n_buf + [_SEM] * n_ws + [_ANY] * len(deps),
        out_specs=[_SEM] * (2 * n_start) + [_HBM] * n_buf + [pl.BlockSpec(memory_space=pltpu.VMEM)],
        input_output_aliases={i: 2 * n_start + i for i in range(n_buf)},
        compiler_params=pltpu.CompilerParams(has_side_effects=_DATAFLOW))(*hbm_bufs, *wait_sems, *deps)
    sems, thru, token = outs[:2 * n_start], outs[2 * n_start:-1], outs[-1]
    through = lambda plan: [tuple(thru[s] for s in item) for item in plan]
    waited = [through(plan) for _, plan in wait_plan]
    handles = [(through(plan), sems[2 * si], sems[2 * si + 1]) for si, (_, plan) in enumerate(start_plan)]
    return waited, handles, token


def gather_small(name, a, deps=()):
    def body(a_ref, *rest):
        o_ref, send_sems, recv_sems, local_sem = rest[len(deps):]
        me = _mesh_pos()[3]
        own = pltpu.make_async_copy(a_ref, o_ref.at[me], local_sem)
        own.start()
        copies = _copies("gather", [(a_ref, o_ref)], send_sems, recv_sems)
        for cp in copies:
            cp.start()
        for cp in copies:
            cp.wait_recv()
        for cp in copies:
            cp.wait_send()
        own.wait()

    return pl.pallas_call(
        body, name=name, in_specs=[_ANY] * (1 + len(deps)), out_specs=_ANY,
        out_shape=jax.ShapeDtypeStruct((N_DEV,) + a.shape, a.dtype),
        scratch_shapes=[pltpu.SemaphoreType.DMA((N_DEV - 1,)), pltpu.SemaphoreType.DMA((N_DEV - 1,)),
                        pltpu.SemaphoreType.DMA],
        compiler_params=pltpu.CompilerParams(has_side_effects=True))(a, *deps)


def _adamw_math(w, g, m, v):
    m2 = ADAM_B1 * m + (1.0 - ADAM_B1) * g
    v2 = ADAM_B2 * v + (1.0 - ADAM_B2) * (g * g)
    m_hat = m2 / (1.0 - ADAM_B1 ** ADAM_STEP)
    v_hat = v2 / (1.0 - ADAM_B2 ** ADAM_STEP)
    delta = -ADAM_LR * (m_hat / (jnp.sqrt(v_hat) + ADAM_EPS) + ADAM_WD * w)
    return delta, m2, v2


def chip_sum(name, a, half):
    _, r, c = a.shape
    tr = r
    chip = 2 * lax.axis_index("x") + lax.axis_index("y")
    where = jnp.stack([lax.axis_index("c"), chip ^ 1, chip ^ 2, chip ^ 3]).astype(jnp.int32)

    def body(where_ref, a_ref, h_ref, o_ref):
        del where_ref
        o_ref[...] = (a_ref[...].astype(F32) + h_ref[...].astype(F32)).astype(o_ref.dtype)

    blk = pl.BlockSpec((None, tr, c), lambda g, i, where: (where[1 + g], i, 0))
    grid_spec = pltpu.PrefetchScalarGridSpec(
        num_scalar_prefetch=1, grid=(N_CHIP - 1, r // tr),
        in_specs=[pl.BlockSpec((None, None, tr, c), lambda g, i, where: (where[1 + g], where[0], i, 0)), blk],
        out_specs=blk)
    return pl.pallas_call(
        body, name=name, grid_spec=grid_spec, out_shape=jax.ShapeDtypeStruct((N_CHIP, r, c), BF16),
        compiler_params=_params("parallel", "parallel"))(where, a.reshape(N_CHIP, 2, r, c), half)


def adamw(name, w, m, v, l, land, a, half, prev=None):
    L, r, c = w.shape
    cp = land.shape[2]
    tr = _pick(r, (256, 176, 128, 64, 32, 16, 8))

    def body(w_ref, m_ref, v_ref, land_ref, a_ref, half_ref, *rest):
        g_ref, d_ref, m2_ref, v2_ref = rest[-4:]
        chip = _mesh_pos()[3] // 2
        mine = a_ref[:, pl.ds(0, c)].astype(F32) + half_ref[:, pl.ds(0, c)].astype(F32)
        g = None
        for s in range(N_CHIP):
            part = jnp.where(chip == s, mine, land_ref[s, :, pl.ds(0, c)].astype(F32))
            g = part if g is None else g + part
        delta, m2, v2 = _adamw_math(w_ref[...], g, m_ref[...], v_ref[...])
        g_ref[...] = g
        d_ref[...] = delta
        m2_ref[...] = m2
        v2_ref[...] = v2

    blk = pl.BlockSpec((None, tr, c), lambda i: (l, i, 0))
    shape = jax.ShapeDtypeStruct((L, r, c), F32)
    extra = [] if prev is None else list(prev)
    return pl.pallas_call(
        body, name=name, grid=(r // tr,),
        in_specs=[blk, blk, blk, pl.BlockSpec((N_CHIP, tr, cp), lambda i: (0, i, 0)),
                  pl.BlockSpec((None, tr, cp), lambda i: (_mesh_pos()[3], i, 0)),
                  pl.BlockSpec((None, tr, cp), lambda i: (_mesh_pos()[3] // 2, i, 0))] + [_ANY] * len(extra),
        out_specs=[blk] * 4, out_shape=[shape] * 4,
        input_output_aliases={6 + k: k for k in range(len(extra))},
        compiler_params=_params("parallel"))(w, m, v, land, a, half, *extra)


def adamw_columns(name, w, m, v, land, a, half):
    r, _, D = w.shape
    tc = _pick(D, (256, 128))

    def body(w_ref, m_ref, v_ref, land_ref, a_ref, half_ref, g_ref, d_ref, m2_ref, v2_ref):
        chip = _mesh_pos()[3] // 2
        mine = a_ref[...].astype(F32) + half_ref[...].astype(F32)
        g = None
        for s in range(N_CHIP):
            part = jnp.where(chip == s, mine, land_ref[s].astype(F32))
            g = part if g is None else g + part
        flat = lambda ref: ref[...].reshape(r, tc)
        delta, m2, v2 = _adamw_math(flat(w_ref), g, flat(m_ref), flat(v_ref))
        for ref, val in ((g_ref, g), (d_ref, delta), (m2_ref, m2), (v2_ref, v2)):
            ref[...] = val.reshape(r, 1, tc)

    blk = pl.BlockSpec((r, 1, tc), lambda i: (0, 0, i))
    shape = jax.ShapeDtypeStruct((r, 1, D), F32)
    return pl.pallas_call(
        body, name=name, grid=(D // tc,),
        in_specs=[blk, blk, blk, pl.BlockSpec((N_CHIP, r, tc), lambda i: (0, 0, i)),
                  pl.BlockSpec((None, r, tc), lambda i: (_mesh_pos()[3], 0, i)),
                  pl.BlockSpec((None, r, tc), lambda i: (_mesh_pos()[3] // 2, 0, i))],
        out_specs=[blk] * 4, out_shape=[shape] * 4,
        compiler_params=_params("parallel"))(w, m, v, land, a, half)


def adamw_small(name, w, m, v, parts):
    n = w.shape[1]

    def body(w_ref, m_ref, v_ref, p_ref, g_ref, d_ref, m2_ref, v2_ref):
        g = p_ref[0:1, :]
        for s in range(1, N_DEV):
            g = g + p_ref[s:s + 1, :]
        delta, m2, v2 = _adamw_math(w_ref[...], g, m_ref[...], v_ref[...])
        g_ref[...] = g
        d_ref[...] = delta
        m2_ref[...] = m2
        v2_ref[...] = v2

    shape = jax.ShapeDtypeStruct((1, n), F32)
    return pl.pallas_call(body, name=name, out_shape=[shape] * 4,
                          compiler_params=pltpu.CompilerParams(vmem_limit_bytes=VMEM_LIMIT_BYTES))(w, m, v, parts)


def _rope_tables(positions):
    half = RET_DK // 2
    inv_freq = 1.0 / jnp.power(RET_THETA_BASE, jnp.linspace(0.0, 1.0, half, dtype=F32))
    ang = positions.astype(F32)[:, None] * inv_freq
    cos, sin = jnp.cos(ang), jnp.sin(ang)
    cosf = jnp.repeat(cos, 2, axis=-1)
    sins = jnp.stack([-sin, sin], axis=-1).reshape(cosf.shape)
    return cosf, sins


def _pad_to(a, axis, size):
    pad = [(0, 0)] * a.ndim
    pad[axis] = (0, size - a.shape[axis])
    return jnp.pad(a, pad)


def _round_up(n, m):
    return -(-n // m) * m


def kernel(x, p, positions, attn_norm_w, ffn_norm_w, ple_norm_w, final_norm_w, ab_w_in, ab_gla_gate_up, ab_gla_gate_b, ab_ret_norm_w, ab_gla_norm_w, ab_w_out, c_w_qkv, c_w_out, ffn_w_gate, ffn_w_up, ffn_w_down, ple_w_proj, ple_w_gate, loss_target, m_attn_norm_w, m_ffn_norm_w, m_ple_norm_w, m_final_norm_w, m_ab_w_in, m_ab_gla_gate_up, m_ab_gla_gate_b, m_ab_ret_norm_w, m_ab_gla_norm_w, m_ab_w_out, m_c_w_qkv, m_c_w_out, m_ffn_w_gate, m_ffn_w_up, m_ffn_w_down, m_ple_w_proj, m_ple_w_gate, v_attn_norm_w, v_ffn_norm_w, v_ple_norm_w, v_final_norm_w, v_ab_w_in, v_ab_gla_gate_up, v_ab_gla_gate_b, v_ab_ret_norm_w, v_ab_gla_norm_w, v_ab_w_out, v_c_w_qkv, v_c_w_out, v_ffn_w_gate, v_ffn_w_up, v_ffn_w_down, v_ple_w_proj, v_ple_w_gate):
    T, D = x.shape[1], x.shape[2]
    depth = attn_norm_w.shape[0]
    assert ab_w_in.shape[0] == 1 and c_w_qkv.shape[0] == 1 and depth == 2, "one even and one odd layer"
    me = 4 * lax.axis_index("x") + 2 * lax.axis_index("y") + lax.axis_index("c")
    in_shard = ab_w_in.shape[2]
    in_width = in_shard * N_DEV
    assert in_width == OFF_LR + GLA_GATE_RANK
    fs = ffn_w_gate.shape[2]
    fp = _round_up(fs, LANE)
    gu_cols = ab_gla_gate_up.shape[2]

    bf = lambda a: a.astype(BF16)
    tr_ = lambda a: jnp.swapaxes(a, -1, -2)
    wg_t, wu_t = tr_(ffn_w_gate), tr_(ffn_w_up)
    srcs = {"w_in": bf(tr_(ab_w_in[0]))}
    group_keys = [["w_in"], ["gu", "w_oab"], ["wg0", "wu0"], ["wd0", "wpg0", "wpp0"], ["w_qkv", "w_oc"],
                  ["wg1", "wu1"], ["wd1", "wpg1", "wpp1"]]
    G_IN, G_OUT, G_QKV = 0, 1, 4
    g_ffn = lambda layer: (2, 3) if layer == 0 else (5, 6)

    def landing(key):
        a = srcs[key]
        rows = fp if key[:2] in ("wg", "wu", "wd") else a.shape[0]
        buf = lax.empty((N_DEV, rows) + a.shape[1:], a.dtype)
        if rows > a.shape[0]:
            zeros = jnp.zeros((N_DEV, rows - a.shape[0]) + a.shape[1:], a.dtype)
            buf = lax.dynamic_update_slice(buf, zeros, (0, a.shape[0]) + (0,) * (a.ndim - 1))
        return lax.dynamic_update_slice(buf, a[None], (me,) + (0,) * a.ndim)

    _, chip_handles, gather_token = exchange_call(
        "gather_start_in", [], [("to_chips", [(srcs[k], landing(k)) for k in group_keys[G_IN]])])
    (gather_token, w_out_, gu_, w_qkv_, w_oc_, wg_, wu_, wd_, wpg_, wpp_) = lax.optimization_barrier(
        (gather_token, ab_w_out, ab_gla_gate_up, c_w_qkv, c_w_out, wg_t, wu_t, ffn_w_down, ple_w_gate, ple_w_proj))
    srcs.update(w_oab=bf(w_out_[0]), gu=gu_[0], w_qkv=bf(w_qkv_[0]), w_oc=bf(w_oc_[0]))
    for l in range(depth):
        srcs[f"wg{l}"] = bf(wg_[l])
        srcs[f"wu{l}"] = bf(wu_[l])
        srcs[f"wd{l}"] = bf(wd_[l])
        srcs[f"wpg{l}"] = bf(wpg_[l])
        srcs[f"wpp{l}"] = bf(wpp_[l])
    relayed = g_ffn(depth - 1)
    items_of = lambda gi: [(srcs[k], landing(k)) for k in group_keys[gi]]
    direct = [gi for gi in range(1, len(group_keys)) if gi not in relayed]
    _, more, gather_token = exchange_call(
        "gather_start", [], [("to_chips", items_of(gi)) for gi in direct] + [("to_near", items_of(relayed[0]))],
        deps=(gather_token,))
    chip_handles = dict(zip([G_IN] + direct + [relayed[0]], chip_handles + more))
    weights = {}

    def gather_wait(gi, dep):
        lands = [(land,) for _, land in chip_handles[gi][0]]
        if gi not in relayed:
            _, (passing,), _ = exchange_call(
                f"gather{gi}_pass", [("to_chips", chip_handles[gi])], [("pass_on", lands)], deps=(dep,))
            (complete,), _, _ = exchange_call(f"gather{gi}_done", [("pass_on", passing)], [])
        else:
            starts = [("relay", lands), ("pass_near", lands)]
            if gi + 1 in relayed:
                starts.append(("to_near", items_of(gi + 1)))
            _, started, _ = exchange_call(f"gather{gi}_relay", [("to_near", chip_handles[gi])], starts, deps=(dep,))
            relay, pass_near = started[:2]
            if gi + 1 in relayed:
                chip_handles[gi + 1] = started[2]
            _, (pass_far,), _ = exchange_call(f"gather{gi}_far", [("relay", relay)], [("pass_far", relay[0])])
            (_, complete), _, _ = exchange_call(
                f"gather{gi}_done", [("pass_near", (pass_far[0],) + pass_near[1:]), ("pass_far", pass_far)], [])
        weights.update(zip(group_keys[gi], [land for (land,) in complete]))

    gb = ab_gla_gate_b
    hn_w = jnp.concatenate([ab_ret_norm_w, ab_gla_norm_w], axis=1)
    cosf, sins = _rope_tables(positions[0])
    p_bf = bf(p[:, 0])

    xs = x[0]
    saved = []
    for i in range(depth):
        nm = f"l{i}_"
        w_attn, w_ffn, w_ple = attn_norm_w[i:i + 1], ffn_norm_w[i:i + 1], ple_norm_w[i:i + 1]
        (xn,) = rowwise(nm + "norm_attn", lambda a, w: (_rms(a, w),), T, [("row", xs), ("full", w_attn)],
                        [("row", D, BF16)], deps=(gather_token,) if i == 0 else ())
        if i % 2 == 0:
            gather_wait(G_IN, xn)
            w_in_t = weights["w_in"].reshape(1, 1, in_width, D)
            w_lr_t = _pad_to(w_in_t[0, 0, OFF_LR:], 0, LANE).reshape(1, 1, LANE, D)
            z = mmt_fwd(nm + "mm_in", xn, w_in_t, 0, F32, n=OFF_LR)
            glr = mmt_fwd(nm + "mm_lr", xn, w_lr_t, 0, F32)
            oraw = retention_fwd(nm + "ret_fwd", z, cosf, sins, RET_V + GLA_V)
            gather_wait(G_OUT, oraw)
            w_oab = weights["w_oab"].reshape(1, 1, D, D)
            gu_full = _pad_to(weights["gu"].transpose(1, 0, 2).reshape(GLA_GATE_RANK, GLA_QK), 0, LANE)
            oraw = gla_fwd(nm + "gla_fwd", z, glr, gu_full, gb, oraw)
            o = headnorm_fwd(nm + "headnorm_fwd", oraw, z, hn_w)
            h1, hn = mm_add_norm(nm + "mm_out", o, w_oab, xs, w_ffn)
            mixer_saved = (z, glr, oraw, o)
        else:
            gather_wait(G_QKV, xn)
            w_qkv = weights["w_qkv"].reshape((1,) + weights["w_qkv"].shape)
            w_oc = weights["w_oc"].reshape(1, 1, D, D)
            qkv = mm_nn(nm + "mm_qkv", xn, w_qkv, 0, BF16)
            o, lse = attn_fwd(nm + "attn_fwd", qkv)
            h1, hn = mm_add_norm(nm + "mm_out", o, w_oc, xs, w_ffn)
            mixer_saved = (qkv, o, lse)
        gather_wait(g_ffn(i)[0], hn)
        wg = weights[f"wg{i}"].reshape(1, N_DEV, fp, D)
        wu = weights[f"wu{i}"].reshape(1, N_DEV, fp, D)
        g, u, act = ffn_gate_up(nm + "ffn_gate_up", hn, wg, wu)
        gather_wait(g_ffn(i)[1], act)
        wd = weights[f"wd{i}"].reshape(1, 1, N_DEV * fp, D)
        wpg = weights[f"wpg{i}"].reshape(1, 1, D, D)
        wpp = weights[f"wpp{i}"].reshape((1,) + weights[f"wpp{i}"].shape)
        h2, pn = mm_add_norm(nm + "mm_down", act, wd, h1, w_ple)
        x_next, s, e = ple_fwd(nm + "ple", pn, wpg, p_bf[i], wpp, h2)
        mixer_w = (w_in_t, w_lr_t, w_oab, gu_full) if i % 2 == 0 else (w_qkv, w_oc)
        saved.append((xs, xn, mixer_saved, mixer_w, (wg, wu, wd, wpg), h1, hn, g, u, act, h2, pn, s, e))
        xs = x_next

    def loss_fn(a, w, t):
        diff = _rms(a, w) - t
        dx, dw = _rms_bwd(a, w, diff * (1.0 / D))
        part = 0.5 * jnp.sum(jnp.mean(diff * diff, axis=-1, keepdims=True), axis=0, keepdims=True)
        return dx, dw, jnp.broadcast_to(part, (1, LANE))

    dx, d_final_w, loss_part = rowwise("loss_head", loss_fn, T,
                                       [("row", xs), ("full", final_norm_w[None, :]), ("row", loss_target[0])],
                                       [("row", D, F32), ("acc", D), ("acc", LANE)])
    loss = lax.psum(loss_part[0, 0], ("x", "y", "c"))

    grads = {}
    on_chip = []
    scatters = []

    def scatter_start(name, keys, deps=()):
        waits = [("halves", on_chip[0][1])] if on_chip else []
        starts = [("halves", [(grads[k], lax.empty((N_CHIP,) + grads[k].shape[1:], BF16)) for k in keys])] if keys else []
        waited, handles, token = exchange_call(name, waits, starts, deps=deps)
        if on_chip:
            done_keys, _ = on_chip.pop()
            sums = [chip_sum(f"{name}_sum{j}", a, half) for j, (a, half) in enumerate(waited[0])]
            _, (handle,), token = exchange_call(
                name + "_chips", [], [("chip_sums", [(cs, lax.empty(cs.shape, BF16)) for cs in sums])])
            scatters.append((done_keys, handle, waited[0]))
        if keys:
            on_chip.append((keys, handles[0]))
        return token

    d_attn_w, d_ffn_w, d_ple_w = [None] * depth, [None] * depth, [None] * depth
    for i in reversed(range(depth)):
        nm = f"l{i}_b_"
        xs_i, xn, mixer_saved, mixer_w, (wg, wu, wd, wpg), h1, hn, g, u, act, h2, pn, s, e = saved[i]
        w_attn, w_ffn, w_ple = attn_norm_w[i:i + 1], ffn_norm_w[i:i + 1], ple_norm_w[i:i + 1]

        def ple_bwd(d, sv, ev):
            gate = _sigmoid(sv)
            return d * gate, d * ev * gate * (1.0 - gate)

        de, ds = rowwise(nm + "ple_out", ple_bwd, T, [("row", dx), ("row", s), ("row", e)],
                         [("row", D, BF16), ("row", D, BF16)], deps=(loss.reshape(1, 1),) if i == depth - 1 else ())
        grads[("ple_w_proj", i)] = mm_tn(nm + "mm_ple_proj_w", p_bf[i], de, N_DEV, BF16)
        grads[("ple_w_gate", i)] = mm_tn(nm + "mm_ple_gate_w", pn, ds, 1, BF16).reshape(N_DEV, D // N_DEV, D)
        dpn = mm_nt(nm + "mm_ple_gate_x", ds, wpg, 0, F32)

        def norm_bwd_add(a, w, dn, dres):
            dxx, dw = _rms_bwd(a, w, dn)
            tot = dres + dxx
            return tot, tot, dw

        dh2, dh2_bf, d_ple_w[i] = rowwise(nm + "norm_ple", norm_bwd_add, T,
                                          [("row", h2), ("full", w_ple), ("row", dpn), ("row", dx)],
                                          [("row", D, F32), ("row", D, BF16), ("acc", D)])
        grads[("ffn_w_down", i)] = mm_tn(nm + "mm_down_w", act, dh2_bf, 1, BF16).reshape(N_DEV, fp, D)
        token = scatter_start(nm + "scatter_ple_down", [("ple_w_proj", i), ("ple_w_gate", i), ("ffn_w_down", i)])
        dg, du = ffn_down_bwd(nm + "ffn_down_x", dh2_bf, wd, g, u, deps=(token,))
        grads[("ffn_w_gate", i)] = mmt_dw(nm + "mm_gate_w", dg, hn, N_DEV, BF16)
        grads[("ffn_w_up", i)] = mmt_dw(nm + "mm_up_w", du, hn, N_DEV, BF16)
        token = scatter_start(nm + "scatter_gate_up", [("ffn_w_gate", i), ("ffn_w_up", i)])
        dhn = mmt_dx_pair(nm + "mm_gate_up_x", dg, wg, du, wu, F32, deps=(token,))
        dh1, dh1_bf, d_ffn_w[i] = rowwise(nm + "norm_ffn", norm_bwd_add, T,
                                          [("row", h1), ("full", w_ffn), ("row", dhn), ("row", dh2)],
                                          [("row", D, F32), ("row", D, BF16), ("acc", D)])
        if i % 2 == 0:
            z, glr, oraw, o = mixer_saved
            w_in_t, w_lr_t, w_oab, gu_full = mixer_w
            grads[("ab_w_out", 0)] = mm_tn(nm + "mm_out_w", o, dh1_bf, 1, BF16).reshape(N_DEV, D // N_DEV, D)
            token = scatter_start(nm + "scatter_out", [("ab_w_out", 0)])
            do = mm_nt(nm + "mm_out_x", dh1_bf, w_oab, 0, F32, deps=(token,))
            d_oraw, d_gates, d_hn_w = headnorm_bwd(nm + "headnorm", oraw, z, hn_w, do)
            d_rq, d_rk, d_rv = retention_bwd(nm + "ret", z, cosf, sins, d_oraw)
            d_gq, d_gk, d_gv, d_glr4, d_gu, d_gb = gla_bwd(nm + "gla", z, glr, gu_full, gb, d_oraw)
            dz = jnp.concatenate([d_rq, d_rk, d_rv, d_gates[:, :RET_V], d_gq, d_gk, d_gv, d_gates[:, RET_V:]], axis=1)
            (d_glr,) = rowwise(nm + "sum_lr", lambda *a: (a[0] + a[1] + a[2] + a[3],), T,
                               [("row", d_glr4[hh]) for hh in range(GLA_HEADS)], [("row", LANE, BF16)])
            dwt_in = mmt_dw(nm + "mm_in_w", dz, xn, 1, BF16, rows=in_width)
            dwt_in = mmt_dw_rows(nm + "mm_lr_w", d_glr, xn, dwt_in, OFF_LR, GLA_GATE_RANK)
            grads[("ab_w_in", 0)] = dwt_in.reshape(N_DEV, in_shard, D)
            token = scatter_start(nm + "scatter_in", [("ab_w_in", 0)])
            dxn_a = mmt_dx_wide(nm + "mm_in_x", dz, w_in_t, F32, n=OFF_LR, deps=(token,))
            token = scatter_start(nm + "scatter_in_on", [], deps=(dxn_a,))
            dxn_b = mmt_dx(nm + "mm_lr_x", d_glr, w_lr_t, 0, F32, deps=(token,))
        else:
            qkv, o, lse = mixer_saved
            w_qkv, w_oc = mixer_w
            grads[("c_w_out", 0)] = mm_tn(nm + "mm_out_w", o, dh1_bf, 1, BF16).reshape(N_DEV, D // N_DEV, D)
            do = mm_nt(nm + "mm_out_x", dh1_bf, w_oc, 0, BF16)
            dq, dk, dv = attn_bwd(nm + "attn", qkv, o, lse, do)
            dqkv = jnp.concatenate([dq, dk, dv], axis=1)
            grads[("c_w_qkv", 0)] = mm_tn(nm + "mm_qkv_w", xn, dqkv, N_DEV, BF16)
            token = scatter_start(nm + "scatter_attn", [("c_w_out", 0), ("c_w_qkv", 0)])
            dxn_a = mm_nt_wide(nm + "mm_qkv_x", dqkv, w_qkv, F32, deps=(token,))
            dxn_b = None
        dxn = [dxn_a] if dxn_b is None else [dxn_a, dxn_b]

        def norm_bwd_in(a, w, *rest):
            dxx, dw = _rms_bwd(a, w, sum(rest[1:-1], rest[0]))
            return rest[-1] + dxx, dw

        dx, d_attn_w[i] = rowwise(nm + "norm_attn", norm_bwd_in, T,
                                  [("row", xs_i), ("full", w_attn)] + [("row", d) for d in dxn] + [("row", dh1)],
                                  [("row", D, F32), ("acc", D)])

    small_names = ["attn_norm_w", "ffn_norm_w", "ple_norm_w", "final_norm_w", "ab_gla_gate_b", "ab_ret_norm_w",
                   "ab_gla_norm_w"]
    small_grads = [jnp.concatenate(d_attn_w, 0), jnp.concatenate(d_ffn_w, 0), jnp.concatenate(d_ple_w, 0), d_final_w[0],
                   d_gb, d_hn_w[:, :RET_V], d_hn_w[:, RET_V:]]
    small_w = [attn_norm_w, ffn_norm_w, ple_norm_w, final_norm_w, ab_gla_gate_b, ab_ret_norm_w, ab_gla_norm_w]
    small_m = [m_attn_norm_w, m_ffn_norm_w, m_ple_norm_w, m_final_norm_w, m_ab_gla_gate_b, m_ab_ret_norm_w, m_ab_gla_norm_w]
    small_v = [v_attn_norm_w, v_ffn_norm_w, v_ple_norm_w, v_final_norm_w, v_ab_gla_gate_b, v_ab_ret_norm_w, v_ab_gla_norm_w]
    sizes = [int(np.prod(a.shape)) for a in small_w]
    n_gu = GLA_GATE_RANK * GLA_QK
    n_small = _round_up(sum(sizes) + n_gu, LANE)
    pack = lambda parts: _pad_to(jnp.concatenate([a.reshape(-1) for a in parts]), 0, n_small)[None, :]
    small_part = pack(small_grads + [d_gu[:GLA_GATE_RANK]])

    cols_first = lambda a: jnp.transpose(a, (2, 0, 1))
    big_w = dict(ab_w_in=tuple(cols_first(a) for a in (ab_w_in, m_ab_w_in, v_ab_w_in)),
                 ab_w_out=(ab_w_out, m_ab_w_out, v_ab_w_out),
                 c_w_qkv=(c_w_qkv, m_c_w_qkv, v_c_w_qkv), c_w_out=(c_w_out, m_c_w_out, v_c_w_out),
                 ffn_w_gate=(wg_t, tr_(m_ffn_w_gate), tr_(v_ffn_w_gate)),
                 ffn_w_up=(wu_t, tr_(m_ffn_w_up), tr_(v_ffn_w_up)),
                 ffn_w_down=(ffn_w_down, m_ffn_w_down, v_ffn_w_down), ple_w_proj=(ple_w_proj, m_ple_w_proj, v_ple_w_proj),
                 ple_w_gate=(ple_w_gate, m_ple_w_gate, v_ple_w_gate))
    if on_chip:
        scatter_start("scatter_last", [], deps=(dx,))
    results, last = {}, dx
    for gi, (keys, handle, partials) in enumerate(scatters):
        (arrived,), _, _ = exchange_call(f"scatter_wait{gi}", [("chip_sums", handle)], [], deps=(last,))
        for (n, l), (_, land), (a, half) in zip(keys, arrived, partials):
            if n == "ab_w_in":
                results[n] = adamw_columns(f"adamw_{n}", *big_w[n], land, a, half)
            else:
                results[n] = adamw(f"adamw_{n}{l}", *big_w[n], l, land, a, half, prev=results.get(n))
            last = results[n][0]
    for n in ("ffn_w_gate", "ffn_w_up"):
        results[n] = [tr_(a) for a in results[n]]
    results["ab_w_in"] = [jnp.transpose(a, (1, 2, 0)) for a in results["ab_w_in"]]
    small_parts = gather_small("gather_small", small_part, deps=(last,)).reshape(N_DEV, n_small)

    gu_off = sum(sizes)
    own_cols = lambda a: lax.dynamic_slice_in_dim(a.reshape(GLA_GATE_RANK, GLA_QK), me * gu_cols, gu_cols, axis=1)
    small_res = adamw_small("adamw_small", pack(small_w + [jnp.zeros((n_gu,), F32)]),
                            pack(small_m + [jnp.zeros((n_gu,), F32)]), pack(small_v + [jnp.ones((n_gu,), F32)]),
                            small_parts)
    g_gu_full = small_res[0][0, gu_off:gu_off + n_gu]
    g_gu = own_cols(g_gu_full)[None]
    gu_res = adamw_small("adamw_gate_up", *[_pad_to(a.reshape(1, -1), 1, _round_up(a.size, LANE)) for a in
                                            (ab_gla_gate_up, m_ab_gla_gate_up, v_ab_gla_gate_up)],
                         jnp.concatenate([_pad_to(g_gu.reshape(1, -1), 1, _round_up(g_gu.size, LANE)),
                                          jnp.zeros((N_DEV - 1, _round_up(g_gu.size, LANE)), F32)], axis=0))
    for k in range(4):
        off = 0
        for n, a, sz in zip(small_names, small_w, sizes):
            results.setdefault(n, [None] * 4)[k] = small_res[k][0, off:off + sz].reshape(a.shape)
            off += sz
        results.setdefault("ab_gla_gate_up", [None] * 4)[k] = gu_res[k][0, :g_gu.size].reshape(ab_gla_gate_up.shape)

    order = ["attn_norm_w", "ffn_norm_w", "ple_norm_w", "final_norm_w", "ab_w_in", "ab_gla_gate_up", "ab_gla_gate_b",
             "ab_ret_norm_w", "ab_gla_norm_w", "ab_w_out", "c_w_qkv", "c_w_out", "ffn_w_gate", "ffn_w_up", "ffn_w_down",
             "ple_w_proj", "ple_w_gate"]
    return (loss, dx[None], *[results[n][0] for n in order], *[results[n][1] for n in order],
            *[results[n][2] for n in order], *[results[n][3] for n in order])
```

```python
import math

import numpy as np
import jax
import jax.numpy as jnp
from jax import lax
from jax.experimental import pallas as pl
from jax.experimental.pallas import tpu as pltpu

F32 = jnp.float32
BF16 = jnp.bfloat16
HIGHEST = lax.Precision.HIGHEST

N_DEV = 8
VMEM_LIMIT_BYTES = 48 * 1024 * 1024
LANE = 128
NORM_EPS = 1e-6

RET_HEADS, RET_DK, RET_DV = 4, 256, 256
RET_THETA_BASE = 10000.0
GLA_HEADS, GLA_DK, GLA_DV = 4, 128, 256
GLA_GATE_RANK = 16
GLA_GATE_NORM = 16.0
CHUNK = 64
ATT_HEADS = 16
DILATED_BRANCHES = ((128, 1), (512, 4), (2048, 16))
BLK = 256

ADAM_LR, ADAM_B1, ADAM_B2, ADAM_EPS, ADAM_WD, ADAM_STEP = 0.001, 0.9, 0.999, 1e-08, 0.01, 10

RET_QK = RET_HEADS * RET_DK
RET_V = RET_HEADS * RET_DV
GLA_QK = GLA_HEADS * GLA_DK
GLA_V = GLA_HEADS * GLA_DV
OFF_RQ, OFF_RK, OFF_RV, OFF_RG = 0, RET_QK, 2 * RET_QK, 2 * RET_QK + RET_V
OFF_GQ = OFF_RG + RET_V
OFF_GK = OFF_GQ + GLA_QK
OFF_GV = OFF_GK + GLA_QK
OFF_GG = OFF_GV + GLA_V
OFF_LR = OFF_GG + GLA_V


def _params(*sem):
    return pltpu.CompilerParams(dimension_semantics=sem or None, vmem_limit_bytes=VMEM_LIMIT_BYTES)


def _pick(n, cands):
    for c in cands:
        if n % c == 0:
            return c
    raise ValueError(f"no tile for {n} in {cands}")


_NN = (((1,), (0,)), ((), ()))
_NT = (((1,), (1,)), ((), ()))
_TN = (((0,), (0,)), ((), ()))
_ANY = pl.BlockSpec(memory_space=pl.ANY)
MAX_CONTRACT = 2048
_TILES = (1024, 768, 512, 256, 128)


def _mm_call(name, dims, grid, in_specs, out_spec, out_shape, args, deps=()):
    steps = grid[2]
    assert steps == 1 or out_shape.dtype == F32

    def body(a_ref, b_ref, *rest):
        o_ref = rest[len(deps)]
        part = lax.dot_general(a_ref[...].astype(BF16), b_ref[...].astype(BF16), dims, preferred_element_type=F32)
        if steps == 1:
            o_ref[...] = part.astype(o_ref.dtype)
        else:
            _accumulate(o_ref, part, pl.program_id(2) == 0)

    return pl.pallas_call(
        body, name=name, grid=grid, in_specs=list(in_specs) + [_ANY] * len(deps), out_specs=out_spec,
        out_shape=out_shape, compiler_params=_params("parallel", "parallel", "arbitrary"))(*args, *deps)


def mm_nn(name, a, w, l, out_dtype, deps=()):
    _, J, K, n = w.shape
    M = a.shape[0]
    tm, tn, tk = _pick(M, _TILES), _pick(n, _TILES), _pick(K, (MAX_CONTRACT,) + _TILES)
    nt = n // tn
    return _mm_call(
        name, _NN, (M // tm, J * nt, K // tk),
        [pl.BlockSpec((tm, tk), lambda i, j, k: (i, k)),
         pl.BlockSpec((None, None, tk, tn), lambda i, j, k: (l, j // nt, k, j % nt))],
        pl.BlockSpec((tm, tn), lambda i, j, k: (i, j)),
        jax.ShapeDtypeStruct((M, J * n), out_dtype), (a, w), deps)


def mm_nt(name, a, w, l, out_dtype, deps=()):
    _, J, K, n = w.shape
    M = a.shape[0]
    tm, tq, tc = _pick(M, _TILES), _pick(K, _TILES), _pick(n, (MAX_CONTRACT,) + _TILES)
    nc = n // tc
    return _mm_call(
        name, _NT, (M // tm, K // tq, J * nc),
        [pl.BlockSpec((tm, tc), lambda i, q, c: (i, c)),
         pl.BlockSpec((None, None, tq, tc), lambda i, q, c: (l, c // nc, q, c % nc))],
        pl.BlockSpec((tm, tq), lambda i, q, c: (i, q)),
        jax.ShapeDtypeStruct((M, K), out_dtype), (a, w), deps)


def mm_tn(name, x, dy, J, out_dtype, deps=()):
    M, K = x.shape
    n = dy.shape[1] // J
    tp, tn = _pick(K, _TILES), _pick(n, _TILES)
    nt = n // tn
    assert M <= MAX_CONTRACT
    return _mm_call(
        name, _TN, (K // tp, J * nt, 1),
        [pl.BlockSpec((M, tp), lambda i, j, r: (0, i)),
         pl.BlockSpec((M, tn), lambda i, j, r: (0, j))],
        pl.BlockSpec((None, tp, tn), lambda i, j, r: (j // nt, i, j % nt)),
        jax.ShapeDtypeStruct((J, K, n), out_dtype), (x, dy), deps)


def mmt_fwd(name, a, wt, l, out_dtype, n=None, deps=()):
    _, J, rows, K = wt.shape
    n = rows if n is None else n
    M = a.shape[0]
    tm, tn = _pick(M, _TILES), _pick(n, _TILES)
    nt = n // tn
    assert K <= MAX_CONTRACT
    return _mm_call(
        name, _NT, (M // tm, J * nt, 1),
        [pl.BlockSpec((tm, K), lambda i, j, k: (i, 0)),
         pl.BlockSpec((None, None, tn, K), lambda i, j, k: (l, j // nt, j % nt, 0))],
        pl.BlockSpec((tm, tn), lambda i, j, k: (i, j)),
        jax.ShapeDtypeStruct((M, J * n), out_dtype), (a, wt), deps)


def mmt_dx(name, dy, wt, l, out_dtype, n=None, deps=()):
    _, J, rows, K = wt.shape
    n = rows if n is None else n
    M = dy.shape[0]
    tm, tq, tc = _pick(M, _TILES), _pick(K, _TILES), _pick(n, _TILES)
    nc = n // tc
    return _mm_call(
        name, _NN, (M // tm, K // tq, J * nc),
        [pl.BlockSpec((tm, tc), lambda i, q, c: (i, c)),
         pl.BlockSpec((None, None, tc, tq), lambda i, q, c: (l, c // nc, c % nc, q))],
        pl.BlockSpec((tm, tq), lambda i, q, c: (i, q)),
        jax.ShapeDtypeStruct((M, K), out_dtype), (dy, wt), deps)


WIDE_TILE = 512


def _wide_call(name, body, M, K, a, w, a_spec, w_spec, out_dtype, deps):
    def kernel_body(a_ref, w_ref, *rest):
        o_ref = rest[len(deps)]
        o_ref[...] = body(a_ref, w_ref).astype(o_ref.dtype)

    return pl.pallas_call(
        kernel_body, name=name, grid=(M // WIDE_TILE, K // WIDE_TILE),
        in_specs=[a_spec, w_spec] + [_ANY] * len(deps),
        out_specs=pl.BlockSpec((WIDE_TILE, WIDE_TILE), lambda i, q: (i, q)),
        out_shape=jax.ShapeDtypeStruct((M, K), out_dtype),
        compiler_params=_params("parallel", "parallel"))(a, w, *deps)


def mmt_dx_wide(name, dy, wt, out_dtype, n=None, deps=()):
    _, J, rows, K = wt.shape
    n = rows if n is None else n
    M = dy.shape[0]

    def body(dy_ref, w_ref):
        return jnp.dot(dy_ref[...].astype(BF16), w_ref[...].reshape(J * n, WIDE_TILE), preferred_element_type=F32)

    return _wide_call(name, body, M, K, dy, wt,
                      pl.BlockSpec((WIDE_TILE, J * n), lambda i, q: (i, 0)),
                      pl.BlockSpec((None, J, n, WIDE_TILE), lambda i, q: (0, 0, 0, q)), out_dtype, deps)


def mmt_dx_pair(name, dy1, wt1, dy2, wt2, out_dtype, deps=()):
    _, J, n, K = wt1.shape
    M = dy1.shape[0]

    def body(dy1_ref, w1_ref, dy2_ref, w2_ref, *rest):
        o_ref = rest[len(deps)]
        acc = jnp.dot(dy1_ref[...], w1_ref[...].reshape(J * n, WIDE_TILE), preferred_element_type=F32)
        acc = acc + jnp.dot(dy2_ref[...], w2_ref[...].reshape(J * n, WIDE_TILE), preferred_element_type=F32)
        o_ref[...] = acc.astype(o_ref.dtype)

    rows = _once((WIDE_TILE, J * n), lambda i, q: (i, 0))
    cols = pl.BlockSpec((None, J, n, WIDE_TILE), lambda i, q: (0, 0, 0, q))
    return pl.pallas_call(
        body, name=name, grid=(M // WIDE_TILE, K // WIDE_TILE),
        in_specs=[rows, cols, rows, cols] + [_ANY] * len(deps),
        out_specs=pl.BlockSpec((WIDE_TILE, WIDE_TILE), lambda i, q: (i, q)),
        out_shape=jax.ShapeDtypeStruct((M, K), out_dtype),
        compiler_params=_params("parallel", "parallel"))(dy1, wt1, dy2, wt2, *deps)


def mm_nt_wide(name, a, w, out_dtype, deps=()):
    _, J, K, n = w.shape
    M = a.shape[0]

    def body(a_ref, w_ref):
        acc = None
        for j in range(J):
            part = lax.dot_general(a_ref[:, j * n:(j + 1) * n].astype(BF16), w_ref[j], _NT, preferred_element_type=F32)
            acc = part if acc is None else acc + part
        return acc

    return _wide_call(name, body, M, K, a, w,
                      pl.BlockSpec((WIDE_TILE, J * n), lambda i, q: (i, 0)),
                      pl.BlockSpec((None, J, WIDE_TILE, n), lambda i, q: (0, 0, q, 0)), out_dtype, deps)


def mmt_dw(name, dy, x, J, out_dtype, deps=(), rows=None):
    M, K = x.shape
    n = dy.shape[1] // J
    tn, tp = _pick(n, _TILES), _pick(K, _TILES)
    nt = n // tn
    assert M <= MAX_CONTRACT
    return _mm_call(
        name, _TN, (J * nt, K // tp, 1),
        [pl.BlockSpec((M, tn), lambda j, i, r: (0, j)),
         pl.BlockSpec((M, tp), lambda j, i, r: (0, i))],
        pl.BlockSpec((None, tn, tp), lambda j, i, r: (j // nt, j % nt, i)),
        jax.ShapeDtypeStruct((J, n if rows is None else rows, K), out_dtype), (dy, x), deps)


def mmt_dw_rows(name, dy, x, out, row0, rank):
    M, K = x.shape
    tp = _pick(K, _TILES)

    def body(dy_ref, x_ref, prev_ref, o_ref):
        del prev_ref
        full = lax.dot_general(dy_ref[...], x_ref[...], _TN, preferred_element_type=F32)
        o_ref[...] = full[:rank].astype(o_ref.dtype)

    return pl.pallas_call(
        body, name=name, grid=(K // tp,),
        in_specs=[pl.BlockSpec((M, dy.shape[1]), lambda i: (0, 0)), pl.BlockSpec((M, tp), lambda i: (0, i)), _ANY],
        out_specs=pl.BlockSpec((None, rank, tp), lambda i: (0, row0 // rank, i)),
        out_shape=jax.ShapeDtypeStruct(out.shape, out.dtype), input_output_aliases={2: 0},
        compiler_params=_params("parallel"))(dy, x, out)


def ffn_gate_up(name, a, wg, wu):
    _, J, n, K = wg.shape
    M = a.shape[0]
    tm, tn = _pick(M, _TILES), _pick(n, _TILES)
    nt = n // tn
    assert K <= MAX_CONTRACT

    def body(a_ref, wg_ref, wu_ref, g_ref, u_ref, act_ref):
        x = a_ref[...]
        g = lax.dot_general(x, wg_ref[...], _NT, preferred_element_type=F32)
        u = lax.dot_general(x, wu_ref[...], _NT, preferred_element_type=F32)
        g_ref[...] = g.astype(g_ref.dtype)
        u_ref[...] = u.astype(u_ref.dtype)
        act_ref[...] = (_silu_and_grad(g)[0] * u).astype(act_ref.dtype)

    w_spec = pl.BlockSpec((None, None, tn, K), lambda i, j: (0, j // nt, j % nt, 0))
    out = pl.BlockSpec((tm, tn), lambda i, j: (i, j))
    return pl.pallas_call(
        body, name=name, grid=(M // tm, J * nt),
        in_specs=[pl.BlockSpec((tm, K), lambda i, j: (i, 0)), w_spec, w_spec],
        out_specs=[out] * 3, out_shape=[jax.ShapeDtypeStruct((M, J * n), BF16)] * 3,
        compiler_params=_params("parallel", "parallel"))(a, wg, wu)


def mm_add_norm(name, a, w, res, norm_w):
    _, _, K, N = w.shape
    M = a.shape[0]
    tm, tk = _pick(M, (WIDE_TILE, 256)), _pick(K, (1024, 512, 256))
    steps = K // tk

    def body(a_ref, w_ref, res_ref, nw_ref, h_ref, hn_ref):
        k = pl.program_id(1)
        part = jnp.dot(a_ref[...], w_ref[...], preferred_element_type=F32)
        _accumulate(h_ref, part, k == 0)

        @pl.when(k == steps - 1)
        def _():
            h = h_ref[...] + res_ref[...]
            h_ref[...] = h
            hn_ref[...] = _rms(h, nw_ref[...]).astype(hn_ref.dtype)

    rows = pl.BlockSpec((tm, N), lambda i, k: (i, 0))
    return pl.pallas_call(
        body, name=name, grid=(M // tm, steps),
        in_specs=[pl.BlockSpec((tm, tk), lambda i, k: (i, k)),
                  pl.BlockSpec((None, None, tk, N), lambda i, k: (0, 0, k, 0)), rows,
                  pl.BlockSpec((1, N), lambda i, k: (0, 0))],
        out_specs=[rows, rows],
        out_shape=[jax.ShapeDtypeStruct((M, N), F32), jax.ShapeDtypeStruct((M, N), BF16)],
        compiler_params=_params("parallel", "arbitrary"))(a, w, res, norm_w)


def ple_fwd(name, pn, wpg, p_in, wpp, h):
    _, J, P, n = wpp.shape
    M, D = h.shape
    tm, tn = _pick(M, (WIDE_TILE, 256)), _pick(D, _TILES)
    per_tile = tn // n

    def body(pn_ref, wg_ref, p_ref, wp_ref, h_ref, x_ref, s_ref, e_ref):
        s = jnp.dot(pn_ref[...], wg_ref[...], preferred_element_type=F32)
        p_blk = p_ref[...]
        e = jnp.concatenate([jnp.dot(p_blk, wp_ref[j], preferred_element_type=F32) for j in range(per_tile)], axis=1)
        s_ref[...] = s
        e_ref[...] = e
        x_ref[...] = h_ref[...] + _sigmoid(s) * e

    tile = pl.BlockSpec((tm, tn), lambda i, j: (i, j))
    return pl.pallas_call(
        body, name=name, grid=(M // tm, D // tn),
        in_specs=[pl.BlockSpec((tm, D), lambda i, j: (i, 0)),
                  pl.BlockSpec((None, None, D, tn), lambda i, j: (0, 0, 0, j)),
                  pl.BlockSpec((tm, P), lambda i, j: (i, 0)),
                  pl.BlockSpec((None, per_tile, P, n), lambda i, j: (0, j, 0, 0)), tile],
        out_specs=[tile] * 3, out_shape=[jax.ShapeDtypeStruct((M, D), F32)] * 3,
        compiler_params=_params("parallel", "parallel"))(pn, wpg, p_in, wpp, h)


def ffn_down_bwd(name, dy, wd, g, u, deps=()):
    _, _, K, n = wd.shape
    M = dy.shape[0]
    tm, tq = _pick(M, _TILES), _pick(K, _TILES)
    assert n <= MAX_CONTRACT

    def body(dy_ref, w_ref, g_ref, u_ref, *rest):
        dg_ref, du_ref = rest[len(deps):]
        dact = lax.dot_general(dy_ref[...], w_ref[...], _NT, preferred_element_type=F32)
        silu, dsilu = _silu_and_grad(g_ref[...].astype(F32))
        dg_ref[...] = (dact * u_ref[...].astype(F32) * dsilu).astype(dg_ref.dtype)
        du_ref[...] = (dact * silu).astype(du_ref.dtype)

    blk = pl.BlockSpec((tm, tq), lambda i, q: (i, q))
    return pl.pallas_call(
        body, name=name, grid=(M // tm, K // tq),
        in_specs=[pl.BlockSpec((tm, n), lambda i, q: (i, 0)),
                  pl.BlockSpec((None, None, tq, n), lambda i, q: (0, 0, q, 0)), blk, blk] + [_ANY] * len(deps),
        out_specs=[blk, blk], out_shape=[jax.ShapeDtypeStruct((M, K), BF16)] * 2,
        compiler_params=_params("parallel", "parallel"))(dy, wd, g, u, *deps)


def rowwise(name, fn, rows, ins, outs, tr=256, deps=()):
    widest = max([s[1].shape[1] if s[0] != "col" else s[3] for s in ins] + [s[1] for s in outs])
    tr = min(tr if widest <= 2048 else tr // 2, rows)
    in_specs, args = [], []
    for spec in ins:
        kind, a = spec[0], spec[1]
        if kind == "row":
            in_specs.append(pl.BlockSpec((tr, a.shape[1]), lambda i: (i, 0)))
        elif kind == "col":
            cb, width = spec[2], spec[3]
            in_specs.append(pl.BlockSpec((tr, width), lambda i, cb=cb: (i, cb)))
        else:
            in_specs.append(pl.BlockSpec(a.shape, lambda i: (0, 0)))
        args.append(a)
    out_specs, out_shapes = [], []
    for spec in outs:
        if spec[0] == "row":
            out_specs.append(pl.BlockSpec((tr, spec[1]), lambda i: (i, 0)))
            out_shapes.append(jax.ShapeDtypeStruct((rows, spec[1]), spec[2]))
        else:
            out_specs.append(pl.BlockSpec((1, spec[1]), lambda i: (0, 0)))
            out_shapes.append(jax.ShapeDtypeStruct((1, spec[1]), F32))
    n_in = len(ins)

    def body(*refs):
        vals = fn(*[r[...] for r in refs[:n_in]])
        first = pl.program_id(0) == 0
        for r, v, spec in zip(refs[n_in + len(deps):], vals, outs):
            if spec[0] == "row":
                r[...] = v.astype(r.dtype)
            else:
                _accumulate(r, v, first)

    return pl.pallas_call(body, name=name, grid=(rows // tr,), in_specs=in_specs + [_ANY] * len(deps),
                          out_specs=out_specs, out_shape=out_shapes,
                          compiler_params=_params("arbitrary"))(*args, *deps)


def _accumulate(ref, v, first):
    @pl.when(first)
    def _():
        ref[...] = v

    @pl.when(jnp.logical_not(first))
    def _():
        ref[...] += v


def _rms(x, w):
    r = lax.rsqrt(jnp.mean(x * x, axis=-1, keepdims=True) + NORM_EPS)
    return x * r * w


def _rms_bwd(x, w, dy):
    r = lax.rsqrt(jnp.mean(x * x, axis=-1, keepdims=True) + NORM_EPS)
    g = dy * w
    dx = r * (g - x * (r * r) * jnp.mean(g * x, axis=-1, keepdims=True))
    dw = jnp.sum(dy * x * r, axis=0, keepdims=True)
    return dx, dw


def _sigmoid(x):
    return 1.0 / (1.0 + jnp.exp(-x))


def _silu_and_grad(g):
    s = _sigmoid(g)
    return g * s, s * (1.0 + g * (1.0 - s))


def _swap_pairs(x):
    n = x.shape[-1]
    lane = lax.broadcasted_iota(jnp.int32, x.shape, x.ndim - 1)
    return jnp.where((lane & 1) == 0, pltpu.roll(x, n - 1, x.ndim - 1), pltpu.roll(x, 1, x.ndim - 1))


def _rot(x, cosf, sins):
    return x * cosf + _swap_pairs(x) * sins


def _unrot(d, cosf, sins):
    return d * cosf + _swap_pairs(d * sins)


def _ret_log_gamma(h):
    vals = [math.log1p(-2.0 ** (-5.0 - i)) for i in range(RET_HEADS)]
    out = jnp.float32(vals[RET_HEADS - 1])
    for i in range(RET_HEADS - 2, -1, -1):
        out = jnp.where(h == i, jnp.float32(vals[i]), out)
    return out


def _fill_decays(dec_ref, lg):
    ri = lax.broadcasted_iota(jnp.int32, (BLK, BLK), 0)
    ci = lax.broadcasted_iota(jnp.int32, (BLK, BLK), 1)
    for d in range(dec_ref.shape[0]):
        dt = d * BLK + ri - ci
        dec_ref[d] = jnp.where(dt >= 0, jnp.exp(jnp.maximum(dt, 0).astype(F32) * lg), 0.0)


def _decay_row(dec_ref, qi):
    return jnp.concatenate([dec_ref[qi - kb] for kb in range(qi + 1)], axis=1)


def _once(block_shape, index_map):
    return pl.BlockSpec(block_shape, index_map, pipeline_mode=pl.Buffered(1))


def _dot(a, b):
    return jnp.dot(a.astype(BF16), b.astype(BF16), preferred_element_type=F32)


def _dot_nt(a, b):
    return lax.dot_general(a.astype(BF16), b.astype(BF16), _NT, preferred_element_type=F32)


def _dot_tn(a, b):
    return lax.dot_general(a.astype(BF16), b.astype(BF16), _TN, preferred_element_type=F32)


def retention_fwd(name, z, cosf, sins, width_out):
    T = z.shape[0]
    nq = T // BLK
    scale = RET_DK ** -0.5

    def body(q_ref, k_ref, v_ref, cos_ref, sin_ref, o_ref, krot, vb, dec_ref):
        _fill_decays(dec_ref, _ret_log_gamma(pl.program_id(0)))
        krot[...] = (_rot(k_ref[...], cos_ref[...], sin_ref[...]) * scale).astype(BF16)
        vb[...] = v_ref[...].astype(BF16)
        for qi in range(nq):
            rows, n = slice(qi * BLK, (qi + 1) * BLK), (qi + 1) * BLK
            q = _rot(q_ref[rows, :], cos_ref[rows, :], sin_ref[rows, :])
            s = _dot_nt(q, krot[0:n, :]) * _decay_row(dec_ref, qi)
            o_ref[rows, :] = _dot(s, vb[0:n, :])

    return pl.pallas_call(
        body, name=name, grid=(RET_HEADS,),
        in_specs=[pl.BlockSpec((T, RET_DK), lambda h: (0, OFF_RQ // RET_DK + h)),
                  pl.BlockSpec((T, RET_DK), lambda h: (0, OFF_RK // RET_DK + h)),
                  pl.BlockSpec((T, RET_DV), lambda h: (0, OFF_RV // RET_DV + h)),
                  _once((T, RET_DK), lambda h: (0, 0)), _once((T, RET_DK), lambda h: (0, 0))],
        out_specs=pl.BlockSpec((T, RET_DV), lambda h: (0, h)),
        out_shape=jax.ShapeDtypeStruct((T, width_out), F32),
        scratch_shapes=[pltpu.VMEM((T, RET_DK), BF16), pltpu.VMEM((T, RET_DV), BF16),
                        pltpu.VMEM((nq, BLK, BLK), F32)],
        compiler_params=_params("arbitrary"))(z, z, z, cosf, sins)


def retention_bwd(name, z, cosf, sins, do):
    T = z.shape[0]
    nq = T // BLK
    scale = RET_DK ** -0.5

    def body(q_ref, k_ref, v_ref, cos_ref, sin_ref, do_ref, dq_ref, dk_ref, dv_ref, krot, vb, dk_acc, dv_acc, dec_ref):
        _fill_decays(dec_ref, _ret_log_gamma(pl.program_id(0)))
        krot[...] = (_rot(k_ref[...], cos_ref[...], sin_ref[...]) * scale).astype(BF16)
        vb[...] = v_ref[...].astype(BF16)
        dk_acc[...] = jnp.zeros_like(dk_acc)
        dv_acc[...] = jnp.zeros_like(dv_acc)
        for qi in range(nq):
            rows, n = slice(qi * BLK, (qi + 1) * BLK), (qi + 1) * BLK
            cos_q, sin_q = cos_ref[rows, :], sin_ref[rows, :]
            q = _rot(q_ref[rows, :], cos_q, sin_q).astype(BF16)
            dout = do_ref[rows, :].astype(BF16)
            kk, vv, dec = krot[0:n, :], vb[0:n, :], _decay_row(dec_ref, qi)
            p = (_dot_nt(q, kk) * dec).astype(BF16)
            ds = (_dot_nt(dout, vv) * dec).astype(BF16)
            dq_ref[rows, :] = _unrot(_dot(ds, kk), cos_q, sin_q).astype(dq_ref.dtype)
            dk_acc[0:n, :] += _dot_tn(ds, q)
            dv_acc[0:n, :] += _dot_tn(p, dout)
        dk_ref[...] = (_unrot(dk_acc[...], cos_ref[...], sin_ref[...]) * scale).astype(dk_ref.dtype)
        dv_ref[...] = dv_acc[...].astype(dv_ref.dtype)

    head = lambda h: (0, h)
    return pl.pallas_call(
        body, name=name, grid=(RET_HEADS,),
        in_specs=[pl.BlockSpec((T, RET_DK), lambda h: (0, OFF_RQ // RET_DK + h)),
                  pl.BlockSpec((T, RET_DK), lambda h: (0, OFF_RK // RET_DK + h)),
                  pl.BlockSpec((T, RET_DV), lambda h: (0, OFF_RV // RET_DV + h)),
                  _once((T, RET_DK), lambda h: (0, 0)), _once((T, RET_DK), lambda h: (0, 0)),
                  pl.BlockSpec((T, RET_DV), head)],
        out_specs=[pl.BlockSpec((T, RET_DK), head), pl.BlockSpec((T, RET_DK), head), pl.BlockSpec((T, RET_DV), head)],
        out_shape=[jax.ShapeDtypeStruct((T, RET_QK), BF16), jax.ShapeDtypeStruct((T, RET_QK), BF16),
                   jax.ShapeDtypeStruct((T, RET_V), BF16)],
        scratch_shapes=[pltpu.VMEM((T, RET_DK), BF16), pltpu.VMEM((T, RET_DV), BF16),
                        pltpu.VMEM((T, RET_DK), F32), pltpu.VMEM((T, RET_DV), F32),
                        pltpu.VMEM((nq, BLK, BLK), F32)],
        compiler_params=_params("arbitrary"))(z, z, z, cosf, sins, do)


GLA_PAIR = 2


def _gla_chunk(q_ref, k_ref, v_ref, glr_ref, gu, gb, rows, hh, trilf):
    ck = slice(hh * GLA_DK, (hh + 1) * GLA_DK)
    zg = _dot(glr_ref[rows, :], gu[:, ck]) + gb[:, ck]
    la = (jnp.minimum(zg, 0.0) - jnp.log(1.0 + jnp.exp(-jnp.abs(zg)))) * (1.0 / GLA_GATE_NORM)
    cum = jnp.dot(trilf, la, precision=HIGHEST, preferred_element_type=F32)
    last = jnp.sum(la, axis=0, keepdims=True)
    ecum = jnp.exp(cum)
    k = k_ref[rows, ck]
    qt = q_ref[rows, ck] * (GLA_DK ** -0.5) * ecum
    kt = k * jnp.exp(-cum)
    kh = k * jnp.exp(last - cum)
    return zg, cum, last, ecum, qt, kt, kh, v_ref[rows, hh * GLA_DV:(hh + 1) * GLA_DV].astype(BF16)


def _state_decay(last):
    e = jnp.exp(jnp.broadcast_to(last, (GLA_DK, GLA_DK)).T)
    return jnp.concatenate([e] * (GLA_DV // GLA_DK), axis=1)


def _gla_specs(T):
    wk, wv = GLA_PAIR * GLA_DK, GLA_PAIR * GLA_DV
    return [_once((T, wk), lambda h: (0, OFF_GQ // wk + h)),
            _once((T, wk), lambda h: (0, OFF_GK // wk + h)),
            _once((T, wv), lambda h: (0, OFF_GV // wv + h)),
            _once((T, LANE), lambda h: (0, 0)),
            pl.BlockSpec((LANE, wk), lambda h: (0, h)),
            pl.BlockSpec((1, wk), lambda h: (0, h))]


def gla_fwd(name, z, glr, gu, gb, o_prev):
    T = z.shape[0]
    nc = T // CHUNK
    wv = GLA_PAIR * GLA_DV

    def body(q_ref, k_ref, v_ref, glr_ref, gu_ref, gb_ref, prev_ref, o_ref, *S):
        del prev_ref
        gu_b, gb_v = gu_ref[...].astype(BF16), gb_ref[...]
        ri = lax.broadcasted_iota(jnp.int32, (CHUNK, CHUNK), 0)
        ci = lax.broadcasted_iota(jnp.int32, (CHUNK, CHUNK), 1)
        tril = ri >= ci
        trilf = tril.astype(F32)
        for s_ref in S:
            s_ref[...] = jnp.zeros_like(s_ref)

        def step(c, carry):
            rows = pl.ds(pl.multiple_of(c * CHUNK, CHUNK), CHUNK)
            heads = range(GLA_PAIR)
            ch = [_gla_chunk(q_ref, k_ref, v_ref, glr_ref, gu_b, gb_v, rows, hh, trilf) for hh in heads]
            a = [jnp.where(tril, _dot_nt(ch[hh][4], ch[hh][5]), 0.0) for hh in heads]
            s_prev = [S[hh][...] for hh in heads]
            intra = [_dot(a[hh], ch[hh][7]) for hh in heads]
            inter = [_dot(ch[hh][4], s_prev[hh]) for hh in heads]
            added = [_dot_tn(ch[hh][6], ch[hh][7]) for hh in heads]
            for hh in heads:
                o_ref[rows, hh * GLA_DV:(hh + 1) * GLA_DV] = intra[hh] + inter[hh]
                S[hh][...] = s_prev[hh] * _state_decay(ch[hh][2]) + added[hh]
            return carry

        lax.fori_loop(0, nc, step, 0)

    n_in = 6
    return pl.pallas_call(
        body, name=name, grid=(GLA_HEADS // GLA_PAIR,),
        in_specs=_gla_specs(T) + [pl.BlockSpec(memory_space=pl.ANY)],
        out_specs=pl.BlockSpec((T, wv), lambda h: (0, RET_V // wv + h)),
        out_shape=jax.ShapeDtypeStruct(o_prev.shape, F32),
        scratch_shapes=[pltpu.VMEM((GLA_DK, GLA_DV), F32)] * GLA_PAIR,
        input_output_aliases={n_in: 0},
        compiler_params=_params("arbitrary"))(z, z, z, glr, gu, gb, o_prev)


def gla_bwd(name, z, glr, gu, gb, do):
    T = z.shape[0]
    nc = T // CHUNK

    def body(q_ref, k_ref, v_ref, glr_ref, gu_ref, gb_ref, do_ref,
             dq_ref, dk_ref, dv_ref, dglr_ref, dgu_ref, dgb_ref, s_all, dS):
        gu_b, gb_v = gu_ref[...].astype(BF16), gb_ref[...]
        ri = lax.broadcasted_iota(jnp.int32, (CHUNK, CHUNK), 0)
        ci = lax.broadcasted_iota(jnp.int32, (CHUNK, CHUNK), 1)
        tril = ri >= ci
        trilf = tril.astype(F32)
        triuf = (ri <= ci).astype(F32)
        last_row = lax.broadcasted_iota(jnp.int32, (CHUNK, GLA_DK), 0) == CHUNK - 1
        ones8 = jnp.ones((8, GLA_DV), F32)

        heads = range(GLA_PAIR)

        def fstep(c, carry):
            rows = pl.ds(pl.multiple_of(c * CHUNK, CHUNK), CHUNK)
            ch = [_gla_chunk(q_ref, k_ref, v_ref, glr_ref, gu_b, gb_v, rows, hh, trilf) for hh in heads]
            added = [_dot_tn(ch[hh][6], ch[hh][7]) for hh in heads]
            for hh in heads:
                s_prev = dS[hh]
                s_all[hh, c] = s_prev
                dS[hh] = s_prev * _state_decay(ch[hh][2]) + added[hh]
            return carry

        dS[...] = jnp.zeros_like(dS)
        lax.fori_loop(0, nc, fstep, 0)
        dS[...] = jnp.zeros_like(dS)
        dgu_ref[...] = jnp.zeros_like(dgu_ref)
        dgb_ref[...] = jnp.zeros_like(dgb_ref)

        def bstep(i, carry):
            c = nc - 1 - i
            rows = pl.ds(pl.multiple_of(c * CHUNK, CHUNK), CHUNK)
            glr_c = glr_ref[rows, :]
            cks = [slice(hh * GLA_DK, (hh + 1) * GLA_DK) for hh in heads]
            cvs = [slice(hh * GLA_DV, (hh + 1) * GLA_DV) for hh in heads]
            ch = [_gla_chunk(q_ref, k_ref, v_ref, glr_ref, gu_b, gb_v, rows, hh, trilf) for hh in heads]
            zg, cum, last, ecum, qt, kt, kh, v = [[ch[hh][j] for hh in heads] for j in range(8)]
            s_prev = [s_all[hh, c] for hh in heads]
            ds_new = [dS[hh] for hh in heads]
            dout = [do_ref[rows, cvs[hh]].astype(BF16) for hh in heads]
            a = [jnp.where(tril, _dot_nt(qt[hh], kt[hh]), 0.0) for hh in heads]
            da = [jnp.where(tril, _dot_nt(dout[hh], v[hh]), 0.0) for hh in heads]
            dv_a = [_dot_tn(a[hh], dout[hh]) for hh in heads]
            dv_b = [_dot(kh[hh], ds_new[hh]) for hh in heads]
            dqt_a = [_dot(da[hh], kt[hh]) for hh in heads]
            dqt_b = [_dot_nt(dout[hh], s_prev[hh]) for hh in heads]
            dkt = [_dot_tn(da[hh], qt[hh]) for hh in heads]
            dkh = [_dot_nt(v[hh], ds_new[hh]) for hh in heads]
            ds_add = [_dot_tn(qt[hh], dout[hh]) for hh in heads]
            rs = [lax.dot_general(ones8, ds_new[hh] * s_prev[hh], _NT, precision=HIGHEST, preferred_element_type=F32)
                  for hh in heads]
            dcum = []
            for hh in heads:
                dv_ref[rows, cvs[hh]] = (dv_a[hh] + dv_b[hh]).astype(dv_ref.dtype)
                dS[hh] = ds_new[hh] * _state_decay(last[hh]) + ds_add[hh]
                dqt = dqt_a[hh] + dqt_b[hh]
                dq_ref[rows, cks[hh]] = (dqt * ecum[hh] * (GLA_DK ** -0.5)).astype(dq_ref.dtype)
                dk_ref[rows, cks[hh]] = (dkt[hh] * jnp.exp(-cum[hh])
                                         + dkh[hh] * jnp.exp(last[hh] - cum[hh])).astype(dk_ref.dtype)
                dkh_kh = dkh[hh] * kh[hh]
                dlast = (jnp.sum(dkh_kh, axis=0, keepdims=True)
                         + jnp.exp(last[hh]) * (jnp.sum(rs[hh], axis=0, keepdims=True) * 0.125))
                dcum.append(dqt * qt[hh] - dkt[hh] * kt[hh] - dkh_kh + jnp.where(last_row, dlast, 0.0))
            dla = [jnp.dot(triuf, dcum[hh], precision=HIGHEST, preferred_element_type=F32) for hh in heads]
            dzg = [dla[hh] * (1.0 / GLA_GATE_NORM) * _sigmoid(-zg[hh]) for hh in heads]
            dglr = [_dot_nt(dzg[hh], gu_b[:, cks[hh]]) for hh in heads]
            dgu = [_dot_tn(glr_c, dzg[hh]) for hh in heads]
            for hh in heads:
                dglr_ref[hh, rows, :] = dglr[hh]
                dgu_ref[:, cks[hh]] += dgu[hh]
                dgb_ref[:, cks[hh]] += jnp.sum(dzg[hh], axis=0, keepdims=True)
            return carry

        lax.fori_loop(0, nc, bstep, 0)

    wk, wv = GLA_PAIR * GLA_DK, GLA_PAIR * GLA_DV
    return pl.pallas_call(
        body, name=name, grid=(GLA_HEADS // GLA_PAIR,),
        in_specs=_gla_specs(T) + [_once((T, wv), lambda h: (0, RET_V // wv + h))],
        out_specs=[pl.BlockSpec((T, wk), lambda h: (0, h)), pl.BlockSpec((T, wk), lambda h: (0, h)),
                   pl.BlockSpec((T, wv), lambda h: (0, h)),
                   pl.BlockSpec((GLA_PAIR, T, LANE), lambda h: (h, 0, 0)),
                   pl.BlockSpec((LANE, wk), lambda h: (0, h)), pl.BlockSpec((1, wk), lambda h: (0, h))],
        out_shape=[jax.ShapeDtypeStruct((T, GLA_QK), BF16), jax.ShapeDtypeStruct((T, GLA_QK), BF16),
                   jax.ShapeDtypeStruct((T, GLA_V), BF16), jax.ShapeDtypeStruct((GLA_HEADS, T, LANE), F32),
                   jax.ShapeDtypeStruct((LANE, GLA_QK), F32), jax.ShapeDtypeStruct((1, GLA_QK), F32)],
        scratch_shapes=[pltpu.VMEM((GLA_PAIR, nc, GLA_DK, GLA_DV), F32), pltpu.VMEM((GLA_PAIR, GLA_DK, GLA_DV), F32)],
        compiler_params=_params("arbitrary"))(z, z, z, glr, gu, gb, do)


HN_HEADS = RET_HEADS + GLA_HEADS
HN_W = RET_DV


def _gate_col(h):
    return jnp.where(h < RET_HEADS, OFF_RG // HN_W + h, OFF_GG // HN_W + h - RET_HEADS)


def headnorm_fwd(name, oraw, z, w):
    T = oraw.shape[0]
    tr = _pick(T, _TILES)

    def body(o_ref, g_ref, w_ref, y_ref):
        y_ref[...] = (_rms(o_ref[...], w_ref[...]) * _silu_and_grad(g_ref[...])[0]).astype(y_ref.dtype)

    return pl.pallas_call(
        body, name=name, grid=(HN_HEADS, T // tr),
        in_specs=[pl.BlockSpec((tr, HN_W), lambda h, i: (i, h)),
                  pl.BlockSpec((tr, HN_W), lambda h, i: (i, _gate_col(h))),
                  pl.BlockSpec((1, HN_W), lambda h, i: (0, h))],
        out_specs=pl.BlockSpec((tr, HN_W), lambda h, i: (i, h)),
        out_shape=jax.ShapeDtypeStruct((T, HN_HEADS * HN_W), BF16),
        compiler_params=_params("arbitrary", "arbitrary"))(oraw, z, w)


def headnorm_bwd(name, oraw, z, w, dy):
    T = oraw.shape[0]
    tr = _pick(T, _TILES)

    def body(o_ref, g_ref, w_ref, dy_ref, do_ref, dg_ref, dw_ref):
        o, wv, dyv = o_ref[...], w_ref[...], dy_ref[...].astype(F32)
        silu, dsilu = _silu_and_grad(g_ref[...])
        n = _rms(o, wv)
        dg_ref[...] = (dyv * n * dsilu).astype(dg_ref.dtype)
        dx, dw = _rms_bwd(o, wv, dyv * silu)
        do_ref[...] = dx
        _accumulate(dw_ref, dw, pl.program_id(1) == 0)

    blk = pl.BlockSpec((tr, HN_W), lambda h, i: (i, h))
    return pl.pallas_call(
        body, name=name, grid=(HN_HEADS, T // tr),
        in_specs=[blk, pl.BlockSpec((tr, HN_W), lambda h, i: (i, _gate_col(h))),
                  pl.BlockSpec((1, HN_W), lambda h, i: (0, h)), blk],
        out_specs=[blk, blk, pl.BlockSpec((1, HN_W), lambda h, i: (0, h))],
        out_shape=[jax.ShapeDtypeStruct((T, HN_HEADS * HN_W), F32),
                   jax.ShapeDtypeStruct((T, HN_HEADS * HN_W), BF16),
                   jax.ShapeDtypeStruct((1, HN_HEADS * HN_W), F32)],
        compiler_params=_params("arbitrary", "arbitrary"))(oraw, z, w, dy)


N_MASKS = 4


def _check_mask_classes(T):
    for window, dilation in DILATED_BRANCHES[:-1]:
        assert window < (N_MASKS - 1) * BLK - (BLK - 1) and BLK % dilation == 0
    assert DILATED_BRANCHES[-1][0] >= T and BLK % DILATED_BRANCHES[-1][1] == 0


def _fill_masks(logm_ref):
    ri = lax.broadcasted_iota(jnp.int32, (BLK, BLK), 0)
    ci = lax.broadcasted_iota(jnp.int32, (BLK, BLK), 1)
    for d in range(N_MASKS):
        dt = d * BLK + ri - ci
        mult = jnp.zeros((BLK, BLK), F32)
        for window, dilation in DILATED_BRANCHES:
            hit = (dt >= 0) & (dt <= window) & ((dt & (dilation - 1)) == 0)
            mult = mult + hit.astype(F32)
        logm_ref[d] = jnp.where(mult > 0, jnp.log(jnp.maximum(mult, 1.0)), -1e30)


def _mask_row(ref, qi):
    return jnp.concatenate([ref[min(qi - kb, N_MASKS - 1)] for kb in range(qi + 1)], axis=1)


def attn_fwd(name, qkv):
    T = qkv.shape[0]
    D = qkv.shape[1] // 3
    dh = D // ATT_HEADS
    nq = T // BLK
    scale = dh ** -0.5

    _check_mask_classes(T)

    def body(q_ref, k_ref, v_ref, o_ref, lse_ref, logm_ref):
        @pl.when(pl.program_id(0) == 0)
        def _():
            _fill_masks(logm_ref)

        for q0 in range(0, nq, 2):
            qis = range(q0, min(q0 + 2, nq))
            rows = [slice(qi * BLK, (qi + 1) * BLK) for qi in qis]
            ns = [(qi + 1) * BLK for qi in qis]
            s = [_dot_nt(q_ref[r, :], k_ref[0:n, :]) for r, n in zip(rows, ns)]
            s = [x * scale + _mask_row(logm_ref, qi) for x, qi in zip(s, qis)]
            m = [jnp.max(x, axis=-1, keepdims=True) for x in s]
            p = [jnp.exp(x - mx) for x, mx in zip(s, m)]
            l = [jnp.sum(x, axis=-1, keepdims=True) for x in p]
            pv = [_dot(x, v_ref[0:n, :]) for x, n in zip(p, ns)]
            for r, x, lx, mx in zip(rows, pv, l, m):
                o_ref[r, :] = (x / lx).astype(o_ref.dtype)
                lse_ref[r, :] = jnp.broadcast_to(mx + jnp.log(lx), (BLK, LANE))

    return pl.pallas_call(
        body, name=name, grid=(ATT_HEADS,),
        in_specs=[pl.BlockSpec((T, dh), lambda h: (0, h)),
                  pl.BlockSpec((T, dh), lambda h: (0, ATT_HEADS + h)),
                  pl.BlockSpec((T, dh), lambda h: (0, 2 * ATT_HEADS + h))],
        out_specs=[pl.BlockSpec((T, dh), lambda h: (0, h)),
                   pl.BlockSpec((None, T, LANE), lambda h: (h, 0, 0))],
        out_shape=[jax.ShapeDtypeStruct((T, D), BF16), jax.ShapeDtypeStruct((ATT_HEADS, T, LANE), F32)],
        scratch_shapes=[pltpu.VMEM((N_MASKS, BLK, BLK), F32)],
        compiler_params=_params("arbitrary"))(qkv, qkv, qkv)


def attn_bwd(name, qkv, o, lse, do):
    T = qkv.shape[0]
    D = qkv.shape[1] // 3
    dh = D // ATT_HEADS
    nq = T // BLK
    scale = dh ** -0.5

    _check_mask_classes(T)

    def body(q_ref, k_ref, v_ref, o_ref, lse_ref, do_ref, dq_ref, dk_ref, dv_ref, dk_acc, dv_acc, logm_ref):
        @pl.when(pl.program_id(0) == 0)
        def _():
            _fill_masks(logm_ref)

        dk_acc[...] = jnp.zeros_like(dk_acc)
        dv_acc[...] = jnp.zeros_like(dv_acc)
        for qi in range(nq):
            rows, n = slice(qi * BLK, (qi + 1) * BLK), (qi + 1) * BLK
            q, dout = q_ref[rows, :], do_ref[rows, :]
            kk, vv = k_ref[0:n, :], v_ref[0:n, :]
            delta = jnp.sum(dout.astype(F32) * o_ref[rows, :].astype(F32), axis=-1, keepdims=True)
            lse = jnp.max(lse_ref[rows, :], axis=-1, keepdims=True)
            p = jnp.exp(_dot_nt(q, kk) * scale + _mask_row(logm_ref, qi) - lse)
            ds = (p * (_dot_nt(dout, vv) - delta) * scale).astype(BF16)
            dq_ref[rows, :] = _dot(ds, kk).astype(dq_ref.dtype)
            dk_acc[0:n, :] += _dot_tn(ds, q)
            dv_acc[0:n, :] += _dot_tn(p, dout)
        dk_ref[...] = dk_acc[...].astype(dk_ref.dtype)
        dv_ref[...] = dv_acc[...].astype(dv_ref.dtype)

    full = pl.BlockSpec((T, dh), lambda h: (0, h))
    return pl.pallas_call(
        body, name=name, grid=(ATT_HEADS,),
        in_specs=[full, pl.BlockSpec((T, dh), lambda h: (0, ATT_HEADS + h)),
                  pl.BlockSpec((T, dh), lambda h: (0, 2 * ATT_HEADS + h)),
                  full, pl.BlockSpec((None, T, LANE), lambda h: (h, 0, 0)), full],
        out_specs=[full, full, full],
        out_shape=[jax.ShapeDtypeStruct((T, D), BF16)] * 3,
        scratch_shapes=[pltpu.VMEM((T, dh), F32), pltpu.VMEM((T, dh), F32), pltpu.VMEM((N_MASKS, BLK, BLK), F32)],
        compiler_params=_params("arbitrary"))(qkv, qkv, qkv, o, lse, do)


def _mesh_pos():
    mx, my, mc = lax.axis_index("x"), lax.axis_index("y"), lax.axis_index("c")
    return mx, my, mc, 4 * mx + 2 * my + mc


def _peer(k, mx, my, mc):
    px, py, pc = mx ^ (k >> 2), my ^ ((k >> 1) & 1), mc ^ (k & 1)
    return (px, py, pc), 4 * px + 2 * py + pc


_SIBLING = 1
_OTHER_CHIPS = (4, 2, 6)
N_CHIP = N_DEV // 2
_PLANS = {"gather": (2, N_DEV - 1), "to_chips": (2, 1 + len(_OTHER_CHIPS)), "pass_on": (1, len(_OTHER_CHIPS)),
          "halves": (2, N_CHIP), "chip_sums": (2, len(_OTHER_CHIPS))}


def _copies(kind, items, send_sems, recv_sems):
    mx, my, mc, me = _mesh_pos()
    out = []

    def add(n, src, dst, peer):
        out.append(pltpu.make_async_remote_copy(
            src_ref=src, dst_ref=dst, send_sem=send_sems.at[n], recv_sem=recv_sems.at[n],
            device_id=peer, device_id_type=pl.DeviceIdType.MESH))

    per_item = _PLANS[kind][1]
    sibling = _peer(_SIBLING, mx, my, mc)[0]
    for i, refs in enumerate(items):
        n = i * per_item
        if kind == "gather":
            for k in range(1, N_DEV):
                add(n + k - 1, refs[0], refs[1].at[me], _peer(k, mx, my, mc)[0])
        elif kind == "to_chips":
            rows = refs[0].shape[0]
            dst = refs[1].at[me] if rows == refs[1].shape[1] else refs[1].at[me, pl.ds(0, rows)]
            for j, k in enumerate((_SIBLING,) + _OTHER_CHIPS):
                add(n + j, refs[0], dst, _peer(k, mx, my, mc)[0])
        elif kind == "pass_on":
            for j, k in enumerate(_OTHER_CHIPS):
                add(n + j, refs[0].at[me ^ k], refs[0].at[me ^ k], sibling)
        elif kind == "halves":
            for chip in range(N_CHIP):
                add(n + chip, refs[0].at[2 * chip + 1 - mc], refs[1].at[chip], sibling)
        else:
            for j, k in enumerate(_OTHER_CHIPS):
                peer, to = _peer(k, mx, my, mc)
                add(n + j, refs[0].at[to // 2], refs[1].at[me // 2], peer)
    return out


_HBM = pl.BlockSpec(memory_space=pltpu.HBM)
_SEM = pl.BlockSpec(memory_space=pltpu.SEMAPHORE)
_DATAFLOW = pltpu.SideEffectType.DATAFLOW_SIDE_EFFECTING


def exchange_call(name, waits, starts, deps=()):
    bufs, slot_of = [], {}

    def slots(items):
        out = []
        for item in items:
            for b in item:
                if id(b) not in slot_of:
                    slot_of[id(b)] = len(bufs)
                    bufs.append(b)
            out.append(tuple(slot_of[id(b)] for b in item))
        return out

    wait_plan = [(kind, slots(handle[0])) for kind, handle in waits]
    start_plan = [(kind, slots(items)) for kind, items in starts]
    wait_sems = [s for _, handle in waits for s in handle[1:]]
    n_buf, n_ws, n_start = len(bufs), len(wait_sems), len(starts)

    def body(*refs):
        buf_refs, sems_in = refs[:n_buf], refs[n_buf:n_buf + n_ws]
        outs = refs[n_buf + n_ws + len(deps):]
        pick = lambda plan: [tuple(buf_refs[s] for s in item) for item in plan]
        for wi, (kind, plan) in enumerate(wait_plan):
            copies = _copies(kind, pick(plan), sems_in[2 * wi], sems_in[2 * wi + 1])
            for cp in copies:
                cp.wait_send()
            for cp in copies:
                cp.wait_recv()
        for si, (kind, plan) in enumerate(start_plan):
            for cp in _copies(kind, pick(plan), outs[2 * si], outs[2 * si + 1]):
                cp.start()
        outs[-1][...] = jnp.zeros_like(outs[-1])

    hbm_bufs = [pltpu.with_memory_space_constraint(b, pltpu.HBM) for b in bufs]
    sem_shapes = []
    for kind, plan in start_plan:
        sem_shapes += [pltpu.SemaphoreType.DMA((len(plan) * _PLANS[kind][1],))] * 2
    outs = pl.pallas_call(
        body, name=name,
        out_shape=sem_shapes + [pltpu.HBM(b.shape, b.dtype) for b in bufs] + [jax.ShapeDtypeStruct((8, LANE), F32)],
        in_specs=[_HBM] * n_buf + [_SEM] * n_ws + [_ANY] * len(deps),
        out_specs=[_SEM] * (2 * n_start) + [_HBM] * n_buf + [pl.BlockSpec(memory_space=pltpu.VMEM)],
        input_output_aliases={i: 2 * n_start + i for i in range(n_buf)},
        compiler_params=pltpu.CompilerParams(has_side_effects=_DATAFLOW))(*hbm_bufs, *wait_sems, *deps)
    sems, thru, token = outs[:2 * n_start], outs[2 * n_start:-1], outs[-1]
    through = lambda plan: [tuple(thru[s] for s in item) for item in plan]
    waited = [through(plan) for _, plan in wait_plan]
    handles = [(through(plan), sems[2 * si], sems[2 * si + 1]) for si, (_, plan) in enumerate(start_plan)]
    return waited, handles, token


def gather_small(name, a, deps=()):
    def body(a_ref, *rest):
        o_ref, send_sems, recv_sems, local_sem = rest[len(deps):]
        me = _mesh_pos()[3]
        own = pltpu.make_async_copy(a_ref, o_ref.at[me], local_sem)
        own.start()
        copies = _copies("gather", [(a_ref, o_ref)], send_sems, recv_sems)
        for cp in copies:
            cp.start()
        for cp in copies:
            cp.wait_recv()
        for cp in copies:
            cp.wait_send()
        own.wait()

    return pl.pallas_call(
        body, name=name, in_specs=[_ANY] * (1 + len(deps)), out_specs=_ANY,
        out_shape=jax.ShapeDtypeStruct((N_DEV,) + a.shape, a.dtype),
        scratch_shapes=[pltpu.SemaphoreType.DMA((N_DEV - 1,)), pltpu.SemaphoreType.DMA((N_DEV - 1,)),
                        pltpu.SemaphoreType.DMA],
        compiler_params=pltpu.CompilerParams(has_side_effects=True))(a, *deps)


def _adamw_math(w, g, m, v):
    m2 = ADAM_B1 * m + (1.0 - ADAM_B1) * g
    v2 = ADAM_B2 * v + (1.0 - ADAM_B2) * (g * g)
    m_hat = m2 / (1.0 - ADAM_B1 ** ADAM_STEP)
    v_hat = v2 / (1.0 - ADAM_B2 ** ADAM_STEP)
    delta = -ADAM_LR * (m_hat / (jnp.sqrt(v_hat) + ADAM_EPS) + ADAM_WD * w)
    return delta, m2, v2


def chip_sum(name, a, half):
    _, r, c = a.shape
    tr = r
    chip = 2 * lax.axis_index("x") + lax.axis_index("y")
    where = jnp.stack([lax.axis_index("c"), chip ^ 1, chip ^ 2, chip ^ 3]).astype(jnp.int32)

    def body(where_ref, a_ref, h_ref, o_ref):
        del where_ref
        o_ref[...] = (a_ref[...].astype(F32) + h_ref[...].astype(F32)).astype(o_ref.dtype)

    blk = pl.BlockSpec((None, tr, c), lambda g, i, where: (where[1 + g], i, 0))
    grid_spec = pltpu.PrefetchScalarGridSpec(
        num_scalar_prefetch=1, grid=(N_CHIP - 1, r // tr),
        in_specs=[pl.BlockSpec((None, None, tr, c), lambda g, i, where: (where[1 + g], where[0], i, 0)), blk],
        out_specs=blk)
    return pl.pallas_call(
        body, name=name, grid_spec=grid_spec, out_shape=jax.ShapeDtypeStruct((N_CHIP, r, c), BF16),
        compiler_params=_params("parallel", "parallel"))(where, a.reshape(N_CHIP, 2, r, c), half)


def adamw(name, w, m, v, l, land, a, half, prev=None):
    L, r, c = w.shape
    cp = land.shape[2]
    tr = _pick(r, (256, 176, 128, 64, 32, 16, 8))

    def body(w_ref, m_ref, v_ref, land_ref, a_ref, half_ref, *rest):
        g_ref, d_ref, m2_ref, v2_ref = rest[-4:]
        chip = _mesh_pos()[3] // 2
        mine = a_ref[:, pl.ds(0, c)].astype(F32) + half_ref[:, pl.ds(0, c)].astype(F32)
        g = None
        for s in range(N_CHIP):
            part = jnp.where(chip == s, mine, land_ref[s, :, pl.ds(0, c)].astype(F32))
            g = part if g is None else g + part
        delta, m2, v2 = _adamw_math(w_ref[...], g, m_ref[...], v_ref[...])
        g_ref[...] = g
        d_ref[...] = delta
        m2_ref[...] = m2
        v2_ref[...] = v2

    blk = pl.BlockSpec((None, tr, c), lambda i: (l, i, 0))
    shape = jax.ShapeDtypeStruct((L, r, c), F32)
    extra = [] if prev is None else list(prev)
    return pl.pallas_call(
        body, name=name, grid=(r // tr,),
        in_specs=[blk, blk, blk, pl.BlockSpec((N_CHIP, tr, cp), lambda i: (0, i, 0)),
                  pl.BlockSpec((None, tr, cp), lambda i: (_mesh_pos()[3], i, 0)),
                  pl.BlockSpec((None, tr, cp), lambda i: (_mesh_pos()[3] // 2, i, 0))] + [_ANY] * len(extra),
        out_specs=[blk] * 4, out_shape=[shape] * 4,
        input_output_aliases={6 + k: k for k in range(len(extra))},
        compiler_params=_params("parallel"))(w, m, v, land, a, half, *extra)


def adamw_columns(name, w, m, v, land, a, half):
    r, _, D = w.shape
    tc = _pick(D, (256, 128))

    def body(w_ref, m_ref, v_ref, land_ref, a_ref, half_ref, g_ref, d_ref, m2_ref, v2_ref):
        chip = _mesh_pos()[3] // 2
        mine = a_ref[...].astype(F32) + half_ref[...].astype(F32)
        g = None
        for s in range(N_CHIP):
            part = jnp.where(chip == s, mine, land_ref[s].astype(F32))
            g = part if g is None else g + part
        flat = lambda ref: ref[...].reshape(r, tc)
        delta, m2, v2 = _adamw_math(flat(w_ref), g, flat(m_ref), flat(v_ref))
        for ref, val in ((g_ref, g), (d_ref, delta), (m2_ref, m2), (v2_ref, v2)):
            ref[...] = val.reshape(r, 1, tc)

    blk = pl.BlockSpec((r, 1, tc), lambda i: (0, 0, i))
    shape = jax.ShapeDtypeStruct((r, 1, D), F32)
    return pl.pallas_call(
        body, name=name, grid=(D // tc,),
        in_specs=[blk, blk, blk, pl.BlockSpec((N_CHIP, r, tc), lambda i: (0, 0, i)),
                  pl.BlockSpec((None, r, tc), lambda i: (_mesh_pos()[3], 0, i)),
                  pl.BlockSpec((None, r, tc), lambda i: (_mesh_pos()[3] // 2, 0, i))],
        out_specs=[blk] * 4, out_shape=[shape] * 4,
        compiler_params=_params("parallel"))(w, m, v, land, a, half)


def adamw_small(name, w, m, v, parts):
    n = w.shape[1]

    def body(w_ref, m_ref, v_ref, p_ref, g_ref, d_ref, m2_ref, v2_ref):
        g = p_ref[0:1, :]
        for s in range(1, N_DEV):
            g = g + p_ref[s:s + 1, :]
        delta, m2, v2 = _adamw_math(w_ref[...], g, m_ref[...], v_ref[...])
        g_ref[...] = g
        d_ref[...] = delta
        m2_ref[...] = m2
        v2_ref[...] = v2

    shape = jax.ShapeDtypeStruct((1, n), F32)
    return pl.pallas_call(body, name=name, out_shape=[shape] * 4,
                          compiler_params=pltpu.CompilerParams(vmem_limit_bytes=VMEM_LIMIT_BYTES))(w, m, v, parts)


def _rope_tables(positions):
    half = RET_DK // 2
    inv_freq = 1.0 / jnp.power(RET_THETA_BASE, jnp.linspace(0.0, 1.0, half, dtype=F32))
    ang = positions.astype(F32)[:, None] * inv_freq
    cos, sin = jnp.cos(ang), jnp.sin(ang)
    cosf = jnp.repeat(cos, 2, axis=-1)
    sins = jnp.stack([-sin, sin], axis=-1).reshape(cosf.shape)
    return cosf, sins


def _pad_to(a, axis, size):
    pad = [(0, 0)] * a.ndim
    pad[axis] = (0, size - a.shape[axis])
    return jnp.pad(a, pad)


def _round_up(n, m):
    return -(-n // m) * m


def kernel(x, p, positions, attn_norm_w, ffn_norm_w, ple_norm_w, final_norm_w, ab_w_in, ab_gla_gate_up, ab_gla_gate_b, ab_ret_norm_w, ab_gla_norm_w, ab_w_out, c_w_qkv, c_w_out, ffn_w_gate, ffn_w_up, ffn_w_down, ple_w_proj, ple_w_gate, loss_target, m_attn_norm_w, m_ffn_norm_w, m_ple_norm_w, m_final_norm_w, m_ab_w_in, m_ab_gla_gate_up, m_ab_gla_gate_b, m_ab_ret_norm_w, m_ab_gla_norm_w, m_ab_w_out, m_c_w_qkv, m_c_w_out, m_ffn_w_gate, m_ffn_w_up, m_ffn_w_down, m_ple_w_proj, m_ple_w_gate, v_attn_norm_w, v_ffn_norm_w, v_ple_norm_w, v_final_norm_w, v_ab_w_in, v_ab_gla_gate_up, v_ab_gla_gate_b, v_ab_ret_norm_w, v_ab_gla_norm_w, v_ab_w_out, v_c_w_qkv, v_c_w_out, v_ffn_w_gate, v_ffn_w_up, v_ffn_w_down, v_ple_w_proj, v_ple_w_gate):
    T, D = x.shape[1], x.shape[2]
    depth = attn_norm_w.shape[0]
    assert ab_w_in.shape[0] == 1 and c_w_qkv.shape[0] == 1 and depth == 2, "one even and one odd layer"
    me = 4 * lax.axis_index("x") + 2 * lax.axis_index("y") + lax.axis_index("c")
    in_shard = ab_w_in.shape[2]
    in_width = in_shard * N_DEV
    assert in_width == OFF_LR + GLA_GATE_RANK
    fs = ffn_w_gate.shape[2]
    fp = _round_up(fs, LANE)
    gu_cols = ab_gla_gate_up.shape[2]

    bf = lambda a: a.astype(BF16)
    tr_ = lambda a: jnp.swapaxes(a, -1, -2)
    wg_t, wu_t = tr_(ffn_w_gate), tr_(ffn_w_up)
    srcs = {"w_in": bf(tr_(ab_w_in[0]))}
    group_keys = [["w_in"], ["gu", "w_oab"], ["wg0", "wu0"], ["wd0", "wpg0", "wpp0"], ["w_qkv", "w_oc"],
                  ["wg1", "wu1"], ["wd1", "wpg1", "wpp1"]]
    G_IN, G_OUT, G_QKV = 0, 1, 4
    g_ffn = lambda layer: (2, 3) if layer == 0 else (5, 6)

    def landing(key):
        a = srcs[key]
        rows = fp if key[:2] in ("wg", "wu", "wd") else a.shape[0]
        buf = lax.empty((N_DEV, rows) + a.shape[1:], a.dtype)
        if rows > a.shape[0]:
            zeros = jnp.zeros((N_DEV, rows - a.shape[0]) + a.shape[1:], a.dtype)
            buf = lax.dynamic_update_slice(buf, zeros, (0, a.shape[0]) + (0,) * (a.ndim - 1))
        return lax.dynamic_update_slice(buf, a[None], (me,) + (0,) * a.ndim)

    _, chip_handles, gather_token = exchange_call(
        "gather_start_in", [], [("to_chips", [(srcs[k], landing(k)) for k in group_keys[G_IN]])])
    (gather_token, w_out_, gu_, w_qkv_, w_oc_, wg_, wu_, wd_, wpg_, wpp_) = lax.optimization_barrier(
        (gather_token, ab_w_out, ab_gla_gate_up, c_w_qkv, c_w_out, wg_t, wu_t, ffn_w_down, ple_w_gate, ple_w_proj))
    srcs.update(w_oab=bf(w_out_[0]), gu=gu_[0], w_qkv=bf(w_qkv_[0]), w_oc=bf(w_oc_[0]))
    for l in range(depth):
        srcs[f"wg{l}"] = bf(wg_[l])
        srcs[f"wu{l}"] = bf(wu_[l])
        srcs[f"wd{l}"] = bf(wd_[l])
        srcs[f"wpg{l}"] = bf(wpg_[l])
        srcs[f"wpp{l}"] = bf(wpp_[l])
    _, more, gather_token = exchange_call(
        "gather_start", [], [("to_chips", [(srcs[k], landing(k)) for k in keys]) for keys in group_keys[1:]],
        deps=(gather_token,))
    chip_handles = chip_handles + more
    weights = {}

    def gather_wait(gi, dep):
        lands = [(land,) for _, land in chip_handles[gi][0]]
        _, (passing,), _ = exchange_call(
            f"gather{gi}_pass", [("to_chips", chip_handles[gi])], [("pass_on", lands)], deps=(dep,))
        (complete,), _, _ = exchange_call(f"gather{gi}_done", [("pass_on", passing)], [])
        weights.update(zip(group_keys[gi], [land for (land,) in complete]))

    gb = ab_gla_gate_b
    hn_w = jnp.concatenate([ab_ret_norm_w, ab_gla_norm_w], axis=1)
    cosf, sins = _rope_tables(positions[0])
    p_bf = bf(p[:, 0])

    xs = x[0]
    saved = []
    for i in range(depth):
        nm = f"l{i}_"
        w_attn, w_ffn, w_ple = attn_norm_w[i:i + 1], ffn_norm_w[i:i + 1], ple_norm_w[i:i + 1]
        (xn,) = rowwise(nm + "norm_attn", lambda a, w: (_rms(a, w),), T, [("row", xs), ("full", w_attn)],
                        [("row", D, BF16)], deps=(gather_token,) if i == 0 else ())
        if i % 2 == 0:
            gather_wait(G_IN, xn)
            w_in_t = weights["w_in"].reshape(1, 1, in_width, D)
            w_lr_t = _pad_to(w_in_t[0, 0, OFF_LR:], 0, LANE).reshape(1, 1, LANE, D)
            z = mmt_fwd(nm + "mm_in", xn, w_in_t, 0, F32, n=OFF_LR)
            glr = mmt_fwd(nm + "mm_lr", xn, w_lr_t, 0, F32)
            oraw = retention_fwd(nm + "ret_fwd", z, cosf, sins, RET_V + GLA_V)
            gather_wait(G_OUT, oraw)
            w_oab = weights["w_oab"].reshape(1, 1, D, D)
            gu_full = _pad_to(weights["gu"].transpose(1, 0, 2).reshape(GLA_GATE_RANK, GLA_QK), 0, LANE)
            oraw = gla_fwd(nm + "gla_fwd", z, glr, gu_full, gb, oraw)
            o = headnorm_fwd(nm + "headnorm_fwd", oraw, z, hn_w)
            h1, hn = mm_add_norm(nm + "mm_out", o, w_oab, xs, w_ffn)
            mixer_saved = (z, glr, oraw, o)
        else:
            gather_wait(G_QKV, xn)
            w_qkv = weights["w_qkv"].reshape((1,) + weights["w_qkv"].shape)
            w_oc = weights["w_oc"].reshape(1, 1, D, D)
            qkv = mm_nn(nm + "mm_qkv", xn, w_qkv, 0, BF16)
            o, lse = attn_fwd(nm + "attn_fwd", qkv)
            h1, hn = mm_add_norm(nm + "mm_out", o, w_oc, xs, w_ffn)
            mixer_saved = (qkv, o, lse)
        gather_wait(g_ffn(i)[0], hn)
        wg = weights[f"wg{i}"].reshape(1, N_DEV, fp, D)
        wu = weights[f"wu{i}"].reshape(1, N_DEV, fp, D)
        g, u, act = ffn_gate_up(nm + "ffn_gate_up", hn, wg, wu)
        gather_wait(g_ffn(i)[1], act)
        wd = weights[f"wd{i}"].reshape(1, 1, N_DEV * fp, D)
        wpg = weights[f"wpg{i}"].reshape(1, 1, D, D)
        wpp = weights[f"wpp{i}"].reshape((1,) + weights[f"wpp{i}"].shape)
        h2, pn = mm_add_norm(nm + "mm_down", act, wd, h1, w_ple)
        x_next, s, e = ple_fwd(nm + "ple", pn, wpg, p_bf[i], wpp, h2)
        mixer_w = (w_in_t, w_lr_t, w_oab, gu_full) if i % 2 == 0 else (w_qkv, w_oc)
        saved.append((xs, xn, mixer_saved, mixer_w, (wg, wu, wd, wpg), h1, hn, g, u, act, h2, pn, s, e))
        xs = x_next

    def loss_fn(a, w, t):
        diff = _rms(a, w) - t
        dx, dw = _rms_bwd(a, w, diff * (1.0 / D))
        part = 0.5 * jnp.sum(jnp.mean(diff * diff, axis=-1, keepdims=True), axis=0, keepdims=True)
        return dx, dw, jnp.broadcast_to(part, (1, LANE))

    dx, d_final_w, loss_part = rowwise("loss_head", loss_fn, T,
                                       [("row", xs), ("full", final_norm_w[None, :]), ("row", loss_target[0])],
                                       [("row", D, F32), ("acc", D), ("acc", LANE)])
    loss = lax.psum(loss_part[0, 0], ("x", "y", "c"))

    grads = {}
    on_chip = []
    scatters = []

    def scatter_start(name, keys, deps=()):
        waits = [("halves", on_chip[0][1])] if on_chip else []
        starts = [("halves", [(grads[k], lax.empty((N_CHIP,) + grads[k].shape[1:], BF16)) for k in keys])] if keys else []
        waited, handles, token = exchange_call(name, waits, starts, deps=deps)
        if on_chip:
            done_keys, _ = on_chip.pop()
            sums = [chip_sum(f"{name}_sum{j}", a, half) for j, (a, half) in enumerate(waited[0])]
            _, (handle,), token = exchange_call(
                name + "_chips", [], [("chip_sums", [(cs, lax.empty(cs.shape, BF16)) for cs in sums])])
            scatters.append((done_keys, handle, waited[0]))
        if keys:
            on_chip.append((keys, handles[0]))
        return token

    d_attn_w, d_ffn_w, d_ple_w = [None] * depth, [None] * depth, [None] * depth
    for i in reversed(range(depth)):
        nm = f"l{i}_b_"
        xs_i, xn, mixer_saved, mixer_w, (wg, wu, wd, wpg), h1, hn, g, u, act, h2, pn, s, e = saved[i]
        w_attn, w_ffn, w_ple = attn_norm_w[i:i + 1], ffn_norm_w[i:i + 1], ple_norm_w[i:i + 1]

        def ple_bwd(d, sv, ev):
            gate = _sigmoid(sv)
            return d * gate, d * ev * gate * (1.0 - gate)

        de, ds = rowwise(nm + "ple_out", ple_bwd, T, [("row", dx), ("row", s), ("row", e)],
                         [("row", D, BF16), ("row", D, BF16)], deps=(loss.reshape(1, 1),) if i == depth - 1 else ())
        grads[("ple_w_proj", i)] = mm_tn(nm + "mm_ple_proj_w", p_bf[i], de, N_DEV, BF16)
        grads[("ple_w_gate", i)] = mm_tn(nm + "mm_ple_gate_w", pn, ds, 1, BF16).reshape(N_DEV, D // N_DEV, D)
        dpn = mm_nt(nm + "mm_ple_gate_x", ds, wpg, 0, F32)

        def norm_bwd_add(a, w, dn, dres):
            dxx, dw = _rms_bwd(a, w, dn)
            tot = dres + dxx
            return tot, tot, dw

        dh2, dh2_bf, d_ple_w[i] = rowwise(nm + "norm_ple", norm_bwd_add, T,
                                          [("row", h2), ("full", w_ple), ("row", dpn), ("row", dx)],
                                          [("row", D, F32), ("row", D, BF16), ("acc", D)])
        grads[("ffn_w_down", i)] = mm_tn(nm + "mm_down_w", act, dh2_bf, 1, BF16).reshape(N_DEV, fp, D)
        token = scatter_start(nm + "scatter_ple_down", [("ple_w_proj", i), ("ple_w_gate", i), ("ffn_w_down", i)])
        dg, du = ffn_down_bwd(nm + "ffn_down_x", dh2_bf, wd, g, u, deps=(token,))
        grads[("ffn_w_gate", i)] = mmt_dw(nm + "mm_gate_w", dg, hn, N_DEV, BF16)
        grads[("ffn_w_up", i)] = mmt_dw(nm + "mm_up_w", du, hn, N_DEV, BF16)
        token = scatter_start(nm + "scatter_gate_up", [("ffn_w_gate", i), ("ffn_w_up", i)])
        dhn = mmt_dx_pair(nm + "mm_gate_up_x", dg, wg, du, wu, F32, deps=(token,))
        dh1, dh1_bf, d_ffn_w[i] = rowwise(nm + "norm_ffn", norm_bwd_add, T,
                                          [("row", h1), ("full", w_ffn), ("row", dhn), ("row", dh2)],
                                          [("row", D, F32), ("row", D, BF16), ("acc", D)])
        if i % 2 == 0:
            z, glr, oraw, o = mixer_saved
            w_in_t, w_lr_t, w_oab, gu_full = mixer_w
            grads[("ab_w_out", 0)] = mm_tn(nm + "mm_out_w", o, dh1_bf, 1, BF16).reshape(N_DEV, D // N_DEV, D)
            token = scatter_start(nm + "scatter_out", [("ab_w_out", 0)])
            do = mm_nt(nm + "mm_out_x", dh1_bf, w_oab, 0, F32, deps=(token,))
            d_oraw, d_gates, d_hn_w = headnorm_bwd(nm + "headnorm", oraw, z, hn_w, do)
            d_rq, d_rk, d_rv = retention_bwd(nm + "ret", z, cosf, sins, d_oraw)
            d_gq, d_gk, d_gv, d_glr4, d_gu, d_gb = gla_bwd(nm + "gla", z, glr, gu_full, gb, d_oraw)
            dz = jnp.concatenate([d_rq, d_rk, d_rv, d_gates[:, :RET_V], d_gq, d_gk, d_gv, d_gates[:, RET_V:]], axis=1)
            (d_glr,) = rowwise(nm + "sum_lr", lambda *a: (a[0] + a[1] + a[2] + a[3],), T,
                               [("row", d_glr4[hh]) for hh in range(GLA_HEADS)], [("row", LANE, BF16)])
            dwt_in = mmt_dw(nm + "mm_in_w", dz, xn, 1, BF16, rows=in_width)
            dwt_in = mmt_dw_rows(nm + "mm_lr_w", d_glr, xn, dwt_in, OFF_LR, GLA_GATE_RANK)
            grads[("ab_w_in", 0)] = dwt_in.reshape(N_DEV, in_shard, D)
            token = scatter_start(nm + "scatter_in", [("ab_w_in", 0)])
            dxn_a = mmt_dx_wide(nm + "mm_in_x", dz, w_in_t, F32, n=OFF_LR, deps=(token,))
            token = scatter_start(nm + "scatter_in_on", [], deps=(dxn_a,))
            dxn_b = mmt_dx(nm + "mm_lr_x", d_glr, w_lr_t, 0, F32, deps=(token,))
        else:
            qkv, o, lse = mixer_saved
            w_qkv, w_oc = mixer_w
            grads[("c_w_out", 0)] = mm_tn(nm + "mm_out_w", o, dh1_bf, 1, BF16).reshape(N_DEV, D // N_DEV, D)
            do = mm_nt(nm + "mm_out_x", dh1_bf, w_oc, 0, BF16)
            dq, dk, dv = attn_bwd(nm + "attn", qkv, o, lse, do)
            dqkv = jnp.concatenate([dq, dk, dv], axis=1)
            grads[("c_w_qkv", 0)] = mm_tn(nm + "mm_qkv_w", xn, dqkv, N_DEV, BF16)
            token = scatter_start(nm + "scatter_attn", [("c_w_out", 0), ("c_w_qkv", 0)])
            dxn_a = mm_nt_wide(nm + "mm_qkv_x", dqkv, w_qkv, F32, deps=(token,))
            dxn_b = None
        dxn = [dxn_a] if dxn_b is None else [dxn_a, dxn_b]

        def norm_bwd_in(a, w, *rest):
            dxx, dw = _rms_bwd(a, w, sum(rest[1:-1], rest[0]))
            return rest[-1] + dxx, dw

        dx, d_attn_w[i] = rowwise(nm + "norm_attn", norm_bwd_in, T,
                                  [("row", xs_i), ("full", w_attn)] + [("row", d) for d in dxn] + [("row", dh1)],
                                  [("row", D, F32), ("acc", D)])

    small_names = ["attn_norm_w", "ffn_norm_w", "ple_norm_w", "final_norm_w", "ab_gla_gate_b", "ab_ret_norm_w",
                   "ab_gla_norm_w"]
    small_grads = [jnp.concatenate(d_attn_w, 0), jnp.concatenate(d_ffn_w, 0), jnp.concatenate(d_ple_w, 0), d_final_w[0],
                   d_gb, d_hn_w[:, :RET_V], d_hn_w[:, RET_V:]]
    small_w = [attn_norm_w, ffn_norm_w, ple_norm_w, final_norm_w, ab_gla_gate_b, ab_ret_norm_w, ab_gla_norm_w]
    small_m = [m_attn_norm_w, m_ffn_norm_w, m_ple_norm_w, m_final_norm_w, m_ab_gla_gate_b, m_ab_ret_norm_w, m_ab_gla_norm_w]
    small_v = [v_attn_norm_w, v_ffn_norm_w, v_ple_norm_w, v_final_norm_w, v_ab_gla_gate_b, v_ab_ret_norm_w, v_ab_gla_norm_w]
    sizes = [int(np.prod(a.shape)) for a in small_w]
    n_gu = GLA_GATE_RANK * GLA_QK
    n_small = _round_up(sum(sizes) + n_gu, LANE)
    pack = lambda parts: _pad_to(jnp.concatenate([a.reshape(-1) for a in parts]), 0, n_small)[None, :]
    small_part = pack(small_grads + [d_gu[:GLA_GATE_RANK]])

    cols_first = lambda a: jnp.transpose(a, (2, 0, 1))
    big_w = dict(ab_w_in=tuple(cols_first(a) for a in (ab_w_in, m_ab_w_in, v_ab_w_in)),
                 ab_w_out=(ab_w_out, m_ab_w_out, v_ab_w_out),
                 c_w_qkv=(c_w_qkv, m_c_w_qkv, v_c_w_qkv), c_w_out=(c_w_out, m_c_w_out, v_c_w_out),
                 ffn_w_gate=(wg_t, tr_(m_ffn_w_gate), tr_(v_ffn_w_gate)),
                 ffn_w_up=(wu_t, tr_(m_ffn_w_up), tr_(v_ffn_w_up)),
                 ffn_w_down=(ffn_w_down, m_ffn_w_down, v_ffn_w_down), ple_w_proj=(ple_w_proj, m_ple_w_proj, v_ple_w_proj),
                 ple_w_gate=(ple_w_gate, m_ple_w_gate, v_ple_w_gate))
    if on_chip:
        scatter_start("scatter_last", [], deps=(dx,))
    results, last = {}, dx
    for gi, (keys, handle, partials) in enumerate(scatters):
        (arrived,), _, _ = exchange_call(f"scatter_wait{gi}", [("chip_sums", handle)], [], deps=(last,))
        for (n, l), (_, land), (a, half) in zip(keys, arrived, partials):
            if n == "ab_w_in":
                results[n] = adamw_columns(f"adamw_{n}", *big_w[n], land, a, half)
            else:
                results[n] = adamw(f"adamw_{n}{l}", *big_w[n], l, land, a, half, prev=results.get(n))
            last = results[n][0]
    for n in ("ffn_w_gate", "ffn_w_up"):
        results[n] = [tr_(a) for a in results[n]]
    results["ab_w_in"] = [jnp.transpose(a, (1, 2, 0)) for a in results["ab_w_in"]]
    small_parts = gather_small("gather_small", small_part, deps=(last,)).reshape(N_DEV, n_small)

    gu_off = sum(sizes)
    own_cols = lambda a: lax.dynamic_slice_in_dim(a.reshape(GLA_GATE_RANK, GLA_QK), me * gu_cols, gu_cols, axis=1)
    small_res = adamw_small("adamw_small", pack(small_w + [jnp.zeros((n_gu,), F32)]),
                            pack(small_m + [jnp.zeros((n_gu,), F32)]), pack(small_v + [jnp.ones((n_gu,), F32)]),
                            small_parts)
    g_gu_full = small_res[0][0, gu_off:gu_off + n_gu]
    g_gu = own_cols(g_gu_full)[None]
    gu_res = adamw_small("adamw_gate_up", *[_pad_to(a.reshape(1, -1), 1, _round_up(a.size, LANE)) for a in
                                            (ab_gla_gate_up, m_ab_gla_gate_up, v_ab_gla_gate_up)],
                         jnp.concatenate([_pad_to(g_gu.reshape(1, -1), 1, _round_up(g_gu.size, LANE)),
                                          jnp.zeros((N_DEV - 1, _round_up(g_gu.size, LANE)), F32)], axis=0))
    for k in range(4):
        off = 0
        for n, a, sz in zip(small_names, small_w, sizes):
            results.setdefault(n, [None] * 4)[k] = small_res[k][0, off:off + sz].reshape(a.shape)
            off += sz
        results.setdefault("ab_gla_gate_up", [None] * 4)[k] = gu_res[k][0, :g_gu.size].reshape(ab_gla_gate_up.shape)

    order = ["attn_norm_w", "ffn_norm_w", "ple_norm_w", "final_norm_w", "ab_w_in", "ab_gla_gate_up", "ab_gla_gate_b",
             "ab_ret_norm_w", "ab_gla_norm_w", "ab_w_out", "c_w_qkv", "c_w_out", "ffn_w_gate", "ffn_w_up", "ffn_w_down",
             "ple_w_proj", "ple_w_gate"]
    return (loss, dx[None], *[results[n][0] for n in order], *[results[n][1] for n in order],
            *[results[n][2] for n in order], *[results[n][3] for n in order])
```

```python
import math

import numpy as np
import jax
import jax.numpy as jnp
from jax import lax
from jax.experimental import pallas as pl
from jax.experimental.pallas import tpu as pltpu

F32 = jnp.float32
BF16 = jnp.bfloat16
HIGHEST = lax.Precision.HIGHEST

N_DEV = 8
VMEM_LIMIT_BYTES = 48 * 1024 * 1024
LANE = 128
NORM_EPS = 1e-6

RET_HEADS, RET_DK, RET_DV = 4, 256, 256
RET_THETA_BASE = 10000.0
GLA_HEADS, GLA_DK, GLA_DV = 4, 128, 256
GLA_GATE_RANK = 16
GLA_GATE_NORM = 16.0
CHUNK = 64
ATT_HEADS = 16
DILATED_BRANCHES = ((128, 1), (512, 4), (2048, 16))
BLK = 256

ADAM_LR, ADAM_B1, ADAM_B2, ADAM_EPS, ADAM_WD, ADAM_STEP = 0.001, 0.9, 0.999, 1e-08, 0.01, 10

RET_QK = RET_HEADS * RET_DK
RET_V = RET_HEADS * RET_DV
GLA_QK = GLA_HEADS * GLA_DK
GLA_V = GLA_HEADS * GLA_DV
OFF_RQ, OFF_RK, OFF_RV, OFF_RG = 0, RET_QK, 2 * RET_QK, 2 * RET_QK + RET_V
OFF_GQ = OFF_RG + RET_V
OFF_GK = OFF_GQ + GLA_QK
OFF_GV = OFF_GK + GLA_QK
OFF_GG = OFF_GV + GLA_V
OFF_LR = OFF_GG + GLA_V


def _params(*sem):
    return pltpu.CompilerParams(dimension_semantics=sem or None, vmem_limit_bytes=VMEM_LIMIT_BYTES)


def _pick(n, cands):
    for c in cands:
        if n % c == 0:
            return c
    raise ValueError(f"no tile for {n} in {cands}")


_NN = (((1,), (0,)), ((), ()))
_NT = (((1,), (1,)), ((), ()))
_TN = (((0,), (0,)), ((), ()))
_ANY = pl.BlockSpec(memory_space=pl.ANY)
MAX_CONTRACT = 2048
_TILES = (1024, 768, 512, 256, 128)


def _mm_call(name, dims, grid, in_specs, out_spec, out_shape, args, deps=()):
    steps = grid[2]
    assert steps == 1 or out_shape.dtype == F32

    def body(a_ref, b_ref, *rest):
        o_ref = rest[len(deps)]
        part = lax.dot_general(a_ref[...].astype(BF16), b_ref[...].astype(BF16), dims, preferred_element_type=F32)
        if steps == 1:
            o_ref[...] = part.astype(o_ref.dtype)
        else:
            _accumulate(o_ref, part, pl.program_id(2) == 0)

    return pl.pallas_call(
        body, name=name, grid=grid, in_specs=list(in_specs) + [_ANY] * len(deps), out_specs=out_spec,
        out_shape=out_shape, compiler_params=_params("parallel", "parallel", "arbitrary"))(*args, *deps)


def mm_nn(name, a, w, l, out_dtype, deps=()):
    _, J, K, n = w.shape
    M = a.shape[0]
    tm, tn, tk = _pick(M, _TILES), _pick(n, _TILES), _pick(K, (MAX_CONTRACT,) + _TILES)
    nt = n // tn
    return _mm_call(
        name, _NN, (M // tm, J * nt, K // tk),
        [pl.BlockSpec((tm, tk), lambda i, j, k: (i, k)),
         pl.BlockSpec((None, None, tk, tn), lambda i, j, k: (l, j // nt, k, j % nt))],
        pl.BlockSpec((tm, tn), lambda i, j, k: (i, j)),
        jax.ShapeDtypeStruct((M, J * n), out_dtype), (a, w), deps)


def mm_nt(name, a, w, l, out_dtype, deps=()):
    _, J, K, n = w.shape
    M = a.shape[0]
    tm, tq, tc = _pick(M, _TILES), _pick(K, _TILES), _pick(n, (MAX_CONTRACT,) + _TILES)
    nc = n // tc
    return _mm_call(
        name, _NT, (M // tm, K // tq, J * nc),
        [pl.BlockSpec((tm, tc), lambda i, q, c: (i, c)),
         pl.BlockSpec((None, None, tq, tc), lambda i, q, c: (l, c // nc, q, c % nc))],
        pl.BlockSpec((tm, tq), lambda i, q, c: (i, q)),
        jax.ShapeDtypeStruct((M, K), out_dtype), (a, w), deps)


def mm_tn(name, x, dy, J, out_dtype, deps=()):
    M, K = x.shape
    n = dy.shape[1] // J
    tp, tn = _pick(K, _TILES), _pick(n, _TILES)
    nt = n // tn
    assert M <= MAX_CONTRACT
    return _mm_call(
        name, _TN, (K // tp, J * nt, 1),
        [pl.BlockSpec((M, tp), lambda i, j, r: (0, i)),
         pl.BlockSpec((M, tn), lambda i, j, r: (0, j))],
        pl.BlockSpec((None, tp, tn), lambda i, j, r: (j // nt, i, j % nt)),
        jax.ShapeDtypeStruct((J, K, n), out_dtype), (x, dy), deps)


def mmt_fwd(name, a, wt, l, out_dtype, n=None, deps=()):
    _, J, rows, K = wt.shape
    n = rows if n is None else n
    M = a.shape[0]
    tm, tn = _pick(M, _TILES), _pick(n, _TILES)
    nt = n // tn
    assert K <= MAX_CONTRACT
    return _mm_call(
        name, _NT, (M // tm, J * nt, 1),
        [pl.BlockSpec((tm, K), lambda i, j, k: (i, 0)),
         pl.BlockSpec((None, None, tn, K), lambda i, j, k: (l, j // nt, j % nt, 0))],
        pl.BlockSpec((tm, tn), lambda i, j, k: (i, j)),
        jax.ShapeDtypeStruct((M, J * n), out_dtype), (a, wt), deps)


def mmt_dx(name, dy, wt, l, out_dtype, n=None, deps=()):
    _, J, rows, K = wt.shape
    n = rows if n is None else n
    M = dy.shape[0]
    tm, tq, tc = _pick(M, _TILES), _pick(K, _TILES), _pick(n, _TILES)
    nc = n // tc
    return _mm_call(
        name, _NN, (M // tm, K // tq, J * nc),
        [pl.BlockSpec((tm, tc), lambda i, q, c: (i, c)),
         pl.BlockSpec((None, None, tc, tq), lambda i, q, c: (l, c // nc, c % nc, q))],
        pl.BlockSpec((tm, tq), lambda i, q, c: (i, q)),
        jax.ShapeDtypeStruct((M, K), out_dtype), (dy, wt), deps)


WIDE_TILE = 512


def _wide_call(name, body, M, K, a, w, a_spec, w_spec, out_dtype, deps):
    def kernel_body(a_ref, w_ref, *rest):
        o_ref = rest[len(deps)]
        o_ref[...] = body(a_ref, w_ref).astype(o_ref.dtype)

    return pl.pallas_call(
        kernel_body, name=name, grid=(M // WIDE_TILE, K // WIDE_TILE),
        in_specs=[a_spec, w_spec] + [_ANY] * len(deps),
        out_specs=pl.BlockSpec((WIDE_TILE, WIDE_TILE), lambda i, q: (i, q)),
        out_shape=jax.ShapeDtypeStruct((M, K), out_dtype),
        compiler_params=_params("parallel", "parallel"))(a, w, *deps)


def mmt_dx_wide(name, dy, wt, out_dtype, n=None, deps=()):
    _, J, rows, K = wt.shape
    n = rows if n is None else n
    M = dy.shape[0]

    def body(dy_ref, w_ref):
        return jnp.dot(dy_ref[...].astype(BF16), w_ref[...].reshape(J * n, WIDE_TILE), preferred_element_type=F32)

    return _wide_call(name, body, M, K, dy, wt,
                      pl.BlockSpec((WIDE_TILE, J * n), lambda i, q: (i, 0)),
                      pl.BlockSpec((None, J, n, WIDE_TILE), lambda i, q: (0, 0, 0, q)), out_dtype, deps)


def mmt_dx_pair(name, dy1, wt1, dy2, wt2, out_dtype, deps=()):
    _, J, n, K = wt1.shape
    M = dy1.shape[0]

    def body(dy1_ref, w1_ref, dy2_ref, w2_ref, *rest):
        o_ref = rest[len(deps)]
        acc = jnp.dot(dy1_ref[...], w1_ref[...].reshape(J * n, WIDE_TILE), preferred_element_type=F32)
        acc = acc + jnp.dot(dy2_ref[...], w2_ref[...].reshape(J * n, WIDE_TILE), preferred_element_type=F32)
        o_ref[...] = acc.astype(o_ref.dtype)

    rows = _once((WIDE_TILE, J * n), lambda i, q: (i, 0))
    cols = pl.BlockSpec((None, J, n, WIDE_TILE), lambda i, q: (0, 0, 0, q))
    return pl.pallas_call(
        body, name=name, grid=(M // WIDE_TILE, K // WIDE_TILE),
        in_specs=[rows, cols, rows, cols] + [_ANY] * len(deps),
        out_specs=pl.BlockSpec((WIDE_TILE, WIDE_TILE), lambda i, q: (i, q)),
        out_shape=jax.ShapeDtypeStruct((M, K), out_dtype),
        compiler_params=_params("parallel", "parallel"))(dy1, wt1, dy2, wt2, *deps)


def mm_nt_wide(name, a, w, out_dtype, deps=()):
    _, J, K, n = w.shape
    M = a.shape[0]

    def body(a_ref, w_ref):
        acc = None
        for j in range(J):
            part = lax.dot_general(a_ref[:, j * n:(j + 1) * n].astype(BF16), w_ref[j], _NT, preferred_element_type=F32)
            acc = part if acc is None else acc + part
        return acc

    return _wide_call(name, body, M, K, a, w,
                      pl.BlockSpec((WIDE_TILE, J * n), lambda i, q: (i, 0)),
                      pl.BlockSpec((None, J, WIDE_TILE, n), lambda i, q: (0, 0, q, 0)), out_dtype, deps)


def mmt_dw(name, dy, x, J, out_dtype, deps=(), rows=None):
    M, K = x.shape
    n = dy.shape[1] // J
    tn, tp = _pick(n, _TILES), _pick(K, _TILES)
    nt = n // tn
    assert M <= MAX_CONTRACT
    return _mm_call(
        name, _TN, (J * nt, K // tp, 1),
        [pl.BlockSpec((M, tn), lambda j, i, r: (0, j)),
         pl.BlockSpec((M, tp), lambda j, i, r: (0, i))],
        pl.BlockSpec((None, tn, tp), lambda j, i, r: (j // nt, j % nt, i)),
        jax.ShapeDtypeStruct((J, n if rows is None else rows, K), out_dtype), (dy, x), deps)


def mmt_dw_rows(name, dy, x, out, row0, rank):
    M, K = x.shape
    tp = _pick(K, _TILES)

    def body(dy_ref, x_ref, prev_ref, o_ref):
        del prev_ref
        full = lax.dot_general(dy_ref[...], x_ref[...], _TN, preferred_element_type=F32)
        o_ref[...] = full[:rank].astype(o_ref.dtype)

    return pl.pallas_call(
        body, name=name, grid=(K // tp,),
        in_specs=[pl.BlockSpec((M, dy.shape[1]), lambda i: (0, 0)), pl.BlockSpec((M, tp), lambda i: (0, i)), _ANY],
        out_specs=pl.BlockSpec((None, rank, tp), lambda i: (0, row0 // rank, i)),
        out_shape=jax.ShapeDtypeStruct(out.shape, out.dtype), input_output_aliases={2: 0},
        compiler_params=_params("parallel"))(dy, x, out)


def ffn_gate_up(name, a, wg, wu):
    _, J, n, K = wg.shape
    M = a.shape[0]
    tm, tn = _pick(M, _TILES), _pick(n, _TILES)
    nt = n // tn
    assert K <= MAX_CONTRACT

    def body(a_ref, wg_ref, wu_ref, g_ref, u_ref, act_ref):
        x = a_ref[...]
        g = lax.dot_general(x, wg_ref[...], _NT, preferred_element_type=F32)
        u = lax.dot_general(x, wu_ref[...], _NT, preferred_element_type=F32)
        g_ref[...] = g.astype(g_ref.dtype)
        u_ref[...] = u.astype(u_ref.dtype)
        act_ref[...] = (_silu_and_grad(g)[0] * u).astype(act_ref.dtype)

    w_spec = pl.BlockSpec((None, None, tn, K), lambda i, j: (0, j // nt, j % nt, 0))
    out = pl.BlockSpec((tm, tn), lambda i, j: (i, j))
    return pl.pallas_call(
        body, name=name, grid=(M // tm, J * nt),
        in_specs=[pl.BlockSpec((tm, K), lambda i, j: (i, 0)), w_spec, w_spec],
        out_specs=[out] * 3, out_shape=[jax.ShapeDtypeStruct((M, J * n), BF16)] * 3,
        compiler_params=_params("parallel", "parallel"))(a, wg, wu)


def mm_add_norm(name, a, w, res, norm_w):
    _, _, K, N = w.shape
    M = a.shape[0]
    tm, tk = _pick(M, (WIDE_TILE, 256)), _pick(K, (1024, 512, 256))
    steps = K // tk

    def body(a_ref, w_ref, res_ref, nw_ref, h_ref, hn_ref):
        k = pl.program_id(1)
        part = jnp.dot(a_ref[...], w_ref[...], preferred_element_type=F32)
        _accumulate(h_ref, part, k == 0)

        @pl.when(k == steps - 1)
        def _():
            h = h_ref[...] + res_ref[...]
            h_ref[...] = h
            hn_ref[...] = _rms(h, nw_ref[...]).astype(hn_ref.dtype)

    rows = pl.BlockSpec((tm, N), lambda i, k: (i, 0))
    return pl.pallas_call(
        body, name=name, grid=(M // tm, steps),
        in_specs=[pl.BlockSpec((tm, tk), lambda i, k: (i, k)),
                  pl.BlockSpec((None, None, tk, N), lambda i, k: (0, 0, k, 0)), rows,
                  pl.BlockSpec((1, N), lambda i, k: (0, 0))],
        out_specs=[rows, rows],
        out_shape=[jax.ShapeDtypeStruct((M, N), F32), jax.ShapeDtypeStruct((M, N), BF16)],
        compiler_params=_params("parallel", "arbitrary"))(a, w, res, norm_w)


def ple_fwd(name, pn, wpg, p_in, wpp, h):
    _, J, P, n = wpp.shape
    M, D = h.shape
    tm, tn = _pick(M, (WIDE_TILE, 256)), _pick(D, _TILES)
    per_tile = tn // n

    def body(pn_ref, wg_ref, p_ref, wp_ref, h_ref, x_ref, s_ref, e_ref):
        s = jnp.dot(pn_ref[...], wg_ref[...], preferred_element_type=F32)
        p_blk = p_ref[...]
        e = jnp.concatenate([jnp.dot(p_blk, wp_ref[j], preferred_element_type=F32) for j in range(per_tile)], axis=1)
        s_ref[...] = s
        e_ref[...] = e
        x_ref[...] = h_ref[...] + _sigmoid(s) * e

    tile = pl.BlockSpec((tm, tn), lambda i, j: (i, j))
    return pl.pallas_call(
        body, name=name, grid=(M // tm, D // tn),
        in_specs=[pl.BlockSpec((tm, D), lambda i, j: (i, 0)),
                  pl.BlockSpec((None, None, D, tn), lambda i, j: (0, 0, 0, j)),
                  pl.BlockSpec((tm, P), lambda i, j: (i, 0)),
                  pl.BlockSpec((None, per_tile, P, n), lambda i, j: (0, j, 0, 0)), tile],
        out_specs=[tile] * 3, out_shape=[jax.ShapeDtypeStruct((M, D), F32)] * 3,
        compiler_params=_params("parallel", "parallel"))(pn, wpg, p_in, wpp, h)


def ffn_down_bwd(name, dy, wd, g, u, deps=()):
    _, _, K, n = wd.shape
    M = dy.shape[0]
    tm, tq = _pick(M, _TILES), _pick(K, _TILES)
    assert n <= MAX_CONTRACT

    def body(dy_ref, w_ref, g_ref, u_ref, *rest):
        dg_ref, du_ref = rest[len(deps):]
        dact = lax.dot_general(dy_ref[...], w_ref[...], _NT, preferred_element_type=F32)
        silu, dsilu = _silu_and_grad(g_ref[...].astype(F32))
        dg_ref[...] = (dact * u_ref[...].astype(F32) * dsilu).astype(dg_ref.dtype)
        du_ref[...] = (dact * silu).astype(du_ref.dtype)

    blk = pl.BlockSpec((tm, tq), lambda i, q: (i, q))
    return pl.pallas_call(
        body, name=name, grid=(M // tm, K // tq),
        in_specs=[pl.BlockSpec((tm, n), lambda i, q: (i, 0)),
                  pl.BlockSpec((None, None, tq, n), lambda i, q: (0, 0, q, 0)), blk, blk] + [_ANY] * len(deps),
        out_specs=[blk, blk], out_shape=[jax.ShapeDtypeStruct((M, K), BF16)] * 2,
        compiler_params=_params("parallel", "parallel"))(dy, wd, g, u, *deps)


def rowwise(name, fn, rows, ins, outs, tr=256, deps=()):
    widest = max([s[1].shape[1] if s[0] != "col" else s[3] for s in ins] + [s[1] for s in outs])
    tr = min(tr if widest <= 2048 else tr // 2, rows)
    in_specs, args = [], []
    for spec in ins:
        kind, a = spec[0], spec[1]
        if kind == "row":
            in_specs.append(pl.BlockSpec((tr, a.shape[1]), lambda i: (i, 0)))
        elif kind == "col":
            cb, width = spec[2], spec[3]
            in_specs.append(pl.BlockSpec((tr, width), lambda i, cb=cb: (i, cb)))
        else:
            in_specs.append(pl.BlockSpec(a.shape, lambda i: (0, 0)))
        args.append(a)
    out_specs, out_shapes = [], []
    for spec in outs:
        if spec[0] == "row":
            out_specs.append(pl.BlockSpec((tr, spec[1]), lambda i: (i, 0)))
            out_shapes.append(jax.ShapeDtypeStruct((rows, spec[1]), spec[2]))
        else:
            out_specs.append(pl.BlockSpec((1, spec[1]), lambda i: (0, 0)))
            out_shapes.append(jax.ShapeDtypeStruct((1, spec[1]), F32))
    n_in = len(ins)

    def body(*refs):
        vals = fn(*[r[...] for r in refs[:n_in]])
        first = pl.program_id(0) == 0
        for r, v, spec in zip(refs[n_in + len(deps):], vals, outs):
            if spec[0] == "row":
                r[...] = v.astype(r.dtype)
            else:
                _accumulate(r, v, first)

    return pl.pallas_call(body, name=name, grid=(rows // tr,), in_specs=in_specs + [_ANY] * len(deps),
                          out_specs=out_specs, out_shape=out_shapes,
                          compiler_params=_params("arbitrary"))(*args, *deps)


def _accumulate(ref, v, first):
    @pl.when(first)
    def _():
        ref[...] = v

    @pl.when(jnp.logical_not(first))
    def _():
        ref[...] += v


def _rms(x, w):
    r = lax.rsqrt(jnp.mean(x * x, axis=-1, keepdims=True) + NORM_EPS)
    return x * r * w


def _rms_bwd(x, w, dy):
    r = lax.rsqrt(jnp.mean(x * x, axis=-1, keepdims=True) + NORM_EPS)
    g = dy * w
    dx = r * (g - x * (r * r) * jnp.mean(g * x, axis=-1, keepdims=True))
    dw = jnp.sum(dy * x * r, axis=0, keepdims=True)
    return dx, dw


def _sigmoid(x):
    return 1.0 / (1.0 + jnp.exp(-x))


def _silu_and_grad(g):
    s = _sigmoid(g)
    return g * s, s * (1.0 + g * (1.0 - s))


def _swap_pairs(x):
    n = x.shape[-1]
    lane = lax.broadcasted_iota(jnp.int32, x.shape, x.ndim - 1)
    return jnp.where((lane & 1) == 0, pltpu.roll(x, n - 1, x.ndim - 1), pltpu.roll(x, 1, x.ndim - 1))


def _rot(x, cosf, sins):
    return x * cosf + _swap_pairs(x) * sins


def _unrot(d, cosf, sins):
    return d * cosf + _swap_pairs(d * sins)


def _ret_log_gamma(h):
    vals = [math.log1p(-2.0 ** (-5.0 - i)) for i in range(RET_HEADS)]
    out = jnp.float32(vals[RET_HEADS - 1])
    for i in range(RET_HEADS - 2, -1, -1):
        out = jnp.where(h == i, jnp.float32(vals[i]), out)
    return out


def _fill_decays(dec_ref, lg):
    ri = lax.broadcasted_iota(jnp.int32, (BLK, BLK), 0)
    ci = lax.broadcasted_iota(jnp.int32, (BLK, BLK), 1)
    for d in range(dec_ref.shape[0]):
        dt = d * BLK + ri - ci
        dec_ref[d] = jnp.where(dt >= 0, jnp.exp(jnp.maximum(dt, 0).astype(F32) * lg), 0.0)


def _decay_row(dec_ref, qi):
    return jnp.concatenate([dec_ref[qi - kb] for kb in range(qi + 1)], axis=1)


def _once(block_shape, index_map):
    return pl.BlockSpec(block_shape, index_map, pipeline_mode=pl.Buffered(1))


def _dot(a, b):
    return jnp.dot(a.astype(BF16), b.astype(BF16), preferred_element_type=F32)


def _dot_nt(a, b):
    return lax.dot_general(a.astype(BF16), b.astype(BF16), _NT, preferred_element_type=F32)


def _dot_tn(a, b):
    return lax.dot_general(a.astype(BF16), b.astype(BF16), _TN, preferred_element_type=F32)


def retention_fwd(name, z, cosf, sins, width_out):
    T = z.shape[0]
    nq = T // BLK
    scale = RET_DK ** -0.5

    def body(q_ref, k_ref, v_ref, cos_ref, sin_ref, o_ref, krot, vb, dec_ref):
        _fill_decays(dec_ref, _ret_log_gamma(pl.program_id(0)))
        krot[...] = (_rot(k_ref[...], cos_ref[...], sin_ref[...]) * scale).astype(BF16)
        vb[...] = v_ref[...].astype(BF16)
        for qi in range(nq):
            rows, n = slice(qi * BLK, (qi + 1) * BLK), (qi + 1) * BLK
            q = _rot(q_ref[rows, :], cos_ref[rows, :], sin_ref[rows, :])
            s = _dot_nt(q, krot[0:n, :]) * _decay_row(dec_ref, qi)
            o_ref[rows, :] = _dot(s, vb[0:n, :])

    return pl.pallas_call(
        body, name=name, grid=(RET_HEADS,),
        in_specs=[pl.BlockSpec((T, RET_DK), lambda h: (0, OFF_RQ // RET_DK + h)),
                  pl.BlockSpec((T, RET_DK), lambda h: (0, OFF_RK // RET_DK + h)),
                  pl.BlockSpec((T, RET_DV), lambda h: (0, OFF_RV // RET_DV + h)),
                  _once((T, RET_DK), lambda h: (0, 0)), _once((T, RET_DK), lambda h: (0, 0))],
        out_specs=pl.BlockSpec((T, RET_DV), lambda h: (0, h)),
        out_shape=jax.ShapeDtypeStruct((T, width_out), F32),
        scratch_shapes=[pltpu.VMEM((T, RET_DK), BF16), pltpu.VMEM((T, RET_DV), BF16),
                        pltpu.VMEM((nq, BLK, BLK), F32)],
        compiler_params=_params("arbitrary"))(z, z, z, cosf, sins)


def retention_bwd(name, z, cosf, sins, do):
    T = z.shape[0]
    nq = T // BLK
    scale = RET_DK ** -0.5

    def body(q_ref, k_ref, v_ref, cos_ref, sin_ref, do_ref, dq_ref, dk_ref, dv_ref, krot, vb, dk_acc, dv_acc, dec_ref):
        _fill_decays(dec_ref, _ret_log_gamma(pl.program_id(0)))
        krot[...] = (_rot(k_ref[...], cos_ref[...], sin_ref[...]) * scale).astype(BF16)
        vb[...] = v_ref[...].astype(BF16)
        dk_acc[...] = jnp.zeros_like(dk_acc)
        dv_acc[...] = jnp.zeros_like(dv_acc)
        for qi in range(nq):
            rows, n = slice(qi * BLK, (qi + 1) * BLK), (qi + 1) * BLK
            cos_q, sin_q = cos_ref[rows, :], sin_ref[rows, :]
            q = _rot(q_ref[rows, :], cos_q, sin_q).astype(BF16)
            dout = do_ref[rows, :].astype(BF16)
            kk, vv, dec = krot[0:n, :], vb[0:n, :], _decay_row(dec_ref, qi)
            p = (_dot_nt(q, kk) * dec).astype(BF16)
            ds = (_dot_nt(dout, vv) * dec).astype(BF16)
            dq_ref[rows, :] = _unrot(_dot(ds, kk), cos_q, sin_q).astype(dq_ref.dtype)
            dk_acc[0:n, :] += _dot_tn(ds, q)
            dv_acc[0:n, :] += _dot_tn(p, dout)
        dk_ref[...] = (_unrot(dk_acc[...], cos_ref[...], sin_ref[...]) * scale).astype(dk_ref.dtype)
        dv_ref[...] = dv_acc[...].astype(dv_ref.dtype)

    head = lambda h: (0, h)
    return pl.pallas_call(
        body, name=name, grid=(RET_HEADS,),
        in_specs=[pl.BlockSpec((T, RET_DK), lambda h: (0, OFF_RQ // RET_DK + h)),
                  pl.BlockSpec((T, RET_DK), lambda h: (0, OFF_RK // RET_DK + h)),
                  pl.BlockSpec((T, RET_DV), lambda h: (0, OFF_RV // RET_DV + h)),
                  _once((T, RET_DK), lambda h: (0, 0)), _once((T, RET_DK), lambda h: (0, 0)),
                  pl.BlockSpec((T, RET_DV), head)],
        out_specs=[pl.BlockSpec((T, RET_DK), head), pl.BlockSpec((T, RET_DK), head), pl.BlockSpec((T, RET_DV), head)],
        out_shape=[jax.ShapeDtypeStruct((T, RET_QK), BF16), jax.ShapeDtypeStruct((T, RET_QK), BF16),
                   jax.ShapeDtypeStruct((T, RET_V), BF16)],
        scratch_shapes=[pltpu.VMEM((T, RET_DK), BF16), pltpu.VMEM((T, RET_DV), BF16),
                        pltpu.VMEM((T, RET_DK), F32), pltpu.VMEM((T, RET_DV), F32),
                        pltpu.VMEM((nq, BLK, BLK), F32)],
        compiler_params=_params("arbitrary"))(z, z, z, cosf, sins, do)


GLA_PAIR = 2


def _gla_chunk(q_ref, k_ref, v_ref, glr_ref, gu, gb, rows, hh, trilf):
    ck = slice(hh * GLA_DK, (hh + 1) * GLA_DK)
    zg = _dot(glr_ref[rows, :], gu[:, ck]) + gb[:, ck]
    la = (jnp.minimum(zg, 0.0) - jnp.log(1.0 + jnp.exp(-jnp.abs(zg)))) * (1.0 / GLA_GATE_NORM)
    cum = jnp.dot(trilf, la, precision=HIGHEST, preferred_element_type=F32)
    last = jnp.sum(la, axis=0, keepdims=True)
    ecum = jnp.exp(cum)
    k = k_ref[rows, ck]
    qt = q_ref[rows, ck] * (GLA_DK ** -0.5) * ecum
    kt = k * jnp.exp(-cum)
    kh = k * jnp.exp(last - cum)
    return zg, cum, last, ecum, qt, kt, kh, v_ref[rows, hh * GLA_DV:(hh + 1) * GLA_DV].astype(BF16)


def _state_decay(last):
    e = jnp.exp(jnp.broadcast_to(last, (GLA_DK, GLA_DK)).T)
    return jnp.concatenate([e] * (GLA_DV // GLA_DK), axis=1)


def _gla_specs(T):
    wk, wv = GLA_PAIR * GLA_DK, GLA_PAIR * GLA_DV
    return [_once((T, wk), lambda h: (0, OFF_GQ // wk + h)),
            _once((T, wk), lambda h: (0, OFF_GK // wk + h)),
            _once((T, wv), lambda h: (0, OFF_GV // wv + h)),
            _once((T, LANE), lambda h: (0, 0)),
            pl.BlockSpec((LANE, wk), lambda h: (0, h)),
            pl.BlockSpec((1, wk), lambda h: (0, h))]


def gla_fwd(name, z, glr, gu, gb, o_prev):
    T = z.shape[0]
    nc = T // CHUNK
    wv = GLA_PAIR * GLA_DV

    def body(q_ref, k_ref, v_ref, glr_ref, gu_ref, gb_ref, prev_ref, o_ref, s_all_ref, *S):
        del prev_ref
        gu_b, gb_v = gu_ref[...].astype(BF16), gb_ref[...]
        ri = lax.broadcasted_iota(jnp.int32, (CHUNK, CHUNK), 0)
        ci = lax.broadcasted_iota(jnp.int32, (CHUNK, CHUNK), 1)
        tril = ri >= ci
        trilf = tril.astype(F32)
        for s_ref in S:
            s_ref[...] = jnp.zeros_like(s_ref)

        def step(c, carry):
            rows = pl.ds(pl.multiple_of(c * CHUNK, CHUNK), CHUNK)
            heads = range(GLA_PAIR)
            ch = [_gla_chunk(q_ref, k_ref, v_ref, glr_ref, gu_b, gb_v, rows, hh, trilf) for hh in heads]
            a = [jnp.where(tril, _dot_nt(ch[hh][4], ch[hh][5]), 0.0) for hh in heads]
            s_prev = [S[hh][...] for hh in heads]
            intra = [_dot(a[hh], ch[hh][7]) for hh in heads]
            inter = [_dot(ch[hh][4], s_prev[hh]) for hh in heads]
            added = [_dot_tn(ch[hh][6], ch[hh][7]) for hh in heads]
            for hh in heads:
                o_ref[rows, hh * GLA_DV:(hh + 1) * GLA_DV] = intra[hh] + inter[hh]
                s_all_ref[hh, c] = s_prev[hh]
                S[hh][...] = s_prev[hh] * _state_decay(ch[hh][2]) + added[hh]
            return carry

        lax.fori_loop(0, nc, step, 0)

    n_in = 6
    return pl.pallas_call(
        body, name=name, grid=(GLA_HEADS // GLA_PAIR,),
        in_specs=_gla_specs(T) + [pl.BlockSpec(memory_space=pl.ANY)],
        out_specs=[pl.BlockSpec((T, wv), lambda h: (0, RET_V // wv + h)),
                   pl.BlockSpec((GLA_PAIR, nc, GLA_DK, GLA_DV), lambda h: (h, 0, 0, 0))],
        out_shape=[jax.ShapeDtypeStruct(o_prev.shape, F32),
                   jax.ShapeDtypeStruct((GLA_HEADS, nc, GLA_DK, GLA_DV), F32)],
        scratch_shapes=[pltpu.VMEM((GLA_DK, GLA_DV), F32)] * GLA_PAIR,
        input_output_aliases={n_in: 0},
        compiler_params=_params("arbitrary"))(z, z, z, glr, gu, gb, o_prev)


def gla_bwd(name, z, glr, gu, gb, do, states):
    T = z.shape[0]
    nc = T // CHUNK

    def body(q_ref, k_ref, v_ref, glr_ref, gu_ref, gb_ref, do_ref, s_all,
             dq_ref, dk_ref, dv_ref, dglr_ref, dgu_ref, dgb_ref, dS):
        gu_b, gb_v = gu_ref[...].astype(BF16), gb_ref[...]
        ri = lax.broadcasted_iota(jnp.int32, (CHUNK, CHUNK), 0)
        ci = lax.broadcasted_iota(jnp.int32, (CHUNK, CHUNK), 1)
        tril = ri >= ci
        trilf = tril.astype(F32)
        triuf = (ri <= ci).astype(F32)
        last_row = lax.broadcasted_iota(jnp.int32, (CHUNK, GLA_DK), 0) == CHUNK - 1
        ones8 = jnp.ones((8, GLA_DV), F32)

        heads = range(GLA_PAIR)

        dS[...] = jnp.zeros_like(dS)
        dgu_ref[...] = jnp.zeros_like(dgu_ref)
        dgb_ref[...] = jnp.zeros_like(dgb_ref)

        def bstep(i, carry):
            c = nc - 1 - i
            rows = pl.ds(pl.multiple_of(c * CHUNK, CHUNK), CHUNK)
            glr_c = glr_ref[rows, :]
            cks = [slice(hh * GLA_DK, (hh + 1) * GLA_DK) for hh in heads]
            cvs = [slice(hh * GLA_DV, (hh + 1) * GLA_DV) for hh in heads]
            ch = [_gla_chunk(q_ref, k_ref, v_ref, glr_ref, gu_b, gb_v, rows, hh, trilf) for hh in heads]
            zg, cum, last, ecum, qt, kt, kh, v = [[ch[hh][j] for hh in heads] for j in range(8)]
            s_prev = [s_all[hh, c] for hh in heads]
            ds_new = [dS[hh] for hh in heads]
            dout = [do_ref[rows, cvs[hh]].astype(BF16) for hh in heads]
            a = [jnp.where(tril, _dot_nt(qt[hh], kt[hh]), 0.0) for hh in heads]
            da = [jnp.where(tril, _dot_nt(dout[hh], v[hh]), 0.0) for hh in heads]
            dv_a = [_dot_tn(a[hh], dout[hh]) for hh in heads]
            dv_b = [_dot(kh[hh], ds_new[hh]) for hh in heads]
            dqt_a = [_dot(da[hh], kt[hh]) for hh in heads]
            dqt_b = [_dot_nt(dout[hh], s_prev[hh]) for hh in heads]
            dkt = [_dot_tn(da[hh], qt[hh]) for hh in heads]
            dkh = [_dot_nt(v[hh], ds_new[hh]) for hh in heads]
            ds_add = [_dot_tn(qt[hh], dout[hh]) for hh in heads]
            rs = [lax.dot_general(ones8, ds_new[hh] * s_prev[hh], _NT, precision=HIGHEST, preferred_element_type=F32)
                  for hh in heads]
            dcum = []
            for hh in heads:
                dv_ref[rows, cvs[hh]] = (dv_a[hh] + dv_b[hh]).astype(dv_ref.dtype)
                dS[hh] = ds_new[hh] * _state_decay(last[hh]) + ds_add[hh]
                dqt = dqt_a[hh] + dqt_b[hh]
                dq_ref[rows, cks[hh]] = (dqt * ecum[hh] * (GLA_DK ** -0.5)).astype(dq_ref.dtype)
                dk_ref[rows, cks[hh]] = (dkt[hh] * jnp.exp(-cum[hh])
                                         + dkh[hh] * jnp.exp(last[hh] - cum[hh])).astype(dk_ref.dtype)
                dkh_kh = dkh[hh] * kh[hh]
                dlast = (jnp.sum(dkh_kh, axis=0, keepdims=True)
                         + jnp.exp(last[hh]) * (jnp.sum(rs[hh], axis=0, keepdims=True) * 0.125))
                dcum.append(dqt * qt[hh] - dkt[hh] * kt[hh] - dkh_kh + jnp.where(last_row, dlast, 0.0))
            dla = [jnp.dot(triuf, dcum[hh], precision=HIGHEST, preferred_element_type=F32) for hh in heads]
            dzg = [dla[hh] * (1.0 / GLA_GATE_NORM) * _sigmoid(-zg[hh]) for hh in heads]
            dglr = [_dot_nt(dzg[hh], gu_b[:, cks[hh]]) for hh in heads]
            dgu = [_dot_tn(glr_c, dzg[hh]) for hh in heads]
            for hh in heads:
                dglr_ref[hh, rows, :] = dglr[hh]
                dgu_ref[:, cks[hh]] += dgu[hh]
                dgb_ref[:, cks[hh]] += jnp.sum(dzg[hh], axis=0, keepdims=True)
            return carry

        lax.fori_loop(0, nc, bstep, 0)

    wk, wv = GLA_PAIR * GLA_DK, GLA_PAIR * GLA_DV
    return pl.pallas_call(
        body, name=name, grid=(GLA_HEADS // GLA_PAIR,),
        in_specs=_gla_specs(T) + [_once((T, wv), lambda h: (0, RET_V // wv + h)),
                                  _once((GLA_PAIR, nc, GLA_DK, GLA_DV), lambda h: (h, 0, 0, 0))],
        out_specs=[pl.BlockSpec((T, wk), lambda h: (0, h)), pl.BlockSpec((T, wk), lambda h: (0, h)),
                   pl.BlockSpec((T, wv), lambda h: (0, h)),
                   pl.BlockSpec((GLA_PAIR, T, LANE), lambda h: (h, 0, 0)),
                   pl.BlockSpec((LANE, wk), lambda h: (0, h)), pl.BlockSpec((1, wk), lambda h: (0, h))],
        out_shape=[jax.ShapeDtypeStruct((T, GLA_QK), BF16), jax.ShapeDtypeStruct((T, GLA_QK), BF16),
                   jax.ShapeDtypeStruct((T, GLA_V), BF16), jax.ShapeDtypeStruct((GLA_HEADS, T, LANE), F32),
                   jax.ShapeDtypeStruct((LANE, GLA_QK), F32), jax.ShapeDtypeStruct((1, GLA_QK), F32)],
        scratch_shapes=[pltpu.VMEM((GLA_PAIR, GLA_DK, GLA_DV), F32)],
        compiler_params=_params("arbitrary"))(z, z, z, glr, gu, gb, do, states)


HN_HEADS = RET_HEADS + GLA_HEADS
HN_W = RET_DV


def _gate_col(h):
    return jnp.where(h < RET_HEADS, OFF_RG // HN_W + h, OFF_GG // HN_W + h - RET_HEADS)


def headnorm_fwd(name, oraw, z, w):
    T = oraw.shape[0]
    tr = _pick(T, _TILES)

    def body(o_ref, g_ref, w_ref, y_ref):
        y_ref[...] = (_rms(o_ref[...], w_ref[...]) * _silu_and_grad(g_ref[...])[0]).astype(y_ref.dtype)

    return pl.pallas_call(
        body, name=name, grid=(HN_HEADS, T // tr),
        in_specs=[pl.BlockSpec((tr, HN_W), lambda h, i: (i, h)),
                  pl.BlockSpec((tr, HN_W), lambda h, i: (i, _gate_col(h))),
                  pl.BlockSpec((1, HN_W), lambda h, i: (0, h))],
        out_specs=pl.BlockSpec((tr, HN_W), lambda h, i: (i, h)),
        out_shape=jax.ShapeDtypeStruct((T, HN_HEADS * HN_W), BF16),
        compiler_params=_params("arbitrary", "arbitrary"))(oraw, z, w)


def headnorm_bwd(name, oraw, z, w, dy):
    T = oraw.shape[0]
    tr = _pick(T, _TILES)

    def body(o_ref, g_ref, w_ref, dy_ref, do_ref, dg_ref, dw_ref):
        o, wv, dyv = o_ref[...], w_ref[...], dy_ref[...].astype(F32)
        silu, dsilu = _silu_and_grad(g_ref[...])
        n = _rms(o, wv)
        dg_ref[...] = (dyv * n * dsilu).astype(dg_ref.dtype)
        dx, dw = _rms_bwd(o, wv, dyv * silu)
        do_ref[...] = dx
        _accumulate(dw_ref, dw, pl.program_id(1) == 0)

    blk = pl.BlockSpec((tr, HN_W), lambda h, i: (i, h))
    return pl.pallas_call(
        body, name=name, grid=(HN_HEADS, T // tr),
        in_specs=[blk, pl.BlockSpec((tr, HN_W), lambda h, i: (i, _gate_col(h))),
                  pl.BlockSpec((1, HN_W), lambda h, i: (0, h)), blk],
        out_specs=[blk, blk, pl.BlockSpec((1, HN_W), lambda h, i: (0, h))],
        out_shape=[jax.ShapeDtypeStruct((T, HN_HEADS * HN_W), F32),
                   jax.ShapeDtypeStruct((T, HN_HEADS * HN_W), BF16),
                   jax.ShapeDtypeStruct((1, HN_HEADS * HN_W), F32)],
        compiler_params=_params("arbitrary", "arbitrary"))(oraw, z, w, dy)


N_MASKS = 4


def _check_mask_classes(T):
    for window, dilation in DILATED_BRANCHES[:-1]:
        assert window < (N_MASKS - 1) * BLK - (BLK - 1) and BLK % dilation == 0
    assert DILATED_BRANCHES[-1][0] >= T and BLK % DILATED_BRANCHES[-1][1] == 0


def _fill_masks(logm_ref):
    ri = lax.broadcasted_iota(jnp.int32, (BLK, BLK), 0)
    ci = lax.broadcasted_iota(jnp.int32, (BLK, BLK), 1)
    for d in range(N_MASKS):
        dt = d * BLK + ri - ci
        mult = jnp.zeros((BLK, BLK), F32)
        for window, dilation in DILATED_BRANCHES:
            hit = (dt >= 0) & (dt <= window) & ((dt & (dilation - 1)) == 0)
            mult = mult + hit.astype(F32)
        logm_ref[d] = jnp.where(mult > 0, jnp.log(jnp.maximum(mult, 1.0)), -1e30)


def _mask_row(ref, qi):
    return jnp.concatenate([ref[min(qi - kb, N_MASKS - 1)] for kb in range(qi + 1)], axis=1)


def attn_fwd(name, qkv):
    T = qkv.shape[0]
    D = qkv.shape[1] // 3
    dh = D // ATT_HEADS
    nq = T // BLK
    scale = dh ** -0.5

    _check_mask_classes(T)

    def body(q_ref, k_ref, v_ref, o_ref, lse_ref, logm_ref):
        @pl.when(pl.program_id(0) == 0)
        def _():
            _fill_masks(logm_ref)

        for q0 in range(0, nq, 2):
            qis = range(q0, min(q0 + 2, nq))
            rows = [slice(qi * BLK, (qi + 1) * BLK) for qi in qis]
            ns = [(qi + 1) * BLK for qi in qis]
            s = [_dot_nt(q_ref[r, :], k_ref[0:n, :]) for r, n in zip(rows, ns)]
            s = [x * scale + _mask_row(logm_ref, qi) for x, qi in zip(s, qis)]
            m = [jnp.max(x, axis=-1, keepdims=True) for x in s]
            p = [jnp.exp(x - mx) for x, mx in zip(s, m)]
            l = [jnp.sum(x, axis=-1, keepdims=True) for x in p]
            pv = [_dot(x, v_ref[0:n, :]) for x, n in zip(p, ns)]
            for r, x, lx, mx in zip(rows, pv, l, m):
                o_ref[r, :] = (x / lx).astype(o_ref.dtype)
                lse_ref[r, :] = jnp.broadcast_to(mx + jnp.log(lx), (BLK, LANE))

    return pl.pallas_call(
        body, name=name, grid=(ATT_HEADS,),
        in_specs=[pl.BlockSpec((T, dh), lambda h: (0, h)),
                  pl.BlockSpec((T, dh), lambda h: (0, ATT_HEADS + h)),
                  pl.BlockSpec((T, dh), lambda h: (0, 2 * ATT_HEADS + h))],
        out_specs=[pl.BlockSpec((T, dh), lambda h: (0, h)),
                   pl.BlockSpec((None, T, LANE), lambda h: (h, 0, 0))],
        out_shape=[jax.ShapeDtypeStruct((T, D), BF16), jax.ShapeDtypeStruct((ATT_HEADS, T, LANE), F32)],
        scratch_shapes=[pltpu.VMEM((N_MASKS, BLK, BLK), F32)],
        compiler_params=_params("arbitrary"))(qkv, qkv, qkv)


def attn_bwd(name, qkv, o, lse, do):
    T = qkv.shape[0]
    D = qkv.shape[1] // 3
    dh = D // ATT_HEADS
    nq = T // BLK
    scale = dh ** -0.5

    _check_mask_classes(T)

    def body(q_ref, k_ref, v_ref, o_ref, lse_ref, do_ref, dq_ref, dk_ref, dv_ref, dk_acc, dv_acc, logm_ref):
        @pl.when(pl.program_id(0) == 0)
        def _():
            _fill_masks(logm_ref)

        dk_acc[...] = jnp.zeros_like(dk_acc)
        dv_acc[...] = jnp.zeros_like(dv_acc)
        for qi in range(nq):
            rows, n = slice(qi * BLK, (qi + 1) * BLK), (qi + 1) * BLK
            q, dout = q_ref[rows, :], do_ref[rows, :]
            kk, vv = k_ref[0:n, :], v_ref[0:n, :]
            delta = jnp.sum(dout.astype(F32) * o_ref[rows, :].astype(F32), axis=-1, keepdims=True)
            lse = jnp.max(lse_ref[rows, :], axis=-1, keepdims=True)
            p = jnp.exp(_dot_nt(q, kk) * scale + _mask_row(logm_ref, qi) - lse)
            ds = (p * (_dot_nt(dout, vv) - delta) * scale).astype(BF16)
            dq_ref[rows, :] = _dot(ds, kk).astype(dq_ref.dtype)
            dk_acc[0:n, :] += _dot_tn(ds, q)
            dv_acc[0:n, :] += _dot_tn(p, dout)
        dk_ref[...] = dk_acc[...].astype(dk_ref.dtype)
        dv_ref[...] = dv_acc[...].astype(dv_ref.dtype)

    full = pl.BlockSpec((T, dh), lambda h: (0, h))
    return pl.pallas_call(
        body, name=name, grid=(ATT_HEADS,),
        in_specs=[full, pl.BlockSpec((T, dh), lambda h: (0, ATT_HEADS + h)),
                  pl.BlockSpec((T, dh), lambda h: (0, 2 * ATT_HEADS + h)),
                  full, pl.BlockSpec((None, T, LANE), lambda h: (h, 0, 0)), full],
        out_specs=[full, full, full],
        out_shape=[jax.ShapeDtypeStruct((T, D), BF16)] * 3,
        scratch_shapes=[pltpu.VMEM((T, dh), F32), pltpu.VMEM((T, dh), F32), pltpu.VMEM((N_MASKS, BLK, BLK), F32)],
        compiler_params=_params("arbitrary"))(qkv, qkv, qkv, o, lse, do)


def _mesh_pos():
    mx, my, mc = lax.axis_index("x"), lax.axis_index("y"), lax.axis_index("c")
    return mx, my, mc, 4 * mx + 2 * my + mc


def _peer(k, mx, my, mc):
    px, py, pc = mx ^ (k >> 2), my ^ ((k >> 1) & 1), mc ^ (k & 1)
    return (px, py, pc), 4 * px + 2 * py + pc


_SIBLING = 1
_OTHER_CHIPS = (4, 2, 6)
N_CHIP = N_DEV // 2
_PLANS = {"gather": (2, N_DEV - 1), "to_chips": (2, 1 + len(_OTHER_CHIPS)), "pass_on": (1, len(_OTHER_CHIPS)),
          "halves": (2, N_CHIP), "chip_sums": (2, len(_OTHER_CHIPS))}


def _copies(kind, items, send_sems, recv_sems):
    mx, my, mc, me = _mesh_pos()
    out = []

    def add(n, src, dst, peer):
        out.append(pltpu.make_async_remote_copy(
            src_ref=src, dst_ref=dst, send_sem=send_sems.at[n], recv_sem=recv_sems.at[n],
            device_id=peer, device_id_type=pl.DeviceIdType.MESH))

    per_item = _PLANS[kind][1]
    sibling = _peer(_SIBLING, mx, my, mc)[0]
    for i, refs in enumerate(items):
        n = i * per_item
        if kind == "gather":
            for k in range(1, N_DEV):
                add(n + k - 1, refs[0], refs[1].at[me], _peer(k, mx, my, mc)[0])
        elif kind == "to_chips":
            rows = refs[0].shape[0]
            dst = refs[1].at[me] if rows == refs[1].shape[1] else refs[1].at[me, pl.ds(0, rows)]
            for j, k in enumerate((_SIBLING,) + _OTHER_CHIPS):
                add(n + j, refs[0], dst, _peer(k, mx, my, mc)[0])
        elif kind == "pass_on":
            for j, k in enumerate(_OTHER_CHIPS):
                add(n + j, refs[0].at[me ^ k], refs[0].at[me ^ k], sibling)
        elif kind == "halves":
            for chip in range(N_CHIP):
                add(n + chip, refs[0].at[2 * chip + 1 - mc], refs[1].at[chip], sibling)
        else:
            for j, k in enumerate(_OTHER_CHIPS):
                peer, to = _peer(k, mx, my, mc)
                add(n + j, refs[0].at[to // 2], refs[1].at[me // 2], peer)
    return out


_HBM = pl.BlockSpec(memory_space=pltpu.HBM)
_SEM = pl.BlockSpec(memory_space=pltpu.SEMAPHORE)
_DATAFLOW = pltpu.SideEffectType.DATAFLOW_SIDE_EFFECTING


def exchange_call(name, waits, starts, deps=()):
    bufs, slot_of = [], {}

    def slots(items):
        out = []
        for item in items:
            for b in item:
                if id(b) not in slot_of:
                    slot_of[id(b)] = len(bufs)
                    bufs.append(b)
            out.append(tuple(slot_of[id(b)] for b in item))
        return out

    wait_plan = [(kind, slots(handle[0])) for kind, handle in waits]
    start_plan = [(kind, slots(items)) for kind, items in starts]
    wait_sems = [s for _, handle in waits for s in handle[1:]]
    n_buf, n_ws, n_start = len(bufs), len(wait_sems), len(starts)

    def body(*refs):
        buf_refs, sems_in = refs[:n_buf], refs[n_buf:n_buf + n_ws]
        outs = refs[n_buf + n_ws + len(deps):]
        pick = lambda plan: [tuple(buf_refs[s] for s in item) for item in plan]
        for wi, (kind, plan) in enumerate(wait_plan):
            copies = _copies(kind, pick(plan), sems_in[2 * wi], sems_in[2 * wi + 1])
            for cp in copies:
                cp.wait_send()
            for cp in copies:
                cp.wait_recv()
        for si, (kind, plan) in enumerate(start_plan):
            for cp in _copies(kind, pick(plan), outs[2 * si], outs[2 * si + 1]):
                cp.start()
        outs[-1][...] = jnp.zeros_like(outs[-1])

    hbm_bufs = [pltpu.with_memory_space_constraint(b, pltpu.HBM) for b in bufs]
    sem_shapes = []
    for kind, plan in start_plan:
        sem_shapes += [pltpu.SemaphoreType.DMA((len(plan) * _PLANS[kind][1],))] * 2
    outs = pl.pallas_call(
        body, name=name,
        out_shape=sem_shapes + [pltpu.HBM(b.shape, b.dtype) for b in bufs] + [jax.ShapeDtypeStruct((8, LANE), F32)],
        in_specs=[_HBM] * n_buf + [_SEM] * n_ws + [_ANY] * len(deps),
        out_specs=[_SEM] * (2 * n_start) + [_HBM] * n_buf + [pl.BlockSpec(memory_space=pltpu.VMEM)],
        input_output_aliases={i: 2 * n_start + i for i in range(n_buf)},
        compiler_params=pltpu.CompilerParams(has_side_effects=_DATAFLOW))(*hbm_bufs, *wait_sems, *deps)
    sems, thru, token = outs[:2 * n_start], outs[2 * n_start:-1], outs[-1]
    through = lambda plan: [tuple(thru[s] for s in item) for item in plan]
    waited = [through(plan) for _, plan in wait_plan]
    handles = [(through(plan), sems[2 * si], sems[2 * si + 1]) for si, (_, plan) in enumerate(start_plan)]
    return waited, handles, token


def gather_small(name, a, deps=()):
    def body(a_ref, *rest):
        o_ref, send_sems, recv_sems, local_sem = rest[len(deps):]
        me = _mesh_pos()[3]
        own = pltpu.make_async_copy(a_ref, o_ref.at[me], local_sem)
        own.start()
        copies = _copies("gather", [(a_ref, o_ref)], send_sems, recv_sems)
        for cp in copies:
            cp.start()
        for cp in copies:
            cp.wait_recv()
        for cp in copies:
            cp.wait_send()
        own.wait()

    return pl.pallas_call(
        body, name=name, in_specs=[_ANY] * (1 + len(deps)), out_specs=_ANY,
        out_shape=jax.ShapeDtypeStruct((N_DEV,) + a.shape, a.dtype),
        scratch_shapes=[pltpu.SemaphoreType.DMA((N_DEV - 1,)), pltpu.SemaphoreType.DMA((N_DEV - 1,)),
                        pltpu.SemaphoreType.DMA],
        compiler_params=pltpu.CompilerParams(has_side_effects=True))(a, *deps)


def _adamw_math(w, g, m, v):
    m2 = ADAM_B1 * m + (1.0 - ADAM_B1) * g
    v2 = ADAM_B2 * v + (1.0 - ADAM_B2) * (g * g)
    m_hat = m2 / (1.0 - ADAM_B1 ** ADAM_STEP)
    v_hat = v2 / (1.0 - ADAM_B2 ** ADAM_STEP)
    delta = -ADAM_LR * (m_hat / (jnp.sqrt(v_hat) + ADAM_EPS) + ADAM_WD * w)
    return delta, m2, v2


def chip_sum(name, a, half):
    _, r, c = a.shape
    tr = r
    chip = 2 * lax.axis_index("x") + lax.axis_index("y")
    where = jnp.stack([lax.axis_index("c"), chip ^ 1, chip ^ 2, chip ^ 3]).astype(jnp.int32)

    def body(where_ref, a_ref, h_ref, o_ref):
        del where_ref
        o_ref[...] = (a_ref[...].astype(F32) + h_ref[...].astype(F32)).astype(o_ref.dtype)

    blk = pl.BlockSpec((None, tr, c), lambda g, i, where: (where[1 + g], i, 0))
    grid_spec = pltpu.PrefetchScalarGridSpec(
        num_scalar_prefetch=1, grid=(N_CHIP - 1, r // tr),
        in_specs=[pl.BlockSpec((None, None, tr, c), lambda g, i, where: (where[1 + g], where[0], i, 0)), blk],
        out_specs=blk)
    return pl.pallas_call(
        body, name=name, grid_spec=grid_spec, out_shape=jax.ShapeDtypeStruct((N_CHIP, r, c), BF16),
        compiler_params=_params("parallel", "parallel"))(where, a.reshape(N_CHIP, 2, r, c), half)


def adamw(name, w, m, v, l, land, a, half, prev=None):
    L, r, c = w.shape
    cp = land.shape[2]
    tr = _pick(r, (256, 176, 128, 64, 32, 16, 8))

    def body(w_ref, m_ref, v_ref, land_ref, a_ref, half_ref, *rest):
        g_ref, d_ref, m2_ref, v2_ref = rest[-4:]
        chip = _mesh_pos()[3] // 2
        mine = a_ref[:, pl.ds(0, c)].astype(F32) + half_ref[:, pl.ds(0, c)].astype(F32)
        g = None
        for s in range(N_CHIP):
            part = jnp.where(chip == s, mine, land_ref[s, :, pl.ds(0, c)].astype(F32))
            g = part if g is None else g + part
        delta, m2, v2 = _adamw_math(w_ref[...], g, m_ref[...], v_ref[...])
        g_ref[...] = g
        d_ref[...] = delta
        m2_ref[...] = m2
        v2_ref[...] = v2

    blk = pl.BlockSpec((None, tr, c), lambda i: (l, i, 0))
    shape = jax.ShapeDtypeStruct((L, r, c), F32)
    extra = [] if prev is None else list(prev)
    return pl.pallas_call(
        body, name=name, grid=(r // tr,),
        in_specs=[blk, blk, blk, pl.BlockSpec((N_CHIP, tr, cp), lambda i: (0, i, 0)),
                  pl.BlockSpec((None, tr, cp), lambda i: (_mesh_pos()[3], i, 0)),
                  pl.BlockSpec((None, tr, cp), lambda i: (_mesh_pos()[3] // 2, i, 0))] + [_ANY] * len(extra),
        out_specs=[blk] * 4, out_shape=[shape] * 4,
        input_output_aliases={6 + k: k for k in range(len(extra))},
        compiler_params=_params("parallel"))(w, m, v, land, a, half, *extra)


def adamw_columns(name, w, m, v, land, a, half):
    r, _, D = w.shape
    tc = _pick(D, (256, 128))

    def body(w_ref, m_ref, v_ref, land_ref, a_ref, half_ref, g_ref, d_ref, m2_ref, v2_ref):
        chip = _mesh_pos()[3] // 2
        mine = a_ref[...].astype(F32) + half_ref[...].astype(F32)
        g = None
        for s in range(N_CHIP):
            part = jnp.where(chip == s, mine, land_ref[s].astype(F32))
            g = part if g is None else g + part
        flat = lambda ref: ref[...].reshape(r, tc)
        delta, m2, v2 = _adamw_math(flat(w_ref), g, flat(m_ref), flat(v_ref))
        for ref, val in ((g_ref, g), (d_ref, delta), (m2_ref, m2), (v2_ref, v2)):
            ref[...] = val.reshape(r, 1, tc)

    blk = pl.BlockSpec((r, 1, tc), lambda i: (0, 0, i))
    shape = jax.ShapeDtypeStruct((r, 1, D), F32)
    return pl.pallas_call(
        body, name=name, grid=(D // tc,),
        in_specs=[blk, blk, blk, pl.BlockSpec((N_CHIP, r, tc), lambda i: (0, 0, i)),
                  pl.BlockSpec((None, r, tc), lambda i: (_mesh_pos()[3], 0, i)),
                  pl.BlockSpec((None, r, tc), lambda i: (_mesh_pos()[3] // 2, 0, i))],
        out_specs=[blk] * 4, out_shape=[shape] * 4,
        compiler_params=_params("parallel"))(w, m, v, land, a, half)


def adamw_small(name, w, m, v, parts):
    n = w.shape[1]

    def body(w_ref, m_ref, v_ref, p_ref, g_ref, d_ref, m2_ref, v2_ref):
        g = p_ref[0:1, :]
        for s in range(1, N_DEV):
            g = g + p_ref[s:s + 1, :]
        delta, m2, v2 = _adamw_math(w_ref[...], g, m_ref[...], v_ref[...])
        g_ref[...] = g
        d_ref[...] = delta
        m2_ref[...] = m2
        v2_ref[...] = v2

    shape = jax.ShapeDtypeStruct((1, n), F32)
    return pl.pallas_call(body, name=name, out_shape=[shape] * 4,
                          compiler_params=pltpu.CompilerParams(vmem_limit_bytes=VMEM_LIMIT_BYTES))(w, m, v, parts)


def _rope_tables(positions):
    half = RET_DK // 2
    inv_freq = 1.0 / jnp.power(RET_THETA_BASE, jnp.linspace(0.0, 1.0, half, dtype=F32))
    ang = positions.astype(F32)[:, None] * inv_freq
    cos, sin = jnp.cos(ang), jnp.sin(ang)
    cosf = jnp.repeat(cos, 2, axis=-1)
    sins = jnp.stack([-sin, sin], axis=-1).reshape(cosf.shape)
    return cosf, sins


def _pad_to(a, axis, size):
    pad = [(0, 0)] * a.ndim
    pad[axis] = (0, size - a.shape[axis])
    return jnp.pad(a, pad)


def _round_up(n, m):
    return -(-n // m) * m


def kernel(x, p, positions, attn_norm_w, ffn_norm_w, ple_norm_w, final_norm_w, ab_w_in, ab_gla_gate_up, ab_gla_gate_b, ab_ret_norm_w, ab_gla_norm_w, ab_w_out, c_w_qkv, c_w_out, ffn_w_gate, ffn_w_up, ffn_w_down, ple_w_proj, ple_w_gate, loss_target, m_attn_norm_w, m_ffn_norm_w, m_ple_norm_w, m_final_norm_w, m_ab_w_in, m_ab_gla_gate_up, m_ab_gla_gate_b, m_ab_ret_norm_w, m_ab_gla_norm_w, m_ab_w_out, m_c_w_qkv, m_c_w_out, m_ffn_w_gate, m_ffn_w_up, m_ffn_w_down, m_ple_w_proj, m_ple_w_gate, v_attn_norm_w, v_ffn_norm_w, v_ple_norm_w, v_final_norm_w, v_ab_w_in, v_ab_gla_gate_up, v_ab_gla_gate_b, v_ab_ret_norm_w, v_ab_gla_norm_w, v_ab_w_out, v_c_w_qkv, v_c_w_out, v_ffn_w_gate, v_ffn_w_up, v_ffn_w_down, v_ple_w_proj, v_ple_w_gate):
    T, D = x.shape[1], x.shape[2]
    depth = attn_norm_w.shape[0]
    assert ab_w_in.shape[0] == 1 and c_w_qkv.shape[0] == 1 and depth == 2, "one even and one odd layer"
    me = 4 * lax.axis_index("x") + 2 * lax.axis_index("y") + lax.axis_index("c")
    in_shard = ab_w_in.shape[2]
    in_width = in_shard * N_DEV
    assert in_width == OFF_LR + GLA_GATE_RANK
    fs = ffn_w_gate.shape[2]
    fp = _round_up(fs, LANE)
    gu_cols = ab_gla_gate_up.shape[2]

    bf = lambda a: a.astype(BF16)
    tr_ = lambda a: jnp.swapaxes(a, -1, -2)
    wg_t, wu_t = tr_(ffn_w_gate), tr_(ffn_w_up)
    srcs = {"w_in": bf(tr_(ab_w_in[0]))}
    group_keys = [["w_in"], ["gu", "w_oab"], ["wg0", "wu0"], ["wd0", "wpg0", "wpp0"], ["w_qkv", "w_oc"],
                  ["wg1", "wu1"], ["wd1", "wpg1", "wpp1"]]
    G_IN, G_OUT, G_QKV = 0, 1, 4
    g_ffn = lambda layer: (2, 3) if layer == 0 else (5, 6)

    def landing(key):
        a = srcs[key]
        rows = fp if key[:2] in ("wg", "wu", "wd") else a.shape[0]
        buf = lax.empty((N_DEV, rows) + a.shape[1:], a.dtype)
        if rows > a.shape[0]:
            zeros = jnp.zeros((N_DEV, rows - a.shape[0]) + a.shape[1:], a.dtype)
            buf = lax.dynamic_update_slice(buf, zeros, (0, a.shape[0]) + (0,) * (a.ndim - 1))
        return lax.dynamic_update_slice(buf, a[None], (me,) + (0,) * a.ndim)

    _, chip_handles, gather_token = exchange_call(
        "gather_start_in", [], [("to_chips", [(srcs[k], landing(k)) for k in group_keys[G_IN]])])
    (gather_token, w_out_, gu_, w_qkv_, w_oc_, wg_, wu_, wd_, wpg_, wpp_) = lax.optimization_barrier(
        (gather_token, ab_w_out, ab_gla_gate_up, c_w_qkv, c_w_out, wg_t, wu_t, ffn_w_down, ple_w_gate, ple_w_proj))
    srcs.update(w_oab=bf(w_out_[0]), gu=gu_[0], w_qkv=bf(w_qkv_[0]), w_oc=bf(w_oc_[0]))
    for l in range(depth):
        srcs[f"wg{l}"] = bf(wg_[l])
        srcs[f"wu{l}"] = bf(wu_[l])
        srcs[f"wd{l}"] = bf(wd_[l])
        srcs[f"wpg{l}"] = bf(wpg_[l])
        srcs[f"wpp{l}"] = bf(wpp_[l])
    _, more, gather_token = exchange_call(
        "gather_start", [], [("to_chips", [(srcs[k], landing(k)) for k in keys]) for keys in group_keys[1:]],
        deps=(gather_token,))
    chip_handles = chip_handles + more
    weights = {}

    def gather_wait(gi, dep):
        lands = [(land,) for _, land in chip_handles[gi][0]]
        _, (passing,), _ = exchange_call(
            f"gather{gi}_pass", [("to_chips", chip_handles[gi])], [("pass_on", lands)], deps=(dep,))
        (complete,), _, _ = exchange_call(f"gather{gi}_done", [("pass_on", passing)], [])
        weights.update(zip(group_keys[gi], [land for (land,) in complete]))

    gb = ab_gla_gate_b
    hn_w = jnp.concatenate([ab_ret_norm_w, ab_gla_norm_w], axis=1)
    cosf, sins = _rope_tables(positions[0])
    p_bf = bf(p[:, 0])

    xs = x[0]
    saved = []
    for i in range(depth):
        nm = f"l{i}_"
        w_attn, w_ffn, w_ple = attn_norm_w[i:i + 1], ffn_norm_w[i:i + 1], ple_norm_w[i:i + 1]
        (xn,) = rowwise(nm + "norm_attn", lambda a, w: (_rms(a, w),), T, [("row", xs), ("full", w_attn)],
                        [("row", D, BF16)], deps=(gather_token,) if i == 0 else ())
        if i % 2 == 0:
            gather_wait(G_IN, xn)
            w_in_t = weights["w_in"].reshape(1, 1, in_width, D)
            w_lr_t = _pad_to(w_in_t[0, 0, OFF_LR:], 0, LANE).reshape(1, 1, LANE, D)
            z = mmt_fwd(nm + "mm_in", xn, w_in_t, 0, F32, n=OFF_LR)
            glr = mmt_fwd(nm + "mm_lr", xn, w_lr_t, 0, F32)
            oraw = retention_fwd(nm + "ret_fwd", z, cosf, sins, RET_V + GLA_V)
            gather_wait(G_OUT, oraw)
            w_oab = weights["w_oab"].reshape(1, 1, D, D)
            gu_full = _pad_to(weights["gu"].transpose(1, 0, 2).reshape(GLA_GATE_RANK, GLA_QK), 0, LANE)
            oraw, gla_states = gla_fwd(nm + "gla_fwd", z, glr, gu_full, gb, oraw)
            o = headnorm_fwd(nm + "headnorm_fwd", oraw, z, hn_w)
            h1, hn = mm_add_norm(nm + "mm_out", o, w_oab, xs, w_ffn)
            mixer_saved = (z, glr, oraw, o, gla_states)
        else:
            gather_wait(G_QKV, xn)
            w_qkv = weights["w_qkv"].reshape((1,) + weights["w_qkv"].shape)
            w_oc = weights["w_oc"].reshape(1, 1, D, D)
            qkv = mm_nn(nm + "mm_qkv", xn, w_qkv, 0, BF16)
            o, lse = attn_fwd(nm + "attn_fwd", qkv)
            h1, hn = mm_add_norm(nm + "mm_out", o, w_oc, xs, w_ffn)
            mixer_saved = (qkv, o, lse)
        gather_wait(g_ffn(i)[0], hn)
        wg = weights[f"wg{i}"].reshape(1, N_DEV, fp, D)
        wu = weights[f"wu{i}"].reshape(1, N_DEV, fp, D)
        g, u, act = ffn_gate_up(nm + "ffn_gate_up", hn, wg, wu)
        gather_wait(g_ffn(i)[1], act)
        wd = weights[f"wd{i}"].reshape(1, 1, N_DEV * fp, D)
        wpg = weights[f"wpg{i}"].reshape(1, 1, D, D)
        wpp = weights[f"wpp{i}"].reshape((1,) + weights[f"wpp{i}"].shape)
        h2, pn = mm_add_norm(nm + "mm_down", act, wd, h1, w_ple)
        x_next, s, e = ple_fwd(nm + "ple", pn, wpg, p_bf[i], wpp, h2)
        mixer_w = (w_in_t, w_lr_t, w_oab, gu_full) if i % 2 == 0 else (w_qkv, w_oc)
        saved.append((xs, xn, mixer_saved, mixer_w, (wg, wu, wd, wpg), h1, hn, g, u, act, h2, pn, s, e))
        xs = x_next

    def loss_fn(a, w, t):
        diff = _rms(a, w) - t
        dx, dw = _rms_bwd(a, w, diff * (1.0 / D))
        part = 0.5 * jnp.sum(jnp.mean(diff * diff, axis=-1, keepdims=True), axis=0, keepdims=True)
        return dx, dw, jnp.broadcast_to(part, (1, LANE))

    dx, d_final_w, loss_part = rowwise("loss_head", loss_fn, T,
                                       [("row", xs), ("full", final_norm_w[None, :]), ("row", loss_target[0])],
                                       [("row", D, F32), ("acc", D), ("acc", LANE)])
    loss = lax.psum(loss_part[0, 0], ("x", "y", "c"))

    grads = {}
    on_chip = []
    scatters = []

    def scatter_start(name, keys, deps=()):
        waits = [("halves", on_chip[0][1])] if on_chip else []
        starts = [("halves", [(grads[k], lax.empty((N_CHIP,) + grads[k].shape[1:], BF16)) for k in keys])] if keys else []
        waited, handles, token = exchange_call(name, waits, starts, deps=deps)
        if on_chip:
            done_keys, _ = on_chip.pop()
            sums = [chip_sum(f"{name}_sum{j}", a, half) for j, (a, half) in enumerate(waited[0])]
            _, (handle,), token = exchange_call(
                name + "_chips", [], [("chip_sums", [(cs, lax.empty(cs.shape, BF16)) for cs in sums])])
            scatters.append((done_keys, handle, waited[0]))
        if keys:
            on_chip.append((keys, handles[0]))
        return token

    d_attn_w, d_ffn_w, d_ple_w = [None] * depth, [None] * depth, [None] * depth
    for i in reversed(range(depth)):
        nm = f"l{i}_b_"
        xs_i, xn, mixer_saved, mixer_w, (wg, wu, wd, wpg), h1, hn, g, u, act, h2, pn, s, e = saved[i]
        w_attn, w_ffn, w_ple = attn_norm_w[i:i + 1], ffn_norm_w[i:i + 1], ple_norm_w[i:i + 1]

        def ple_bwd(d, sv, ev):
            gate = _sigmoid(sv)
            return d * gate, d * ev * gate * (1.0 - gate)

        de, ds = rowwise(nm + "ple_out", ple_bwd, T, [("row", dx), ("row", s), ("row", e)],
                         [("row", D, BF16), ("row", D, BF16)], deps=(loss.reshape(1, 1),) if i == depth - 1 else ())
        grads[("ple_w_proj", i)] = mm_tn(nm + "mm_ple_proj_w", p_bf[i], de, N_DEV, BF16)
        grads[("ple_w_gate", i)] = mm_tn(nm + "mm_ple_gate_w", pn, ds, 1, BF16).reshape(N_DEV, D // N_DEV, D)
        dpn = mm_nt(nm + "mm_ple_gate_x", ds, wpg, 0, F32)

        def norm_bwd_add(a, w, dn, dres):
            dxx, dw = _rms_bwd(a, w, dn)
            tot = dres + dxx
            return tot, tot, dw

        dh2, dh2_bf, d_ple_w[i] = rowwise(nm + "norm_ple", norm_bwd_add, T,
                                          [("row", h2), ("full", w_ple), ("row", dpn), ("row", dx)],
                                          [("row", D, F32), ("row", D, BF16), ("acc", D)])
        grads[("ffn_w_down", i)] = mm_tn(nm + "mm_down_w", act, dh2_bf, 1, BF16).reshape(N_DEV, fp, D)
        token = scatter_start(nm + "scatter_ple_down", [("ple_w_proj", i), ("ple_w_gate", i), ("ffn_w_down", i)])
        dg, du = ffn_down_bwd(nm + "ffn_down_x", dh2_bf, wd, g, u, deps=(token,))
        grads[("ffn_w_gate", i)] = mmt_dw(nm + "mm_gate_w", dg, hn, N_DEV, BF16)
        grads[("ffn_w_up", i)] = mmt_dw(nm + "mm_up_w", du, hn, N_DEV, BF16)
        token = scatter_start(nm + "scatter_gate_up", [("ffn_w_gate", i), ("ffn_w_up", i)])
        dhn = mmt_dx_pair(nm + "mm_gate_up_x", dg, wg, du, wu, F32, deps=(token,))
        dh1, dh1_bf, d_ffn_w[i] = rowwise(nm + "norm_ffn", norm_bwd_add, T,
                                          [("row", h1), ("full", w_ffn), ("row", dhn), ("row", dh2)],
                                          [("row", D, F32), ("row", D, BF16), ("acc", D)])
        if i % 2 == 0:
            z, glr, oraw, o, gla_states = mixer_saved
            w_in_t, w_lr_t, w_oab, gu_full = mixer_w
            grads[("ab_w_out", 0)] = mm_tn(nm + "mm_out_w", o, dh1_bf, 1, BF16).reshape(N_DEV, D // N_DEV, D)
            token = scatter_start(nm + "scatter_out", [("ab_w_out", 0)])
            do = mm_nt(nm + "mm_out_x", dh1_bf, w_oab, 0, F32, deps=(token,))
            d_oraw, d_gates, d_hn_w = headnorm_bwd(nm + "headnorm", oraw, z, hn_w, do)
            d_rq, d_rk, d_rv = retention_bwd(nm + "ret", z, cosf, sins, d_oraw)
            d_gq, d_gk, d_gv, d_glr4, d_gu, d_gb = gla_bwd(nm + "gla", z, glr, gu_full, gb, d_oraw, gla_states)
            dz = jnp.concatenate([d_rq, d_rk, d_rv, d_gates[:, :RET_V], d_gq, d_gk, d_gv, d_gates[:, RET_V:]], axis=1)
            (d_glr,) = rowwise(nm + "sum_lr", lambda *a: (a[0] + a[1] + a[2] + a[3],), T,
                               [("row", d_glr4[hh]) for hh in range(GLA_HEADS)], [("row", LANE, BF16)])
            dwt_in = mmt_dw(nm + "mm_in_w", dz, xn, 1, BF16, rows=in_width)
            dwt_in = mmt_dw_rows(nm + "mm_lr_w", d_glr, xn, dwt_in, OFF_LR, GLA_GATE_RANK)
            grads[("ab_w_in", 0)] = dwt_in.reshape(N_DEV, in_shard, D)
            token = scatter_start(nm + "scatter_in", [("ab_w_in", 0)])
            dxn_a = mmt_dx_wide(nm + "mm_in_x", dz, w_in_t, F32, n=OFF_LR, deps=(token,))
            token = scatter_start(nm + "scatter_in_on", [], deps=(dxn_a,))
            dxn_b = mmt_dx(nm + "mm_lr_x", d_glr, w_lr_t, 0, F32, deps=(token,))
        else:
            qkv, o, lse = mixer_saved
            w_qkv, w_oc = mixer_w
            grads[("c_w_out", 0)] = mm_tn(nm + "mm_out_w", o, dh1_bf, 1, BF16).reshape(N_DEV, D // N_DEV, D)
            do = mm_nt(nm + "mm_out_x", dh1_bf, w_oc, 0, BF16)
            dq, dk, dv = attn_bwd(nm + "attn", qkv, o, lse, do)
            dqkv = jnp.concatenate([dq, dk, dv], axis=1)
            grads[("c_w_qkv", 0)] = mm_tn(nm + "mm_qkv_w", xn, dqkv, N_DEV, BF16)
            token = scatter_start(nm + "scatter_attn", [("c_w_out", 0), ("c_w_qkv", 0)])
            dxn_a = mm_nt_wide(nm + "mm_qkv_x", dqkv, w_qkv, F32, deps=(token,))
            dxn_b = None
        dxn = [dxn_a] if dxn_b is None else [dxn_a, dxn_b]

        def norm_bwd_in(a, w, *rest):
            dxx, dw = _rms_bwd(a, w, sum(rest[1:-1], rest[0]))
            return rest[-1] + dxx, dw

        dx, d_attn_w[i] = rowwise(nm + "norm_attn", norm_bwd_in, T,
                                  [("row", xs_i), ("full", w_attn)] + [("row", d) for d in dxn] + [("row", dh1)],
                                  [("row", D, F32), ("acc", D)])

    small_names = ["attn_norm_w", "ffn_norm_w", "ple_norm_w", "final_norm_w", "ab_gla_gate_b", "ab_ret_norm_w",
                   "ab_gla_norm_w"]
    small_grads = [jnp.concatenate(d_attn_w, 0), jnp.concatenate(d_ffn_w, 0), jnp.concatenate(d_ple_w, 0), d_final_w[0],
                   d_gb, d_hn_w[:, :RET_V], d_hn_w[:, RET_V:]]
    small_w = [attn_norm_w, ffn_norm_w, ple_norm_w, final_norm_w, ab_gla_gate_b, ab_ret_norm_w, ab_gla_norm_w]
    small_m = [m_attn_norm_w, m_ffn_norm_w, m_ple_norm_w, m_final_norm_w, m_ab_gla_gate_b, m_ab_ret_norm_w, m_ab_gla_norm_w]
    small_v = [v_attn_norm_w, v_ffn_norm_w, v_ple_norm_w, v_final_norm_w, v_ab_gla_gate_b, v_ab_ret_norm_w, v_ab_gla_norm_w]
    sizes = [int(np.prod(a.shape)) for a in small_w]
    n_gu = GLA_GATE_RANK * GLA_QK
    n_small = _round_up(sum(sizes) + n_gu, LANE)
    pack = lambda parts: _pad_to(jnp.concatenate([a.reshape(-1) for a in parts]), 0, n_small)[None, :]
    small_part = pack(small_grads + [d_gu[:GLA_GATE_RANK]])

    cols_first = lambda a: jnp.transpose(a, (2, 0, 1))
    big_w = dict(ab_w_in=tuple(cols_first(a) for a in (ab_w_in, m_ab_w_in, v_ab_w_in)),
                 ab_w_out=(ab_w_out, m_ab_w_out, v_ab_w_out),
                 c_w_qkv=(c_w_qkv, m_c_w_qkv, v_c_w_qkv), c_w_out=(c_w_out, m_c_w_out, v_c_w_out),
                 ffn_w_gate=(wg_t, tr_(m_ffn_w_gate), tr_(v_ffn_w_gate)),
                 ffn_w_up=(wu_t, tr_(m_ffn_w_up), tr_(v_ffn_w_up)),
                 ffn_w_down=(ffn_w_down, m_ffn_w_down, v_ffn_w_down), ple_w_proj=(ple_w_proj, m_ple_w_proj, v_ple_w_proj),
                 ple_w_gate=(ple_w_gate, m_ple_w_gate, v_ple_w_gate))
    if on_chip:
        scatter_start("scatter_last", [], deps=(dx,))
    results, last = {}, dx
    for gi, (keys, handle, partials) in enumerate(scatters):
        (arrived,), _, _ = exchange_call(f"scatter_wait{gi}", [("chip_sums", handle)], [], deps=(last,))
        for (n, l), (_, land), (a, half) in zip(keys, arrived, partials):
            if n == "ab_w_in":
                results[n] = adamw_columns(f"adamw_{n}", *big_w[n], land, a, half)
            else:
                results[n] = adamw(f"adamw_{n}{l}", *big_w[n], l, land, a, half, prev=results.get(n))
            last = results[n][0]
    for n in ("ffn_w_gate", "ffn_w_up"):
        results[n] = [tr_(a) for a in results[n]]
    results["ab_w_in"] = [jnp.transpose(a, (1, 2, 0)) for a in results["ab_w_in"]]
    small_parts = gather_small("gather_small", small_part, deps=(last,)).reshape(N_DEV, n_small)

    gu_off = sum(sizes)
    own_cols = lambda a: lax.dynamic_slice_in_dim(a.reshape(GLA_GATE_RANK, GLA_QK), me * gu_cols, gu_cols, axis=1)
    small_res = adamw_small("adamw_small", pack(small_w + [jnp.zeros((n_gu,), F32)]),
                            pack(small_m + [jnp.zeros((n_gu,), F32)]), pack(small_v + [jnp.ones((n_gu,), F32)]),
                            small_parts)
    g_gu_full = small_res[0][0, gu_off:gu_off + n_gu]
    g_gu = own_cols(g_gu_full)[None]
    gu_res = adamw_small("adamw_gate_up", *[_pad_to(a.reshape(1, -1), 1, _round_up(a.size, LANE)) for a in
                                            (ab_gla_gate_up, m_ab_gla_gate_up, v_ab_gla_gate_up)],
                         jnp.concatenate([_pad_to(g_gu.reshape(1, -1), 1, _round_up(g_gu.size, LANE)),
                                          jnp.zeros((N_DEV - 1, _round_up(g_gu.size, LANE)), F32)], axis=0))
    for k in range(4):
        off = 0
        for n, a, sz in zip(small_names, small_w, sizes):
            results.setdefault(n, [None] * 4)[k] = small_res[k][0, off:off + sz].reshape(a.shape)
            off += sz
        results.setdefault("ab_gla_gate_up", [None] * 4)[k] = gu_res[k][0, :g_gu.size].reshape(ab_gla_gate_up.shape)

    order = ["attn_norm_w", "ffn_norm_w", "ple_norm_w", "final_norm_w", "ab_w_in", "ab_gla_gate_up", "ab_gla_gate_b",
             "ab_ret_norm_w", "ab_gla_norm_w", "ab_w_out", "c_w_qkv", "c_w_out", "ffn_w_gate", "ffn_w_up", "ffn_w_down",
             "ple_w_proj", "ple_w_gate"]
    return (loss, dx[None], *[results[n][0] for n in order], *[results[n][1] for n in order],
            *[results[n][2] for n in order], *[results[n][3] for n in order])
```

```python
import math

import numpy as np
import jax
import jax.numpy as jnp
from jax import lax
from jax.experimental import pallas as pl
from jax.experimental.pallas import tpu as pltpu

F32 = jnp.float32
BF16 = jnp.bfloat16
HIGHEST = lax.Precision.HIGHEST

N_DEV = 8
VMEM_LIMIT_BYTES = 48 * 1024 * 1024
LANE = 128
NORM_EPS = 1e-6

RET_HEADS, RET_DK, RET_DV = 4, 256, 256
RET_THETA_BASE = 10000.0
GLA_HEADS, GLA_DK, GLA_DV = 4, 128, 256
GLA_GATE_RANK = 16
GLA_GATE_NORM = 16.0
CHUNK = 64
ATT_HEADS = 16
DILATED_BRANCHES = ((128, 1), (512, 4), (2048, 16))
BLK = 256

ADAM_LR, ADAM_B1, ADAM_B2, ADAM_EPS, ADAM_WD, ADAM_STEP = 0.001, 0.9, 0.999, 1e-08, 0.01, 10

RET_QK = RET_HEADS * RET_DK
RET_V = RET_HEADS * RET_DV
GLA_QK = GLA_HEADS * GLA_DK
GLA_V = GLA_HEADS * GLA_DV
OFF_RQ, OFF_RK, OFF_RV, OFF_RG = 0, RET_QK, 2 * RET_QK, 2 * RET_QK + RET_V
OFF_GQ = OFF_RG + RET_V
OFF_GK = OFF_GQ + GLA_QK
OFF_GV = OFF_GK + GLA_QK
OFF_GG = OFF_GV + GLA_V
OFF_LR = OFF_GG + GLA_V


def _params(*sem):
    return pltpu.CompilerParams(dimension_semantics=sem or None, vmem_limit_bytes=VMEM_LIMIT_BYTES)


def _pick(n, cands):
    for c in cands:
        if n % c == 0:
            return c
    raise ValueError(f"no tile for {n} in {cands}")


_NN = (((1,), (0,)), ((), ()))
_NT = (((1,), (1,)), ((), ()))
_TN = (((0,), (0,)), ((), ()))
_ANY = pl.BlockSpec(memory_space=pl.ANY)
MAX_CONTRACT = 2048
_TILES = (1024, 768, 512, 256, 128)


def _mm_call(name, dims, grid, in_specs, out_spec, out_shape, args, deps=()):
    steps = grid[2]
    assert steps == 1 or out_shape.dtype == F32

    def body(a_ref, b_ref, *rest):
        o_ref = rest[len(deps)]
        part = lax.dot_general(a_ref[...].astype(BF16), b_ref[...].astype(BF16), dims, preferred_element_type=F32)
        if steps == 1:
            o_ref[...] = part.astype(o_ref.dtype)
        else:
            _accumulate(o_ref, part, pl.program_id(2) == 0)

    return pl.pallas_call(
        body, name=name, grid=grid, in_specs=list(in_specs) + [_ANY] * len(deps), out_specs=out_spec,
        out_shape=out_shape, compiler_params=_params("parallel", "parallel", "arbitrary"))(*args, *deps)


def mm_nn(name, a, w, l, out_dtype, deps=()):
    _, J, K, n = w.shape
    M = a.shape[0]
    tm, tn, tk = _pick(M, _TILES), _pick(n, _TILES), _pick(K, (MAX_CONTRACT,) + _TILES)
    nt = n // tn
    return _mm_call(
        name, _NN, (M // tm, J * nt, K // tk),
        [pl.BlockSpec((tm, tk), lambda i, j, k: (i, k)),
         pl.BlockSpec((None, None, tk, tn), lambda i, j, k: (l, j // nt, k, j % nt))],
        pl.BlockSpec((tm, tn), lambda i, j, k: (i, j)),
        jax.ShapeDtypeStruct((M, J * n), out_dtype), (a, w), deps)


def mm_nt(name, a, w, l, out_dtype, deps=()):
    _, J, K, n = w.shape
    M = a.shape[0]
    tm, tq, tc = _pick(M, _TILES), _pick(K, _TILES), _pick(n, (MAX_CONTRACT,) + _TILES)
    nc = n // tc
    return _mm_call(
        name, _NT, (M // tm, K // tq, J * nc),
        [pl.BlockSpec((tm, tc), lambda i, q, c: (i, c)),
         pl.BlockSpec((None, None, tq, tc), lambda i, q, c: (l, c // nc, q, c % nc))],
        pl.BlockSpec((tm, tq), lambda i, q, c: (i, q)),
        jax.ShapeDtypeStruct((M, K), out_dtype), (a, w), deps)


def mm_tn(name, x, dy, J, out_dtype, deps=()):
    M, K = x.shape
    n = dy.shape[1] // J
    tp, tn = _pick(K, _TILES), _pick(n, _TILES)
    nt = n // tn
    assert M <= MAX_CONTRACT
    return _mm_call(
        name, _TN, (K // tp, J * nt, 1),
        [pl.BlockSpec((M, tp), lambda i, j, r: (0, i)),
         pl.BlockSpec((M, tn), lambda i, j, r: (0, j))],
        pl.BlockSpec((None, tp, tn), lambda i, j, r: (j // nt, i, j % nt)),
        jax.ShapeDtypeStruct((J, K, n), out_dtype), (x, dy), deps)


def mmt_fwd(name, a, wt, l, out_dtype, n=None, deps=()):
    _, J, rows, K = wt.shape
    n = rows if n is None else n
    M = a.shape[0]
    tm, tn = _pick(M, _TILES), _pick(n, _TILES)
    nt = n // tn
    assert K <= MAX_CONTRACT
    return _mm_call(
        name, _NT, (M // tm, J * nt, 1),
        [pl.BlockSpec((tm, K), lambda i, j, k: (i, 0)),
         pl.BlockSpec((None, None, tn, K), lambda i, j, k: (l, j // nt, j % nt, 0))],
        pl.BlockSpec((tm, tn), lambda i, j, k: (i, j)),
        jax.ShapeDtypeStruct((M, J * n), out_dtype), (a, wt), deps)


def mmt_dx(name, dy, wt, l, out_dtype, n=None, deps=()):
    _, J, rows, K = wt.shape
    n = rows if n is None else n
    M = dy.shape[0]
    tm, tq, tc = _pick(M, _TILES), _pick(K, _TILES), _pick(n, _TILES)
    nc = n // tc
    return _mm_call(
        name, _NN, (M // tm, K // tq, J * nc),
        [pl.BlockSpec((tm, tc), lambda i, q, c: (i, c)),
         pl.BlockSpec((None, None, tc, tq), lambda i, q, c: (l, c // nc, c % nc, q))],
        pl.BlockSpec((tm, tq), lambda i, q, c: (i, q)),
        jax.ShapeDtypeStruct((M, K), out_dtype), (dy, wt), deps)


WIDE_TILE = 512


def _wide_call(name, body, M, K, a, w, a_spec, w_spec, out_dtype, deps):
    def kernel_body(a_ref, w_ref, *rest):
        o_ref = rest[len(deps)]
        o_ref[...] = body(a_ref, w_ref).astype(o_ref.dtype)

    return pl.pallas_call(
        kernel_body, name=name, grid=(M // WIDE_TILE, K // WIDE_TILE),
        in_specs=[a_spec, w_spec] + [_ANY] * len(deps),
        out_specs=pl.BlockSpec((WIDE_TILE, WIDE_TILE), lambda i, q: (i, q)),
        out_shape=jax.ShapeDtypeStruct((M, K), out_dtype),
        compiler_params=_params("parallel", "parallel"))(a, w, *deps)


def mmt_dx_wide(name, dy, wt, out_dtype, n=None, deps=()):
    _, J, rows, K = wt.shape
    n = rows if n is None else n
    M = dy.shape[0]

    def body(dy_ref, w_ref):
        return jnp.dot(dy_ref[...].astype(BF16), w_ref[...].reshape(J * n, WIDE_TILE), preferred_element_type=F32)

    return _wide_call(name, body, M, K, dy, wt,
                      pl.BlockSpec((WIDE_TILE, J * n), lambda i, q: (i, 0)),
                      pl.BlockSpec((None, J, n, WIDE_TILE), lambda i, q: (0, 0, 0, q)), out_dtype, deps)


def mmt_dx_pair(name, dy1, wt1, dy2, wt2, out_dtype, deps=()):
    _, J, n, K = wt1.shape
    M = dy1.shape[0]

    def body(dy1_ref, w1_ref, dy2_ref, w2_ref, *rest):
        o_ref = rest[len(deps)]
        acc = jnp.dot(dy1_ref[...], w1_ref[...].reshape(J * n, WIDE_TILE), preferred_element_type=F32)
        acc = acc + jnp.dot(dy2_ref[...], w2_ref[...].reshape(J * n, WIDE_TILE), preferred_element_type=F32)
        o_ref[...] = acc.astype(o_ref.dtype)

    rows = _once((WIDE_TILE, J * n), lambda i, q: (i, 0))
    cols = pl.BlockSpec((None, J, n, WIDE_TILE), lambda i, q: (0, 0, 0, q))
    return pl.pallas_call(
        body, name=name, grid=(M // WIDE_TILE, K // WIDE_TILE),
        in_specs=[rows, cols, rows, cols] + [_ANY] * len(deps),
        out_specs=pl.BlockSpec((WIDE_TILE, WIDE_TILE), lambda i, q: (i, q)),
        out_shape=jax.ShapeDtypeStruct((M, K), out_dtype),
        compiler_params=_params("parallel", "parallel"))(dy1, wt1, dy2, wt2, *deps)


def mm_nt_wide(name, a, w, out_dtype, deps=()):
    _, J, K, n = w.shape
    M = a.shape[0]

    def body(a_ref, w_ref):
        acc = None
        for j in range(J):
            part = lax.dot_general(a_ref[:, j * n:(j + 1) * n].astype(BF16), w_ref[j], _NT, preferred_element_type=F32)
            acc = part if acc is None else acc + part
        return acc

    return _wide_call(name, body, M, K, a, w,
                      pl.BlockSpec((WIDE_TILE, J * n), lambda i, q: (i, 0)),
                      pl.BlockSpec((None, J, WIDE_TILE, n), lambda i, q: (0, 0, q, 0)), out_dtype, deps)


def mmt_dw(name, dy, x, J, out_dtype, deps=(), rows=None):
    M, K = x.shape
    n = dy.shape[1] // J
    tn, tp = _pick(n, _TILES), _pick(K, _TILES)
    nt = n // tn
    assert M <= MAX_CONTRACT
    return _mm_call(
        name, _TN, (J * nt, K // tp, 1),
        [pl.BlockSpec((M, tn), lambda j, i, r: (0, j)),
         pl.BlockSpec((M, tp), lambda j, i, r: (0, i))],
        pl.BlockSpec((None, tn, tp), lambda j, i, r: (j // nt, j % nt, i)),
        jax.ShapeDtypeStruct((J, n if rows is None else rows, K), out_dtype), (dy, x), deps)


def mmt_dw_rows(name, dy, x, out, row0, rank):
    M, K = x.shape
    tp = _pick(K, _TILES)

    def body(dy_ref, x_ref, prev_ref, o_ref):
        del prev_ref
        full = lax.dot_general(dy_ref[...], x_ref[...], _TN, preferred_element_type=F32)
        o_ref[...] = full[:rank].astype(o_ref.dtype)

    return pl.pallas_call(
        body, name=name, grid=(K // tp,),
        in_specs=[pl.BlockSpec((M, dy.shape[1]), lambda i: (0, 0)), pl.BlockSpec((M, tp), lambda i: (0, i)), _ANY],
        out_specs=pl.BlockSpec((None, rank, tp), lambda i: (0, row0 // rank, i)),
        out_shape=jax.ShapeDtypeStruct(out.shape, out.dtype), input_output_aliases={2: 0},
        compiler_params=_params("parallel"))(dy, x, out)


def ffn_gate_up(name, a, wg, wu):
    _, J, n, K = wg.shape
    M = a.shape[0]
    tm, tn = _pick(M, _TILES), _pick(n, _TILES)
    nt = n // tn
    assert K <= MAX_CONTRACT

    def body(a_ref, wg_ref, wu_ref, dup_ref, dgate_ref, act_ref):
        x = a_ref[...]
        g = lax.dot_general(x, wg_ref[...], _NT, preferred_element_type=F32)
        u = lax.dot_general(x, wu_ref[...], _NT, preferred_element_type=F32)
        silu, dsilu = _silu_and_grad(g)
        dup_ref[...] = silu.astype(dup_ref.dtype)
        dgate_ref[...] = (u * dsilu).astype(dgate_ref.dtype)
        act_ref[...] = (silu * u).astype(act_ref.dtype)

    w_spec = pl.BlockSpec((None, None, tn, K), lambda i, j: (0, j // nt, j % nt, 0))
    out = pl.BlockSpec((tm, tn), lambda i, j: (i, j))
    return pl.pallas_call(
        body, name=name, grid=(M // tm, J * nt),
        in_specs=[pl.BlockSpec((tm, K), lambda i, j: (i, 0)), w_spec, w_spec],
        out_specs=[out] * 3, out_shape=[jax.ShapeDtypeStruct((M, J * n), BF16)] * 3,
        compiler_params=_params("parallel", "parallel"))(a, wg, wu)


def mm_add_norm(name, a, w, res, norm_w):
    _, _, K, N = w.shape
    M = a.shape[0]
    tm, tk = _pick(M, (WIDE_TILE, 256)), _pick(K, (1024, 512, 256))
    steps = K // tk

    def body(a_ref, w_ref, res_ref, nw_ref, h_ref, hn_ref):
        k = pl.program_id(1)
        part = jnp.dot(a_ref[...], w_ref[...], preferred_element_type=F32)
        _accumulate(h_ref, part, k == 0)

        @pl.when(k == steps - 1)
        def _():
            h = h_ref[...] + res_ref[...]
            h_ref[...] = h
            hn_ref[...] = _rms(h, nw_ref[...]).astype(hn_ref.dtype)

    rows = pl.BlockSpec((tm, N), lambda i, k: (i, 0))
    return pl.pallas_call(
        body, name=name, grid=(M // tm, steps),
        in_specs=[pl.BlockSpec((tm, tk), lambda i, k: (i, k)),
                  pl.BlockSpec((None, None, tk, N), lambda i, k: (0, 0, k, 0)), rows,
                  pl.BlockSpec((1, N), lambda i, k: (0, 0))],
        out_specs=[rows, rows],
        out_shape=[jax.ShapeDtypeStruct((M, N), F32), jax.ShapeDtypeStruct((M, N), BF16)],
        compiler_params=_params("parallel", "arbitrary"))(a, w, res, norm_w)


def ple_fwd(name, pn, wpg, p_in, wpp, h):
    _, J, P, n = wpp.shape
    M, D = h.shape
    tm, tn = _pick(M, (WIDE_TILE, 256)), _pick(D, _TILES)
    per_tile = tn // n

    def body(pn_ref, wg_ref, p_ref, wp_ref, h_ref, x_ref, s_ref, e_ref):
        s = jnp.dot(pn_ref[...], wg_ref[...], preferred_element_type=F32)
        p_blk = p_ref[...]
        e = jnp.concatenate([jnp.dot(p_blk, wp_ref[j], preferred_element_type=F32) for j in range(per_tile)], axis=1)
        s_ref[...] = s
        e_ref[...] = e
        x_ref[...] = h_ref[...] + _sigmoid(s) * e

    tile = pl.BlockSpec((tm, tn), lambda i, j: (i, j))
    return pl.pallas_call(
        body, name=name, grid=(M // tm, D // tn),
        in_specs=[pl.BlockSpec((tm, D), lambda i, j: (i, 0)),
                  pl.BlockSpec((None, None, D, tn), lambda i, j: (0, 0, 0, j)),
                  pl.BlockSpec((tm, P), lambda i, j: (i, 0)),
                  pl.BlockSpec((None, per_tile, P, n), lambda i, j: (0, j, 0, 0)), tile],
        out_specs=[tile] * 3, out_shape=[jax.ShapeDtypeStruct((M, D), F32)] * 3,
        compiler_params=_params("parallel", "parallel"))(pn, wpg, p_in, wpp, h)


def ffn_down_bwd(name, dy, wd, dup, dgate, deps=()):
    _, _, K, n = wd.shape
    M = dy.shape[0]
    tm, tq = _pick(M, _TILES), _pick(K, _TILES)
    assert n <= MAX_CONTRACT

    def body(dy_ref, w_ref, dup_ref, dgate_ref, *rest):
        dg_ref, du_ref = rest[len(deps):]
        dact = lax.dot_general(dy_ref[...], w_ref[...], _NT, preferred_element_type=F32)
        dg_ref[...] = (dact * dgate_ref[...].astype(F32)).astype(dg_ref.dtype)
        du_ref[...] = (dact * dup_ref[...].astype(F32)).astype(du_ref.dtype)

    blk = pl.BlockSpec((tm, tq), lambda i, q: (i, q))
    return pl.pallas_call(
        body, name=name, grid=(M // tm, K // tq),
        in_specs=[pl.BlockSpec((tm, n), lambda i, q: (i, 0)),
                  pl.BlockSpec((None, None, tq, n), lambda i, q: (0, 0, q, 0)), blk, blk] + [_ANY] * len(deps),
        out_specs=[blk, blk], out_shape=[jax.ShapeDtypeStruct((M, K), BF16)] * 2,
        compiler_params=_params("parallel", "parallel"))(dy, wd, dup, dgate, *deps)


def rowwise(name, fn, rows, ins, outs, tr=256, deps=()):
    widest = max([s[1].shape[1] if s[0] != "col" else s[3] for s in ins] + [s[1] for s in outs])
    tr = min(tr if widest <= 2048 else tr // 2, rows)
    in_specs, args = [], []
    for spec in ins:
        kind, a = spec[0], spec[1]
        if kind == "row":
            in_specs.append(pl.BlockSpec((tr, a.shape[1]), lambda i: (i, 0)))
        elif kind == "col":
            cb, width = spec[2], spec[3]
            in_specs.append(pl.BlockSpec((tr, width), lambda i, cb=cb: (i, cb)))
        else:
            in_specs.append(pl.BlockSpec(a.shape, lambda i: (0, 0)))
        args.append(a)
    out_specs, out_shapes = [], []
    for spec in outs:
        if spec[0] == "row":
            out_specs.append(pl.BlockSpec((tr, spec[1]), lambda i: (i, 0)))
            out_shapes.append(jax.ShapeDtypeStruct((rows, spec[1]), spec[2]))
        else:
            out_specs.append(pl.BlockSpec((1, spec[1]), lambda i: (0, 0)))
            out_shapes.append(jax.ShapeDtypeStruct((1, spec[1]), F32))
    n_in = len(ins)

    def body(*refs):
        vals = fn(*[r[...] for r in refs[:n_in]])
        first = pl.program_id(0) == 0
        for r, v, spec in zip(refs[n_in + len(deps):], vals, outs):
            if spec[0] == "row":
                r[...] = v.astype(r.dtype)
            else:
                _accumulate(r, v, first)

    return pl.pallas_call(body, name=name, grid=(rows // tr,), in_specs=in_specs + [_ANY] * len(deps),
                          out_specs=out_specs, out_shape=out_shapes,
                          compiler_params=_params("arbitrary"))(*args, *deps)


def _accumulate(ref, v, first):
    @pl.when(first)
    def _():
        ref[...] = v

    @pl.when(jnp.logical_not(first))
    def _():
        ref[...] += v


def _rms(x, w):
    r = lax.rsqrt(jnp.mean(x * x, axis=-1, keepdims=True) + NORM_EPS)
    return x * r * w


def _rms_bwd(x, w, dy):
    r = lax.rsqrt(jnp.mean(x * x, axis=-1, keepdims=True) + NORM_EPS)
    g = dy * w
    dx = r * (g - x * (r * r) * jnp.mean(g * x, axis=-1, keepdims=True))
    dw = jnp.sum(dy * x * r, axis=0, keepdims=True)
    return dx, dw


def _sigmoid(x):
    return 1.0 / (1.0 + jnp.exp(-x))


def _silu_and_grad(g):
    s = _sigmoid(g)
    return g * s, s * (1.0 + g * (1.0 - s))


def _swap_pairs(x):
    n = x.shape[-1]
    lane = lax.broadcasted_iota(jnp.int32, x.shape, x.ndim - 1)
    return jnp.where((lane & 1) == 0, pltpu.roll(x, n - 1, x.ndim - 1), pltpu.roll(x, 1, x.ndim - 1))


def _rot(x, cosf, sins):
    return x * cosf + _swap_pairs(x) * sins


def _unrot(d, cosf, sins):
    return d * cosf + _swap_pairs(d * sins)


def _ret_log_gamma(h):
    vals = [math.log1p(-2.0 ** (-5.0 - i)) for i in range(RET_HEADS)]
    out = jnp.float32(vals[RET_HEADS - 1])
    for i in range(RET_HEADS - 2, -1, -1):
        out = jnp.where(h == i, jnp.float32(vals[i]), out)
    return out


def _fill_decays(dec_ref, lg):
    ri = lax.broadcasted_iota(jnp.int32, (BLK, BLK), 0)
    ci = lax.broadcasted_iota(jnp.int32, (BLK, BLK), 1)
    for d in range(dec_ref.shape[0]):
        dt = d * BLK + ri - ci
        dec_ref[d] = jnp.where(dt >= 0, jnp.exp(jnp.maximum(dt, 0).astype(F32) * lg), 0.0)


def _decay_row(dec_ref, qi):
    return jnp.concatenate([dec_ref[qi - kb] for kb in range(qi + 1)], axis=1)


def _once(block_shape, index_map):
    return pl.BlockSpec(block_shape, index_map, pipeline_mode=pl.Buffered(1))


def _dot(a, b):
    return jnp.dot(a.astype(BF16), b.astype(BF16), preferred_element_type=F32)


def _dot_nt(a, b):
    return lax.dot_general(a.astype(BF16), b.astype(BF16), _NT, preferred_element_type=F32)


def _dot_tn(a, b):
    return lax.dot_general(a.astype(BF16), b.astype(BF16), _TN, preferred_element_type=F32)


def retention_fwd(name, z, cosf, sins, width_out):
    T = z.shape[0]
    nq = T // BLK
    scale = RET_DK ** -0.5

    def body(q_ref, k_ref, v_ref, cos_ref, sin_ref, o_ref, krot, vb, dec_ref):
        _fill_decays(dec_ref, _ret_log_gamma(pl.program_id(0)))
        krot[...] = (_rot(k_ref[...], cos_ref[...], sin_ref[...]) * scale).astype(BF16)
        vb[...] = v_ref[...].astype(BF16)
        for qi in range(nq):
            rows, n = slice(qi * BLK, (qi + 1) * BLK), (qi + 1) * BLK
            q = _rot(q_ref[rows, :], cos_ref[rows, :], sin_ref[rows, :])
            s = _dot_nt(q, krot[0:n, :]) * _decay_row(dec_ref, qi)
            o_ref[rows, :] = _dot(s, vb[0:n, :])

    return pl.pallas_call(
        body, name=name, grid=(RET_HEADS,),
        in_specs=[pl.BlockSpec((T, RET_DK), lambda h: (0, OFF_RQ // RET_DK + h)),
                  pl.BlockSpec((T, RET_DK), lambda h: (0, OFF_RK // RET_DK + h)),
                  pl.BlockSpec((T, RET_DV), lambda h: (0, OFF_RV // RET_DV + h)),
                  _once((T, RET_DK), lambda h: (0, 0)), _once((T, RET_DK), lambda h: (0, 0))],
        out_specs=pl.BlockSpec((T, RET_DV), lambda h: (0, h)),
        out_shape=jax.ShapeDtypeStruct((T, width_out), F32),
        scratch_shapes=[pltpu.VMEM((T, RET_DK), BF16), pltpu.VMEM((T, RET_DV), BF16),
                        pltpu.VMEM((nq, BLK, BLK), F32)],
        compiler_params=_params("arbitrary"))(z, z, z, cosf, sins)


def retention_bwd(name, z, cosf, sins, do):
    T = z.shape[0]
    nq = T // BLK
    scale = RET_DK ** -0.5

    def body(q_ref, k_ref, v_ref, cos_ref, sin_ref, do_ref, dq_ref, dk_ref, dv_ref, krot, vb, dk_acc, dv_acc, dec_ref):
        _fill_decays(dec_ref, _ret_log_gamma(pl.program_id(0)))
        krot[...] = (_rot(k_ref[...], cos_ref[...], sin_ref[...]) * scale).astype(BF16)
        vb[...] = v_ref[...].astype(BF16)
        dk_acc[...] = jnp.zeros_like(dk_acc)
        dv_acc[...] = jnp.zeros_like(dv_acc)
        for qi in range(nq):
            rows, n = slice(qi * BLK, (qi + 1) * BLK), (qi + 1) * BLK
            cos_q, sin_q = cos_ref[rows, :], sin_ref[rows, :]
            q = _rot(q_ref[rows, :], cos_q, sin_q).astype(BF16)
            dout = do_ref[rows, :].astype(BF16)
            kk, vv, dec = krot[0:n, :], vb[0:n, :], _decay_row(dec_ref, qi)
            p = (_dot_nt(q, kk) * dec).astype(BF16)
            ds = (_dot_nt(dout, vv) * dec).astype(BF16)
            dq_ref[rows, :] = _unrot(_dot(ds, kk), cos_q, sin_q).astype(dq_ref.dtype)
            dk_acc[0:n, :] += _dot_tn(ds, q)
            dv_acc[0:n, :] += _dot_tn(p, dout)
        dk_ref[...] = (_unrot(dk_acc[...], cos_ref[...], sin_ref[...]) * scale).astype(dk_ref.dtype)
        dv_ref[...] = dv_acc[...].astype(dv_ref.dtype)

    head = lambda h: (0, h)
    return pl.pallas_call(
        body, name=name, grid=(RET_HEADS,),
        in_specs=[pl.BlockSpec((T, RET_DK), lambda h: (0, OFF_RQ // RET_DK + h)),
                  pl.BlockSpec((T, RET_DK), lambda h: (0, OFF_RK // RET_DK + h)),
                  pl.BlockSpec((T, RET_DV), lambda h: (0, OFF_RV // RET_DV + h)),
                  _once((T, RET_DK), lambda h: (0, 0)), _once((T, RET_DK), lambda h: (0, 0)),
                  pl.BlockSpec((T, RET_DV), head)],
        out_specs=[pl.BlockSpec((T, RET_DK), head), pl.BlockSpec((T, RET_DK), head), pl.BlockSpec((T, RET_DV), head)],
        out_shape=[jax.ShapeDtypeStruct((T, RET_QK), BF16), jax.ShapeDtypeStruct((T, RET_QK), BF16),
                   jax.ShapeDtypeStruct((T, RET_V), BF16)],
        scratch_shapes=[pltpu.VMEM((T, RET_DK), BF16), pltpu.VMEM((T, RET_DV), BF16),
                        pltpu.VMEM((T, RET_DK), F32), pltpu.VMEM((T, RET_DV), F32),
                        pltpu.VMEM((nq, BLK, BLK), F32)],
        compiler_params=_params("arbitrary"))(z, z, z, cosf, sins, do)


GLA_PAIR = 2


def _gla_chunk(q_ref, k_ref, v_ref, glr_ref, gu, gb, rows, hh, trilf):
    ck = slice(hh * GLA_DK, (hh + 1) * GLA_DK)
    zg = _dot(glr_ref[rows, :], gu[:, ck]) + gb[:, ck]
    la = (jnp.minimum(zg, 0.0) - jnp.log(1.0 + jnp.exp(-jnp.abs(zg)))) * (1.0 / GLA_GATE_NORM)
    cum = jnp.dot(trilf, la, precision=HIGHEST, preferred_element_type=F32)
    last = jnp.sum(la, axis=0, keepdims=True)
    ecum = jnp.exp(cum)
    k = k_ref[rows, ck]
    qt = q_ref[rows, ck] * (GLA_DK ** -0.5) * ecum
    kt = k * jnp.exp(-cum)
    kh = k * jnp.exp(last - cum)
    return zg, cum, last, ecum, qt, kt, kh, v_ref[rows, hh * GLA_DV:(hh + 1) * GLA_DV].astype(BF16)


def _state_decay(last):
    e = jnp.exp(jnp.broadcast_to(last, (GLA_DK, GLA_DK)).T)
    return jnp.concatenate([e] * (GLA_DV // GLA_DK), axis=1)


def _gla_specs(T):
    wk, wv = GLA_PAIR * GLA_DK, GLA_PAIR * GLA_DV
    return [_once((T, wk), lambda h: (0, OFF_GQ // wk + h)),
            _once((T, wk), lambda h: (0, OFF_GK // wk + h)),
            _once((T, wv), lambda h: (0, OFF_GV // wv + h)),
            _once((T, LANE), lambda h: (0, 0)),
            pl.BlockSpec((LANE, wk), lambda h: (0, h)),
            pl.BlockSpec((1, wk), lambda h: (0, h))]


def gla_fwd(name, z, glr, gu, gb, o_prev):
    T = z.shape[0]
    nc = T // CHUNK
    wv = GLA_PAIR * GLA_DV

    def body(q_ref, k_ref, v_ref, glr_ref, gu_ref, gb_ref, prev_ref, o_ref, s_all_ref, *S):
        del prev_ref
        gu_b, gb_v = gu_ref[...].astype(BF16), gb_ref[...]
        ri = lax.broadcasted_iota(jnp.int32, (CHUNK, CHUNK), 0)
        ci = lax.broadcasted_iota(jnp.int32, (CHUNK, CHUNK), 1)
        tril = ri >= ci
        trilf = tril.astype(F32)
        for s_ref in S:
            s_ref[...] = jnp.zeros_like(s_ref)

        def step(c, carry):
            rows = pl.ds(pl.multiple_of(c * CHUNK, CHUNK), CHUNK)
            heads = range(GLA_PAIR)
            ch = [_gla_chunk(q_ref, k_ref, v_ref, glr_ref, gu_b, gb_v, rows, hh, trilf) for hh in heads]
            a = [jnp.where(tril, _dot_nt(ch[hh][4], ch[hh][5]), 0.0) for hh in heads]
            s_prev = [S[hh][...] for hh in heads]
            intra = [_dot(a[hh], ch[hh][7]) for hh in heads]
            inter = [_dot(ch[hh][4], s_prev[hh]) for hh in heads]
            added = [_dot_tn(ch[hh][6], ch[hh][7]) for hh in heads]
            for hh in heads:
                o_ref[rows, hh * GLA_DV:(hh + 1) * GLA_DV] = intra[hh] + inter[hh]
                s_all_ref[hh, c] = s_prev[hh]
                S[hh][...] = s_prev[hh] * _state_decay(ch[hh][2]) + added[hh]
            return carry

        lax.fori_loop(0, nc, step, 0)

    n_in = 6
    return pl.pallas_call(
        body, name=name, grid=(GLA_HEADS // GLA_PAIR,),
        in_specs=_gla_specs(T) + [pl.BlockSpec(memory_space=pl.ANY)],
        out_specs=[pl.BlockSpec((T, wv), lambda h: (0, RET_V // wv + h)),
                   pl.BlockSpec((GLA_PAIR, nc, GLA_DK, GLA_DV), lambda h: (h, 0, 0, 0))],
        out_shape=[jax.ShapeDtypeStruct(o_prev.shape, F32),
                   jax.ShapeDtypeStruct((GLA_HEADS, nc, GLA_DK, GLA_DV), F32)],
        scratch_shapes=[pltpu.VMEM((GLA_DK, GLA_DV), F32)] * GLA_PAIR,
        input_output_aliases={n_in: 0},
        compiler_params=_params("arbitrary"))(z, z, z, glr, gu, gb, o_prev)


def gla_bwd(name, z, glr, gu, gb, do, states):
    T = z.shape[0]
    nc = T // CHUNK

    def body(q_ref, k_ref, v_ref, glr_ref, gu_ref, gb_ref, do_ref, s_all,
             dq_ref, dk_ref, dv_ref, dglr_ref, dgu_ref, dgb_ref, dS):
        gu_b, gb_v = gu_ref[...].astype(BF16), gb_ref[...]
        ri = lax.broadcasted_iota(jnp.int32, (CHUNK, CHUNK), 0)
        ci = lax.broadcasted_iota(jnp.int32, (CHUNK, CHUNK), 1)
        tril = ri >= ci
        trilf = tril.astype(F32)
        triuf = (ri <= ci).astype(F32)
        last_row = lax.broadcasted_iota(jnp.int32, (CHUNK, GLA_DK), 0) == CHUNK - 1
        ones8 = jnp.ones((8, GLA_DV), F32)

        heads = range(GLA_PAIR)

        dS[...] = jnp.zeros_like(dS)
        dgu_ref[...] = jnp.zeros_like(dgu_ref)
        dgb_ref[...] = jnp.zeros_like(dgb_ref)

        def bstep(i, carry):
            c = nc - 1 - i
            rows = pl.ds(pl.multiple_of(c * CHUNK, CHUNK), CHUNK)
            glr_c = glr_ref[rows, :]
            cks = [slice(hh * GLA_DK, (hh + 1) * GLA_DK) for hh in heads]
            cvs = [slice(hh * GLA_DV, (hh + 1) * GLA_DV) for hh in heads]
            ch = [_gla_chunk(q_ref, k_ref, v_ref, glr_ref, gu_b, gb_v, rows, hh, trilf) for hh in heads]
            zg, cum, last, ecum, qt, kt, kh, v = [[ch[hh][j] for hh in heads] for j in range(8)]
            s_prev = [s_all[hh, c] for hh in heads]
            ds_new = [dS[hh] for hh in heads]
            dout = [do_ref[rows, cvs[hh]].astype(BF16) for hh in heads]
            a = [jnp.where(tril, _dot_nt(qt[hh], kt[hh]), 0.0) for hh in heads]
            da = [jnp.where(tril, _dot_nt(dout[hh], v[hh]), 0.0) for hh in heads]
            dv_a = [_dot_tn(a[hh], dout[hh]) for hh in heads]
            dv_b = [_dot(kh[hh], ds_new[hh]) for hh in heads]
            dqt_a = [_dot(da[hh], kt[hh]) for hh in heads]
            dqt_b = [_dot_nt(dout[hh], s_prev[hh]) for hh in heads]
            dkt = [_dot_tn(da[hh], qt[hh]) for hh in heads]
            dkh = [_dot_nt(v[hh], ds_new[hh]) for hh in heads]
            ds_add = [_dot_tn(qt[hh], dout[hh]) for hh in heads]
            rs = [lax.dot_general(ones8, ds_new[hh] * s_prev[hh], _NT, precision=HIGHEST, preferred_element_type=F32)
                  for hh in heads]
            dcum = []
            for hh in heads:
                dv_ref[rows, cvs[hh]] = (dv_a[hh] + dv_b[hh]).astype(dv_ref.dtype)
                dS[hh] = ds_new[hh] * _state_decay(last[hh]) + ds_add[hh]
                dqt = dqt_a[hh] + dqt_b[hh]
                dq_ref[rows, cks[hh]] = (dqt * ecum[hh] * (GLA_DK ** -0.5)).astype(dq_ref.dtype)
                dk_ref[rows, cks[hh]] = (dkt[hh] * jnp.exp(-cum[hh])
                                         + dkh[hh] * jnp.exp(last[hh] - cum[hh])).astype(dk_ref.dtype)
                dkh_kh = dkh[hh] * kh[hh]
                dlast = (jnp.sum(dkh_kh, axis=0, keepdims=True)
                         + jnp.exp(last[hh]) * (jnp.sum(rs[hh], axis=0, keepdims=True) * 0.125))
                dcum.append(dqt * qt[hh] - dkt[hh] * kt[hh] - dkh_kh + jnp.where(last_row, dlast, 0.0))
            dla = [jnp.dot(triuf, dcum[hh], precision=HIGHEST, preferred_element_type=F32) for hh in heads]
            dzg = [dla[hh] * (1.0 / GLA_GATE_NORM) * _sigmoid(-zg[hh]) for hh in heads]
            dglr = [_dot_nt(dzg[hh], gu_b[:, cks[hh]]) for hh in heads]
            dgu = [_dot_tn(glr_c, dzg[hh]) for hh in heads]
            for hh in heads:
                dglr_ref[hh, rows, :] = dglr[hh]
                dgu_ref[:, cks[hh]] += dgu[hh]
                dgb_ref[:, cks[hh]] += jnp.sum(dzg[hh], axis=0, keepdims=True)
            return carry

        lax.fori_loop(0, nc, bstep, 0)

    wk, wv = GLA_PAIR * GLA_DK, GLA_PAIR * GLA_DV
    return pl.pallas_call(
        body, name=name, grid=(GLA_HEADS // GLA_PAIR,),
        in_specs=_gla_specs(T) + [_once((T, wv), lambda h: (0, RET_V // wv + h)),
                                  _once((GLA_PAIR, nc, GLA_DK, GLA_DV), lambda h: (h, 0, 0, 0))],
        out_specs=[pl.BlockSpec((T, wk), lambda h: (0, h)), pl.BlockSpec((T, wk), lambda h: (0, h)),
                   pl.BlockSpec((T, wv), lambda h: (0, h)),
                   pl.BlockSpec((GLA_PAIR, T, LANE), lambda h: (h, 0, 0)),
                   pl.BlockSpec((LANE, wk), lambda h: (0, h)), pl.BlockSpec((1, wk), lambda h: (0, h))],
        out_shape=[jax.ShapeDtypeStruct((T, GLA_QK), BF16), jax.ShapeDtypeStruct((T, GLA_QK), BF16),
                   jax.ShapeDtypeStruct((T, GLA_V), BF16), jax.ShapeDtypeStruct((GLA_HEADS, T, LANE), F32),
                   jax.ShapeDtypeStruct((LANE, GLA_QK), F32), jax.ShapeDtypeStruct((1, GLA_QK), F32)],
        scratch_shapes=[pltpu.VMEM((GLA_PAIR, GLA_DK, GLA_DV), F32)],
        compiler_params=_params("arbitrary"))(z, z, z, glr, gu, gb, do, states)


HN_HEADS = RET_HEADS + GLA_HEADS
HN_W = RET_DV


def _gate_col(h):
    return jnp.where(h < RET_HEADS, OFF_RG // HN_W + h, OFF_GG // HN_W + h - RET_HEADS)


def headnorm_fwd(name, oraw, z, w):
    T = oraw.shape[0]
    tr = _pick(T, _TILES)

    def body(o_ref, g_ref, w_ref, y_ref):
        y_ref[...] = (_rms(o_ref[...], w_ref[...]) * _silu_and_grad(g_ref[...])[0]).astype(y_ref.dtype)

    return pl.pallas_call(
        body, name=name, grid=(HN_HEADS, T // tr),
        in_specs=[pl.BlockSpec((tr, HN_W), lambda h, i: (i, h)),
                  pl.BlockSpec((tr, HN_W), lambda h, i: (i, _gate_col(h))),
                  pl.BlockSpec((1, HN_W), lambda h, i: (0, h))],
        out_specs=pl.BlockSpec((tr, HN_W), lambda h, i: (i, h)),
        out_shape=jax.ShapeDtypeStruct((T, HN_HEADS * HN_W), BF16),
        compiler_params=_params("arbitrary", "arbitrary"))(oraw, z, w)


def headnorm_bwd(name, oraw, z, w, dy):
    T = oraw.shape[0]
    tr = _pick(T, _TILES)

    def body(o_ref, g_ref, w_ref, dy_ref, do_ref, dg_ref, dw_ref):
        o, wv, dyv = o_ref[...], w_ref[...], dy_ref[...].astype(F32)
        silu, dsilu = _silu_and_grad(g_ref[...])
        n = _rms(o, wv)
        dg_ref[...] = (dyv * n * dsilu).astype(dg_ref.dtype)
        dx, dw = _rms_bwd(o, wv, dyv * silu)
        do_ref[...] = dx
        _accumulate(dw_ref, dw, pl.program_id(1) == 0)

    blk = pl.BlockSpec((tr, HN_W), lambda h, i: (i, h))
    return pl.pallas_call(
        body, name=name, grid=(HN_HEADS, T // tr),
        in_specs=[blk, pl.BlockSpec((tr, HN_W), lambda h, i: (i, _gate_col(h))),
                  pl.BlockSpec((1, HN_W), lambda h, i: (0, h)), blk],
        out_specs=[blk, blk, pl.BlockSpec((1, HN_W), lambda h, i: (0, h))],
        out_shape=[jax.ShapeDtypeStruct((T, HN_HEADS * HN_W), F32),
                   jax.ShapeDtypeStruct((T, HN_HEADS * HN_W), BF16),
                   jax.ShapeDtypeStruct((1, HN_HEADS * HN_W), F32)],
        compiler_params=_params("arbitrary", "arbitrary"))(oraw, z, w, dy)


N_MASKS = 4


def _check_mask_classes(T):
    for window, dilation in DILATED_BRANCHES[:-1]:
        assert window < (N_MASKS - 1) * BLK - (BLK - 1) and BLK % dilation == 0
    assert DILATED_BRANCHES[-1][0] >= T and BLK % DILATED_BRANCHES[-1][1] == 0


def _fill_masks(logm_ref):
    ri = lax.broadcasted_iota(jnp.int32, (BLK, BLK), 0)
    ci = lax.broadcasted_iota(jnp.int32, (BLK, BLK), 1)
    for d in range(N_MASKS):
        dt = d * BLK + ri - ci
        mult = jnp.zeros((BLK, BLK), F32)
        for window, dilation in DILATED_BRANCHES:
            hit = (dt >= 0) & (dt <= window) & ((dt & (dilation - 1)) == 0)
            mult = mult + hit.astype(F32)
        logm_ref[d] = jnp.where(mult > 0, jnp.log(jnp.maximum(mult, 1.0)), -1e30)


def _mask_row(ref, qi):
    return jnp.concatenate([ref[min(qi - kb, N_MASKS - 1)] for kb in range(qi + 1)], axis=1)


def attn_fwd(name, qkv):
    T = qkv.shape[0]
    D = qkv.shape[1] // 3
    dh = D // ATT_HEADS
    nq = T // BLK
    scale = dh ** -0.5

    _check_mask_classes(T)

    def body(q_ref, k_ref, v_ref, o_ref, lse_ref, logm_ref):
        @pl.when(pl.program_id(0) == 0)
        def _():
            _fill_masks(logm_ref)

        for q0 in range(0, nq, 2):
            qis = range(q0, min(q0 + 2, nq))
            rows = [slice(qi * BLK, (qi + 1) * BLK) for qi in qis]
            ns = [(qi + 1) * BLK for qi in qis]
            s = [_dot_nt(q_ref[r, :], k_ref[0:n, :]) for r, n in zip(rows, ns)]
            s = [x * scale + _mask_row(logm_ref, qi) for x, qi in zip(s, qis)]
            m = [jnp.max(x, axis=-1, keepdims=True) for x in s]
            p = [jnp.exp(x - mx) for x, mx in zip(s, m)]
            l = [jnp.sum(x, axis=-1, keepdims=True) for x in p]
            pv = [_dot(x, v_ref[0:n, :]) for x, n in zip(p, ns)]
            for r, x, lx, mx in zip(rows, pv, l, m):
                o_ref[r, :] = (x / lx).astype(o_ref.dtype)
                lse_ref[r, :] = jnp.broadcast_to(mx + jnp.log(lx), (BLK, LANE))

    return pl.pallas_call(
        body, name=name, grid=(ATT_HEADS,),
        in_specs=[pl.BlockSpec((T, dh), lambda h: (0, h)),
                  pl.BlockSpec((T, dh), lambda h: (0, ATT_HEADS + h)),
                  pl.BlockSpec((T, dh), lambda h: (0, 2 * ATT_HEADS + h))],
        out_specs=[pl.BlockSpec((T, dh), lambda h: (0, h)),
                   pl.BlockSpec((None, T, LANE), lambda h: (h, 0, 0))],
        out_shape=[jax.ShapeDtypeStruct((T, D), BF16), jax.ShapeDtypeStruct((ATT_HEADS, T, LANE), F32)],
        scratch_shapes=[pltpu.VMEM((N_MASKS, BLK, BLK), F32)],
        compiler_params=_params("arbitrary"))(qkv, qkv, qkv)


def attn_bwd(name, qkv, o, lse, do):
    T = qkv.shape[0]
    D = qkv.shape[1] // 3
    dh = D // ATT_HEADS
    nq = T // BLK
    scale = dh ** -0.5

    _check_mask_classes(T)

    def body(q_ref, k_ref, v_ref, o_ref, lse_ref, do_ref, dq_ref, dk_ref, dv_ref, dk_acc, dv_acc, logm_ref):
        @pl.when(pl.program_id(0) == 0)
        def _():
            _fill_masks(logm_ref)

        dk_acc[...] = jnp.zeros_like(dk_acc)
        dv_acc[...] = jnp.zeros_like(dv_acc)
        for qi in range(nq):
            rows, n = slice(qi * BLK, (qi + 1) * BLK), (qi + 1) * BLK
            q, dout = q_ref[rows, :], do_ref[rows, :]
            kk, vv = k_ref[0:n, :], v_ref[0:n, :]
            delta = jnp.sum(dout.astype(F32) * o_ref[rows, :].astype(F32), axis=-1, keepdims=True)
            lse = jnp.max(lse_ref[rows, :], axis=-1, keepdims=True)
            p = jnp.exp(_dot_nt(q, kk) * scale + _mask_row(logm_ref, qi) - lse)
            ds = (p * (_dot_nt(dout, vv) - delta) * scale).astype(BF16)
            dq_ref[rows, :] = _dot(ds, kk).astype(dq_ref.dtype)
            dk_acc[0:n, :] += _dot_tn(ds, q)
            dv_acc[0:n, :] += _dot_tn(p, dout)
        dk_ref[...] = dk_acc[...].astype(dk_ref.dtype)
        dv_ref[...] = dv_acc[...].astype(dv_ref.dtype)

    full = pl.BlockSpec((T, dh), lambda h: (0, h))
    return pl.pallas_call(
        body, name=name, grid=(ATT_HEADS,),
        in_specs=[full, pl.BlockSpec((T, dh), lambda h: (0, ATT_HEADS + h)),
                  pl.BlockSpec((T, dh), lambda h: (0, 2 * ATT_HEADS + h)),
                  full, pl.BlockSpec((None, T, LANE), lambda h: (h, 0, 0)), full],
        out_specs=[full, full, full],
        out_shape=[jax.ShapeDtypeStruct((T, D), BF16)] * 3,
        scratch_shapes=[pltpu.VMEM((T, dh), F32), pltpu.VMEM((T, dh), F32), pltpu.VMEM((N_MASKS, BLK, BLK), F32)],
        compiler_params=_params("arbitrary"))(qkv, qkv, qkv, o, lse, do)


def _mesh_pos():
    mx, my, mc = lax.axis_index("x"), lax.axis_index("y"), lax.axis_index("c")
    return mx, my, mc, 4 * mx + 2 * my + mc


def _peer(k, mx, my, mc):
    px, py, pc = mx ^ (k >> 2), my ^ ((k >> 1) & 1), mc ^ (k & 1)
    return (px, py, pc), 4 * px + 2 * py + pc


_SIBLING = 1
_OTHER_CHIPS = (4, 2, 6)
N_CHIP = N_DEV // 2
_PLANS = {"gather": (2, N_DEV - 1), "to_chips": (2, 1 + len(_OTHER_CHIPS)), "pass_on": (1, len(_OTHER_CHIPS)),
          "halves": (2, N_CHIP), "chip_sums": (2, len(_OTHER_CHIPS))}


def _copies(kind, items, send_sems, recv_sems):
    mx, my, mc, me = _mesh_pos()
    out = []

    def add(n, src, dst, peer):
        out.append(pltpu.make_async_remote_copy(
            src_ref=src, dst_ref=dst, send_sem=send_sems.at[n], recv_sem=recv_sems.at[n],
            device_id=peer, device_id_type=pl.DeviceIdType.MESH))

    per_item = _PLANS[kind][1]
    sibling = _peer(_SIBLING, mx, my, mc)[0]
    for i, refs in enumerate(items):
        n = i * per_item
        if kind == "gather":
            for k in range(1, N_DEV):
                add(n + k - 1, refs[0], refs[1].at[me], _peer(k, mx, my, mc)[0])
        elif kind == "to_chips":
            rows = refs[0].shape[0]
            dst = refs[1].at[me] if rows == refs[1].shape[1] else refs[1].at[me, pl.ds(0, rows)]
            for j, k in enumerate((_SIBLING,) + _OTHER_CHIPS):
                add(n + j, refs[0], dst, _peer(k, mx, my, mc)[0])
        elif kind == "pass_on":
            for j, k in enumerate(_OTHER_CHIPS):
                add(n + j, refs[0].at[me ^ k], refs[0].at[me ^ k], sibling)
        elif kind == "halves":
            for chip in range(N_CHIP):
                add(n + chip, refs[0].at[2 * chip + 1 - mc], refs[1].at[chip], sibling)
        else:
            for j, k in enumerate(_OTHER_CHIPS):
                peer, to = _peer(k, mx, my, mc)
                add(n + j, refs[0].at[to // 2], refs[1].at[me // 2], peer)
    return out


_HBM = pl.BlockSpec(memory_space=pltpu.HBM)
_SEM = pl.BlockSpec(memory_space=pltpu.SEMAPHORE)
_DATAFLOW = pltpu.SideEffectType.DATAFLOW_SIDE_EFFECTING


def exchange_call(name, waits, starts, deps=()):
    bufs, slot_of = [], {}

    def slots(items):
        out = []
        for item in items:
            for b in item:
                if id(b) not in slot_of:
                    slot_of[id(b)] = len(bufs)
                    bufs.append(b)
            out.append(tuple(slot_of[id(b)] for b in item))
        return out

    wait_plan = [(kind, slots(handle[0])) for kind, handle in waits]
    start_plan = [(kind, slots(items)) for kind, items in starts]
    wait_sems = [s for _, handle in waits for s in handle[1:]]
    n_buf, n_ws, n_start = len(bufs), len(wait_sems), len(starts)

    def body(*refs):
        buf_refs, sems_in = refs[:n_buf], refs[n_buf:n_buf + n_ws]
        outs = refs[n_buf + n_ws + len(deps):]
        pick = lambda plan: [tuple(buf_refs[s] for s in item) for item in plan]
        for wi, (kind, plan) in enumerate(wait_plan):
            copies = _copies(kind, pick(plan), sems_in[2 * wi], sems_in[2 * wi + 1])
            for cp in copies:
                cp.wait_send()
            for cp in copies:
                cp.wait_recv()
        for si, (kind, plan) in enumerate(start_plan):
            for cp in _copies(kind, pick(plan), outs[2 * si], outs[2 * si + 1]):
                cp.start()
        outs[-1][...] = jnp.zeros_like(outs[-1])

    hbm_bufs = [pltpu.with_memory_space_constraint(b, pltpu.HBM) for b in bufs]
    sem_shapes = []
    for kind, plan in start_plan:
        sem_shapes += [pltpu.SemaphoreType.DMA((len(plan) * _PLANS[kind][1],))] * 2
    outs = pl.pallas_call(
        body, name=name,
        out_shape=sem_shapes + [pltpu.HBM(b.shape, b.dtype) for b in bufs] + [jax.ShapeDtypeStruct((8, LANE), F32)],
        in_specs=[_HBM] * n_buf + [_SEM] * n_ws + [_ANY] * len(deps),
        out_specs=[_SEM] * (2 * n_start) + [_HBM] * n_buf + [pl.BlockSpec(memory_space=pltpu.VMEM)],
        input_output_aliases={i: 2 * n_start + i for i in range(n_buf)},
        compiler_params=pltpu.CompilerParams(has_side_effects=_DATAFLOW))(*hbm_bufs, *wait_sems, *deps)
    sems, thru, token = outs[:2 * n_start], outs[2 * n_start:-1], outs[-1]
    through = lambda plan: [tuple(thru[s] for s in item) for item in plan]
    waited = [through(plan) for _, plan in wait_plan]
    handles = [(through(plan), sems[2 * si], sems[2 * si + 1]) for si, (_, plan) in enumerate(start_plan)]
    return waited, handles, token


def gather_small(name, a, deps=()):
    def body(a_ref, *rest):
        o_ref, send_sems, recv_sems, local_sem = rest[len(deps):]
        me = _mesh_pos()[3]
        own = pltpu.make_async_copy(a_ref, o_ref.at[me], local_sem)
        own.start()
        copies = _copies("gather", [(a_ref, o_ref)], send_sems, recv_sems)
        for cp in copies:
            cp.start()
        for cp in copies:
            cp.wait_recv()
        for cp in copies:
            cp.wait_send()
        own.wait()

    return pl.pallas_call(
        body, name=name, in_specs=[_ANY] * (1 + len(deps)), out_specs=_ANY,
        out_shape=jax.ShapeDtypeStruct((N_DEV,) + a.shape, a.dtype),
        scratch_shapes=[pltpu.SemaphoreType.DMA((N_DEV - 1,)), pltpu.SemaphoreType.DMA((N_DEV - 1,)),
                        pltpu.SemaphoreType.DMA],
        compiler_params=pltpu.CompilerParams(has_side_effects=True))(a, *deps)


def _adamw_math(w, g, m, v):
    m2 = ADAM_B1 * m + (1.0 - ADAM_B1) * g
    v2 = ADAM_B2 * v + (1.0 - ADAM_B2) * (g * g)
    m_hat = m2 / (1.0 - ADAM_B1 ** ADAM_STEP)
    v_hat = v2 / (1.0 - ADAM_B2 ** ADAM_STEP)
    delta = -ADAM_LR * (m_hat / (jnp.sqrt(v_hat) + ADAM_EPS) + ADAM_WD * w)
    return delta, m2, v2


def chip_sum(name, a, half):
    _, r, c = a.shape
    tr = r
    chip = 2 * lax.axis_index("x") + lax.axis_index("y")
    where = jnp.stack([lax.axis_index("c"), chip ^ 1, chip ^ 2, chip ^ 3]).astype(jnp.int32)

    def body(where_ref, a_ref, h_ref, o_ref):
        del where_ref
        o_ref[...] = (a_ref[...].astype(F32) + h_ref[...].astype(F32)).astype(o_ref.dtype)

    blk = pl.BlockSpec((None, tr, c), lambda g, i, where: (where[1 + g], i, 0))
    grid_spec = pltpu.PrefetchScalarGridSpec(
        num_scalar_prefetch=1, grid=(N_CHIP - 1, r // tr),
        in_specs=[pl.BlockSpec((None, None, tr, c), lambda g, i, where: (where[1 + g], where[0], i, 0)), blk],
        out_specs=blk)
    return pl.pallas_call(
        body, name=name, grid_spec=grid_spec, out_shape=jax.ShapeDtypeStruct((N_CHIP, r, c), BF16),
        compiler_params=_params("parallel", "parallel"))(where, a.reshape(N_CHIP, 2, r, c), half)


def adamw(name, w, m, v, l, land, a, half, prev=None):
    L, r, c = w.shape
    cp = land.shape[2]
    tr = _pick(r, (256, 176, 128, 64, 32, 16, 8))

    def body(w_ref, m_ref, v_ref, land_ref, a_ref, half_ref, *rest):
        g_ref, d_ref, m2_ref, v2_ref = rest[-4:]
        chip = _mesh_pos()[3] // 2
        mine = a_ref[:, pl.ds(0, c)].astype(F32) + half_ref[:, pl.ds(0, c)].astype(F32)
        g = None
        for s in range(N_CHIP):
            part = jnp.where(chip == s, mine, land_ref[s, :, pl.ds(0, c)].astype(F32))
            g = part if g is None else g + part
        delta, m2, v2 = _adamw_math(w_ref[...], g, m_ref[...], v_ref[...])
        g_ref[...] = g
        d_ref[...] = delta
        m2_ref[...] = m2
        v2_ref[...] = v2

    blk = pl.BlockSpec((None, tr, c), lambda i: (l, i, 0))
    shape = jax.ShapeDtypeStruct((L, r, c), F32)
    extra = [] if prev is None else list(prev)
    return pl.pallas_call(
        body, name=name, grid=(r // tr,),
        in_specs=[blk, blk, blk, pl.BlockSpec((N_CHIP, tr, cp), lambda i: (0, i, 0)),
                  pl.BlockSpec((None, tr, cp), lambda i: (_mesh_pos()[3], i, 0)),
                  pl.BlockSpec((None, tr, cp), lambda i: (_mesh_pos()[3] // 2, i, 0))] + [_ANY] * len(extra),
        out_specs=[blk] * 4, out_shape=[shape] * 4,
        input_output_aliases={6 + k: k for k in range(len(extra))},
        compiler_params=_params("parallel"))(w, m, v, land, a, half, *extra)


def adamw_columns(name, w, m, v, land, a, half):
    r, _, D = w.shape
    tc = _pick(D, (256, 128))

    def body(w_ref, m_ref, v_ref, land_ref, a_ref, half_ref, g_ref, d_ref, m2_ref, v2_ref):
        chip = _mesh_pos()[3] // 2
        mine = a_ref[...].astype(F32) + half_ref[...].astype(F32)
        g = None
        for s in range(N_CHIP):
            part = jnp.where(chip == s, mine, land_ref[s].astype(F32))
            g = part if g is None else g + part
        flat = lambda ref: ref[...].reshape(r, tc)
        delta, m2, v2 = _adamw_math(flat(w_ref), g, flat(m_ref), flat(v_ref))
        for ref, val in ((g_ref, g), (d_ref, delta), (m2_ref, m2), (v2_ref, v2)):
            ref[...] = val.reshape(r, 1, tc)

    blk = pl.BlockSpec((r, 1, tc), lambda i: (0, 0, i))
    shape = jax.ShapeDtypeStruct((r, 1, D), F32)
    return pl.pallas_call(
        body, name=name, grid=(D // tc,),
        in_specs=[blk, blk, blk, pl.BlockSpec((N_CHIP, r, tc), lambda i: (0, 0, i)),
                  pl.BlockSpec((None, r, tc), lambda i: (_mesh_pos()[3], 0, i)),
                  pl.BlockSpec((None, r, tc), lambda i: (_mesh_pos()[3] // 2, 0, i))],
        out_specs=[blk] * 4, out_shape=[shape] * 4,
        compiler_params=_params("parallel"))(w, m, v, land, a, half)


def adamw_small(name, w, m, v, parts):
    n = w.shape[1]

    def body(w_ref, m_ref, v_ref, p_ref, g_ref, d_ref, m2_ref, v2_ref):
        g = p_ref[0:1, :]
        for s in range(1, N_DEV):
            g = g + p_ref[s:s + 1, :]
        delta, m2, v2 = _adamw_math(w_ref[...], g, m_ref[...], v_ref[...])
        g_ref[...] = g
        d_ref[...] = delta
        m2_ref[...] = m2
        v2_ref[...] = v2

    shape = jax.ShapeDtypeStruct((1, n), F32)
    return pl.pallas_call(body, name=name, out_shape=[shape] * 4,
                          compiler_params=pltpu.CompilerParams(vmem_limit_bytes=VMEM_LIMIT_BYTES))(w, m, v, parts)


def _rope_tables(positions):
    half = RET_DK // 2
    inv_freq = 1.0 / jnp.power(RET_THETA_BASE, jnp.linspace(0.0, 1.0, half, dtype=F32))
    ang = positions.astype(F32)[:, None] * inv_freq
    cos, sin = jnp.cos(ang), jnp.sin(ang)
    cosf = jnp.repeat(cos, 2, axis=-1)
    sins = jnp.stack([-sin, sin], axis=-1).reshape(cosf.shape)
    return cosf, sins


def _pad_to(a, axis, size):
    pad = [(0, 0)] * a.ndim
    pad[axis] = (0, size - a.shape[axis])
    return jnp.pad(a, pad)


def _round_up(n, m):
    return -(-n // m) * m


def kernel(x, p, positions, attn_norm_w, ffn_norm_w, ple_norm_w, final_norm_w, ab_w_in, ab_gla_gate_up, ab_gla_gate_b, ab_ret_norm_w, ab_gla_norm_w, ab_w_out, c_w_qkv, c_w_out, ffn_w_gate, ffn_w_up, ffn_w_down, ple_w_proj, ple_w_gate, loss_target, m_attn_norm_w, m_ffn_norm_w, m_ple_norm_w, m_final_norm_w, m_ab_w_in, m_ab_gla_gate_up, m_ab_gla_gate_b, m_ab_ret_norm_w, m_ab_gla_norm_w, m_ab_w_out, m_c_w_qkv, m_c_w_out, m_ffn_w_gate, m_ffn_w_up, m_ffn_w_down, m_ple_w_proj, m_ple_w_gate, v_attn_norm_w, v_ffn_norm_w, v_ple_norm_w, v_final_norm_w, v_ab_w_in, v_ab_gla_gate_up, v_ab_gla_gate_b, v_ab_ret_norm_w, v_ab_gla_norm_w, v_ab_w_out, v_c_w_qkv, v_c_w_out, v_ffn_w_gate, v_ffn_w_up, v_ffn_w_down, v_ple_w_proj, v_ple_w_gate):
    T, D = x.shape[1], x.shape[2]
    depth = attn_norm_w.shape[0]
    assert ab_w_in.shape[0] == 1 and c_w_qkv.shape[0] == 1 and depth == 2, "one even and one odd layer"
    me = 4 * lax.axis_index("x") + 2 * lax.axis_index("y") + lax.axis_index("c")
    in_shard = ab_w_in.shape[2]
    in_width = in_shard * N_DEV
    assert in_width == OFF_LR + GLA_GATE_RANK
    fs = ffn_w_gate.shape[2]
    fp = _round_up(fs, LANE)
    gu_cols = ab_gla_gate_up.shape[2]

    bf = lambda a: a.astype(BF16)
    tr_ = lambda a: jnp.swapaxes(a, -1, -2)
    wg_t, wu_t = tr_(ffn_w_gate), tr_(ffn_w_up)
    srcs = {"w_in": bf(tr_(ab_w_in[0]))}
    group_keys = [["w_in"], ["gu", "w_oab"], ["wg0", "wu0"], ["wd0", "wpg0", "wpp0"], ["w_qkv", "w_oc"],
                  ["wg1", "wu1"], ["wd1", "wpg1", "wpp1"]]
    G_IN, G_OUT, G_QKV = 0, 1, 4
    g_ffn = lambda layer: (2, 3) if layer == 0 else (5, 6)

    def landing(key):
        a = srcs[key]
        rows = fp if key[:2] in ("wg", "wu", "wd") else a.shape[0]
        buf = lax.empty((N_DEV, rows) + a.shape[1:], a.dtype)
        if rows > a.shape[0]:
            zeros = jnp.zeros((N_DEV, rows - a.shape[0]) + a.shape[1:], a.dtype)
            buf = lax.dynamic_update_slice(buf, zeros, (0, a.shape[0]) + (0,) * (a.ndim - 1))
        return lax.dynamic_update_slice(buf, a[None], (me,) + (0,) * a.ndim)

    _, chip_handles, gather_token = exchange_call(
        "gather_start_in", [], [("to_chips", [(srcs[k], landing(k)) for k in group_keys[G_IN]])])
    (gather_token, w_out_, gu_, w_qkv_, w_oc_, wg_, wu_, wd_, wpg_, wpp_) = lax.optimization_barrier(
        (gather_token, ab_w_out, ab_gla_gate_up, c_w_qkv, c_w_out, wg_t, wu_t, ffn_w_down, ple_w_gate, ple_w_proj))
    srcs.update(w_oab=bf(w_out_[0]), gu=gu_[0], w_qkv=bf(w_qkv_[0]), w_oc=bf(w_oc_[0]))
    for l in range(depth):
        srcs[f"wg{l}"] = bf(wg_[l])
        srcs[f"wu{l}"] = bf(wu_[l])
        srcs[f"wd{l}"] = bf(wd_[l])
        srcs[f"wpg{l}"] = bf(wpg_[l])
        srcs[f"wpp{l}"] = bf(wpp_[l])
    _, more, gather_token = exchange_call(
        "gather_start", [], [("to_chips", [(srcs[k], landing(k)) for k in keys]) for keys in group_keys[1:]],
        deps=(gather_token,))
    chip_handles = chip_handles + more
    weights = {}

    def gather_wait(gi, dep):
        lands = [(land,) for _, land in chip_handles[gi][0]]
        _, (passing,), _ = exchange_call(
            f"gather{gi}_pass", [("to_chips", chip_handles[gi])], [("pass_on", lands)], deps=(dep,))
        (complete,), _, _ = exchange_call(f"gather{gi}_done", [("pass_on", passing)], [])
        weights.update(zip(group_keys[gi], [land for (land,) in complete]))

    gb = ab_gla_gate_b
    hn_w = jnp.concatenate([ab_ret_norm_w, ab_gla_norm_w], axis=1)
    cosf, sins = _rope_tables(positions[0])
    p_bf = bf(p[:, 0])

    xs = x[0]
    saved = []
    for i in range(depth):
        nm = f"l{i}_"
        w_attn, w_ffn, w_ple = attn_norm_w[i:i + 1], ffn_norm_w[i:i + 1], ple_norm_w[i:i + 1]
        (xn,) = rowwise(nm + "norm_attn", lambda a, w: (_rms(a, w),), T, [("row", xs), ("full", w_attn)],
                        [("row", D, BF16)], deps=(gather_token,) if i == 0 else ())
        if i % 2 == 0:
            gather_wait(G_IN, xn)
            w_in_t = weights["w_in"].reshape(1, 1, in_width, D)
            w_lr_t = _pad_to(w_in_t[0, 0, OFF_LR:], 0, LANE).reshape(1, 1, LANE, D)
            z = mmt_fwd(nm + "mm_in", xn, w_in_t, 0, F32, n=OFF_LR)
            glr = mmt_fwd(nm + "mm_lr", xn, w_lr_t, 0, F32)
            oraw = retention_fwd(nm + "ret_fwd", z, cosf, sins, RET_V + GLA_V)
            gather_wait(G_OUT, oraw)
            w_oab = weights["w_oab"].reshape(1, 1, D, D)
            gu_full = _pad_to(weights["gu"].transpose(1, 0, 2).reshape(GLA_GATE_RANK, GLA_QK), 0, LANE)
            oraw, gla_states = gla_fwd(nm + "gla_fwd", z, glr, gu_full, gb, oraw)
            o = headnorm_fwd(nm + "headnorm_fwd", oraw, z, hn_w)
            h1, hn = mm_add_norm(nm + "mm_out", o, w_oab, xs, w_ffn)
            mixer_saved = (z, glr, oraw, o, gla_states)
        else:
            gather_wait(G_QKV, xn)
            w_qkv = weights["w_qkv"].reshape((1,) + weights["w_qkv"].shape)
            w_oc = weights["w_oc"].reshape(1, 1, D, D)
            qkv = mm_nn(nm + "mm_qkv", xn, w_qkv, 0, BF16)
            o, lse = attn_fwd(nm + "attn_fwd", qkv)
            h1, hn = mm_add_norm(nm + "mm_out", o, w_oc, xs, w_ffn)
            mixer_saved = (qkv, o, lse)
        gather_wait(g_ffn(i)[0], hn)
        wg = weights[f"wg{i}"].reshape(1, N_DEV, fp, D)
        wu = weights[f"wu{i}"].reshape(1, N_DEV, fp, D)
        dup, dgate, act = ffn_gate_up(nm + "ffn_gate_up", hn, wg, wu)
        gather_wait(g_ffn(i)[1], act)
        wd = weights[f"wd{i}"].reshape(1, 1, N_DEV * fp, D)
        wpg = weights[f"wpg{i}"].reshape(1, 1, D, D)
        wpp = weights[f"wpp{i}"].reshape((1,) + weights[f"wpp{i}"].shape)
        h2, pn = mm_add_norm(nm + "mm_down", act, wd, h1, w_ple)
        x_next, s, e = ple_fwd(nm + "ple", pn, wpg, p_bf[i], wpp, h2)
        mixer_w = (w_in_t, w_lr_t, w_oab, gu_full) if i % 2 == 0 else (w_qkv, w_oc)
        saved.append((xs, xn, mixer_saved, mixer_w, (wg, wu, wd, wpg), h1, hn, dup, dgate, act, h2, pn, s, e))
        xs = x_next

    def loss_fn(a, w, t):
        diff = _rms(a, w) - t
        dx, dw = _rms_bwd(a, w, diff * (1.0 / D))
        part = 0.5 * jnp.sum(jnp.mean(diff * diff, axis=-1, keepdims=True), axis=0, keepdims=True)
        return dx, dw, jnp.broadcast_to(part, (1, LANE))

    dx, d_final_w, loss_part = rowwise("loss_head", loss_fn, T,
                                       [("row", xs), ("full", final_norm_w[None, :]), ("row", loss_target[0])],
                                       [("row", D, F32), ("acc", D), ("acc", LANE)])
    loss = lax.psum(loss_part[0, 0], ("x", "y", "c"))

    grads = {}
    on_chip = []
    scatters = []

    def scatter_start(name, keys, deps=()):
        waits = [("halves", on_chip[0][1])] if on_chip else []
        starts = [("halves", [(grads[k], lax.empty((N_CHIP,) + grads[k].shape[1:], BF16)) for k in keys])] if keys else []
        waited, handles, token = exchange_call(name, waits, starts, deps=deps)
        if on_chip:
            done_keys, _ = on_chip.pop()
            sums = [chip_sum(f"{name}_sum{j}", a, half) for j, (a, half) in enumerate(waited[0])]
            _, (handle,), token = exchange_call(
                name + "_chips", [], [("chip_sums", [(cs, lax.empty(cs.shape, BF16)) for cs in sums])])
            scatters.append((done_keys, handle, waited[0]))
        if keys:
            on_chip.append((keys, handles[0]))
        return token

    d_attn_w, d_ffn_w, d_ple_w = [None] * depth, [None] * depth, [None] * depth
    for i in reversed(range(depth)):
        nm = f"l{i}_b_"
        xs_i, xn, mixer_saved, mixer_w, (wg, wu, wd, wpg), h1, hn, dup, dgate, act, h2, pn, s, e = saved[i]
        w_attn, w_ffn, w_ple = attn_norm_w[i:i + 1], ffn_norm_w[i:i + 1], ple_norm_w[i:i + 1]

        def ple_bwd(d, sv, ev):
            gate = _sigmoid(sv)
            return d * gate, d * ev * gate * (1.0 - gate)

        de, ds = rowwise(nm + "ple_out", ple_bwd, T, [("row", dx), ("row", s), ("row", e)],
                         [("row", D, BF16), ("row", D, BF16)], deps=(loss.reshape(1, 1),) if i == depth - 1 else ())
        grads[("ple_w_proj", i)] = mm_tn(nm + "mm_ple_proj_w", p_bf[i], de, N_DEV, BF16)
        grads[("ple_w_gate", i)] = mm_tn(nm + "mm_ple_gate_w", pn, ds, 1, BF16).reshape(N_DEV, D // N_DEV, D)
        dpn = mm_nt(nm + "mm_ple_gate_x", ds, wpg, 0, F32)

        def norm_bwd_add(a, w, dn, dres):
            dxx, dw = _rms_bwd(a, w, dn)
            tot = dres + dxx
            return tot, tot, dw

        dh2, dh2_bf, d_ple_w[i] = rowwise(nm + "norm_ple", norm_bwd_add, T,
                                          [("row", h2), ("full", w_ple), ("row", dpn), ("row", dx)],
                                          [("row", D, F32), ("row", D, BF16), ("acc", D)])
        grads[("ffn_w_down", i)] = mm_tn(nm + "mm_down_w", act, dh2_bf, 1, BF16).reshape(N_DEV, fp, D)
        token = scatter_start(nm + "scatter_ple_down", [("ple_w_proj", i), ("ple_w_gate", i), ("ffn_w_down", i)])
        dg, du = ffn_down_bwd(nm + "ffn_down_x", dh2_bf, wd, dup, dgate, deps=(token,))
        grads[("ffn_w_gate", i)] = mmt_dw(nm + "mm_gate_w", dg, hn, N_DEV, BF16)
        grads[("ffn_w_up", i)] = mmt_dw(nm + "mm_up_w", du, hn, N_DEV, BF16)
        token = scatter_start(nm + "scatter_gate_up", [("ffn_w_gate", i), ("ffn_w_up", i)])
        dhn = mmt_dx_pair(nm + "mm_gate_up_x", dg, wg, du, wu, F32, deps=(token,))
        dh1, dh1_bf, d_ffn_w[i] = rowwise(nm + "norm_ffn", norm_bwd_add, T,
                                          [("row", h1), ("full", w_ffn), ("row", dhn), ("row", dh2)],
                                          [("row", D, F32), ("row", D, BF16), ("acc", D)])
        if i % 2 == 0:
            z, glr, oraw, o, gla_states = mixer_saved
            w_in_t, w_lr_t, w_oab, gu_full = mixer_w
            grads[("ab_w_out", 0)] = mm_tn(nm + "mm_out_w", o, dh1_bf, 1, BF16).reshape(N_DEV, D // N_DEV, D)
            token = scatter_start(nm + "scatter_out", [("ab_w_out", 0)])
            do = mm_nt(nm + "mm_out_x", dh1_bf, w_oab, 0, F32, deps=(token,))
            d_oraw, d_gates, d_hn_w = headnorm_bwd(nm + "headnorm", oraw, z, hn_w, do)
            d_rq, d_rk, d_rv = retention_bwd(nm + "ret", z, cosf, sins, d_oraw)
            d_gq, d_gk, d_gv, d_glr4, d_gu, d_gb = gla_bwd(nm + "gla", z, glr, gu_full, gb, d_oraw, gla_states)
            dz = jnp.concatenate([d_rq, d_rk, d_rv, d_gates[:, :RET_V], d_gq, d_gk, d_gv, d_gates[:, RET_V:]], axis=1)
            (d_glr,) = rowwise(nm + "sum_lr", lambda *a: (a[0] + a[1] + a[2] + a[3],), T,
                               [("row", d_glr4[hh]) for hh in range(GLA_HEADS)], [("row", LANE, BF16)])
            dwt_in = mmt_dw(nm + "mm_in_w", dz, xn, 1, BF16, rows=in_width)
            dwt_in = mmt_dw_rows(nm + "mm_lr_w", d_glr, xn, dwt_in, OFF_LR, GLA_GATE_RANK)
            grads[("ab_w_in", 0)] = dwt_in.reshape(N_DEV, in_shard, D)
            token = scatter_start(nm + "scatter_in", [("ab_w_in", 0)])
            dxn_a = mmt_dx_wide(nm + "mm_in_x", dz, w_in_t, F32, n=OFF_LR, deps=(token,))
            token = scatter_start(nm + "scatter_in_on", [], deps=(dxn_a,))
            dxn_b = mmt_dx(nm + "mm_lr_x", d_glr, w_lr_t, 0, F32, deps=(token,))
        else:
            qkv, o, lse = mixer_saved
            w_qkv, w_oc = mixer_w
            grads[("c_w_out", 0)] = mm_tn(nm + "mm_out_w", o, dh1_bf, 1, BF16).reshape(N_DEV, D // N_DEV, D)
            do = mm_nt(nm + "mm_out_x", dh1_bf, w_oc, 0, BF16)
            dq, dk, dv = attn_bwd(nm + "attn", qkv, o, lse, do)
            dqkv = jnp.concatenate([dq, dk, dv], axis=1)
            grads[("c_w_qkv", 0)] = mm_tn(nm + "mm_qkv_w", xn, dqkv, N_DEV, BF16)
            token = scatter_start(nm + "scatter_attn", [("c_w_out", 0), ("c_w_qkv", 0)])
            dxn_a = mm_nt_wide(nm + "mm_qkv_x", dqkv, w_qkv, F32, deps=(token,))
            dxn_b = None
        dxn = [dxn_a] if dxn_b is None else [dxn_a, dxn_b]

        def norm_bwd_in(a, w, *rest):
            dxx, dw = _rms_bwd(a, w, sum(rest[1:-1], rest[0]))
            return rest[-1] + dxx, dw

        dx, d_attn_w[i] = rowwise(nm + "norm_attn", norm_bwd_in, T,
                                  [("row", xs_i), ("full", w_attn)] + [("row", d) for d in dxn] + [("row", dh1)],
                                  [("row", D, F32), ("acc", D)])

    small_names = ["attn_norm_w", "ffn_norm_w", "ple_norm_w", "final_norm_w", "ab_gla_gate_b", "ab_ret_norm_w",
                   "ab_gla_norm_w"]
    small_grads = [jnp.concatenate(d_attn_w, 0), jnp.concatenate(d_ffn_w, 0), jnp.concatenate(d_ple_w, 0), d_final_w[0],
                   d_gb, d_hn_w[:, :RET_V], d_hn_w[:, RET_V:]]
    small_w = [attn_norm_w, ffn_norm_w, ple_norm_w, final_norm_w, ab_gla_gate_b, ab_ret_norm_w, ab_gla_norm_w]
    small_m = [m_attn_norm_w, m_ffn_norm_w, m_ple_norm_w, m_final_norm_w, m_ab_gla_gate_b, m_ab_ret_norm_w, m_ab_gla_norm_w]
    small_v = [v_attn_norm_w, v_ffn_norm_w, v_ple_norm_w, v_final_norm_w, v_ab_gla_gate_b, v_ab_ret_norm_w, v_ab_gla_norm_w]
    sizes = [int(np.prod(a.shape)) for a in small_w]
    n_gu = GLA_GATE_RANK * GLA_QK
    n_small = _round_up(sum(sizes) + n_gu, LANE)
    pack = lambda parts: _pad_to(jnp.concatenate([a.reshape(-1) for a in parts]), 0, n_small)[None, :]
    small_part = pack(small_grads + [d_gu[:GLA_GATE_RANK]])

    cols_first = lambda a: jnp.transpose(a, (2, 0, 1))
    big_w = dict(ab_w_in=tuple(cols_first(a) for a in (ab_w_in, m_ab_w_in, v_ab_w_in)),
                 ab_w_out=(ab_w_out, m_ab_w_out, v_ab_w_out),
                 c_w_qkv=(c_w_qkv, m_c_w_qkv, v_c_w_qkv), c_w_out=(c_w_out, m_c_w_out, v_c_w_out),
                 ffn_w_gate=(wg_t, tr_(m_ffn_w_gate), tr_(v_ffn_w_gate)),
                 ffn_w_up=(wu_t, tr_(m_ffn_w_up), tr_(v_ffn_w_up)),
                 ffn_w_down=(ffn_w_down, m_ffn_w_down, v_ffn_w_down), ple_w_proj=(ple_w_proj, m_ple_w_proj, v_ple_w_proj),
                 ple_w_gate=(ple_w_gate, m_ple_w_gate, v_ple_w_gate))
    if on_chip:
        scatter_start("scatter_last", [], deps=(dx,))
    results, last = {}, dx
    for gi, (keys, handle, partials) in enumerate(scatters):
        (arrived,), _, _ = exchange_call(f"scatter_wait{gi}", [("chip_sums", handle)], [], deps=(last,))
        for (n, l), (_, land), (a, half) in zip(keys, arrived, partials):
            if n == "ab_w_in":
                results[n] = adamw_columns(f"adamw_{n}", *big_w[n], land, a, half)
            else:
                results[n] = adamw(f"adamw_{n}{l}", *big_w[n], l, land, a, half, prev=results.get(n))
            last = results[n][0]
    for n in ("ffn_w_gate", "ffn_w_up"):
        results[n] = [tr_(a) for a in results[n]]
    results["ab_w_in"] = [jnp.transpose(a, (1, 2, 0)) for a in results["ab_w_in"]]
    small_parts = gather_small("gather_small", small_part, deps=(last,)).reshape(N_DEV, n_small)

    gu_off = sum(sizes)
    own_cols = lambda a: lax.dynamic_slice_in_dim(a.reshape(GLA_GATE_RANK, GLA_QK), me * gu_cols, gu_cols, axis=1)
    small_res = adamw_small("adamw_small", pack(small_w + [jnp.zeros((n_gu,), F32)]),
                            pack(small_m + [jnp.zeros((n_gu,), F32)]), pack(small_v + [jnp.ones((n_gu,), F32)]),
                            small_parts)
    g_gu_full = small_res[0][0, gu_off:gu_off + n_gu]
    g_gu = own_cols(g_gu_full)[None]
    gu_res = adamw_small("adamw_gate_up", *[_pad_to(a.reshape(1, -1), 1, _round_up(a.size, LANE)) for a in
                                            (ab_gla_gate_up, m_ab_gla_gate_up, v_ab_gla_gate_up)],
                         jnp.concatenate([_pad_to(g_gu.reshape(1, -1), 1, _round_up(g_gu.size, LANE)),
                                          jnp.zeros((N_DEV - 1, _round_up(g_gu.size, LANE)), F32)], axis=0))
    for k in range(4):
        off = 0
        for n, a, sz in zip(small_names, small_w, sizes):
            results.setdefault(n, [None] * 4)[k] = small_res[k][0, off:off + sz].reshape(a.shape)
            off += sz
        results.setdefault("ab_gla_gate_up", [None] * 4)[k] = gu_res[k][0, :g_gu.size].reshape(ab_gla_gate_up.shape)

    order = ["attn_norm_w", "ffn_norm_w", "ple_norm_w", "final_norm_w", "ab_w_in", "ab_gla_gate_up", "ab_gla_gate_b",
             "ab_ret_norm_w", "ab_gla_norm_w", "ab_w_out", "c_w_qkv", "c_w_out", "ffn_w_gate", "ffn_w_up", "ffn_w_down",
             "ple_w_proj", "ple_w_gate"]
    return (loss, dx[None], *[results[n][0] for n in order], *[results[n][1] for n in order],
            *[results[n][2] for n in order], *[results[n][3] for n in order])
```

```python
import math

import numpy as np
import jax
import jax.numpy as jnp
from jax import lax
from jax.experimental import pallas as pl
from jax.experimental.pallas import tpu as pltpu

F32 = jnp.float32
BF16 = jnp.bfloat16
HIGHEST = lax.Precision.HIGHEST

N_DEV = 8
VMEM_LIMIT_BYTES = 48 * 1024 * 1024
LANE = 128
NORM_EPS = 1e-6

RET_HEADS, RET_DK, RET_DV = 4, 256, 256
RET_THETA_BASE = 10000.0
GLA_HEADS, GLA_DK, GLA_DV = 4, 128, 256
GLA_GATE_RANK = 16
GLA_GATE_NORM = 16.0
CHUNK = 64
ATT_HEADS = 16
DILATED_BRANCHES = ((128, 1), (512, 4), (2048, 16))
BLK = 256

ADAM_LR, ADAM_B1, ADAM_B2, ADAM_EPS, ADAM_WD, ADAM_STEP = 0.001, 0.9, 0.999, 1e-08, 0.01, 10

RET_QK = RET_HEADS * RET_DK
RET_V = RET_HEADS * RET_DV
GLA_QK = GLA_HEADS * GLA_DK
GLA_V = GLA_HEADS * GLA_DV
OFF_RQ, OFF_RK, OFF_RV, OFF_RG = 0, RET_QK, 2 * RET_QK, 2 * RET_QK + RET_V
OFF_GQ = OFF_RG + RET_V
OFF_GK = OFF_GQ + GLA_QK
OFF_GV = OFF_GK + GLA_QK
OFF_GG = OFF_GV + GLA_V
OFF_LR = OFF_GG + GLA_V


def _params(*sem):
    return pltpu.CompilerParams(dimension_semantics=sem or None, vmem_limit_bytes=VMEM_LIMIT_BYTES)


def _pick(n, cands):
    for c in cands:
        if n % c == 0:
            return c
    raise ValueError(f"no tile for {n} in {cands}")


_NN = (((1,), (0,)), ((), ()))
_NT = (((1,), (1,)), ((), ()))
_TN = (((0,), (0,)), ((), ()))
_ANY = pl.BlockSpec(memory_space=pl.ANY)
MAX_CONTRACT = 2048
_TILES = (1024, 768, 512, 256, 128)


def _mm_call(name, dims, grid, in_specs, out_spec, out_shape, args, deps=()):
    steps = grid[2]
    assert steps == 1 or out_shape.dtype == F32

    def body(a_ref, b_ref, *rest):
        o_ref = rest[len(deps)]
        part = lax.dot_general(a_ref[...].astype(BF16), b_ref[...].astype(BF16), dims, preferred_element_type=F32)
        if steps == 1:
            o_ref[...] = part.astype(o_ref.dtype)
        else:
            _accumulate(o_ref, part, pl.program_id(2) == 0)

    return pl.pallas_call(
        body, name=name, grid=grid, in_specs=list(in_specs) + [_ANY] * len(deps), out_specs=out_spec,
        out_shape=out_shape, compiler_params=_params("parallel", "parallel", "arbitrary"))(*args, *deps)


def mm_nn(name, a, w, l, out_dtype, deps=()):
    _, J, K, n = w.shape
    M = a.shape[0]
    tm, tn, tk = _pick(M, _TILES), _pick(n, _TILES), _pick(K, (MAX_CONTRACT,) + _TILES)
    nt = n // tn
    return _mm_call(
        name, _NN, (M // tm, J * nt, K // tk),
        [pl.BlockSpec((tm, tk), lambda i, j, k: (i, k)),
         pl.BlockSpec((None, None, tk, tn), lambda i, j, k: (l, j // nt, k, j % nt))],
        pl.BlockSpec((tm, tn), lambda i, j, k: (i, j)),
        jax.ShapeDtypeStruct((M, J * n), out_dtype), (a, w), deps)


def mm_nt(name, a, w, l, out_dtype, deps=()):
    _, J, K, n = w.shape
    M = a.shape[0]
    tm, tq, tc = _pick(M, _TILES), _pick(K, _TILES), _pick(n, (MAX_CONTRACT,) + _TILES)
    nc = n // tc
    return _mm_call(
        name, _NT, (M // tm, K // tq, J * nc),
        [pl.BlockSpec((tm, tc), lambda i, q, c: (i, c)),
         pl.BlockSpec((None, None, tq, tc), lambda i, q, c: (l, c // nc, q, c % nc))],
        pl.BlockSpec((tm, tq), lambda i, q, c: (i, q)),
        jax.ShapeDtypeStruct((M, K), out_dtype), (a, w), deps)


def mm_tn(name, x, dy, J, out_dtype, deps=()):
    M, K = x.shape
    n = dy.shape[1] // J
    tp, tn = _pick(K, _TILES), _pick(n, _TILES)
    nt = n // tn
    assert M <= MAX_CONTRACT
    return _mm_call(
        name, _TN, (K // tp, J * nt, 1),
        [pl.BlockSpec((M, tp), lambda i, j, r: (0, i)),
         pl.BlockSpec((M, tn), lambda i, j, r: (0, j))],
        pl.BlockSpec((None, tp, tn), lambda i, j, r: (j // nt, i, j % nt)),
        jax.ShapeDtypeStruct((J, K, n), out_dtype), (x, dy), deps)


def mmt_fwd(name, a, wt, l, out_dtype, n=None, deps=()):
    _, J, rows, K = wt.shape
    n = rows if n is None else n
    M = a.shape[0]
    tm, tn = _pick(M, _TILES), _pick(n, _TILES)
    nt = n // tn
    assert K <= MAX_CONTRACT
    return _mm_call(
        name, _NT, (M // tm, J * nt, 1),
        [pl.BlockSpec((tm, K), lambda i, j, k: (i, 0)),
         pl.BlockSpec((None, None, tn, K), lambda i, j, k: (l, j // nt, j % nt, 0))],
        pl.BlockSpec((tm, tn), lambda i, j, k: (i, j)),
        jax.ShapeDtypeStruct((M, J * n), out_dtype), (a, wt), deps)


def mmt_dx(name, dy, wt, l, out_dtype, n=None, deps=()):
    _, J, rows, K = wt.shape
    n = rows if n is None else n
    M = dy.shape[0]
    tm, tq, tc = _pick(M, _TILES), _pick(K, _TILES), _pick(n, _TILES)
    nc = n // tc
    return _mm_call(
        name, _NN, (M // tm, K // tq, J * nc),
        [pl.BlockSpec((tm, tc), lambda i, q, c: (i, c)),
         pl.BlockSpec((None, None, tc, tq), lambda i, q, c: (l, c // nc, c % nc, q))],
        pl.BlockSpec((tm, tq), lambda i, q, c: (i, q)),
        jax.ShapeDtypeStruct((M, K), out_dtype), (dy, wt), deps)


WIDE_TILE = 512


def _wide_call(name, body, M, K, a, w, a_spec, w_spec, out_dtype, deps):
    def kernel_body(a_ref, w_ref, *rest):
        o_ref = rest[len(deps)]
        o_ref[...] = body(a_ref, w_ref).astype(o_ref.dtype)

    return pl.pallas_call(
        kernel_body, name=name, grid=(M // WIDE_TILE, K // WIDE_TILE),
        in_specs=[a_spec, w_spec] + [_ANY] * len(deps),
        out_specs=pl.BlockSpec((WIDE_TILE, WIDE_TILE), lambda i, q: (i, q)),
        out_shape=jax.ShapeDtypeStruct((M, K), out_dtype),
        compiler_params=_params("parallel", "parallel"))(a, w, *deps)


def mmt_dx_wide(name, dy, wt, out_dtype, n=None, deps=()):
    _, J, rows, K = wt.shape
    n = rows if n is None else n
    M = dy.shape[0]

    def body(dy_ref, w_ref):
        return jnp.dot(dy_ref[...].astype(BF16), w_ref[...].reshape(J * n, WIDE_TILE), preferred_element_type=F32)

    return _wide_call(name, body, M, K, dy, wt,
                      pl.BlockSpec((WIDE_TILE, J * n), lambda i, q: (i, 0)),
                      pl.BlockSpec((None, J, n, WIDE_TILE), lambda i, q: (0, 0, 0, q)), out_dtype, deps)


def mmt_dx_pair(name, dy1, wt1, dy2, wt2, out_dtype, deps=()):
    _, J, n, K = wt1.shape
    M = dy1.shape[0]

    def body(dy1_ref, w1_ref, dy2_ref, w2_ref, *rest):
        o_ref = rest[len(deps)]
        acc = jnp.dot(dy1_ref[...], w1_ref[...].reshape(J * n, WIDE_TILE), preferred_element_type=F32)
        acc = acc + jnp.dot(dy2_ref[...], w2_ref[...].reshape(J * n, WIDE_TILE), preferred_element_type=F32)
        o_ref[...] = acc.astype(o_ref.dtype)

    rows = _once((WIDE_TILE, J * n), lambda i, q: (i, 0))
    cols = pl.BlockSpec((None, J, n, WIDE_TILE), lambda i, q: (0, 0, 0, q))
    return pl.pallas_call(
        body, name=name, grid=(M // WIDE_TILE, K // WIDE_TILE),
        in_specs=[rows, cols, rows, cols] + [_ANY] * len(deps),
        out_specs=pl.BlockSpec((WIDE_TILE, WIDE_TILE), lambda i, q: (i, q)),
        out_shape=jax.ShapeDtypeStruct((M, K), out_dtype),
        compiler_params=_params("parallel", "parallel"))(dy1, wt1, dy2, wt2, *deps)


def mm_nt_wide(name, a, w, out_dtype, deps=()):
    _, J, K, n = w.shape
    M = a.shape[0]

    def body(a_ref, w_ref):
        acc = None
        for j in range(J):
            part = lax.dot_general(a_ref[:, j * n:(j + 1) * n].astype(BF16), w_ref[j], _NT, preferred_element_type=F32)
            acc = part if acc is None else acc + part
        return acc

    return _wide_call(name, body, M, K, a, w,
                      pl.BlockSpec((WIDE_TILE, J * n), lambda i, q: (i, 0)),
                      pl.BlockSpec((None, J, WIDE_TILE, n), lambda i, q: (0, 0, q, 0)), out_dtype, deps)


def mmt_dw(name, dy, x, J, out_dtype, deps=(), rows=None):
    M, K = x.shape
    n = dy.shape[1] // J
    tn, tp = _pick(n, _TILES), _pick(K, _TILES)
    nt = n // tn
    assert M <= MAX_CONTRACT
    return _mm_call(
        name, _TN, (J * nt, K // tp, 1),
        [pl.BlockSpec((M, tn), lambda j, i, r: (0, j)),
         pl.BlockSpec((M, tp), lambda j, i, r: (0, i))],
        pl.BlockSpec((None, tn, tp), lambda j, i, r: (j // nt, j % nt, i)),
        jax.ShapeDtypeStruct((J, n if rows is None else rows, K), out_dtype), (dy, x), deps)


def mmt_dw_rows(name, dy, x, out, row0, rank):
    M, K = x.shape
    tp = _pick(K, _TILES)

    def body(dy_ref, x_ref, prev_ref, o_ref):
        del prev_ref
        full = lax.dot_general(dy_ref[...], x_ref[...], _TN, preferred_element_type=F32)
        o_ref[...] = full[:rank].astype(o_ref.dtype)

    return pl.pallas_call(
        body, name=name, grid=(K // tp,),
        in_specs=[pl.BlockSpec((M, dy.shape[1]), lambda i: (0, 0)), pl.BlockSpec((M, tp), lambda i: (0, i)), _ANY],
        out_specs=pl.BlockSpec((None, rank, tp), lambda i: (0, row0 // rank, i)),
        out_shape=jax.ShapeDtypeStruct(out.shape, out.dtype), input_output_aliases={2: 0},
        compiler_params=_params("parallel"))(dy, x, out)


def ffn_gate_up(name, a, wg, wu):
    _, J, n, K = wg.shape
    M = a.shape[0]
    tm, tn = _pick(M, _TILES), _pick(n, _TILES)
    nt = n // tn
    assert K <= MAX_CONTRACT

    def body(a_ref, wg_ref, wu_ref, dup_ref, dgate_ref, act_ref):
        x = a_ref[...]
        g = lax.dot_general(x, wg_ref[...], _NT, preferred_element_type=F32)
        u = lax.dot_general(x, wu_ref[...], _NT, preferred_element_type=F32)
        silu, dsilu = _silu_and_grad(g)
        dup_ref[...] = silu.astype(dup_ref.dtype)
        dgate_ref[...] = (u * dsilu).astype(dgate_ref.dtype)
        act_ref[...] = (silu * u).astype(act_ref.dtype)

    w_spec = pl.BlockSpec((None, None, tn, K), lambda i, j: (0, j // nt, j % nt, 0))
    out = pl.BlockSpec((tm, tn), lambda i, j: (i, j))
    return pl.pallas_call(
        body, name=name, grid=(M // tm, J * nt),
        in_specs=[pl.BlockSpec((tm, K), lambda i, j: (i, 0)), w_spec, w_spec],
        out_specs=[out] * 3, out_shape=[jax.ShapeDtypeStruct((M, J * n), BF16)] * 3,
        compiler_params=_params("parallel", "parallel"))(a, wg, wu)


def mm_add_norm(name, a, w, res, norm_w):
    _, _, K, N = w.shape
    M = a.shape[0]
    tm, tk = _pick(M, (WIDE_TILE, 256)), _pick(K, (1024, 512, 256))
    steps = K // tk

    def body(a_ref, w_ref, res_ref, nw_ref, h_ref, hn_ref):
        k = pl.program_id(1)
        part = jnp.dot(a_ref[...], w_ref[...], preferred_element_type=F32)
        _accumulate(h_ref, part, k == 0)

        @pl.when(k == steps - 1)
        def _():
            h = h_ref[...] + res_ref[...]
            h_ref[...] = h
            hn_ref[...] = _rms(h, nw_ref[...]).astype(hn_ref.dtype)

    rows = pl.BlockSpec((tm, N), lambda i, k: (i, 0))
    return pl.pallas_call(
        body, name=name, grid=(M // tm, steps),
        in_specs=[pl.BlockSpec((tm, tk), lambda i, k: (i, k)),
                  pl.BlockSpec((None, None, tk, N), lambda i, k: (0, 0, k, 0)), rows,
                  pl.BlockSpec((1, N), lambda i, k: (0, 0))],
        out_specs=[rows, rows],
        out_shape=[jax.ShapeDtypeStruct((M, N), F32), jax.ShapeDtypeStruct((M, N), BF16)],
        compiler_params=_params("parallel", "arbitrary"))(a, w, res, norm_w)


def ple_fwd(name, pn, wpg, p_in, wpp, h):
    _, J, P, n = wpp.shape
    M, D = h.shape
    tm, tn = _pick(M, (WIDE_TILE, 256)), _pick(D, _TILES)
    per_tile = tn // n

    def body(pn_ref, wg_ref, p_ref, wp_ref, h_ref, x_ref, s_ref, e_ref):
        s = jnp.dot(pn_ref[...], wg_ref[...], preferred_element_type=F32)
        p_blk = p_ref[...]
        e = jnp.concatenate([jnp.dot(p_blk, wp_ref[j], preferred_element_type=F32) for j in range(per_tile)], axis=1)
        s_ref[...] = s
        e_ref[...] = e
        x_ref[...] = h_ref[...] + _sigmoid(s) * e

    tile = pl.BlockSpec((tm, tn), lambda i, j: (i, j))
    return pl.pallas_call(
        body, name=name, grid=(M // tm, D // tn),
        in_specs=[pl.BlockSpec((tm, D), lambda i, j: (i, 0)),
                  pl.BlockSpec((None, None, D, tn), lambda i, j: (0, 0, 0, j)),
                  pl.BlockSpec((tm, P), lambda i, j: (i, 0)),
                  pl.BlockSpec((None, per_tile, P, n), lambda i, j: (0, j, 0, 0)), tile],
        out_specs=[tile] * 3, out_shape=[jax.ShapeDtypeStruct((M, D), F32)] * 3,
        compiler_params=_params("parallel", "parallel"))(pn, wpg, p_in, wpp, h)


def ffn_down_bwd(name, dy, wd, dup, dgate, deps=()):
    _, _, K, n = wd.shape
    M = dy.shape[0]
    tm, tq = _pick(M, _TILES), _pick(K, _TILES)
    assert n <= MAX_CONTRACT

    def body(dy_ref, w_ref, dup_ref, dgate_ref, *rest):
        dg_ref, du_ref = rest[len(deps):]
        dact = lax.dot_general(dy_ref[...], w_ref[...], _NT, preferred_element_type=F32)
        dg_ref[...] = (dact * dgate_ref[...].astype(F32)).astype(dg_ref.dtype)
        du_ref[...] = (dact * dup_ref[...].astype(F32)).astype(du_ref.dtype)

    blk = pl.BlockSpec((tm, tq), lambda i, q: (i, q))
    return pl.pallas_call(
        body, name=name, grid=(M // tm, K // tq),
        in_specs=[pl.BlockSpec((tm, n), lambda i, q: (i, 0)),
                  pl.BlockSpec((None, None, tq, n), lambda i, q: (0, 0, q, 0)), blk, blk] + [_ANY] * len(deps),
        out_specs=[blk, blk], out_shape=[jax.ShapeDtypeStruct((M, K), BF16)] * 2,
        compiler_params=_params("parallel", "parallel"))(dy, wd, dup, dgate, *deps)


def rowwise(name, fn, rows, ins, outs, tr=256, deps=()):
    widest = max([s[1].shape[1] if s[0] != "col" else s[3] for s in ins] + [s[1] for s in outs])
    tr = min(tr if widest <= 2048 else tr // 2, rows)
    in_specs, args = [], []
    for spec in ins:
        kind, a = spec[0], spec[1]
        if kind == "row":
            in_specs.append(pl.BlockSpec((tr, a.shape[1]), lambda i: (i, 0)))
        elif kind == "col":
            cb, width = spec[2], spec[3]
            in_specs.append(pl.BlockSpec((tr, width), lambda i, cb=cb: (i, cb)))
        else:
            in_specs.append(pl.BlockSpec(a.shape, lambda i: (0, 0)))
        args.append(a)
    out_specs, out_shapes = [], []
    for spec in outs:
        if spec[0] == "row":
            out_specs.append(pl.BlockSpec((tr, spec[1]), lambda i: (i, 0)))
            out_shapes.append(jax.ShapeDtypeStruct((rows, spec[1]), spec[2]))
        else:
            out_specs.append(pl.BlockSpec((1, spec[1]), lambda i: (0, 0)))
            out_shapes.append(jax.ShapeDtypeStruct((1, spec[1]), F32))
    n_in = len(ins)

    def body(*refs):
        vals = fn(*[r[...] for r in refs[:n_in]])
        first = pl.program_id(0) == 0
        for r, v, spec in zip(refs[n_in + len(deps):], vals, outs):
            if spec[0] == "row":
                r[...] = v.astype(r.dtype)
            else:
                _accumulate(r, v, first)

    return pl.pallas_call(body, name=name, grid=(rows // tr,), in_specs=in_specs + [_ANY] * len(deps),
                          out_specs=out_specs, out_shape=out_shapes,
                          compiler_params=_params("arbitrary"))(*args, *deps)


def _accumulate(ref, v, first):
    @pl.when(first)
    def _():
        ref[...] = v

    @pl.when(jnp.logical_not(first))
    def _():
        ref[...] += v


def _rms(x, w):
    r = lax.rsqrt(jnp.mean(x * x, axis=-1, keepdims=True) + NORM_EPS)
    return x * r * w


def _rms_bwd(x, w, dy):
    r = lax.rsqrt(jnp.mean(x * x, axis=-1, keepdims=True) + NORM_EPS)
    g = dy * w
    dx = r * (g - x * (r * r) * jnp.mean(g * x, axis=-1, keepdims=True))
    dw = jnp.sum(dy * x * r, axis=0, keepdims=True)
    return dx, dw


def _sigmoid(x):
    return 1.0 / (1.0 + jnp.exp(-x))


def _silu_and_grad(g):
    s = _sigmoid(g)
    return g * s, s * (1.0 + g * (1.0 - s))


def _swap_pairs(x):
    n = x.shape[-1]
    lane = lax.broadcasted_iota(jnp.int32, x.shape, x.ndim - 1)
    return jnp.where((lane & 1) == 0, pltpu.roll(x, n - 1, x.ndim - 1), pltpu.roll(x, 1, x.ndim - 1))


def _rot(x, cosf, sins):
    return x * cosf + _swap_pairs(x) * sins


def _unrot(d, cosf, sins):
    return d * cosf + _swap_pairs(d * sins)


def _ret_log_gamma(h):
    vals = [math.log1p(-2.0 ** (-5.0 - i)) for i in range(RET_HEADS)]
    out = jnp.float32(vals[RET_HEADS - 1])
    for i in range(RET_HEADS - 2, -1, -1):
        out = jnp.where(h == i, jnp.float32(vals[i]), out)
    return out


def _fill_decays(dec_ref, lg):
    ri = lax.broadcasted_iota(jnp.int32, (BLK, BLK), 0)
    ci = lax.broadcasted_iota(jnp.int32, (BLK, BLK), 1)
    for d in range(dec_ref.shape[0]):
        dt = d * BLK + ri - ci
        dec_ref[d] = jnp.where(dt >= 0, jnp.exp(jnp.maximum(dt, 0).astype(F32) * lg), 0.0)


def _decay_row(dec_ref, qi):
    return jnp.concatenate([dec_ref[qi - kb] for kb in range(qi + 1)], axis=1)


def _once(block_shape, index_map):
    return pl.BlockSpec(block_shape, index_map, pipeline_mode=pl.Buffered(1))


def _dot(a, b):
    return jnp.dot(a.astype(BF16), b.astype(BF16), preferred_element_type=F32)


def _dot_nt(a, b):
    return lax.dot_general(a.astype(BF16), b.astype(BF16), _NT, preferred_element_type=F32)


def _dot_tn(a, b):
    return lax.dot_general(a.astype(BF16), b.astype(BF16), _TN, preferred_element_type=F32)


def retention_fwd(name, z, cosf, sins, width_out):
    T = z.shape[0]
    nq = T // BLK
    scale = RET_DK ** -0.5

    def body(q_ref, k_ref, v_ref, cos_ref, sin_ref, o_ref, krot, vb, dec_ref):
        _fill_decays(dec_ref, _ret_log_gamma(pl.program_id(0)))
        krot[...] = (_rot(k_ref[...], cos_ref[...], sin_ref[...]) * scale).astype(BF16)
        vb[...] = v_ref[...].astype(BF16)
        for qi in range(nq):
            rows, n = slice(qi * BLK, (qi + 1) * BLK), (qi + 1) * BLK
            q = _rot(q_ref[rows, :], cos_ref[rows, :], sin_ref[rows, :])
            s = _dot_nt(q, krot[0:n, :]) * _decay_row(dec_ref, qi)
            o_ref[rows, :] = _dot(s, vb[0:n, :])

    return pl.pallas_call(
        body, name=name, grid=(RET_HEADS,),
        in_specs=[pl.BlockSpec((T, RET_DK), lambda h: (0, OFF_RQ // RET_DK + h)),
                  pl.BlockSpec((T, RET_DK), lambda h: (0, OFF_RK // RET_DK + h)),
                  pl.BlockSpec((T, RET_DV), lambda h: (0, OFF_RV // RET_DV + h)),
                  _once((T, RET_DK), lambda h: (0, 0)), _once((T, RET_DK), lambda h: (0, 0))],
        out_specs=pl.BlockSpec((T, RET_DV), lambda h: (0, h)),
        out_shape=jax.ShapeDtypeStruct((T, width_out), F32),
        scratch_shapes=[pltpu.VMEM((T, RET_DK), BF16), pltpu.VMEM((T, RET_DV), BF16),
                        pltpu.VMEM((nq, BLK, BLK), F32)],
        compiler_params=_params("arbitrary"))(z, z, z, cosf, sins)


def retention_bwd(name, z, cosf, sins, do):
    T = z.shape[0]
    nq = T // BLK
    scale = RET_DK ** -0.5

    def body(q_ref, k_ref, v_ref, cos_ref, sin_ref, do_ref, dq_ref, dk_ref, dv_ref, krot, vb, dk_acc, dv_acc, dec_ref):
        _fill_decays(dec_ref, _ret_log_gamma(pl.program_id(0)))
        krot[...] = (_rot(k_ref[...], cos_ref[...], sin_ref[...]) * scale).astype(BF16)
        vb[...] = v_ref[...].astype(BF16)
        dk_acc[...] = jnp.zeros_like(dk_acc)
        dv_acc[...] = jnp.zeros_like(dv_acc)
        for qi in range(nq):
            rows, n = slice(qi * BLK, (qi + 1) * BLK), (qi + 1) * BLK
            cos_q, sin_q = cos_ref[rows, :], sin_ref[rows, :]
            q = _rot(q_ref[rows, :], cos_q, sin_q).astype(BF16)
            dout = do_ref[rows, :].astype(BF16)
            kk, vv, dec = krot[0:n, :], vb[0:n, :], _decay_row(dec_ref, qi)
            p = (_dot_nt(q, kk) * dec).astype(BF16)
            ds = (_dot_nt(dout, vv) * dec).astype(BF16)
            dq_ref[rows, :] = _unrot(_dot(ds, kk), cos_q, sin_q).astype(dq_ref.dtype)
            dk_acc[0:n, :] += _dot_tn(ds, q)
            dv_acc[0:n, :] += _dot_tn(p, dout)
        dk_ref[...] = (_unrot(dk_acc[...], cos_ref[...], sin_ref[...]) * scale).astype(dk_ref.dtype)
        dv_ref[...] = dv_acc[...].astype(dv_ref.dtype)

    head = lambda h: (0, h)
    return pl.pallas_call(
        body, name=name, grid=(RET_HEADS,),
        in_specs=[pl.BlockSpec((T, RET_DK), lambda h: (0, OFF_RQ // RET_DK + h)),
                  pl.BlockSpec((T, RET_DK), lambda h: (0, OFF_RK // RET_DK + h)),
                  pl.BlockSpec((T, RET_DV), lambda h: (0, OFF_RV // RET_DV + h)),
                  _once((T, RET_DK), lambda h: (0, 0)), _once((T, RET_DK), lambda h: (0, 0)),
                  pl.BlockSpec((T, RET_DV), head)],
        out_specs=[pl.BlockSpec((T, RET_DK), head), pl.BlockSpec((T, RET_DK), head), pl.BlockSpec((T, RET_DV), head)],
        out_shape=[jax.ShapeDtypeStruct((T, RET_QK), BF16), jax.ShapeDtypeStruct((T, RET_QK), BF16),
                   jax.ShapeDtypeStruct((T, RET_V), BF16)],
        scratch_shapes=[pltpu.VMEM((T, RET_DK), BF16), pltpu.VMEM((T, RET_DV), BF16),
                        pltpu.VMEM((T, RET_DK), F32), pltpu.VMEM((T, RET_DV), F32),
                        pltpu.VMEM((nq, BLK, BLK), F32)],
        compiler_params=_params("arbitrary"))(z, z, z, cosf, sins, do)


GLA_PAIR = 2


def _gla_chunk(q_ref, k_ref, v_ref, glr_ref, gu, gb, rows, hh, trilf):
    ck = slice(hh * GLA_DK, (hh + 1) * GLA_DK)
    zg = _dot(glr_ref[rows, :], gu[:, ck]) + gb[:, ck]
    la = (jnp.minimum(zg, 0.0) - jnp.log(1.0 + jnp.exp(-jnp.abs(zg)))) * (1.0 / GLA_GATE_NORM)
    cum = jnp.dot(trilf, la, precision=HIGHEST, preferred_element_type=F32)
    last = jnp.sum(la, axis=0, keepdims=True)
    ecum = jnp.exp(cum)
    k = k_ref[rows, ck]
    qt = q_ref[rows, ck] * (GLA_DK ** -0.5) * ecum
    kt = k * jnp.exp(-cum)
    kh = k * jnp.exp(last - cum)
    return zg, cum, last, ecum, qt, kt, kh, v_ref[rows, hh * GLA_DV:(hh + 1) * GLA_DV].astype(BF16)


def _state_decay(last):
    e = jnp.exp(jnp.broadcast_to(last, (GLA_DK, GLA_DK)).T)
    return jnp.concatenate([e] * (GLA_DV // GLA_DK), axis=1)


def _gla_specs(T):
    wk, wv = GLA_PAIR * GLA_DK, GLA_PAIR * GLA_DV
    return [_once((T, wk), lambda h: (0, OFF_GQ // wk + h)),
            _once((T, wk), lambda h: (0, OFF_GK // wk + h)),
            _once((T, wv), lambda h: (0, OFF_GV // wv + h)),
            _once((T, LANE), lambda h: (0, 0)),
            pl.BlockSpec((LANE, wk), lambda h: (0, h)),
            pl.BlockSpec((1, wk), lambda h: (0, h))]


def gla_fwd(name, z, glr, gu, gb, o_prev):
    T = z.shape[0]
    nc = T // CHUNK
    wv = GLA_PAIR * GLA_DV

    def body(q_ref, k_ref, v_ref, glr_ref, gu_ref, gb_ref, prev_ref, o_ref, s_all_ref, *S):
        del prev_ref
        gu_b, gb_v = gu_ref[...].astype(BF16), gb_ref[...]
        ri = lax.broadcasted_iota(jnp.int32, (CHUNK, CHUNK), 0)
        ci = lax.broadcasted_iota(jnp.int32, (CHUNK, CHUNK), 1)
        tril = ri >= ci
        trilf = tril.astype(F32)
        for s_ref in S:
            s_ref[...] = jnp.zeros_like(s_ref)

        def step(c, carry):
            rows = pl.ds(pl.multiple_of(c * CHUNK, CHUNK), CHUNK)
            heads = range(GLA_PAIR)
            ch = [_gla_chunk(q_ref, k_ref, v_ref, glr_ref, gu_b, gb_v, rows, hh, trilf) for hh in heads]
            a = [jnp.where(tril, _dot_nt(ch[hh][4], ch[hh][5]), 0.0) for hh in heads]
            s_prev = [S[hh][...] for hh in heads]
            intra = [_dot(a[hh], ch[hh][7]) for hh in heads]
            inter = [_dot(ch[hh][4], s_prev[hh]) for hh in heads]
            added = [_dot_tn(ch[hh][6], ch[hh][7]) for hh in heads]
            for hh in heads:
                o_ref[rows, hh * GLA_DV:(hh + 1) * GLA_DV] = intra[hh] + inter[hh]
                s_all_ref[hh, c] = s_prev[hh]
                S[hh][...] = s_prev[hh] * _state_decay(ch[hh][2]) + added[hh]
            return carry

        lax.fori_loop(0, nc, step, 0)

    n_in = 6
    return pl.pallas_call(
        body, name=name, grid=(GLA_HEADS // GLA_PAIR,),
        in_specs=_gla_specs(T) + [pl.BlockSpec(memory_space=pl.ANY)],
        out_specs=[pl.BlockSpec((T, wv), lambda h: (0, RET_V // wv + h)),
                   pl.BlockSpec((GLA_PAIR, nc, GLA_DK, GLA_DV), lambda h: (h, 0, 0, 0))],
        out_shape=[jax.ShapeDtypeStruct(o_prev.shape, F32),
                   jax.ShapeDtypeStruct((GLA_HEADS, nc, GLA_DK, GLA_DV), F32)],
        scratch_shapes=[pltpu.VMEM((GLA_DK, GLA_DV), F32)] * GLA_PAIR,
        input_output_aliases={n_in: 0},
        compiler_params=_params("arbitrary"))(z, z, z, glr, gu, gb, o_prev)


def gla_bwd(name, z, glr, gu, gb, do, states):
    T = z.shape[0]
    nc = T // CHUNK

    def body(q_ref, k_ref, v_ref, glr_ref, gu_ref, gb_ref, do_ref, s_all,
             dq_ref, dk_ref, dv_ref, dglr_ref, dgu_ref, dgb_ref, dS):
        gu_b, gb_v = gu_ref[...].astype(BF16), gb_ref[...]
        ri = lax.broadcasted_iota(jnp.int32, (CHUNK, CHUNK), 0)
        ci = lax.broadcasted_iota(jnp.int32, (CHUNK, CHUNK), 1)
        tril = ri >= ci
        trilf = tril.astype(F32)
        triuf = (ri <= ci).astype(F32)
        last_row = lax.broadcasted_iota(jnp.int32, (CHUNK, GLA_DK), 0) == CHUNK - 1
        ones8 = jnp.ones((8, GLA_DV), F32)

        heads = range(GLA_PAIR)

        dS[...] = jnp.zeros_like(dS)
        dgu_ref[...] = jnp.zeros_like(dgu_ref)
        dgb_ref[...] = jnp.zeros_like(dgb_ref)

        def bstep(i, carry):
            c = nc - 1 - i
            rows = pl.ds(pl.multiple_of(c * CHUNK, CHUNK), CHUNK)
            glr_c = glr_ref[rows, :]
            cks = [slice(hh * GLA_DK, (hh + 1) * GLA_DK) for hh in heads]
            cvs = [slice(hh * GLA_DV, (hh + 1) * GLA_DV) for hh in heads]
            ch = [_gla_chunk(q_ref, k_ref, v_ref, glr_ref, gu_b, gb_v, rows, hh, trilf) for hh in heads]
            zg, cum, last, ecum, qt, kt, kh, v = [[ch[hh][j] for hh in heads] for j in range(8)]
            s_prev = [s_all[hh, c] for hh in heads]
            ds_new = [dS[hh] for hh in heads]
            dout = [do_ref[rows, cvs[hh]].astype(BF16) for hh in heads]
            a = [jnp.where(tril, _dot_nt(qt[hh], kt[hh]), 0.0) for hh in heads]
            da = [jnp.where(tril, _dot_nt(dout[hh], v[hh]), 0.0) for hh in heads]
            dv_a = [_dot_tn(a[hh], dout[hh]) for hh in heads]
            dv_b = [_dot(kh[hh], ds_new[hh]) for hh in heads]
            dqt_a = [_dot(da[hh], kt[hh]) for hh in heads]
            dqt_b = [_dot_nt(dout[hh], s_prev[hh]) for hh in heads]
            dkt = [_dot_tn(da[hh], qt[hh]) for hh in heads]
            dkh = [_dot_nt(v[hh], ds_new[hh]) for hh in heads]
            ds_add = [_dot_tn(qt[hh], dout[hh]) for hh in heads]
            rs = [lax.dot_general(ones8, ds_new[hh] * s_prev[hh], _NT, precision=HIGHEST, preferred_element_type=F32)
                  for hh in heads]
            dcum = []
            for hh in heads:
                dv_ref[rows, cvs[hh]] = (dv_a[hh] + dv_b[hh]).astype(dv_ref.dtype)
                dS[hh] = ds_new[hh] * _state_decay(last[hh]) + ds_add[hh]
                dqt = dqt_a[hh] + dqt_b[hh]
                dq_ref[rows, cks[hh]] = (dqt * ecum[hh] * (GLA_DK ** -0.5)).astype(dq_ref.dtype)
                dk_ref[rows, cks[hh]] = (dkt[hh] * jnp.exp(-cum[hh])
                                         + dkh[hh] * jnp.exp(last[hh] - cum[hh])).astype(dk_ref.dtype)
                dkh_kh = dkh[hh] * kh[hh]
                dlast = (jnp.sum(dkh_kh, axis=0, keepdims=True)
                         + jnp.exp(last[hh]) * (jnp.sum(rs[hh], axis=0, keepdims=True) * 0.125))
                dcum.append(dqt * qt[hh] - dkt[hh] * kt[hh] - dkh_kh + jnp.where(last_row, dlast, 0.0))
            dla = [jnp.dot(triuf, dcum[hh], precision=HIGHEST, preferred_element_type=F32) for hh in heads]
            dzg = [dla[hh] * (1.0 / GLA_GATE_NORM) * _sigmoid(-zg[hh]) for hh in heads]
            dglr = [_dot_nt(dzg[hh], gu_b[:, cks[hh]]) for hh in heads]
            dgu = [_dot_tn(glr_c, dzg[hh]) for hh in heads]
            for hh in heads:
                dglr_ref[hh, rows, :] = dglr[hh]
                dgu_ref[:, cks[hh]] += dgu[hh]
                dgb_ref[:, cks[hh]] += jnp.sum(dzg[hh], axis=0, keepdims=True)
            return carry

        lax.fori_loop(0, nc, bstep, 0)

    wk, wv = GLA_PAIR * GLA_DK, GLA_PAIR * GLA_DV
    return pl.pallas_call(
        body, name=name, grid=(GLA_HEADS // GLA_PAIR,),
        in_specs=_gla_specs(T) + [_once((T, wv), lambda h: (0, RET_V // wv + h)),
                                  _once((GLA_PAIR, nc, GLA_DK, GLA_DV), lambda h: (h, 0, 0, 0))],
        out_specs=[pl.BlockSpec((T, wk), lambda h: (0, h)), pl.BlockSpec((T, wk), lambda h: (0, h)),
                   pl.BlockSpec((T, wv), lambda h: (0, h)),
                   pl.BlockSpec((GLA_PAIR, T, LANE), lambda h: (h, 0, 0)),
                   pl.BlockSpec((LANE, wk), lambda h: (0, h)), pl.BlockSpec((1, wk), lambda h: (0, h))],
        out_shape=[jax.ShapeDtypeStruct((T, GLA_QK), BF16), jax.ShapeDtypeStruct((T, GLA_QK), BF16),
                   jax.ShapeDtypeStruct((T, GLA_V), BF16), jax.ShapeDtypeStruct((GLA_HEADS, T, LANE), F32),
                   jax.ShapeDtypeStruct((LANE, GLA_QK), F32), jax.ShapeDtypeStruct((1, GLA_QK), F32)],
        scratch_shapes=[pltpu.VMEM((GLA_PAIR, GLA_DK, GLA_DV), F32)],
        compiler_params=_params("arbitrary"))(z, z, z, glr, gu, gb, do, states)


HN_HEADS = RET_HEADS + GLA_HEADS
HN_W = RET_DV


def _gate_col(h):
    return jnp.where(h < RET_HEADS, OFF_RG // HN_W + h, OFF_GG // HN_W + h - RET_HEADS)


def headnorm_fwd(name, oraw, z, w):
    T = oraw.shape[0]
    tr = _pick(T, _TILES)

    def body(o_ref, g_ref, w_ref, y_ref):
        y_ref[...] = (_rms(o_ref[...], w_ref[...]) * _silu_and_grad(g_ref[...])[0]).astype(y_ref.dtype)

    return pl.pallas_call(
        body, name=name, grid=(HN_HEADS, T // tr),
        in_specs=[pl.BlockSpec((tr, HN_W), lambda h, i: (i, h)),
                  pl.BlockSpec((tr, HN_W), lambda h, i: (i, _gate_col(h))),
                  pl.BlockSpec((1, HN_W), lambda h, i: (0, h))],
        out_specs=pl.BlockSpec((tr, HN_W), lambda h, i: (i, h)),
        out_shape=jax.ShapeDtypeStruct((T, HN_HEADS * HN_W), BF16),
        compiler_params=_params("arbitrary", "arbitrary"))(oraw, z, w)


def headnorm_bwd(name, oraw, z, w, dy):
    T = oraw.shape[0]
    tr = _pick(T, _TILES)

    def body(o_ref, g_ref, w_ref, dy_ref, do_ref, dg_ref, dw_ref):
        o, wv, dyv = o_ref[...], w_ref[...], dy_ref[...].astype(F32)
        silu, dsilu = _silu_and_grad(g_ref[...])
        n = _rms(o, wv)
        dg_ref[...] = (dyv * n * dsilu).astype(dg_ref.dtype)
        dx, dw = _rms_bwd(o, wv, dyv * silu)
        do_ref[...] = dx
        _accumulate(dw_ref, dw, pl.program_id(1) == 0)

    blk = pl.BlockSpec((tr, HN_W), lambda h, i: (i, h))
    return pl.pallas_call(
        body, name=name, grid=(HN_HEADS, T // tr),
        in_specs=[blk, pl.BlockSpec((tr, HN_W), lambda h, i: (i, _gate_col(h))),
                  pl.BlockSpec((1, HN_W), lambda h, i: (0, h)), blk],
        out_specs=[blk, blk, pl.BlockSpec((1, HN_W), lambda h, i: (0, h))],
        out_shape=[jax.ShapeDtypeStruct((T, HN_HEADS * HN_W), F32),
                   jax.ShapeDtypeStruct((T, HN_HEADS * HN_W), BF16),
                   jax.ShapeDtypeStruct((1, HN_HEADS * HN_W), F32)],
        compiler_params=_params("arbitrary", "arbitrary"))(oraw, z, w, dy)


N_MASKS = 4


def _check_mask_classes(T):
    for window, dilation in DILATED_BRANCHES[:-1]:
        assert window < (N_MASKS - 1) * BLK - (BLK - 1) and BLK % dilation == 0
    assert DILATED_BRANCHES[-1][0] >= T and BLK % DILATED_BRANCHES[-1][1] == 0


def _fill_masks(logm_ref):
    ri = lax.broadcasted_iota(jnp.int32, (BLK, BLK), 0)
    ci = lax.broadcasted_iota(jnp.int32, (BLK, BLK), 1)
    for d in range(N_MASKS):
        dt = d * BLK + ri - ci
        mult = jnp.zeros((BLK, BLK), F32)
        for window, dilation in DILATED_BRANCHES:
            hit = (dt >= 0) & (dt <= window) & ((dt & (dilation - 1)) == 0)
            mult = mult + hit.astype(F32)
        logm_ref[d] = jnp.where(mult > 0, jnp.log(jnp.maximum(mult, 1.0)), -1e30)


def _mask_row(ref, qi):
    return jnp.concatenate([ref[min(qi - kb, N_MASKS - 1)] for kb in range(qi + 1)], axis=1)


def attn_fwd(name, qkv):
    T = qkv.shape[0]
    D = qkv.shape[1] // 3
    dh = D // ATT_HEADS
    nq = T // BLK
    scale = dh ** -0.5

    _check_mask_classes(T)

    def body(q_ref, k_ref, v_ref, o_ref, lse_ref, logm_ref):
        @pl.when(pl.program_id(0) == 0)
        def _():
            _fill_masks(logm_ref)

        for q0 in range(0, nq, 2):
            qis = range(q0, min(q0 + 2, nq))
            rows = [slice(qi * BLK, (qi + 1) * BLK) for qi in qis]
            ns = [(qi + 1) * BLK for qi in qis]
            s = [_dot_nt(q_ref[r, :], k_ref[0:n, :]) for r, n in zip(rows, ns)]
            s = [x * scale + _mask_row(logm_ref, qi) for x, qi in zip(s, qis)]
            m = [jnp.max(x, axis=-1, keepdims=True) for x in s]
            p = [jnp.exp(x - mx) for x, mx in zip(s, m)]
            l = [jnp.sum(x, axis=-1, keepdims=True) for x in p]
            pv = [_dot(x, v_ref[0:n, :]) for x, n in zip(p, ns)]
            for r, x, lx, mx in zip(rows, pv, l, m):
                o_ref[r, :] = (x / lx).astype(o_ref.dtype)
                lse_ref[r, :] = jnp.broadcast_to(mx + jnp.log(lx), (BLK, LANE))

    return pl.pallas_call(
        body, name=name, grid=(ATT_HEADS,),
        in_specs=[pl.BlockSpec((T, dh), lambda h: (0, h)),
                  pl.BlockSpec((T, dh), lambda h: (0, ATT_HEADS + h)),
                  pl.BlockSpec((T, dh), lambda h: (0, 2 * ATT_HEADS + h))],
        out_specs=[pl.BlockSpec((T, dh), lambda h: (0, h)),
                   pl.BlockSpec((None, T, LANE), lambda h: (h, 0, 0))],
        out_shape=[jax.ShapeDtypeStruct((T, D), BF16), jax.ShapeDtypeStruct((ATT_HEADS, T, LANE), F32)],
        scratch_shapes=[pltpu.VMEM((N_MASKS, BLK, BLK), F32)],
        compiler_params=_params("arbitrary"))(qkv, qkv, qkv)


def attn_bwd(name, qkv, o, lse, do):
    T = qkv.shape[0]
    D = qkv.shape[1] // 3
    dh = D // ATT_HEADS
    nq = T // BLK
    scale = dh ** -0.5

    _check_mask_classes(T)

    def body(q_ref, k_ref, v_ref, o_ref, lse_ref, do_ref, dq_ref, dk_ref, dv_ref, dk_acc, dv_acc, logm_ref):
        @pl.when(pl.program_id(0) == 0)
        def _():
            _fill_masks(logm_ref)

        dk_acc[...] = jnp.zeros_like(dk_acc)
        dv_acc[...] = jnp.zeros_like(dv_acc)
        for qi in range(nq):
            rows, n = slice(qi * BLK, (qi + 1) * BLK), (qi + 1) * BLK
            q, dout = q_ref[rows, :], do_ref[rows, :]
            kk, vv = k_ref[0:n, :], v_ref[0:n, :]
            delta = jnp.sum(dout.astype(F32) * o_ref[rows, :].astype(F32), axis=-1, keepdims=True)
            lse = jnp.max(lse_ref[rows, :], axis=-1, keepdims=True)
            p = jnp.exp(_dot_nt(q, kk) * scale + _mask_row(logm_ref, qi) - lse)
            ds = (p * (_dot_nt(dout, vv) - delta) * scale).astype(BF16)
            dq_ref[rows, :] = _dot(ds, kk).astype(dq_ref.dtype)
            dk_acc[0:n, :] += _dot_tn(ds, q)
            dv_acc[0:n, :] += _dot_tn(p, dout)
        dk_ref[...] = dk_acc[...].astype(dk_ref.dtype)
        dv_ref[...] = dv_acc[...].astype(dv_ref.dtype)

    full = pl.BlockSpec((T, dh), lambda h: (0, h))
    return pl.pallas_call(
        body, name=name, grid=(ATT_HEADS,),
        in_specs=[full, pl.BlockSpec((T, dh), lambda h: (0, ATT_HEADS + h)),
                  pl.BlockSpec((T, dh), lambda h: (0, 2 * ATT_HEADS + h)),
                  full, pl.BlockSpec((None, T, LANE), lambda h: (h, 0, 0)), full],
        out_specs=[full, full, full],
        out_shape=[jax.ShapeDtypeStruct((T, D), BF16)] * 3,
        scratch_shapes=[pltpu.VMEM((T, dh), F32), pltpu.VMEM((T, dh), F32), pltpu.VMEM((N_MASKS, BLK, BLK), F32)],
        compiler_params=_params("arbitrary"))(qkv, qkv, qkv, o, lse, do)


def _mesh_pos():
    mx, my, mc = lax.axis_index("x"), lax.axis_index("y"), lax.axis_index("c")
    return mx, my, mc, 4 * mx + 2 * my + mc


def _peer(k, mx, my, mc):
    px, py, pc = mx ^ (k >> 2), my ^ ((k >> 1) & 1), mc ^ (k & 1)
    return (px, py, pc), 4 * px + 2 * py + pc


_SIBLING = 1
_OTHER_CHIPS = (4, 2, 6)
N_CHIP = N_DEV // 2
_PLANS = {"gather": (2, N_DEV - 1), "to_chips": (2, 1 + len(_OTHER_CHIPS)), "pass_on": (1, len(_OTHER_CHIPS)),
          "halves": (2, N_CHIP), "chip_sums": (2, len(_OTHER_CHIPS))}


def _copies(kind, items, send_sems, recv_sems):
    mx, my, mc, me = _mesh_pos()
    out = []

    def add(n, src, dst, peer):
        out.append(pltpu.make_async_remote_copy(
            src_ref=src, dst_ref=dst, send_sem=send_sems.at[n], recv_sem=recv_sems.at[n],
            device_id=peer, device_id_type=pl.DeviceIdType.MESH))

    per_item = _PLANS[kind][1]
    sibling = _peer(_SIBLING, mx, my, mc)[0]
    for i, refs in enumerate(items):
        n = i * per_item
        if kind == "gather":
            for k in range(1, N_DEV):
                add(n + k - 1, refs[0], refs[1].at[me], _peer(k, mx, my, mc)[0])
        elif kind == "to_chips":
            rows = refs[0].shape[0]
            dst = refs[1].at[me] if rows == refs[1].shape[1] else refs[1].at[me, pl.ds(0, rows)]
            for j, k in enumerate((_SIBLING,) + _OTHER_CHIPS):
                add(n + j, refs[0], dst, _peer(k, mx, my, mc)[0])
        elif kind == "pass_on":
            for j, k in enumerate(_OTHER_CHIPS):
                add(n + j, refs[0].at[me ^ k], refs[0].at[me ^ k], sibling)
        elif kind == "halves":
            for chip in range(N_CHIP):
                add(n + chip, refs[0].at[2 * chip + 1 - mc], refs[1].at[chip], sibling)
        else:
            for j, k in enumerate(_OTHER_CHIPS):
                peer, to = _peer(k, mx, my, mc)
                add(n + j, refs[0].at[to // 2], refs[1].at[me // 2], peer)
    return out


_HBM = pl.BlockSpec(memory_space=pltpu.HBM)
_SEM = pl.BlockSpec(memory_space=pltpu.SEMAPHORE)
_DATAFLOW = pltpu.SideEffectType.DATAFLOW_SIDE_EFFECTING


def exchange_call(name, waits, starts, deps=()):
    bufs, slot_of = [], {}

    def slots(items):
        out = []
        for item in items:
            for b in item:
                if id(b) not in slot_of:
                    slot_of[id(b)] = len(bufs)
                    bufs.append(b)
            out.append(tuple(slot_of[id(b)] for b in item))
        return out

    wait_plan = [(kind, slots(handle[0])) for kind, handle in waits]
    start_plan = [(kind, slots(items)) for kind, items in starts]
    wait_sems = [s for _, handle in waits for s in handle[1:]]
    n_buf, n_ws, n_start = len(bufs), len(wait_sems), len(starts)

    def body(*refs):
        buf_refs, sems_in = refs[:n_buf], refs[n_buf:n_buf + n_ws]
        outs = refs[n_buf + n_ws + len(deps):]
        pick = lambda plan: [tuple(buf_refs[s] for s in item) for item in plan]
        for wi, (kind, plan) in enumerate(wait_plan):
            copies = _copies(kind, pick(plan), sems_in[2 * wi], sems_in[2 * wi + 1])
            for cp in copies:
                cp.wait_send()
            for cp in copies:
                cp.wait_recv()
        for si, (kind, plan) in enumerate(start_plan):
            for cp in _copies(kind, pick(plan), outs[2 * si], outs[2 * si + 1]):
                cp.start()
        outs[-1][...] = jnp.zeros_like(outs[-1])

    hbm_bufs = [pltpu.with_memory_space_constraint(b, pltpu.HBM) for b in bufs]
    sem_shapes = []
    for kind, plan in start_plan:
        sem_shapes += [pltpu.SemaphoreType.DMA((len(plan) * _PLANS[kind][1],))] * 2
    outs = pl.pallas_call(
        body, name=name,
        out_shape=sem_shapes + [pltpu.HBM(b.shape, b.dtype) for b in bufs] + [jax.ShapeDtypeStruct((8, LANE), F32)],
        in_specs=[_HBM] * n_buf + [_SEM] * n_ws + [_ANY] * len(deps),
        out_specs=[_SEM] * (2 * n_start) + [_HBM] * n_buf + [pl.BlockSpec(memory_space=pltpu.VMEM)],
        input_output_aliases={i: 2 * n_start + i for i in range(n_buf)},
        compiler_params=pltpu.CompilerParams(has_side_effects=_DATAFLOW))(*hbm_bufs, *wait_sems, *deps)
    sems, thru, token = outs[:2 * n_start], outs[2 * n_start:-1], outs[-1]
    through = lambda plan: [tuple(thru[s] for s in item) for item in plan]
    waited = [through(plan) for _, plan in wait_plan]
    handles = [(through(plan), sems[2 * si], sems[2 * si + 1]) for si, (_, plan) in enumerate(start_plan)]
    return waited, handles, token


def gather_small(name, a, deps=()):
    def body(a_ref, *rest):
        o_ref, send_sems, recv_sems, local_sem = rest[len(deps):]
        me = _mesh_pos()[3]
        own = pltpu.make_async_copy(a_ref, o_ref.at[me], local_sem)
        own.start()
        copies = _copies("gather", [(a_ref, o_ref)], send_sems, recv_sems)
        for cp in copies:
            cp.start()
        for cp in copies:
            cp.wait_recv()
        for cp in copies:
            cp.wait_send()
        own.wait()

    return pl.pallas_call(
        body, name=name, in_specs=[_ANY] * (1 + len(deps)), out_specs=_ANY,
        out_shape=jax.ShapeDtypeStruct((N_DEV,) + a.shape, a.dtype),
        scratch_shapes=[pltpu.SemaphoreType.DMA((N_DEV - 1,)), pltpu.SemaphoreType.DMA((N_DEV - 1,)),
                        pltpu.SemaphoreType.DMA],
        compiler_params=pltpu.CompilerParams(has_side_effects=True))(a, *deps)


def _adamw_math(w, g, m, v):
    m2 = ADAM_B1 * m + (1.0 - ADAM_B1) * g
    v2 = ADAM_B2 * v + (1.0 - ADAM_B2) * (g * g)
    m_hat = m2 / (1.0 - ADAM_B1 ** ADAM_STEP)
    v_hat = v2 / (1.0 - ADAM_B2 ** ADAM_STEP)
    delta = -ADAM_LR * (m_hat / (jnp.sqrt(v_hat) + ADAM_EPS) + ADAM_WD * w)
    return delta, m2, v2


def chip_sum(name, a, half):
    _, r, c = a.shape
    tr = r
    chip = 2 * lax.axis_index("x") + lax.axis_index("y")
    where = jnp.stack([lax.axis_index("c"), chip ^ 1, chip ^ 2, chip ^ 3]).astype(jnp.int32)

    def body(where_ref, a_ref, h_ref, o_ref):
        del where_ref
        o_ref[...] = (a_ref[...].astype(F32) + h_ref[...].astype(F32)).astype(o_ref.dtype)

    blk = pl.BlockSpec((None, tr, c), lambda g, i, where: (where[1 + g], i, 0))
    grid_spec = pltpu.PrefetchScalarGridSpec(
        num_scalar_prefetch=1, grid=(N_CHIP - 1, r // tr),
        in_specs=[pl.BlockSpec((None, None, tr, c), lambda g, i, where: (where[1 + g], where[0], i, 0)), blk],
        out_specs=blk)
    return pl.pallas_call(
        body, name=name, grid_spec=grid_spec, out_shape=jax.ShapeDtypeStruct((N_CHIP, r, c), BF16),
        compiler_params=_params("parallel", "parallel"))(where, a.reshape(N_CHIP, 2, r, c), half)


def adamw(name, w, m, v, l, land, a, half, prev=None):
    L, r, c = w.shape
    cp = land.shape[2]
    tr = _pick(r, (256, 176, 128, 64, 32, 16, 8))

    def body(w_ref, m_ref, v_ref, land_ref, a_ref, half_ref, *rest):
        g_ref, d_ref, m2_ref, v2_ref = rest[-4:]
        chip = _mesh_pos()[3] // 2
        mine = a_ref[:, pl.ds(0, c)].astype(F32) + half_ref[:, pl.ds(0, c)].astype(F32)
        g = None
        for s in range(N_CHIP):
            part = jnp.where(chip == s, mine, land_ref[s, :, pl.ds(0, c)].astype(F32))
            g = part if g is None else g + part
        delta, m2, v2 = _adamw_math(w_ref[...], g, m_ref[...], v_ref[...])
        g_ref[...] = g
        d_ref[...] = delta
        m2_ref[...] = m2
        v2_ref[...] = v2

    blk = pl.BlockSpec((None, tr, c), lambda i: (l, i, 0))
    shape = jax.ShapeDtypeStruct((L, r, c), F32)
    extra = [] if prev is None else list(prev)
    return pl.pallas_call(
        body, name=name, grid=(r // tr,),
        in_specs=[blk, blk, blk, pl.BlockSpec((N_CHIP, tr, cp), lambda i: (0, i, 0)),
                  pl.BlockSpec((None, tr, cp), lambda i: (_mesh_pos()[3], i, 0)),
                  pl.BlockSpec((None, tr, cp), lambda i: (_mesh_pos()[3] // 2, i, 0))] + [_ANY] * len(extra),
        out_specs=[blk] * 4, out_shape=[shape] * 4,
        input_output_aliases={6 + k: k for k in range(len(extra))},
        compiler_params=_params("parallel"))(w, m, v, land, a, half, *extra)


def adamw_columns(name, w, m, v, land, a, half):
    r, _, D = w.shape
    tc = _pick(D, (256, 128))

    def body(w_ref, m_ref, v_ref, land_ref, a_ref, half_ref, g_ref, d_ref, m2_ref, v2_ref):
        chip = _mesh_pos()[3] // 2
        mine = a_ref[...].astype(F32) + half_ref[...].astype(F32)
        g = None
        for s in range(N_CHIP):
            part = jnp.where(chip == s, mine, land_ref[s].astype(F32))
            g = part if g is None else g + part
        flat = lambda ref: ref[...].reshape(r, tc)
        delta, m2, v2 = _adamw_math(flat(w_ref), g, flat(m_ref), flat(v_ref))
        for ref, val in ((g_ref, g), (d_ref, delta), (m2_ref, m2), (v2_ref, v2)):
            ref[...] = val.reshape(r, 1, tc)

    blk = pl.BlockSpec((r, 1, tc), lambda i: (0, 0, i))
    shape = jax.ShapeDtypeStruct((r, 1, D), F32)
    return pl.pallas_call(
        body, name=name, grid=(D // tc,),
        in_specs=[blk, blk, blk, pl.BlockSpec((N_CHIP, r, tc), lambda i: (0, 0, i)),
                  pl.BlockSpec((None, r, tc), lambda i: (_mesh_pos()[3], 0, i)),
                  pl.BlockSpec((None, r, tc), lambda i: (_mesh_pos()[3] // 2, 0, i))],
        out_specs=[blk] * 4, out_shape=[shape] * 4,
        compiler_params=_params("parallel"))(w, m, v, land, a, half)


def adamw_small(name, w, m, v, parts):
    n = w.shape[1]

    def body(w_ref, m_ref, v_ref, p_ref, g_ref, d_ref, m2_ref, v2_ref):
        g = p_ref[0:1, :]
        for s in range(1, N_DEV):
            g = g + p_ref[s:s + 1, :]
        delta, m2, v2 = _adamw_math(w_ref[...], g, m_ref[...], v_ref[...])
        g_ref[...] = g
        d_ref[...] = delta
        m2_ref[...] = m2
        v2_ref[...] = v2

    shape = jax.ShapeDtypeStruct((1, n), F32)
    return pl.pallas_call(body, name=name, out_shape=[shape] * 4,
                          compiler_params=pltpu.CompilerParams(vmem_limit_bytes=VMEM_LIMIT_BYTES))(w, m, v, parts)


def _rope_tables(positions):
    half = RET_DK // 2
    inv_freq = 1.0 / jnp.power(RET_THETA_BASE, jnp.linspace(0.0, 1.0, half, dtype=F32))
    ang = positions.astype(F32)[:, None] * inv_freq
    cos, sin = jnp.cos(ang), jnp.sin(ang)
    cosf = jnp.repeat(cos, 2, axis=-1)
    sins = jnp.stack([-sin, sin], axis=-1).reshape(cosf.shape)
    return cosf, sins


def _pad_to(a, axis, size):
    pad = [(0, 0)] * a.ndim
    pad[axis] = (0, size - a.shape[axis])
    return jnp.pad(a, pad)


def _round_up(n, m):
    return -(-n // m) * m


def kernel(x, p, positions, attn_norm_w, ffn_norm_w, ple_norm_w, final_norm_w, ab_w_in, ab_gla_gate_up, ab_gla_gate_b, ab_ret_norm_w, ab_gla_norm_w, ab_w_out, c_w_qkv, c_w_out, ffn_w_gate, ffn_w_up, ffn_w_down, ple_w_proj, ple_w_gate, loss_target, m_attn_norm_w, m_ffn_norm_w, m_ple_norm_w, m_final_norm_w, m_ab_w_in, m_ab_gla_gate_up, m_ab_gla_gate_b, m_ab_ret_norm_w, m_ab_gla_norm_w, m_ab_w_out, m_c_w_qkv, m_c_w_out, m_ffn_w_gate, m_ffn_w_up, m_ffn_w_down, m_ple_w_proj, m_ple_w_gate, v_attn_norm_w, v_ffn_norm_w, v_ple_norm_w, v_final_norm_w, v_ab_w_in, v_ab_gla_gate_up, v_ab_gla_gate_b, v_ab_ret_norm_w, v_ab_gla_norm_w, v_ab_w_out, v_c_w_qkv, v_c_w_out, v_ffn_w_gate, v_ffn_w_up, v_ffn_w_down, v_ple_w_proj, v_ple_w_gate):
    T, D = x.shape[1], x.shape[2]
    depth = attn_norm_w.shape[0]
    assert ab_w_in.shape[0] == 1 and c_w_qkv.shape[0] == 1 and depth == 2, "one even and one odd layer"
    me = 4 * lax.axis_index("x") + 2 * lax.axis_index("y") + lax.axis_index("c")
    in_shard = ab_w_in.shape[2]
    in_width = in_shard * N_DEV
    assert in_width == OFF_LR + GLA_GATE_RANK
    fs = ffn_w_gate.shape[2]
    fp = _round_up(fs, LANE)
    gu_cols = ab_gla_gate_up.shape[2]

    bf = lambda a: a.astype(BF16)
    tr_ = lambda a: jnp.swapaxes(a, -1, -2)
    wg_t, wu_t = tr_(ffn_w_gate), tr_(ffn_w_up)
    srcs = {"w_in": bf(tr_(ab_w_in[0]))}
    group_keys = [["w_in"], ["gu", "w_oab"], ["wg0", "wu0"], ["wd0", "wpg0", "wpp0"], ["w_qkv", "w_oc"],
                  ["wg1", "wu1"], ["wd1", "wpg1", "wpp1"]]
    G_IN, G_OUT, G_QKV = 0, 1, 4
    g_ffn = lambda layer: (2, 3) if layer == 0 else (5, 6)

    def landing(key):
        a = srcs[key]
        rows = fp if key[:2] in ("wg", "wu", "wd") else a.shape[0]
        buf = lax.empty((N_DEV, rows) + a.shape[1:], a.dtype)
        if rows > a.shape[0]:
            zeros = jnp.zeros((N_DEV, rows - a.shape[0]) + a.shape[1:], a.dtype)
            buf = lax.dynamic_update_slice(buf, zeros, (0, a.shape[0]) + (0,) * (a.ndim - 1))
        return lax.dynamic_update_slice(buf, a[None], (me,) + (0,) * a.ndim)

    _, chip_handles, gather_token = exchange_call(
        "gather_start_in", [], [("to_chips", [(srcs[k], landing(k)) for k in group_keys[G_IN]])])
    (gather_token, w_out_, gu_, w_qkv_, w_oc_, wg_, wu_, wd_, wpg_, wpp_) = lax.optimization_barrier(
        (gather_token, ab_w_out, ab_gla_gate_up, c_w_qkv, c_w_out, wg_t, wu_t, ffn_w_down, ple_w_gate, ple_w_proj))
    srcs.update(w_oab=bf(w_out_[0]), gu=gu_[0], w_qkv=bf(w_qkv_[0]), w_oc=bf(w_oc_[0]))
    for l in range(depth):
        srcs[f"wg{l}"] = bf(wg_[l])
        srcs[f"wu{l}"] = bf(wu_[l])
        srcs[f"wd{l}"] = bf(wd_[l])
        srcs[f"wpg{l}"] = bf(wpg_[l])
        srcs[f"wpp{l}"] = bf(wpp_[l])
    _, more, gather_token = exchange_call(
        "gather_start", [], [("to_chips", [(srcs[k], landing(k)) for k in keys]) for keys in group_keys[1:]],
        deps=(gather_token,))
    chip_handles = chip_handles + more
    weights = {}

    def gather_wait(gi, dep):
        lands = [(land,) for _, land in chip_handles[gi][0]]
        _, (passing,), _ = exchange_call(
            f"gather{gi}_pass", [("to_chips", chip_handles[gi])], [("pass_on", lands)], deps=(dep,))
        (complete,), _, _ = exchange_call(f"gather{gi}_done", [("pass_on", passing)], [])
        weights.update(zip(group_keys[gi], [land for (land,) in complete]))

    gb = ab_gla_gate_b
    hn_w = jnp.concatenate([ab_ret_norm_w, ab_gla_norm_w], axis=1)
    cosf, sins = _rope_tables(positions[0])
    p_bf = bf(p[:, 0])

    xs = x[0]
    saved = []
    for i in range(depth):
        nm = f"l{i}_"
        w_attn, w_ffn, w_ple = attn_norm_w[i:i + 1], ffn_norm_w[i:i + 1], ple_norm_w[i:i + 1]
        (xn,) = rowwise(nm + "norm_attn", lambda a, w: (_rms(a, w),), T, [("row", xs), ("full", w_attn)],
                        [("row", D, BF16)], deps=(gather_token,) if i == 0 else ())
        if i % 2 == 0:
            gather_wait(G_IN, xn)
            w_in_t = weights["w_in"].reshape(1, 1, in_width, D)
            w_lr_t = _pad_to(w_in_t[0, 0, OFF_LR:], 0, LANE).reshape(1, 1, LANE, D)
            z = mmt_fwd(nm + "mm_in", xn, w_in_t, 0, F32, n=OFF_LR)
            glr = mmt_fwd(nm + "mm_lr", xn, w_lr_t, 0, F32)
            oraw = retention_fwd(nm + "ret_fwd", z, cosf, sins, RET_V + GLA_V)
            gather_wait(G_OUT, oraw)
            w_oab = weights["w_oab"].reshape(1, 1, D, D)
            gu_full = _pad_to(weights["gu"].transpose(1, 0, 2).reshape(GLA_GATE_RANK, GLA_QK), 0, LANE)
            oraw, gla_states = gla_fwd(nm + "gla_fwd", z, glr, gu_full, gb, oraw)
            o = headnorm_fwd(nm + "headnorm_fwd", oraw, z, hn_w)
            h1, hn = mm_add_norm(nm + "mm_out", o, w_oab, xs, w_ffn)
            mixer_saved = (z, glr, oraw, o, gla_states)
        else:
            gather_wait(G_QKV, xn)
            w_qkv = weights["w_qkv"].reshape((1,) + weights["w_qkv"].shape)
            w_oc = weights["w_oc"].reshape(1, 1, D, D)
            qkv = mm_nn(nm + "mm_qkv", xn, w_qkv, 0, BF16)
            o, lse = attn_fwd(nm + "attn_fwd", qkv)
            h1, hn = mm_add_norm(nm + "mm_out", o, w_oc, xs, w_ffn)
            mixer_saved = (qkv, o, lse)
        gather_wait(g_ffn(i)[0], hn)
        wg = weights[f"wg{i}"].reshape(1, N_DEV, fp, D)
        wu = weights[f"wu{i}"].reshape(1, N_DEV, fp, D)
        dup, dgate, act = ffn_gate_up(nm + "ffn_gate_up", hn, wg, wu)
        gather_wait(g_ffn(i)[1], act)
        wd = weights[f"wd{i}"].reshape(1, 1, N_DEV * fp, D)
        wpg = weights[f"wpg{i}"].reshape(1, 1, D, D)
        wpp = weights[f"wpp{i}"].reshape((1,) + weights[f"wpp{i}"].shape)
        h2, pn = mm_add_norm(nm + "mm_down", act, wd, h1, w_ple)
        x_next, s, e = ple_fwd(nm + "ple", pn, wpg, p_bf[i], wpp, h2)
        mixer_w = (w_in_t, w_lr_t, w_oab, gu_full) if i % 2 == 0 else (w_qkv, w_oc)
        saved.append((xs, xn, mixer_saved, mixer_w, (wg, wu, wd, wpg), h1, hn, dup, dgate, act, h2, pn, s, e))
        xs = x_next

    def ple_bwd(d, sv, ev):
        gate = _sigmoid(sv)
        return d * gate, d * ev * gate * (1.0 - gate)

    def loss_fn(a, w, t, sv, ev):
        diff = _rms(a, w) - t
        dx, dw = _rms_bwd(a, w, diff * (1.0 / D))
        part = 0.5 * jnp.sum(jnp.mean(diff * diff, axis=-1, keepdims=True), axis=0, keepdims=True)
        return (dx, dw, jnp.broadcast_to(part, (1, LANE))) + ple_bwd(dx, sv, ev)

    dx, d_final_w, loss_part, de_last, ds_last = rowwise(
        "loss_head", loss_fn, T,
        [("row", xs), ("full", final_norm_w[None, :]), ("row", loss_target[0]), ("row", saved[-1][-2]), ("row", saved[-1][-1])],
        [("row", D, F32), ("acc", D), ("acc", LANE), ("row", D, BF16), ("row", D, BF16)])
    loss = lax.psum(loss_part[0, 0], ("x", "y", "c"))

    grads = {}
    on_chip = []
    scatters = []

    def scatter_start(name, keys, deps=()):
        waits = [("halves", on_chip[0][1])] if on_chip else []
        starts = [("halves", [(grads[k], lax.empty((N_CHIP,) + grads[k].shape[1:], BF16)) for k in keys])] if keys else []
        waited, handles, token = exchange_call(name, waits, starts, deps=deps)
        if on_chip:
            done_keys, _ = on_chip.pop()
            sums = [chip_sum(f"{name}_sum{j}", a, half) for j, (a, half) in enumerate(waited[0])]
            _, (handle,), token = exchange_call(
                name + "_chips", [], [("chip_sums", [(cs, lax.empty(cs.shape, BF16)) for cs in sums])])
            scatters.append((done_keys, handle, waited[0]))
        if keys:
            on_chip.append((keys, handles[0]))
        return token

    d_attn_w, d_ffn_w, d_ple_w = [None] * depth, [None] * depth, [None] * depth
    for i in reversed(range(depth)):
        nm = f"l{i}_b_"
        xs_i, xn, mixer_saved, mixer_w, (wg, wu, wd, wpg), h1, hn, dup, dgate, act, h2, pn, s, e = saved[i]
        w_attn, w_ffn, w_ple = attn_norm_w[i:i + 1], ffn_norm_w[i:i + 1], ple_norm_w[i:i + 1]

        if i == depth - 1:
            de, ds, first_deps = de_last, ds_last, (loss.reshape(1, 1),)
        else:
            de, ds = rowwise(nm + "ple_out", ple_bwd, T, [("row", dx), ("row", s), ("row", e)],
                             [("row", D, BF16), ("row", D, BF16)])
            first_deps = ()
        grads[("ple_w_proj", i)] = mm_tn(nm + "mm_ple_proj_w", p_bf[i], de, N_DEV, BF16, deps=first_deps)
        grads[("ple_w_gate", i)] = mm_tn(nm + "mm_ple_gate_w", pn, ds, 1, BF16).reshape(N_DEV, D // N_DEV, D)
        dpn = mm_nt(nm + "mm_ple_gate_x", ds, wpg, 0, F32)

        def norm_bwd_add(a, w, dn, dres):
            dxx, dw = _rms_bwd(a, w, dn)
            tot = dres + dxx
            return tot, tot, dw

        dh2, dh2_bf, d_ple_w[i] = rowwise(nm + "norm_ple", norm_bwd_add, T,
                                          [("row", h2), ("full", w_ple), ("row", dpn), ("row", dx)],
                                          [("row", D, F32), ("row", D, BF16), ("acc", D)])
        grads[("ffn_w_down", i)] = mm_tn(nm + "mm_down_w", act, dh2_bf, 1, BF16).reshape(N_DEV, fp, D)
        token = scatter_start(nm + "scatter_ple_down", [("ple_w_proj", i), ("ple_w_gate", i), ("ffn_w_down", i)])
        dg, du = ffn_down_bwd(nm + "ffn_down_x", dh2_bf, wd, dup, dgate, deps=(token,))
        grads[("ffn_w_gate", i)] = mmt_dw(nm + "mm_gate_w", dg, hn, N_DEV, BF16)
        grads[("ffn_w_up", i)] = mmt_dw(nm + "mm_up_w", du, hn, N_DEV, BF16)
        token = scatter_start(nm + "scatter_gate_up", [("ffn_w_gate", i), ("ffn_w_up", i)])
        dhn = mmt_dx_pair(nm + "mm_gate_up_x", dg, wg, du, wu, F32, deps=(token,))
        dh1, dh1_bf, d_ffn_w[i] = rowwise(nm + "norm_ffn", norm_bwd_add, T,
                                          [("row", h1), ("full", w_ffn), ("row", dhn), ("row", dh2)],
                                          [("row", D, F32), ("row", D, BF16), ("acc", D)])
        if i % 2 == 0:
            z, glr, oraw, o, gla_states = mixer_saved
            w_in_t, w_lr_t, w_oab, gu_full = mixer_w
            grads[("ab_w_out", 0)] = mm_tn(nm + "mm_out_w", o, dh1_bf, 1, BF16).reshape(N_DEV, D // N_DEV, D)
            token = scatter_start(nm + "scatter_out", [("ab_w_out", 0)])
            do = mm_nt(nm + "mm_out_x", dh1_bf, w_oab, 0, F32, deps=(token,))
            d_oraw, d_gates, d_hn_w = headnorm_bwd(nm + "headnorm", oraw, z, hn_w, do)
            d_rq, d_rk, d_rv = retention_bwd(nm + "ret", z, cosf, sins, d_oraw)
            d_gq, d_gk, d_gv, d_glr4, d_gu, d_gb = gla_bwd(nm + "gla", z, glr, gu_full, gb, d_oraw, gla_states)
            dz = jnp.concatenate([d_rq, d_rk, d_rv, d_gates[:, :RET_V], d_gq, d_gk, d_gv, d_gates[:, RET_V:]], axis=1)
            (d_glr,) = rowwise(nm + "sum_lr", lambda *a: (a[0] + a[1] + a[2] + a[3],), T,
                               [("row", d_glr4[hh]) for hh in range(GLA_HEADS)], [("row", LANE, BF16)])
            dwt_in = mmt_dw(nm + "mm_in_w", dz, xn, 1, BF16, rows=in_width)
            dwt_in = mmt_dw_rows(nm + "mm_lr_w", d_glr, xn, dwt_in, OFF_LR, GLA_GATE_RANK)
            grads[("ab_w_in", 0)] = dwt_in.reshape(N_DEV, in_shard, D)
            token = scatter_start(nm + "scatter_in", [("ab_w_in", 0)])
            dxn_a = mmt_dx_wide(nm + "mm_in_x", dz, w_in_t, F32, n=OFF_LR, deps=(token,))
            token = scatter_start(nm + "scatter_in_on", [], deps=(dxn_a,))
            dxn_b = mmt_dx(nm + "mm_lr_x", d_glr, w_lr_t, 0, F32, deps=(token,))
        else:
            qkv, o, lse = mixer_saved
            w_qkv, w_oc = mixer_w
            grads[("c_w_out", 0)] = mm_tn(nm + "mm_out_w", o, dh1_bf, 1, BF16).reshape(N_DEV, D // N_DEV, D)
            do = mm_nt(nm + "mm_out_x", dh1_bf, w_oc, 0, BF16)
            dq, dk, dv = attn_bwd(nm + "attn", qkv, o, lse, do)
            dqkv = jnp.concatenate([dq, dk, dv], axis=1)
            grads[("c_w_qkv", 0)] = mm_tn(nm + "mm_qkv_w", xn, dqkv, N_DEV, BF16)
            token = scatter_start(nm + "scatter_attn", [("c_w_out", 0), ("c_w_qkv", 0)])
            dxn_a = mm_nt_wide(nm + "mm_qkv_x", dqkv, w_qkv, F32, deps=(token,))
            dxn_b = None
        dxn = [dxn_a] if dxn_b is None else [dxn_a, dxn_b]

        def norm_bwd_in(a, w, *rest):
            dxx, dw = _rms_bwd(a, w, sum(rest[1:-1], rest[0]))
            return rest[-1] + dxx, dw

        dx, d_attn_w[i] = rowwise(nm + "norm_attn", norm_bwd_in, T,
                                  [("row", xs_i), ("full", w_attn)] + [("row", d) for d in dxn] + [("row", dh1)],
                                  [("row", D, F32), ("acc", D)])

    small_names = ["attn_norm_w", "ffn_norm_w", "ple_norm_w", "final_norm_w", "ab_gla_gate_b", "ab_ret_norm_w",
                   "ab_gla_norm_w"]
    small_grads = [jnp.concatenate(d_attn_w, 0), jnp.concatenate(d_ffn_w, 0), jnp.concatenate(d_ple_w, 0), d_final_w[0],
                   d_gb, d_hn_w[:, :RET_V], d_hn_w[:, RET_V:]]
    small_w = [attn_norm_w, ffn_norm_w, ple_norm_w, final_norm_w, ab_gla_gate_b, ab_ret_norm_w, ab_gla_norm_w]
    small_m = [m_attn_norm_w, m_ffn_norm_w, m_ple_norm_w, m_final_norm_w, m_ab_gla_gate_b, m_ab_ret_norm_w, m_ab_gla_norm_w]
    small_v = [v_attn_norm_w, v_ffn_norm_w, v_ple_norm_w, v_final_norm_w, v_ab_gla_gate_b, v_ab_ret_norm_w, v_ab_gla_norm_w]
    sizes = [int(np.prod(a.shape)) for a in small_w]
    n_gu = GLA_GATE_RANK * GLA_QK
    n_small = _round_up(sum(sizes) + n_gu, LANE)
    pack = lambda parts: _pad_to(jnp.concatenate([a.reshape(-1) for a in parts]), 0, n_small)[None, :]
    small_part = pack(small_grads + [d_gu[:GLA_GATE_RANK]])

    cols_first = lambda a: jnp.transpose(a, (2, 0, 1))
    big_w = dict(ab_w_in=tuple(cols_first(a) for a in (ab_w_in, m_ab_w_in, v_ab_w_in)),
                 ab_w_out=(ab_w_out, m_ab_w_out, v_ab_w_out),
                 c_w_qkv=(c_w_qkv, m_c_w_qkv, v_c_w_qkv), c_w_out=(c_w_out, m_c_w_out, v_c_w_out),
                 ffn_w_gate=(wg_t, tr_(m_ffn_w_gate), tr_(v_ffn_w_gate)),
                 ffn_w_up=(wu_t, tr_(m_ffn_w_up), tr_(v_ffn_w_up)),
                 ffn_w_down=(ffn_w_down, m_ffn_w_down, v_ffn_w_down), ple_w_proj=(ple_w_proj, m_ple_w_proj, v_ple_w_proj),
                 ple_w_gate=(ple_w_gate, m_ple_w_gate, v_ple_w_gate))
    if on_chip:
        scatter_start("scatter_last", [], deps=(dx,))
    results, last = {}, dx
    for gi, (keys, handle, partials) in enumerate(scatters):
        (arrived,), _, _ = exchange_call(f"scatter_wait{gi}", [("chip_sums", handle)], [], deps=(last,))
        for (n, l), (_, land), (a, half) in zip(keys, arrived, partials):
            if n == "ab_w_in":
                results[n] = adamw_columns(f"adamw_{n}", *big_w[n], land, a, half)
            else:
                results[n] = adamw(f"adamw_{n}{l}", *big_w[n], l, land, a, half, prev=results.get(n))
            last = results[n][0]
    for n in ("ffn_w_gate", "ffn_w_up"):
        results[n] = [tr_(a) for a in results[n]]
    results["ab_w_in"] = [jnp.transpose(a, (1, 2, 0)) for a in results["ab_w_in"]]
    small_parts = gather_small("gather_small", small_part, deps=(last,)).reshape(N_DEV, n_small)

    gu_off = sum(sizes)
    own_cols = lambda a: lax.dynamic_slice_in_dim(a.reshape(GLA_GATE_RANK, GLA_QK), me * gu_cols, gu_cols, axis=1)
    small_res = adamw_small("adamw_small", pack(small_w + [jnp.zeros((n_gu,), F32)]),
                            pack(small_m + [jnp.zeros((n_gu,), F32)]), pack(small_v + [jnp.ones((n_gu,), F32)]),
                            small_parts)
    g_gu_full = small_res[0][0, gu_off:gu_off + n_gu]
    g_gu = own_cols(g_gu_full)[None]
    gu_res = adamw_small("adamw_gate_up", *[_pad_to(a.reshape(1, -1), 1, _round_up(a.size, LANE)) for a in
                                            (ab_gla_gate_up, m_ab_gla_gate_up, v_ab_gla_gate_up)],
                         jnp.concatenate([_pad_to(g_gu.reshape(1, -1), 1, _round_up(g_gu.size, LANE)),
                                          jnp.zeros((N_DEV - 1, _round_up(g_gu.size, LANE)), F32)], axis=0))
    for k in range(4):
        off = 0
        for n, a, sz in zip(small_names, small_w, sizes):
            results.setdefault(n, [None] * 4)[k] = small_res[k][0, off:off + sz].reshape(a.shape)
            off += sz
        results.setdefault("ab_gla_gate_up", [None] * 4)[k] = gu_res[k][0, :g_gu.size].reshape(ab_gla_gate_up.shape)

    order = ["attn_norm_w", "ffn_norm_w", "ple_norm_w", "final_norm_w", "ab_w_in", "ab_gla_gate_up", "ab_gla_gate_b",
             "ab_ret_norm_w", "ab_gla_norm_w", "ab_w_out", "c_w_qkv", "c_w_out", "ffn_w_gate", "ffn_w_up", "ffn_w_down",
             "ple_w_proj", "ple_w_gate"]
    return (loss, dx[None], *[results[n][0] for n in order], *[results[n][1] for n in order],
            *[results[n][2] for n in order], *[results[n][3] for n in order])
```

```python
import math

import numpy as np
import jax
import jax.numpy as jnp
from jax import lax
from jax.experimental import pallas as pl
from jax.experimental.pallas import tpu as pltpu

F32 = jnp.float32
BF16 = jnp.bfloat16
HIGHEST = lax.Precision.HIGHEST

N_DEV = 8
VMEM_LIMIT_BYTES = 48 * 1024 * 1024
LANE = 128
NORM_EPS = 1e-6

RET_HEADS, RET_DK, RET_DV = 4, 256, 256
RET_THETA_BASE = 10000.0
GLA_HEADS, GLA_DK, GLA_DV = 4, 128, 256
GLA_GATE_RANK = 16
GLA_GATE_NORM = 16.0
CHUNK = 64
ATT_HEADS = 16
DILATED_BRANCHES = ((128, 1), (512, 4), (2048, 16))
BLK = 256

ADAM_LR, ADAM_B1, ADAM_B2, ADAM_EPS, ADAM_WD, ADAM_STEP = 0.001, 0.9, 0.999, 1e-08, 0.01, 10

RET_QK = RET_HEADS * RET_DK
RET_V = RET_HEADS * RET_DV
GLA_QK = GLA_HEADS * GLA_DK
GLA_V = GLA_HEADS * GLA_DV
OFF_RQ, OFF_RK, OFF_RV, OFF_RG = 0, RET_QK, 2 * RET_QK, 2 * RET_QK + RET_V
OFF_GQ = OFF_RG + RET_V
OFF_GK = OFF_GQ + GLA_QK
OFF_GV = OFF_GK + GLA_QK
OFF_GG = OFF_GV + GLA_V
OFF_LR = OFF_GG + GLA_V


def _params(*sem):
    return pltpu.CompilerParams(dimension_semantics=sem or None, vmem_limit_bytes=VMEM_LIMIT_BYTES)


def _pick(n, cands):
    for c in cands:
        if n % c == 0:
            return c
    raise ValueError(f"no tile for {n} in {cands}")


_NN = (((1,), (0,)), ((), ()))
_NT = (((1,), (1,)), ((), ()))
_TN = (((0,), (0,)), ((), ()))
_ANY = pl.BlockSpec(memory_space=pl.ANY)
MAX_CONTRACT = 2048
_TILES = (1024, 768, 512, 256, 128)


def _mm_call(name, dims, grid, in_specs, out_spec, out_shape, args, deps=()):
    steps = grid[2]
    assert steps == 1 or out_shape.dtype == F32

    def body(a_ref, b_ref, *rest):
        o_ref = rest[len(deps)]
        part = lax.dot_general(a_ref[...].astype(BF16), b_ref[...].astype(BF16), dims, preferred_element_type=F32)
        if steps == 1:
            o_ref[...] = part.astype(o_ref.dtype)
        else:
            _accumulate(o_ref, part, pl.program_id(2) == 0)

    return pl.pallas_call(
        body, name=name, grid=grid, in_specs=list(in_specs) + [_ANY] * len(deps), out_specs=out_spec,
        out_shape=out_shape, compiler_params=_params("parallel", "parallel", "arbitrary"))(*args, *deps)


def mm_nn(name, a, w, l, out_dtype, deps=()):
    _, J, K, n = w.shape
    M = a.shape[0]
    tm, tn, tk = _pick(M, _TILES), _pick(n, _TILES), _pick(K, (MAX_CONTRACT,) + _TILES)
    nt = n // tn
    return _mm_call(
        name, _NN, (M // tm, J * nt, K // tk),
        [pl.BlockSpec((tm, tk), lambda i, j, k: (i, k)),
         pl.BlockSpec((None, None, tk, tn), lambda i, j, k: (l, j // nt, k, j % nt))],
        pl.BlockSpec((tm, tn), lambda i, j, k: (i, j)),
        jax.ShapeDtypeStruct((M, J * n), out_dtype), (a, w), deps)


def mm_nt(name, a, w, l, out_dtype, deps=()):
    _, J, K, n = w.shape
    M = a.shape[0]
    tm, tq, tc = _pick(M, _TILES), _pick(K, _TILES), _pick(n, (MAX_CONTRACT,) + _TILES)
    nc = n // tc
    return _mm_call(
        name, _NT, (M // tm, K // tq, J * nc),
        [pl.BlockSpec((tm, tc), lambda i, q, c: (i, c)),
         pl.BlockSpec((None, None, tq, tc), lambda i, q, c: (l, c // nc, q, c % nc))],
        pl.BlockSpec((tm, tq), lambda i, q, c: (i, q)),
        jax.ShapeDtypeStruct((M, K), out_dtype), (a, w), deps)


def mm_tn(name, x, dy, J, out_dtype, deps=()):
    M, K = x.shape
    n = dy.shape[1] // J
    tp, tn = _pick(K, _TILES), _pick(n, _TILES)
    nt = n // tn
    assert M <= MAX_CONTRACT
    return _mm_call(
        name, _TN, (K // tp, J * nt, 1),
        [pl.BlockSpec((M, tp), lambda i, j, r: (0, i)),
         pl.BlockSpec((M, tn), lambda i, j, r: (0, j))],
        pl.BlockSpec((None, tp, tn), lambda i, j, r: (j // nt, i, j % nt)),
        jax.ShapeDtypeStruct((J, K, n), out_dtype), (x, dy), deps)


def mmt_fwd(name, a, wt, l, out_dtype, n=None, deps=()):
    _, J, rows, K = wt.shape
    n = rows if n is None else n
    M = a.shape[0]
    tm, tn = _pick(M, _TILES), _pick(n, _TILES)
    nt = n // tn
    assert K <= MAX_CONTRACT
    return _mm_call(
        name, _NT, (M // tm, J * nt, 1),
        [pl.BlockSpec((tm, K), lambda i, j, k: (i, 0)),
         pl.BlockSpec((None, None, tn, K), lambda i, j, k: (l, j // nt, j % nt, 0))],
        pl.BlockSpec((tm, tn), lambda i, j, k: (i, j)),
        jax.ShapeDtypeStruct((M, J * n), out_dtype), (a, wt), deps)


def mmt_dx(name, dy, wt, l, out_dtype, n=None, deps=()):
    _, J, rows, K = wt.shape
    n = rows if n is None else n
    M = dy.shape[0]
    tm, tq, tc = _pick(M, _TILES), _pick(K, _TILES), _pick(n, _TILES)
    nc = n // tc
    return _mm_call(
        name, _NN, (M // tm, K // tq, J * nc),
        [pl.BlockSpec((tm, tc), lambda i, q, c: (i, c)),
         pl.BlockSpec((None, None, tc, tq), lambda i, q, c: (l, c // nc, c % nc, q))],
        pl.BlockSpec((tm, tq), lambda i, q, c: (i, q)),
        jax.ShapeDtypeStruct((M, K), out_dtype), (dy, wt), deps)


WIDE_TILE = 512


def _wide_call(name, body, M, K, a, w, a_spec, w_spec, out_dtype, deps):
    def kernel_body(a_ref, w_ref, *rest):
        o_ref = rest[len(deps)]
        o_ref[...] = body(a_ref, w_ref).astype(o_ref.dtype)

    return pl.pallas_call(
        kernel_body, name=name, grid=(M // WIDE_TILE, K // WIDE_TILE),
        in_specs=[a_spec, w_spec] + [_ANY] * len(deps),
        out_specs=pl.BlockSpec((WIDE_TILE, WIDE_TILE), lambda i, q: (i, q)),
        out_shape=jax.ShapeDtypeStruct((M, K), out_dtype),
        compiler_params=_params("parallel", "parallel"))(a, w, *deps)


def mmt_dx_wide(name, dy, wt, out_dtype, n=None, deps=()):
    _, J, rows, K = wt.shape
    n = rows if n is None else n
    M = dy.shape[0]

    def body(dy_ref, w_ref):
        return jnp.dot(dy_ref[...].astype(BF16), w_ref[...].reshape(J * n, WIDE_TILE), preferred_element_type=F32)

    return _wide_call(name, body, M, K, dy, wt,
                      pl.BlockSpec((WIDE_TILE, J * n), lambda i, q: (i, 0)),
                      pl.BlockSpec((None, J, n, WIDE_TILE), lambda i, q: (0, 0, 0, q)), out_dtype, deps)


def mmt_dx_pair(name, dy1, wt1, dy2, wt2, out_dtype, deps=()):
    _, J, n, K = wt1.shape
    M = dy1.shape[0]

    def body(dy1_ref, w1_ref, dy2_ref, w2_ref, *rest):
        o_ref = rest[len(deps)]
        acc = jnp.dot(dy1_ref[...], w1_ref[...].reshape(J * n, WIDE_TILE), preferred_element_type=F32)
        acc = acc + jnp.dot(dy2_ref[...], w2_ref[...].reshape(J * n, WIDE_TILE), preferred_element_type=F32)
        o_ref[...] = acc.astype(o_ref.dtype)

    rows = _once((WIDE_TILE, J * n), lambda i, q: (i, 0))
    cols = pl.BlockSpec((None, J, n, WIDE_TILE), lambda i, q: (0, 0, 0, q))
    return pl.pallas_call(
        body, name=name, grid=(M // WIDE_TILE, K // WIDE_TILE),
        in_specs=[rows, cols, rows, cols] + [_ANY] * len(deps),
        out_specs=pl.BlockSpec((WIDE_TILE, WIDE_TILE), lambda i, q: (i, q)),
        out_shape=jax.ShapeDtypeStruct((M, K), out_dtype),
        compiler_params=_params("parallel", "parallel"))(dy1, wt1, dy2, wt2, *deps)


def mm_nt_wide(name, a, w, out_dtype, deps=()):
    _, J, K, n = w.shape
    M = a.shape[0]

    def body(a_ref, w_ref):
        acc = None
        for j in range(J):
            part = lax.dot_general(a_ref[:, j * n:(j + 1) * n].astype(BF16), w_ref[j], _NT, preferred_element_type=F32)
            acc = part if acc is None else acc + part
        return acc

    return _wide_call(name, body, M, K, a, w,
                      pl.BlockSpec((WIDE_TILE, J * n), lambda i, q: (i, 0)),
                      pl.BlockSpec((None, J, WIDE_TILE, n), lambda i, q: (0, 0, q, 0)), out_dtype, deps)


def mmt_dw(name, dy, x, J, out_dtype, deps=(), rows=None):
    M, K = x.shape
    n = dy.shape[1] // J
    tn, tp = _pick(n, _TILES), _pick(K, _TILES)
    nt = n // tn
    assert M <= MAX_CONTRACT
    return _mm_call(
        name, _TN, (J * nt, K // tp, 1),
        [pl.BlockSpec((M, tn), lambda j, i, r: (0, j)),
         pl.BlockSpec((M, tp), lambda j, i, r: (0, i))],
        pl.BlockSpec((None, tn, tp), lambda j, i, r: (j // nt, j % nt, i)),
        jax.ShapeDtypeStruct((J, n if rows is None else rows, K), out_dtype), (dy, x), deps)


def mmt_dw_rows(name, dy, x, out, row0, rank):
    M, K = x.shape
    tp = _pick(K, _TILES)

    def body(dy_ref, x_ref, prev_ref, o_ref):
        del prev_ref
        full = lax.dot_general(dy_ref[...], x_ref[...], _TN, preferred_element_type=F32)
        o_ref[...] = full[:rank].astype(o_ref.dtype)

    return pl.pallas_call(
        body, name=name, grid=(K // tp,),
        in_specs=[pl.BlockSpec((M, dy.shape[1]), lambda i: (0, 0)), pl.BlockSpec((M, tp), lambda i: (0, i)), _ANY],
        out_specs=pl.BlockSpec((None, rank, tp), lambda i: (0, row0 // rank, i)),
        out_shape=jax.ShapeDtypeStruct(out.shape, out.dtype), input_output_aliases={2: 0},
        compiler_params=_params("parallel"))(dy, x, out)


def ffn_gate_up(name, a, wg, wu):
    _, J, n, K = wg.shape
    M = a.shape[0]
    tm, tn = _pick(M, _TILES), _pick(n, _TILES)
    nt = n // tn
    assert K <= MAX_CONTRACT

    def body(a_ref, wg_ref, wu_ref, dup_ref, dgate_ref, act_ref):
        x = a_ref[...]
        g = lax.dot_general(x, wg_ref[...], _NT, preferred_element_type=F32)
        u = lax.dot_general(x, wu_ref[...], _NT, preferred_element_type=F32)
        silu, dsilu = _silu_and_grad(g)
        dup_ref[...] = silu.astype(dup_ref.dtype)
        dgate_ref[...] = (u * dsilu).astype(dgate_ref.dtype)
        act_ref[...] = (silu * u).astype(act_ref.dtype)

    w_spec = pl.BlockSpec((None, None, tn, K), lambda i, j: (0, j // nt, j % nt, 0))
    out = pl.BlockSpec((tm, tn), lambda i, j: (i, j))
    return pl.pallas_call(
        body, name=name, grid=(M // tm, J * nt),
        in_specs=[pl.BlockSpec((tm, K), lambda i, j: (i, 0)), w_spec, w_spec],
        out_specs=[out] * 3, out_shape=[jax.ShapeDtypeStruct((M, J * n), BF16)] * 3,
        compiler_params=_params("parallel", "parallel"))(a, wg, wu)


def mm_add_norm(name, a, w, res, norm_w):
    _, _, K, N = w.shape
    M = a.shape[0]
    tm, tk = _pick(M, (WIDE_TILE, 256)), _pick(K, (1024, 512, 256))
    steps = K // tk

    def body(a_ref, w_ref, res_ref, nw_ref, h_ref, hn_ref):
        k = pl.program_id(1)
        part = jnp.dot(a_ref[...], w_ref[...], preferred_element_type=F32)
        _accumulate(h_ref, part, k == 0)

        @pl.when(k == steps - 1)
        def _():
            h = h_ref[...] + res_ref[...]
            h_ref[...] = h
            hn_ref[...] = _rms(h, nw_ref[...]).astype(hn_ref.dtype)

    rows = pl.BlockSpec((tm, N), lambda i, k: (i, 0))
    return pl.pallas_call(
        body, name=name, grid=(M // tm, steps),
        in_specs=[pl.BlockSpec((tm, tk), lambda i, k: (i, k)),
                  pl.BlockSpec((None, None, tk, N), lambda i, k: (0, 0, k, 0)), rows,
                  pl.BlockSpec((1, N), lambda i, k: (0, 0))],
        out_specs=[rows, rows],
        out_shape=[jax.ShapeDtypeStruct((M, N), F32), jax.ShapeDtypeStruct((M, N), BF16)],
        compiler_params=_params("parallel", "arbitrary"))(a, w, res, norm_w)


def ple_fwd(name, pn, wpg, p_in, wpp, h):
    _, J, P, n = wpp.shape
    M, D = h.shape
    tm, tn = _pick(M, (WIDE_TILE, 256)), _pick(D, _TILES)
    per_tile = tn // n

    def body(pn_ref, wg_ref, p_ref, wp_ref, h_ref, x_ref, s_ref, e_ref):
        s = jnp.dot(pn_ref[...], wg_ref[...], preferred_element_type=F32)
        p_blk = p_ref[...]
        e = jnp.concatenate([jnp.dot(p_blk, wp_ref[j], preferred_element_type=F32) for j in range(per_tile)], axis=1)
        s_ref[...] = s
        e_ref[...] = e
        x_ref[...] = h_ref[...] + _sigmoid(s) * e

    tile = pl.BlockSpec((tm, tn), lambda i, j: (i, j))
    return pl.pallas_call(
        body, name=name, grid=(M // tm, D // tn),
        in_specs=[pl.BlockSpec((tm, D), lambda i, j: (i, 0)),
                  pl.BlockSpec((None, None, D, tn), lambda i, j: (0, 0, 0, j)),
                  pl.BlockSpec((tm, P), lambda i, j: (i, 0)),
                  pl.BlockSpec((None, per_tile, P, n), lambda i, j: (0, j, 0, 0)), tile],
        out_specs=[tile] * 3, out_shape=[jax.ShapeDtypeStruct((M, D), F32)] * 3,
        compiler_params=_params("parallel", "parallel"))(pn, wpg, p_in, wpp, h)


def ffn_down_bwd(name, dy, wd, dup, dgate, deps=()):
    _, _, K, n = wd.shape
    M = dy.shape[0]
    tm, tq = _pick(M, _TILES), _pick(K, _TILES)
    assert n <= MAX_CONTRACT

    def body(dy_ref, w_ref, dup_ref, dgate_ref, *rest):
        dg_ref, du_ref = rest[len(deps):]
        dact = lax.dot_general(dy_ref[...], w_ref[...], _NT, preferred_element_type=F32)
        dg_ref[...] = (dact * dgate_ref[...].astype(F32)).astype(dg_ref.dtype)
        du_ref[...] = (dact * dup_ref[...].astype(F32)).astype(du_ref.dtype)

    blk = pl.BlockSpec((tm, tq), lambda i, q: (i, q))
    return pl.pallas_call(
        body, name=name, grid=(M // tm, K // tq),
        in_specs=[pl.BlockSpec((tm, n), lambda i, q: (i, 0)),
                  pl.BlockSpec((None, None, tq, n), lambda i, q: (0, 0, q, 0)), blk, blk] + [_ANY] * len(deps),
        out_specs=[blk, blk], out_shape=[jax.ShapeDtypeStruct((M, K), BF16)] * 2,
        compiler_params=_params("parallel", "parallel"))(dy, wd, dup, dgate, *deps)


def rowwise(name, fn, rows, ins, outs, tr=256, deps=()):
    widest = max([s[1].shape[1] if s[0] != "col" else s[3] for s in ins] + [s[1] for s in outs])
    tr = min(tr if widest <= 2048 else tr // 2, rows)
    in_specs, args = [], []
    for spec in ins:
        kind, a = spec[0], spec[1]
        if kind == "row":
            in_specs.append(pl.BlockSpec((tr, a.shape[1]), lambda i: (i, 0)))
        elif kind == "col":
            cb, width = spec[2], spec[3]
            in_specs.append(pl.BlockSpec((tr, width), lambda i, cb=cb: (i, cb)))
        else:
            in_specs.append(pl.BlockSpec(a.shape, lambda i: (0, 0)))
        args.append(a)
    out_specs, out_shapes = [], []
    for spec in outs:
        if spec[0] == "row":
            out_specs.append(pl.BlockSpec((tr, spec[1]), lambda i: (i, 0)))
            out_shapes.append(jax.ShapeDtypeStruct((rows, spec[1]), spec[2]))
        else:
            out_specs.append(pl.BlockSpec((1, spec[1]), lambda i: (0, 0)))
            out_shapes.append(jax.ShapeDtypeStruct((1, spec[1]), F32))
    n_in = len(ins)

    def body(*refs):
        vals = fn(*[r[...] for r in refs[:n_in]])
        first = pl.program_id(0) == 0
        for r, v, spec in zip(refs[n_in + len(deps):], vals, outs):
            if spec[0] == "row":
                r[...] = v.astype(r.dtype)
            else:
                _accumulate(r, v, first)

    return pl.pallas_call(body, name=name, grid=(rows // tr,), in_specs=in_specs + [_ANY] * len(deps),
                          out_specs=out_specs, out_shape=out_shapes,
                          compiler_params=_params("arbitrary"))(*args, *deps)


def _accumulate(ref, v, first):
    @pl.when(first)
    def _():
        ref[...] = v

    @pl.when(jnp.logical_not(first))
    def _():
        ref[...] += v


def _rms(x, w):
    r = lax.rsqrt(jnp.mean(x * x, axis=-1, keepdims=True) + NORM_EPS)
    return x * r * w


def _rms_bwd(x, w, dy):
    r = lax.rsqrt(jnp.mean(x * x, axis=-1, keepdims=True) + NORM_EPS)
    g = dy * w
    dx = r * (g - x * (r * r) * jnp.mean(g * x, axis=-1, keepdims=True))
    dw = jnp.sum(dy * x * r, axis=0, keepdims=True)
    return dx, dw


def _sigmoid(x):
    return 1.0 / (1.0 + jnp.exp(-x))


def _silu_and_grad(g):
    s = _sigmoid(g)
    return g * s, s * (1.0 + g * (1.0 - s))


def _swap_pairs(x):
    n = x.shape[-1]
    lane = lax.broadcasted_iota(jnp.int32, x.shape, x.ndim - 1)
    return jnp.where((lane & 1) == 0, pltpu.roll(x, n - 1, x.ndim - 1), pltpu.roll(x, 1, x.ndim - 1))


def _rot(x, cosf, sins):
    return x * cosf + _swap_pairs(x) * sins


def _unrot(d, cosf, sins):
    return d * cosf + _swap_pairs(d * sins)


def _ret_log_gamma(h):
    vals = [math.log1p(-2.0 ** (-5.0 - i)) for i in range(RET_HEADS)]
    out = jnp.float32(vals[RET_HEADS - 1])
    for i in range(RET_HEADS - 2, -1, -1):
        out = jnp.where(h == i, jnp.float32(vals[i]), out)
    return out


def _fill_decays(dec_ref, lg):
    ri = lax.broadcasted_iota(jnp.int32, (BLK, BLK), 0)
    ci = lax.broadcasted_iota(jnp.int32, (BLK, BLK), 1)
    for d in range(dec_ref.shape[0]):
        dt = d * BLK + ri - ci
        dec_ref[d] = jnp.where(dt >= 0, jnp.exp(jnp.maximum(dt, 0).astype(F32) * lg), 0.0)


def _decay_row(dec_ref, qi):
    return jnp.concatenate([dec_ref[qi - kb] for kb in range(qi + 1)], axis=1)


def _once(block_shape, index_map):
    return pl.BlockSpec(block_shape, index_map, pipeline_mode=pl.Buffered(1))


def _dot(a, b):
    return jnp.dot(a.astype(BF16), b.astype(BF16), preferred_element_type=F32)


def _dot_nt(a, b):
    return lax.dot_general(a.astype(BF16), b.astype(BF16), _NT, preferred_element_type=F32)


def _dot_tn(a, b):
    return lax.dot_general(a.astype(BF16), b.astype(BF16), _TN, preferred_element_type=F32)


def retention_fwd(name, z, cosf, sins, width_out):
    T = z.shape[0]
    nq = T // BLK
    scale = RET_DK ** -0.5

    def body(q_ref, k_ref, v_ref, cos_ref, sin_ref, o_ref, krot, vb, dec_ref):
        _fill_decays(dec_ref, _ret_log_gamma(pl.program_id(0)))
        krot[...] = (_rot(k_ref[...], cos_ref[...], sin_ref[...]) * scale).astype(BF16)
        vb[...] = v_ref[...].astype(BF16)
        for qi in range(nq):
            rows, n = slice(qi * BLK, (qi + 1) * BLK), (qi + 1) * BLK
            q = _rot(q_ref[rows, :], cos_ref[rows, :], sin_ref[rows, :])
            s = _dot_nt(q, krot[0:n, :]) * _decay_row(dec_ref, qi)
            o_ref[rows, :] = _dot(s, vb[0:n, :])

    return pl.pallas_call(
        body, name=name, grid=(RET_HEADS,),
        in_specs=[pl.BlockSpec((T, RET_DK), lambda h: (0, OFF_RQ // RET_DK + h)),
                  pl.BlockSpec((T, RET_DK), lambda h: (0, OFF_RK // RET_DK + h)),
                  pl.BlockSpec((T, RET_DV), lambda h: (0, OFF_RV // RET_DV + h)),
                  _once((T, RET_DK), lambda h: (0, 0)), _once((T, RET_DK), lambda h: (0, 0))],
        out_specs=pl.BlockSpec((T, RET_DV), lambda h: (0, h)),
        out_shape=jax.ShapeDtypeStruct((T, width_out), F32),
        scratch_shapes=[pltpu.VMEM((T, RET_DK), BF16), pltpu.VMEM((T, RET_DV), BF16),
                        pltpu.VMEM((nq, BLK, BLK), F32)],
        compiler_params=_params("arbitrary"))(z, z, z, cosf, sins)


def retention_bwd(name, z, cosf, sins, do):
    T = z.shape[0]
    nq = T // BLK
    scale = RET_DK ** -0.5

    def body(q_ref, k_ref, v_ref, cos_ref, sin_ref, do_ref, dq_ref, dk_ref, dv_ref, krot, vb, dk_acc, dv_acc, dec_ref):
        _fill_decays(dec_ref, _ret_log_gamma(pl.program_id(0)))
        krot[...] = (_rot(k_ref[...], cos_ref[...], sin_ref[...]) * scale).astype(BF16)
        vb[...] = v_ref[...].astype(BF16)
        dk_acc[...] = jnp.zeros_like(dk_acc)
        dv_acc[...] = jnp.zeros_like(dv_acc)
        for qi in range(nq):
            rows, n = slice(qi * BLK, (qi + 1) * BLK), (qi + 1) * BLK
            cos_q, sin_q = cos_ref[rows, :], sin_ref[rows, :]
            q = _rot(q_ref[rows, :], cos_q, sin_q).astype(BF16)
            dout = do_ref[rows, :].astype(BF16)
            kk, vv, dec = krot[0:n, :], vb[0:n, :], _decay_row(dec_ref, qi)
            p = (_dot_nt(q, kk) * dec).astype(BF16)
            ds = (_dot_nt(dout, vv) * dec).astype(BF16)
            dq_ref[rows, :] = _unrot(_dot(ds, kk), cos_q, sin_q).astype(dq_ref.dtype)
            dk_acc[0:n, :] += _dot_tn(ds, q)
            dv_acc[0:n, :] += _dot_tn(p, dout)
        dk_ref[...] = (_unrot(dk_acc[...], cos_ref[...], sin_ref[...]) * scale).astype(dk_ref.dtype)
        dv_ref[...] = dv_acc[...].astype(dv_ref.dtype)

    head = lambda h: (0, h)
    return pl.pallas_call(
        body, name=name, grid=(RET_HEADS,),
        in_specs=[pl.BlockSpec((T, RET_DK), lambda h: (0, OFF_RQ // RET_DK + h)),
                  pl.BlockSpec((T, RET_DK), lambda h: (0, OFF_RK // RET_DK + h)),
                  pl.BlockSpec((T, RET_DV), lambda h: (0, OFF_RV // RET_DV + h)),
                  _once((T, RET_DK), lambda h: (0, 0)), _once((T, RET_DK), lambda h: (0, 0)),
                  pl.BlockSpec((T, RET_DV), head)],
        out_specs=[pl.BlockSpec((T, RET_DK), head), pl.BlockSpec((T, RET_DK), head), pl.BlockSpec((T, RET_DV), head)],
        out_shape=[jax.ShapeDtypeStruct((T, RET_QK), BF16), jax.ShapeDtypeStruct((T, RET_QK), BF16),
                   jax.ShapeDtypeStruct((T, RET_V), BF16)],
        scratch_shapes=[pltpu.VMEM((T, RET_DK), BF16), pltpu.VMEM((T, RET_DV), BF16),
                        pltpu.VMEM((T, RET_DK), F32), pltpu.VMEM((T, RET_DV), F32),
                        pltpu.VMEM((nq, BLK, BLK), F32)],
        compiler_params=_params("arbitrary"))(z, z, z, cosf, sins, do)


GLA_PAIR = 2


def _gla_chunk(q_ref, k_ref, v_ref, glr_ref, gu, gb, rows, hh, trilf):
    ck = slice(hh * GLA_DK, (hh + 1) * GLA_DK)
    zg = _dot(glr_ref[rows, :], gu[:, ck]) + gb[:, ck]
    la = (jnp.minimum(zg, 0.0) - jnp.log(1.0 + jnp.exp(-jnp.abs(zg)))) * (1.0 / GLA_GATE_NORM)
    cum = jnp.dot(trilf, la, precision=HIGHEST, preferred_element_type=F32)
    last = jnp.sum(la, axis=0, keepdims=True)
    ecum = jnp.exp(cum)
    k = k_ref[rows, ck]
    qt = q_ref[rows, ck] * (GLA_DK ** -0.5) * ecum
    kt = k * jnp.exp(-cum)
    kh = k * jnp.exp(last - cum)
    return zg, cum, last, ecum, qt, kt, kh, v_ref[rows, hh * GLA_DV:(hh + 1) * GLA_DV].astype(BF16)


def _state_decay(last):
    e = jnp.exp(jnp.broadcast_to(last, (GLA_DK, GLA_DK)).T)
    return jnp.concatenate([e] * (GLA_DV // GLA_DK), axis=1)


def _gla_specs(T):
    wk, wv = GLA_PAIR * GLA_DK, GLA_PAIR * GLA_DV
    return [_once((T, wk), lambda h: (0, OFF_GQ // wk + h)),
            _once((T, wk), lambda h: (0, OFF_GK // wk + h)),
            _once((T, wv), lambda h: (0, OFF_GV // wv + h)),
            _once((T, LANE), lambda h: (0, 0)),
            pl.BlockSpec((LANE, wk), lambda h: (0, h)),
            pl.BlockSpec((1, wk), lambda h: (0, h))]


def gla_fwd(name, z, glr, gu, gb, o_prev):
    T = z.shape[0]
    nc = T // CHUNK
    wv = GLA_PAIR * GLA_DV

    def body(q_ref, k_ref, v_ref, glr_ref, gu_ref, gb_ref, prev_ref, o_ref, s_all_ref, *S):
        del prev_ref
        gu_b, gb_v = gu_ref[...].astype(BF16), gb_ref[...]
        ri = lax.broadcasted_iota(jnp.int32, (CHUNK, CHUNK), 0)
        ci = lax.broadcasted_iota(jnp.int32, (CHUNK, CHUNK), 1)
        tril = ri >= ci
        trilf = tril.astype(F32)
        for s_ref in S:
            s_ref[...] = jnp.zeros_like(s_ref)

        def step(c, carry):
            rows = pl.ds(pl.multiple_of(c * CHUNK, CHUNK), CHUNK)
            heads = range(GLA_PAIR)
            ch = [_gla_chunk(q_ref, k_ref, v_ref, glr_ref, gu_b, gb_v, rows, hh, trilf) for hh in heads]
            a = [jnp.where(tril, _dot_nt(ch[hh][4], ch[hh][5]), 0.0) for hh in heads]
            s_prev = [S[hh][...] for hh in heads]
            intra = [_dot(a[hh], ch[hh][7]) for hh in heads]
            inter = [_dot(ch[hh][4], s_prev[hh]) for hh in heads]
            added = [_dot_tn(ch[hh][6], ch[hh][7]) for hh in heads]
            for hh in heads:
                o_ref[rows, hh * GLA_DV:(hh + 1) * GLA_DV] = intra[hh] + inter[hh]
                s_all_ref[hh, c] = s_prev[hh]
                S[hh][...] = s_prev[hh] * _state_decay(ch[hh][2]) + added[hh]
            return carry

        lax.fori_loop(0, nc, step, 0)

    n_in = 6
    return pl.pallas_call(
        body, name=name, grid=(GLA_HEADS // GLA_PAIR,),
        in_specs=_gla_specs(T) + [pl.BlockSpec(memory_space=pl.ANY)],
        out_specs=[pl.BlockSpec((T, wv), lambda h: (0, RET_V // wv + h)),
                   pl.BlockSpec((GLA_PAIR, nc, GLA_DK, GLA_DV), lambda h: (h, 0, 0, 0))],
        out_shape=[jax.ShapeDtypeStruct(o_prev.shape, F32),
                   jax.ShapeDtypeStruct((GLA_HEADS, nc, GLA_DK, GLA_DV), F32)],
        scratch_shapes=[pltpu.VMEM((GLA_DK, GLA_DV), F32)] * GLA_PAIR,
        input_output_aliases={n_in: 0},
        compiler_params=_params("arbitrary"))(z, z, z, glr, gu, gb, o_prev)


def gla_bwd(name, z, glr, gu, gb, do, states):
    T = z.shape[0]
    nc = T // CHUNK

    def body(q_ref, k_ref, v_ref, glr_ref, gu_ref, gb_ref, do_ref, s_all,
             dq_ref, dk_ref, dv_ref, dglr_ref, dgu_ref, dgb_ref, dS):
        gu_b, gb_v = gu_ref[...].astype(BF16), gb_ref[...]
        ri = lax.broadcasted_iota(jnp.int32, (CHUNK, CHUNK), 0)
        ci = lax.broadcasted_iota(jnp.int32, (CHUNK, CHUNK), 1)
        tril = ri >= ci
        trilf = tril.astype(F32)
        triuf = (ri <= ci).astype(F32)
        last_row = lax.broadcasted_iota(jnp.int32, (CHUNK, GLA_DK), 0) == CHUNK - 1
        ones8 = jnp.ones((8, GLA_DV), F32)

        heads = range(GLA_PAIR)

        dS[...] = jnp.zeros_like(dS)
        dgu_ref[...] = jnp.zeros_like(dgu_ref)
        dgb_ref[...] = jnp.zeros_like(dgb_ref)

        def bstep(i, carry):
            c = nc - 1 - i
            rows = pl.ds(pl.multiple_of(c * CHUNK, CHUNK), CHUNK)
            glr_c = glr_ref[rows, :]
            cks = [slice(hh * GLA_DK, (hh + 1) * GLA_DK) for hh in heads]
            cvs = [slice(hh * GLA_DV, (hh + 1) * GLA_DV) for hh in heads]
            ch = [_gla_chunk(q_ref, k_ref, v_ref, glr_ref, gu_b, gb_v, rows, hh, trilf) for hh in heads]
            zg, cum, last, ecum, qt, kt, kh, v = [[ch[hh][j] for hh in heads] for j in range(8)]
            s_prev = [s_all[hh, c] for hh in heads]
            ds_new = [dS[hh] for hh in heads]
            dout = [do_ref[rows, cvs[hh]].astype(BF16) for hh in heads]
            a = [jnp.where(tril, _dot_nt(qt[hh], kt[hh]), 0.0) for hh in heads]
            da = [jnp.where(tril, _dot_nt(dout[hh], v[hh]), 0.0) for hh in heads]
            dv_a = [_dot_tn(a[hh], dout[hh]) for hh in heads]
            dv_b = [_dot(kh[hh], ds_new[hh]) for hh in heads]
            dqt_a = [_dot(da[hh], kt[hh]) for hh in heads]
            dqt_b = [_dot_nt(dout[hh], s_prev[hh]) for hh in heads]
            dkt = [_dot_tn(da[hh], qt[hh]) for hh in heads]
            dkh = [_dot_nt(v[hh], ds_new[hh]) for hh in heads]
            ds_add = [_dot_tn(qt[hh], dout[hh]) for hh in heads]
            rs = [lax.dot_general(ones8, ds_new[hh] * s_prev[hh], _NT, precision=HIGHEST, preferred_element_type=F32)
                  for hh in heads]
            dcum = []
            for hh in heads:
                dv_ref[rows, cvs[hh]] = (dv_a[hh] + dv_b[hh]).astype(dv_ref.dtype)
                dS[hh] = ds_new[hh] * _state_decay(last[hh]) + ds_add[hh]
                dqt = dqt_a[hh] + dqt_b[hh]
                dq_ref[rows, cks[hh]] = (dqt * ecum[hh] * (GLA_DK ** -0.5)).astype(dq_ref.dtype)
                dk_ref[rows, cks[hh]] = (dkt[hh] * jnp.exp(-cum[hh])
                                         + dkh[hh] * jnp.exp(last[hh] - cum[hh])).astype(dk_ref.dtype)
                dkh_kh = dkh[hh] * kh[hh]
                dlast = (jnp.sum(dkh_kh, axis=0, keepdims=True)
                         + jnp.exp(last[hh]) * (jnp.sum(rs[hh], axis=0, keepdims=True) * 0.125))
                dcum.append(dqt * qt[hh] - dkt[hh] * kt[hh] - dkh_kh + jnp.where(last_row, dlast, 0.0))
            dla = [jnp.dot(triuf, dcum[hh], precision=HIGHEST, preferred_element_type=F32) for hh in heads]
            dzg = [dla[hh] * (1.0 / GLA_GATE_NORM) * _sigmoid(-zg[hh]) for hh in heads]
            dglr = [_dot_nt(dzg[hh], gu_b[:, cks[hh]]) for hh in heads]
            dgu = [_dot_tn(glr_c, dzg[hh]) for hh in heads]
            for hh in heads:
                dglr_ref[hh, rows, :] = dglr[hh]
                dgu_ref[:, cks[hh]] += dgu[hh]
                dgb_ref[:, cks[hh]] += jnp.sum(dzg[hh], axis=0, keepdims=True)
            return carry

        lax.fori_loop(0, nc, bstep, 0)

    wk, wv = GLA_PAIR * GLA_DK, GLA_PAIR * GLA_DV
    return pl.pallas_call(
        body, name=name, grid=(GLA_HEADS // GLA_PAIR,),
        in_specs=_gla_specs(T) + [_once((T, wv), lambda h: (0, RET_V // wv + h)),
                                  _once((GLA_PAIR, nc, GLA_DK, GLA_DV), lambda h: (h, 0, 0, 0))],
        out_specs=[pl.BlockSpec((T, wk), lambda h: (0, h)), pl.BlockSpec((T, wk), lambda h: (0, h)),
                   pl.BlockSpec((T, wv), lambda h: (0, h)),
                   pl.BlockSpec((GLA_PAIR, T, LANE), lambda h: (h, 0, 0)),
                   pl.BlockSpec((LANE, wk), lambda h: (0, h)), pl.BlockSpec((1, wk), lambda h: (0, h))],
        out_shape=[jax.ShapeDtypeStruct((T, GLA_QK), BF16), jax.ShapeDtypeStruct((T, GLA_QK), BF16),
                   jax.ShapeDtypeStruct((T, GLA_V), BF16), jax.ShapeDtypeStruct((GLA_HEADS, T, LANE), F32),
                   jax.ShapeDtypeStruct((LANE, GLA_QK), F32), jax.ShapeDtypeStruct((1, GLA_QK), F32)],
        scratch_shapes=[pltpu.VMEM((GLA_PAIR, GLA_DK, GLA_DV), F32)],
        compiler_params=_params("arbitrary"))(z, z, z, glr, gu, gb, do, states)


HN_HEADS = RET_HEADS + GLA_HEADS
HN_W = RET_DV


def _gate_col(h):
    return jnp.where(h < RET_HEADS, OFF_RG // HN_W + h, OFF_GG // HN_W + h - RET_HEADS)


def headnorm_fwd(name, oraw, z, w):
    T = oraw.shape[0]
    tr = _pick(T, _TILES)

    def body(o_ref, g_ref, w_ref, y_ref):
        y_ref[...] = (_rms(o_ref[...], w_ref[...]) * _silu_and_grad(g_ref[...])[0]).astype(y_ref.dtype)

    return pl.pallas_call(
        body, name=name, grid=(HN_HEADS, T // tr),
        in_specs=[pl.BlockSpec((tr, HN_W), lambda h, i: (i, h)),
                  pl.BlockSpec((tr, HN_W), lambda h, i: (i, _gate_col(h))),
                  pl.BlockSpec((1, HN_W), lambda h, i: (0, h))],
        out_specs=pl.BlockSpec((tr, HN_W), lambda h, i: (i, h)),
        out_shape=jax.ShapeDtypeStruct((T, HN_HEADS * HN_W), BF16),
        compiler_params=_params("arbitrary", "arbitrary"))(oraw, z, w)


def headnorm_bwd(name, oraw, z, w, dy):
    T = oraw.shape[0]
    tr = _pick(T, _TILES)

    def body(o_ref, g_ref, w_ref, dy_ref, do_ref, dg_ref, dw_ref):
        o, wv, dyv = o_ref[...], w_ref[...], dy_ref[...].astype(F32)
        silu, dsilu = _silu_and_grad(g_ref[...])
        n = _rms(o, wv)
        dg_ref[...] = (dyv * n * dsilu).astype(dg_ref.dtype)
        dx, dw = _rms_bwd(o, wv, dyv * silu)
        do_ref[...] = dx
        _accumulate(dw_ref, dw, pl.program_id(1) == 0)

    blk = pl.BlockSpec((tr, HN_W), lambda h, i: (i, h))
    return pl.pallas_call(
        body, name=name, grid=(HN_HEADS, T // tr),
        in_specs=[blk, pl.BlockSpec((tr, HN_W), lambda h, i: (i, _gate_col(h))),
                  pl.BlockSpec((1, HN_W), lambda h, i: (0, h)), blk],
        out_specs=[blk, blk, pl.BlockSpec((1, HN_W), lambda h, i: (0, h))],
        out_shape=[jax.ShapeDtypeStruct((T, HN_HEADS * HN_W), F32),
                   jax.ShapeDtypeStruct((T, HN_HEADS * HN_W), BF16),
                   jax.ShapeDtypeStruct((1, HN_HEADS * HN_W), F32)],
        compiler_params=_params("arbitrary", "arbitrary"))(oraw, z, w, dy)


N_MASKS = 4


def _check_mask_classes(T):
    for window, dilation in DILATED_BRANCHES[:-1]:
        assert window < (N_MASKS - 1) * BLK - (BLK - 1) and BLK % dilation == 0
    assert DILATED_BRANCHES[-1][0] >= T and BLK % DILATED_BRANCHES[-1][1] == 0


def _fill_masks(logm_ref):
    ri = lax.broadcasted_iota(jnp.int32, (BLK, BLK), 0)
    ci = lax.broadcasted_iota(jnp.int32, (BLK, BLK), 1)
    for d in range(N_MASKS):
        dt = d * BLK + ri - ci
        mult = jnp.zeros((BLK, BLK), F32)
        for window, dilation in DILATED_BRANCHES:
            hit = (dt >= 0) & (dt <= window) & ((dt & (dilation - 1)) == 0)
            mult = mult + hit.astype(F32)
        logm_ref[d] = jnp.where(mult > 0, jnp.log(jnp.maximum(mult, 1.0)), -1e30)


def _mask_row(ref, qi):
    return jnp.concatenate([ref[min(qi - kb, N_MASKS - 1)] for kb in range(qi + 1)], axis=1)


def attn_fwd(name, qkv):
    T = qkv.shape[0]
    D = qkv.shape[1] // 3
    dh = D // ATT_HEADS
    nq = T // BLK
    scale = dh ** -0.5

    _check_mask_classes(T)

    def body(q_ref, k_ref, v_ref, o_ref, lse_ref, logm_ref):
        @pl.when(pl.program_id(0) == 0)
        def _():
            _fill_masks(logm_ref)

        for q0 in range(0, nq, 2):
            qis = range(q0, min(q0 + 2, nq))
            rows = [slice(qi * BLK, (qi + 1) * BLK) for qi in qis]
            ns = [(qi + 1) * BLK for qi in qis]
            s = [_dot_nt(q_ref[r, :], k_ref[0:n, :]) for r, n in zip(rows, ns)]
            s = [x * scale + _mask_row(logm_ref, qi) for x, qi in zip(s, qis)]
            m = [jnp.max(x, axis=-1, keepdims=True) for x in s]
            p = [jnp.exp(x - mx) for x, mx in zip(s, m)]
            l = [jnp.sum(x, axis=-1, keepdims=True) for x in p]
            pv = [_dot(x, v_ref[0:n, :]) for x, n in zip(p, ns)]
            for r, x, lx, mx in zip(rows, pv, l, m):
                o_ref[r, :] = (x / lx).astype(o_ref.dtype)
                lse_ref[r, :] = jnp.broadcast_to(mx + jnp.log(lx), (BLK, LANE))

    return pl.pallas_call(
        body, name=name, grid=(ATT_HEADS,),
        in_specs=[pl.BlockSpec((T, dh), lambda h: (0, h)),
                  pl.BlockSpec((T, dh), lambda h: (0, ATT_HEADS + h)),
                  pl.BlockSpec((T, dh), lambda h: (0, 2 * ATT_HEADS + h))],
        out_specs=[pl.BlockSpec((T, dh), lambda h: (0, h)),
                   pl.BlockSpec((None, T, LANE), lambda h: (h, 0, 0))],
        out_shape=[jax.ShapeDtypeStruct((T, D), BF16), jax.ShapeDtypeStruct((ATT_HEADS, T, LANE), F32)],
        scratch_shapes=[pltpu.VMEM((N_MASKS, BLK, BLK), F32)],
        compiler_params=_params("arbitrary"))(qkv, qkv, qkv)


def attn_bwd(name, qkv, o, lse, do):
    T = qkv.shape[0]
    D = qkv.shape[1] // 3
    dh = D // ATT_HEADS
    nq = T // BLK
    scale = dh ** -0.5

    _check_mask_classes(T)

    def body(q_ref, k_ref, v_ref, o_ref, lse_ref, do_ref, dq_ref, dk_ref, dv_ref, dk_acc, dv_acc, logm_ref):
        @pl.when(pl.program_id(0) == 0)
        def _():
            _fill_masks(logm_ref)

        dk_acc[...] = jnp.zeros_like(dk_acc)
        dv_acc[...] = jnp.zeros_like(dv_acc)
        for qi in range(nq):
            rows, n = slice(qi * BLK, (qi + 1) * BLK), (qi + 1) * BLK
            q, dout = q_ref[rows, :], do_ref[rows, :]
            kk, vv = k_ref[0:n, :], v_ref[0:n, :]
            delta = jnp.sum(dout.astype(F32) * o_ref[rows, :].astype(F32), axis=-1, keepdims=True)
            lse = jnp.max(lse_ref[rows, :], axis=-1, keepdims=True)
            p = jnp.exp(_dot_nt(q, kk) * scale + _mask_row(logm_ref, qi) - lse)
            ds = (p * (_dot_nt(dout, vv) - delta) * scale).astype(BF16)
            dq_ref[rows, :] = _dot(ds, kk).astype(dq_ref.dtype)
            dk_acc[0:n, :] += _dot_tn(ds, q)
            dv_acc[0:n, :] += _dot_tn(p, dout)
        dk_ref[...] = dk_acc[...].astype(dk_ref.dtype)
        dv_ref[...] = dv_acc[...].astype(dv_ref.dtype)

    full = pl.BlockSpec((T, dh), lambda h: (0, h))
    return pl.pallas_call(
        body, name=name, grid=(ATT_HEADS,),
        in_specs=[full, pl.BlockSpec((T, dh), lambda h: (0, ATT_HEADS + h)),
                  pl.BlockSpec((T, dh), lambda h: (0, 2 * ATT_HEADS + h)),
                  full, pl.BlockSpec((None, T, LANE), lambda h: (h, 0, 0)), full],
        out_specs=[full, full, full],
        out_shape=[jax.ShapeDtypeStruct((T, D), BF16)] * 3,
        scratch_shapes=[pltpu.VMEM((T, dh), F32), pltpu.VMEM((T, dh), F32), pltpu.VMEM((N_MASKS, BLK, BLK), F32)],
        compiler_params=_params("arbitrary"))(qkv, qkv, qkv, o, lse, do)


def _mesh_pos():
    mx, my, mc = lax.axis_index("x"), lax.axis_index("y"), lax.axis_index("c")
    return mx, my, mc, 4 * mx + 2 * my + mc


def _peer(k, mx, my, mc):
    px, py, pc = mx ^ (k >> 2), my ^ ((k >> 1) & 1), mc ^ (k & 1)
    return (px, py, pc), 4 * px + 2 * py + pc


_SIBLING = 1
_OTHER_CHIPS = (4, 2, 6)
N_CHIP = N_DEV // 2
_PLANS = {"gather": (2, N_DEV - 1), "to_chips": (2, 1 + len(_OTHER_CHIPS)), "pass_on": (1, len(_OTHER_CHIPS)),
          "halves": (2, N_CHIP), "chip_sums": (2, len(_OTHER_CHIPS))}


def _copies(kind, items, send_sems, recv_sems):
    mx, my, mc, me = _mesh_pos()
    out = []

    def add(n, src, dst, peer):
        out.append(pltpu.make_async_remote_copy(
            src_ref=src, dst_ref=dst, send_sem=send_sems.at[n], recv_sem=recv_sems.at[n],
            device_id=peer, device_id_type=pl.DeviceIdType.MESH))

    per_item = _PLANS[kind][1]
    sibling = _peer(_SIBLING, mx, my, mc)[0]
    for i, refs in enumerate(items):
        n = i * per_item
        if kind == "gather":
            for k in range(1, N_DEV):
                add(n + k - 1, refs[0], refs[1].at[me], _peer(k, mx, my, mc)[0])
        elif kind == "to_chips":
            rows = refs[0].shape[0]
            dst = refs[1].at[me] if rows == refs[1].shape[1] else refs[1].at[me, pl.ds(0, rows)]
            for j, k in enumerate((_SIBLING,) + _OTHER_CHIPS):
                add(n + j, refs[0], dst, _peer(k, mx, my, mc)[0])
        elif kind == "pass_on":
            for j, k in enumerate(_OTHER_CHIPS):
                add(n + j, refs[0].at[me ^ k], refs[0].at[me ^ k], sibling)
        elif kind == "halves":
            for chip in range(N_CHIP):
                add(n + chip, refs[0].at[2 * chip + 1 - mc], refs[1].at[chip], sibling)
        else:
            for j, k in enumerate(_OTHER_CHIPS):
                peer, to = _peer(k, mx, my, mc)
                add(n + j, refs[0].at[to // 2], refs[1].at[me // 2], peer)
    return out


_HBM = pl.BlockSpec(memory_space=pltpu.HBM)
_SEM = pl.BlockSpec(memory_space=pltpu.SEMAPHORE)
_DATAFLOW = pltpu.SideEffectType.DATAFLOW_SIDE_EFFECTING


def exchange_call(name, waits, starts, deps=()):
    bufs, slot_of = [], {}

    def slots(items):
        out = []
        for item in items:
            for b in item:
                if id(b) not in slot_of:
                    slot_of[id(b)] = len(bufs)
                    bufs.append(b)
            out.append(tuple(slot_of[id(b)] for b in item))
        return out

    wait_plan = [(kind, slots(handle[0])) for kind, handle in waits]
    start_plan = [(kind, slots(items)) for kind, items in starts]
    wait_sems = [s for _, handle in waits for s in handle[1:]]
    n_buf, n_ws, n_start = len(bufs), len(wait_sems), len(starts)

    def body(*refs):
        buf_refs, sems_in = refs[:n_buf], refs[n_buf:n_buf + n_ws]
        outs = refs[n_buf + n_ws + len(deps):]
        pick = lambda plan: [tuple(buf_refs[s] for s in item) for item in plan]
        for wi, (kind, plan) in enumerate(wait_plan):
            copies = _copies(kind, pick(plan), sems_in[2 * wi], sems_in[2 * wi + 1])
            for cp in copies:
                cp.wait_send()
            for cp in copies:
                cp.wait_recv()
        for si, (kind, plan) in enumerate(start_plan):
            for cp in _copies(kind, pick(plan), outs[2 * si], outs[2 * si + 1]):
                cp.start()
        outs[-1][...] = jnp.zeros_like(outs[-1])

    hbm_bufs = [pltpu.with_memory_space_constraint(b, pltpu.HBM) for b in bufs]
    sem_shapes = []
    for kind, plan in start_plan:
        sem_shapes += [pltpu.SemaphoreType.DMA((len(plan) * _PLANS[kind][1],))] * 2
    outs = pl.pallas_call(
        body, name=name,
        out_shape=sem_shapes + [pltpu.HBM(b.shape, b.dtype) for b in bufs] + [jax.ShapeDtypeStruct((8, LANE), F32)],
        in_specs=[_HBM] * n_buf + [_SEM] * n_ws + [_ANY] * len(deps),
        out_specs=[_SEM] * (2 * n_start) + [_HBM] * n_buf + [pl.BlockSpec(memory_space=pltpu.VMEM)],
        input_output_aliases={i: 2 * n_start + i for i in range(n_buf)},
        compiler_params=pltpu.CompilerParams(has_side_effects=_DATAFLOW))(*hbm_bufs, *wait_sems, *deps)
    sems, thru, token = outs[:2 * n_start], outs[2 * n_start:-1], outs[-1]
    through = lambda plan: [tuple(thru[s] for s in item) for item in plan]
    waited = [through(plan) for _, plan in wait_plan]
    handles = [(through(plan), sems[2 * si], sems[2 * si + 1]) for si, (_, plan) in enumerate(start_plan)]
    return waited, handles, token


def gather_small(name, a, deps=()):
    def body(a_ref, *rest):
        o_ref, send_sems, recv_sems, local_sem = rest[len(deps):]
        me = _mesh_pos()[3]
        own = pltpu.make_async_copy(a_ref, o_ref.at[me], local_sem)
        own.start()
        copies = _copies("gather", [(a_ref, o_ref)], send_sems, recv_sems)
        for cp in copies:
            cp.start()
        for cp in copies:
            cp.wait_recv()
        for cp in copies:
            cp.wait_send()
        own.wait()

    return pl.pallas_call(
        body, name=name, in_specs=[_ANY] * (1 + len(deps)), out_specs=_ANY,
        out_shape=jax.ShapeDtypeStruct((N_DEV,) + a.shape, a.dtype),
        scratch_shapes=[pltpu.SemaphoreType.DMA((N_DEV - 1,)), pltpu.SemaphoreType.DMA((N_DEV - 1,)),
                        pltpu.SemaphoreType.DMA],
        compiler_params=pltpu.CompilerParams(has_side_effects=True))(a, *deps)


def _adamw_math(w, g, m, v):
    m2 = ADAM_B1 * m + (1.0 - ADAM_B1) * g
    v2 = ADAM_B2 * v + (1.0 - ADAM_B2) * (g * g)
    m_hat = m2 / (1.0 - ADAM_B1 ** ADAM_STEP)
    v_hat = v2 / (1.0 - ADAM_B2 ** ADAM_STEP)
    delta = -ADAM_LR * (m_hat / (jnp.sqrt(v_hat) + ADAM_EPS) + ADAM_WD * w)
    return delta, m2, v2


def chip_sum(name, a, half):
    _, r, c = a.shape
    tr = r
    chip = 2 * lax.axis_index("x") + lax.axis_index("y")
    where = jnp.stack([lax.axis_index("c"), chip ^ 1, chip ^ 2, chip ^ 3]).astype(jnp.int32)

    def body(where_ref, a_ref, h_ref, o_ref):
        del where_ref
        o_ref[...] = (a_ref[...].astype(F32) + h_ref[...].astype(F32)).astype(o_ref.dtype)

    blk = pl.BlockSpec((None, tr, c), lambda g, i, where: (where[1 + g], i, 0))
    grid_spec = pltpu.PrefetchScalarGridSpec(
        num_scalar_prefetch=1, grid=(N_CHIP - 1, r // tr),
        in_specs=[pl.BlockSpec((None, None, tr, c), lambda g, i, where: (where[1 + g], where[0], i, 0)), blk],
        out_specs=blk)
    return pl.pallas_call(
        body, name=name, grid_spec=grid_spec, out_shape=jax.ShapeDtypeStruct((N_CHIP, r, c), BF16),
        compiler_params=_params("parallel", "parallel"))(where, a.reshape(N_CHIP, 2, r, c), half)


def adamw(name, w, m, v, l, land, a, half, prev=None):
    L, r, c = w.shape
    cp = land.shape[2]
    tr = _pick(r, (256, 176, 128, 64, 32, 16, 8))

    def body(w_ref, m_ref, v_ref, land_ref, a_ref, half_ref, *rest):
        g_ref, d_ref, m2_ref, v2_ref = rest[-4:]
        chip = _mesh_pos()[3] // 2
        mine = a_ref[:, pl.ds(0, c)].astype(F32) + half_ref[:, pl.ds(0, c)].astype(F32)
        g = None
        for s in range(N_CHIP):
            part = jnp.where(chip == s, mine, land_ref[s, :, pl.ds(0, c)].astype(F32))
            g = part if g is None else g + part
        delta, m2, v2 = _adamw_math(w_ref[...], g, m_ref[...], v_ref[...])
        g_ref[...] = g
        d_ref[...] = delta
        m2_ref[...] = m2
        v2_ref[...] = v2

    blk = pl.BlockSpec((None, tr, c), lambda i: (l, i, 0))
    shape = jax.ShapeDtypeStruct((L, r, c), F32)
    extra = [] if prev is None else list(prev)
    return pl.pallas_call(
        body, name=name, grid=(r // tr,),
        in_specs=[blk, blk, blk, pl.BlockSpec((N_CHIP, tr, cp), lambda i: (0, i, 0)),
                  pl.BlockSpec((None, tr, cp), lambda i: (_mesh_pos()[3], i, 0)),
                  pl.BlockSpec((None, tr, cp), lambda i: (_mesh_pos()[3] // 2, i, 0))] + [_ANY] * len(extra),
        out_specs=[blk] * 4, out_shape=[shape] * 4,
        input_output_aliases={6 + k: k for k in range(len(extra))},
        compiler_params=_params("parallel"))(w, m, v, land, a, half, *extra)


def adamw_columns(name, w, m, v, land, a, half):
    r, _, D = w.shape
    tc = _pick(D, (256, 128))

    def body(w_ref, m_ref, v_ref, land_ref, a_ref, half_ref, g_ref, d_ref, m2_ref, v2_ref):
        chip = _mesh_pos()[3] // 2
        mine = a_ref[...].astype(F32) + half_ref[...].astype(F32)
        g = None
        for s in range(N_CHIP):
            part = jnp.where(chip == s, mine, land_ref[s].astype(F32))
            g = part if g is None else g + part
        flat = lambda ref: ref[...].reshape(r, tc)
        delta, m2, v2 = _adamw_math(flat(w_ref), g, flat(m_ref), flat(v_ref))
        for ref, val in ((g_ref, g), (d_ref, delta), (m2_ref, m2), (v2_ref, v2)):
            ref[...] = val.reshape(r, 1, tc)

    blk = pl.BlockSpec((r, 1, tc), lambda i: (0, 0, i))
    shape = jax.ShapeDtypeStruct((r, 1, D), F32)
    return pl.pallas_call(
        body, name=name, grid=(D // tc,),
        in_specs=[blk, blk, blk, pl.BlockSpec((N_CHIP, r, tc), lambda i: (0, 0, i)),
                  pl.BlockSpec((None, r, tc), lambda i: (_mesh_pos()[3], 0, i)),
                  pl.BlockSpec((None, r, tc), lambda i: (_mesh_pos()[3] // 2, 0, i))],
        out_specs=[blk] * 4, out_shape=[shape] * 4,
        compiler_params=_params("parallel"))(w, m, v, land, a, half)


def adamw_small(name, w, m, v, parts):
    n = w.shape[1]

    def body(w_ref, m_ref, v_ref, p_ref, g_ref, d_ref, m2_ref, v2_ref):
        g = p_ref[0:1, :]
        for s in range(1, N_DEV):
            g = g + p_ref[s:s + 1, :]
        delta, m2, v2 = _adamw_math(w_ref[...], g, m_ref[...], v_ref[...])
        g_ref[...] = g
        d_ref[...] = delta
        m2_ref[...] = m2
        v2_ref[...] = v2

    shape = jax.ShapeDtypeStruct((1, n), F32)
    return pl.pallas_call(body, name=name, out_shape=[shape] * 4,
                          compiler_params=pltpu.CompilerParams(vmem_limit_bytes=VMEM_LIMIT_BYTES))(w, m, v, parts)


def _rope_tables(positions):
    half = RET_DK // 2
    inv_freq = 1.0 / jnp.power(RET_THETA_BASE, jnp.linspace(0.0, 1.0, half, dtype=F32))
    ang = positions.astype(F32)[:, None] * inv_freq
    cos, sin = jnp.cos(ang), jnp.sin(ang)
    cosf = jnp.repeat(cos, 2, axis=-1)
    sins = jnp.stack([-sin, sin], axis=-1).reshape(cosf.shape)
    return cosf, sins


def _pad_to(a, axis, size):
    pad = [(0, 0)] * a.ndim
    pad[axis] = (0, size - a.shape[axis])
    return jnp.pad(a, pad)


def _round_up(n, m):
    return -(-n // m) * m


def kernel(x, p, positions, attn_norm_w, ffn_norm_w, ple_norm_w, final_norm_w, ab_w_in, ab_gla_gate_up, ab_gla_gate_b, ab_ret_norm_w, ab_gla_norm_w, ab_w_out, c_w_qkv, c_w_out, ffn_w_gate, ffn_w_up, ffn_w_down, ple_w_proj, ple_w_gate, loss_target, m_attn_norm_w, m_ffn_norm_w, m_ple_norm_w, m_final_norm_w, m_ab_w_in, m_ab_gla_gate_up, m_ab_gla_gate_b, m_ab_ret_norm_w, m_ab_gla_norm_w, m_ab_w_out, m_c_w_qkv, m_c_w_out, m_ffn_w_gate, m_ffn_w_up, m_ffn_w_down, m_ple_w_proj, m_ple_w_gate, v_attn_norm_w, v_ffn_norm_w, v_ple_norm_w, v_final_norm_w, v_ab_w_in, v_ab_gla_gate_up, v_ab_gla_gate_b, v_ab_ret_norm_w, v_ab_gla_norm_w, v_ab_w_out, v_c_w_qkv, v_c_w_out, v_ffn_w_gate, v_ffn_w_up, v_ffn_w_down, v_ple_w_proj, v_ple_w_gate):
    T, D = x.shape[1], x.shape[2]
    depth = attn_norm_w.shape[0]
    assert ab_w_in.shape[0] == 1 and c_w_qkv.shape[0] == 1 and depth == 2, "one even and one odd layer"
    me = 4 * lax.axis_index("x") + 2 * lax.axis_index("y") + lax.axis_index("c")
    in_shard = ab_w_in.shape[2]
    in_width = in_shard * N_DEV
    assert in_width == OFF_LR + GLA_GATE_RANK
    fs = ffn_w_gate.shape[2]
    fp = _round_up(fs, LANE)
    gu_cols = ab_gla_gate_up.shape[2]

    bf = lambda a: a.astype(BF16)
    tr_ = lambda a: jnp.swapaxes(a, -1, -2)
    wg_t, wu_t = tr_(ffn_w_gate), tr_(ffn_w_up)
    srcs = {"w_in": bf(tr_(ab_w_in[0]))}
    group_keys = [["w_in"], ["gu", "w_oab"], ["wg0", "wu0"], ["wd0", "wpg0", "wpp0"], ["w_qkv", "w_oc"],
                  ["wg1", "wu1"], ["wd1", "wpg1", "wpp1"]]
    G_IN, G_OUT, G_QKV = 0, 1, 4
    g_ffn = lambda layer: (2, 3) if layer == 0 else (5, 6)

    def landing(key):
        a = srcs[key]
        rows = fp if key[:2] in ("wg", "wu", "wd") else a.shape[0]
        buf = lax.empty((N_DEV, rows) + a.shape[1:], a.dtype)
        if rows > a.shape[0]:
            zeros = jnp.zeros((N_DEV, rows - a.shape[0]) + a.shape[1:], a.dtype)
            buf = lax.dynamic_update_slice(buf, zeros, (0, a.shape[0]) + (0,) * (a.ndim - 1))
        return lax.dynamic_update_slice(buf, a[None], (me,) + (0,) * a.ndim)

    _, chip_handles, gather_token = exchange_call(
        "gather_start_in", [], [("to_chips", [(srcs[k], landing(k)) for k in group_keys[G_IN]])])
    (gather_token, w_out_, gu_, w_qkv_, w_oc_, wg_, wu_, wd_, wpg_, wpp_) = lax.optimization_barrier(
        (gather_token, ab_w_out, ab_gla_gate_up, c_w_qkv, c_w_out, wg_t, wu_t, ffn_w_down, ple_w_gate, ple_w_proj))
    srcs.update(w_oab=bf(w_out_[0]), gu=gu_[0], w_qkv=bf(w_qkv_[0]), w_oc=bf(w_oc_[0]))
    for l in range(depth):
        srcs[f"wg{l}"] = bf(wg_[l])
        srcs[f"wu{l}"] = bf(wu_[l])
        srcs[f"wd{l}"] = bf(wd_[l])
        srcs[f"wpg{l}"] = bf(wpg_[l])
        srcs[f"wpp{l}"] = bf(wpp_[l])
    _, more, gather_token = exchange_call(
        "gather_start", [], [("to_chips", [(srcs[k], landing(k)) for k in keys]) for keys in group_keys[1:]],
        deps=(gather_token,))
    chip_handles = chip_handles + more
    weights = {}

    def gather_wait(gi, dep):
        lands = [(land,) for _, land in chip_handles[gi][0]]
        _, (passing,), _ = exchange_call(
            f"gather{gi}_pass", [("to_chips", chip_handles[gi])], [("pass_on", lands)], deps=(dep,))
        (complete,), _, _ = exchange_call(f"gather{gi}_done", [("pass_on", passing)], [])
        weights.update(zip(group_keys[gi], [land for (land,) in complete]))

    gb = ab_gla_gate_b
    hn_w = jnp.concatenate([ab_ret_norm_w, ab_gla_norm_w], axis=1)
    cosf, sins = _rope_tables(positions[0])
    p_bf = bf(p[:, 0])

    xs = x[0]
    saved = []
    for i in range(depth):
        nm = f"l{i}_"
        w_attn, w_ffn, w_ple = attn_norm_w[i:i + 1], ffn_norm_w[i:i + 1], ple_norm_w[i:i + 1]
        (xn,) = rowwise(nm + "norm_attn", lambda a, w: (_rms(a, w),), T, [("row", xs), ("full", w_attn)],
                        [("row", D, BF16)], deps=(gather_token,) if i == 0 else ())
        if i % 2 == 0:
            gather_wait(G_IN, xn)
            w_in_t = weights["w_in"].reshape(1, 1, in_width, D)
            w_lr_t = _pad_to(w_in_t[0, 0, OFF_LR:], 0, LANE).reshape(1, 1, LANE, D)
            z = mmt_fwd(nm + "mm_in", xn, w_in_t, 0, F32, n=OFF_LR)
            glr = mmt_fwd(nm + "mm_lr", xn, w_lr_t, 0, F32)
            oraw = retention_fwd(nm + "ret_fwd", z, cosf, sins, RET_V + GLA_V)
            gather_wait(G_OUT, oraw)
            w_oab = weights["w_oab"].reshape(1, 1, D, D)
            gu_full = _pad_to(weights["gu"].transpose(1, 0, 2).reshape(GLA_GATE_RANK, GLA_QK), 0, LANE)
            oraw, gla_states = gla_fwd(nm + "gla_fwd", z, glr, gu_full, gb, oraw)
            o = headnorm_fwd(nm + "headnorm_fwd", oraw, z, hn_w)
            h1, hn = mm_add_norm(nm + "mm_out", o, w_oab, xs, w_ffn)
            mixer_saved = (z, glr, oraw, o, gla_states)
        else:
            gather_wait(G_QKV, xn)
            w_qkv = weights["w_qkv"].reshape((1,) + weights["w_qkv"].shape)
            w_oc = weights["w_oc"].reshape(1, 1, D, D)
            qkv = mm_nn(nm + "mm_qkv", xn, w_qkv, 0, BF16)
            o, lse = attn_fwd(nm + "attn_fwd", qkv)
            h1, hn = mm_add_norm(nm + "mm_out", o, w_oc, xs, w_ffn)
            mixer_saved = (qkv, o, lse)
        gather_wait(g_ffn(i)[0], hn)
        wg = weights[f"wg{i}"].reshape(1, N_DEV, fp, D)
        wu = weights[f"wu{i}"].reshape(1, N_DEV, fp, D)
        dup, dgate, act = ffn_gate_up(nm + "ffn_gate_up", hn, wg, wu)
        gather_wait(g_ffn(i)[1], act)
        wd = weights[f"wd{i}"].reshape(1, 1, N_DEV * fp, D)
        wpg = weights[f"wpg{i}"].reshape(1, 1, D, D)
        wpp = weights[f"wpp{i}"].reshape((1,) + weights[f"wpp{i}"].shape)
        h2, pn = mm_add_norm(nm + "mm_down", act, wd, h1, w_ple)
        x_next, s, e = ple_fwd(nm + "ple", pn, wpg, p_bf[i], wpp, h2)
        mixer_w = (w_in_t, w_lr_t, w_oab, gu_full) if i % 2 == 0 else (w_qkv, w_oc)
        saved.append((xs, xn, mixer_saved, mixer_w, (wg, wu, wd, wpg), h1, hn, dup, dgate, act, h2, pn, s, e))
        xs = x_next

    def ple_bwd(d, sv, ev):
        gate = _sigmoid(sv)
        return d * gate, d * ev * gate * (1.0 - gate)

    def loss_fn(a, w, t, sv, ev):
        diff = _rms(a, w) - t
        dx, dw = _rms_bwd(a, w, diff * (1.0 / D))
        part = 0.5 * jnp.sum(jnp.mean(diff * diff, axis=-1, keepdims=True), axis=0, keepdims=True)
        return (dx, dw, jnp.broadcast_to(part, (1, LANE))) + ple_bwd(dx, sv, ev)

    dx, d_final_w, loss_part, *ple_grads = rowwise(
        "loss_head", loss_fn, T,
        [("row", xs), ("full", final_norm_w[None, :]), ("row", loss_target[0]), ("row", saved[-1][-2]), ("row", saved[-1][-1])],
        [("row", D, F32), ("acc", D), ("acc", LANE), ("row", D, BF16), ("row", D, BF16)])
    loss = lax.psum(loss_part[0, 0], ("x", "y", "c"))

    grads = {}
    on_chip = []
    scatters = []

    def scatter_start(name, keys, deps=()):
        waits = [("halves", on_chip[0][1])] if on_chip else []
        starts = [("halves", [(grads[k], lax.empty((N_CHIP,) + grads[k].shape[1:], BF16)) for k in keys])] if keys else []
        waited, handles, token = exchange_call(name, waits, starts, deps=deps)
        if on_chip:
            done_keys, _ = on_chip.pop()
            sums = [chip_sum(f"{name}_sum{j}", a, half) for j, (a, half) in enumerate(waited[0])]
            _, (handle,), token = exchange_call(
                name + "_chips", [], [("chip_sums", [(cs, lax.empty(cs.shape, BF16)) for cs in sums])])
            scatters.append((done_keys, handle, waited[0]))
        if keys:
            on_chip.append((keys, handles[0]))
        return token

    d_attn_w, d_ffn_w, d_ple_w = [None] * depth, [None] * depth, [None] * depth
    for i in reversed(range(depth)):
        nm = f"l{i}_b_"
        xs_i, xn, mixer_saved, mixer_w, (wg, wu, wd, wpg), h1, hn, dup, dgate, act, h2, pn, _, _ = saved[i]
        w_attn, w_ffn, w_ple = attn_norm_w[i:i + 1], ffn_norm_w[i:i + 1], ple_norm_w[i:i + 1]

        first_deps = (loss.reshape(1, 1),) if i == depth - 1 else ()
        de, ds = ple_grads
        grads[("ple_w_proj", i)] = mm_tn(nm + "mm_ple_proj_w", p_bf[i], de, N_DEV, BF16, deps=first_deps)
        grads[("ple_w_gate", i)] = mm_tn(nm + "mm_ple_gate_w", pn, ds, 1, BF16).reshape(N_DEV, D // N_DEV, D)
        dpn = mm_nt(nm + "mm_ple_gate_x", ds, wpg, 0, F32)

        def norm_bwd_add(a, w, dn, dres):
            dxx, dw = _rms_bwd(a, w, dn)
            tot = dres + dxx
            return tot, tot, dw

        dh2, dh2_bf, d_ple_w[i] = rowwise(nm + "norm_ple", norm_bwd_add, T,
                                          [("row", h2), ("full", w_ple), ("row", dpn), ("row", dx)],
                                          [("row", D, F32), ("row", D, BF16), ("acc", D)])
        grads[("ffn_w_down", i)] = mm_tn(nm + "mm_down_w", act, dh2_bf, 1, BF16).reshape(N_DEV, fp, D)
        token = scatter_start(nm + "scatter_ple_down", [("ple_w_proj", i), ("ple_w_gate", i), ("ffn_w_down", i)])
        dg, du = ffn_down_bwd(nm + "ffn_down_x", dh2_bf, wd, dup, dgate, deps=(token,))
        grads[("ffn_w_gate", i)] = mmt_dw(nm + "mm_gate_w", dg, hn, N_DEV, BF16)
        grads[("ffn_w_up", i)] = mmt_dw(nm + "mm_up_w", du, hn, N_DEV, BF16)
        token = scatter_start(nm + "scatter_gate_up", [("ffn_w_gate", i), ("ffn_w_up", i)])
        dhn = mmt_dx_pair(nm + "mm_gate_up_x", dg, wg, du, wu, F32, deps=(token,))
        dh1, dh1_bf, d_ffn_w[i] = rowwise(nm + "norm_ffn", norm_bwd_add, T,
                                          [("row", h1), ("full", w_ffn), ("row", dhn), ("row", dh2)],
                                          [("row", D, F32), ("row", D, BF16), ("acc", D)])
        if i % 2 == 0:
            z, glr, oraw, o, gla_states = mixer_saved
            w_in_t, w_lr_t, w_oab, gu_full = mixer_w
            grads[("ab_w_out", 0)] = mm_tn(nm + "mm_out_w", o, dh1_bf, 1, BF16).reshape(N_DEV, D // N_DEV, D)
            token = scatter_start(nm + "scatter_out", [("ab_w_out", 0)])
            do = mm_nt(nm + "mm_out_x", dh1_bf, w_oab, 0, F32, deps=(token,))
            d_oraw, d_gates, d_hn_w = headnorm_bwd(nm + "headnorm", oraw, z, hn_w, do)
            d_rq, d_rk, d_rv = retention_bwd(nm + "ret", z, cosf, sins, d_oraw)
            d_gq, d_gk, d_gv, d_glr4, d_gu, d_gb = gla_bwd(nm + "gla", z, glr, gu_full, gb, d_oraw, gla_states)
            dz = jnp.concatenate([d_rq, d_rk, d_rv, d_gates[:, :RET_V], d_gq, d_gk, d_gv, d_gates[:, RET_V:]], axis=1)
            (d_glr,) = rowwise(nm + "sum_lr", lambda *a: (a[0] + a[1] + a[2] + a[3],), T,
                               [("row", d_glr4[hh]) for hh in range(GLA_HEADS)], [("row", LANE, BF16)])
            dwt_in = mmt_dw(nm + "mm_in_w", dz, xn, 1, BF16, rows=in_width)
            dwt_in = mmt_dw_rows(nm + "mm_lr_w", d_glr, xn, dwt_in, OFF_LR, GLA_GATE_RANK)
            grads[("ab_w_in", 0)] = dwt_in.reshape(N_DEV, in_shard, D)
            token = scatter_start(nm + "scatter_in", [("ab_w_in", 0)])
            dxn_a = mmt_dx_wide(nm + "mm_in_x", dz, w_in_t, F32, n=OFF_LR, deps=(token,))
            token = scatter_start(nm + "scatter_in_on", [], deps=(dxn_a,))
            dxn_b = mmt_dx(nm + "mm_lr_x", d_glr, w_lr_t, 0, F32, deps=(token,))
        else:
            qkv, o, lse = mixer_saved
            w_qkv, w_oc = mixer_w
            grads[("c_w_out", 0)] = mm_tn(nm + "mm_out_w", o, dh1_bf, 1, BF16).reshape(N_DEV, D // N_DEV, D)
            do = mm_nt(nm + "mm_out_x", dh1_bf, w_oc, 0, BF16)
            dq, dk, dv = attn_bwd(nm + "attn", qkv, o, lse, do)
            dqkv = jnp.concatenate([dq, dk, dv], axis=1)
            grads[("c_w_qkv", 0)] = mm_tn(nm + "mm_qkv_w", xn, dqkv, N_DEV, BF16)
            token = scatter_start(nm + "scatter_attn", [("c_w_out", 0), ("c_w_qkv", 0)])
            dxn_a = mm_nt_wide(nm + "mm_qkv_x", dqkv, w_qkv, F32, deps=(token,))
            dxn_b = None
        dxn = [dxn_a] if dxn_b is None else [dxn_a, dxn_b]

        below = list(saved[i - 1][-2:]) if i > 0 else []

        def norm_bwd_in(a, w, *rest, n_d=len(dxn)):
            dxx, dw = _rms_bwd(a, w, sum(rest[1:n_d], rest[0]))
            tot = rest[n_d] + dxx
            return (tot, dw) + (ple_bwd(tot, *rest[n_d + 1:]) if len(rest) > n_d + 1 else ())

        dx, d_attn_w[i], *ple_grads = rowwise(
            nm + "norm_attn", norm_bwd_in, T,
            [("row", xs_i), ("full", w_attn)] + [("row", d) for d in dxn + [dh1] + below],
            [("row", D, F32), ("acc", D)] + [("row", D, BF16)] * len(below))

    small_names = ["attn_norm_w", "ffn_norm_w", "ple_norm_w", "final_norm_w", "ab_gla_gate_b", "ab_ret_norm_w",
                   "ab_gla_norm_w"]
    small_grads = [jnp.concatenate(d_attn_w, 0), jnp.concatenate(d_ffn_w, 0), jnp.concatenate(d_ple_w, 0), d_final_w[0],
                   d_gb, d_hn_w[:, :RET_V], d_hn_w[:, RET_V:]]
    small_w = [attn_norm_w, ffn_norm_w, ple_norm_w, final_norm_w, ab_gla_gate_b, ab_ret_norm_w, ab_gla_norm_w]
    small_m = [m_attn_norm_w, m_ffn_norm_w, m_ple_norm_w, m_final_norm_w, m_ab_gla_gate_b, m_ab_ret_norm_w, m_ab_gla_norm_w]
    small_v = [v_attn_norm_w, v_ffn_norm_w, v_ple_norm_w, v_final_norm_w, v_ab_gla_gate_b, v_ab_ret_norm_w, v_ab_gla_norm_w]
    sizes = [int(np.prod(a.shape)) for a in small_w]
    n_gu = GLA_GATE_RANK * GLA_QK
    n_small = _round_up(sum(sizes) + n_gu, LANE)
    pack = lambda parts: _pad_to(jnp.concatenate([a.reshape(-1) for a in parts]), 0, n_small)[None, :]
    small_part = pack(small_grads + [d_gu[:GLA_GATE_RANK]])

    cols_first = lambda a: jnp.transpose(a, (2, 0, 1))
    big_w = dict(ab_w_in=tuple(cols_first(a) for a in (ab_w_in, m_ab_w_in, v_ab_w_in)),
                 ab_w_out=(ab_w_out, m_ab_w_out, v_ab_w_out),
                 c_w_qkv=(c_w_qkv, m_c_w_qkv, v_c_w_qkv), c_w_out=(c_w_out, m_c_w_out, v_c_w_out),
                 ffn_w_gate=(wg_t, tr_(m_ffn_w_gate), tr_(v_ffn_w_gate)),
                 ffn_w_up=(wu_t, tr_(m_ffn_w_up), tr_(v_ffn_w_up)),
                 ffn_w_down=(ffn_w_down, m_ffn_w_down, v_ffn_w_down), ple_w_proj=(ple_w_proj, m_ple_w_proj, v_ple_w_proj),
                 ple_w_gate=(ple_w_gate, m_ple_w_gate, v_ple_w_gate))
    if on_chip:
        scatter_start("scatter_last", [], deps=(dx,))
    results, last = {}, dx
    for gi, (keys, handle, partials) in enumerate(scatters):
        (arrived,), _, _ = exchange_call(f"scatter_wait{gi}", [("chip_sums", handle)], [], deps=(last,))
        for (n, l), (_, land), (a, half) in zip(keys, arrived, partials):
            if n == "ab_w_in":
                results[n] = adamw_columns(f"adamw_{n}", *big_w[n], land, a, half)
            else:
                results[n] = adamw(f"adamw_{n}{l}", *big_w[n], l, land, a, half, prev=results.get(n))
            last = results[n][0]
    for n in ("ffn_w_gate", "ffn_w_up"):
        results[n] = [tr_(a) for a in results[n]]
    results["ab_w_in"] = [jnp.transpose(a, (1, 2, 0)) for a in results["ab_w_in"]]
    small_parts = gather_small("gather_small", small_part, deps=(last,)).reshape(N_DEV, n_small)

    gu_off = sum(sizes)
    own_cols = lambda a: lax.dynamic_slice_in_dim(a.reshape(GLA_GATE_RANK, GLA_QK), me * gu_cols, gu_cols, axis=1)
    small_res = adamw_small("adamw_small", pack(small_w + [jnp.zeros((n_gu,), F32)]),
                            pack(small_m + [jnp.zeros((n_gu,), F32)]), pack(small_v + [jnp.ones((n_gu,), F32)]),
                            small_parts)
    g_gu_full = small_res[0][0, gu_off:gu_off + n_gu]
    g_gu = own_cols(g_gu_full)[None]
    gu_res = adamw_small("adamw_gate_up", *[_pad_to(a.reshape(1, -1), 1, _round_up(a.size, LANE)) for a in
                                            (ab_gla_gate_up, m_ab_gla_gate_up, v_ab_gla_gate_up)],
                         jnp.concatenate([_pad_to(g_gu.reshape(1, -1), 1, _round_up(g_gu.size, LANE)),
                                          jnp.zeros((N_DEV - 1, _round_up(g_gu.size, LANE)), F32)], axis=0))
    for k in range(4):
        off = 0
        for n, a, sz in zip(small_names, small_w, sizes):
            results.setdefault(n, [None] * 4)[k] = small_res[k][0, off:off + sz].reshape(a.shape)
            off += sz
        results.setdefault("ab_gla_gate_up", [None] * 4)[k] = gu_res[k][0, :g_gu.size].reshape(ab_gla_gate_up.shape)

    order = ["attn_norm_w", "ffn_norm_w", "ple_norm_w", "final_norm_w", "ab_w_in", "ab_gla_gate_up", "ab_gla_gate_b",
             "ab_ret_norm_w", "ab_gla_norm_w", "ab_w_out", "c_w_qkv", "c_w_out", "ffn_w_gate", "ffn_w_up", "ffn_w_down",
             "ple_w_proj", "ple_w_gate"]
    return (loss, dx[None], *[results[n][0] for n in order], *[results[n][1] for n in order],
            *[results[n][2] for n in order], *[results[n][3] for n in order])
```

```python
import math

import numpy as np
import jax
import jax.numpy as jnp
from jax import lax
from jax.experimental import pallas as pl
from jax.experimental.pallas import tpu as pltpu

F32 = jnp.float32
BF16 = jnp.bfloat16
HIGHEST = lax.Precision.HIGHEST

N_DEV = 8
VMEM_LIMIT_BYTES = 48 * 1024 * 1024
LANE = 128
NORM_EPS = 1e-6

RET_HEADS, RET_DK, RET_DV = 4, 256, 256
RET_THETA_BASE = 10000.0
GLA_HEADS, GLA_DK, GLA_DV = 4, 128, 256
GLA_GATE_RANK = 16
GLA_GATE_NORM = 16.0
CHUNK = 64
ATT_HEADS = 16
DILATED_BRANCHES = ((128, 1), (512, 4), (2048, 16))
BLK = 256

ADAM_LR, ADAM_B1, ADAM_B2, ADAM_EPS, ADAM_WD, ADAM_STEP = 0.001, 0.9, 0.999, 1e-08, 0.01, 10

RET_QK = RET_HEADS * RET_DK
RET_V = RET_HEADS * RET_DV
GLA_QK = GLA_HEADS * GLA_DK
GLA_V = GLA_HEADS * GLA_DV
OFF_RQ, OFF_RK, OFF_RV, OFF_RG = 0, RET_QK, 2 * RET_QK, 2 * RET_QK + RET_V
OFF_GQ = OFF_RG + RET_V
OFF_GK = OFF_GQ + GLA_QK
OFF_GV = OFF_GK + GLA_QK
OFF_GG = OFF_GV + GLA_V
OFF_LR = OFF_GG + GLA_V


def _params(*sem):
    return pltpu.CompilerParams(dimension_semantics=sem or None, vmem_limit_bytes=VMEM_LIMIT_BYTES)


def _pick(n, cands):
    for c in cands:
        if n % c == 0:
            return c
    raise ValueError(f"no tile for {n} in {cands}")


_NN = (((1,), (0,)), ((), ()))
_NT = (((1,), (1,)), ((), ()))
_TN = (((0,), (0,)), ((), ()))
_ANY = pl.BlockSpec(memory_space=pl.ANY)
MAX_CONTRACT = 2048
_TILES = (1024, 768, 512, 256, 128)


def _mm_call(name, dims, grid, in_specs, out_spec, out_shape, args, deps=()):
    steps = grid[2]
    assert steps == 1 or out_shape.dtype == F32

    def body(a_ref, b_ref, *rest):
        o_ref = rest[len(deps)]
        part = lax.dot_general(a_ref[...].astype(BF16), b_ref[...].astype(BF16), dims, preferred_element_type=F32)
        if steps == 1:
            o_ref[...] = part.astype(o_ref.dtype)
        else:
            _accumulate(o_ref, part, pl.program_id(2) == 0)

    return pl.pallas_call(
        body, name=name, grid=grid, in_specs=list(in_specs) + [_ANY] * len(deps), out_specs=out_spec,
        out_shape=out_shape, compiler_params=_params("parallel", "parallel", "arbitrary"))(*args, *deps)


def mm_nn(name, a, w, l, out_dtype, deps=()):
    _, J, K, n = w.shape
    M = a.shape[0]
    tm, tn, tk = _pick(M, _TILES), _pick(n, _TILES), _pick(K, (MAX_CONTRACT,) + _TILES)
    nt = n // tn
    return _mm_call(
        name, _NN, (M // tm, J * nt, K // tk),
        [pl.BlockSpec((tm, tk), lambda i, j, k: (i, k)),
         pl.BlockSpec((None, None, tk, tn), lambda i, j, k: (l, j // nt, k, j % nt))],
        pl.BlockSpec((tm, tn), lambda i, j, k: (i, j)),
        jax.ShapeDtypeStruct((M, J * n), out_dtype), (a, w), deps)


def mm_nt(name, a, w, l, out_dtype, deps=()):
    _, J, K, n = w.shape
    M = a.shape[0]
    tm, tq, tc = _pick(M, _TILES), _pick(K, _TILES), _pick(n, (MAX_CONTRACT,) + _TILES)
    nc = n // tc
    return _mm_call(
        name, _NT, (M // tm, K // tq, J * nc),
        [pl.BlockSpec((tm, tc), lambda i, q, c: (i, c)),
         pl.BlockSpec((None, None, tq, tc), lambda i, q, c: (l, c // nc, q, c % nc))],
        pl.BlockSpec((tm, tq), lambda i, q, c: (i, q)),
        jax.ShapeDtypeStruct((M, K), out_dtype), (a, w), deps)


def mm_tn(name, x, dy, J, out_dtype, deps=()):
    M, K = x.shape
    n = dy.shape[1] // J
    tp, tn = _pick(K, _TILES), _pick(n, _TILES)
    nt = n // tn
    assert M <= MAX_CONTRACT
    return _mm_call(
        name, _TN, (K // tp, J * nt, 1),
        [pl.BlockSpec((M, tp), lambda i, j, r: (0, i)),
         pl.BlockSpec((M, tn), lambda i, j, r: (0, j))],
        pl.BlockSpec((None, tp, tn), lambda i, j, r: (j // nt, i, j % nt)),
        jax.ShapeDtypeStruct((J, K, n), out_dtype), (x, dy), deps)


def mmt_fwd(name, a, wt, l, out_dtype, n=None, deps=()):
    _, J, rows, K = wt.shape
    n = rows if n is None else n
    M = a.shape[0]
    tm, tn = _pick(M, _TILES), _pick(n, _TILES)
    nt = n // tn
    assert K <= MAX_CONTRACT
    return _mm_call(
        name, _NT, (M // tm, J * nt, 1),
        [pl.BlockSpec((tm, K), lambda i, j, k: (i, 0)),
         pl.BlockSpec((None, None, tn, K), lambda i, j, k: (l, j // nt, j % nt, 0))],
        pl.BlockSpec((tm, tn), lambda i, j, k: (i, j)),
        jax.ShapeDtypeStruct((M, J * n), out_dtype), (a, wt), deps)


def mmt_dx(name, dy, wt, l, out_dtype, n=None, deps=()):
    _, J, rows, K = wt.shape
    n = rows if n is None else n
    M = dy.shape[0]
    tm, tq, tc = _pick(M, _TILES), _pick(K, _TILES), _pick(n, _TILES)
    nc = n // tc
    return _mm_call(
        name, _NN, (M // tm, K // tq, J * nc),
        [pl.BlockSpec((tm, tc), lambda i, q, c: (i, c)),
         pl.BlockSpec((None, None, tc, tq), lambda i, q, c: (l, c // nc, c % nc, q))],
        pl.BlockSpec((tm, tq), lambda i, q, c: (i, q)),
        jax.ShapeDtypeStruct((M, K), out_dtype), (dy, wt), deps)


WIDE_TILE = 512


def _wide_call(name, body, M, K, a, w, a_spec, w_spec, out_dtype, deps):
    def kernel_body(a_ref, w_ref, *rest):
        o_ref = rest[len(deps)]
        o_ref[...] = body(a_ref, w_ref).astype(o_ref.dtype)

    return pl.pallas_call(
        kernel_body, name=name, grid=(M // WIDE_TILE, K // WIDE_TILE),
        in_specs=[a_spec, w_spec] + [_ANY] * len(deps),
        out_specs=pl.BlockSpec((WIDE_TILE, WIDE_TILE), lambda i, q: (i, q)),
        out_shape=jax.ShapeDtypeStruct((M, K), out_dtype),
        compiler_params=_params("parallel", "parallel"))(a, w, *deps)


def mmt_dx_wide(name, dy, wt, out_dtype, n=None, deps=()):
    _, J, rows, K = wt.shape
    n = rows if n is None else n
    M = dy.shape[0]

    def body(dy_ref, w_ref):
        return jnp.dot(dy_ref[...].astype(BF16), w_ref[...].reshape(J * n, WIDE_TILE), preferred_element_type=F32)

    return _wide_call(name, body, M, K, dy, wt,
                      pl.BlockSpec((WIDE_TILE, J * n), lambda i, q: (i, 0)),
                      pl.BlockSpec((None, J, n, WIDE_TILE), lambda i, q: (0, 0, 0, q)), out_dtype, deps)


def mmt_dx_pair(name, dy1, wt1, dy2, wt2, out_dtype, deps=()):
    _, J, n, K = wt1.shape
    M = dy1.shape[0]

    def body(dy1_ref, w1_ref, dy2_ref, w2_ref, *rest):
        o_ref = rest[len(deps)]
        acc = jnp.dot(dy1_ref[...], w1_ref[...].reshape(J * n, WIDE_TILE), preferred_element_type=F32)
        acc = acc + jnp.dot(dy2_ref[...], w2_ref[...].reshape(J * n, WIDE_TILE), preferred_element_type=F32)
        o_ref[...] = acc.astype(o_ref.dtype)

    rows = _once((WIDE_TILE, J * n), lambda i, q: (i, 0))
    cols = pl.BlockSpec((None, J, n, WIDE_TILE), lambda i, q: (0, 0, 0, q))
    return pl.pallas_call(
        body, name=name, grid=(M // WIDE_TILE, K // WIDE_TILE),
        in_specs=[rows, cols, rows, cols] + [_ANY] * len(deps),
        out_specs=pl.BlockSpec((WIDE_TILE, WIDE_TILE), lambda i, q: (i, q)),
        out_shape=jax.ShapeDtypeStruct((M, K), out_dtype),
        compiler_params=_params("parallel", "parallel"))(dy1, wt1, dy2, wt2, *deps)


def mm_nt_wide(name, a, w, out_dtype, deps=()):
    _, J, K, n = w.shape
    M = a.shape[0]

    def body(a_ref, w_ref):
        acc = None
        for j in range(J):
            part = lax.dot_general(a_ref[:, j * n:(j + 1) * n].astype(BF16), w_ref[j], _NT, preferred_element_type=F32)
            acc = part if acc is None else acc + part
        return acc

    return _wide_call(name, body, M, K, a, w,
                      pl.BlockSpec((WIDE_TILE, J * n), lambda i, q: (i, 0)),
                      pl.BlockSpec((None, J, WIDE_TILE, n), lambda i, q: (0, 0, q, 0)), out_dtype, deps)


def mmt_dw(name, dy, x, J, out_dtype, deps=(), rows=None):
    M, K = x.shape
    n = dy.shape[1] // J
    tn, tp = _pick(n, _TILES), _pick(K, _TILES)
    nt = n // tn
    assert M <= MAX_CONTRACT
    return _mm_call(
        name, _TN, (J * nt, K // tp, 1),
        [pl.BlockSpec((M, tn), lambda j, i, r: (0, j)),
         pl.BlockSpec((M, tp), lambda j, i, r: (0, i))],
        pl.BlockSpec((None, tn, tp), lambda j, i, r: (j // nt, j % nt, i)),
        jax.ShapeDtypeStruct((J, n if rows is None else rows, K), out_dtype), (dy, x), deps)


def mmt_dw_rows(name, dy, x, out, row0, rank):
    M, K = x.shape
    tp = _pick(K, _TILES)

    def body(dy_ref, x_ref, prev_ref, o_ref):
        del prev_ref
        full = lax.dot_general(dy_ref[...], x_ref[...], _TN, preferred_element_type=F32)
        o_ref[...] = full[:rank].astype(o_ref.dtype)

    return pl.pallas_call(
        body, name=name, grid=(K // tp,),
        in_specs=[pl.BlockSpec((M, dy.shape[1]), lambda i: (0, 0)), pl.BlockSpec((M, tp), lambda i: (0, i)), _ANY],
        out_specs=pl.BlockSpec((None, rank, tp), lambda i: (0, row0 // rank, i)),
        out_shape=jax.ShapeDtypeStruct(out.shape, out.dtype), input_output_aliases={2: 0},
        compiler_params=_params("parallel"))(dy, x, out)


def ffn_gate_up(name, a, wg, wu):
    _, J, n, K = wg.shape
    M = a.shape[0]
    tm, tn = _pick(M, _TILES), _pick(n, _TILES)
    nt = n // tn
    assert K <= MAX_CONTRACT

    def body(a_ref, wg_ref, wu_ref, dup_ref, dgate_ref, act_ref):
        x = a_ref[...]
        g = lax.dot_general(x, wg_ref[...], _NT, preferred_element_type=F32)
        u = lax.dot_general(x, wu_ref[...], _NT, preferred_element_type=F32)
        silu, dsilu = _silu_and_grad(g)
        dup_ref[...] = silu.astype(dup_ref.dtype)
        dgate_ref[...] = (u * dsilu).astype(dgate_ref.dtype)
        act_ref[...] = (silu * u).astype(act_ref.dtype)

    w_spec = pl.BlockSpec((None, None, tn, K), lambda i, j: (0, j // nt, j % nt, 0))
    out = pl.BlockSpec((tm, tn), lambda i, j: (i, j))
    return pl.pallas_call(
        body, name=name, grid=(M // tm, J * nt),
        in_specs=[pl.BlockSpec((tm, K), lambda i, j: (i, 0)), w_spec, w_spec],
        out_specs=[out] * 3, out_shape=[jax.ShapeDtypeStruct((M, J * n), BF16)] * 3,
        compiler_params=_params("parallel", "parallel"))(a, wg, wu)


def mm_add_norm(name, a, w, res, norm_w):
    _, _, K, N = w.shape
    M = a.shape[0]
    tm, tk = _pick(M, (WIDE_TILE, 256)), _pick(K, (1024, 512, 256))
    steps = K // tk

    def body(a_ref, w_ref, res_ref, nw_ref, h_ref, hn_ref):
        k = pl.program_id(1)
        part = jnp.dot(a_ref[...], w_ref[...], preferred_element_type=F32)
        _accumulate(h_ref, part, k == 0)

        @pl.when(k == steps - 1)
        def _():
            h = h_ref[...] + res_ref[...]
            h_ref[...] = h
            hn_ref[...] = _rms(h, nw_ref[...]).astype(hn_ref.dtype)

    rows = pl.BlockSpec((tm, N), lambda i, k: (i, 0))
    return pl.pallas_call(
        body, name=name, grid=(M // tm, steps),
        in_specs=[pl.BlockSpec((tm, tk), lambda i, k: (i, k)),
                  pl.BlockSpec((None, None, tk, N), lambda i, k: (0, 0, k, 0)), rows,
                  pl.BlockSpec((1, N), lambda i, k: (0, 0))],
        out_specs=[rows, rows],
        out_shape=[jax.ShapeDtypeStruct((M, N), F32), jax.ShapeDtypeStruct((M, N), BF16)],
        compiler_params=_params("parallel", "arbitrary"))(a, w, res, norm_w)


def ple_fwd(name, pn, wpg, p_in, wpp, h):
    _, J, P, n = wpp.shape
    M, D = h.shape
    tm, tn = _pick(M, (WIDE_TILE, 256)), _pick(D, _TILES)
    per_tile = tn // n

    def body(pn_ref, wg_ref, p_ref, wp_ref, h_ref, x_ref, s_ref, e_ref):
        s = jnp.dot(pn_ref[...], wg_ref[...], preferred_element_type=F32)
        p_blk = p_ref[...]
        e = jnp.concatenate([jnp.dot(p_blk, wp_ref[j], preferred_element_type=F32) for j in range(per_tile)], axis=1)
        s_ref[...] = s
        e_ref[...] = e
        x_ref[...] = h_ref[...] + _sigmoid(s) * e

    tile = pl.BlockSpec((tm, tn), lambda i, j: (i, j))
    return pl.pallas_call(
        body, name=name, grid=(M // tm, D // tn),
        in_specs=[pl.BlockSpec((tm, D), lambda i, j: (i, 0)),
                  pl.BlockSpec((None, None, D, tn), lambda i, j: (0, 0, 0, j)),
                  pl.BlockSpec((tm, P), lambda i, j: (i, 0)),
                  pl.BlockSpec((None, per_tile, P, n), lambda i, j: (0, j, 0, 0)), tile],
        out_specs=[tile] * 3, out_shape=[jax.ShapeDtypeStruct((M, D), F32)] * 3,
        compiler_params=_params("parallel", "parallel"))(pn, wpg, p_in, wpp, h)


def ffn_down_bwd(name, dy, wd, dup, dgate, deps=()):
    _, _, K, n = wd.shape
    M = dy.shape[0]
    tm, tq = _pick(M, _TILES), _pick(K, _TILES)
    assert n <= MAX_CONTRACT

    def body(dy_ref, w_ref, dup_ref, dgate_ref, *rest):
        dg_ref, du_ref = rest[len(deps):]
        dact = lax.dot_general(dy_ref[...], w_ref[...], _NT, preferred_element_type=F32)
        dg_ref[...] = (dact * dgate_ref[...].astype(F32)).astype(dg_ref.dtype)
        du_ref[...] = (dact * dup_ref[...].astype(F32)).astype(du_ref.dtype)

    blk = pl.BlockSpec((tm, tq), lambda i, q: (i, q))
    return pl.pallas_call(
        body, name=name, grid=(M // tm, K // tq),
        in_specs=[pl.BlockSpec((tm, n), lambda i, q: (i, 0)),
                  pl.BlockSpec((None, None, tq, n), lambda i, q: (0, 0, q, 0)), blk, blk] + [_ANY] * len(deps),
        out_specs=[blk, blk], out_shape=[jax.ShapeDtypeStruct((M, K), BF16)] * 2,
        compiler_params=_params("parallel", "parallel"))(dy, wd, dup, dgate, *deps)


def rowwise(name, fn, rows, ins, outs, tr=256, deps=()):
    widest = max([s[1].shape[1] if s[0] != "col" else s[3] for s in ins] + [s[1] for s in outs])
    tr = min(tr if widest <= 2048 else tr // 2, rows)
    in_specs, args = [], []
    for spec in ins:
        kind, a = spec[0], spec[1]
        if kind == "row":
            in_specs.append(pl.BlockSpec((tr, a.shape[1]), lambda i: (i, 0)))
        elif kind == "col":
            cb, width = spec[2], spec[3]
            in_specs.append(pl.BlockSpec((tr, width), lambda i, cb=cb: (i, cb)))
        else:
            in_specs.append(pl.BlockSpec(a.shape, lambda i: (0, 0)))
        args.append(a)
    out_specs, out_shapes = [], []
    for spec in outs:
        if spec[0] == "row":
            out_specs.append(pl.BlockSpec((tr, spec[1]), lambda i: (i, 0)))
            out_shapes.append(jax.ShapeDtypeStruct((rows, spec[1]), spec[2]))
        else:
            out_specs.append(pl.BlockSpec((1, spec[1]), lambda i: (0, 0)))
            out_shapes.append(jax.ShapeDtypeStruct((1, spec[1]), F32))
    n_in = len(ins)

    def body(*refs):
        vals = fn(*[r[...] for r in refs[:n_in]])
        first = pl.program_id(0) == 0
        for r, v, spec in zip(refs[n_in + len(deps):], vals, outs):
            if spec[0] == "row":
                r[...] = v.astype(r.dtype)
            else:
                _accumulate(r, v, first)

    return pl.pallas_call(body, name=name, grid=(rows // tr,), in_specs=in_specs + [_ANY] * len(deps),
                          out_specs=out_specs, out_shape=out_shapes,
                          compiler_params=_params("arbitrary"))(*args, *deps)


def _accumulate(ref, v, first):
    @pl.when(first)
    def _():
        ref[...] = v

    @pl.when(jnp.logical_not(first))
    def _():
        ref[...] += v


def _rms(x, w):
    r = lax.rsqrt(jnp.mean(x * x, axis=-1, keepdims=True) + NORM_EPS)
    return x * r * w


def _rms_bwd(x, w, dy):
    r = lax.rsqrt(jnp.mean(x * x, axis=-1, keepdims=True) + NORM_EPS)
    g = dy * w
    dx = r * (g - x * (r * r) * jnp.mean(g * x, axis=-1, keepdims=True))
    dw = jnp.sum(dy * x * r, axis=0, keepdims=True)
    return dx, dw


def _sigmoid(x):
    return 1.0 / (1.0 + jnp.exp(-x))


def _silu_and_grad(g):
    s = _sigmoid(g)
    return g * s, s * (1.0 + g * (1.0 - s))


def _swap_pairs(x):
    n = x.shape[-1]
    lane = lax.broadcasted_iota(jnp.int32, x.shape, x.ndim - 1)
    return jnp.where((lane & 1) == 0, pltpu.roll(x, n - 1, x.ndim - 1), pltpu.roll(x, 1, x.ndim - 1))


def _rot(x, cosf, sins):
    return x * cosf + _swap_pairs(x) * sins


def _unrot(d, cosf, sins):
    return d * cosf + _swap_pairs(d * sins)


def _ret_log_gamma(h):
    vals = [math.log1p(-2.0 ** (-5.0 - i)) for i in range(RET_HEADS)]
    out = jnp.float32(vals[RET_HEADS - 1])
    for i in range(RET_HEADS - 2, -1, -1):
        out = jnp.where(h == i, jnp.float32(vals[i]), out)
    return out


def _fill_decays(dec_ref, lg):
    ri = lax.broadcasted_iota(jnp.int32, (BLK, BLK), 0)
    ci = lax.broadcasted_iota(jnp.int32, (BLK, BLK), 1)
    for d in range(dec_ref.shape[0]):
        dt = d * BLK + ri - ci
        dec_ref[d] = jnp.where(dt >= 0, jnp.exp(jnp.maximum(dt, 0).astype(F32) * lg), 0.0)


def _decay_row(dec_ref, qi):
    return jnp.concatenate([dec_ref[qi - kb] for kb in range(qi + 1)], axis=1)


def _once(block_shape, index_map):
    return pl.BlockSpec(block_shape, index_map, pipeline_mode=pl.Buffered(1))


def _dot(a, b):
    return jnp.dot(a.astype(BF16), b.astype(BF16), preferred_element_type=F32)


def _dot_nt(a, b):
    return lax.dot_general(a.astype(BF16), b.astype(BF16), _NT, preferred_element_type=F32)


def _dot_tn(a, b):
    return lax.dot_general(a.astype(BF16), b.astype(BF16), _TN, preferred_element_type=F32)


def retention_fwd(name, z, cosf, sins, width_out):
    T = z.shape[0]
    nq = T // BLK
    scale = RET_DK ** -0.5

    def body(q_ref, k_ref, v_ref, cos_ref, sin_ref, o_ref, krot, vb, dec_ref):
        _fill_decays(dec_ref, _ret_log_gamma(pl.program_id(0)))
        krot[...] = (_rot(k_ref[...], cos_ref[...], sin_ref[...]) * scale).astype(BF16)
        vb[...] = v_ref[...].astype(BF16)
        for qi in range(nq):
            rows, n = slice(qi * BLK, (qi + 1) * BLK), (qi + 1) * BLK
            q = _rot(q_ref[rows, :], cos_ref[rows, :], sin_ref[rows, :])
            s = _dot_nt(q, krot[0:n, :]) * _decay_row(dec_ref, qi)
            o_ref[rows, :] = _dot(s, vb[0:n, :])

    return pl.pallas_call(
        body, name=name, grid=(RET_HEADS,),
        in_specs=[pl.BlockSpec((T, RET_DK), lambda h: (0, OFF_RQ // RET_DK + h)),
                  pl.BlockSpec((T, RET_DK), lambda h: (0, OFF_RK // RET_DK + h)),
                  pl.BlockSpec((T, RET_DV), lambda h: (0, OFF_RV // RET_DV + h)),
                  _once((T, RET_DK), lambda h: (0, 0)), _once((T, RET_DK), lambda h: (0, 0))],
        out_specs=pl.BlockSpec((T, RET_DV), lambda h: (0, h)),
        out_shape=jax.ShapeDtypeStruct((T, width_out), F32),
        scratch_shapes=[pltpu.VMEM((T, RET_DK), BF16), pltpu.VMEM((T, RET_DV), BF16),
                        pltpu.VMEM((nq, BLK, BLK), F32)],
        compiler_params=_params("arbitrary"))(z, z, z, cosf, sins)


def retention_bwd(name, z, cosf, sins, do):
    T = z.shape[0]
    nq = T // BLK
    scale = RET_DK ** -0.5

    def body(q_ref, k_ref, v_ref, cos_ref, sin_ref, do_ref, dq_ref, dk_ref, dv_ref, krot, vb, dk_acc, dv_acc, dec_ref):
        _fill_decays(dec_ref, _ret_log_gamma(pl.program_id(0)))
        krot[...] = (_rot(k_ref[...], cos_ref[...], sin_ref[...]) * scale).astype(BF16)
        vb[...] = v_ref[...].astype(BF16)
        dk_acc[...] = jnp.zeros_like(dk_acc)
        dv_acc[...] = jnp.zeros_like(dv_acc)
        for qi in range(nq):
            rows, n = slice(qi * BLK, (qi + 1) * BLK), (qi + 1) * BLK
            cos_q, sin_q = cos_ref[rows, :], sin_ref[rows, :]
            q = _rot(q_ref[rows, :], cos_q, sin_q).astype(BF16)
            dout = do_ref[rows, :].astype(BF16)
            kk, vv, dec = krot[0:n, :], vb[0:n, :], _decay_row(dec_ref, qi)
            p = (_dot_nt(q, kk) * dec).astype(BF16)
            ds = (_dot_nt(dout, vv) * dec).astype(BF16)
            dq_ref[rows, :] = _unrot(_dot(ds, kk), cos_q, sin_q).astype(dq_ref.dtype)
            dk_acc[0:n, :] += _dot_tn(ds, q)
            dv_acc[0:n, :] += _dot_tn(p, dout)
        dk_ref[...] = (_unrot(dk_acc[...], cos_ref[...], sin_ref[...]) * scale).astype(dk_ref.dtype)
        dv_ref[...] = dv_acc[...].astype(dv_ref.dtype)

    head = lambda h: (0, h)
    return pl.pallas_call(
        body, name=name, grid=(RET_HEADS,),
        in_specs=[pl.BlockSpec((T, RET_DK), lambda h: (0, OFF_RQ // RET_DK + h)),
                  pl.BlockSpec((T, RET_DK), lambda h: (0, OFF_RK // RET_DK + h)),
                  pl.BlockSpec((T, RET_DV), lambda h: (0, OFF_RV // RET_DV + h)),
                  _once((T, RET_DK), lambda h: (0, 0)), _once((T, RET_DK), lambda h: (0, 0)),
                  pl.BlockSpec((T, RET_DV), head)],
        out_specs=[pl.BlockSpec((T, RET_DK), head), pl.BlockSpec((T, RET_DK), head), pl.BlockSpec((T, RET_DV), head)],
        out_shape=[jax.ShapeDtypeStruct((T, RET_QK), BF16), jax.ShapeDtypeStruct((T, RET_QK), BF16),
                   jax.ShapeDtypeStruct((T, RET_V), BF16)],
        scratch_shapes=[pltpu.VMEM((T, RET_DK), BF16), pltpu.VMEM((T, RET_DV), BF16),
                        pltpu.VMEM((T, RET_DK), F32), pltpu.VMEM((T, RET_DV), F32),
                        pltpu.VMEM((nq, BLK, BLK), F32)],
        compiler_params=_params("arbitrary"))(z, z, z, cosf, sins, do)


GLA_PAIR = 2


def _gla_chunk(q_ref, k_ref, v_ref, glr_ref, gu, gb, rows, hh, trilf):
    ck = slice(hh * GLA_DK, (hh + 1) * GLA_DK)
    zg = _dot(glr_ref[rows, :], gu[:, ck]) + gb[:, ck]
    la = (jnp.minimum(zg, 0.0) - jnp.log(1.0 + jnp.exp(-jnp.abs(zg)))) * (1.0 / GLA_GATE_NORM)
    cum = jnp.dot(trilf, la, precision=HIGHEST, preferred_element_type=F32)
    last = jnp.sum(la, axis=0, keepdims=True)
    ecum = jnp.exp(cum)
    k = k_ref[rows, ck]
    qt = q_ref[rows, ck] * (GLA_DK ** -0.5) * ecum
    kt = k * jnp.exp(-cum)
    kh = k * jnp.exp(last - cum)
    return zg, cum, last, ecum, qt, kt, kh, v_ref[rows, hh * GLA_DV:(hh + 1) * GLA_DV].astype(BF16)


def _state_decay(last):
    e = jnp.exp(jnp.broadcast_to(last, (GLA_DK, GLA_DK)).T)
    return jnp.concatenate([e] * (GLA_DV // GLA_DK), axis=1)


def _gla_specs(T):
    wk, wv = GLA_PAIR * GLA_DK, GLA_PAIR * GLA_DV
    return [_once((T, wk), lambda h: (0, OFF_GQ // wk + h)),
            _once((T, wk), lambda h: (0, OFF_GK // wk + h)),
            _once((T, wv), lambda h: (0, OFF_GV // wv + h)),
            _once((T, LANE), lambda h: (0, 0)),
            pl.BlockSpec((LANE, wk), lambda h: (0, h)),
            pl.BlockSpec((1, wk), lambda h: (0, h))]


def gla_fwd(name, z, glr, gu, gb, o_prev):
    T = z.shape[0]
    nc = T // CHUNK
    wv = GLA_PAIR * GLA_DV

    def body(q_ref, k_ref, v_ref, glr_ref, gu_ref, gb_ref, prev_ref, o_ref, s_all_ref, *S):
        del prev_ref
        gu_b, gb_v = gu_ref[...].astype(BF16), gb_ref[...]
        ri = lax.broadcasted_iota(jnp.int32, (CHUNK, CHUNK), 0)
        ci = lax.broadcasted_iota(jnp.int32, (CHUNK, CHUNK), 1)
        tril = ri >= ci
        trilf = tril.astype(F32)
        for s_ref in S:
            s_ref[...] = jnp.zeros_like(s_ref)

        def step(c, carry):
            rows = pl.ds(pl.multiple_of(c * CHUNK, CHUNK), CHUNK)
            heads = range(GLA_PAIR)
            ch = [_gla_chunk(q_ref, k_ref, v_ref, glr_ref, gu_b, gb_v, rows, hh, trilf) for hh in heads]
            a = [jnp.where(tril, _dot_nt(ch[hh][4], ch[hh][5]), 0.0) for hh in heads]
            s_prev = [S[hh][...] for hh in heads]
            intra = [_dot(a[hh], ch[hh][7]) for hh in heads]
            inter = [_dot(ch[hh][4], s_prev[hh]) for hh in heads]
            added = [_dot_tn(ch[hh][6], ch[hh][7]) for hh in heads]
            for hh in heads:
                o_ref[rows, hh * GLA_DV:(hh + 1) * GLA_DV] = intra[hh] + inter[hh]
                s_all_ref[hh, c] = s_prev[hh]
                S[hh][...] = s_prev[hh] * _state_decay(ch[hh][2]) + added[hh]
            return carry

        lax.fori_loop(0, nc, step, 0)

    n_in = 6
    return pl.pallas_call(
        body, name=name, grid=(GLA_HEADS // GLA_PAIR,),
        in_specs=_gla_specs(T) + [pl.BlockSpec(memory_space=pl.ANY)],
        out_specs=[pl.BlockSpec((T, wv), lambda h: (0, RET_V // wv + h)),
                   pl.BlockSpec((GLA_PAIR, nc, GLA_DK, GLA_DV), lambda h: (h, 0, 0, 0))],
        out_shape=[jax.ShapeDtypeStruct(o_prev.shape, F32),
                   jax.ShapeDtypeStruct((GLA_HEADS, nc, GLA_DK, GLA_DV), F32)],
        scratch_shapes=[pltpu.VMEM((GLA_DK, GLA_DV), F32)] * GLA_PAIR,
        input_output_aliases={n_in: 0},
        compiler_params=_params("arbitrary"))(z, z, z, glr, gu, gb, o_prev)


def gla_bwd(name, z, glr, gu, gb, do, states):
    T = z.shape[0]
    nc = T // CHUNK

    def body(q_ref, k_ref, v_ref, glr_ref, gu_ref, gb_ref, do_ref, s_all,
             dq_ref, dk_ref, dv_ref, dglr_ref, dgu_ref, dgb_ref, dS):
        gu_b, gb_v = gu_ref[...].astype(BF16), gb_ref[...]
        ri = lax.broadcasted_iota(jnp.int32, (CHUNK, CHUNK), 0)
        ci = lax.broadcasted_iota(jnp.int32, (CHUNK, CHUNK), 1)
        tril = ri >= ci
        trilf = tril.astype(F32)
        triuf = (ri <= ci).astype(F32)
        last_row = lax.broadcasted_iota(jnp.int32, (CHUNK, GLA_DK), 0) == CHUNK - 1
        ones8 = jnp.ones((8, GLA_DV), F32)

        heads = range(GLA_PAIR)

        dS[...] = jnp.zeros_like(dS)
        dgu_ref[...] = jnp.zeros_like(dgu_ref)
        dgb_ref[...] = jnp.zeros_like(dgb_ref)

        def bstep(i, carry):
            c = nc - 1 - i
            rows = pl.ds(pl.multiple_of(c * CHUNK, CHUNK), CHUNK)
            glr_c = glr_ref[rows, :]
            cks = [slice(hh * GLA_DK, (hh + 1) * GLA_DK) for hh in heads]
            cvs = [slice(hh * GLA_DV, (hh + 1) * GLA_DV) for hh in heads]
            ch = [_gla_chunk(q_ref, k_ref, v_ref, glr_ref, gu_b, gb_v, rows, hh, trilf) for hh in heads]
            zg, cum, last, ecum, qt, kt, kh, v = [[ch[hh][j] for hh in heads] for j in range(8)]
            s_prev = [s_all[hh, c] for hh in heads]
            ds_new = [dS[hh] for hh in heads]
            dout = [do_ref[rows, cvs[hh]].astype(BF16) for hh in heads]
            a = [jnp.where(tril, _dot_nt(qt[hh], kt[hh]), 0.0) for hh in heads]
            da = [jnp.where(tril, _dot_nt(dout[hh], v[hh]), 0.0) for hh in heads]
            dv_a = [_dot_tn(a[hh], dout[hh]) for hh in heads]
            dv_b = [_dot(kh[hh], ds_new[hh]) for hh in heads]
            dqt_a = [_dot(da[hh], kt[hh]) for hh in heads]
            dqt_b = [_dot_nt(dout[hh], s_prev[hh]) for hh in heads]
            dkt = [_dot_tn(da[hh], qt[hh]) for hh in heads]
            dkh = [_dot_nt(v[hh], ds_new[hh]) for hh in heads]
            ds_add = [_dot_tn(qt[hh], dout[hh]) for hh in heads]
            rs = [lax.dot_general(ones8, ds_new[hh] * s_prev[hh], _NT, precision=HIGHEST, preferred_element_type=F32)
                  for hh in heads]
            dcum = []
            for hh in heads:
                dv_ref[rows, cvs[hh]] = (dv_a[hh] + dv_b[hh]).astype(dv_ref.dtype)
                dS[hh] = ds_new[hh] * _state_decay(last[hh]) + ds_add[hh]
                dqt = dqt_a[hh] + dqt_b[hh]
                dq_ref[rows, cks[hh]] = (dqt * ecum[hh] * (GLA_DK ** -0.5)).astype(dq_ref.dtype)
                dk_ref[rows, cks[hh]] = (dkt[hh] * jnp.exp(-cum[hh])
                                         + dkh[hh] * jnp.exp(last[hh] - cum[hh])).astype(dk_ref.dtype)
                dkh_kh = dkh[hh] * kh[hh]
                dlast = (jnp.sum(dkh_kh, axis=0, keepdims=True)
                         + jnp.exp(last[hh]) * (jnp.sum(rs[hh], axis=0, keepdims=True) * 0.125))
                dcum.append(dqt * qt[hh] - dkt[hh] * kt[hh] - dkh_kh + jnp.where(last_row, dlast, 0.0))
            dla = [jnp.dot(triuf, dcum[hh], precision=HIGHEST, preferred_element_type=F32) for hh in heads]
            dzg = [dla[hh] * (1.0 / GLA_GATE_NORM) * _sigmoid(-zg[hh]) for hh in heads]
            dglr = [_dot_nt(dzg[hh], gu_b[:, cks[hh]]) for hh in heads]
            dgu = [_dot_tn(glr_c, dzg[hh]) for hh in heads]
            for hh in heads:
                dglr_ref[hh, rows, :] = dglr[hh]
                dgu_ref[:, cks[hh]] += dgu[hh]
                dgb_ref[:, cks[hh]] += jnp.sum(dzg[hh], axis=0, keepdims=True)
            return carry

        lax.fori_loop(0, nc, bstep, 0)

    wk, wv = GLA_PAIR * GLA_DK, GLA_PAIR * GLA_DV
    return pl.pallas_call(
        body, name=name, grid=(GLA_HEADS // GLA_PAIR,),
        in_specs=_gla_specs(T) + [_once((T, wv), lambda h: (0, RET_V // wv + h)),
                                  _once((GLA_PAIR, nc, GLA_DK, GLA_DV), lambda h: (h, 0, 0, 0))],
        out_specs=[pl.BlockSpec((T, wk), lambda h: (0, h)), pl.BlockSpec((T, wk), lambda h: (0, h)),
                   pl.BlockSpec((T, wv), lambda h: (0, h)),
                   pl.BlockSpec((GLA_PAIR, T, LANE), lambda h: (h, 0, 0)),
                   pl.BlockSpec((LANE, wk), lambda h: (0, h)), pl.BlockSpec((1, wk), lambda h: (0, h))],
        out_shape=[jax.ShapeDtypeStruct((T, GLA_QK), BF16), jax.ShapeDtypeStruct((T, GLA_QK), BF16),
                   jax.ShapeDtypeStruct((T, GLA_V), BF16), jax.ShapeDtypeStruct((GLA_HEADS, T, LANE), F32),
                   jax.ShapeDtypeStruct((LANE, GLA_QK), F32), jax.ShapeDtypeStruct((1, GLA_QK), F32)],
        scratch_shapes=[pltpu.VMEM((GLA_PAIR, GLA_DK, GLA_DV), F32)],
        compiler_params=_params("arbitrary"))(z, z, z, glr, gu, gb, do, states)


HN_HEADS = RET_HEADS + GLA_HEADS
HN_W = RET_DV


def _gate_col(h):
    return jnp.where(h < RET_HEADS, OFF_RG // HN_W + h, OFF_GG // HN_W + h - RET_HEADS)


def headnorm_fwd(name, oraw, z, w):
    T = oraw.shape[0]
    tr = _pick(T, _TILES)

    def body(o_ref, g_ref, w_ref, y_ref):
        y_ref[...] = (_rms(o_ref[...], w_ref[...]) * _silu_and_grad(g_ref[...])[0]).astype(y_ref.dtype)

    return pl.pallas_call(
        body, name=name, grid=(HN_HEADS, T // tr),
        in_specs=[pl.BlockSpec((tr, HN_W), lambda h, i: (i, h)),
                  pl.BlockSpec((tr, HN_W), lambda h, i: (i, _gate_col(h))),
                  pl.BlockSpec((1, HN_W), lambda h, i: (0, h))],
        out_specs=pl.BlockSpec((tr, HN_W), lambda h, i: (i, h)),
        out_shape=jax.ShapeDtypeStruct((T, HN_HEADS * HN_W), BF16),
        compiler_params=_params("arbitrary", "arbitrary"))(oraw, z, w)


def headnorm_bwd(name, oraw, z, w, dy):
    T = oraw.shape[0]
    tr = _pick(T, _TILES)

    def body(o_ref, g_ref, w_ref, dy_ref, do_ref, dg_ref, dw_ref):
        o, wv, dyv = o_ref[...], w_ref[...], dy_ref[...].astype(F32)
        silu, dsilu = _silu_and_grad(g_ref[...])
        n = _rms(o, wv)
        dg_ref[...] = (dyv * n * dsilu).astype(dg_ref.dtype)
        dx, dw = _rms_bwd(o, wv, dyv * silu)
        do_ref[...] = dx
        _accumulate(dw_ref, dw, pl.program_id(1) == 0)

    blk = pl.BlockSpec((tr, HN_W), lambda h, i: (i, h))
    return pl.pallas_call(
        body, name=name, grid=(HN_HEADS, T // tr),
        in_specs=[blk, pl.BlockSpec((tr, HN_W), lambda h, i: (i, _gate_col(h))),
                  pl.BlockSpec((1, HN_W), lambda h, i: (0, h)), blk],
        out_specs=[blk, blk, pl.BlockSpec((1, HN_W), lambda h, i: (0, h))],
        out_shape=[jax.ShapeDtypeStruct((T, HN_HEADS * HN_W), F32),
                   jax.ShapeDtypeStruct((T, HN_HEADS * HN_W), BF16),
                   jax.ShapeDtypeStruct((1, HN_HEADS * HN_W), F32)],
        compiler_params=_params("arbitrary", "arbitrary"))(oraw, z, w, dy)


N_MASKS = 4


def _check_mask_classes(T):
    for window, dilation in DILATED_BRANCHES[:-1]:
        assert window < (N_MASKS - 1) * BLK - (BLK - 1) and BLK % dilation == 0
    assert DILATED_BRANCHES[-1][0] >= T and BLK % DILATED_BRANCHES[-1][1] == 0


def _fill_masks(logm_ref):
    ri = lax.broadcasted_iota(jnp.int32, (BLK, BLK), 0)
    ci = lax.broadcasted_iota(jnp.int32, (BLK, BLK), 1)
    for d in range(N_MASKS):
        dt = d * BLK + ri - ci
        mult = jnp.zeros((BLK, BLK), F32)
        for window, dilation in DILATED_BRANCHES:
            hit = (dt >= 0) & (dt <= window) & ((dt & (dilation - 1)) == 0)
            mult = mult + hit.astype(F32)
        logm_ref[d] = jnp.where(mult > 0, jnp.log(jnp.maximum(mult, 1.0)), -1e30)


def _mask_row(ref, qi):
    return jnp.concatenate([ref[min(qi - kb, N_MASKS - 1)] for kb in range(qi + 1)], axis=1)


def attn_fwd(name, qkv):
    T = qkv.shape[0]
    D = qkv.shape[1] // 3
    dh = D // ATT_HEADS
    nq = T // BLK
    scale = dh ** -0.5

    _check_mask_classes(T)

    def body(q_ref, k_ref, v_ref, o_ref, lse_ref, logm_ref):
        @pl.when(pl.program_id(0) == 0)
        def _():
            _fill_masks(logm_ref)

        for q0 in range(0, nq, 2):
            qis = range(q0, min(q0 + 2, nq))
            rows = [slice(qi * BLK, (qi + 1) * BLK) for qi in qis]
            ns = [(qi + 1) * BLK for qi in qis]
            s = [_dot_nt(q_ref[r, :], k_ref[0:n, :]) for r, n in zip(rows, ns)]
            s = [x * scale + _mask_row(logm_ref, qi) for x, qi in zip(s, qis)]
            m = [jnp.max(x, axis=-1, keepdims=True) for x in s]
            p = [jnp.exp(x - mx) for x, mx in zip(s, m)]
            l = [jnp.sum(x, axis=-1, keepdims=True) for x in p]
            pv = [_dot(x, v_ref[0:n, :]) for x, n in zip(p, ns)]
            for r, x, lx, mx in zip(rows, pv, l, m):
                o_ref[r, :] = (x / lx).astype(o_ref.dtype)
                lse_ref[r, :] = jnp.broadcast_to(mx + jnp.log(lx), (BLK, LANE))

    return pl.pallas_call(
        body, name=name, grid=(ATT_HEADS,),
        in_specs=[pl.BlockSpec((T, dh), lambda h: (0, h)),
                  pl.BlockSpec((T, dh), lambda h: (0, ATT_HEADS + h)),
                  pl.BlockSpec((T, dh), lambda h: (0, 2 * ATT_HEADS + h))],
        out_specs=[pl.BlockSpec((T, dh), lambda h: (0, h)),
                   pl.BlockSpec((None, T, LANE), lambda h: (h, 0, 0))],
        out_shape=[jax.ShapeDtypeStruct((T, D), BF16), jax.ShapeDtypeStruct((ATT_HEADS, T, LANE), F32)],
        scratch_shapes=[pltpu.VMEM((N_MASKS, BLK, BLK), F32)],
        compiler_params=_params("arbitrary"))(qkv, qkv, qkv)


def attn_bwd(name, qkv, o, lse, do):
    T = qkv.shape[0]
    D = qkv.shape[1] // 3
    dh = D // ATT_HEADS
    nq = T // BLK
    scale = dh ** -0.5

    _check_mask_classes(T)

    def body(q_ref, k_ref, v_ref, o_ref, lse_ref, do_ref, dq_ref, dk_ref, dv_ref, dk_acc, dv_acc, logm_ref):
        @pl.when(pl.program_id(0) == 0)
        def _():
            _fill_masks(logm_ref)

        dk_acc[...] = jnp.zeros_like(dk_acc)
        dv_acc[...] = jnp.zeros_like(dv_acc)
        for qi in range(nq):
            rows, n = slice(qi * BLK, (qi + 1) * BLK), (qi + 1) * BLK
            q, dout = q_ref[rows, :], do_ref[rows, :]
            kk, vv = k_ref[0:n, :], v_ref[0:n, :]
            delta = jnp.sum(dout.astype(F32) * o_ref[rows, :].astype(F32), axis=-1, keepdims=True)
            lse = jnp.max(lse_ref[rows, :], axis=-1, keepdims=True)
            p = jnp.exp(_dot_nt(q, kk) * scale + _mask_row(logm_ref, qi) - lse)
            ds = (p * (_dot_nt(dout, vv) - delta) * scale).astype(BF16)
            dq_ref[rows, :] = _dot(ds, kk).astype(dq_ref.dtype)
            dk_acc[0:n, :] += _dot_tn(ds, q)
            dv_acc[0:n, :] += _dot_tn(p, dout)
        dk_ref[...] = dk_acc[...].astype(dk_ref.dtype)
        dv_ref[...] = dv_acc[...].astype(dv_ref.dtype)

    full = pl.BlockSpec((T, dh), lambda h: (0, h))
    return pl.pallas_call(
        body, name=name, grid=(ATT_HEADS,),
        in_specs=[full, pl.BlockSpec((T, dh), lambda h: (0, ATT_HEADS + h)),
                  pl.BlockSpec((T, dh), lambda h: (0, 2 * ATT_HEADS + h)),
                  full, pl.BlockSpec((None, T, LANE), lambda h: (h, 0, 0)), full],
        out_specs=[full, full, full],
        out_shape=[jax.ShapeDtypeStruct((T, D), BF16)] * 3,
        scratch_shapes=[pltpu.VMEM((T, dh), F32), pltpu.VMEM((T, dh), F32), pltpu.VMEM((N_MASKS, BLK, BLK), F32)],
        compiler_params=_params("arbitrary"))(qkv, qkv, qkv, o, lse, do)


def _mesh_pos():
    mx, my, mc = lax.axis_index("x"), lax.axis_index("y"), lax.axis_index("c")
    return mx, my, mc, 4 * mx + 2 * my + mc


def _peer(k, mx, my, mc):
    px, py, pc = mx ^ (k >> 2), my ^ ((k >> 1) & 1), mc ^ (k & 1)
    return (px, py, pc), 4 * px + 2 * py + pc


_SIBLING = 1
_OTHER_CHIPS = (4, 2, 6)
N_CHIP = N_DEV // 2
_PLANS = {"gather": (2, N_DEV - 1), "to_chips": (2, 1 + len(_OTHER_CHIPS)), "pass_on": (1, len(_OTHER_CHIPS)),
          "halves": (2, N_CHIP), "chip_sums": (2, len(_OTHER_CHIPS))}


def _copies(kind, items, send_sems, recv_sems):
    mx, my, mc, me = _mesh_pos()
    out = []

    def add(n, src, dst, peer):
        out.append(pltpu.make_async_remote_copy(
            src_ref=src, dst_ref=dst, send_sem=send_sems.at[n], recv_sem=recv_sems.at[n],
            device_id=peer, device_id_type=pl.DeviceIdType.MESH))

    per_item = _PLANS[kind][1]
    sibling = _peer(_SIBLING, mx, my, mc)[0]
    for i, refs in enumerate(items):
        n = i * per_item
        if kind == "gather":
            for k in range(1, N_DEV):
                add(n + k - 1, refs[0], refs[1].at[me], _peer(k, mx, my, mc)[0])
        elif kind == "to_chips":
            rows = refs[0].shape[0]
            dst = refs[1].at[me] if rows == refs[1].shape[1] else refs[1].at[me, pl.ds(0, rows)]
            for j, k in enumerate((_SIBLING,) + _OTHER_CHIPS):
                add(n + j, refs[0], dst, _peer(k, mx, my, mc)[0])
        elif kind == "pass_on":
            for j, k in enumerate(_OTHER_CHIPS):
                add(n + j, refs[0].at[me ^ k], refs[0].at[me ^ k], sibling)
        elif kind == "halves":
            for chip in range(N_CHIP):
                add(n + chip, refs[0].at[2 * chip + 1 - mc], refs[1].at[chip], sibling)
        else:
            for j, k in enumerate(_OTHER_CHIPS):
                peer, to = _peer(k, mx, my, mc)
                add(n + j, refs[0].at[to // 2], refs[1].at[me // 2], peer)
    return out


_HBM = pl.BlockSpec(memory_space=pltpu.HBM)
_SEM = pl.BlockSpec(memory_space=pltpu.SEMAPHORE)
_DATAFLOW = pltpu.SideEffectType.DATAFLOW_SIDE_EFFECTING


def exchange_call(name, waits, starts, deps=()):
    bufs, slot_of = [], {}

    def slots(items):
        out = []
        for item in items:
            for b in item:
                if id(b) not in slot_of:
                    slot_of[id(b)] = len(bufs)
                    bufs.append(b)
            out.append(tuple(slot_of[id(b)] for b in item))
        return out

    wait_plan = [(kind, slots(handle[0])) for kind, handle in waits]
    start_plan = [(kind, slots(items)) for kind, items in starts]
    wait_sems = [s for _, handle in waits for s in handle[1:]]
    n_buf, n_ws, n_start = len(bufs), len(wait_sems), len(starts)

    def body(*refs):
        buf_refs, sems_in = refs[:n_buf], refs[n_buf:n_buf + n_ws]
        outs = refs[n_buf + n_ws + len(deps):]
        pick = lambda plan: [tuple(buf_refs[s] for s in item) for item in plan]
        for wi, (kind, plan) in enumerate(wait_plan):
            copies = _copies(kind, pick(plan), sems_in[2 * wi], sems_in[2 * wi + 1])
            for cp in copies:
                cp.wait_send()
            for cp in copies:
                cp.wait_recv()
        for si, (kind, plan) in enumerate(start_plan):
            for cp in _copies(kind, pick(plan), outs[2 * si], outs[2 * si + 1]):
                cp.start()
        outs[-1][...] = jnp.zeros_like(outs[-1])

    hbm_bufs = [pltpu.with_memory_space_constraint(b, pltpu.HBM) for b in bufs]
    sem_shapes = []
    for kind, plan in start_plan:
        sem_shapes += [pltpu.SemaphoreType.DMA((len(plan) * _PLANS[kind][1],))] * 2
    outs = pl.pallas_call(
        body, name=name,
        out_shape=sem_shapes + [pltpu.HBM(b.shape, b.dtype) for b in bufs] + [jax.ShapeDtypeStruct((8, LANE), F32)],
        in_specs=[_HBM] * n_buf + [_SEM] * n_ws + [_ANY] * len(deps),
        out_specs=[_SEM] * (2 * n_start) + [_HBM] * n_buf + [pl.BlockSpec(memory_space=pltpu.VMEM)],
        input_output_aliases={i: 2 * n_start + i for i in range(n_buf)},
        compiler_params=pltpu.CompilerParams(has_side_effects=_DATAFLOW))(*hbm_bufs, *wait_sems, *deps)
    sems, thru, token = outs[:2 * n_start], outs[2 * n_start:-1], outs[-1]
    through = lambda plan: [tuple(thru[s] for s in item) for item in plan]
    waited = [through(plan) for _, plan in wait_plan]
    handles = [(through(plan), sems[2 * si], sems[2 * si + 1]) for si, (_, plan) in enumerate(start_plan)]
    return waited, handles, token


def gather_small(name, a, deps=()):
    def body(a_ref, *rest):
        o_ref, send_sems, recv_sems, local_sem = rest[len(deps):]
        me = _mesh_pos()[3]
        own = pltpu.make_async_copy(a_ref, o_ref.at[me], local_sem)
        own.start()
        copies = _copies("gather", [(a_ref, o_ref)], send_sems, recv_sems)
        for cp in copies:
            cp.start()
        for cp in copies:
            cp.wait_recv()
        for cp in copies:
            cp.wait_send()
        own.wait()

    return pl.pallas_call(
        body, name=name, in_specs=[_ANY] * (1 + len(deps)), out_specs=_ANY,
        out_shape=jax.ShapeDtypeStruct((N_DEV,) + a.shape, a.dtype),
        scratch_shapes=[pltpu.SemaphoreType.DMA((N_DEV - 1,)), pltpu.SemaphoreType.DMA((N_DEV - 1,)),
                        pltpu.SemaphoreType.DMA],
        compiler_params=pltpu.CompilerParams(has_side_effects=True))(a, *deps)


def _adamw_math(w, g, m, v):
    m2 = ADAM_B1 * m + (1.0 - ADAM_B1) * g
    v2 = ADAM_B2 * v + (1.0 - ADAM_B2) * (g * g)
    m_hat = m2 / (1.0 - ADAM_B1 ** ADAM_STEP)
    v_hat = v2 / (1.0 - ADAM_B2 ** ADAM_STEP)
    delta = -ADAM_LR * (m_hat / (jnp.sqrt(v_hat) + ADAM_EPS) + ADAM_WD * w)
    return delta, m2, v2


def chip_sum(name, a, half):
    _, r, c = a.shape
    tr = r
    chip = 2 * lax.axis_index("x") + lax.axis_index("y")
    where = jnp.stack([lax.axis_index("c"), chip ^ 1, chip ^ 2, chip ^ 3]).astype(jnp.int32)

    def body(where_ref, a_ref, h_ref, o_ref):
        del where_ref
        o_ref[...] = (a_ref[...].astype(F32) + h_ref[...].astype(F32)).astype(o_ref.dtype)

    blk = pl.BlockSpec((None, tr, c), lambda g, i, where: (where[1 + g], i, 0))
    grid_spec = pltpu.PrefetchScalarGridSpec(
        num_scalar_prefetch=1, grid=(N_CHIP - 1, r // tr),
        in_specs=[pl.BlockSpec((None, None, tr, c), lambda g, i, where: (where[1 + g], where[0], i, 0)), blk],
        out_specs=blk)
    return pl.pallas_call(
        body, name=name, grid_spec=grid_spec, out_shape=jax.ShapeDtypeStruct((N_CHIP, r, c), BF16),
        compiler_params=_params("parallel", "parallel"))(where, a.reshape(N_CHIP, 2, r, c), half)


def adamw(name, w, m, v, l, land, a, half, prev=None):
    L, r, c = w.shape
    cp = land.shape[2]
    tr = _pick(r, (256, 176, 128, 64, 32, 16, 8))

    def body(w_ref, m_ref, v_ref, land_ref, a_ref, half_ref, *rest):
        g_ref, d_ref, m2_ref, v2_ref = rest[-4:]
        chip = _mesh_pos()[3] // 2
        mine = a_ref[:, pl.ds(0, c)].astype(F32) + half_ref[:, pl.ds(0, c)].astype(F32)
        g = None
        for s in range(N_CHIP):
            part = jnp.where(chip == s, mine, land_ref[s, :, pl.ds(0, c)].astype(F32))
            g = part if g is None else g + part
        delta, m2, v2 = _adamw_math(w_ref[...], g, m_ref[...], v_ref[...])
        g_ref[...] = g
        d_ref[...] = delta
        m2_ref[...] = m2
        v2_ref[...] = v2

    blk = pl.BlockSpec((None, tr, c), lambda i: (l, i, 0))
    shape = jax.ShapeDtypeStruct((L, r, c), F32)
    extra = [] if prev is None else list(prev)
    return pl.pallas_call(
        body, name=name, grid=(r // tr,),
        in_specs=[blk, blk, blk, pl.BlockSpec((N_CHIP, tr, cp), lambda i: (0, i, 0)),
                  pl.BlockSpec((None, tr, cp), lambda i: (_mesh_pos()[3], i, 0)),
                  pl.BlockSpec((None, tr, cp), lambda i: (_mesh_pos()[3] // 2, i, 0))] + [_ANY] * len(extra),
        out_specs=[blk] * 4, out_shape=[shape] * 4,
        input_output_aliases={6 + k: k for k in range(len(extra))},
        compiler_params=_params("parallel"))(w, m, v, land, a, half, *extra)


def adamw_columns(name, w, m, v, land, a, half):
    r, _, D = w.shape
    tc = _pick(D, (256, 128))

    def body(w_ref, m_ref, v_ref, land_ref, a_ref, half_ref, g_ref, d_ref, m2_ref, v2_ref):
        chip = _mesh_pos()[3] // 2
        mine = a_ref[...].astype(F32) + half_ref[...].astype(F32)
        g = None
        for s in range(N_CHIP):
            part = jnp.where(chip == s, mine, land_ref[s].astype(F32))
            g = part if g is None else g + part
        flat = lambda ref: ref[...].reshape(r, tc)
        delta, m2, v2 = _adamw_math(flat(w_ref), g, flat(m_ref), flat(v_ref))
        for ref, val in ((g_ref, g), (d_ref, delta), (m2_ref, m2), (v2_ref, v2)):
            ref[...] = val.reshape(r, 1, tc)

    blk = pl.BlockSpec((r, 1, tc), lambda i: (0, 0, i))
    shape = jax.ShapeDtypeStruct((r, 1, D), F32)
    return pl.pallas_call(
        body, name=name, grid=(D // tc,),
        in_specs=[blk, blk, blk, pl.BlockSpec((N_CHIP, r, tc), lambda i: (0, 0, i)),
                  pl.BlockSpec((None, r, tc), lambda i: (_mesh_pos()[3], 0, i)),
                  pl.BlockSpec((None, r, tc), lambda i: (_mesh_pos()[3] // 2, 0, i))],
        out_specs=[blk] * 4, out_shape=[shape] * 4,
        compiler_params=_params("parallel"))(w, m, v, land, a, half)


def adamw_small(name, w, m, v, parts):
    n = w.shape[1]

    def body(w_ref, m_ref, v_ref, p_ref, g_ref, d_ref, m2_ref, v2_ref):
        g = p_ref[0:1, :]
        for s in range(1, N_DEV):
            g = g + p_ref[s:s + 1, :]
        delta, m2, v2 = _adamw_math(w_ref[...], g, m_ref[...], v_ref[...])
        g_ref[...] = g
        d_ref[...] = delta
        m2_ref[...] = m2
        v2_ref[...] = v2

    shape = jax.ShapeDtypeStruct((1, n), F32)
    return pl.pallas_call(body, name=name, out_shape=[shape] * 4,
                          compiler_params=pltpu.CompilerParams(vmem_limit_bytes=VMEM_LIMIT_BYTES))(w, m, v, parts)


def _rope_tables(positions):
    half = RET_DK // 2
    inv_freq = 1.0 / jnp.power(RET_THETA_BASE, jnp.linspace(0.0, 1.0, half, dtype=F32))
    ang = positions.astype(F32)[:, None] * inv_freq
    cos, sin = jnp.cos(ang), jnp.sin(ang)
    cosf = jnp.repeat(cos, 2, axis=-1)
    sins = jnp.stack([-sin, sin], axis=-1).reshape(cosf.shape)
    return cosf, sins


def _pad_to(a, axis, size):
    pad = [(0, 0)] * a.ndim
    pad[axis] = (0, size - a.shape[axis])
    return jnp.pad(a, pad)


def _round_up(n, m):
    return -(-n // m) * m


def kernel(x, p, positions, attn_norm_w, ffn_norm_w, ple_norm_w, final_norm_w, ab_w_in, ab_gla_gate_up, ab_gla_gate_b, ab_ret_norm_w, ab_gla_norm_w, ab_w_out, c_w_qkv, c_w_out, ffn_w_gate, ffn_w_up, ffn_w_down, ple_w_proj, ple_w_gate, loss_target, m_attn_norm_w, m_ffn_norm_w, m_ple_norm_w, m_final_norm_w, m_ab_w_in, m_ab_gla_gate_up, m_ab_gla_gate_b, m_ab_ret_norm_w, m_ab_gla_norm_w, m_ab_w_out, m_c_w_qkv, m_c_w_out, m_ffn_w_gate, m_ffn_w_up, m_ffn_w_down, m_ple_w_proj, m_ple_w_gate, v_attn_norm_w, v_ffn_norm_w, v_ple_norm_w, v_final_norm_w, v_ab_w_in, v_ab_gla_gate_up, v_ab_gla_gate_b, v_ab_ret_norm_w, v_ab_gla_norm_w, v_ab_w_out, v_c_w_qkv, v_c_w_out, v_ffn_w_gate, v_ffn_w_up, v_ffn_w_down, v_ple_w_proj, v_ple_w_gate):
    T, D = x.shape[1], x.shape[2]
    depth = attn_norm_w.shape[0]
    assert ab_w_in.shape[0] == 1 and c_w_qkv.shape[0] == 1 and depth == 2, "one even and one odd layer"
    me = 4 * lax.axis_index("x") + 2 * lax.axis_index("y") + lax.axis_index("c")
    in_shard = ab_w_in.shape[2]
    in_width = in_shard * N_DEV
    assert in_width == OFF_LR + GLA_GATE_RANK
    fs = ffn_w_gate.shape[2]
    fp = _round_up(fs, LANE)
    gu_cols = ab_gla_gate_up.shape[2]

    bf = lambda a: a.astype(BF16)
    tr_ = lambda a: jnp.swapaxes(a, -1, -2)
    wg_t, wu_t = tr_(ffn_w_gate), tr_(ffn_w_up)
    srcs = {"w_in": bf(tr_(ab_w_in[0]))}
    group_keys = [["w_in"], ["gu", "w_oab"], ["wg0", "wu0"], ["wd0", "wpg0", "wpp0"], ["w_qkv", "w_oc"],
                  ["wg1", "wu1"], ["wd1"], ["wpg1", "wpp1"]]
    G_IN, G_OUT, G_QKV = 0, 1, 4
    g_ffn = lambda layer: (2, 3, 3) if layer == 0 else (5, 6, 7)

    def landing(key):
        a = srcs[key]
        rows = fp if key[:2] in ("wg", "wu", "wd") else a.shape[0]
        buf = lax.empty((N_DEV, rows) + a.shape[1:], a.dtype)
        if rows > a.shape[0]:
            zeros = jnp.zeros((N_DEV, rows - a.shape[0]) + a.shape[1:], a.dtype)
            buf = lax.dynamic_update_slice(buf, zeros, (0, a.shape[0]) + (0,) * (a.ndim - 1))
        return lax.dynamic_update_slice(buf, a[None], (me,) + (0,) * a.ndim)

    _, chip_handles, gather_token = exchange_call(
        "gather_start_in", [], [("to_chips", [(srcs[k], landing(k)) for k in group_keys[G_IN]])])
    (gather_token, w_out_, gu_, w_qkv_, w_oc_, wg_, wu_, wd_, wpg_, wpp_) = lax.optimization_barrier(
        (gather_token, ab_w_out, ab_gla_gate_up, c_w_qkv, c_w_out, wg_t, wu_t, ffn_w_down, ple_w_gate, ple_w_proj))
    srcs.update(w_oab=bf(w_out_[0]), gu=gu_[0], w_qkv=bf(w_qkv_[0]), w_oc=bf(w_oc_[0]))
    for l in range(depth):
        srcs[f"wg{l}"] = bf(wg_[l])
        srcs[f"wu{l}"] = bf(wu_[l])
        srcs[f"wd{l}"] = bf(wd_[l])
        srcs[f"wpg{l}"] = bf(wpg_[l])
        srcs[f"wpp{l}"] = bf(wpp_[l])
    _, more, gather_token = exchange_call(
        "gather_start", [], [("to_chips", [(srcs[k], landing(k)) for k in keys]) for keys in group_keys[1:]],
        deps=(gather_token,))
    chip_handles = chip_handles + more
    weights = {}

    def gather_wait(gi, dep):
        lands = [(land,) for _, land in chip_handles[gi][0]]
        _, (passing,), _ = exchange_call(
            f"gather{gi}_pass", [("to_chips", chip_handles[gi])], [("pass_on", lands)], deps=(dep,))
        (complete,), _, _ = exchange_call(f"gather{gi}_done", [("pass_on", passing)], [])
        weights.update(zip(group_keys[gi], [land for (land,) in complete]))

    gb = ab_gla_gate_b
    hn_w = jnp.concatenate([ab_ret_norm_w, ab_gla_norm_w], axis=1)
    cosf, sins = _rope_tables(positions[0])
    p_bf = bf(p[:, 0])

    xs = x[0]
    saved = []
    for i in range(depth):
        nm = f"l{i}_"
        w_attn, w_ffn, w_ple = attn_norm_w[i:i + 1], ffn_norm_w[i:i + 1], ple_norm_w[i:i + 1]
        (xn,) = rowwise(nm + "norm_attn", lambda a, w: (_rms(a, w),), T, [("row", xs), ("full", w_attn)],
                        [("row", D, BF16)], deps=(gather_token,) if i == 0 else ())
        if i % 2 == 0:
            gather_wait(G_IN, xn)
            w_in_t = weights["w_in"].reshape(1, 1, in_width, D)
            w_lr_t = _pad_to(w_in_t[0, 0, OFF_LR:], 0, LANE).reshape(1, 1, LANE, D)
            z = mmt_fwd(nm + "mm_in", xn, w_in_t, 0, F32, n=OFF_LR)
            glr = mmt_fwd(nm + "mm_lr", xn, w_lr_t, 0, F32)
            oraw = retention_fwd(nm + "ret_fwd", z, cosf, sins, RET_V + GLA_V)
            gather_wait(G_OUT, oraw)
            w_oab = weights["w_oab"].reshape(1, 1, D, D)
            gu_full = _pad_to(weights["gu"].transpose(1, 0, 2).reshape(GLA_GATE_RANK, GLA_QK), 0, LANE)
            oraw, gla_states = gla_fwd(nm + "gla_fwd", z, glr, gu_full, gb, oraw)
            o = headnorm_fwd(nm + "headnorm_fwd", oraw, z, hn_w)
            h1, hn = mm_add_norm(nm + "mm_out", o, w_oab, xs, w_ffn)
            mixer_saved = (z, glr, oraw, o, gla_states)
        else:
            gather_wait(G_QKV, xn)
            w_qkv = weights["w_qkv"].reshape((1,) + weights["w_qkv"].shape)
            w_oc = weights["w_oc"].reshape(1, 1, D, D)
            qkv = mm_nn(nm + "mm_qkv", xn, w_qkv, 0, BF16)
            o, lse = attn_fwd(nm + "attn_fwd", qkv)
            h1, hn = mm_add_norm(nm + "mm_out", o, w_oc, xs, w_ffn)
            mixer_saved = (qkv, o, lse)
        gather_wait(g_ffn(i)[0], hn)
        wg = weights[f"wg{i}"].reshape(1, N_DEV, fp, D)
        wu = weights[f"wu{i}"].reshape(1, N_DEV, fp, D)
        dup, dgate, act = ffn_gate_up(nm + "ffn_gate_up", hn, wg, wu)
        gather_wait(g_ffn(i)[1], act)
        wd = weights[f"wd{i}"].reshape(1, 1, N_DEV * fp, D)
        h2, pn = mm_add_norm(nm + "mm_down", act, wd, h1, w_ple)
        if g_ffn(i)[2] != g_ffn(i)[1]:
            gather_wait(g_ffn(i)[2], pn)
        wpg = weights[f"wpg{i}"].reshape(1, 1, D, D)
        wpp = weights[f"wpp{i}"].reshape((1,) + weights[f"wpp{i}"].shape)
        x_next, s, e = ple_fwd(nm + "ple", pn, wpg, p_bf[i], wpp, h2)
        mixer_w = (w_in_t, w_lr_t, w_oab, gu_full) if i % 2 == 0 else (w_qkv, w_oc)
        saved.append((xs, xn, mixer_saved, mixer_w, (wg, wu, wd, wpg), h1, hn, dup, dgate, act, h2, pn, s, e))
        xs = x_next

    def ple_bwd(d, sv, ev):
        gate = _sigmoid(sv)
        return d * gate, d * ev * gate * (1.0 - gate)

    def loss_fn(a, w, t, sv, ev):
        diff = _rms(a, w) - t
        dx, dw = _rms_bwd(a, w, diff * (1.0 / D))
        part = 0.5 * jnp.sum(jnp.mean(diff * diff, axis=-1, keepdims=True), axis=0, keepdims=True)
        return (dx, dw, jnp.broadcast_to(part, (1, LANE))) + ple_bwd(dx, sv, ev)

    dx, d_final_w, loss_part, *ple_grads = rowwise(
        "loss_head", loss_fn, T,
        [("row", xs), ("full", final_norm_w[None, :]), ("row", loss_target[0]), ("row", saved[-1][-2]), ("row", saved[-1][-1])],
        [("row", D, F32), ("acc", D), ("acc", LANE), ("row", D, BF16), ("row", D, BF16)])
    loss = lax.psum(loss_part[0, 0], ("x", "y", "c"))

    grads = {}
    on_chip = []
    scatters = []

    def scatter_start(name, keys, deps=()):
        waits = [("halves", on_chip[0][1])] if on_chip else []
        starts = [("halves", [(grads[k], lax.empty((N_CHIP,) + grads[k].shape[1:], BF16)) for k in keys])] if keys else []
        waited, handles, token = exchange_call(name, waits, starts, deps=deps)
        if on_chip:
            done_keys, _ = on_chip.pop()
            sums = [chip_sum(f"{name}_sum{j}", a, half) for j, (a, half) in enumerate(waited[0])]
            _, (handle,), token = exchange_call(
                name + "_chips", [], [("chip_sums", [(cs, lax.empty(cs.shape, BF16)) for cs in sums])])
            scatters.append((done_keys, handle, waited[0]))
        if keys:
            on_chip.append((keys, handles[0]))
        return token

    d_attn_w, d_ffn_w, d_ple_w = [None] * depth, [None] * depth, [None] * depth
    for i in reversed(range(depth)):
        nm = f"l{i}_b_"
        xs_i, xn, mixer_saved, mixer_w, (wg, wu, wd, wpg), h1, hn, dup, dgate, act, h2, pn, _, _ = saved[i]
        w_attn, w_ffn, w_ple = attn_norm_w[i:i + 1], ffn_norm_w[i:i + 1], ple_norm_w[i:i + 1]

        first_deps = (loss.reshape(1, 1),) if i == depth - 1 else ()
        de, ds = ple_grads
        grads[("ple_w_proj", i)] = mm_tn(nm + "mm_ple_proj_w", p_bf[i], de, N_DEV, BF16, deps=first_deps)
        grads[("ple_w_gate", i)] = mm_tn(nm + "mm_ple_gate_w", pn, ds, 1, BF16).reshape(N_DEV, D // N_DEV, D)
        dpn = mm_nt(nm + "mm_ple_gate_x", ds, wpg, 0, F32)

        def norm_bwd_add(a, w, dn, dres):
            dxx, dw = _rms_bwd(a, w, dn)
            tot = dres + dxx
            return tot, tot, dw

        dh2, dh2_bf, d_ple_w[i] = rowwise(nm + "norm_ple", norm_bwd_add, T,
                                          [("row", h2), ("full", w_ple), ("row", dpn), ("row", dx)],
                                          [("row", D, F32), ("row", D, BF16), ("acc", D)])
        grads[("ffn_w_down", i)] = mm_tn(nm + "mm_down_w", act, dh2_bf, 1, BF16).reshape(N_DEV, fp, D)
        token = scatter_start(nm + "scatter_ple_down", [("ple_w_proj", i), ("ple_w_gate", i), ("ffn_w_down", i)])
        dg, du = ffn_down_bwd(nm + "ffn_down_x", dh2_bf, wd, dup, dgate, deps=(token,))
        grads[("ffn_w_gate", i)] = mmt_dw(nm + "mm_gate_w", dg, hn, N_DEV, BF16)
        grads[("ffn_w_up", i)] = mmt_dw(nm + "mm_up_w", du, hn, N_DEV, BF16)
        token = scatter_start(nm + "scatter_gate_up", [("ffn_w_gate", i), ("ffn_w_up", i)])
        dhn = mmt_dx_pair(nm + "mm_gate_up_x", dg, wg, du, wu, F32, deps=(token,))
        dh1, dh1_bf, d_ffn_w[i] = rowwise(nm + "norm_ffn", norm_bwd_add, T,
                                          [("row", h1), ("full", w_ffn), ("row", dhn), ("row", dh2)],
                                          [("row", D, F32), ("row", D, BF16), ("acc", D)])
        if i % 2 == 0:
            z, glr, oraw, o, gla_states = mixer_saved
            w_in_t, w_lr_t, w_oab, gu_full = mixer_w
            grads[("ab_w_out", 0)] = mm_tn(nm + "mm_out_w", o, dh1_bf, 1, BF16).reshape(N_DEV, D // N_DEV, D)
            token = scatter_start(nm + "scatter_out", [("ab_w_out", 0)])
            do = mm_nt(nm + "mm_out_x", dh1_bf, w_oab, 0, F32, deps=(token,))
            d_oraw, d_gates, d_hn_w = headnorm_bwd(nm + "headnorm", oraw, z, hn_w, do)
            d_rq, d_rk, d_rv = retention_bwd(nm + "ret", z, cosf, sins, d_oraw)
            d_gq, d_gk, d_gv, d_glr4, d_gu, d_gb = gla_bwd(nm + "gla", z, glr, gu_full, gb, d_oraw, gla_states)
            dz = jnp.concatenate([d_rq, d_rk, d_rv, d_gates[:, :RET_V], d_gq, d_gk, d_gv, d_gates[:, RET_V:]], axis=1)
            (d_glr,) = rowwise(nm + "sum_lr", lambda *a: (a[0] + a[1] + a[2] + a[3],), T,
                               [("row", d_glr4[hh]) for hh in range(GLA_HEADS)], [("row", LANE, BF16)])
            dwt_in = mmt_dw(nm + "mm_in_w", dz, xn, 1, BF16, rows=in_width)
            dwt_in = mmt_dw_rows(nm + "mm_lr_w", d_glr, xn, dwt_in, OFF_LR, GLA_GATE_RANK)
            grads[("ab_w_in", 0)] = dwt_in.reshape(N_DEV, in_shard, D)
            token = scatter_start(nm + "scatter_in", [("ab_w_in", 0)])
            dxn_a = mmt_dx_wide(nm + "mm_in_x", dz, w_in_t, F32, n=OFF_LR, deps=(token,))
            token = scatter_start(nm + "scatter_in_on", [], deps=(dxn_a,))
            dxn_b = mmt_dx(nm + "mm_lr_x", d_glr, w_lr_t, 0, F32, deps=(token,))
        else:
            qkv, o, lse = mixer_saved
            w_qkv, w_oc = mixer_w
            grads[("c_w_out", 0)] = mm_tn(nm + "mm_out_w", o, dh1_bf, 1, BF16).reshape(N_DEV, D // N_DEV, D)
            do = mm_nt(nm + "mm_out_x", dh1_bf, w_oc, 0, BF16)
            dq, dk, dv = attn_bwd(nm + "attn", qkv, o, lse, do)
            dqkv = jnp.concatenate([dq, dk, dv], axis=1)
            grads[("c_w_qkv", 0)] = mm_tn(nm + "mm_qkv_w", xn, dqkv, N_DEV, BF16)
            token = scatter_start(nm + "scatter_attn", [("c_w_out", 0), ("c_w_qkv", 0)])
            dxn_a = mm_nt_wide(nm + "mm_qkv_x", dqkv, w_qkv, F32, deps=(token,))
            dxn_b = None
        dxn = [dxn_a] if dxn_b is None else [dxn_a, dxn_b]

        below = list(saved[i - 1][-2:]) if i > 0 else []

        def norm_bwd_in(a, w, *rest, n_d=len(dxn)):
            dxx, dw = _rms_bwd(a, w, sum(rest[1:n_d], rest[0]))
            tot = rest[n_d] + dxx
            return (tot, dw) + (ple_bwd(tot, *rest[n_d + 1:]) if len(rest) > n_d + 1 else ())

        dx, d_attn_w[i], *ple_grads = rowwise(
            nm + "norm_attn", norm_bwd_in, T,
            [("row", xs_i), ("full", w_attn)] + [("row", d) for d in dxn + [dh1] + below],
            [("row", D, F32), ("acc", D)] + [("row", D, BF16)] * len(below))

    small_names = ["attn_norm_w", "ffn_norm_w", "ple_norm_w", "final_norm_w", "ab_gla_gate_b", "ab_ret_norm_w",
                   "ab_gla_norm_w"]
    small_grads = [jnp.concatenate(d_attn_w, 0), jnp.concatenate(d_ffn_w, 0), jnp.concatenate(d_ple_w, 0), d_final_w[0],
                   d_gb, d_hn_w[:, :RET_V], d_hn_w[:, RET_V:]]
    small_w = [attn_norm_w, ffn_norm_w, ple_norm_w, final_norm_w, ab_gla_gate_b, ab_ret_norm_w, ab_gla_norm_w]
    small_m = [m_attn_norm_w, m_ffn_norm_w, m_ple_norm_w, m_final_norm_w, m_ab_gla_gate_b, m_ab_ret_norm_w, m_ab_gla_norm_w]
    small_v = [v_attn_norm_w, v_ffn_norm_w, v_ple_norm_w, v_final_norm_w, v_ab_gla_gate_b, v_ab_ret_norm_w, v_ab_gla_norm_w]
    sizes = [int(np.prod(a.shape)) for a in small_w]
    n_gu = GLA_GATE_RANK * GLA_QK
    n_small = _round_up(sum(sizes) + n_gu, LANE)
    pack = lambda parts: _pad_to(jnp.concatenate([a.reshape(-1) for a in parts]), 0, n_small)[None, :]
    small_part = pack(small_grads + [d_gu[:GLA_GATE_RANK]])

    cols_first = lambda a: jnp.transpose(a, (2, 0, 1))
    big_w = dict(ab_w_in=tuple(cols_first(a) for a in (ab_w_in, m_ab_w_in, v_ab_w_in)),
                 ab_w_out=(ab_w_out, m_ab_w_out, v_ab_w_out),
                 c_w_qkv=(c_w_qkv, m_c_w_qkv, v_c_w_qkv), c_w_out=(c_w_out, m_c_w_out, v_c_w_out),
                 ffn_w_gate=(wg_t, tr_(m_ffn_w_gate), tr_(v_ffn_w_gate)),
                 ffn_w_up=(wu_t, tr_(m_ffn_w_up), tr_(v_ffn_w_up)),
                 ffn_w_down=(ffn_w_down, m_ffn_w_down, v_ffn_w_down), ple_w_proj=(ple_w_proj, m_ple_w_proj, v_ple_w_proj),
                 ple_w_gate=(ple_w_gate, m_ple_w_gate, v_ple_w_gate))
    if on_chip:
        scatter_start("scatter_last", [], deps=(dx,))
    results, last = {}, dx
    for gi, (keys, handle, partials) in enumerate(scatters):
        (arrived,), _, _ = exchange_call(f"scatter_wait{gi}", [("chip_sums", handle)], [], deps=(last,))
        for (n, l), (_, land), (a, half) in zip(keys, arrived, partials):
            if n == "ab_w_in":
                results[n] = adamw_columns(f"adamw_{n}", *big_w[n], land, a, half)
            else:
                results[n] = adamw(f"adamw_{n}{l}", *big_w[n], l, land, a, half, prev=results.get(n))
            last = results[n][0]
    for n in ("ffn_w_gate", "ffn_w_up"):
        results[n] = [tr_(a) for a in results[n]]
    results["ab_w_in"] = [jnp.transpose(a, (1, 2, 0)) for a in results["ab_w_in"]]
    small_parts = gather_small("gather_small", small_part, deps=(last,)).reshape(N_DEV, n_small)

    gu_off = sum(sizes)
    own_cols = lambda a: lax.dynamic_slice_in_dim(a.reshape(GLA_GATE_RANK, GLA_QK), me * gu_cols, gu_cols, axis=1)
    small_res = adamw_small("adamw_small", pack(small_w + [jnp.zeros((n_gu,), F32)]),
                            pack(small_m + [jnp.zeros((n_gu,), F32)]), pack(small_v + [jnp.ones((n_gu,), F32)]),
                            small_parts)
    g_gu_full = small_res[0][0, gu_off:gu_off + n_gu]
    g_gu = own_cols(g_gu_full)[None]
    gu_res = adamw_small("adamw_gate_up", *[_pad_to(a.reshape(1, -1), 1, _round_up(a.size, LANE)) for a in
                                            (ab_gla_gate_up, m_ab_gla_gate_up, v_ab_gla_gate_up)],
                         jnp.concatenate([_pad_to(g_gu.reshape(1, -1), 1, _round_up(g_gu.size, LANE)),
                                          jnp.zeros((N_DEV - 1, _round_up(g_gu.size, LANE)), F32)], axis=0))
    for k in range(4):
        off = 0
        for n, a, sz in zip(small_names, small_w, sizes):
            results.setdefault(n, [None] * 4)[k] = small_res[k][0, off:off + sz].reshape(a.shape)
            off += sz
        results.setdefault("ab_gla_gate_up", [None] * 4)[k] = gu_res[k][0, :g_gu.size].reshape(ab_gla_gate_up.shape)

    order = ["attn_norm_w", "ffn_norm_w", "ple_norm_w", "final_norm_w", "ab_w_in", "ab_gla_gate_up", "ab_gla_gate_b",
             "ab_ret_norm_w", "ab_gla_norm_w", "ab_w_out", "c_w_qkv", "c_w_out", "ffn_w_gate", "ffn_w_up", "ffn_w_down",
             "ple_w_proj", "ple_w_gate"]
    return (loss, dx[None], *[results[n][0] for n in order], *[results[n][1] for n in order],
            *[results[n][2] for n in order], *[results[n][3] for n in order])
```
